```python
import jax, jax.numpy as jnp
from jax import lax
import numpy as np

D_MODEL = 2048
BATCH = 8
SEQ = 2048
DEPTH = 2

N_MIXERS = 2
GLA_HEADS = 4
GLA_DK = D_MODEL // 2
GLA_DV = D_MODEL
GLA_DK_HEAD = GLA_DK // GLA_HEADS
GLA_DV_HEAD = GLA_DV // GLA_HEADS
GLA_RANK = 16
GLA_TAU = 16.0
GLA_CHUNK = 64
FOX_HEADS = 16
FOX_HEAD_DIM = D_MODEL // FOX_HEADS
FOX_Q_BLOCK = 128
D_FF = 5632
CONV_WIDTH = 3
NORM_EPS = 1e-6
MOD_SCALE = 0.1

kernel_name = "hybrid_gla_fox_convffn_adaln"


def rmsnorm(x, gain):
    xf = x.astype(jnp.float32)
    xf = xf * lax.rsqrt(jnp.mean(xf * xf, axis=-1, keepdims=True) + NORM_EPS)
    return xf.astype(x.dtype) * gain


def gla_mixer(h, w_in, w_gate, b_gate, g_norm, w_out):
    bsz, seq, _ = h.shape
    n_chunks = seq // GLA_CHUNK
    proj = h @ w_in
    q, k, v, r, a = jnp.split(
        proj, [GLA_DK, 2 * GLA_DK, 2 * GLA_DK + GLA_DV, 2 * GLA_DK + 2 * GLA_DV], axis=-1)
    log_alpha = jax.nn.log_sigmoid((a @ w_gate + b_gate).astype(jnp.float32)) / GLA_TAU

    def chunks(t):
        return t.astype(jnp.float32).reshape(
            bsz, n_chunks, GLA_CHUNK, GLA_HEADS, -1).transpose(0, 3, 1, 2, 4)

    q = chunks(q) * (GLA_DK_HEAD ** -0.5)
    k = chunks(k)
    v = chunks(v)
    b = jnp.cumsum(chunks(log_alpha), axis=3)

    q_dec = q * jnp.exp(b)
    k_inv = k * jnp.exp(-b)
    causal = jnp.tril(jnp.ones((GLA_CHUNK, GLA_CHUNK), dtype=bool))
    attn = jnp.where(causal, jnp.einsum('bhnck,bhnsk->bhncs', q_dec, k_inv), 0.0)
    o_intra = jnp.einsum('bhncs,bhnsv->bhncv', attn, v)

    b_last = b[:, :, :, -1:, :]
    k_end = k * jnp.exp(b_last - b)
    chunk_decay = jnp.exp(b_last[:, :, :, 0, :])

    def step(state, xs):
        qd, kd, vv, dec = xs
        o = jnp.einsum('bhck,bhkv->bhcv', qd, state)
        state = dec[..., None] * state + jnp.einsum('bhck,bhcv->bhkv', kd, vv)
        return state, o

    state0 = jnp.zeros((bsz, GLA_HEADS, GLA_DK_HEAD, GLA_DV_HEAD), jnp.float32)
    xs = tuple(jnp.moveaxis(t, 2, 0) for t in (q_dec, k_end, v, chunk_decay))
    _, o_inter = lax.scan(step, state0, xs)
    o = o_intra + jnp.moveaxis(o_inter, 0, 2)
    o = o.transpose(0, 2, 3, 1, 4).reshape(bsz, seq, GLA_HEADS, GLA_DV_HEAD)
    o = rmsnorm(o, g_norm.reshape(GLA_HEADS, GLA_DV_HEAD).astype(jnp.float32))
    o = o.reshape(bsz, seq, GLA_DV).astype(h.dtype) * jax.nn.silu(r)
    return o @ w_out


def fox_mixer(h, w_in, b_f, q_gain, k_gain, w_out):
    bsz, seq, _ = h.shape
    proj = h @ w_in
    q, k, v, og, fl = jnp.split(proj, [D_MODEL, 2 * D_MODEL, 3 * D_MODEL, 4 * D_MODEL], axis=-1)
    q = rmsnorm(q.reshape(bsz, seq, FOX_HEADS, FOX_HEAD_DIM), q_gain) * (FOX_HEAD_DIM ** -0.5)
    k = rmsnorm(k.reshape(bsz, seq, FOX_HEADS, FOX_HEAD_DIM), k_gain)
    v = v.reshape(bsz, seq, FOX_HEADS, FOX_HEAD_DIM)
    q, k, v = (t.transpose(0, 2, 1, 3) for t in (q, k, v))
    log_f = jax.nn.log_sigmoid((fl + b_f).astype(jnp.float32))
    cum = jnp.cumsum(log_f, axis=1).transpose(0, 2, 1)

    n_blocks = seq // FOX_Q_BLOCK
    q_blocks = q.reshape(bsz, FOX_HEADS, n_blocks, FOX_Q_BLOCK, FOX_HEAD_DIM).transpose(2, 0, 1, 3, 4)
    cum_blocks = cum.reshape(bsz, FOX_HEADS, n_blocks, FOX_Q_BLOCK).transpose(2, 0, 1, 3)
    key_pos = jnp.arange(seq)

    def attend(args):
        qb, cb, blk = args
        q_pos = blk * FOX_Q_BLOCK + jnp.arange(FOX_Q_BLOCK)
        logits = (jnp.einsum('bhqd,bhkd->bhqk', qb, k).astype(jnp.float32)
                  + cb[..., None] - cum[:, :, None, :])
        logits = jnp.where(key_pos[None, :] <= q_pos[:, None], logits, -jnp.inf)
        p = jax.nn.softmax(logits, axis=-1)
        return jnp.einsum('bhqk,bhkd->bhqd', p.astype(v.dtype), v)

    o = lax.map(attend, (q_blocks, cum_blocks, jnp.arange(n_blocks)))
    o = o.transpose(1, 0, 3, 2, 4).reshape(bsz, seq, D_MODEL)
    o = o * jax.nn.sigmoid(og)
    return o @ w_out


def conv_ffn(h, w_up, conv_w, conv_b, w_down):
    u = h @ w_up
    u = lax.conv_general_dilated(
        u, conv_w[:, None, :].astype(u.dtype), window_strides=(1,),
        padding=[(CONV_WIDTH - 1, 0)], dimension_numbers=('NWC', 'WIO', 'NWC'),
        feature_group_count=2 * D_FF) + conv_b
    gate, val = jnp.split(u, 2, axis=-1)
    return (jax.nn.silu(gate) * val) @ w_down


def _fwd_setup_inputs(seed: int = 0) -> dict:
    key = jax.random.key(seed)
    ks = jax.random.split(key, 24)
    n_gla = (DEPTH + 1) // 2
    n_fox = DEPTH // 2
    f32 = jnp.float32

    def w(k, shape, fan_in, scale=1.0):
        return (scale * fan_in ** -0.5) * jax.random.normal(k, shape, f32)

    def gain(k, shape):
        return 1.0 + 0.05 * jax.random.normal(k, shape, f32)

    gla_in = 2 * GLA_DK + 2 * GLA_DV + GLA_RANK
    fox_in = 4 * D_MODEL + FOX_HEADS
    return {
        "x": jax.random.normal(ks[0], (BATCH, SEQ, D_MODEL), f32),
        "c": jax.random.normal(ks[1], (BATCH, D_MODEL), f32),
        "w_mod": w(ks[2], (DEPTH, D_MODEL, 6 * D_MODEL), D_MODEL, MOD_SCALE),
        "b_mod": 0.02 * jax.random.normal(ks[3], (DEPTH, 6 * D_MODEL), f32),
        "norm_mix": gain(ks[4], (DEPTH, D_MODEL)),
        "norm_ffn": gain(ks[5], (DEPTH, D_MODEL)),
        "gla_w_in": w(ks[6], (n_gla, D_MODEL, gla_in), D_MODEL),
        "gla_w_gate": w(ks[7], (n_gla, GLA_RANK, GLA_DK), GLA_RANK),
        "gla_b_gate": 0.1 * jax.random.normal(ks[8], (n_gla, GLA_DK), f32),
        "gla_norm": gain(ks[9], (n_gla, GLA_DV)),
        "gla_w_out": w(ks[10], (n_gla, GLA_DV, D_MODEL), GLA_DV),
        "fox_w_in": w(ks[11], (n_fox, D_MODEL, fox_in), D_MODEL),
        "fox_b_f": 3.0 + 0.5 * jax.random.normal(ks[12], (n_fox, FOX_HEADS), f32),
        "fox_q_norm": gain(ks[13], (n_fox, FOX_HEAD_DIM)),
        "fox_k_norm": gain(ks[14], (n_fox, FOX_HEAD_DIM)),
        "fox_w_out": w(ks[15], (n_fox, D_MODEL, D_MODEL), D_MODEL),
        "ffn_w_up": w(ks[16], (DEPTH, D_MODEL, 2 * D_FF), D_MODEL),
        "ffn_conv_w": w(ks[17], (DEPTH, CONV_WIDTH, 2 * D_FF), CONV_WIDTH),
        "ffn_conv_b": 0.02 * jax.random.normal(ks[18], (DEPTH, 2 * D_FF), f32),
        "ffn_w_down": w(ks[19], (DEPTH, D_FF, D_MODEL), D_FF),
        "norm_final": gain(ks[20], (D_MODEL,)),
    }


def _fwd_reference(x, c, w_mod, b_mod, norm_mix, norm_ffn,
              gla_w_in, gla_w_gate, gla_b_gate, gla_norm, gla_w_out,
              fox_w_in, fox_b_f, fox_q_norm, fox_k_norm, fox_w_out,
              ffn_w_up, ffn_conv_w, ffn_conv_b, ffn_w_down, norm_final):
    cond = jax.nn.silu(c)
    for i in range(DEPTH):
        mod = (cond @ w_mod[i] + b_mod[i])[:, None, :]
        sh_m, sc_m, g_m, sh_f, sc_f, g_f = jnp.split(mod, 6, axis=-1)
        h = rmsnorm(x, norm_mix[i]) * (1.0 + sc_m) + sh_m
        j = i // N_MIXERS
        if i % N_MIXERS == 0:
            y = gla_mixer(h, gla_w_in[j], gla_w_gate[j], gla_b_gate[j], gla_norm[j], gla_w_out[j])
        else:
            y = fox_mixer(h, fox_w_in[j], fox_b_f[j], fox_q_norm[j], fox_k_norm[j], fox_w_out[j])
        x = x + (1.0 + g_m) * y
        h = rmsnorm(x, norm_ffn[i]) * (1.0 + sc_f) + sh_f
        x = x + (1.0 + g_f) * conv_ffn(h, ffn_w_up[i], ffn_conv_w[i], ffn_conv_b[i], ffn_w_down[i])
    return rmsnorm(x, norm_final)


import jax as _jax
import jax.numpy as _jnp

TWIN_FORMAT = 'train_step'
FWD_PARAMS = ['x', 'c', 'w_mod', 'b_mod', 'norm_mix', 'norm_ffn', 'gla_w_in', 'gla_w_gate', 'gla_b_gate', 'gla_norm', 'gla_w_out', 'fox_w_in', 'fox_b_f', 'fox_q_norm', 'fox_k_norm', 'fox_w_out', 'ffn_w_up', 'ffn_conv_w', 'ffn_conv_b', 'ffn_w_down', 'norm_final']
TWIN_WEIGHTS = ['w_mod', 'b_mod', 'norm_mix', 'norm_ffn', 'gla_w_in', 'gla_w_gate', 'gla_b_gate', 'gla_norm', 'gla_w_out', 'fox_w_in', 'fox_b_f', 'fox_q_norm', 'fox_k_norm', 'fox_w_out', 'ffn_w_up', 'ffn_conv_w', 'ffn_conv_b', 'ffn_w_down', 'norm_final']
TWIN_DIFF_INPUT = 'x'
TWIN_INPUTS = ['x', 'c', 'w_mod', 'b_mod', 'norm_mix', 'norm_ffn', 'gla_w_in', 'gla_w_gate', 'gla_b_gate', 'gla_norm', 'gla_w_out', 'fox_w_in', 'fox_b_f', 'fox_q_norm', 'fox_k_norm', 'fox_w_out', 'ffn_w_up', 'ffn_conv_w', 'ffn_conv_b', 'ffn_w_down', 'norm_final', 'loss_target', 'm_w_mod', 'm_b_mod', 'm_norm_mix', 'm_norm_ffn', 'm_gla_w_in', 'm_gla_w_gate', 'm_gla_b_gate', 'm_gla_norm', 'm_gla_w_out', 'm_fox_w_in', 'm_fox_b_f', 'm_fox_q_norm', 'm_fox_k_norm', 'm_fox_w_out', 'm_ffn_w_up', 'm_ffn_conv_w', 'm_ffn_conv_b', 'm_ffn_w_down', 'm_norm_final', 'v_w_mod', 'v_b_mod', 'v_norm_mix', 'v_norm_ffn', 'v_gla_w_in', 'v_gla_w_gate', 'v_gla_b_gate', 'v_gla_norm', 'v_gla_w_out', 'v_fox_w_in', 'v_fox_b_f', 'v_fox_q_norm', 'v_fox_k_norm', 'v_fox_w_out', 'v_ffn_w_up', 'v_ffn_conv_w', 'v_ffn_conv_b', 'v_ffn_w_down', 'v_norm_final']
TWIN_OUTPUTS = ['loss', 'grad_x', 'grad_w_mod', 'grad_b_mod', 'grad_norm_mix', 'grad_norm_ffn', 'grad_gla_w_in', 'grad_gla_w_gate', 'grad_gla_b_gate', 'grad_gla_norm', 'grad_gla_w_out', 'grad_fox_w_in', 'grad_fox_b_f', 'grad_fox_q_norm', 'grad_fox_k_norm', 'grad_fox_w_out', 'grad_ffn_w_up', 'grad_ffn_conv_w', 'grad_ffn_conv_b', 'grad_ffn_w_down', 'grad_norm_final', 'delta_w_mod', 'delta_b_mod', 'delta_norm_mix', 'delta_norm_ffn', 'delta_gla_w_in', 'delta_gla_w_gate', 'delta_gla_b_gate', 'delta_gla_norm', 'delta_gla_w_out', 'delta_fox_w_in', 'delta_fox_b_f', 'delta_fox_q_norm', 'delta_fox_k_norm', 'delta_fox_w_out', 'delta_ffn_w_up', 'delta_ffn_conv_w', 'delta_ffn_conv_b', 'delta_ffn_w_down', 'delta_norm_final', 'new_m_w_mod', 'new_m_b_mod', 'new_m_norm_mix', 'new_m_norm_ffn', 'new_m_gla_w_in', 'new_m_gla_w_gate', 'new_m_gla_b_gate', 'new_m_gla_norm', 'new_m_gla_w_out', 'new_m_fox_w_in', 'new_m_fox_b_f', 'new_m_fox_q_norm', 'new_m_fox_k_norm', 'new_m_fox_w_out', 'new_m_ffn_w_up', 'new_m_ffn_conv_w', 'new_m_ffn_conv_b', 'new_m_ffn_w_down', 'new_m_norm_final', 'new_v_w_mod', 'new_v_b_mod', 'new_v_norm_mix', 'new_v_norm_ffn', 'new_v_gla_w_in', 'new_v_gla_w_gate', 'new_v_gla_b_gate', 'new_v_gla_norm', 'new_v_gla_w_out', 'new_v_fox_w_in', 'new_v_fox_b_f', 'new_v_fox_q_norm', 'new_v_fox_k_norm', 'new_v_fox_w_out', 'new_v_ffn_w_up', 'new_v_ffn_conv_w', 'new_v_ffn_conv_b', 'new_v_ffn_w_down', 'new_v_norm_final']
TWIN_LEAF_KINDS = {'loss': 'loss', 'grad_x': 'grad_x', 'grad_w_mod': 'grad_w', 'grad_b_mod': 'grad_w', 'grad_norm_mix': 'grad_w', 'grad_norm_ffn': 'grad_w', 'grad_gla_w_in': 'grad_w', 'grad_gla_w_gate': 'grad_w', 'grad_gla_b_gate': 'grad_w', 'grad_gla_norm': 'grad_w', 'grad_gla_w_out': 'grad_w', 'grad_fox_w_in': 'grad_w', 'grad_fox_b_f': 'grad_w', 'grad_fox_q_norm': 'grad_w', 'grad_fox_k_norm': 'grad_w', 'grad_fox_w_out': 'grad_w', 'grad_ffn_w_up': 'grad_w', 'grad_ffn_conv_w': 'grad_w', 'grad_ffn_conv_b': 'grad_w', 'grad_ffn_w_down': 'grad_w', 'grad_norm_final': 'grad_w', 'delta_w_mod': 'delta_w', 'delta_b_mod': 'delta_w', 'delta_norm_mix': 'delta_w', 'delta_norm_ffn': 'delta_w', 'delta_gla_w_in': 'delta_w', 'delta_gla_w_gate': 'delta_w', 'delta_gla_b_gate': 'delta_w', 'delta_gla_norm': 'delta_w', 'delta_gla_w_out': 'delta_w', 'delta_fox_w_in': 'delta_w', 'delta_fox_b_f': 'delta_w', 'delta_fox_q_norm': 'delta_w', 'delta_fox_k_norm': 'delta_w', 'delta_fox_w_out': 'delta_w', 'delta_ffn_w_up': 'delta_w', 'delta_ffn_conv_w': 'delta_w', 'delta_ffn_conv_b': 'delta_w', 'delta_ffn_w_down': 'delta_w', 'delta_norm_final': 'delta_w', 'new_m_w_mod': 'new_m', 'new_m_b_mod': 'new_m', 'new_m_norm_mix': 'new_m', 'new_m_norm_ffn': 'new_m', 'new_m_gla_w_in': 'new_m', 'new_m_gla_w_gate': 'new_m', 'new_m_gla_b_gate': 'new_m', 'new_m_gla_norm': 'new_m', 'new_m_gla_w_out': 'new_m', 'new_m_fox_w_in': 'new_m', 'new_m_fox_b_f': 'new_m', 'new_m_fox_q_norm': 'new_m', 'new_m_fox_k_norm': 'new_m', 'new_m_fox_w_out': 'new_m', 'new_m_ffn_w_up': 'new_m', 'new_m_ffn_conv_w': 'new_m', 'new_m_ffn_conv_b': 'new_m', 'new_m_ffn_w_down': 'new_m', 'new_m_norm_final': 'new_m', 'new_v_w_mod': 'new_v', 'new_v_b_mod': 'new_v', 'new_v_norm_mix': 'new_v', 'new_v_norm_ffn': 'new_v', 'new_v_gla_w_in': 'new_v', 'new_v_gla_w_gate': 'new_v', 'new_v_gla_b_gate': 'new_v', 'new_v_gla_norm': 'new_v', 'new_v_gla_w_out': 'new_v', 'new_v_fox_w_in': 'new_v', 'new_v_fox_b_f': 'new_v', 'new_v_fox_q_norm': 'new_v', 'new_v_fox_k_norm': 'new_v', 'new_v_fox_w_out': 'new_v', 'new_v_ffn_w_up': 'new_v', 'new_v_ffn_conv_w': 'new_v', 'new_v_ffn_conv_b': 'new_v', 'new_v_ffn_w_down': 'new_v', 'new_v_norm_final': 'new_v'}


def _forward(args):
    return _fwd_reference(*[args[k] for k in FWD_PARAMS])


def _output_shape():
    out = _jax.eval_shape(lambda: _forward(_fwd_setup_inputs(0)))
    return out.shape, out.dtype

N_MICROBATCH = 1
ADAM_LR = 0.001
ADAM_B1 = 0.9
ADAM_B2 = 0.999
ADAM_EPS = 1e-08
ADAM_WD = 0.01
ADAM_STEP = 10
PER_EXAMPLE_BATCH_AXIS = {'x': 0, 'c': 0, 'loss_target': 0}
SHARED_INPUTS = []
_WEIGHT_DTYPES = {'w_mod': _jnp.float32, 'b_mod': _jnp.float32, 'norm_mix': _jnp.float32, 'norm_ffn': _jnp.float32, 'gla_w_in': _jnp.float32, 'gla_w_gate': _jnp.float32, 'gla_b_gate': _jnp.float32, 'gla_norm': _jnp.float32, 'gla_w_out': _jnp.float32, 'fox_w_in': _jnp.float32, 'fox_b_f': _jnp.float32, 'fox_q_norm': _jnp.float32, 'fox_k_norm': _jnp.float32, 'fox_w_out': _jnp.float32, 'ffn_w_up': _jnp.float32, 'ffn_conv_w': _jnp.float32, 'ffn_conv_b': _jnp.float32, 'ffn_w_down': _jnp.float32, 'norm_final': _jnp.float32}
MOMENT_SCALE = {'w_mod': 3.217349e-02, 'b_mod': 6.660098e-02, 'norm_mix': 5.912257e-02, 'norm_ffn': 4.296069e-02, 'gla_w_in': 4.748076e-02, 'gla_w_gate': 6.786034e-03, 'gla_b_gate': 2.658211e-02, 'gla_norm': 4.008832e-02, 'gla_w_out': 3.981273e-02, 'fox_w_in': 7.250014e-03, 'fox_b_f': 4.419020e-02, 'fox_q_norm': 3.081917e-02, 'fox_k_norm': 3.069228e-02, 'fox_w_out': 9.238665e-03, 'ffn_w_up': 1.820164e-02, 'ffn_conv_w': 1.820109e-02, 'ffn_conv_b': 1.760403e-02, 'ffn_w_down': 2.973098e-02, 'norm_final': 8.025502e+00}


def _to_microbatches(a, axis):
    t = _jnp.moveaxis(a, axis, 0)
    t = t.reshape((N_MICROBATCH, t.shape[0] // N_MICROBATCH) + t.shape[1:])
    return _jnp.moveaxis(t, 1, axis + 1)


def setup_inputs(seed: int = 0) -> dict:
    inp = _fwd_setup_inputs(seed)
    key = _jax.random.fold_in(_jax.random.key(seed), 7919)
    shape, _ = _output_shape()
    out = dict(inp)
    out["loss_target"] = _jax.random.normal(_jax.random.fold_in(key, 0), shape, _jnp.float32)
    for i, name in enumerate(TWIN_WEIGHTS):
        w = inp[name].astype(_jnp.float32)
        if MOMENT_SCALE is None:
            s = _jnp.sqrt(_jnp.mean(_jnp.square(w)) + 1e-30)
        else:
            s = MOMENT_SCALE[name]
        km, kv = _jax.random.split(_jax.random.fold_in(key, i + 1))
        out[name] = w
        out["m_" + name] = s * _jax.random.normal(km, w.shape, _jnp.float32)
        out["v_" + name] = (s * s) * _jax.random.uniform(kv, w.shape, _jnp.float32, 0.5, 1.5)
    if N_MICROBATCH > 1:
        for name, axis in PER_EXAMPLE_BATCH_AXIS.items():
            out[name] = _to_microbatches(out[name], axis)
    return {'x': out['x'], 'c': out['c'], 'w_mod': out['w_mod'], 'b_mod': out['b_mod'], 'norm_mix': out['norm_mix'], 'norm_ffn': out['norm_ffn'], 'gla_w_in': out['gla_w_in'], 'gla_w_gate': out['gla_w_gate'], 'gla_b_gate': out['gla_b_gate'], 'gla_norm': out['gla_norm'], 'gla_w_out': out['gla_w_out'], 'fox_w_in': out['fox_w_in'], 'fox_b_f': out['fox_b_f'], 'fox_q_norm': out['fox_q_norm'], 'fox_k_norm': out['fox_k_norm'], 'fox_w_out': out['fox_w_out'], 'ffn_w_up': out['ffn_w_up'], 'ffn_conv_w': out['ffn_conv_w'], 'ffn_conv_b': out['ffn_conv_b'], 'ffn_w_down': out['ffn_w_down'], 'norm_final': out['norm_final'], 'loss_target': out['loss_target'], 'm_w_mod': out['m_w_mod'], 'm_b_mod': out['m_b_mod'], 'm_norm_mix': out['m_norm_mix'], 'm_norm_ffn': out['m_norm_ffn'], 'm_gla_w_in': out['m_gla_w_in'], 'm_gla_w_gate': out['m_gla_w_gate'], 'm_gla_b_gate': out['m_gla_b_gate'], 'm_gla_norm': out['m_gla_norm'], 'm_gla_w_out': out['m_gla_w_out'], 'm_fox_w_in': out['m_fox_w_in'], 'm_fox_b_f': out['m_fox_b_f'], 'm_fox_q_norm': out['m_fox_q_norm'], 'm_fox_k_norm': out['m_fox_k_norm'], 'm_fox_w_out': out['m_fox_w_out'], 'm_ffn_w_up': out['m_ffn_w_up'], 'm_ffn_conv_w': out['m_ffn_conv_w'], 'm_ffn_conv_b': out['m_ffn_conv_b'], 'm_ffn_w_down': out['m_ffn_w_down'], 'm_norm_final': out['m_norm_final'], 'v_w_mod': out['v_w_mod'], 'v_b_mod': out['v_b_mod'], 'v_norm_mix': out['v_norm_mix'], 'v_norm_ffn': out['v_norm_ffn'], 'v_gla_w_in': out['v_gla_w_in'], 'v_gla_w_gate': out['v_gla_w_gate'], 'v_gla_b_gate': out['v_gla_b_gate'], 'v_gla_norm': out['v_gla_norm'], 'v_gla_w_out': out['v_gla_w_out'], 'v_fox_w_in': out['v_fox_w_in'], 'v_fox_b_f': out['v_fox_b_f'], 'v_fox_q_norm': out['v_fox_q_norm'], 'v_fox_k_norm': out['v_fox_k_norm'], 'v_fox_w_out': out['v_fox_w_out'], 'v_ffn_w_up': out['v_ffn_w_up'], 'v_ffn_conv_w': out['v_ffn_conv_w'], 'v_ffn_conv_b': out['v_ffn_conv_b'], 'v_ffn_w_down': out['v_ffn_w_down'], 'v_norm_final': out['v_norm_final']}


def _loss(weights, diff, rest, loss_target):
    with _jax.named_scope("forward"):
        args = {**rest, TWIN_DIFF_INPUT: diff, **{k: w.astype(_WEIGHT_DTYPES[k]) for k, w in weights.items()}}
        y = _forward(args)
    with _jax.named_scope("loss_head"):
        err = _jnp.square(y.astype(_jnp.float32) - loss_target)
        return 0.5 * _jnp.sum(_jnp.mean(err, axis=-1)) if err.ndim else 0.5 * err


def _adamw(w, g, m, v):
    m = ADAM_B1 * m + (1.0 - ADAM_B1) * g
    v = ADAM_B2 * v + (1.0 - ADAM_B2) * _jnp.square(g)
    m_hat = m / (1.0 - ADAM_B1 ** ADAM_STEP)
    v_hat = v / (1.0 - ADAM_B2 ** ADAM_STEP)
    delta = -ADAM_LR * (m_hat / (_jnp.sqrt(v_hat) + ADAM_EPS) + ADAM_WD * w)
    return delta, m, v


def reference(x, c, w_mod, b_mod, norm_mix, norm_ffn, gla_w_in, gla_w_gate, gla_b_gate, gla_norm, gla_w_out, fox_w_in, fox_b_f, fox_q_norm, fox_k_norm, fox_w_out, ffn_w_up, ffn_conv_w, ffn_conv_b, ffn_w_down, norm_final, loss_target, m_w_mod, m_b_mod, m_norm_mix, m_norm_ffn, m_gla_w_in, m_gla_w_gate, m_gla_b_gate, m_gla_norm, m_gla_w_out, m_fox_w_in, m_fox_b_f, m_fox_q_norm, m_fox_k_norm, m_fox_w_out, m_ffn_w_up, m_ffn_conv_w, m_ffn_conv_b, m_ffn_w_down, m_norm_final, v_w_mod, v_b_mod, v_norm_mix, v_norm_ffn, v_gla_w_in, v_gla_w_gate, v_gla_b_gate, v_gla_norm, v_gla_w_out, v_fox_w_in, v_fox_b_f, v_fox_q_norm, v_fox_k_norm, v_fox_w_out, v_ffn_w_up, v_ffn_conv_w, v_ffn_conv_b, v_ffn_w_down, v_norm_final):
    given = dict(x=x, c=c, w_mod=w_mod, b_mod=b_mod, norm_mix=norm_mix, norm_ffn=norm_ffn, gla_w_in=gla_w_in, gla_w_gate=gla_w_gate, gla_b_gate=gla_b_gate, gla_norm=gla_norm, gla_w_out=gla_w_out, fox_w_in=fox_w_in, fox_b_f=fox_b_f, fox_q_norm=fox_q_norm, fox_k_norm=fox_k_norm, fox_w_out=fox_w_out, ffn_w_up=ffn_w_up, ffn_conv_w=ffn_conv_w, ffn_conv_b=ffn_conv_b, ffn_w_down=ffn_w_down, norm_final=norm_final, loss_target=loss_target, m_w_mod=m_w_mod, m_b_mod=m_b_mod, m_norm_mix=m_norm_mix, m_norm_ffn=m_norm_ffn, m_gla_w_in=m_gla_w_in, m_gla_w_gate=m_gla_w_gate, m_gla_b_gate=m_gla_b_gate, m_gla_norm=m_gla_norm, m_gla_w_out=m_gla_w_out, m_fox_w_in=m_fox_w_in, m_fox_b_f=m_fox_b_f, m_fox_q_norm=m_fox_q_norm, m_fox_k_norm=m_fox_k_norm, m_fox_w_out=m_fox_w_out, m_ffn_w_up=m_ffn_w_up, m_ffn_conv_w=m_ffn_conv_w, m_ffn_conv_b=m_ffn_conv_b, m_ffn_w_down=m_ffn_w_down, m_norm_final=m_norm_final, v_w_mod=v_w_mod, v_b_mod=v_b_mod, v_norm_mix=v_norm_mix, v_norm_ffn=v_norm_ffn, v_gla_w_in=v_gla_w_in, v_gla_w_gate=v_gla_w_gate, v_gla_b_gate=v_gla_b_gate, v_gla_norm=v_gla_norm, v_gla_w_out=v_gla_w_out, v_fox_w_in=v_fox_w_in, v_fox_b_f=v_fox_b_f, v_fox_q_norm=v_fox_q_norm, v_fox_k_norm=v_fox_k_norm, v_fox_w_out=v_fox_w_out, v_ffn_w_up=v_ffn_w_up, v_ffn_conv_w=v_ffn_conv_w, v_ffn_conv_b=v_ffn_conv_b, v_ffn_w_down=v_ffn_w_down, v_norm_final=v_norm_final)
    weights = {n: given[n] for n in TWIN_WEIGHTS}
    shared = {n: given[n] for n in SHARED_INPUTS}
    per_example = {n: given[n] for n in ['x', 'c']}
    grad_fn = _jax.value_and_grad(_loss, argnums=(0, 1))

    def one_microbatch(ex, loss_target):
        ex = dict(ex)
        diff = ex.pop(TWIN_DIFF_INPUT)
        return grad_fn(weights, diff, {**shared, **ex}, loss_target)

    if N_MICROBATCH == 1:
        loss, (grad_w, grad_x) = one_microbatch(per_example, given["loss_target"])
    else:
        def body(carry, xs):
            loss_sum, grad_sum = carry
            l_k, (gw_k, gx_k) = one_microbatch(xs[0], xs[1])
            with _jax.named_scope("update"):
                return (loss_sum + l_k, _jax.tree.map(_jnp.add, grad_sum, gw_k)), gx_k

        init = (_jnp.zeros((), _jnp.float32), _jax.tree.map(_jnp.zeros_like, weights))
        (loss, grad_w), grad_x = _jax.lax.scan(body, init, (per_example, given["loss_target"]))
    with _jax.named_scope("update"):
        delta_w, new_m, new_v = {}, {}, {}
        for n in TWIN_WEIGHTS:
            delta_w[n], new_m[n], new_v[n] = _adamw(weights[n], grad_w[n], given["m_" + n], given["v_" + n])
    return (loss, grad_x, *[grad_w[n] for n in TWIN_WEIGHTS], *[delta_w[n] for n in TWIN_WEIGHTS],
            *[new_m[n] for n in TWIN_WEIGHTS], *[new_v[n] for n in TWIN_WEIGHTS])
```

```python
import jax
import jax.numpy as jnp
from jax import lax
from jax.experimental import pallas as pl
from jax.experimental.pallas import tpu as pltpu

F32, BF16 = jnp.float32, jnp.bfloat16
N_DEV = 8
GLA_HEADS = 4
GLA_TAU = 16.0
GLA_CHUNK = 64
NORM_EPS = 1e-6
ADAM_LR, ADAM_B1, ADAM_B2, ADAM_EPS, ADAM_WD, ADAM_STEP = 0.001, 0.9, 0.999, 1e-08, 0.01, 10
LANE = 128
VMEM_LIMIT = 56 * 1024 * 1024
NEG = -1e30


def _pcall(body, **kw):
    return pl.pallas_call(body, **kw)


def _params(n_axes):
    return pltpu.CompilerParams(dimension_semantics=("arbitrary",) * n_axes, vmem_limit_bytes=VMEM_LIMIT)


def _tile(dim, pref):
    if dim <= pref:
        return dim
    t = pref
    while dim % t:
        t -= LANE
    assert t > 0, (dim, pref)
    return t


def _dot(a, b, ta=False, tb=False):
    dims = (((0,) if ta else (1,), (1,) if tb else (0,)), ((), ()))
    return lax.dot_general(a.astype(BF16), b.astype(BF16), dims, preferred_element_type=F32)


def _split3(x):
    hi = x.astype(BF16)
    r1 = x - hi.astype(F32)
    mid = r1.astype(BF16)
    lo = (r1 - mid.astype(F32)).astype(BF16)
    return hi, mid, lo


def _tri_matmul(tri, x):
    hi, mid, lo = _split3(x)
    return _dot(tri, hi) + _dot(tri, mid) + _dot(tri, lo)


def _tri(n, upper=False):
    r = lax.broadcasted_iota(jnp.int32, (n, n), 0)
    c = lax.broadcasted_iota(jnp.int32, (n, n), 1)
    return jnp.where((r <= c) if upper else (r >= c), 1.0, 0.0).astype(BF16)


def _log_sigmoid(x):
    return jnp.minimum(x, 0.0) - jnp.log(1.0 + jnp.exp(-jnp.abs(x)))


def _sigmoid(x):
    return 1.0 / (1.0 + jnp.exp(-x))


def _silu(x):
    return x * _sigmoid(x)


def _dsilu(x):
    s = _sigmoid(x)
    return s * (1.0 + x * (1.0 - s))


def _matmul(a, b, *, name, ta=False, tb=False, out_dtypes=(F32,), tm=1024, tn=1024, tk=2048,
            epilogue=None, extras=()):
    m, k = (a.shape[1], a.shape[0]) if ta else a.shape
    n = b.shape[0] if tb else b.shape[1]
    assert (b.shape[1] if tb else b.shape[0]) == k, (a.shape, b.shape, ta, tb)
    tm, tn, tk = _tile(m, tm), _tile(n, tn), _tile(k, tk)
    nk = k // tk
    a_spec = pl.BlockSpec((tk, tm), lambda i, j, kk: (kk, i)) if ta else pl.BlockSpec((tm, tk), lambda i, j, kk: (i, kk))
    b_spec = pl.BlockSpec((tn, tk), lambda i, j, kk: (j, kk)) if tb else pl.BlockSpec((tk, tn), lambda i, j, kk: (kk, j))
    ex_specs = []
    for kind, arr in extras:
        if kind == "mn":
            assert arr.shape == (m, n), (arr.shape, m, n)
            ex_specs.append(pl.BlockSpec((tm, tn), lambda i, j, kk: (i, j)))
        else:
            assert arr.shape == (1, n), (arr.shape, n)
            ex_specs.append(pl.BlockSpec((1, tn), lambda i, j, kk: (0, j)))
    n_ex, n_out = len(extras), len(out_dtypes)

    def body(a_ref, b_ref, *rest):
        ex, outs, acc = rest[:n_ex], rest[n_ex:n_ex + n_out], rest[-1]
        kk = pl.program_id(2)

        @pl.when(kk == 0)
        def _():
            acc[...] = jnp.zeros_like(acc)

        acc[...] += _dot(a_ref[...], b_ref[...], ta, tb)

        @pl.when(kk == nk - 1)
        def _():
            if epilogue is None:
                vals = (acc[...],)
            else:
                vals = epilogue(acc[...], *[e[...] for e in ex])
            for o, v in zip(outs, vals):
                o[...] = v.astype(o.dtype)

    res = _pcall(
        body, name=name, grid=(m // tm, n // tn, nk),
        in_specs=[a_spec, b_spec] + ex_specs,
        out_specs=[pl.BlockSpec((tm, tn), lambda i, j, kk: (i, j))] * n_out,
        out_shape=[jax.ShapeDtypeStruct((m, n), d) for d in out_dtypes],
        scratch_shapes=[pltpu.VMEM((tm, tn), F32)],
        compiler_params=_params(3),
    )(a, b, *[arr for _, arr in extras])
    return res[0] if n_out == 1 else res


def _rowwise(fn, ins, outs, *, name, tr=128):
    rows = next(arr.shape[0] for kind, arr in ins if kind == "row")
    tr = _tile(rows, tr)
    in_specs = []
    for kind, arr in ins:
        if kind == "row":
            assert arr.shape[0] == rows and arr.ndim == 2
            in_specs.append(pl.BlockSpec((tr, arr.shape[1]), lambda i: (i, 0)))
        else:
            in_specs.append(pl.BlockSpec(arr.shape, lambda i, nd=arr.ndim: (0,) * nd))
    out_specs, out_shape = [], []
    for kind, w, dt in outs:
        if kind == "row":
            out_specs.append(pl.BlockSpec((tr, w), lambda i: (i, 0)))
            out_shape.append(jax.ShapeDtypeStruct((rows, w), dt))
        else:
            out_specs.append(pl.BlockSpec((1, w), lambda i: (0, 0)))
            out_shape.append(jax.ShapeDtypeStruct((1, w), dt))
    n_in = len(ins)

    def body(*refs):
        i = pl.program_id(0)
        vals = fn(*[r[...] for r in refs[:n_in]])
        for (kind, _, _), o, v in zip(outs, refs[n_in:], vals):
            if kind == "row":
                o[...] = v.astype(o.dtype)
            else:
                @pl.when(i == 0)
                def _(o=o):
                    o[...] = jnp.zeros_like(o)

                o[...] += v.astype(o.dtype)

    return _pcall(body, name=name, grid=(rows // tr,), in_specs=in_specs, out_specs=out_specs,
                  out_shape=out_shape, compiler_params=_params(1))(*[arr for _, arr in ins])


def _colsum(x):
    return jnp.sum(x, axis=0, keepdims=True)


def _norm_stats(x):
    rstd = lax.rsqrt(jnp.mean(x * x, axis=-1, keepdims=True) + NORM_EPS)
    return x * rstd, rstd


def _norm_bwd(dxhat, xhat, rstd):
    return rstd * (dxhat - xhat * jnp.mean(dxhat * xhat, axis=-1, keepdims=True))


def _adaln_fwd(x, gain, sc, sh, name):
    def fn(x, gain, sc, sh):
        xhat, _ = _norm_stats(x)
        return ((xhat * gain) * (1.0 + sc) + sh,)

    return _rowwise(fn, [("row", x), ("full", gain), ("full", sc), ("full", sh)],
                    [("row", x.shape[1], BF16)], name=name)[0]


def _adaln_bwd(x, dh, dres, gain, sc, name):
    d = x.shape[1]

    def fn(x, dh, dres, gain, sc):
        xhat, rstd = _norm_stats(x)
        dxhat = dh * (gain * (1.0 + sc))
        dx = dres + _norm_bwd(dxhat, xhat, rstd)
        return dx, _colsum(dh), _colsum(dh * (xhat * gain)), _colsum(dh * xhat * (1.0 + sc))

    return _rowwise(fn, [("row", x), ("row", dh), ("row", dres), ("full", gain), ("full", sc)],
                    [("row", d, F32), ("acc", d, F32), ("acc", d, F32), ("acc", d, F32)], name=name)


def _residual_bwd(dx, y, g, name):
    d = dx.shape[1]

    def fn(dx, y, g):
        return dx * (1.0 + g), _colsum(dx * y)

    return _rowwise(fn, [("row", dx), ("row", y), ("full", g)], [("row", d, BF16), ("acc", d, F32)], name=name)


def _final_loss(x, target, gain, name):
    d = x.shape[1]

    def fn(x, t, gain):
        xhat, rstd = _norm_stats(x)
        err = xhat * gain - t
        dy = err * (1.0 / d)
        loss = 0.5 * jnp.sum(jnp.mean(err * err, axis=-1, keepdims=True), axis=0, keepdims=True)
        dx = _norm_bwd(dy * gain, xhat, rstd)
        return dx, _colsum(dy * xhat), jnp.broadcast_to(loss, (1, LANE))

    return _rowwise(fn, [("row", x), ("row", target), ("full", gain)],
                    [("row", d, F32), ("acc", d, F32), ("acc", LANE, F32)], name=name)


def _gla_gates(q_ref, k_ref, a_ref, wg_ref, bg_ref, scale, c):
    ga = _dot(a_ref[...], wg_ref[...]) + bg_ref[...]
    la = _log_sigmoid(ga) * (1.0 / GLA_TAU)
    b = _tri_matmul(_tri(c), la)
    bl = _colsum(la)
    eb, enb, eend = jnp.exp(b), jnp.exp(-b), jnp.exp(bl - b)
    q = q_ref[...] * scale
    k = k_ref[...]
    return dict(ga=ga, eb=eb, enb=enb, eend=eend, dec=jnp.exp(bl), q_dec=q * eb, k_inv=k * enb, k_end=k * eend)


def _causal(c):
    return lax.broadcasted_iota(jnp.int32, (c, c), 0) >= lax.broadcasted_iota(jnp.int32, (c, c), 1)


def _gla_specs(heads, c, dk, dv, rev, n_chunks):
    def ch(n):
        return (n_chunks - 1 - n) if rev else n

    return [
        pl.BlockSpec((c, dk), lambda h, n: (ch(n), h)),
        pl.BlockSpec((c, dk), lambda h, n: (ch(n), heads + h)),
        pl.BlockSpec((c, dv), lambda h, n: (ch(n), heads + h)),
        pl.BlockSpec((c, LANE), lambda h, n: (ch(n), 0)),
        pl.BlockSpec((LANE, dk), lambda h, n: (0, h)),
        pl.BlockSpec((1, dk), lambda h, n: (0, h)),
    ]


def _gla_fwd(proj, a_tail, wg_p, bg, name):
    s = proj.shape[0]
    heads, c = GLA_HEADS, GLA_CHUNK
    dk = wg_p.shape[1] // heads
    dv = 2 * dk
    n_chunks = s // c
    scale = dk ** -0.5

    def body(q_ref, k_ref, v_ref, a_ref, wg_ref, bg_ref, o_ref, st_ref, state):
        @pl.when(pl.program_id(1) == 0)
        def _():
            state[...] = jnp.zeros_like(state)

        g = _gla_gates(q_ref, k_ref, a_ref, wg_ref, bg_ref, scale, c)
        v = v_ref[...]
        st = state[...]
        attn = jnp.where(_causal(c), _dot(g["q_dec"], g["k_inv"], tb=True), 0.0)
        o_ref[...] = _dot(attn, v) + _dot(g["q_dec"], st, tb=True)
        st_ref[...] = st.astype(st_ref.dtype)
        state[...] = g["dec"] * st + _dot(v, g["k_end"], ta=True)

    return _pcall(
        body, name=name, grid=(heads, n_chunks),
        in_specs=_gla_specs(heads, c, dk, dv, False, n_chunks),
        out_specs=[pl.BlockSpec((c, dv), lambda h, n: (n, h)),
                   pl.BlockSpec((None, None, dv, dk), lambda h, n: (h, n, 0, 0))],
        out_shape=[jax.ShapeDtypeStruct((s, heads * dv), F32),
                   jax.ShapeDtypeStruct((heads, n_chunks, dv, dk), BF16)],
        scratch_shapes=[pltpu.VMEM((dv, dk), F32)],
        compiler_params=_params(2),
    )(proj, proj, proj, a_tail, wg_p, bg)


def _gla_bwd(proj, a_tail, wg_p, bg, states, d_o, name):
    s = proj.shape[0]
    heads, c = GLA_HEADS, GLA_CHUNK
    dk = wg_p.shape[1] // heads
    dv = 2 * dk
    n_chunks = s // c
    scale = dk ** -0.5

    def body(q_ref, k_ref, v_ref, a_ref, wg_ref, bg_ref, st_ref, do_ref, dq_ref, dk_ref, dv_ref, dga_ref, dstate):
        @pl.when(pl.program_id(1) == 0)
        def _():
            dstate[...] = jnp.zeros_like(dstate)

        g = _gla_gates(q_ref, k_ref, a_ref, wg_ref, bg_ref, scale, c)
        v, st, dst, d_out = v_ref[...], st_ref[...], dstate[...], do_ref[...]
        q_dec, k_inv, k_end = g["q_dec"], g["k_inv"], g["k_end"]
        mask = _causal(c)
        attn = jnp.where(mask, _dot(q_dec, k_inv, tb=True), 0.0)
        d_attn = jnp.where(mask, _dot(d_out, v, tb=True), 0.0)
        d_qdec = _dot(d_attn, k_inv) + _dot(d_out, st)
        d_kinv = _dot(d_attn, q_dec, ta=True)
        d_kend = _dot(v, dst)
        dv_ref[...] = (_dot(attn, d_out, ta=True) + _dot(k_end, dst, tb=True)).astype(dv_ref.dtype)
        d_dec = jnp.sum(dst * st.astype(F32), axis=0, keepdims=True)
        dstate[...] = g["dec"] * dst + _dot(d_out, q_dec, ta=True)

        dq_ref[...] = (d_qdec * (scale * g["eb"])).astype(dq_ref.dtype)
        dk_ref[...] = (d_kinv * g["enb"] + d_kend * g["eend"]).astype(dk_ref.dtype)
        kk = d_kend * k_end
        db = d_qdec * q_dec - d_kinv * k_inv - kk
        dbl = jnp.sum(kk, axis=0, keepdims=True) + d_dec * g["dec"]
        last = lax.broadcasted_iota(jnp.int32, db.shape, 0) == c - 1
        db = db + jnp.where(last, dbl, 0.0)
        dla = _tri_matmul(_tri(c, upper=True), db)
        dga_ref[...] = dla * (1.0 / GLA_TAU) * _sigmoid(-g["ga"])

    rev = lambda h, n: (n_chunks - 1 - n, h)
    return _pcall(
        body, name=name, grid=(heads, n_chunks),
        in_specs=_gla_specs(heads, c, dk, dv, True, n_chunks) + [
            pl.BlockSpec((None, None, dv, dk), lambda h, n: (h, n_chunks - 1 - n, 0, 0)),
            pl.BlockSpec((c, dv), rev)],
        out_specs=[pl.BlockSpec((c, dk), rev), pl.BlockSpec((c, dk), rev), pl.BlockSpec((c, dv), rev),
                   pl.BlockSpec((c, dk), rev)],
        out_shape=[jax.ShapeDtypeStruct((s, heads * dk), BF16), jax.ShapeDtypeStruct((s, heads * dk), BF16),
                   jax.ShapeDtypeStruct((s, heads * dv), BF16), jax.ShapeDtypeStruct((s, heads * dk), F32)],
        scratch_shapes=[pltpu.VMEM((dv, dk), F32)],
        compiler_params=_params(2),
    )(proj, proj, proj, a_tail, wg_p, bg, states, d_o)


def _gla_post_fwd(o, r, gn, name):
    dvt = o.shape[1]
    dv = dvt // GLA_HEADS

    def fn(o, r, gn):
        outs = []
        for h in range(GLA_HEADS):
            sl = slice(h * dv, (h + 1) * dv)
            ohat, _ = _norm_stats(o[:, sl])
            outs.append((ohat * gn[:, sl]) * _silu(r[:, sl]))
        return (jnp.concatenate(outs, axis=1),)

    return _rowwise(fn, [("row", o), ("row", r), ("full", gn)], [("row", dvt, BF16)], name=name)[0]


def _gla_post_bwd(o, r, gn, dog, name):
    dvt = o.shape[1]
    dv = dvt // GLA_HEADS

    def fn(o, r, gn, dog):
        d_o, d_r, d_g = [], [], []
        for h in range(GLA_HEADS):
            sl = slice(h * dv, (h + 1) * dv)
            ohat, rstd = _norm_stats(o[:, sl])
            g, rr, dd = gn[:, sl], r[:, sl], dog[:, sl]
            d_r.append(dd * (ohat * g) * _dsilu(rr))
            don = dd * _silu(rr)
            d_g.append(_colsum(don * ohat))
            d_o.append(_norm_bwd(don * g, ohat, rstd))
        return jnp.concatenate(d_o, axis=1), jnp.concatenate(d_r, axis=1), jnp.concatenate(d_g, axis=1)

    return _rowwise(fn, [("row", o), ("row", r), ("full", gn), ("row", dog)],
                    [("row", dvt, F32), ("row", dvt, BF16), ("acc", dvt, F32)], name=name)


def _fox_prep(q, k, v, qg, kg, hd, name):
    d = q.shape[1]
    heads = d // hd
    scale = hd ** -0.5

    def fn(q, k, v, qg, kg):
        qs, ks = [], []
        for h in range(heads):
            sl = slice(h * hd, (h + 1) * hd)
            qs.append(_norm_stats(q[:, sl])[0] * qg * scale)
            ks.append(_norm_stats(k[:, sl])[0] * kg)
        return jnp.concatenate(qs, axis=1), jnp.concatenate(ks, axis=1), v

    return _rowwise(fn, [("row", q), ("row", k), ("row", v), ("full", qg), ("full", kg)],
                    [("row", d, BF16)] * 3, name=name)


def _fox_prep_bwd(q, k, dqn, dkn, qg, kg, hd, name):
    d = q.shape[1]
    heads = d // hd
    scale = hd ** -0.5

    def fn(q, k, dqn, dkn, qg, kg):
        dq, dk, gq, gk = [], [], [], []
        for h in range(heads):
            sl = slice(h * hd, (h + 1) * hd)
            for x, dxn, g, s, dl, gl in ((q, dqn, qg, scale, dq, gq), (k, dkn, kg, 1.0, dk, gk)):
                xhat, rstd = _norm_stats(x[:, sl])
                dn = dxn[:, sl] * s
                gl.append(_colsum(dn * xhat))
                dl.append(_norm_bwd(dn * g, xhat, rstd))
        cat = lambda t: jnp.concatenate(t, axis=1)
        return cat(dq), cat(dk), cat(gq), cat(gk)

    return _rowwise(fn, [("row", q), ("row", k), ("row", dqn), ("row", dkn), ("full", qg), ("full", kg)],
                    [("row", d, BF16), ("row", d, BF16), ("acc", d, F32), ("acc", d, F32)], name=name)


def _fox_cum(fl, bf_p, name, tb=256):
    s = fl.shape[0]
    tb = _tile(s, tb)

    def body(fl_ref, bf_ref, cum_ref, carry):
        @pl.when(pl.program_id(0) == 0)
        def _():
            carry[...] = jnp.zeros_like(carry)

        lf = _log_sigmoid(fl_ref[...] + bf_ref[...])
        cum_ref[...] = _tri_matmul(_tri(tb), lf) + carry[...]
        carry[...] += _colsum(lf)

    return _pcall(
        body, name=name, grid=(s // tb,),
        in_specs=[pl.BlockSpec((tb, LANE), lambda i: (i, 0)), pl.BlockSpec((1, LANE), lambda i: (0, 0))],
        out_specs=pl.BlockSpec((tb, LANE), lambda i: (i, 0)),
        out_shape=jax.ShapeDtypeStruct((s, LANE), F32),
        scratch_shapes=[pltpu.VMEM((1, LANE), F32)],
        compiler_params=_params(1),
    )(fl, bf_p)


def _fox_cum_bwd(dcum, fl, bf_p, name, tb=256):
    s = fl.shape[0]
    tb = _tile(s, tb)
    nb = s // tb

    def body(dc_ref, fl_ref, bf_ref, dfl_ref, dbf_ref, carry):
        @pl.when(pl.program_id(0) == 0)
        def _():
            carry[...] = jnp.zeros_like(carry)
            dbf_ref[...] = jnp.zeros_like(dbf_ref)

        dc = dc_ref[...]
        dlf = _tri_matmul(_tri(tb, upper=True), dc) + carry[...]
        carry[...] += _colsum(dc)
        dfl = dlf * _sigmoid(-(fl_ref[...] + bf_ref[...]))
        dfl_ref[...] = dfl
        dbf_ref[...] += _colsum(dfl)

    rev = lambda i: (nb - 1 - i, 0)
    return _pcall(
        body, name=name, grid=(nb,),
        in_specs=[pl.BlockSpec((tb, LANE), rev), pl.BlockSpec((tb, LANE), rev), pl.BlockSpec((1, LANE), lambda i: (0, 0))],
        out_specs=[pl.BlockSpec((tb, LANE), rev), pl.BlockSpec((1, LANE), lambda i: (0, 0))],
        out_shape=[jax.ShapeDtypeStruct((s, LANE), F32), jax.ShapeDtypeStruct((1, LANE), F32)],
        scratch_shapes=[pltpu.VMEM((1, LANE), F32)],
        compiler_params=_params(1),
    )(dcum, fl, bf_p)


def _fox_attn_fwd(qn, kn, vb, cum_col, cum_row, hd, t, name):
    s, d = qn.shape
    heads = d // hd
    nq = s // t

    def body(q_ref, k_ref, v_ref, cc_ref, cr_ref, o_ref, lse_ref):
        qi = pl.program_id(1)
        q = q_ref[...]
        cq = cc_ref[...]
        qpos = qi * t + lax.broadcasted_iota(jnp.int32, (t, 1), 0)

        def step(kj, carry):
            m, l, acc = carry
            off = pl.multiple_of(kj * t, t)
            ks, vs = k_ref[pl.ds(off, t), :], v_ref[pl.ds(off, t), :]
            sc = _dot(q, ks, tb=True) + cq - cr_ref[kj]
            kpos = off + lax.broadcasted_iota(jnp.int32, (1, t), 1)
            sc = jnp.where(kpos <= qpos, sc, NEG)
            m_new = jnp.maximum(m, jnp.max(sc, axis=1, keepdims=True))
            alpha = jnp.exp(m - m_new)
            p = jnp.exp(sc - m_new)
            return m_new, alpha * l + jnp.sum(p, axis=1, keepdims=True), alpha * acc + _dot(p, vs)

        init = (jnp.full((t, 1), NEG, F32), jnp.zeros((t, 1), F32), jnp.zeros((t, hd), F32))
        m, l, acc = lax.fori_loop(0, qi + 1, step, init)
        o_ref[...] = acc / l
        lse_ref[...] = m + jnp.log(l)

    return _pcall(
        body, name=name, grid=(heads, nq),
        in_specs=[pl.BlockSpec((t, hd), lambda h, i: (i, h)),
                  pl.BlockSpec((s, hd), lambda h, i: (0, h)),
                  pl.BlockSpec((s, hd), lambda h, i: (0, h)),
                  pl.BlockSpec((None, t, 1), lambda h, i: (h, i, 0)),
                  pl.BlockSpec((None, nq, 1, t), lambda h, i: (h, 0, 0, 0))],
        out_specs=[pl.BlockSpec((t, hd), lambda h, i: (i, h)), pl.BlockSpec((None, t, 1), lambda h, i: (h, i, 0))],
        out_shape=[jax.ShapeDtypeStruct((s, d), F32), jax.ShapeDtypeStruct((heads, s, 1), F32)],
        compiler_params=_params(2),
    )(qn, kn, vb, cum_col, cum_row)


def _fox_attn_bwd(qn, kn, vb, d_o, o, lse, cum_col, cum_row, hd, t, name):
    s, d = qn.shape
    heads = d // hd
    nq = s // t

    def body(q_ref, k_ref, v_ref, do_ref, o_ref, lse_ref, cc_ref, cr_ref,
             dq_ref, dk_ref, dv_ref, dcq_ref, dck_ref, delta):
        kj = pl.program_id(1)

        @pl.when(kj == 0)
        def _():
            dq_ref[...] = jnp.zeros_like(dq_ref)
            dcq_ref[...] = jnp.zeros_like(dcq_ref)
            delta[...] = jnp.sum(do_ref[...] * o_ref[...], axis=1, keepdims=True)

        ks, vs, cr = k_ref[...], v_ref[...], cr_ref[...]
        kpos = kj * t + lax.broadcasted_iota(jnp.int32, (1, t), 1)

        def step(qi, carry):
            dk, dv, dck = carry
            rows = pl.ds(pl.multiple_of(qi * t, t), t)
            q, d_out = q_ref[rows, :], do_ref[rows, :]
            sc = _dot(q, ks, tb=True) + cc_ref[rows, :] - cr
            qpos = qi * t + lax.broadcasted_iota(jnp.int32, (t, 1), 0)
            p = jnp.where(kpos <= qpos, jnp.exp(sc - lse_ref[rows, :]), 0.0)
            ds = p * (_dot(d_out, vs, tb=True) - delta[rows, :])
            dq_ref[rows, :] += _dot(ds, ks)
            dcq_ref[rows, :] += jnp.sum(ds, axis=1, keepdims=True)
            return dk + _dot(ds, q, ta=True), dv + _dot(p, d_out, ta=True), dck + _colsum(ds)

        init = (jnp.zeros((t, hd), F32), jnp.zeros((t, hd), F32), jnp.zeros((1, t), F32))
        dk, dv, dck = lax.fori_loop(kj, nq, step, init)
        dk_ref[...] = dk.astype(dk_ref.dtype)
        dv_ref[...] = dv.astype(dv_ref.dtype)
        dck_ref[...] = dck

    head_rows = lambda h, j: (0, h)
    blk = lambda h, j: (j, h)
    return _pcall(
        body, name=name, grid=(heads, nq),
        in_specs=[pl.BlockSpec((s, hd), head_rows), pl.BlockSpec((t, hd), blk), pl.BlockSpec((t, hd), blk),
                  pl.BlockSpec((s, hd), head_rows), pl.BlockSpec((s, hd), head_rows),
                  pl.BlockSpec((None, s, 1), lambda h, j: (h, 0, 0)),
                  pl.BlockSpec((None, s, 1), lambda h, j: (h, 0, 0)),
                  pl.BlockSpec((None, None, 1, t), lambda h, j: (h, j, 0, 0))],
        out_specs=[pl.BlockSpec((s, hd), head_rows), pl.BlockSpec((t, hd), blk), pl.BlockSpec((t, hd), blk),
                   pl.BlockSpec((None, s, 1), lambda h, j: (h, 0, 0)),
                   pl.BlockSpec((None, None, 1, t), lambda h, j: (h, j, 0, 0))],
        out_shape=[jax.ShapeDtypeStruct((s, d), F32), jax.ShapeDtypeStruct((s, d), BF16),
                   jax.ShapeDtypeStruct((s, d), BF16), jax.ShapeDtypeStruct((heads, s, 1), F32),
                   jax.ShapeDtypeStruct((heads, nq, 1, t), F32)],
        scratch_shapes=[pltpu.VMEM((s, 1), F32)],
        compiler_params=_params(2),
    )(qn, kn, vb, d_o, o, lse, cum_col, cum_row)


def _fox_gate_fwd(o, og, name):
    def fn(o, og):
        return (o * _sigmoid(og),)

    return _rowwise(fn, [("row", o), ("row", og)], [("row", o.shape[1], BF16)], name=name)[0]


def _fox_gate_bwd(o, og, dact, name):
    def fn(o, og, dact):
        sg = _sigmoid(og)
        return dact * sg, dact * o * sg * (1.0 - sg)

    d = o.shape[1]
    return _rowwise(fn, [("row", o), ("row", og), ("row", dact)], [("row", d, F32), ("row", d, BF16)], name=name)


def _shift_down(x, n):
    rows = lax.broadcasted_iota(jnp.int32, x.shape, 0)
    return jnp.where(rows >= n, pltpu.roll(x, n, 0), 0.0)


def _shift_up(x, n):
    rows = lax.broadcasted_iota(jnp.int32, x.shape, 0)
    return jnp.where(rows < x.shape[0] - n, pltpu.roll(x, x.shape[0] - n, 0), 0.0)


def _conv(u, w_ref, b):
    return w_ref[0:1, :] * _shift_down(u, 2) + w_ref[1:2, :] * _shift_down(u, 1) + w_ref[2:3, :] * u + b


def _conv_act_fwd(u, cw, cb, name, tc=256):
    s, two_f = u.shape
    dff = two_f // 2
    tc = _tile(dff, tc)
    nb = dff // tc

    def body(ug_ref, uv_ref, wg_ref, wv_ref, bg_ref, bv_ref, a_ref):
        gate = _conv(ug_ref[...], wg_ref, bg_ref[...])
        val = _conv(uv_ref[...], wv_ref, bv_ref[...])
        a_ref[...] = (_silu(gate) * val).astype(a_ref.dtype)

    lo, hi = (lambda j: (0, j)), (lambda j: (0, j + nb))
    return _pcall(
        body, name=name, grid=(nb,),
        in_specs=[pl.BlockSpec((s, tc), lo), pl.BlockSpec((s, tc), hi), pl.BlockSpec((3, tc), lo),
                  pl.BlockSpec((3, tc), hi), pl.BlockSpec((1, tc), lo), pl.BlockSpec((1, tc), hi)],
        out_specs=pl.BlockSpec((s, tc), lo),
        out_shape=jax.ShapeDtypeStruct((s, dff), BF16),
        compiler_params=_params(1),
    )(u, u, cw, cw, cb, cb)


def _conv_act_bwd(u, cw, cb, da, name, tc=128):
    s, two_f = u.shape
    dff = two_f // 2
    tc = _tile(dff, tc)
    nb = dff // tc

    def body(u_ref, up_ref, w_ref, wp_ref, b_ref, bp_ref, da_ref, du_ref, dw_ref, db_ref):
        is_gate = pl.program_id(0) < nb
        u = u_ref[...]
        mine = _conv(u, w_ref, b_ref[...])
        other = _conv(up_ref[...], wp_ref, bp_ref[...])
        da = da_ref[...]
        dc = jnp.where(is_gate, da * other * _dsilu(mine), da * _silu(other))
        du = w_ref[0:1, :] * _shift_up(dc, 2) + w_ref[1:2, :] * _shift_up(dc, 1) + w_ref[2:3, :] * dc
        du_ref[...] = du.astype(du_ref.dtype)
        dw_ref[0:1, :] = _colsum(dc * _shift_down(u, 2))
        dw_ref[1:2, :] = _colsum(dc * _shift_down(u, 1))
        dw_ref[2:3, :] = _colsum(dc * u)
        db_ref[...] = _colsum(dc)

    own = lambda j: (0, j)
    partner = lambda j: (0, (j + nb) % (2 * nb))
    return _pcall(
        body, name=name, grid=(2 * nb,),
        in_specs=[pl.BlockSpec((s, tc), own), pl.BlockSpec((s, tc), partner), pl.BlockSpec((3, tc), own),
                  pl.BlockSpec((3, tc), partner), pl.BlockSpec((1, tc), own), pl.BlockSpec((1, tc), partner),
                  pl.BlockSpec((s, tc), lambda j: (0, j % nb))],
        out_specs=[pl.BlockSpec((s, tc), own), pl.BlockSpec((3, tc), own), pl.BlockSpec((1, tc), own)],
        out_shape=[jax.ShapeDtypeStruct((s, two_f), BF16), jax.ShapeDtypeStruct((3, two_f), F32),
                   jax.ShapeDtypeStruct((1, two_f), F32)],
        compiler_params=_params(1),
    )(u, u, cw, cw, cb, cb, da)


def _adamw_math(w, g, m, v):
    m = ADAM_B1 * m + (1.0 - ADAM_B1) * g
    v = ADAM_B2 * v + (1.0 - ADAM_B2) * (g * g)
    m_hat = m / (1.0 - ADAM_B1 ** ADAM_STEP)
    v_hat = v / (1.0 - ADAM_B2 ** ADAM_STEP)
    delta = -ADAM_LR * (m_hat / (jnp.sqrt(v_hat) + ADAM_EPS) + ADAM_WD * w)
    return delta, m, v


def _adamw(w, g, m, v, name, pieces=False, tr=128):
    r, c = w.shape
    tr = _tile(r, tr) if r % 8 == 0 else r
    while r % tr:
        tr -= 8

    def body(w_ref, g_ref, m_ref, v_ref, go_ref, d_ref, mo_ref, vo_ref):
        if pieces:
            g = g_ref[0].astype(F32)
            for i in range(1, N_DEV):
                g = g + g_ref[i].astype(F32)
        else:
            g = g_ref[...]
        delta, m, v = _adamw_math(w_ref[...], g, m_ref[...], v_ref[...])
        go_ref[...], d_ref[...], mo_ref[...], vo_ref[...] = g, delta, m, v

    spec = pl.BlockSpec((tr, c), lambda i: (i, 0))
    g_spec = pl.BlockSpec((N_DEV, tr, c), lambda i: (0, i, 0)) if pieces else spec
    return _pcall(
        body, name=name, grid=(r // tr,), in_specs=[spec, g_spec, spec, spec], out_specs=[spec] * 4,
        out_shape=[jax.ShapeDtypeStruct((r, c), F32)] * 4, compiler_params=_params(1),
    )(w, g, m, v)


def _sum8(x, name):
    p = x.shape[2]
    tp = _tile(p, 16 * 1024)

    def body(x_ref, o_ref):
        acc = x_ref[0]
        for i in range(1, N_DEV):
            acc = acc + x_ref[i]
        o_ref[...] = acc

    return _pcall(
        body, name=name, grid=(p // tp,), in_specs=[pl.BlockSpec((N_DEV, 1, tp), lambda i: (0, 0, i))],
        out_specs=pl.BlockSpec((1, tp), lambda i: (0, i)), out_shape=jax.ShapeDtypeStruct((1, p), x.dtype),
        compiler_params=_params(1),
    )(x)


def _exchange(arrays, name, scatter):
    n = len(arrays)
    hbm = pl.BlockSpec(memory_space=pl.ANY)

    def body(*refs):
        ins, outs = refs[:n], refs[n:2 * n]
        send_sems, recv_sems, local_sems = refs[2 * n:]
        x, y, c = lax.axis_index("x"), lax.axis_index("y"), lax.axis_index("c")
        me = 4 * x + 2 * y + c
        copies = []
        for a in range(n):
            src_mine = ins[a].at[me] if scatter else ins[a]
            local = pltpu.make_async_copy(src_mine, outs[a].at[me], local_sems.at[a])
            local.start()
            copies.append(local)
            for k in range(1, N_DEV):
                px = 1 - x if k & 4 else x
                py = 1 - y if k & 2 else y
                pc = 1 - c if k & 1 else c
                src = ins[a].at[4 * px + 2 * py + pc] if scatter else ins[a]
                cp = pltpu.make_async_remote_copy(
                    src_ref=src, dst_ref=outs[a].at[me],
                    send_sem=send_sems.at[a * (N_DEV - 1) + k - 1], recv_sem=recv_sems.at[a * (N_DEV - 1) + k - 1],
                    device_id=(px, py, pc), device_id_type=pl.DeviceIdType.MESH)
                cp.start()
                copies.append(cp)
        for cp in copies:
            cp.wait()

    out_shape = [jax.ShapeDtypeStruct(a.shape if scatter else (N_DEV,) + a.shape, a.dtype) for a in arrays]
    return _pcall(
        body, name=name, in_specs=[hbm] * n, out_specs=[hbm] * n, out_shape=out_shape,
        scratch_shapes=[pltpu.SemaphoreType.DMA((n * (N_DEV - 1),)), pltpu.SemaphoreType.DMA((n * (N_DEV - 1),)),
                        pltpu.SemaphoreType.DMA((n,))],
        compiler_params=pltpu.CompilerParams(has_side_effects=True),
    )(*arrays)


def _pad_cols(x, width=LANE):
    return jnp.pad(x, ((0, 0), (0, width - x.shape[1])))


def _cols_full(g):
    return jnp.transpose(g, (1, 0, 2)).reshape(g.shape[1], -1)


def _cols_pieces(dw):
    k = dw.shape[0]
    return jnp.transpose(dw.reshape(k, N_DEV, -1), (1, 0, 2))


def _ffn_fwd(x1, p, i, tag):
    h2 = _adaln_fwd(x1, p["norm_ffn"][i], p["sc_f"][i], p["sh_f"][i], f"ffn_norm_{tag}")
    u = _matmul(h2, p["w_up"][i], name=f"ffn_up_{tag}")
    a = _conv_act_fwd(u, p["conv_w"][i], p["conv_b"][i], f"ffn_act_{tag}")
    g_f = p["g_f"][i]
    x2, f = _matmul(a, p["w_down"][i], name=f"ffn_down_{tag}", tk=512, out_dtypes=(F32, F32),
                    epilogue=lambda acc, x1, g: (x1 + (1.0 + g) * acc, acc), extras=(("mn", x1), ("n", g_f)))
    return x2, dict(h2=h2, u=u, a=a, f=f)


def _ffn_bwd(dx2, x1, saved, p, i, tag):
    df, dg_f = _residual_bwd(dx2, saved["f"], p["g_f"][i], f"ffn_res_bwd_{tag}")
    da = _matmul(df, p["w_down"][i], tb=True, name=f"ffn_down_dx_{tag}", tn=512)
    dw_down = _matmul(saved["a"], df, ta=True, name=f"ffn_down_dw_{tag}", tm=1408, out_dtypes=(BF16,))
    du, dcw, dcb = _conv_act_bwd(saved["u"], p["conv_w"][i], p["conv_b"][i], da, f"ffn_act_bwd_{tag}")
    dh2 = _matmul(du, p["w_up"][i], tb=True, name=f"ffn_up_dx_{tag}", tk=1024)
    dw_up = _matmul(saved["h2"], du, ta=True, name=f"ffn_up_dw_{tag}", out_dtypes=(BF16,))
    dx1, dsh, dsc, dgain = _adaln_bwd(x1, dh2, dx2, p["norm_ffn"][i], p["sc_f"][i], f"ffn_norm_bwd_{tag}")
    grads = dict(w_up=dw_up, w_down=dw_down, conv_w=dcw, conv_b=dcb, norm_ffn=dgain, sh_f=dsh, sc_f=dsc, g_f=dg_f)
    return dx1, grads


def _gla_layer_fwd(x, p, i):
    h1 = _adaln_fwd(x, p["norm_mix"][i], p["sc_m"][i], p["sh_m"][i], "gla_norm")
    proj = _matmul(h1, p["gla_w_main"], name="gla_in")
    a_tail = _matmul(h1, p["gla_w_tail"], name="gla_in_tail")
    dk_total = p["gla_wg_p"].shape[1]
    o, states = _gla_fwd(proj, a_tail, p["gla_wg_p"], p["gla_b_gate"], "gla_chunks")
    r = proj[:, 2 * dk_total + o.shape[1]:]
    og = _gla_post_fwd(o, r, p["gla_norm"], "gla_post")
    x1, y = _matmul(og, p["gla_w_out"], name="gla_out", out_dtypes=(F32, F32),
                    epilogue=lambda acc, x, g: (x + (1.0 + g) * acc, acc), extras=(("mn", x), ("n", p["g_m"][i])))
    return x1, dict(h1=h1, proj=proj, a_tail=a_tail, o=o, r=r, states=states, og=og, y=y)


def _gla_layer_bwd(dx1, x, sv, p, i):
    dy, dg_m = _residual_bwd(dx1, sv["y"], p["g_m"][i], "gla_res_bwd")
    dog = _matmul(dy, p["gla_w_out"], tb=True, name="gla_out_dx")
    dw_out = _matmul(sv["og"], dy, ta=True, name="gla_out_dw", out_dtypes=(BF16,))
    d_o, d_r, dgn = _gla_post_bwd(sv["o"], sv["r"], p["gla_norm"], dog, "gla_post_bwd")
    dq, dk, dv, dga = _gla_bwd(sv["proj"], sv["a_tail"], p["gla_wg_p"], p["gla_b_gate"], sv["states"], d_o,
                               "gla_chunks_bwd")
    da_tail = _matmul(dga, p["gla_wg_p"], tb=True, name="gla_gate_dx", out_dtypes=(BF16,))
    dwg = _matmul(sv["a_tail"], dga, ta=True, name="gla_gate_dw")
    dbg = _rowwise(lambda t: (_colsum(t),), [("row", dga)], [("acc", dga.shape[1], F32)], name="gla_gate_db")[0]
    dproj = jnp.concatenate([dq, dk, dv, d_r], axis=1)
    dh_tail = _matmul(da_tail, p["gla_w_tail"], tb=True, name="gla_in_tail_dx")
    dh1 = _matmul(dproj, p["gla_w_main"], tb=True, name="gla_in_dx", tk=1024,
                  epilogue=lambda acc, t: (acc + t,), extras=(("mn", dh_tail),))
    dw_main = _matmul(sv["h1"], dproj, ta=True, name="gla_in_dw", out_dtypes=(BF16,))
    dw_tail = _matmul(sv["h1"], da_tail, ta=True, name="gla_in_tail_dw", out_dtypes=(BF16,))
    dx, dsh, dsc, dgain = _adaln_bwd(x, dh1, dx1, p["norm_mix"][i], p["sc_m"][i], "gla_norm_bwd")
    rank = p["gla_rank"]
    grads = dict(gla_w_in=jnp.concatenate([dw_main, dw_tail[:, :rank]], axis=1), gla_w_out=dw_out,
                 gla_w_gate=dwg[:rank], gla_b_gate=dbg, gla_norm=dgn, norm_mix=dgain, sh_m=dsh, sc_m=dsc, g_m=dg_m)
    return dx, grads


def _fox_layer_fwd(x, p, i):
    d = x.shape[1]
    hd = p["fox_q_norm"].shape[1]
    heads = d // hd
    s = x.shape[0]
    t = _tile(s, 256)
    h1 = _adaln_fwd(x, p["norm_mix"][i], p["sc_m"][i], p["sh_m"][i], "fox_norm")
    proj = _matmul(h1, p["fox_w_main"], name="fox_in")
    fl = _matmul(h1, p["fox_w_tail"], name="fox_in_tail")
    q, k, v, og = (proj[:, j * d:(j + 1) * d] for j in range(4))
    qn, kn, vb = _fox_prep(q, k, v, p["fox_q_norm"], p["fox_k_norm"], hd, "fox_prep")
    cum = _fox_cum(fl, p["fox_bf_p"], "fox_cum")
    cum_t = jnp.transpose(cum[:, :heads])
    cum_col, cum_row = cum_t[:, :, None], cum_t.reshape(heads, s // t, 1, t)
    o, lse = _fox_attn_fwd(qn, kn, vb, cum_col, cum_row, hd, t, "fox_attn")
    act = _fox_gate_fwd(o, og, "fox_gate")
    x1, y = _matmul(act, p["fox_w_out"], name="fox_out", out_dtypes=(F32, F32),
                    epilogue=lambda acc, x, g: (x + (1.0 + g) * acc, acc), extras=(("mn", x), ("n", p["g_m"][i])))
    return x1, dict(h1=h1, q=q, k=k, og=og, fl=fl, qn=qn, kn=kn, vb=vb, cum_col=cum_col, cum_row=cum_row,
                    o=o, lse=lse, act=act, y=y, t=t, hd=hd)


def _fox_layer_bwd(dx1, x, sv, p, i):
    d = x.shape[1]
    hd, t = sv["hd"], sv["t"]
    heads = d // hd
    s = x.shape[0]
    dy, dg_m = _residual_bwd(dx1, sv["y"], p["g_m"][i], "fox_res_bwd")
    dact = _matmul(dy, p["fox_w_out"], tb=True, name="fox_out_dx")
    dw_out = _matmul(sv["act"], dy, ta=True, name="fox_out_dw", out_dtypes=(BF16,))
    d_o, d_og = _fox_gate_bwd(sv["o"], sv["og"], dact, "fox_gate_bwd")
    dqn, dkn, dvb, dcq, dck = _fox_attn_bwd(sv["qn"], sv["kn"], sv["vb"], d_o, sv["o"], sv["lse"], sv["cum_col"],
                                            sv["cum_row"], hd, t, "fox_attn_bwd")
    dq, dk, gq, gk = _fox_prep_bwd(sv["q"], sv["k"], dqn, dkn, p["fox_q_norm"], p["fox_k_norm"], hd, "fox_prep_bwd")
    dcum = _pad_cols(jnp.transpose(dcq[:, :, 0] - dck.reshape(heads, s)))
    dfl, dbf = _fox_cum_bwd(dcum, sv["fl"], p["fox_bf_p"], "fox_cum_bwd")
    dfl_b = dfl.astype(BF16)
    dproj = jnp.concatenate([dq, dk, dvb, d_og], axis=1)
    dh_tail = _matmul(dfl_b, p["fox_w_tail"], tb=True, name="fox_in_tail_dx")
    dh1 = _matmul(dproj, p["fox_w_main"], tb=True, name="fox_in_dx", tk=1024,
                  epilogue=lambda acc, tl: (acc + tl,), extras=(("mn", dh_tail),))
    dw_main = _matmul(sv["h1"], dproj, ta=True, name="fox_in_dw", out_dtypes=(BF16,))
    dw_tail = _matmul(sv["h1"], dfl_b, ta=True, name="fox_in_tail_dw", out_dtypes=(BF16,))
    dx, dsh, dsc, dgain = _adaln_bwd(x, dh1, dx1, p["norm_mix"][i], p["sc_m"][i], "fox_norm_bwd")
    grads = dict(fox_w_in=jnp.concatenate([dw_main, dw_tail[:, :heads]], axis=1), fox_w_out=dw_out,
                 fox_b_f=dbf[:, :heads], fox_q_norm=gq.reshape(heads, hd).sum(0, keepdims=True),
                 fox_k_norm=gk.reshape(heads, hd).sum(0, keepdims=True), norm_mix=dgain, sh_m=dsh, sc_m=dsc, g_m=dg_m)
    return dx, grads


SMALL = ("b_mod", "norm_mix", "norm_ffn", "gla_b_gate", "gla_norm", "fox_b_f", "fox_q_norm", "fox_k_norm",
         "ffn_conv_b", "norm_final")
SMALL_SHARDED = ("gla_w_gate", "ffn_conv_w")
BIG = ("gla_w_in", "gla_w_out", "fox_w_in", "fox_w_out", "ffn_w_up", "ffn_w_down")
WEIGHTS = ("w_mod", "b_mod", "norm_mix", "norm_ffn", "gla_w_in", "gla_w_gate", "gla_b_gate", "gla_norm", "gla_w_out",
           "fox_w_in", "fox_b_f", "fox_q_norm", "fox_k_norm", "fox_w_out", "ffn_w_up", "ffn_conv_w", "ffn_conv_b",
           "ffn_w_down", "norm_final")


def _pack(parts):
    flat = jnp.concatenate([p.reshape(-1) for p in parts])
    pad = (-flat.shape[0]) % 1024
    return jnp.pad(flat, (0, pad)).reshape(1, -1)


def _unpack(flat, shapes):
    out, off = [], 0
    for shp in shapes:
        n = 1
        for s in shp:
            n *= s
        out.append(flat[0, off:off + n].reshape(shp))
        off += n
    return out


def kernel(x, c, w_mod, b_mod, norm_mix, norm_ffn, gla_w_in, gla_w_gate, gla_b_gate, gla_norm, gla_w_out, fox_w_in, fox_b_f, fox_q_norm, fox_k_norm, fox_w_out, ffn_w_up, ffn_conv_w, ffn_conv_b, ffn_w_down, norm_final, loss_target, m_w_mod, m_b_mod, m_norm_mix, m_norm_ffn, m_gla_w_in, m_gla_w_gate, m_gla_b_gate, m_gla_norm, m_gla_w_out, m_fox_w_in, m_fox_b_f, m_fox_q_norm, m_fox_k_norm, m_fox_w_out, m_ffn_w_up, m_ffn_conv_w, m_ffn_conv_b, m_ffn_w_down, m_norm_final, v_w_mod, v_b_mod, v_norm_mix, v_norm_ffn, v_gla_w_in, v_gla_w_gate, v_gla_b_gate, v_gla_norm, v_gla_w_out, v_fox_w_in, v_fox_b_f, v_fox_q_norm, v_fox_k_norm, v_fox_w_out, v_ffn_w_up, v_ffn_conv_w, v_ffn_conv_b, v_ffn_w_down, v_norm_final):
    w = dict(w_mod=w_mod, b_mod=b_mod, norm_mix=norm_mix, norm_ffn=norm_ffn, gla_w_in=gla_w_in, gla_w_gate=gla_w_gate,
             gla_b_gate=gla_b_gate, gla_norm=gla_norm, gla_w_out=gla_w_out, fox_w_in=fox_w_in, fox_b_f=fox_b_f,
             fox_q_norm=fox_q_norm, fox_k_norm=fox_k_norm, fox_w_out=fox_w_out, ffn_w_up=ffn_w_up,
             ffn_conv_w=ffn_conv_w, ffn_conv_b=ffn_conv_b, ffn_w_down=ffn_w_down, norm_final=norm_final)
    mom_m = dict(w_mod=m_w_mod, b_mod=m_b_mod, norm_mix=m_norm_mix, norm_ffn=m_norm_ffn, gla_w_in=m_gla_w_in,
                 gla_w_gate=m_gla_w_gate, gla_b_gate=m_gla_b_gate, gla_norm=m_gla_norm, gla_w_out=m_gla_w_out,
                 fox_w_in=m_fox_w_in, fox_b_f=m_fox_b_f, fox_q_norm=m_fox_q_norm, fox_k_norm=m_fox_k_norm,
                 fox_w_out=m_fox_w_out, ffn_w_up=m_ffn_w_up, ffn_conv_w=m_ffn_conv_w, ffn_conv_b=m_ffn_conv_b,
                 ffn_w_down=m_ffn_w_down, norm_final=m_norm_final)
    mom_v = dict(w_mod=v_w_mod, b_mod=v_b_mod, norm_mix=v_norm_mix, norm_ffn=v_norm_ffn, gla_w_in=v_gla_w_in,
                 gla_w_gate=v_gla_w_gate, gla_b_gate=v_gla_b_gate, gla_norm=v_gla_norm, gla_w_out=v_gla_w_out,
                 fox_w_in=v_fox_w_in, fox_b_f=v_fox_b_f, fox_q_norm=v_fox_q_norm, fox_k_norm=v_fox_k_norm,
                 fox_w_out=v_fox_w_out, ffn_w_up=v_ffn_w_up, ffn_conv_w=v_ffn_conv_w, ffn_conv_b=v_ffn_conv_b,
                 ffn_w_down=v_ffn_w_down, norm_final=v_norm_final)

    me = 4 * lax.axis_index("x") + 2 * lax.axis_index("y") + lax.axis_index("c")
    xs, target = x[0], loss_target[0]
    s, d = xs.shape
    depth = w_mod.shape[0]
    mod_cols = w_mod.shape[2]
    rank = gla_w_gate.shape[1]
    hd = fox_q_norm.shape[1]
    fox_heads = d // hd
    dk_total = gla_w_gate.shape[2] * N_DEV

    cond = c * (1.0 / (1.0 + jnp.exp(-c)))
    g = _exchange([gla_w_in[0].astype(BF16), gla_w_out[0].astype(BF16), fox_w_in[0].astype(BF16),
                   fox_w_out[0].astype(BF16), ffn_w_up.astype(BF16), ffn_w_down.astype(BF16),
                   gla_w_gate[0], ffn_conv_w, cond], "gather_weights", scatter=False)
    gla_in_full, fox_in_full = _cols_full(g[0]), _cols_full(g[2])
    main_gla, main_fox = gla_in_full.shape[1] - rank, fox_in_full.shape[1] - fox_heads
    cond_all = g[8][:, 0, :]
    p = dict(
        gla_w_main=gla_in_full[:, :main_gla], gla_w_tail=_pad_cols(gla_in_full[:, main_gla:]),
        gla_w_out=g[1].reshape(-1, d),
        fox_w_main=fox_in_full[:, :main_fox], fox_w_tail=_pad_cols(fox_in_full[:, main_fox:]),
        fox_w_out=g[3].reshape(-1, d),
        w_up=[jnp.transpose(g[4][:, i], (1, 0, 2)).reshape(d, -1) for i in range(depth)],
        w_down=[g[5][:, i].reshape(-1, d) for i in range(depth)],
        gla_wg_p=jnp.pad(_cols_full(g[6]), ((0, LANE - rank), (0, 0))),
        conv_w=[jnp.transpose(g[7][:, i], (1, 0, 2)).reshape(ffn_conv_w.shape[1], -1) for i in range(depth)],
        conv_b=[ffn_conv_b[i:i + 1] for i in range(depth)],
        gla_b_gate=gla_b_gate, gla_norm=gla_norm, fox_q_norm=fox_q_norm, fox_k_norm=fox_k_norm,
        fox_bf_p=_pad_cols(fox_b_f), gla_rank=rank,
        norm_mix=[norm_mix[i:i + 1] for i in range(depth)], norm_ffn=[norm_ffn[i:i + 1] for i in range(depth)],
    )

    cond_pad = jnp.pad(cond_all, ((0, 16 - N_DEV), (0, 0)))
    mod_part = []
    for i in range(depth):
        b_cols = lax.dynamic_slice(b_mod[i:i + 1], (0, me * mod_cols), (1, mod_cols))
        mod_part.append(_matmul(cond_pad, w_mod[i], name=f"mod_{i}", tn=768,
                                epilogue=lambda acc, b: (acc + b,), extras=(("n", b_cols),))[:N_DEV])
    mod_all = _exchange([jnp.stack(mod_part)], "gather_mod", scatter=False)[0]
    mod = lax.dynamic_index_in_dim(mod_all, me, axis=2, keepdims=False)
    mod = jnp.transpose(mod, (1, 0, 2)).reshape(depth, 6, 1, d)
    for j, nm in enumerate(("sh_m", "sc_m", "g_m", "sh_f", "sc_f", "g_f")):
        p[nm] = [mod[i, j] for i in range(depth)]

    acts, saved = [xs], []
    for i in range(depth):
        layer_fwd = _gla_layer_fwd if i % 2 == 0 else _fox_layer_fwd
        x1, sv_mix = layer_fwd(acts[-1], p, i)
        x2, sv_ffn = _ffn_fwd(x1, p, i, str(i))
        saved.append((acts[-1], x1, sv_mix, sv_ffn))
        acts.append(x2)
    dx, d_norm_final, loss_part = _final_loss(acts[-1], target, norm_final.reshape(1, d), "final_loss")

    lg = [None] * depth
    for i in reversed(range(depth)):
        x_in, x1, sv_mix, sv_ffn = saved[i]
        dx, g_ffn = _ffn_bwd(dx, x1, sv_ffn, p, i, str(i))
        layer_bwd = _gla_layer_bwd if i % 2 == 0 else _fox_layer_bwd
        dx, g_mix = layer_bwd(dx, x_in, sv_mix, p, i)
        lg[i] = {**g_ffn, **g_mix}
    grad_x = dx[None]

    gla_l = [i for i in range(depth) if i % 2 == 0]
    fox_l = [i for i in range(depth) if i % 2 == 1]
    small_parts = dict(
        norm_mix=jnp.concatenate([lg[i]["norm_mix"] for i in range(depth)]),
        norm_ffn=jnp.concatenate([lg[i]["norm_ffn"] for i in range(depth)]),
        gla_b_gate=jnp.concatenate([lg[i]["gla_b_gate"] for i in gla_l]),
        gla_norm=jnp.concatenate([lg[i]["gla_norm"] for i in gla_l]),
        fox_b_f=jnp.concatenate([lg[i]["fox_b_f"] for i in fox_l]),
        fox_q_norm=jnp.concatenate([lg[i]["fox_q_norm"] for i in fox_l]),
        fox_k_norm=jnp.concatenate([lg[i]["fox_k_norm"] for i in fox_l]),
        ffn_conv_b=jnp.concatenate([lg[i]["conv_b"] for i in range(depth)]),
        norm_final=d_norm_final,
        gla_w_gate=jnp.stack([lg[i]["gla_w_gate"] for i in gla_l]),
        ffn_conv_w=jnp.stack([lg[i]["conv_w"] for i in range(depth)]),
        loss=loss_part[:, :1],
    )
    order = ("norm_mix", "norm_ffn", "gla_b_gate", "gla_norm", "fox_b_f", "fox_q_norm", "fox_k_norm", "ffn_conv_b",
             "norm_final", "gla_w_gate", "ffn_conv_w", "loss")
    packed = _pack([small_parts[nm] for nm in order])
    dmod = jnp.stack([jnp.concatenate([lg[i][nm] for nm in ("sh_m", "sc_m", "g_m", "sh_f", "sc_f", "g_f")], axis=1)
                      for i in range(depth)])
    packed_all, dmod_all = _exchange([packed, dmod], "gather_small_grads", scatter=False)
    summed = _unpack(_sum8(packed_all, "sum_small_grads"), [small_parts[nm].shape for nm in order])
    small_g = dict(zip(order, summed))
    loss = small_g["loss"][0, 0]
    dmod_all = dmod_all[:, :, 0, :]

    grads = {}
    cond_t = _pad_cols(jnp.transpose(cond_all)).astype(BF16)
    dmod_cols = lax.dynamic_slice(dmod_all, (0, 0, me * mod_cols), (N_DEV, depth, mod_cols))
    g_w_mod = []
    for i in range(depth):
        rhs = jnp.pad(dmod_cols[:, i], ((0, LANE - N_DEV), (0, 0)))
        g_w_mod.append(_matmul(cond_t, rhs, name=f"mod_dw_{i}", tn=768))
    grads["w_mod"] = jnp.stack(g_w_mod)
    small_g["b_mod"] = _sum8(dmod_all.reshape(N_DEV, 1, -1), "sum_b_mod").reshape(depth, -1)

    big_pieces = [
        _cols_pieces(lg[0]["gla_w_in"]),
        lg[0]["gla_w_out"].reshape(N_DEV, -1, d),
        _cols_pieces(lg[1]["fox_w_in"]),
        lg[1]["fox_w_out"].reshape(N_DEV, -1, d),
        jnp.stack([_cols_pieces(lg[i]["w_up"]) for i in range(depth)], axis=1),
        jnp.stack([lg[i]["w_down"].reshape(N_DEV, -1, d) for i in range(depth)], axis=1),
    ]
    received = _exchange(big_pieces, "scatter_big_grads", scatter=True)

    out_g, out_d, out_m, out_v = {}, {}, {}, {}

    def update(nm, g_arr, pieces=False):
        shp = w[nm].shape
        cols = shp[-1]
        g2 = g_arr.reshape((N_DEV, -1, cols) if pieces else (-1, cols))
        res = _adamw(w[nm].reshape(-1, cols), g2, mom_m[nm].reshape(-1, cols), mom_v[nm].reshape(-1, cols),
                     f"adamw_{nm}", pieces=pieces)
        out_g[nm], out_d[nm], out_m[nm], out_v[nm] = (r.reshape(shp) for r in res)

    for nm, rec in zip(BIG, received):
        update(nm, rec, pieces=True)
    update("w_mod", grads["w_mod"])

    gate_cols = gla_w_gate.shape[2]
    conv_cols = ffn_conv_w.shape[2]
    local_small = dict(small_g)
    local_small["gla_w_gate"] = lax.dynamic_slice_in_dim(small_g["gla_w_gate"], me * gate_cols, gate_cols, axis=2)
    local_small["ffn_conv_w"] = lax.dynamic_slice_in_dim(small_g["ffn_conv_w"], me * conv_cols, conv_cols, axis=2)
    names = SMALL + SMALL_SHARDED
    shapes = [w[nm].shape for nm in names]
    res = _adamw(_pack([w[nm] for nm in names]), _pack([local_small[nm].reshape(w[nm].shape) for nm in names]),
                 _pack([mom_m[nm] for nm in names]), _pack([mom_v[nm] for nm in names]), "adamw_small")
    for tgt, flat in zip((out_g, out_d, out_m, out_v), res):
        for nm, arr in zip(names, _unpack(flat, shapes)):
            tgt[nm] = arr

    return (loss, grad_x, *[out_g[n] for n in WEIGHTS], *[out_d[n] for n in WEIGHTS],
            *[out_m[n] for n in WEIGHTS], *[out_v[n] for n in WEIGHTS])
```

```python
import jax
import jax.numpy as jnp
from jax import lax
from jax.experimental import pallas as pl
from jax.experimental.pallas import tpu as pltpu

F32, BF16 = jnp.float32, jnp.bfloat16
N_DEV = 8
GLA_HEADS = 4
GLA_TAU = 16.0
GLA_CHUNK = 64
NORM_EPS = 1e-6
ADAM_LR, ADAM_B1, ADAM_B2, ADAM_EPS, ADAM_WD, ADAM_STEP = 0.001, 0.9, 0.999, 1e-08, 0.01, 10
LANE = 128
VMEM_LIMIT = 56 * 1024 * 1024
NEG = -1e30


def _pcall(body, **kw):
    return pl.pallas_call(body, **kw)


def _params(n_axes):
    return pltpu.CompilerParams(dimension_semantics=("arbitrary",) * n_axes, vmem_limit_bytes=VMEM_LIMIT)


def _tile(dim, pref):
    if dim <= pref:
        return dim
    t = pref
    while dim % t:
        t -= LANE
    assert t > 0, (dim, pref)
    return t


def _dot(a, b, ta=False, tb=False):
    dims = (((0,) if ta else (1,), (1,) if tb else (0,)), ((), ()))
    return lax.dot_general(a.astype(BF16), b.astype(BF16), dims, preferred_element_type=F32)


def _split3(x):
    hi = x.astype(BF16)
    r1 = x - hi.astype(F32)
    mid = r1.astype(BF16)
    lo = (r1 - mid.astype(F32)).astype(BF16)
    return hi, mid, lo


def _tri_matmul(tri, x):
    hi, mid, lo = _split3(x)
    return _dot(tri, hi) + _dot(tri, mid) + _dot(tri, lo)


def _tri(n, upper=False):
    r = lax.broadcasted_iota(jnp.int32, (n, n), 0)
    c = lax.broadcasted_iota(jnp.int32, (n, n), 1)
    return jnp.where((r <= c) if upper else (r >= c), 1.0, 0.0).astype(BF16)


def _log_sigmoid(x):
    return jnp.minimum(x, 0.0) - jnp.log(1.0 + jnp.exp(-jnp.abs(x)))


def _sigmoid(x):
    return 1.0 / (1.0 + jnp.exp(-x))


def _silu(x):
    return x * _sigmoid(x)


def _dsilu(x):
    s = _sigmoid(x)
    return s * (1.0 + x * (1.0 - s))


def _matmul(a, b, *, name, ta=False, tb=False, out_dtypes=(F32,), tm=1024, tn=1024, tk=2048,
            epilogue=None, extras=()):
    m, k = (a.shape[1], a.shape[0]) if ta else a.shape
    n = b.shape[0] if tb else b.shape[1]
    assert (b.shape[1] if tb else b.shape[0]) == k, (a.shape, b.shape, ta, tb)
    tm, tn, tk = _tile(m, tm), _tile(n, tn), _tile(k, tk)
    nk = k // tk
    a_spec = pl.BlockSpec((tk, tm), lambda i, j, kk: (kk, i)) if ta else pl.BlockSpec((tm, tk), lambda i, j, kk: (i, kk))
    b_spec = pl.BlockSpec((tn, tk), lambda i, j, kk: (j, kk)) if tb else pl.BlockSpec((tk, tn), lambda i, j, kk: (kk, j))
    ex_specs = []
    for kind, arr in extras:
        if kind == "mn":
            assert arr.shape == (m, n), (arr.shape, m, n)
            ex_specs.append(pl.BlockSpec((tm, tn), lambda i, j, kk: (i, j)))
        else:
            assert arr.shape == (1, n), (arr.shape, n)
            ex_specs.append(pl.BlockSpec((1, tn), lambda i, j, kk: (0, j)))
    n_ex, n_out = len(extras), len(out_dtypes)

    def body(a_ref, b_ref, *rest):
        ex, outs, acc = rest[:n_ex], rest[n_ex:n_ex + n_out], rest[-1]
        kk = pl.program_id(2)

        @pl.when(kk == 0)
        def _():
            acc[...] = jnp.zeros_like(acc)

        acc[...] += _dot(a_ref[...], b_ref[...], ta, tb)

        @pl.when(kk == nk - 1)
        def _():
            if epilogue is None:
                vals = (acc[...],)
            else:
                vals = epilogue(acc[...], *[e[...] for e in ex])
            for o, v in zip(outs, vals):
                o[...] = v.astype(o.dtype)

    res = _pcall(
        body, name=name, grid=(m // tm, n // tn, nk),
        in_specs=[a_spec, b_spec] + ex_specs,
        out_specs=[pl.BlockSpec((tm, tn), lambda i, j, kk: (i, j))] * n_out,
        out_shape=[jax.ShapeDtypeStruct((m, n), d) for d in out_dtypes],
        scratch_shapes=[pltpu.VMEM((tm, tn), F32)],
        compiler_params=_params(3),
    )(a, b, *[arr for _, arr in extras])
    return res[0] if n_out == 1 else res


def _rowwise(fn, ins, outs, *, name, tr=128):
    rows = next(arr.shape[0] for kind, arr in ins if kind == "row")
    tr = _tile(rows, tr)
    in_specs = []
    for kind, arr in ins:
        if kind == "row":
            assert arr.shape[0] == rows and arr.ndim == 2
            in_specs.append(pl.BlockSpec((tr, arr.shape[1]), lambda i: (i, 0)))
        else:
            in_specs.append(pl.BlockSpec(arr.shape, lambda i, nd=arr.ndim: (0,) * nd))
    out_specs, out_shape = [], []
    for kind, w, dt in outs:
        if kind == "row":
            out_specs.append(pl.BlockSpec((tr, w), lambda i: (i, 0)))
            out_shape.append(jax.ShapeDtypeStruct((rows, w), dt))
        else:
            out_specs.append(pl.BlockSpec((1, w), lambda i: (0, 0)))
            out_shape.append(jax.ShapeDtypeStruct((1, w), dt))
    n_in = len(ins)

    def body(*refs):
        i = pl.program_id(0)
        vals = fn(*[r[...] for r in refs[:n_in]])
        for (kind, _, _), o, v in zip(outs, refs[n_in:], vals):
            if kind == "row":
                o[...] = v.astype(o.dtype)
            else:
                @pl.when(i == 0)
                def _(o=o):
                    o[...] = jnp.zeros_like(o)

                o[...] += v.astype(o.dtype)

    return _pcall(body, name=name, grid=(rows // tr,), in_specs=in_specs, out_specs=out_specs,
                  out_shape=out_shape, compiler_params=_params(1))(*[arr for _, arr in ins])


def _colsum(x):
    return jnp.sum(x, axis=0, keepdims=True)


def _norm_stats(x):
    rstd = lax.rsqrt(jnp.mean(x * x, axis=-1, keepdims=True) + NORM_EPS)
    return x * rstd, rstd


def _norm_bwd(dxhat, xhat, rstd):
    return rstd * (dxhat - xhat * jnp.mean(dxhat * xhat, axis=-1, keepdims=True))


def _adaln_fwd(x, gain, sc, sh, name):
    def fn(x, gain, sc, sh):
        xhat, _ = _norm_stats(x)
        return ((xhat * gain) * (1.0 + sc) + sh,)

    return _rowwise(fn, [("row", x), ("full", gain), ("full", sc), ("full", sh)],
                    [("row", x.shape[1], BF16)], name=name)[0]


def _adaln_bwd(x, dh, dres, gain, sc, name):
    d = x.shape[1]

    def fn(x, dh, dres, gain, sc):
        xhat, rstd = _norm_stats(x)
        dxhat = dh * (gain * (1.0 + sc))
        dx = dres + _norm_bwd(dxhat, xhat, rstd)
        return dx, _colsum(dh), _colsum(dh * (xhat * gain)), _colsum(dh * xhat * (1.0 + sc))

    return _rowwise(fn, [("row", x), ("row", dh), ("row", dres), ("full", gain), ("full", sc)],
                    [("row", d, F32), ("acc", d, F32), ("acc", d, F32), ("acc", d, F32)], name=name)


def _residual_bwd(dx, y, g, name):
    d = dx.shape[1]

    def fn(dx, y, g):
        return dx * (1.0 + g), _colsum(dx * y)

    return _rowwise(fn, [("row", dx), ("row", y), ("full", g)], [("row", d, BF16), ("acc", d, F32)], name=name)


def _final_loss(x, target, gain, name):
    d = x.shape[1]

    def fn(x, t, gain):
        xhat, rstd = _norm_stats(x)
        err = xhat * gain - t
        dy = err * (1.0 / d)
        loss = 0.5 * jnp.sum(jnp.mean(err * err, axis=-1, keepdims=True), axis=0, keepdims=True)
        dx = _norm_bwd(dy * gain, xhat, rstd)
        return dx, _colsum(dy * xhat), jnp.broadcast_to(loss, (1, LANE))

    return _rowwise(fn, [("row", x), ("row", target), ("full", gain)],
                    [("row", d, F32), ("acc", d, F32), ("acc", LANE, F32)], name=name)


def _gla_gates(q_ref, k_ref, a_ref, wg_ref, bg_ref, scale, c):
    ga = _dot(a_ref[...], wg_ref[...]) + bg_ref[...]
    la = _log_sigmoid(ga) * (1.0 / GLA_TAU)
    b = _tri_matmul(_tri(c), la)
    bl = _colsum(la)
    eb, enb, eend = jnp.exp(b), jnp.exp(-b), jnp.exp(bl - b)
    q = q_ref[...] * scale
    k = k_ref[...]
    return dict(ga=ga, eb=eb, enb=enb, eend=eend, dec=jnp.exp(bl), q_dec=q * eb, k_inv=k * enb, k_end=k * eend)


def _causal(c):
    return lax.broadcasted_iota(jnp.int32, (c, c), 0) >= lax.broadcasted_iota(jnp.int32, (c, c), 1)


def _gla_specs(heads, c, dk, dv, rev, n_chunks):
    def ch(n):
        return (n_chunks - 1 - n) if rev else n

    return [
        pl.BlockSpec((c, dk), lambda h, n: (ch(n), h)),
        pl.BlockSpec((c, dk), lambda h, n: (ch(n), heads + h)),
        pl.BlockSpec((c, dv), lambda h, n: (ch(n), heads + h)),
        pl.BlockSpec((c, LANE), lambda h, n: (ch(n), 0)),
        pl.BlockSpec((LANE, dk), lambda h, n: (0, h)),
        pl.BlockSpec((1, dk), lambda h, n: (0, h)),
    ]


def _gla_fwd(proj, a_tail, wg_p, bg, name):
    s = proj.shape[0]
    heads, c = GLA_HEADS, GLA_CHUNK
    dk = wg_p.shape[1] // heads
    dv = 2 * dk
    n_chunks = s // c
    scale = dk ** -0.5

    def body(q_ref, k_ref, v_ref, a_ref, wg_ref, bg_ref, o_ref, st_ref, state):
        @pl.when(pl.program_id(1) == 0)
        def _():
            state[...] = jnp.zeros_like(state)

        g = _gla_gates(q_ref, k_ref, a_ref, wg_ref, bg_ref, scale, c)
        v = v_ref[...]
        st = state[...]
        attn = jnp.where(_causal(c), _dot(g["q_dec"], g["k_inv"], tb=True), 0.0)
        o_ref[...] = _dot(attn, v) + _dot(g["q_dec"], st, tb=True)
        st_ref[...] = st.astype(st_ref.dtype)
        state[...] = g["dec"] * st + _dot(v, g["k_end"], ta=True)

    return _pcall(
        body, name=name, grid=(heads, n_chunks),
        in_specs=_gla_specs(heads, c, dk, dv, False, n_chunks),
        out_specs=[pl.BlockSpec((c, dv), lambda h, n: (n, h)),
                   pl.BlockSpec((None, None, dv, dk), lambda h, n: (h, n, 0, 0))],
        out_shape=[jax.ShapeDtypeStruct((s, heads * dv), F32),
                   jax.ShapeDtypeStruct((heads, n_chunks, dv, dk), BF16)],
        scratch_shapes=[pltpu.VMEM((dv, dk), F32)],
        compiler_params=_params(2),
    )(proj, proj, proj, a_tail, wg_p, bg)


def _gla_bwd(proj, a_tail, wg_p, bg, states, d_o, name):
    s = proj.shape[0]
    heads, c = GLA_HEADS, GLA_CHUNK
    dk = wg_p.shape[1] // heads
    dv = 2 * dk
    n_chunks = s // c
    scale = dk ** -0.5

    def body(q_ref, k_ref, v_ref, a_ref, wg_ref, bg_ref, st_ref, do_ref, dq_ref, dk_ref, dv_ref, dga_ref, dstate):
        @pl.when(pl.program_id(1) == 0)
        def _():
            dstate[...] = jnp.zeros_like(dstate)

        g = _gla_gates(q_ref, k_ref, a_ref, wg_ref, bg_ref, scale, c)
        v, st, dst, d_out = v_ref[...], st_ref[...], dstate[...], do_ref[...]
        q_dec, k_inv, k_end = g["q_dec"], g["k_inv"], g["k_end"]
        mask = _causal(c)
        attn = jnp.where(mask, _dot(q_dec, k_inv, tb=True), 0.0)
        d_attn = jnp.where(mask, _dot(d_out, v, tb=True), 0.0)
        d_qdec = _dot(d_attn, k_inv) + _dot(d_out, st)
        d_kinv = _dot(d_attn, q_dec, ta=True)
        d_kend = _dot(v, dst)
        dv_ref[...] = (_dot(attn, d_out, ta=True) + _dot(k_end, dst, tb=True)).astype(dv_ref.dtype)
        d_dec = jnp.sum(dst * st.astype(F32), axis=0, keepdims=True)
        dstate[...] = g["dec"] * dst + _dot(d_out, q_dec, ta=True)

        dq_ref[...] = (d_qdec * (scale * g["eb"])).astype(dq_ref.dtype)
        dk_ref[...] = (d_kinv * g["enb"] + d_kend * g["eend"]).astype(dk_ref.dtype)
        kk = d_kend * k_end
        db = d_qdec * q_dec - d_kinv * k_inv - kk
        dbl = jnp.sum(kk, axis=0, keepdims=True) + d_dec * g["dec"]
        last = lax.broadcasted_iota(jnp.int32, db.shape, 0) == c - 1
        db = db + jnp.where(last, dbl, 0.0)
        dla = _tri_matmul(_tri(c, upper=True), db)
        dga_ref[...] = dla * (1.0 / GLA_TAU) * _sigmoid(-g["ga"])

    rev = lambda h, n: (n_chunks - 1 - n, h)
    return _pcall(
        body, name=name, grid=(heads, n_chunks),
        in_specs=_gla_specs(heads, c, dk, dv, True, n_chunks) + [
            pl.BlockSpec((None, None, dv, dk), lambda h, n: (h, n_chunks - 1 - n, 0, 0)),
            pl.BlockSpec((c, dv), rev)],
        out_specs=[pl.BlockSpec((c, dk), rev), pl.BlockSpec((c, dk), rev), pl.BlockSpec((c, dv), rev),
                   pl.BlockSpec((c, dk), rev)],
        out_shape=[jax.ShapeDtypeStruct((s, heads * dk), BF16), jax.ShapeDtypeStruct((s, heads * dk), BF16),
                   jax.ShapeDtypeStruct((s, heads * dv), BF16), jax.ShapeDtypeStruct((s, heads * dk), F32)],
        scratch_shapes=[pltpu.VMEM((dv, dk), F32)],
        compiler_params=_params(2),
    )(proj, proj, proj, a_tail, wg_p, bg, states, d_o)


def _gla_post_fwd(o, r, gn, name):
    dvt = o.shape[1]
    dv = dvt // GLA_HEADS

    def fn(o, r, gn):
        outs = []
        for h in range(GLA_HEADS):
            sl = slice(h * dv, (h + 1) * dv)
            ohat, _ = _norm_stats(o[:, sl])
            outs.append((ohat * gn[:, sl]) * _silu(r[:, sl]))
        return (jnp.concatenate(outs, axis=1),)

    return _rowwise(fn, [("row", o), ("row", r), ("full", gn)], [("row", dvt, BF16)], name=name)[0]


def _gla_post_bwd(o, r, gn, dog, name):
    dvt = o.shape[1]
    dv = dvt // GLA_HEADS

    def fn(o, r, gn, dog):
        d_o, d_r, d_g = [], [], []
        for h in range(GLA_HEADS):
            sl = slice(h * dv, (h + 1) * dv)
            ohat, rstd = _norm_stats(o[:, sl])
            g, rr, dd = gn[:, sl], r[:, sl], dog[:, sl]
            d_r.append(dd * (ohat * g) * _dsilu(rr))
            don = dd * _silu(rr)
            d_g.append(_colsum(don * ohat))
            d_o.append(_norm_bwd(don * g, ohat, rstd))
        return jnp.concatenate(d_o, axis=1), jnp.concatenate(d_r, axis=1), jnp.concatenate(d_g, axis=1)

    return _rowwise(fn, [("row", o), ("row", r), ("full", gn), ("row", dog)],
                    [("row", dvt, F32), ("row", dvt, BF16), ("acc", dvt, F32)], name=name)


def _fox_prep(q, k, v, qg, kg, hd, name):
    d = q.shape[1]
    heads = d // hd
    scale = hd ** -0.5

    def fn(q, k, v, qg, kg):
        qs, ks = [], []
        for h in range(heads):
            sl = slice(h * hd, (h + 1) * hd)
            qs.append(_norm_stats(q[:, sl])[0] * qg * scale)
            ks.append(_norm_stats(k[:, sl])[0] * kg)
        return jnp.concatenate(qs, axis=1), jnp.concatenate(ks, axis=1), v

    return _rowwise(fn, [("row", q), ("row", k), ("row", v), ("full", qg), ("full", kg)],
                    [("row", d, BF16)] * 3, name=name)


def _fox_prep_bwd(q, k, dqn, dkn, qg, kg, hd, name):
    d = q.shape[1]
    heads = d // hd
    scale = hd ** -0.5

    def fn(q, k, dqn, dkn, qg, kg):
        dq, dk, gq, gk = [], [], [], []
        for h in range(heads):
            sl = slice(h * hd, (h + 1) * hd)
            for x, dxn, g, s, dl, gl in ((q, dqn, qg, scale, dq, gq), (k, dkn, kg, 1.0, dk, gk)):
                xhat, rstd = _norm_stats(x[:, sl])
                dn = dxn[:, sl] * s
                gl.append(_colsum(dn * xhat))
                dl.append(_norm_bwd(dn * g, xhat, rstd))
        cat = lambda t: jnp.concatenate(t, axis=1)
        return cat(dq), cat(dk), cat(gq), cat(gk)

    return _rowwise(fn, [("row", q), ("row", k), ("row", dqn), ("row", dkn), ("full", qg), ("full", kg)],
                    [("row", d, BF16), ("row", d, BF16), ("acc", d, F32), ("acc", d, F32)], name=name)


def _fox_cum(fl, bf_p, name, tb=256):
    s = fl.shape[0]
    tb = _tile(s, tb)

    def body(fl_ref, bf_ref, cum_ref, carry):
        @pl.when(pl.program_id(0) == 0)
        def _():
            carry[...] = jnp.zeros_like(carry)

        lf = _log_sigmoid(fl_ref[...] + bf_ref[...])
        cum_ref[...] = _tri_matmul(_tri(tb), lf) + carry[...]
        carry[...] += _colsum(lf)

    return _pcall(
        body, name=name, grid=(s // tb,),
        in_specs=[pl.BlockSpec((tb, LANE), lambda i: (i, 0)), pl.BlockSpec((1, LANE), lambda i: (0, 0))],
        out_specs=pl.BlockSpec((tb, LANE), lambda i: (i, 0)),
        out_shape=jax.ShapeDtypeStruct((s, LANE), F32),
        scratch_shapes=[pltpu.VMEM((1, LANE), F32)],
        compiler_params=_params(1),
    )(fl, bf_p)


def _fox_cum_bwd(dcum, fl, bf_p, name, tb=256):
    s = fl.shape[0]
    tb = _tile(s, tb)
    nb = s // tb

    def body(dc_ref, fl_ref, bf_ref, dfl_ref, dbf_ref, carry):
        @pl.when(pl.program_id(0) == 0)
        def _():
            carry[...] = jnp.zeros_like(carry)
            dbf_ref[...] = jnp.zeros_like(dbf_ref)

        dc = dc_ref[...]
        dlf = _tri_matmul(_tri(tb, upper=True), dc) + carry[...]
        carry[...] += _colsum(dc)
        dfl = dlf * _sigmoid(-(fl_ref[...] + bf_ref[...]))
        dfl_ref[...] = dfl
        dbf_ref[...] += _colsum(dfl)

    rev = lambda i: (nb - 1 - i, 0)
    return _pcall(
        body, name=name, grid=(nb,),
        in_specs=[pl.BlockSpec((tb, LANE), rev), pl.BlockSpec((tb, LANE), rev), pl.BlockSpec((1, LANE), lambda i: (0, 0))],
        out_specs=[pl.BlockSpec((tb, LANE), rev), pl.BlockSpec((1, LANE), lambda i: (0, 0))],
        out_shape=[jax.ShapeDtypeStruct((s, LANE), F32), jax.ShapeDtypeStruct((1, LANE), F32)],
        scratch_shapes=[pltpu.VMEM((1, LANE), F32)],
        compiler_params=_params(1),
    )(dcum, fl, bf_p)


def _fox_attn_fwd(qn, kn, vb, cum_col, cum_row, hd, t, name):
    s, d = qn.shape
    heads = d // hd
    nq = s // t

    def body(q_ref, k_ref, v_ref, cc_ref, cr_ref, o_ref, lse_ref):
        qi = pl.program_id(1)
        q = q_ref[...]
        cq = cc_ref[...]
        qpos = qi * t + lax.broadcasted_iota(jnp.int32, (t, 1), 0)

        def step(kj, carry):
            m, l, acc = carry
            off = pl.multiple_of(kj * t, t)
            ks, vs = k_ref[pl.ds(off, t), :], v_ref[pl.ds(off, t), :]
            sc = _dot(q, ks, tb=True) + cq - cr_ref[kj]
            kpos = off + lax.broadcasted_iota(jnp.int32, (1, t), 1)
            sc = jnp.where(kpos <= qpos, sc, NEG)
            m_new = jnp.maximum(m, jnp.max(sc, axis=1, keepdims=True))
            alpha = jnp.exp(m - m_new)
            p = jnp.exp(sc - m_new)
            return m_new, alpha * l + jnp.sum(p, axis=1, keepdims=True), alpha * acc + _dot(p, vs)

        init = (jnp.full((t, 1), NEG, F32), jnp.zeros((t, 1), F32), jnp.zeros((t, hd), F32))
        m, l, acc = lax.fori_loop(0, qi + 1, step, init)
        o_ref[...] = acc / l
        lse_ref[...] = m + jnp.log(l)

    return _pcall(
        body, name=name, grid=(heads, nq),
        in_specs=[pl.BlockSpec((t, hd), lambda h, i: (i, h)),
                  pl.BlockSpec((s, hd), lambda h, i: (0, h)),
                  pl.BlockSpec((s, hd), lambda h, i: (0, h)),
                  pl.BlockSpec((None, t, 1), lambda h, i: (h, i, 0)),
                  pl.BlockSpec((None, nq, 1, t), lambda h, i: (h, 0, 0, 0))],
        out_specs=[pl.BlockSpec((t, hd), lambda h, i: (i, h)), pl.BlockSpec((None, t, 1), lambda h, i: (h, i, 0))],
        out_shape=[jax.ShapeDtypeStruct((s, d), F32), jax.ShapeDtypeStruct((heads, s, 1), F32)],
        compiler_params=_params(2),
    )(qn, kn, vb, cum_col, cum_row)


def _fox_attn_bwd(qn, kn, vb, d_o, o, lse, cum_col, cum_row, hd, t, name):
    s, d = qn.shape
    heads = d // hd
    nq = s // t

    def body(q_ref, k_ref, v_ref, do_ref, o_ref, lse_ref, cc_ref, cr_ref,
             dq_ref, dk_ref, dv_ref, dcq_ref, dck_ref, delta):
        kj = pl.program_id(1)

        @pl.when(kj == 0)
        def _():
            dq_ref[...] = jnp.zeros_like(dq_ref)
            dcq_ref[...] = jnp.zeros_like(dcq_ref)
            delta[...] = jnp.sum(do_ref[...] * o_ref[...], axis=1, keepdims=True)

        ks, vs, cr = k_ref[...], v_ref[...], cr_ref[...]
        kpos = kj * t + lax.broadcasted_iota(jnp.int32, (1, t), 1)

        def step(qi, carry):
            dk, dv, dck = carry
            rows = pl.ds(pl.multiple_of(qi * t, t), t)
            q, d_out = q_ref[rows, :], do_ref[rows, :]
            sc = _dot(q, ks, tb=True) + cc_ref[rows, :] - cr
            qpos = qi * t + lax.broadcasted_iota(jnp.int32, (t, 1), 0)
            p = jnp.where(kpos <= qpos, jnp.exp(sc - lse_ref[rows, :]), 0.0)
            ds = p * (_dot(d_out, vs, tb=True) - delta[rows, :])
            dq_ref[rows, :] += _dot(ds, ks)
            dcq_ref[rows, :] += jnp.sum(ds, axis=1, keepdims=True)
            return dk + _dot(ds, q, ta=True), dv + _dot(p, d_out, ta=True), dck + _colsum(ds)

        init = (jnp.zeros((t, hd), F32), jnp.zeros((t, hd), F32), jnp.zeros((1, t), F32))
        dk, dv, dck = lax.fori_loop(kj, nq, step, init)
        dk_ref[...] = dk.astype(dk_ref.dtype)
        dv_ref[...] = dv.astype(dv_ref.dtype)
        dck_ref[...] = dck

    head_rows = lambda h, j: (0, h)
    blk = lambda h, j: (j, h)
    return _pcall(
        body, name=name, grid=(heads, nq),
        in_specs=[pl.BlockSpec((s, hd), head_rows), pl.BlockSpec((t, hd), blk), pl.BlockSpec((t, hd), blk),
                  pl.BlockSpec((s, hd), head_rows), pl.BlockSpec((s, hd), head_rows),
                  pl.BlockSpec((None, s, 1), lambda h, j: (h, 0, 0)),
                  pl.BlockSpec((None, s, 1), lambda h, j: (h, 0, 0)),
                  pl.BlockSpec((None, None, 1, t), lambda h, j: (h, j, 0, 0))],
        out_specs=[pl.BlockSpec((s, hd), head_rows), pl.BlockSpec((t, hd), blk), pl.BlockSpec((t, hd), blk),
                   pl.BlockSpec((None, s, 1), lambda h, j: (h, 0, 0)),
                   pl.BlockSpec((None, None, 1, t), lambda h, j: (h, j, 0, 0))],
        out_shape=[jax.ShapeDtypeStruct((s, d), F32), jax.ShapeDtypeStruct((s, d), BF16),
                   jax.ShapeDtypeStruct((s, d), BF16), jax.ShapeDtypeStruct((heads, s, 1), F32),
                   jax.ShapeDtypeStruct((heads, nq, 1, t), F32)],
        scratch_shapes=[pltpu.VMEM((s, 1), F32)],
        compiler_params=_params(2),
    )(qn, kn, vb, d_o, o, lse, cum_col, cum_row)


def _fox_gate_fwd(o, og, name):
    def fn(o, og):
        return (o * _sigmoid(og),)

    return _rowwise(fn, [("row", o), ("row", og)], [("row", o.shape[1], BF16)], name=name)[0]


def _fox_gate_bwd(o, og, dact, name):
    def fn(o, og, dact):
        sg = _sigmoid(og)
        return dact * sg, dact * o * sg * (1.0 - sg)

    d = o.shape[1]
    return _rowwise(fn, [("row", o), ("row", og), ("row", dact)], [("row", d, F32), ("row", d, BF16)], name=name)


def _shift_down(x, n):
    rows = lax.broadcasted_iota(jnp.int32, x.shape, 0)
    return jnp.where(rows >= n, pltpu.roll(x, n, 0), 0.0)


def _shift_up(x, n):
    rows = lax.broadcasted_iota(jnp.int32, x.shape, 0)
    return jnp.where(rows < x.shape[0] - n, pltpu.roll(x, x.shape[0] - n, 0), 0.0)


def _conv(u, w_ref, b):
    return w_ref[0:1, :] * _shift_down(u, 2) + w_ref[1:2, :] * _shift_down(u, 1) + w_ref[2:3, :] * u + b


def _conv_act_fwd(u, cw, cb, name, tc=256):
    s, two_f = u.shape
    dff = two_f // 2
    tc = _tile(dff, tc)
    nb = dff // tc

    def body(ug_ref, uv_ref, wg_ref, wv_ref, bg_ref, bv_ref, a_ref):
        gate = _conv(ug_ref[...], wg_ref, bg_ref[...])
        val = _conv(uv_ref[...], wv_ref, bv_ref[...])
        a_ref[...] = (_silu(gate) * val).astype(a_ref.dtype)

    lo, hi = (lambda j: (0, j)), (lambda j: (0, j + nb))
    return _pcall(
        body, name=name, grid=(nb,),
        in_specs=[pl.BlockSpec((s, tc), lo), pl.BlockSpec((s, tc), hi), pl.BlockSpec((3, tc), lo),
                  pl.BlockSpec((3, tc), hi), pl.BlockSpec((1, tc), lo), pl.BlockSpec((1, tc), hi)],
        out_specs=pl.BlockSpec((s, tc), lo),
        out_shape=jax.ShapeDtypeStruct((s, dff), BF16),
        compiler_params=_params(1),
    )(u, u, cw, cw, cb, cb)


def _conv_act_bwd(u, cw, cb, da, name, tc=128):
    s, two_f = u.shape
    dff = two_f // 2
    tc = _tile(dff, tc)
    nb = dff // tc

    def body(u_ref, up_ref, w_ref, wp_ref, b_ref, bp_ref, da_ref, du_ref, dw_ref, db_ref):
        is_gate = pl.program_id(0) < nb
        u = u_ref[...]
        mine = _conv(u, w_ref, b_ref[...])
        other = _conv(up_ref[...], wp_ref, bp_ref[...])
        da = da_ref[...]
        dc = jnp.where(is_gate, da * other * _dsilu(mine), da * _silu(other))
        du = w_ref[0:1, :] * _shift_up(dc, 2) + w_ref[1:2, :] * _shift_up(dc, 1) + w_ref[2:3, :] * dc
        du_ref[...] = du.astype(du_ref.dtype)
        dw_ref[0:1, :] = _colsum(dc * _shift_down(u, 2))
        dw_ref[1:2, :] = _colsum(dc * _shift_down(u, 1))
        dw_ref[2:3, :] = _colsum(dc * u)
        db_ref[...] = _colsum(dc)

    own = lambda j: (0, j)
    partner = lambda j: (0, (j + nb) % (2 * nb))
    return _pcall(
        body, name=name, grid=(2 * nb,),
        in_specs=[pl.BlockSpec((s, tc), own), pl.BlockSpec((s, tc), partner), pl.BlockSpec((3, tc), own),
                  pl.BlockSpec((3, tc), partner), pl.BlockSpec((1, tc), own), pl.BlockSpec((1, tc), partner),
                  pl.BlockSpec((s, tc), lambda j: (0, j % nb))],
        out_specs=[pl.BlockSpec((s, tc), own), pl.BlockSpec((3, tc), own), pl.BlockSpec((1, tc), own)],
        out_shape=[jax.ShapeDtypeStruct((s, two_f), BF16), jax.ShapeDtypeStruct((3, two_f), F32),
                   jax.ShapeDtypeStruct((1, two_f), F32)],
        compiler_params=_params(1),
    )(u, u, cw, cw, cb, cb, da)


def _adamw_math(w, g, m, v):
    m = ADAM_B1 * m + (1.0 - ADAM_B1) * g
    v = ADAM_B2 * v + (1.0 - ADAM_B2) * (g * g)
    m_hat = m / (1.0 - ADAM_B1 ** ADAM_STEP)
    v_hat = v / (1.0 - ADAM_B2 ** ADAM_STEP)
    delta = -ADAM_LR * (m_hat / (jnp.sqrt(v_hat) + ADAM_EPS) + ADAM_WD * w)
    return delta, m, v


def _adamw(w, g, m, v, name, tr=128):
    layers, r, c = w.shape
    pieces = isinstance(g, (list, tuple))
    if r <= tr or r % 8:
        tr = r
    while r % tr:
        tr -= 8
    nr = r // tr
    g_list = list(g) if pieces else [g]

    def body(w_ref, *rest):
        g_refs, (m_ref, v_ref, go_ref, d_ref, mo_ref, vo_ref) = rest[:len(g_list)], rest[len(g_list):]

        def update(grad):
            delta, m_new, v_new = _adamw_math(w_ref[...], grad, m_ref[...], v_ref[...])
            go_ref[...], d_ref[...], mo_ref[...], vo_ref[...] = grad, delta, m_new, v_new

        if not pieces:
            update(g_refs[0][...])
            return
        for layer, g_ref in enumerate(g_refs):
            @pl.when(pl.program_id(0) == layer)
            def _(g_ref=g_ref):
                grad = g_ref[0].astype(F32)
                for i in range(1, N_DEV):
                    grad = grad + g_ref[i].astype(F32)
                update(grad)

    spec = pl.BlockSpec((None, tr, c), lambda l, i: (l, i, 0))
    if pieces:
        g_specs = [pl.BlockSpec((N_DEV, tr, c),
                                lambda l, i, k=k: (0, jnp.where(l == k, i, jnp.where(l < k, 0, nr - 1)), 0))
                   for k in range(layers)]
    else:
        g_specs = [spec]
    return _pcall(
        body, name=name, grid=(layers, nr), in_specs=[spec] + g_specs + [spec, spec], out_specs=[spec] * 4,
        out_shape=[jax.ShapeDtypeStruct((layers, r, c), F32)] * 4, compiler_params=_params(2),
    )(w, *g_list, m, v)


def _sum8(x, name):
    p = x.shape[2]
    tp = _tile(p, 16 * 1024)

    def body(x_ref, o_ref):
        acc = x_ref[0]
        for i in range(1, N_DEV):
            acc = acc + x_ref[i]
        o_ref[...] = acc

    return _pcall(
        body, name=name, grid=(p // tp,), in_specs=[pl.BlockSpec((N_DEV, 1, tp), lambda i: (0, 0, i))],
        out_specs=pl.BlockSpec((1, tp), lambda i: (0, i)), out_shape=jax.ShapeDtypeStruct((1, p), x.dtype),
        compiler_params=_params(1),
    )(x)


def _exchange(arrays, name, scatter):
    n = len(arrays)
    hbm = pl.BlockSpec(memory_space=pl.ANY)

    def body(*refs):
        ins, outs = refs[:n], refs[n:2 * n]
        send_sems, recv_sems, local_sems = refs[2 * n:]
        x, y, c = lax.axis_index("x"), lax.axis_index("y"), lax.axis_index("c")
        me = 4 * x + 2 * y + c
        copies = []
        for a in range(n):
            src_mine = ins[a].at[me] if scatter else ins[a]
            local = pltpu.make_async_copy(src_mine, outs[a].at[me], local_sems.at[a])
            local.start()
            copies.append(local)
            for k in range(1, N_DEV):
                px = 1 - x if k & 4 else x
                py = 1 - y if k & 2 else y
                pc = 1 - c if k & 1 else c
                src = ins[a].at[4 * px + 2 * py + pc] if scatter else ins[a]
                cp = pltpu.make_async_remote_copy(
                    src_ref=src, dst_ref=outs[a].at[me],
                    send_sem=send_sems.at[a * (N_DEV - 1) + k - 1], recv_sem=recv_sems.at[a * (N_DEV - 1) + k - 1],
                    device_id=(px, py, pc), device_id_type=pl.DeviceIdType.MESH)
                cp.start()
                copies.append(cp)
        for cp in copies:
            cp.wait()

    out_shape = [jax.ShapeDtypeStruct(a.shape if scatter else (N_DEV,) + a.shape, a.dtype) for a in arrays]
    return _pcall(
        body, name=name, in_specs=[hbm] * n, out_specs=[hbm] * n, out_shape=out_shape,
        scratch_shapes=[pltpu.SemaphoreType.DMA((n * (N_DEV - 1),)), pltpu.SemaphoreType.DMA((n * (N_DEV - 1),)),
                        pltpu.SemaphoreType.DMA((n,))],
        compiler_params=pltpu.CompilerParams(has_side_effects=True),
    )(*arrays)


_HBM = pl.BlockSpec(memory_space=pltpu.HBM)
_SEM = pl.BlockSpec(memory_space=pltpu.SEMAPHORE)
_DATAFLOW = pltpu.SideEffectType.DATAFLOW_SIDE_EFFECTING


def _peer(k, x, y, c):
    return (1 - x if k & 4 else x, 1 - y if k & 2 else y, 1 - c if k & 1 else c)


def _exchange_start(arrays, name, scatter):
    n = len(arrays)
    lands = [lax.empty(a.shape if scatter else (N_DEV,) + a.shape, a.dtype) for a in arrays]

    def body(*refs):
        srcs, dsts = refs[:n], refs[n:2 * n]
        send_sems, recv_sems, token = refs[4 * n:5 * n], refs[5 * n:6 * n], refs[6 * n]
        x, y, c = lax.axis_index("x"), lax.axis_index("y"), lax.axis_index("c")
        me = 4 * x + 2 * y + c
        for a in range(n):
            for k in range(1, N_DEV):
                px, py, pc = _peer(k, x, y, c)
                pltpu.make_async_remote_copy(
                    src_ref=srcs[a].at[4 * px + 2 * py + pc] if scatter else srcs[a], dst_ref=dsts[a].at[me],
                    send_sem=send_sems[a].at[k - 1], recv_sem=recv_sems[a].at[k - 1],
                    device_id=(px, py, pc), device_id_type=pl.DeviceIdType.MESH).start()
        token[...] = jnp.zeros_like(token)

    sems = [pltpu.SemaphoreType.DMA((N_DEV - 1,))] * (2 * n)
    res = _pcall(
        body, name=name,
        in_specs=[_HBM] * (2 * n),
        out_specs=[_HBM] * (2 * n) + [_SEM] * (2 * n) + [pl.BlockSpec(memory_space=pltpu.VMEM)],
        out_shape=[pltpu.HBM(a.shape, a.dtype) for a in arrays] + [pltpu.HBM(l.shape, l.dtype) for l in lands]
        + sems + [jax.ShapeDtypeStruct((8, LANE), F32)],
        input_output_aliases={i: i for i in range(2 * n)},
        compiler_params=pltpu.CompilerParams(has_side_effects=_DATAFLOW),
    )(*[pltpu.with_memory_space_constraint(a, pltpu.HBM) for a in arrays],
      *[pltpu.with_memory_space_constraint(l, pltpu.HBM) for l in lands])
    handles = [(res[a], res[n + a], res[2 * n + a], res[3 * n + a]) for a in range(n)]
    return handles, res[4 * n][0, 0]


def _exchange_wait(handles, after, name, scatter):
    n = len(handles)

    def body(*refs):
        srcs, dsts = refs[:n], refs[n:2 * n]
        send_sems, recv_sems = refs[2 * n:3 * n], refs[3 * n:4 * n]
        x, y, c = lax.axis_index("x"), lax.axis_index("y"), lax.axis_index("c")
        me = 4 * x + 2 * y + c
        for a in range(n):
            for k in range(1, N_DEV):
                cp = pltpu.make_async_remote_copy(
                    src_ref=srcs[a].at[me] if scatter else srcs[a], dst_ref=dsts[a].at[me],
                    send_sem=send_sems[a].at[k - 1], recv_sem=recv_sems[a].at[k - 1],
                    device_id=_peer(k, x, y, c), device_id_type=pl.DeviceIdType.MESH)
                cp.wait_send()
                cp.wait_recv()

    srcs, lands = [h[0] for h in handles], [h[1] for h in handles]
    res = _pcall(
        body, name=name,
        in_specs=[_HBM] * (2 * n) + [_SEM] * (2 * n) + [pl.BlockSpec(memory_space=pl.ANY)],
        out_specs=[_HBM] * (2 * n),
        out_shape=[pltpu.HBM(t.shape, t.dtype) for t in srcs + lands],
        input_output_aliases={i: i for i in range(2 * n)},
        compiler_params=pltpu.CompilerParams(has_side_effects=_DATAFLOW),
    )(*srcs, *lands, *[h[2] for h in handles], *[h[3] for h in handles], after)
    return res[n:]


def _with_own_block(land, mine, me):
    return lax.dynamic_update_slice(land, mine[None], (me,) + (0,) * mine.ndim)


def _pad_cols(x, width=LANE):
    return jnp.pad(x, ((0, 0), (0, width - x.shape[1])))


def _cols_full(g):
    return jnp.transpose(g, (1, 0, 2)).reshape(g.shape[1], -1)


def _cols_pieces(dw):
    k = dw.shape[0]
    return jnp.transpose(dw.reshape(k, N_DEV, -1), (1, 0, 2))


def _ffn_fwd(x1, p, i, tag):
    h2 = _adaln_fwd(x1, p["norm_ffn"][i], p["sc_f"][i], p["sh_f"][i], f"ffn_norm_{tag}")
    u = _matmul(h2, p["fetch"](f"up{i}", h2), name=f"ffn_up_{tag}")
    a = _conv_act_fwd(u, p["conv_w"][i], p["conv_b"][i], f"ffn_act_{tag}")
    g_f = p["g_f"][i]
    x2, f = _matmul(a, p["fetch"](f"down{i}", a), name=f"ffn_down_{tag}", tk=512, out_dtypes=(F32, F32),
                    epilogue=lambda acc, x1, g: (x1 + (1.0 + g) * acc, acc), extras=(("mn", x1), ("n", g_f)))
    return x2, dict(h2=h2, u=u, a=a, f=f)


def _ffn_bwd(dx2, x1, saved, p, i, tag):
    d = x1.shape[1]
    w_up, w_down = p["fetch"](f"up{i}", None), p["fetch"](f"down{i}", None)
    df, dg_f = _residual_bwd(dx2, saved["f"], p["g_f"][i], f"ffn_res_bwd_{tag}")
    da = _matmul(df, w_down, tb=True, name=f"ffn_down_dx_{tag}", tn=512)
    dw_down = _matmul(saved["a"], df, ta=True, name=f"ffn_down_dw_{tag}", tm=1408, out_dtypes=(BF16,))
    du, dcw, dcb = _conv_act_bwd(saved["u"], p["conv_w"][i], p["conv_b"][i], da, f"ffn_act_bwd_{tag}")
    dh2 = _matmul(du, w_up, tb=True, name=f"ffn_up_dx_{tag}", tk=1024)
    dw_up = _matmul(saved["h2"], du, ta=True, name=f"ffn_up_dw_{tag}", out_dtypes=(BF16,))
    tok = p["send"](f"ffn{i}", [_cols_pieces(dw_up), dw_down.reshape(N_DEV, -1, d)])
    dx1, dsh, dsc, dgain = _adaln_bwd(x1, dh2, dx2, p["norm_ffn"][i] + tok, p["sc_f"][i], f"ffn_norm_bwd_{tag}")
    grads = dict(conv_w=dcw, conv_b=dcb, norm_ffn=dgain, sh_f=dsh, sc_f=dsc, g_f=dg_f)
    return dx1, grads


def _gla_layer_fwd(x, p, i):
    h1 = _adaln_fwd(x, p["norm_mix"][i], p["sc_m"][i], p["sh_m"][i], "gla_norm")
    w_main, w_tail = p["fetch"]("gla_in", h1)
    proj = _matmul(h1, w_main, name="gla_in")
    a_tail = _matmul(h1, w_tail, name="gla_in_tail")
    dk_total = p["gla_wg_p"].shape[1]
    o, states = _gla_fwd(proj, a_tail, p["gla_wg_p"], p["gla_b_gate"], "gla_chunks")
    r = proj[:, 2 * dk_total + o.shape[1]:]
    og = _gla_post_fwd(o, r, p["gla_norm"], "gla_post")
    x1, y = _matmul(og, p["fetch"]("gla_out", og), name="gla_out", out_dtypes=(F32, F32),
                    epilogue=lambda acc, x, g: (x + (1.0 + g) * acc, acc), extras=(("mn", x), ("n", p["g_m"][i])))
    return x1, dict(h1=h1, proj=proj, a_tail=a_tail, o=o, r=r, states=states, og=og, y=y)


def _gla_layer_bwd(dx1, x, sv, p, i):
    d = x.shape[1]
    (w_main, w_tail), w_out = p["fetch"]("gla_in", None), p["fetch"]("gla_out", None)
    dy, dg_m = _residual_bwd(dx1, sv["y"], p["g_m"][i], "gla_res_bwd")
    dog = _matmul(dy, w_out, tb=True, name="gla_out_dx")
    dw_out = _matmul(sv["og"], dy, ta=True, name="gla_out_dw", out_dtypes=(BF16,))
    d_o, d_r, dgn = _gla_post_bwd(sv["o"], sv["r"], p["gla_norm"], dog, "gla_post_bwd")
    dq, dk, dv, dga = _gla_bwd(sv["proj"], sv["a_tail"], p["gla_wg_p"], p["gla_b_gate"], sv["states"], d_o,
                               "gla_chunks_bwd")
    da_tail = _matmul(dga, p["gla_wg_p"], tb=True, name="gla_gate_dx", out_dtypes=(BF16,))
    dwg = _matmul(sv["a_tail"], dga, ta=True, name="gla_gate_dw")
    dbg = _rowwise(lambda t: (_colsum(t),), [("row", dga)], [("acc", dga.shape[1], F32)], name="gla_gate_db")[0]
    dproj = jnp.concatenate([dq, dk, dv, d_r], axis=1)
    dh_tail = _matmul(da_tail, w_tail, tb=True, name="gla_in_tail_dx")
    dh1 = _matmul(dproj, w_main, tb=True, name="gla_in_dx", tk=1024,
                  epilogue=lambda acc, t: (acc + t,), extras=(("mn", dh_tail),))
    dw_main = _matmul(sv["h1"], dproj, ta=True, name="gla_in_dw", out_dtypes=(BF16,))
    dw_tail = _matmul(sv["h1"], da_tail, ta=True, name="gla_in_tail_dw", out_dtypes=(BF16,))
    rank = p["gla_rank"]
    dw_in = jnp.concatenate([dw_main, dw_tail[:, :rank]], axis=1)
    tok = p["send"]("gla", [_cols_pieces(dw_in), dw_out.reshape(N_DEV, -1, d)])
    dx, dsh, dsc, dgain = _adaln_bwd(x, dh1, dx1, p["norm_mix"][i] + tok, p["sc_m"][i], "gla_norm_bwd")
    grads = dict(gla_w_gate=dwg[:rank], gla_b_gate=dbg, gla_norm=dgn, norm_mix=dgain, sh_m=dsh, sc_m=dsc, g_m=dg_m)
    return dx, grads


def _fox_layer_fwd(x, p, i):
    d = x.shape[1]
    hd = p["fox_q_norm"].shape[1]
    heads = d // hd
    s = x.shape[0]
    t = _tile(s, 256)
    h1 = _adaln_fwd(x, p["norm_mix"][i], p["sc_m"][i], p["sh_m"][i], "fox_norm")
    w_main, w_tail = p["fetch"]("fox_in", h1)
    proj = _matmul(h1, w_main, name="fox_in")
    fl = _matmul(h1, w_tail, name="fox_in_tail")
    q, k, v, og = (proj[:, j * d:(j + 1) * d] for j in range(4))
    qn, kn, vb = _fox_prep(q, k, v, p["fox_q_norm"], p["fox_k_norm"], hd, "fox_prep")
    cum = _fox_cum(fl, p["fox_bf_p"], "fox_cum")
    cum_t = jnp.transpose(cum[:, :heads])
    cum_col, cum_row = cum_t[:, :, None], cum_t.reshape(heads, s // t, 1, t)
    o, lse = _fox_attn_fwd(qn, kn, vb, cum_col, cum_row, hd, t, "fox_attn")
    act = _fox_gate_fwd(o, og, "fox_gate")
    x1, y = _matmul(act, p["fetch"]("fox_out", act), name="fox_out", out_dtypes=(F32, F32),
                    epilogue=lambda acc, x, g: (x + (1.0 + g) * acc, acc), extras=(("mn", x), ("n", p["g_m"][i])))
    return x1, dict(h1=h1, q=q, k=k, og=og, fl=fl, qn=qn, kn=kn, vb=vb, cum_col=cum_col, cum_row=cum_row,
                    o=o, lse=lse, act=act, y=y, t=t, hd=hd)


def _fox_layer_bwd(dx1, x, sv, p, i):
    d = x.shape[1]
    hd, t = sv["hd"], sv["t"]
    heads = d // hd
    s = x.shape[0]
    (w_main, w_tail), w_out = p["fetch"]("fox_in", None), p["fetch"]("fox_out", None)
    dy, dg_m = _residual_bwd(dx1, sv["y"], p["g_m"][i], "fox_res_bwd")
    dact = _matmul(dy, w_out, tb=True, name="fox_out_dx")
    dw_out = _matmul(sv["act"], dy, ta=True, name="fox_out_dw", out_dtypes=(BF16,))
    d_o, d_og = _fox_gate_bwd(sv["o"], sv["og"], dact, "fox_gate_bwd")
    dqn, dkn, dvb, dcq, dck = _fox_attn_bwd(sv["qn"], sv["kn"], sv["vb"], d_o, sv["o"], sv["lse"], sv["cum_col"],
                                            sv["cum_row"], hd, t, "fox_attn_bwd")
    dq, dk, gq, gk = _fox_prep_bwd(sv["q"], sv["k"], dqn, dkn, p["fox_q_norm"], p["fox_k_norm"], hd, "fox_prep_bwd")
    dcum = _pad_cols(jnp.transpose(dcq[:, :, 0] - dck.reshape(heads, s)))
    dfl, dbf = _fox_cum_bwd(dcum, sv["fl"], p["fox_bf_p"], "fox_cum_bwd")
    dfl_b = dfl.astype(BF16)
    dproj = jnp.concatenate([dq, dk, dvb, d_og], axis=1)
    dh_tail = _matmul(dfl_b, w_tail, tb=True, name="fox_in_tail_dx")
    dh1 = _matmul(dproj, w_main, tb=True, name="fox_in_dx", tk=1024,
                  epilogue=lambda acc, tl: (acc + tl,), extras=(("mn", dh_tail),))
    dw_main = _matmul(sv["h1"], dproj, ta=True, name="fox_in_dw", out_dtypes=(BF16,))
    dw_tail = _matmul(sv["h1"], dfl_b, ta=True, name="fox_in_tail_dw", out_dtypes=(BF16,))
    dw_in = jnp.concatenate([dw_main, dw_tail[:, :heads]], axis=1)
    tok = p["send"]("fox", [_cols_pieces(dw_in), dw_out.reshape(N_DEV, -1, d)])
    dx, dsh, dsc, dgain = _adaln_bwd(x, dh1, dx1, p["norm_mix"][i] + tok, p["sc_m"][i], "fox_norm_bwd")
    grads = dict(fox_b_f=dbf[:, :heads], fox_q_norm=gq.reshape(heads, hd).sum(0, keepdims=True),
                 fox_k_norm=gk.reshape(heads, hd).sum(0, keepdims=True), norm_mix=dgain, sh_m=dsh, sc_m=dsc, g_m=dg_m)
    return dx, grads


SMALL = ("b_mod", "norm_mix", "norm_ffn", "gla_b_gate", "gla_norm", "fox_b_f", "fox_q_norm", "fox_k_norm",
         "ffn_conv_b", "norm_final")
SMALL_SHARDED = ("gla_w_gate", "ffn_conv_w")
BIG = ("gla_w_in", "gla_w_out", "fox_w_in", "fox_w_out", "ffn_w_up", "ffn_w_down")
WEIGHTS = ("w_mod", "b_mod", "norm_mix", "norm_ffn", "gla_w_in", "gla_w_gate", "gla_b_gate", "gla_norm", "gla_w_out",
           "fox_w_in", "fox_b_f", "fox_q_norm", "fox_k_norm", "fox_w_out", "ffn_w_up", "ffn_conv_w", "ffn_conv_b",
           "ffn_w_down", "norm_final")


def _pack(parts):
    flat = jnp.concatenate([p.reshape(-1) for p in parts])
    pad = (-flat.shape[0]) % 1024
    return jnp.pad(flat, (0, pad)).reshape(1, -1)


def _unpack(flat, shapes):
    out, off = [], 0
    for shp in shapes:
        n = 1
        for s in shp:
            n *= s
        out.append(flat[0, off:off + n].reshape(shp))
        off += n
    return out


def kernel(x, c, w_mod, b_mod, norm_mix, norm_ffn, gla_w_in, gla_w_gate, gla_b_gate, gla_norm, gla_w_out, fox_w_in, fox_b_f, fox_q_norm, fox_k_norm, fox_w_out, ffn_w_up, ffn_conv_w, ffn_conv_b, ffn_w_down, norm_final, loss_target, m_w_mod, m_b_mod, m_norm_mix, m_norm_ffn, m_gla_w_in, m_gla_w_gate, m_gla_b_gate, m_gla_norm, m_gla_w_out, m_fox_w_in, m_fox_b_f, m_fox_q_norm, m_fox_k_norm, m_fox_w_out, m_ffn_w_up, m_ffn_conv_w, m_ffn_conv_b, m_ffn_w_down, m_norm_final, v_w_mod, v_b_mod, v_norm_mix, v_norm_ffn, v_gla_w_in, v_gla_w_gate, v_gla_b_gate, v_gla_norm, v_gla_w_out, v_fox_w_in, v_fox_b_f, v_fox_q_norm, v_fox_k_norm, v_fox_w_out, v_ffn_w_up, v_ffn_conv_w, v_ffn_conv_b, v_ffn_w_down, v_norm_final):
    w = dict(w_mod=w_mod, b_mod=b_mod, norm_mix=norm_mix, norm_ffn=norm_ffn, gla_w_in=gla_w_in, gla_w_gate=gla_w_gate,
             gla_b_gate=gla_b_gate, gla_norm=gla_norm, gla_w_out=gla_w_out, fox_w_in=fox_w_in, fox_b_f=fox_b_f,
             fox_q_norm=fox_q_norm, fox_k_norm=fox_k_norm, fox_w_out=fox_w_out, ffn_w_up=ffn_w_up,
             ffn_conv_w=ffn_conv_w, ffn_conv_b=ffn_conv_b, ffn_w_down=ffn_w_down, norm_final=norm_final)
    mom_m = dict(w_mod=m_w_mod, b_mod=m_b_mod, norm_mix=m_norm_mix, norm_ffn=m_norm_ffn, gla_w_in=m_gla_w_in,
                 gla_w_gate=m_gla_w_gate, gla_b_gate=m_gla_b_gate, gla_norm=m_gla_norm, gla_w_out=m_gla_w_out,
                 fox_w_in=m_fox_w_in, fox_b_f=m_fox_b_f, fox_q_norm=m_fox_q_norm, fox_k_norm=m_fox_k_norm,
                 fox_w_out=m_fox_w_out, ffn_w_up=m_ffn_w_up, ffn_conv_w=m_ffn_conv_w, ffn_conv_b=m_ffn_conv_b,
                 ffn_w_down=m_ffn_w_down, norm_final=m_norm_final)
    mom_v = dict(w_mod=v_w_mod, b_mod=v_b_mod, norm_mix=v_norm_mix, norm_ffn=v_norm_ffn, gla_w_in=v_gla_w_in,
                 gla_w_gate=v_gla_w_gate, gla_b_gate=v_gla_b_gate, gla_norm=v_gla_norm, gla_w_out=v_gla_w_out,
                 fox_w_in=v_fox_w_in, fox_b_f=v_fox_b_f, fox_q_norm=v_fox_q_norm, fox_k_norm=v_fox_k_norm,
                 fox_w_out=v_fox_w_out, ffn_w_up=v_ffn_w_up, ffn_conv_w=v_ffn_conv_w, ffn_conv_b=v_ffn_conv_b,
                 ffn_w_down=v_ffn_w_down, norm_final=v_norm_final)

    me = 4 * lax.axis_index("x") + 2 * lax.axis_index("y") + lax.axis_index("c")
    xs, target = x[0], loss_target[0]
    s, d = xs.shape
    depth = w_mod.shape[0]
    mod_cols = w_mod.shape[2]
    rank = gla_w_gate.shape[1]
    hd = fox_q_norm.shape[1]
    fox_heads = d // hd
    dk_total = gla_w_gate.shape[2] * N_DEV

    big_names = ["gla_in", "gla_out", "up0", "down0", "fox_in", "fox_out", "up1", "down1"]
    big_shards = [gla_w_in[0], gla_w_out[0], ffn_w_up[0], ffn_w_down[0], fox_w_in[0], fox_w_out[0],
                  ffn_w_up[1], ffn_w_down[1]]
    big_shards = [t.astype(BF16) for t in big_shards]
    handles, tok0 = _exchange_start(big_shards, "gather_weights_start", scatter=False)
    in_flight = dict(zip(big_names, zip(handles, big_shards)))
    ready = {}

    def split_tail(full, tail):
        main = full.shape[1] - tail
        return full[:, :main], _pad_cols(full[:, main:])

    def fetch(key, after):
        if key not in ready:
            handle, mine = in_flight[key]
            land = _exchange_wait([handle], after, f"gather_{key}_wait", scatter=False)[0]
            full = _with_own_block(land, mine, me)
            if key == "gla_in":
                ready[key] = split_tail(_cols_full(full), rank)
            elif key == "fox_in":
                ready[key] = split_tail(_cols_full(full), fox_heads)
            elif key.startswith("up"):
                ready[key] = _cols_full(full)
            else:
                ready[key] = full.reshape(-1, d)
        return ready[key]

    sent = {}

    def send(key, pieces):
        hs, tok = _exchange_start(pieces, f"scatter_{key}_start", scatter=True)
        sent[key] = (hs, pieces)
        return tok

    cond = c * (1.0 / (1.0 + jnp.exp(-c))) + tok0
    g = _exchange([gla_w_gate[0], ffn_conv_w, cond], "gather_small", scatter=False)
    cond_all = g[2][:, 0, :]
    p = dict(
        fetch=fetch, send=send,
        gla_wg_p=jnp.pad(_cols_full(g[0]), ((0, LANE - rank), (0, 0))),
        conv_w=[jnp.transpose(g[1][:, i], (1, 0, 2)).reshape(ffn_conv_w.shape[1], -1) for i in range(depth)],
        conv_b=[ffn_conv_b[i:i + 1] for i in range(depth)],
        gla_b_gate=gla_b_gate, gla_norm=gla_norm, fox_q_norm=fox_q_norm, fox_k_norm=fox_k_norm,
        fox_bf_p=_pad_cols(fox_b_f), gla_rank=rank,
        norm_mix=[norm_mix[i:i + 1] + (tok0 if i == 0 else 0.0) for i in range(depth)],
        norm_ffn=[norm_ffn[i:i + 1] for i in range(depth)],
    )

    cond_pad = jnp.pad(cond_all, ((0, 16 - N_DEV), (0, 0)))
    mod_part = []
    for i in range(depth):
        b_cols = lax.dynamic_slice(b_mod[i:i + 1], (0, me * mod_cols), (1, mod_cols))
        mod_part.append(_matmul(cond_pad, w_mod[i], name=f"mod_{i}", tn=768,
                                epilogue=lambda acc, b: (acc + b,), extras=(("n", b_cols),))[:N_DEV])
    mod_all = _exchange([jnp.stack(mod_part)], "gather_mod", scatter=False)[0]
    mod = lax.dynamic_index_in_dim(mod_all, me, axis=2, keepdims=False)
    mod = jnp.transpose(mod, (1, 0, 2)).reshape(depth, 6, 1, d)
    for j, nm in enumerate(("sh_m", "sc_m", "g_m", "sh_f", "sc_f", "g_f")):
        p[nm] = [mod[i, j] for i in range(depth)]

    acts, saved = [xs], []
    for i in range(depth):
        layer_fwd = _gla_layer_fwd if i % 2 == 0 else _fox_layer_fwd
        x1, sv_mix = layer_fwd(acts[-1], p, i)
        x2, sv_ffn = _ffn_fwd(x1, p, i, str(i))
        saved.append((acts[-1], x1, sv_mix, sv_ffn))
        acts.append(x2)
    dx, d_norm_final, loss_part = _final_loss(acts[-1], target, norm_final.reshape(1, d), "final_loss")

    lg = [None] * depth
    for i in reversed(range(depth)):
        x_in, x1, sv_mix, sv_ffn = saved[i]
        dx, g_ffn = _ffn_bwd(dx, x1, sv_ffn, p, i, str(i))
        layer_bwd = _gla_layer_bwd if i % 2 == 0 else _fox_layer_bwd
        dx, g_mix = layer_bwd(dx, x_in, sv_mix, p, i)
        lg[i] = {**g_ffn, **g_mix}
    grad_x = dx[None]

    gla_l = [i for i in range(depth) if i % 2 == 0]
    fox_l = [i for i in range(depth) if i % 2 == 1]
    small_parts = dict(
        norm_mix=jnp.concatenate([lg[i]["norm_mix"] for i in range(depth)]),
        norm_ffn=jnp.concatenate([lg[i]["norm_ffn"] for i in range(depth)]),
        gla_b_gate=jnp.concatenate([lg[i]["gla_b_gate"] for i in gla_l]),
        gla_norm=jnp.concatenate([lg[i]["gla_norm"] for i in gla_l]),
        fox_b_f=jnp.concatenate([lg[i]["fox_b_f"] for i in fox_l]),
        fox_q_norm=jnp.concatenate([lg[i]["fox_q_norm"] for i in fox_l]),
        fox_k_norm=jnp.concatenate([lg[i]["fox_k_norm"] for i in fox_l]),
        ffn_conv_b=jnp.concatenate([lg[i]["conv_b"] for i in range(depth)]),
        norm_final=d_norm_final,
        gla_w_gate=jnp.stack([lg[i]["gla_w_gate"] for i in gla_l]),
        ffn_conv_w=jnp.stack([lg[i]["conv_w"] for i in range(depth)]),
        loss=loss_part[:, :1],
    )
    order = ("norm_mix", "norm_ffn", "gla_b_gate", "gla_norm", "fox_b_f", "fox_q_norm", "fox_k_norm", "ffn_conv_b",
             "norm_final", "gla_w_gate", "ffn_conv_w", "loss")
    packed = _pack([small_parts[nm] for nm in order])
    dmod = jnp.stack([jnp.concatenate([lg[i][nm] for nm in ("sh_m", "sc_m", "g_m", "sh_f", "sc_f", "g_f")], axis=1)
                      for i in range(depth)])
    packed_all, dmod_all = _exchange([packed, dmod], "gather_small_grads", scatter=False)
    summed = _unpack(_sum8(packed_all, "sum_small_grads"), [small_parts[nm].shape for nm in order])
    small_g = dict(zip(order, summed))
    loss = small_g["loss"][0, 0]
    dmod_all = dmod_all[:, :, 0, :]

    grads = {}
    cond_t = _pad_cols(jnp.transpose(cond_all)).astype(BF16)
    dmod_cols = lax.dynamic_slice(dmod_all, (0, 0, me * mod_cols), (N_DEV, depth, mod_cols))
    g_w_mod = []
    for i in range(depth):
        rhs = jnp.pad(dmod_cols[:, i], ((0, LANE - N_DEV), (0, 0)))
        g_w_mod.append(_matmul(cond_t, rhs, name=f"mod_dw_{i}", tn=768))
    grads["w_mod"] = jnp.stack(g_w_mod)
    small_g["b_mod"] = _sum8(dmod_all.reshape(N_DEV, 1, -1), "sum_b_mod").reshape(depth, -1)

    received = {}
    for key in ("ffn1", "fox", "ffn0", "gla"):
        hs, pieces = sent[key]
        lands = _exchange_wait(hs, dx, f"scatter_{key}_wait", scatter=True)
        received[key] = [_with_own_block(land, lax.dynamic_index_in_dim(pc, me, 0, keepdims=False), me)
                         for land, pc in zip(lands, pieces)]

    out_g, out_d, out_m, out_v = {}, {}, {}, {}

    def update(nm, g_arr):
        res = _adamw(w[nm], g_arr, mom_m[nm], mom_v[nm], f"adamw_{nm}")
        out_g[nm], out_d[nm], out_m[nm], out_v[nm] = res

    update("gla_w_in", [received["gla"][0]])
    update("gla_w_out", [received["gla"][1]])
    update("fox_w_in", [received["fox"][0]])
    update("fox_w_out", [received["fox"][1]])
    update("ffn_w_up", [received[f"ffn{i}"][0] for i in range(depth)])
    update("ffn_w_down", [received[f"ffn{i}"][1] for i in range(depth)])
    update("w_mod", grads["w_mod"])

    gate_cols = gla_w_gate.shape[2]
    conv_cols = ffn_conv_w.shape[2]
    local_small = dict(small_g)
    local_small["gla_w_gate"] = lax.dynamic_slice_in_dim(small_g["gla_w_gate"], me * gate_cols, gate_cols, axis=2)
    local_small["ffn_conv_w"] = lax.dynamic_slice_in_dim(small_g["ffn_conv_w"], me * conv_cols, conv_cols, axis=2)
    names = SMALL + SMALL_SHARDED
    shapes = [w[nm].shape for nm in names]
    res = _adamw(_pack([w[nm] for nm in names])[None], _pack([local_small[nm] for nm in names])[None],
                 _pack([mom_m[nm] for nm in names])[None], _pack([mom_v[nm] for nm in names])[None], "adamw_small")
    for tgt, flat in zip((out_g, out_d, out_m, out_v), res):
        for nm, arr in zip(names, _unpack(flat[0], shapes)):
            tgt[nm] = arr

    return (loss, grad_x, *[out_g[n] for n in WEIGHTS], *[out_d[n] for n in WEIGHTS],
            *[out_m[n] for n in WEIGHTS], *[out_v[n] for n in WEIGHTS])
```

```python
import jax
import jax.numpy as jnp
from jax import lax
from jax.experimental import pallas as pl
from jax.experimental.pallas import tpu as pltpu

F32, BF16 = jnp.float32, jnp.bfloat16
N_DEV = 8
GLA_HEADS = 4
GLA_TAU = 16.0
GLA_CHUNK = 64
NORM_EPS = 1e-6
ADAM_LR, ADAM_B1, ADAM_B2, ADAM_EPS, ADAM_WD, ADAM_STEP = 0.001, 0.9, 0.999, 1e-08, 0.01, 10
LANE = 128
VMEM_LIMIT = 56 * 1024 * 1024
NEG = -1e30


def _pcall(body, **kw):
    return pl.pallas_call(body, **kw)


def _params(n_axes):
    return pltpu.CompilerParams(dimension_semantics=("arbitrary",) * n_axes, vmem_limit_bytes=VMEM_LIMIT)


def _tile(dim, pref):
    if dim <= pref:
        return dim
    t = pref
    while dim % t:
        t -= LANE
    assert t > 0, (dim, pref)
    return t


def _dot(a, b, ta=False, tb=False):
    dims = (((0,) if ta else (1,), (1,) if tb else (0,)), ((), ()))
    return lax.dot_general(a.astype(BF16), b.astype(BF16), dims, preferred_element_type=F32)


def _split3(x):
    hi = x.astype(BF16)
    r1 = x - hi.astype(F32)
    mid = r1.astype(BF16)
    lo = (r1 - mid.astype(F32)).astype(BF16)
    return hi, mid, lo


def _tri_matmul(tri, x):
    hi, mid, lo = _split3(x)
    return _dot(tri, hi) + _dot(tri, mid) + _dot(tri, lo)


def _tri(n, upper=False):
    r = lax.broadcasted_iota(jnp.int32, (n, n), 0)
    c = lax.broadcasted_iota(jnp.int32, (n, n), 1)
    return jnp.where((r <= c) if upper else (r >= c), 1.0, 0.0).astype(BF16)


def _log_sigmoid(x):
    return jnp.minimum(x, 0.0) - jnp.log(1.0 + jnp.exp(-jnp.abs(x)))


def _sigmoid(x):
    return 1.0 / (1.0 + jnp.exp(-x))


def _silu(x):
    return x * _sigmoid(x)


def _dsilu(x):
    s = _sigmoid(x)
    return s * (1.0 + x * (1.0 - s))


def _matmul(a, b, *, name, ta=False, tb=False, out_dtypes=(F32,), tm=1024, tn=1024, tk=2048,
            epilogue=None, extras=(), a_halves=False, b_halves=False):
    if a_halves:
        assert not ta
        m, k = a.shape[1], 2 * a.shape[2]
    else:
        m, k = (a.shape[1], a.shape[0]) if ta else a.shape
    if b_halves:
        assert not tb and b.shape[1] == k
        n = 2 * b.shape[2]
    else:
        n = b.shape[0] if tb else b.shape[1]
        assert (b.shape[1] if tb else b.shape[0]) == k, (a.shape, b.shape, ta, tb)
    tm, tn, tk = _tile(m, tm), _tile(n // 2 if b_halves else n, tn), _tile(k // 2 if a_halves else k, tk)
    nk = k // tk
    if a_halves:
        a_spec = pl.BlockSpec((None, tm, tk), lambda i, j, kk: (kk // (nk // 2), i, kk % (nk // 2)))
    elif ta:
        a_spec = pl.BlockSpec((tk, tm), lambda i, j, kk: (kk, i))
    else:
        a_spec = pl.BlockSpec((tm, tk), lambda i, j, kk: (i, kk))
    if b_halves:
        b_spec = pl.BlockSpec((None, tk, tn), lambda i, j, kk: (j // (n // tn // 2), kk, j % (n // tn // 2)))
    elif tb:
        b_spec = pl.BlockSpec((tn, tk), lambda i, j, kk: (j, kk))
    else:
        b_spec = pl.BlockSpec((tk, tn), lambda i, j, kk: (kk, j))
    ex_specs = []
    for kind, arr in extras:
        if kind == "mn":
            assert arr.shape == (m, n), (arr.shape, m, n)
            ex_specs.append(pl.BlockSpec((tm, tn), lambda i, j, kk: (i, j)))
        else:
            assert arr.shape == (1, n), (arr.shape, n)
            ex_specs.append(pl.BlockSpec((1, tn), lambda i, j, kk: (0, j)))
    n_ex, n_out = len(extras), len(out_dtypes)

    def body(a_ref, b_ref, *rest):
        ex, outs, acc = rest[:n_ex], rest[n_ex:n_ex + n_out], rest[-1]
        kk = pl.program_id(2)

        @pl.when(kk == 0)
        def _():
            acc[...] = jnp.zeros_like(acc)

        acc[...] += _dot(a_ref[...], b_ref[...], ta, tb)

        @pl.when(kk == nk - 1)
        def _():
            if epilogue is None:
                vals = (acc[...],)
            else:
                vals = epilogue(acc[...], *[e[...] for e in ex])
            for o, v in zip(outs, vals):
                o[...] = v.astype(o.dtype)

    res = _pcall(
        body, name=name, grid=(m // tm, n // tn, nk),
        in_specs=[a_spec, b_spec] + ex_specs,
        out_specs=[pl.BlockSpec((tm, tn), lambda i, j, kk: (i, j))] * n_out,
        out_shape=[jax.ShapeDtypeStruct((m, n), d) for d in out_dtypes],
        scratch_shapes=[pltpu.VMEM((tm, tn), F32)],
        compiler_params=_params(3),
    )(a, b, *[arr for _, arr in extras])
    return res[0] if n_out == 1 else res


def _rowwise(fn, ins, outs, *, name, tr=128):
    rows = next(arr.shape[0] for kind, arr in ins if kind == "row")
    tr = _tile(rows, tr)
    in_specs = []
    for kind, arr in ins:
        if kind == "row":
            assert arr.shape[0] == rows and arr.ndim == 2
            in_specs.append(pl.BlockSpec((tr, arr.shape[1]), lambda i: (i, 0)))
        else:
            in_specs.append(pl.BlockSpec(arr.shape, lambda i, nd=arr.ndim: (0,) * nd))
    out_specs, out_shape = [], []
    for kind, w, dt in outs:
        if kind == "row":
            out_specs.append(pl.BlockSpec((tr, w), lambda i: (i, 0)))
            out_shape.append(jax.ShapeDtypeStruct((rows, w), dt))
        else:
            out_specs.append(pl.BlockSpec((1, w), lambda i: (0, 0)))
            out_shape.append(jax.ShapeDtypeStruct((1, w), dt))
    n_in = len(ins)

    def body(*refs):
        i = pl.program_id(0)
        vals = fn(*[r[...] for r in refs[:n_in]])
        for (kind, _, _), o, v in zip(outs, refs[n_in:], vals):
            if kind == "row":
                o[...] = v.astype(o.dtype)
            else:
                @pl.when(i == 0)
                def _(o=o):
                    o[...] = jnp.zeros_like(o)

                o[...] += v.astype(o.dtype)

    return _pcall(body, name=name, grid=(rows // tr,), in_specs=in_specs, out_specs=out_specs,
                  out_shape=out_shape, compiler_params=_params(1))(*[arr for _, arr in ins])


def _colsum(x):
    return jnp.sum(x, axis=0, keepdims=True)


def _norm_stats(x):
    rstd = lax.rsqrt(jnp.mean(x * x, axis=-1, keepdims=True) + NORM_EPS)
    return x * rstd, rstd


def _norm_bwd(dxhat, xhat, rstd):
    return rstd * (dxhat - xhat * jnp.mean(dxhat * xhat, axis=-1, keepdims=True))


def _adaln_fwd(x, gain, sc, sh, name):
    def fn(x, gain, sc, sh):
        xhat, _ = _norm_stats(x)
        return ((xhat * gain) * (1.0 + sc) + sh,)

    return _rowwise(fn, [("row", x), ("full", gain), ("full", sc), ("full", sh)],
                    [("row", x.shape[1], BF16)], name=name)[0]


def _adaln_bwd(x, dh, dres, gain, sc, name):
    d = x.shape[1]

    def fn(x, dh, dres, gain, sc):
        xhat, rstd = _norm_stats(x)
        dxhat = dh * (gain * (1.0 + sc))
        dx = dres + _norm_bwd(dxhat, xhat, rstd)
        return dx, _colsum(dh), _colsum(dh * (xhat * gain)), _colsum(dh * xhat * (1.0 + sc))

    return _rowwise(fn, [("row", x), ("row", dh), ("row", dres), ("full", gain), ("full", sc)],
                    [("row", d, F32), ("acc", d, F32), ("acc", d, F32), ("acc", d, F32)], name=name)


def _residual_bwd(dx, y, g, name):
    d = dx.shape[1]

    def fn(dx, y, g):
        return dx * (1.0 + g), _colsum(dx * y)

    return _rowwise(fn, [("row", dx), ("row", y), ("full", g)], [("row", d, BF16), ("acc", d, F32)], name=name)


def _final_loss(x, target, gain, name):
    d = x.shape[1]

    def fn(x, t, gain):
        xhat, rstd = _norm_stats(x)
        err = xhat * gain - t
        dy = err * (1.0 / d)
        loss = 0.5 * jnp.sum(jnp.mean(err * err, axis=-1, keepdims=True), axis=0, keepdims=True)
        dx = _norm_bwd(dy * gain, xhat, rstd)
        return dx, _colsum(dy * xhat), jnp.broadcast_to(loss, (1, LANE))

    return _rowwise(fn, [("row", x), ("row", target), ("full", gain)],
                    [("row", d, F32), ("acc", d, F32), ("acc", LANE, F32)], name=name)


def _gla_gates(q_ref, k_ref, a_ref, wg_ref, bg_ref, scale, c):
    ga = _dot(a_ref[...], wg_ref[...]) + bg_ref[...]
    la = _log_sigmoid(ga) * (1.0 / GLA_TAU)
    b = _tri_matmul(_tri(c), la)
    bl = _colsum(la)
    eb, enb, eend = jnp.exp(b), jnp.exp(-b), jnp.exp(bl - b)
    q = q_ref[...] * scale
    k = k_ref[...]
    return dict(ga=ga, eb=eb, enb=enb, eend=eend, dec=jnp.exp(bl), q_dec=q * eb, k_inv=k * enb, k_end=k * eend)


def _causal(c):
    return lax.broadcasted_iota(jnp.int32, (c, c), 0) >= lax.broadcasted_iota(jnp.int32, (c, c), 1)


def _gla_specs(heads, c, dk, dv, rev, n_chunks):
    def ch(n):
        return (n_chunks - 1 - n) if rev else n

    return [
        pl.BlockSpec((c, dk), lambda h, n: (ch(n), h)),
        pl.BlockSpec((c, dk), lambda h, n: (ch(n), heads + h)),
        pl.BlockSpec((c, dv), lambda h, n: (ch(n), heads + h)),
        pl.BlockSpec((c, LANE), lambda h, n: (ch(n), 0)),
        pl.BlockSpec((LANE, dk), lambda h, n: (0, h)),
        pl.BlockSpec((1, dk), lambda h, n: (0, h)),
    ]


def _gla_fwd(proj, a_tail, wg_p, bg, name):
    s = proj.shape[0]
    heads, c = GLA_HEADS, GLA_CHUNK
    dk = wg_p.shape[1] // heads
    dv = 2 * dk
    n_chunks = s // c
    scale = dk ** -0.5

    def body(q_ref, k_ref, v_ref, a_ref, wg_ref, bg_ref, o_ref, st_ref, state):
        @pl.when(pl.program_id(1) == 0)
        def _():
            state[...] = jnp.zeros_like(state)

        g = _gla_gates(q_ref, k_ref, a_ref, wg_ref, bg_ref, scale, c)
        v = v_ref[...]
        st = state[...]
        attn = jnp.where(_causal(c), _dot(g["q_dec"], g["k_inv"], tb=True), 0.0)
        o_ref[...] = _dot(attn, v) + _dot(g["q_dec"], st, tb=True)
        st_ref[...] = st.astype(st_ref.dtype)
        state[...] = g["dec"] * st + _dot(v, g["k_end"], ta=True)

    return _pcall(
        body, name=name, grid=(heads, n_chunks),
        in_specs=_gla_specs(heads, c, dk, dv, False, n_chunks),
        out_specs=[pl.BlockSpec((c, dv), lambda h, n: (n, h)),
                   pl.BlockSpec((None, None, dv, dk), lambda h, n: (h, n, 0, 0))],
        out_shape=[jax.ShapeDtypeStruct((s, heads * dv), F32),
                   jax.ShapeDtypeStruct((heads, n_chunks, dv, dk), BF16)],
        scratch_shapes=[pltpu.VMEM((dv, dk), F32)],
        compiler_params=_params(2),
    )(proj, proj, proj, a_tail, wg_p, bg)


def _gla_bwd(proj, a_tail, wg_p, bg, states, d_o, name):
    s = proj.shape[0]
    heads, c = GLA_HEADS, GLA_CHUNK
    dk = wg_p.shape[1] // heads
    dv = 2 * dk
    n_chunks = s // c
    scale = dk ** -0.5

    def body(q_ref, k_ref, v_ref, a_ref, wg_ref, bg_ref, st_ref, do_ref, dq_ref, dk_ref, dv_ref, dga_ref, dstate):
        @pl.when(pl.program_id(1) == 0)
        def _():
            dstate[...] = jnp.zeros_like(dstate)

        g = _gla_gates(q_ref, k_ref, a_ref, wg_ref, bg_ref, scale, c)
        v, st, dst, d_out = v_ref[...], st_ref[...], dstate[...], do_ref[...]
        q_dec, k_inv, k_end = g["q_dec"], g["k_inv"], g["k_end"]
        mask = _causal(c)
        attn = jnp.where(mask, _dot(q_dec, k_inv, tb=True), 0.0)
        d_attn = jnp.where(mask, _dot(d_out, v, tb=True), 0.0)
        d_qdec = _dot(d_attn, k_inv) + _dot(d_out, st)
        d_kinv = _dot(d_attn, q_dec, ta=True)
        d_kend = _dot(v, dst)
        dv_ref[...] = (_dot(attn, d_out, ta=True) + _dot(k_end, dst, tb=True)).astype(dv_ref.dtype)
        d_dec = jnp.sum(dst * st.astype(F32), axis=0, keepdims=True)
        dstate[...] = g["dec"] * dst + _dot(d_out, q_dec, ta=True)

        dq_ref[...] = (d_qdec * (scale * g["eb"])).astype(dq_ref.dtype)
        dk_ref[...] = (d_kinv * g["enb"] + d_kend * g["eend"]).astype(dk_ref.dtype)
        kk = d_kend * k_end
        db = d_qdec * q_dec - d_kinv * k_inv - kk
        dbl = jnp.sum(kk, axis=0, keepdims=True) + d_dec * g["dec"]
        last = lax.broadcasted_iota(jnp.int32, db.shape, 0) == c - 1
        db = db + jnp.where(last, dbl, 0.0)
        dla = _tri_matmul(_tri(c, upper=True), db)
        dga_ref[...] = dla * (1.0 / GLA_TAU) * _sigmoid(-g["ga"])

    rev = lambda h, n: (n_chunks - 1 - n, h)
    return _pcall(
        body, name=name, grid=(heads, n_chunks),
        in_specs=_gla_specs(heads, c, dk, dv, True, n_chunks) + [
            pl.BlockSpec((None, None, dv, dk), lambda h, n: (h, n_chunks - 1 - n, 0, 0)),
            pl.BlockSpec((c, dv), rev)],
        out_specs=[pl.BlockSpec((c, dk), rev), pl.BlockSpec((c, dk), rev), pl.BlockSpec((c, dv), rev),
                   pl.BlockSpec((c, dk), rev)],
        out_shape=[jax.ShapeDtypeStruct((s, heads * dk), BF16), jax.ShapeDtypeStruct((s, heads * dk), BF16),
                   jax.ShapeDtypeStruct((s, heads * dv), BF16), jax.ShapeDtypeStruct((s, heads * dk), F32)],
        scratch_shapes=[pltpu.VMEM((dv, dk), F32)],
        compiler_params=_params(2),
    )(proj, proj, proj, a_tail, wg_p, bg, states, d_o)


def _gla_post_fwd(o, r, gn, name):
    dvt = o.shape[1]
    dv = dvt // GLA_HEADS

    def fn(o, r, gn):
        outs = []
        for h in range(GLA_HEADS):
            sl = slice(h * dv, (h + 1) * dv)
            ohat, _ = _norm_stats(o[:, sl])
            outs.append((ohat * gn[:, sl]) * _silu(r[:, sl]))
        return (jnp.concatenate(outs, axis=1),)

    return _rowwise(fn, [("row", o), ("row", r), ("full", gn)], [("row", dvt, BF16)], name=name)[0]


def _gla_post_bwd(o, r, gn, dog, name):
    dvt = o.shape[1]
    dv = dvt // GLA_HEADS

    def fn(o, r, gn, dog):
        d_o, d_r, d_g = [], [], []
        for h in range(GLA_HEADS):
            sl = slice(h * dv, (h + 1) * dv)
            ohat, rstd = _norm_stats(o[:, sl])
            g, rr, dd = gn[:, sl], r[:, sl], dog[:, sl]
            d_r.append(dd * (ohat * g) * _dsilu(rr))
            don = dd * _silu(rr)
            d_g.append(_colsum(don * ohat))
            d_o.append(_norm_bwd(don * g, ohat, rstd))
        return jnp.concatenate(d_o, axis=1), jnp.concatenate(d_r, axis=1), jnp.concatenate(d_g, axis=1)

    return _rowwise(fn, [("row", o), ("row", r), ("full", gn), ("row", dog)],
                    [("row", dvt, F32), ("row", dvt, BF16), ("acc", dvt, F32)], name=name)


def _fox_prep(q, k, v, qg, kg, hd, name):
    d = q.shape[1]
    heads = d // hd
    scale = hd ** -0.5

    def fn(q, k, v, qg, kg):
        qs, ks = [], []
        for h in range(heads):
            sl = slice(h * hd, (h + 1) * hd)
            qs.append(_norm_stats(q[:, sl])[0] * qg * scale)
            ks.append(_norm_stats(k[:, sl])[0] * kg)
        return jnp.concatenate(qs, axis=1), jnp.concatenate(ks, axis=1), v

    return _rowwise(fn, [("row", q), ("row", k), ("row", v), ("full", qg), ("full", kg)],
                    [("row", d, BF16)] * 3, name=name)


def _fox_prep_bwd(q, k, dqn, dkn, qg, kg, hd, name):
    d = q.shape[1]
    heads = d // hd
    scale = hd ** -0.5

    def fn(q, k, dqn, dkn, qg, kg):
        dq, dk, gq, gk = [], [], [], []
        for h in range(heads):
            sl = slice(h * hd, (h + 1) * hd)
            for x, dxn, g, s, dl, gl in ((q, dqn, qg, scale, dq, gq), (k, dkn, kg, 1.0, dk, gk)):
                xhat, rstd = _norm_stats(x[:, sl])
                dn = dxn[:, sl] * s
                gl.append(_colsum(dn * xhat))
                dl.append(_norm_bwd(dn * g, xhat, rstd))
        cat = lambda t: jnp.concatenate(t, axis=1)
        return cat(dq), cat(dk), cat(gq), cat(gk)

    return _rowwise(fn, [("row", q), ("row", k), ("row", dqn), ("row", dkn), ("full", qg), ("full", kg)],
                    [("row", d, BF16), ("row", d, BF16), ("acc", d, F32), ("acc", d, F32)], name=name)


def _fox_cum(fl, bf_p, name, tb=256):
    s = fl.shape[0]
    tb = _tile(s, tb)

    def body(fl_ref, bf_ref, cum_ref, carry):
        @pl.when(pl.program_id(0) == 0)
        def _():
            carry[...] = jnp.zeros_like(carry)

        lf = _log_sigmoid(fl_ref[...] + bf_ref[...])
        cum_ref[...] = _tri_matmul(_tri(tb), lf) + carry[...]
        carry[...] += _colsum(lf)

    return _pcall(
        body, name=name, grid=(s // tb,),
        in_specs=[pl.BlockSpec((tb, LANE), lambda i: (i, 0)), pl.BlockSpec((1, LANE), lambda i: (0, 0))],
        out_specs=pl.BlockSpec((tb, LANE), lambda i: (i, 0)),
        out_shape=jax.ShapeDtypeStruct((s, LANE), F32),
        scratch_shapes=[pltpu.VMEM((1, LANE), F32)],
        compiler_params=_params(1),
    )(fl, bf_p)


def _fox_cum_bwd(dcum, fl, bf_p, name, tb=256):
    s = fl.shape[0]
    tb = _tile(s, tb)
    nb = s // tb

    def body(dc_ref, fl_ref, bf_ref, dfl_ref, dbf_ref, carry):
        @pl.when(pl.program_id(0) == 0)
        def _():
            carry[...] = jnp.zeros_like(carry)
            dbf_ref[...] = jnp.zeros_like(dbf_ref)

        dc = dc_ref[...]
        dlf = _tri_matmul(_tri(tb, upper=True), dc) + carry[...]
        carry[...] += _colsum(dc)
        dfl = dlf * _sigmoid(-(fl_ref[...] + bf_ref[...]))
        dfl_ref[...] = dfl
        dbf_ref[...] += _colsum(dfl)

    rev = lambda i: (nb - 1 - i, 0)
    return _pcall(
        body, name=name, grid=(nb,),
        in_specs=[pl.BlockSpec((tb, LANE), rev), pl.BlockSpec((tb, LANE), rev), pl.BlockSpec((1, LANE), lambda i: (0, 0))],
        out_specs=[pl.BlockSpec((tb, LANE), rev), pl.BlockSpec((1, LANE), lambda i: (0, 0))],
        out_shape=[jax.ShapeDtypeStruct((s, LANE), F32), jax.ShapeDtypeStruct((1, LANE), F32)],
        scratch_shapes=[pltpu.VMEM((1, LANE), F32)],
        compiler_params=_params(1),
    )(dcum, fl, bf_p)


def _fox_attn_fwd(qn, kn, vb, cum_col, cum_row, hd, t, name):
    s, d = qn.shape
    heads = d // hd
    nq = s // t

    def body(q_ref, k_ref, v_ref, cc_ref, cr_ref, o_ref, lse_ref):
        qi = pl.program_id(1)
        q = q_ref[...]
        cq = cc_ref[...]
        qpos = qi * t + lax.broadcasted_iota(jnp.int32, (t, 1), 0)

        def step(kj, carry):
            m, l, acc = carry
            off = pl.multiple_of(kj * t, t)
            ks, vs = k_ref[pl.ds(off, t), :], v_ref[pl.ds(off, t), :]
            sc = _dot(q, ks, tb=True) + cq - cr_ref[kj]
            kpos = off + lax.broadcasted_iota(jnp.int32, (1, t), 1)
            sc = jnp.where(kpos <= qpos, sc, NEG)
            m_new = jnp.maximum(m, jnp.max(sc, axis=1, keepdims=True))
            alpha = jnp.exp(m - m_new)
            p = jnp.exp(sc - m_new)
            return m_new, alpha * l + jnp.sum(p, axis=1, keepdims=True), alpha * acc + _dot(p, vs)

        init = (jnp.full((t, 1), NEG, F32), jnp.zeros((t, 1), F32), jnp.zeros((t, hd), F32))
        m, l, acc = lax.fori_loop(0, qi + 1, step, init)
        o_ref[...] = acc / l
        lse_ref[...] = m + jnp.log(l)

    return _pcall(
        body, name=name, grid=(heads, nq),
        in_specs=[pl.BlockSpec((t, hd), lambda h, i: (i, h)),
                  pl.BlockSpec((s, hd), lambda h, i: (0, h)),
                  pl.BlockSpec((s, hd), lambda h, i: (0, h)),
                  pl.BlockSpec((None, t, 1), lambda h, i: (h, i, 0)),
                  pl.BlockSpec((None, nq, 1, t), lambda h, i: (h, 0, 0, 0))],
        out_specs=[pl.BlockSpec((t, hd), lambda h, i: (i, h)), pl.BlockSpec((None, t, 1), lambda h, i: (h, i, 0))],
        out_shape=[jax.ShapeDtypeStruct((s, d), F32), jax.ShapeDtypeStruct((heads, s, 1), F32)],
        compiler_params=_params(2),
    )(qn, kn, vb, cum_col, cum_row)


def _fox_attn_bwd(qn, kn, vb, d_o, o, lse, cum_col, cum_row, hd, t, name):
    s, d = qn.shape
    heads = d // hd
    nq = s // t

    def body(q_ref, k_ref, v_ref, do_ref, o_ref, lse_ref, cc_ref, cr_ref,
             dq_ref, dk_ref, dv_ref, dcq_ref, dck_ref, delta):
        kj = pl.program_id(1)

        @pl.when(kj == 0)
        def _():
            dq_ref[...] = jnp.zeros_like(dq_ref)
            dcq_ref[...] = jnp.zeros_like(dcq_ref)
            delta[...] = jnp.sum(do_ref[...] * o_ref[...], axis=1, keepdims=True)

        ks, vs, cr = k_ref[...], v_ref[...], cr_ref[...]
        kpos = kj * t + lax.broadcasted_iota(jnp.int32, (1, t), 1)

        def step(qi, carry):
            dk, dv, dck = carry
            rows = pl.ds(pl.multiple_of(qi * t, t), t)
            q, d_out = q_ref[rows, :], do_ref[rows, :]
            sc = _dot(q, ks, tb=True) + cc_ref[rows, :] - cr
            qpos = qi * t + lax.broadcasted_iota(jnp.int32, (t, 1), 0)
            p = jnp.where(kpos <= qpos, jnp.exp(sc - lse_ref[rows, :]), 0.0)
            ds = p * (_dot(d_out, vs, tb=True) - delta[rows, :])
            dq_ref[rows, :] += _dot(ds, ks)
            dcq_ref[rows, :] += jnp.sum(ds, axis=1, keepdims=True)
            return dk + _dot(ds, q, ta=True), dv + _dot(p, d_out, ta=True), dck + _colsum(ds)

        init = (jnp.zeros((t, hd), F32), jnp.zeros((t, hd), F32), jnp.zeros((1, t), F32))
        dk, dv, dck = lax.fori_loop(kj, nq, step, init)
        dk_ref[...] = dk.astype(dk_ref.dtype)
        dv_ref[...] = dv.astype(dv_ref.dtype)
        dck_ref[...] = dck

    head_rows = lambda h, j: (0, h)
    blk = lambda h, j: (j, h)
    return _pcall(
        body, name=name, grid=(heads, nq),
        in_specs=[pl.BlockSpec((s, hd), head_rows), pl.BlockSpec((t, hd), blk), pl.BlockSpec((t, hd), blk),
                  pl.BlockSpec((s, hd), head_rows), pl.BlockSpec((s, hd), head_rows),
                  pl.BlockSpec((None, s, 1), lambda h, j: (h, 0, 0)),
                  pl.BlockSpec((None, s, 1), lambda h, j: (h, 0, 0)),
                  pl.BlockSpec((None, None, 1, t), lambda h, j: (h, j, 0, 0))],
        out_specs=[pl.BlockSpec((s, hd), head_rows), pl.BlockSpec((t, hd), blk), pl.BlockSpec((t, hd), blk),
                   pl.BlockSpec((None, s, 1), lambda h, j: (h, 0, 0)),
                   pl.BlockSpec((None, None, 1, t), lambda h, j: (h, j, 0, 0))],
        out_shape=[jax.ShapeDtypeStruct((s, d), F32), jax.ShapeDtypeStruct((s, d), BF16),
                   jax.ShapeDtypeStruct((s, d), BF16), jax.ShapeDtypeStruct((heads, s, 1), F32),
                   jax.ShapeDtypeStruct((heads, nq, 1, t), F32)],
        scratch_shapes=[pltpu.VMEM((s, 1), F32)],
        compiler_params=_params(2),
    )(qn, kn, vb, d_o, o, lse, cum_col, cum_row)


def _fox_gate_fwd(o, og, name):
    def fn(o, og):
        return (o * _sigmoid(og),)

    return _rowwise(fn, [("row", o), ("row", og)], [("row", o.shape[1], BF16)], name=name)[0]


def _fox_gate_bwd(o, og, dact, name):
    def fn(o, og, dact):
        sg = _sigmoid(og)
        return dact * sg, dact * o * sg * (1.0 - sg)

    d = o.shape[1]
    return _rowwise(fn, [("row", o), ("row", og), ("row", dact)], [("row", d, F32), ("row", d, BF16)], name=name)


def _shift_down(x, n):
    rows = lax.broadcasted_iota(jnp.int32, x.shape, 0)
    return jnp.where(rows >= n, pltpu.roll(x, n, 0), 0.0)


def _shift_up(x, n):
    rows = lax.broadcasted_iota(jnp.int32, x.shape, 0)
    return jnp.where(rows < x.shape[0] - n, pltpu.roll(x, x.shape[0] - n, 0), 0.0)


def _conv(u, w_ref, b):
    return w_ref[0:1, :] * _shift_down(u, 2) + w_ref[1:2, :] * _shift_down(u, 1) + w_ref[2:3, :] * u + b


def _conv_act_fwd(u, cw, cb, name, tc=256):
    s, two_f = u.shape
    dff = two_f // 2
    tc = _tile(dff, tc)
    nb = dff // tc

    def body(ug_ref, uv_ref, wg_ref, wv_ref, bg_ref, bv_ref, a_ref):
        gate = _conv(ug_ref[...], wg_ref, bg_ref[...])
        val = _conv(uv_ref[...], wv_ref, bv_ref[...])
        a_ref[...] = (_silu(gate) * val).astype(a_ref.dtype)

    lo, hi = (lambda j: (0, j)), (lambda j: (0, j + nb))
    return _pcall(
        body, name=name, grid=(nb,),
        in_specs=[pl.BlockSpec((s, tc), lo), pl.BlockSpec((s, tc), hi), pl.BlockSpec((3, tc), lo),
                  pl.BlockSpec((3, tc), hi), pl.BlockSpec((1, tc), lo), pl.BlockSpec((1, tc), hi)],
        out_specs=pl.BlockSpec((s, tc), lo),
        out_shape=jax.ShapeDtypeStruct((s, dff), BF16),
        compiler_params=_params(1),
    )(u, u, cw, cw, cb, cb)


def _conv_act_bwd(u, cw, cb, da, name, tc=128):
    s, two_f = u.shape
    dff = two_f // 2
    tc = _tile(dff, tc)
    nb = dff // tc

    def body(ug_ref, uv_ref, wg_ref, wv_ref, bg_ref, bv_ref, da_ref, du_ref, dw_ref, db_ref):
        ug, uv, da = ug_ref[...], uv_ref[...], da_ref[...]
        gate = _conv(ug, wg_ref, bg_ref[...])
        val = _conv(uv, wv_ref, bv_ref[...])
        sg = _sigmoid(gate)
        d_val = da * (gate * sg)
        d_gate = da * val * (sg * (1.0 + gate * (1.0 - sg)))
        for half, (dc, uu, w_ref) in enumerate(((d_gate, ug, wg_ref), (d_val, uv, wv_ref))):
            du = w_ref[0:1, :] * _shift_up(dc, 2) + w_ref[1:2, :] * _shift_up(dc, 1) + w_ref[2:3, :] * dc
            du_ref[half] = du.astype(du_ref.dtype)
            dw_ref[half, 0:1, :] = _colsum(dc * _shift_down(uu, 2))
            dw_ref[half, 1:2, :] = _colsum(dc * _shift_down(uu, 1))
            dw_ref[half, 2:3, :] = _colsum(dc * uu)
            db_ref[half] = _colsum(dc)

    lo, hi = (lambda j: (0, j)), (lambda j: (0, j + nb))
    both = lambda j: (0, 0, j)
    return _pcall(
        body, name=name, grid=(nb,),
        in_specs=[pl.BlockSpec((s, tc), lo), pl.BlockSpec((s, tc), hi), pl.BlockSpec((3, tc), lo),
                  pl.BlockSpec((3, tc), hi), pl.BlockSpec((1, tc), lo), pl.BlockSpec((1, tc), hi),
                  pl.BlockSpec((s, tc), lo)],
        out_specs=[pl.BlockSpec((2, s, tc), both), pl.BlockSpec((2, 3, tc), both), pl.BlockSpec((2, 1, tc), both)],
        out_shape=[jax.ShapeDtypeStruct((2, s, dff), BF16), jax.ShapeDtypeStruct((2, 3, dff), F32),
                   jax.ShapeDtypeStruct((2, 1, dff), F32)],
        compiler_params=_params(1),
    )(u, u, cw, cw, cb, cb, da)


def _adamw_math(w, g, m, v):
    m = ADAM_B1 * m + (1.0 - ADAM_B1) * g
    v = ADAM_B2 * v + (1.0 - ADAM_B2) * (g * g)
    m_hat = m / (1.0 - ADAM_B1 ** ADAM_STEP)
    v_hat = v / (1.0 - ADAM_B2 ** ADAM_STEP)
    delta = -ADAM_LR * (m_hat / (jnp.sqrt(v_hat) + ADAM_EPS) + ADAM_WD * w)
    return delta, m, v


def _adamw(w, g, m, v, name, tr=128):
    layers, r, c = w.shape
    pieces = isinstance(g, (list, tuple))
    if r <= tr or r % 8:
        tr = r
    while r % tr:
        tr -= 8
    nr = r // tr
    g_list = list(g) if pieces else [g]

    def body(w_ref, *rest):
        g_refs, (m_ref, v_ref, go_ref, d_ref, mo_ref, vo_ref) = rest[:len(g_list)], rest[len(g_list):]

        def update(grad):
            delta, m_new, v_new = _adamw_math(w_ref[...], grad, m_ref[...], v_ref[...])
            go_ref[...], d_ref[...], mo_ref[...], vo_ref[...] = grad, delta, m_new, v_new

        if not pieces:
            update(g_refs[0][...])
            return
        for layer, g_ref in enumerate(g_refs):
            @pl.when(pl.program_id(0) == layer)
            def _(g_ref=g_ref):
                grad = g_ref[0].astype(F32)
                for i in range(1, N_DEV):
                    grad = grad + g_ref[i].astype(F32)
                update(grad)

    spec = pl.BlockSpec((None, tr, c), lambda l, i: (l, i, 0))
    if pieces:
        g_specs = [pl.BlockSpec((N_DEV, tr, c),
                                lambda l, i, k=k: (0, jnp.where(l == k, i, jnp.where(l < k, 0, nr - 1)), 0))
                   for k in range(layers)]
    else:
        g_specs = [spec]
    return _pcall(
        body, name=name, grid=(layers, nr), in_specs=[spec] + g_specs + [spec, spec], out_specs=[spec] * 4,
        out_shape=[jax.ShapeDtypeStruct((layers, r, c), F32)] * 4, compiler_params=_params(2),
    )(w, *g_list, m, v)


def _sum8(x, name):
    p = x.shape[2]
    tp = _tile(p, 16 * 1024)

    def body(x_ref, o_ref):
        acc = x_ref[0]
        for i in range(1, N_DEV):
            acc = acc + x_ref[i]
        o_ref[...] = acc

    return _pcall(
        body, name=name, grid=(p // tp,), in_specs=[pl.BlockSpec((N_DEV, 1, tp), lambda i: (0, 0, i))],
        out_specs=pl.BlockSpec((1, tp), lambda i: (0, i)), out_shape=jax.ShapeDtypeStruct((1, p), x.dtype),
        compiler_params=_params(1),
    )(x)


def _exchange(arrays, name, scatter):
    n = len(arrays)
    hbm = pl.BlockSpec(memory_space=pl.ANY)

    def body(*refs):
        ins, outs, token = refs[:n], refs[n:2 * n], refs[2 * n]
        send_sems, recv_sems, local_sems = refs[2 * n + 1:]
        token[...] = jnp.zeros_like(token)
        x, y, c = lax.axis_index("x"), lax.axis_index("y"), lax.axis_index("c")
        me = 4 * x + 2 * y + c
        copies = []
        for a in range(n):
            src_mine = ins[a].at[me] if scatter else ins[a]
            local = pltpu.make_async_copy(src_mine, outs[a].at[me], local_sems.at[a])
            local.start()
            copies.append(local)
            for k in range(1, N_DEV):
                px = 1 - x if k & 4 else x
                py = 1 - y if k & 2 else y
                pc = 1 - c if k & 1 else c
                src = ins[a].at[4 * px + 2 * py + pc] if scatter else ins[a]
                cp = pltpu.make_async_remote_copy(
                    src_ref=src, dst_ref=outs[a].at[me],
                    send_sem=send_sems.at[a * (N_DEV - 1) + k - 1], recv_sem=recv_sems.at[a * (N_DEV - 1) + k - 1],
                    device_id=(px, py, pc), device_id_type=pl.DeviceIdType.MESH)
                cp.start()
                copies.append(cp)
        for cp in copies:
            cp.wait()

    out_shape = [jax.ShapeDtypeStruct(a.shape if scatter else (N_DEV,) + a.shape, a.dtype) for a in arrays]
    res = _pcall(
        body, name=name, in_specs=[hbm] * n, out_specs=[hbm] * n + [pl.BlockSpec(memory_space=pltpu.VMEM)],
        out_shape=out_shape + [jax.ShapeDtypeStruct((8, LANE), F32)],
        scratch_shapes=[pltpu.SemaphoreType.DMA((n * (N_DEV - 1),)), pltpu.SemaphoreType.DMA((n * (N_DEV - 1),)),
                        pltpu.SemaphoreType.DMA((n,))],
        compiler_params=pltpu.CompilerParams(has_side_effects=True),
    )(*arrays)
    return res[:n], res[n][0, 0]


_HBM = pl.BlockSpec(memory_space=pltpu.HBM)
_SEM = pl.BlockSpec(memory_space=pltpu.SEMAPHORE)
_DATAFLOW = pltpu.SideEffectType.DATAFLOW_SIDE_EFFECTING


def _peer(k, x, y, c):
    return (1 - x if k & 4 else x, 1 - y if k & 2 else y, 1 - c if k & 1 else c)


def _exchange_start(arrays, name, scatter):
    n = len(arrays)
    lands = [lax.empty(a.shape if scatter else (N_DEV,) + a.shape, a.dtype) for a in arrays]

    def body(*refs):
        srcs, dsts = refs[:n], refs[n:2 * n]
        send_sems, recv_sems, token = refs[4 * n:5 * n], refs[5 * n:6 * n], refs[6 * n]
        x, y, c = lax.axis_index("x"), lax.axis_index("y"), lax.axis_index("c")
        me = 4 * x + 2 * y + c
        for a in range(n):
            for k in range(1, N_DEV):
                px, py, pc = _peer(k, x, y, c)
                pltpu.make_async_remote_copy(
                    src_ref=srcs[a].at[4 * px + 2 * py + pc] if scatter else srcs[a], dst_ref=dsts[a].at[me],
                    send_sem=send_sems[a].at[k - 1], recv_sem=recv_sems[a].at[k - 1],
                    device_id=(px, py, pc), device_id_type=pl.DeviceIdType.MESH).start()
        token[...] = jnp.zeros_like(token)

    sems = [pltpu.SemaphoreType.DMA((N_DEV - 1,))] * (2 * n)
    res = _pcall(
        body, name=name,
        in_specs=[_HBM] * (2 * n),
        out_specs=[_HBM] * (2 * n) + [_SEM] * (2 * n) + [pl.BlockSpec(memory_space=pltpu.VMEM)],
        out_shape=[pltpu.HBM(a.shape, a.dtype) for a in arrays] + [pltpu.HBM(l.shape, l.dtype) for l in lands]
        + sems + [jax.ShapeDtypeStruct((8, LANE), F32)],
        input_output_aliases={i: i for i in range(2 * n)},
        compiler_params=pltpu.CompilerParams(has_side_effects=_DATAFLOW),
    )(*[pltpu.with_memory_space_constraint(a, pltpu.HBM) for a in arrays],
      *[pltpu.with_memory_space_constraint(l, pltpu.HBM) for l in lands])
    handles = [(res[a], res[n + a], res[2 * n + a], res[3 * n + a]) for a in range(n)]
    return handles, res[4 * n][0, 0]


def _exchange_wait(handles, after, name, scatter):
    n = len(handles)

    def body(*refs):
        srcs, dsts = refs[:n], refs[n:2 * n]
        send_sems, recv_sems = refs[2 * n:3 * n], refs[3 * n:4 * n]
        x, y, c = lax.axis_index("x"), lax.axis_index("y"), lax.axis_index("c")
        me = 4 * x + 2 * y + c
        for a in range(n):
            for k in range(1, N_DEV):
                cp = pltpu.make_async_remote_copy(
                    src_ref=srcs[a].at[me] if scatter else srcs[a], dst_ref=dsts[a].at[me],
                    send_sem=send_sems[a].at[k - 1], recv_sem=recv_sems[a].at[k - 1],
                    device_id=_peer(k, x, y, c), device_id_type=pl.DeviceIdType.MESH)
                cp.wait_send()
                cp.wait_recv()

    srcs, lands = [h[0] for h in handles], [h[1] for h in handles]
    res = _pcall(
        body, name=name,
        in_specs=[_HBM] * (2 * n) + [_SEM] * (2 * n) + [pl.BlockSpec(memory_space=pl.ANY)],
        out_specs=[_HBM] * (2 * n),
        out_shape=[pltpu.HBM(t.shape, t.dtype) for t in srcs + lands],
        input_output_aliases={i: i for i in range(2 * n)},
        compiler_params=pltpu.CompilerParams(has_side_effects=_DATAFLOW),
    )(*srcs, *lands, *[h[2] for h in handles], *[h[3] for h in handles], after)
    return res[n:]


def _with_own_block(land, mine, me):
    return lax.dynamic_update_slice(land, mine[None], (me,) + (0,) * mine.ndim)


def _pad_cols(x, width=LANE):
    return jnp.pad(x, ((0, 0), (0, width - x.shape[1])))


def _cols_full(g):
    return jnp.transpose(g, (1, 0, 2)).reshape(g.shape[1], -1)


def _cols_pieces(dw):
    k = dw.shape[0]
    return jnp.transpose(dw.reshape(k, N_DEV, -1), (1, 0, 2))


def _ffn_fwd(x1, p, i, tag):
    h2 = _adaln_fwd(x1, p["norm_ffn"][i], p["sc_f"][i], p["sh_f"][i], f"ffn_norm_{tag}")
    u = _matmul(h2, p["fetch"](f"up{i}", h2), name=f"ffn_up_{tag}")
    a = _conv_act_fwd(u, p["conv_w"][i], p["conv_b"][i], f"ffn_act_{tag}")
    g_f = p["g_f"][i]
    x2, f = _matmul(a, p["fetch"](f"down{i}", a), name=f"ffn_down_{tag}", tk=512, out_dtypes=(F32, F32),
                    epilogue=lambda acc, x1, g: (x1 + (1.0 + g) * acc, acc), extras=(("mn", x1), ("n", g_f)))
    return x2, dict(h2=h2, u=u, a=a, f=f)


def _ffn_bwd(dx2, x1, saved, p, i, tag):
    d = x1.shape[1]
    w_up, w_down = p["fetch"](f"up{i}", None), p["fetch"](f"down{i}", None)
    df, dg_f = _residual_bwd(dx2, saved["f"], p["g_f"][i], f"ffn_res_bwd_{tag}")
    da = _matmul(df, w_down, tb=True, name=f"ffn_down_dx_{tag}", tn=512)
    dw_down = _matmul(saved["a"], df, ta=True, name=f"ffn_down_dw_{tag}", tm=1408, out_dtypes=(BF16,))
    du, dcw, dcb = _conv_act_bwd(saved["u"], p["conv_w"][i], p["conv_b"][i], da, f"ffn_act_bwd_{tag}")
    dcw, dcb = (jnp.concatenate([t[0], t[1]], axis=1) for t in (dcw, dcb))
    dh2 = _matmul(du, w_up, tb=True, name=f"ffn_up_dx_{tag}", tk=1408, a_halves=True)
    dw_up = _matmul(saved["h2"], du, ta=True, name=f"ffn_up_dw_{tag}", tn=1408, out_dtypes=(BF16,), b_halves=True)
    tok = p["send"](f"ffn{i}", [_cols_pieces(dw_up), dw_down.reshape(N_DEV, -1, d)])
    dx1, dsh, dsc, dgain = _adaln_bwd(x1, dh2, dx2, p["norm_ffn"][i] + tok, p["sc_f"][i], f"ffn_norm_bwd_{tag}")
    grads = dict(conv_w=dcw, conv_b=dcb, norm_ffn=dgain, sh_f=dsh, sc_f=dsc, g_f=dg_f)
    return dx1, grads


def _gla_layer_fwd(x, p, i):
    h1 = _adaln_fwd(x, p["norm_mix"][i], p["sc_m"][i], p["sh_m"][i], "gla_norm")
    w_main, w_tail = p["fetch"]("gla_in", h1)
    proj = _matmul(h1, w_main, name="gla_in")
    a_tail = _matmul(h1, w_tail, name="gla_in_tail")
    dk_total = p["gla_wg_p"].shape[1]
    o, states = _gla_fwd(proj, a_tail, p["gla_wg_p"], p["gla_b_gate"], "gla_chunks")
    r = proj[:, 2 * dk_total + o.shape[1]:]
    og = _gla_post_fwd(o, r, p["gla_norm"], "gla_post")
    x1, y = _matmul(og, p["fetch"]("gla_out", og), name="gla_out", out_dtypes=(F32, F32),
                    epilogue=lambda acc, x, g: (x + (1.0 + g) * acc, acc), extras=(("mn", x), ("n", p["g_m"][i])))
    return x1, dict(h1=h1, proj=proj, a_tail=a_tail, o=o, r=r, states=states, og=og, y=y)


def _gla_layer_bwd(dx1, x, sv, p, i):
    d = x.shape[1]
    (w_main, w_tail), w_out = p["fetch"]("gla_in", None), p["fetch"]("gla_out", None)
    dy, dg_m = _residual_bwd(dx1, sv["y"], p["g_m"][i], "gla_res_bwd")
    dog = _matmul(dy, w_out, tb=True, name="gla_out_dx")
    dw_out = _matmul(sv["og"], dy, ta=True, name="gla_out_dw", out_dtypes=(BF16,))
    d_o, d_r, dgn = _gla_post_bwd(sv["o"], sv["r"], p["gla_norm"], dog, "gla_post_bwd")
    dq, dk, dv, dga = _gla_bwd(sv["proj"], sv["a_tail"], p["gla_wg_p"], p["gla_b_gate"], sv["states"], d_o,
                               "gla_chunks_bwd")
    da_tail = _matmul(dga, p["gla_wg_p"], tb=True, name="gla_gate_dx", out_dtypes=(BF16,))
    dwg = _matmul(sv["a_tail"], dga, ta=True, name="gla_gate_dw")
    dbg = _rowwise(lambda t: (_colsum(t),), [("row", dga)], [("acc", dga.shape[1], F32)], name="gla_gate_db")[0]
    dproj = jnp.concatenate([dq, dk, dv, d_r], axis=1)
    dh_tail = _matmul(da_tail, w_tail, tb=True, name="gla_in_tail_dx")
    dh1 = _matmul(dproj, w_main, tb=True, name="gla_in_dx", tk=1024,
                  epilogue=lambda acc, t: (acc + t,), extras=(("mn", dh_tail),))
    dw_main = _matmul(sv["h1"], dproj, ta=True, name="gla_in_dw", out_dtypes=(BF16,))
    dw_tail = _matmul(sv["h1"], da_tail, ta=True, name="gla_in_tail_dw", out_dtypes=(BF16,))
    rank = p["gla_rank"]
    dw_in = jnp.concatenate([dw_main, dw_tail[:, :rank]], axis=1)
    tok = p["send"]("gla", [_cols_pieces(dw_in), dw_out.reshape(N_DEV, -1, d)])
    dx, dsh, dsc, dgain = _adaln_bwd(x, dh1, dx1, p["norm_mix"][i] + tok, p["sc_m"][i], "gla_norm_bwd")
    grads = dict(gla_w_gate=dwg[:rank], gla_b_gate=dbg, gla_norm=dgn, norm_mix=dgain, sh_m=dsh, sc_m=dsc, g_m=dg_m)
    return dx, grads


def _fox_layer_fwd(x, p, i):
    d = x.shape[1]
    hd = p["fox_q_norm"].shape[1]
    heads = d // hd
    s = x.shape[0]
    t = _tile(s, 256)
    h1 = _adaln_fwd(x, p["norm_mix"][i], p["sc_m"][i], p["sh_m"][i], "fox_norm")
    w_main, w_tail = p["fetch"]("fox_in", h1)
    proj = _matmul(h1, w_main, name="fox_in")
    fl = _matmul(h1, w_tail, name="fox_in_tail")
    q, k, v, og = (proj[:, j * d:(j + 1) * d] for j in range(4))
    qn, kn, vb = _fox_prep(q, k, v, p["fox_q_norm"], p["fox_k_norm"], hd, "fox_prep")
    cum = _fox_cum(fl, p["fox_bf_p"], "fox_cum")
    cum_t = jnp.transpose(cum[:, :heads])
    cum_col, cum_row = cum_t[:, :, None], cum_t.reshape(heads, s // t, 1, t)
    o, lse = _fox_attn_fwd(qn, kn, vb, cum_col, cum_row, hd, t, "fox_attn")
    act = _fox_gate_fwd(o, og, "fox_gate")
    x1, y = _matmul(act, p["fetch"]("fox_out", act), name="fox_out", out_dtypes=(F32, F32),
                    epilogue=lambda acc, x, g: (x + (1.0 + g) * acc, acc), extras=(("mn", x), ("n", p["g_m"][i])))
    return x1, dict(h1=h1, q=q, k=k, og=og, fl=fl, qn=qn, kn=kn, vb=vb, cum_col=cum_col, cum_row=cum_row,
                    o=o, lse=lse, act=act, y=y, t=t, hd=hd)


def _fox_layer_bwd(dx1, x, sv, p, i):
    d = x.shape[1]
    hd, t = sv["hd"], sv["t"]
    heads = d // hd
    s = x.shape[0]
    (w_main, w_tail), w_out = p["fetch"]("fox_in", None), p["fetch"]("fox_out", None)
    dy, dg_m = _residual_bwd(dx1, sv["y"], p["g_m"][i], "fox_res_bwd")
    dact = _matmul(dy, w_out, tb=True, name="fox_out_dx")
    dw_out = _matmul(sv["act"], dy, ta=True, name="fox_out_dw", out_dtypes=(BF16,))
    d_o, d_og = _fox_gate_bwd(sv["o"], sv["og"], dact, "fox_gate_bwd")
    dqn, dkn, dvb, dcq, dck = _fox_attn_bwd(sv["qn"], sv["kn"], sv["vb"], d_o, sv["o"], sv["lse"], sv["cum_col"],
                                            sv["cum_row"], hd, t, "fox_attn_bwd")
    dq, dk, gq, gk = _fox_prep_bwd(sv["q"], sv["k"], dqn, dkn, p["fox_q_norm"], p["fox_k_norm"], hd, "fox_prep_bwd")
    dcum = _pad_cols(jnp.transpose(dcq[:, :, 0] - dck.reshape(heads, s)))
    dfl, dbf = _fox_cum_bwd(dcum, sv["fl"], p["fox_bf_p"], "fox_cum_bwd")
    dfl_b = dfl.astype(BF16)
    dproj = jnp.concatenate([dq, dk, dvb, d_og], axis=1)
    dh_tail = _matmul(dfl_b, w_tail, tb=True, name="fox_in_tail_dx")
    dh1 = _matmul(dproj, w_main, tb=True, name="fox_in_dx", tk=1024,
                  epilogue=lambda acc, tl: (acc + tl,), extras=(("mn", dh_tail),))
    dw_main = _matmul(sv["h1"], dproj, ta=True, name="fox_in_dw", out_dtypes=(BF16,))
    dw_tail = _matmul(sv["h1"], dfl_b, ta=True, name="fox_in_tail_dw", out_dtypes=(BF16,))
    dw_in = jnp.concatenate([dw_main, dw_tail[:, :heads]], axis=1)
    tok = p["send"]("fox", [_cols_pieces(dw_in), dw_out.reshape(N_DEV, -1, d)])
    dx, dsh, dsc, dgain = _adaln_bwd(x, dh1, dx1, p["norm_mix"][i] + tok, p["sc_m"][i], "fox_norm_bwd")
    grads = dict(fox_b_f=dbf[:, :heads], fox_q_norm=gq.reshape(heads, hd).sum(0, keepdims=True),
                 fox_k_norm=gk.reshape(heads, hd).sum(0, keepdims=True), norm_mix=dgain, sh_m=dsh, sc_m=dsc, g_m=dg_m)
    return dx, grads


SMALL = ("b_mod", "norm_mix", "norm_ffn", "gla_b_gate", "gla_norm", "fox_b_f", "fox_q_norm", "fox_k_norm",
         "ffn_conv_b", "norm_final")
SMALL_SHARDED = ("gla_w_gate", "ffn_conv_w")
BIG = ("gla_w_in", "gla_w_out", "fox_w_in", "fox_w_out", "ffn_w_up", "ffn_w_down")
WEIGHTS = ("w_mod", "b_mod", "norm_mix", "norm_ffn", "gla_w_in", "gla_w_gate", "gla_b_gate", "gla_norm", "gla_w_out",
           "fox_w_in", "fox_b_f", "fox_q_norm", "fox_k_norm", "fox_w_out", "ffn_w_up", "ffn_conv_w", "ffn_conv_b",
           "ffn_w_down", "norm_final")


def _pack(parts):
    flat = jnp.concatenate([p.reshape(-1) for p in parts])
    pad = (-flat.shape[0]) % 1024
    return jnp.pad(flat, (0, pad)).reshape(1, -1)


def _unpack(flat, shapes):
    out, off = [], 0
    for shp in shapes:
        n = 1
        for s in shp:
            n *= s
        out.append(flat[0, off:off + n].reshape(shp))
        off += n
    return out


def kernel(x, c, w_mod, b_mod, norm_mix, norm_ffn, gla_w_in, gla_w_gate, gla_b_gate, gla_norm, gla_w_out, fox_w_in, fox_b_f, fox_q_norm, fox_k_norm, fox_w_out, ffn_w_up, ffn_conv_w, ffn_conv_b, ffn_w_down, norm_final, loss_target, m_w_mod, m_b_mod, m_norm_mix, m_norm_ffn, m_gla_w_in, m_gla_w_gate, m_gla_b_gate, m_gla_norm, m_gla_w_out, m_fox_w_in, m_fox_b_f, m_fox_q_norm, m_fox_k_norm, m_fox_w_out, m_ffn_w_up, m_ffn_conv_w, m_ffn_conv_b, m_ffn_w_down, m_norm_final, v_w_mod, v_b_mod, v_norm_mix, v_norm_ffn, v_gla_w_in, v_gla_w_gate, v_gla_b_gate, v_gla_norm, v_gla_w_out, v_fox_w_in, v_fox_b_f, v_fox_q_norm, v_fox_k_norm, v_fox_w_out, v_ffn_w_up, v_ffn_conv_w, v_ffn_conv_b, v_ffn_w_down, v_norm_final):
    w = dict(w_mod=w_mod, b_mod=b_mod, norm_mix=norm_mix, norm_ffn=norm_ffn, gla_w_in=gla_w_in, gla_w_gate=gla_w_gate,
             gla_b_gate=gla_b_gate, gla_norm=gla_norm, gla_w_out=gla_w_out, fox_w_in=fox_w_in, fox_b_f=fox_b_f,
             fox_q_norm=fox_q_norm, fox_k_norm=fox_k_norm, fox_w_out=fox_w_out, ffn_w_up=ffn_w_up,
             ffn_conv_w=ffn_conv_w, ffn_conv_b=ffn_conv_b, ffn_w_down=ffn_w_down, norm_final=norm_final)
    mom_m = dict(w_mod=m_w_mod, b_mod=m_b_mod, norm_mix=m_norm_mix, norm_ffn=m_norm_ffn, gla_w_in=m_gla_w_in,
                 gla_w_gate=m_gla_w_gate, gla_b_gate=m_gla_b_gate, gla_norm=m_gla_norm, gla_w_out=m_gla_w_out,
                 fox_w_in=m_fox_w_in, fox_b_f=m_fox_b_f, fox_q_norm=m_fox_q_norm, fox_k_norm=m_fox_k_norm,
                 fox_w_out=m_fox_w_out, ffn_w_up=m_ffn_w_up, ffn_conv_w=m_ffn_conv_w, ffn_conv_b=m_ffn_conv_b,
                 ffn_w_down=m_ffn_w_down, norm_final=m_norm_final)
    mom_v = dict(w_mod=v_w_mod, b_mod=v_b_mod, norm_mix=v_norm_mix, norm_ffn=v_norm_ffn, gla_w_in=v_gla_w_in,
                 gla_w_gate=v_gla_w_gate, gla_b_gate=v_gla_b_gate, gla_norm=v_gla_norm, gla_w_out=v_gla_w_out,
                 fox_w_in=v_fox_w_in, fox_b_f=v_fox_b_f, fox_q_norm=v_fox_q_norm, fox_k_norm=v_fox_k_norm,
                 fox_w_out=v_fox_w_out, ffn_w_up=v_ffn_w_up, ffn_conv_w=v_ffn_conv_w, ffn_conv_b=v_ffn_conv_b,
                 ffn_w_down=v_ffn_w_down, norm_final=v_norm_final)

    me = 4 * lax.axis_index("x") + 2 * lax.axis_index("y") + lax.axis_index("c")
    xs, target = x[0], loss_target[0]
    s, d = xs.shape
    depth = w_mod.shape[0]
    mod_cols = w_mod.shape[2]
    rank = gla_w_gate.shape[1]
    hd = fox_q_norm.shape[1]
    fox_heads = d // hd
    dk_total = gla_w_gate.shape[2] * N_DEV

    cond = c * (1.0 / (1.0 + jnp.exp(-c)))
    g, _ = _exchange([gla_w_gate[0], ffn_conv_w, cond], "gather_small", scatter=False)
    cond_all = g[2][:, 0, :]

    cond_pad = jnp.pad(cond_all, ((0, 16 - N_DEV), (0, 0)))
    mod_part = []
    for i in range(depth):
        b_cols = lax.dynamic_slice(b_mod[i:i + 1], (0, me * mod_cols), (1, mod_cols))
        mod_part.append(_matmul(cond_pad, w_mod[i], name=f"mod_{i}", tn=768,
                                epilogue=lambda acc, b: (acc + b,), extras=(("n", b_cols),))[:N_DEV])
    (mod_all,), tok_mod = _exchange([jnp.stack(mod_part)], "gather_mod", scatter=False)
    mod = lax.dynamic_index_in_dim(mod_all, me, axis=2, keepdims=False)
    mod = jnp.transpose(mod, (1, 0, 2)).reshape(depth, 6, 1, d)

    big_names = ["gla_in", "gla_out", "up0", "down0", "fox_in", "fox_out", "up1", "down1"]
    big_shards = [gla_w_in[0] + tok_mod, gla_w_out[0], ffn_w_up[0], ffn_w_down[0], fox_w_in[0], fox_w_out[0],
                  ffn_w_up[1], ffn_w_down[1]]
    big_shards = [t.astype(BF16) for t in big_shards]
    handles, tok0 = _exchange_start(big_shards, "gather_weights_start", scatter=False)
    in_flight = dict(zip(big_names, zip(handles, big_shards)))
    ready = {}

    def split_tail(full, tail):
        main = full.shape[1] - tail
        return full[:, :main], _pad_cols(full[:, main:])

    def fetch(key, after):
        if key not in ready:
            handle, mine = in_flight[key]
            land = _exchange_wait([handle], after, f"gather_{key}_wait", scatter=False)[0]
            full = _with_own_block(land, mine, me)
            if key == "gla_in":
                ready[key] = split_tail(_cols_full(full), rank)
            elif key == "fox_in":
                ready[key] = split_tail(_cols_full(full), fox_heads)
            elif key.startswith("up"):
                ready[key] = _cols_full(full)
            else:
                ready[key] = full.reshape(-1, d)
        return ready[key]

    sent = {}

    def send(key, pieces):
        hs, tok = _exchange_start(pieces, f"scatter_{key}_start", scatter=True)
        sent[key] = (hs, pieces)
        return tok

    p = dict(
        fetch=fetch, send=send,
        gla_wg_p=jnp.pad(_cols_full(g[0]), ((0, LANE - rank), (0, 0))),
        conv_w=[jnp.transpose(g[1][:, i], (1, 0, 2)).reshape(ffn_conv_w.shape[1], -1) for i in range(depth)],
        conv_b=[ffn_conv_b[i:i + 1] for i in range(depth)],
        gla_b_gate=gla_b_gate, gla_norm=gla_norm, fox_q_norm=fox_q_norm, fox_k_norm=fox_k_norm,
        fox_bf_p=_pad_cols(fox_b_f), gla_rank=rank,
        norm_mix=[norm_mix[i:i + 1] + (tok0 if i == 0 else 0.0) for i in range(depth)],
        norm_ffn=[norm_ffn[i:i + 1] for i in range(depth)],
    )

    for j, nm in enumerate(("sh_m", "sc_m", "g_m", "sh_f", "sc_f", "g_f")):
        p[nm] = [mod[i, j] for i in range(depth)]

    acts, saved = [xs], []
    for i in range(depth):
        layer_fwd = _gla_layer_fwd if i % 2 == 0 else _fox_layer_fwd
        x1, sv_mix = layer_fwd(acts[-1], p, i)
        x2, sv_ffn = _ffn_fwd(x1, p, i, str(i))
        saved.append((acts[-1], x1, sv_mix, sv_ffn))
        acts.append(x2)
    dx, d_norm_final, loss_part = _final_loss(acts[-1], target, norm_final.reshape(1, d), "final_loss")

    lg = [None] * depth
    for i in reversed(range(depth)):
        x_in, x1, sv_mix, sv_ffn = saved[i]
        dx, g_ffn = _ffn_bwd(dx, x1, sv_ffn, p, i, str(i))
        layer_bwd = _gla_layer_bwd if i % 2 == 0 else _fox_layer_bwd
        dx, g_mix = layer_bwd(dx, x_in, sv_mix, p, i)
        lg[i] = {**g_ffn, **g_mix}
    grad_x = dx[None]

    gla_l = [i for i in range(depth) if i % 2 == 0]
    fox_l = [i for i in range(depth) if i % 2 == 1]
    small_parts = dict(
        norm_mix=jnp.concatenate([lg[i]["norm_mix"] for i in range(depth)]),
        norm_ffn=jnp.concatenate([lg[i]["norm_ffn"] for i in range(depth)]),
        gla_b_gate=jnp.concatenate([lg[i]["gla_b_gate"] for i in gla_l]),
        gla_norm=jnp.concatenate([lg[i]["gla_norm"] for i in gla_l]),
        fox_b_f=jnp.concatenate([lg[i]["fox_b_f"] for i in fox_l]),
        fox_q_norm=jnp.concatenate([lg[i]["fox_q_norm"] for i in fox_l]),
        fox_k_norm=jnp.concatenate([lg[i]["fox_k_norm"] for i in fox_l]),
        ffn_conv_b=jnp.concatenate([lg[i]["conv_b"] for i in range(depth)]),
        norm_final=d_norm_final,
        gla_w_gate=jnp.stack([lg[i]["gla_w_gate"] for i in gla_l]),
        ffn_conv_w=jnp.stack([lg[i]["conv_w"] for i in range(depth)]),
        loss=loss_part[:, :1],
    )
    order = ("norm_mix", "norm_ffn", "gla_b_gate", "gla_norm", "fox_b_f", "fox_q_norm", "fox_k_norm", "ffn_conv_b",
             "norm_final", "gla_w_gate", "ffn_conv_w", "loss")
    packed = _pack([small_parts[nm] for nm in order])
    dmod = jnp.stack([jnp.concatenate([lg[i][nm] for nm in ("sh_m", "sc_m", "g_m", "sh_f", "sc_f", "g_f")], axis=1)
                      for i in range(depth)])
    (packed_all, dmod_all), _ = _exchange([packed, dmod], "gather_small_grads", scatter=False)
    summed = _unpack(_sum8(packed_all, "sum_small_grads"), [small_parts[nm].shape for nm in order])
    small_g = dict(zip(order, summed))
    loss = small_g["loss"][0, 0]
    dmod_all = dmod_all[:, :, 0, :]

    grads = {}
    cond_t = _pad_cols(jnp.transpose(cond_all)).astype(BF16)
    dmod_cols = lax.dynamic_slice(dmod_all, (0, 0, me * mod_cols), (N_DEV, depth, mod_cols))
    g_w_mod = []
    for i in range(depth):
        rhs = jnp.pad(dmod_cols[:, i], ((0, LANE - N_DEV), (0, 0)))
        g_w_mod.append(_matmul(cond_t, rhs, name=f"mod_dw_{i}", tn=768))
    grads["w_mod"] = jnp.stack(g_w_mod)
    small_g["b_mod"] = _sum8(dmod_all.reshape(N_DEV, 1, -1), "sum_b_mod").reshape(depth, -1)

    received = {}
    for key in ("ffn1", "fox", "ffn0", "gla"):
        hs, pieces = sent[key]
        lands = _exchange_wait(hs, dx, f"scatter_{key}_wait", scatter=True)
        received[key] = [_with_own_block(land, lax.dynamic_index_in_dim(pc, me, 0, keepdims=False), me)
                         for land, pc in zip(lands, pieces)]

    out_g, out_d, out_m, out_v = {}, {}, {}, {}

    def update(nm, g_arr):
        res = _adamw(w[nm], g_arr, mom_m[nm], mom_v[nm], f"adamw_{nm}")
        out_g[nm], out_d[nm], out_m[nm], out_v[nm] = res

    update("gla_w_in", [received["gla"][0]])
    update("gla_w_out", [received["gla"][1]])
    update("fox_w_in", [received["fox"][0]])
    update("fox_w_out", [received["fox"][1]])
    update("ffn_w_up", [received[f"ffn{i}"][0] for i in range(depth)])
    update("ffn_w_down", [received[f"ffn{i}"][1] for i in range(depth)])
    update("w_mod", grads["w_mod"])

    gate_cols = gla_w_gate.shape[2]
    conv_cols = ffn_conv_w.shape[2]
    local_small = dict(small_g)
    local_small["gla_w_gate"] = lax.dynamic_slice_in_dim(small_g["gla_w_gate"], me * gate_cols, gate_cols, axis=2)
    local_small["ffn_conv_w"] = lax.dynamic_slice_in_dim(small_g["ffn_conv_w"], me * conv_cols, conv_cols, axis=2)
    names = SMALL + SMALL_SHARDED
    shapes = [w[nm].shape for nm in names]
    res = _adamw(_pack([w[nm] for nm in names])[None], _pack([local_small[nm] for nm in names])[None],
                 _pack([mom_m[nm] for nm in names])[None], _pack([mom_v[nm] for nm in names])[None], "adamw_small")
    for tgt, flat in zip((out_g, out_d, out_m, out_v), res):
        for nm, arr in zip(names, _unpack(flat[0], shapes)):
            tgt[nm] = arr

    return (loss, grad_x, *[out_g[n] for n in WEIGHTS], *[out_d[n] for n in WEIGHTS],
            *[out_m[n] for n in WEIGHTS], *[out_v[n] for n in WEIGHTS])
```

```python
import jax
import jax.numpy as jnp
from jax import lax
from jax.experimental import pallas as pl
from jax.experimental.pallas import tpu as pltpu

F32, BF16 = jnp.float32, jnp.bfloat16
N_DEV = 8
GLA_HEADS = 4
GLA_TAU = 16.0
GLA_CHUNK = 64
NORM_EPS = 1e-6
ADAM_LR, ADAM_B1, ADAM_B2, ADAM_EPS, ADAM_WD, ADAM_STEP = 0.001, 0.9, 0.999, 1e-08, 0.01, 10
LANE = 128
VMEM_LIMIT = 56 * 1024 * 1024
NEG = -1e30


def _pcall(body, **kw):
    return pl.pallas_call(body, **kw)


def _params(n_axes):
    return pltpu.CompilerParams(dimension_semantics=("arbitrary",) * n_axes, vmem_limit_bytes=VMEM_LIMIT)


def _tile(dim, pref):
    if dim <= pref:
        return dim
    t = pref
    while dim % t:
        t -= LANE
    assert t > 0, (dim, pref)
    return t


def _dot(a, b, ta=False, tb=False):
    dims = (((0,) if ta else (1,), (1,) if tb else (0,)), ((), ()))
    return lax.dot_general(a.astype(BF16), b.astype(BF16), dims, preferred_element_type=F32)


def _split3(x):
    hi = x.astype(BF16)
    r1 = x - hi.astype(F32)
    mid = r1.astype(BF16)
    lo = (r1 - mid.astype(F32)).astype(BF16)
    return hi, mid, lo


def _tri_matmul(tri, x):
    hi, mid, lo = _split3(x)
    return _dot(tri, hi) + _dot(tri, mid) + _dot(tri, lo)


def _tri(n, upper=False):
    r = lax.broadcasted_iota(jnp.int32, (n, n), 0)
    c = lax.broadcasted_iota(jnp.int32, (n, n), 1)
    return jnp.where((r <= c) if upper else (r >= c), 1.0, 0.0).astype(BF16)


def _log_sigmoid(x):
    return jnp.minimum(x, 0.0) - jnp.log(1.0 + jnp.exp(-jnp.abs(x)))


def _sigmoid(x):
    return 1.0 / (1.0 + jnp.exp(-x))


def _silu(x):
    return x * _sigmoid(x)


def _dsilu(x):
    s = _sigmoid(x)
    return s * (1.0 + x * (1.0 - s))


def _matmul(a, b, *, name, ta=False, tb=False, out_dtypes=(F32,), tm=1024, tn=1024, tk=2048,
            epilogue=None, extras=(), a_halves=False, b_halves=False, b_shards=False, out_shards=False):
    if a_halves:
        assert not ta
        m, k = a.shape[1], 2 * a.shape[2]
    else:
        m, k = (a.shape[1], a.shape[0]) if ta else a.shape
    if b_halves:
        assert not tb and b.shape[1] == k
        n = 2 * b.shape[2]
    elif b_shards:
        n = b.shape[1] if tb else N_DEV * b.shape[2]
        assert (N_DEV * b.shape[2] if tb else b.shape[1]) == k, (a.shape, b.shape, ta, tb)
    else:
        n = b.shape[0] if tb else b.shape[1]
        assert (b.shape[1] if tb else b.shape[0]) == k, (a.shape, b.shape, ta, tb)
    n_unit = n // N_DEV if (out_shards or (b_shards and not tb)) else (n // 2 if b_halves else n)
    k_unit = k // N_DEV if (b_shards and tb) else (k // 2 if a_halves else k)
    tm, tn, tk = _tile(m, tm), _tile(n_unit, tn), _tile(k_unit, tk)
    nk = k // tk
    if a_halves:
        a_spec = pl.BlockSpec((None, tm, tk), lambda i, j, kk: (kk // (nk // 2), i, kk % (nk // 2)))
    elif ta:
        a_spec = pl.BlockSpec((tk, tm), lambda i, j, kk: (kk, i))
    else:
        a_spec = pl.BlockSpec((tm, tk), lambda i, j, kk: (i, kk))
    n_per, k_per = n // tn // N_DEV, nk // N_DEV
    if b_halves:
        b_spec = pl.BlockSpec((None, tk, tn), lambda i, j, kk: (j // (n // tn // 2), kk, j % (n // tn // 2)))
    elif b_shards and tb:
        b_spec = pl.BlockSpec((None, tn, tk), lambda i, j, kk: (kk // k_per, j, kk % k_per))
    elif b_shards:
        b_spec = pl.BlockSpec((None, tk, tn), lambda i, j, kk: (j // n_per, kk, j % n_per))
    elif tb:
        b_spec = pl.BlockSpec((tn, tk), lambda i, j, kk: (j, kk))
    else:
        b_spec = pl.BlockSpec((tk, tn), lambda i, j, kk: (kk, j))
    ex_specs = []
    for kind, arr in extras:
        if kind == "mn":
            assert arr.shape == (m, n), (arr.shape, m, n)
            ex_specs.append(pl.BlockSpec((tm, tn), lambda i, j, kk: (i, j)))
        else:
            assert arr.shape == (1, n), (arr.shape, n)
            ex_specs.append(pl.BlockSpec((1, tn), lambda i, j, kk: (0, j)))
    n_ex, n_out = len(extras), len(out_dtypes)

    def body(a_ref, b_ref, *rest):
        ex, outs, acc = rest[:n_ex], rest[n_ex:n_ex + n_out], rest[-1]
        kk = pl.program_id(2)

        @pl.when(kk == 0)
        def _():
            acc[...] = jnp.zeros_like(acc)

        acc[...] += _dot(a_ref[...], b_ref[...], ta, tb)

        @pl.when(kk == nk - 1)
        def _():
            if epilogue is None:
                vals = (acc[...],)
            else:
                vals = epilogue(acc[...], *[e[...] for e in ex])
            for o, v in zip(outs, vals):
                o[...] = v.astype(o.dtype)

    if out_shards:
        out_spec = pl.BlockSpec((None, tm, tn), lambda i, j, kk: (j // n_per, i, j % n_per))
        out_dims = (N_DEV, m, n // N_DEV)
    else:
        out_spec = pl.BlockSpec((tm, tn), lambda i, j, kk: (i, j))
        out_dims = (m, n)
    res = _pcall(
        body, name=name, grid=(m // tm, n // tn, nk),
        in_specs=[a_spec, b_spec] + ex_specs,
        out_specs=[out_spec] * n_out,
        out_shape=[jax.ShapeDtypeStruct(out_dims, d) for d in out_dtypes],
        scratch_shapes=[pltpu.VMEM((tm, tn), F32)],
        compiler_params=_params(3),
    )(a, b, *[arr for _, arr in extras])
    return res[0] if n_out == 1 else res


def _rowwise(fn, ins, outs, *, name, tr=128):
    rows = next(e[1].shape[0] for e in ins if e[0] != "full")
    tr = _tile(rows, tr)
    in_specs = []
    for entry in ins:
        kind, arr = entry[0], entry[1]
        assert kind == "full" or (arr.shape[0] == rows and arr.ndim == 2)
        if kind == "row":
            in_specs.append(pl.BlockSpec((tr, arr.shape[1]), lambda i: (i, 0)))
        elif kind == "cols":
            in_specs.append(pl.BlockSpec((tr, entry[3]), lambda i, cb=entry[2]: (i, cb)))
        else:
            in_specs.append(pl.BlockSpec(arr.shape, lambda i, nd=arr.ndim: (0,) * nd))
    out_specs, out_shape = [], []
    for kind, w, dt in outs:
        if kind == "row":
            out_specs.append(pl.BlockSpec((tr, w), lambda i: (i, 0)))
            out_shape.append(jax.ShapeDtypeStruct((rows, w), dt))
        else:
            out_specs.append(pl.BlockSpec((1, w), lambda i: (0, 0)))
            out_shape.append(jax.ShapeDtypeStruct((1, w), dt))
    n_in = len(ins)

    def body(*refs):
        i = pl.program_id(0)
        vals = fn(*[r[...] for r in refs[:n_in]])
        for (kind, _, _), o, v in zip(outs, refs[n_in:], vals):
            if kind == "row":
                o[...] = v.astype(o.dtype)
            else:
                @pl.when(i == 0)
                def _(o=o):
                    o[...] = jnp.zeros_like(o)

                o[...] += v.astype(o.dtype)

    return _pcall(body, name=name, grid=(rows // tr,), in_specs=in_specs, out_specs=out_specs,
                  out_shape=out_shape, compiler_params=_params(1))(*[e[1] for e in ins])


def _colsum(x):
    return jnp.sum(x, axis=0, keepdims=True)


def _norm_stats(x):
    rstd = lax.rsqrt(jnp.mean(x * x, axis=-1, keepdims=True) + NORM_EPS)
    return x * rstd, rstd


def _norm_bwd(dxhat, xhat, rstd):
    return rstd * (dxhat - xhat * jnp.mean(dxhat * xhat, axis=-1, keepdims=True))


def _adaln_fwd(x, gain, sc, sh, name):
    def fn(x, gain, sc, sh):
        xhat, _ = _norm_stats(x)
        return ((xhat * gain) * (1.0 + sc) + sh,)

    return _rowwise(fn, [("row", x), ("full", gain), ("full", sc), ("full", sh)],
                    [("row", x.shape[1], BF16)], name=name)[0]


def _adaln_bwd(x, dh, dres, gain, sc, name):
    d = x.shape[1]

    def fn(x, dh, dres, gain, sc):
        xhat, rstd = _norm_stats(x)
        dxhat = dh * (gain * (1.0 + sc))
        dx = dres + _norm_bwd(dxhat, xhat, rstd)
        return dx, _colsum(dh), _colsum(dh * (xhat * gain)), _colsum(dh * xhat * (1.0 + sc))

    return _rowwise(fn, [("row", x), ("row", dh), ("row", dres), ("full", gain), ("full", sc)],
                    [("row", d, F32), ("acc", d, F32), ("acc", d, F32), ("acc", d, F32)], name=name)


def _residual_bwd(dx, y, g, name):
    d = dx.shape[1]

    def fn(dx, y, g):
        return dx * (1.0 + g), _colsum(dx * y)

    return _rowwise(fn, [("row", dx), ("row", y), ("full", g)], [("row", d, BF16), ("acc", d, F32)], name=name)


def _final_loss(x, target, gain, name):
    d = x.shape[1]

    def fn(x, t, gain):
        xhat, rstd = _norm_stats(x)
        err = xhat * gain - t
        dy = err * (1.0 / d)
        loss = 0.5 * jnp.sum(jnp.mean(err * err, axis=-1, keepdims=True), axis=0, keepdims=True)
        dx = _norm_bwd(dy * gain, xhat, rstd)
        return dx, _colsum(dy * xhat), jnp.broadcast_to(loss, (1, LANE))

    return _rowwise(fn, [("row", x), ("row", target), ("full", gain)],
                    [("row", d, F32), ("acc", d, F32), ("acc", LANE, F32)], name=name)


def _gla_gates(q_ref, k_ref, a_ref, wg_ref, bg_ref, scale, c):
    ga = _dot(a_ref[...], wg_ref[...]) + bg_ref[...]
    la = _log_sigmoid(ga) * (1.0 / GLA_TAU)
    b = _tri_matmul(_tri(c), la)
    bl = _colsum(la)
    eb, enb, eend = jnp.exp(b), jnp.exp(-b), jnp.exp(bl - b)
    q = q_ref[...] * scale
    k = k_ref[...]
    return dict(ga=ga, eb=eb, enb=enb, eend=eend, dec=jnp.exp(bl), q_dec=q * eb, k_inv=k * enb, k_end=k * eend)


def _causal(c):
    return lax.broadcasted_iota(jnp.int32, (c, c), 0) >= lax.broadcasted_iota(jnp.int32, (c, c), 1)


def _gla_specs(heads, c, dk, dv, rev, n_chunks):
    def ch(n):
        return (n_chunks - 1 - n) if rev else n

    return [
        pl.BlockSpec((c, dk), lambda h, n: (ch(n), h)),
        pl.BlockSpec((c, dk), lambda h, n: (ch(n), heads + h)),
        pl.BlockSpec((c, dv), lambda h, n: (ch(n), heads + h)),
        pl.BlockSpec((c, LANE), lambda h, n: (ch(n), 0)),
        pl.BlockSpec((LANE, dk), lambda h, n: (0, h)),
        pl.BlockSpec((1, dk), lambda h, n: (0, h)),
    ]


def _gla_fwd(proj, a_tail, wg_p, bg, name):
    s = proj.shape[0]
    heads, c = GLA_HEADS, GLA_CHUNK
    dk = wg_p.shape[1] // heads
    dv = 2 * dk
    n_chunks = s // c
    scale = dk ** -0.5

    def body(q_ref, k_ref, v_ref, a_ref, wg_ref, bg_ref, o_ref, st_ref, state):
        @pl.when(pl.program_id(1) == 0)
        def _():
            state[...] = jnp.zeros_like(state)

        g = _gla_gates(q_ref, k_ref, a_ref, wg_ref, bg_ref, scale, c)
        v = v_ref[...]
        st = state[...]
        attn = jnp.where(_causal(c), _dot(g["q_dec"], g["k_inv"], tb=True), 0.0)
        o_ref[...] = _dot(attn, v) + _dot(g["q_dec"], st, tb=True)
        st_ref[...] = st.astype(st_ref.dtype)
        state[...] = g["dec"] * st + _dot(v, g["k_end"], ta=True)

    return _pcall(
        body, name=name, grid=(heads, n_chunks),
        in_specs=_gla_specs(heads, c, dk, dv, False, n_chunks),
        out_specs=[pl.BlockSpec((c, dv), lambda h, n: (n, h)),
                   pl.BlockSpec((None, None, dv, dk), lambda h, n: (h, n, 0, 0))],
        out_shape=[jax.ShapeDtypeStruct((s, heads * dv), F32),
                   jax.ShapeDtypeStruct((heads, n_chunks, dv, dk), BF16)],
        scratch_shapes=[pltpu.VMEM((dv, dk), F32)],
        compiler_params=_params(2),
    )(proj, proj, proj, a_tail, wg_p, bg)


def _gla_bwd(proj, a_tail, wg_p, bg, states, d_o, name):
    s = proj.shape[0]
    heads, c = GLA_HEADS, GLA_CHUNK
    dk = wg_p.shape[1] // heads
    dv = 2 * dk
    n_chunks = s // c
    scale = dk ** -0.5

    def body(q_ref, k_ref, v_ref, a_ref, wg_ref, bg_ref, st_ref, do_ref, dq_ref, dk_ref, dv_ref, dga_ref, dstate):
        @pl.when(pl.program_id(1) == 0)
        def _():
            dstate[...] = jnp.zeros_like(dstate)

        g = _gla_gates(q_ref, k_ref, a_ref, wg_ref, bg_ref, scale, c)
        v, st, dst, d_out = v_ref[...], st_ref[...], dstate[...], do_ref[...]
        q_dec, k_inv, k_end = g["q_dec"], g["k_inv"], g["k_end"]
        mask = _causal(c)
        attn = jnp.where(mask, _dot(q_dec, k_inv, tb=True), 0.0)
        d_attn = jnp.where(mask, _dot(d_out, v, tb=True), 0.0)
        d_qdec = _dot(d_attn, k_inv) + _dot(d_out, st)
        d_kinv = _dot(d_attn, q_dec, ta=True)
        d_kend = _dot(v, dst)
        dv_ref[...] = (_dot(attn, d_out, ta=True) + _dot(k_end, dst, tb=True)).astype(dv_ref.dtype)
        d_dec = jnp.sum(dst * st.astype(F32), axis=0, keepdims=True)
        dstate[...] = g["dec"] * dst + _dot(d_out, q_dec, ta=True)

        dq_ref[...] = (d_qdec * (scale * g["eb"])).astype(dq_ref.dtype)
        dk_ref[...] = (d_kinv * g["enb"] + d_kend * g["eend"]).astype(dk_ref.dtype)
        kk = d_kend * k_end
        db = d_qdec * q_dec - d_kinv * k_inv - kk
        dbl = jnp.sum(kk, axis=0, keepdims=True) + d_dec * g["dec"]
        last = lax.broadcasted_iota(jnp.int32, db.shape, 0) == c - 1
        db = db + jnp.where(last, dbl, 0.0)
        dla = _tri_matmul(_tri(c, upper=True), db)
        dga_ref[...] = dla * (1.0 / GLA_TAU) * _sigmoid(-g["ga"])

    rev = lambda h, n: (n_chunks - 1 - n, h)
    return _pcall(
        body, name=name, grid=(heads, n_chunks),
        in_specs=_gla_specs(heads, c, dk, dv, True, n_chunks) + [
            pl.BlockSpec((None, None, dv, dk), lambda h, n: (h, n_chunks - 1 - n, 0, 0)),
            pl.BlockSpec((c, dv), rev)],
        out_specs=[pl.BlockSpec((c, dk), rev), pl.BlockSpec((c, dk), rev), pl.BlockSpec((c, dv), rev),
                   pl.BlockSpec((c, dk), rev)],
        out_shape=[jax.ShapeDtypeStruct((s, heads * dk), BF16), jax.ShapeDtypeStruct((s, heads * dk), BF16),
                   jax.ShapeDtypeStruct((s, heads * dv), BF16), jax.ShapeDtypeStruct((s, heads * dk), F32)],
        scratch_shapes=[pltpu.VMEM((dv, dk), F32)],
        compiler_params=_params(2),
    )(proj, proj, proj, a_tail, wg_p, bg, states, d_o)


def _gla_post_fwd(o, r, gn, name):
    dvt = o.shape[1]
    dv = dvt // GLA_HEADS

    def fn(o, r, gn):
        outs = []
        for h in range(GLA_HEADS):
            sl = slice(h * dv, (h + 1) * dv)
            ohat, _ = _norm_stats(o[:, sl])
            outs.append((ohat * gn[:, sl]) * _silu(r[:, sl]))
        return (jnp.concatenate(outs, axis=1),)

    return _rowwise(fn, [("row", o), r, ("full", gn)], [("row", dvt, BF16)], name=name)[0]


def _gla_post_bwd(o, r, gn, dog, name):
    dvt = o.shape[1]
    dv = dvt // GLA_HEADS

    def fn(o, r, gn, dog):
        d_o, d_r, d_g = [], [], []
        for h in range(GLA_HEADS):
            sl = slice(h * dv, (h + 1) * dv)
            ohat, rstd = _norm_stats(o[:, sl])
            g, rr, dd = gn[:, sl], r[:, sl], dog[:, sl]
            d_r.append(dd * (ohat * g) * _dsilu(rr))
            don = dd * _silu(rr)
            d_g.append(_colsum(don * ohat))
            d_o.append(_norm_bwd(don * g, ohat, rstd))
        return jnp.concatenate(d_o, axis=1), jnp.concatenate(d_r, axis=1), jnp.concatenate(d_g, axis=1)

    return _rowwise(fn, [("row", o), r, ("full", gn), ("row", dog)],
                    [("row", dvt, F32), ("row", dvt, BF16), ("acc", dvt, F32)], name=name)


def _fox_prep(q, k, v, qg, kg, d, hd, name):
    heads = d // hd
    scale = hd ** -0.5

    def fn(q, k, v, qg, kg):
        qs, ks = [], []
        for h in range(heads):
            sl = slice(h * hd, (h + 1) * hd)
            qs.append(_norm_stats(q[:, sl])[0] * qg * scale)
            ks.append(_norm_stats(k[:, sl])[0] * kg)
        return jnp.concatenate(qs, axis=1), jnp.concatenate(ks, axis=1), v

    return _rowwise(fn, [q, k, v, ("full", qg), ("full", kg)],
                    [("row", d, BF16)] * 3, name=name)


def _fox_prep_bwd(q, k, dqn, dkn, qg, kg, hd, name):
    d = dqn.shape[1]
    heads = d // hd
    scale = hd ** -0.5

    def fn(q, k, dqn, dkn, qg, kg):
        dq, dk, gq, gk = [], [], [], []
        for h in range(heads):
            sl = slice(h * hd, (h + 1) * hd)
            for x, dxn, g, s, dl, gl in ((q, dqn, qg, scale, dq, gq), (k, dkn, kg, 1.0, dk, gk)):
                xhat, rstd = _norm_stats(x[:, sl])
                dn = dxn[:, sl] * s
                gl.append(_colsum(dn * xhat))
                dl.append(_norm_bwd(dn * g, xhat, rstd))
        cat = lambda t: jnp.concatenate(t, axis=1)
        return cat(dq), cat(dk), cat(gq), cat(gk)

    return _rowwise(fn, [q, k, ("row", dqn), ("row", dkn), ("full", qg), ("full", kg)],
                    [("row", d, BF16), ("row", d, BF16), ("acc", d, F32), ("acc", d, F32)], name=name)


def _fox_cum(fl, bf_p, name, tb=256):
    s = fl.shape[0]
    tb = _tile(s, tb)

    def body(fl_ref, bf_ref, cum_ref, carry):
        @pl.when(pl.program_id(0) == 0)
        def _():
            carry[...] = jnp.zeros_like(carry)

        lf = _log_sigmoid(fl_ref[...] + bf_ref[...])
        cum_ref[...] = _tri_matmul(_tri(tb), lf) + carry[...]
        carry[...] += _colsum(lf)

    return _pcall(
        body, name=name, grid=(s // tb,),
        in_specs=[pl.BlockSpec((tb, LANE), lambda i: (i, 0)), pl.BlockSpec((1, LANE), lambda i: (0, 0))],
        out_specs=pl.BlockSpec((tb, LANE), lambda i: (i, 0)),
        out_shape=jax.ShapeDtypeStruct((s, LANE), F32),
        scratch_shapes=[pltpu.VMEM((1, LANE), F32)],
        compiler_params=_params(1),
    )(fl, bf_p)


def _fox_cum_bwd(dcum, fl, bf_p, name, tb=256):
    s = fl.shape[0]
    tb = _tile(s, tb)
    nb = s // tb

    def body(dc_ref, fl_ref, bf_ref, dfl_ref, dbf_ref, carry):
        @pl.when(pl.program_id(0) == 0)
        def _():
            carry[...] = jnp.zeros_like(carry)
            dbf_ref[...] = jnp.zeros_like(dbf_ref)

        dc = dc_ref[...]
        dlf = _tri_matmul(_tri(tb, upper=True), dc) + carry[...]
        carry[...] += _colsum(dc)
        dfl = dlf * _sigmoid(-(fl_ref[...] + bf_ref[...]))
        dfl_ref[...] = dfl
        dbf_ref[...] += _colsum(dfl)

    rev = lambda i: (nb - 1 - i, 0)
    return _pcall(
        body, name=name, grid=(nb,),
        in_specs=[pl.BlockSpec((tb, LANE), rev), pl.BlockSpec((tb, LANE), rev), pl.BlockSpec((1, LANE), lambda i: (0, 0))],
        out_specs=[pl.BlockSpec((tb, LANE), rev), pl.BlockSpec((1, LANE), lambda i: (0, 0))],
        out_shape=[jax.ShapeDtypeStruct((s, LANE), F32), jax.ShapeDtypeStruct((1, LANE), F32)],
        scratch_shapes=[pltpu.VMEM((1, LANE), F32)],
        compiler_params=_params(1),
    )(dcum, fl, bf_p)


def _fox_attn_fwd(qn, kn, vb, cum_col, cum_row, hd, t, name):
    s, d = qn.shape
    heads = d // hd
    nq = s // t

    def body(q_ref, k_ref, v_ref, cc_ref, cr_ref, o_ref, lse_ref):
        qi = pl.program_id(1)
        q = q_ref[...]
        cq = cc_ref[...]
        qpos = qi * t + lax.broadcasted_iota(jnp.int32, (t, 1), 0)

        def step(kj, carry):
            m, l, acc = carry
            off = pl.multiple_of(kj * t, t)
            ks, vs = k_ref[pl.ds(off, t), :], v_ref[pl.ds(off, t), :]
            sc = _dot(q, ks, tb=True) + cq - cr_ref[kj]
            kpos = off + lax.broadcasted_iota(jnp.int32, (1, t), 1)
            sc = jnp.where(kpos <= qpos, sc, NEG)
            m_new = jnp.maximum(m, jnp.max(sc, axis=1, keepdims=True))
            alpha = jnp.exp(m - m_new)
            p = jnp.exp(sc - m_new)
            return m_new, alpha * l + jnp.sum(p, axis=1, keepdims=True), alpha * acc + _dot(p, vs)

        init = (jnp.full((t, 1), NEG, F32), jnp.zeros((t, 1), F32), jnp.zeros((t, hd), F32))
        m, l, acc = lax.fori_loop(0, qi + 1, step, init)
        o_ref[...] = acc / l
        lse_ref[...] = m + jnp.log(l)

    return _pcall(
        body, name=name, grid=(heads, nq),
        in_specs=[pl.BlockSpec((t, hd), lambda h, i: (i, h)),
                  pl.BlockSpec((s, hd), lambda h, i: (0, h)),
                  pl.BlockSpec((s, hd), lambda h, i: (0, h)),
                  pl.BlockSpec((None, t, 1), lambda h, i: (h, i, 0)),
                  pl.BlockSpec((None, nq, 1, t), lambda h, i: (h, 0, 0, 0))],
        out_specs=[pl.BlockSpec((t, hd), lambda h, i: (i, h)), pl.BlockSpec((None, t, 1), lambda h, i: (h, i, 0))],
        out_shape=[jax.ShapeDtypeStruct((s, d), F32), jax.ShapeDtypeStruct((heads, s, 1), F32)],
        compiler_params=_params(2),
    )(qn, kn, vb, cum_col, cum_row)


def _fox_attn_bwd(qn, kn, vb, d_o, o, lse, cum_col, cum_row, hd, t, name):
    s, d = qn.shape
    heads = d // hd
    nq = s // t

    def body(q_ref, k_ref, v_ref, do_ref, o_ref, lse_ref, cc_ref, cr_ref,
             dq_ref, dk_ref, dv_ref, dcq_ref, dck_ref, delta):
        kj = pl.program_id(1)

        @pl.when(kj == 0)
        def _():
            dq_ref[...] = jnp.zeros_like(dq_ref)
            dcq_ref[...] = jnp.zeros_like(dcq_ref)
            delta[...] = jnp.sum(do_ref[...] * o_ref[...], axis=1, keepdims=True)

        ks, vs, cr = k_ref[...], v_ref[...], cr_ref[...]
        kpos = kj * t + lax.broadcasted_iota(jnp.int32, (1, t), 1)

        def step(qi, carry):
            dk, dv, dck = carry
            rows = pl.ds(pl.multiple_of(qi * t, t), t)
            q, d_out = q_ref[rows, :], do_ref[rows, :]
            sc = _dot(q, ks, tb=True) + cc_ref[rows, :] - cr
            qpos = qi * t + lax.broadcasted_iota(jnp.int32, (t, 1), 0)
            p = jnp.where(kpos <= qpos, jnp.exp(sc - lse_ref[rows, :]), 0.0)
            ds = p * (_dot(d_out, vs, tb=True) - delta[rows, :])
            dq_ref[rows, :] += _dot(ds, ks)
            dcq_ref[rows, :] += jnp.sum(ds, axis=1, keepdims=True)
            return dk + _dot(ds, q, ta=True), dv + _dot(p, d_out, ta=True), dck + _colsum(ds)

        init = (jnp.zeros((t, hd), F32), jnp.zeros((t, hd), F32), jnp.zeros((1, t), F32))
        dk, dv, dck = lax.fori_loop(kj, nq, step, init)
        dk_ref[...] = dk.astype(dk_ref.dtype)
        dv_ref[...] = dv.astype(dv_ref.dtype)
        dck_ref[...] = dck

    head_rows = lambda h, j: (0, h)
    blk = lambda h, j: (j, h)
    return _pcall(
        body, name=name, grid=(heads, nq),
        in_specs=[pl.BlockSpec((s, hd), head_rows), pl.BlockSpec((t, hd), blk), pl.BlockSpec((t, hd), blk),
                  pl.BlockSpec((s, hd), head_rows), pl.BlockSpec((s, hd), head_rows),
                  pl.BlockSpec((None, s, 1), lambda h, j: (h, 0, 0)),
                  pl.BlockSpec((None, s, 1), lambda h, j: (h, 0, 0)),
                  pl.BlockSpec((None, None, 1, t), lambda h, j: (h, j, 0, 0))],
        out_specs=[pl.BlockSpec((s, hd), head_rows), pl.BlockSpec((t, hd), blk), pl.BlockSpec((t, hd), blk),
                   pl.BlockSpec((None, s, 1), lambda h, j: (h, 0, 0)),
                   pl.BlockSpec((None, None, 1, t), lambda h, j: (h, j, 0, 0))],
        out_shape=[jax.ShapeDtypeStruct((s, d), F32), jax.ShapeDtypeStruct((s, d), BF16),
                   jax.ShapeDtypeStruct((s, d), BF16), jax.ShapeDtypeStruct((heads, s, 1), F32),
                   jax.ShapeDtypeStruct((heads, nq, 1, t), F32)],
        scratch_shapes=[pltpu.VMEM((s, 1), F32)],
        compiler_params=_params(2),
    )(qn, kn, vb, d_o, o, lse, cum_col, cum_row)


def _fox_gate_fwd(o, og, name):
    def fn(o, og):
        return (o * _sigmoid(og),)

    return _rowwise(fn, [("row", o), og], [("row", o.shape[1], BF16)], name=name)[0]


def _fox_gate_bwd(o, og, dact, name):
    def fn(o, og, dact):
        sg = _sigmoid(og)
        return dact * sg, dact * o * sg * (1.0 - sg)

    d = o.shape[1]
    return _rowwise(fn, [("row", o), og, ("row", dact)], [("row", d, F32), ("row", d, BF16)], name=name)


def _shift_down(x, n):
    rows = lax.broadcasted_iota(jnp.int32, x.shape, 0)
    return jnp.where(rows >= n, pltpu.roll(x, n, 0), 0.0)


def _shift_up(x, n):
    rows = lax.broadcasted_iota(jnp.int32, x.shape, 0)
    return jnp.where(rows < x.shape[0] - n, pltpu.roll(x, x.shape[0] - n, 0), 0.0)


def _conv(u, w_ref, b):
    return w_ref[0:1, :] * _shift_down(u, 2) + w_ref[1:2, :] * _shift_down(u, 1) + w_ref[2:3, :] * u + b


def _conv_act_fwd(u, cw, cb, name, tc=256):
    s, two_f = u.shape
    dff = two_f // 2
    tc = _tile(dff, tc)
    nb = dff // tc

    def body(ug_ref, uv_ref, wg_ref, wv_ref, bg_ref, bv_ref, a_ref):
        gate = _conv(ug_ref[...], wg_ref, bg_ref[...])
        val = _conv(uv_ref[...], wv_ref, bv_ref[...])
        a_ref[...] = (_silu(gate) * val).astype(a_ref.dtype)

    lo, hi = (lambda j: (0, j)), (lambda j: (0, j + nb))
    return _pcall(
        body, name=name, grid=(nb,),
        in_specs=[pl.BlockSpec((s, tc), lo), pl.BlockSpec((s, tc), hi), pl.BlockSpec((3, tc), lo),
                  pl.BlockSpec((3, tc), hi), pl.BlockSpec((1, tc), lo), pl.BlockSpec((1, tc), hi)],
        out_specs=pl.BlockSpec((s, tc), lo),
        out_shape=jax.ShapeDtypeStruct((s, dff), BF16),
        compiler_params=_params(1),
    )(u, u, cw, cw, cb, cb)


def _conv_act_bwd(u, cw, cb, da, name, tc=128):
    s, two_f = u.shape
    dff = two_f // 2
    tc = _tile(dff, tc)
    nb = dff // tc

    def body(ug_ref, uv_ref, wg_ref, wv_ref, bg_ref, bv_ref, da_ref, du_ref, dw_ref, db_ref):
        ug, uv, da = ug_ref[...], uv_ref[...], da_ref[...]
        gate = _conv(ug, wg_ref, bg_ref[...])
        val = _conv(uv, wv_ref, bv_ref[...])
        sg = _sigmoid(gate)
        d_val = da * (gate * sg)
        d_gate = da * val * (sg * (1.0 + gate * (1.0 - sg)))
        for half, (dc, uu, w_ref) in enumerate(((d_gate, ug, wg_ref), (d_val, uv, wv_ref))):
            du = w_ref[0:1, :] * _shift_up(dc, 2) + w_ref[1:2, :] * _shift_up(dc, 1) + w_ref[2:3, :] * dc
            du_ref[half] = du.astype(du_ref.dtype)
            dw_ref[half, 0:1, :] = _colsum(dc * _shift_down(uu, 2))
            dw_ref[half, 1:2, :] = _colsum(dc * _shift_down(uu, 1))
            dw_ref[half, 2:3, :] = _colsum(dc * uu)
            db_ref[half] = _colsum(dc)

    lo, hi = (lambda j: (0, j)), (lambda j: (0, j + nb))
    both = lambda j: (0, 0, j)
    return _pcall(
        body, name=name, grid=(nb,),
        in_specs=[pl.BlockSpec((s, tc), lo), pl.BlockSpec((s, tc), hi), pl.BlockSpec((3, tc), lo),
                  pl.BlockSpec((3, tc), hi), pl.BlockSpec((1, tc), lo), pl.BlockSpec((1, tc), hi),
                  pl.BlockSpec((s, tc), lo)],
        out_specs=[pl.BlockSpec((2, s, tc), both), pl.BlockSpec((2, 3, tc), both), pl.BlockSpec((2, 1, tc), both)],
        out_shape=[jax.ShapeDtypeStruct((2, s, dff), BF16), jax.ShapeDtypeStruct((2, 3, dff), F32),
                   jax.ShapeDtypeStruct((2, 1, dff), F32)],
        compiler_params=_params(1),
    )(u, u, cw, cw, cb, cb, da)


def _adamw_math(w, g, m, v):
    m = ADAM_B1 * m + (1.0 - ADAM_B1) * g
    v = ADAM_B2 * v + (1.0 - ADAM_B2) * (g * g)
    m_hat = m / (1.0 - ADAM_B1 ** ADAM_STEP)
    v_hat = v / (1.0 - ADAM_B2 ** ADAM_STEP)
    delta = -ADAM_LR * (m_hat / (jnp.sqrt(v_hat) + ADAM_EPS) + ADAM_WD * w)
    return delta, m, v


def _adamw(w, g, m, v, name, tr=128):
    layers, r, c = w.shape
    pieces = isinstance(g, (list, tuple))
    if r <= tr or r % 8:
        tr = r
    while r % tr:
        tr -= 8
    nr = r // tr
    g_list = list(g) if pieces else [g]

    def body(w_ref, *rest):
        g_refs, (m_ref, v_ref, go_ref, d_ref, mo_ref, vo_ref) = rest[:len(g_list)], rest[len(g_list):]

        def update(grad):
            delta, m_new, v_new = _adamw_math(w_ref[...], grad, m_ref[...], v_ref[...])
            go_ref[...], d_ref[...], mo_ref[...], vo_ref[...] = grad, delta, m_new, v_new

        if not pieces:
            update(g_refs[0][...])
            return
        for layer, g_ref in enumerate(g_refs):
            @pl.when(pl.program_id(0) == layer)
            def _(g_ref=g_ref):
                grad = g_ref[0].astype(F32)
                for i in range(1, N_DEV):
                    grad = grad + g_ref[i].astype(F32)
                update(grad)

    spec = pl.BlockSpec((None, tr, c), lambda l, i: (l, i, 0))
    if pieces:
        g_specs = [pl.BlockSpec((N_DEV, tr, c),
                                lambda l, i, k=k: (0, jnp.where(l == k, i, jnp.where(l < k, 0, nr - 1)), 0))
                   for k in range(layers)]
    else:
        g_specs = [spec]
    return _pcall(
        body, name=name, grid=(layers, nr), in_specs=[spec] + g_specs + [spec, spec], out_specs=[spec] * 4,
        out_shape=[jax.ShapeDtypeStruct((layers, r, c), F32)] * 4, compiler_params=_params(2),
    )(w, *g_list, m, v)


def _sum8(x, name):
    p = x.shape[2]
    tp = _tile(p, 16 * 1024)

    def body(x_ref, o_ref):
        acc = x_ref[0]
        for i in range(1, N_DEV):
            acc = acc + x_ref[i]
        o_ref[...] = acc

    return _pcall(
        body, name=name, grid=(p // tp,), in_specs=[pl.BlockSpec((N_DEV, 1, tp), lambda i: (0, 0, i))],
        out_specs=pl.BlockSpec((1, tp), lambda i: (0, i)), out_shape=jax.ShapeDtypeStruct((1, p), x.dtype),
        compiler_params=_params(1),
    )(x)


def _exchange(arrays, name, scatter):
    n = len(arrays)
    hbm = pl.BlockSpec(memory_space=pl.ANY)

    def body(*refs):
        ins, outs, token = refs[:n], refs[n:2 * n], refs[2 * n]
        send_sems, recv_sems, local_sems = refs[2 * n + 1:]
        token[...] = jnp.zeros_like(token)
        x, y, c = lax.axis_index("x"), lax.axis_index("y"), lax.axis_index("c")
        me = 4 * x + 2 * y + c
        copies = []
        for a in range(n):
            src_mine = ins[a].at[me] if scatter else ins[a]
            local = pltpu.make_async_copy(src_mine, outs[a].at[me], local_sems.at[a])
            local.start()
            copies.append(local)
            for k in range(1, N_DEV):
                px = 1 - x if k & 4 else x
                py = 1 - y if k & 2 else y
                pc = 1 - c if k & 1 else c
                src = ins[a].at[4 * px + 2 * py + pc] if scatter else ins[a]
                cp = pltpu.make_async_remote_copy(
                    src_ref=src, dst_ref=outs[a].at[me],
                    send_sem=send_sems.at[a * (N_DEV - 1) + k - 1], recv_sem=recv_sems.at[a * (N_DEV - 1) + k - 1],
                    device_id=(px, py, pc), device_id_type=pl.DeviceIdType.MESH)
                cp.start()
                copies.append(cp)
        for cp in copies:
            cp.wait()

    out_shape = [jax.ShapeDtypeStruct(a.shape if scatter else (N_DEV,) + a.shape, a.dtype) for a in arrays]
    res = _pcall(
        body, name=name, in_specs=[hbm] * n, out_specs=[hbm] * n + [pl.BlockSpec(memory_space=pltpu.VMEM)],
        out_shape=out_shape + [jax.ShapeDtypeStruct((8, LANE), F32)],
        scratch_shapes=[pltpu.SemaphoreType.DMA((n * (N_DEV - 1),)), pltpu.SemaphoreType.DMA((n * (N_DEV - 1),)),
                        pltpu.SemaphoreType.DMA((n,))],
        compiler_params=pltpu.CompilerParams(has_side_effects=True),
    )(*arrays)
    return res[:n], res[n][0, 0]


_HBM = pl.BlockSpec(memory_space=pltpu.HBM)
_SEM = pl.BlockSpec(memory_space=pltpu.SEMAPHORE)
_DATAFLOW = pltpu.SideEffectType.DATAFLOW_SIDE_EFFECTING


def _peer(k, x, y, c):
    return (1 - x if k & 4 else x, 1 - y if k & 2 else y, 1 - c if k & 1 else c)


def _exchange_start(arrays, name, scatter):
    n = len(arrays)
    lands = [lax.empty(a.shape if scatter else (N_DEV,) + a.shape, a.dtype) for a in arrays]

    def body(*refs):
        srcs, dsts = refs[:n], refs[n:2 * n]
        send_sems, recv_sems, token = refs[4 * n:5 * n], refs[5 * n:6 * n], refs[6 * n]
        x, y, c = lax.axis_index("x"), lax.axis_index("y"), lax.axis_index("c")
        me = 4 * x + 2 * y + c
        for a in range(n):
            for k in range(1, N_DEV):
                px, py, pc = _peer(k, x, y, c)
                pltpu.make_async_remote_copy(
                    src_ref=srcs[a].at[4 * px + 2 * py + pc] if scatter else srcs[a], dst_ref=dsts[a].at[me],
                    send_sem=send_sems[a].at[k - 1], recv_sem=recv_sems[a].at[k - 1],
                    device_id=(px, py, pc), device_id_type=pl.DeviceIdType.MESH).start()
        token[...] = jnp.zeros_like(token)

    sems = [pltpu.SemaphoreType.DMA((N_DEV - 1,))] * (2 * n)
    res = _pcall(
        body, name=name,
        in_specs=[_HBM] * (2 * n),
        out_specs=[_HBM] * (2 * n) + [_SEM] * (2 * n) + [pl.BlockSpec(memory_space=pltpu.VMEM)],
        out_shape=[pltpu.HBM(a.shape, a.dtype) for a in arrays] + [pltpu.HBM(l.shape, l.dtype) for l in lands]
        + sems + [jax.ShapeDtypeStruct((8, LANE), F32)],
        input_output_aliases={i: i for i in range(2 * n)},
        compiler_params=pltpu.CompilerParams(has_side_effects=_DATAFLOW),
    )(*[pltpu.with_memory_space_constraint(a, pltpu.HBM) for a in arrays],
      *[pltpu.with_memory_space_constraint(l, pltpu.HBM) for l in lands])
    handles = [(res[a], res[n + a], res[2 * n + a], res[3 * n + a]) for a in range(n)]
    return handles, res[4 * n][0, 0]


def _exchange_wait(handles, after, name, scatter):
    n = len(handles)

    def body(*refs):
        srcs, dsts = refs[:n], refs[n:2 * n]
        send_sems, recv_sems = refs[2 * n:3 * n], refs[3 * n:4 * n]
        x, y, c = lax.axis_index("x"), lax.axis_index("y"), lax.axis_index("c")
        me = 4 * x + 2 * y + c
        for a in range(n):
            for k in range(1, N_DEV):
                cp = pltpu.make_async_remote_copy(
                    src_ref=srcs[a].at[me] if scatter else srcs[a], dst_ref=dsts[a].at[me],
                    send_sem=send_sems[a].at[k - 1], recv_sem=recv_sems[a].at[k - 1],
                    device_id=_peer(k, x, y, c), device_id_type=pl.DeviceIdType.MESH)
                cp.wait_send()
                cp.wait_recv()

    srcs, lands = [h[0] for h in handles], [h[1] for h in handles]
    res = _pcall(
        body, name=name,
        in_specs=[_HBM] * (2 * n) + [_SEM] * (2 * n) + [pl.BlockSpec(memory_space=pl.ANY)],
        out_specs=[_HBM] * (2 * n),
        out_shape=[pltpu.HBM(t.shape, t.dtype) for t in srcs + lands],
        input_output_aliases={i: i for i in range(2 * n)},
        compiler_params=pltpu.CompilerParams(has_side_effects=_DATAFLOW),
    )(*srcs, *lands, *[h[2] for h in handles], *[h[3] for h in handles], after)
    return res[n:]


def _with_own_block(land, mine, me):
    return lax.dynamic_update_slice(land, mine[None], (me,) + (0,) * mine.ndim)


def _pad_cols(x, width=LANE):
    return jnp.pad(x, ((0, 0), (0, width - x.shape[1])))


def _cols_full(g):
    return jnp.transpose(g, (1, 0, 2)).reshape(g.shape[1], -1)


def _cols_pieces(dw):
    k = dw.shape[0]
    return jnp.transpose(dw.reshape(k, N_DEV, -1), (1, 0, 2))


def _ffn_fwd(x1, p, i, tag):
    h2 = _adaln_fwd(x1, p["norm_ffn"][i], p["sc_f"][i], p["sh_f"][i], f"ffn_norm_{tag}")
    u = _matmul(h2, p["fetch"](f"up{i}", h2), name=f"ffn_up_{tag}", tn=1408, b_shards=True)
    a = _conv_act_fwd(u, p["conv_w"][i], p["conv_b"][i], f"ffn_act_{tag}")
    g_f = p["g_f"][i]
    x2, f = _matmul(a, p["fetch"](f"down{i}", a), name=f"ffn_down_{tag}", tk=512, out_dtypes=(F32, F32),
                    epilogue=lambda acc, x1, g: (x1 + (1.0 + g) * acc, acc), extras=(("mn", x1), ("n", g_f)))
    return x2, dict(h2=h2, u=u, a=a, f=f)


def _ffn_bwd(dx2, x1, saved, p, i, tag):
    d = x1.shape[1]
    w_up, w_down = p["fetch"](f"up{i}", None), p["fetch"](f"down{i}", None)
    df, dg_f = _residual_bwd(dx2, saved["f"], p["g_f"][i], f"ffn_res_bwd_{tag}")
    da = _matmul(df, w_down, tb=True, name=f"ffn_down_dx_{tag}", tn=512)
    dw_down = _matmul(saved["a"], df, ta=True, name=f"ffn_down_dw_{tag}", tm=1408, out_dtypes=(BF16,))
    du, dcw, dcb = _conv_act_bwd(saved["u"], p["conv_w"][i], p["conv_b"][i], da, f"ffn_act_bwd_{tag}")
    dcw, dcb = (jnp.concatenate([t[0], t[1]], axis=1) for t in (dcw, dcb))
    dh2 = _matmul(du, w_up, tb=True, name=f"ffn_up_dx_{tag}", tk=1408, a_halves=True, b_shards=True)
    dw_up = _matmul(saved["h2"], du, ta=True, name=f"ffn_up_dw_{tag}", tn=1408, out_dtypes=(BF16,), b_halves=True,
                    out_shards=True)
    tok = p["send"](f"ffn{i}", [dw_up, dw_down.reshape(N_DEV, -1, d)])
    dx1, dsh, dsc, dgain = _adaln_bwd(x1, dh2, dx2, p["norm_ffn"][i] + tok, p["sc_f"][i], f"ffn_norm_bwd_{tag}")
    grads = dict(conv_w=dcw, conv_b=dcb, norm_ffn=dgain, sh_f=dsh, sc_f=dsc, g_f=dg_f)
    return dx1, grads


def _gla_layer_fwd(x, p, i):
    h1 = _adaln_fwd(x, p["norm_mix"][i], p["sc_m"][i], p["sh_m"][i], "gla_norm")
    w_main, w_tail = p["fetch"]("gla_in", h1)
    proj = _matmul(h1, w_main, name="gla_in")
    a_tail = _matmul(h1, w_tail, name="gla_in_tail")
    dk_total = p["gla_wg_p"].shape[1]
    o, states = _gla_fwd(proj, a_tail, p["gla_wg_p"], p["gla_b_gate"], "gla_chunks")
    assert 2 * dk_total == o.shape[1]
    r = ("cols", proj, 2, o.shape[1])
    og = _gla_post_fwd(o, r, p["gla_norm"], "gla_post")
    x1, y = _matmul(og, p["fetch"]("gla_out", og), name="gla_out", out_dtypes=(F32, F32),
                    epilogue=lambda acc, x, g: (x + (1.0 + g) * acc, acc), extras=(("mn", x), ("n", p["g_m"][i])))
    return x1, dict(h1=h1, proj=proj, a_tail=a_tail, o=o, r=r, states=states, og=og, y=y)


def _gla_layer_bwd(dx1, x, sv, p, i):
    d = x.shape[1]
    (w_main, w_tail), w_out = p["fetch"]("gla_in", None), p["fetch"]("gla_out", None)
    dy, dg_m = _residual_bwd(dx1, sv["y"], p["g_m"][i], "gla_res_bwd")
    dog = _matmul(dy, w_out, tb=True, name="gla_out_dx")
    dw_out = _matmul(sv["og"], dy, ta=True, name="gla_out_dw", out_dtypes=(BF16,))
    tok = p["send"]("gla_out", [dw_out.reshape(N_DEV, -1, d)])
    d_o, d_r, dgn = _gla_post_bwd(sv["o"], sv["r"], p["gla_norm"] + tok, dog, "gla_post_bwd")
    dq, dk, dv, dga = _gla_bwd(sv["proj"], sv["a_tail"], p["gla_wg_p"], p["gla_b_gate"], sv["states"], d_o,
                               "gla_chunks_bwd")
    da_tail = _matmul(dga, p["gla_wg_p"], tb=True, name="gla_gate_dx", out_dtypes=(BF16,))
    dwg = _matmul(sv["a_tail"], dga, ta=True, name="gla_gate_dw")
    dbg = _rowwise(lambda t: (_colsum(t),), [("row", dga)], [("acc", dga.shape[1], F32)], name="gla_gate_db")[0]
    dproj = jnp.concatenate([dq, dk, dv, d_r], axis=1)
    dh_tail = _matmul(da_tail, w_tail, tb=True, name="gla_in_tail_dx")
    dh1 = _matmul(dproj, w_main, tb=True, name="gla_in_dx", tk=1024,
                  epilogue=lambda acc, t: (acc + t,), extras=(("mn", dh_tail),))
    dw_main = _matmul(sv["h1"], dproj, ta=True, name="gla_in_dw", out_dtypes=(BF16,))
    dw_tail = _matmul(sv["h1"], da_tail, ta=True, name="gla_in_tail_dw", out_dtypes=(BF16,))
    rank = p["gla_rank"]
    dw_in = jnp.concatenate([dw_main, dw_tail[:, :rank]], axis=1)
    dx, dsh, dsc, dgain = _adaln_bwd(x, dh1, dx1, p["norm_mix"][i], p["sc_m"][i], "gla_norm_bwd")
    grads = dict(gla_w_gate=dwg[:rank], gla_b_gate=dbg, gla_norm=dgn, norm_mix=dgain, sh_m=dsh, sc_m=dsc, g_m=dg_m,
                 gla_w_in_unsent=dw_in)
    return dx, grads


def _fox_layer_fwd(x, p, i):
    d = x.shape[1]
    hd = p["fox_q_norm"].shape[1]
    heads = d // hd
    s = x.shape[0]
    t = _tile(s, 512)
    h1 = _adaln_fwd(x, p["norm_mix"][i], p["sc_m"][i], p["sh_m"][i], "fox_norm")
    w_main, w_tail = p["fetch"]("fox_in", h1)
    proj = _matmul(h1, w_main, name="fox_in")
    fl = _matmul(h1, w_tail, name="fox_in_tail")
    q, k, v, og = (("cols", proj, j, d) for j in range(4))
    qn, kn, vb = _fox_prep(q, k, v, p["fox_q_norm"], p["fox_k_norm"], d, hd, "fox_prep")
    cum = _fox_cum(fl, p["fox_bf_p"], "fox_cum")
    cum_t = jnp.transpose(cum[:, :heads])
    cum_col, cum_row = cum_t[:, :, None], cum_t.reshape(heads, s // t, 1, t)
    o, lse = _fox_attn_fwd(qn, kn, vb, cum_col, cum_row, hd, t, "fox_attn")
    act = _fox_gate_fwd(o, og, "fox_gate")
    x1, y = _matmul(act, p["fetch"]("fox_out", act), name="fox_out", out_dtypes=(F32, F32),
                    epilogue=lambda acc, x, g: (x + (1.0 + g) * acc, acc), extras=(("mn", x), ("n", p["g_m"][i])))
    return x1, dict(h1=h1, q=q, k=k, og=og, fl=fl, qn=qn, kn=kn, vb=vb, cum_col=cum_col, cum_row=cum_row,
                    o=o, lse=lse, act=act, y=y, t=t, hd=hd)


def _fox_layer_bwd(dx1, x, sv, p, i):
    d = x.shape[1]
    hd, t = sv["hd"], sv["t"]
    heads = d // hd
    s = x.shape[0]
    (w_main, w_tail), w_out = p["fetch"]("fox_in", None), p["fetch"]("fox_out", None)
    dy, dg_m = _residual_bwd(dx1, sv["y"], p["g_m"][i], "fox_res_bwd")
    dact = _matmul(dy, w_out, tb=True, name="fox_out_dx")
    dw_out = _matmul(sv["act"], dy, ta=True, name="fox_out_dw", out_dtypes=(BF16,))
    d_o, d_og = _fox_gate_bwd(sv["o"], sv["og"], dact, "fox_gate_bwd")
    dqn, dkn, dvb, dcq, dck = _fox_attn_bwd(sv["qn"], sv["kn"], sv["vb"], d_o, sv["o"], sv["lse"], sv["cum_col"],
                                            sv["cum_row"], hd, t, "fox_attn_bwd")
    dq, dk, gq, gk = _fox_prep_bwd(sv["q"], sv["k"], dqn, dkn, p["fox_q_norm"], p["fox_k_norm"], hd, "fox_prep_bwd")
    dcum = _pad_cols(jnp.transpose(dcq[:, :, 0] - dck.reshape(heads, s)))
    dfl, dbf = _fox_cum_bwd(dcum, sv["fl"], p["fox_bf_p"], "fox_cum_bwd")
    dfl_b = dfl.astype(BF16)
    dproj = jnp.concatenate([dq, dk, dvb, d_og], axis=1)
    dh_tail = _matmul(dfl_b, w_tail, tb=True, name="fox_in_tail_dx")
    dh1 = _matmul(dproj, w_main, tb=True, name="fox_in_dx", tk=1024,
                  epilogue=lambda acc, tl: (acc + tl,), extras=(("mn", dh_tail),))
    dw_main = _matmul(sv["h1"], dproj, ta=True, name="fox_in_dw", out_dtypes=(BF16,))
    dw_tail = _matmul(sv["h1"], dfl_b, ta=True, name="fox_in_tail_dw", out_dtypes=(BF16,))
    dw_in = jnp.concatenate([dw_main, dw_tail[:, :heads]], axis=1)
    tok = p["send"]("fox", [_cols_pieces(dw_in), dw_out.reshape(N_DEV, -1, d)])
    dx, dsh, dsc, dgain = _adaln_bwd(x, dh1, dx1, p["norm_mix"][i] + tok, p["sc_m"][i], "fox_norm_bwd")
    grads = dict(fox_b_f=dbf[:, :heads], fox_q_norm=gq.reshape(heads, hd).sum(0, keepdims=True),
                 fox_k_norm=gk.reshape(heads, hd).sum(0, keepdims=True), norm_mix=dgain, sh_m=dsh, sc_m=dsc, g_m=dg_m)
    return dx, grads


SMALL = ("b_mod", "norm_mix", "norm_ffn", "gla_b_gate", "gla_norm", "fox_b_f", "fox_q_norm", "fox_k_norm",
         "ffn_conv_b", "norm_final")
SMALL_SHARDED = ("gla_w_gate", "ffn_conv_w")
BIG = ("gla_w_in", "gla_w_out", "fox_w_in", "fox_w_out", "ffn_w_up", "ffn_w_down")
WEIGHTS = ("w_mod", "b_mod", "norm_mix", "norm_ffn", "gla_w_in", "gla_w_gate", "gla_b_gate", "gla_norm", "gla_w_out",
           "fox_w_in", "fox_b_f", "fox_q_norm", "fox_k_norm", "fox_w_out", "ffn_w_up", "ffn_conv_w", "ffn_conv_b",
           "ffn_w_down", "norm_final")


def _pack(parts):
    flat = jnp.concatenate([p.reshape(-1) for p in parts])
    pad = (-flat.shape[0]) % 1024
    return jnp.pad(flat, (0, pad)).reshape(1, -1)


def _unpack(flat, shapes):
    out, off = [], 0
    for shp in shapes:
        n = 1
        for s in shp:
            n *= s
        out.append(flat[0, off:off + n].reshape(shp))
        off += n
    return out


def kernel(x, c, w_mod, b_mod, norm_mix, norm_ffn, gla_w_in, gla_w_gate, gla_b_gate, gla_norm, gla_w_out, fox_w_in, fox_b_f, fox_q_norm, fox_k_norm, fox_w_out, ffn_w_up, ffn_conv_w, ffn_conv_b, ffn_w_down, norm_final, loss_target, m_w_mod, m_b_mod, m_norm_mix, m_norm_ffn, m_gla_w_in, m_gla_w_gate, m_gla_b_gate, m_gla_norm, m_gla_w_out, m_fox_w_in, m_fox_b_f, m_fox_q_norm, m_fox_k_norm, m_fox_w_out, m_ffn_w_up, m_ffn_conv_w, m_ffn_conv_b, m_ffn_w_down, m_norm_final, v_w_mod, v_b_mod, v_norm_mix, v_norm_ffn, v_gla_w_in, v_gla_w_gate, v_gla_b_gate, v_gla_norm, v_gla_w_out, v_fox_w_in, v_fox_b_f, v_fox_q_norm, v_fox_k_norm, v_fox_w_out, v_ffn_w_up, v_ffn_conv_w, v_ffn_conv_b, v_ffn_w_down, v_norm_final):
    w = dict(w_mod=w_mod, b_mod=b_mod, norm_mix=norm_mix, norm_ffn=norm_ffn, gla_w_in=gla_w_in, gla_w_gate=gla_w_gate,
             gla_b_gate=gla_b_gate, gla_norm=gla_norm, gla_w_out=gla_w_out, fox_w_in=fox_w_in, fox_b_f=fox_b_f,
             fox_q_norm=fox_q_norm, fox_k_norm=fox_k_norm, fox_w_out=fox_w_out, ffn_w_up=ffn_w_up,
             ffn_conv_w=ffn_conv_w, ffn_conv_b=ffn_conv_b, ffn_w_down=ffn_w_down, norm_final=norm_final)
    mom_m = dict(w_mod=m_w_mod, b_mod=m_b_mod, norm_mix=m_norm_mix, norm_ffn=m_norm_ffn, gla_w_in=m_gla_w_in,
                 gla_w_gate=m_gla_w_gate, gla_b_gate=m_gla_b_gate, gla_norm=m_gla_norm, gla_w_out=m_gla_w_out,
                 fox_w_in=m_fox_w_in, fox_b_f=m_fox_b_f, fox_q_norm=m_fox_q_norm, fox_k_norm=m_fox_k_norm,
                 fox_w_out=m_fox_w_out, ffn_w_up=m_ffn_w_up, ffn_conv_w=m_ffn_conv_w, ffn_conv_b=m_ffn_conv_b,
                 ffn_w_down=m_ffn_w_down, norm_final=m_norm_final)
    mom_v = dict(w_mod=v_w_mod, b_mod=v_b_mod, norm_mix=v_norm_mix, norm_ffn=v_norm_ffn, gla_w_in=v_gla_w_in,
                 gla_w_gate=v_gla_w_gate, gla_b_gate=v_gla_b_gate, gla_norm=v_gla_norm, gla_w_out=v_gla_w_out,
                 fox_w_in=v_fox_w_in, fox_b_f=v_fox_b_f, fox_q_norm=v_fox_q_norm, fox_k_norm=v_fox_k_norm,
                 fox_w_out=v_fox_w_out, ffn_w_up=v_ffn_w_up, ffn_conv_w=v_ffn_conv_w, ffn_conv_b=v_ffn_conv_b,
                 ffn_w_down=v_ffn_w_down, norm_final=v_norm_final)

    me = 4 * lax.axis_index("x") + 2 * lax.axis_index("y") + lax.axis_index("c")
    xs, target = x[0], loss_target[0]
    s, d = xs.shape
    depth = w_mod.shape[0]
    mod_cols = w_mod.shape[2]
    rank = gla_w_gate.shape[1]
    hd = fox_q_norm.shape[1]
    fox_heads = d // hd
    dk_total = gla_w_gate.shape[2] * N_DEV

    cond = c * (1.0 / (1.0 + jnp.exp(-c)))
    g, _ = _exchange([gla_w_gate[0], ffn_conv_w, cond], "gather_small", scatter=False)
    cond_all = g[2][:, 0, :]

    cond_pad = jnp.pad(cond_all, ((0, 16 - N_DEV), (0, 0)))
    mod_part = []
    for i in range(depth):
        b_cols = lax.dynamic_slice(b_mod[i:i + 1], (0, me * mod_cols), (1, mod_cols))
        mod_part.append(_matmul(cond_pad, w_mod[i], name=f"mod_{i}", tn=768,
                                epilogue=lambda acc, b: (acc + b,), extras=(("n", b_cols),))[:N_DEV])
    (mod_all,), tok_mod = _exchange([jnp.stack(mod_part)], "gather_mod", scatter=False)
    mod = lax.dynamic_index_in_dim(mod_all, me, axis=2, keepdims=False)
    mod = jnp.transpose(mod, (1, 0, 2)).reshape(depth, 6, 1, d)

    big_names = ["gla_in", "gla_out", "up0", "down0", "fox_in", "fox_out", "up1", "down1"]
    big_shards = [gla_w_in[0] + tok_mod, gla_w_out[0], ffn_w_up[0], ffn_w_down[0], fox_w_in[0], fox_w_out[0],
                  ffn_w_up[1], ffn_w_down[1]]
    big_shards = [t.astype(BF16) for t in big_shards]
    handles, tok0 = _exchange_start(big_shards, "gather_weights_start", scatter=False)
    in_flight = dict(zip(big_names, zip(handles, big_shards)))
    ready = {}

    def split_tail(full, tail):
        main = full.shape[1] - tail
        return full[:, :main], _pad_cols(full[:, main:])

    def fetch(key, after):
        if key not in ready:
            handle, mine = in_flight[key]
            land = _exchange_wait([handle], after, f"gather_{key}_wait", scatter=False)[0]
            full = _with_own_block(land, mine, me)
            if key == "gla_in":
                ready[key] = split_tail(_cols_full(full), rank)
            elif key == "fox_in":
                ready[key] = split_tail(_cols_full(full), fox_heads)
            elif key.startswith("up"):
                ready[key] = full
            else:
                ready[key] = full.reshape(-1, d)
        return ready[key]

    sent = {}

    def send(key, pieces):
        hs, tok = _exchange_start(pieces, f"scatter_{key}_start", scatter=True)
        sent[key] = (hs, pieces)
        return tok

    p = dict(
        fetch=fetch, send=send,
        gla_wg_p=jnp.pad(_cols_full(g[0]), ((0, LANE - rank), (0, 0))),
        conv_w=[jnp.transpose(g[1][:, i], (1, 0, 2)).reshape(ffn_conv_w.shape[1], -1) for i in range(depth)],
        conv_b=[ffn_conv_b[i:i + 1] for i in range(depth)],
        gla_b_gate=gla_b_gate, gla_norm=gla_norm, fox_q_norm=fox_q_norm, fox_k_norm=fox_k_norm,
        fox_bf_p=_pad_cols(fox_b_f), gla_rank=rank,
        norm_mix=[norm_mix[i:i + 1] + (tok0 if i == 0 else 0.0) for i in range(depth)],
        norm_ffn=[norm_ffn[i:i + 1] for i in range(depth)],
    )

    for j, nm in enumerate(("sh_m", "sc_m", "g_m", "sh_f", "sc_f", "g_f")):
        p[nm] = [mod[i, j] for i in range(depth)]

    acts, saved = [xs], []
    for i in range(depth):
        layer_fwd = _gla_layer_fwd if i % 2 == 0 else _fox_layer_fwd
        x1, sv_mix = layer_fwd(acts[-1], p, i)
        x2, sv_ffn = _ffn_fwd(x1, p, i, str(i))
        saved.append((acts[-1], x1, sv_mix, sv_ffn))
        acts.append(x2)
    dx, d_norm_final, loss_part = _final_loss(acts[-1], target, norm_final.reshape(1, d), "final_loss")

    lg = [None] * depth
    for i in reversed(range(depth)):
        x_in, x1, sv_mix, sv_ffn = saved[i]
        dx, g_ffn = _ffn_bwd(dx, x1, sv_ffn, p, i, str(i))
        layer_bwd = _gla_layer_bwd if i % 2 == 0 else _fox_layer_bwd
        dx, g_mix = layer_bwd(dx, x_in, sv_mix, p, i)
        lg[i] = {**g_ffn, **g_mix}
    grad_x = dx[None]

    gla_l = [i for i in range(depth) if i % 2 == 0]
    fox_l = [i for i in range(depth) if i % 2 == 1]
    small_parts = dict(
        norm_mix=jnp.concatenate([lg[i]["norm_mix"] for i in range(depth)]),
        norm_ffn=jnp.concatenate([lg[i]["norm_ffn"] for i in range(depth)]),
        gla_b_gate=jnp.concatenate([lg[i]["gla_b_gate"] for i in gla_l]),
        gla_norm=jnp.concatenate([lg[i]["gla_norm"] for i in gla_l]),
        fox_b_f=jnp.concatenate([lg[i]["fox_b_f"] for i in fox_l]),
        fox_q_norm=jnp.concatenate([lg[i]["fox_q_norm"] for i in fox_l]),
        fox_k_norm=jnp.concatenate([lg[i]["fox_k_norm"] for i in fox_l]),
        ffn_conv_b=jnp.concatenate([lg[i]["conv_b"] for i in range(depth)]),
        norm_final=d_norm_final,
        gla_w_gate=jnp.stack([lg[i]["gla_w_gate"] for i in gla_l]),
        ffn_conv_w=jnp.stack([lg[i]["conv_w"] for i in range(depth)]),
        loss=loss_part[:, :1],
    )
    order = ("norm_mix", "norm_ffn", "gla_b_gate", "gla_norm", "fox_b_f", "fox_q_norm", "fox_k_norm", "ffn_conv_b",
             "norm_final", "gla_w_gate", "ffn_conv_w", "loss")
    packed = _pack([small_parts[nm] for nm in order])
    dmod = jnp.stack([jnp.concatenate([lg[i][nm] for nm in ("sh_m", "sc_m", "g_m", "sh_f", "sc_f", "g_f")], axis=1)
                      for i in range(depth)])
    (packed_all, dmod_all), tok_small = _exchange([packed, dmod], "gather_small_grads", scatter=False)
    tok_last = send("gla_in", [_cols_pieces(lg[0]["gla_w_in_unsent"] + tok_small.astype(BF16))])
    summed = _unpack(_sum8(packed_all + tok_last, "sum_small_grads"), [small_parts[nm].shape for nm in order])
    small_g = dict(zip(order, summed))
    loss = small_g["loss"][0, 0]
    dmod_all = dmod_all[:, :, 0, :]

    grads = {}
    cond_t = _pad_cols(jnp.transpose(cond_all)).astype(BF16)
    dmod_cols = lax.dynamic_slice(dmod_all, (0, 0, me * mod_cols), (N_DEV, depth, mod_cols))
    g_w_mod = []
    for i in range(depth):
        rhs = jnp.pad(dmod_cols[:, i], ((0, LANE - N_DEV), (0, 0)))
        g_w_mod.append(_matmul(cond_t, rhs, name=f"mod_dw_{i}", tn=768))
    grads["w_mod"] = jnp.stack(g_w_mod)
    small_g["b_mod"] = _sum8(dmod_all.reshape(N_DEV, 1, -1), "sum_b_mod").reshape(depth, -1)

    received = {}

    def arrive(key, after):
        hs, pieces = sent[key]
        lands = _exchange_wait(hs, after, f"scatter_{key}_wait", scatter=True)
        received[key] = [_with_own_block(land, lax.dynamic_index_in_dim(pc, me, 0, keepdims=False), me)
                         for land, pc in zip(lands, pieces)]

    for key in ("ffn1", "fox", "ffn0", "gla_out"):
        arrive(key, summed[0])

    out_g, out_d, out_m, out_v = {}, {}, {}, {}

    def update(nm, g_arr):
        res = _adamw(w[nm], g_arr, mom_m[nm], mom_v[nm], f"adamw_{nm}")
        out_g[nm], out_d[nm], out_m[nm], out_v[nm] = res

    update("gla_w_out", [received["gla_out"][0]])
    update("fox_w_in", [received["fox"][0]])
    update("fox_w_out", [received["fox"][1]])
    update("ffn_w_up", [received[f"ffn{i}"][0] for i in range(depth)])
    update("ffn_w_down", [received[f"ffn{i}"][1] for i in range(depth)])
    update("w_mod", grads["w_mod"])

    gate_cols = gla_w_gate.shape[2]
    conv_cols = ffn_conv_w.shape[2]
    local_small = dict(small_g)
    local_small["gla_w_gate"] = lax.dynamic_slice_in_dim(small_g["gla_w_gate"], me * gate_cols, gate_cols, axis=2)
    local_small["ffn_conv_w"] = lax.dynamic_slice_in_dim(small_g["ffn_conv_w"], me * conv_cols, conv_cols, axis=2)
    names = SMALL + SMALL_SHARDED
    shapes = [w[nm].shape for nm in names]
    res = _adamw(_pack([w[nm] for nm in names])[None], _pack([local_small[nm] for nm in names])[None],
                 _pack([mom_m[nm] for nm in names])[None], _pack([mom_v[nm] for nm in names])[None], "adamw_small")
    for tgt, flat in zip((out_g, out_d, out_m, out_v), res):
        for nm, arr in zip(names, _unpack(flat[0], shapes)):
            tgt[nm] = arr

    done = sum(out_d[nm].reshape(-1)[0] for nm in ("gla_w_out", "fox_w_in", "fox_w_out", "ffn_w_up", "ffn_w_down",
                                                   "w_mod", "norm_final"))
    arrive("gla_in", done.reshape(1, 1))
    update("gla_w_in", [received["gla_in"][0]])

    return (loss, grad_x, *[out_g[n] for n in WEIGHTS], *[out_d[n] for n in WEIGHTS],
            *[out_m[n] for n in WEIGHTS], *[out_v[n] for n in WEIGHTS])
```

```python
import jax
import jax.numpy as jnp
from jax import lax
from jax.experimental import pallas as pl
from jax.experimental.pallas import tpu as pltpu

F32, BF16 = jnp.float32, jnp.bfloat16
N_DEV = 8
GLA_HEADS = 4
GLA_TAU = 16.0
GLA_CHUNK = 64
NORM_EPS = 1e-6
ADAM_LR, ADAM_B1, ADAM_B2, ADAM_EPS, ADAM_WD, ADAM_STEP = 0.001, 0.9, 0.999, 1e-08, 0.01, 10
LANE = 128
VMEM_LIMIT = 56 * 1024 * 1024
NEG = -1e30


def _pcall(body, **kw):
    return pl.pallas_call(body, **kw)


def _params(n_axes):
    return pltpu.CompilerParams(dimension_semantics=("arbitrary",) * n_axes, vmem_limit_bytes=VMEM_LIMIT)


def _tile(dim, pref):
    if dim <= pref:
        return dim
    t = pref
    while dim % t:
        t -= LANE
    assert t > 0, (dim, pref)
    return t


def _dot(a, b, ta=False, tb=False):
    dims = (((0,) if ta else (1,), (1,) if tb else (0,)), ((), ()))
    return lax.dot_general(a.astype(BF16), b.astype(BF16), dims, preferred_element_type=F32)


def _split3(x):
    hi = x.astype(BF16)
    r1 = x - hi.astype(F32)
    mid = r1.astype(BF16)
    lo = (r1 - mid.astype(F32)).astype(BF16)
    return hi, mid, lo


def _tri_matmul(tri, x):
    hi, mid, lo = _split3(x)
    return _dot(tri, hi) + _dot(tri, mid) + _dot(tri, lo)


def _tri(n, upper=False):
    r = lax.broadcasted_iota(jnp.int32, (n, n), 0)
    c = lax.broadcasted_iota(jnp.int32, (n, n), 1)
    return jnp.where((r <= c) if upper else (r >= c), 1.0, 0.0).astype(BF16)


def _log_sigmoid(x):
    return jnp.minimum(x, 0.0) - jnp.log(1.0 + jnp.exp(-jnp.abs(x)))


def _sigmoid(x):
    return 1.0 / (1.0 + jnp.exp(-x))


def _silu(x):
    return x * _sigmoid(x)


def _dsilu(x):
    s = _sigmoid(x)
    return s * (1.0 + x * (1.0 - s))


def _matmul(a, b, *, name, ta=False, tb=False, out_dtypes=(F32,), tm=1024, tn=1024, tk=2048,
            epilogue=None, extras=(), a_halves=False, b_halves=False, b_shards=False, out_shards=False):
    if a_halves:
        assert not ta
        m, k = a.shape[1], 2 * a.shape[2]
    else:
        m, k = (a.shape[1], a.shape[0]) if ta else a.shape
    if b_halves:
        assert not tb and b.shape[1] == k
        n = 2 * b.shape[2]
    elif b_shards:
        n = b.shape[1] if tb else N_DEV * b.shape[2]
        assert (N_DEV * b.shape[2] if tb else b.shape[1]) == k, (a.shape, b.shape, ta, tb)
    else:
        n = b.shape[0] if tb else b.shape[1]
        assert (b.shape[1] if tb else b.shape[0]) == k, (a.shape, b.shape, ta, tb)
    n_unit = n // N_DEV if (out_shards or (b_shards and not tb)) else (n // 2 if b_halves else n)
    k_unit = k // N_DEV if (b_shards and tb) else (k // 2 if a_halves else k)
    tm, tn, tk = _tile(m, tm), _tile(n_unit, tn), _tile(k_unit, tk)
    nk = k // tk
    if a_halves:
        a_spec = pl.BlockSpec((None, tm, tk), lambda i, j, kk: (kk // (nk // 2), i, kk % (nk // 2)))
    elif ta:
        a_spec = pl.BlockSpec((tk, tm), lambda i, j, kk: (kk, i))
    else:
        a_spec = pl.BlockSpec((tm, tk), lambda i, j, kk: (i, kk))
    n_per, k_per = n // tn // N_DEV, nk // N_DEV
    if b_halves:
        b_spec = pl.BlockSpec((None, tk, tn), lambda i, j, kk: (j // (n // tn // 2), kk, j % (n // tn // 2)))
    elif b_shards and tb:
        b_spec = pl.BlockSpec((None, tn, tk), lambda i, j, kk: (kk // k_per, j, kk % k_per))
    elif b_shards:
        b_spec = pl.BlockSpec((None, tk, tn), lambda i, j, kk: (j // n_per, kk, j % n_per))
    elif tb:
        b_spec = pl.BlockSpec((tn, tk), lambda i, j, kk: (j, kk))
    else:
        b_spec = pl.BlockSpec((tk, tn), lambda i, j, kk: (kk, j))
    ex_specs = []
    for kind, arr in extras:
        if kind == "mn":
            assert arr.shape == (m, n), (arr.shape, m, n)
            ex_specs.append(pl.BlockSpec((tm, tn), lambda i, j, kk: (i, j)))
        else:
            assert arr.shape == (1, n), (arr.shape, n)
            ex_specs.append(pl.BlockSpec((1, tn), lambda i, j, kk: (0, j)))
    n_ex, n_out = len(extras), len(out_dtypes)

    def body(a_ref, b_ref, *rest):
        ex, outs, acc = rest[:n_ex], rest[n_ex:n_ex + n_out], rest[-1]
        kk = pl.program_id(2)

        @pl.when(kk == 0)
        def _():
            acc[...] = jnp.zeros_like(acc)

        acc[...] += _dot(a_ref[...], b_ref[...], ta, tb)

        @pl.when(kk == nk - 1)
        def _():
            if epilogue is None:
                vals = (acc[...],)
            else:
                vals = epilogue(acc[...], *[e[...] for e in ex])
            for o, v in zip(outs, vals):
                o[...] = v.astype(o.dtype)

    if out_shards:
        out_spec = pl.BlockSpec((None, tm, tn), lambda i, j, kk: (j // n_per, i, j % n_per))
        out_dims = (N_DEV, m, n // N_DEV)
    else:
        out_spec = pl.BlockSpec((tm, tn), lambda i, j, kk: (i, j))
        out_dims = (m, n)
    res = _pcall(
        body, name=name, grid=(m // tm, n // tn, nk),
        in_specs=[a_spec, b_spec] + ex_specs,
        out_specs=[out_spec] * n_out,
        out_shape=[jax.ShapeDtypeStruct(out_dims, d) for d in out_dtypes],
        scratch_shapes=[pltpu.VMEM((tm, tn), F32)],
        compiler_params=_params(3),
    )(a, b, *[arr for _, arr in extras])
    return res[0] if n_out == 1 else res


def _rowwise(fn, ins, outs, *, name, tr=128):
    rows = next(e[1].shape[0] for e in ins if e[0] != "full")
    tr = _tile(rows, tr)
    in_specs = []
    for entry in ins:
        kind, arr = entry[0], entry[1]
        assert kind == "full" or (arr.shape[0] == rows and arr.ndim == 2)
        if kind == "row":
            in_specs.append(pl.BlockSpec((tr, arr.shape[1]), lambda i: (i, 0)))
        elif kind == "cols":
            in_specs.append(pl.BlockSpec((tr, entry[3]), lambda i, cb=entry[2]: (i, cb)))
        else:
            in_specs.append(pl.BlockSpec(arr.shape, lambda i, nd=arr.ndim: (0,) * nd))
    out_specs, out_shape = [], []
    for kind, w, dt in outs:
        if kind == "row":
            out_specs.append(pl.BlockSpec((tr, w), lambda i: (i, 0)))
            out_shape.append(jax.ShapeDtypeStruct((rows, w), dt))
        else:
            out_specs.append(pl.BlockSpec((1, w), lambda i: (0, 0)))
            out_shape.append(jax.ShapeDtypeStruct((1, w), dt))
    n_in = len(ins)

    def body(*refs):
        i = pl.program_id(0)
        vals = fn(*[r[...] for r in refs[:n_in]])
        for (kind, _, _), o, v in zip(outs, refs[n_in:], vals):
            if kind == "row":
                o[...] = v.astype(o.dtype)
            else:
                @pl.when(i == 0)
                def _(o=o):
                    o[...] = jnp.zeros_like(o)

                o[...] += v.astype(o.dtype)

    return _pcall(body, name=name, grid=(rows // tr,), in_specs=in_specs, out_specs=out_specs,
                  out_shape=out_shape, compiler_params=_params(1))(*[e[1] for e in ins])


def _colsum(x):
    return jnp.sum(x, axis=0, keepdims=True)


def _norm_stats(x):
    rstd = lax.rsqrt(jnp.mean(x * x, axis=-1, keepdims=True) + NORM_EPS)
    return x * rstd, rstd


def _norm_bwd(dxhat, xhat, rstd):
    return rstd * (dxhat - xhat * jnp.mean(dxhat * xhat, axis=-1, keepdims=True))


def _adaln_fwd(x, gain, sc, sh, name):
    def fn(x, gain, sc, sh):
        xhat, _ = _norm_stats(x)
        return ((xhat * gain) * (1.0 + sc) + sh,)

    return _rowwise(fn, [("row", x), ("full", gain), ("full", sc), ("full", sh)],
                    [("row", x.shape[1], BF16)], name=name)[0]


def _adaln_bwd(x, dh, dres, gain, sc, name):
    d = x.shape[1]

    def fn(x, dh, dres, gain, sc):
        xhat, rstd = _norm_stats(x)
        dxhat = dh * (gain * (1.0 + sc))
        dx = dres + _norm_bwd(dxhat, xhat, rstd)
        return dx, _colsum(dh), _colsum(dh * (xhat * gain)), _colsum(dh * xhat * (1.0 + sc))

    return _rowwise(fn, [("row", x), ("row", dh), ("row", dres), ("full", gain), ("full", sc)],
                    [("row", d, F32), ("acc", d, F32), ("acc", d, F32), ("acc", d, F32)], name=name)


def _residual_bwd(dx, y, g, name):
    d = dx.shape[1]

    def fn(dx, y, g):
        return dx * (1.0 + g), _colsum(dx * y)

    return _rowwise(fn, [("row", dx), ("row", y), ("full", g)], [("row", d, BF16), ("acc", d, F32)], name=name)


def _final_loss(x, target, gain, name):
    d = x.shape[1]

    def fn(x, t, gain):
        xhat, rstd = _norm_stats(x)
        err = xhat * gain - t
        dy = err * (1.0 / d)
        loss = 0.5 * jnp.sum(jnp.mean(err * err, axis=-1, keepdims=True), axis=0, keepdims=True)
        dx = _norm_bwd(dy * gain, xhat, rstd)
        return dx, _colsum(dy * xhat), jnp.broadcast_to(loss, (1, LANE))

    return _rowwise(fn, [("row", x), ("row", target), ("full", gain)],
                    [("row", d, F32), ("acc", d, F32), ("acc", LANE, F32)], name=name)


def _gla_gates(q_ref, k_ref, a_ref, wg_ref, bg_ref, scale, c):
    ga = _dot(a_ref[...], wg_ref[...]) + bg_ref[...]
    la = _log_sigmoid(ga) * (1.0 / GLA_TAU)
    b = _tri_matmul(_tri(c), la)
    bl = _colsum(la)
    eb, enb, eend = jnp.exp(b), jnp.exp(-b), jnp.exp(bl - b)
    q = q_ref[...] * scale
    k = k_ref[...]
    return dict(ga=ga, eb=eb, enb=enb, eend=eend, dec=jnp.exp(bl), q_dec=q * eb, k_inv=k * enb, k_end=k * eend)


def _causal(c):
    return lax.broadcasted_iota(jnp.int32, (c, c), 0) >= lax.broadcasted_iota(jnp.int32, (c, c), 1)


def _gla_specs(heads, c, dk, dv, rev, n_chunks):
    def ch(n):
        return (n_chunks - 1 - n) if rev else n

    return [
        pl.BlockSpec((c, dk), lambda h, n: (ch(n), h)),
        pl.BlockSpec((c, dk), lambda h, n: (ch(n), heads + h)),
        pl.BlockSpec((c, dv), lambda h, n: (ch(n), heads + h)),
        pl.BlockSpec((c, LANE), lambda h, n: (ch(n), 0)),
        pl.BlockSpec((LANE, dk), lambda h, n: (0, h)),
        pl.BlockSpec((1, dk), lambda h, n: (0, h)),
    ]


def _gla_fwd(proj, a_tail, wg_p, bg, name):
    s = proj.shape[0]
    heads, c = GLA_HEADS, GLA_CHUNK
    dk = wg_p.shape[1] // heads
    dv = 2 * dk
    n_chunks = s // c
    scale = dk ** -0.5

    def body(q_ref, k_ref, v_ref, a_ref, wg_ref, bg_ref, o_ref, st_ref, state):
        @pl.when(pl.program_id(1) == 0)
        def _():
            state[...] = jnp.zeros_like(state)

        g = _gla_gates(q_ref, k_ref, a_ref, wg_ref, bg_ref, scale, c)
        v = v_ref[...]
        st = state[...]
        attn = jnp.where(_causal(c), _dot(g["q_dec"], g["k_inv"], tb=True), 0.0)
        o_ref[...] = _dot(attn, v) + _dot(g["q_dec"], st, tb=True)
        st_ref[...] = st.astype(st_ref.dtype)
        state[...] = g["dec"] * st + _dot(v, g["k_end"], ta=True)

    return _pcall(
        body, name=name, grid=(heads, n_chunks),
        in_specs=_gla_specs(heads, c, dk, dv, False, n_chunks),
        out_specs=[pl.BlockSpec((c, dv), lambda h, n: (n, h)),
                   pl.BlockSpec((None, None, dv, dk), lambda h, n: (h, n, 0, 0))],
        out_shape=[jax.ShapeDtypeStruct((s, heads * dv), F32),
                   jax.ShapeDtypeStruct((heads, n_chunks, dv, dk), BF16)],
        scratch_shapes=[pltpu.VMEM((dv, dk), F32)],
        compiler_params=_params(2),
    )(proj, proj, proj, a_tail, wg_p, bg)


def _gla_bwd(proj, a_tail, wg_p, bg, states, d_o, name):
    s = proj.shape[0]
    heads, c = GLA_HEADS, GLA_CHUNK
    dk = wg_p.shape[1] // heads
    dv = 2 * dk
    n_chunks = s // c
    scale = dk ** -0.5

    def body(q_ref, k_ref, v_ref, a_ref, wg_ref, bg_ref, st_ref, do_ref, dq_ref, dk_ref, dv_ref, dga_ref, dstate):
        @pl.when(pl.program_id(1) == 0)
        def _():
            dstate[...] = jnp.zeros_like(dstate)

        g = _gla_gates(q_ref, k_ref, a_ref, wg_ref, bg_ref, scale, c)
        v, st, dst, d_out = v_ref[...], st_ref[...], dstate[...], do_ref[...]
        q_dec, k_inv, k_end = g["q_dec"], g["k_inv"], g["k_end"]
        mask = _causal(c)
        attn = jnp.where(mask, _dot(q_dec, k_inv, tb=True), 0.0)
        d_attn = jnp.where(mask, _dot(d_out, v, tb=True), 0.0)
        d_qdec = _dot(d_attn, k_inv) + _dot(d_out, st)
        d_kinv = _dot(d_attn, q_dec, ta=True)
        d_kend = _dot(v, dst)
        dv_ref[...] = (_dot(attn, d_out, ta=True) + _dot(k_end, dst, tb=True)).astype(dv_ref.dtype)
        d_dec = jnp.sum(dst * st.astype(F32), axis=0, keepdims=True)
        dstate[...] = g["dec"] * dst + _dot(d_out, q_dec, ta=True)

        dq_ref[...] = (d_qdec * (scale * g["eb"])).astype(dq_ref.dtype)
        dk_ref[...] = (d_kinv * g["enb"] + d_kend * g["eend"]).astype(dk_ref.dtype)
        kk = d_kend * k_end
        db = d_qdec * q_dec - d_kinv * k_inv - kk
        dbl = jnp.sum(kk, axis=0, keepdims=True) + d_dec * g["dec"]
        last = lax.broadcasted_iota(jnp.int32, db.shape, 0) == c - 1
        db = db + jnp.where(last, dbl, 0.0)
        dla = _tri_matmul(_tri(c, upper=True), db)
        dga_ref[...] = dla * (1.0 / GLA_TAU) * _sigmoid(-g["ga"])

    rev = lambda h, n: (n_chunks - 1 - n, h)
    return _pcall(
        body, name=name, grid=(heads, n_chunks),
        in_specs=_gla_specs(heads, c, dk, dv, True, n_chunks) + [
            pl.BlockSpec((None, None, dv, dk), lambda h, n: (h, n_chunks - 1 - n, 0, 0)),
            pl.BlockSpec((c, dv), rev)],
        out_specs=[pl.BlockSpec((c, dk), rev), pl.BlockSpec((c, dk), rev), pl.BlockSpec((c, dv), rev),
                   pl.BlockSpec((c, dk), rev)],
        out_shape=[jax.ShapeDtypeStruct((s, heads * dk), BF16), jax.ShapeDtypeStruct((s, heads * dk), BF16),
                   jax.ShapeDtypeStruct((s, heads * dv), BF16), jax.ShapeDtypeStruct((s, heads * dk), F32)],
        scratch_shapes=[pltpu.VMEM((dv, dk), F32)],
        compiler_params=_params(2),
    )(proj, proj, proj, a_tail, wg_p, bg, states, d_o)


def _gla_post_fwd(o, r, gn, name):
    dvt = o.shape[1]
    dv = dvt // GLA_HEADS

    def fn(o, r, gn):
        outs = []
        for h in range(GLA_HEADS):
            sl = slice(h * dv, (h + 1) * dv)
            ohat, _ = _norm_stats(o[:, sl])
            outs.append((ohat * gn[:, sl]) * _silu(r[:, sl]))
        return (jnp.concatenate(outs, axis=1),)

    return _rowwise(fn, [("row", o), r, ("full", gn)], [("row", dvt, BF16)], name=name)[0]


def _gla_post_bwd(o, r, gn, dog, name):
    dvt = o.shape[1]
    dv = dvt // GLA_HEADS

    def fn(o, r, gn, dog):
        d_o, d_r, d_g = [], [], []
        for h in range(GLA_HEADS):
            sl = slice(h * dv, (h + 1) * dv)
            ohat, rstd = _norm_stats(o[:, sl])
            g, rr, dd = gn[:, sl], r[:, sl], dog[:, sl]
            d_r.append(dd * (ohat * g) * _dsilu(rr))
            don = dd * _silu(rr)
            d_g.append(_colsum(don * ohat))
            d_o.append(_norm_bwd(don * g, ohat, rstd))
        return jnp.concatenate(d_o, axis=1), jnp.concatenate(d_r, axis=1), jnp.concatenate(d_g, axis=1)

    return _rowwise(fn, [("row", o), r, ("full", gn), ("row", dog)],
                    [("row", dvt, F32), ("row", dvt, BF16), ("acc", dvt, F32)], name=name)


def _fox_prep(q, k, v, qg, kg, d, hd, name):
    heads = d // hd
    scale = hd ** -0.5

    def fn(q, k, v, qg, kg):
        qs, ks = [], []
        for h in range(heads):
            sl = slice(h * hd, (h + 1) * hd)
            qs.append(_norm_stats(q[:, sl])[0] * qg * scale)
            ks.append(_norm_stats(k[:, sl])[0] * kg)
        return jnp.concatenate(qs, axis=1), jnp.concatenate(ks, axis=1), v

    return _rowwise(fn, [q, k, v, ("full", qg), ("full", kg)],
                    [("row", d, BF16)] * 3, name=name)


def _fox_prep_bwd(q, k, dqn, dkn, qg, kg, hd, name):
    d = dqn.shape[1]
    heads = d // hd
    scale = hd ** -0.5

    def fn(q, k, dqn, dkn, qg, kg):
        dq, dk, gq, gk = [], [], [], []
        for h in range(heads):
            sl = slice(h * hd, (h + 1) * hd)
            for x, dxn, g, s, dl, gl in ((q, dqn, qg, scale, dq, gq), (k, dkn, kg, 1.0, dk, gk)):
                xhat, rstd = _norm_stats(x[:, sl])
                dn = dxn[:, sl] * s
                gl.append(_colsum(dn * xhat))
                dl.append(_norm_bwd(dn * g, xhat, rstd))
        cat = lambda t: jnp.concatenate(t, axis=1)
        return cat(dq), cat(dk), cat(gq), cat(gk)

    return _rowwise(fn, [q, k, ("row", dqn), ("row", dkn), ("full", qg), ("full", kg)],
                    [("row", d, BF16), ("row", d, BF16), ("acc", d, F32), ("acc", d, F32)], name=name)


def _fox_cum(fl, bf_p, name, tb=256):
    s = fl.shape[0]
    tb = _tile(s, tb)

    def body(fl_ref, bf_ref, cum_ref, carry):
        @pl.when(pl.program_id(0) == 0)
        def _():
            carry[...] = jnp.zeros_like(carry)

        lf = _log_sigmoid(fl_ref[...] + bf_ref[...])
        cum_ref[...] = _tri_matmul(_tri(tb), lf) + carry[...]
        carry[...] += _colsum(lf)

    return _pcall(
        body, name=name, grid=(s // tb,),
        in_specs=[pl.BlockSpec((tb, LANE), lambda i: (i, 0)), pl.BlockSpec((1, LANE), lambda i: (0, 0))],
        out_specs=pl.BlockSpec((tb, LANE), lambda i: (i, 0)),
        out_shape=jax.ShapeDtypeStruct((s, LANE), F32),
        scratch_shapes=[pltpu.VMEM((1, LANE), F32)],
        compiler_params=_params(1),
    )(fl, bf_p)


def _fox_cum_bwd(dcum, fl, bf_p, name, tb=256):
    s = fl.shape[0]
    tb = _tile(s, tb)
    nb = s // tb

    def body(dc_ref, fl_ref, bf_ref, dfl_ref, dbf_ref, carry):
        @pl.when(pl.program_id(0) == 0)
        def _():
            carry[...] = jnp.zeros_like(carry)
            dbf_ref[...] = jnp.zeros_like(dbf_ref)

        dc = dc_ref[...]
        dlf = _tri_matmul(_tri(tb, upper=True), dc) + carry[...]
        carry[...] += _colsum(dc)
        dfl = dlf * _sigmoid(-(fl_ref[...] + bf_ref[...]))
        dfl_ref[...] = dfl
        dbf_ref[...] += _colsum(dfl)

    rev = lambda i: (nb - 1 - i, 0)
    return _pcall(
        body, name=name, grid=(nb,),
        in_specs=[pl.BlockSpec((tb, LANE), rev), pl.BlockSpec((tb, LANE), rev), pl.BlockSpec((1, LANE), lambda i: (0, 0))],
        out_specs=[pl.BlockSpec((tb, LANE), rev), pl.BlockSpec((1, LANE), lambda i: (0, 0))],
        out_shape=[jax.ShapeDtypeStruct((s, LANE), F32), jax.ShapeDtypeStruct((1, LANE), F32)],
        scratch_shapes=[pltpu.VMEM((1, LANE), F32)],
        compiler_params=_params(1),
    )(dcum, fl, bf_p)


def _fox_attn_fwd(qn, kn, vb, cum_col, cum_row, hd, t, name):
    s, d = qn.shape
    heads = d // hd
    nq = s // t

    def body(q_ref, k_ref, v_ref, cc_ref, cr_ref, o_ref, lse_ref):
        qi = pl.program_id(1)
        q = q_ref[...]
        cq = cc_ref[...]
        qpos = qi * t + lax.broadcasted_iota(jnp.int32, (t, 1), 0)

        def step(kj, carry):
            m, l, acc = carry
            off = pl.multiple_of(kj * t, t)
            ks, vs = k_ref[pl.ds(off, t), :], v_ref[pl.ds(off, t), :]
            sc = _dot(q, ks, tb=True) + cq - cr_ref[kj]
            kpos = off + lax.broadcasted_iota(jnp.int32, (1, t), 1)
            sc = jnp.where(kpos <= qpos, sc, NEG)
            m_new = jnp.maximum(m, jnp.max(sc, axis=1, keepdims=True))
            alpha = jnp.exp(m - m_new)
            p = jnp.exp(sc - m_new)
            return m_new, alpha * l + jnp.sum(p, axis=1, keepdims=True), alpha * acc + _dot(p, vs)

        init = (jnp.full((t, 1), NEG, F32), jnp.zeros((t, 1), F32), jnp.zeros((t, hd), F32))
        m, l, acc = lax.fori_loop(0, qi + 1, step, init)
        o_ref[...] = acc / l
        lse_ref[...] = m + jnp.log(l)

    return _pcall(
        body, name=name, grid=(heads, nq),
        in_specs=[pl.BlockSpec((t, hd), lambda h, i: (i, h)),
                  pl.BlockSpec((s, hd), lambda h, i: (0, h)),
                  pl.BlockSpec((s, hd), lambda h, i: (0, h)),
                  pl.BlockSpec((None, t, 1), lambda h, i: (h, i, 0)),
                  pl.BlockSpec((None, nq, 1, t), lambda h, i: (h, 0, 0, 0))],
        out_specs=[pl.BlockSpec((t, hd), lambda h, i: (i, h)), pl.BlockSpec((None, t, 1), lambda h, i: (h, i, 0))],
        out_shape=[jax.ShapeDtypeStruct((s, d), F32), jax.ShapeDtypeStruct((heads, s, 1), F32)],
        compiler_params=_params(2),
    )(qn, kn, vb, cum_col, cum_row)


def _fox_attn_bwd(qn, kn, vb, d_o, o, lse, cum_col, cum_row, hd, t, name):
    s, d = qn.shape
    heads = d // hd
    nq = s // t

    def body(q_ref, k_ref, v_ref, do_ref, o_ref, lse_ref, cc_ref, cr_ref,
             dq_ref, dk_ref, dv_ref, dcq_ref, dck_ref, delta):
        kj = pl.program_id(1)

        @pl.when(kj == 0)
        def _():
            dq_ref[...] = jnp.zeros_like(dq_ref)
            dcq_ref[...] = jnp.zeros_like(dcq_ref)
            delta[...] = jnp.sum(do_ref[...] * o_ref[...], axis=1, keepdims=True)

        ks, vs, cr = k_ref[...], v_ref[...], cr_ref[...]
        kpos = kj * t + lax.broadcasted_iota(jnp.int32, (1, t), 1)

        def step(qi, carry):
            dk, dv, dck = carry
            rows = pl.ds(pl.multiple_of(qi * t, t), t)
            q, d_out = q_ref[rows, :], do_ref[rows, :]
            sc = _dot(q, ks, tb=True) + cc_ref[rows, :] - cr
            qpos = qi * t + lax.broadcasted_iota(jnp.int32, (t, 1), 0)
            p = jnp.where(kpos <= qpos, jnp.exp(sc - lse_ref[rows, :]), 0.0)
            ds = p * (_dot(d_out, vs, tb=True) - delta[rows, :])
            dq_ref[rows, :] += _dot(ds, ks)
            dcq_ref[rows, :] += jnp.sum(ds, axis=1, keepdims=True)
            return dk + _dot(ds, q, ta=True), dv + _dot(p, d_out, ta=True), dck + _colsum(ds)

        init = (jnp.zeros((t, hd), F32), jnp.zeros((t, hd), F32), jnp.zeros((1, t), F32))
        dk, dv, dck = lax.fori_loop(kj, nq, step, init)
        dk_ref[...] = dk.astype(dk_ref.dtype)
        dv_ref[...] = dv.astype(dv_ref.dtype)
        dck_ref[...] = dck

    head_rows = lambda h, j: (0, h)
    blk = lambda h, j: (j, h)
    return _pcall(
        body, name=name, grid=(heads, nq),
        in_specs=[pl.BlockSpec((s, hd), head_rows), pl.BlockSpec((t, hd), blk), pl.BlockSpec((t, hd), blk),
                  pl.BlockSpec((s, hd), head_rows), pl.BlockSpec((s, hd), head_rows),
                  pl.BlockSpec((None, s, 1), lambda h, j: (h, 0, 0)),
                  pl.BlockSpec((None, s, 1), lambda h, j: (h, 0, 0)),
                  pl.BlockSpec((None, None, 1, t), lambda h, j: (h, j, 0, 0))],
        out_specs=[pl.BlockSpec((s, hd), head_rows), pl.BlockSpec((t, hd), blk), pl.BlockSpec((t, hd), blk),
                   pl.BlockSpec((None, s, 1), lambda h, j: (h, 0, 0)),
                   pl.BlockSpec((None, None, 1, t), lambda h, j: (h, j, 0, 0))],
        out_shape=[jax.ShapeDtypeStruct((s, d), F32), jax.ShapeDtypeStruct((s, d), BF16),
                   jax.ShapeDtypeStruct((s, d), BF16), jax.ShapeDtypeStruct((heads, s, 1), F32),
                   jax.ShapeDtypeStruct((heads, nq, 1, t), F32)],
        scratch_shapes=[pltpu.VMEM((s, 1), F32)],
        compiler_params=_params(2),
    )(qn, kn, vb, d_o, o, lse, cum_col, cum_row)


def _fox_gate_fwd(o, og, name):
    def fn(o, og):
        return (o * _sigmoid(og),)

    return _rowwise(fn, [("row", o), og], [("row", o.shape[1], BF16)], name=name)[0]


def _fox_gate_bwd(o, og, dact, name):
    def fn(o, og, dact):
        sg = _sigmoid(og)
        return dact * sg, dact * o * sg * (1.0 - sg)

    d = o.shape[1]
    return _rowwise(fn, [("row", o), og, ("row", dact)], [("row", d, F32), ("row", d, BF16)], name=name)


def _shift_down(x, n):
    rows = lax.broadcasted_iota(jnp.int32, x.shape, 0)
    return jnp.where(rows >= n, pltpu.roll(x, n, 0), 0.0)


def _shift_up(x, n):
    rows = lax.broadcasted_iota(jnp.int32, x.shape, 0)
    return jnp.where(rows < x.shape[0] - n, pltpu.roll(x, x.shape[0] - n, 0), 0.0)


def _conv(u, w_ref, b):
    return w_ref[0:1, :] * _shift_down(u, 2) + w_ref[1:2, :] * _shift_down(u, 1) + w_ref[2:3, :] * u + b


def _conv_act_fwd(u, cw, cb, name, tc=256):
    s, two_f = u.shape
    dff = two_f // 2
    tc = _tile(dff, tc)
    nb = dff // tc

    def body(ug_ref, uv_ref, wg_ref, wv_ref, bg_ref, bv_ref, a_ref):
        gate = _conv(ug_ref[...], wg_ref, bg_ref[...])
        val = _conv(uv_ref[...], wv_ref, bv_ref[...])
        a_ref[...] = (_silu(gate) * val).astype(a_ref.dtype)

    lo, hi = (lambda j: (0, j)), (lambda j: (0, j + nb))
    return _pcall(
        body, name=name, grid=(nb,),
        in_specs=[pl.BlockSpec((s, tc), lo), pl.BlockSpec((s, tc), hi), pl.BlockSpec((3, tc), lo),
                  pl.BlockSpec((3, tc), hi), pl.BlockSpec((1, tc), lo), pl.BlockSpec((1, tc), hi)],
        out_specs=pl.BlockSpec((s, tc), lo),
        out_shape=jax.ShapeDtypeStruct((s, dff), BF16),
        compiler_params=_params(1),
    )(u, u, cw, cw, cb, cb)


def _conv_act_bwd(u, cw, cb, da, name, tc=128):
    s, two_f = u.shape
    dff = two_f // 2
    tc = _tile(dff, tc)
    nb = dff // tc

    def body(ug_ref, uv_ref, wg_ref, wv_ref, bg_ref, bv_ref, da_ref, du_ref, dw_ref, db_ref):
        ug, uv, da = ug_ref[...], uv_ref[...], da_ref[...]
        gate = _conv(ug, wg_ref, bg_ref[...])
        val = _conv(uv, wv_ref, bv_ref[...])
        sg = _sigmoid(gate)
        d_val = da * (gate * sg)
        d_gate = da * val * (sg * (1.0 + gate * (1.0 - sg)))
        for half, (dc, uu, w_ref) in enumerate(((d_gate, ug, wg_ref), (d_val, uv, wv_ref))):
            du = w_ref[0:1, :] * _shift_up(dc, 2) + w_ref[1:2, :] * _shift_up(dc, 1) + w_ref[2:3, :] * dc
            du_ref[half] = du.astype(du_ref.dtype)
            dw_ref[half, 0:1, :] = _colsum(dc * _shift_down(uu, 2))
            dw_ref[half, 1:2, :] = _colsum(dc * _shift_down(uu, 1))
            dw_ref[half, 2:3, :] = _colsum(dc * uu)
            db_ref[half] = _colsum(dc)

    lo, hi = (lambda j: (0, j)), (lambda j: (0, j + nb))
    both = lambda j: (0, 0, j)
    return _pcall(
        body, name=name, grid=(nb,),
        in_specs=[pl.BlockSpec((s, tc), lo), pl.BlockSpec((s, tc), hi), pl.BlockSpec((3, tc), lo),
                  pl.BlockSpec((3, tc), hi), pl.BlockSpec((1, tc), lo), pl.BlockSpec((1, tc), hi),
                  pl.BlockSpec((s, tc), lo)],
        out_specs=[pl.BlockSpec((2, s, tc), both), pl.BlockSpec((2, 3, tc), both), pl.BlockSpec((2, 1, tc), both)],
        out_shape=[jax.ShapeDtypeStruct((2, s, dff), BF16), jax.ShapeDtypeStruct((2, 3, dff), F32),
                   jax.ShapeDtypeStruct((2, 1, dff), F32)],
        compiler_params=_params(1),
    )(u, u, cw, cw, cb, cb, da)


def _adamw_math(w, g, m, v):
    m = ADAM_B1 * m + (1.0 - ADAM_B1) * g
    v = ADAM_B2 * v + (1.0 - ADAM_B2) * (g * g)
    m_hat = m / (1.0 - ADAM_B1 ** ADAM_STEP)
    v_hat = v / (1.0 - ADAM_B2 ** ADAM_STEP)
    delta = -ADAM_LR * (m_hat / (jnp.sqrt(v_hat) + ADAM_EPS) + ADAM_WD * w)
    return delta, m, v


def _adamw(w, g, m, v, name, tr=128):
    layers, r, c = w.shape
    pieces = isinstance(g, (list, tuple))
    if r <= tr or r % 8:
        tr = r
    while r % tr:
        tr -= 8
    nr = r // tr
    g_list = list(g) if pieces else [g]

    def body(w_ref, *rest):
        g_refs, (m_ref, v_ref, go_ref, d_ref, mo_ref, vo_ref) = rest[:len(g_list)], rest[len(g_list):]

        def update(grad):
            delta, m_new, v_new = _adamw_math(w_ref[...], grad, m_ref[...], v_ref[...])
            go_ref[...], d_ref[...], mo_ref[...], vo_ref[...] = grad, delta, m_new, v_new

        if not pieces:
            update(g_refs[0][...])
            return
        for layer, g_ref in enumerate(g_refs):
            @pl.when(pl.program_id(0) == layer)
            def _(g_ref=g_ref):
                grad = g_ref[0].astype(F32)
                for i in range(1, N_DEV):
                    grad = grad + g_ref[i].astype(F32)
                update(grad)

    spec = pl.BlockSpec((None, tr, c), lambda l, i: (l, i, 0))
    if pieces:
        g_specs = [pl.BlockSpec((N_DEV, tr, c),
                                lambda l, i, k=k: (0, jnp.where(l == k, i, jnp.where(l < k, 0, nr - 1)), 0))
                   for k in range(layers)]
    else:
        g_specs = [spec]
    return _pcall(
        body, name=name, grid=(layers, nr), in_specs=[spec] + g_specs + [spec, spec], out_specs=[spec] * 4,
        out_shape=[jax.ShapeDtypeStruct((layers, r, c), F32)] * 4, compiler_params=_params(2),
    )(w, *g_list, m, v)


def _sum8(x, name):
    p = x.shape[2]
    tp = _tile(p, 16 * 1024)

    def body(x_ref, o_ref):
        acc = x_ref[0]
        for i in range(1, N_DEV):
            acc = acc + x_ref[i]
        o_ref[...] = acc

    return _pcall(
        body, name=name, grid=(p // tp,), in_specs=[pl.BlockSpec((N_DEV, 1, tp), lambda i: (0, 0, i))],
        out_specs=pl.BlockSpec((1, tp), lambda i: (0, i)), out_shape=jax.ShapeDtypeStruct((1, p), x.dtype),
        compiler_params=_params(1),
    )(x)


def _exchange(arrays, name, scatter):
    n = len(arrays)
    hbm = pl.BlockSpec(memory_space=pl.ANY)

    def body(*refs):
        ins, outs, token = refs[:n], refs[n:2 * n], refs[2 * n]
        send_sems, recv_sems, local_sems = refs[2 * n + 1:]
        token[...] = jnp.zeros_like(token)
        x, y, c = lax.axis_index("x"), lax.axis_index("y"), lax.axis_index("c")
        me = 4 * x + 2 * y + c
        copies = []
        for a in range(n):
            src_mine = ins[a].at[me] if scatter else ins[a]
            local = pltpu.make_async_copy(src_mine, outs[a].at[me], local_sems.at[a])
            local.start()
            copies.append(local)
            for k in range(1, N_DEV):
                px = 1 - x if k & 4 else x
                py = 1 - y if k & 2 else y
                pc = 1 - c if k & 1 else c
                src = ins[a].at[4 * px + 2 * py + pc] if scatter else ins[a]
                cp = pltpu.make_async_remote_copy(
                    src_ref=src, dst_ref=outs[a].at[me],
                    send_sem=send_sems.at[a * (N_DEV - 1) + k - 1], recv_sem=recv_sems.at[a * (N_DEV - 1) + k - 1],
                    device_id=(px, py, pc), device_id_type=pl.DeviceIdType.MESH)
                cp.start()
                copies.append(cp)
        for cp in copies:
            cp.wait()

    out_shape = [jax.ShapeDtypeStruct(a.shape if scatter else (N_DEV,) + a.shape, a.dtype) for a in arrays]
    res = _pcall(
        body, name=name, in_specs=[hbm] * n, out_specs=[hbm] * n + [pl.BlockSpec(memory_space=pltpu.VMEM)],
        out_shape=out_shape + [jax.ShapeDtypeStruct((8, LANE), F32)],
        scratch_shapes=[pltpu.SemaphoreType.DMA((n * (N_DEV - 1),)), pltpu.SemaphoreType.DMA((n * (N_DEV - 1),)),
                        pltpu.SemaphoreType.DMA((n,))],
        compiler_params=pltpu.CompilerParams(has_side_effects=True),
    )(*arrays)
    return res[:n], res[n][0, 0]


_HBM = pl.BlockSpec(memory_space=pltpu.HBM)
_SEM = pl.BlockSpec(memory_space=pltpu.SEMAPHORE)
_DATAFLOW = pltpu.SideEffectType.DATAFLOW_SIDE_EFFECTING


def _peer(k, x, y, c):
    return (1 - x if k & 4 else x, 1 - y if k & 2 else y, 1 - c if k & 1 else c)


def _exchange_start(arrays, name, scatter):
    n = len(arrays)
    lands = [lax.empty(a.shape if scatter else (N_DEV,) + a.shape, a.dtype) for a in arrays]

    def body(*refs):
        srcs, dsts = refs[:n], refs[n:2 * n]
        send_sems, recv_sems, token = refs[4 * n:5 * n], refs[5 * n:6 * n], refs[6 * n]
        x, y, c = lax.axis_index("x"), lax.axis_index("y"), lax.axis_index("c")
        me = 4 * x + 2 * y + c
        for a in range(n):
            for k in range(1, N_DEV):
                px, py, pc = _peer(k, x, y, c)
                pltpu.make_async_remote_copy(
                    src_ref=srcs[a].at[4 * px + 2 * py + pc] if scatter else srcs[a], dst_ref=dsts[a].at[me],
                    send_sem=send_sems[a].at[k - 1], recv_sem=recv_sems[a].at[k - 1],
                    device_id=(px, py, pc), device_id_type=pl.DeviceIdType.MESH).start()
        token[...] = jnp.zeros_like(token)

    sems = [pltpu.SemaphoreType.DMA((N_DEV - 1,))] * (2 * n)
    res = _pcall(
        body, name=name,
        in_specs=[_HBM] * (2 * n),
        out_specs=[_HBM] * (2 * n) + [_SEM] * (2 * n) + [pl.BlockSpec(memory_space=pltpu.VMEM)],
        out_shape=[pltpu.HBM(a.shape, a.dtype) for a in arrays] + [pltpu.HBM(l.shape, l.dtype) for l in lands]
        + sems + [jax.ShapeDtypeStruct((8, LANE), F32)],
        input_output_aliases={i: i for i in range(2 * n)},
        compiler_params=pltpu.CompilerParams(has_side_effects=_DATAFLOW),
    )(*[pltpu.with_memory_space_constraint(a, pltpu.HBM) for a in arrays],
      *[pltpu.with_memory_space_constraint(l, pltpu.HBM) for l in lands])
    handles = [(res[a], res[n + a], res[2 * n + a], res[3 * n + a]) for a in range(n)]
    return handles, res[4 * n][0, 0]


def _exchange_wait(handles, after, name, scatter):
    n = len(handles)

    def body(*refs):
        srcs, dsts = refs[:n], refs[n:2 * n]
        send_sems, recv_sems = refs[2 * n:3 * n], refs[3 * n:4 * n]
        x, y, c = lax.axis_index("x"), lax.axis_index("y"), lax.axis_index("c")
        me = 4 * x + 2 * y + c
        for a in range(n):
            for k in range(1, N_DEV):
                cp = pltpu.make_async_remote_copy(
                    src_ref=srcs[a].at[me] if scatter else srcs[a], dst_ref=dsts[a].at[me],
                    send_sem=send_sems[a].at[k - 1], recv_sem=recv_sems[a].at[k - 1],
                    device_id=_peer(k, x, y, c), device_id_type=pl.DeviceIdType.MESH)
                cp.wait_send()
                cp.wait_recv()

    srcs, lands = [h[0] for h in handles], [h[1] for h in handles]
    res = _pcall(
        body, name=name,
        in_specs=[_HBM] * (2 * n) + [_SEM] * (2 * n) + [pl.BlockSpec(memory_space=pl.ANY)],
        out_specs=[_HBM] * (2 * n),
        out_shape=[pltpu.HBM(t.shape, t.dtype) for t in srcs + lands],
        input_output_aliases={i: i for i in range(2 * n)},
        compiler_params=pltpu.CompilerParams(has_side_effects=_DATAFLOW),
    )(*srcs, *lands, *[h[2] for h in handles], *[h[3] for h in handles], after)
    return res[n:]


_ICI_PEERS = (2, 4, 6)


def _gather2_start(shards, name):
    n = len(shards)
    lands = [lax.empty((N_DEV,) + a.shape, a.dtype) for a in shards]

    def body(*refs):
        srcs, dsts = refs[:n], refs[n:2 * n]
        send_sems, d2d_sems, ici_sems = refs[4 * n:5 * n], refs[5 * n:6 * n], refs[6 * n:7 * n]
        token = refs[7 * n]
        x, y, c = lax.axis_index("x"), lax.axis_index("y"), lax.axis_index("c")
        me = 4 * x + 2 * y + c
        for a in range(n):
            for j, k in enumerate((1,) + _ICI_PEERS):
                recv = d2d_sems[a].at[0] if j == 0 else ici_sems[a].at[j - 1]
                pltpu.make_async_remote_copy(
                    src_ref=srcs[a], dst_ref=dsts[a].at[me], send_sem=send_sems[a].at[j], recv_sem=recv,
                    device_id=_peer(k, x, y, c), device_id_type=pl.DeviceIdType.MESH).start()
        token[...] = jnp.zeros_like(token)

    dma = pltpu.SemaphoreType.DMA
    res = _pcall(
        body, name=name,
        in_specs=[_HBM] * (2 * n),
        out_specs=[_HBM] * (2 * n) + [_SEM] * (3 * n) + [pl.BlockSpec(memory_space=pltpu.VMEM)],
        out_shape=[pltpu.HBM(a.shape, a.dtype) for a in shards] + [pltpu.HBM(l.shape, l.dtype) for l in lands]
        + [dma((4,))] * n + [dma((1,))] * n + [dma((3,))] * n + [jax.ShapeDtypeStruct((8, LANE), F32)],
        input_output_aliases={i: i for i in range(2 * n)},
        compiler_params=pltpu.CompilerParams(has_side_effects=_DATAFLOW),
    )(*[pltpu.with_memory_space_constraint(a, pltpu.HBM) for a in shards],
      *[pltpu.with_memory_space_constraint(l, pltpu.HBM) for l in lands])
    handles = [tuple(res[i * n + a] for i in range(5)) for a in range(n)]
    return handles, res[5 * n][0, 0]


def _gather2_forward(handle, after, name):
    src, land, send_sems, d2d_sem, ici_sems = handle

    def body(land_ref, ici_ref, after_ref, land_out, fwd_send, fwd_recv, token):
        x, y, c = lax.axis_index("x"), lax.axis_index("y"), lax.axis_index("c")
        for j, k in enumerate(_ICI_PEERS):
            px, py, pc = _peer(k, x, y, c)
            block = land_ref.at[4 * px + 2 * py + pc]
            pltpu.make_async_remote_copy(
                src_ref=block, dst_ref=block, send_sem=fwd_send.at[j], recv_sem=ici_ref.at[j],
                device_id=(px, py, pc), device_id_type=pl.DeviceIdType.MESH).wait_recv()
            pltpu.make_async_remote_copy(
                src_ref=block, dst_ref=block, send_sem=fwd_send.at[j], recv_sem=fwd_recv.at[j],
                device_id=(x, y, 1 - c), device_id_type=pl.DeviceIdType.MESH).start()
        token[...] = jnp.zeros_like(token)

    dma = pltpu.SemaphoreType.DMA
    land, fwd_send, fwd_recv, token = _pcall(
        body, name=name,
        in_specs=[_HBM, _SEM, pl.BlockSpec(memory_space=pl.ANY)],
        out_specs=[_HBM, _SEM, _SEM, pl.BlockSpec(memory_space=pltpu.VMEM)],
        out_shape=[pltpu.HBM(land.shape, land.dtype), dma((3,)), dma((3,)), jax.ShapeDtypeStruct((8, LANE), F32)],
        input_output_aliases={0: 0},
        compiler_params=pltpu.CompilerParams(has_side_effects=_DATAFLOW),
    )(land, ici_sems, after)
    return (src, land, send_sems, d2d_sem, fwd_send, fwd_recv), token[0, 0]


def _gather2_wait(handle, after, name):
    src, land, send_sems, d2d_sem, fwd_send, fwd_recv = handle

    def body(src_ref, land_ref, send_ref, d2d_ref, fsend_ref, frecv_ref, after_ref, src_out, land_out):
        x, y, c = lax.axis_index("x"), lax.axis_index("y"), lax.axis_index("c")
        me = 4 * x + 2 * y + c
        sibling = (x, y, 1 - c)
        block = land_ref.at[me]

        def copy(send, recv):
            return pltpu.make_async_remote_copy(src_ref=src_ref, dst_ref=block, send_sem=send, recv_sem=recv,
                                                device_id=sibling, device_id_type=pl.DeviceIdType.MESH)

        for j in range(4):
            copy(send_ref.at[j], d2d_ref.at[0]).wait_send()
        copy(send_ref.at[0], d2d_ref.at[0]).wait_recv()
        for j in range(3):
            copy(fsend_ref.at[j], frecv_ref.at[j]).wait_send()
            copy(fsend_ref.at[j], frecv_ref.at[j]).wait_recv()

    res = _pcall(
        body, name=name,
        in_specs=[_HBM, _HBM, _SEM, _SEM, _SEM, _SEM, pl.BlockSpec(memory_space=pl.ANY)],
        out_specs=[_HBM, _HBM],
        out_shape=[pltpu.HBM(src.shape, src.dtype), pltpu.HBM(land.shape, land.dtype)],
        input_output_aliases={0: 0, 1: 1},
        compiler_params=pltpu.CompilerParams(has_side_effects=_DATAFLOW),
    )(src, land, send_sems, d2d_sem, fwd_send, fwd_recv, after)
    return res[1]


def _with_own_block(land, mine, me):
    return lax.dynamic_update_slice(land, mine[None], (me,) + (0,) * mine.ndim)


def _pad_cols(x, width=LANE):
    return jnp.pad(x, ((0, 0), (0, width - x.shape[1])))


def _cols_full(g):
    return jnp.transpose(g, (1, 0, 2)).reshape(g.shape[1], -1)


def _cols_pieces(dw):
    k = dw.shape[0]
    return jnp.transpose(dw.reshape(k, N_DEV, -1), (1, 0, 2))


def _ffn_fwd(x1, p, i, tag):
    h2 = _adaln_fwd(x1, p["norm_ffn"][i], p["sc_f"][i], p["sh_f"][i], f"ffn_norm_{tag}")
    u = _matmul(h2, p["fetch"](f"up{i}", h2), name=f"ffn_up_{tag}", tn=1408, b_shards=True)
    a = _conv_act_fwd(u, p["conv_w"][i], p["conv_b"][i], f"ffn_act_{tag}")
    g_f = p["g_f"][i]
    x2, f = _matmul(a, p["fetch"](f"down{i}", a), name=f"ffn_down_{tag}", tk=512, out_dtypes=(F32, F32),
                    epilogue=lambda acc, x1, g: (x1 + (1.0 + g) * acc, acc), extras=(("mn", x1), ("n", g_f)))
    return x2, dict(h2=h2, u=u, a=a, f=f)


def _ffn_bwd(dx2, x1, saved, p, i, tag):
    d = x1.shape[1]
    w_up, w_down = p["fetch"](f"up{i}", None), p["fetch"](f"down{i}", None)
    df, dg_f = _residual_bwd(dx2, saved["f"], p["g_f"][i], f"ffn_res_bwd_{tag}")
    da = _matmul(df, w_down, tb=True, name=f"ffn_down_dx_{tag}", tn=512)
    dw_down = _matmul(saved["a"], df, ta=True, name=f"ffn_down_dw_{tag}", tm=1408, out_dtypes=(BF16,))
    du, dcw, dcb = _conv_act_bwd(saved["u"], p["conv_w"][i], p["conv_b"][i], da, f"ffn_act_bwd_{tag}")
    dcw, dcb = (jnp.concatenate([t[0], t[1]], axis=1) for t in (dcw, dcb))
    dh2 = _matmul(du, w_up, tb=True, name=f"ffn_up_dx_{tag}", tk=1408, a_halves=True, b_shards=True)
    dw_up = _matmul(saved["h2"], du, ta=True, name=f"ffn_up_dw_{tag}", tn=1408, out_dtypes=(BF16,), b_halves=True,
                    out_shards=True)
    tok = p["send"](f"ffn{i}", [dw_up, dw_down.reshape(N_DEV, -1, d)])
    dx1, dsh, dsc, dgain = _adaln_bwd(x1, dh2, dx2, p["norm_ffn"][i] + tok, p["sc_f"][i], f"ffn_norm_bwd_{tag}")
    grads = dict(conv_w=dcw, conv_b=dcb, norm_ffn=dgain, sh_f=dsh, sc_f=dsc, g_f=dg_f)
    return dx1, grads


def _gla_layer_fwd(x, p, i):
    h1 = _adaln_fwd(x, p["norm_mix"][i], p["sc_m"][i], p["sh_m"][i], "gla_norm")
    w_main, w_tail = p["fetch"]("gla_in", h1)
    proj = _matmul(h1, w_main, name="gla_in")
    a_tail = _matmul(h1, w_tail, name="gla_in_tail")
    dk_total = p["gla_wg_p"].shape[1]
    o, states = _gla_fwd(proj, a_tail, p["gla_wg_p"], p["gla_b_gate"], "gla_chunks")
    assert 2 * dk_total == o.shape[1]
    r = ("cols", proj, 2, o.shape[1])
    og = _gla_post_fwd(o, r, p["gla_norm"], "gla_post")
    x1, y = _matmul(og, p["fetch"]("gla_out", og), name="gla_out", out_dtypes=(F32, F32),
                    epilogue=lambda acc, x, g: (x + (1.0 + g) * acc, acc), extras=(("mn", x), ("n", p["g_m"][i])))
    return x1, dict(h1=h1, proj=proj, a_tail=a_tail, o=o, r=r, states=states, og=og, y=y)


def _gla_layer_bwd(dx1, x, sv, p, i):
    d = x.shape[1]
    (w_main, w_tail), w_out = p["fetch"]("gla_in", None), p["fetch"]("gla_out", None)
    dy, dg_m = _residual_bwd(dx1, sv["y"], p["g_m"][i], "gla_res_bwd")
    dog = _matmul(dy, w_out, tb=True, name="gla_out_dx")
    dw_out = _matmul(sv["og"], dy, ta=True, name="gla_out_dw", out_dtypes=(BF16,))
    tok = p["send"]("gla_out", [dw_out.reshape(N_DEV, -1, d)])
    d_o, d_r, dgn = _gla_post_bwd(sv["o"], sv["r"], p["gla_norm"] + tok, dog, "gla_post_bwd")
    dq, dk, dv, dga = _gla_bwd(sv["proj"], sv["a_tail"], p["gla_wg_p"], p["gla_b_gate"], sv["states"], d_o,
                               "gla_chunks_bwd")
    da_tail = _matmul(dga, p["gla_wg_p"], tb=True, name="gla_gate_dx", out_dtypes=(BF16,))
    dwg = _matmul(sv["a_tail"], dga, ta=True, name="gla_gate_dw")
    dbg = _rowwise(lambda t: (_colsum(t),), [("row", dga)], [("acc", dga.shape[1], F32)], name="gla_gate_db")[0]
    dproj = jnp.concatenate([dq, dk, dv, d_r], axis=1)
    dh_tail = _matmul(da_tail, w_tail, tb=True, name="gla_in_tail_dx")
    dh1 = _matmul(dproj, w_main, tb=True, name="gla_in_dx", tk=1024,
                  epilogue=lambda acc, t: (acc + t,), extras=(("mn", dh_tail),))
    dw_main = _matmul(sv["h1"], dproj, ta=True, name="gla_in_dw", out_dtypes=(BF16,))
    dw_tail = _matmul(sv["h1"], da_tail, ta=True, name="gla_in_tail_dw", out_dtypes=(BF16,))
    rank = p["gla_rank"]
    dw_in = jnp.concatenate([dw_main, dw_tail[:, :rank]], axis=1)
    dx, dsh, dsc, dgain = _adaln_bwd(x, dh1, dx1, p["norm_mix"][i], p["sc_m"][i], "gla_norm_bwd")
    grads = dict(gla_w_gate=dwg[:rank], gla_b_gate=dbg, gla_norm=dgn, norm_mix=dgain, sh_m=dsh, sc_m=dsc, g_m=dg_m,
                 gla_w_in_unsent=dw_in)
    return dx, grads


def _fox_layer_fwd(x, p, i):
    d = x.shape[1]
    hd = p["fox_q_norm"].shape[1]
    heads = d // hd
    s = x.shape[0]
    t = _tile(s, 512)
    h1 = _adaln_fwd(x, p["norm_mix"][i], p["sc_m"][i], p["sh_m"][i], "fox_norm")
    w_main, w_tail = p["fetch"]("fox_in", h1)
    proj = _matmul(h1, w_main, name="fox_in")
    fl = _matmul(h1, w_tail, name="fox_in_tail")
    q, k, v, og = (("cols", proj, j, d) for j in range(4))
    qn, kn, vb = _fox_prep(q, k, v, p["fox_q_norm"], p["fox_k_norm"], d, hd, "fox_prep")
    cum = _fox_cum(fl, p["fox_bf_p"], "fox_cum")
    cum_t = jnp.transpose(cum[:, :heads])
    cum_col, cum_row = cum_t[:, :, None], cum_t.reshape(heads, s // t, 1, t)
    o, lse = _fox_attn_fwd(qn, kn, vb, cum_col, cum_row, hd, t, "fox_attn")
    act = _fox_gate_fwd(o, og, "fox_gate")
    x1, y = _matmul(act, p["fetch"]("fox_out", act), name="fox_out", out_dtypes=(F32, F32),
                    epilogue=lambda acc, x, g: (x + (1.0 + g) * acc, acc), extras=(("mn", x), ("n", p["g_m"][i])))
    return x1, dict(h1=h1, q=q, k=k, og=og, fl=fl, qn=qn, kn=kn, vb=vb, cum_col=cum_col, cum_row=cum_row,
                    o=o, lse=lse, act=act, y=y, t=t, hd=hd)


def _fox_layer_bwd(dx1, x, sv, p, i):
    d = x.shape[1]
    hd, t = sv["hd"], sv["t"]
    heads = d // hd
    s = x.shape[0]
    (w_main, w_tail), w_out = p["fetch"]("fox_in", None), p["fetch"]("fox_out", None)
    dy, dg_m = _residual_bwd(dx1, sv["y"], p["g_m"][i], "fox_res_bwd")
    dact = _matmul(dy, w_out, tb=True, name="fox_out_dx")
    dw_out = _matmul(sv["act"], dy, ta=True, name="fox_out_dw", out_dtypes=(BF16,))
    d_o, d_og = _fox_gate_bwd(sv["o"], sv["og"], dact, "fox_gate_bwd")
    dqn, dkn, dvb, dcq, dck = _fox_attn_bwd(sv["qn"], sv["kn"], sv["vb"], d_o, sv["o"], sv["lse"], sv["cum_col"],
                                            sv["cum_row"], hd, t, "fox_attn_bwd")
    dq, dk, gq, gk = _fox_prep_bwd(sv["q"], sv["k"], dqn, dkn, p["fox_q_norm"], p["fox_k_norm"], hd, "fox_prep_bwd")
    dcum = _pad_cols(jnp.transpose(dcq[:, :, 0] - dck.reshape(heads, s)))
    dfl, dbf = _fox_cum_bwd(dcum, sv["fl"], p["fox_bf_p"], "fox_cum_bwd")
    dfl_b = dfl.astype(BF16)
    dproj = jnp.concatenate([dq, dk, dvb, d_og], axis=1)
    dh_tail = _matmul(dfl_b, w_tail, tb=True, name="fox_in_tail_dx")
    dh1 = _matmul(dproj, w_main, tb=True, name="fox_in_dx", tk=1024,
                  epilogue=lambda acc, tl: (acc + tl,), extras=(("mn", dh_tail),))
    dw_main = _matmul(sv["h1"], dproj, ta=True, name="fox_in_dw", out_dtypes=(BF16,))
    dw_tail = _matmul(sv["h1"], dfl_b, ta=True, name="fox_in_tail_dw", out_dtypes=(BF16,))
    dw_in = jnp.concatenate([dw_main, dw_tail[:, :heads]], axis=1)
    tok = p["send"]("fox", [_cols_pieces(dw_in), dw_out.reshape(N_DEV, -1, d)])
    dx, dsh, dsc, dgain = _adaln_bwd(x, dh1, dx1, p["norm_mix"][i] + tok, p["sc_m"][i], "fox_norm_bwd")
    grads = dict(fox_b_f=dbf[:, :heads], fox_q_norm=gq.reshape(heads, hd).sum(0, keepdims=True),
                 fox_k_norm=gk.reshape(heads, hd).sum(0, keepdims=True), norm_mix=dgain, sh_m=dsh, sc_m=dsc, g_m=dg_m)
    return dx, grads


SMALL = ("b_mod", "norm_mix", "norm_ffn", "gla_b_gate", "gla_norm", "fox_b_f", "fox_q_norm", "fox_k_norm",
         "ffn_conv_b", "norm_final")
SMALL_SHARDED = ("gla_w_gate", "ffn_conv_w")
BIG = ("gla_w_in", "gla_w_out", "fox_w_in", "fox_w_out", "ffn_w_up", "ffn_w_down")
WEIGHTS = ("w_mod", "b_mod", "norm_mix", "norm_ffn", "gla_w_in", "gla_w_gate", "gla_b_gate", "gla_norm", "gla_w_out",
           "fox_w_in", "fox_b_f", "fox_q_norm", "fox_k_norm", "fox_w_out", "ffn_w_up", "ffn_conv_w", "ffn_conv_b",
           "ffn_w_down", "norm_final")


def _pack(parts):
    flat = jnp.concatenate([p.reshape(-1) for p in parts])
    pad = (-flat.shape[0]) % 1024
    return jnp.pad(flat, (0, pad)).reshape(1, -1)


def _unpack(flat, shapes):
    out, off = [], 0
    for shp in shapes:
        n = 1
        for s in shp:
            n *= s
        out.append(flat[0, off:off + n].reshape(shp))
        off += n
    return out


def kernel(x, c, w_mod, b_mod, norm_mix, norm_ffn, gla_w_in, gla_w_gate, gla_b_gate, gla_norm, gla_w_out, fox_w_in, fox_b_f, fox_q_norm, fox_k_norm, fox_w_out, ffn_w_up, ffn_conv_w, ffn_conv_b, ffn_w_down, norm_final, loss_target, m_w_mod, m_b_mod, m_norm_mix, m_norm_ffn, m_gla_w_in, m_gla_w_gate, m_gla_b_gate, m_gla_norm, m_gla_w_out, m_fox_w_in, m_fox_b_f, m_fox_q_norm, m_fox_k_norm, m_fox_w_out, m_ffn_w_up, m_ffn_conv_w, m_ffn_conv_b, m_ffn_w_down, m_norm_final, v_w_mod, v_b_mod, v_norm_mix, v_norm_ffn, v_gla_w_in, v_gla_w_gate, v_gla_b_gate, v_gla_norm, v_gla_w_out, v_fox_w_in, v_fox_b_f, v_fox_q_norm, v_fox_k_norm, v_fox_w_out, v_ffn_w_up, v_ffn_conv_w, v_ffn_conv_b, v_ffn_w_down, v_norm_final):
    w = dict(w_mod=w_mod, b_mod=b_mod, norm_mix=norm_mix, norm_ffn=norm_ffn, gla_w_in=gla_w_in, gla_w_gate=gla_w_gate,
             gla_b_gate=gla_b_gate, gla_norm=gla_norm, gla_w_out=gla_w_out, fox_w_in=fox_w_in, fox_b_f=fox_b_f,
             fox_q_norm=fox_q_norm, fox_k_norm=fox_k_norm, fox_w_out=fox_w_out, ffn_w_up=ffn_w_up,
             ffn_conv_w=ffn_conv_w, ffn_conv_b=ffn_conv_b, ffn_w_down=ffn_w_down, norm_final=norm_final)
    mom_m = dict(w_mod=m_w_mod, b_mod=m_b_mod, norm_mix=m_norm_mix, norm_ffn=m_norm_ffn, gla_w_in=m_gla_w_in,
                 gla_w_gate=m_gla_w_gate, gla_b_gate=m_gla_b_gate, gla_norm=m_gla_norm, gla_w_out=m_gla_w_out,
                 fox_w_in=m_fox_w_in, fox_b_f=m_fox_b_f, fox_q_norm=m_fox_q_norm, fox_k_norm=m_fox_k_norm,
                 fox_w_out=m_fox_w_out, ffn_w_up=m_ffn_w_up, ffn_conv_w=m_ffn_conv_w, ffn_conv_b=m_ffn_conv_b,
                 ffn_w_down=m_ffn_w_down, norm_final=m_norm_final)
    mom_v = dict(w_mod=v_w_mod, b_mod=v_b_mod, norm_mix=v_norm_mix, norm_ffn=v_norm_ffn, gla_w_in=v_gla_w_in,
                 gla_w_gate=v_gla_w_gate, gla_b_gate=v_gla_b_gate, gla_norm=v_gla_norm, gla_w_out=v_gla_w_out,
                 fox_w_in=v_fox_w_in, fox_b_f=v_fox_b_f, fox_q_norm=v_fox_q_norm, fox_k_norm=v_fox_k_norm,
                 fox_w_out=v_fox_w_out, ffn_w_up=v_ffn_w_up, ffn_conv_w=v_ffn_conv_w, ffn_conv_b=v_ffn_conv_b,
                 ffn_w_down=v_ffn_w_down, norm_final=v_norm_final)

    me = 4 * lax.axis_index("x") + 2 * lax.axis_index("y") + lax.axis_index("c")
    xs, target = x[0], loss_target[0]
    s, d = xs.shape
    depth = w_mod.shape[0]
    mod_cols = w_mod.shape[2]
    rank = gla_w_gate.shape[1]
    hd = fox_q_norm.shape[1]
    fox_heads = d // hd
    dk_total = gla_w_gate.shape[2] * N_DEV

    cond = c * (1.0 / (1.0 + jnp.exp(-c)))
    g, _ = _exchange([gla_w_gate[0], ffn_conv_w, cond], "gather_small", scatter=False)
    cond_all = g[2][:, 0, :]

    cond_pad = jnp.pad(cond_all, ((0, 16 - N_DEV), (0, 0)))
    mod_part = []
    for i in range(depth):
        b_cols = lax.dynamic_slice(b_mod[i:i + 1], (0, me * mod_cols), (1, mod_cols))
        mod_part.append(_matmul(cond_pad, w_mod[i], name=f"mod_{i}", tn=768,
                                epilogue=lambda acc, b: (acc + b,), extras=(("n", b_cols),))[:N_DEV])
    (mod_all,), tok_mod = _exchange([jnp.stack(mod_part)], "gather_mod", scatter=False)
    mod = lax.dynamic_index_in_dim(mod_all, me, axis=2, keepdims=False)
    mod = jnp.transpose(mod, (1, 0, 2)).reshape(depth, 6, 1, d)

    big_names = ["gla_in", "gla_out", "up0", "down0", "fox_in", "fox_out", "up1", "down1"]
    big_shards = [gla_w_in[0] + tok_mod, gla_w_out[0], ffn_w_up[0], ffn_w_down[0], fox_w_in[0], fox_w_out[0],
                  ffn_w_up[1], ffn_w_down[1]]
    big_shards = [t.astype(BF16) for t in big_shards]
    handles, tok0 = _gather2_start(big_shards, "gather_weights_start")
    ready, forwarded = {}, {}

    def split_tail(full, tail):
        main = full.shape[1] - tail
        return full[:, :main], _pad_cols(full[:, main:])

    def forward(idx, after):
        key = big_names[idx]
        forwarded[key] = _gather2_forward(handles[idx], after, f"gather_{key}_forward")

    def fetch(key, after):
        if key not in ready:
            idx = big_names.index(key)
            if idx == 0:
                forward(0, after)
            handle, _ = forwarded[key]
            land = _gather2_wait(handle, after, f"gather_{key}_wait")
            tok = 0.0
            if idx + 1 < len(big_names):
                forward(idx + 1, land)
                tok = forwarded[big_names[idx + 1]][1]
            full = _with_own_block(land, big_shards[idx] + jnp.asarray(tok, F32).astype(BF16), me)
            if key == "gla_in":
                ready[key] = split_tail(_cols_full(full), rank)
            elif key == "fox_in":
                ready[key] = split_tail(_cols_full(full), fox_heads)
            elif key.startswith("up"):
                ready[key] = full
            else:
                ready[key] = full.reshape(-1, d)
        return ready[key]

    sent = {}

    def send(key, pieces):
        hs, tok = _exchange_start(pieces, f"scatter_{key}_start", scatter=True)
        sent[key] = (hs, pieces)
        return tok

    p = dict(
        fetch=fetch, send=send,
        gla_wg_p=jnp.pad(_cols_full(g[0]), ((0, LANE - rank), (0, 0))),
        conv_w=[jnp.transpose(g[1][:, i], (1, 0, 2)).reshape(ffn_conv_w.shape[1], -1) for i in range(depth)],
        conv_b=[ffn_conv_b[i:i + 1] for i in range(depth)],
        gla_b_gate=gla_b_gate, gla_norm=gla_norm, fox_q_norm=fox_q_norm, fox_k_norm=fox_k_norm,
        fox_bf_p=_pad_cols(fox_b_f), gla_rank=rank,
        norm_mix=[norm_mix[i:i + 1] + (tok0 if i == 0 else 0.0) for i in range(depth)],
        norm_ffn=[norm_ffn[i:i + 1] for i in range(depth)],
    )

    for j, nm in enumerate(("sh_m", "sc_m", "g_m", "sh_f", "sc_f", "g_f")):
        p[nm] = [mod[i, j] for i in range(depth)]

    acts, saved = [xs], []
    for i in range(depth):
        layer_fwd = _gla_layer_fwd if i % 2 == 0 else _fox_layer_fwd
        x1, sv_mix = layer_fwd(acts[-1], p, i)
        x2, sv_ffn = _ffn_fwd(x1, p, i, str(i))
        saved.append((acts[-1], x1, sv_mix, sv_ffn))
        acts.append(x2)
    dx, d_norm_final, loss_part = _final_loss(acts[-1], target, norm_final.reshape(1, d), "final_loss")

    lg = [None] * depth
    for i in reversed(range(depth)):
        x_in, x1, sv_mix, sv_ffn = saved[i]
        dx, g_ffn = _ffn_bwd(dx, x1, sv_ffn, p, i, str(i))
        layer_bwd = _gla_layer_bwd if i % 2 == 0 else _fox_layer_bwd
        dx, g_mix = layer_bwd(dx, x_in, sv_mix, p, i)
        lg[i] = {**g_ffn, **g_mix}
    grad_x = dx[None]

    gla_l = [i for i in range(depth) if i % 2 == 0]
    fox_l = [i for i in range(depth) if i % 2 == 1]
    small_parts = dict(
        norm_mix=jnp.concatenate([lg[i]["norm_mix"] for i in range(depth)]),
        norm_ffn=jnp.concatenate([lg[i]["norm_ffn"] for i in range(depth)]),
        gla_b_gate=jnp.concatenate([lg[i]["gla_b_gate"] for i in gla_l]),
        gla_norm=jnp.concatenate([lg[i]["gla_norm"] for i in gla_l]),
        fox_b_f=jnp.concatenate([lg[i]["fox_b_f"] for i in fox_l]),
        fox_q_norm=jnp.concatenate([lg[i]["fox_q_norm"] for i in fox_l]),
        fox_k_norm=jnp.concatenate([lg[i]["fox_k_norm"] for i in fox_l]),
        ffn_conv_b=jnp.concatenate([lg[i]["conv_b"] for i in range(depth)]),
        norm_final=d_norm_final,
        gla_w_gate=jnp.stack([lg[i]["gla_w_gate"] for i in gla_l]),
        ffn_conv_w=jnp.stack([lg[i]["conv_w"] for i in range(depth)]),
        loss=loss_part[:, :1],
    )
    order = ("norm_mix", "norm_ffn", "gla_b_gate", "gla_norm", "fox_b_f", "fox_q_norm", "fox_k_norm", "ffn_conv_b",
             "norm_final", "gla_w_gate", "ffn_conv_w", "loss")
    packed = _pack([small_parts[nm] for nm in order])
    dmod = jnp.stack([jnp.concatenate([lg[i][nm] for nm in ("sh_m", "sc_m", "g_m", "sh_f", "sc_f", "g_f")], axis=1)
                      for i in range(depth)])
    (packed_all, dmod_all), tok_small = _exchange([packed, dmod], "gather_small_grads", scatter=False)
    tok_last = send("gla_in", [_cols_pieces(lg[0]["gla_w_in_unsent"] + tok_small.astype(BF16))])
    summed = _unpack(_sum8(packed_all + tok_last, "sum_small_grads"), [small_parts[nm].shape for nm in order])
    small_g = dict(zip(order, summed))
    loss = small_g["loss"][0, 0]
    dmod_all = dmod_all[:, :, 0, :]

    grads = {}
    cond_t = _pad_cols(jnp.transpose(cond_all)).astype(BF16)
    dmod_cols = lax.dynamic_slice(dmod_all, (0, 0, me * mod_cols), (N_DEV, depth, mod_cols))
    g_w_mod = []
    for i in range(depth):
        rhs = jnp.pad(dmod_cols[:, i], ((0, LANE - N_DEV), (0, 0)))
        g_w_mod.append(_matmul(cond_t, rhs, name=f"mod_dw_{i}", tn=768))
    grads["w_mod"] = jnp.stack(g_w_mod)
    small_g["b_mod"] = _sum8(dmod_all.reshape(N_DEV, 1, -1), "sum_b_mod").reshape(depth, -1)

    received = {}

    def arrive(key, after):
        hs, pieces = sent[key]
        lands = _exchange_wait(hs, after, f"scatter_{key}_wait", scatter=True)
        received[key] = [_with_own_block(land, lax.dynamic_index_in_dim(pc, me, 0, keepdims=False), me)
                         for land, pc in zip(lands, pieces)]

    for key in ("ffn1", "fox", "ffn0", "gla_out"):
        arrive(key, summed[0])

    out_g, out_d, out_m, out_v = {}, {}, {}, {}

    def update(nm, g_arr):
        res = _adamw(w[nm], g_arr, mom_m[nm], mom_v[nm], f"adamw_{nm}")
        out_g[nm], out_d[nm], out_m[nm], out_v[nm] = res

    update("gla_w_out", [received["gla_out"][0]])
    update("fox_w_in", [received["fox"][0]])
    update("fox_w_out", [received["fox"][1]])
    update("ffn_w_up", [received[f"ffn{i}"][0] for i in range(depth)])
    update("ffn_w_down", [received[f"ffn{i}"][1] for i in range(depth)])
    update("w_mod", grads["w_mod"])

    gate_cols = gla_w_gate.shape[2]
    conv_cols = ffn_conv_w.shape[2]
    local_small = dict(small_g)
    local_small["gla_w_gate"] = lax.dynamic_slice_in_dim(small_g["gla_w_gate"], me * gate_cols, gate_cols, axis=2)
    local_small["ffn_conv_w"] = lax.dynamic_slice_in_dim(small_g["ffn_conv_w"], me * conv_cols, conv_cols, axis=2)
    names = SMALL + SMALL_SHARDED
    shapes = [w[nm].shape for nm in names]
    res = _adamw(_pack([w[nm] for nm in names])[None], _pack([local_small[nm] for nm in names])[None],
                 _pack([mom_m[nm] for nm in names])[None], _pack([mom_v[nm] for nm in names])[None], "adamw_small")
    for tgt, flat in zip((out_g, out_d, out_m, out_v), res):
        for nm, arr in zip(names, _unpack(flat[0], shapes)):
            tgt[nm] = arr

    done = sum(out_d[nm].reshape(-1)[0] for nm in ("gla_w_out", "fox_w_in", "fox_w_out", "ffn_w_up", "ffn_w_down",
                                                   "w_mod", "norm_final"))
    arrive("gla_in", done.reshape(1, 1))
    update("gla_w_in", [received["gla_in"][0]])

    return (loss, grad_x, *[out_g[n] for n in WEIGHTS], *[out_d[n] for n in WEIGHTS],
            *[out_m[n] for n in WEIGHTS], *[out_v[n] for n in WEIGHTS])
```

```python
import jax
import jax.numpy as jnp
from jax import lax
from jax.experimental import pallas as pl
from jax.experimental.pallas import tpu as pltpu

F32, BF16 = jnp.float32, jnp.bfloat16
N_DEV = 8
GLA_HEADS = 4
GLA_TAU = 16.0
GLA_CHUNK = 64
NORM_EPS = 1e-6
ADAM_LR, ADAM_B1, ADAM_B2, ADAM_EPS, ADAM_WD, ADAM_STEP = 0.001, 0.9, 0.999, 1e-08, 0.01, 10
LANE = 128
VMEM_LIMIT = 56 * 1024 * 1024
NEG = -1e30


def _pcall(body, **kw):
    return pl.pallas_call(body, **kw)


def _params(n_axes):
    return pltpu.CompilerParams(dimension_semantics=("arbitrary",) * n_axes, vmem_limit_bytes=VMEM_LIMIT)


def _tile(dim, pref):
    if dim <= pref:
        return dim
    t = pref
    while dim % t:
        t -= LANE
    assert t > 0, (dim, pref)
    return t


def _dot(a, b, ta=False, tb=False):
    dims = (((0,) if ta else (1,), (1,) if tb else (0,)), ((), ()))
    return lax.dot_general(a.astype(BF16), b.astype(BF16), dims, preferred_element_type=F32)


def _split3(x):
    hi = x.astype(BF16)
    r1 = x - hi.astype(F32)
    mid = r1.astype(BF16)
    lo = (r1 - mid.astype(F32)).astype(BF16)
    return hi, mid, lo


def _tri_matmul(tri, x):
    hi, mid, lo = _split3(x)
    return _dot(tri, hi) + _dot(tri, mid) + _dot(tri, lo)


def _tri(n, upper=False):
    r = lax.broadcasted_iota(jnp.int32, (n, n), 0)
    c = lax.broadcasted_iota(jnp.int32, (n, n), 1)
    return jnp.where((r <= c) if upper else (r >= c), 1.0, 0.0).astype(BF16)


def _log_sigmoid(x):
    return jnp.minimum(x, 0.0) - jnp.log(1.0 + jnp.exp(-jnp.abs(x)))


def _sigmoid(x):
    return 1.0 / (1.0 + jnp.exp(-x))


def _silu(x):
    return x * _sigmoid(x)


def _dsilu(x):
    s = _sigmoid(x)
    return s * (1.0 + x * (1.0 - s))


def _matmul(a, b, *, name, ta=False, tb=False, out_dtypes=(F32,), tm=1024, tn=1024, tk=2048,
            epilogue=None, extras=(), a_halves=False, b_halves=False, b_shards=False, out_shards=False,
            b_rows=None):
    if a_halves:
        assert not ta
        m, k = a.shape[1], 2 * a.shape[2]
    else:
        m, k = (a.shape[1], a.shape[0]) if ta else a.shape
    if b_halves:
        assert not tb and b.shape[1] == k
        n = 2 * b.shape[2]
    elif b_shards:
        n = b.shape[1] if tb else N_DEV * b.shape[2]
        assert (N_DEV * b.shape[2] if tb else b.shape[1]) == k, (a.shape, b.shape, ta, tb)
    else:
        rows = b.shape[0] if b_rows is None else b_rows
        n = rows if tb else b.shape[1]
        assert (b.shape[1] if tb else rows) == k, (a.shape, b.shape, ta, tb)
    n_unit = n // N_DEV if (out_shards or (b_shards and not tb)) else (n // 2 if b_halves else n)
    k_unit = k // N_DEV if (b_shards and tb) else (k // 2 if a_halves else k)
    tm, tn, tk = _tile(m, tm), _tile(n_unit, tn), _tile(k_unit, tk)
    nk = k // tk
    if a_halves:
        a_spec = pl.BlockSpec((None, tm, tk), lambda i, j, kk: (kk // (nk // 2), i, kk % (nk // 2)))
    elif ta:
        a_spec = pl.BlockSpec((tk, tm), lambda i, j, kk: (kk, i))
    else:
        a_spec = pl.BlockSpec((tm, tk), lambda i, j, kk: (i, kk))
    n_per, k_per = n // tn // N_DEV, nk // N_DEV
    if b_halves:
        b_spec = pl.BlockSpec((None, tk, tn), lambda i, j, kk: (j // (n // tn // 2), kk, j % (n // tn // 2)))
    elif b_shards and tb:
        b_spec = pl.BlockSpec((None, tn, tk), lambda i, j, kk: (kk // k_per, j, kk % k_per))
    elif b_shards:
        b_spec = pl.BlockSpec((None, tk, tn), lambda i, j, kk: (j // n_per, kk, j % n_per))
    elif tb:
        b_spec = pl.BlockSpec((tn, tk), lambda i, j, kk: (j, kk))
    else:
        b_spec = pl.BlockSpec((tk, tn), lambda i, j, kk: (kk, j))
    ex_specs = []
    for kind, arr in extras:
        if kind == "mn":
            assert arr.shape == (m, n), (arr.shape, m, n)
            ex_specs.append(pl.BlockSpec((tm, tn), lambda i, j, kk: (i, j)))
        else:
            assert arr.shape == (1, n), (arr.shape, n)
            ex_specs.append(pl.BlockSpec((1, tn), lambda i, j, kk: (0, j)))
    n_ex, n_out = len(extras), len(out_dtypes)

    def body(a_ref, b_ref, *rest):
        ex, outs, acc = rest[:n_ex], rest[n_ex:n_ex + n_out], rest[-1]
        kk = pl.program_id(2)

        @pl.when(kk == 0)
        def _():
            acc[...] = jnp.zeros_like(acc)

        acc[...] += _dot(a_ref[...], b_ref[...], ta, tb)

        @pl.when(kk == nk - 1)
        def _():
            if epilogue is None:
                vals = (acc[...],)
            else:
                vals = epilogue(acc[...], *[e[...] for e in ex])
            for o, v in zip(outs, vals):
                o[...] = v.astype(o.dtype)

    if out_shards:
        out_spec = pl.BlockSpec((None, tm, tn), lambda i, j, kk: (j // n_per, i, j % n_per))
        out_dims = (N_DEV, m, n // N_DEV)
    else:
        out_spec = pl.BlockSpec((tm, tn), lambda i, j, kk: (i, j))
        out_dims = (m, n)
    res = _pcall(
        body, name=name, grid=(m // tm, n // tn, nk),
        in_specs=[a_spec, b_spec] + ex_specs,
        out_specs=[out_spec] * n_out,
        out_shape=[jax.ShapeDtypeStruct(out_dims, d) for d in out_dtypes],
        scratch_shapes=[pltpu.VMEM((tm, tn), F32)],
        compiler_params=_params(3),
    )(a, b, *[arr for _, arr in extras])
    return res[0] if n_out == 1 else res


def _rowwise(fn, ins, outs, *, name, tr=128):
    rows = next(e[1].shape[0] for e in ins if e[0] != "full")
    tr = _tile(rows, tr)
    in_specs = []
    for entry in ins:
        kind, arr = entry[0], entry[1]
        assert kind == "full" or (arr.shape[0] == rows and arr.ndim == 2)
        if kind == "row":
            in_specs.append(pl.BlockSpec((tr, arr.shape[1]), lambda i: (i, 0)))
        elif kind == "cols":
            in_specs.append(pl.BlockSpec((tr, entry[3]), lambda i, cb=entry[2]: (i, cb)))
        else:
            in_specs.append(pl.BlockSpec(arr.shape, lambda i, nd=arr.ndim: (0,) * nd))
    out_specs, out_shape = [], []
    for kind, w, dt in outs:
        if kind == "row":
            out_specs.append(pl.BlockSpec((tr, w), lambda i: (i, 0)))
            out_shape.append(jax.ShapeDtypeStruct((rows, w), dt))
        else:
            out_specs.append(pl.BlockSpec((1, w), lambda i: (0, 0)))
            out_shape.append(jax.ShapeDtypeStruct((1, w), dt))
    n_in = len(ins)

    def body(*refs):
        i = pl.program_id(0)
        vals = fn(*[r[...] for r in refs[:n_in]])
        for (kind, _, _), o, v in zip(outs, refs[n_in:], vals):
            if kind == "row":
                o[...] = v.astype(o.dtype)
            else:
                @pl.when(i == 0)
                def _(o=o):
                    o[...] = jnp.zeros_like(o)

                o[...] += v.astype(o.dtype)

    return _pcall(body, name=name, grid=(rows // tr,), in_specs=in_specs, out_specs=out_specs,
                  out_shape=out_shape, compiler_params=_params(1))(*[e[1] for e in ins])


def _colsum(x):
    return jnp.sum(x, axis=0, keepdims=True)


def _norm_stats(x):
    rstd = lax.rsqrt(jnp.mean(x * x, axis=-1, keepdims=True) + NORM_EPS)
    return x * rstd, rstd


def _norm_bwd(dxhat, xhat, rstd):
    return rstd * (dxhat - xhat * jnp.mean(dxhat * xhat, axis=-1, keepdims=True))


def _adaln_fwd(x, gain, sc, sh, name):
    def fn(x, gain, sc, sh):
        xhat, _ = _norm_stats(x)
        return ((xhat * gain) * (1.0 + sc) + sh,)

    return _rowwise(fn, [("row", x), ("full", gain), ("full", sc), ("full", sh)],
                    [("row", x.shape[1], BF16)], name=name)[0]


def _adaln_bwd(x, dh, dres, gain, sc, name):
    d = x.shape[1]

    def fn(x, dh, dres, gain, sc):
        xhat, rstd = _norm_stats(x)
        dxhat = dh * (gain * (1.0 + sc))
        dx = dres + _norm_bwd(dxhat, xhat, rstd)
        return dx, _colsum(dh), _colsum(dh * (xhat * gain)), _colsum(dh * xhat * (1.0 + sc))

    return _rowwise(fn, [("row", x), ("row", dh), ("row", dres), ("full", gain), ("full", sc)],
                    [("row", d, F32), ("acc", d, F32), ("acc", d, F32), ("acc", d, F32)], name=name)


def _residual_bwd(dx, y, g, name):
    d = dx.shape[1]

    def fn(dx, y, g):
        return dx * (1.0 + g), _colsum(dx * y)

    return _rowwise(fn, [("row", dx), ("row", y), ("full", g)], [("row", d, BF16), ("acc", d, F32)], name=name)


def _final_loss(x, target, gain, name):
    d = x.shape[1]

    def fn(x, t, gain):
        xhat, rstd = _norm_stats(x)
        err = xhat * gain - t
        dy = err * (1.0 / d)
        loss = 0.5 * jnp.sum(jnp.mean(err * err, axis=-1, keepdims=True), axis=0, keepdims=True)
        dx = _norm_bwd(dy * gain, xhat, rstd)
        return dx, _colsum(dy * xhat), jnp.broadcast_to(loss, (1, LANE))

    return _rowwise(fn, [("row", x), ("row", target), ("full", gain)],
                    [("row", d, F32), ("acc", d, F32), ("acc", LANE, F32)], name=name)


def _gla_gates(q_ref, k_ref, a_ref, wg_ref, bg_ref, scale, c):
    ga = _dot(a_ref[...], wg_ref[...]) + bg_ref[...]
    la = _log_sigmoid(ga) * (1.0 / GLA_TAU)
    b = _tri_matmul(_tri(c), la)
    bl = _colsum(la)
    eb, enb, eend = jnp.exp(b), jnp.exp(-b), jnp.exp(bl - b)
    q = q_ref[...] * scale
    k = k_ref[...]
    return dict(ga=ga, eb=eb, enb=enb, eend=eend, dec=jnp.exp(bl), q_dec=q * eb, k_inv=k * enb, k_end=k * eend)


def _causal(c):
    return lax.broadcasted_iota(jnp.int32, (c, c), 0) >= lax.broadcasted_iota(jnp.int32, (c, c), 1)


def _gla_specs(heads, c, dk, dv, rev, n_chunks):
    def ch(n):
        return (n_chunks - 1 - n) if rev else n

    return [
        pl.BlockSpec((c, dk), lambda h, n: (ch(n), h)),
        pl.BlockSpec((c, dk), lambda h, n: (ch(n), heads + h)),
        pl.BlockSpec((c, dv), lambda h, n: (ch(n), heads + h)),
        pl.BlockSpec((c, LANE), lambda h, n: (ch(n), 0)),
        pl.BlockSpec((LANE, dk), lambda h, n: (0, h)),
        pl.BlockSpec((1, dk), lambda h, n: (0, h)),
    ]


def _gla_fwd(proj, a_tail, wg_p, bg, name):
    s = proj.shape[0]
    heads, c = GLA_HEADS, GLA_CHUNK
    dk = wg_p.shape[1] // heads
    dv = 2 * dk
    n_chunks = s // c
    scale = dk ** -0.5

    def body(q_ref, k_ref, v_ref, a_ref, wg_ref, bg_ref, o_ref, st_ref, state):
        @pl.when(pl.program_id(1) == 0)
        def _():
            state[...] = jnp.zeros_like(state)

        g = _gla_gates(q_ref, k_ref, a_ref, wg_ref, bg_ref, scale, c)
        v = v_ref[...]
        st = state[...]
        attn = jnp.where(_causal(c), _dot(g["q_dec"], g["k_inv"], tb=True), 0.0)
        o_ref[...] = _dot(attn, v) + _dot(g["q_dec"], st, tb=True)
        st_ref[...] = st.astype(st_ref.dtype)
        state[...] = g["dec"] * st + _dot(v, g["k_end"], ta=True)

    return _pcall(
        body, name=name, grid=(heads, n_chunks),
        in_specs=_gla_specs(heads, c, dk, dv, False, n_chunks),
        out_specs=[pl.BlockSpec((c, dv), lambda h, n: (n, h)),
                   pl.BlockSpec((None, None, dv, dk), lambda h, n: (h, n, 0, 0))],
        out_shape=[jax.ShapeDtypeStruct((s, heads * dv), F32),
                   jax.ShapeDtypeStruct((heads, n_chunks, dv, dk), BF16)],
        scratch_shapes=[pltpu.VMEM((dv, dk), F32)],
        compiler_params=_params(2),
    )(proj, proj, proj, a_tail, wg_p, bg)


def _gla_bwd(proj, a_tail, wg_p, bg, states, d_o, name):
    s = proj.shape[0]
    heads, c = GLA_HEADS, GLA_CHUNK
    dk = wg_p.shape[1] // heads
    dv = 2 * dk
    n_chunks = s // c
    scale = dk ** -0.5

    def body(q_ref, k_ref, v_ref, a_ref, wg_ref, bg_ref, st_ref, do_ref, dq_ref, dk_ref, dv_ref, dga_ref, dstate):
        @pl.when(pl.program_id(1) == 0)
        def _():
            dstate[...] = jnp.zeros_like(dstate)

        g = _gla_gates(q_ref, k_ref, a_ref, wg_ref, bg_ref, scale, c)
        v, st, dst, d_out = v_ref[...], st_ref[...], dstate[...], do_ref[...]
        q_dec, k_inv, k_end = g["q_dec"], g["k_inv"], g["k_end"]
        mask = _causal(c)
        attn = jnp.where(mask, _dot(q_dec, k_inv, tb=True), 0.0)
        d_attn = jnp.where(mask, _dot(d_out, v, tb=True), 0.0)
        d_qdec = _dot(d_attn, k_inv) + _dot(d_out, st)
        d_kinv = _dot(d_attn, q_dec, ta=True)
        d_kend = _dot(v, dst)
        dv_ref[...] = (_dot(attn, d_out, ta=True) + _dot(k_end, dst, tb=True)).astype(dv_ref.dtype)
        d_dec = jnp.sum(dst * st.astype(F32), axis=0, keepdims=True)
        dstate[...] = g["dec"] * dst + _dot(d_out, q_dec, ta=True)

        dq_ref[...] = (d_qdec * (scale * g["eb"])).astype(dq_ref.dtype)
        dk_ref[...] = (d_kinv * g["enb"] + d_kend * g["eend"]).astype(dk_ref.dtype)
        kk = d_kend * k_end
        db = d_qdec * q_dec - d_kinv * k_inv - kk
        dbl = jnp.sum(kk, axis=0, keepdims=True) + d_dec * g["dec"]
        last = lax.broadcasted_iota(jnp.int32, db.shape, 0) == c - 1
        db = db + jnp.where(last, dbl, 0.0)
        dla = _tri_matmul(_tri(c, upper=True), db)
        dga_ref[...] = dla * (1.0 / GLA_TAU) * _sigmoid(-g["ga"])

    rev = lambda h, n: (n_chunks - 1 - n, h)
    return _pcall(
        body, name=name, grid=(heads, n_chunks),
        in_specs=_gla_specs(heads, c, dk, dv, True, n_chunks) + [
            pl.BlockSpec((None, None, dv, dk), lambda h, n: (h, n_chunks - 1 - n, 0, 0)),
            pl.BlockSpec((c, dv), rev)],
        out_specs=[pl.BlockSpec((c, dk), rev), pl.BlockSpec((c, dk), rev), pl.BlockSpec((c, dv), rev),
                   pl.BlockSpec((c, dk), rev)],
        out_shape=[jax.ShapeDtypeStruct((s, heads * dk), BF16), jax.ShapeDtypeStruct((s, heads * dk), BF16),
                   jax.ShapeDtypeStruct((s, heads * dv), BF16), jax.ShapeDtypeStruct((s, heads * dk), F32)],
        scratch_shapes=[pltpu.VMEM((dv, dk), F32)],
        compiler_params=_params(2),
    )(proj, proj, proj, a_tail, wg_p, bg, states, d_o)


def _gla_post_fwd(o, r, gn, name):
    dvt = o.shape[1]
    dv = dvt // GLA_HEADS

    def fn(o, r, gn):
        outs = []
        for h in range(GLA_HEADS):
            sl = slice(h * dv, (h + 1) * dv)
            ohat, _ = _norm_stats(o[:, sl])
            outs.append((ohat * gn[:, sl]) * _silu(r[:, sl]))
        return (jnp.concatenate(outs, axis=1),)

    return _rowwise(fn, [("row", o), r, ("full", gn)], [("row", dvt, BF16)], name=name)[0]


def _gla_post_bwd(o, r, gn, dog, name):
    dvt = o.shape[1]
    dv = dvt // GLA_HEADS

    def fn(o, r, gn, dog):
        d_o, d_r, d_g = [], [], []
        for h in range(GLA_HEADS):
            sl = slice(h * dv, (h + 1) * dv)
            ohat, rstd = _norm_stats(o[:, sl])
            g, rr, dd = gn[:, sl], r[:, sl], dog[:, sl]
            d_r.append(dd * (ohat * g) * _dsilu(rr))
            don = dd * _silu(rr)
            d_g.append(_colsum(don * ohat))
            d_o.append(_norm_bwd(don * g, ohat, rstd))
        return jnp.concatenate(d_o, axis=1), jnp.concatenate(d_r, axis=1), jnp.concatenate(d_g, axis=1)

    return _rowwise(fn, [("row", o), r, ("full", gn), ("row", dog)],
                    [("row", dvt, F32), ("row", dvt, BF16), ("acc", dvt, F32)], name=name)


def _fox_prep(q, k, v, qg, kg, d, hd, name):
    heads = d // hd
    scale = hd ** -0.5

    def fn(q, k, v, qg, kg):
        qs, ks = [], []
        for h in range(heads):
            sl = slice(h * hd, (h + 1) * hd)
            qs.append(_norm_stats(q[:, sl])[0] * qg * scale)
            ks.append(_norm_stats(k[:, sl])[0] * kg)
        return jnp.concatenate(qs, axis=1), jnp.concatenate(ks, axis=1), v

    return _rowwise(fn, [q, k, v, ("full", qg), ("full", kg)],
                    [("row", d, BF16)] * 3, name=name)


def _fox_prep_bwd(q, k, dqn, dkn, qg, kg, hd, name):
    d = dqn.shape[1]
    heads = d // hd
    scale = hd ** -0.5

    def fn(q, k, dqn, dkn, qg, kg):
        dq, dk, gq, gk = [], [], [], []
        for h in range(heads):
            sl = slice(h * hd, (h + 1) * hd)
            for x, dxn, g, s, dl, gl in ((q, dqn, qg, scale, dq, gq), (k, dkn, kg, 1.0, dk, gk)):
                xhat, rstd = _norm_stats(x[:, sl])
                dn = dxn[:, sl] * s
                gl.append(_colsum(dn * xhat))
                dl.append(_norm_bwd(dn * g, xhat, rstd))
        cat = lambda t: jnp.concatenate(t, axis=1)
        return cat(dq), cat(dk), cat(gq), cat(gk)

    return _rowwise(fn, [q, k, ("row", dqn), ("row", dkn), ("full", qg), ("full", kg)],
                    [("row", d, BF16), ("row", d, BF16), ("acc", d, F32), ("acc", d, F32)], name=name)


def _fox_cum(fl, bf_p, name, tb=256):
    s = fl.shape[0]
    tb = _tile(s, tb)

    def body(fl_ref, bf_ref, cum_ref, carry):
        @pl.when(pl.program_id(0) == 0)
        def _():
            carry[...] = jnp.zeros_like(carry)

        lf = _log_sigmoid(fl_ref[...] + bf_ref[...])
        cum_ref[...] = _tri_matmul(_tri(tb), lf) + carry[...]
        carry[...] += _colsum(lf)

    return _pcall(
        body, name=name, grid=(s // tb,),
        in_specs=[pl.BlockSpec((tb, LANE), lambda i: (i, 0)), pl.BlockSpec((1, LANE), lambda i: (0, 0))],
        out_specs=pl.BlockSpec((tb, LANE), lambda i: (i, 0)),
        out_shape=jax.ShapeDtypeStruct((s, LANE), F32),
        scratch_shapes=[pltpu.VMEM((1, LANE), F32)],
        compiler_params=_params(1),
    )(fl, bf_p)


def _fox_cum_bwd(dcum, fl, bf_p, name, tb=256):
    s = fl.shape[0]
    tb = _tile(s, tb)
    nb = s // tb

    def body(dc_ref, fl_ref, bf_ref, dfl_ref, dbf_ref, carry):
        @pl.when(pl.program_id(0) == 0)
        def _():
            carry[...] = jnp.zeros_like(carry)
            dbf_ref[...] = jnp.zeros_like(dbf_ref)

        dc = dc_ref[...]
        dlf = _tri_matmul(_tri(tb, upper=True), dc) + carry[...]
        carry[...] += _colsum(dc)
        dfl = dlf * _sigmoid(-(fl_ref[...] + bf_ref[...]))
        dfl_ref[...] = dfl
        dbf_ref[...] += _colsum(dfl)

    rev = lambda i: (nb - 1 - i, 0)
    return _pcall(
        body, name=name, grid=(nb,),
        in_specs=[pl.BlockSpec((tb, LANE), rev), pl.BlockSpec((tb, LANE), rev), pl.BlockSpec((1, LANE), lambda i: (0, 0))],
        out_specs=[pl.BlockSpec((tb, LANE), rev), pl.BlockSpec((1, LANE), lambda i: (0, 0))],
        out_shape=[jax.ShapeDtypeStruct((s, LANE), F32), jax.ShapeDtypeStruct((1, LANE), F32)],
        scratch_shapes=[pltpu.VMEM((1, LANE), F32)],
        compiler_params=_params(1),
    )(dcum, fl, bf_p)


def _fox_attn_fwd(qn, kn, vb, cum_col, cum_row, hd, t, name):
    s, d = qn.shape
    heads = d // hd
    nq = s // t

    def body(q_ref, k_ref, v_ref, cc_ref, cr_ref, o_ref, lse_ref):
        qi = pl.program_id(1)
        q = q_ref[...]
        cq = cc_ref[...]
        qpos = qi * t + lax.broadcasted_iota(jnp.int32, (t, 1), 0)

        def step(kj, carry):
            m, l, acc = carry
            off = pl.multiple_of(kj * t, t)
            ks, vs = k_ref[pl.ds(off, t), :], v_ref[pl.ds(off, t), :]
            sc = _dot(q, ks, tb=True) + cq - cr_ref[kj]
            kpos = off + lax.broadcasted_iota(jnp.int32, (1, t), 1)
            sc = jnp.where(kpos <= qpos, sc, NEG)
            m_new = jnp.maximum(m, jnp.max(sc, axis=1, keepdims=True))
            alpha = jnp.exp(m - m_new)
            p = jnp.exp(sc - m_new)
            return m_new, alpha * l + jnp.sum(p, axis=1, keepdims=True), alpha * acc + _dot(p, vs)

        init = (jnp.full((t, 1), NEG, F32), jnp.zeros((t, 1), F32), jnp.zeros((t, hd), F32))
        m, l, acc = lax.fori_loop(0, qi + 1, step, init)
        o_ref[...] = acc / l
        lse_ref[...] = m + jnp.log(l)

    return _pcall(
        body, name=name, grid=(heads, nq),
        in_specs=[pl.BlockSpec((t, hd), lambda h, i: (i, h)),
                  pl.BlockSpec((s, hd), lambda h, i: (0, h)),
                  pl.BlockSpec((s, hd), lambda h, i: (0, h)),
                  pl.BlockSpec((None, t, 1), lambda h, i: (h, i, 0)),
                  pl.BlockSpec((None, nq, 1, t), lambda h, i: (h, 0, 0, 0))],
        out_specs=[pl.BlockSpec((t, hd), lambda h, i: (i, h)), pl.BlockSpec((None, t, 1), lambda h, i: (h, i, 0))],
        out_shape=[jax.ShapeDtypeStruct((s, d), F32), jax.ShapeDtypeStruct((heads, s, 1), F32)],
        compiler_params=_params(2),
    )(qn, kn, vb, cum_col, cum_row)


def _fox_attn_bwd(qn, kn, vb, d_o, o, lse, cum_col, cum_row, hd, t, name):
    s, d = qn.shape
    heads = d // hd
    nq = s // t

    def body(q_ref, k_ref, v_ref, do_ref, o_ref, lse_ref, cc_ref, cr_ref,
             dq_ref, dk_ref, dv_ref, dcq_ref, dck_ref, delta):
        kj = pl.program_id(1)

        @pl.when(kj == 0)
        def _():
            dq_ref[...] = jnp.zeros_like(dq_ref)
            dcq_ref[...] = jnp.zeros_like(dcq_ref)
            delta[...] = jnp.sum(do_ref[...] * o_ref[...], axis=1, keepdims=True)

        ks, vs, cr = k_ref[...], v_ref[...], cr_ref[...]
        kpos = kj * t + lax.broadcasted_iota(jnp.int32, (1, t), 1)

        def step(qi, carry):
            dk, dv, dck = carry
            rows = pl.ds(pl.multiple_of(qi * t, t), t)
            q, d_out = q_ref[rows, :], do_ref[rows, :]
            sc = _dot(q, ks, tb=True) + cc_ref[rows, :] - cr
            qpos = qi * t + lax.broadcasted_iota(jnp.int32, (t, 1), 0)
            p = jnp.where(kpos <= qpos, jnp.exp(sc - lse_ref[rows, :]), 0.0)
            ds = p * (_dot(d_out, vs, tb=True) - delta[rows, :])
            dq_ref[rows, :] += _dot(ds, ks)
            dcq_ref[rows, :] += jnp.sum(ds, axis=1, keepdims=True)
            return dk + _dot(ds, q, ta=True), dv + _dot(p, d_out, ta=True), dck + _colsum(ds)

        init = (jnp.zeros((t, hd), F32), jnp.zeros((t, hd), F32), jnp.zeros((1, t), F32))
        dk, dv, dck = lax.fori_loop(kj, nq, step, init)
        dk_ref[...] = dk.astype(dk_ref.dtype)
        dv_ref[...] = dv.astype(dv_ref.dtype)
        dck_ref[...] = dck

    head_rows = lambda h, j: (0, h)
    blk = lambda h, j: (j, h)
    return _pcall(
        body, name=name, grid=(heads, nq),
        in_specs=[pl.BlockSpec((s, hd), head_rows), pl.BlockSpec((t, hd), blk), pl.BlockSpec((t, hd), blk),
                  pl.BlockSpec((s, hd), head_rows), pl.BlockSpec((s, hd), head_rows),
                  pl.BlockSpec((None, s, 1), lambda h, j: (h, 0, 0)),
                  pl.BlockSpec((None, s, 1), lambda h, j: (h, 0, 0)),
                  pl.BlockSpec((None, None, 1, t), lambda h, j: (h, j, 0, 0))],
        out_specs=[pl.BlockSpec((s, hd), head_rows), pl.BlockSpec((t, hd), blk), pl.BlockSpec((t, hd), blk),
                   pl.BlockSpec((None, s, 1), lambda h, j: (h, 0, 0)),
                   pl.BlockSpec((None, None, 1, t), lambda h, j: (h, j, 0, 0))],
        out_shape=[jax.ShapeDtypeStruct((s, d), F32), jax.ShapeDtypeStruct((s, d), BF16),
                   jax.ShapeDtypeStruct((s, d), BF16), jax.ShapeDtypeStruct((heads, s, 1), F32),
                   jax.ShapeDtypeStruct((heads, nq, 1, t), F32)],
        scratch_shapes=[pltpu.VMEM((s, 1), F32)],
        compiler_params=_params(2),
    )(qn, kn, vb, d_o, o, lse, cum_col, cum_row)


def _fox_gate_fwd(o, og, name):
    def fn(o, og):
        return (o * _sigmoid(og),)

    return _rowwise(fn, [("row", o), og], [("row", o.shape[1], BF16)], name=name)[0]


def _fox_gate_bwd(o, og, dact, name):
    def fn(o, og, dact):
        sg = _sigmoid(og)
        return dact * sg, dact * o * sg * (1.0 - sg)

    d = o.shape[1]
    return _rowwise(fn, [("row", o), og, ("row", dact)], [("row", d, F32), ("row", d, BF16)], name=name)


def _shift_down(x, n):
    rows = lax.broadcasted_iota(jnp.int32, x.shape, 0)
    return jnp.where(rows >= n, pltpu.roll(x, n, 0), 0.0)


def _shift_up(x, n):
    rows = lax.broadcasted_iota(jnp.int32, x.shape, 0)
    return jnp.where(rows < x.shape[0] - n, pltpu.roll(x, x.shape[0] - n, 0), 0.0)


def _conv(u, w_ref, b):
    return w_ref[0:1, :] * _shift_down(u, 2) + w_ref[1:2, :] * _shift_down(u, 1) + w_ref[2:3, :] * u + b


def _conv_act_fwd(u, cw, cb, name, tc=256):
    s, two_f = u.shape
    dff = two_f // 2
    tc = _tile(dff, tc)
    nb = dff // tc

    def body(ug_ref, uv_ref, wg_ref, wv_ref, bg_ref, bv_ref, a_ref):
        gate = _conv(ug_ref[...], wg_ref, bg_ref[...])
        val = _conv(uv_ref[...], wv_ref, bv_ref[...])
        a_ref[...] = (_silu(gate) * val).astype(a_ref.dtype)

    lo, hi = (lambda j: (0, j)), (lambda j: (0, j + nb))
    return _pcall(
        body, name=name, grid=(nb,),
        in_specs=[pl.BlockSpec((s, tc), lo), pl.BlockSpec((s, tc), hi), pl.BlockSpec((3, tc), lo),
                  pl.BlockSpec((3, tc), hi), pl.BlockSpec((1, tc), lo), pl.BlockSpec((1, tc), hi)],
        out_specs=pl.BlockSpec((s, tc), lo),
        out_shape=jax.ShapeDtypeStruct((s, dff), BF16),
        compiler_params=_params(1),
    )(u, u, cw, cw, cb, cb)


def _conv_act_bwd(u, cw, cb, da, name, tc=128):
    s, two_f = u.shape
    dff = two_f // 2
    tc = _tile(dff, tc)
    nb = dff // tc

    def body(ug_ref, uv_ref, wg_ref, wv_ref, bg_ref, bv_ref, da_ref, du_ref, dw_ref, db_ref):
        ug, uv, da = ug_ref[...], uv_ref[...], da_ref[...]
        gate = _conv(ug, wg_ref, bg_ref[...])
        val = _conv(uv, wv_ref, bv_ref[...])
        sg = _sigmoid(gate)
        d_val = da * (gate * sg)
        d_gate = da * val * (sg * (1.0 + gate * (1.0 - sg)))
        for half, (dc, uu, w_ref) in enumerate(((d_gate, ug, wg_ref), (d_val, uv, wv_ref))):
            du = w_ref[0:1, :] * _shift_up(dc, 2) + w_ref[1:2, :] * _shift_up(dc, 1) + w_ref[2:3, :] * dc
            du_ref[half] = du.astype(du_ref.dtype)
            dw_ref[half, 0:1, :] = _colsum(dc * _shift_down(uu, 2))
            dw_ref[half, 1:2, :] = _colsum(dc * _shift_down(uu, 1))
            dw_ref[half, 2:3, :] = _colsum(dc * uu)
            db_ref[half] = _colsum(dc)

    lo, hi = (lambda j: (0, j)), (lambda j: (0, j + nb))
    both = lambda j: (0, 0, j)
    return _pcall(
        body, name=name, grid=(nb,),
        in_specs=[pl.BlockSpec((s, tc), lo), pl.BlockSpec((s, tc), hi), pl.BlockSpec((3, tc), lo),
                  pl.BlockSpec((3, tc), hi), pl.BlockSpec((1, tc), lo), pl.BlockSpec((1, tc), hi),
                  pl.BlockSpec((s, tc), lo)],
        out_specs=[pl.BlockSpec((2, s, tc), both), pl.BlockSpec((2, 3, tc), both), pl.BlockSpec((2, 1, tc), both)],
        out_shape=[jax.ShapeDtypeStruct((2, s, dff), BF16), jax.ShapeDtypeStruct((2, 3, dff), F32),
                   jax.ShapeDtypeStruct((2, 1, dff), F32)],
        compiler_params=_params(1),
    )(u, u, cw, cw, cb, cb, da)


def _adamw_math(w, g, m, v):
    m = ADAM_B1 * m + (1.0 - ADAM_B1) * g
    v = ADAM_B2 * v + (1.0 - ADAM_B2) * (g * g)
    m_hat = m / (1.0 - ADAM_B1 ** ADAM_STEP)
    v_hat = v / (1.0 - ADAM_B2 ** ADAM_STEP)
    delta = -ADAM_LR * (m_hat / (jnp.sqrt(v_hat) + ADAM_EPS) + ADAM_WD * w)
    return delta, m, v


def _adamw(w, g, m, v, name, tr=128):
    layers, r, c = w.shape
    pieces = isinstance(g, (list, tuple))
    tc = c
    if r % 8:
        tr, tc = r, _tile(c, max(LANE, 256 * 1024 // r // LANE * LANE))
    elif r <= tr:
        tr = r
    while r % tr:
        tr -= 8
    nr, nc = r // tr, c // tc
    g_list = list(g) if pieces else [g]
    n_pieces = g_list[0].shape[0] if pieces else 0

    def body(w_ref, *rest):
        g_refs, (m_ref, v_ref, go_ref, d_ref, mo_ref, vo_ref) = rest[:len(g_list)], rest[len(g_list):]

        def update(grad):
            delta, m_new, v_new = _adamw_math(w_ref[...], grad, m_ref[...], v_ref[...])
            go_ref[...], d_ref[...], mo_ref[...], vo_ref[...] = grad, delta, m_new, v_new

        if not pieces:
            update(g_refs[0][...])
            return
        for layer, g_ref in enumerate(g_refs):
            @pl.when(pl.program_id(0) == layer)
            def _(g_ref=g_ref):
                grad = g_ref[0].astype(F32)
                for i in range(1, n_pieces):
                    grad = grad + g_ref[i].astype(F32)
                update(grad)

    spec = pl.BlockSpec((None, tr, tc), lambda l, i, j: (l, i, j))
    if pieces:
        def walk(k):
            def index(l, i, j):
                here = l == k
                return (0, jnp.where(here, i, jnp.where(l < k, 0, nr - 1)), jnp.where(here, j, jnp.where(l < k, 0, nc - 1)))
            return index

        g_specs = [pl.BlockSpec((n_pieces, tr, tc), walk(k)) for k in range(layers)]
    else:
        g_specs = [spec]
    return _pcall(
        body, name=name, grid=(layers, nr, nc), in_specs=[spec] + g_specs + [spec, spec], out_specs=[spec] * 4,
        out_shape=[jax.ShapeDtypeStruct((layers, r, c), F32)] * 4, compiler_params=_params(3),
    )(w, *g_list, m, v)


def _sum8(x, name):
    p = x.shape[2]
    tp = _tile(p, 16 * 1024)

    def body(x_ref, o_ref):
        acc = x_ref[0]
        for i in range(1, N_DEV):
            acc = acc + x_ref[i]
        o_ref[...] = acc

    return _pcall(
        body, name=name, grid=(p // tp,), in_specs=[pl.BlockSpec((N_DEV, 1, tp), lambda i: (0, 0, i))],
        out_specs=pl.BlockSpec((1, tp), lambda i: (0, i)), out_shape=jax.ShapeDtypeStruct((1, p), x.dtype),
        compiler_params=_params(1),
    )(x)


def _exchange(arrays, name, scatter):
    n = len(arrays)
    hbm = pl.BlockSpec(memory_space=pl.ANY)

    def body(*refs):
        ins, outs, token = refs[:n], refs[n:2 * n], refs[2 * n]
        send_sems, recv_sems, local_sems = refs[2 * n + 1:]
        token[...] = jnp.zeros_like(token)
        x, y, c = lax.axis_index("x"), lax.axis_index("y"), lax.axis_index("c")
        me = 4 * x + 2 * y + c
        copies = []
        for a in range(n):
            src_mine = ins[a].at[me] if scatter else ins[a]
            local = pltpu.make_async_copy(src_mine, outs[a].at[me], local_sems.at[a])
            local.start()
            copies.append(local)
            for k in range(1, N_DEV):
                px = 1 - x if k & 4 else x
                py = 1 - y if k & 2 else y
                pc = 1 - c if k & 1 else c
                src = ins[a].at[4 * px + 2 * py + pc] if scatter else ins[a]
                cp = pltpu.make_async_remote_copy(
                    src_ref=src, dst_ref=outs[a].at[me],
                    send_sem=send_sems.at[a * (N_DEV - 1) + k - 1], recv_sem=recv_sems.at[a * (N_DEV - 1) + k - 1],
                    device_id=(px, py, pc), device_id_type=pl.DeviceIdType.MESH)
                cp.start()
                copies.append(cp)
        for cp in copies:
            cp.wait()

    out_shape = [jax.ShapeDtypeStruct(a.shape if scatter else (N_DEV,) + a.shape, a.dtype) for a in arrays]
    res = _pcall(
        body, name=name, in_specs=[hbm] * n, out_specs=[hbm] * n + [pl.BlockSpec(memory_space=pltpu.VMEM)],
        out_shape=out_shape + [jax.ShapeDtypeStruct((8, LANE), F32)],
        scratch_shapes=[pltpu.SemaphoreType.DMA((n * (N_DEV - 1),)), pltpu.SemaphoreType.DMA((n * (N_DEV - 1),)),
                        pltpu.SemaphoreType.DMA((n,))],
        compiler_params=pltpu.CompilerParams(has_side_effects=True),
    )(*arrays)
    return res[:n], res[n][0, 0]


_HBM = pl.BlockSpec(memory_space=pltpu.HBM)
_SEM = pl.BlockSpec(memory_space=pltpu.SEMAPHORE)
_DATAFLOW = pltpu.SideEffectType.DATAFLOW_SIDE_EFFECTING


def _peer(k, x, y, c):
    return (1 - x if k & 4 else x, 1 - y if k & 2 else y, 1 - c if k & 1 else c)


def _exchange_start(arrays, name, scatter):
    n = len(arrays)
    lands = [lax.empty(a.shape if scatter else (N_DEV,) + a.shape, a.dtype) for a in arrays]

    def body(*refs):
        srcs, dsts = refs[:n], refs[n:2 * n]
        send_sems, recv_sems, token = refs[4 * n:5 * n], refs[5 * n:6 * n], refs[6 * n]
        x, y, c = lax.axis_index("x"), lax.axis_index("y"), lax.axis_index("c")
        me = 4 * x + 2 * y + c
        for a in range(n):
            for k in range(1, N_DEV):
                px, py, pc = _peer(k, x, y, c)
                pltpu.make_async_remote_copy(
                    src_ref=srcs[a].at[4 * px + 2 * py + pc] if scatter else srcs[a], dst_ref=dsts[a].at[me],
                    send_sem=send_sems[a].at[k - 1], recv_sem=recv_sems[a].at[k - 1],
                    device_id=(px, py, pc), device_id_type=pl.DeviceIdType.MESH).start()
        token[...] = jnp.zeros_like(token)

    sems = [pltpu.SemaphoreType.DMA((N_DEV - 1,))] * (2 * n)
    res = _pcall(
        body, name=name,
        in_specs=[_HBM] * (2 * n),
        out_specs=[_HBM] * (2 * n) + [_SEM] * (2 * n) + [pl.BlockSpec(memory_space=pltpu.VMEM)],
        out_shape=[pltpu.HBM(a.shape, a.dtype) for a in arrays] + [pltpu.HBM(l.shape, l.dtype) for l in lands]
        + sems + [jax.ShapeDtypeStruct((8, LANE), F32)],
        input_output_aliases={i: i for i in range(2 * n)},
        compiler_params=pltpu.CompilerParams(has_side_effects=_DATAFLOW),
    )(*[pltpu.with_memory_space_constraint(a, pltpu.HBM) for a in arrays],
      *[pltpu.with_memory_space_constraint(l, pltpu.HBM) for l in lands])
    handles = [(res[a], res[n + a], res[2 * n + a], res[3 * n + a]) for a in range(n)]
    return handles, res[4 * n][0, 0]


def _exchange_wait(handles, after, name, scatter):
    n = len(handles)

    def body(*refs):
        srcs, dsts = refs[:n], refs[n:2 * n]
        send_sems, recv_sems = refs[2 * n:3 * n], refs[3 * n:4 * n]
        x, y, c = lax.axis_index("x"), lax.axis_index("y"), lax.axis_index("c")
        me = 4 * x + 2 * y + c
        for a in range(n):
            for k in range(1, N_DEV):
                cp = pltpu.make_async_remote_copy(
                    src_ref=srcs[a].at[me] if scatter else srcs[a], dst_ref=dsts[a].at[me],
                    send_sem=send_sems[a].at[k - 1], recv_sem=recv_sems[a].at[k - 1],
                    device_id=_peer(k, x, y, c), device_id_type=pl.DeviceIdType.MESH)
                cp.wait_send()
                cp.wait_recv()

    srcs, lands = [h[0] for h in handles], [h[1] for h in handles]
    res = _pcall(
        body, name=name,
        in_specs=[_HBM] * (2 * n) + [_SEM] * (2 * n) + [pl.BlockSpec(memory_space=pl.ANY)],
        out_specs=[_HBM] * (2 * n),
        out_shape=[pltpu.HBM(t.shape, t.dtype) for t in srcs + lands],
        input_output_aliases={i: i for i in range(2 * n)},
        compiler_params=pltpu.CompilerParams(has_side_effects=_DATAFLOW),
    )(*srcs, *lands, *[h[2] for h in handles], *[h[3] for h in handles], after)
    return res[:n], res[n:]


_ICI_PEERS = (2, 4, 6)


def _gather2_start(shards, name):
    n = len(shards)
    lands = [lax.empty((N_DEV,) + a.shape, a.dtype) for a in shards]

    def body(*refs):
        srcs, dsts = refs[:n], refs[n:2 * n]
        send_sems, d2d_sems, ici_sems = refs[4 * n:5 * n], refs[5 * n:6 * n], refs[6 * n:7 * n]
        token = refs[7 * n]
        x, y, c = lax.axis_index("x"), lax.axis_index("y"), lax.axis_index("c")
        me = 4 * x + 2 * y + c
        for a in range(n):
            for j, k in enumerate((1,) + _ICI_PEERS):
                recv = d2d_sems[a].at[0] if j == 0 else ici_sems[a].at[j - 1]
                pltpu.make_async_remote_copy(
                    src_ref=srcs[a], dst_ref=dsts[a].at[me], send_sem=send_sems[a].at[j], recv_sem=recv,
                    device_id=_peer(k, x, y, c), device_id_type=pl.DeviceIdType.MESH).start()
        token[...] = jnp.zeros_like(token)

    dma = pltpu.SemaphoreType.DMA
    res = _pcall(
        body, name=name,
        in_specs=[_HBM] * (2 * n),
        out_specs=[_HBM] * (2 * n) + [_SEM] * (3 * n) + [pl.BlockSpec(memory_space=pltpu.VMEM)],
        out_shape=[pltpu.HBM(a.shape, a.dtype) for a in shards] + [pltpu.HBM(l.shape, l.dtype) for l in lands]
        + [dma((4,))] * n + [dma((1,))] * n + [dma((3,))] * n + [jax.ShapeDtypeStruct((8, LANE), F32)],
        input_output_aliases={i: i for i in range(2 * n)},
        compiler_params=pltpu.CompilerParams(has_side_effects=_DATAFLOW),
    )(*[pltpu.with_memory_space_constraint(a, pltpu.HBM) for a in shards],
      *[pltpu.with_memory_space_constraint(l, pltpu.HBM) for l in lands])
    handles = [tuple(res[i * n + a] for i in range(5)) for a in range(n)]
    return handles, res[5 * n][0, 0]


def _gather2_forward(handle, after, name):
    src, land, send_sems, d2d_sem, ici_sems = handle

    def body(land_ref, ici_ref, after_ref, land_out, fwd_send, fwd_recv, token):
        x, y, c = lax.axis_index("x"), lax.axis_index("y"), lax.axis_index("c")
        for j, k in enumerate(_ICI_PEERS):
            px, py, pc = _peer(k, x, y, c)
            block = land_ref.at[4 * px + 2 * py + pc]
            pltpu.make_async_remote_copy(
                src_ref=block, dst_ref=block, send_sem=fwd_send.at[j], recv_sem=ici_ref.at[j],
                device_id=(px, py, pc), device_id_type=pl.DeviceIdType.MESH).wait_recv()
            pltpu.make_async_remote_copy(
                src_ref=block, dst_ref=block, send_sem=fwd_send.at[j], recv_sem=fwd_recv.at[j],
                device_id=(x, y, 1 - c), device_id_type=pl.DeviceIdType.MESH).start()
        token[...] = jnp.zeros_like(token)

    dma = pltpu.SemaphoreType.DMA
    land, fwd_send, fwd_recv, token = _pcall(
        body, name=name,
        in_specs=[_HBM, _SEM, pl.BlockSpec(memory_space=pl.ANY)],
        out_specs=[_HBM, _SEM, _SEM, pl.BlockSpec(memory_space=pltpu.VMEM)],
        out_shape=[pltpu.HBM(land.shape, land.dtype), dma((3,)), dma((3,)), jax.ShapeDtypeStruct((8, LANE), F32)],
        input_output_aliases={0: 0},
        compiler_params=pltpu.CompilerParams(has_side_effects=_DATAFLOW),
    )(land, ici_sems, after)
    return (src, land, send_sems, d2d_sem, fwd_send, fwd_recv), token[0, 0]


def _gather2_wait(handle, after, name):
    src, land, send_sems, d2d_sem, fwd_send, fwd_recv = handle

    def body(src_ref, land_ref, send_ref, d2d_ref, fsend_ref, frecv_ref, after_ref, src_out, land_out):
        x, y, c = lax.axis_index("x"), lax.axis_index("y"), lax.axis_index("c")
        me = 4 * x + 2 * y + c
        sibling = (x, y, 1 - c)
        block = land_ref.at[me]

        def copy(send, recv):
            return pltpu.make_async_remote_copy(src_ref=src_ref, dst_ref=block, send_sem=send, recv_sem=recv,
                                                device_id=sibling, device_id_type=pl.DeviceIdType.MESH)

        for j in range(4):
            copy(send_ref.at[j], d2d_ref.at[0]).wait_send()
        copy(send_ref.at[0], d2d_ref.at[0]).wait_recv()
        for j in range(3):
            copy(fsend_ref.at[j], frecv_ref.at[j]).wait_send()
            copy(fsend_ref.at[j], frecv_ref.at[j]).wait_recv()

    res = _pcall(
        body, name=name,
        in_specs=[_HBM, _HBM, _SEM, _SEM, _SEM, _SEM, pl.BlockSpec(memory_space=pl.ANY)],
        out_specs=[_HBM, _HBM],
        out_shape=[pltpu.HBM(src.shape, src.dtype), pltpu.HBM(land.shape, land.dtype)],
        input_output_aliases={0: 0, 1: 1},
        compiler_params=pltpu.CompilerParams(has_side_effects=_DATAFLOW),
    )(src, land, send_sems, d2d_sem, fwd_send, fwd_recv, after)
    return res[0], res[1]


def _with_own_block(land, mine, me):
    return lax.dynamic_update_slice(land, mine[None], (me,) + (0,) * mine.ndim)


def _pad_cols(x, width=LANE):
    return jnp.pad(x, ((0, 0), (0, width - x.shape[1])))


def _cols_full(g):
    return jnp.transpose(g, (1, 0, 2)).reshape(g.shape[1], -1)


def _cols_pieces(dw):
    k = dw.shape[0]
    return jnp.transpose(dw.reshape(k, N_DEV, -1), (1, 0, 2))


def _ffn_fwd(x1, p, i, tag):
    h2 = _adaln_fwd(x1, p["norm_ffn"][i], p["sc_f"][i], p["sh_f"][i], f"ffn_norm_{tag}")
    u = _matmul(h2, p["fetch"](f"up{i}", h2), name=f"ffn_up_{tag}", tn=1408, b_shards=True)
    a = _conv_act_fwd(u, p["conv_w"][i], p["conv_b"][i], f"ffn_act_{tag}")
    g_f = p["g_f"][i]
    x2, f = _matmul(a, p["fetch"](f"down{i}", a), name=f"ffn_down_{tag}", tk=512, out_dtypes=(F32, F32),
                    epilogue=lambda acc, x1, g: (x1 + (1.0 + g) * acc, acc), extras=(("mn", x1), ("n", g_f)))
    return x2, dict(h2=h2, u=u, a=a, f=f)


def _ffn_bwd(dx2, x1, saved, p, i, tag):
    d = x1.shape[1]
    w_up, w_down = p["fetch"](f"up{i}", None), p["fetch"](f"down{i}", None)
    df, dg_f = _residual_bwd(dx2, saved["f"], p["g_f"][i], f"ffn_res_bwd_{tag}")
    da = _matmul(df, w_down, tb=True, name=f"ffn_down_dx_{tag}", tn=512)
    dw_down = _matmul(saved["a"], df, ta=True, name=f"ffn_down_dw_{tag}", tm=1408, out_dtypes=(BF16,))
    du, dcw, dcb = _conv_act_bwd(saved["u"], p["conv_w"][i], p["conv_b"][i], da, f"ffn_act_bwd_{tag}")
    dcw, dcb = (jnp.concatenate([t[0], t[1]], axis=1) for t in (dcw, dcb))
    dh2 = _matmul(du, w_up, tb=True, name=f"ffn_up_dx_{tag}", tk=1408, a_halves=True, b_shards=True)
    dw_up = _matmul(saved["h2"], du, ta=True, name=f"ffn_up_dw_{tag}", tn=1408, out_dtypes=(BF16,), b_halves=True,
                    out_shards=True)
    tok = p["send"](f"ffn{i}", [dw_up, dw_down.reshape(N_DEV, -1, d)])
    dx1, dsh, dsc, dgain = _adaln_bwd(x1, dh2, dx2, p["norm_ffn"][i] + tok, p["sc_f"][i], f"ffn_norm_bwd_{tag}")
    grads = dict(conv_w=dcw, conv_b=dcb, norm_ffn=dgain, sh_f=dsh, sc_f=dsc, g_f=dg_f)
    return dx1, grads


def _gla_layer_fwd(x, p, i):
    h1 = _adaln_fwd(x, p["norm_mix"][i], p["sc_m"][i], p["sh_m"][i], "gla_norm")
    w_t, w_tail_t, main = p["fetch"]("gla_in", h1)
    proj = _matmul(h1, w_t, tb=True, b_rows=main, name="gla_in")
    a_tail = _matmul(h1, w_tail_t, tb=True, name="gla_in_tail")
    dk_total = p["gla_wg_p"].shape[1]
    o, states = _gla_fwd(proj, a_tail, p["gla_wg_p"], p["gla_b_gate"], "gla_chunks")
    assert 2 * dk_total == o.shape[1]
    r = ("cols", proj, 2, o.shape[1])
    og = _gla_post_fwd(o, r, p["gla_norm"], "gla_post")
    x1, y = _matmul(og, p["fetch"]("gla_out", og), name="gla_out", out_dtypes=(F32, F32),
                    epilogue=lambda acc, x, g: (x + (1.0 + g) * acc, acc), extras=(("mn", x), ("n", p["g_m"][i])))
    return x1, dict(h1=h1, proj=proj, a_tail=a_tail, o=o, r=r, states=states, og=og, y=y)


def _gla_layer_bwd(dx1, x, sv, p, i):
    d = x.shape[1]
    (w_t, w_tail_t, main), w_out = p["fetch"]("gla_in", None), p["fetch"]("gla_out", None)
    dy, dg_m = _residual_bwd(dx1, sv["y"], p["g_m"][i], "gla_res_bwd")
    dog = _matmul(dy, w_out, tb=True, name="gla_out_dx")
    dw_out = _matmul(sv["og"], dy, ta=True, name="gla_out_dw", out_dtypes=(BF16,))
    tok = p["send"]("gla_out", [dw_out.reshape(N_DEV, -1, d)])
    d_o, d_r, dgn = _gla_post_bwd(sv["o"], sv["r"], p["gla_norm"] + tok, dog, "gla_post_bwd")
    dq, dk, dv, dga = _gla_bwd(sv["proj"], sv["a_tail"], p["gla_wg_p"], p["gla_b_gate"], sv["states"], d_o,
                               "gla_chunks_bwd")
    da_tail = _matmul(dga, p["gla_wg_p"], tb=True, name="gla_gate_dx", out_dtypes=(BF16,))
    dwg = _matmul(sv["a_tail"], dga, ta=True, name="gla_gate_dw")
    dbg = _rowwise(lambda t: (_colsum(t),), [("row", dga)], [("acc", dga.shape[1], F32)], name="gla_gate_db")[0]
    dproj = jnp.concatenate([dq, dk, dv, d_r], axis=1)
    dh_tail = _matmul(da_tail, w_tail_t, name="gla_in_tail_dx")
    dh1 = _matmul(dproj, w_t, b_rows=main, name="gla_in_dx", tk=1024,
                  epilogue=lambda acc, t: (acc + t,), extras=(("mn", dh_tail),))
    dw_main = _matmul(dproj, sv["h1"], ta=True, name="gla_in_dw", out_dtypes=(BF16,))
    dw_tail = _matmul(da_tail, sv["h1"], ta=True, name="gla_in_tail_dw", out_dtypes=(BF16,))
    rank = p["gla_rank"]
    dx, dsh, dsc, dgain = _adaln_bwd(x, dh1, dx1, p["norm_mix"][i], p["sc_m"][i], "gla_norm_bwd")
    grads = dict(gla_w_gate=dwg[:rank], gla_b_gate=dbg, gla_norm=dgn, norm_mix=dgain, sh_m=dsh, sc_m=dsc, g_m=dg_m,
                 gla_w_in_unsent=(dw_main, dw_tail[:rank]))
    return dx, grads


def _fox_layer_fwd(x, p, i):
    d = x.shape[1]
    hd = p["fox_q_norm"].shape[1]
    heads = d // hd
    s = x.shape[0]
    t = _tile(s, 512)
    h1 = _adaln_fwd(x, p["norm_mix"][i], p["sc_m"][i], p["sh_m"][i], "fox_norm")
    w_t, w_tail_t, main = p["fetch"]("fox_in", h1)
    proj = _matmul(h1, w_t, tb=True, b_rows=main, name="fox_in")
    fl = _matmul(h1, w_tail_t, tb=True, name="fox_in_tail")
    q, k, v, og = (("cols", proj, j, d) for j in range(4))
    qn, kn, vb = _fox_prep(q, k, v, p["fox_q_norm"], p["fox_k_norm"], d, hd, "fox_prep")
    cum = _fox_cum(fl, p["fox_bf_p"], "fox_cum")
    cum_t = jnp.transpose(cum[:, :heads])
    cum_col, cum_row = cum_t[:, :, None], cum_t.reshape(heads, s // t, 1, t)
    o, lse = _fox_attn_fwd(qn, kn, vb, cum_col, cum_row, hd, t, "fox_attn")
    act = _fox_gate_fwd(o, og, "fox_gate")
    x1, y = _matmul(act, p["fetch"]("fox_out", act), name="fox_out", out_dtypes=(F32, F32),
                    epilogue=lambda acc, x, g: (x + (1.0 + g) * acc, acc), extras=(("mn", x), ("n", p["g_m"][i])))
    return x1, dict(h1=h1, q=q, k=k, og=og, fl=fl, qn=qn, kn=kn, vb=vb, cum_col=cum_col, cum_row=cum_row,
                    o=o, lse=lse, act=act, y=y, t=t, hd=hd)


def _fox_layer_bwd(dx1, x, sv, p, i):
    d = x.shape[1]
    hd, t = sv["hd"], sv["t"]
    heads = d // hd
    s = x.shape[0]
    (w_t, w_tail_t, main), w_out = p["fetch"]("fox_in", None), p["fetch"]("fox_out", None)
    dy, dg_m = _residual_bwd(dx1, sv["y"], p["g_m"][i], "fox_res_bwd")
    dact = _matmul(dy, w_out, tb=True, name="fox_out_dx")
    dw_out = _matmul(sv["act"], dy, ta=True, name="fox_out_dw", out_dtypes=(BF16,))
    d_o, d_og = _fox_gate_bwd(sv["o"], sv["og"], dact, "fox_gate_bwd")
    dqn, dkn, dvb, dcq, dck = _fox_attn_bwd(sv["qn"], sv["kn"], sv["vb"], d_o, sv["o"], sv["lse"], sv["cum_col"],
                                            sv["cum_row"], hd, t, "fox_attn_bwd")
    dq, dk, gq, gk = _fox_prep_bwd(sv["q"], sv["k"], dqn, dkn, p["fox_q_norm"], p["fox_k_norm"], hd, "fox_prep_bwd")
    dcum = _pad_cols(jnp.transpose(dcq[:, :, 0] - dck.reshape(heads, s)))
    dfl, dbf = _fox_cum_bwd(dcum, sv["fl"], p["fox_bf_p"], "fox_cum_bwd")
    dfl_b = dfl.astype(BF16)
    dproj = jnp.concatenate([dq, dk, dvb, d_og], axis=1)
    dh_tail = _matmul(dfl_b, w_tail_t, name="fox_in_tail_dx")
    dh1 = _matmul(dproj, w_t, b_rows=main, name="fox_in_dx", tk=1024,
                  epilogue=lambda acc, tl: (acc + tl,), extras=(("mn", dh_tail),))
    dw_main = _matmul(dproj, sv["h1"], ta=True, name="fox_in_dw", out_dtypes=(BF16,))
    dw_tail = _matmul(dfl_b, sv["h1"], ta=True, name="fox_in_tail_dw", out_dtypes=(BF16,))
    dw_in = jnp.concatenate([dw_main, dw_tail[:heads]], axis=0).reshape(N_DEV, -1, d)
    tok = p["send"]("fox", [dw_in, dw_out.reshape(N_DEV, -1, d)])
    dx, dsh, dsc, dgain = _adaln_bwd(x, dh1, dx1, p["norm_mix"][i] + tok, p["sc_m"][i], "fox_norm_bwd")
    grads = dict(fox_b_f=dbf[:, :heads], fox_q_norm=gq.reshape(heads, hd).sum(0, keepdims=True),
                 fox_k_norm=gk.reshape(heads, hd).sum(0, keepdims=True), norm_mix=dgain, sh_m=dsh, sc_m=dsc, g_m=dg_m)
    return dx, grads


SMALL = ("b_mod", "norm_mix", "norm_ffn", "gla_b_gate", "gla_norm", "fox_b_f", "fox_q_norm", "fox_k_norm",
         "ffn_conv_b", "norm_final")
SMALL_SHARDED = ("gla_w_gate", "ffn_conv_w")
BIG = ("gla_w_in", "gla_w_out", "fox_w_in", "fox_w_out", "ffn_w_up", "ffn_w_down")
WEIGHTS = ("w_mod", "b_mod", "norm_mix", "norm_ffn", "gla_w_in", "gla_w_gate", "gla_b_gate", "gla_norm", "gla_w_out",
           "fox_w_in", "fox_b_f", "fox_q_norm", "fox_k_norm", "fox_w_out", "ffn_w_up", "ffn_conv_w", "ffn_conv_b",
           "ffn_w_down", "norm_final")


def _pack(parts):
    flat = jnp.concatenate([p.reshape(-1) for p in parts])
    pad = (-flat.shape[0]) % 1024
    return jnp.pad(flat, (0, pad)).reshape(1, -1)


def _unpack(flat, shapes):
    out, off = [], 0
    for shp in shapes:
        n = 1
        for s in shp:
            n *= s
        out.append(flat[0, off:off + n].reshape(shp))
        off += n
    return out


def kernel(x, c, w_mod, b_mod, norm_mix, norm_ffn, gla_w_in, gla_w_gate, gla_b_gate, gla_norm, gla_w_out, fox_w_in, fox_b_f, fox_q_norm, fox_k_norm, fox_w_out, ffn_w_up, ffn_conv_w, ffn_conv_b, ffn_w_down, norm_final, loss_target, m_w_mod, m_b_mod, m_norm_mix, m_norm_ffn, m_gla_w_in, m_gla_w_gate, m_gla_b_gate, m_gla_norm, m_gla_w_out, m_fox_w_in, m_fox_b_f, m_fox_q_norm, m_fox_k_norm, m_fox_w_out, m_ffn_w_up, m_ffn_conv_w, m_ffn_conv_b, m_ffn_w_down, m_norm_final, v_w_mod, v_b_mod, v_norm_mix, v_norm_ffn, v_gla_w_in, v_gla_w_gate, v_gla_b_gate, v_gla_norm, v_gla_w_out, v_fox_w_in, v_fox_b_f, v_fox_q_norm, v_fox_k_norm, v_fox_w_out, v_ffn_w_up, v_ffn_conv_w, v_ffn_conv_b, v_ffn_w_down, v_norm_final):
    w = dict(w_mod=w_mod, b_mod=b_mod, norm_mix=norm_mix, norm_ffn=norm_ffn, gla_w_in=gla_w_in, gla_w_gate=gla_w_gate,
             gla_b_gate=gla_b_gate, gla_norm=gla_norm, gla_w_out=gla_w_out, fox_w_in=fox_w_in, fox_b_f=fox_b_f,
             fox_q_norm=fox_q_norm, fox_k_norm=fox_k_norm, fox_w_out=fox_w_out, ffn_w_up=ffn_w_up,
             ffn_conv_w=ffn_conv_w, ffn_conv_b=ffn_conv_b, ffn_w_down=ffn_w_down, norm_final=norm_final)
    mom_m = dict(w_mod=m_w_mod, b_mod=m_b_mod, norm_mix=m_norm_mix, norm_ffn=m_norm_ffn, gla_w_in=m_gla_w_in,
                 gla_w_gate=m_gla_w_gate, gla_b_gate=m_gla_b_gate, gla_norm=m_gla_norm, gla_w_out=m_gla_w_out,
                 fox_w_in=m_fox_w_in, fox_b_f=m_fox_b_f, fox_q_norm=m_fox_q_norm, fox_k_norm=m_fox_k_norm,
                 fox_w_out=m_fox_w_out, ffn_w_up=m_ffn_w_up, ffn_conv_w=m_ffn_conv_w, ffn_conv_b=m_ffn_conv_b,
                 ffn_w_down=m_ffn_w_down, norm_final=m_norm_final)
    mom_v = dict(w_mod=v_w_mod, b_mod=v_b_mod, norm_mix=v_norm_mix, norm_ffn=v_norm_ffn, gla_w_in=v_gla_w_in,
                 gla_w_gate=v_gla_w_gate, gla_b_gate=v_gla_b_gate, gla_norm=v_gla_norm, gla_w_out=v_gla_w_out,
                 fox_w_in=v_fox_w_in, fox_b_f=v_fox_b_f, fox_q_norm=v_fox_q_norm, fox_k_norm=v_fox_k_norm,
                 fox_w_out=v_fox_w_out, ffn_w_up=v_ffn_w_up, ffn_conv_w=v_ffn_conv_w, ffn_conv_b=v_ffn_conv_b,
                 ffn_w_down=v_ffn_w_down, norm_final=v_norm_final)

    me = 4 * lax.axis_index("x") + 2 * lax.axis_index("y") + lax.axis_index("c")
    xs, target = x[0], loss_target[0]
    s, d = xs.shape
    depth = w_mod.shape[0]
    mod_cols = w_mod.shape[2]
    rank = gla_w_gate.shape[1]
    hd = fox_q_norm.shape[1]
    fox_heads = d // hd
    dk_total = gla_w_gate.shape[2] * N_DEV

    cond = c * (1.0 / (1.0 + jnp.exp(-c)))
    g, _ = _exchange([gla_w_gate[0], ffn_conv_w, cond], "gather_small", scatter=False)
    cond_all = g[2][:, 0, :]

    cond_pad = jnp.pad(cond_all, ((0, 16 - N_DEV), (0, 0)))
    mod_part = []
    for i in range(depth):
        b_cols = lax.dynamic_slice(b_mod[i:i + 1], (0, me * mod_cols), (1, mod_cols))
        mod_part.append(_matmul(cond_pad, w_mod[i], name=f"mod_{i}", tn=768,
                                epilogue=lambda acc, b: (acc + b,), extras=(("n", b_cols),))[:N_DEV])
    (mod_all,), tok_mod = _exchange([jnp.stack(mod_part)], "gather_mod", scatter=False)
    mod = lax.dynamic_index_in_dim(mod_all, me, axis=2, keepdims=False)
    mod = jnp.transpose(mod, (1, 0, 2)).reshape(depth, 6, 1, d)

    big_names = ["gla_in", "gla_out", "up0", "down0", "fox_in", "fox_out", "up1", "down1"]
    big_shards = [jnp.transpose(gla_w_in[0] + tok_mod), gla_w_out[0], ffn_w_up[0], ffn_w_down[0],
                  jnp.transpose(fox_w_in[0]), fox_w_out[0], ffn_w_up[1], ffn_w_down[1]]
    big_shards = [t.astype(BF16) for t in big_shards]
    handles, tok0 = _gather2_start(big_shards, "gather_weights_start")
    ready, forwarded = {}, {}

    def split_tail(full_t, tail):
        main = full_t.shape[0] - tail
        return full_t, jnp.pad(full_t[main:], ((0, LANE - tail), (0, 0))), main

    def forward(idx, after):
        key = big_names[idx]
        forwarded[key] = _gather2_forward(handles[idx], after, f"gather_{key}_forward")

    def fetch(key, after):
        if key not in ready:
            idx = big_names.index(key)
            if idx == 0:
                forward(0, after)
            handle, _ = forwarded[key]
            mine, land = _gather2_wait(handle, after, f"gather_{key}_wait")
            tok = 0.0
            if idx + 1 < len(big_names):
                forward(idx + 1, land)
                tok = forwarded[big_names[idx + 1]][1]
            full = _with_own_block(land, mine + jnp.asarray(tok, F32).astype(BF16), me)
            if key == "gla_in":
                ready[key] = split_tail(full.reshape(-1, d), rank)
            elif key == "fox_in":
                ready[key] = split_tail(full.reshape(-1, d), fox_heads)
            elif key.startswith("up"):
                ready[key] = full
            else:
                ready[key] = full.reshape(-1, d)
        return ready[key]

    sent = {}

    def send(key, pieces):
        hs, tok = _exchange_start(pieces, f"scatter_{key}_start", scatter=True)
        sent[key] = hs
        return tok

    p = dict(
        fetch=fetch, send=send,
        gla_wg_p=jnp.pad(_cols_full(g[0]), ((0, LANE - rank), (0, 0))),
        conv_w=[jnp.transpose(g[1][:, i], (1, 0, 2)).reshape(ffn_conv_w.shape[1], -1) for i in range(depth)],
        conv_b=[ffn_conv_b[i:i + 1] for i in range(depth)],
        gla_b_gate=gla_b_gate, gla_norm=gla_norm, fox_q_norm=fox_q_norm, fox_k_norm=fox_k_norm,
        fox_bf_p=_pad_cols(fox_b_f), gla_rank=rank,
        norm_mix=[norm_mix[i:i + 1] + (tok0 if i == 0 else 0.0) for i in range(depth)],
        norm_ffn=[norm_ffn[i:i + 1] for i in range(depth)],
    )

    for j, nm in enumerate(("sh_m", "sc_m", "g_m", "sh_f", "sc_f", "g_f")):
        p[nm] = [mod[i, j] for i in range(depth)]

    acts, saved = [xs], []
    for i in range(depth):
        layer_fwd = _gla_layer_fwd if i % 2 == 0 else _fox_layer_fwd
        x1, sv_mix = layer_fwd(acts[-1], p, i)
        x2, sv_ffn = _ffn_fwd(x1, p, i, str(i))
        saved.append((acts[-1], x1, sv_mix, sv_ffn))
        acts.append(x2)
    dx, d_norm_final, loss_part = _final_loss(acts[-1], target, norm_final.reshape(1, d), "final_loss")

    lg = [None] * depth
    for i in reversed(range(depth)):
        x_in, x1, sv_mix, sv_ffn = saved[i]
        dx, g_ffn = _ffn_bwd(dx, x1, sv_ffn, p, i, str(i))
        layer_bwd = _gla_layer_bwd if i % 2 == 0 else _fox_layer_bwd
        dx, g_mix = layer_bwd(dx, x_in, sv_mix, p, i)
        lg[i] = {**g_ffn, **g_mix}
    grad_x = dx[None]

    gla_l = [i for i in range(depth) if i % 2 == 0]
    fox_l = [i for i in range(depth) if i % 2 == 1]
    small_parts = dict(
        norm_mix=jnp.concatenate([lg[i]["norm_mix"] for i in range(depth)]),
        norm_ffn=jnp.concatenate([lg[i]["norm_ffn"] for i in range(depth)]),
        gla_b_gate=jnp.concatenate([lg[i]["gla_b_gate"] for i in gla_l]),
        gla_norm=jnp.concatenate([lg[i]["gla_norm"] for i in gla_l]),
        fox_b_f=jnp.concatenate([lg[i]["fox_b_f"] for i in fox_l]),
        fox_q_norm=jnp.concatenate([lg[i]["fox_q_norm"] for i in fox_l]),
        fox_k_norm=jnp.concatenate([lg[i]["fox_k_norm"] for i in fox_l]),
        ffn_conv_b=jnp.concatenate([lg[i]["conv_b"] for i in range(depth)]),
        norm_final=d_norm_final,
        gla_w_gate=jnp.stack([lg[i]["gla_w_gate"] for i in gla_l]),
        ffn_conv_w=jnp.stack([lg[i]["conv_w"] for i in range(depth)]),
        loss=loss_part[:, :1],
    )
    order = ("norm_mix", "norm_ffn", "gla_b_gate", "gla_norm", "fox_b_f", "fox_q_norm", "fox_k_norm", "ffn_conv_b",
             "norm_final", "gla_w_gate", "ffn_conv_w", "loss")
    packed = _pack([small_parts[nm] for nm in order])
    dmod = jnp.stack([jnp.concatenate([lg[i][nm] for nm in ("sh_m", "sc_m", "g_m", "sh_f", "sc_f", "g_f")], axis=1)
                      for i in range(depth)])
    (packed_all, dmod_all), tok_small = _exchange([packed, dmod], "gather_small_grads", scatter=False)
    dw_main, dw_tail = lg[0]["gla_w_in_unsent"]
    dw_in_t = jnp.concatenate([dw_main, dw_tail + tok_small.astype(BF16)], axis=0)
    tok_last = send("gla_in", [dw_in_t.reshape(N_DEV, -1, d)])
    summed = _unpack(_sum8(packed_all + tok_last, "sum_small_grads"), [small_parts[nm].shape for nm in order])
    small_g = dict(zip(order, summed))
    loss = small_g["loss"][0, 0]
    dmod_all = dmod_all[:, :, 0, :]

    grads = {}
    cond_t = _pad_cols(jnp.transpose(cond_all)).astype(BF16)
    dmod_cols = lax.dynamic_slice(dmod_all, (0, 0, me * mod_cols), (N_DEV, depth, mod_cols))
    g_w_mod = []
    for i in range(depth):
        rhs = jnp.pad(dmod_cols[:, i], ((0, LANE - N_DEV), (0, 0)))
        g_w_mod.append(_matmul(cond_t, rhs, name=f"mod_dw_{i}", tn=768))
    grads["w_mod"] = jnp.stack(g_w_mod)
    small_g["b_mod"] = _sum8(dmod_all.reshape(N_DEV, 1, -1), "sum_b_mod").reshape(depth, -1)

    received = {}

    def arrive(key, after):
        pieces, lands = _exchange_wait(sent[key], after, f"scatter_{key}_wait", scatter=True)
        received[key] = [_with_own_block(land, lax.dynamic_index_in_dim(pc, me, 0, keepdims=False), me)
                         for land, pc in zip(lands, pieces)]

    for key in ("ffn1", "fox", "ffn0", "gla_out"):
        arrive(key, summed[0])

    out_g, out_d, out_m, out_v = {}, {}, {}, {}

    def update(nm, g_arr, transposed=False):
        swap = (lambda t: jnp.transpose(t, (0, 2, 1))) if transposed else (lambda t: t)
        res = _adamw(swap(w[nm]), g_arr, swap(mom_m[nm]), swap(mom_v[nm]), f"adamw_{nm}")
        out_g[nm], out_d[nm], out_m[nm], out_v[nm] = (swap(t) for t in res)

    update("gla_w_out", [received["gla_out"][0]])
    update("fox_w_in", [received["fox"][0]], transposed=True)
    update("fox_w_out", [received["fox"][1]])
    update("ffn_w_up", [received[f"ffn{i}"][0] for i in range(depth)])
    update("ffn_w_down", [received[f"ffn{i}"][1] for i in range(depth)])
    update("w_mod", grads["w_mod"])

    gate_cols = gla_w_gate.shape[2]
    conv_cols = ffn_conv_w.shape[2]
    local_small = dict(small_g)
    local_small["gla_w_gate"] = lax.dynamic_slice_in_dim(small_g["gla_w_gate"], me * gate_cols, gate_cols, axis=2)
    local_small["ffn_conv_w"] = lax.dynamic_slice_in_dim(small_g["ffn_conv_w"], me * conv_cols, conv_cols, axis=2)
    names = SMALL + SMALL_SHARDED
    shapes = [w[nm].shape for nm in names]
    res = _adamw(_pack([w[nm] for nm in names])[None], _pack([local_small[nm] for nm in names])[None],
                 _pack([mom_m[nm] for nm in names])[None], _pack([mom_v[nm] for nm in names])[None], "adamw_small")
    for tgt, flat in zip((out_g, out_d, out_m, out_v), res):
        for nm, arr in zip(names, _unpack(flat[0], shapes)):
            tgt[nm] = arr

    done = sum(out_g[nm][0, 0, :1] for nm in ("gla_w_out", "fox_w_in", "fox_w_out", "ffn_w_up", "ffn_w_down", "w_mod"))
    arrive("gla_in", (done + res[1][0, 0, :1]).reshape(1, 1))
    update("gla_w_in", [received["gla_in"][0]], transposed=True)

    return (loss, grad_x, *[out_g[n] for n in WEIGHTS], *[out_d[n] for n in WEIGHTS],
            *[out_m[n] for n in WEIGHTS], *[out_v[n] for n in WEIGHTS])
```

```python
import jax
import jax.numpy as jnp
from jax import lax
from jax.experimental import pallas as pl
from jax.experimental.pallas import tpu as pltpu

F32, BF16 = jnp.float32, jnp.bfloat16
N_DEV = 8
GLA_HEADS = 4
GLA_TAU = 16.0
GLA_CHUNK = 64
NORM_EPS = 1e-6
ADAM_LR, ADAM_B1, ADAM_B2, ADAM_EPS, ADAM_WD, ADAM_STEP = 0.001, 0.9, 0.999, 1e-08, 0.01, 10
LANE = 128
VMEM_LIMIT = 56 * 1024 * 1024
NEG = -1e30


def _pcall(body, **kw):
    return pl.pallas_call(body, **kw)


def _params(n_axes):
    return pltpu.CompilerParams(dimension_semantics=("arbitrary",) * n_axes, vmem_limit_bytes=VMEM_LIMIT)


def _tile(dim, pref):
    if dim <= pref:
        return dim
    t = pref
    while dim % t:
        t -= LANE
    assert t > 0, (dim, pref)
    return t


def _dot(a, b, ta=False, tb=False):
    dims = (((0,) if ta else (1,), (1,) if tb else (0,)), ((), ()))
    return lax.dot_general(a.astype(BF16), b.astype(BF16), dims, preferred_element_type=F32)


def _split3(x):
    hi = x.astype(BF16)
    r1 = x - hi.astype(F32)
    mid = r1.astype(BF16)
    lo = (r1 - mid.astype(F32)).astype(BF16)
    return hi, mid, lo


def _tri_matmul(tri, x):
    hi, mid, lo = _split3(x)
    return _dot(tri, hi) + _dot(tri, mid) + _dot(tri, lo)


def _tri(n, upper=False):
    r = lax.broadcasted_iota(jnp.int32, (n, n), 0)
    c = lax.broadcasted_iota(jnp.int32, (n, n), 1)
    return jnp.where((r <= c) if upper else (r >= c), 1.0, 0.0).astype(BF16)


def _log_sigmoid(x):
    return jnp.minimum(x, 0.0) - jnp.log(1.0 + jnp.exp(-jnp.abs(x)))


def _sigmoid(x):
    return 1.0 / (1.0 + jnp.exp(-x))


def _silu(x):
    return x * _sigmoid(x)


def _dsilu(x):
    s = _sigmoid(x)
    return s * (1.0 + x * (1.0 - s))


def _matmul(a, b, *, name, ta=False, tb=False, out_dtypes=(F32,), tm=1024, tn=1024, tk=2048,
            epilogue=None, extras=(), a_halves=False, b_halves=False, b_shards=False, out_shards=False,
            b_rows=None):
    if a_halves:
        assert not ta
        m, k = a.shape[1], 2 * a.shape[2]
    else:
        m, k = (a.shape[1], a.shape[0]) if ta else a.shape
    if b_halves:
        assert not tb and b.shape[1] == k
        n = 2 * b.shape[2]
    elif b_shards:
        n = b.shape[1] if tb else N_DEV * b.shape[2]
        assert (N_DEV * b.shape[2] if tb else b.shape[1]) == k, (a.shape, b.shape, ta, tb)
    else:
        rows = b.shape[0] if b_rows is None else b_rows
        n = rows if tb else b.shape[1]
        assert (b.shape[1] if tb else rows) == k, (a.shape, b.shape, ta, tb)
    n_unit = n // N_DEV if (out_shards or (b_shards and not tb)) else (n // 2 if b_halves else n)
    k_unit = k // N_DEV if (b_shards and tb) else (k // 2 if a_halves else k)
    tm, tn, tk = _tile(m, tm), _tile(n_unit, tn), _tile(k_unit, tk)
    nk = k // tk
    if a_halves:
        a_spec = pl.BlockSpec((None, tm, tk), lambda i, j, kk: (kk // (nk // 2), i, kk % (nk // 2)))
    elif ta:
        a_spec = pl.BlockSpec((tk, tm), lambda i, j, kk: (kk, i))
    else:
        a_spec = pl.BlockSpec((tm, tk), lambda i, j, kk: (i, kk))
    n_per, k_per = n // tn // N_DEV, nk // N_DEV
    if b_halves:
        b_spec = pl.BlockSpec((None, tk, tn), lambda i, j, kk: (j // (n // tn // 2), kk, j % (n // tn // 2)))
    elif b_shards and tb:
        b_spec = pl.BlockSpec((None, tn, tk), lambda i, j, kk: (kk // k_per, j, kk % k_per))
    elif b_shards:
        b_spec = pl.BlockSpec((None, tk, tn), lambda i, j, kk: (j // n_per, kk, j % n_per))
    elif tb:
        b_spec = pl.BlockSpec((tn, tk), lambda i, j, kk: (j, kk))
    else:
        b_spec = pl.BlockSpec((tk, tn), lambda i, j, kk: (kk, j))
    ex_specs = []
    for kind, arr in extras:
        if kind == "mn":
            assert arr.shape == (m, n), (arr.shape, m, n)
            ex_specs.append(pl.BlockSpec((tm, tn), lambda i, j, kk: (i, j)))
        else:
            assert arr.shape == (1, n), (arr.shape, n)
            ex_specs.append(pl.BlockSpec((1, tn), lambda i, j, kk: (0, j)))
    n_ex, n_out = len(extras), len(out_dtypes)

    def body(a_ref, b_ref, *rest):
        ex, outs, acc = rest[:n_ex], rest[n_ex:n_ex + n_out], rest[-1]
        kk = pl.program_id(2)

        @pl.when(kk == 0)
        def _():
            acc[...] = jnp.zeros_like(acc)

        acc[...] += _dot(a_ref[...], b_ref[...], ta, tb)

        @pl.when(kk == nk - 1)
        def _():
            if epilogue is None:
                vals = (acc[...],)
            else:
                vals = epilogue(acc[...], *[e[...] for e in ex])
            for o, v in zip(outs, vals):
                o[...] = v.astype(o.dtype)

    if out_shards:
        out_spec = pl.BlockSpec((None, tm, tn), lambda i, j, kk: (j // n_per, i, j % n_per))
        out_dims = (N_DEV, m, n // N_DEV)
    else:
        out_spec = pl.BlockSpec((tm, tn), lambda i, j, kk: (i, j))
        out_dims = (m, n)
    res = _pcall(
        body, name=name, grid=(m // tm, n // tn, nk),
        in_specs=[a_spec, b_spec] + ex_specs,
        out_specs=[out_spec] * n_out,
        out_shape=[jax.ShapeDtypeStruct(out_dims, d) for d in out_dtypes],
        scratch_shapes=[pltpu.VMEM((tm, tn), F32)],
        compiler_params=_params(3),
    )(a, b, *[arr for _, arr in extras])
    return res[0] if n_out == 1 else res


def _rowwise(fn, ins, outs, *, name, tr=128):
    rows = next(e[1].shape[0] for e in ins if e[0] != "full")
    tr = _tile(rows, tr)
    in_specs = []
    for entry in ins:
        kind, arr = entry[0], entry[1]
        assert kind == "full" or (arr.shape[0] == rows and arr.ndim == 2)
        if kind == "row":
            in_specs.append(pl.BlockSpec((tr, arr.shape[1]), lambda i: (i, 0)))
        elif kind == "cols":
            in_specs.append(pl.BlockSpec((tr, entry[3]), lambda i, cb=entry[2]: (i, cb)))
        else:
            in_specs.append(pl.BlockSpec(arr.shape, lambda i, nd=arr.ndim: (0,) * nd))
    out_specs, out_shape = [], []
    for kind, w, dt in outs:
        if kind == "row":
            out_specs.append(pl.BlockSpec((tr, w), lambda i: (i, 0)))
            out_shape.append(jax.ShapeDtypeStruct((rows, w), dt))
        else:
            out_specs.append(pl.BlockSpec((1, w), lambda i: (0, 0)))
            out_shape.append(jax.ShapeDtypeStruct((1, w), dt))
    n_in = len(ins)

    def body(*refs):
        i = pl.program_id(0)
        vals = fn(*[r[...] for r in refs[:n_in]])
        for (kind, _, _), o, v in zip(outs, refs[n_in:], vals):
            if kind == "row":
                o[...] = v.astype(o.dtype)
            else:
                @pl.when(i == 0)
                def _(o=o):
                    o[...] = jnp.zeros_like(o)

                o[...] += v.astype(o.dtype)

    return _pcall(body, name=name, grid=(rows // tr,), in_specs=in_specs, out_specs=out_specs,
                  out_shape=out_shape, compiler_params=_params(1))(*[e[1] for e in ins])


def _colsum(x):
    return jnp.sum(x, axis=0, keepdims=True)


def _norm_stats(x):
    rstd = lax.rsqrt(jnp.mean(x * x, axis=-1, keepdims=True) + NORM_EPS)
    return x * rstd, rstd


def _norm_bwd(dxhat, xhat, rstd):
    return rstd * (dxhat - xhat * jnp.mean(dxhat * xhat, axis=-1, keepdims=True))


def _adaln_fwd(x, gain, sc, sh, name):
    def fn(x, gain, sc, sh):
        xhat, _ = _norm_stats(x)
        return ((xhat * gain) * (1.0 + sc) + sh,)

    return _rowwise(fn, [("row", x), ("full", gain), ("full", sc), ("full", sh)],
                    [("row", x.shape[1], BF16)], name=name)[0]


def _adaln_bwd(x, dh, dres, gain, sc, name):
    d = x.shape[1]

    def fn(x, dh, dres, gain, sc):
        xhat, rstd = _norm_stats(x)
        dxhat = dh * (gain * (1.0 + sc))
        dx = dres + _norm_bwd(dxhat, xhat, rstd)
        return dx, _colsum(dh), _colsum(dh * (xhat * gain)), _colsum(dh * xhat * (1.0 + sc))

    return _rowwise(fn, [("row", x), ("row", dh), ("row", dres), ("full", gain), ("full", sc)],
                    [("row", d, F32), ("acc", d, F32), ("acc", d, F32), ("acc", d, F32)], name=name)


def _residual_bwd(dx, y, g, name):
    d = dx.shape[1]

    def fn(dx, y, g):
        return dx * (1.0 + g), _colsum(dx * y)

    return _rowwise(fn, [("row", dx), ("row", y), ("full", g)], [("row", d, BF16), ("acc", d, F32)], name=name)


def _final_loss(x, target, gain, name):
    d = x.shape[1]

    def fn(x, t, gain):
        xhat, rstd = _norm_stats(x)
        err = xhat * gain - t
        dy = err * (1.0 / d)
        loss = 0.5 * jnp.sum(jnp.mean(err * err, axis=-1, keepdims=True), axis=0, keepdims=True)
        dx = _norm_bwd(dy * gain, xhat, rstd)
        return dx, _colsum(dy * xhat), jnp.broadcast_to(loss, (1, LANE))

    return _rowwise(fn, [("row", x), ("row", target), ("full", gain)],
                    [("row", d, F32), ("acc", d, F32), ("acc", LANE, F32)], name=name)


def _gla_gates(q_ref, k_ref, a_ref, wg_ref, bg_ref, scale, c):
    ga = _dot(a_ref[...], wg_ref[...]) + bg_ref[...]
    la = _log_sigmoid(ga) * (1.0 / GLA_TAU)
    b = _tri_matmul(_tri(c), la)
    bl = _colsum(la)
    eb, enb, eend = jnp.exp(b), jnp.exp(-b), jnp.exp(bl - b)
    q = q_ref[...] * scale
    k = k_ref[...]
    return dict(ga=ga, eb=eb, enb=enb, eend=eend, dec=jnp.exp(bl), q_dec=q * eb, k_inv=k * enb, k_end=k * eend)


def _causal(c):
    return lax.broadcasted_iota(jnp.int32, (c, c), 0) >= lax.broadcasted_iota(jnp.int32, (c, c), 1)


def _gla_specs(heads, c, dk, dv, rev, n_chunks):
    def ch(n):
        return (n_chunks - 1 - n) if rev else n

    return [
        pl.BlockSpec((c, dk), lambda h, n: (ch(n), h)),
        pl.BlockSpec((c, dk), lambda h, n: (ch(n), heads + h)),
        pl.BlockSpec((c, dv), lambda h, n: (ch(n), heads + h)),
        pl.BlockSpec((c, LANE), lambda h, n: (ch(n), 0)),
        pl.BlockSpec((LANE, dk), lambda h, n: (0, h)),
        pl.BlockSpec((1, dk), lambda h, n: (0, h)),
    ]


def _gla_fwd(proj, a_tail, wg_p, bg, name):
    s = proj.shape[0]
    heads, c = GLA_HEADS, GLA_CHUNK
    dk = wg_p.shape[1] // heads
    dv = 2 * dk
    n_chunks = s // c
    scale = dk ** -0.5

    def body(q_ref, k_ref, v_ref, a_ref, wg_ref, bg_ref, o_ref, st_ref, state):
        @pl.when(pl.program_id(1) == 0)
        def _():
            state[...] = jnp.zeros_like(state)

        g = _gla_gates(q_ref, k_ref, a_ref, wg_ref, bg_ref, scale, c)
        v = v_ref[...]
        st = state[...]
        attn = jnp.where(_causal(c), _dot(g["q_dec"], g["k_inv"], tb=True), 0.0)
        o_ref[...] = _dot(attn, v) + _dot(g["q_dec"], st, tb=True)
        st_ref[...] = st.astype(st_ref.dtype)
        state[...] = g["dec"] * st + _dot(v, g["k_end"], ta=True)

    return _pcall(
        body, name=name, grid=(heads, n_chunks),
        in_specs=_gla_specs(heads, c, dk, dv, False, n_chunks),
        out_specs=[pl.BlockSpec((c, dv), lambda h, n: (n, h)),
                   pl.BlockSpec((None, None, dv, dk), lambda h, n: (h, n, 0, 0))],
        out_shape=[jax.ShapeDtypeStruct((s, heads * dv), F32),
                   jax.ShapeDtypeStruct((heads, n_chunks, dv, dk), BF16)],
        scratch_shapes=[pltpu.VMEM((dv, dk), F32)],
        compiler_params=_params(2),
    )(proj, proj, proj, a_tail, wg_p, bg)


def _gla_bwd(proj, a_tail, wg_p, bg, states, d_o, name):
    s = proj.shape[0]
    heads, c = GLA_HEADS, GLA_CHUNK
    dk = wg_p.shape[1] // heads
    dv = 2 * dk
    n_chunks = s // c
    scale = dk ** -0.5

    def body(q_ref, k_ref, v_ref, a_ref, wg_ref, bg_ref, st_ref, do_ref, dq_ref, dk_ref, dv_ref, dga_ref, dstate):
        @pl.when(pl.program_id(1) == 0)
        def _():
            dstate[...] = jnp.zeros_like(dstate)

        g = _gla_gates(q_ref, k_ref, a_ref, wg_ref, bg_ref, scale, c)
        v, st, dst, d_out = v_ref[...], st_ref[...], dstate[...], do_ref[...]
        q_dec, k_inv, k_end = g["q_dec"], g["k_inv"], g["k_end"]
        mask = _causal(c)
        attn = jnp.where(mask, _dot(q_dec, k_inv, tb=True), 0.0)
        d_attn = jnp.where(mask, _dot(d_out, v, tb=True), 0.0)
        d_qdec = _dot(d_attn, k_inv) + _dot(d_out, st)
        d_kinv = _dot(d_attn, q_dec, ta=True)
        d_kend = _dot(v, dst)
        dv_ref[...] = (_dot(attn, d_out, ta=True) + _dot(k_end, dst, tb=True)).astype(dv_ref.dtype)
        d_dec = jnp.sum(dst * st.astype(F32), axis=0, keepdims=True)
        dstate[...] = g["dec"] * dst + _dot(d_out, q_dec, ta=True)

        dq_ref[...] = (d_qdec * (scale * g["eb"])).astype(dq_ref.dtype)
        dk_ref[...] = (d_kinv * g["enb"] + d_kend * g["eend"]).astype(dk_ref.dtype)
        kk = d_kend * k_end
        db = d_qdec * q_dec - d_kinv * k_inv - kk
        dbl = jnp.sum(kk, axis=0, keepdims=True) + d_dec * g["dec"]
        last = lax.broadcasted_iota(jnp.int32, db.shape, 0) == c - 1
        db = db + jnp.where(last, dbl, 0.0)
        dla = _tri_matmul(_tri(c, upper=True), db)
        dga_ref[...] = dla * (1.0 / GLA_TAU) * _sigmoid(-g["ga"])

    rev = lambda h, n: (n_chunks - 1 - n, h)
    return _pcall(
        body, name=name, grid=(heads, n_chunks),
        in_specs=_gla_specs(heads, c, dk, dv, True, n_chunks) + [
            pl.BlockSpec((None, None, dv, dk), lambda h, n: (h, n_chunks - 1 - n, 0, 0)),
            pl.BlockSpec((c, dv), rev)],
        out_specs=[pl.BlockSpec((c, dk), rev), pl.BlockSpec((c, dk), rev), pl.BlockSpec((c, dv), rev),
                   pl.BlockSpec((c, dk), rev)],
        out_shape=[jax.ShapeDtypeStruct((s, heads * dk), BF16), jax.ShapeDtypeStruct((s, heads * dk), BF16),
                   jax.ShapeDtypeStruct((s, heads * dv), BF16), jax.ShapeDtypeStruct((s, heads * dk), F32)],
        scratch_shapes=[pltpu.VMEM((dv, dk), F32)],
        compiler_params=_params(2),
    )(proj, proj, proj, a_tail, wg_p, bg, states, d_o)


def _gla_post_fwd(o, r, gn, name):
    dvt = o.shape[1]
    dv = dvt // GLA_HEADS

    def fn(o, r, gn):
        outs = []
        for h in range(GLA_HEADS):
            sl = slice(h * dv, (h + 1) * dv)
            ohat, _ = _norm_stats(o[:, sl])
            outs.append((ohat * gn[:, sl]) * _silu(r[:, sl]))
        return (jnp.concatenate(outs, axis=1),)

    return _rowwise(fn, [("row", o), r, ("full", gn)], [("row", dvt, BF16)], name=name)[0]


def _gla_post_bwd(o, r, gn, dog, name):
    dvt = o.shape[1]
    dv = dvt // GLA_HEADS

    def fn(o, r, gn, dog):
        d_o, d_r, d_g = [], [], []
        for h in range(GLA_HEADS):
            sl = slice(h * dv, (h + 1) * dv)
            ohat, rstd = _norm_stats(o[:, sl])
            g, rr, dd = gn[:, sl], r[:, sl], dog[:, sl]
            d_r.append(dd * (ohat * g) * _dsilu(rr))
            don = dd * _silu(rr)
            d_g.append(_colsum(don * ohat))
            d_o.append(_norm_bwd(don * g, ohat, rstd))
        return jnp.concatenate(d_o, axis=1), jnp.concatenate(d_r, axis=1), jnp.concatenate(d_g, axis=1)

    return _rowwise(fn, [("row", o), r, ("full", gn), ("row", dog)],
                    [("row", dvt, F32), ("row", dvt, BF16), ("acc", dvt, F32)], name=name)


def _fox_prep(q, k, v, qg, kg, d, hd, name):
    heads = d // hd
    scale = hd ** -0.5

    def fn(q, k, v, qg, kg):
        qs, ks = [], []
        for h in range(heads):
            sl = slice(h * hd, (h + 1) * hd)
            qs.append(_norm_stats(q[:, sl])[0] * qg * scale)
            ks.append(_norm_stats(k[:, sl])[0] * kg)
        return jnp.concatenate(qs, axis=1), jnp.concatenate(ks, axis=1), v

    return _rowwise(fn, [q, k, v, ("full", qg), ("full", kg)],
                    [("row", d, BF16)] * 3, name=name)


def _fox_prep_bwd(q, k, dqn, dkn, qg, kg, hd, name):
    d = dqn.shape[1]
    heads = d // hd
    scale = hd ** -0.5

    def fn(q, k, dqn, dkn, qg, kg):
        dq, dk, gq, gk = [], [], [], []
        for h in range(heads):
            sl = slice(h * hd, (h + 1) * hd)
            for x, dxn, g, s, dl, gl in ((q, dqn, qg, scale, dq, gq), (k, dkn, kg, 1.0, dk, gk)):
                xhat, rstd = _norm_stats(x[:, sl])
                dn = dxn[:, sl] * s
                gl.append(_colsum(dn * xhat))
                dl.append(_norm_bwd(dn * g, xhat, rstd))
        cat = lambda t: jnp.concatenate(t, axis=1)
        return cat(dq), cat(dk), cat(gq), cat(gk)

    return _rowwise(fn, [q, k, ("row", dqn), ("row", dkn), ("full", qg), ("full", kg)],
                    [("row", d, BF16), ("row", d, BF16), ("acc", d, F32), ("acc", d, F32)], name=name)


def _fox_cum(fl, bf_p, name, tb=256):
    s = fl.shape[0]
    tb = _tile(s, tb)

    def body(fl_ref, bf_ref, cum_ref, carry):
        @pl.when(pl.program_id(0) == 0)
        def _():
            carry[...] = jnp.zeros_like(carry)

        lf = _log_sigmoid(fl_ref[...] + bf_ref[...])
        cum_ref[...] = _tri_matmul(_tri(tb), lf) + carry[...]
        carry[...] += _colsum(lf)

    return _pcall(
        body, name=name, grid=(s // tb,),
        in_specs=[pl.BlockSpec((tb, LANE), lambda i: (i, 0)), pl.BlockSpec((1, LANE), lambda i: (0, 0))],
        out_specs=pl.BlockSpec((tb, LANE), lambda i: (i, 0)),
        out_shape=jax.ShapeDtypeStruct((s, LANE), F32),
        scratch_shapes=[pltpu.VMEM((1, LANE), F32)],
        compiler_params=_params(1),
    )(fl, bf_p)


def _fox_cum_bwd(dcum, fl, bf_p, name, tb=256):
    s = fl.shape[0]
    tb = _tile(s, tb)
    nb = s // tb

    def body(dc_ref, fl_ref, bf_ref, dfl_ref, dbf_ref, carry):
        @pl.when(pl.program_id(0) == 0)
        def _():
            carry[...] = jnp.zeros_like(carry)
            dbf_ref[...] = jnp.zeros_like(dbf_ref)

        dc = dc_ref[...]
        dlf = _tri_matmul(_tri(tb, upper=True), dc) + carry[...]
        carry[...] += _colsum(dc)
        dfl = dlf * _sigmoid(-(fl_ref[...] + bf_ref[...]))
        dfl_ref[...] = dfl
        dbf_ref[...] += _colsum(dfl)

    rev = lambda i: (nb - 1 - i, 0)
    return _pcall(
        body, name=name, grid=(nb,),
        in_specs=[pl.BlockSpec((tb, LANE), rev), pl.BlockSpec((tb, LANE), rev), pl.BlockSpec((1, LANE), lambda i: (0, 0))],
        out_specs=[pl.BlockSpec((tb, LANE), rev), pl.BlockSpec((1, LANE), lambda i: (0, 0))],
        out_shape=[jax.ShapeDtypeStruct((s, LANE), F32), jax.ShapeDtypeStruct((1, LANE), F32)],
        scratch_shapes=[pltpu.VMEM((1, LANE), F32)],
        compiler_params=_params(1),
    )(dcum, fl, bf_p)


def _fox_attn_fwd(qn, kn, vb, cum_col, cum_row, hd, t, name):
    s, d = qn.shape
    heads = d // hd
    nq = s // t

    def body(q_ref, k_ref, v_ref, cc_ref, cr_ref, o_ref, lse_ref):
        qi = pl.program_id(1)
        q = q_ref[...]
        cq = cc_ref[...]
        qpos = qi * t + lax.broadcasted_iota(jnp.int32, (t, 1), 0)

        def step(kj, carry):
            m, l, acc = carry
            off = pl.multiple_of(kj * t, t)
            ks, vs = k_ref[pl.ds(off, t), :], v_ref[pl.ds(off, t), :]
            sc = _dot(q, ks, tb=True) + cq - cr_ref[kj]
            kpos = off + lax.broadcasted_iota(jnp.int32, (1, t), 1)
            sc = jnp.where(kpos <= qpos, sc, NEG)
            m_new = jnp.maximum(m, jnp.max(sc, axis=1, keepdims=True))
            alpha = jnp.exp(m - m_new)
            p = jnp.exp(sc - m_new)
            return m_new, alpha * l + jnp.sum(p, axis=1, keepdims=True), alpha * acc + _dot(p, vs)

        init = (jnp.full((t, 1), NEG, F32), jnp.zeros((t, 1), F32), jnp.zeros((t, hd), F32))
        m, l, acc = lax.fori_loop(0, qi + 1, step, init)
        o_ref[...] = acc / l
        lse_ref[...] = m + jnp.log(l)

    return _pcall(
        body, name=name, grid=(heads, nq),
        in_specs=[pl.BlockSpec((t, hd), lambda h, i: (i, h)),
                  pl.BlockSpec((s, hd), lambda h, i: (0, h)),
                  pl.BlockSpec((s, hd), lambda h, i: (0, h)),
                  pl.BlockSpec((None, t, 1), lambda h, i: (h, i, 0)),
                  pl.BlockSpec((None, nq, 1, t), lambda h, i: (h, 0, 0, 0))],
        out_specs=[pl.BlockSpec((t, hd), lambda h, i: (i, h)), pl.BlockSpec((None, t, 1), lambda h, i: (h, i, 0))],
        out_shape=[jax.ShapeDtypeStruct((s, d), F32), jax.ShapeDtypeStruct((heads, s, 1), F32)],
        compiler_params=_params(2),
    )(qn, kn, vb, cum_col, cum_row)


def _fox_attn_bwd(qn, kn, vb, d_o, o, lse, cum_col, cum_row, hd, t, name):
    s, d = qn.shape
    heads = d // hd
    nq = s // t

    def body(q_ref, k_ref, v_ref, do_ref, o_ref, lse_ref, cc_ref, cr_ref,
             dq_ref, dk_ref, dv_ref, dcq_ref, dck_ref, delta):
        kj = pl.program_id(1)

        @pl.when(kj == 0)
        def _():
            dq_ref[...] = jnp.zeros_like(dq_ref)
            dcq_ref[...] = jnp.zeros_like(dcq_ref)
            delta[...] = jnp.sum(do_ref[...] * o_ref[...], axis=1, keepdims=True)

        ks, vs, cr = k_ref[...], v_ref[...], cr_ref[...]
        kpos = kj * t + lax.broadcasted_iota(jnp.int32, (1, t), 1)

        def step(qi, carry):
            dk, dv, dck = carry
            rows = pl.ds(pl.multiple_of(qi * t, t), t)
            q, d_out = q_ref[rows, :], do_ref[rows, :]
            sc = _dot(q, ks, tb=True) + cc_ref[rows, :] - cr
            qpos = qi * t + lax.broadcasted_iota(jnp.int32, (t, 1), 0)
            p = jnp.where(kpos <= qpos, jnp.exp(sc - lse_ref[rows, :]), 0.0)
            ds = p * (_dot(d_out, vs, tb=True) - delta[rows, :])
            dq_ref[rows, :] += _dot(ds, ks)
            dcq_ref[rows, :] += jnp.sum(ds, axis=1, keepdims=True)
            return dk + _dot(ds, q, ta=True), dv + _dot(p, d_out, ta=True), dck + _colsum(ds)

        init = (jnp.zeros((t, hd), F32), jnp.zeros((t, hd), F32), jnp.zeros((1, t), F32))
        dk, dv, dck = lax.fori_loop(kj, nq, step, init)
        dk_ref[...] = dk.astype(dk_ref.dtype)
        dv_ref[...] = dv.astype(dv_ref.dtype)
        dck_ref[...] = dck

    head_rows = lambda h, j: (0, h)
    blk = lambda h, j: (j, h)
    return _pcall(
        body, name=name, grid=(heads, nq),
        in_specs=[pl.BlockSpec((s, hd), head_rows), pl.BlockSpec((t, hd), blk), pl.BlockSpec((t, hd), blk),
                  pl.BlockSpec((s, hd), head_rows), pl.BlockSpec((s, hd), head_rows),
                  pl.BlockSpec((None, s, 1), lambda h, j: (h, 0, 0)),
                  pl.BlockSpec((None, s, 1), lambda h, j: (h, 0, 0)),
                  pl.BlockSpec((None, None, 1, t), lambda h, j: (h, j, 0, 0))],
        out_specs=[pl.BlockSpec((s, hd), head_rows), pl.BlockSpec((t, hd), blk), pl.BlockSpec((t, hd), blk),
                   pl.BlockSpec((None, s, 1), lambda h, j: (h, 0, 0)),
                   pl.BlockSpec((None, None, 1, t), lambda h, j: (h, j, 0, 0))],
        out_shape=[jax.ShapeDtypeStruct((s, d), F32), jax.ShapeDtypeStruct((s, d), BF16),
                   jax.ShapeDtypeStruct((s, d), BF16), jax.ShapeDtypeStruct((heads, s, 1), F32),
                   jax.ShapeDtypeStruct((heads, nq, 1, t), F32)],
        scratch_shapes=[pltpu.VMEM((s, 1), F32)],
        compiler_params=_params(2),
    )(qn, kn, vb, d_o, o, lse, cum_col, cum_row)


def _fox_gate_fwd(o, og, name):
    def fn(o, og):
        return (o * _sigmoid(og),)

    return _rowwise(fn, [("row", o), og], [("row", o.shape[1], BF16)], name=name)[0]


def _fox_gate_bwd(o, og, dact, name):
    def fn(o, og, dact):
        sg = _sigmoid(og)
        return dact * sg, dact * o * sg * (1.0 - sg)

    d = o.shape[1]
    return _rowwise(fn, [("row", o), og, ("row", dact)], [("row", d, F32), ("row", d, BF16)], name=name)


def _shift_down(x, n):
    rows = lax.broadcasted_iota(jnp.int32, x.shape, 0)
    return jnp.where(rows >= n, pltpu.roll(x, n, 0), 0.0)


def _shift_up(x, n):
    rows = lax.broadcasted_iota(jnp.int32, x.shape, 0)
    return jnp.where(rows < x.shape[0] - n, pltpu.roll(x, x.shape[0] - n, 0), 0.0)


def _conv(u, w_ref, b):
    return w_ref[0:1, :] * _shift_down(u, 2) + w_ref[1:2, :] * _shift_down(u, 1) + w_ref[2:3, :] * u + b


def _conv_act_fwd(u, cw, cb, name, tc=256):
    s, two_f = u.shape
    dff = two_f // 2
    tc = _tile(dff, tc)
    nb = dff // tc

    def body(ug_ref, uv_ref, wg_ref, wv_ref, bg_ref, bv_ref, a_ref):
        gate = _conv(ug_ref[...], wg_ref, bg_ref[...])
        val = _conv(uv_ref[...], wv_ref, bv_ref[...])
        a_ref[...] = (_silu(gate) * val).astype(a_ref.dtype)

    lo, hi = (lambda j: (0, j)), (lambda j: (0, j + nb))
    return _pcall(
        body, name=name, grid=(nb,),
        in_specs=[pl.BlockSpec((s, tc), lo), pl.BlockSpec((s, tc), hi), pl.BlockSpec((3, tc), lo),
                  pl.BlockSpec((3, tc), hi), pl.BlockSpec((1, tc), lo), pl.BlockSpec((1, tc), hi)],
        out_specs=pl.BlockSpec((s, tc), lo),
        out_shape=jax.ShapeDtypeStruct((s, dff), BF16),
        compiler_params=_params(1),
    )(u, u, cw, cw, cb, cb)


def _conv_act_bwd(u, cw, cb, da, name, tc=128):
    s, two_f = u.shape
    dff = two_f // 2
    tc = _tile(dff, tc)
    nb = dff // tc

    def body(ug_ref, uv_ref, wg_ref, wv_ref, bg_ref, bv_ref, da_ref, du_ref, dw_ref, db_ref):
        ug, uv, da = ug_ref[...], uv_ref[...], da_ref[...]
        gate = _conv(ug, wg_ref, bg_ref[...])
        val = _conv(uv, wv_ref, bv_ref[...])
        sg = _sigmoid(gate)
        d_val = da * (gate * sg)
        d_gate = da * val * (sg * (1.0 + gate * (1.0 - sg)))
        for half, (dc, uu, w_ref) in enumerate(((d_gate, ug, wg_ref), (d_val, uv, wv_ref))):
            du = w_ref[0:1, :] * _shift_up(dc, 2) + w_ref[1:2, :] * _shift_up(dc, 1) + w_ref[2:3, :] * dc
            du_ref[half] = du.astype(du_ref.dtype)
            dw_ref[half, 0:1, :] = _colsum(dc * _shift_down(uu, 2))
            dw_ref[half, 1:2, :] = _colsum(dc * _shift_down(uu, 1))
            dw_ref[half, 2:3, :] = _colsum(dc * uu)
            db_ref[half] = _colsum(dc)

    lo, hi = (lambda j: (0, j)), (lambda j: (0, j + nb))
    both = lambda j: (0, 0, j)
    return _pcall(
        body, name=name, grid=(nb,),
        in_specs=[pl.BlockSpec((s, tc), lo), pl.BlockSpec((s, tc), hi), pl.BlockSpec((3, tc), lo),
                  pl.BlockSpec((3, tc), hi), pl.BlockSpec((1, tc), lo), pl.BlockSpec((1, tc), hi),
                  pl.BlockSpec((s, tc), lo)],
        out_specs=[pl.BlockSpec((2, s, tc), both), pl.BlockSpec((2, 3, tc), both), pl.BlockSpec((2, 1, tc), both)],
        out_shape=[jax.ShapeDtypeStruct((2, s, dff), BF16), jax.ShapeDtypeStruct((2, 3, dff), F32),
                   jax.ShapeDtypeStruct((2, 1, dff), F32)],
        compiler_params=_params(1),
    )(u, u, cw, cw, cb, cb, da)


def _adamw_math(w, g, m, v):
    m = ADAM_B1 * m + (1.0 - ADAM_B1) * g
    v = ADAM_B2 * v + (1.0 - ADAM_B2) * (g * g)
    m_hat = m / (1.0 - ADAM_B1 ** ADAM_STEP)
    v_hat = v / (1.0 - ADAM_B2 ** ADAM_STEP)
    delta = -ADAM_LR * (m_hat / (jnp.sqrt(v_hat) + ADAM_EPS) + ADAM_WD * w)
    return delta, m, v


def _adamw(w, g, m, v, name, tr=128):
    layers, r, c = w.shape
    pieces = isinstance(g, (list, tuple))
    tc = c
    if r % 8:
        tr, tc = r, _tile(c, max(LANE, 256 * 1024 // r // LANE * LANE))
    elif r <= tr:
        tr = r
    while r % tr:
        tr -= 8
    nr, nc = r // tr, c // tc
    g_list = list(g) if pieces else [g]
    n_pieces = g_list[0].shape[0] if pieces else 0

    def body(w_ref, *rest):
        g_refs, (m_ref, v_ref, go_ref, d_ref, mo_ref, vo_ref) = rest[:len(g_list)], rest[len(g_list):]

        def update(grad):
            delta, m_new, v_new = _adamw_math(w_ref[...], grad, m_ref[...], v_ref[...])
            go_ref[...], d_ref[...], mo_ref[...], vo_ref[...] = grad, delta, m_new, v_new

        if not pieces:
            update(g_refs[0][...])
            return
        for layer, g_ref in enumerate(g_refs):
            @pl.when(pl.program_id(0) == layer)
            def _(g_ref=g_ref):
                grad = g_ref[0].astype(F32)
                for i in range(1, n_pieces):
                    grad = grad + g_ref[i].astype(F32)
                update(grad)

    spec = pl.BlockSpec((None, tr, tc), lambda l, i, j: (l, i, j))
    if pieces:
        def walk(k):
            def index(l, i, j):
                here = l == k
                return (0, jnp.where(here, i, jnp.where(l < k, 0, nr - 1)), jnp.where(here, j, jnp.where(l < k, 0, nc - 1)))
            return index

        g_specs = [pl.BlockSpec((n_pieces, tr, tc), walk(k)) for k in range(layers)]
    else:
        g_specs = [spec]
    return _pcall(
        body, name=name, grid=(layers, nr, nc), in_specs=[spec] + g_specs + [spec, spec], out_specs=[spec] * 4,
        out_shape=[jax.ShapeDtypeStruct((layers, r, c), F32)] * 4, compiler_params=_params(3),
    )(w, *g_list, m, v)


def _pair_sum(pieces, partner, core, name, tr=256):
    _, r, c = pieces.shape
    tc = c
    if r % 8:
        tr, tc = r, _tile(c, max(LANE, 256 * 1024 // r // LANE * LANE))
    elif r <= tr:
        tr = r
    while r % tr:
        tr -= 8

    def body(core_ref, mine_ref, partner_ref, out_ref):
        out_ref[...] = (mine_ref[...].astype(F32) + partner_ref[...].astype(F32)).astype(out_ref.dtype)

    return _pcall(
        body, name=name,
        grid_spec=pltpu.PrefetchScalarGridSpec(
            num_scalar_prefetch=1, grid=(4, r // tr, c // tc),
            in_specs=[pl.BlockSpec((None, tr, tc), lambda q, i, j, core_ref: (2 * q + core_ref[0], i, j)),
                      pl.BlockSpec((None, tr, tc), lambda q, i, j, core_ref: (q, i, j))],
            out_specs=pl.BlockSpec((None, tr, tc), lambda q, i, j, core_ref: (q, i, j))),
        out_shape=jax.ShapeDtypeStruct((4, r, c), pieces.dtype), compiler_params=_params(3),
    )(core, pieces, partner)


def _sum8(x, name):
    p = x.shape[2]
    tp = _tile(p, 16 * 1024)

    def body(x_ref, o_ref):
        acc = x_ref[0]
        for i in range(1, N_DEV):
            acc = acc + x_ref[i]
        o_ref[...] = acc

    return _pcall(
        body, name=name, grid=(p // tp,), in_specs=[pl.BlockSpec((N_DEV, 1, tp), lambda i: (0, 0, i))],
        out_specs=pl.BlockSpec((1, tp), lambda i: (0, i)), out_shape=jax.ShapeDtypeStruct((1, p), x.dtype),
        compiler_params=_params(1),
    )(x)


def _exchange(arrays, name, scatter):
    n = len(arrays)
    hbm = pl.BlockSpec(memory_space=pl.ANY)

    def body(*refs):
        ins, outs, token = refs[:n], refs[n:2 * n], refs[2 * n]
        send_sems, recv_sems, local_sems = refs[2 * n + 1:]
        token[...] = jnp.zeros_like(token)
        x, y, c = lax.axis_index("x"), lax.axis_index("y"), lax.axis_index("c")
        me = 4 * x + 2 * y + c
        copies = []
        for a in range(n):
            src_mine = ins[a].at[me] if scatter else ins[a]
            local = pltpu.make_async_copy(src_mine, outs[a].at[me], local_sems.at[a])
            local.start()
            copies.append(local)
            for k in range(1, N_DEV):
                px = 1 - x if k & 4 else x
                py = 1 - y if k & 2 else y
                pc = 1 - c if k & 1 else c
                src = ins[a].at[4 * px + 2 * py + pc] if scatter else ins[a]
                cp = pltpu.make_async_remote_copy(
                    src_ref=src, dst_ref=outs[a].at[me],
                    send_sem=send_sems.at[a * (N_DEV - 1) + k - 1], recv_sem=recv_sems.at[a * (N_DEV - 1) + k - 1],
                    device_id=(px, py, pc), device_id_type=pl.DeviceIdType.MESH)
                cp.start()
                copies.append(cp)
        for cp in copies:
            cp.wait()

    out_shape = [jax.ShapeDtypeStruct(a.shape if scatter else (N_DEV,) + a.shape, a.dtype) for a in arrays]
    res = _pcall(
        body, name=name, in_specs=[hbm] * n, out_specs=[hbm] * n + [pl.BlockSpec(memory_space=pltpu.VMEM)],
        out_shape=out_shape + [jax.ShapeDtypeStruct((8, LANE), F32)],
        scratch_shapes=[pltpu.SemaphoreType.DMA((n * (N_DEV - 1),)), pltpu.SemaphoreType.DMA((n * (N_DEV - 1),)),
                        pltpu.SemaphoreType.DMA((n,))],
        compiler_params=pltpu.CompilerParams(has_side_effects=True),
    )(*arrays)
    return res[:n], res[n][0, 0]


_HBM = pl.BlockSpec(memory_space=pltpu.HBM)
_SEM = pl.BlockSpec(memory_space=pltpu.SEMAPHORE)
_DATAFLOW = pltpu.SideEffectType.DATAFLOW_SIDE_EFFECTING


def _peer(k, x, y, c):
    return (1 - x if k & 4 else x, 1 - y if k & 2 else y, 1 - c if k & 1 else c)


def _pair_plan(x, y, c):
    return [(2 * q + (1 - c), q, (x, y, 1 - c)) for q in range(4)]


def _chip_plan(x, y, c):
    out = []
    for k in _ICI_PEERS:
        px, py, pc = _peer(k, x, y, c)
        out.append((2 * px + py, 2 * x + y, (px, py, pc)))
    return out


def _split_start(arrays, plan, name):
    n = len(arrays)
    lands = [lax.empty((4,) + a.shape[1:], a.dtype) for a in arrays]
    n_copies = len(plan(0, 0, 0))

    def body(*refs):
        srcs, dsts = refs[:n], refs[n:2 * n]
        send_sems, recv_sems, token = refs[4 * n:5 * n], refs[5 * n:6 * n], refs[6 * n]
        copies = plan(lax.axis_index("x"), lax.axis_index("y"), lax.axis_index("c"))
        for a in range(n):
            for j, (src_block, dst_block, peer) in enumerate(copies):
                pltpu.make_async_remote_copy(
                    src_ref=srcs[a].at[src_block], dst_ref=dsts[a].at[dst_block],
                    send_sem=send_sems[a].at[j], recv_sem=recv_sems[a].at[j],
                    device_id=peer, device_id_type=pl.DeviceIdType.MESH).start()
        token[...] = jnp.zeros_like(token)

    sems = [pltpu.SemaphoreType.DMA((n_copies,))] * (2 * n)
    res = _pcall(
        body, name=name,
        in_specs=[_HBM] * (2 * n),
        out_specs=[_HBM] * (2 * n) + [_SEM] * (2 * n) + [pl.BlockSpec(memory_space=pltpu.VMEM)],
        out_shape=[pltpu.HBM(a.shape, a.dtype) for a in arrays] + [pltpu.HBM(l.shape, l.dtype) for l in lands]
        + sems + [jax.ShapeDtypeStruct((8, LANE), F32)],
        input_output_aliases={i: i for i in range(2 * n)},
        compiler_params=pltpu.CompilerParams(has_side_effects=_DATAFLOW),
    )(*[pltpu.with_memory_space_constraint(a, pltpu.HBM) for a in arrays],
      *[pltpu.with_memory_space_constraint(l, pltpu.HBM) for l in lands])
    handles = [(res[a], res[n + a], res[2 * n + a], res[3 * n + a]) for a in range(n)]
    return handles, res[4 * n][0, 0]


def _split_wait(handles, plan, after, name):
    n = len(handles)
    after = list(after) if isinstance(after, (list, tuple)) else [after]

    def body(*refs):
        srcs, dsts = refs[:n], refs[n:2 * n]
        send_sems, recv_sems = refs[2 * n:3 * n], refs[3 * n:4 * n]
        copies = plan(lax.axis_index("x"), lax.axis_index("y"), lax.axis_index("c"))
        for a in range(n):
            for j, (src_block, dst_block, peer) in enumerate(copies):
                cp = pltpu.make_async_remote_copy(
                    src_ref=srcs[a].at[src_block], dst_ref=dsts[a].at[dst_block],
                    send_sem=send_sems[a].at[j], recv_sem=recv_sems[a].at[j],
                    device_id=peer, device_id_type=pl.DeviceIdType.MESH)
                cp.wait_send()
                cp.wait_recv()

    srcs, lands = [h[0] for h in handles], [h[1] for h in handles]
    res = _pcall(
        body, name=name,
        in_specs=[_HBM] * (2 * n) + [_SEM] * (2 * n) + [pl.BlockSpec(memory_space=pl.ANY)] * len(after),
        out_specs=[_HBM] * (2 * n),
        out_shape=[pltpu.HBM(t.shape, t.dtype) for t in srcs + lands],
        input_output_aliases={i: i for i in range(2 * n)},
        compiler_params=pltpu.CompilerParams(has_side_effects=_DATAFLOW),
    )(*srcs, *lands, *[h[2] for h in handles], *[h[3] for h in handles], *after)
    return res[:n], res[n:]


_ICI_PEERS = (2, 4, 6)


def _gather2_start(shards, name):
    n = len(shards)
    lands = [lax.empty((N_DEV,) + a.shape, a.dtype) for a in shards]

    def body(*refs):
        srcs, dsts = refs[:n], refs[n:2 * n]
        send_sems, d2d_sems, ici_sems = refs[4 * n:5 * n], refs[5 * n:6 * n], refs[6 * n:7 * n]
        token = refs[7 * n]
        x, y, c = lax.axis_index("x"), lax.axis_index("y"), lax.axis_index("c")
        me = 4 * x + 2 * y + c
        for a in range(n):
            for j, k in enumerate((1,) + _ICI_PEERS):
                recv = d2d_sems[a].at[0] if j == 0 else ici_sems[a].at[j - 1]
                pltpu.make_async_remote_copy(
                    src_ref=srcs[a], dst_ref=dsts[a].at[me], send_sem=send_sems[a].at[j], recv_sem=recv,
                    device_id=_peer(k, x, y, c), device_id_type=pl.DeviceIdType.MESH).start()
        token[...] = jnp.zeros_like(token)

    dma = pltpu.SemaphoreType.DMA
    res = _pcall(
        body, name=name,
        in_specs=[_HBM] * (2 * n),
        out_specs=[_HBM] * (2 * n) + [_SEM] * (3 * n) + [pl.BlockSpec(memory_space=pltpu.VMEM)],
        out_shape=[pltpu.HBM(a.shape, a.dtype) for a in shards] + [pltpu.HBM(l.shape, l.dtype) for l in lands]
        + [dma((4,))] * n + [dma((1,))] * n + [dma((3,))] * n + [jax.ShapeDtypeStruct((8, LANE), F32)],
        input_output_aliases={i: i for i in range(2 * n)},
        compiler_params=pltpu.CompilerParams(has_side_effects=_DATAFLOW),
    )(*[pltpu.with_memory_space_constraint(a, pltpu.HBM) for a in shards],
      *[pltpu.with_memory_space_constraint(l, pltpu.HBM) for l in lands])
    handles = [tuple(res[i * n + a] for i in range(5)) for a in range(n)]
    return handles, res[5 * n][0, 0]


def _gather2_forward(handle, after, name):
    src, land, send_sems, d2d_sem, ici_sems = handle

    def body(land_ref, ici_ref, after_ref, land_out, fwd_send, fwd_recv, token):
        x, y, c = lax.axis_index("x"), lax.axis_index("y"), lax.axis_index("c")
        for j, k in enumerate(_ICI_PEERS):
            px, py, pc = _peer(k, x, y, c)
            block = land_ref.at[4 * px + 2 * py + pc]
            pltpu.make_async_remote_copy(
                src_ref=block, dst_ref=block, send_sem=fwd_send.at[j], recv_sem=ici_ref.at[j],
                device_id=(px, py, pc), device_id_type=pl.DeviceIdType.MESH).wait_recv()
            pltpu.make_async_remote_copy(
                src_ref=block, dst_ref=block, send_sem=fwd_send.at[j], recv_sem=fwd_recv.at[j],
                device_id=(x, y, 1 - c), device_id_type=pl.DeviceIdType.MESH).start()
        token[...] = jnp.zeros_like(token)

    dma = pltpu.SemaphoreType.DMA
    land, fwd_send, fwd_recv, token = _pcall(
        body, name=name,
        in_specs=[_HBM, _SEM, pl.BlockSpec(memory_space=pl.ANY)],
        out_specs=[_HBM, _SEM, _SEM, pl.BlockSpec(memory_space=pltpu.VMEM)],
        out_shape=[pltpu.HBM(land.shape, land.dtype), dma((3,)), dma((3,)), jax.ShapeDtypeStruct((8, LANE), F32)],
        input_output_aliases={0: 0},
        compiler_params=pltpu.CompilerParams(has_side_effects=_DATAFLOW),
    )(land, ici_sems, after)
    return (src, land, send_sems, d2d_sem, fwd_send, fwd_recv), token[0, 0]


def _gather2_wait(handle, after, name):
    src, land, send_sems, d2d_sem, fwd_send, fwd_recv = handle

    def body(src_ref, land_ref, send_ref, d2d_ref, fsend_ref, frecv_ref, after_ref, src_out, land_out):
        x, y, c = lax.axis_index("x"), lax.axis_index("y"), lax.axis_index("c")
        me = 4 * x + 2 * y + c
        sibling = (x, y, 1 - c)
        block = land_ref.at[me]

        def copy(send, recv):
            return pltpu.make_async_remote_copy(src_ref=src_ref, dst_ref=block, send_sem=send, recv_sem=recv,
                                                device_id=sibling, device_id_type=pl.DeviceIdType.MESH)

        for j in range(4):
            copy(send_ref.at[j], d2d_ref.at[0]).wait_send()
        copy(send_ref.at[0], d2d_ref.at[0]).wait_recv()
        for j in range(3):
            copy(fsend_ref.at[j], frecv_ref.at[j]).wait_send()
            copy(fsend_ref.at[j], frecv_ref.at[j]).wait_recv()

    res = _pcall(
        body, name=name,
        in_specs=[_HBM, _HBM, _SEM, _SEM, _SEM, _SEM, pl.BlockSpec(memory_space=pl.ANY)],
        out_specs=[_HBM, _HBM],
        out_shape=[pltpu.HBM(src.shape, src.dtype), pltpu.HBM(land.shape, land.dtype)],
        input_output_aliases={0: 0, 1: 1},
        compiler_params=pltpu.CompilerParams(has_side_effects=_DATAFLOW),
    )(src, land, send_sems, d2d_sem, fwd_send, fwd_recv, after)
    return res[0], res[1]


def _with_own_block(land, mine, me):
    return lax.dynamic_update_slice(land, mine[None], (me,) + (0,) * mine.ndim)


def _pad_cols(x, width=LANE):
    return jnp.pad(x, ((0, 0), (0, width - x.shape[1])))


def _cols_full(g):
    return jnp.transpose(g, (1, 0, 2)).reshape(g.shape[1], -1)


def _cols_pieces(dw):
    k = dw.shape[0]
    return jnp.transpose(dw.reshape(k, N_DEV, -1), (1, 0, 2))


def _ffn_fwd(x1, p, i, tag):
    h2 = _adaln_fwd(x1, p["norm_ffn"][i], p["sc_f"][i], p["sh_f"][i], f"ffn_norm_{tag}")
    u = _matmul(h2, p["fetch"](f"up{i}", h2), name=f"ffn_up_{tag}", tn=1408, b_shards=True)
    a = _conv_act_fwd(u, p["conv_w"][i], p["conv_b"][i], f"ffn_act_{tag}")
    g_f = p["g_f"][i]
    x2, f = _matmul(a, p["fetch"](f"down{i}", a), name=f"ffn_down_{tag}", tk=512, out_dtypes=(F32, F32),
                    epilogue=lambda acc, x1, g: (x1 + (1.0 + g) * acc, acc), extras=(("mn", x1), ("n", g_f)))
    return x2, dict(h2=h2, u=u, a=a, f=f)


def _ffn_bwd(dx2, x1, saved, p, i, tag):
    d = x1.shape[1]
    w_up, w_down = p["fetch"](f"up{i}", None), p["fetch"](f"down{i}", None)
    df, dg_f = _residual_bwd(dx2, saved["f"], p["g_f"][i], f"ffn_res_bwd_{tag}")
    da = _matmul(df, w_down, tb=True, name=f"ffn_down_dx_{tag}", tn=512)
    dw_down = _matmul(saved["a"], df, ta=True, name=f"ffn_down_dw_{tag}", tm=1408, out_dtypes=(BF16,))
    du, dcw, dcb = _conv_act_bwd(saved["u"], p["conv_w"][i], p["conv_b"][i], da, f"ffn_act_bwd_{tag}")
    dcw, dcb = (jnp.concatenate([t[0], t[1]], axis=1) for t in (dcw, dcb))
    tok = p["flush"](du)
    dh2 = _matmul(du, w_up, tb=True, name=f"ffn_up_dx_{tag}", tk=1408, a_halves=True, b_shards=True)
    dw_up = _matmul(saved["h2"], du, ta=True, name=f"ffn_up_dw_{tag}", tn=1408, out_dtypes=(BF16,), b_halves=True,
                    out_shards=True)
    tok = tok + p["send"](f"ffn{i}", [dw_up, dw_down.reshape(N_DEV, -1, d)])
    dx1, dsh, dsc, dgain = _adaln_bwd(x1, dh2, dx2, p["norm_ffn"][i] + tok, p["sc_f"][i], f"ffn_norm_bwd_{tag}")
    grads = dict(conv_w=dcw, conv_b=dcb, norm_ffn=dgain, sh_f=dsh, sc_f=dsc, g_f=dg_f)
    return dx1, grads


def _gla_layer_fwd(x, p, i):
    h1 = _adaln_fwd(x, p["norm_mix"][i], p["sc_m"][i], p["sh_m"][i], "gla_norm")
    w_t, w_tail_t, main = p["fetch"]("gla_in", h1)
    proj = _matmul(h1, w_t, tb=True, b_rows=main, name="gla_in")
    a_tail = _matmul(h1, w_tail_t, tb=True, name="gla_in_tail")
    dk_total = p["gla_wg_p"].shape[1]
    o, states = _gla_fwd(proj, a_tail, p["gla_wg_p"], p["gla_b_gate"], "gla_chunks")
    assert 2 * dk_total == o.shape[1]
    r = ("cols", proj, 2, o.shape[1])
    og = _gla_post_fwd(o, r, p["gla_norm"], "gla_post")
    x1, y = _matmul(og, p["fetch"]("gla_out", og), name="gla_out", out_dtypes=(F32, F32),
                    epilogue=lambda acc, x, g: (x + (1.0 + g) * acc, acc), extras=(("mn", x), ("n", p["g_m"][i])))
    return x1, dict(h1=h1, proj=proj, a_tail=a_tail, o=o, r=r, states=states, og=og, y=y)


def _gla_layer_bwd(dx1, x, sv, p, i):
    d = x.shape[1]
    (w_t, w_tail_t, main), w_out = p["fetch"]("gla_in", None), p["fetch"]("gla_out", None)
    dy, dg_m = _residual_bwd(dx1, sv["y"], p["g_m"][i], "gla_res_bwd")
    dog = _matmul(dy, w_out, tb=True, name="gla_out_dx")
    dw_out = _matmul(sv["og"], dy, ta=True, name="gla_out_dw", out_dtypes=(BF16,))
    tok = p["flush"](dog) + p["send"]("gla_out", [dw_out.reshape(N_DEV, -1, d)])
    d_o, d_r, dgn = _gla_post_bwd(sv["o"], sv["r"], p["gla_norm"] + tok, dog, "gla_post_bwd")
    dq, dk, dv, dga = _gla_bwd(sv["proj"], sv["a_tail"], p["gla_wg_p"], p["gla_b_gate"], sv["states"], d_o,
                               "gla_chunks_bwd")
    tok = p["flush"](dga)
    da_tail = _matmul(dga, p["gla_wg_p"], tb=True, name="gla_gate_dx", out_dtypes=(BF16,))
    dwg = _matmul(sv["a_tail"], dga, ta=True, name="gla_gate_dw")
    dbg = _rowwise(lambda t: (_colsum(t),), [("row", dga)], [("acc", dga.shape[1], F32)], name="gla_gate_db")[0]
    dproj = jnp.concatenate([dq, dk, dv, d_r], axis=1)
    dh_tail = _matmul(da_tail, w_tail_t, name="gla_in_tail_dx")
    dh1 = _matmul(dproj, w_t, b_rows=main, name="gla_in_dx", tk=1024,
                  epilogue=lambda acc, t: (acc + t,), extras=(("mn", dh_tail),))
    dw_main = _matmul(dproj, sv["h1"], ta=True, name="gla_in_dw", out_dtypes=(BF16,))
    dw_tail = _matmul(da_tail, sv["h1"], ta=True, name="gla_in_tail_dw", out_dtypes=(BF16,))
    rank = p["gla_rank"]
    dx, dsh, dsc, dgain = _adaln_bwd(x, dh1, dx1, p["norm_mix"][i] + tok, p["sc_m"][i], "gla_norm_bwd")
    grads = dict(gla_w_gate=dwg[:rank], gla_b_gate=dbg, gla_norm=dgn, norm_mix=dgain, sh_m=dsh, sc_m=dsc, g_m=dg_m,
                 gla_w_in_unsent=(dw_main, dw_tail[:rank]))
    return dx, grads


def _fox_layer_fwd(x, p, i):
    d = x.shape[1]
    hd = p["fox_q_norm"].shape[1]
    heads = d // hd
    s = x.shape[0]
    t = _tile(s, 512)
    h1 = _adaln_fwd(x, p["norm_mix"][i], p["sc_m"][i], p["sh_m"][i], "fox_norm")
    w_t, w_tail_t, main = p["fetch"]("fox_in", h1)
    proj = _matmul(h1, w_t, tb=True, b_rows=main, name="fox_in")
    fl = _matmul(h1, w_tail_t, tb=True, name="fox_in_tail")
    q, k, v, og = (("cols", proj, j, d) for j in range(4))
    qn, kn, vb = _fox_prep(q, k, v, p["fox_q_norm"], p["fox_k_norm"], d, hd, "fox_prep")
    cum = _fox_cum(fl, p["fox_bf_p"], "fox_cum")
    cum_t = jnp.transpose(cum[:, :heads])
    cum_col, cum_row = cum_t[:, :, None], cum_t.reshape(heads, s // t, 1, t)
    o, lse = _fox_attn_fwd(qn, kn, vb, cum_col, cum_row, hd, t, "fox_attn")
    act = _fox_gate_fwd(o, og, "fox_gate")
    x1, y = _matmul(act, p["fetch"]("fox_out", act), name="fox_out", out_dtypes=(F32, F32),
                    epilogue=lambda acc, x, g: (x + (1.0 + g) * acc, acc), extras=(("mn", x), ("n", p["g_m"][i])))
    return x1, dict(h1=h1, q=q, k=k, og=og, fl=fl, qn=qn, kn=kn, vb=vb, cum_col=cum_col, cum_row=cum_row,
                    o=o, lse=lse, act=act, y=y, t=t, hd=hd)


def _fox_layer_bwd(dx1, x, sv, p, i):
    d = x.shape[1]
    hd, t = sv["hd"], sv["t"]
    heads = d // hd
    s = x.shape[0]
    (w_t, w_tail_t, main), w_out = p["fetch"]("fox_in", None), p["fetch"]("fox_out", None)
    dy, dg_m = _residual_bwd(dx1, sv["y"], p["g_m"][i], "fox_res_bwd")
    dact = _matmul(dy, w_out, tb=True, name="fox_out_dx")
    dw_out = _matmul(sv["act"], dy, ta=True, name="fox_out_dw", out_dtypes=(BF16,))
    d_o, d_og = _fox_gate_bwd(sv["o"], sv["og"], dact, "fox_gate_bwd")
    tok_flush = p["flush"](d_og)
    dqn, dkn, dvb, dcq, dck = _fox_attn_bwd(sv["qn"], sv["kn"], sv["vb"], d_o, sv["o"], sv["lse"], sv["cum_col"],
                                            sv["cum_row"], hd, t, "fox_attn_bwd")
    dq, dk, gq, gk = _fox_prep_bwd(sv["q"], sv["k"], dqn, dkn, p["fox_q_norm"], p["fox_k_norm"], hd, "fox_prep_bwd")
    dcum = _pad_cols(jnp.transpose(dcq[:, :, 0] - dck.reshape(heads, s)))
    dfl, dbf = _fox_cum_bwd(dcum, sv["fl"], p["fox_bf_p"], "fox_cum_bwd")
    dfl_b = dfl.astype(BF16)
    dproj = jnp.concatenate([dq, dk, dvb, d_og], axis=1)
    dh_tail = _matmul(dfl_b, w_tail_t, name="fox_in_tail_dx")
    dh1 = _matmul(dproj, w_t, b_rows=main, name="fox_in_dx", tk=1024,
                  epilogue=lambda acc, tl: (acc + tl,), extras=(("mn", dh_tail),))
    dw_main = _matmul(dproj, sv["h1"], ta=True, name="fox_in_dw", out_dtypes=(BF16,))
    dw_tail = _matmul(dfl_b, sv["h1"], ta=True, name="fox_in_tail_dw", out_dtypes=(BF16,))
    dw_in = jnp.concatenate([dw_main, dw_tail[:heads]], axis=0).reshape(N_DEV, -1, d)
    tok = tok_flush + p["send"]("fox", [dw_in, dw_out.reshape(N_DEV, -1, d)])
    dx, dsh, dsc, dgain = _adaln_bwd(x, dh1, dx1, p["norm_mix"][i] + tok, p["sc_m"][i], "fox_norm_bwd")
    grads = dict(fox_b_f=dbf[:, :heads], fox_q_norm=gq.reshape(heads, hd).sum(0, keepdims=True),
                 fox_k_norm=gk.reshape(heads, hd).sum(0, keepdims=True), norm_mix=dgain, sh_m=dsh, sc_m=dsc, g_m=dg_m)
    return dx, grads


SMALL = ("b_mod", "norm_mix", "norm_ffn", "gla_b_gate", "gla_norm", "fox_b_f", "fox_q_norm", "fox_k_norm",
         "ffn_conv_b", "norm_final")
SMALL_SHARDED = ("gla_w_gate", "ffn_conv_w")
BIG = ("gla_w_in", "gla_w_out", "fox_w_in", "fox_w_out", "ffn_w_up", "ffn_w_down")
WEIGHTS = ("w_mod", "b_mod", "norm_mix", "norm_ffn", "gla_w_in", "gla_w_gate", "gla_b_gate", "gla_norm", "gla_w_out",
           "fox_w_in", "fox_b_f", "fox_q_norm", "fox_k_norm", "fox_w_out", "ffn_w_up", "ffn_conv_w", "ffn_conv_b",
           "ffn_w_down", "norm_final")


def _pack(parts):
    flat = jnp.concatenate([p.reshape(-1) for p in parts])
    pad = (-flat.shape[0]) % 1024
    return jnp.pad(flat, (0, pad)).reshape(1, -1)


def _unpack(flat, shapes):
    out, off = [], 0
    for shp in shapes:
        n = 1
        for s in shp:
            n *= s
        out.append(flat[0, off:off + n].reshape(shp))
        off += n
    return out


def kernel(x, c, w_mod, b_mod, norm_mix, norm_ffn, gla_w_in, gla_w_gate, gla_b_gate, gla_norm, gla_w_out, fox_w_in, fox_b_f, fox_q_norm, fox_k_norm, fox_w_out, ffn_w_up, ffn_conv_w, ffn_conv_b, ffn_w_down, norm_final, loss_target, m_w_mod, m_b_mod, m_norm_mix, m_norm_ffn, m_gla_w_in, m_gla_w_gate, m_gla_b_gate, m_gla_norm, m_gla_w_out, m_fox_w_in, m_fox_b_f, m_fox_q_norm, m_fox_k_norm, m_fox_w_out, m_ffn_w_up, m_ffn_conv_w, m_ffn_conv_b, m_ffn_w_down, m_norm_final, v_w_mod, v_b_mod, v_norm_mix, v_norm_ffn, v_gla_w_in, v_gla_w_gate, v_gla_b_gate, v_gla_norm, v_gla_w_out, v_fox_w_in, v_fox_b_f, v_fox_q_norm, v_fox_k_norm, v_fox_w_out, v_ffn_w_up, v_ffn_conv_w, v_ffn_conv_b, v_ffn_w_down, v_norm_final):
    w = dict(w_mod=w_mod, b_mod=b_mod, norm_mix=norm_mix, norm_ffn=norm_ffn, gla_w_in=gla_w_in, gla_w_gate=gla_w_gate,
             gla_b_gate=gla_b_gate, gla_norm=gla_norm, gla_w_out=gla_w_out, fox_w_in=fox_w_in, fox_b_f=fox_b_f,
             fox_q_norm=fox_q_norm, fox_k_norm=fox_k_norm, fox_w_out=fox_w_out, ffn_w_up=ffn_w_up,
             ffn_conv_w=ffn_conv_w, ffn_conv_b=ffn_conv_b, ffn_w_down=ffn_w_down, norm_final=norm_final)
    mom_m = dict(w_mod=m_w_mod, b_mod=m_b_mod, norm_mix=m_norm_mix, norm_ffn=m_norm_ffn, gla_w_in=m_gla_w_in,
                 gla_w_gate=m_gla_w_gate, gla_b_gate=m_gla_b_gate, gla_norm=m_gla_norm, gla_w_out=m_gla_w_out,
                 fox_w_in=m_fox_w_in, fox_b_f=m_fox_b_f, fox_q_norm=m_fox_q_norm, fox_k_norm=m_fox_k_norm,
                 fox_w_out=m_fox_w_out, ffn_w_up=m_ffn_w_up, ffn_conv_w=m_ffn_conv_w, ffn_conv_b=m_ffn_conv_b,
                 ffn_w_down=m_ffn_w_down, norm_final=m_norm_final)
    mom_v = dict(w_mod=v_w_mod, b_mod=v_b_mod, norm_mix=v_norm_mix, norm_ffn=v_norm_ffn, gla_w_in=v_gla_w_in,
                 gla_w_gate=v_gla_w_gate, gla_b_gate=v_gla_b_gate, gla_norm=v_gla_norm, gla_w_out=v_gla_w_out,
                 fox_w_in=v_fox_w_in, fox_b_f=v_fox_b_f, fox_q_norm=v_fox_q_norm, fox_k_norm=v_fox_k_norm,
                 fox_w_out=v_fox_w_out, ffn_w_up=v_ffn_w_up, ffn_conv_w=v_ffn_conv_w, ffn_conv_b=v_ffn_conv_b,
                 ffn_w_down=v_ffn_w_down, norm_final=v_norm_final)

    me = 4 * lax.axis_index("x") + 2 * lax.axis_index("y") + lax.axis_index("c")
    xs, target = x[0], loss_target[0]
    s, d = xs.shape
    depth = w_mod.shape[0]
    mod_cols = w_mod.shape[2]
    rank = gla_w_gate.shape[1]
    hd = fox_q_norm.shape[1]
    fox_heads = d // hd
    dk_total = gla_w_gate.shape[2] * N_DEV

    cond = c * (1.0 / (1.0 + jnp.exp(-c)))
    g, _ = _exchange([gla_w_gate[0], ffn_conv_w, cond], "gather_small", scatter=False)
    cond_all = g[2][:, 0, :]

    cond_pad = jnp.pad(cond_all, ((0, 16 - N_DEV), (0, 0)))
    mod_part = []
    for i in range(depth):
        b_cols = lax.dynamic_slice(b_mod[i:i + 1], (0, me * mod_cols), (1, mod_cols))
        mod_part.append(_matmul(cond_pad, w_mod[i], name=f"mod_{i}", tn=768,
                                epilogue=lambda acc, b: (acc + b,), extras=(("n", b_cols),))[:N_DEV])
    (mod_all,), tok_mod = _exchange([jnp.stack(mod_part)], "gather_mod", scatter=False)
    mod = lax.dynamic_index_in_dim(mod_all, me, axis=2, keepdims=False)
    mod = jnp.transpose(mod, (1, 0, 2)).reshape(depth, 6, 1, d)

    big_names = ["gla_in", "gla_out", "up0", "down0", "fox_in", "fox_out", "up1", "down1"]
    big_shards = [jnp.transpose(gla_w_in[0] + tok_mod), gla_w_out[0], ffn_w_up[0], ffn_w_down[0],
                  jnp.transpose(fox_w_in[0]), fox_w_out[0], ffn_w_up[1], ffn_w_down[1]]
    big_shards = [t.astype(BF16) for t in big_shards]
    handles, tok0 = _gather2_start(big_shards, "gather_weights_start")
    ready, forwarded = {}, {}

    def split_tail(full_t, tail):
        main = full_t.shape[0] - tail
        return full_t, jnp.pad(full_t[main:], ((0, LANE - tail), (0, 0))), main

    def forward(idx, after):
        key = big_names[idx]
        forwarded[key] = _gather2_forward(handles[idx], after, f"gather_{key}_forward")

    def fetch(key, after):
        if key not in ready:
            idx = big_names.index(key)
            if idx == 0:
                forward(0, after)
            handle, _ = forwarded[key]
            mine, land = _gather2_wait(handle, after, f"gather_{key}_wait")
            tok = 0.0
            if idx + 1 < len(big_names):
                forward(idx + 1, land)
                tok = forwarded[big_names[idx + 1]][1]
            full = _with_own_block(land, mine + jnp.asarray(tok, F32).astype(BF16), me)
            if key == "gla_in":
                ready[key] = split_tail(full.reshape(-1, d), rank)
            elif key == "fox_in":
                ready[key] = split_tail(full.reshape(-1, d), fox_heads)
            elif key.startswith("up"):
                ready[key] = full
            else:
                ready[key] = full.reshape(-1, d)
        return ready[key]

    pending, sent = [], {}
    core = lax.axis_index("c").astype(jnp.int32).reshape(1)
    chip = 2 * lax.axis_index("x") + lax.axis_index("y")

    def send(key, pieces):
        hs, tok = _split_start(pieces, _pair_plan, f"scatter_{key}_pair_start")
        pending.append((key, hs))
        return tok

    def flush(after):
        tok = 0.0
        while pending:
            key, hs = pending.pop(0)
            mine, partner = _split_wait(hs, _pair_plan, after, f"scatter_{key}_pair_wait")
            sums = [_pair_sum(pc, pt, core, f"scatter_{key}_pair_sum{a}")
                    for a, (pc, pt) in enumerate(zip(mine, partner))]
            sent[key], t = _split_start(sums, _chip_plan, f"scatter_{key}_chip_start")
            tok = tok + t
        return tok

    p = dict(
        fetch=fetch, send=send, flush=flush,
        gla_wg_p=jnp.pad(_cols_full(g[0]), ((0, LANE - rank), (0, 0))),
        conv_w=[jnp.transpose(g[1][:, i], (1, 0, 2)).reshape(ffn_conv_w.shape[1], -1) for i in range(depth)],
        conv_b=[ffn_conv_b[i:i + 1] for i in range(depth)],
        gla_b_gate=gla_b_gate, gla_norm=gla_norm, fox_q_norm=fox_q_norm, fox_k_norm=fox_k_norm,
        fox_bf_p=_pad_cols(fox_b_f), gla_rank=rank,
        norm_mix=[norm_mix[i:i + 1] + (tok0 if i == 0 else 0.0) for i in range(depth)],
        norm_ffn=[norm_ffn[i:i + 1] for i in range(depth)],
    )

    for j, nm in enumerate(("sh_m", "sc_m", "g_m", "sh_f", "sc_f", "g_f")):
        p[nm] = [mod[i, j] for i in range(depth)]

    acts, saved = [xs], []
    for i in range(depth):
        layer_fwd = _gla_layer_fwd if i % 2 == 0 else _fox_layer_fwd
        x1, sv_mix = layer_fwd(acts[-1], p, i)
        x2, sv_ffn = _ffn_fwd(x1, p, i, str(i))
        saved.append((acts[-1], x1, sv_mix, sv_ffn))
        acts.append(x2)
    dx, d_norm_final, loss_part = _final_loss(acts[-1], target, norm_final.reshape(1, d), "final_loss")

    lg = [None] * depth
    for i in reversed(range(depth)):
        x_in, x1, sv_mix, sv_ffn = saved[i]
        dx, g_ffn = _ffn_bwd(dx, x1, sv_ffn, p, i, str(i))
        layer_bwd = _gla_layer_bwd if i % 2 == 0 else _fox_layer_bwd
        dx, g_mix = layer_bwd(dx, x_in, sv_mix, p, i)
        lg[i] = {**g_ffn, **g_mix}
    grad_x = dx[None]

    gla_l = [i for i in range(depth) if i % 2 == 0]
    fox_l = [i for i in range(depth) if i % 2 == 1]
    small_parts = dict(
        norm_mix=jnp.concatenate([lg[i]["norm_mix"] for i in range(depth)]),
        norm_ffn=jnp.concatenate([lg[i]["norm_ffn"] for i in range(depth)]),
        gla_b_gate=jnp.concatenate([lg[i]["gla_b_gate"] for i in gla_l]),
        gla_norm=jnp.concatenate([lg[i]["gla_norm"] for i in gla_l]),
        fox_b_f=jnp.concatenate([lg[i]["fox_b_f"] for i in fox_l]),
        fox_q_norm=jnp.concatenate([lg[i]["fox_q_norm"] for i in fox_l]),
        fox_k_norm=jnp.concatenate([lg[i]["fox_k_norm"] for i in fox_l]),
        ffn_conv_b=jnp.concatenate([lg[i]["conv_b"] for i in range(depth)]),
        norm_final=d_norm_final,
        gla_w_gate=jnp.stack([lg[i]["gla_w_gate"] for i in gla_l]),
        ffn_conv_w=jnp.stack([lg[i]["conv_w"] for i in range(depth)]),
        loss=loss_part[:, :1],
    )
    order = ("norm_mix", "norm_ffn", "gla_b_gate", "gla_norm", "fox_b_f", "fox_q_norm", "fox_k_norm", "ffn_conv_b",
             "norm_final", "gla_w_gate", "ffn_conv_w", "loss")
    packed = _pack([small_parts[nm] for nm in order])
    dmod = jnp.stack([jnp.concatenate([lg[i][nm] for nm in ("sh_m", "sc_m", "g_m", "sh_f", "sc_f", "g_f")], axis=1)
                      for i in range(depth)])
    (packed_all, dmod_all), tok_small = _exchange([packed, dmod], "gather_small_grads", scatter=False)
    dw_main, dw_tail = lg[0]["gla_w_in_unsent"]
    dw_in_t = jnp.concatenate([dw_main, dw_tail + tok_small.astype(BF16)], axis=0)
    tok_last = send("gla_in", [dw_in_t.reshape(N_DEV, -1, d)])
    packed_all = packed_all + tok_last
    summed = _unpack(_sum8(packed_all, "sum_small_grads"), [small_parts[nm].shape for nm in order])
    small_g = dict(zip(order, summed))
    loss = small_g["loss"][0, 0]
    dmod_all = dmod_all[:, :, 0, :]

    grads = {}
    cond_t = _pad_cols(jnp.transpose(cond_all)).astype(BF16)
    dmod_cols = lax.dynamic_slice(dmod_all, (0, 0, me * mod_cols), (N_DEV, depth, mod_cols))
    g_w_mod = []
    for i in range(depth):
        rhs = jnp.pad(dmod_cols[:, i], ((0, LANE - N_DEV), (0, 0)))
        g_w_mod.append(_matmul(cond_t, rhs, name=f"mod_dw_{i}", tn=768))
    grads["w_mod"] = jnp.stack(g_w_mod)
    small_g["b_mod"] = _sum8(dmod_all.reshape(N_DEV, 1, -1), "sum_b_mod").reshape(depth, -1)

    received = {}

    def arrive(key, after):
        sums, lands = _split_wait(sent[key], _chip_plan, after, f"scatter_{key}_chip_wait")
        received[key] = [_with_own_block(land, lax.dynamic_index_in_dim(q, chip, 0, keepdims=False), chip)
                         for land, q in zip(lands, sums)]

    for key in ("ffn1", "fox", "ffn0", "gla_out"):
        arrive(key, packed_all)

    out_g, out_d, out_m, out_v = {}, {}, {}, {}

    def update(nm, g_arr, transposed=False):
        swap = (lambda t: jnp.transpose(t, (0, 2, 1))) if transposed else (lambda t: t)
        res = _adamw(swap(w[nm]), g_arr, swap(mom_m[nm]), swap(mom_v[nm]), f"adamw_{nm}")
        out_g[nm], out_d[nm], out_m[nm], out_v[nm] = (swap(t) for t in res)

    update("ffn_w_up", [received[f"ffn{i}"][0] for i in range(depth)])
    tok_flush = flush(out_g["ffn_w_up"])
    update("gla_w_out", [received["gla_out"][0]])
    update("fox_w_in", [received["fox"][0]], transposed=True)
    update("fox_w_out", [received["fox"][1]])
    update("ffn_w_down", [received[f"ffn{i}"][1] for i in range(depth)])
    update("w_mod", grads["w_mod"])

    gate_cols = gla_w_gate.shape[2]
    conv_cols = ffn_conv_w.shape[2]
    local_small = dict(small_g)
    local_small["gla_w_gate"] = lax.dynamic_slice_in_dim(small_g["gla_w_gate"], me * gate_cols, gate_cols, axis=2)
    local_small["ffn_conv_w"] = lax.dynamic_slice_in_dim(small_g["ffn_conv_w"], me * conv_cols, conv_cols, axis=2)
    names = SMALL + SMALL_SHARDED
    shapes = [w[nm].shape for nm in names]
    res = _adamw(_pack([w[nm] for nm in names])[None], (_pack([local_small[nm] for nm in names]) + tok_flush)[None],
                 _pack([mom_m[nm] for nm in names])[None], _pack([mom_v[nm] for nm in names])[None], "adamw_small")
    for tgt, flat in zip((out_g, out_d, out_m, out_v), res):
        for nm, arr in zip(names, _unpack(flat[0], shapes)):
            tgt[nm] = arr

    arrive("gla_in", [out_d[nm] for nm in ("gla_w_out", "fox_w_in", "fox_w_out", "ffn_w_up", "ffn_w_down", "w_mod")])
    update("gla_w_in", [received["gla_in"][0]], transposed=True)

    return (loss, grad_x, *[out_g[n] for n in WEIGHTS], *[out_d[n] for n in WEIGHTS],
            *[out_m[n] for n in WEIGHTS], *[out_v[n] for n in WEIGHTS])
```

```python
import jax
import jax.numpy as jnp
from jax import lax
from jax.experimental import pallas as pl
from jax.experimental.pallas import tpu as pltpu

F32, BF16 = jnp.float32, jnp.bfloat16
N_DEV = 8
GLA_HEADS = 4
GLA_TAU = 16.0
GLA_CHUNK = 64
NORM_EPS = 1e-6
ADAM_LR, ADAM_B1, ADAM_B2, ADAM_EPS, ADAM_WD, ADAM_STEP = 0.001, 0.9, 0.999, 1e-08, 0.01, 10
LANE = 128
VMEM_LIMIT = 56 * 1024 * 1024
NEG = -1e30


def _pcall(body, **kw):
    return pl.pallas_call(body, **kw)


def _params(n_axes):
    return pltpu.CompilerParams(dimension_semantics=("arbitrary",) * n_axes, vmem_limit_bytes=VMEM_LIMIT)


def _tile(dim, pref):
    if dim <= pref:
        return dim
    t = pref
    while dim % t:
        t -= LANE
    assert t > 0, (dim, pref)
    return t


def _dot(a, b, ta=False, tb=False):
    dims = (((0,) if ta else (1,), (1,) if tb else (0,)), ((), ()))
    return lax.dot_general(a.astype(BF16), b.astype(BF16), dims, preferred_element_type=F32)


def _split3(x):
    hi = x.astype(BF16)
    r1 = x - hi.astype(F32)
    mid = r1.astype(BF16)
    lo = (r1 - mid.astype(F32)).astype(BF16)
    return hi, mid, lo


def _tri_matmul(tri, x):
    hi, mid, lo = _split3(x)
    return _dot(tri, hi) + _dot(tri, mid) + _dot(tri, lo)


def _tri(n, upper=False):
    r = lax.broadcasted_iota(jnp.int32, (n, n), 0)
    c = lax.broadcasted_iota(jnp.int32, (n, n), 1)
    return jnp.where((r <= c) if upper else (r >= c), 1.0, 0.0).astype(BF16)


def _log_sigmoid(x):
    return jnp.minimum(x, 0.0) - jnp.log(1.0 + jnp.exp(-jnp.abs(x)))


def _sigmoid(x):
    return 1.0 / (1.0 + jnp.exp(-x))


def _silu(x):
    return x * _sigmoid(x)


def _dsilu(x):
    s = _sigmoid(x)
    return s * (1.0 + x * (1.0 - s))


def _matmul(a, b, *, name, ta=False, tb=False, out_dtypes=(F32,), tm=1024, tn=1024, tk=2048,
            epilogue=None, extras=(), a_halves=False, b_halves=False, b_shards=False, out_shards=False,
            b_rows=None):
    if a_halves:
        assert not ta
        m, k = a.shape[1], 2 * a.shape[2]
    else:
        m, k = (a.shape[1], a.shape[0]) if ta else a.shape
    if b_halves:
        assert not tb and b.shape[1] == k
        n = 2 * b.shape[2]
    elif b_shards:
        n = b.shape[1] if tb else N_DEV * b.shape[2]
        assert (N_DEV * b.shape[2] if tb else b.shape[1]) == k, (a.shape, b.shape, ta, tb)
    else:
        rows = b.shape[0] if b_rows is None else b_rows
        n = rows if tb else b.shape[1]
        assert (b.shape[1] if tb else rows) == k, (a.shape, b.shape, ta, tb)
    n_unit = n // N_DEV if (out_shards or (b_shards and not tb)) else (n // 2 if b_halves else n)
    k_unit = k // N_DEV if (b_shards and tb) else (k // 2 if a_halves else k)
    tm, tn, tk = _tile(m, tm), _tile(n_unit, tn), _tile(k_unit, tk)
    nk = k // tk
    if a_halves:
        a_spec = pl.BlockSpec((None, tm, tk), lambda i, j, kk: (kk // (nk // 2), i, kk % (nk // 2)))
    elif ta:
        a_spec = pl.BlockSpec((tk, tm), lambda i, j, kk: (kk, i))
    else:
        a_spec = pl.BlockSpec((tm, tk), lambda i, j, kk: (i, kk))
    n_per, k_per = n // tn // N_DEV, nk // N_DEV
    if b_halves:
        b_spec = pl.BlockSpec((None, tk, tn), lambda i, j, kk: (j // (n // tn // 2), kk, j % (n // tn // 2)))
    elif b_shards and tb:
        b_spec = pl.BlockSpec((None, tn, tk), lambda i, j, kk: (kk // k_per, j, kk % k_per))
    elif b_shards:
        b_spec = pl.BlockSpec((None, tk, tn), lambda i, j, kk: (j // n_per, kk, j % n_per))
    elif tb:
        b_spec = pl.BlockSpec((tn, tk), lambda i, j, kk: (j, kk))
    else:
        b_spec = pl.BlockSpec((tk, tn), lambda i, j, kk: (kk, j))
    ex_specs = []
    for kind, arr in extras:
        if kind == "mn":
            assert arr.shape == (m, n), (arr.shape, m, n)
            ex_specs.append(pl.BlockSpec((tm, tn), lambda i, j, kk: (i, j)))
        else:
            assert arr.shape == (1, n), (arr.shape, n)
            ex_specs.append(pl.BlockSpec((1, tn), lambda i, j, kk: (0, j)))
    n_ex, n_out = len(extras), len(out_dtypes)

    def body(a_ref, b_ref, *rest):
        ex, outs, acc = rest[:n_ex], rest[n_ex:n_ex + n_out], rest[-1]
        kk = pl.program_id(2)

        @pl.when(kk == 0)
        def _():
            acc[...] = jnp.zeros_like(acc)

        acc[...] += _dot(a_ref[...], b_ref[...], ta, tb)

        @pl.when(kk == nk - 1)
        def _():
            if epilogue is None:
                vals = (acc[...],)
            else:
                vals = epilogue(acc[...], *[e[...] for e in ex])
            for o, v in zip(outs, vals):
                o[...] = v.astype(o.dtype)

    if out_shards:
        out_spec = pl.BlockSpec((None, tm, tn), lambda i, j, kk: (j // n_per, i, j % n_per))
        out_dims = (N_DEV, m, n // N_DEV)
    else:
        out_spec = pl.BlockSpec((tm, tn), lambda i, j, kk: (i, j))
        out_dims = (m, n)
    res = _pcall(
        body, name=name, grid=(m // tm, n // tn, nk),
        in_specs=[a_spec, b_spec] + ex_specs,
        out_specs=[out_spec] * n_out,
        out_shape=[jax.ShapeDtypeStruct(out_dims, d) for d in out_dtypes],
        scratch_shapes=[pltpu.VMEM((tm, tn), F32)],
        compiler_params=_params(3),
    )(a, b, *[arr for _, arr in extras])
    return res[0] if n_out == 1 else res


def _rowwise(fn, ins, outs, *, name, tr=128):
    rows = next(e[1].shape[0] for e in ins if e[0] != "full")
    tr = _tile(rows, tr)
    in_specs = []
    for entry in ins:
        kind, arr = entry[0], entry[1]
        assert kind == "full" or (arr.shape[0] == rows and arr.ndim == 2)
        if kind == "row":
            in_specs.append(pl.BlockSpec((tr, arr.shape[1]), lambda i: (i, 0)))
        elif kind == "cols":
            in_specs.append(pl.BlockSpec((tr, entry[3]), lambda i, cb=entry[2]: (i, cb)))
        else:
            in_specs.append(pl.BlockSpec(arr.shape, lambda i, nd=arr.ndim: (0,) * nd))
    out_specs, out_shape = [], []
    for kind, w, dt in outs:
        if kind == "row":
            out_specs.append(pl.BlockSpec((tr, w), lambda i: (i, 0)))
            out_shape.append(jax.ShapeDtypeStruct((rows, w), dt))
        else:
            out_specs.append(pl.BlockSpec((1, w), lambda i: (0, 0)))
            out_shape.append(jax.ShapeDtypeStruct((1, w), dt))
    n_in = len(ins)

    def body(*refs):
        i = pl.program_id(0)
        vals = fn(*[r[...] for r in refs[:n_in]])
        for (kind, _, _), o, v in zip(outs, refs[n_in:], vals):
            if kind == "row":
                o[...] = v.astype(o.dtype)
            else:
                @pl.when(i == 0)
                def _(o=o):
                    o[...] = jnp.zeros_like(o)

                o[...] += v.astype(o.dtype)

    return _pcall(body, name=name, grid=(rows // tr,), in_specs=in_specs, out_specs=out_specs,
                  out_shape=out_shape, compiler_params=_params(1))(*[e[1] for e in ins])


def _colsum(x):
    return jnp.sum(x, axis=0, keepdims=True)


def _norm_stats(x):
    rstd = lax.rsqrt(jnp.mean(x * x, axis=-1, keepdims=True) + NORM_EPS)
    return x * rstd, rstd


def _norm_bwd(dxhat, xhat, rstd):
    return rstd * (dxhat - xhat * jnp.mean(dxhat * xhat, axis=-1, keepdims=True))


def _adaln_fwd(x, gain, sc, sh, name):
    def fn(x, gain, sc, sh):
        xhat, _ = _norm_stats(x)
        return ((xhat * gain) * (1.0 + sc) + sh,)

    return _rowwise(fn, [("row", x), ("full", gain), ("full", sc), ("full", sh)],
                    [("row", x.shape[1], BF16)], name=name)[0]


def _adaln_bwd(x, dh, dres, gain, sc, name):
    d = x.shape[1]

    def fn(x, dh, dres, gain, sc):
        xhat, rstd = _norm_stats(x)
        dxhat = dh * (gain * (1.0 + sc))
        dx = dres + _norm_bwd(dxhat, xhat, rstd)
        return dx, _colsum(dh), _colsum(dh * (xhat * gain)), _colsum(dh * xhat * (1.0 + sc))

    return _rowwise(fn, [("row", x), ("row", dh), ("row", dres), ("full", gain), ("full", sc)],
                    [("row", d, F32), ("acc", d, F32), ("acc", d, F32), ("acc", d, F32)], name=name)


def _residual_bwd(dx, y, g, name):
    d = dx.shape[1]

    def fn(dx, y, g):
        return dx * (1.0 + g), _colsum(dx * y)

    return _rowwise(fn, [("row", dx), ("row", y), ("full", g)], [("row", d, BF16), ("acc", d, F32)], name=name)


def _final_loss(x, target, gain, name):
    d = x.shape[1]

    def fn(x, t, gain):
        xhat, rstd = _norm_stats(x)
        err = xhat * gain - t
        dy = err * (1.0 / d)
        loss = 0.5 * jnp.sum(jnp.mean(err * err, axis=-1, keepdims=True), axis=0, keepdims=True)
        dx = _norm_bwd(dy * gain, xhat, rstd)
        return dx, _colsum(dy * xhat), jnp.broadcast_to(loss, (1, LANE))

    return _rowwise(fn, [("row", x), ("row", target), ("full", gain)],
                    [("row", d, F32), ("acc", d, F32), ("acc", LANE, F32)], name=name)


def _gla_gates(q, k, a, wg, bg, scale, c):
    ga = _dot(a, wg) + bg
    la = _log_sigmoid(ga) * (1.0 / GLA_TAU)
    b = _tri_matmul(_tri(c), la)
    bl = _colsum(la)
    eb, enb, eend = jnp.exp(b), jnp.exp(-b), jnp.exp(bl - b)
    q = q * scale
    return dict(ga=ga, eb=eb, enb=enb, eend=eend, dec=jnp.exp(bl), q_dec=q * eb, k_inv=k * enb, k_end=k * eend)


def _causal(c):
    return lax.broadcasted_iota(jnp.int32, (c, c), 0) >= lax.broadcasted_iota(jnp.int32, (c, c), 1)


def _gla_specs(heads, c, dk, dv, chunk):
    return [
        pl.BlockSpec((c, heads * dk), lambda n: (chunk(n), 0)),
        pl.BlockSpec((c, heads * dk), lambda n: (chunk(n), 1)),
        pl.BlockSpec((c, heads * dv), lambda n: (chunk(n), 1)),
        pl.BlockSpec((c, LANE), lambda n: (chunk(n), 0)),
        pl.BlockSpec((LANE, heads * dk), lambda n: (0, 0)),
        pl.BlockSpec((1, heads * dk), lambda n: (0, 0)),
    ]


def _gla_fwd(proj, a_tail, wg_p, bg, name):
    s = proj.shape[0]
    heads, c = GLA_HEADS, GLA_CHUNK
    dk = wg_p.shape[1] // heads
    dv = 2 * dk
    n_chunks = s // c
    scale = dk ** -0.5

    def body(q_ref, k_ref, v_ref, a_ref, wg_ref, bg_ref, o_ref, st_ref, state):
        @pl.when(pl.program_id(0) == 0)
        def _():
            state[...] = jnp.zeros_like(state)

        a = a_ref[...]
        for h in range(heads):
            sk, sv = slice(h * dk, (h + 1) * dk), slice(h * dv, (h + 1) * dv)
            g = _gla_gates(q_ref[:, sk], k_ref[:, sk], a, wg_ref[:, sk], bg_ref[:, sk], scale, c)
            v = v_ref[:, sv]
            st = state[h]
            attn = jnp.where(_causal(c), _dot(g["q_dec"], g["k_inv"], tb=True), 0.0)
            o_ref[:, sv] = _dot(attn, v) + _dot(g["q_dec"], st, tb=True)
            st_ref[h] = st.astype(st_ref.dtype)
            state[h] = g["dec"] * st + _dot(v, g["k_end"], ta=True)

    return _pcall(
        body, name=name, grid=(n_chunks,),
        in_specs=_gla_specs(heads, c, dk, dv, lambda n: n),
        out_specs=[pl.BlockSpec((c, heads * dv), lambda n: (n, 0)),
                   pl.BlockSpec((heads, None, dv, dk), lambda n: (0, n, 0, 0))],
        out_shape=[jax.ShapeDtypeStruct((s, heads * dv), F32),
                   jax.ShapeDtypeStruct((heads, n_chunks, dv, dk), BF16)],
        scratch_shapes=[pltpu.VMEM((heads, dv, dk), F32)],
        compiler_params=_params(1),
    )(proj, proj, proj, a_tail, wg_p, bg)


def _gla_bwd(proj, a_tail, wg_p, bg, states, d_o, name):
    s = proj.shape[0]
    heads, c = GLA_HEADS, GLA_CHUNK
    dk = wg_p.shape[1] // heads
    dv = 2 * dk
    n_chunks = s // c
    scale = dk ** -0.5

    def body(q_ref, k_ref, v_ref, a_ref, wg_ref, bg_ref, st_ref, do_ref, dq_ref, dk_ref, dv_ref, dga_ref, dstate):
        @pl.when(pl.program_id(0) == 0)
        def _():
            dstate[...] = jnp.zeros_like(dstate)

        a = a_ref[...]
        mask = _causal(c)
        for h in range(heads):
            sk, sv = slice(h * dk, (h + 1) * dk), slice(h * dv, (h + 1) * dv)
            g = _gla_gates(q_ref[:, sk], k_ref[:, sk], a, wg_ref[:, sk], bg_ref[:, sk], scale, c)
            v, st, dst, d_out = v_ref[:, sv], st_ref[h], dstate[h], do_ref[:, sv]
            q_dec, k_inv, k_end = g["q_dec"], g["k_inv"], g["k_end"]
            attn = jnp.where(mask, _dot(q_dec, k_inv, tb=True), 0.0)
            d_attn = jnp.where(mask, _dot(d_out, v, tb=True), 0.0)
            d_qdec = _dot(d_attn, k_inv) + _dot(d_out, st)
            d_kinv = _dot(d_attn, q_dec, ta=True)
            d_kend = _dot(v, dst)
            dv_ref[:, sv] = (_dot(attn, d_out, ta=True) + _dot(k_end, dst, tb=True)).astype(dv_ref.dtype)
            d_dec = jnp.sum(dst * st.astype(F32), axis=0, keepdims=True)
            dstate[h] = g["dec"] * dst + _dot(d_out, q_dec, ta=True)

            dq_ref[:, sk] = (d_qdec * (scale * g["eb"])).astype(dq_ref.dtype)
            dk_ref[:, sk] = (d_kinv * g["enb"] + d_kend * g["eend"]).astype(dk_ref.dtype)
            kk = d_kend * k_end
            db = d_qdec * q_dec - d_kinv * k_inv - kk
            dbl = jnp.sum(kk, axis=0, keepdims=True) + d_dec * g["dec"]
            last = lax.broadcasted_iota(jnp.int32, db.shape, 0) == c - 1
            db = db + jnp.where(last, dbl, 0.0)
            dla = _tri_matmul(_tri(c, upper=True), db)
            dga_ref[:, sk] = dla * (1.0 / GLA_TAU) * _sigmoid(-g["ga"])

    chunk = lambda n: n_chunks - 1 - n
    rev = lambda n: (chunk(n), 0)
    return _pcall(
        body, name=name, grid=(n_chunks,),
        in_specs=_gla_specs(heads, c, dk, dv, chunk) + [
            pl.BlockSpec((heads, None, dv, dk), lambda n: (0, chunk(n), 0, 0)),
            pl.BlockSpec((c, heads * dv), rev)],
        out_specs=[pl.BlockSpec((c, heads * dk), rev), pl.BlockSpec((c, heads * dk), rev),
                   pl.BlockSpec((c, heads * dv), rev), pl.BlockSpec((c, heads * dk), rev)],
        out_shape=[jax.ShapeDtypeStruct((s, heads * dk), BF16), jax.ShapeDtypeStruct((s, heads * dk), BF16),
                   jax.ShapeDtypeStruct((s, heads * dv), BF16), jax.ShapeDtypeStruct((s, heads * dk), F32)],
        scratch_shapes=[pltpu.VMEM((heads, dv, dk), F32)],
        compiler_params=_params(1),
    )(proj, proj, proj, a_tail, wg_p, bg, states, d_o)


def _gla_post_fwd(o, r, gn, name):
    dvt = o.shape[1]
    dv = dvt // GLA_HEADS

    def fn(o, r, gn):
        outs = []
        for h in range(GLA_HEADS):
            sl = slice(h * dv, (h + 1) * dv)
            ohat, _ = _norm_stats(o[:, sl])
            outs.append((ohat * gn[:, sl]) * _silu(r[:, sl]))
        return (jnp.concatenate(outs, axis=1),)

    return _rowwise(fn, [("row", o), r, ("full", gn)], [("row", dvt, BF16)], name=name)[0]


def _gla_post_bwd(o, r, gn, dog, name):
    dvt = o.shape[1]
    dv = dvt // GLA_HEADS

    def fn(o, r, gn, dog):
        d_o, d_r, d_g = [], [], []
        for h in range(GLA_HEADS):
            sl = slice(h * dv, (h + 1) * dv)
            ohat, rstd = _norm_stats(o[:, sl])
            g, rr, dd = gn[:, sl], r[:, sl], dog[:, sl]
            d_r.append(dd * (ohat * g) * _dsilu(rr))
            don = dd * _silu(rr)
            d_g.append(_colsum(don * ohat))
            d_o.append(_norm_bwd(don * g, ohat, rstd))
        return jnp.concatenate(d_o, axis=1), jnp.concatenate(d_r, axis=1), jnp.concatenate(d_g, axis=1)

    return _rowwise(fn, [("row", o), r, ("full", gn), ("row", dog)],
                    [("row", dvt, F32), ("row", dvt, BF16), ("acc", dvt, F32)], name=name)


def _fox_prep(q, k, v, qg, kg, d, hd, name):
    heads = d // hd
    scale = hd ** -0.5

    def fn(q, k, v, qg, kg):
        qs, ks = [], []
        for h in range(heads):
            sl = slice(h * hd, (h + 1) * hd)
            qs.append(_norm_stats(q[:, sl])[0] * qg * scale)
            ks.append(_norm_stats(k[:, sl])[0] * kg)
        return jnp.concatenate(qs, axis=1), jnp.concatenate(ks, axis=1), v

    return _rowwise(fn, [q, k, v, ("full", qg), ("full", kg)],
                    [("row", d, BF16)] * 3, name=name)


def _fox_prep_bwd(q, k, dqn, dkn, qg, kg, hd, name):
    d = dqn.shape[1]
    heads = d // hd
    scale = hd ** -0.5

    def fn(q, k, dqn, dkn, qg, kg):
        dq, dk, gq, gk = [], [], [], []
        for h in range(heads):
            sl = slice(h * hd, (h + 1) * hd)
            for x, dxn, g, s, dl, gl in ((q, dqn, qg, scale, dq, gq), (k, dkn, kg, 1.0, dk, gk)):
                xhat, rstd = _norm_stats(x[:, sl])
                dn = dxn[:, sl] * s
                gl.append(_colsum(dn * xhat))
                dl.append(_norm_bwd(dn * g, xhat, rstd))
        cat = lambda t: jnp.concatenate(t, axis=1)
        return cat(dq), cat(dk), cat(gq), cat(gk)

    return _rowwise(fn, [q, k, ("row", dqn), ("row", dkn), ("full", qg), ("full", kg)],
                    [("row", d, BF16), ("row", d, BF16), ("acc", d, F32), ("acc", d, F32)], name=name)


def _fox_cum(fl, bf_p, name, tb=256):
    s = fl.shape[0]
    tb = _tile(s, tb)

    def body(fl_ref, bf_ref, cum_ref, carry):
        @pl.when(pl.program_id(0) == 0)
        def _():
            carry[...] = jnp.zeros_like(carry)

        lf = _log_sigmoid(fl_ref[...] + bf_ref[...])
        cum_ref[...] = _tri_matmul(_tri(tb), lf) + carry[...]
        carry[...] += _colsum(lf)

    return _pcall(
        body, name=name, grid=(s // tb,),
        in_specs=[pl.BlockSpec((tb, LANE), lambda i: (i, 0)), pl.BlockSpec((1, LANE), lambda i: (0, 0))],
        out_specs=pl.BlockSpec((tb, LANE), lambda i: (i, 0)),
        out_shape=jax.ShapeDtypeStruct((s, LANE), F32),
        scratch_shapes=[pltpu.VMEM((1, LANE), F32)],
        compiler_params=_params(1),
    )(fl, bf_p)


def _fox_cum_bwd(dcum, fl, bf_p, name, tb=256):
    s = fl.shape[0]
    tb = _tile(s, tb)
    nb = s // tb

    def body(dc_ref, fl_ref, bf_ref, dfl_ref, dbf_ref, carry):
        @pl.when(pl.program_id(0) == 0)
        def _():
            carry[...] = jnp.zeros_like(carry)
            dbf_ref[...] = jnp.zeros_like(dbf_ref)

        dc = dc_ref[...]
        dlf = _tri_matmul(_tri(tb, upper=True), dc) + carry[...]
        carry[...] += _colsum(dc)
        dfl = dlf * _sigmoid(-(fl_ref[...] + bf_ref[...]))
        dfl_ref[...] = dfl
        dbf_ref[...] += _colsum(dfl)

    rev = lambda i: (nb - 1 - i, 0)
    return _pcall(
        body, name=name, grid=(nb,),
        in_specs=[pl.BlockSpec((tb, LANE), rev), pl.BlockSpec((tb, LANE), rev), pl.BlockSpec((1, LANE), lambda i: (0, 0))],
        out_specs=[pl.BlockSpec((tb, LANE), rev), pl.BlockSpec((1, LANE), lambda i: (0, 0))],
        out_shape=[jax.ShapeDtypeStruct((s, LANE), F32), jax.ShapeDtypeStruct((1, LANE), F32)],
        scratch_shapes=[pltpu.VMEM((1, LANE), F32)],
        compiler_params=_params(1),
    )(dcum, fl, bf_p)


def _fox_attn_fwd(qn, kn, vb, cum_col, cum_row, hd, t, name):
    s, d = qn.shape
    heads = d // hd
    nq = s // t

    def body(q_ref, k_ref, v_ref, cc_ref, cr_ref, o_ref, lse_ref):
        qi = pl.program_id(1)
        q = q_ref[...]
        cq = cc_ref[...]
        qpos = qi * t + lax.broadcasted_iota(jnp.int32, (t, 1), 0)

        def step(kj, carry):
            m, l, acc = carry
            off = pl.multiple_of(kj * t, t)
            ks, vs = k_ref[pl.ds(off, t), :], v_ref[pl.ds(off, t), :]
            sc = _dot(q, ks, tb=True) + cq - cr_ref[kj]
            kpos = off + lax.broadcasted_iota(jnp.int32, (1, t), 1)
            sc = jnp.where(kpos <= qpos, sc, NEG)
            m_new = jnp.maximum(m, jnp.max(sc, axis=1, keepdims=True))
            alpha = jnp.exp(m - m_new)
            p = jnp.exp(sc - m_new)
            return m_new, alpha * l + jnp.sum(p, axis=1, keepdims=True), alpha * acc + _dot(p, vs)

        init = (jnp.full((t, 1), NEG, F32), jnp.zeros((t, 1), F32), jnp.zeros((t, hd), F32))
        m, l, acc = lax.fori_loop(0, qi + 1, step, init)
        o_ref[...] = acc / l
        lse_ref[...] = m + jnp.log(l)

    return _pcall(
        body, name=name, grid=(heads, nq),
        in_specs=[pl.BlockSpec((t, hd), lambda h, i: (i, h)),
                  pl.BlockSpec((s, hd), lambda h, i: (0, h)),
                  pl.BlockSpec((s, hd), lambda h, i: (0, h)),
                  pl.BlockSpec((None, t, 1), lambda h, i: (h, i, 0)),
                  pl.BlockSpec((None, nq, 1, t), lambda h, i: (h, 0, 0, 0))],
        out_specs=[pl.BlockSpec((t, hd), lambda h, i: (i, h)), pl.BlockSpec((None, t, 1), lambda h, i: (h, i, 0))],
        out_shape=[jax.ShapeDtypeStruct((s, d), F32), jax.ShapeDtypeStruct((heads, s, 1), F32)],
        compiler_params=_params(2),
    )(qn, kn, vb, cum_col, cum_row)


def _fox_attn_bwd(qn, kn, vb, d_o, o, lse, cum_col, cum_row, hd, t, name):
    s, d = qn.shape
    heads = d // hd
    nq = s // t

    def body(q_ref, k_ref, v_ref, do_ref, o_ref, lse_ref, cc_ref, cr_ref,
             dq_ref, dk_ref, dv_ref, dcq_ref, dck_ref, delta):
        kj = pl.program_id(1)

        @pl.when(kj == 0)
        def _():
            dq_ref[...] = jnp.zeros_like(dq_ref)
            dcq_ref[...] = jnp.zeros_like(dcq_ref)
            delta[...] = jnp.sum(do_ref[...] * o_ref[...], axis=1, keepdims=True)

        ks, vs, cr = k_ref[...], v_ref[...], cr_ref[...]
        kpos = kj * t + lax.broadcasted_iota(jnp.int32, (1, t), 1)

        def step(qi, carry):
            dk, dv, dck = carry
            rows = pl.ds(pl.multiple_of(qi * t, t), t)
            q, d_out = q_ref[rows, :], do_ref[rows, :]
            sc = _dot(q, ks, tb=True) + cc_ref[rows, :] - cr
            qpos = qi * t + lax.broadcasted_iota(jnp.int32, (t, 1), 0)
            p = jnp.where(kpos <= qpos, jnp.exp(sc - lse_ref[rows, :]), 0.0)
            ds = p * (_dot(d_out, vs, tb=True) - delta[rows, :])
            dq_ref[rows, :] += _dot(ds, ks)
            dcq_ref[rows, :] += jnp.sum(ds, axis=1, keepdims=True)
            return dk + _dot(ds, q, ta=True), dv + _dot(p, d_out, ta=True), dck + _colsum(ds)

        init = (jnp.zeros((t, hd), F32), jnp.zeros((t, hd), F32), jnp.zeros((1, t), F32))
        dk, dv, dck = lax.fori_loop(kj, nq, step, init)
        dk_ref[...] = dk.astype(dk_ref.dtype)
        dv_ref[...] = dv.astype(dv_ref.dtype)
        dck_ref[...] = dck

    head_rows = lambda h, j: (0, h)
    blk = lambda h, j: (j, h)
    return _pcall(
        body, name=name, grid=(heads, nq),
        in_specs=[pl.BlockSpec((s, hd), head_rows), pl.BlockSpec((t, hd), blk), pl.BlockSpec((t, hd), blk),
                  pl.BlockSpec((s, hd), head_rows), pl.BlockSpec((s, hd), head_rows),
                  pl.BlockSpec((None, s, 1), lambda h, j: (h, 0, 0)),
                  pl.BlockSpec((None, s, 1), lambda h, j: (h, 0, 0)),
                  pl.BlockSpec((None, None, 1, t), lambda h, j: (h, j, 0, 0))],
        out_specs=[pl.BlockSpec((s, hd), head_rows), pl.BlockSpec((t, hd), blk), pl.BlockSpec((t, hd), blk),
                   pl.BlockSpec((None, s, 1), lambda h, j: (h, 0, 0)),
                   pl.BlockSpec((None, None, 1, t), lambda h, j: (h, j, 0, 0))],
        out_shape=[jax.ShapeDtypeStruct((s, d), F32), jax.ShapeDtypeStruct((s, d), BF16),
                   jax.ShapeDtypeStruct((s, d), BF16), jax.ShapeDtypeStruct((heads, s, 1), F32),
                   jax.ShapeDtypeStruct((heads, nq, 1, t), F32)],
        scratch_shapes=[pltpu.VMEM((s, 1), F32)],
        compiler_params=_params(2),
    )(qn, kn, vb, d_o, o, lse, cum_col, cum_row)


def _fox_gate_fwd(o, og, name):
    def fn(o, og):
        return (o * _sigmoid(og),)

    return _rowwise(fn, [("row", o), og], [("row", o.shape[1], BF16)], name=name)[0]


def _fox_gate_bwd(o, og, dact, name):
    def fn(o, og, dact):
        sg = _sigmoid(og)
        return dact * sg, dact * o * sg * (1.0 - sg)

    d = o.shape[1]
    return _rowwise(fn, [("row", o), og, ("row", dact)], [("row", d, F32), ("row", d, BF16)], name=name)


def _shift_down(x, n):
    rows = lax.broadcasted_iota(jnp.int32, x.shape, 0)
    return jnp.where(rows >= n, pltpu.roll(x, n, 0), 0.0)


def _shift_up(x, n):
    rows = lax.broadcasted_iota(jnp.int32, x.shape, 0)
    return jnp.where(rows < x.shape[0] - n, pltpu.roll(x, x.shape[0] - n, 0), 0.0)


def _conv(u, w_ref, b):
    return w_ref[0:1, :] * _shift_down(u, 2) + w_ref[1:2, :] * _shift_down(u, 1) + w_ref[2:3, :] * u + b


def _conv_act_fwd(u, cw, cb, name, tc=256):
    s, two_f = u.shape
    dff = two_f // 2
    tc = _tile(dff, tc)
    nb = dff // tc

    def body(ug_ref, uv_ref, wg_ref, wv_ref, bg_ref, bv_ref, a_ref):
        gate = _conv(ug_ref[...], wg_ref, bg_ref[...])
        val = _conv(uv_ref[...], wv_ref, bv_ref[...])
        a_ref[...] = (_silu(gate) * val).astype(a_ref.dtype)

    lo, hi = (lambda j: (0, j)), (lambda j: (0, j + nb))
    return _pcall(
        body, name=name, grid=(nb,),
        in_specs=[pl.BlockSpec((s, tc), lo), pl.BlockSpec((s, tc), hi), pl.BlockSpec((3, tc), lo),
                  pl.BlockSpec((3, tc), hi), pl.BlockSpec((1, tc), lo), pl.BlockSpec((1, tc), hi)],
        out_specs=pl.BlockSpec((s, tc), lo),
        out_shape=jax.ShapeDtypeStruct((s, dff), BF16),
        compiler_params=_params(1),
    )(u, u, cw, cw, cb, cb)


def _conv_act_bwd(u, cw, cb, da, name, tc=128):
    s, two_f = u.shape
    dff = two_f // 2
    tc = _tile(dff, tc)
    nb = dff // tc

    def body(ug_ref, uv_ref, wg_ref, wv_ref, bg_ref, bv_ref, da_ref, du_ref, dw_ref, db_ref):
        ug, uv, da = ug_ref[...], uv_ref[...], da_ref[...]
        gate = _conv(ug, wg_ref, bg_ref[...])
        val = _conv(uv, wv_ref, bv_ref[...])
        sg = _sigmoid(gate)
        d_val = da * (gate * sg)
        d_gate = da * val * (sg * (1.0 + gate * (1.0 - sg)))
        for half, (dc, uu, w_ref) in enumerate(((d_gate, ug, wg_ref), (d_val, uv, wv_ref))):
            du = w_ref[0:1, :] * _shift_up(dc, 2) + w_ref[1:2, :] * _shift_up(dc, 1) + w_ref[2:3, :] * dc
            du_ref[half] = du.astype(du_ref.dtype)
            dw_ref[half, 0:1, :] = _colsum(dc * _shift_down(uu, 2))
            dw_ref[half, 1:2, :] = _colsum(dc * _shift_down(uu, 1))
            dw_ref[half, 2:3, :] = _colsum(dc * uu)
            db_ref[half] = _colsum(dc)

    lo, hi = (lambda j: (0, j)), (lambda j: (0, j + nb))
    both = lambda j: (0, 0, j)
    return _pcall(
        body, name=name, grid=(nb,),
        in_specs=[pl.BlockSpec((s, tc), lo), pl.BlockSpec((s, tc), hi), pl.BlockSpec((3, tc), lo),
                  pl.BlockSpec((3, tc), hi), pl.BlockSpec((1, tc), lo), pl.BlockSpec((1, tc), hi),
                  pl.BlockSpec((s, tc), lo)],
        out_specs=[pl.BlockSpec((2, s, tc), both), pl.BlockSpec((2, 3, tc), both), pl.BlockSpec((2, 1, tc), both)],
        out_shape=[jax.ShapeDtypeStruct((2, s, dff), BF16), jax.ShapeDtypeStruct((2, 3, dff), F32),
                   jax.ShapeDtypeStruct((2, 1, dff), F32)],
        compiler_params=_params(1),
    )(u, u, cw, cw, cb, cb, da)


def _adamw_math(w, g, m, v):
    m = ADAM_B1 * m + (1.0 - ADAM_B1) * g
    v = ADAM_B2 * v + (1.0 - ADAM_B2) * (g * g)
    m_hat = m / (1.0 - ADAM_B1 ** ADAM_STEP)
    v_hat = v / (1.0 - ADAM_B2 ** ADAM_STEP)
    delta = -ADAM_LR * (m_hat / (jnp.sqrt(v_hat) + ADAM_EPS) + ADAM_WD * w)
    return delta, m, v


def _adamw(w, g, m, v, name, tr=128):
    layers, r, c = w.shape
    pieces = isinstance(g, (list, tuple))
    tc = c
    if r % 8:
        tr, tc = r, _tile(c, max(LANE, 512 * 1024 // r // LANE * LANE))
    elif r <= tr:
        tr = r
    while r % tr:
        tr -= 8
    nr, nc = r // tr, c // tc
    g_list = list(g) if pieces else [g]
    n_pieces = g_list[0].shape[0] if pieces else 0

    def body(w_ref, *rest):
        g_refs, (m_ref, v_ref, go_ref, d_ref, mo_ref, vo_ref) = rest[:len(g_list)], rest[len(g_list):]

        def update(grad):
            delta, m_new, v_new = _adamw_math(w_ref[...], grad, m_ref[...], v_ref[...])
            go_ref[...], d_ref[...], mo_ref[...], vo_ref[...] = grad, delta, m_new, v_new

        if not pieces:
            update(g_refs[0][...])
            return
        for layer, g_ref in enumerate(g_refs):
            @pl.when(pl.program_id(0) == layer)
            def _(g_ref=g_ref):
                grad = g_ref[0].astype(F32)
                for i in range(1, n_pieces):
                    grad = grad + g_ref[i].astype(F32)
                update(grad)

    spec = pl.BlockSpec((None, tr, tc), lambda l, i, j: (l, i, j))
    if pieces:
        def walk(k):
            def index(l, i, j):
                here = l == k
                return (0, jnp.where(here, i, jnp.where(l < k, 0, nr - 1)), jnp.where(here, j, jnp.where(l < k, 0, nc - 1)))
            return index

        g_specs = [pl.BlockSpec((n_pieces, tr, tc), walk(k)) for k in range(layers)]
    else:
        g_specs = [spec]
    return _pcall(
        body, name=name, grid=(layers, nr, nc), in_specs=[spec] + g_specs + [spec, spec], out_specs=[spec] * 4,
        out_shape=[jax.ShapeDtypeStruct((layers, r, c), F32)] * 4, compiler_params=_params(3),
    )(w, *g_list, m, v)


def _pair_sum(pieces, partner, core, name, tr=512):
    _, r, c = pieces.shape
    tc = c
    if r % 8:
        tr, tc = r, _tile(c, max(LANE, 1024 * 1024 // r // LANE * LANE))
    elif r <= tr:
        tr = r
    while r % tr:
        tr -= 8

    def body(core_ref, mine_ref, partner_ref, out_ref):
        out_ref[...] = (mine_ref[...].astype(F32) + partner_ref[...].astype(F32)).astype(out_ref.dtype)

    return _pcall(
        body, name=name,
        grid_spec=pltpu.PrefetchScalarGridSpec(
            num_scalar_prefetch=1, grid=(4, r // tr, c // tc),
            in_specs=[pl.BlockSpec((None, tr, tc), lambda q, i, j, core_ref: (2 * q + core_ref[0], i, j)),
                      pl.BlockSpec((None, tr, tc), lambda q, i, j, core_ref: (q, i, j))],
            out_specs=pl.BlockSpec((None, tr, tc), lambda q, i, j, core_ref: (q, i, j))),
        out_shape=jax.ShapeDtypeStruct((4, r, c), pieces.dtype), compiler_params=_params(3),
    )(core, pieces, partner)


def _sum8(x, name):
    p = x.shape[2]
    tp = _tile(p, 16 * 1024)

    def body(x_ref, o_ref):
        acc = x_ref[0]
        for i in range(1, N_DEV):
            acc = acc + x_ref[i]
        o_ref[...] = acc

    return _pcall(
        body, name=name, grid=(p // tp,), in_specs=[pl.BlockSpec((N_DEV, 1, tp), lambda i: (0, 0, i))],
        out_specs=pl.BlockSpec((1, tp), lambda i: (0, i)), out_shape=jax.ShapeDtypeStruct((1, p), x.dtype),
        compiler_params=_params(1),
    )(x)


def _exchange(arrays, name, scatter):
    n = len(arrays)
    hbm = pl.BlockSpec(memory_space=pl.ANY)

    def body(*refs):
        ins, outs, token = refs[:n], refs[n:2 * n], refs[2 * n]
        send_sems, recv_sems, local_sems = refs[2 * n + 1:]
        token[...] = jnp.zeros_like(token)
        x, y, c = lax.axis_index("x"), lax.axis_index("y"), lax.axis_index("c")
        me = 4 * x + 2 * y + c
        copies = []
        for a in range(n):
            src_mine = ins[a].at[me] if scatter else ins[a]
            local = pltpu.make_async_copy(src_mine, outs[a].at[me], local_sems.at[a])
            local.start()
            copies.append(local)
            for k in range(1, N_DEV):
                px = 1 - x if k & 4 else x
                py = 1 - y if k & 2 else y
                pc = 1 - c if k & 1 else c
                src = ins[a].at[4 * px + 2 * py + pc] if scatter else ins[a]
                cp = pltpu.make_async_remote_copy(
                    src_ref=src, dst_ref=outs[a].at[me],
                    send_sem=send_sems.at[a * (N_DEV - 1) + k - 1], recv_sem=recv_sems.at[a * (N_DEV - 1) + k - 1],
                    device_id=(px, py, pc), device_id_type=pl.DeviceIdType.MESH)
                cp.start()
                copies.append(cp)
        for cp in copies:
            cp.wait()

    out_shape = [jax.ShapeDtypeStruct(a.shape if scatter else (N_DEV,) + a.shape, a.dtype) for a in arrays]
    res = _pcall(
        body, name=name, in_specs=[hbm] * n, out_specs=[hbm] * n + [pl.BlockSpec(memory_space=pltpu.VMEM)],
        out_shape=out_shape + [jax.ShapeDtypeStruct((8, LANE), F32)],
        scratch_shapes=[pltpu.SemaphoreType.DMA((n * (N_DEV - 1),)), pltpu.SemaphoreType.DMA((n * (N_DEV - 1),)),
                        pltpu.SemaphoreType.DMA((n,))],
        compiler_params=pltpu.CompilerParams(has_side_effects=True),
    )(*arrays)
    return res[:n], res[n][0, 0]


_HBM = pl.BlockSpec(memory_space=pltpu.HBM)
_SEM = pl.BlockSpec(memory_space=pltpu.SEMAPHORE)
_DATAFLOW = pltpu.SideEffectType.DATAFLOW_SIDE_EFFECTING


def _peer(k, x, y, c):
    return (1 - x if k & 4 else x, 1 - y if k & 2 else y, 1 - c if k & 1 else c)


def _pair_plan(x, y, c):
    return [(2 * q + (1 - c), q, (x, y, 1 - c)) for q in range(4)]


def _chip_plan(x, y, c):
    out = []
    for k in _ICI_PEERS:
        px, py, pc = _peer(k, x, y, c)
        out.append((2 * px + py, 2 * x + y, (px, py, pc)))
    return out


def _split_start(arrays, plan, name):
    n = len(arrays)
    lands = [lax.empty((4,) + a.shape[1:], a.dtype) for a in arrays]
    n_copies = len(plan(0, 0, 0))

    def body(*refs):
        srcs, dsts = refs[:n], refs[n:2 * n]
        send_sems, recv_sems, token = refs[4 * n:5 * n], refs[5 * n:6 * n], refs[6 * n]
        copies = plan(lax.axis_index("x"), lax.axis_index("y"), lax.axis_index("c"))
        for a in range(n):
            for j, (src_block, dst_block, peer) in enumerate(copies):
                pltpu.make_async_remote_copy(
                    src_ref=srcs[a].at[src_block], dst_ref=dsts[a].at[dst_block],
                    send_sem=send_sems[a].at[j], recv_sem=recv_sems[a].at[j],
                    device_id=peer, device_id_type=pl.DeviceIdType.MESH).start()
        token[...] = jnp.zeros_like(token)

    sems = [pltpu.SemaphoreType.DMA((n_copies,))] * (2 * n)
    res = _pcall(
        body, name=name,
        in_specs=[_HBM] * (2 * n),
        out_specs=[_HBM] * (2 * n) + [_SEM] * (2 * n) + [pl.BlockSpec(memory_space=pltpu.VMEM)],
        out_shape=[pltpu.HBM(a.shape, a.dtype) for a in arrays] + [pltpu.HBM(l.shape, l.dtype) for l in lands]
        + sems + [jax.ShapeDtypeStruct((8, LANE), F32)],
        input_output_aliases={i: i for i in range(2 * n)},
        compiler_params=pltpu.CompilerParams(has_side_effects=_DATAFLOW),
    )(*[pltpu.with_memory_space_constraint(a, pltpu.HBM) for a in arrays],
      *[pltpu.with_memory_space_constraint(l, pltpu.HBM) for l in lands])
    handles = [(res[a], res[n + a], res[2 * n + a], res[3 * n + a]) for a in range(n)]
    return handles, res[4 * n][0, 0]


def _split_wait(handles, plan, after, name, own_block=None):
    n = len(handles)
    after = list(after) if isinstance(after, (list, tuple)) else [after]

    def body(*refs):
        srcs, dsts = refs[:n], refs[n:2 * n]
        send_sems, recv_sems = refs[2 * n:3 * n], refs[3 * n:4 * n]
        x, y, c = lax.axis_index("x"), lax.axis_index("y"), lax.axis_index("c")
        copies = plan(x, y, c)
        local = []
        if own_block is not None:
            own_sems = refs[-1]
            for a in range(n):
                local.append(pltpu.make_async_copy(srcs[a].at[own_block(x, y, c)], dsts[a].at[own_block(x, y, c)],
                                                   own_sems.at[a]))
                local[-1].start()
        for a in range(n):
            for j, (src_block, dst_block, peer) in enumerate(copies):
                cp = pltpu.make_async_remote_copy(
                    src_ref=srcs[a].at[src_block], dst_ref=dsts[a].at[dst_block],
                    send_sem=send_sems[a].at[j], recv_sem=recv_sems[a].at[j],
                    device_id=peer, device_id_type=pl.DeviceIdType.MESH)
                cp.wait_send()
                cp.wait_recv()
        for cp in local:
            cp.wait()

    srcs, lands = [h[0] for h in handles], [h[1] for h in handles]
    res = _pcall(
        body, name=name,
        in_specs=[_HBM] * (2 * n) + [_SEM] * (2 * n) + [pl.BlockSpec(memory_space=pl.ANY)] * len(after),
        out_specs=[_HBM] * (2 * n),
        out_shape=[pltpu.HBM(t.shape, t.dtype) for t in srcs + lands],
        input_output_aliases={i: i for i in range(2 * n)},
        scratch_shapes=[pltpu.SemaphoreType.DMA((n,))] if own_block is not None else [],
        compiler_params=pltpu.CompilerParams(has_side_effects=_DATAFLOW),
    )(*srcs, *lands, *[h[2] for h in handles], *[h[3] for h in handles], *after)
    return res[:n], res[n:]


_ICI_PEERS = (2, 4, 6)


def _gather2_start(shards, name):
    n = len(shards)
    lands = [lax.empty((N_DEV,) + a.shape, a.dtype) for a in shards]

    def body(*refs):
        srcs, dsts = refs[:n], refs[n:2 * n]
        send_sems, d2d_sems, ici_sems = refs[4 * n:5 * n], refs[5 * n:6 * n], refs[6 * n:7 * n]
        token = refs[7 * n]
        x, y, c = lax.axis_index("x"), lax.axis_index("y"), lax.axis_index("c")
        me = 4 * x + 2 * y + c
        for a in range(n):
            for j, k in enumerate((1,) + _ICI_PEERS):
                recv = d2d_sems[a].at[0] if j == 0 else ici_sems[a].at[j - 1]
                pltpu.make_async_remote_copy(
                    src_ref=srcs[a], dst_ref=dsts[a].at[me], send_sem=send_sems[a].at[j], recv_sem=recv,
                    device_id=_peer(k, x, y, c), device_id_type=pl.DeviceIdType.MESH).start()
        token[...] = jnp.zeros_like(token)

    dma = pltpu.SemaphoreType.DMA
    res = _pcall(
        body, name=name,
        in_specs=[_HBM] * (2 * n),
        out_specs=[_HBM] * (2 * n) + [_SEM] * (3 * n) + [pl.BlockSpec(memory_space=pltpu.VMEM)],
        out_shape=[pltpu.HBM(a.shape, a.dtype) for a in shards] + [pltpu.HBM(l.shape, l.dtype) for l in lands]
        + [dma((4,))] * n + [dma((1,))] * n + [dma((3,))] * n + [jax.ShapeDtypeStruct((8, LANE), F32)],
        input_output_aliases={i: i for i in range(2 * n)},
        compiler_params=pltpu.CompilerParams(has_side_effects=_DATAFLOW),
    )(*[pltpu.with_memory_space_constraint(a, pltpu.HBM) for a in shards],
      *[pltpu.with_memory_space_constraint(l, pltpu.HBM) for l in lands])
    handles = [tuple(res[i * n + a] for i in range(5)) for a in range(n)]
    return handles, res[5 * n][0, 0]


def _gather2_forward(handle, after, name):
    src, land, send_sems, d2d_sem, ici_sems = handle

    def body(land_ref, ici_ref, after_ref, land_out, fwd_send, fwd_recv, token):
        x, y, c = lax.axis_index("x"), lax.axis_index("y"), lax.axis_index("c")
        for j, k in enumerate(_ICI_PEERS):
            px, py, pc = _peer(k, x, y, c)
            block = land_ref.at[4 * px + 2 * py + pc]
            pltpu.make_async_remote_copy(
                src_ref=block, dst_ref=block, send_sem=fwd_send.at[j], recv_sem=ici_ref.at[j],
                device_id=(px, py, pc), device_id_type=pl.DeviceIdType.MESH).wait_recv()
            pltpu.make_async_remote_copy(
                src_ref=block, dst_ref=block, send_sem=fwd_send.at[j], recv_sem=fwd_recv.at[j],
                device_id=(x, y, 1 - c), device_id_type=pl.DeviceIdType.MESH).start()
        token[...] = jnp.zeros_like(token)

    dma = pltpu.SemaphoreType.DMA
    land, fwd_send, fwd_recv, token = _pcall(
        body, name=name,
        in_specs=[_HBM, _SEM, pl.BlockSpec(memory_space=pl.ANY)],
        out_specs=[_HBM, _SEM, _SEM, pl.BlockSpec(memory_space=pltpu.VMEM)],
        out_shape=[pltpu.HBM(land.shape, land.dtype), dma((3,)), dma((3,)), jax.ShapeDtypeStruct((8, LANE), F32)],
        input_output_aliases={0: 0},
        compiler_params=pltpu.CompilerParams(has_side_effects=_DATAFLOW),
    )(land, ici_sems, after)
    return (src, land, send_sems, d2d_sem, fwd_send, fwd_recv), token[0, 0]


def _gather2_wait(handle, after, name):
    src, land, send_sems, d2d_sem, fwd_send, fwd_recv = handle

    def body(src_ref, land_ref, send_ref, d2d_ref, fsend_ref, frecv_ref, after_ref, src_out, land_out, own_sem):
        x, y, c = lax.axis_index("x"), lax.axis_index("y"), lax.axis_index("c")
        me = 4 * x + 2 * y + c
        sibling = (x, y, 1 - c)
        block = land_ref.at[me]
        own = pltpu.make_async_copy(src_ref, block, own_sem)
        own.start()

        def copy(send, recv):
            return pltpu.make_async_remote_copy(src_ref=src_ref, dst_ref=block, send_sem=send, recv_sem=recv,
                                                device_id=sibling, device_id_type=pl.DeviceIdType.MESH)

        for j in range(4):
            copy(send_ref.at[j], d2d_ref.at[0]).wait_send()
        copy(send_ref.at[0], d2d_ref.at[0]).wait_recv()
        for j in range(3):
            copy(fsend_ref.at[j], frecv_ref.at[j]).wait_send()
            copy(fsend_ref.at[j], frecv_ref.at[j]).wait_recv()
        own.wait()

    res = _pcall(
        body, name=name,
        in_specs=[_HBM, _HBM, _SEM, _SEM, _SEM, _SEM, pl.BlockSpec(memory_space=pl.ANY)],
        out_specs=[_HBM, _HBM],
        out_shape=[pltpu.HBM(src.shape, src.dtype), pltpu.HBM(land.shape, land.dtype)],
        input_output_aliases={0: 0, 1: 1},
        scratch_shapes=[pltpu.SemaphoreType.DMA],
        compiler_params=pltpu.CompilerParams(has_side_effects=_DATAFLOW),
    )(src, land, send_sems, d2d_sem, fwd_send, fwd_recv, after)
    return res[0], res[1]


def _pad_cols(x, width=LANE):
    return jnp.pad(x, ((0, 0), (0, width - x.shape[1])))


def _cols_full(g):
    return jnp.transpose(g, (1, 0, 2)).reshape(g.shape[1], -1)


def _ffn_fwd(x1, p, i, tag):
    h2 = _adaln_fwd(x1, p["norm_ffn"][i], p["sc_f"][i], p["sh_f"][i], f"ffn_norm_{tag}")
    u = _matmul(h2, p["fetch"](f"up{i}", h2), name=f"ffn_up_{tag}", tn=1408, b_shards=True)
    a = _conv_act_fwd(u, p["conv_w"][i], p["conv_b"][i], f"ffn_act_{tag}")
    g_f = p["g_f"][i]
    x2, f = _matmul(a, p["fetch"](f"down{i}", a), name=f"ffn_down_{tag}", tk=512, out_dtypes=(F32, F32),
                    epilogue=lambda acc, x1, g: (x1 + (1.0 + g) * acc, acc), extras=(("mn", x1), ("n", g_f)))
    return x2, dict(h2=h2, u=u, a=a, f=f)


def _ffn_bwd(dx2, x1, saved, p, i, tag):
    d = x1.shape[1]
    w_up, w_down = p["fetch"](f"up{i}", None), p["fetch"](f"down{i}", None)
    df, dg_f = _residual_bwd(dx2, saved["f"], p["g_f"][i], f"ffn_res_bwd_{tag}")
    da = _matmul(df, w_down, tb=True, name=f"ffn_down_dx_{tag}", tn=512)
    dw_down = _matmul(saved["a"], df, ta=True, name=f"ffn_down_dw_{tag}", tm=1408, out_dtypes=(BF16,))
    du, dcw, dcb = _conv_act_bwd(saved["u"], p["conv_w"][i], p["conv_b"][i], da, f"ffn_act_bwd_{tag}")
    dcw, dcb = (jnp.concatenate([t[0], t[1]], axis=1) for t in (dcw, dcb))
    tok = p["flush"](du)
    dh2 = _matmul(du, w_up, tb=True, name=f"ffn_up_dx_{tag}", tk=1408, a_halves=True, b_shards=True)
    dw_up = _matmul(saved["h2"], du, ta=True, name=f"ffn_up_dw_{tag}", tn=1408, out_dtypes=(BF16,), b_halves=True,
                    out_shards=True)
    tok = tok + p["send"](f"ffn{i}", [dw_up, dw_down.reshape(N_DEV, -1, d)])
    dx1, dsh, dsc, dgain = _adaln_bwd(x1, dh2, dx2, p["norm_ffn"][i] + tok, p["sc_f"][i], f"ffn_norm_bwd_{tag}")
    grads = dict(conv_w=dcw, conv_b=dcb, norm_ffn=dgain, sh_f=dsh, sc_f=dsc, g_f=dg_f)
    return dx1, grads


def _gla_layer_fwd(x, p, i):
    h1 = _adaln_fwd(x, p["norm_mix"][i], p["sc_m"][i], p["sh_m"][i], "gla_norm")
    w_t, w_tail_t, main = p["fetch"]("gla_in", h1)
    proj = _matmul(h1, w_t, tb=True, b_rows=main, name="gla_in")
    a_tail = _matmul(h1, w_tail_t, tb=True, name="gla_in_tail")
    dk_total = p["gla_wg_p"].shape[1]
    o, states = _gla_fwd(proj, a_tail, p["gla_wg_p"], p["gla_b_gate"], "gla_chunks")
    assert 2 * dk_total == o.shape[1]
    r = ("cols", proj, 2, o.shape[1])
    og = _gla_post_fwd(o, r, p["gla_norm"], "gla_post")
    x1, y = _matmul(og, p["fetch"]("gla_out", og), name="gla_out", out_dtypes=(F32, F32),
                    epilogue=lambda acc, x, g: (x + (1.0 + g) * acc, acc), extras=(("mn", x), ("n", p["g_m"][i])))
    return x1, dict(h1=h1, proj=proj, a_tail=a_tail, o=o, r=r, states=states, og=og, y=y)


def _gla_layer_bwd(dx1, x, sv, p, i):
    d = x.shape[1]
    (w_t, w_tail_t, main), w_out = p["fetch"]("gla_in", None), p["fetch"]("gla_out", None)
    dy, dg_m = _residual_bwd(dx1, sv["y"], p["g_m"][i], "gla_res_bwd")
    dog = _matmul(dy, w_out, tb=True, name="gla_out_dx")
    dw_out = _matmul(sv["og"], dy, ta=True, name="gla_out_dw", out_dtypes=(BF16,))
    tok = p["flush"](dog) + p["send"]("gla_out", [dw_out.reshape(N_DEV, -1, d)])
    d_o, d_r, dgn = _gla_post_bwd(sv["o"], sv["r"], p["gla_norm"] + tok, dog, "gla_post_bwd")
    dq, dk, dv, dga = _gla_bwd(sv["proj"], sv["a_tail"], p["gla_wg_p"], p["gla_b_gate"], sv["states"], d_o,
                               "gla_chunks_bwd")
    tok = p["flush"](dga)
    da_tail = _matmul(dga, p["gla_wg_p"], tb=True, name="gla_gate_dx", out_dtypes=(BF16,))
    dwg = _matmul(sv["a_tail"], dga, ta=True, name="gla_gate_dw")
    dbg = _rowwise(lambda t: (_colsum(t),), [("row", dga)], [("acc", dga.shape[1], F32)], name="gla_gate_db")[0]
    dproj = jnp.concatenate([dq, dk, dv, d_r], axis=1)
    dh_tail = _matmul(da_tail, w_tail_t, name="gla_in_tail_dx")
    dh1 = _matmul(dproj, w_t, b_rows=main, name="gla_in_dx", tk=1024,
                  epilogue=lambda acc, t: (acc + t,), extras=(("mn", dh_tail),))
    dw_main = _matmul(dproj, sv["h1"], ta=True, name="gla_in_dw", out_dtypes=(BF16,))
    dw_tail = _matmul(da_tail, sv["h1"], ta=True, name="gla_in_tail_dw", out_dtypes=(BF16,))
    rank = p["gla_rank"]
    dx, dsh, dsc, dgain = _adaln_bwd(x, dh1, dx1, p["norm_mix"][i] + tok, p["sc_m"][i], "gla_norm_bwd")
    grads = dict(gla_w_gate=dwg[:rank], gla_b_gate=dbg, gla_norm=dgn, norm_mix=dgain, sh_m=dsh, sc_m=dsc, g_m=dg_m,
                 gla_w_in_unsent=(dw_main, dw_tail[:rank]))
    return dx, grads


def _fox_layer_fwd(x, p, i):
    d = x.shape[1]
    hd = p["fox_q_norm"].shape[1]
    heads = d // hd
    s = x.shape[0]
    t = _tile(s, 512)
    h1 = _adaln_fwd(x, p["norm_mix"][i], p["sc_m"][i], p["sh_m"][i], "fox_norm")
    w_t, w_tail_t, main = p["fetch"]("fox_in", h1)
    proj = _matmul(h1, w_t, tb=True, b_rows=main, name="fox_in")
    fl = _matmul(h1, w_tail_t, tb=True, name="fox_in_tail")
    q, k, v, og = (("cols", proj, j, d) for j in range(4))
    qn, kn, vb = _fox_prep(q, k, v, p["fox_q_norm"], p["fox_k_norm"], d, hd, "fox_prep")
    cum = _fox_cum(fl, p["fox_bf_p"], "fox_cum")
    cum_t = jnp.transpose(cum[:, :heads])
    cum_col, cum_row = cum_t[:, :, None], cum_t.reshape(heads, s // t, 1, t)
    o, lse = _fox_attn_fwd(qn, kn, vb, cum_col, cum_row, hd, t, "fox_attn")
    act = _fox_gate_fwd(o, og, "fox_gate")
    x1, y = _matmul(act, p["fetch"]("fox_out", act), name="fox_out", out_dtypes=(F32, F32),
                    epilogue=lambda acc, x, g: (x + (1.0 + g) * acc, acc), extras=(("mn", x), ("n", p["g_m"][i])))
    return x1, dict(h1=h1, q=q, k=k, og=og, fl=fl, qn=qn, kn=kn, vb=vb, cum_col=cum_col, cum_row=cum_row,
                    o=o, lse=lse, act=act, y=y, t=t, hd=hd)


def _fox_layer_bwd(dx1, x, sv, p, i):
    d = x.shape[1]
    hd, t = sv["hd"], sv["t"]
    heads = d // hd
    s = x.shape[0]
    (w_t, w_tail_t, main), w_out = p["fetch"]("fox_in", None), p["fetch"]("fox_out", None)
    dy, dg_m = _residual_bwd(dx1, sv["y"], p["g_m"][i], "fox_res_bwd")
    dact = _matmul(dy, w_out, tb=True, name="fox_out_dx")
    dw_out = _matmul(sv["act"], dy, ta=True, name="fox_out_dw", out_dtypes=(BF16,))
    d_o, d_og = _fox_gate_bwd(sv["o"], sv["og"], dact, "fox_gate_bwd")
    tok_flush = p["flush"](d_og)
    dqn, dkn, dvb, dcq, dck = _fox_attn_bwd(sv["qn"], sv["kn"], sv["vb"], d_o, sv["o"], sv["lse"], sv["cum_col"],
                                            sv["cum_row"], hd, t, "fox_attn_bwd")
    dq, dk, gq, gk = _fox_prep_bwd(sv["q"], sv["k"], dqn, dkn, p["fox_q_norm"], p["fox_k_norm"], hd, "fox_prep_bwd")
    dcum = _pad_cols(jnp.transpose(dcq[:, :, 0] - dck.reshape(heads, s)))
    dfl, dbf = _fox_cum_bwd(dcum, sv["fl"], p["fox_bf_p"], "fox_cum_bwd")
    dfl_b = dfl.astype(BF16)
    dproj = jnp.concatenate([dq, dk, dvb, d_og], axis=1)
    dh_tail = _matmul(dfl_b, w_tail_t, name="fox_in_tail_dx")
    dh1 = _matmul(dproj, w_t, b_rows=main, name="fox_in_dx", tk=1024,
                  epilogue=lambda acc, tl: (acc + tl,), extras=(("mn", dh_tail),))
    dw_main = _matmul(dproj, sv["h1"], ta=True, name="fox_in_dw", out_dtypes=(BF16,))
    dw_tail = _matmul(dfl_b, sv["h1"], ta=True, name="fox_in_tail_dw", out_dtypes=(BF16,))
    dw_in = jnp.concatenate([dw_main, dw_tail[:heads]], axis=0).reshape(N_DEV, -1, d)
    tok = tok_flush + p["send"]("fox", [dw_in, dw_out.reshape(N_DEV, -1, d)])
    dx, dsh, dsc, dgain = _adaln_bwd(x, dh1, dx1, p["norm_mix"][i] + tok, p["sc_m"][i], "fox_norm_bwd")
    grads = dict(fox_b_f=dbf[:, :heads], fox_q_norm=gq.reshape(heads, hd).sum(0, keepdims=True),
                 fox_k_norm=gk.reshape(heads, hd).sum(0, keepdims=True), norm_mix=dgain, sh_m=dsh, sc_m=dsc, g_m=dg_m)
    return dx, grads


SMALL = ("b_mod", "norm_mix", "norm_ffn", "gla_b_gate", "gla_norm", "fox_b_f", "fox_q_norm", "fox_k_norm",
         "ffn_conv_b", "norm_final")
SMALL_SHARDED = ("gla_w_gate", "ffn_conv_w")
BIG = ("gla_w_in", "gla_w_out", "fox_w_in", "fox_w_out", "ffn_w_up", "ffn_w_down")
WEIGHTS = ("w_mod", "b_mod", "norm_mix", "norm_ffn", "gla_w_in", "gla_w_gate", "gla_b_gate", "gla_norm", "gla_w_out",
           "fox_w_in", "fox_b_f", "fox_q_norm", "fox_k_norm", "fox_w_out", "ffn_w_up", "ffn_conv_w", "ffn_conv_b",
           "ffn_w_down", "norm_final")


def _pack(parts):
    flat = jnp.concatenate([p.reshape(-1) for p in parts])
    pad = (-flat.shape[0]) % 1024
    return jnp.pad(flat, (0, pad)).reshape(1, -1)


def _unpack(flat, shapes):
    out, off = [], 0
    for shp in shapes:
        n = 1
        for s in shp:
            n *= s
        out.append(flat[0, off:off + n].reshape(shp))
        off += n
    return out


def kernel(x, c, w_mod, b_mod, norm_mix, norm_ffn, gla_w_in, gla_w_gate, gla_b_gate, gla_norm, gla_w_out, fox_w_in, fox_b_f, fox_q_norm, fox_k_norm, fox_w_out, ffn_w_up, ffn_conv_w, ffn_conv_b, ffn_w_down, norm_final, loss_target, m_w_mod, m_b_mod, m_norm_mix, m_norm_ffn, m_gla_w_in, m_gla_w_gate, m_gla_b_gate, m_gla_norm, m_gla_w_out, m_fox_w_in, m_fox_b_f, m_fox_q_norm, m_fox_k_norm, m_fox_w_out, m_ffn_w_up, m_ffn_conv_w, m_ffn_conv_b, m_ffn_w_down, m_norm_final, v_w_mod, v_b_mod, v_norm_mix, v_norm_ffn, v_gla_w_in, v_gla_w_gate, v_gla_b_gate, v_gla_norm, v_gla_w_out, v_fox_w_in, v_fox_b_f, v_fox_q_norm, v_fox_k_norm, v_fox_w_out, v_ffn_w_up, v_ffn_conv_w, v_ffn_conv_b, v_ffn_w_down, v_norm_final):
    w = dict(w_mod=w_mod, b_mod=b_mod, norm_mix=norm_mix, norm_ffn=norm_ffn, gla_w_in=gla_w_in, gla_w_gate=gla_w_gate,
             gla_b_gate=gla_b_gate, gla_norm=gla_norm, gla_w_out=gla_w_out, fox_w_in=fox_w_in, fox_b_f=fox_b_f,
             fox_q_norm=fox_q_norm, fox_k_norm=fox_k_norm, fox_w_out=fox_w_out, ffn_w_up=ffn_w_up,
             ffn_conv_w=ffn_conv_w, ffn_conv_b=ffn_conv_b, ffn_w_down=ffn_w_down, norm_final=norm_final)
    mom_m = dict(w_mod=m_w_mod, b_mod=m_b_mod, norm_mix=m_norm_mix, norm_ffn=m_norm_ffn, gla_w_in=m_gla_w_in,
                 gla_w_gate=m_gla_w_gate, gla_b_gate=m_gla_b_gate, gla_norm=m_gla_norm, gla_w_out=m_gla_w_out,
                 fox_w_in=m_fox_w_in, fox_b_f=m_fox_b_f, fox_q_norm=m_fox_q_norm, fox_k_norm=m_fox_k_norm,
                 fox_w_out=m_fox_w_out, ffn_w_up=m_ffn_w_up, ffn_conv_w=m_ffn_conv_w, ffn_conv_b=m_ffn_conv_b,
                 ffn_w_down=m_ffn_w_down, norm_final=m_norm_final)
    mom_v = dict(w_mod=v_w_mod, b_mod=v_b_mod, norm_mix=v_norm_mix, norm_ffn=v_norm_ffn, gla_w_in=v_gla_w_in,
                 gla_w_gate=v_gla_w_gate, gla_b_gate=v_gla_b_gate, gla_norm=v_gla_norm, gla_w_out=v_gla_w_out,
                 fox_w_in=v_fox_w_in, fox_b_f=v_fox_b_f, fox_q_norm=v_fox_q_norm, fox_k_norm=v_fox_k_norm,
                 fox_w_out=v_fox_w_out, ffn_w_up=v_ffn_w_up, ffn_conv_w=v_ffn_conv_w, ffn_conv_b=v_ffn_conv_b,
                 ffn_w_down=v_ffn_w_down, norm_final=v_norm_final)

    me = 4 * lax.axis_index("x") + 2 * lax.axis_index("y") + lax.axis_index("c")
    xs, target = x[0], loss_target[0]
    s, d = xs.shape
    depth = w_mod.shape[0]
    mod_cols = w_mod.shape[2]
    rank = gla_w_gate.shape[1]
    hd = fox_q_norm.shape[1]
    fox_heads = d // hd
    dk_total = gla_w_gate.shape[2] * N_DEV

    cond = c * (1.0 / (1.0 + jnp.exp(-c)))
    g, _ = _exchange([gla_w_gate[0], ffn_conv_w, cond], "gather_small", scatter=False)
    cond_all = g[2][:, 0, :]

    cond_pad = jnp.pad(cond_all, ((0, 16 - N_DEV), (0, 0)))
    mod_part = []
    for i in range(depth):
        b_cols = lax.dynamic_slice(b_mod[i:i + 1], (0, me * mod_cols), (1, mod_cols))
        mod_part.append(_matmul(cond_pad, w_mod[i], name=f"mod_{i}", tn=768,
                                epilogue=lambda acc, b: (acc + b,), extras=(("n", b_cols),))[:N_DEV])
    (mod_all,), tok_mod = _exchange([jnp.stack(mod_part)], "gather_mod", scatter=False)
    mod = lax.dynamic_index_in_dim(mod_all, me, axis=2, keepdims=False)
    mod = jnp.transpose(mod, (1, 0, 2)).reshape(depth, 6, 1, d)

    big_names = ["gla_in", "gla_out", "up0", "down0", "fox_in", "fox_out", "up1", "down1"]
    big_shards = [jnp.transpose(gla_w_in[0] + tok_mod), gla_w_out[0], ffn_w_up[0], ffn_w_down[0],
                  jnp.transpose(fox_w_in[0]), fox_w_out[0], ffn_w_up[1], ffn_w_down[1]]
    big_shards = [t.astype(BF16) for t in big_shards]
    handles, tok0 = _gather2_start(big_shards, "gather_weights_start")
    ready, forwarded = {}, {}

    def split_tail(full_t, tail):
        main = full_t.shape[0] - tail
        return full_t, jnp.pad(full_t[main:], ((0, LANE - tail), (0, 0))), main

    def forward(idx, after):
        key = big_names[idx]
        forwarded[key] = _gather2_forward(handles[idx], after, f"gather_{key}_forward")

    def fetch(key, after):
        if key not in ready:
            idx = big_names.index(key)
            if idx == 0:
                forward(0, after)
            handle, _ = forwarded[key]
            _, full = _gather2_wait(handle, after, f"gather_{key}_wait")
            if idx + 1 < len(big_names):
                forward(idx + 1, full)
            if key == "gla_in":
                ready[key] = split_tail(full.reshape(-1, d), rank)
            elif key == "fox_in":
                ready[key] = split_tail(full.reshape(-1, d), fox_heads)
            elif key.startswith("up"):
                ready[key] = full
            else:
                ready[key] = full.reshape(-1, d)
        return ready[key]

    pending, sent = [], {}
    core = lax.axis_index("c").astype(jnp.int32).reshape(1)

    def send(key, pieces):
        hs, tok = _split_start(pieces, _pair_plan, f"scatter_{key}_pair_start")
        pending.append((key, hs))
        return tok

    def flush(after):
        tok = 0.0
        while pending:
            key, hs = pending.pop(0)
            mine, partner = _split_wait(hs, _pair_plan, after, f"scatter_{key}_pair_wait")
            sums = [_pair_sum(pc, pt, core, f"scatter_{key}_pair_sum{a}")
                    for a, (pc, pt) in enumerate(zip(mine, partner))]
            sent[key], t = _split_start(sums, _chip_plan, f"scatter_{key}_chip_start")
            tok = tok + t
        return tok

    p = dict(
        fetch=fetch, send=send, flush=flush,
        gla_wg_p=jnp.pad(_cols_full(g[0]), ((0, LANE - rank), (0, 0))),
        conv_w=[jnp.transpose(g[1][:, i], (1, 0, 2)).reshape(ffn_conv_w.shape[1], -1) for i in range(depth)],
        conv_b=[ffn_conv_b[i:i + 1] for i in range(depth)],
        gla_b_gate=gla_b_gate, gla_norm=gla_norm, fox_q_norm=fox_q_norm, fox_k_norm=fox_k_norm,
        fox_bf_p=_pad_cols(fox_b_f), gla_rank=rank,
        norm_mix=[norm_mix[i:i + 1] + (tok0 if i == 0 else 0.0) for i in range(depth)],
        norm_ffn=[norm_ffn[i:i + 1] for i in range(depth)],
    )

    for j, nm in enumerate(("sh_m", "sc_m", "g_m", "sh_f", "sc_f", "g_f")):
        p[nm] = [mod[i, j] for i in range(depth)]

    acts, saved = [xs], []
    for i in range(depth):
        layer_fwd = _gla_layer_fwd if i % 2 == 0 else _fox_layer_fwd
        x1, sv_mix = layer_fwd(acts[-1], p, i)
        x2, sv_ffn = _ffn_fwd(x1, p, i, str(i))
        saved.append((acts[-1], x1, sv_mix, sv_ffn))
        acts.append(x2)
    dx, d_norm_final, loss_part = _final_loss(acts[-1], target, norm_final.reshape(1, d), "final_loss")

    lg = [None] * depth
    for i in reversed(range(depth)):
        x_in, x1, sv_mix, sv_ffn = saved[i]
        dx, g_ffn = _ffn_bwd(dx, x1, sv_ffn, p, i, str(i))
        layer_bwd = _gla_layer_bwd if i % 2 == 0 else _fox_layer_bwd
        dx, g_mix = layer_bwd(dx, x_in, sv_mix, p, i)
        lg[i] = {**g_ffn, **g_mix}
    grad_x = dx[None]

    gla_l = [i for i in range(depth) if i % 2 == 0]
    fox_l = [i for i in range(depth) if i % 2 == 1]
    small_parts = dict(
        norm_mix=jnp.concatenate([lg[i]["norm_mix"] for i in range(depth)]),
        norm_ffn=jnp.concatenate([lg[i]["norm_ffn"] for i in range(depth)]),
        gla_b_gate=jnp.concatenate([lg[i]["gla_b_gate"] for i in gla_l]),
        gla_norm=jnp.concatenate([lg[i]["gla_norm"] for i in gla_l]),
        fox_b_f=jnp.concatenate([lg[i]["fox_b_f"] for i in fox_l]),
        fox_q_norm=jnp.concatenate([lg[i]["fox_q_norm"] for i in fox_l]),
        fox_k_norm=jnp.concatenate([lg[i]["fox_k_norm"] for i in fox_l]),
        ffn_conv_b=jnp.concatenate([lg[i]["conv_b"] for i in range(depth)]),
        norm_final=d_norm_final,
        gla_w_gate=jnp.stack([lg[i]["gla_w_gate"] for i in gla_l]),
        ffn_conv_w=jnp.stack([lg[i]["conv_w"] for i in range(depth)]),
        loss=loss_part[:, :1],
    )
    order = ("norm_mix", "norm_ffn", "gla_b_gate", "gla_norm", "fox_b_f", "fox_q_norm", "fox_k_norm", "ffn_conv_b",
             "norm_final", "gla_w_gate", "ffn_conv_w", "loss")
    packed = _pack([small_parts[nm] for nm in order])
    dmod = jnp.stack([jnp.concatenate([lg[i][nm] for nm in ("sh_m", "sc_m", "g_m", "sh_f", "sc_f", "g_f")], axis=1)
                      for i in range(depth)])
    (packed_all, dmod_all), tok_small = _exchange([packed, dmod], "gather_small_grads", scatter=False)
    dw_main, dw_tail = lg[0]["gla_w_in_unsent"]
    dw_in_t = jnp.concatenate([dw_main, dw_tail + tok_small.astype(BF16)], axis=0)
    tok_last = send("gla_in", [dw_in_t.reshape(N_DEV, -1, d)])
    packed_all = packed_all + tok_last
    summed = _unpack(_sum8(packed_all, "sum_small_grads"), [small_parts[nm].shape for nm in order])
    small_g = dict(zip(order, summed))
    loss = small_g["loss"][0, 0]
    dmod_all = dmod_all[:, :, 0, :]

    grads = {}
    cond_t = _pad_cols(jnp.transpose(cond_all)).astype(BF16)
    dmod_cols = lax.dynamic_slice(dmod_all, (0, 0, me * mod_cols), (N_DEV, depth, mod_cols))
    g_w_mod = []
    for i in range(depth):
        rhs = jnp.pad(dmod_cols[:, i], ((0, LANE - N_DEV), (0, 0)))
        g_w_mod.append(_matmul(cond_t, rhs, name=f"mod_dw_{i}", tn=768))
    grads["w_mod"] = jnp.stack(g_w_mod)
    small_g["b_mod"] = _sum8(dmod_all.reshape(N_DEV, 1, -1), "sum_b_mod").reshape(depth, -1)

    received = {}

    def arrive(key, after):
        _, received[key] = _split_wait(sent[key], _chip_plan, after, f"scatter_{key}_chip_wait",
                                       own_block=lambda x, y, c: 2 * x + y)

    for key in ("ffn1", "fox", "ffn0", "gla_out"):
        arrive(key, packed_all)

    out_g, out_d, out_m, out_v = {}, {}, {}, {}

    def update(nm, g_arr, transposed=False):
        swap = (lambda t: jnp.transpose(t, (0, 2, 1))) if transposed else (lambda t: t)
        res = _adamw(swap(w[nm]), g_arr, swap(mom_m[nm]), swap(mom_v[nm]), f"adamw_{nm}")
        out_g[nm], out_d[nm], out_m[nm], out_v[nm] = (swap(t) for t in res)

    update("ffn_w_up", [received[f"ffn{i}"][0] for i in range(depth)])
    tok_flush = flush(out_g["ffn_w_up"])
    update("gla_w_out", [received["gla_out"][0]])
    update("fox_w_in", [received["fox"][0]], transposed=True)
    update("fox_w_out", [received["fox"][1]])
    update("ffn_w_down", [received[f"ffn{i}"][1] for i in range(depth)])
    update("w_mod", grads["w_mod"])

    gate_cols = gla_w_gate.shape[2]
    conv_cols = ffn_conv_w.shape[2]
    local_small = dict(small_g)
    local_small["gla_w_gate"] = lax.dynamic_slice_in_dim(small_g["gla_w_gate"], me * gate_cols, gate_cols, axis=2)
    local_small["ffn_conv_w"] = lax.dynamic_slice_in_dim(small_g["ffn_conv_w"], me * conv_cols, conv_cols, axis=2)
    names = SMALL + SMALL_SHARDED
    shapes = [w[nm].shape for nm in names]
    res = _adamw(_pack([w[nm] for nm in names])[None], (_pack([local_small[nm] for nm in names]) + tok_flush)[None],
                 _pack([mom_m[nm] for nm in names])[None], _pack([mom_v[nm] for nm in names])[None], "adamw_small")
    for tgt, flat in zip((out_g, out_d, out_m, out_v), res):
        for nm, arr in zip(names, _unpack(flat[0], shapes)):
            tgt[nm] = arr

    arrive("gla_in", [out_d[nm] for nm in ("gla_w_out", "fox_w_in", "fox_w_out", "ffn_w_up", "ffn_w_down", "w_mod")])
    update("gla_w_in", [received["gla_in"][0]], transposed=True)

    return (loss, grad_x, *[out_g[n] for n in WEIGHTS], *[out_d[n] for n in WEIGHTS],
            *[out_m[n] for n in WEIGHTS], *[out_v[n] for n in WEIGHTS])
```

```python
import jax
import jax.numpy as jnp
from jax import lax
from jax.experimental import pallas as pl
from jax.experimental.pallas import tpu as pltpu

F32, BF16 = jnp.float32, jnp.bfloat16
N_DEV = 8
GLA_HEADS = 4
GLA_TAU = 16.0
GLA_CHUNK = 64
NORM_EPS = 1e-6
ADAM_LR, ADAM_B1, ADAM_B2, ADAM_EPS, ADAM_WD, ADAM_STEP = 0.001, 0.9, 0.999, 1e-08, 0.01, 10
LANE = 128
VMEM_LIMIT = 56 * 1024 * 1024
NEG = -1e30


def _pcall(body, **kw):
    return pl.pallas_call(body, **kw)


def _params(n_axes):
    return pltpu.CompilerParams(dimension_semantics=("arbitrary",) * n_axes, vmem_limit_bytes=VMEM_LIMIT)


def _tile(dim, pref):
    if dim <= pref:
        return dim
    t = pref
    while dim % t:
        t -= LANE
    assert t > 0, (dim, pref)
    return t


def _dot(a, b, ta=False, tb=False):
    dims = (((0,) if ta else (1,), (1,) if tb else (0,)), ((), ()))
    return lax.dot_general(a.astype(BF16), b.astype(BF16), dims, preferred_element_type=F32)


def _split3(x):
    hi = x.astype(BF16)
    r1 = x - hi.astype(F32)
    mid = r1.astype(BF16)
    lo = (r1 - mid.astype(F32)).astype(BF16)
    return hi, mid, lo


def _tri_matmul(tri, x):
    hi, mid, lo = _split3(x)
    return _dot(tri, hi) + _dot(tri, mid) + _dot(tri, lo)


def _tri(n, upper=False):
    r = lax.broadcasted_iota(jnp.int32, (n, n), 0)
    c = lax.broadcasted_iota(jnp.int32, (n, n), 1)
    return jnp.where((r <= c) if upper else (r >= c), 1.0, 0.0).astype(BF16)


def _log_sigmoid(x):
    return jnp.minimum(x, 0.0) - jnp.log(1.0 + jnp.exp(-jnp.abs(x)))


def _sigmoid(x):
    return 1.0 / (1.0 + jnp.exp(-x))


def _silu(x):
    return x * _sigmoid(x)


def _dsilu(x):
    s = _sigmoid(x)
    return s * (1.0 + x * (1.0 - s))


def _matmul(a, b, *, name, ta=False, tb=False, out_dtypes=(F32,), tm=1024, tn=1024, tk=2048,
            epilogue=None, extras=(), a_halves=False, b_halves=False, b_shards=False, out_shards=False,
            b_rows=None):
    if a_halves:
        assert not ta
        m, k = a.shape[1], 2 * a.shape[2]
    else:
        m, k = (a.shape[1], a.shape[0]) if ta else a.shape
    if b_halves:
        assert not tb and b.shape[1] == k
        n = 2 * b.shape[2]
    elif b_shards:
        n = b.shape[1] if tb else N_DEV * b.shape[2]
        assert (N_DEV * b.shape[2] if tb else b.shape[1]) == k, (a.shape, b.shape, ta, tb)
    else:
        rows = b.shape[0] if b_rows is None else b_rows
        n = rows if tb else b.shape[1]
        assert (b.shape[1] if tb else rows) == k, (a.shape, b.shape, ta, tb)
    n_unit = n // N_DEV if (out_shards or (b_shards and not tb)) else (n // 2 if b_halves else n)
    k_unit = k // N_DEV if (b_shards and tb) else (k // 2 if a_halves else k)
    tm, tn, tk = _tile(m, tm), _tile(n_unit, tn), _tile(k_unit, tk)
    nk = k // tk
    if a_halves:
        a_spec = pl.BlockSpec((None, tm, tk), lambda i, j, kk: (kk // (nk // 2), i, kk % (nk // 2)))
    elif ta:
        a_spec = pl.BlockSpec((tk, tm), lambda i, j, kk: (kk, i))
    else:
        a_spec = pl.BlockSpec((tm, tk), lambda i, j, kk: (i, kk))
    n_per, k_per = n // tn // N_DEV, nk // N_DEV
    if b_halves:
        b_spec = pl.BlockSpec((None, tk, tn), lambda i, j, kk: (j // (n // tn // 2), kk, j % (n // tn // 2)))
    elif b_shards and tb:
        b_spec = pl.BlockSpec((None, tn, tk), lambda i, j, kk: (kk // k_per, j, kk % k_per))
    elif b_shards:
        b_spec = pl.BlockSpec((None, tk, tn), lambda i, j, kk: (j // n_per, kk, j % n_per))
    elif tb:
        b_spec = pl.BlockSpec((tn, tk), lambda i, j, kk: (j, kk))
    else:
        b_spec = pl.BlockSpec((tk, tn), lambda i, j, kk: (kk, j))
    ex_specs = []
    for kind, arr in extras:
        if kind == "mn":
            assert arr.shape == (m, n), (arr.shape, m, n)
            ex_specs.append(pl.BlockSpec((tm, tn), lambda i, j, kk: (i, j)))
        else:
            assert arr.shape == (1, n), (arr.shape, n)
            ex_specs.append(pl.BlockSpec((1, tn), lambda i, j, kk: (0, j)))
    n_ex, n_out = len(extras), len(out_dtypes)

    def body(a_ref, b_ref, *rest):
        ex, outs, acc = rest[:n_ex], rest[n_ex:n_ex + n_out], rest[-1]
        kk = pl.program_id(2)

        @pl.when(kk == 0)
        def _():
            acc[...] = jnp.zeros_like(acc)

        acc[...] += _dot(a_ref[...], b_ref[...], ta, tb)

        @pl.when(kk == nk - 1)
        def _():
            if epilogue is None:
                vals = (acc[...],)
            else:
                vals = epilogue(acc[...], *[e[...] for e in ex])
            for o, v in zip(outs, vals):
                o[...] = v.astype(o.dtype)

    if out_shards:
        out_spec = pl.BlockSpec((None, tm, tn), lambda i, j, kk: (j // n_per, i, j % n_per))
        out_dims = (N_DEV, m, n // N_DEV)
    else:
        out_spec = pl.BlockSpec((tm, tn), lambda i, j, kk: (i, j))
        out_dims = (m, n)
    res = _pcall(
        body, name=name, grid=(m // tm, n // tn, nk),
        in_specs=[a_spec, b_spec] + ex_specs,
        out_specs=[out_spec] * n_out,
        out_shape=[jax.ShapeDtypeStruct(out_dims, d) for d in out_dtypes],
        scratch_shapes=[pltpu.VMEM((tm, tn), F32)],
        compiler_params=_params(3),
    )(a, b, *[arr for _, arr in extras])
    return res[0] if n_out == 1 else res


def _rowwise(fn, ins, outs, *, name, tr=128):
    rows = next(e[1].shape[0] for e in ins if e[0] != "full")
    tr = _tile(rows, tr)
    in_specs = []
    for entry in ins:
        kind, arr = entry[0], entry[1]
        assert kind == "full" or (arr.shape[0] == rows and arr.ndim == 2)
        if kind == "row":
            in_specs.append(pl.BlockSpec((tr, arr.shape[1]), lambda i: (i, 0)))
        elif kind == "cols":
            in_specs.append(pl.BlockSpec((tr, entry[3]), lambda i, cb=entry[2]: (i, cb)))
        else:
            in_specs.append(pl.BlockSpec(arr.shape, lambda i, nd=arr.ndim: (0,) * nd))
    out_specs, out_shape = [], []
    for kind, w, dt in outs:
        if kind == "row":
            out_specs.append(pl.BlockSpec((tr, w), lambda i: (i, 0)))
            out_shape.append(jax.ShapeDtypeStruct((rows, w), dt))
        else:
            out_specs.append(pl.BlockSpec((1, w), lambda i: (0, 0)))
            out_shape.append(jax.ShapeDtypeStruct((1, w), dt))
    n_in = len(ins)

    def body(*refs):
        i = pl.program_id(0)
        vals = fn(*[r[...] for r in refs[:n_in]])
        for (kind, _, _), o, v in zip(outs, refs[n_in:], vals):
            if kind == "row":
                o[...] = v.astype(o.dtype)
            else:
                @pl.when(i == 0)
                def _(o=o):
                    o[...] = jnp.zeros_like(o)

                o[...] += v.astype(o.dtype)

    return _pcall(body, name=name, grid=(rows // tr,), in_specs=in_specs, out_specs=out_specs,
                  out_shape=out_shape, compiler_params=_params(1))(*[e[1] for e in ins])


def _colsum(x):
    return jnp.sum(x, axis=0, keepdims=True)


def _norm_stats(x):
    rstd = lax.rsqrt(jnp.mean(x * x, axis=-1, keepdims=True) + NORM_EPS)
    return x * rstd, rstd


def _norm_bwd(dxhat, xhat, rstd):
    return rstd * (dxhat - xhat * jnp.mean(dxhat * xhat, axis=-1, keepdims=True))


def _adaln_fwd(x, gain, sc, sh, name):
    def fn(x, gain, sc, sh):
        xhat, _ = _norm_stats(x)
        return ((xhat * gain) * (1.0 + sc) + sh,)

    return _rowwise(fn, [("row", x), ("full", gain), ("full", sc), ("full", sh)],
                    [("row", x.shape[1], BF16)], name=name)[0]


def _adaln_bwd(x, dh, dres, gain, sc, name):
    d = x.shape[1]

    def fn(x, dh, dres, gain, sc):
        xhat, rstd = _norm_stats(x)
        dxhat = dh * (gain * (1.0 + sc))
        dx = dres + _norm_bwd(dxhat, xhat, rstd)
        return dx, _colsum(dh), _colsum(dh * (xhat * gain)), _colsum(dh * xhat * (1.0 + sc))

    return _rowwise(fn, [("row", x), ("row", dh), ("row", dres), ("full", gain), ("full", sc)],
                    [("row", d, F32), ("acc", d, F32), ("acc", d, F32), ("acc", d, F32)], name=name)


def _residual_bwd(dx, y, g, name):
    d = dx.shape[1]

    def fn(dx, y, g):
        return dx * (1.0 + g), _colsum(dx * y)

    return _rowwise(fn, [("row", dx), ("row", y), ("full", g)], [("row", d, BF16), ("acc", d, F32)], name=name)


def _final_loss(x, target, gain, name):
    d = x.shape[1]

    def fn(x, t, gain):
        xhat, rstd = _norm_stats(x)
        err = xhat * gain - t
        dy = err * (1.0 / d)
        loss = 0.5 * jnp.sum(jnp.mean(err * err, axis=-1, keepdims=True), axis=0, keepdims=True)
        dx = _norm_bwd(dy * gain, xhat, rstd)
        return dx, _colsum(dy * xhat), jnp.broadcast_to(loss, (1, LANE))

    return _rowwise(fn, [("row", x), ("row", target), ("full", gain)],
                    [("row", d, F32), ("acc", d, F32), ("acc", LANE, F32)], name=name)


def _gla_gates(q, k, a, wg, bg, scale, c):
    ga = _dot(a, wg) + bg
    la = _log_sigmoid(ga) * (1.0 / GLA_TAU)
    b = _tri_matmul(_tri(c), la)
    bl = _colsum(la)
    eb, enb, eend = jnp.exp(b), jnp.exp(-b), jnp.exp(bl - b)
    q = q * scale
    return dict(ga=ga, eb=eb, enb=enb, eend=eend, dec=jnp.exp(bl), q_dec=q * eb, k_inv=k * enb, k_end=k * eend)


def _causal(c):
    return lax.broadcasted_iota(jnp.int32, (c, c), 0) >= lax.broadcasted_iota(jnp.int32, (c, c), 1)


def _gla_specs(heads, c, dk, dv, chunk):
    return [
        pl.BlockSpec((c, heads * dk), lambda n: (chunk(n), 0)),
        pl.BlockSpec((c, heads * dk), lambda n: (chunk(n), 1)),
        pl.BlockSpec((c, heads * dv), lambda n: (chunk(n), 1)),
        pl.BlockSpec((c, LANE), lambda n: (chunk(n), 0)),
        pl.BlockSpec((LANE, heads * dk), lambda n: (0, 0)),
        pl.BlockSpec((1, heads * dk), lambda n: (0, 0)),
    ]


def _gla_fwd(proj, a_tail, wg_p, bg, name):
    s = proj.shape[0]
    heads, c = GLA_HEADS, GLA_CHUNK
    dk = wg_p.shape[1] // heads
    dv = 2 * dk
    n_chunks = s // c
    scale = dk ** -0.5

    def body(q_ref, k_ref, v_ref, a_ref, wg_ref, bg_ref, o_ref, st_ref, state):
        @pl.when(pl.program_id(0) == 0)
        def _():
            state[...] = jnp.zeros_like(state)

        a = a_ref[...]
        for h in range(heads):
            sk, sv = slice(h * dk, (h + 1) * dk), slice(h * dv, (h + 1) * dv)
            g = _gla_gates(q_ref[:, sk], k_ref[:, sk], a, wg_ref[:, sk], bg_ref[:, sk], scale, c)
            v = v_ref[:, sv]
            st = state[h]
            attn = jnp.where(_causal(c), _dot(g["q_dec"], g["k_inv"], tb=True), 0.0)
            o_ref[:, sv] = _dot(attn, v) + _dot(g["q_dec"], st, tb=True)
            st_ref[h] = st.astype(st_ref.dtype)
            state[h] = g["dec"] * st + _dot(v, g["k_end"], ta=True)

    return _pcall(
        body, name=name, grid=(n_chunks,),
        in_specs=_gla_specs(heads, c, dk, dv, lambda n: n),
        out_specs=[pl.BlockSpec((c, heads * dv), lambda n: (n, 0)),
                   pl.BlockSpec((heads, None, dv, dk), lambda n: (0, n, 0, 0))],
        out_shape=[jax.ShapeDtypeStruct((s, heads * dv), F32),
                   jax.ShapeDtypeStruct((heads, n_chunks, dv, dk), BF16)],
        scratch_shapes=[pltpu.VMEM((heads, dv, dk), F32)],
        compiler_params=_params(1),
    )(proj, proj, proj, a_tail, wg_p, bg)


def _gla_bwd(proj, a_tail, wg_p, bg, states, d_o, name):
    s = proj.shape[0]
    heads, c = GLA_HEADS, GLA_CHUNK
    dk = wg_p.shape[1] // heads
    dv = 2 * dk
    n_chunks = s // c
    scale = dk ** -0.5

    def body(q_ref, k_ref, v_ref, a_ref, wg_ref, bg_ref, st_ref, do_ref, dq_ref, dk_ref, dv_ref, dga_ref, dstate):
        @pl.when(pl.program_id(0) == 0)
        def _():
            dstate[...] = jnp.zeros_like(dstate)

        a = a_ref[...]
        mask = _causal(c)
        for h in range(heads):
            sk, sv = slice(h * dk, (h + 1) * dk), slice(h * dv, (h + 1) * dv)
            g = _gla_gates(q_ref[:, sk], k_ref[:, sk], a, wg_ref[:, sk], bg_ref[:, sk], scale, c)
            v, st, dst, d_out = v_ref[:, sv], st_ref[h], dstate[h], do_ref[:, sv]
            q_dec, k_inv, k_end = g["q_dec"], g["k_inv"], g["k_end"]
            attn = jnp.where(mask, _dot(q_dec, k_inv, tb=True), 0.0)
            d_attn = jnp.where(mask, _dot(d_out, v, tb=True), 0.0)
            d_qdec = _dot(d_attn, k_inv) + _dot(d_out, st)
            d_kinv = _dot(d_attn, q_dec, ta=True)
            d_kend = _dot(v, dst)
            dv_ref[:, sv] = (_dot(attn, d_out, ta=True) + _dot(k_end, dst, tb=True)).astype(dv_ref.dtype)
            d_dec = jnp.sum(dst * st.astype(F32), axis=0, keepdims=True)
            dstate[h] = g["dec"] * dst + _dot(d_out, q_dec, ta=True)

            dq_ref[:, sk] = (d_qdec * (scale * g["eb"])).astype(dq_ref.dtype)
            dk_ref[:, sk] = (d_kinv * g["enb"] + d_kend * g["eend"]).astype(dk_ref.dtype)
            kk = d_kend * k_end
            db = d_qdec * q_dec - d_kinv * k_inv - kk
            dbl = jnp.sum(kk, axis=0, keepdims=True) + d_dec * g["dec"]
            last = lax.broadcasted_iota(jnp.int32, db.shape, 0) == c - 1
            db = db + jnp.where(last, dbl, 0.0)
            dla = _tri_matmul(_tri(c, upper=True), db)
            dga_ref[:, sk] = dla * (1.0 / GLA_TAU) * _sigmoid(-g["ga"])

    chunk = lambda n: n_chunks - 1 - n
    rev = lambda n: (chunk(n), 0)
    return _pcall(
        body, name=name, grid=(n_chunks,),
        in_specs=_gla_specs(heads, c, dk, dv, chunk) + [
            pl.BlockSpec((heads, None, dv, dk), lambda n: (0, chunk(n), 0, 0)),
            pl.BlockSpec((c, heads * dv), rev)],
        out_specs=[pl.BlockSpec((c, heads * dk), rev), pl.BlockSpec((c, heads * dk), rev),
                   pl.BlockSpec((c, heads * dv), rev), pl.BlockSpec((c, heads * dk), rev)],
        out_shape=[jax.ShapeDtypeStruct((s, heads * dk), BF16), jax.ShapeDtypeStruct((s, heads * dk), BF16),
                   jax.ShapeDtypeStruct((s, heads * dv), BF16), jax.ShapeDtypeStruct((s, heads * dk), F32)],
        scratch_shapes=[pltpu.VMEM((heads, dv, dk), F32)],
        compiler_params=_params(1),
    )(proj, proj, proj, a_tail, wg_p, bg, states, d_o)


def _gla_post_fwd(o, r, gn, name):
    dvt = o.shape[1]
    dv = dvt // GLA_HEADS

    def fn(o, r, gn):
        outs = []
        for h in range(GLA_HEADS):
            sl = slice(h * dv, (h + 1) * dv)
            ohat, _ = _norm_stats(o[:, sl])
            outs.append((ohat * gn[:, sl]) * _silu(r[:, sl]))
        return (jnp.concatenate(outs, axis=1),)

    return _rowwise(fn, [("row", o), r, ("full", gn)], [("row", dvt, BF16)], name=name)[0]


def _gla_post_bwd(o, r, gn, dog, name):
    dvt = o.shape[1]
    dv = dvt // GLA_HEADS

    def fn(o, r, gn, dog):
        d_o, d_r, d_g = [], [], []
        for h in range(GLA_HEADS):
            sl = slice(h * dv, (h + 1) * dv)
            ohat, rstd = _norm_stats(o[:, sl])
            g, rr, dd = gn[:, sl], r[:, sl], dog[:, sl]
            d_r.append(dd * (ohat * g) * _dsilu(rr))
            don = dd * _silu(rr)
            d_g.append(_colsum(don * ohat))
            d_o.append(_norm_bwd(don * g, ohat, rstd))
        return jnp.concatenate(d_o, axis=1), jnp.concatenate(d_r, axis=1), jnp.concatenate(d_g, axis=1)

    return _rowwise(fn, [("row", o), r, ("full", gn), ("row", dog)],
                    [("row", dvt, F32), ("row", dvt, BF16), ("acc", dvt, F32)], name=name)


def _fox_prep(q, k, v, qg, kg, d, hd, name):
    heads = d // hd
    scale = hd ** -0.5

    def fn(q, k, v, qg, kg):
        qs, ks = [], []
        for h in range(heads):
            sl = slice(h * hd, (h + 1) * hd)
            qs.append(_norm_stats(q[:, sl])[0] * qg * scale)
            ks.append(_norm_stats(k[:, sl])[0] * kg)
        return jnp.concatenate(qs, axis=1), jnp.concatenate(ks, axis=1), v

    return _rowwise(fn, [q, k, v, ("full", qg), ("full", kg)],
                    [("row", d, BF16)] * 3, name=name)


def _fox_prep_bwd(q, k, dqn, dkn, qg, kg, hd, name):
    d = dqn.shape[1]
    heads = d // hd
    scale = hd ** -0.5

    def fn(q, k, dqn, dkn, qg, kg):
        dq, dk, gq, gk = [], [], [], []
        for h in range(heads):
            sl = slice(h * hd, (h + 1) * hd)
            for x, dxn, g, s, dl, gl in ((q, dqn, qg, scale, dq, gq), (k, dkn, kg, 1.0, dk, gk)):
                xhat, rstd = _norm_stats(x[:, sl])
                dn = dxn[:, sl] * s
                gl.append(_colsum(dn * xhat))
                dl.append(_norm_bwd(dn * g, xhat, rstd))
        cat = lambda t: jnp.concatenate(t, axis=1)
        return cat(dq), cat(dk), cat(gq), cat(gk)

    return _rowwise(fn, [q, k, ("row", dqn), ("row", dkn), ("full", qg), ("full", kg)],
                    [("row", d, BF16), ("row", d, BF16), ("acc", d, F32), ("acc", d, F32)], name=name)


def _fox_cum(fl, bf_p, name, tb=256):
    s = fl.shape[0]
    tb = _tile(s, tb)

    def body(fl_ref, bf_ref, cum_ref, carry):
        @pl.when(pl.program_id(0) == 0)
        def _():
            carry[...] = jnp.zeros_like(carry)

        lf = _log_sigmoid(fl_ref[...] + bf_ref[...])
        cum_ref[...] = _tri_matmul(_tri(tb), lf) + carry[...]
        carry[...] += _colsum(lf)

    return _pcall(
        body, name=name, grid=(s // tb,),
        in_specs=[pl.BlockSpec((tb, LANE), lambda i: (i, 0)), pl.BlockSpec((1, LANE), lambda i: (0, 0))],
        out_specs=pl.BlockSpec((tb, LANE), lambda i: (i, 0)),
        out_shape=jax.ShapeDtypeStruct((s, LANE), F32),
        scratch_shapes=[pltpu.VMEM((1, LANE), F32)],
        compiler_params=_params(1),
    )(fl, bf_p)


def _fox_cum_bwd(dcum, fl, bf_p, name, tb=256):
    s = fl.shape[0]
    tb = _tile(s, tb)
    nb = s // tb

    def body(dc_ref, fl_ref, bf_ref, dfl_ref, dbf_ref, carry):
        @pl.when(pl.program_id(0) == 0)
        def _():
            carry[...] = jnp.zeros_like(carry)
            dbf_ref[...] = jnp.zeros_like(dbf_ref)

        dc = dc_ref[...]
        dlf = _tri_matmul(_tri(tb, upper=True), dc) + carry[...]
        carry[...] += _colsum(dc)
        dfl = dlf * _sigmoid(-(fl_ref[...] + bf_ref[...]))
        dfl_ref[...] = dfl
        dbf_ref[...] += _colsum(dfl)

    rev = lambda i: (nb - 1 - i, 0)
    return _pcall(
        body, name=name, grid=(nb,),
        in_specs=[pl.BlockSpec((tb, LANE), rev), pl.BlockSpec((tb, LANE), rev), pl.BlockSpec((1, LANE), lambda i: (0, 0))],
        out_specs=[pl.BlockSpec((tb, LANE), rev), pl.BlockSpec((1, LANE), lambda i: (0, 0))],
        out_shape=[jax.ShapeDtypeStruct((s, LANE), F32), jax.ShapeDtypeStruct((1, LANE), F32)],
        scratch_shapes=[pltpu.VMEM((1, LANE), F32)],
        compiler_params=_params(1),
    )(dcum, fl, bf_p)


def _fox_attn_fwd(qn, kn, vb, cum_col, cum_row, hd, t, name):
    s, d = qn.shape
    heads = d // hd
    nq = s // t

    def body(q_ref, k_ref, v_ref, cc_ref, cr_ref, o_ref, lse_ref):
        qi = pl.program_id(1)
        q = q_ref[...]
        cq = cc_ref[...]
        qpos = qi * t + lax.broadcasted_iota(jnp.int32, (t, 1), 0)

        def step(kj, carry, diagonal=False):
            m, l, acc = carry
            off = pl.multiple_of(kj * t, t)
            ks, vs = k_ref[pl.ds(off, t), :], v_ref[pl.ds(off, t), :]
            sc = _dot(q, ks, tb=True) + cq - cr_ref[kj]
            if diagonal:
                kpos = off + lax.broadcasted_iota(jnp.int32, (1, t), 1)
                sc = jnp.where(kpos <= qpos, sc, NEG)
            m_new = jnp.maximum(m, jnp.max(sc, axis=1, keepdims=True))
            alpha = jnp.exp(m - m_new)
            p = jnp.exp(sc - m_new)
            return m_new, alpha * l + jnp.sum(p, axis=1, keepdims=True), alpha * acc + _dot(p, vs)

        init = (jnp.full((t, 1), NEG, F32), jnp.zeros((t, 1), F32), jnp.zeros((t, hd), F32))
        m, l, acc = step(qi, lax.fori_loop(0, qi, step, init), diagonal=True)
        o_ref[...] = acc / l
        lse_ref[...] = m + jnp.log(l)

    return _pcall(
        body, name=name, grid=(heads, nq),
        in_specs=[pl.BlockSpec((t, hd), lambda h, i: (i, h)),
                  pl.BlockSpec((s, hd), lambda h, i: (0, h)),
                  pl.BlockSpec((s, hd), lambda h, i: (0, h)),
                  pl.BlockSpec((None, t, 1), lambda h, i: (h, i, 0)),
                  pl.BlockSpec((None, nq, 1, t), lambda h, i: (h, 0, 0, 0))],
        out_specs=[pl.BlockSpec((t, hd), lambda h, i: (i, h)), pl.BlockSpec((None, t, 1), lambda h, i: (h, i, 0))],
        out_shape=[jax.ShapeDtypeStruct((s, d), F32), jax.ShapeDtypeStruct((heads, s, 1), F32)],
        compiler_params=_params(2),
    )(qn, kn, vb, cum_col, cum_row)


def _fox_attn_bwd(qn, kn, vb, d_o, o, lse, cum_col, cum_row, hd, t, name):
    s, d = qn.shape
    heads = d // hd
    nq = s // t

    def body(q_ref, k_ref, v_ref, do_ref, o_ref, lse_ref, cc_ref, cr_ref,
             dq_ref, dk_ref, dv_ref, dcq_ref, dck_ref, delta):
        kj = pl.program_id(1)

        @pl.when(kj == 0)
        def _():
            dq_ref[...] = jnp.zeros_like(dq_ref)
            dcq_ref[...] = jnp.zeros_like(dcq_ref)
            delta[...] = jnp.sum(do_ref[...] * o_ref[...], axis=1, keepdims=True)

        ks, vs, cr = k_ref[...], v_ref[...], cr_ref[...]
        kpos = kj * t + lax.broadcasted_iota(jnp.int32, (1, t), 1)

        def step(qi, carry, diagonal=False):
            dk, dv, dck = carry
            rows = pl.ds(pl.multiple_of(qi * t, t), t)
            q, d_out = q_ref[rows, :], do_ref[rows, :]
            sc = _dot(q, ks, tb=True) + cc_ref[rows, :] - cr
            p = jnp.exp(sc - lse_ref[rows, :])
            if diagonal:
                qpos = qi * t + lax.broadcasted_iota(jnp.int32, (t, 1), 0)
                p = jnp.where(kpos <= qpos, p, 0.0)
            ds = p * (_dot(d_out, vs, tb=True) - delta[rows, :])
            dq_ref[rows, :] += _dot(ds, ks)
            dcq_ref[rows, :] += jnp.sum(ds, axis=1, keepdims=True)
            return dk + _dot(ds, q, ta=True), dv + _dot(p, d_out, ta=True), dck + _colsum(ds)

        init = (jnp.zeros((t, hd), F32), jnp.zeros((t, hd), F32), jnp.zeros((1, t), F32))
        dk, dv, dck = lax.fori_loop(kj + 1, nq, step, step(kj, init, diagonal=True))
        dk_ref[...] = dk.astype(dk_ref.dtype)
        dv_ref[...] = dv.astype(dv_ref.dtype)
        dck_ref[...] = dck

    head_rows = lambda h, j: (0, h)
    blk = lambda h, j: (j, h)
    return _pcall(
        body, name=name, grid=(heads, nq),
        in_specs=[pl.BlockSpec((s, hd), head_rows), pl.BlockSpec((t, hd), blk), pl.BlockSpec((t, hd), blk),
                  pl.BlockSpec((s, hd), head_rows), pl.BlockSpec((s, hd), head_rows),
                  pl.BlockSpec((None, s, 1), lambda h, j: (h, 0, 0)),
                  pl.BlockSpec((None, s, 1), lambda h, j: (h, 0, 0)),
                  pl.BlockSpec((None, None, 1, t), lambda h, j: (h, j, 0, 0))],
        out_specs=[pl.BlockSpec((s, hd), head_rows), pl.BlockSpec((t, hd), blk), pl.BlockSpec((t, hd), blk),
                   pl.BlockSpec((None, s, 1), lambda h, j: (h, 0, 0)),
                   pl.BlockSpec((None, None, 1, t), lambda h, j: (h, j, 0, 0))],
        out_shape=[jax.ShapeDtypeStruct((s, d), F32), jax.ShapeDtypeStruct((s, d), BF16),
                   jax.ShapeDtypeStruct((s, d), BF16), jax.ShapeDtypeStruct((heads, s, 1), F32),
                   jax.ShapeDtypeStruct((heads, nq, 1, t), F32)],
        scratch_shapes=[pltpu.VMEM((s, 1), F32)],
        compiler_params=_params(2),
    )(qn, kn, vb, d_o, o, lse, cum_col, cum_row)


def _fox_gate_fwd(o, og, name):
    def fn(o, og):
        return (o * _sigmoid(og),)

    return _rowwise(fn, [("row", o), og], [("row", o.shape[1], BF16)], name=name)[0]


def _fox_gate_bwd(o, og, dact, name):
    def fn(o, og, dact):
        sg = _sigmoid(og)
        return dact * sg, dact * o * sg * (1.0 - sg)

    d = o.shape[1]
    return _rowwise(fn, [("row", o), og, ("row", dact)], [("row", d, F32), ("row", d, BF16)], name=name)


def _shift_down(x, n):
    rows = lax.broadcasted_iota(jnp.int32, x.shape, 0)
    return jnp.where(rows >= n, pltpu.roll(x, n, 0), 0.0)


def _shift_up(x, n):
    rows = lax.broadcasted_iota(jnp.int32, x.shape, 0)
    return jnp.where(rows < x.shape[0] - n, pltpu.roll(x, x.shape[0] - n, 0), 0.0)


def _conv(u, w_ref, b):
    return w_ref[0:1, :] * _shift_down(u, 2) + w_ref[1:2, :] * _shift_down(u, 1) + w_ref[2:3, :] * u + b


def _conv_act_fwd(u, cw, cb, name, tc=256):
    s, two_f = u.shape
    dff = two_f // 2
    tc = _tile(dff, tc)
    nb = dff // tc

    def body(ug_ref, uv_ref, wg_ref, wv_ref, bg_ref, bv_ref, a_ref):
        gate = _conv(ug_ref[...], wg_ref, bg_ref[...])
        val = _conv(uv_ref[...], wv_ref, bv_ref[...])
        a_ref[...] = (_silu(gate) * val).astype(a_ref.dtype)

    lo, hi = (lambda j: (0, j)), (lambda j: (0, j + nb))
    return _pcall(
        body, name=name, grid=(nb,),
        in_specs=[pl.BlockSpec((s, tc), lo), pl.BlockSpec((s, tc), hi), pl.BlockSpec((3, tc), lo),
                  pl.BlockSpec((3, tc), hi), pl.BlockSpec((1, tc), lo), pl.BlockSpec((1, tc), hi)],
        out_specs=pl.BlockSpec((s, tc), lo),
        out_shape=jax.ShapeDtypeStruct((s, dff), BF16),
        compiler_params=_params(1),
    )(u, u, cw, cw, cb, cb)


def _conv_act_bwd(u, cw, cb, da, name, tc=128):
    s, two_f = u.shape
    dff = two_f // 2
    tc = _tile(dff, tc)
    nb = dff // tc

    def body(ug_ref, uv_ref, wg_ref, wv_ref, bg_ref, bv_ref, da_ref, du_ref, dw_ref, db_ref):
        ug, uv, da = ug_ref[...], uv_ref[...], da_ref[...]
        gate = _conv(ug, wg_ref, bg_ref[...])
        val = _conv(uv, wv_ref, bv_ref[...])
        sg = _sigmoid(gate)
        d_val = da * (gate * sg)
        d_gate = da * val * (sg * (1.0 + gate * (1.0 - sg)))
        for half, (dc, uu, w_ref) in enumerate(((d_gate, ug, wg_ref), (d_val, uv, wv_ref))):
            du = w_ref[0:1, :] * _shift_up(dc, 2) + w_ref[1:2, :] * _shift_up(dc, 1) + w_ref[2:3, :] * dc
            du_ref[half] = du.astype(du_ref.dtype)
            dw_ref[half, 0:1, :] = _colsum(dc * _shift_down(uu, 2))
            dw_ref[half, 1:2, :] = _colsum(dc * _shift_down(uu, 1))
            dw_ref[half, 2:3, :] = _colsum(dc * uu)
            db_ref[half] = _colsum(dc)

    lo, hi = (lambda j: (0, j)), (lambda j: (0, j + nb))
    both = lambda j: (0, 0, j)
    return _pcall(
        body, name=name, grid=(nb,),
        in_specs=[pl.BlockSpec((s, tc), lo), pl.BlockSpec((s, tc), hi), pl.BlockSpec((3, tc), lo),
                  pl.BlockSpec((3, tc), hi), pl.BlockSpec((1, tc), lo), pl.BlockSpec((1, tc), hi),
                  pl.BlockSpec((s, tc), lo)],
        out_specs=[pl.BlockSpec((2, s, tc), both), pl.BlockSpec((2, 3, tc), both), pl.BlockSpec((2, 1, tc), both)],
        out_shape=[jax.ShapeDtypeStruct((2, s, dff), BF16), jax.ShapeDtypeStruct((2, 3, dff), F32),
                   jax.ShapeDtypeStruct((2, 1, dff), F32)],
        compiler_params=_params(1),
    )(u, u, cw, cw, cb, cb, da)


def _adamw_math(w, g, m, v):
    m = ADAM_B1 * m + (1.0 - ADAM_B1) * g
    v = ADAM_B2 * v + (1.0 - ADAM_B2) * (g * g)
    m_hat = m / (1.0 - ADAM_B1 ** ADAM_STEP)
    v_hat = v / (1.0 - ADAM_B2 ** ADAM_STEP)
    delta = -ADAM_LR * (m_hat / (jnp.sqrt(v_hat) + ADAM_EPS) + ADAM_WD * w)
    return delta, m, v


def _adamw(w, g, m, v, name, tr=128):
    layers, r, c = w.shape
    pieces = isinstance(g, (list, tuple))
    tc = c
    if r % 8:
        tr, tc = r, _tile(c, max(LANE, 512 * 1024 // r // LANE * LANE))
    elif r <= tr:
        tr = r
    while r % tr:
        tr -= 8
    nr, nc = r // tr, c // tc
    g_list = list(g) if pieces else [g]
    n_pieces = g_list[0].shape[0] if pieces else 0

    def body(w_ref, *rest):
        g_refs, (m_ref, v_ref, go_ref, d_ref, mo_ref, vo_ref) = rest[:len(g_list)], rest[len(g_list):]

        def update(grad):
            delta, m_new, v_new = _adamw_math(w_ref[...], grad, m_ref[...], v_ref[...])
            go_ref[...], d_ref[...], mo_ref[...], vo_ref[...] = grad, delta, m_new, v_new

        if not pieces:
            update(g_refs[0][...])
            return
        for layer, g_ref in enumerate(g_refs):
            @pl.when(pl.program_id(0) == layer)
            def _(g_ref=g_ref):
                grad = g_ref[0].astype(F32)
                for i in range(1, n_pieces):
                    grad = grad + g_ref[i].astype(F32)
                update(grad)

    spec = pl.BlockSpec((None, tr, tc), lambda l, i, j: (l, i, j))
    if pieces:
        def walk(k):
            def index(l, i, j):
                here = l == k
                return (0, jnp.where(here, i, jnp.where(l < k, 0, nr - 1)), jnp.where(here, j, jnp.where(l < k, 0, nc - 1)))
            return index

        g_specs = [pl.BlockSpec((n_pieces, tr, tc), walk(k)) for k in range(layers)]
    else:
        g_specs = [spec]
    return _pcall(
        body, name=name, grid=(layers, nr, nc), in_specs=[spec] + g_specs + [spec, spec], out_specs=[spec] * 4,
        out_shape=[jax.ShapeDtypeStruct((layers, r, c), F32)] * 4, compiler_params=_params(3),
    )(w, *g_list, m, v)


def _pair_sum(pieces, partner, core, name, tr=512):
    _, r, c = pieces.shape
    tc = c
    if r % 8:
        tr, tc = r, _tile(c, max(LANE, 1024 * 1024 // r // LANE * LANE))
    elif r <= tr:
        tr = r
    while r % tr:
        tr -= 8

    def body(core_ref, mine_ref, partner_ref, out_ref):
        out_ref[...] = (mine_ref[...].astype(F32) + partner_ref[...].astype(F32)).astype(out_ref.dtype)

    return _pcall(
        body, name=name,
        grid_spec=pltpu.PrefetchScalarGridSpec(
            num_scalar_prefetch=1, grid=(4, r // tr, c // tc),
            in_specs=[pl.BlockSpec((None, tr, tc), lambda q, i, j, core_ref: (2 * q + core_ref[0], i, j)),
                      pl.BlockSpec((None, tr, tc), lambda q, i, j, core_ref: (q, i, j))],
            out_specs=pl.BlockSpec((None, tr, tc), lambda q, i, j, core_ref: (q, i, j))),
        out_shape=jax.ShapeDtypeStruct((4, r, c), pieces.dtype), compiler_params=_params(3),
    )(core, pieces, partner)


def _sum8(x, name):
    p = x.shape[2]
    tp = _tile(p, 16 * 1024)

    def body(x_ref, o_ref):
        acc = x_ref[0]
        for i in range(1, N_DEV):
            acc = acc + x_ref[i]
        o_ref[...] = acc

    return _pcall(
        body, name=name, grid=(p // tp,), in_specs=[pl.BlockSpec((N_DEV, 1, tp), lambda i: (0, 0, i))],
        out_specs=pl.BlockSpec((1, tp), lambda i: (0, i)), out_shape=jax.ShapeDtypeStruct((1, p), x.dtype),
        compiler_params=_params(1),
    )(x)


def _exchange(arrays, name, scatter):
    n = len(arrays)
    hbm = pl.BlockSpec(memory_space=pl.ANY)

    def body(*refs):
        ins, outs, token = refs[:n], refs[n:2 * n], refs[2 * n]
        send_sems, recv_sems, local_sems = refs[2 * n + 1:]
        token[...] = jnp.zeros_like(token)
        x, y, c = lax.axis_index("x"), lax.axis_index("y"), lax.axis_index("c")
        me = 4 * x + 2 * y + c
        copies = []
        for a in range(n):
            src_mine = ins[a].at[me] if scatter else ins[a]
            local = pltpu.make_async_copy(src_mine, outs[a].at[me], local_sems.at[a])
            local.start()
            copies.append(local)
            for k in range(1, N_DEV):
                px = 1 - x if k & 4 else x
                py = 1 - y if k & 2 else y
                pc = 1 - c if k & 1 else c
                src = ins[a].at[4 * px + 2 * py + pc] if scatter else ins[a]
                cp = pltpu.make_async_remote_copy(
                    src_ref=src, dst_ref=outs[a].at[me],
                    send_sem=send_sems.at[a * (N_DEV - 1) + k - 1], recv_sem=recv_sems.at[a * (N_DEV - 1) + k - 1],
                    device_id=(px, py, pc), device_id_type=pl.DeviceIdType.MESH)
                cp.start()
                copies.append(cp)
        for cp in copies:
            cp.wait()

    out_shape = [jax.ShapeDtypeStruct(a.shape if scatter else (N_DEV,) + a.shape, a.dtype) for a in arrays]
    res = _pcall(
        body, name=name, in_specs=[hbm] * n, out_specs=[hbm] * n + [pl.BlockSpec(memory_space=pltpu.VMEM)],
        out_shape=out_shape + [jax.ShapeDtypeStruct((8, LANE), F32)],
        scratch_shapes=[pltpu.SemaphoreType.DMA((n * (N_DEV - 1),)), pltpu.SemaphoreType.DMA((n * (N_DEV - 1),)),
                        pltpu.SemaphoreType.DMA((n,))],
        compiler_params=pltpu.CompilerParams(has_side_effects=True),
    )(*arrays)
    return res[:n], res[n][0, 0]


_HBM = pl.BlockSpec(memory_space=pltpu.HBM)
_SEM = pl.BlockSpec(memory_space=pltpu.SEMAPHORE)
_DATAFLOW = pltpu.SideEffectType.DATAFLOW_SIDE_EFFECTING


def _peer(k, x, y, c):
    return (1 - x if k & 4 else x, 1 - y if k & 2 else y, 1 - c if k & 1 else c)


def _pair_plan(x, y, c):
    return [(2 * q + (1 - c), q, (x, y, 1 - c)) for q in range(4)]


def _chip_plan(x, y, c):
    out = []
    for k in _ICI_PEERS:
        px, py, pc = _peer(k, x, y, c)
        out.append((2 * px + py, 2 * x + y, (px, py, pc)))
    return out


def _split_start(arrays, plan, name):
    n = len(arrays)
    lands = [lax.empty((4,) + a.shape[1:], a.dtype) for a in arrays]
    n_copies = len(plan(0, 0, 0))

    def body(*refs):
        srcs, dsts = refs[:n], refs[n:2 * n]
        send_sems, recv_sems, token = refs[4 * n:5 * n], refs[5 * n:6 * n], refs[6 * n]
        copies = plan(lax.axis_index("x"), lax.axis_index("y"), lax.axis_index("c"))
        for a in range(n):
            for j, (src_block, dst_block, peer) in enumerate(copies):
                pltpu.make_async_remote_copy(
                    src_ref=srcs[a].at[src_block], dst_ref=dsts[a].at[dst_block],
                    send_sem=send_sems[a].at[j], recv_sem=recv_sems[a].at[j],
                    device_id=peer, device_id_type=pl.DeviceIdType.MESH).start()
        token[...] = jnp.zeros_like(token)

    sems = [pltpu.SemaphoreType.DMA((n_copies,))] * (2 * n)
    res = _pcall(
        body, name=name,
        in_specs=[_HBM] * (2 * n),
        out_specs=[_HBM] * (2 * n) + [_SEM] * (2 * n) + [pl.BlockSpec(memory_space=pltpu.VMEM)],
        out_shape=[pltpu.HBM(a.shape, a.dtype) for a in arrays] + [pltpu.HBM(l.shape, l.dtype) for l in lands]
        + sems + [jax.ShapeDtypeStruct((8, LANE), F32)],
        input_output_aliases={i: i for i in range(2 * n)},
        compiler_params=pltpu.CompilerParams(has_side_effects=_DATAFLOW),
    )(*[pltpu.with_memory_space_constraint(a, pltpu.HBM) for a in arrays],
      *[pltpu.with_memory_space_constraint(l, pltpu.HBM) for l in lands])
    handles = [(res[a], res[n + a], res[2 * n + a], res[3 * n + a]) for a in range(n)]
    return handles, res[4 * n][0, 0]


def _split_wait(handles, plan, after, name):
    n = len(handles)
    after = list(after) if isinstance(after, (list, tuple)) else [after]

    def body(*refs):
        srcs, dsts = refs[:n], refs[n:2 * n]
        send_sems, recv_sems = refs[2 * n:3 * n], refs[3 * n:4 * n]
        copies = plan(lax.axis_index("x"), lax.axis_index("y"), lax.axis_index("c"))
        for a in range(n):
            for j, (src_block, dst_block, peer) in enumerate(copies):
                cp = pltpu.make_async_remote_copy(
                    src_ref=srcs[a].at[src_block], dst_ref=dsts[a].at[dst_block],
                    send_sem=send_sems[a].at[j], recv_sem=recv_sems[a].at[j],
                    device_id=peer, device_id_type=pl.DeviceIdType.MESH)
                cp.wait_send()
                cp.wait_recv()

    srcs, lands = [h[0] for h in handles], [h[1] for h in handles]
    res = _pcall(
        body, name=name,
        in_specs=[_HBM] * (2 * n) + [_SEM] * (2 * n) + [pl.BlockSpec(memory_space=pl.ANY)] * len(after),
        out_specs=[_HBM] * (2 * n),
        out_shape=[pltpu.HBM(t.shape, t.dtype) for t in srcs + lands],
        input_output_aliases={i: i for i in range(2 * n)},
        compiler_params=pltpu.CompilerParams(has_side_effects=_DATAFLOW),
    )(*srcs, *lands, *[h[2] for h in handles], *[h[3] for h in handles], *after)
    return res[:n], res[n:]


_ICI_PEERS = (2, 4, 6)


def _gather2_start(shards, name):
    n = len(shards)
    lands = [lax.empty((N_DEV,) + a.shape, a.dtype) for a in shards]

    def body(*refs):
        srcs, dsts = refs[:n], refs[n:2 * n]
        send_sems, d2d_sems, ici_sems = refs[4 * n:5 * n], refs[5 * n:6 * n], refs[6 * n:7 * n]
        token = refs[7 * n]
        x, y, c = lax.axis_index("x"), lax.axis_index("y"), lax.axis_index("c")
        me = 4 * x + 2 * y + c
        for a in range(n):
            for j, k in enumerate((1,) + _ICI_PEERS):
                recv = d2d_sems[a].at[0] if j == 0 else ici_sems[a].at[j - 1]
                pltpu.make_async_remote_copy(
                    src_ref=srcs[a], dst_ref=dsts[a].at[me], send_sem=send_sems[a].at[j], recv_sem=recv,
                    device_id=_peer(k, x, y, c), device_id_type=pl.DeviceIdType.MESH).start()
        token[...] = jnp.zeros_like(token)

    dma = pltpu.SemaphoreType.DMA
    res = _pcall(
        body, name=name,
        in_specs=[_HBM] * (2 * n),
        out_specs=[_HBM] * (2 * n) + [_SEM] * (3 * n) + [pl.BlockSpec(memory_space=pltpu.VMEM)],
        out_shape=[pltpu.HBM(a.shape, a.dtype) for a in shards] + [pltpu.HBM(l.shape, l.dtype) for l in lands]
        + [dma((4,))] * n + [dma((1,))] * n + [dma((3,))] * n + [jax.ShapeDtypeStruct((8, LANE), F32)],
        input_output_aliases={i: i for i in range(2 * n)},
        compiler_params=pltpu.CompilerParams(has_side_effects=_DATAFLOW),
    )(*[pltpu.with_memory_space_constraint(a, pltpu.HBM) for a in shards],
      *[pltpu.with_memory_space_constraint(l, pltpu.HBM) for l in lands])
    handles = [tuple(res[i * n + a] for i in range(5)) for a in range(n)]
    return handles, res[5 * n][0, 0]


def _gather2_forward(handle, after, name):
    src, land, send_sems, d2d_sem, ici_sems = handle

    def body(land_ref, ici_ref, after_ref, land_out, fwd_send, fwd_recv, token):
        x, y, c = lax.axis_index("x"), lax.axis_index("y"), lax.axis_index("c")
        for j, k in enumerate(_ICI_PEERS):
            px, py, pc = _peer(k, x, y, c)
            block = land_ref.at[4 * px + 2 * py + pc]
            pltpu.make_async_remote_copy(
                src_ref=block, dst_ref=block, send_sem=fwd_send.at[j], recv_sem=ici_ref.at[j],
                device_id=(px, py, pc), device_id_type=pl.DeviceIdType.MESH).wait_recv()
            pltpu.make_async_remote_copy(
                src_ref=block, dst_ref=block, send_sem=fwd_send.at[j], recv_sem=fwd_recv.at[j],
                device_id=(x, y, 1 - c), device_id_type=pl.DeviceIdType.MESH).start()
        token[...] = jnp.zeros_like(token)

    dma = pltpu.SemaphoreType.DMA
    land, fwd_send, fwd_recv, token = _pcall(
        body, name=name,
        in_specs=[_HBM, _SEM, pl.BlockSpec(memory_space=pl.ANY)],
        out_specs=[_HBM, _SEM, _SEM, pl.BlockSpec(memory_space=pltpu.VMEM)],
        out_shape=[pltpu.HBM(land.shape, land.dtype), dma((3,)), dma((3,)), jax.ShapeDtypeStruct((8, LANE), F32)],
        input_output_aliases={0: 0},
        compiler_params=pltpu.CompilerParams(has_side_effects=_DATAFLOW),
    )(land, ici_sems, after)
    return (src, land, send_sems, d2d_sem, fwd_send, fwd_recv), token[0, 0]


def _gather2_wait(handle, after, name):
    src, land, send_sems, d2d_sem, fwd_send, fwd_recv = handle

    def body(src_ref, land_ref, send_ref, d2d_ref, fsend_ref, frecv_ref, after_ref, src_out, land_out):
        x, y, c = lax.axis_index("x"), lax.axis_index("y"), lax.axis_index("c")
        me = 4 * x + 2 * y + c
        sibling = (x, y, 1 - c)
        block = land_ref.at[me]

        def copy(send, recv):
            return pltpu.make_async_remote_copy(src_ref=src_ref, dst_ref=block, send_sem=send, recv_sem=recv,
                                                device_id=sibling, device_id_type=pl.DeviceIdType.MESH)

        for j in range(4):
            copy(send_ref.at[j], d2d_ref.at[0]).wait_send()
        copy(send_ref.at[0], d2d_ref.at[0]).wait_recv()
        for j in range(3):
            copy(fsend_ref.at[j], frecv_ref.at[j]).wait_send()
            copy(fsend_ref.at[j], frecv_ref.at[j]).wait_recv()

    res = _pcall(
        body, name=name,
        in_specs=[_HBM, _HBM, _SEM, _SEM, _SEM, _SEM, pl.BlockSpec(memory_space=pl.ANY)],
        out_specs=[_HBM, _HBM],
        out_shape=[pltpu.HBM(src.shape, src.dtype), pltpu.HBM(land.shape, land.dtype)],
        input_output_aliases={0: 0, 1: 1},
        compiler_params=pltpu.CompilerParams(has_side_effects=_DATAFLOW),
    )(src, land, send_sems, d2d_sem, fwd_send, fwd_recv, after)
    return res[0], res[1]


def _pad_cols(x, width=LANE):
    return jnp.pad(x, ((0, 0), (0, width - x.shape[1])))


def _cols_full(g):
    return jnp.transpose(g, (1, 0, 2)).reshape(g.shape[1], -1)


def _ffn_fwd(x1, p, i, tag):
    h2 = _adaln_fwd(x1, p["norm_ffn"][i], p["sc_f"][i], p["sh_f"][i], f"ffn_norm_{tag}")
    u = _matmul(h2, p["fetch"](f"up{i}", h2), name=f"ffn_up_{tag}", tn=1408, b_shards=True)
    a = _conv_act_fwd(u, p["conv_w"][i], p["conv_b"][i], f"ffn_act_{tag}")
    g_f = p["g_f"][i]
    x2, f = _matmul(a, p["fetch"](f"down{i}", a), name=f"ffn_down_{tag}", tk=512, out_dtypes=(F32, F32),
                    epilogue=lambda acc, x1, g: (x1 + (1.0 + g) * acc, acc), extras=(("mn", x1), ("n", g_f)))
    return x2, dict(h2=h2, u=u, a=a, f=f)


def _ffn_bwd(dx2, x1, saved, p, i, tag):
    d = x1.shape[1]
    w_up, w_down = p["fetch"](f"up{i}", None), p["fetch"](f"down{i}", None)
    df, dg_f = _residual_bwd(dx2, saved["f"], p["g_f"][i], f"ffn_res_bwd_{tag}")
    da = _matmul(df, w_down, tb=True, name=f"ffn_down_dx_{tag}", tn=512)
    dw_down = _matmul(saved["a"], df, ta=True, name=f"ffn_down_dw_{tag}", tm=1408, out_dtypes=(BF16,))
    du, dcw, dcb = _conv_act_bwd(saved["u"], p["conv_w"][i], p["conv_b"][i], da, f"ffn_act_bwd_{tag}")
    dcw, dcb = (jnp.concatenate([t[0], t[1]], axis=1) for t in (dcw, dcb))
    tok = p["flush"](du)
    dh2 = _matmul(du, w_up, tb=True, name=f"ffn_up_dx_{tag}", tk=1408, a_halves=True, b_shards=True)
    dw_up = _matmul(saved["h2"], du, ta=True, name=f"ffn_up_dw_{tag}", tn=1408, out_dtypes=(BF16,), b_halves=True,
                    out_shards=True)
    tok = tok + p["send"](f"ffn{i}", [dw_up, dw_down.reshape(N_DEV, -1, d)])
    dx1, dsh, dsc, dgain = _adaln_bwd(x1, dh2, dx2, p["norm_ffn"][i] + tok, p["sc_f"][i], f"ffn_norm_bwd_{tag}")
    grads = dict(conv_w=dcw, conv_b=dcb, norm_ffn=dgain, sh_f=dsh, sc_f=dsc, g_f=dg_f)
    return dx1, grads


def _gla_layer_fwd(x, p, i):
    h1 = _adaln_fwd(x, p["norm_mix"][i], p["sc_m"][i], p["sh_m"][i], "gla_norm")
    w_t, w_tail_t, main = p["fetch"]("gla_in", h1)
    proj = _matmul(h1, w_t, tb=True, b_rows=main, name="gla_in")
    a_tail = _matmul(h1, w_tail_t, tb=True, name="gla_in_tail")
    dk_total = p["gla_wg_p"].shape[1]
    o, states = _gla_fwd(proj, a_tail, p["gla_wg_p"], p["gla_b_gate"], "gla_chunks")
    assert 2 * dk_total == o.shape[1]
    r = ("cols", proj, 2, o.shape[1])
    og = _gla_post_fwd(o, r, p["gla_norm"], "gla_post")
    x1, y = _matmul(og, p["fetch"]("gla_out", og), name="gla_out", out_dtypes=(F32, F32),
                    epilogue=lambda acc, x, g: (x + (1.0 + g) * acc, acc), extras=(("mn", x), ("n", p["g_m"][i])))
    return x1, dict(h1=h1, proj=proj, a_tail=a_tail, o=o, r=r, states=states, og=og, y=y)


def _gla_layer_bwd(dx1, x, sv, p, i):
    d = x.shape[1]
    (w_t, w_tail_t, main), w_out = p["fetch"]("gla_in", None), p["fetch"]("gla_out", None)
    dy, dg_m = _residual_bwd(dx1, sv["y"], p["g_m"][i], "gla_res_bwd")
    dog = _matmul(dy, w_out, tb=True, name="gla_out_dx")
    dw_out = _matmul(sv["og"], dy, ta=True, name="gla_out_dw", out_dtypes=(BF16,))
    tok = p["flush"](dog) + p["send"]("gla_out", [dw_out.reshape(N_DEV, -1, d)])
    d_o, d_r, dgn = _gla_post_bwd(sv["o"], sv["r"], p["gla_norm"] + tok, dog, "gla_post_bwd")
    dq, dk, dv, dga = _gla_bwd(sv["proj"], sv["a_tail"], p["gla_wg_p"], p["gla_b_gate"], sv["states"], d_o,
                               "gla_chunks_bwd")
    tok = p["flush"](dga)
    da_tail = _matmul(dga, p["gla_wg_p"], tb=True, name="gla_gate_dx", out_dtypes=(BF16,))
    dwg = _matmul(sv["a_tail"], dga, ta=True, name="gla_gate_dw")
    dbg = _rowwise(lambda t: (_colsum(t),), [("row", dga)], [("acc", dga.shape[1], F32)], name="gla_gate_db")[0]
    dproj = jnp.concatenate([dq, dk, dv, d_r], axis=1)
    dh_tail = _matmul(da_tail, w_tail_t, name="gla_in_tail_dx")
    dh1 = _matmul(dproj, w_t, b_rows=main, name="gla_in_dx", tk=1024,
                  epilogue=lambda acc, t: (acc + t,), extras=(("mn", dh_tail),))
    dw_main = _matmul(dproj, sv["h1"], ta=True, name="gla_in_dw", out_dtypes=(BF16,))
    dw_tail = _matmul(da_tail, sv["h1"], ta=True, name="gla_in_tail_dw", out_dtypes=(BF16,))
    rank = p["gla_rank"]
    dx, dsh, dsc, dgain = _adaln_bwd(x, dh1, dx1, p["norm_mix"][i] + tok, p["sc_m"][i], "gla_norm_bwd")
    grads = dict(gla_w_gate=dwg[:rank], gla_b_gate=dbg, gla_norm=dgn, norm_mix=dgain, sh_m=dsh, sc_m=dsc, g_m=dg_m,
                 gla_w_in_unsent=(dw_main, dw_tail[:rank]))
    return dx, grads


def _fox_layer_fwd(x, p, i):
    d = x.shape[1]
    hd = p["fox_q_norm"].shape[1]
    heads = d // hd
    s = x.shape[0]
    t = _tile(s, 512)
    h1 = _adaln_fwd(x, p["norm_mix"][i], p["sc_m"][i], p["sh_m"][i], "fox_norm")
    w_t, w_tail_t, main = p["fetch"]("fox_in", h1)
    proj = _matmul(h1, w_t, tb=True, b_rows=main, name="fox_in")
    fl = _matmul(h1, w_tail_t, tb=True, name="fox_in_tail")
    q, k, v, og = (("cols", proj, j, d) for j in range(4))
    qn, kn, vb = _fox_prep(q, k, v, p["fox_q_norm"], p["fox_k_norm"], d, hd, "fox_prep")
    cum = _fox_cum(fl, p["fox_bf_p"], "fox_cum")
    cum_t = jnp.transpose(cum[:, :heads])
    cum_col, cum_row = cum_t[:, :, None], cum_t.reshape(heads, s // t, 1, t)
    o, lse = _fox_attn_fwd(qn, kn, vb, cum_col, cum_row, hd, t, "fox_attn")
    act = _fox_gate_fwd(o, og, "fox_gate")
    x1, y = _matmul(act, p["fetch"]("fox_out", act), name="fox_out", out_dtypes=(F32, F32),
                    epilogue=lambda acc, x, g: (x + (1.0 + g) * acc, acc), extras=(("mn", x), ("n", p["g_m"][i])))
    return x1, dict(h1=h1, q=q, k=k, og=og, fl=fl, qn=qn, kn=kn, vb=vb, cum_col=cum_col, cum_row=cum_row,
                    o=o, lse=lse, act=act, y=y, t=t, hd=hd)


def _fox_layer_bwd(dx1, x, sv, p, i):
    d = x.shape[1]
    hd, t = sv["hd"], sv["t"]
    heads = d // hd
    s = x.shape[0]
    (w_t, w_tail_t, main), w_out = p["fetch"]("fox_in", None), p["fetch"]("fox_out", None)
    dy, dg_m = _residual_bwd(dx1, sv["y"], p["g_m"][i], "fox_res_bwd")
    dact = _matmul(dy, w_out, tb=True, name="fox_out_dx")
    dw_out = _matmul(sv["act"], dy, ta=True, name="fox_out_dw", out_dtypes=(BF16,))
    d_o, d_og = _fox_gate_bwd(sv["o"], sv["og"], dact, "fox_gate_bwd")
    tok_flush = p["flush"](d_og)
    dqn, dkn, dvb, dcq, dck = _fox_attn_bwd(sv["qn"], sv["kn"], sv["vb"], d_o, sv["o"], sv["lse"], sv["cum_col"],
                                            sv["cum_row"], hd, t, "fox_attn_bwd")
    dq, dk, gq, gk = _fox_prep_bwd(sv["q"], sv["k"], dqn, dkn, p["fox_q_norm"], p["fox_k_norm"], hd, "fox_prep_bwd")
    dcum = _pad_cols(jnp.transpose(dcq[:, :, 0] - dck.reshape(heads, s)))
    dfl, dbf = _fox_cum_bwd(dcum, sv["fl"], p["fox_bf_p"], "fox_cum_bwd")
    dfl_b = dfl.astype(BF16)
    dproj = jnp.concatenate([dq, dk, dvb, d_og], axis=1)
    dh_tail = _matmul(dfl_b, w_tail_t, name="fox_in_tail_dx")
    dh1 = _matmul(dproj, w_t, b_rows=main, name="fox_in_dx", tk=1024,
                  epilogue=lambda acc, tl: (acc + tl,), extras=(("mn", dh_tail),))
    dw_main = _matmul(dproj, sv["h1"], ta=True, name="fox_in_dw", out_dtypes=(BF16,))
    dw_tail = _matmul(dfl_b, sv["h1"], ta=True, name="fox_in_tail_dw", out_dtypes=(BF16,))
    dw_in = jnp.concatenate([dw_main, dw_tail[:heads]], axis=0).reshape(N_DEV, -1, d)
    tok = tok_flush + p["send"]("fox", [dw_in, dw_out.reshape(N_DEV, -1, d)])
    dx, dsh, dsc, dgain = _adaln_bwd(x, dh1, dx1, p["norm_mix"][i] + tok, p["sc_m"][i], "fox_norm_bwd")
    grads = dict(fox_b_f=dbf[:, :heads], fox_q_norm=gq.reshape(heads, hd).sum(0, keepdims=True),
                 fox_k_norm=gk.reshape(heads, hd).sum(0, keepdims=True), norm_mix=dgain, sh_m=dsh, sc_m=dsc, g_m=dg_m)
    return dx, grads


SMALL = ("b_mod", "norm_mix", "norm_ffn", "gla_b_gate", "gla_norm", "fox_b_f", "fox_q_norm", "fox_k_norm",
         "ffn_conv_b", "norm_final")
SMALL_SHARDED = ("gla_w_gate", "ffn_conv_w")
BIG = ("gla_w_in", "gla_w_out", "fox_w_in", "fox_w_out", "ffn_w_up", "ffn_w_down")
WEIGHTS = ("w_mod", "b_mod", "norm_mix", "norm_ffn", "gla_w_in", "gla_w_gate", "gla_b_gate", "gla_norm", "gla_w_out",
           "fox_w_in", "fox_b_f", "fox_q_norm", "fox_k_norm", "fox_w_out", "ffn_w_up", "ffn_conv_w", "ffn_conv_b",
           "ffn_w_down", "norm_final")


def _pack(parts):
    flat = jnp.concatenate([p.reshape(-1) for p in parts])
    pad = (-flat.shape[0]) % 1024
    return jnp.pad(flat, (0, pad)).reshape(1, -1)


def _unpack(flat, shapes):
    out, off = [], 0
    for shp in shapes:
        n = 1
        for s in shp:
            n *= s
        out.append(flat[0, off:off + n].reshape(shp))
        off += n
    return out


def kernel(x, c, w_mod, b_mod, norm_mix, norm_ffn, gla_w_in, gla_w_gate, gla_b_gate, gla_norm, gla_w_out, fox_w_in, fox_b_f, fox_q_norm, fox_k_norm, fox_w_out, ffn_w_up, ffn_conv_w, ffn_conv_b, ffn_w_down, norm_final, loss_target, m_w_mod, m_b_mod, m_norm_mix, m_norm_ffn, m_gla_w_in, m_gla_w_gate, m_gla_b_gate, m_gla_norm, m_gla_w_out, m_fox_w_in, m_fox_b_f, m_fox_q_norm, m_fox_k_norm, m_fox_w_out, m_ffn_w_up, m_ffn_conv_w, m_ffn_conv_b, m_ffn_w_down, m_norm_final, v_w_mod, v_b_mod, v_norm_mix, v_norm_ffn, v_gla_w_in, v_gla_w_gate, v_gla_b_gate, v_gla_norm, v_gla_w_out, v_fox_w_in, v_fox_b_f, v_fox_q_norm, v_fox_k_norm, v_fox_w_out, v_ffn_w_up, v_ffn_conv_w, v_ffn_conv_b, v_ffn_w_down, v_norm_final):
    w = dict(w_mod=w_mod, b_mod=b_mod, norm_mix=norm_mix, norm_ffn=norm_ffn, gla_w_in=gla_w_in, gla_w_gate=gla_w_gate,
             gla_b_gate=gla_b_gate, gla_norm=gla_norm, gla_w_out=gla_w_out, fox_w_in=fox_w_in, fox_b_f=fox_b_f,
             fox_q_norm=fox_q_norm, fox_k_norm=fox_k_norm, fox_w_out=fox_w_out, ffn_w_up=ffn_w_up,
             ffn_conv_w=ffn_conv_w, ffn_conv_b=ffn_conv_b, ffn_w_down=ffn_w_down, norm_final=norm_final)
    mom_m = dict(w_mod=m_w_mod, b_mod=m_b_mod, norm_mix=m_norm_mix, norm_ffn=m_norm_ffn, gla_w_in=m_gla_w_in,
                 gla_w_gate=m_gla_w_gate, gla_b_gate=m_gla_b_gate, gla_norm=m_gla_norm, gla_w_out=m_gla_w_out,
                 fox_w_in=m_fox_w_in, fox_b_f=m_fox_b_f, fox_q_norm=m_fox_q_norm, fox_k_norm=m_fox_k_norm,
                 fox_w_out=m_fox_w_out, ffn_w_up=m_ffn_w_up, ffn_conv_w=m_ffn_conv_w, ffn_conv_b=m_ffn_conv_b,
                 ffn_w_down=m_ffn_w_down, norm_final=m_norm_final)
    mom_v = dict(w_mod=v_w_mod, b_mod=v_b_mod, norm_mix=v_norm_mix, norm_ffn=v_norm_ffn, gla_w_in=v_gla_w_in,
                 gla_w_gate=v_gla_w_gate, gla_b_gate=v_gla_b_gate, gla_norm=v_gla_norm, gla_w_out=v_gla_w_out,
                 fox_w_in=v_fox_w_in, fox_b_f=v_fox_b_f, fox_q_norm=v_fox_q_norm, fox_k_norm=v_fox_k_norm,
                 fox_w_out=v_fox_w_out, ffn_w_up=v_ffn_w_up, ffn_conv_w=v_ffn_conv_w, ffn_conv_b=v_ffn_conv_b,
                 ffn_w_down=v_ffn_w_down, norm_final=v_norm_final)

    me = 4 * lax.axis_index("x") + 2 * lax.axis_index("y") + lax.axis_index("c")
    xs, target = x[0], loss_target[0]
    s, d = xs.shape
    depth = w_mod.shape[0]
    mod_cols = w_mod.shape[2]
    rank = gla_w_gate.shape[1]
    hd = fox_q_norm.shape[1]
    fox_heads = d // hd
    dk_total = gla_w_gate.shape[2] * N_DEV

    cond = c * (1.0 / (1.0 + jnp.exp(-c)))
    g, _ = _exchange([gla_w_gate[0], ffn_conv_w, cond], "gather_small", scatter=False)
    cond_all = g[2][:, 0, :]

    cond_pad = jnp.pad(cond_all, ((0, 16 - N_DEV), (0, 0)))
    mod_part = []
    for i in range(depth):
        b_cols = lax.dynamic_slice(b_mod[i:i + 1], (0, me * mod_cols), (1, mod_cols))
        mod_part.append(_matmul(cond_pad, w_mod[i], name=f"mod_{i}", tn=768,
                                epilogue=lambda acc, b: (acc + b,), extras=(("n", b_cols),))[:N_DEV])
    (mod_all,), tok_mod = _exchange([jnp.stack(mod_part)], "gather_mod", scatter=False)
    mod = lax.dynamic_index_in_dim(mod_all, me, axis=2, keepdims=False)
    mod = jnp.transpose(mod, (1, 0, 2)).reshape(depth, 6, 1, d)

    big_names = ["gla_in", "gla_out", "up0", "down0", "fox_in", "fox_out", "up1", "down1"]
    big_shards = [jnp.transpose(gla_w_in[0] + tok_mod), gla_w_out[0], ffn_w_up[0], ffn_w_down[0],
                  jnp.transpose(fox_w_in[0]), fox_w_out[0], ffn_w_up[1], ffn_w_down[1]]
    big_shards = [t.astype(BF16) for t in big_shards]
    handles, tok0 = _gather2_start(big_shards, "gather_weights_start")
    ready, forwarded = {}, {}

    def split_tail(full_t, tail):
        main = full_t.shape[0] - tail
        return full_t, jnp.pad(full_t[main:], ((0, LANE - tail), (0, 0))), main

    def forward(idx, after):
        key = big_names[idx]
        forwarded[key] = _gather2_forward(handles[idx], after, f"gather_{key}_forward")

    def fetch(key, after):
        if key not in ready:
            idx = big_names.index(key)
            if idx == 0:
                forward(0, after)
            handle, _ = forwarded[key]
            mine, land = _gather2_wait(handle, after, f"gather_{key}_wait")
            if idx + 1 < len(big_names):
                forward(idx + 1, land)
                mine = mine + forwarded[big_names[idx + 1]][1].astype(BF16)
            full = lax.dynamic_update_slice(land, mine[None], (me,) + (0,) * mine.ndim)
            if key == "gla_in":
                ready[key] = split_tail(full.reshape(-1, d), rank)
            elif key == "fox_in":
                ready[key] = split_tail(full.reshape(-1, d), fox_heads)
            elif key.startswith("up"):
                ready[key] = full
            else:
                ready[key] = full.reshape(-1, d)
        return ready[key]

    pending, sent = [], {}
    core = lax.axis_index("c").astype(jnp.int32).reshape(1)
    chip = 2 * lax.axis_index("x") + lax.axis_index("y")

    def send(key, pieces):
        hs, tok = _split_start(pieces, _pair_plan, f"scatter_{key}_pair_start")
        pending.append((key, hs))
        return tok

    def flush(after):
        tok = 0.0
        while pending:
            key, hs = pending.pop(0)
            mine, partner = _split_wait(hs, _pair_plan, after, f"scatter_{key}_pair_wait")
            sums = [_pair_sum(pc, pt, core, f"scatter_{key}_pair_sum{a}")
                    for a, (pc, pt) in enumerate(zip(mine, partner))]
            sent[key], t = _split_start(sums, _chip_plan, f"scatter_{key}_chip_start")
            tok = tok + t
        return tok

    p = dict(
        fetch=fetch, send=send, flush=flush,
        gla_wg_p=jnp.pad(_cols_full(g[0]), ((0, LANE - rank), (0, 0))),
        conv_w=[jnp.transpose(g[1][:, i], (1, 0, 2)).reshape(ffn_conv_w.shape[1], -1) for i in range(depth)],
        conv_b=[ffn_conv_b[i:i + 1] for i in range(depth)],
        gla_b_gate=gla_b_gate, gla_norm=gla_norm, fox_q_norm=fox_q_norm, fox_k_norm=fox_k_norm,
        fox_bf_p=_pad_cols(fox_b_f), gla_rank=rank,
        norm_mix=[norm_mix[i:i + 1] + (tok0 if i == 0 else 0.0) for i in range(depth)],
        norm_ffn=[norm_ffn[i:i + 1] for i in range(depth)],
    )

    for j, nm in enumerate(("sh_m", "sc_m", "g_m", "sh_f", "sc_f", "g_f")):
        p[nm] = [mod[i, j] for i in range(depth)]

    acts, saved = [xs], []
    for i in range(depth):
        layer_fwd = _gla_layer_fwd if i % 2 == 0 else _fox_layer_fwd
        x1, sv_mix = layer_fwd(acts[-1], p, i)
        x2, sv_ffn = _ffn_fwd(x1, p, i, str(i))
        saved.append((acts[-1], x1, sv_mix, sv_ffn))
        acts.append(x2)
    dx, d_norm_final, loss_part = _final_loss(acts[-1], target, norm_final.reshape(1, d), "final_loss")

    lg = [None] * depth
    for i in reversed(range(depth)):
        x_in, x1, sv_mix, sv_ffn = saved[i]
        dx, g_ffn = _ffn_bwd(dx, x1, sv_ffn, p, i, str(i))
        layer_bwd = _gla_layer_bwd if i % 2 == 0 else _fox_layer_bwd
        dx, g_mix = layer_bwd(dx, x_in, sv_mix, p, i)
        lg[i] = {**g_ffn, **g_mix}
    grad_x = dx[None]

    gla_l = [i for i in range(depth) if i % 2 == 0]
    fox_l = [i for i in range(depth) if i % 2 == 1]
    small_parts = dict(
        norm_mix=jnp.concatenate([lg[i]["norm_mix"] for i in range(depth)]),
        norm_ffn=jnp.concatenate([lg[i]["norm_ffn"] for i in range(depth)]),
        gla_b_gate=jnp.concatenate([lg[i]["gla_b_gate"] for i in gla_l]),
        gla_norm=jnp.concatenate([lg[i]["gla_norm"] for i in gla_l]),
        fox_b_f=jnp.concatenate([lg[i]["fox_b_f"] for i in fox_l]),
        fox_q_norm=jnp.concatenate([lg[i]["fox_q_norm"] for i in fox_l]),
        fox_k_norm=jnp.concatenate([lg[i]["fox_k_norm"] for i in fox_l]),
        ffn_conv_b=jnp.concatenate([lg[i]["conv_b"] for i in range(depth)]),
        norm_final=d_norm_final,
        gla_w_gate=jnp.stack([lg[i]["gla_w_gate"] for i in gla_l]),
        ffn_conv_w=jnp.stack([lg[i]["conv_w"] for i in range(depth)]),
        loss=loss_part[:, :1],
    )
    order = ("norm_mix", "norm_ffn", "gla_b_gate", "gla_norm", "fox_b_f", "fox_q_norm", "fox_k_norm", "ffn_conv_b",
             "norm_final", "gla_w_gate", "ffn_conv_w", "loss")
    packed = _pack([small_parts[nm] for nm in order])
    dmod = jnp.stack([jnp.concatenate([lg[i][nm] for nm in ("sh_m", "sc_m", "g_m", "sh_f", "sc_f", "g_f")], axis=1)
                      for i in range(depth)])
    (packed_all, dmod_all), tok_small = _exchange([packed, dmod], "gather_small_grads", scatter=False)
    dw_main, dw_tail = lg[0]["gla_w_in_unsent"]
    dw_in_t = jnp.concatenate([dw_main, dw_tail + tok_small.astype(BF16)], axis=0)
    tok_last = send("gla_in", [dw_in_t.reshape(N_DEV, -1, d)])
    packed_all = packed_all + tok_last
    summed = _unpack(_sum8(packed_all, "sum_small_grads"), [small_parts[nm].shape for nm in order])
    small_g = dict(zip(order, summed))
    loss = small_g["loss"][0, 0]
    dmod_all = dmod_all[:, :, 0, :]

    grads = {}
    cond_t = _pad_cols(jnp.transpose(cond_all)).astype(BF16)
    dmod_cols = lax.dynamic_slice(dmod_all, (0, 0, me * mod_cols), (N_DEV, depth, mod_cols))
    g_w_mod = []
    for i in range(depth):
        rhs = jnp.pad(dmod_cols[:, i], ((0, LANE - N_DEV), (0, 0)))
        g_w_mod.append(_matmul(cond_t, rhs, name=f"mod_dw_{i}", tn=768))
    grads["w_mod"] = jnp.stack(g_w_mod)
    small_g["b_mod"] = _sum8(dmod_all.reshape(N_DEV, 1, -1), "sum_b_mod").reshape(depth, -1)

    received = {}

    def arrive(key, after):
        sums, lands = _split_wait(sent[key], _chip_plan, after, f"scatter_{key}_chip_wait")
        received[key] = [lax.dynamic_update_slice(land, lax.dynamic_index_in_dim(q, chip, 0, keepdims=True),
                                                  (chip,) + (0,) * (q.ndim - 1))
                         for land, q in zip(lands, sums)]

    for key in ("ffn1", "fox", "ffn0", "gla_out"):
        arrive(key, packed_all)

    out_g, out_d, out_m, out_v = {}, {}, {}, {}

    def update(nm, g_arr, transposed=False):
        swap = (lambda t: jnp.transpose(t, (0, 2, 1))) if transposed else (lambda t: t)
        res = _adamw(swap(w[nm]), g_arr, swap(mom_m[nm]), swap(mom_v[nm]), f"adamw_{nm}")
        out_g[nm], out_d[nm], out_m[nm], out_v[nm] = (swap(t) for t in res)

    update("ffn_w_up", [received[f"ffn{i}"][0] for i in range(depth)])
    tok_flush = flush(out_g["ffn_w_up"])
    update("gla_w_out", [received["gla_out"][0]])
    update("fox_w_in", [received["fox"][0]], transposed=True)
    update("fox_w_out", [received["fox"][1]])
    update("ffn_w_down", [received[f"ffn{i}"][1] for i in range(depth)])
    update("w_mod", grads["w_mod"])

    gate_cols = gla_w_gate.shape[2]
    conv_cols = ffn_conv_w.shape[2]
    local_small = dict(small_g)
    local_small["gla_w_gate"] = lax.dynamic_slice_in_dim(small_g["gla_w_gate"], me * gate_cols, gate_cols, axis=2)
    local_small["ffn_conv_w"] = lax.dynamic_slice_in_dim(small_g["ffn_conv_w"], me * conv_cols, conv_cols, axis=2)
    names = SMALL + SMALL_SHARDED
    shapes = [w[nm].shape for nm in names]
    res = _adamw(_pack([w[nm] for nm in names])[None], (_pack([local_small[nm] for nm in names]) + tok_flush)[None],
                 _pack([mom_m[nm] for nm in names])[None], _pack([mom_v[nm] for nm in names])[None], "adamw_small")
    for tgt, flat in zip((out_g, out_d, out_m, out_v), res):
        for nm, arr in zip(names, _unpack(flat[0], shapes)):
            tgt[nm] = arr

    arrive("gla_in", [out_d[nm] for nm in ("gla_w_out", "fox_w_in", "fox_w_out", "ffn_w_up", "ffn_w_down", "w_mod")])
    update("gla_w_in", [received["gla_in"][0]], transposed=True)

    return (loss, grad_x, *[out_g[n] for n in WEIGHTS], *[out_d[n] for n in WEIGHTS],
            *[out_m[n] for n in WEIGHTS], *[out_v[n] for n in WEIGHTS])
```

```python
import jax
import jax.numpy as jnp
from jax import lax
from jax.experimental import pallas as pl
from jax.experimental.pallas import tpu as pltpu

F32, BF16 = jnp.float32, jnp.bfloat16
N_DEV = 8
GLA_HEADS = 4
GLA_TAU = 16.0
GLA_CHUNK = 64
NORM_EPS = 1e-6
ADAM_LR, ADAM_B1, ADAM_B2, ADAM_EPS, ADAM_WD, ADAM_STEP = 0.001, 0.9, 0.999, 1e-08, 0.01, 10
LANE = 128
VMEM_LIMIT = 56 * 1024 * 1024
NEG = -1e30


def _pcall(body, **kw):
    return pl.pallas_call(body, **kw)


def _params(n_axes):
    return pltpu.CompilerParams(dimension_semantics=("arbitrary",) * n_axes, vmem_limit_bytes=VMEM_LIMIT)


def _tile(dim, pref):
    if dim <= pref:
        return dim
    t = pref
    while dim % t:
        t -= LANE
    assert t > 0, (dim, pref)
    return t


def _dot(a, b, ta=False, tb=False):
    dims = (((0,) if ta else (1,), (1,) if tb else (0,)), ((), ()))
    return lax.dot_general(a.astype(BF16), b.astype(BF16), dims, preferred_element_type=F32)


def _split3(x):
    hi = x.astype(BF16)
    r1 = x - hi.astype(F32)
    mid = r1.astype(BF16)
    lo = (r1 - mid.astype(F32)).astype(BF16)
    return hi, mid, lo


def _tri_matmul(tri, x):
    hi, mid, lo = _split3(x)
    return _dot(tri, hi) + _dot(tri, mid) + _dot(tri, lo)


def _tri(n, upper=False):
    r = lax.broadcasted_iota(jnp.int32, (n, n), 0)
    c = lax.broadcasted_iota(jnp.int32, (n, n), 1)
    return jnp.where((r <= c) if upper else (r >= c), 1.0, 0.0).astype(BF16)


def _log_sigmoid(x):
    return jnp.minimum(x, 0.0) - jnp.log(1.0 + jnp.exp(-jnp.abs(x)))


def _sigmoid(x):
    return 1.0 / (1.0 + jnp.exp(-x))


def _silu(x):
    return x * _sigmoid(x)


def _dsilu(x):
    s = _sigmoid(x)
    return s * (1.0 + x * (1.0 - s))


def _matmul(a, b, *, name, ta=False, tb=False, out_dtypes=(F32,), tm=1024, tn=1024, tk=2048,
            epilogue=None, extras=(), a_halves=False, b_halves=False, b_shards=False, out_shards=False,
            b_rows=None, out_rows=None):
    if a_halves:
        assert not ta
        m, k = a.shape[1], 2 * a.shape[2]
    else:
        m, k = (a.shape[1], a.shape[0]) if ta else a.shape
    if b_halves:
        assert not tb and b.shape[1] == k
        n = 2 * b.shape[2]
    elif b_shards:
        n = b.shape[1] if tb else N_DEV * b.shape[2]
        assert (N_DEV * b.shape[2] if tb else b.shape[1]) == k, (a.shape, b.shape, ta, tb)
    else:
        rows = b.shape[0] if b_rows is None else b_rows
        n = rows if tb else b.shape[1]
        assert (b.shape[1] if tb else rows) == k, (a.shape, b.shape, ta, tb)
    n_unit = n // N_DEV if (out_shards or (b_shards and not tb)) else (n // 2 if b_halves else n)
    k_unit = k // N_DEV if (b_shards and tb) else (k // 2 if a_halves else k)
    tm, tn, tk = _tile(m, tm), _tile(n_unit, tn), _tile(k_unit, tk)
    nk = k // tk
    if a_halves:
        a_spec = pl.BlockSpec((None, tm, tk), lambda i, j, kk: (kk // (nk // 2), i, kk % (nk // 2)))
    elif ta:
        a_spec = pl.BlockSpec((tk, tm), lambda i, j, kk: (kk, i))
    else:
        a_spec = pl.BlockSpec((tm, tk), lambda i, j, kk: (i, kk))
    n_per, k_per = n // tn // N_DEV, nk // N_DEV
    if b_halves:
        b_spec = pl.BlockSpec((None, tk, tn), lambda i, j, kk: (j // (n // tn // 2), kk, j % (n // tn // 2)))
    elif b_shards and tb:
        b_spec = pl.BlockSpec((None, tn, tk), lambda i, j, kk: (kk // k_per, j, kk % k_per))
    elif b_shards:
        b_spec = pl.BlockSpec((None, tk, tn), lambda i, j, kk: (j // n_per, kk, j % n_per))
    elif tb:
        b_spec = pl.BlockSpec((tn, tk), lambda i, j, kk: (j, kk))
    else:
        b_spec = pl.BlockSpec((tk, tn), lambda i, j, kk: (kk, j))
    ex_specs = []
    for kind, arr in extras:
        if kind == "mn":
            assert arr.shape == (m, n), (arr.shape, m, n)
            ex_specs.append(pl.BlockSpec((tm, tn), lambda i, j, kk: (i, j)))
        else:
            assert arr.shape == (1, n), (arr.shape, n)
            ex_specs.append(pl.BlockSpec((1, tn), lambda i, j, kk: (0, j)))
    n_ex, n_out = len(extras), len(out_dtypes)

    def body(a_ref, b_ref, *rest):
        ex, outs, acc = rest[:n_ex], rest[n_ex:n_ex + n_out], rest[-1]
        kk = pl.program_id(2)

        @pl.when(kk == 0)
        def _():
            acc[...] = jnp.zeros_like(acc)

        acc[...] += _dot(a_ref[...], b_ref[...], ta, tb)

        @pl.when(kk == nk - 1)
        def _():
            if epilogue is None:
                vals = (acc[...],)
            else:
                vals = epilogue(acc[...], *[e[...] for e in ex])
            for o, v in zip(outs, vals):
                o[...] = v.astype(o.dtype)

    if out_shards:
        out_spec = pl.BlockSpec((None, tm, tn), lambda i, j, kk: (j // n_per, i, j % n_per))
        out_dims = (N_DEV, m, n // N_DEV)
    else:
        out_spec = pl.BlockSpec((tm, tn), lambda i, j, kk: (i, j))
        out_dims = (m if out_rows is None else out_rows, n)
    res = _pcall(
        body, name=name, grid=(m // tm, n // tn, nk),
        in_specs=[a_spec, b_spec] + ex_specs,
        out_specs=[out_spec] * n_out,
        out_shape=[jax.ShapeDtypeStruct(out_dims, d) for d in out_dtypes],
        scratch_shapes=[pltpu.VMEM((tm, tn), F32)],
        compiler_params=_params(3),
    )(a, b, *[arr for _, arr in extras])
    return res[0] if n_out == 1 else res


def _tail_rows(a, b, into, rows, name, tn=1024):
    k, n = b.shape
    m_total = into.shape[0]
    tn = _tile(n, tn)

    def body(a_ref, b_ref, into_ref, out_ref):
        out_ref[...] = _dot(a_ref[...], b_ref[...], ta=True)[:rows].astype(out_ref.dtype)

    return _pcall(
        body, name=name, grid=(n // tn,),
        in_specs=[pl.BlockSpec((k, a.shape[1]), lambda j: (0, 0)), pl.BlockSpec((k, tn), lambda j: (0, j)),
                  pl.BlockSpec(memory_space=pl.ANY)],
        out_specs=pl.BlockSpec((rows, tn), lambda j: (m_total // rows - 1, j)),
        out_shape=jax.ShapeDtypeStruct(into.shape, into.dtype),
        input_output_aliases={2: 0}, compiler_params=_params(1),
    )(a, b, into)


def _rowwise(fn, ins, outs, *, name, tr=128):
    rows = next(e[1].shape[0] for e in ins if e[0] != "full")
    tr = _tile(rows, tr)
    in_specs = []
    for entry in ins:
        kind, arr = entry[0], entry[1]
        assert kind == "full" or (arr.shape[0] == rows and arr.ndim == 2)
        if kind == "row":
            in_specs.append(pl.BlockSpec((tr, arr.shape[1]), lambda i: (i, 0)))
        elif kind == "cols":
            in_specs.append(pl.BlockSpec((tr, entry[3]), lambda i, cb=entry[2]: (i, cb)))
        else:
            in_specs.append(pl.BlockSpec(arr.shape, lambda i, nd=arr.ndim: (0,) * nd))
    out_specs, out_shape = [], []
    for kind, w, dt in outs:
        if kind == "row":
            out_specs.append(pl.BlockSpec((tr, w), lambda i: (i, 0)))
            out_shape.append(jax.ShapeDtypeStruct((rows, w), dt))
        else:
            out_specs.append(pl.BlockSpec((1, w), lambda i: (0, 0)))
            out_shape.append(jax.ShapeDtypeStruct((1, w), dt))
    n_in = len(ins)

    def body(*refs):
        i = pl.program_id(0)
        vals = fn(*[r[...] for r in refs[:n_in]])
        for (kind, _, _), o, v in zip(outs, refs[n_in:], vals):
            if kind == "row":
                o[...] = v.astype(o.dtype)
            else:
                @pl.when(i == 0)
                def _(o=o):
                    o[...] = jnp.zeros_like(o)

                o[...] += v.astype(o.dtype)

    return _pcall(body, name=name, grid=(rows // tr,), in_specs=in_specs, out_specs=out_specs,
                  out_shape=out_shape, compiler_params=_params(1))(*[e[1] for e in ins])


def _colsum(x):
    return jnp.sum(x, axis=0, keepdims=True)


def _norm_stats(x):
    rstd = lax.rsqrt(jnp.mean(x * x, axis=-1, keepdims=True) + NORM_EPS)
    return x * rstd, rstd


def _norm_bwd(dxhat, xhat, rstd):
    return rstd * (dxhat - xhat * jnp.mean(dxhat * xhat, axis=-1, keepdims=True))


def _adaln_fwd(x, gain, sc, sh, name):
    def fn(x, gain, sc, sh):
        xhat, _ = _norm_stats(x)
        return ((xhat * gain) * (1.0 + sc) + sh,)

    return _rowwise(fn, [("row", x), ("full", gain), ("full", sc), ("full", sh)],
                    [("row", x.shape[1], BF16)], name=name)[0]


def _adaln_bwd(x, dh, dres, gain, sc, name):
    d = x.shape[1]

    def fn(x, dh, dres, gain, sc):
        xhat, rstd = _norm_stats(x)
        dxhat = dh * (gain * (1.0 + sc))
        dx = dres + _norm_bwd(dxhat, xhat, rstd)
        return dx, _colsum(dh), _colsum(dh * (xhat * gain)), _colsum(dh * xhat * (1.0 + sc))

    return _rowwise(fn, [("row", x), ("row", dh), ("row", dres), ("full", gain), ("full", sc)],
                    [("row", d, F32), ("acc", d, F32), ("acc", d, F32), ("acc", d, F32)], name=name)


def _residual_bwd(dx, y, g, name):
    d = dx.shape[1]

    def fn(dx, y, g):
        return dx * (1.0 + g), _colsum(dx * y)

    return _rowwise(fn, [("row", dx), ("row", y), ("full", g)], [("row", d, BF16), ("acc", d, F32)], name=name)


def _final_loss(x, target, gain, name):
    d = x.shape[1]

    def fn(x, t, gain):
        xhat, rstd = _norm_stats(x)
        err = xhat * gain - t
        dy = err * (1.0 / d)
        loss = 0.5 * jnp.sum(jnp.mean(err * err, axis=-1, keepdims=True), axis=0, keepdims=True)
        dx = _norm_bwd(dy * gain, xhat, rstd)
        return dx, _colsum(dy * xhat), jnp.broadcast_to(loss, (1, LANE))

    return _rowwise(fn, [("row", x), ("row", target), ("full", gain)],
                    [("row", d, F32), ("acc", d, F32), ("acc", LANE, F32)], name=name)


def _gla_gates(q, k, a, wg, bg, scale, c):
    ga = _dot(a, wg) + bg
    la = _log_sigmoid(ga) * (1.0 / GLA_TAU)
    b = _tri_matmul(_tri(c), la)
    bl = _colsum(la)
    eb, enb, eend = jnp.exp(b), jnp.exp(-b), jnp.exp(bl - b)
    q = q * scale
    return dict(ga=ga, eb=eb, enb=enb, eend=eend, dec=jnp.exp(bl), q_dec=q * eb, k_inv=k * enb, k_end=k * eend)


def _causal(c):
    return lax.broadcasted_iota(jnp.int32, (c, c), 0) >= lax.broadcasted_iota(jnp.int32, (c, c), 1)


def _gla_specs(heads, c, dk, dv, chunk):
    return [
        pl.BlockSpec((c, heads * dk), lambda n: (chunk(n), 0)),
        pl.BlockSpec((c, heads * dk), lambda n: (chunk(n), 1)),
        pl.BlockSpec((c, heads * dv), lambda n: (chunk(n), 1)),
        pl.BlockSpec((c, LANE), lambda n: (chunk(n), 0)),
        pl.BlockSpec((LANE, heads * dk), lambda n: (0, 0)),
        pl.BlockSpec((1, heads * dk), lambda n: (0, 0)),
    ]


def _gla_fwd(proj, a_tail, wg_p, bg, name):
    s = proj.shape[0]
    heads, c = GLA_HEADS, GLA_CHUNK
    dk = wg_p.shape[1] // heads
    dv = 2 * dk
    n_chunks = s // c
    scale = dk ** -0.5

    def body(q_ref, k_ref, v_ref, a_ref, wg_ref, bg_ref, o_ref, st_ref, state):
        @pl.when(pl.program_id(0) == 0)
        def _():
            state[...] = jnp.zeros_like(state)

        a = a_ref[...]
        for h in range(heads):
            sk, sv = slice(h * dk, (h + 1) * dk), slice(h * dv, (h + 1) * dv)
            g = _gla_gates(q_ref[:, sk], k_ref[:, sk], a, wg_ref[:, sk], bg_ref[:, sk], scale, c)
            v = v_ref[:, sv]
            st = state[h]
            attn = jnp.where(_causal(c), _dot(g["q_dec"], g["k_inv"], tb=True), 0.0)
            o_ref[:, sv] = _dot(attn, v) + _dot(g["q_dec"], st, tb=True)
            st_ref[h] = st.astype(st_ref.dtype)
            state[h] = g["dec"] * st + _dot(v, g["k_end"], ta=True)

    return _pcall(
        body, name=name, grid=(n_chunks,),
        in_specs=_gla_specs(heads, c, dk, dv, lambda n: n),
        out_specs=[pl.BlockSpec((c, heads * dv), lambda n: (n, 0)),
                   pl.BlockSpec((heads, None, dv, dk), lambda n: (0, n, 0, 0))],
        out_shape=[jax.ShapeDtypeStruct((s, heads * dv), F32),
                   jax.ShapeDtypeStruct((heads, n_chunks, dv, dk), BF16)],
        scratch_shapes=[pltpu.VMEM((heads, dv, dk), F32)],
        compiler_params=_params(1),
    )(proj, proj, proj, a_tail, wg_p, bg)


def _gla_bwd(proj, a_tail, wg_p, bg, states, d_o, name):
    s = proj.shape[0]
    heads, c = GLA_HEADS, GLA_CHUNK
    dk = wg_p.shape[1] // heads
    dv = 2 * dk
    n_chunks = s // c
    scale = dk ** -0.5

    def body(q_ref, k_ref, v_ref, a_ref, wg_ref, bg_ref, st_ref, do_ref, dq_ref, dk_ref, dv_ref, dga_ref, dstate):
        @pl.when(pl.program_id(0) == 0)
        def _():
            dstate[...] = jnp.zeros_like(dstate)

        a = a_ref[...]
        mask = _causal(c)
        for h in range(heads):
            sk, sv = slice(h * dk, (h + 1) * dk), slice(h * dv, (h + 1) * dv)
            g = _gla_gates(q_ref[:, sk], k_ref[:, sk], a, wg_ref[:, sk], bg_ref[:, sk], scale, c)
            v, st, dst, d_out = v_ref[:, sv], st_ref[h], dstate[h], do_ref[:, sv]
            q_dec, k_inv, k_end = g["q_dec"], g["k_inv"], g["k_end"]
            attn = jnp.where(mask, _dot(q_dec, k_inv, tb=True), 0.0)
            d_attn = jnp.where(mask, _dot(d_out, v, tb=True), 0.0)
            d_qdec = _dot(d_attn, k_inv) + _dot(d_out, st)
            d_kinv = _dot(d_attn, q_dec, ta=True)
            d_kend = _dot(v, dst)
            dv_ref[:, sv] = (_dot(attn, d_out, ta=True) + _dot(k_end, dst, tb=True)).astype(dv_ref.dtype)
            d_dec = jnp.sum(dst * st.astype(F32), axis=0, keepdims=True)
            dstate[h] = g["dec"] * dst + _dot(d_out, q_dec, ta=True)

            dq_ref[:, sk] = (d_qdec * (scale * g["eb"])).astype(dq_ref.dtype)
            dk_ref[:, sk] = (d_kinv * g["enb"] + d_kend * g["eend"]).astype(dk_ref.dtype)
            kk = d_kend * k_end
            db = d_qdec * q_dec - d_kinv * k_inv - kk
            dbl = jnp.sum(kk, axis=0, keepdims=True) + d_dec * g["dec"]
            last = lax.broadcasted_iota(jnp.int32, db.shape, 0) == c - 1
            db = db + jnp.where(last, dbl, 0.0)
            dla = _tri_matmul(_tri(c, upper=True), db)
            dga_ref[:, sk] = dla * (1.0 / GLA_TAU) * _sigmoid(-g["ga"])

    chunk = lambda n: n_chunks - 1 - n
    rev = lambda n: (chunk(n), 0)
    return _pcall(
        body, name=name, grid=(n_chunks,),
        in_specs=_gla_specs(heads, c, dk, dv, chunk) + [
            pl.BlockSpec((heads, None, dv, dk), lambda n: (0, chunk(n), 0, 0)),
            pl.BlockSpec((c, heads * dv), rev)],
        out_specs=[pl.BlockSpec((c, heads * dk), rev), pl.BlockSpec((c, heads * dk), rev),
                   pl.BlockSpec((c, heads * dv), rev), pl.BlockSpec((c, heads * dk), rev)],
        out_shape=[jax.ShapeDtypeStruct((s, heads * dk), BF16), jax.ShapeDtypeStruct((s, heads * dk), BF16),
                   jax.ShapeDtypeStruct((s, heads * dv), BF16), jax.ShapeDtypeStruct((s, heads * dk), F32)],
        scratch_shapes=[pltpu.VMEM((heads, dv, dk), F32)],
        compiler_params=_params(1),
    )(proj, proj, proj, a_tail, wg_p, bg, states, d_o)


def _gla_post_fwd(o, r, gn, name):
    dvt = o.shape[1]
    dv = dvt // GLA_HEADS

    def fn(o, r, gn):
        outs = []
        for h in range(GLA_HEADS):
            sl = slice(h * dv, (h + 1) * dv)
            ohat, _ = _norm_stats(o[:, sl])
            outs.append((ohat * gn[:, sl]) * _silu(r[:, sl]))
        return (jnp.concatenate(outs, axis=1),)

    return _rowwise(fn, [("row", o), r, ("full", gn)], [("row", dvt, BF16)], name=name)[0]


def _gla_post_bwd(o, r, gn, dog, name):
    dvt = o.shape[1]
    dv = dvt // GLA_HEADS

    def fn(o, r, gn, dog):
        d_o, d_r, d_g = [], [], []
        for h in range(GLA_HEADS):
            sl = slice(h * dv, (h + 1) * dv)
            ohat, rstd = _norm_stats(o[:, sl])
            g, rr, dd = gn[:, sl], r[:, sl], dog[:, sl]
            d_r.append(dd * (ohat * g) * _dsilu(rr))
            don = dd * _silu(rr)
            d_g.append(_colsum(don * ohat))
            d_o.append(_norm_bwd(don * g, ohat, rstd))
        return jnp.concatenate(d_o, axis=1), jnp.concatenate(d_r, axis=1), jnp.concatenate(d_g, axis=1)

    return _rowwise(fn, [("row", o), r, ("full", gn), ("row", dog)],
                    [("row", dvt, F32), ("row", dvt, BF16), ("acc", dvt, F32)], name=name)


def _fox_prep(q, k, v, qg, kg, d, hd, name):
    heads = d // hd
    scale = hd ** -0.5

    def fn(q, k, v, qg, kg):
        qs, ks = [], []
        for h in range(heads):
            sl = slice(h * hd, (h + 1) * hd)
            qs.append(_norm_stats(q[:, sl])[0] * qg * scale)
            ks.append(_norm_stats(k[:, sl])[0] * kg)
        return jnp.concatenate(qs, axis=1), jnp.concatenate(ks, axis=1), v

    return _rowwise(fn, [q, k, v, ("full", qg), ("full", kg)],
                    [("row", d, BF16)] * 3, name=name)


def _fox_prep_bwd(q, k, dqn, dkn, qg, kg, hd, name):
    d = dqn.shape[1]
    heads = d // hd
    scale = hd ** -0.5

    def fn(q, k, dqn, dkn, qg, kg):
        dq, dk, gq, gk = [], [], [], []
        for h in range(heads):
            sl = slice(h * hd, (h + 1) * hd)
            for x, dxn, g, s, dl, gl in ((q, dqn, qg, scale, dq, gq), (k, dkn, kg, 1.0, dk, gk)):
                xhat, rstd = _norm_stats(x[:, sl])
                dn = dxn[:, sl] * s
                gl.append(_colsum(dn * xhat))
                dl.append(_norm_bwd(dn * g, xhat, rstd))
        cat = lambda t: jnp.concatenate(t, axis=1)
        return cat(dq), cat(dk), cat(gq), cat(gk)

    return _rowwise(fn, [q, k, ("row", dqn), ("row", dkn), ("full", qg), ("full", kg)],
                    [("row", d, BF16), ("row", d, BF16), ("acc", d, F32), ("acc", d, F32)], name=name)


def _fox_cum(fl, bf_p, name, tb=256):
    s = fl.shape[0]
    tb = _tile(s, tb)

    def body(fl_ref, bf_ref, cum_ref, carry):
        @pl.when(pl.program_id(0) == 0)
        def _():
            carry[...] = jnp.zeros_like(carry)

        lf = _log_sigmoid(fl_ref[...] + bf_ref[...])
        cum_ref[...] = _tri_matmul(_tri(tb), lf) + carry[...]
        carry[...] += _colsum(lf)

    return _pcall(
        body, name=name, grid=(s // tb,),
        in_specs=[pl.BlockSpec((tb, LANE), lambda i: (i, 0)), pl.BlockSpec((1, LANE), lambda i: (0, 0))],
        out_specs=pl.BlockSpec((tb, LANE), lambda i: (i, 0)),
        out_shape=jax.ShapeDtypeStruct((s, LANE), F32),
        scratch_shapes=[pltpu.VMEM((1, LANE), F32)],
        compiler_params=_params(1),
    )(fl, bf_p)


def _fox_cum_bwd(dcum, fl, bf_p, name, tb=256):
    s = fl.shape[0]
    tb = _tile(s, tb)
    nb = s // tb

    def body(dc_ref, fl_ref, bf_ref, dfl_ref, dbf_ref, carry):
        @pl.when(pl.program_id(0) == 0)
        def _():
            carry[...] = jnp.zeros_like(carry)
            dbf_ref[...] = jnp.zeros_like(dbf_ref)

        dc = dc_ref[...]
        dlf = _tri_matmul(_tri(tb, upper=True), dc) + carry[...]
        carry[...] += _colsum(dc)
        dfl = dlf * _sigmoid(-(fl_ref[...] + bf_ref[...]))
        dfl_ref[...] = dfl
        dbf_ref[...] += _colsum(dfl)

    rev = lambda i: (nb - 1 - i, 0)
    return _pcall(
        body, name=name, grid=(nb,),
        in_specs=[pl.BlockSpec((tb, LANE), rev), pl.BlockSpec((tb, LANE), rev), pl.BlockSpec((1, LANE), lambda i: (0, 0))],
        out_specs=[pl.BlockSpec((tb, LANE), rev), pl.BlockSpec((1, LANE), lambda i: (0, 0))],
        out_shape=[jax.ShapeDtypeStruct((s, LANE), F32), jax.ShapeDtypeStruct((1, LANE), F32)],
        scratch_shapes=[pltpu.VMEM((1, LANE), F32)],
        compiler_params=_params(1),
    )(dcum, fl, bf_p)


def _fox_attn_fwd(qn, kn, vb, cum_col, cum_row, hd, t, name):
    s, d = qn.shape
    heads = d // hd
    nq = s // t

    def body(q_ref, k_ref, v_ref, cc_ref, cr_ref, o_ref, lse_ref):
        qi = pl.program_id(1)
        q = q_ref[...]
        cq = cc_ref[...]
        qpos = qi * t + lax.broadcasted_iota(jnp.int32, (t, 1), 0)

        def step(kj, carry, diagonal=False):
            m, l, acc = carry
            off = pl.multiple_of(kj * t, t)
            ks, vs = k_ref[pl.ds(off, t), :], v_ref[pl.ds(off, t), :]
            sc = _dot(q, ks, tb=True) + cq - cr_ref[kj]
            if diagonal:
                kpos = off + lax.broadcasted_iota(jnp.int32, (1, t), 1)
                sc = jnp.where(kpos <= qpos, sc, NEG)
            m_new = jnp.maximum(m, jnp.max(sc, axis=1, keepdims=True))
            alpha = jnp.exp(m - m_new)
            p = jnp.exp(sc - m_new)
            return m_new, alpha * l + jnp.sum(p, axis=1, keepdims=True), alpha * acc + _dot(p, vs)

        init = (jnp.full((t, 1), NEG, F32), jnp.zeros((t, 1), F32), jnp.zeros((t, hd), F32))
        m, l, acc = step(qi, lax.fori_loop(0, qi, step, init), diagonal=True)
        o_ref[...] = acc / l
        lse_ref[...] = m + jnp.log(l)

    return _pcall(
        body, name=name, grid=(heads, nq),
        in_specs=[pl.BlockSpec((t, hd), lambda h, i: (i, h)),
                  pl.BlockSpec((s, hd), lambda h, i: (0, h)),
                  pl.BlockSpec((s, hd), lambda h, i: (0, h)),
                  pl.BlockSpec((None, t, 1), lambda h, i: (h, i, 0)),
                  pl.BlockSpec((None, nq, 1, t), lambda h, i: (h, 0, 0, 0))],
        out_specs=[pl.BlockSpec((t, hd), lambda h, i: (i, h)), pl.BlockSpec((None, t, 1), lambda h, i: (h, i, 0))],
        out_shape=[jax.ShapeDtypeStruct((s, d), F32), jax.ShapeDtypeStruct((heads, s, 1), F32)],
        compiler_params=_params(2),
    )(qn, kn, vb, cum_col, cum_row)


def _fox_attn_bwd(qn, kn, vb, d_o, o, lse, cum_col, cum_row, hd, t, name):
    s, d = qn.shape
    heads = d // hd
    nq = s // t

    def body(q_ref, k_ref, v_ref, do_ref, o_ref, lse_ref, cc_ref, cr_ref,
             dq_ref, dk_ref, dv_ref, dcq_ref, dck_ref, delta):
        kj = pl.program_id(1)

        @pl.when(kj == 0)
        def _():
            dq_ref[...] = jnp.zeros_like(dq_ref)
            dcq_ref[...] = jnp.zeros_like(dcq_ref)
            delta[...] = jnp.sum(do_ref[...] * o_ref[...], axis=1, keepdims=True)

        ks, vs, cr = k_ref[...], v_ref[...], cr_ref[...]
        kpos = kj * t + lax.broadcasted_iota(jnp.int32, (1, t), 1)

        def step(qi, carry, diagonal=False):
            dk, dv, dck = carry
            rows = pl.ds(pl.multiple_of(qi * t, t), t)
            q, d_out = q_ref[rows, :], do_ref[rows, :]
            sc = _dot(q, ks, tb=True) + cc_ref[rows, :] - cr
            p = jnp.exp(sc - lse_ref[rows, :])
            if diagonal:
                qpos = qi * t + lax.broadcasted_iota(jnp.int32, (t, 1), 0)
                p = jnp.where(kpos <= qpos, p, 0.0)
            ds = p * (_dot(d_out, vs, tb=True) - delta[rows, :])
            dq_ref[rows, :] += _dot(ds, ks)
            dcq_ref[rows, :] += jnp.sum(ds, axis=1, keepdims=True)
            return dk + _dot(ds, q, ta=True), dv + _dot(p, d_out, ta=True), dck + _colsum(ds)

        init = (jnp.zeros((t, hd), F32), jnp.zeros((t, hd), F32), jnp.zeros((1, t), F32))
        dk, dv, dck = lax.fori_loop(kj + 1, nq, step, step(kj, init, diagonal=True))
        dk_ref[...] = dk.astype(dk_ref.dtype)
        dv_ref[...] = dv.astype(dv_ref.dtype)
        dck_ref[...] = dck

    head_rows = lambda h, j: (0, h)
    blk = lambda h, j: (j, h)
    return _pcall(
        body, name=name, grid=(heads, nq),
        in_specs=[pl.BlockSpec((s, hd), head_rows), pl.BlockSpec((t, hd), blk), pl.BlockSpec((t, hd), blk),
                  pl.BlockSpec((s, hd), head_rows), pl.BlockSpec((s, hd), head_rows),
                  pl.BlockSpec((None, s, 1), lambda h, j: (h, 0, 0)),
                  pl.BlockSpec((None, s, 1), lambda h, j: (h, 0, 0)),
                  pl.BlockSpec((None, None, 1, t), lambda h, j: (h, j, 0, 0))],
        out_specs=[pl.BlockSpec((s, hd), head_rows), pl.BlockSpec((t, hd), blk), pl.BlockSpec((t, hd), blk),
                   pl.BlockSpec((None, s, 1), lambda h, j: (h, 0, 0)),
                   pl.BlockSpec((None, None, 1, t), lambda h, j: (h, j, 0, 0))],
        out_shape=[jax.ShapeDtypeStruct((s, d), F32), jax.ShapeDtypeStruct((s, d), BF16),
                   jax.ShapeDtypeStruct((s, d), BF16), jax.ShapeDtypeStruct((heads, s, 1), F32),
                   jax.ShapeDtypeStruct((heads, nq, 1, t), F32)],
        scratch_shapes=[pltpu.VMEM((s, 1), F32)],
        compiler_params=_params(2),
    )(qn, kn, vb, d_o, o, lse, cum_col, cum_row)


def _fox_gate_fwd(o, og, name):
    def fn(o, og):
        return (o * _sigmoid(og),)

    return _rowwise(fn, [("row", o), og], [("row", o.shape[1], BF16)], name=name)[0]


def _fox_gate_bwd(o, og, dact, name):
    def fn(o, og, dact):
        sg = _sigmoid(og)
        return dact * sg, dact * o * sg * (1.0 - sg)

    d = o.shape[1]
    return _rowwise(fn, [("row", o), og, ("row", dact)], [("row", d, F32), ("row", d, BF16)], name=name)


def _shift_down(x, n):
    rows = lax.broadcasted_iota(jnp.int32, x.shape, 0)
    return jnp.where(rows >= n, pltpu.roll(x, n, 0), 0.0)


def _shift_up(x, n):
    rows = lax.broadcasted_iota(jnp.int32, x.shape, 0)
    return jnp.where(rows < x.shape[0] - n, pltpu.roll(x, x.shape[0] - n, 0), 0.0)


def _conv(u, w_ref, b):
    return w_ref[0:1, :] * _shift_down(u, 2) + w_ref[1:2, :] * _shift_down(u, 1) + w_ref[2:3, :] * u + b


def _conv_act_fwd(u, cw, cb, name, tc=256):
    s, two_f = u.shape
    dff = two_f // 2
    tc = _tile(dff, tc)
    nb = dff // tc

    def body(ug_ref, uv_ref, wg_ref, wv_ref, bg_ref, bv_ref, a_ref):
        gate = _conv(ug_ref[...], wg_ref, bg_ref[...])
        val = _conv(uv_ref[...], wv_ref, bv_ref[...])
        a_ref[...] = (_silu(gate) * val).astype(a_ref.dtype)

    lo, hi = (lambda j: (0, j)), (lambda j: (0, j + nb))
    return _pcall(
        body, name=name, grid=(nb,),
        in_specs=[pl.BlockSpec((s, tc), lo), pl.BlockSpec((s, tc), hi), pl.BlockSpec((3, tc), lo),
                  pl.BlockSpec((3, tc), hi), pl.BlockSpec((1, tc), lo), pl.BlockSpec((1, tc), hi)],
        out_specs=pl.BlockSpec((s, tc), lo),
        out_shape=jax.ShapeDtypeStruct((s, dff), BF16),
        compiler_params=_params(1),
    )(u, u, cw, cw, cb, cb)


def _conv_act_bwd(u, cw, cb, da, name, tc=128):
    s, two_f = u.shape
    dff = two_f // 2
    tc = _tile(dff, tc)
    nb = dff // tc

    def body(ug_ref, uv_ref, wg_ref, wv_ref, bg_ref, bv_ref, da_ref, du_ref, dw_ref, db_ref):
        ug, uv, da = ug_ref[...], uv_ref[...], da_ref[...]
        gate = _conv(ug, wg_ref, bg_ref[...])
        val = _conv(uv, wv_ref, bv_ref[...])
        sg = _sigmoid(gate)
        d_val = da * (gate * sg)
        d_gate = da * val * (sg * (1.0 + gate * (1.0 - sg)))
        for half, (dc, uu, w_ref) in enumerate(((d_gate, ug, wg_ref), (d_val, uv, wv_ref))):
            du = w_ref[0:1, :] * _shift_up(dc, 2) + w_ref[1:2, :] * _shift_up(dc, 1) + w_ref[2:3, :] * dc
            du_ref[half] = du.astype(du_ref.dtype)
            dw_ref[half, 0:1, :] = _colsum(dc * _shift_down(uu, 2))
            dw_ref[half, 1:2, :] = _colsum(dc * _shift_down(uu, 1))
            dw_ref[half, 2:3, :] = _colsum(dc * uu)
            db_ref[half] = _colsum(dc)

    lo, hi = (lambda j: (0, j)), (lambda j: (0, j + nb))
    both = lambda j: (0, 0, j)
    return _pcall(
        body, name=name, grid=(nb,),
        in_specs=[pl.BlockSpec((s, tc), lo), pl.BlockSpec((s, tc), hi), pl.BlockSpec((3, tc), lo),
                  pl.BlockSpec((3, tc), hi), pl.BlockSpec((1, tc), lo), pl.BlockSpec((1, tc), hi),
                  pl.BlockSpec((s, tc), lo)],
        out_specs=[pl.BlockSpec((2, s, tc), both), pl.BlockSpec((2, 3, tc), both), pl.BlockSpec((2, 1, tc), both)],
        out_shape=[jax.ShapeDtypeStruct((2, s, dff), BF16), jax.ShapeDtypeStruct((2, 3, dff), F32),
                   jax.ShapeDtypeStruct((2, 1, dff), F32)],
        compiler_params=_params(1),
    )(u, u, cw, cw, cb, cb, da)


def _adamw_math(w, g, m, v):
    m = ADAM_B1 * m + (1.0 - ADAM_B1) * g
    v = ADAM_B2 * v + (1.0 - ADAM_B2) * (g * g)
    m_hat = m / (1.0 - ADAM_B1 ** ADAM_STEP)
    v_hat = v / (1.0 - ADAM_B2 ** ADAM_STEP)
    delta = -ADAM_LR * (m_hat / (jnp.sqrt(v_hat) + ADAM_EPS) + ADAM_WD * w)
    return delta, m, v


def _adamw(w, g, m, v, name, tr=128):
    layers, r, c = w.shape
    pieces = isinstance(g, (list, tuple))
    tc = c
    if r % 8:
        tr, tc = r, _tile(c, max(LANE, 512 * 1024 // r // LANE * LANE))
    elif r <= tr:
        tr = r
    while r % tr:
        tr -= 8
    nr, nc = r // tr, c // tc
    g_list = list(g) if pieces else [g]
    n_pieces = g_list[0].shape[0] if pieces else 0

    def body(w_ref, *rest):
        g_refs, (m_ref, v_ref, go_ref, d_ref, mo_ref, vo_ref) = rest[:len(g_list)], rest[len(g_list):]

        def update(grad):
            delta, m_new, v_new = _adamw_math(w_ref[...], grad, m_ref[...], v_ref[...])
            go_ref[...], d_ref[...], mo_ref[...], vo_ref[...] = grad, delta, m_new, v_new

        if not pieces:
            update(g_refs[0][...])
            return
        for layer, g_ref in enumerate(g_refs):
            @pl.when(pl.program_id(0) == layer)
            def _(g_ref=g_ref):
                grad = g_ref[0].astype(F32)
                for i in range(1, n_pieces):
                    grad = grad + g_ref[i].astype(F32)
                update(grad)

    spec = pl.BlockSpec((None, tr, tc), lambda l, i, j: (l, i, j))
    if pieces:
        def walk(k):
            def index(l, i, j):
                here = l == k
                return (0, jnp.where(here, i, jnp.where(l < k, 0, nr - 1)), jnp.where(here, j, jnp.where(l < k, 0, nc - 1)))
            return index

        g_specs = [pl.BlockSpec((n_pieces, tr, tc), walk(k)) for k in range(layers)]
    else:
        g_specs = [spec]
    return _pcall(
        body, name=name, grid=(layers, nr, nc), in_specs=[spec] + g_specs + [spec, spec], out_specs=[spec] * 4,
        out_shape=[jax.ShapeDtypeStruct((layers, r, c), F32)] * 4, compiler_params=_params(3),
    )(w, *g_list, m, v)


def _pair_sum(pieces, partner, core, name, tr=512):
    _, r, c = pieces.shape
    tc = c
    if r % 8:
        tr, tc = r, _tile(c, max(LANE, 1024 * 1024 // r // LANE * LANE))
    elif r <= tr:
        tr = r
    while r % tr:
        tr -= 8

    def body(core_ref, mine_ref, partner_ref, out_ref):
        out_ref[...] = (mine_ref[...].astype(F32) + partner_ref[...].astype(F32)).astype(out_ref.dtype)

    return _pcall(
        body, name=name,
        grid_spec=pltpu.PrefetchScalarGridSpec(
            num_scalar_prefetch=1, grid=(4, r // tr, c // tc),
            in_specs=[pl.BlockSpec((None, tr, tc), lambda q, i, j, core_ref: (2 * q + core_ref[0], i, j)),
                      pl.BlockSpec((None, tr, tc), lambda q, i, j, core_ref: (q, i, j))],
            out_specs=pl.BlockSpec((None, tr, tc), lambda q, i, j, core_ref: (q, i, j))),
        out_shape=jax.ShapeDtypeStruct((4, r, c), pieces.dtype), compiler_params=_params(3),
    )(core, pieces, partner)


def _sum8(x, name):
    p = x.shape[2]
    tp = _tile(p, 16 * 1024)

    def body(x_ref, o_ref):
        acc = x_ref[0]
        for i in range(1, N_DEV):
            acc = acc + x_ref[i]
        o_ref[...] = acc

    return _pcall(
        body, name=name, grid=(p // tp,), in_specs=[pl.BlockSpec((N_DEV, 1, tp), lambda i: (0, 0, i))],
        out_specs=pl.BlockSpec((1, tp), lambda i: (0, i)), out_shape=jax.ShapeDtypeStruct((1, p), x.dtype),
        compiler_params=_params(1),
    )(x)


def _exchange(arrays, name, scatter):
    n = len(arrays)
    hbm = pl.BlockSpec(memory_space=pl.ANY)

    def body(*refs):
        ins, outs, token = refs[:n], refs[n:2 * n], refs[2 * n]
        send_sems, recv_sems, local_sems = refs[2 * n + 1:]
        token[...] = jnp.zeros_like(token)
        x, y, c = lax.axis_index("x"), lax.axis_index("y"), lax.axis_index("c")
        me = 4 * x + 2 * y + c
        copies = []
        for a in range(n):
            src_mine = ins[a].at[me] if scatter else ins[a]
            local = pltpu.make_async_copy(src_mine, outs[a].at[me], local_sems.at[a])
            local.start()
            copies.append(local)
            for k in range(1, N_DEV):
                px = 1 - x if k & 4 else x
                py = 1 - y if k & 2 else y
                pc = 1 - c if k & 1 else c
                src = ins[a].at[4 * px + 2 * py + pc] if scatter else ins[a]
                cp = pltpu.make_async_remote_copy(
                    src_ref=src, dst_ref=outs[a].at[me],
                    send_sem=send_sems.at[a * (N_DEV - 1) + k - 1], recv_sem=recv_sems.at[a * (N_DEV - 1) + k - 1],
                    device_id=(px, py, pc), device_id_type=pl.DeviceIdType.MESH)
                cp.start()
                copies.append(cp)
        for cp in copies:
            cp.wait()

    out_shape = [jax.ShapeDtypeStruct(a.shape if scatter else (N_DEV,) + a.shape, a.dtype) for a in arrays]
    res = _pcall(
        body, name=name, in_specs=[hbm] * n, out_specs=[hbm] * n + [pl.BlockSpec(memory_space=pltpu.VMEM)],
        out_shape=out_shape + [jax.ShapeDtypeStruct((8, LANE), F32)],
        scratch_shapes=[pltpu.SemaphoreType.DMA((n * (N_DEV - 1),)), pltpu.SemaphoreType.DMA((n * (N_DEV - 1),)),
                        pltpu.SemaphoreType.DMA((n,))],
        compiler_params=pltpu.CompilerParams(has_side_effects=True),
    )(*arrays)
    return res[:n], res[n][0, 0]


_HBM = pl.BlockSpec(memory_space=pltpu.HBM)
_SEM = pl.BlockSpec(memory_space=pltpu.SEMAPHORE)
_DATAFLOW = pltpu.SideEffectType.DATAFLOW_SIDE_EFFECTING


def _peer(k, x, y, c):
    return (1 - x if k & 4 else x, 1 - y if k & 2 else y, 1 - c if k & 1 else c)


def _pair_plan(x, y, c):
    return [(2 * q + (1 - c), q, (x, y, 1 - c)) for q in range(4)]


def _chip_plan(x, y, c):
    out = []
    for k in _ICI_PEERS:
        px, py, pc = _peer(k, x, y, c)
        out.append((2 * px + py, 2 * x + y, (px, py, pc)))
    return out


def _split_start(arrays, plan, name):
    n = len(arrays)
    lands = [lax.empty((4,) + a.shape[1:], a.dtype) for a in arrays]
    n_copies = len(plan(0, 0, 0))

    def body(*refs):
        srcs, dsts = refs[:n], refs[n:2 * n]
        send_sems, recv_sems, token = refs[4 * n:5 * n], refs[5 * n:6 * n], refs[6 * n]
        copies = plan(lax.axis_index("x"), lax.axis_index("y"), lax.axis_index("c"))
        for a in range(n):
            for j, (src_block, dst_block, peer) in enumerate(copies):
                pltpu.make_async_remote_copy(
                    src_ref=srcs[a].at[src_block], dst_ref=dsts[a].at[dst_block],
                    send_sem=send_sems[a].at[j], recv_sem=recv_sems[a].at[j],
                    device_id=peer, device_id_type=pl.DeviceIdType.MESH).start()
        token[...] = jnp.zeros_like(token)

    sems = [pltpu.SemaphoreType.DMA((n_copies,))] * (2 * n)
    res = _pcall(
        body, name=name,
        in_specs=[_HBM] * (2 * n),
        out_specs=[_HBM] * (2 * n) + [_SEM] * (2 * n) + [pl.BlockSpec(memory_space=pltpu.VMEM)],
        out_shape=[pltpu.HBM(a.shape, a.dtype) for a in arrays] + [pltpu.HBM(l.shape, l.dtype) for l in lands]
        + sems + [jax.ShapeDtypeStruct((8, LANE), F32)],
        input_output_aliases={i: i for i in range(2 * n)},
        compiler_params=pltpu.CompilerParams(has_side_effects=_DATAFLOW),
    )(*[pltpu.with_memory_space_constraint(a, pltpu.HBM) for a in arrays],
      *[pltpu.with_memory_space_constraint(l, pltpu.HBM) for l in lands])
    handles = [(res[a], res[n + a], res[2 * n + a], res[3 * n + a]) for a in range(n)]
    return handles, res[4 * n][0, 0]


def _split_wait(handles, plan, after, name):
    n = len(handles)
    after = list(after) if isinstance(after, (list, tuple)) else [after]

    def body(*refs):
        srcs, dsts = refs[:n], refs[n:2 * n]
        send_sems, recv_sems = refs[2 * n:3 * n], refs[3 * n:4 * n]
        copies = plan(lax.axis_index("x"), lax.axis_index("y"), lax.axis_index("c"))
        for a in range(n):
            for j, (src_block, dst_block, peer) in enumerate(copies):
                cp = pltpu.make_async_remote_copy(
                    src_ref=srcs[a].at[src_block], dst_ref=dsts[a].at[dst_block],
                    send_sem=send_sems[a].at[j], recv_sem=recv_sems[a].at[j],
                    device_id=peer, device_id_type=pl.DeviceIdType.MESH)
                cp.wait_send()
                cp.wait_recv()

    srcs, lands = [h[0] for h in handles], [h[1] for h in handles]
    res = _pcall(
        body, name=name,
        in_specs=[_HBM] * (2 * n) + [_SEM] * (2 * n) + [pl.BlockSpec(memory_space=pl.ANY)] * len(after),
        out_specs=[_HBM] * (2 * n),
        out_shape=[pltpu.HBM(t.shape, t.dtype) for t in srcs + lands],
        input_output_aliases={i: i for i in range(2 * n)},
        compiler_params=pltpu.CompilerParams(has_side_effects=_DATAFLOW),
    )(*srcs, *lands, *[h[2] for h in handles], *[h[3] for h in handles], *after)
    return res[:n], res[n:]


_ICI_PEERS = (2, 4, 6)


def _gather2_start(shards, name):
    n = len(shards)
    lands = [lax.empty((N_DEV,) + a.shape, a.dtype) for a in shards]

    def body(*refs):
        srcs, dsts = refs[:n], refs[n:2 * n]
        send_sems, d2d_sems, ici_sems = refs[4 * n:5 * n], refs[5 * n:6 * n], refs[6 * n:7 * n]
        token = refs[7 * n]
        x, y, c = lax.axis_index("x"), lax.axis_index("y"), lax.axis_index("c")
        me = 4 * x + 2 * y + c
        for a in range(n):
            for j, k in enumerate((1,) + _ICI_PEERS):
                recv = d2d_sems[a].at[0] if j == 0 else ici_sems[a].at[j - 1]
                pltpu.make_async_remote_copy(
                    src_ref=srcs[a], dst_ref=dsts[a].at[me], send_sem=send_sems[a].at[j], recv_sem=recv,
                    device_id=_peer(k, x, y, c), device_id_type=pl.DeviceIdType.MESH).start()
        token[...] = jnp.zeros_like(token)

    dma = pltpu.SemaphoreType.DMA
    res = _pcall(
        body, name=name,
        in_specs=[_HBM] * (2 * n),
        out_specs=[_HBM] * (2 * n) + [_SEM] * (3 * n) + [pl.BlockSpec(memory_space=pltpu.VMEM)],
        out_shape=[pltpu.HBM(a.shape, a.dtype) for a in shards] + [pltpu.HBM(l.shape, l.dtype) for l in lands]
        + [dma((4,))] * n + [dma((1,))] * n + [dma((3,))] * n + [jax.ShapeDtypeStruct((8, LANE), F32)],
        input_output_aliases={i: i for i in range(2 * n)},
        compiler_params=pltpu.CompilerParams(has_side_effects=_DATAFLOW),
    )(*[pltpu.with_memory_space_constraint(a, pltpu.HBM) for a in shards],
      *[pltpu.with_memory_space_constraint(l, pltpu.HBM) for l in lands])
    handles = [tuple(res[i * n + a] for i in range(5)) for a in range(n)]
    return handles, res[5 * n][0, 0]


def _gather2_forward(handle, after, name):
    src, land, send_sems, d2d_sem, ici_sems = handle

    def body(land_ref, ici_ref, after_ref, land_out, fwd_send, fwd_recv, token):
        x, y, c = lax.axis_index("x"), lax.axis_index("y"), lax.axis_index("c")
        for j, k in enumerate(_ICI_PEERS):
            px, py, pc = _peer(k, x, y, c)
            block = land_ref.at[4 * px + 2 * py + pc]
            pltpu.make_async_remote_copy(
                src_ref=block, dst_ref=block, send_sem=fwd_send.at[j], recv_sem=ici_ref.at[j],
                device_id=(px, py, pc), device_id_type=pl.DeviceIdType.MESH).wait_recv()
            pltpu.make_async_remote_copy(
                src_ref=block, dst_ref=block, send_sem=fwd_send.at[j], recv_sem=fwd_recv.at[j],
                device_id=(x, y, 1 - c), device_id_type=pl.DeviceIdType.MESH).start()
        token[...] = jnp.zeros_like(token)

    dma = pltpu.SemaphoreType.DMA
    land, fwd_send, fwd_recv, token = _pcall(
        body, name=name,
        in_specs=[_HBM, _SEM, pl.BlockSpec(memory_space=pl.ANY)],
        out_specs=[_HBM, _SEM, _SEM, pl.BlockSpec(memory_space=pltpu.VMEM)],
        out_shape=[pltpu.HBM(land.shape, land.dtype), dma((3,)), dma((3,)), jax.ShapeDtypeStruct((8, LANE), F32)],
        input_output_aliases={0: 0},
        compiler_params=pltpu.CompilerParams(has_side_effects=_DATAFLOW),
    )(land, ici_sems, after)
    return (src, land, send_sems, d2d_sem, fwd_send, fwd_recv), token[0, 0]


def _gather2_wait(handle, after, name):
    src, land, send_sems, d2d_sem, fwd_send, fwd_recv = handle

    def body(src_ref, land_ref, send_ref, d2d_ref, fsend_ref, frecv_ref, after_ref, src_out, land_out):
        x, y, c = lax.axis_index("x"), lax.axis_index("y"), lax.axis_index("c")
        me = 4 * x + 2 * y + c
        sibling = (x, y, 1 - c)
        block = land_ref.at[me]

        def copy(send, recv):
            return pltpu.make_async_remote_copy(src_ref=src_ref, dst_ref=block, send_sem=send, recv_sem=recv,
                                                device_id=sibling, device_id_type=pl.DeviceIdType.MESH)

        for j in range(4):
            copy(send_ref.at[j], d2d_ref.at[0]).wait_send()
        copy(send_ref.at[0], d2d_ref.at[0]).wait_recv()
        for j in range(3):
            copy(fsend_ref.at[j], frecv_ref.at[j]).wait_send()
            copy(fsend_ref.at[j], frecv_ref.at[j]).wait_recv()

    res = _pcall(
        body, name=name,
        in_specs=[_HBM, _HBM, _SEM, _SEM, _SEM, _SEM, pl.BlockSpec(memory_space=pl.ANY)],
        out_specs=[_HBM, _HBM],
        out_shape=[pltpu.HBM(src.shape, src.dtype), pltpu.HBM(land.shape, land.dtype)],
        input_output_aliases={0: 0, 1: 1},
        compiler_params=pltpu.CompilerParams(has_side_effects=_DATAFLOW),
    )(src, land, send_sems, d2d_sem, fwd_send, fwd_recv, after)
    return res[0], res[1]


def _pad_cols(x, width=LANE):
    return jnp.pad(x, ((0, 0), (0, width - x.shape[1])))


def _cols_full(g):
    return jnp.transpose(g, (1, 0, 2)).reshape(g.shape[1], -1)


def _ffn_fwd(x1, p, i, tag):
    h2 = _adaln_fwd(x1, p["norm_ffn"][i], p["sc_f"][i], p["sh_f"][i], f"ffn_norm_{tag}")
    u = _matmul(h2, p["fetch"](f"up{i}", h2), name=f"ffn_up_{tag}", tn=1408, b_shards=True)
    a = _conv_act_fwd(u, p["conv_w"][i], p["conv_b"][i], f"ffn_act_{tag}")
    g_f = p["g_f"][i]
    x2, f = _matmul(a, p["fetch"](f"down{i}", a), name=f"ffn_down_{tag}", tk=1408, out_dtypes=(F32, F32),
                    epilogue=lambda acc, x1, g: (x1 + (1.0 + g) * acc, acc), extras=(("mn", x1), ("n", g_f)))
    return x2, dict(h2=h2, u=u, a=a, f=f)


def _ffn_bwd(dx2, x1, saved, p, i, tag):
    d = x1.shape[1]
    w_up, w_down = p["fetch"](f"up{i}", None), p["fetch"](f"down{i}", None)
    df, dg_f = _residual_bwd(dx2, saved["f"], p["g_f"][i], f"ffn_res_bwd_{tag}")
    da = _matmul(df, w_down, tb=True, name=f"ffn_down_dx_{tag}", tn=1408)
    dw_down = _matmul(saved["a"], df, ta=True, name=f"ffn_down_dw_{tag}", tm=1408, out_dtypes=(BF16,))
    du, dcw, dcb = _conv_act_bwd(saved["u"], p["conv_w"][i], p["conv_b"][i], da, f"ffn_act_bwd_{tag}")
    dcw, dcb = (jnp.concatenate([t[0], t[1]], axis=1) for t in (dcw, dcb))
    tok = p["flush"](du)
    dh2 = _matmul(du, w_up, tb=True, name=f"ffn_up_dx_{tag}", tk=1408, a_halves=True, b_shards=True)
    dw_up = _matmul(saved["h2"], du, ta=True, name=f"ffn_up_dw_{tag}", tn=1408, out_dtypes=(BF16,), b_halves=True,
                    out_shards=True)
    tok = tok + p["send"](f"ffn{i}", [dw_up, dw_down.reshape(N_DEV, -1, d)])
    dx1, dsh, dsc, dgain = _adaln_bwd(x1, dh2, dx2, p["norm_ffn"][i] + tok, p["sc_f"][i], f"ffn_norm_bwd_{tag}")
    grads = dict(conv_w=dcw, conv_b=dcb, norm_ffn=dgain, sh_f=dsh, sc_f=dsc, g_f=dg_f)
    return dx1, grads


def _gla_layer_fwd(x, p, i):
    h1 = _adaln_fwd(x, p["norm_mix"][i], p["sc_m"][i], p["sh_m"][i], "gla_norm")
    w_t, w_tail_t, main = p["fetch"]("gla_in", h1)
    proj = _matmul(h1, w_t, tb=True, b_rows=main, name="gla_in")
    a_tail = _matmul(h1, w_tail_t, tb=True, name="gla_in_tail")
    dk_total = p["gla_wg_p"].shape[1]
    o, states = _gla_fwd(proj, a_tail, p["gla_wg_p"], p["gla_b_gate"], "gla_chunks")
    assert 2 * dk_total == o.shape[1]
    r = ("cols", proj, 2, o.shape[1])
    og = _gla_post_fwd(o, r, p["gla_norm"], "gla_post")
    x1, y = _matmul(og, p["fetch"]("gla_out", og), name="gla_out", out_dtypes=(F32, F32),
                    epilogue=lambda acc, x, g: (x + (1.0 + g) * acc, acc), extras=(("mn", x), ("n", p["g_m"][i])))
    return x1, dict(h1=h1, proj=proj, a_tail=a_tail, o=o, r=r, states=states, og=og, y=y)


def _gla_layer_bwd(dx1, x, sv, p, i):
    d = x.shape[1]
    (w_t, w_tail_t, main), w_out = p["fetch"]("gla_in", None), p["fetch"]("gla_out", None)
    dy, dg_m = _residual_bwd(dx1, sv["y"], p["g_m"][i], "gla_res_bwd")
    dog = _matmul(dy, w_out, tb=True, name="gla_out_dx")
    dw_out = _matmul(sv["og"], dy, ta=True, name="gla_out_dw", out_dtypes=(BF16,))
    tok = p["flush"](dog) + p["send"]("gla_out", [dw_out.reshape(N_DEV, -1, d)])
    d_o, d_r, dgn = _gla_post_bwd(sv["o"], sv["r"], p["gla_norm"] + tok, dog, "gla_post_bwd")
    dq, dk, dv, dga = _gla_bwd(sv["proj"], sv["a_tail"], p["gla_wg_p"], p["gla_b_gate"], sv["states"], d_o,
                               "gla_chunks_bwd")
    tok = p["flush"](dga)
    da_tail = _matmul(dga, p["gla_wg_p"], tb=True, name="gla_gate_dx", out_dtypes=(BF16,))
    dwg = _matmul(sv["a_tail"], dga, ta=True, name="gla_gate_dw")
    dbg = _rowwise(lambda t: (_colsum(t),), [("row", dga)], [("acc", dga.shape[1], F32)], name="gla_gate_db")[0]
    dproj = jnp.concatenate([dq, dk, dv, d_r], axis=1)
    dh_tail = _matmul(da_tail, w_tail_t, name="gla_in_tail_dx")
    dh1 = _matmul(dproj, w_t, b_rows=main, name="gla_in_dx", tk=2048,
                  epilogue=lambda acc, t: (acc + t,), extras=(("mn", dh_tail),))
    rank = p["gla_rank"]
    dw_main = _matmul(dproj, sv["h1"], ta=True, name="gla_in_dw", out_dtypes=(BF16,), out_rows=main + rank)
    dx, dsh, dsc, dgain = _adaln_bwd(x, dh1, dx1, p["norm_mix"][i] + tok, p["sc_m"][i], "gla_norm_bwd")
    grads = dict(gla_w_gate=dwg[:rank], gla_b_gate=dbg, gla_norm=dgn, norm_mix=dgain, sh_m=dsh, sc_m=dsc, g_m=dg_m,
                 gla_w_in_unsent=(dw_main, da_tail, sv["h1"]))
    return dx, grads


def _fox_layer_fwd(x, p, i):
    d = x.shape[1]
    hd = p["fox_q_norm"].shape[1]
    heads = d // hd
    s = x.shape[0]
    t = _tile(s, 512)
    h1 = _adaln_fwd(x, p["norm_mix"][i], p["sc_m"][i], p["sh_m"][i], "fox_norm")
    w_t, w_tail_t, main = p["fetch"]("fox_in", h1)
    proj = _matmul(h1, w_t, tb=True, b_rows=main, name="fox_in")
    fl = _matmul(h1, w_tail_t, tb=True, name="fox_in_tail")
    q, k, v, og = (("cols", proj, j, d) for j in range(4))
    qn, kn, vb = _fox_prep(q, k, v, p["fox_q_norm"], p["fox_k_norm"], d, hd, "fox_prep")
    cum = _fox_cum(fl, p["fox_bf_p"], "fox_cum")
    cum_t = jnp.transpose(cum[:, :heads])
    cum_col, cum_row = cum_t[:, :, None], cum_t.reshape(heads, s // t, 1, t)
    o, lse = _fox_attn_fwd(qn, kn, vb, cum_col, cum_row, hd, t, "fox_attn")
    act = _fox_gate_fwd(o, og, "fox_gate")
    x1, y = _matmul(act, p["fetch"]("fox_out", act), name="fox_out", out_dtypes=(F32, F32),
                    epilogue=lambda acc, x, g: (x + (1.0 + g) * acc, acc), extras=(("mn", x), ("n", p["g_m"][i])))
    return x1, dict(h1=h1, q=q, k=k, og=og, fl=fl, qn=qn, kn=kn, vb=vb, cum_col=cum_col, cum_row=cum_row,
                    o=o, lse=lse, act=act, y=y, t=t, hd=hd)


def _fox_layer_bwd(dx1, x, sv, p, i):
    d = x.shape[1]
    hd, t = sv["hd"], sv["t"]
    heads = d // hd
    s = x.shape[0]
    (w_t, w_tail_t, main), w_out = p["fetch"]("fox_in", None), p["fetch"]("fox_out", None)
    dy, dg_m = _residual_bwd(dx1, sv["y"], p["g_m"][i], "fox_res_bwd")
    dact = _matmul(dy, w_out, tb=True, name="fox_out_dx")
    dw_out = _matmul(sv["act"], dy, ta=True, name="fox_out_dw", out_dtypes=(BF16,))
    d_o, d_og = _fox_gate_bwd(sv["o"], sv["og"], dact, "fox_gate_bwd")
    tok_flush = p["flush"](d_og)
    dqn, dkn, dvb, dcq, dck = _fox_attn_bwd(sv["qn"], sv["kn"], sv["vb"], d_o, sv["o"], sv["lse"], sv["cum_col"],
                                            sv["cum_row"], hd, t, "fox_attn_bwd")
    dq, dk, gq, gk = _fox_prep_bwd(sv["q"], sv["k"], dqn, dkn, p["fox_q_norm"], p["fox_k_norm"], hd, "fox_prep_bwd")
    dcum = _pad_cols(jnp.transpose(dcq[:, :, 0] - dck.reshape(heads, s)))
    dfl, dbf = _fox_cum_bwd(dcum, sv["fl"], p["fox_bf_p"], "fox_cum_bwd")
    dfl_b = dfl.astype(BF16)
    dproj = jnp.concatenate([dq, dk, dvb, d_og], axis=1)
    dh_tail = _matmul(dfl_b, w_tail_t, name="fox_in_tail_dx")
    dh1 = _matmul(dproj, w_t, b_rows=main, name="fox_in_dx", tk=2048,
                  epilogue=lambda acc, tl: (acc + tl,), extras=(("mn", dh_tail),))
    dw_main = _matmul(dproj, sv["h1"], ta=True, name="fox_in_dw", out_dtypes=(BF16,), out_rows=main + heads)
    dw_in = _tail_rows(dfl_b, sv["h1"], dw_main, heads, "fox_in_tail_dw").reshape(N_DEV, -1, d)
    tok = tok_flush + p["send"]("fox", [dw_in, dw_out.reshape(N_DEV, -1, d)])
    dx, dsh, dsc, dgain = _adaln_bwd(x, dh1, dx1, p["norm_mix"][i] + tok, p["sc_m"][i], "fox_norm_bwd")
    grads = dict(fox_b_f=dbf[:, :heads], fox_q_norm=gq.reshape(heads, hd).sum(0, keepdims=True),
                 fox_k_norm=gk.reshape(heads, hd).sum(0, keepdims=True), norm_mix=dgain, sh_m=dsh, sc_m=dsc, g_m=dg_m)
    return dx, grads


SMALL = ("b_mod", "norm_mix", "norm_ffn", "gla_b_gate", "gla_norm", "fox_b_f", "fox_q_norm", "fox_k_norm",
         "ffn_conv_b", "norm_final")
SMALL_SHARDED = ("gla_w_gate", "ffn_conv_w")
BIG = ("gla_w_in", "gla_w_out", "fox_w_in", "fox_w_out", "ffn_w_up", "ffn_w_down")
WEIGHTS = ("w_mod", "b_mod", "norm_mix", "norm_ffn", "gla_w_in", "gla_w_gate", "gla_b_gate", "gla_norm", "gla_w_out",
           "fox_w_in", "fox_b_f", "fox_q_norm", "fox_k_norm", "fox_w_out", "ffn_w_up", "ffn_conv_w", "ffn_conv_b",
           "ffn_w_down", "norm_final")


def _pack(parts):
    flat = jnp.concatenate([p.reshape(-1) for p in parts])
    pad = (-flat.shape[0]) % 1024
    return jnp.pad(flat, (0, pad)).reshape(1, -1)


def _unpack(flat, shapes):
    out, off = [], 0
    for shp in shapes:
        n = 1
        for s in shp:
            n *= s
        out.append(flat[0, off:off + n].reshape(shp))
        off += n
    return out


def kernel(x, c, w_mod, b_mod, norm_mix, norm_ffn, gla_w_in, gla_w_gate, gla_b_gate, gla_norm, gla_w_out, fox_w_in, fox_b_f, fox_q_norm, fox_k_norm, fox_w_out, ffn_w_up, ffn_conv_w, ffn_conv_b, ffn_w_down, norm_final, loss_target, m_w_mod, m_b_mod, m_norm_mix, m_norm_ffn, m_gla_w_in, m_gla_w_gate, m_gla_b_gate, m_gla_norm, m_gla_w_out, m_fox_w_in, m_fox_b_f, m_fox_q_norm, m_fox_k_norm, m_fox_w_out, m_ffn_w_up, m_ffn_conv_w, m_ffn_conv_b, m_ffn_w_down, m_norm_final, v_w_mod, v_b_mod, v_norm_mix, v_norm_ffn, v_gla_w_in, v_gla_w_gate, v_gla_b_gate, v_gla_norm, v_gla_w_out, v_fox_w_in, v_fox_b_f, v_fox_q_norm, v_fox_k_norm, v_fox_w_out, v_ffn_w_up, v_ffn_conv_w, v_ffn_conv_b, v_ffn_w_down, v_norm_final):
    w = dict(w_mod=w_mod, b_mod=b_mod, norm_mix=norm_mix, norm_ffn=norm_ffn, gla_w_in=gla_w_in, gla_w_gate=gla_w_gate,
             gla_b_gate=gla_b_gate, gla_norm=gla_norm, gla_w_out=gla_w_out, fox_w_in=fox_w_in, fox_b_f=fox_b_f,
             fox_q_norm=fox_q_norm, fox_k_norm=fox_k_norm, fox_w_out=fox_w_out, ffn_w_up=ffn_w_up,
             ffn_conv_w=ffn_conv_w, ffn_conv_b=ffn_conv_b, ffn_w_down=ffn_w_down, norm_final=norm_final)
    mom_m = dict(w_mod=m_w_mod, b_mod=m_b_mod, norm_mix=m_norm_mix, norm_ffn=m_norm_ffn, gla_w_in=m_gla_w_in,
                 gla_w_gate=m_gla_w_gate, gla_b_gate=m_gla_b_gate, gla_norm=m_gla_norm, gla_w_out=m_gla_w_out,
                 fox_w_in=m_fox_w_in, fox_b_f=m_fox_b_f, fox_q_norm=m_fox_q_norm, fox_k_norm=m_fox_k_norm,
                 fox_w_out=m_fox_w_out, ffn_w_up=m_ffn_w_up, ffn_conv_w=m_ffn_conv_w, ffn_conv_b=m_ffn_conv_b,
                 ffn_w_down=m_ffn_w_down, norm_final=m_norm_final)
    mom_v = dict(w_mod=v_w_mod, b_mod=v_b_mod, norm_mix=v_norm_mix, norm_ffn=v_norm_ffn, gla_w_in=v_gla_w_in,
                 gla_w_gate=v_gla_w_gate, gla_b_gate=v_gla_b_gate, gla_norm=v_gla_norm, gla_w_out=v_gla_w_out,
                 fox_w_in=v_fox_w_in, fox_b_f=v_fox_b_f, fox_q_norm=v_fox_q_norm, fox_k_norm=v_fox_k_norm,
                 fox_w_out=v_fox_w_out, ffn_w_up=v_ffn_w_up, ffn_conv_w=v_ffn_conv_w, ffn_conv_b=v_ffn_conv_b,
                 ffn_w_down=v_ffn_w_down, norm_final=v_norm_final)

    me = 4 * lax.axis_index("x") + 2 * lax.axis_index("y") + lax.axis_index("c")
    xs, target = x[0], loss_target[0]
    s, d = xs.shape
    depth = w_mod.shape[0]
    mod_cols = w_mod.shape[2]
    rank = gla_w_gate.shape[1]
    hd = fox_q_norm.shape[1]
    fox_heads = d // hd
    dk_total = gla_w_gate.shape[2] * N_DEV

    cond = c * (1.0 / (1.0 + jnp.exp(-c)))
    g, _ = _exchange([gla_w_gate[0], ffn_conv_w, cond], "gather_small", scatter=False)
    cond_all = g[2][:, 0, :]

    cond_pad = jnp.pad(cond_all, ((0, 16 - N_DEV), (0, 0)))
    mod_part = []
    for i in range(depth):
        b_cols = lax.dynamic_slice(b_mod[i:i + 1], (0, me * mod_cols), (1, mod_cols))
        mod_part.append(_matmul(cond_pad, w_mod[i], name=f"mod_{i}", tn=768,
                                epilogue=lambda acc, b: (acc + b,), extras=(("n", b_cols),))[:N_DEV])
    (mod_all,), tok_mod = _exchange([jnp.stack(mod_part)], "gather_mod", scatter=False)
    mod = lax.dynamic_index_in_dim(mod_all, me, axis=2, keepdims=False)
    mod = jnp.transpose(mod, (1, 0, 2)).reshape(depth, 6, 1, d)

    big_names = ["gla_in", "gla_out", "up0", "down0", "fox_in", "fox_out", "up1", "down1"]
    first = [jnp.transpose(gla_w_in[0] + tok_mod).astype(BF16), gla_w_out[0].astype(BF16)]
    handles, tok_first = _gather2_start(first, "gather_weights_start_first")
    rest = [ffn_w_up[0] + tok_first, ffn_w_down[0], jnp.transpose(fox_w_in[0]), fox_w_out[0], ffn_w_up[1],
            ffn_w_down[1]]
    handles_rest, tok0 = _gather2_start([t.astype(BF16) for t in rest], "gather_weights_start_rest")
    handles = handles + handles_rest
    ready, forwarded = {}, {}

    def split_tail(full_t, tail):
        main = full_t.shape[0] - tail
        return full_t, jnp.pad(full_t[main:], ((0, LANE - tail), (0, 0))), main

    def forward(idx, after):
        key = big_names[idx]
        forwarded[key] = _gather2_forward(handles[idx], after, f"gather_{key}_forward")

    def fetch(key, after):
        if key not in ready:
            idx = big_names.index(key)
            if idx == 0:
                forward(0, after)
            handle, _ = forwarded[key]
            mine, land = _gather2_wait(handle, after, f"gather_{key}_wait")
            if idx + 1 < len(big_names):
                forward(idx + 1, land)
                mine = mine + forwarded[big_names[idx + 1]][1].astype(BF16)
            full = lax.dynamic_update_slice(land, mine[None], (me,) + (0,) * mine.ndim)
            if key == "gla_in":
                ready[key] = split_tail(full.reshape(-1, d), rank)
            elif key == "fox_in":
                ready[key] = split_tail(full.reshape(-1, d), fox_heads)
            elif key.startswith("up"):
                ready[key] = full
            else:
                ready[key] = full.reshape(-1, d)
        return ready[key]

    pending, sent = [], {}
    core = lax.axis_index("c").astype(jnp.int32).reshape(1)
    chip = 2 * lax.axis_index("x") + lax.axis_index("y")

    def send(key, pieces):
        hs, tok = _split_start(pieces, _pair_plan, f"scatter_{key}_pair_start")
        pending.append((key, hs))
        return tok

    def flush(after):
        tok = 0.0
        while pending:
            key, hs = pending.pop(0)
            mine, partner = _split_wait(hs, _pair_plan, after, f"scatter_{key}_pair_wait")
            sums = [_pair_sum(pc, pt, core, f"scatter_{key}_pair_sum{a}")
                    for a, (pc, pt) in enumerate(zip(mine, partner))]
            sent[key], t = _split_start(sums, _chip_plan, f"scatter_{key}_chip_start")
            tok = tok + t
        return tok

    p = dict(
        fetch=fetch, send=send, flush=flush,
        gla_wg_p=jnp.pad(_cols_full(g[0]), ((0, LANE - rank), (0, 0))),
        conv_w=[jnp.transpose(g[1][:, i], (1, 0, 2)).reshape(ffn_conv_w.shape[1], -1) for i in range(depth)],
        conv_b=[ffn_conv_b[i:i + 1] for i in range(depth)],
        gla_b_gate=gla_b_gate, gla_norm=gla_norm, fox_q_norm=fox_q_norm, fox_k_norm=fox_k_norm,
        fox_bf_p=_pad_cols(fox_b_f), gla_rank=rank,
        norm_mix=[norm_mix[i:i + 1] + (tok0 if i == 0 else 0.0) for i in range(depth)],
        norm_ffn=[norm_ffn[i:i + 1] for i in range(depth)],
    )

    for j, nm in enumerate(("sh_m", "sc_m", "g_m", "sh_f", "sc_f", "g_f")):
        p[nm] = [mod[i, j] for i in range(depth)]

    acts, saved = [xs], []
    for i in range(depth):
        layer_fwd = _gla_layer_fwd if i % 2 == 0 else _fox_layer_fwd
        x1, sv_mix = layer_fwd(acts[-1], p, i)
        x2, sv_ffn = _ffn_fwd(x1, p, i, str(i))
        saved.append((acts[-1], x1, sv_mix, sv_ffn))
        acts.append(x2)
    dx, d_norm_final, loss_part = _final_loss(acts[-1], target, norm_final.reshape(1, d), "final_loss")

    lg = [None] * depth
    for i in reversed(range(depth)):
        x_in, x1, sv_mix, sv_ffn = saved[i]
        dx, g_ffn = _ffn_bwd(dx, x1, sv_ffn, p, i, str(i))
        layer_bwd = _gla_layer_bwd if i % 2 == 0 else _fox_layer_bwd
        dx, g_mix = layer_bwd(dx, x_in, sv_mix, p, i)
        lg[i] = {**g_ffn, **g_mix}
    grad_x = dx[None]

    gla_l = [i for i in range(depth) if i % 2 == 0]
    fox_l = [i for i in range(depth) if i % 2 == 1]
    small_parts = dict(
        norm_mix=jnp.concatenate([lg[i]["norm_mix"] for i in range(depth)]),
        norm_ffn=jnp.concatenate([lg[i]["norm_ffn"] for i in range(depth)]),
        gla_b_gate=jnp.concatenate([lg[i]["gla_b_gate"] for i in gla_l]),
        gla_norm=jnp.concatenate([lg[i]["gla_norm"] for i in gla_l]),
        fox_b_f=jnp.concatenate([lg[i]["fox_b_f"] for i in fox_l]),
        fox_q_norm=jnp.concatenate([lg[i]["fox_q_norm"] for i in fox_l]),
        fox_k_norm=jnp.concatenate([lg[i]["fox_k_norm"] for i in fox_l]),
        ffn_conv_b=jnp.concatenate([lg[i]["conv_b"] for i in range(depth)]),
        norm_final=d_norm_final,
        gla_w_gate=jnp.stack([lg[i]["gla_w_gate"] for i in gla_l]),
        ffn_conv_w=jnp.stack([lg[i]["conv_w"] for i in range(depth)]),
        loss=loss_part[:, :1],
    )
    order = ("norm_mix", "norm_ffn", "gla_b_gate", "gla_norm", "fox_b_f", "fox_q_norm", "fox_k_norm", "ffn_conv_b",
             "norm_final", "gla_w_gate", "ffn_conv_w", "loss")
    packed = _pack([small_parts[nm] for nm in order])
    dmod = jnp.stack([jnp.concatenate([lg[i][nm] for nm in ("sh_m", "sc_m", "g_m", "sh_f", "sc_f", "g_f")], axis=1)
                      for i in range(depth)])
    (packed_all, dmod_all), tok_small = _exchange([packed, dmod], "gather_small_grads", scatter=False)
    dw_main, da_tail, h1_gla = lg[0]["gla_w_in_unsent"]
    dw_in_t = _tail_rows(da_tail + tok_small.astype(BF16), h1_gla, dw_main, rank, "gla_in_tail_dw")
    tok_last = send("gla_in", [dw_in_t.reshape(N_DEV, -1, d)])
    packed_all = packed_all + tok_last
    summed = _unpack(_sum8(packed_all, "sum_small_grads"), [small_parts[nm].shape for nm in order])
    small_g = dict(zip(order, summed))
    loss = small_g["loss"][0, 0]
    dmod_all = dmod_all[:, :, 0, :]

    grads = {}
    cond_t = _pad_cols(jnp.transpose(cond_all)).astype(BF16)
    dmod_cols = lax.dynamic_slice(dmod_all, (0, 0, me * mod_cols), (N_DEV, depth, mod_cols))
    g_w_mod = []
    for i in range(depth):
        rhs = jnp.pad(dmod_cols[:, i], ((0, LANE - N_DEV), (0, 0)))
        g_w_mod.append(_matmul(cond_t, rhs, name=f"mod_dw_{i}", tn=768))
    grads["w_mod"] = jnp.stack(g_w_mod)
    small_g["b_mod"] = _sum8(dmod_all.reshape(N_DEV, 1, -1), "sum_b_mod").reshape(depth, -1)

    received = {}

    def arrive(key, after):
        sums, lands = _split_wait(sent[key], _chip_plan, after, f"scatter_{key}_chip_wait")
        received[key] = [lax.dynamic_update_slice(land, lax.dynamic_index_in_dim(q, chip, 0, keepdims=True),
                                                  (chip,) + (0,) * (q.ndim - 1))
                         for land, q in zip(lands, sums)]

    for key in ("ffn1", "fox", "ffn0", "gla_out"):
        arrive(key, packed_all)

    out_g, out_d, out_m, out_v = {}, {}, {}, {}

    def update(nm, g_arr, transposed=False):
        swap = (lambda t: jnp.transpose(t, (0, 2, 1))) if transposed else (lambda t: t)
        res = _adamw(swap(w[nm]), g_arr, swap(mom_m[nm]), swap(mom_v[nm]), f"adamw_{nm}")
        out_g[nm], out_d[nm], out_m[nm], out_v[nm] = (swap(t) for t in res)

    update("ffn_w_up", [received[f"ffn{i}"][0] for i in range(depth)])
    tok_flush = flush(out_g["ffn_w_up"])
    update("gla_w_out", [received["gla_out"][0]])
    update("fox_w_in", [received["fox"][0]], transposed=True)
    update("fox_w_out", [received["fox"][1]])
    update("ffn_w_down", [received[f"ffn{i}"][1] for i in range(depth)])
    update("w_mod", grads["w_mod"])

    gate_cols = gla_w_gate.shape[2]
    conv_cols = ffn_conv_w.shape[2]
    local_small = dict(small_g)
    local_small["gla_w_gate"] = lax.dynamic_slice_in_dim(small_g["gla_w_gate"], me * gate_cols, gate_cols, axis=2)
    local_small["ffn_conv_w"] = lax.dynamic_slice_in_dim(small_g["ffn_conv_w"], me * conv_cols, conv_cols, axis=2)
    names = SMALL + SMALL_SHARDED
    shapes = [w[nm].shape for nm in names]
    res = _adamw(_pack([w[nm] for nm in names])[None], (_pack([local_small[nm] for nm in names]) + tok_flush)[None],
                 _pack([mom_m[nm] for nm in names])[None], _pack([mom_v[nm] for nm in names])[None], "adamw_small")
    for tgt, flat in zip((out_g, out_d, out_m, out_v), res):
        for nm, arr in zip(names, _unpack(flat[0], shapes)):
            tgt[nm] = arr

    arrive("gla_in", [out_d[nm] for nm in ("gla_w_out", "fox_w_in", "fox_w_out", "ffn_w_up", "ffn_w_down", "w_mod")])
    update("gla_w_in", [received["gla_in"][0]], transposed=True)

    return (loss, grad_x, *[out_g[n] for n in WEIGHTS], *[out_d[n] for n in WEIGHTS],
            *[out_m[n] for n in WEIGHTS], *[out_v[n] for n in WEIGHTS])
```

```python
import jax
import jax.numpy as jnp
from jax import lax
from jax.experimental import pallas as pl
from jax.experimental.pallas import tpu as pltpu

F32, BF16 = jnp.float32, jnp.bfloat16
N_DEV = 8
GLA_HEADS = 4
GLA_TAU = 16.0
GLA_CHUNK = 64
NORM_EPS = 1e-6
ADAM_LR, ADAM_B1, ADAM_B2, ADAM_EPS, ADAM_WD, ADAM_STEP = 0.001, 0.9, 0.999, 1e-08, 0.01, 10
LANE = 128
VMEM_LIMIT = 56 * 1024 * 1024
NEG = -1e30


def _pcall(body, **kw):
    return pl.pallas_call(body, **kw)


def _params(n_axes):
    return pltpu.CompilerParams(dimension_semantics=("arbitrary",) * n_axes, vmem_limit_bytes=VMEM_LIMIT)


def _tile(dim, pref):
    if dim <= pref:
        return dim
    t = pref
    while dim % t:
        t -= LANE
    assert t > 0, (dim, pref)
    return t


def _dot(a, b, ta=False, tb=False):
    dims = (((0,) if ta else (1,), (1,) if tb else (0,)), ((), ()))
    return lax.dot_general(a.astype(BF16), b.astype(BF16), dims, preferred_element_type=F32)


def _split3(x):
    hi = x.astype(BF16)
    r1 = x - hi.astype(F32)
    mid = r1.astype(BF16)
    lo = (r1 - mid.astype(F32)).astype(BF16)
    return hi, mid, lo


def _tri_matmul(tri, x):
    hi, mid, lo = _split3(x)
    return _dot(tri, hi) + _dot(tri, mid) + _dot(tri, lo)


def _tri(n, upper=False):
    r = lax.broadcasted_iota(jnp.int32, (n, n), 0)
    c = lax.broadcasted_iota(jnp.int32, (n, n), 1)
    return jnp.where((r <= c) if upper else (r >= c), 1.0, 0.0).astype(BF16)


def _log_sigmoid(x):
    return jnp.minimum(x, 0.0) - jnp.log(1.0 + jnp.exp(-jnp.abs(x)))


def _sigmoid(x):
    return 1.0 / (1.0 + jnp.exp(-x))


def _silu(x):
    return x * _sigmoid(x)


def _dsilu(x):
    s = _sigmoid(x)
    return s * (1.0 + x * (1.0 - s))


def _matmul(a, b, *, name, ta=False, tb=False, out_dtypes=(F32,), tm=1024, tn=1024, tk=2048,
            epilogue=None, extras=(), a_halves=False, b_halves=False, b_shards=False, out_shards=False,
            b_rows=None, out_rows=None):
    if a_halves:
        assert not ta
        m, k = a.shape[1], 2 * a.shape[2]
    else:
        m, k = (a.shape[1], a.shape[0]) if ta else a.shape
    if b_halves:
        assert not tb and b.shape[1] == k
        n = 2 * b.shape[2]
    elif b_shards:
        n = b.shape[1] if tb else N_DEV * b.shape[2]
        assert (N_DEV * b.shape[2] if tb else b.shape[1]) == k, (a.shape, b.shape, ta, tb)
    else:
        rows = b.shape[0] if b_rows is None else b_rows
        n = rows if tb else b.shape[1]
        assert (b.shape[1] if tb else rows) == k, (a.shape, b.shape, ta, tb)
    n_unit = n // N_DEV if (out_shards or (b_shards and not tb)) else (n // 2 if b_halves else n)
    k_unit = k // N_DEV if (b_shards and tb) else (k // 2 if a_halves else k)
    tm, tn, tk = _tile(m, tm), _tile(n_unit, tn), _tile(k_unit, tk)
    nk = k // tk
    if a_halves:
        a_spec = pl.BlockSpec((None, tm, tk), lambda i, j, kk: (kk // (nk // 2), i, kk % (nk // 2)))
    elif ta:
        a_spec = pl.BlockSpec((tk, tm), lambda i, j, kk: (kk, i))
    else:
        a_spec = pl.BlockSpec((tm, tk), lambda i, j, kk: (i, kk))
    n_per, k_per = n // tn // N_DEV, nk // N_DEV
    if b_halves:
        b_spec = pl.BlockSpec((None, tk, tn), lambda i, j, kk: (j // (n // tn // 2), kk, j % (n // tn // 2)))
    elif b_shards and tb:
        b_spec = pl.BlockSpec((None, tn, tk), lambda i, j, kk: (kk // k_per, j, kk % k_per))
    elif b_shards:
        b_spec = pl.BlockSpec((None, tk, tn), lambda i, j, kk: (j // n_per, kk, j % n_per))
    elif tb:
        b_spec = pl.BlockSpec((tn, tk), lambda i, j, kk: (j, kk))
    else:
        b_spec = pl.BlockSpec((tk, tn), lambda i, j, kk: (kk, j))
    ex_specs = []
    for kind, arr in extras:
        if kind == "mn":
            assert arr.shape == (m, n), (arr.shape, m, n)
            ex_specs.append(pl.BlockSpec((tm, tn), lambda i, j, kk: (i, j)))
        else:
            assert arr.shape == (1, n), (arr.shape, n)
            ex_specs.append(pl.BlockSpec((1, tn), lambda i, j, kk: (0, j)))
    n_ex, n_out = len(extras), len(out_dtypes)

    def body(a_ref, b_ref, *rest):
        ex, outs, acc = rest[:n_ex], rest[n_ex:n_ex + n_out], rest[-1]
        kk = pl.program_id(2)

        @pl.when(kk == 0)
        def _():
            acc[...] = jnp.zeros_like(acc)

        acc[...] += _dot(a_ref[...], b_ref[...], ta, tb)

        @pl.when(kk == nk - 1)
        def _():
            if epilogue is None:
                vals = (acc[...],)
            else:
                vals = epilogue(acc[...], *[e[...] for e in ex])
            for o, v in zip(outs, vals):
                o[...] = v.astype(o.dtype)

    if out_shards:
        out_spec = pl.BlockSpec((None, tm, tn), lambda i, j, kk: (j // n_per, i, j % n_per))
        out_dims = (N_DEV, m, n // N_DEV)
    else:
        out_spec = pl.BlockSpec((tm, tn), lambda i, j, kk: (i, j))
        out_dims = (m if out_rows is None else out_rows, n)
    res = _pcall(
        body, name=name, grid=(m // tm, n // tn, nk),
        in_specs=[a_spec, b_spec] + ex_specs,
        out_specs=[out_spec] * n_out,
        out_shape=[jax.ShapeDtypeStruct(out_dims, d) for d in out_dtypes],
        scratch_shapes=[pltpu.VMEM((tm, tn), F32)],
        compiler_params=_params(3),
    )(a, b, *[arr for _, arr in extras])
    return res[0] if n_out == 1 else res


def _tail_rows(a, b, into, rows, name, tn=1024):
    k, n = b.shape
    m_total = into.shape[0]
    tn = _tile(n, tn)

    def body(a_ref, b_ref, into_ref, out_ref):
        out_ref[...] = _dot(a_ref[...], b_ref[...], ta=True)[:rows].astype(out_ref.dtype)

    return _pcall(
        body, name=name, grid=(n // tn,),
        in_specs=[pl.BlockSpec((k, a.shape[1]), lambda j: (0, 0)), pl.BlockSpec((k, tn), lambda j: (0, j)),
                  pl.BlockSpec(memory_space=pl.ANY)],
        out_specs=pl.BlockSpec((rows, tn), lambda j: (m_total // rows - 1, j)),
        out_shape=jax.ShapeDtypeStruct(into.shape, into.dtype),
        input_output_aliases={2: 0}, compiler_params=_params(1),
    )(a, b, into)


def _rowwise(fn, ins, outs, *, name, tr=128):
    rows = next(e[1].shape[0] for e in ins if e[0] != "full")
    tr = _tile(rows, tr)
    in_specs = []
    for entry in ins:
        kind, arr = entry[0], entry[1]
        assert kind == "full" or (arr.shape[0] == rows and arr.ndim == 2)
        if kind == "row":
            in_specs.append(pl.BlockSpec((tr, arr.shape[1]), lambda i: (i, 0)))
        elif kind == "cols":
            in_specs.append(pl.BlockSpec((tr, entry[3]), lambda i, cb=entry[2]: (i, cb)))
        else:
            in_specs.append(pl.BlockSpec(arr.shape, lambda i, nd=arr.ndim: (0,) * nd))
    out_specs, out_shape = [], []
    for kind, w, dt in outs:
        if kind == "row":
            out_specs.append(pl.BlockSpec((tr, w), lambda i: (i, 0)))
            out_shape.append(jax.ShapeDtypeStruct((rows, w), dt))
        else:
            out_specs.append(pl.BlockSpec((1, w), lambda i: (0, 0)))
            out_shape.append(jax.ShapeDtypeStruct((1, w), dt))
    n_in = len(ins)

    def body(*refs):
        i = pl.program_id(0)
        vals = fn(*[r[...] for r in refs[:n_in]])
        for (kind, _, _), o, v in zip(outs, refs[n_in:], vals):
            if kind == "row":
                o[...] = v.astype(o.dtype)
            else:
                @pl.when(i == 0)
                def _(o=o):
                    o[...] = jnp.zeros_like(o)

                o[...] += v.astype(o.dtype)

    return _pcall(body, name=name, grid=(rows // tr,), in_specs=in_specs, out_specs=out_specs,
                  out_shape=out_shape, compiler_params=_params(1))(*[e[1] for e in ins])


def _colsum(x):
    return jnp.sum(x, axis=0, keepdims=True)


def _norm_stats(x):
    rstd = lax.rsqrt(jnp.mean(x * x, axis=-1, keepdims=True) + NORM_EPS)
    return x * rstd, rstd


def _norm_bwd(dxhat, xhat, rstd):
    return rstd * (dxhat - xhat * jnp.mean(dxhat * xhat, axis=-1, keepdims=True))


def _adaln_fwd(x, gain, sc, sh, name):
    def fn(x, gain, sc, sh):
        xhat, _ = _norm_stats(x)
        return ((xhat * gain) * (1.0 + sc) + sh,)

    return _rowwise(fn, [("row", x), ("full", gain), ("full", sc), ("full", sh)],
                    [("row", x.shape[1], BF16)], name=name)[0]


def _adaln_bwd(x, dh, dres, gain, sc, name):
    d = x.shape[1]

    def fn(x, dh, dres, gain, sc):
        xhat, rstd = _norm_stats(x)
        dxhat = dh * (gain * (1.0 + sc))
        dx = dres + _norm_bwd(dxhat, xhat, rstd)
        return dx, _colsum(dh), _colsum(dh * (xhat * gain)), _colsum(dh * xhat * (1.0 + sc))

    return _rowwise(fn, [("row", x), ("row", dh), ("row", dres), ("full", gain), ("full", sc)],
                    [("row", d, F32), ("acc", d, F32), ("acc", d, F32), ("acc", d, F32)], name=name)


def _residual_bwd(dx, y, g, name):
    d = dx.shape[1]

    def fn(dx, y, g):
        return dx * (1.0 + g), _colsum(dx * y)

    return _rowwise(fn, [("row", dx), ("row", y), ("full", g)], [("row", d, BF16), ("acc", d, F32)], name=name)


def _final_loss(x, target, gain, name):
    d = x.shape[1]

    def fn(x, t, gain):
        xhat, rstd = _norm_stats(x)
        err = xhat * gain - t
        dy = err * (1.0 / d)
        loss = 0.5 * jnp.sum(jnp.mean(err * err, axis=-1, keepdims=True), axis=0, keepdims=True)
        dx = _norm_bwd(dy * gain, xhat, rstd)
        return dx, _colsum(dy * xhat), jnp.broadcast_to(loss, (1, LANE))

    return _rowwise(fn, [("row", x), ("row", target), ("full", gain)],
                    [("row", d, F32), ("acc", d, F32), ("acc", LANE, F32)], name=name)


def _gla_gates(q, k, a, wg, bg, scale, c):
    ga = _dot(a, wg) + bg
    la = _log_sigmoid(ga) * (1.0 / GLA_TAU)
    b = _tri_matmul(_tri(c), la)
    bl = _colsum(la)
    eb, enb, eend = jnp.exp(b), jnp.exp(-b), jnp.exp(bl - b)
    q = q * scale
    return dict(ga=ga, eb=eb, enb=enb, eend=eend, dec=jnp.exp(bl), q_dec=q * eb, k_inv=k * enb, k_end=k * eend)


def _causal(c):
    return lax.broadcasted_iota(jnp.int32, (c, c), 0) >= lax.broadcasted_iota(jnp.int32, (c, c), 1)


def _gla_specs(heads, c, dk, dv, chunk):
    return [
        pl.BlockSpec((c, heads * dk), lambda n: (chunk(n), 0)),
        pl.BlockSpec((c, heads * dk), lambda n: (chunk(n), 1)),
        pl.BlockSpec((c, heads * dv), lambda n: (chunk(n), 1)),
        pl.BlockSpec((c, LANE), lambda n: (chunk(n), 0)),
        pl.BlockSpec((LANE, heads * dk), lambda n: (0, 0)),
        pl.BlockSpec((1, heads * dk), lambda n: (0, 0)),
    ]


def _gla_fwd(proj, a_tail, wg_p, bg, name):
    s = proj.shape[0]
    heads, c = GLA_HEADS, GLA_CHUNK
    dk = wg_p.shape[1] // heads
    dv = 2 * dk
    n_chunks = s // c
    scale = dk ** -0.5

    def body(q_ref, k_ref, v_ref, a_ref, wg_ref, bg_ref, o_ref, st_ref, state):
        @pl.when(pl.program_id(0) == 0)
        def _():
            state[...] = jnp.zeros_like(state)

        a = a_ref[...]
        for h in range(heads):
            sk, sv = slice(h * dk, (h + 1) * dk), slice(h * dv, (h + 1) * dv)
            g = _gla_gates(q_ref[:, sk], k_ref[:, sk], a, wg_ref[:, sk], bg_ref[:, sk], scale, c)
            v = v_ref[:, sv]
            st = state[h]
            attn = jnp.where(_causal(c), _dot(g["q_dec"], g["k_inv"], tb=True), 0.0)
            o_ref[:, sv] = _dot(attn, v) + _dot(g["q_dec"], st, tb=True)
            st_ref[h] = st.astype(st_ref.dtype)
            state[h] = g["dec"] * st + _dot(v, g["k_end"], ta=True)

    return _pcall(
        body, name=name, grid=(n_chunks,),
        in_specs=_gla_specs(heads, c, dk, dv, lambda n: n),
        out_specs=[pl.BlockSpec((c, heads * dv), lambda n: (n, 0)),
                   pl.BlockSpec((heads, None, dv, dk), lambda n: (0, n, 0, 0))],
        out_shape=[jax.ShapeDtypeStruct((s, heads * dv), F32),
                   jax.ShapeDtypeStruct((heads, n_chunks, dv, dk), BF16)],
        scratch_shapes=[pltpu.VMEM((heads, dv, dk), F32)],
        compiler_params=_params(1),
    )(proj, proj, proj, a_tail, wg_p, bg)


def _gla_bwd(proj, a_tail, wg_p, bg, states, d_o, name):
    s = proj.shape[0]
    heads, c = GLA_HEADS, GLA_CHUNK
    dk = wg_p.shape[1] // heads
    dv = 2 * dk
    n_chunks = s // c
    scale = dk ** -0.5

    def body(q_ref, k_ref, v_ref, a_ref, wg_ref, bg_ref, st_ref, do_ref, dq_ref, dk_ref, dv_ref, dga_ref, dstate):
        @pl.when(pl.program_id(0) == 0)
        def _():
            dstate[...] = jnp.zeros_like(dstate)

        a = a_ref[...]
        mask = _causal(c)
        for h in range(heads):
            sk, sv = slice(h * dk, (h + 1) * dk), slice(h * dv, (h + 1) * dv)
            g = _gla_gates(q_ref[:, sk], k_ref[:, sk], a, wg_ref[:, sk], bg_ref[:, sk], scale, c)
            v, st, dst, d_out = v_ref[:, sv], st_ref[h], dstate[h], do_ref[:, sv]
            q_dec, k_inv, k_end = g["q_dec"], g["k_inv"], g["k_end"]
            attn = jnp.where(mask, _dot(q_dec, k_inv, tb=True), 0.0)
            d_attn = jnp.where(mask, _dot(d_out, v, tb=True), 0.0)
            d_qdec = _dot(d_attn, k_inv) + _dot(d_out, st)
            d_kinv = _dot(d_attn, q_dec, ta=True)
            d_kend = _dot(v, dst)
            dv_ref[:, sv] = (_dot(attn, d_out, ta=True) + _dot(k_end, dst, tb=True)).astype(dv_ref.dtype)
            d_dec = jnp.sum(dst * st.astype(F32), axis=0, keepdims=True)
            dstate[h] = g["dec"] * dst + _dot(d_out, q_dec, ta=True)

            dq_ref[:, sk] = (d_qdec * (scale * g["eb"])).astype(dq_ref.dtype)
            dk_ref[:, sk] = (d_kinv * g["enb"] + d_kend * g["eend"]).astype(dk_ref.dtype)
            kk = d_kend * k_end
            db = d_qdec * q_dec - d_kinv * k_inv - kk
            dbl = jnp.sum(kk, axis=0, keepdims=True) + d_dec * g["dec"]
            last = lax.broadcasted_iota(jnp.int32, db.shape, 0) == c - 1
            db = db + jnp.where(last, dbl, 0.0)
            dla = _tri_matmul(_tri(c, upper=True), db)
            dga_ref[:, sk] = dla * (1.0 / GLA_TAU) * _sigmoid(-g["ga"])

    chunk = lambda n: n_chunks - 1 - n
    rev = lambda n: (chunk(n), 0)
    return _pcall(
        body, name=name, grid=(n_chunks,),
        in_specs=_gla_specs(heads, c, dk, dv, chunk) + [
            pl.BlockSpec((heads, None, dv, dk), lambda n: (0, chunk(n), 0, 0)),
            pl.BlockSpec((c, heads * dv), rev)],
        out_specs=[pl.BlockSpec((c, heads * dk), rev), pl.BlockSpec((c, heads * dk), rev),
                   pl.BlockSpec((c, heads * dv), rev), pl.BlockSpec((c, heads * dk), rev)],
        out_shape=[jax.ShapeDtypeStruct((s, heads * dk), BF16), jax.ShapeDtypeStruct((s, heads * dk), BF16),
                   jax.ShapeDtypeStruct((s, heads * dv), BF16), jax.ShapeDtypeStruct((s, heads * dk), F32)],
        scratch_shapes=[pltpu.VMEM((heads, dv, dk), F32)],
        compiler_params=_params(1),
    )(proj, proj, proj, a_tail, wg_p, bg, states, d_o)


def _gla_post_fwd(o, r, gn, name):
    dvt = o.shape[1]
    dv = dvt // GLA_HEADS

    def fn(o, r, gn):
        outs = []
        for h in range(GLA_HEADS):
            sl = slice(h * dv, (h + 1) * dv)
            ohat, _ = _norm_stats(o[:, sl])
            outs.append((ohat * gn[:, sl]) * _silu(r[:, sl]))
        return (jnp.concatenate(outs, axis=1),)

    return _rowwise(fn, [("row", o), r, ("full", gn)], [("row", dvt, BF16)], name=name)[0]


def _gla_post_bwd(o, r, gn, dog, name):
    dvt = o.shape[1]
    dv = dvt // GLA_HEADS

    def fn(o, r, gn, dog):
        d_o, d_r, d_g = [], [], []
        for h in range(GLA_HEADS):
            sl = slice(h * dv, (h + 1) * dv)
            ohat, rstd = _norm_stats(o[:, sl])
            g, rr, dd = gn[:, sl], r[:, sl], dog[:, sl]
            d_r.append(dd * (ohat * g) * _dsilu(rr))
            don = dd * _silu(rr)
            d_g.append(_colsum(don * ohat))
            d_o.append(_norm_bwd(don * g, ohat, rstd))
        return jnp.concatenate(d_o, axis=1), jnp.concatenate(d_r, axis=1), jnp.concatenate(d_g, axis=1)

    return _rowwise(fn, [("row", o), r, ("full", gn), ("row", dog)],
                    [("row", dvt, F32), ("row", dvt, BF16), ("acc", dvt, F32)], name=name)


def _fox_prep(q, k, v, qg, kg, d, hd, name):
    heads = d // hd
    scale = hd ** -0.5

    def fn(q, k, v, qg, kg):
        qs, ks = [], []
        for h in range(heads):
            sl = slice(h * hd, (h + 1) * hd)
            qs.append(_norm_stats(q[:, sl])[0] * qg * scale)
            ks.append(_norm_stats(k[:, sl])[0] * kg)
        return jnp.concatenate(qs, axis=1), jnp.concatenate(ks, axis=1), v

    return _rowwise(fn, [q, k, v, ("full", qg), ("full", kg)],
                    [("row", d, BF16)] * 3, name=name)


def _fox_prep_bwd(q, k, dqn, dkn, qg, kg, hd, name):
    d = dqn.shape[1]
    heads = d // hd
    scale = hd ** -0.5

    def fn(q, k, dqn, dkn, qg, kg):
        dq, dk, gq, gk = [], [], [], []
        for h in range(heads):
            sl = slice(h * hd, (h + 1) * hd)
            for x, dxn, g, s, dl, gl in ((q, dqn, qg, scale, dq, gq), (k, dkn, kg, 1.0, dk, gk)):
                xhat, rstd = _norm_stats(x[:, sl])
                dn = dxn[:, sl] * s
                gl.append(_colsum(dn * xhat))
                dl.append(_norm_bwd(dn * g, xhat, rstd))
        cat = lambda t: jnp.concatenate(t, axis=1)
        return cat(dq), cat(dk), cat(gq), cat(gk)

    return _rowwise(fn, [q, k, ("row", dqn), ("row", dkn), ("full", qg), ("full", kg)],
                    [("row", d, BF16), ("row", d, BF16), ("acc", d, F32), ("acc", d, F32)], name=name)


def _fox_cum(fl, bf_p, name, tb=256):
    s = fl.shape[0]
    tb = _tile(s, tb)

    def body(fl_ref, bf_ref, cum_ref, carry):
        @pl.when(pl.program_id(0) == 0)
        def _():
            carry[...] = jnp.zeros_like(carry)

        lf = _log_sigmoid(fl_ref[...] + bf_ref[...])
        cum_ref[...] = _tri_matmul(_tri(tb), lf) + carry[...]
        carry[...] += _colsum(lf)

    return _pcall(
        body, name=name, grid=(s // tb,),
        in_specs=[pl.BlockSpec((tb, LANE), lambda i: (i, 0)), pl.BlockSpec((1, LANE), lambda i: (0, 0))],
        out_specs=pl.BlockSpec((tb, LANE), lambda i: (i, 0)),
        out_shape=jax.ShapeDtypeStruct((s, LANE), F32),
        scratch_shapes=[pltpu.VMEM((1, LANE), F32)],
        compiler_params=_params(1),
    )(fl, bf_p)


def _fox_cum_bwd(dcum, fl, bf_p, name, tb=256):
    s = fl.shape[0]
    tb = _tile(s, tb)
    nb = s // tb

    def body(dc_ref, fl_ref, bf_ref, dfl_ref, dbf_ref, carry):
        @pl.when(pl.program_id(0) == 0)
        def _():
            carry[...] = jnp.zeros_like(carry)
            dbf_ref[...] = jnp.zeros_like(dbf_ref)

        dc = dc_ref[...]
        dlf = _tri_matmul(_tri(tb, upper=True), dc) + carry[...]
        carry[...] += _colsum(dc)
        dfl = dlf * _sigmoid(-(fl_ref[...] + bf_ref[...]))
        dfl_ref[...] = dfl
        dbf_ref[...] += _colsum(dfl)

    rev = lambda i: (nb - 1 - i, 0)
    return _pcall(
        body, name=name, grid=(nb,),
        in_specs=[pl.BlockSpec((tb, LANE), rev), pl.BlockSpec((tb, LANE), rev), pl.BlockSpec((1, LANE), lambda i: (0, 0))],
        out_specs=[pl.BlockSpec((tb, LANE), rev), pl.BlockSpec((1, LANE), lambda i: (0, 0))],
        out_shape=[jax.ShapeDtypeStruct((s, LANE), F32), jax.ShapeDtypeStruct((1, LANE), F32)],
        scratch_shapes=[pltpu.VMEM((1, LANE), F32)],
        compiler_params=_params(1),
    )(dcum, fl, bf_p)


def _fox_attn_fwd(qn, kn, vb, cum_col, cum_row, hd, t, name):
    s, d = qn.shape
    heads = d // hd
    nq = s // t

    def body(q_ref, k_ref, v_ref, cc_ref, cr_ref, o_ref, lse_ref):
        qi = pl.program_id(1)
        q = q_ref[...]
        cq = cc_ref[...]
        qpos = qi * t + lax.broadcasted_iota(jnp.int32, (t, 1), 0)

        def step(kj, carry, diagonal=False):
            m, l, acc = carry
            off = pl.multiple_of(kj * t, t)
            ks, vs = k_ref[pl.ds(off, t), :], v_ref[pl.ds(off, t), :]
            sc = _dot(q, ks, tb=True) + cq - cr_ref[kj]
            if diagonal:
                kpos = off + lax.broadcasted_iota(jnp.int32, (1, t), 1)
                sc = jnp.where(kpos <= qpos, sc, NEG)
            m_new = jnp.maximum(m, jnp.max(sc, axis=1, keepdims=True))
            alpha = jnp.exp(m - m_new)
            p = jnp.exp(sc - m_new)
            return m_new, alpha * l + jnp.sum(p, axis=1, keepdims=True), alpha * acc + _dot(p, vs)

        init = (jnp.full((t, 1), NEG, F32), jnp.zeros((t, 1), F32), jnp.zeros((t, hd), F32))
        m, l, acc = step(qi, lax.fori_loop(0, qi, step, init), diagonal=True)
        o_ref[...] = acc / l
        lse_ref[...] = m + jnp.log(l)

    return _pcall(
        body, name=name, grid=(heads, nq),
        in_specs=[pl.BlockSpec((t, hd), lambda h, i: (i, h)),
                  pl.BlockSpec((s, hd), lambda h, i: (0, h)),
                  pl.BlockSpec((s, hd), lambda h, i: (0, h)),
                  pl.BlockSpec((None, t, 1), lambda h, i: (h, i, 0)),
                  pl.BlockSpec((None, nq, 1, t), lambda h, i: (h, 0, 0, 0))],
        out_specs=[pl.BlockSpec((t, hd), lambda h, i: (i, h)), pl.BlockSpec((None, t, 1), lambda h, i: (h, i, 0))],
        out_shape=[jax.ShapeDtypeStruct((s, d), F32), jax.ShapeDtypeStruct((heads, s, 1), F32)],
        compiler_params=_params(2),
    )(qn, kn, vb, cum_col, cum_row)


def _fox_attn_bwd(qn, kn, vb, d_o, o, lse, cum_col, cum_row, hd, t, name):
    s, d = qn.shape
    heads = d // hd
    nq = s // t

    def body(q_ref, k_ref, v_ref, do_ref, o_ref, lse_ref, cc_ref, cr_ref,
             dq_ref, dk_ref, dv_ref, dcq_ref, dck_ref, delta):
        kj = pl.program_id(1)

        @pl.when(kj == 0)
        def _():
            dq_ref[...] = jnp.zeros_like(dq_ref)
            dcq_ref[...] = jnp.zeros_like(dcq_ref)
            delta[...] = jnp.sum(do_ref[...] * o_ref[...], axis=1, keepdims=True)

        ks, vs, cr = k_ref[...], v_ref[...], cr_ref[...]
        kpos = kj * t + lax.broadcasted_iota(jnp.int32, (1, t), 1)

        def step(qi, carry, diagonal=False):
            dk, dv, dck = carry
            rows = pl.ds(pl.multiple_of(qi * t, t), t)
            q, d_out = q_ref[rows, :], do_ref[rows, :]
            sc = _dot(q, ks, tb=True) + cc_ref[rows, :] - cr
            p = jnp.exp(sc - lse_ref[rows, :])
            if diagonal:
                qpos = qi * t + lax.broadcasted_iota(jnp.int32, (t, 1), 0)
                p = jnp.where(kpos <= qpos, p, 0.0)
            ds = p * (_dot(d_out, vs, tb=True) - delta[rows, :])
            dq_ref[rows, :] += _dot(ds, ks)
            dcq_ref[rows, :] += jnp.sum(ds, axis=1, keepdims=True)
            return dk + _dot(ds, q, ta=True), dv + _dot(p, d_out, ta=True), dck + _colsum(ds)

        init = (jnp.zeros((t, hd), F32), jnp.zeros((t, hd), F32), jnp.zeros((1, t), F32))
        dk, dv, dck = lax.fori_loop(kj + 1, nq, step, step(kj, init, diagonal=True))
        dk_ref[...] = dk.astype(dk_ref.dtype)
        dv_ref[...] = dv.astype(dv_ref.dtype)
        dck_ref[...] = dck

    head_rows = lambda h, j: (0, h)
    blk = lambda h, j: (j, h)
    return _pcall(
        body, name=name, grid=(heads, nq),
        in_specs=[pl.BlockSpec((s, hd), head_rows), pl.BlockSpec((t, hd), blk), pl.BlockSpec((t, hd), blk),
                  pl.BlockSpec((s, hd), head_rows), pl.BlockSpec((s, hd), head_rows),
                  pl.BlockSpec((None, s, 1), lambda h, j: (h, 0, 0)),
                  pl.BlockSpec((None, s, 1), lambda h, j: (h, 0, 0)),
                  pl.BlockSpec((None, None, 1, t), lambda h, j: (h, j, 0, 0))],
        out_specs=[pl.BlockSpec((s, hd), head_rows), pl.BlockSpec((t, hd), blk), pl.BlockSpec((t, hd), blk),
                   pl.BlockSpec((None, s, 1), lambda h, j: (h, 0, 0)),
                   pl.BlockSpec((None, None, 1, t), lambda h, j: (h, j, 0, 0))],
        out_shape=[jax.ShapeDtypeStruct((s, d), F32), jax.ShapeDtypeStruct((s, d), BF16),
                   jax.ShapeDtypeStruct((s, d), BF16), jax.ShapeDtypeStruct((heads, s, 1), F32),
                   jax.ShapeDtypeStruct((heads, nq, 1, t), F32)],
        scratch_shapes=[pltpu.VMEM((s, 1), F32)],
        compiler_params=_params(2),
    )(qn, kn, vb, d_o, o, lse, cum_col, cum_row)


def _fox_gate_fwd(o, og, name):
    def fn(o, og):
        return (o * _sigmoid(og),)

    return _rowwise(fn, [("row", o), og], [("row", o.shape[1], BF16)], name=name)[0]


def _fox_gate_bwd(o, og, dact, name):
    def fn(o, og, dact):
        sg = _sigmoid(og)
        return dact * sg, dact * o * sg * (1.0 - sg)

    d = o.shape[1]
    return _rowwise(fn, [("row", o), og, ("row", dact)], [("row", d, F32), ("row", d, BF16)], name=name)


def _shift_down(x, n):
    rows = lax.broadcasted_iota(jnp.int32, x.shape, 0)
    return jnp.where(rows >= n, pltpu.roll(x, n, 0), 0.0)


def _shift_up(x, n):
    rows = lax.broadcasted_iota(jnp.int32, x.shape, 0)
    return jnp.where(rows < x.shape[0] - n, pltpu.roll(x, x.shape[0] - n, 0), 0.0)


def _conv(u, w_ref, b):
    return w_ref[0:1, :] * _shift_down(u, 2) + w_ref[1:2, :] * _shift_down(u, 1) + w_ref[2:3, :] * u + b


def _conv_act_fwd(u, cw, cb, name, tc=256):
    s, two_f = u.shape
    dff = two_f // 2
    tc = _tile(dff, tc)
    nb = dff // tc

    def body(ug_ref, uv_ref, wg_ref, wv_ref, bg_ref, bv_ref, a_ref):
        gate = _conv(ug_ref[...], wg_ref, bg_ref[...])
        val = _conv(uv_ref[...], wv_ref, bv_ref[...])
        a_ref[...] = (_silu(gate) * val).astype(a_ref.dtype)

    lo, hi = (lambda j: (0, j)), (lambda j: (0, j + nb))
    return _pcall(
        body, name=name, grid=(nb,),
        in_specs=[pl.BlockSpec((s, tc), lo), pl.BlockSpec((s, tc), hi), pl.BlockSpec((3, tc), lo),
                  pl.BlockSpec((3, tc), hi), pl.BlockSpec((1, tc), lo), pl.BlockSpec((1, tc), hi)],
        out_specs=pl.BlockSpec((s, tc), lo),
        out_shape=jax.ShapeDtypeStruct((s, dff), BF16),
        compiler_params=_params(1),
    )(u, u, cw, cw, cb, cb)


def _conv_act_bwd(u, cw, cb, da, name, tc=128):
    s, two_f = u.shape
    dff = two_f // 2
    tc = _tile(dff, tc)
    nb = dff // tc

    def body(ug_ref, uv_ref, wg_ref, wv_ref, bg_ref, bv_ref, da_ref, du_ref, dw_ref, db_ref):
        ug, uv, da = ug_ref[...], uv_ref[...], da_ref[...]
        gate = _conv(ug, wg_ref, bg_ref[...])
        val = _conv(uv, wv_ref, bv_ref[...])
        sg = _sigmoid(gate)
        d_val = da * (gate * sg)
        d_gate = da * val * (sg * (1.0 + gate * (1.0 - sg)))
        for half, (dc, uu, w_ref) in enumerate(((d_gate, ug, wg_ref), (d_val, uv, wv_ref))):
            du = w_ref[0:1, :] * _shift_up(dc, 2) + w_ref[1:2, :] * _shift_up(dc, 1) + w_ref[2:3, :] * dc
            du_ref[half] = du.astype(du_ref.dtype)
            dw_ref[half, 0:1, :] = _colsum(dc * _shift_down(uu, 2))
            dw_ref[half, 1:2, :] = _colsum(dc * _shift_down(uu, 1))
            dw_ref[half, 2:3, :] = _colsum(dc * uu)
            db_ref[half] = _colsum(dc)

    lo, hi = (lambda j: (0, j)), (lambda j: (0, j + nb))
    both = lambda j: (0, 0, j)
    return _pcall(
        body, name=name, grid=(nb,),
        in_specs=[pl.BlockSpec((s, tc), lo), pl.BlockSpec((s, tc), hi), pl.BlockSpec((3, tc), lo),
                  pl.BlockSpec((3, tc), hi), pl.BlockSpec((1, tc), lo), pl.BlockSpec((1, tc), hi),
                  pl.BlockSpec((s, tc), lo)],
        out_specs=[pl.BlockSpec((2, s, tc), both), pl.BlockSpec((2, 3, tc), both), pl.BlockSpec((2, 1, tc), both)],
        out_shape=[jax.ShapeDtypeStruct((2, s, dff), BF16), jax.ShapeDtypeStruct((2, 3, dff), F32),
                   jax.ShapeDtypeStruct((2, 1, dff), F32)],
        compiler_params=_params(1),
    )(u, u, cw, cw, cb, cb, da)


def _adamw_math(w, g, m, v):
    m = ADAM_B1 * m + (1.0 - ADAM_B1) * g
    v = ADAM_B2 * v + (1.0 - ADAM_B2) * (g * g)
    m_hat = m / (1.0 - ADAM_B1 ** ADAM_STEP)
    v_hat = v / (1.0 - ADAM_B2 ** ADAM_STEP)
    delta = -ADAM_LR * (m_hat / (jnp.sqrt(v_hat) + ADAM_EPS) + ADAM_WD * w)
    return delta, m, v


def _adamw(w, g, m, v, name, tr=128):
    layers, r, c = w.shape
    pieces = isinstance(g, (list, tuple))
    tc = c
    if r % 8:
        tr, tc = r, _tile(c, max(LANE, 512 * 1024 // r // LANE * LANE))
    elif r <= tr:
        tr = r
    while r % tr:
        tr -= 8
    nr, nc = r // tr, c // tc
    g_list = list(g) if pieces else [g]
    n_pieces = g_list[0].shape[0] if pieces else 0

    def body(w_ref, *rest):
        g_refs, (m_ref, v_ref, go_ref, d_ref, mo_ref, vo_ref) = rest[:len(g_list)], rest[len(g_list):]

        def update(grad):
            delta, m_new, v_new = _adamw_math(w_ref[...], grad, m_ref[...], v_ref[...])
            go_ref[...], d_ref[...], mo_ref[...], vo_ref[...] = grad, delta, m_new, v_new

        if not pieces:
            update(g_refs[0][...])
            return
        for layer, g_ref in enumerate(g_refs):
            @pl.when(pl.program_id(0) == layer)
            def _(g_ref=g_ref):
                grad = g_ref[0].astype(F32)
                for i in range(1, n_pieces):
                    grad = grad + g_ref[i].astype(F32)
                update(grad)

    spec = pl.BlockSpec((None, tr, tc), lambda l, i, j: (l, i, j))
    if pieces:
        def walk(k):
            def index(l, i, j):
                here = l == k
                return (0, jnp.where(here, i, jnp.where(l < k, 0, nr - 1)), jnp.where(here, j, jnp.where(l < k, 0, nc - 1)))
            return index

        g_specs = [pl.BlockSpec((n_pieces, tr, tc), walk(k)) for k in range(layers)]
    else:
        g_specs = [spec]
    return _pcall(
        body, name=name, grid=(layers, nr, nc), in_specs=[spec] + g_specs + [spec, spec], out_specs=[spec] * 4,
        out_shape=[jax.ShapeDtypeStruct((layers, r, c), F32)] * 4, compiler_params=_params(3),
    )(w, *g_list, m, v)


def _pair_sum(pieces, partner, core, name, tr=512):
    _, r, c = pieces.shape
    tc = c
    if r % 8:
        tr, tc = r, _tile(c, max(LANE, 1024 * 1024 // r // LANE * LANE))
    elif r <= tr:
        tr = r
    while r % tr:
        tr -= 8

    def body(core_ref, mine_ref, partner_ref, out_ref):
        out_ref[...] = (mine_ref[...].astype(F32) + partner_ref[...].astype(F32)).astype(out_ref.dtype)

    return _pcall(
        body, name=name,
        grid_spec=pltpu.PrefetchScalarGridSpec(
            num_scalar_prefetch=1, grid=(4, r // tr, c // tc),
            in_specs=[pl.BlockSpec((None, tr, tc), lambda q, i, j, core_ref: (2 * q + core_ref[0], i, j)),
                      pl.BlockSpec((None, tr, tc), lambda q, i, j, core_ref: (q, i, j))],
            out_specs=pl.BlockSpec((None, tr, tc), lambda q, i, j, core_ref: (q, i, j))),
        out_shape=jax.ShapeDtypeStruct((4, r, c), pieces.dtype), compiler_params=_params(3),
    )(core, pieces, partner)


def _sum8(x, name):
    p = x.shape[2]
    tp = _tile(p, 16 * 1024)

    def body(x_ref, o_ref):
        acc = x_ref[0]
        for i in range(1, N_DEV):
            acc = acc + x_ref[i]
        o_ref[...] = acc

    return _pcall(
        body, name=name, grid=(p // tp,), in_specs=[pl.BlockSpec((N_DEV, 1, tp), lambda i: (0, 0, i))],
        out_specs=pl.BlockSpec((1, tp), lambda i: (0, i)), out_shape=jax.ShapeDtypeStruct((1, p), x.dtype),
        compiler_params=_params(1),
    )(x)


def _exchange(arrays, name, scatter):
    n = len(arrays)
    hbm = pl.BlockSpec(memory_space=pl.ANY)

    def body(*refs):
        ins, outs, token = refs[:n], refs[n:2 * n], refs[2 * n]
        send_sems, recv_sems, local_sems = refs[2 * n + 1:]
        token[...] = jnp.zeros_like(token)
        x, y, c = lax.axis_index("x"), lax.axis_index("y"), lax.axis_index("c")
        me = 4 * x + 2 * y + c
        copies = []
        for a in range(n):
            src_mine = ins[a].at[me] if scatter else ins[a]
            local = pltpu.make_async_copy(src_mine, outs[a].at[me], local_sems.at[a])
            local.start()
            copies.append(local)
            for k in range(1, N_DEV):
                px = 1 - x if k & 4 else x
                py = 1 - y if k & 2 else y
                pc = 1 - c if k & 1 else c
                src = ins[a].at[4 * px + 2 * py + pc] if scatter else ins[a]
                cp = pltpu.make_async_remote_copy(
                    src_ref=src, dst_ref=outs[a].at[me],
                    send_sem=send_sems.at[a * (N_DEV - 1) + k - 1], recv_sem=recv_sems.at[a * (N_DEV - 1) + k - 1],
                    device_id=(px, py, pc), device_id_type=pl.DeviceIdType.MESH)
                cp.start()
                copies.append(cp)
        for cp in copies:
            cp.wait()

    out_shape = [jax.ShapeDtypeStruct(a.shape if scatter else (N_DEV,) + a.shape, a.dtype) for a in arrays]
    res = _pcall(
        body, name=name, in_specs=[hbm] * n, out_specs=[hbm] * n + [pl.BlockSpec(memory_space=pltpu.VMEM)],
        out_shape=out_shape + [jax.ShapeDtypeStruct((8, LANE), F32)],
        scratch_shapes=[pltpu.SemaphoreType.DMA((n * (N_DEV - 1),)), pltpu.SemaphoreType.DMA((n * (N_DEV - 1),)),
                        pltpu.SemaphoreType.DMA((n,))],
        compiler_params=pltpu.CompilerParams(has_side_effects=True),
    )(*arrays)
    return res[:n], res[n][0, 0]


_HBM = pl.BlockSpec(memory_space=pltpu.HBM)
_SEM = pl.BlockSpec(memory_space=pltpu.SEMAPHORE)
_DATAFLOW = pltpu.SideEffectType.DATAFLOW_SIDE_EFFECTING


def _peer(k, x, y, c):
    return (1 - x if k & 4 else x, 1 - y if k & 2 else y, 1 - c if k & 1 else c)


def _pair_plan(x, y, c):
    return [(2 * q + (1 - c), q, (x, y, 1 - c)) for q in range(4)]


def _chip_plan(x, y, c):
    out = []
    for k in _ICI_PEERS:
        px, py, pc = _peer(k, x, y, c)
        out.append((2 * px + py, 2 * x + y, (px, py, pc)))
    return out


def _all_plan(x, y, c):
    return [(0, 4 * x + 2 * y + c, _peer(k, x, y, c)) for k in range(1, N_DEV)]


def _split_start(arrays, plan, name, land_blocks=4):
    n = len(arrays)
    lands = [lax.empty((land_blocks,) + a.shape[1:], a.dtype) for a in arrays]
    n_copies = len(plan(0, 0, 0))

    def body(*refs):
        srcs, dsts = refs[:n], refs[n:2 * n]
        send_sems, recv_sems, token = refs[4 * n:5 * n], refs[5 * n:6 * n], refs[6 * n]
        copies = plan(lax.axis_index("x"), lax.axis_index("y"), lax.axis_index("c"))
        for a in range(n):
            for j, (src_block, dst_block, peer) in enumerate(copies):
                pltpu.make_async_remote_copy(
                    src_ref=srcs[a].at[src_block], dst_ref=dsts[a].at[dst_block],
                    send_sem=send_sems[a].at[j], recv_sem=recv_sems[a].at[j],
                    device_id=peer, device_id_type=pl.DeviceIdType.MESH).start()
        token[...] = jnp.zeros_like(token)

    sems = [pltpu.SemaphoreType.DMA((n_copies,))] * (2 * n)
    res = _pcall(
        body, name=name,
        in_specs=[_HBM] * (2 * n),
        out_specs=[_HBM] * (2 * n) + [_SEM] * (2 * n) + [pl.BlockSpec(memory_space=pltpu.VMEM)],
        out_shape=[pltpu.HBM(a.shape, a.dtype) for a in arrays] + [pltpu.HBM(l.shape, l.dtype) for l in lands]
        + sems + [jax.ShapeDtypeStruct((8, LANE), F32)],
        input_output_aliases={i: i for i in range(2 * n)},
        compiler_params=pltpu.CompilerParams(has_side_effects=_DATAFLOW),
    )(*[pltpu.with_memory_space_constraint(a, pltpu.HBM) for a in arrays],
      *[pltpu.with_memory_space_constraint(l, pltpu.HBM) for l in lands])
    handles = [(res[a], res[n + a], res[2 * n + a], res[3 * n + a]) for a in range(n)]
    return handles, res[4 * n][0, 0]


def _split_wait(handles, plan, after, name):
    n = len(handles)
    after = list(after) if isinstance(after, (list, tuple)) else [after]

    def body(*refs):
        srcs, dsts = refs[:n], refs[n:2 * n]
        send_sems, recv_sems = refs[2 * n:3 * n], refs[3 * n:4 * n]
        copies = plan(lax.axis_index("x"), lax.axis_index("y"), lax.axis_index("c"))
        for a in range(n):
            for j, (src_block, dst_block, peer) in enumerate(copies):
                cp = pltpu.make_async_remote_copy(
                    src_ref=srcs[a].at[src_block], dst_ref=dsts[a].at[dst_block],
                    send_sem=send_sems[a].at[j], recv_sem=recv_sems[a].at[j],
                    device_id=peer, device_id_type=pl.DeviceIdType.MESH)
                cp.wait_send()
                cp.wait_recv()

    srcs, lands = [h[0] for h in handles], [h[1] for h in handles]
    res = _pcall(
        body, name=name,
        in_specs=[_HBM] * (2 * n) + [_SEM] * (2 * n) + [pl.BlockSpec(memory_space=pl.ANY)] * len(after),
        out_specs=[_HBM] * (2 * n),
        out_shape=[pltpu.HBM(t.shape, t.dtype) for t in srcs + lands],
        input_output_aliases={i: i for i in range(2 * n)},
        compiler_params=pltpu.CompilerParams(has_side_effects=_DATAFLOW),
    )(*srcs, *lands, *[h[2] for h in handles], *[h[3] for h in handles], *after)
    return res[:n], res[n:]


_ICI_PEERS = (2, 4, 6)


def _gather2_start(shards, name):
    n = len(shards)
    lands = [lax.empty((N_DEV,) + a.shape, a.dtype) for a in shards]

    def body(*refs):
        srcs, dsts = refs[:n], refs[n:2 * n]
        send_sems, d2d_sems, ici_sems = refs[4 * n:5 * n], refs[5 * n:6 * n], refs[6 * n:7 * n]
        token = refs[7 * n]
        x, y, c = lax.axis_index("x"), lax.axis_index("y"), lax.axis_index("c")
        me = 4 * x + 2 * y + c
        for a in range(n):
            for j, k in enumerate((1,) + _ICI_PEERS):
                recv = d2d_sems[a].at[0] if j == 0 else ici_sems[a].at[j - 1]
                pltpu.make_async_remote_copy(
                    src_ref=srcs[a], dst_ref=dsts[a].at[me], send_sem=send_sems[a].at[j], recv_sem=recv,
                    device_id=_peer(k, x, y, c), device_id_type=pl.DeviceIdType.MESH).start()
        token[...] = jnp.zeros_like(token)

    dma = pltpu.SemaphoreType.DMA
    res = _pcall(
        body, name=name,
        in_specs=[_HBM] * (2 * n),
        out_specs=[_HBM] * (2 * n) + [_SEM] * (3 * n) + [pl.BlockSpec(memory_space=pltpu.VMEM)],
        out_shape=[pltpu.HBM(a.shape, a.dtype) for a in shards] + [pltpu.HBM(l.shape, l.dtype) for l in lands]
        + [dma((4,))] * n + [dma((1,))] * n + [dma((3,))] * n + [jax.ShapeDtypeStruct((8, LANE), F32)],
        input_output_aliases={i: i for i in range(2 * n)},
        compiler_params=pltpu.CompilerParams(has_side_effects=_DATAFLOW),
    )(*[pltpu.with_memory_space_constraint(a, pltpu.HBM) for a in shards],
      *[pltpu.with_memory_space_constraint(l, pltpu.HBM) for l in lands])
    handles = [tuple(res[i * n + a] for i in range(5)) for a in range(n)]
    return handles, res[5 * n][0, 0]


def _gather2_forward(handle, after, name):
    src, land, send_sems, d2d_sem, ici_sems = handle

    def body(land_ref, ici_ref, after_ref, land_out, fwd_send, fwd_recv, token):
        x, y, c = lax.axis_index("x"), lax.axis_index("y"), lax.axis_index("c")
        for j, k in enumerate(_ICI_PEERS):
            px, py, pc = _peer(k, x, y, c)
            block = land_ref.at[4 * px + 2 * py + pc]
            pltpu.make_async_remote_copy(
                src_ref=block, dst_ref=block, send_sem=fwd_send.at[j], recv_sem=ici_ref.at[j],
                device_id=(px, py, pc), device_id_type=pl.DeviceIdType.MESH).wait_recv()
            pltpu.make_async_remote_copy(
                src_ref=block, dst_ref=block, send_sem=fwd_send.at[j], recv_sem=fwd_recv.at[j],
                device_id=(x, y, 1 - c), device_id_type=pl.DeviceIdType.MESH).start()
        token[...] = jnp.zeros_like(token)

    dma = pltpu.SemaphoreType.DMA
    land, fwd_send, fwd_recv, token = _pcall(
        body, name=name,
        in_specs=[_HBM, _SEM, pl.BlockSpec(memory_space=pl.ANY)],
        out_specs=[_HBM, _SEM, _SEM, pl.BlockSpec(memory_space=pltpu.VMEM)],
        out_shape=[pltpu.HBM(land.shape, land.dtype), dma((3,)), dma((3,)), jax.ShapeDtypeStruct((8, LANE), F32)],
        input_output_aliases={0: 0},
        compiler_params=pltpu.CompilerParams(has_side_effects=_DATAFLOW),
    )(land, ici_sems, after)
    return (src, land, send_sems, d2d_sem, fwd_send, fwd_recv), token[0, 0]


def _gather2_wait(handle, after, name):
    src, land, send_sems, d2d_sem, fwd_send, fwd_recv = handle

    def body(src_ref, land_ref, send_ref, d2d_ref, fsend_ref, frecv_ref, after_ref, src_out, land_out):
        x, y, c = lax.axis_index("x"), lax.axis_index("y"), lax.axis_index("c")
        me = 4 * x + 2 * y + c
        sibling = (x, y, 1 - c)
        block = land_ref.at[me]

        def copy(send, recv):
            return pltpu.make_async_remote_copy(src_ref=src_ref, dst_ref=block, send_sem=send, recv_sem=recv,
                                                device_id=sibling, device_id_type=pl.DeviceIdType.MESH)

        for j in range(4):
            copy(send_ref.at[j], d2d_ref.at[0]).wait_send()
        copy(send_ref.at[0], d2d_ref.at[0]).wait_recv()
        for j in range(3):
            copy(fsend_ref.at[j], frecv_ref.at[j]).wait_send()
            copy(fsend_ref.at[j], frecv_ref.at[j]).wait_recv()

    res = _pcall(
        body, name=name,
        in_specs=[_HBM, _HBM, _SEM, _SEM, _SEM, _SEM, pl.BlockSpec(memory_space=pl.ANY)],
        out_specs=[_HBM, _HBM],
        out_shape=[pltpu.HBM(src.shape, src.dtype), pltpu.HBM(land.shape, land.dtype)],
        input_output_aliases={0: 0, 1: 1},
        compiler_params=pltpu.CompilerParams(has_side_effects=_DATAFLOW),
    )(src, land, send_sems, d2d_sem, fwd_send, fwd_recv, after)
    return res[0], res[1]


def _pad_cols(x, width=LANE):
    return jnp.pad(x, ((0, 0), (0, width - x.shape[1])))


def _cols_full(g):
    return jnp.transpose(g, (1, 0, 2)).reshape(g.shape[1], -1)


def _ffn_fwd(x1, p, i, tag):
    h2 = _adaln_fwd(x1, p["norm_ffn"][i], p["sc_f"][i], p["sh_f"][i], f"ffn_norm_{tag}")
    u = _matmul(h2, p["fetch"](f"up{i}", h2), name=f"ffn_up_{tag}", tn=1408, b_shards=True)
    a = _conv_act_fwd(u, p["conv_w"][i], p["conv_b"][i], f"ffn_act_{tag}")
    g_f = p["g_f"][i]
    x2, f = _matmul(a, p["fetch"](f"down{i}", a), name=f"ffn_down_{tag}", tk=1408, out_dtypes=(F32, F32),
                    epilogue=lambda acc, x1, g: (x1 + (1.0 + g) * acc, acc), extras=(("mn", x1), ("n", g_f)))
    return x2, dict(h2=h2, u=u, a=a, f=f)


def _ffn_bwd(dx2, x1, saved, p, i, tag):
    d = x1.shape[1]
    w_up, w_down = p["fetch"](f"up{i}", None), p["fetch"](f"down{i}", None)
    df, dg_f = _residual_bwd(dx2, saved["f"], p["g_f"][i], f"ffn_res_bwd_{tag}")
    da = _matmul(df, w_down, tb=True, name=f"ffn_down_dx_{tag}", tn=1408)
    dw_down = _matmul(saved["a"], df, ta=True, name=f"ffn_down_dw_{tag}", tm=1408, out_dtypes=(BF16,))
    du, dcw, dcb = _conv_act_bwd(saved["u"], p["conv_w"][i], p["conv_b"][i], da, f"ffn_act_bwd_{tag}")
    dcw, dcb = (jnp.concatenate([t[0], t[1]], axis=1) for t in (dcw, dcb))
    tok = p["flush"](du)
    dh2 = _matmul(du, w_up, tb=True, name=f"ffn_up_dx_{tag}", tk=1408, a_halves=True, b_shards=True)
    dw_up = _matmul(saved["h2"], du, ta=True, name=f"ffn_up_dw_{tag}", tn=1408, out_dtypes=(BF16,), b_halves=True,
                    out_shards=True)
    tok = tok + p["send"](f"ffn{i}", [dw_up, dw_down.reshape(N_DEV, -1, d)])
    dx1, dsh, dsc, dgain = _adaln_bwd(x1, dh2, dx2, p["norm_ffn"][i] + tok, p["sc_f"][i], f"ffn_norm_bwd_{tag}")
    grads = dict(conv_w=dcw, conv_b=dcb, norm_ffn=dgain, sh_f=dsh, sc_f=dsc, g_f=dg_f)
    return dx1, grads


def _gla_layer_fwd(x, p, i):
    h1 = _adaln_fwd(x, p["norm_mix"][i], p["sc_m"][i], p["sh_m"][i], "gla_norm")
    w_t, w_tail_t, main = p["fetch"]("gla_in", h1)
    proj = _matmul(h1, w_t, tb=True, b_rows=main, name="gla_in")
    a_tail = _matmul(h1, w_tail_t, tb=True, name="gla_in_tail")
    dk_total = p["gla_wg_p"].shape[1]
    o, states = _gla_fwd(proj, a_tail, p["gla_wg_p"], p["gla_b_gate"], "gla_chunks")
    assert 2 * dk_total == o.shape[1]
    r = ("cols", proj, 2, o.shape[1])
    og = _gla_post_fwd(o, r, p["gla_norm"], "gla_post")
    x1, y = _matmul(og, p["fetch"]("gla_out", og), name="gla_out", out_dtypes=(F32, F32),
                    epilogue=lambda acc, x, g: (x + (1.0 + g) * acc, acc), extras=(("mn", x), ("n", p["g_m"][i])))
    return x1, dict(h1=h1, proj=proj, a_tail=a_tail, o=o, r=r, states=states, og=og, y=y)


def _gla_layer_bwd(dx1, x, sv, p, i):
    d = x.shape[1]
    (w_t, w_tail_t, main), w_out = p["fetch"]("gla_in", None), p["fetch"]("gla_out", None)
    dy, dg_m = _residual_bwd(dx1, sv["y"], p["g_m"][i], "gla_res_bwd")
    dog = _matmul(dy, w_out, tb=True, name="gla_out_dx")
    dw_out = _matmul(sv["og"], dy, ta=True, name="gla_out_dw", out_dtypes=(BF16,))
    tok = p["flush"](dog) + p["send"]("gla_out", [dw_out.reshape(N_DEV, -1, d)])
    d_o, d_r, dgn = _gla_post_bwd(sv["o"], sv["r"], p["gla_norm"] + tok, dog, "gla_post_bwd")
    dq, dk, dv, dga = _gla_bwd(sv["proj"], sv["a_tail"], p["gla_wg_p"], p["gla_b_gate"], sv["states"], d_o,
                               "gla_chunks_bwd")
    tok = p["flush"](dga)
    da_tail = _matmul(dga, p["gla_wg_p"], tb=True, name="gla_gate_dx", out_dtypes=(BF16,))
    dwg = _matmul(sv["a_tail"], dga, ta=True, name="gla_gate_dw")
    dbg = _rowwise(lambda t: (_colsum(t),), [("row", dga)], [("acc", dga.shape[1], F32)], name="gla_gate_db")[0]
    dproj = jnp.concatenate([dq, dk, dv, d_r], axis=1)
    dh_tail = _matmul(da_tail, w_tail_t, name="gla_in_tail_dx")
    dh1 = _matmul(dproj, w_t, b_rows=main, name="gla_in_dx", tk=2048,
                  epilogue=lambda acc, t: (acc + t,), extras=(("mn", dh_tail),))
    rank = p["gla_rank"]
    dw_main = _matmul(dproj, sv["h1"], ta=True, name="gla_in_dw", out_dtypes=(BF16,), out_rows=main + rank)
    dx, dsh, dsc, dgain = _adaln_bwd(x, dh1, dx1, p["norm_mix"][i] + tok, p["sc_m"][i], "gla_norm_bwd")
    grads = dict(gla_w_gate=dwg[:rank], gla_b_gate=dbg, gla_norm=dgn, norm_mix=dgain, sh_m=dsh, sc_m=dsc, g_m=dg_m,
                 gla_w_in_unsent=(dw_main, da_tail, sv["h1"]))
    return dx, grads


def _fox_layer_fwd(x, p, i):
    d = x.shape[1]
    hd = p["fox_q_norm"].shape[1]
    heads = d // hd
    s = x.shape[0]
    t = _tile(s, 512)
    h1 = _adaln_fwd(x, p["norm_mix"][i], p["sc_m"][i], p["sh_m"][i], "fox_norm")
    w_t, w_tail_t, main = p["fetch"]("fox_in", h1)
    proj = _matmul(h1, w_t, tb=True, b_rows=main, name="fox_in")
    fl = _matmul(h1, w_tail_t, tb=True, name="fox_in_tail")
    q, k, v, og = (("cols", proj, j, d) for j in range(4))
    qn, kn, vb = _fox_prep(q, k, v, p["fox_q_norm"], p["fox_k_norm"], d, hd, "fox_prep")
    cum = _fox_cum(fl, p["fox_bf_p"], "fox_cum")
    cum_t = jnp.transpose(cum[:, :heads])
    cum_col, cum_row = cum_t[:, :, None], cum_t.reshape(heads, s // t, 1, t)
    o, lse = _fox_attn_fwd(qn, kn, vb, cum_col, cum_row, hd, t, "fox_attn")
    act = _fox_gate_fwd(o, og, "fox_gate")
    x1, y = _matmul(act, p["fetch"]("fox_out", act), name="fox_out", out_dtypes=(F32, F32),
                    epilogue=lambda acc, x, g: (x + (1.0 + g) * acc, acc), extras=(("mn", x), ("n", p["g_m"][i])))
    return x1, dict(h1=h1, q=q, k=k, og=og, fl=fl, qn=qn, kn=kn, vb=vb, cum_col=cum_col, cum_row=cum_row,
                    o=o, lse=lse, act=act, y=y, t=t, hd=hd)


def _fox_layer_bwd(dx1, x, sv, p, i):
    d = x.shape[1]
    hd, t = sv["hd"], sv["t"]
    heads = d // hd
    s = x.shape[0]
    (w_t, w_tail_t, main), w_out = p["fetch"]("fox_in", None), p["fetch"]("fox_out", None)
    dy, dg_m = _residual_bwd(dx1, sv["y"], p["g_m"][i], "fox_res_bwd")
    dact = _matmul(dy, w_out, tb=True, name="fox_out_dx")
    dw_out = _matmul(sv["act"], dy, ta=True, name="fox_out_dw", out_dtypes=(BF16,))
    d_o, d_og = _fox_gate_bwd(sv["o"], sv["og"], dact, "fox_gate_bwd")
    tok_flush = p["flush"](d_og)
    dqn, dkn, dvb, dcq, dck = _fox_attn_bwd(sv["qn"], sv["kn"], sv["vb"], d_o, sv["o"], sv["lse"], sv["cum_col"],
                                            sv["cum_row"], hd, t, "fox_attn_bwd")
    dq, dk, gq, gk = _fox_prep_bwd(sv["q"], sv["k"], dqn, dkn, p["fox_q_norm"], p["fox_k_norm"], hd, "fox_prep_bwd")
    dcum = _pad_cols(jnp.transpose(dcq[:, :, 0] - dck.reshape(heads, s)))
    dfl, dbf = _fox_cum_bwd(dcum, sv["fl"], p["fox_bf_p"], "fox_cum_bwd")
    dfl_b = dfl.astype(BF16)
    dproj = jnp.concatenate([dq, dk, dvb, d_og], axis=1)
    dh_tail = _matmul(dfl_b, w_tail_t, name="fox_in_tail_dx")
    dh1 = _matmul(dproj, w_t, b_rows=main, name="fox_in_dx", tk=2048,
                  epilogue=lambda acc, tl: (acc + tl,), extras=(("mn", dh_tail),))
    dw_main = _matmul(dproj, sv["h1"], ta=True, name="fox_in_dw", out_dtypes=(BF16,), out_rows=main + heads)
    dw_in = _tail_rows(dfl_b, sv["h1"], dw_main, heads, "fox_in_tail_dw").reshape(N_DEV, -1, d)
    tok = tok_flush + p["send"]("fox", [dw_in, dw_out.reshape(N_DEV, -1, d)])
    dx, dsh, dsc, dgain = _adaln_bwd(x, dh1, dx1, p["norm_mix"][i] + tok, p["sc_m"][i], "fox_norm_bwd")
    grads = dict(fox_b_f=dbf[:, :heads], fox_q_norm=gq.reshape(heads, hd).sum(0, keepdims=True),
                 fox_k_norm=gk.reshape(heads, hd).sum(0, keepdims=True), norm_mix=dgain, sh_m=dsh, sc_m=dsc, g_m=dg_m)
    return dx, grads


SMALL = ("b_mod", "norm_mix", "norm_ffn", "gla_b_gate", "gla_norm", "fox_b_f", "fox_q_norm", "fox_k_norm",
         "ffn_conv_b", "norm_final")
SMALL_SHARDED = ("gla_w_gate", "ffn_conv_w")
BIG = ("gla_w_in", "gla_w_out", "fox_w_in", "fox_w_out", "ffn_w_up", "ffn_w_down")
WEIGHTS = ("w_mod", "b_mod", "norm_mix", "norm_ffn", "gla_w_in", "gla_w_gate", "gla_b_gate", "gla_norm", "gla_w_out",
           "fox_w_in", "fox_b_f", "fox_q_norm", "fox_k_norm", "fox_w_out", "ffn_w_up", "ffn_conv_w", "ffn_conv_b",
           "ffn_w_down", "norm_final")


def _pack(parts):
    flat = jnp.concatenate([p.reshape(-1) for p in parts])
    pad = (-flat.shape[0]) % 1024
    return jnp.pad(flat, (0, pad)).reshape(1, -1)


def _unpack(flat, shapes):
    out, off = [], 0
    for shp in shapes:
        n = 1
        for s in shp:
            n *= s
        out.append(flat[0, off:off + n].reshape(shp))
        off += n
    return out


def kernel(x, c, w_mod, b_mod, norm_mix, norm_ffn, gla_w_in, gla_w_gate, gla_b_gate, gla_norm, gla_w_out, fox_w_in, fox_b_f, fox_q_norm, fox_k_norm, fox_w_out, ffn_w_up, ffn_conv_w, ffn_conv_b, ffn_w_down, norm_final, loss_target, m_w_mod, m_b_mod, m_norm_mix, m_norm_ffn, m_gla_w_in, m_gla_w_gate, m_gla_b_gate, m_gla_norm, m_gla_w_out, m_fox_w_in, m_fox_b_f, m_fox_q_norm, m_fox_k_norm, m_fox_w_out, m_ffn_w_up, m_ffn_conv_w, m_ffn_conv_b, m_ffn_w_down, m_norm_final, v_w_mod, v_b_mod, v_norm_mix, v_norm_ffn, v_gla_w_in, v_gla_w_gate, v_gla_b_gate, v_gla_norm, v_gla_w_out, v_fox_w_in, v_fox_b_f, v_fox_q_norm, v_fox_k_norm, v_fox_w_out, v_ffn_w_up, v_ffn_conv_w, v_ffn_conv_b, v_ffn_w_down, v_norm_final):
    w = dict(w_mod=w_mod, b_mod=b_mod, norm_mix=norm_mix, norm_ffn=norm_ffn, gla_w_in=gla_w_in, gla_w_gate=gla_w_gate,
             gla_b_gate=gla_b_gate, gla_norm=gla_norm, gla_w_out=gla_w_out, fox_w_in=fox_w_in, fox_b_f=fox_b_f,
             fox_q_norm=fox_q_norm, fox_k_norm=fox_k_norm, fox_w_out=fox_w_out, ffn_w_up=ffn_w_up,
             ffn_conv_w=ffn_conv_w, ffn_conv_b=ffn_conv_b, ffn_w_down=ffn_w_down, norm_final=norm_final)
    mom_m = dict(w_mod=m_w_mod, b_mod=m_b_mod, norm_mix=m_norm_mix, norm_ffn=m_norm_ffn, gla_w_in=m_gla_w_in,
                 gla_w_gate=m_gla_w_gate, gla_b_gate=m_gla_b_gate, gla_norm=m_gla_norm, gla_w_out=m_gla_w_out,
                 fox_w_in=m_fox_w_in, fox_b_f=m_fox_b_f, fox_q_norm=m_fox_q_norm, fox_k_norm=m_fox_k_norm,
                 fox_w_out=m_fox_w_out, ffn_w_up=m_ffn_w_up, ffn_conv_w=m_ffn_conv_w, ffn_conv_b=m_ffn_conv_b,
                 ffn_w_down=m_ffn_w_down, norm_final=m_norm_final)
    mom_v = dict(w_mod=v_w_mod, b_mod=v_b_mod, norm_mix=v_norm_mix, norm_ffn=v_norm_ffn, gla_w_in=v_gla_w_in,
                 gla_w_gate=v_gla_w_gate, gla_b_gate=v_gla_b_gate, gla_norm=v_gla_norm, gla_w_out=v_gla_w_out,
                 fox_w_in=v_fox_w_in, fox_b_f=v_fox_b_f, fox_q_norm=v_fox_q_norm, fox_k_norm=v_fox_k_norm,
                 fox_w_out=v_fox_w_out, ffn_w_up=v_ffn_w_up, ffn_conv_w=v_ffn_conv_w, ffn_conv_b=v_ffn_conv_b,
                 ffn_w_down=v_ffn_w_down, norm_final=v_norm_final)

    me = 4 * lax.axis_index("x") + 2 * lax.axis_index("y") + lax.axis_index("c")
    xs, target = x[0], loss_target[0]
    s, d = xs.shape
    depth = w_mod.shape[0]
    mod_cols = w_mod.shape[2]
    rank = gla_w_gate.shape[1]
    hd = fox_q_norm.shape[1]
    fox_heads = d // hd
    dk_total = gla_w_gate.shape[2] * N_DEV

    cond = c * (1.0 / (1.0 + jnp.exp(-c)))
    g, _ = _exchange([gla_w_gate[0], ffn_conv_w, cond], "gather_small", scatter=False)
    cond_all = g[2][:, 0, :]

    cond_pad = jnp.pad(cond_all, ((0, 16 - N_DEV), (0, 0)))
    mod_part = []
    for i in range(depth):
        b_cols = lax.dynamic_slice(b_mod[i:i + 1], (0, me * mod_cols), (1, mod_cols))
        mod_part.append(_matmul(cond_pad, w_mod[i], name=f"mod_{i}", tn=768,
                                epilogue=lambda acc, b: (acc + b,), extras=(("n", b_cols),))[:N_DEV])
    (mod_all,), tok_mod = _exchange([jnp.stack(mod_part)], "gather_mod", scatter=False)
    mod = lax.dynamic_index_in_dim(mod_all, me, axis=2, keepdims=False)
    mod = jnp.transpose(mod, (1, 0, 2)).reshape(depth, 6, 1, d)

    big_names = ["gla_in", "gla_out", "up0", "down0", "fox_in", "fox_out", "up1", "down1"]
    first = [jnp.transpose(gla_w_in[0] + tok_mod).astype(BF16), gla_w_out[0].astype(BF16)]
    handles, tok_first = _gather2_start(first, "gather_weights_start_first")
    rest = [ffn_w_up[0] + tok_first, ffn_w_down[0], jnp.transpose(fox_w_in[0]), fox_w_out[0], ffn_w_up[1],
            ffn_w_down[1]]
    handles_rest, tok0 = _gather2_start([t.astype(BF16) for t in rest], "gather_weights_start_rest")
    handles = handles + handles_rest
    ready, forwarded = {}, {}

    def split_tail(full_t, tail):
        main = full_t.shape[0] - tail
        return full_t, jnp.pad(full_t[main:], ((0, LANE - tail), (0, 0))), main

    def forward(idx, after):
        key = big_names[idx]
        forwarded[key] = _gather2_forward(handles[idx], after, f"gather_{key}_forward")

    def fetch(key, after):
        if key not in ready:
            idx = big_names.index(key)
            if idx == 0:
                forward(0, after)
            handle, _ = forwarded[key]
            mine, land = _gather2_wait(handle, after, f"gather_{key}_wait")
            if idx + 1 < len(big_names):
                forward(idx + 1, land)
                mine = mine + forwarded[big_names[idx + 1]][1].astype(BF16)
            full = lax.dynamic_update_slice(land, mine[None], (me,) + (0,) * mine.ndim)
            if key == "gla_in":
                ready[key] = split_tail(full.reshape(-1, d), rank)
            elif key == "fox_in":
                ready[key] = split_tail(full.reshape(-1, d), fox_heads)
            elif key.startswith("up"):
                ready[key] = full
            else:
                ready[key] = full.reshape(-1, d)
        return ready[key]

    pending, sent = [], {}
    core = lax.axis_index("c").astype(jnp.int32).reshape(1)
    chip = 2 * lax.axis_index("x") + lax.axis_index("y")

    def send(key, pieces):
        hs, tok = _split_start(pieces, _pair_plan, f"scatter_{key}_pair_start")
        pending.append((key, hs))
        return tok

    def flush(after):
        tok = 0.0
        while pending:
            key, hs = pending.pop(0)
            mine, partner = _split_wait(hs, _pair_plan, after, f"scatter_{key}_pair_wait")
            sums = [_pair_sum(pc, pt, core, f"scatter_{key}_pair_sum{a}")
                    for a, (pc, pt) in enumerate(zip(mine, partner))]
            sent[key], t = _split_start(sums, _chip_plan, f"scatter_{key}_chip_start")
            tok = tok + t
        return tok

    p = dict(
        fetch=fetch, send=send, flush=flush,
        gla_wg_p=jnp.pad(_cols_full(g[0]), ((0, LANE - rank), (0, 0))),
        conv_w=[jnp.transpose(g[1][:, i], (1, 0, 2)).reshape(ffn_conv_w.shape[1], -1) for i in range(depth)],
        conv_b=[ffn_conv_b[i:i + 1] for i in range(depth)],
        gla_b_gate=gla_b_gate, gla_norm=gla_norm, fox_q_norm=fox_q_norm, fox_k_norm=fox_k_norm,
        fox_bf_p=_pad_cols(fox_b_f), gla_rank=rank,
        norm_mix=[norm_mix[i:i + 1] + (tok0 if i == 0 else 0.0) for i in range(depth)],
        norm_ffn=[norm_ffn[i:i + 1] for i in range(depth)],
    )

    for j, nm in enumerate(("sh_m", "sc_m", "g_m", "sh_f", "sc_f", "g_f")):
        p[nm] = [mod[i, j] for i in range(depth)]

    acts, saved = [xs], []
    for i in range(depth):
        layer_fwd = _gla_layer_fwd if i % 2 == 0 else _fox_layer_fwd
        x1, sv_mix = layer_fwd(acts[-1], p, i)
        x2, sv_ffn = _ffn_fwd(x1, p, i, str(i))
        saved.append((acts[-1], x1, sv_mix, sv_ffn))
        acts.append(x2)
    dx, d_norm_final, loss_part = _final_loss(acts[-1], target, norm_final.reshape(1, d), "final_loss")

    lg = [None] * depth
    for i in reversed(range(depth)):
        x_in, x1, sv_mix, sv_ffn = saved[i]
        dx, g_ffn = _ffn_bwd(dx, x1, sv_ffn, p, i, str(i))
        layer_bwd = _gla_layer_bwd if i % 2 == 0 else _fox_layer_bwd
        dx, g_mix = layer_bwd(dx, x_in, sv_mix, p, i)
        lg[i] = {**g_ffn, **g_mix}
    grad_x = dx[None]

    gla_l = [i for i in range(depth) if i % 2 == 0]
    fox_l = [i for i in range(depth) if i % 2 == 1]
    small_parts = dict(
        norm_mix=jnp.concatenate([lg[i]["norm_mix"] for i in range(depth)]),
        norm_ffn=jnp.concatenate([lg[i]["norm_ffn"] for i in range(depth)]),
        gla_b_gate=jnp.concatenate([lg[i]["gla_b_gate"] for i in gla_l]),
        gla_norm=jnp.concatenate([lg[i]["gla_norm"] for i in gla_l]),
        fox_b_f=jnp.concatenate([lg[i]["fox_b_f"] for i in fox_l]),
        fox_q_norm=jnp.concatenate([lg[i]["fox_q_norm"] for i in fox_l]),
        fox_k_norm=jnp.concatenate([lg[i]["fox_k_norm"] for i in fox_l]),
        ffn_conv_b=jnp.concatenate([lg[i]["conv_b"] for i in range(depth)]),
        norm_final=d_norm_final,
        gla_w_gate=jnp.stack([lg[i]["gla_w_gate"] for i in gla_l]),
        ffn_conv_w=jnp.stack([lg[i]["conv_w"] for i in range(depth)]),
        loss=loss_part[:, :1],
    )
    order = ("norm_mix", "norm_ffn", "gla_b_gate", "gla_norm", "fox_b_f", "fox_q_norm", "fox_k_norm", "ffn_conv_b",
             "norm_final", "gla_w_gate", "ffn_conv_w", "loss")
    packed = _pack([small_parts[nm] for nm in order])
    dmod = jnp.stack([jnp.concatenate([lg[i][nm] for nm in ("sh_m", "sc_m", "g_m", "sh_f", "sc_f", "g_f")], axis=1)
                      for i in range(depth)])
    hs_small, tok_small = _split_start([packed[None], dmod[None]], _all_plan, "gather_small_grads_start",
                                       land_blocks=N_DEV)
    dw_main, da_tail, h1_gla = lg[0]["gla_w_in_unsent"]
    dw_in_t = _tail_rows(da_tail + tok_small.astype(BF16), h1_gla, dw_main, rank, "gla_in_tail_dw")
    send("gla_in", [dw_in_t.reshape(N_DEV, -1, d)])
    started = pending[-1][1][0][0]

    received = {}

    def arrive(key, after):
        sums, lands = _split_wait(sent[key], _chip_plan, after, f"scatter_{key}_chip_wait")
        received[key] = [lax.dynamic_update_slice(land, lax.dynamic_index_in_dim(q, chip, 0, keepdims=True),
                                                  (chip,) + (0,) * (q.ndim - 1))
                         for land, q in zip(lands, sums)]

    for key in ("ffn1", "fox", "ffn0", "gla_out"):
        arrive(key, started)

    out_g, out_d, out_m, out_v = {}, {}, {}, {}

    def update(nm, g_arr, transposed=False):
        swap = (lambda t: jnp.transpose(t, (0, 2, 1))) if transposed else (lambda t: t)
        res = _adamw(swap(w[nm]), g_arr, swap(mom_m[nm]), swap(mom_v[nm]), f"adamw_{nm}")
        out_g[nm], out_d[nm], out_m[nm], out_v[nm] = (swap(t) for t in res)

    update("ffn_w_up", [received[f"ffn{i}"][0] for i in range(depth)])
    tok_flush = flush(out_g["ffn_w_up"])
    update("gla_w_out", [received["gla_out"][0]])
    update("fox_w_in", [received["fox"][0]], transposed=True)
    update("fox_w_out", [received["fox"][1]])
    update("ffn_w_down", [received[f"ffn{i}"][1] for i in range(depth)])

    updated = ("gla_w_out", "fox_w_in", "fox_w_out", "ffn_w_up", "ffn_w_down")
    (packed_mine, dmod_mine), (packed_all, dmod_all) = _split_wait(
        hs_small, _all_plan, [out_d[nm] for nm in updated], "gather_small_grads_wait")
    packed_all = lax.dynamic_update_slice(packed_all, packed_mine + tok_flush, (me, 0, 0))
    dmod_all = lax.dynamic_update_slice(dmod_all, dmod_mine, (me, 0, 0, 0))
    summed = _unpack(_sum8(packed_all, "sum_small_grads"), [small_parts[nm].shape for nm in order])
    small_g = dict(zip(order, summed))
    loss = small_g["loss"][0, 0]
    dmod_all = dmod_all[:, :, 0, :]
    grads = {}
    cond_t = _pad_cols(jnp.transpose(cond_all)).astype(BF16)
    dmod_cols = lax.dynamic_slice(dmod_all, (0, 0, me * mod_cols), (N_DEV, depth, mod_cols))
    g_w_mod = []
    for i in range(depth):
        rhs = jnp.pad(dmod_cols[:, i], ((0, LANE - N_DEV), (0, 0)))
        g_w_mod.append(_matmul(cond_t, rhs, name=f"mod_dw_{i}", tn=768))
    grads["w_mod"] = jnp.stack(g_w_mod)
    small_g["b_mod"] = _sum8(dmod_all.reshape(N_DEV, 1, -1), "sum_b_mod").reshape(depth, -1)
    update("w_mod", grads["w_mod"])

    gate_cols = gla_w_gate.shape[2]
    conv_cols = ffn_conv_w.shape[2]
    local_small = dict(small_g)
    local_small["gla_w_gate"] = lax.dynamic_slice_in_dim(small_g["gla_w_gate"], me * gate_cols, gate_cols, axis=2)
    local_small["ffn_conv_w"] = lax.dynamic_slice_in_dim(small_g["ffn_conv_w"], me * conv_cols, conv_cols, axis=2)
    names = SMALL + SMALL_SHARDED
    shapes = [w[nm].shape for nm in names]
    res = _adamw(_pack([w[nm] for nm in names])[None], _pack([local_small[nm] for nm in names])[None],
                 _pack([mom_m[nm] for nm in names])[None], _pack([mom_v[nm] for nm in names])[None], "adamw_small")
    for tgt, flat in zip((out_g, out_d, out_m, out_v), res):
        for nm, arr in zip(names, _unpack(flat[0], shapes)):
            tgt[nm] = arr

    arrive("gla_in", [out_d[nm] for nm in updated + ("w_mod",)])
    update("gla_w_in", [received["gla_in"][0]], transposed=True)

    return (loss, grad_x, *[out_g[n] for n in WEIGHTS], *[out_d[n] for n in WEIGHTS],
            *[out_m[n] for n in WEIGHTS], *[out_v[n] for n in WEIGHTS])
```

```python
import jax
import jax.numpy as jnp
from jax import lax
from jax.experimental import pallas as pl
from jax.experimental.pallas import tpu as pltpu

F32, BF16 = jnp.float32, jnp.bfloat16
N_DEV = 8
GLA_HEADS = 4
GLA_TAU = 16.0
GLA_CHUNK = 64
NORM_EPS = 1e-6
ADAM_LR, ADAM_B1, ADAM_B2, ADAM_EPS, ADAM_WD, ADAM_STEP = 0.001, 0.9, 0.999, 1e-08, 0.01, 10
LANE = 128
VMEM_LIMIT = 56 * 1024 * 1024
NEG = -1e30


def _pcall(body, **kw):
    return pl.pallas_call(body, **kw)


def _params(n_axes):
    return pltpu.CompilerParams(dimension_semantics=("arbitrary",) * n_axes, vmem_limit_bytes=VMEM_LIMIT)


def _tile(dim, pref):
    if dim <= pref:
        return dim
    t = pref
    while dim % t:
        t -= LANE
    assert t > 0, (dim, pref)
    return t


def _dot(a, b, ta=False, tb=False):
    dims = (((0,) if ta else (1,), (1,) if tb else (0,)), ((), ()))
    return lax.dot_general(a.astype(BF16), b.astype(BF16), dims, preferred_element_type=F32)


def _split3(x):
    hi = x.astype(BF16)
    r1 = x - hi.astype(F32)
    mid = r1.astype(BF16)
    lo = (r1 - mid.astype(F32)).astype(BF16)
    return hi, mid, lo


def _tri_matmul(tri, x):
    hi, mid, lo = _split3(x)
    return _dot(tri, hi) + _dot(tri, mid) + _dot(tri, lo)


def _tri(n, upper=False):
    r = lax.broadcasted_iota(jnp.int32, (n, n), 0)
    c = lax.broadcasted_iota(jnp.int32, (n, n), 1)
    return jnp.where((r <= c) if upper else (r >= c), 1.0, 0.0).astype(BF16)


def _log_sigmoid(x):
    return jnp.minimum(x, 0.0) - jnp.log(1.0 + jnp.exp(-jnp.abs(x)))


def _sigmoid(x):
    return 1.0 / (1.0 + jnp.exp(-x))


def _silu(x):
    return x * _sigmoid(x)


def _dsilu(x):
    s = _sigmoid(x)
    return s * (1.0 + x * (1.0 - s))


def _matmul(a, b, *, name, ta=False, tb=False, out_dtypes=(F32,), tm=1024, tn=1024, tk=2048,
            epilogue=None, extras=(), a_halves=False, b_halves=False, b_shards=False, out_shards=False,
            b_rows=None, out_rows=None):
    if a_halves:
        assert not ta
        m, k = a.shape[1], 2 * a.shape[2]
    else:
        m, k = (a.shape[1], a.shape[0]) if ta else a.shape
    if b_halves:
        assert not tb and b.shape[1] == k
        n = 2 * b.shape[2]
    elif b_shards:
        n = b.shape[1] if tb else N_DEV * b.shape[2]
        assert (N_DEV * b.shape[2] if tb else b.shape[1]) == k, (a.shape, b.shape, ta, tb)
    else:
        rows = b.shape[0] if b_rows is None else b_rows
        n = rows if tb else b.shape[1]
        assert (b.shape[1] if tb else rows) == k, (a.shape, b.shape, ta, tb)
    n_unit = n // N_DEV if (out_shards or (b_shards and not tb)) else (n // 2 if b_halves else n)
    k_unit = k // N_DEV if (b_shards and tb) else (k // 2 if a_halves else k)
    tm, tn, tk = _tile(m, tm), _tile(n_unit, tn), _tile(k_unit, tk)
    nk = k // tk
    if a_halves:
        a_spec = pl.BlockSpec((None, tm, tk), lambda i, j, kk: (kk // (nk // 2), i, kk % (nk // 2)))
    elif ta:
        a_spec = pl.BlockSpec((tk, tm), lambda i, j, kk: (kk, i))
    else:
        a_spec = pl.BlockSpec((tm, tk), lambda i, j, kk: (i, kk))
    n_per, k_per = n // tn // N_DEV, nk // N_DEV
    if b_halves:
        b_spec = pl.BlockSpec((None, tk, tn), lambda i, j, kk: (j // (n // tn // 2), kk, j % (n // tn // 2)))
    elif b_shards and tb:
        b_spec = pl.BlockSpec((None, tn, tk), lambda i, j, kk: (kk // k_per, j, kk % k_per))
    elif b_shards:
        b_spec = pl.BlockSpec((None, tk, tn), lambda i, j, kk: (j // n_per, kk, j % n_per))
    elif tb:
        b_spec = pl.BlockSpec((tn, tk), lambda i, j, kk: (j, kk))
    else:
        b_spec = pl.BlockSpec((tk, tn), lambda i, j, kk: (kk, j))
    ex_specs = []
    for kind, arr in extras:
        if kind == "mn":
            assert arr.shape == (m, n), (arr.shape, m, n)
            ex_specs.append(pl.BlockSpec((tm, tn), lambda i, j, kk: (i, j)))
        else:
            assert arr.shape == (1, n), (arr.shape, n)
            ex_specs.append(pl.BlockSpec((1, tn), lambda i, j, kk: (0, j)))
    n_ex, n_out = len(extras), len(out_dtypes)

    def body(a_ref, b_ref, *rest):
        ex, outs, acc = rest[:n_ex], rest[n_ex:n_ex + n_out], rest[-1]
        kk = pl.program_id(2)

        @pl.when(kk == 0)
        def _():
            acc[...] = jnp.zeros_like(acc)

        acc[...] += _dot(a_ref[...], b_ref[...], ta, tb)

        @pl.when(kk == nk - 1)
        def _():
            if epilogue is None:
                vals = (acc[...],)
            else:
                vals = epilogue(acc[...], *[e[...] for e in ex])
            for o, v in zip(outs, vals):
                o[...] = v.astype(o.dtype)

    if out_shards:
        out_spec = pl.BlockSpec((None, tm, tn), lambda i, j, kk: (j // n_per, i, j % n_per))
        out_dims = (N_DEV, m, n // N_DEV)
    else:
        out_spec = pl.BlockSpec((tm, tn), lambda i, j, kk: (i, j))
        out_dims = (m if out_rows is None else out_rows, n)
    res = _pcall(
        body, name=name, grid=(m // tm, n // tn, nk),
        in_specs=[a_spec, b_spec] + ex_specs,
        out_specs=[out_spec] * n_out,
        out_shape=[jax.ShapeDtypeStruct(out_dims, d) for d in out_dtypes],
        scratch_shapes=[pltpu.VMEM((tm, tn), F32)],
        compiler_params=_params(3),
    )(a, b, *[arr for _, arr in extras])
    return res[0] if n_out == 1 else res


def _tail_rows(a, b, into, rows, name, tn=1024):
    k, n = b.shape
    m_total = into.shape[0]
    tn = _tile(n, tn)

    def body(a_ref, b_ref, into_ref, out_ref):
        out_ref[...] = _dot(a_ref[...], b_ref[...], ta=True)[:rows].astype(out_ref.dtype)

    return _pcall(
        body, name=name, grid=(n // tn,),
        in_specs=[pl.BlockSpec((k, a.shape[1]), lambda j: (0, 0)), pl.BlockSpec((k, tn), lambda j: (0, j)),
                  pl.BlockSpec(memory_space=pl.ANY)],
        out_specs=pl.BlockSpec((rows, tn), lambda j: (m_total // rows - 1, j)),
        out_shape=jax.ShapeDtypeStruct(into.shape, into.dtype),
        input_output_aliases={2: 0}, compiler_params=_params(1),
    )(a, b, into)


def _rowwise(fn, ins, outs, *, name, tr=128):
    rows = next(e[1].shape[0] for e in ins if e[0] != "full")
    tr = _tile(rows, tr)
    in_specs = []
    for entry in ins:
        kind, arr = entry[0], entry[1]
        assert kind == "full" or (arr.shape[0] == rows and arr.ndim == 2)
        if kind == "row":
            in_specs.append(pl.BlockSpec((tr, arr.shape[1]), lambda i: (i, 0)))
        elif kind == "cols":
            in_specs.append(pl.BlockSpec((tr, entry[3]), lambda i, cb=entry[2]: (i, cb)))
        else:
            in_specs.append(pl.BlockSpec(arr.shape, lambda i, nd=arr.ndim: (0,) * nd))
    out_specs, out_shape = [], []
    for kind, w, dt in outs:
        if kind == "row":
            out_specs.append(pl.BlockSpec((tr, w), lambda i: (i, 0)))
            out_shape.append(jax.ShapeDtypeStruct((rows, w), dt))
        else:
            out_specs.append(pl.BlockSpec((1, w), lambda i: (0, 0)))
            out_shape.append(jax.ShapeDtypeStruct((1, w), dt))
    n_in = len(ins)

    def body(*refs):
        i = pl.program_id(0)
        vals = fn(*[r[...] for r in refs[:n_in]])
        for (kind, _, _), o, v in zip(outs, refs[n_in:], vals):
            if kind == "row":
                o[...] = v.astype(o.dtype)
            else:
                @pl.when(i == 0)
                def _(o=o):
                    o[...] = jnp.zeros_like(o)

                o[...] += v.astype(o.dtype)

    return _pcall(body, name=name, grid=(rows // tr,), in_specs=in_specs, out_specs=out_specs,
                  out_shape=out_shape, compiler_params=_params(1))(*[e[1] for e in ins])


def _colsum(x):
    return jnp.sum(x, axis=0, keepdims=True)


def _norm_stats(x):
    rstd = lax.rsqrt(jnp.mean(x * x, axis=-1, keepdims=True) + NORM_EPS)
    return x * rstd, rstd


def _norm_bwd(dxhat, xhat, rstd):
    return rstd * (dxhat - xhat * jnp.mean(dxhat * xhat, axis=-1, keepdims=True))


def _adaln_fwd(x, gain, sc, sh, name):
    def fn(x, gain, sc, sh):
        xhat, _ = _norm_stats(x)
        return ((xhat * gain) * (1.0 + sc) + sh,)

    return _rowwise(fn, [("row", x), ("full", gain), ("full", sc), ("full", sh)],
                    [("row", x.shape[1], BF16)], name=name)[0]


def _adaln_bwd(x, dh, dres, gain, sc, name):
    d = x.shape[1]

    def fn(x, dh, dres, gain, sc):
        xhat, rstd = _norm_stats(x)
        dxhat = dh * (gain * (1.0 + sc))
        dx = dres + _norm_bwd(dxhat, xhat, rstd)
        return dx, _colsum(dh), _colsum(dh * (xhat * gain)), _colsum(dh * xhat * (1.0 + sc))

    return _rowwise(fn, [("row", x), ("row", dh), ("row", dres), ("full", gain), ("full", sc)],
                    [("row", d, F32), ("acc", d, F32), ("acc", d, F32), ("acc", d, F32)], name=name)


def _residual_bwd(dx, y, g, name):
    d = dx.shape[1]

    def fn(dx, y, g):
        return dx * (1.0 + g), _colsum(dx * y)

    return _rowwise(fn, [("row", dx), ("row", y), ("full", g)], [("row", d, BF16), ("acc", d, F32)], name=name)


def _final_loss(x, target, gain, name):
    d = x.shape[1]

    def fn(x, t, gain):
        xhat, rstd = _norm_stats(x)
        err = xhat * gain - t
        dy = err * (1.0 / d)
        loss = 0.5 * jnp.sum(jnp.mean(err * err, axis=-1, keepdims=True), axis=0, keepdims=True)
        dx = _norm_bwd(dy * gain, xhat, rstd)
        return dx, _colsum(dy * xhat), jnp.broadcast_to(loss, (1, LANE))

    return _rowwise(fn, [("row", x), ("row", target), ("full", gain)],
                    [("row", d, F32), ("acc", d, F32), ("acc", LANE, F32)], name=name)


def _gla_gates(q, k, a, wg, bg, scale, c):
    ga = _dot(a, wg) + bg
    la = _log_sigmoid(ga) * (1.0 / GLA_TAU)
    b = _tri_matmul(_tri(c), la)
    bl = _colsum(la)
    eb, enb, eend = jnp.exp(b), jnp.exp(-b), jnp.exp(bl - b)
    q = q * scale
    return dict(ga=ga, eb=eb, enb=enb, eend=eend, dec=jnp.exp(bl), q_dec=q * eb, k_inv=k * enb, k_end=k * eend)


def _causal(c):
    return lax.broadcasted_iota(jnp.int32, (c, c), 0) >= lax.broadcasted_iota(jnp.int32, (c, c), 1)


def _gla_specs(heads, c, dk, dv, chunk):
    return [
        pl.BlockSpec((c, heads * dk), lambda n: (chunk(n), 0)),
        pl.BlockSpec((c, heads * dk), lambda n: (chunk(n), 1)),
        pl.BlockSpec((c, heads * dv), lambda n: (chunk(n), 1)),
        pl.BlockSpec((c, LANE), lambda n: (chunk(n), 0)),
        pl.BlockSpec((LANE, heads * dk), lambda n: (0, 0)),
        pl.BlockSpec((1, heads * dk), lambda n: (0, 0)),
    ]


def _gla_fwd(proj, a_tail, wg_p, bg, name):
    s = proj.shape[0]
    heads, c = GLA_HEADS, GLA_CHUNK
    dk = wg_p.shape[1] // heads
    dv = 2 * dk
    n_chunks = s // c
    scale = dk ** -0.5

    def body(q_ref, k_ref, v_ref, a_ref, wg_ref, bg_ref, o_ref, st_ref, state):
        @pl.when(pl.program_id(0) == 0)
        def _():
            state[...] = jnp.zeros_like(state)

        a = a_ref[...]
        for h in range(heads):
            sk, sv = slice(h * dk, (h + 1) * dk), slice(h * dv, (h + 1) * dv)
            g = _gla_gates(q_ref[:, sk], k_ref[:, sk], a, wg_ref[:, sk], bg_ref[:, sk], scale, c)
            v = v_ref[:, sv]
            st = state[h]
            attn = jnp.where(_causal(c), _dot(g["q_dec"], g["k_inv"], tb=True), 0.0)
            o_ref[:, sv] = _dot(attn, v) + _dot(g["q_dec"], st, tb=True)
            st_ref[h] = st.astype(st_ref.dtype)
            state[h] = g["dec"] * st + _dot(v, g["k_end"], ta=True)

    return _pcall(
        body, name=name, grid=(n_chunks,),
        in_specs=_gla_specs(heads, c, dk, dv, lambda n: n),
        out_specs=[pl.BlockSpec((c, heads * dv), lambda n: (n, 0)),
                   pl.BlockSpec((heads, None, dv, dk), lambda n: (0, n, 0, 0))],
        out_shape=[jax.ShapeDtypeStruct((s, heads * dv), F32),
                   jax.ShapeDtypeStruct((heads, n_chunks, dv, dk), BF16)],
        scratch_shapes=[pltpu.VMEM((heads, dv, dk), F32)],
        compiler_params=_params(1),
    )(proj, proj, proj, a_tail, wg_p, bg)


def _gla_bwd(proj, a_tail, wg_p, bg, states, d_o, name):
    s = proj.shape[0]
    heads, c = GLA_HEADS, GLA_CHUNK
    dk = wg_p.shape[1] // heads
    dv = 2 * dk
    n_chunks = s // c
    scale = dk ** -0.5

    def body(q_ref, k_ref, v_ref, a_ref, wg_ref, bg_ref, st_ref, do_ref, dq_ref, dk_ref, dv_ref, dga_ref, dstate):
        @pl.when(pl.program_id(0) == 0)
        def _():
            dstate[...] = jnp.zeros_like(dstate)

        a = a_ref[...]
        mask = _causal(c)
        for h in range(heads):
            sk, sv = slice(h * dk, (h + 1) * dk), slice(h * dv, (h + 1) * dv)
            g = _gla_gates(q_ref[:, sk], k_ref[:, sk], a, wg_ref[:, sk], bg_ref[:, sk], scale, c)
            v, st, dst, d_out = v_ref[:, sv], st_ref[h], dstate[h], do_ref[:, sv]
            q_dec, k_inv, k_end = g["q_dec"], g["k_inv"], g["k_end"]
            attn = jnp.where(mask, _dot(q_dec, k_inv, tb=True), 0.0)
            d_attn = jnp.where(mask, _dot(d_out, v, tb=True), 0.0)
            d_qdec = _dot(d_attn, k_inv) + _dot(d_out, st)
            d_kinv = _dot(d_attn, q_dec, ta=True)
            d_kend = _dot(v, dst)
            dv_ref[:, sv] = (_dot(attn, d_out, ta=True) + _dot(k_end, dst, tb=True)).astype(dv_ref.dtype)
            d_dec = jnp.sum(dst * st.astype(F32), axis=0, keepdims=True)
            dstate[h] = g["dec"] * dst + _dot(d_out, q_dec, ta=True)

            dq_ref[:, sk] = (d_qdec * (scale * g["eb"])).astype(dq_ref.dtype)
            dk_ref[:, sk] = (d_kinv * g["enb"] + d_kend * g["eend"]).astype(dk_ref.dtype)
            kk = d_kend * k_end
            db = d_qdec * q_dec - d_kinv * k_inv - kk
            dbl = jnp.sum(kk, axis=0, keepdims=True) + d_dec * g["dec"]
            last = lax.broadcasted_iota(jnp.int32, db.shape, 0) == c - 1
            db = db + jnp.where(last, dbl, 0.0)
            dla = _tri_matmul(_tri(c, upper=True), db)
            dga_ref[:, sk] = dla * (1.0 / GLA_TAU) * _sigmoid(-g["ga"])

    chunk = lambda n: n_chunks - 1 - n
    rev = lambda n: (chunk(n), 0)
    return _pcall(
        body, name=name, grid=(n_chunks,),
        in_specs=_gla_specs(heads, c, dk, dv, chunk) + [
            pl.BlockSpec((heads, None, dv, dk), lambda n: (0, chunk(n), 0, 0)),
            pl.BlockSpec((c, heads * dv), rev)],
        out_specs=[pl.BlockSpec((c, heads * dk), rev), pl.BlockSpec((c, heads * dk), rev),
                   pl.BlockSpec((c, heads * dv), rev), pl.BlockSpec((c, heads * dk), rev)],
        out_shape=[jax.ShapeDtypeStruct((s, heads * dk), BF16), jax.ShapeDtypeStruct((s, heads * dk), BF16),
                   jax.ShapeDtypeStruct((s, heads * dv), BF16), jax.ShapeDtypeStruct((s, heads * dk), F32)],
        scratch_shapes=[pltpu.VMEM((heads, dv, dk), F32)],
        compiler_params=_params(1),
    )(proj, proj, proj, a_tail, wg_p, bg, states, d_o)


def _gla_post_fwd(o, r, gn, name):
    dvt = o.shape[1]
    dv = dvt // GLA_HEADS

    def fn(o, r, gn):
        outs = []
        for h in range(GLA_HEADS):
            sl = slice(h * dv, (h + 1) * dv)
            ohat, _ = _norm_stats(o[:, sl])
            outs.append((ohat * gn[:, sl]) * _silu(r[:, sl]))
        return (jnp.concatenate(outs, axis=1),)

    return _rowwise(fn, [("row", o), r, ("full", gn)], [("row", dvt, BF16)], name=name)[0]


def _gla_post_bwd(o, r, gn, dog, name):
    dvt = o.shape[1]
    dv = dvt // GLA_HEADS

    def fn(o, r, gn, dog):
        d_o, d_r, d_g = [], [], []
        for h in range(GLA_HEADS):
            sl = slice(h * dv, (h + 1) * dv)
            ohat, rstd = _norm_stats(o[:, sl])
            g, rr, dd = gn[:, sl], r[:, sl], dog[:, sl]
            d_r.append(dd * (ohat * g) * _dsilu(rr))
            don = dd * _silu(rr)
            d_g.append(_colsum(don * ohat))
            d_o.append(_norm_bwd(don * g, ohat, rstd))
        return jnp.concatenate(d_o, axis=1), jnp.concatenate(d_r, axis=1), jnp.concatenate(d_g, axis=1)

    return _rowwise(fn, [("row", o), r, ("full", gn), ("row", dog)],
                    [("row", dvt, F32), ("row", dvt, BF16), ("acc", dvt, F32)], name=name)


def _fox_prep(q, k, v, qg, kg, d, hd, name):
    heads = d // hd
    scale = hd ** -0.5

    def fn(q, k, v, qg, kg):
        qs, ks = [], []
        for h in range(heads):
            sl = slice(h * hd, (h + 1) * hd)
            qs.append(_norm_stats(q[:, sl])[0] * qg * scale)
            ks.append(_norm_stats(k[:, sl])[0] * kg)
        return jnp.concatenate(qs, axis=1), jnp.concatenate(ks, axis=1), v

    return _rowwise(fn, [q, k, v, ("full", qg), ("full", kg)],
                    [("row", d, BF16)] * 3, name=name)


def _fox_prep_bwd(q, k, dqn, dkn, qg, kg, hd, name):
    d = dqn.shape[1]
    heads = d // hd
    scale = hd ** -0.5

    def fn(q, k, dqn, dkn, qg, kg):
        dq, dk, gq, gk = [], [], [], []
        for h in range(heads):
            sl = slice(h * hd, (h + 1) * hd)
            for x, dxn, g, s, dl, gl in ((q, dqn, qg, scale, dq, gq), (k, dkn, kg, 1.0, dk, gk)):
                xhat, rstd = _norm_stats(x[:, sl])
                dn = dxn[:, sl] * s
                gl.append(_colsum(dn * xhat))
                dl.append(_norm_bwd(dn * g, xhat, rstd))
        cat = lambda t: jnp.concatenate(t, axis=1)
        return cat(dq), cat(dk), cat(gq), cat(gk)

    return _rowwise(fn, [q, k, ("row", dqn), ("row", dkn), ("full", qg), ("full", kg)],
                    [("row", d, BF16), ("row", d, BF16), ("acc", d, F32), ("acc", d, F32)], name=name)


def _fox_cum(fl, bf_p, name, tb=256):
    s = fl.shape[0]
    tb = _tile(s, tb)

    def body(fl_ref, bf_ref, cum_ref, carry):
        @pl.when(pl.program_id(0) == 0)
        def _():
            carry[...] = jnp.zeros_like(carry)

        lf = _log_sigmoid(fl_ref[...] + bf_ref[...])
        cum_ref[...] = _tri_matmul(_tri(tb), lf) + carry[...]
        carry[...] += _colsum(lf)

    return _pcall(
        body, name=name, grid=(s // tb,),
        in_specs=[pl.BlockSpec((tb, LANE), lambda i: (i, 0)), pl.BlockSpec((1, LANE), lambda i: (0, 0))],
        out_specs=pl.BlockSpec((tb, LANE), lambda i: (i, 0)),
        out_shape=jax.ShapeDtypeStruct((s, LANE), F32),
        scratch_shapes=[pltpu.VMEM((1, LANE), F32)],
        compiler_params=_params(1),
    )(fl, bf_p)


def _fox_cum_bwd(dcum, fl, bf_p, name, tb=256):
    s = fl.shape[0]
    tb = _tile(s, tb)
    nb = s // tb

    def body(dc_ref, fl_ref, bf_ref, dfl_ref, dbf_ref, carry):
        @pl.when(pl.program_id(0) == 0)
        def _():
            carry[...] = jnp.zeros_like(carry)
            dbf_ref[...] = jnp.zeros_like(dbf_ref)

        dc = dc_ref[...]
        dlf = _tri_matmul(_tri(tb, upper=True), dc) + carry[...]
        carry[...] += _colsum(dc)
        dfl = dlf * _sigmoid(-(fl_ref[...] + bf_ref[...]))
        dfl_ref[...] = dfl
        dbf_ref[...] += _colsum(dfl)

    rev = lambda i: (nb - 1 - i, 0)
    return _pcall(
        body, name=name, grid=(nb,),
        in_specs=[pl.BlockSpec((tb, LANE), rev), pl.BlockSpec((tb, LANE), rev), pl.BlockSpec((1, LANE), lambda i: (0, 0))],
        out_specs=[pl.BlockSpec((tb, LANE), rev), pl.BlockSpec((1, LANE), lambda i: (0, 0))],
        out_shape=[jax.ShapeDtypeStruct((s, LANE), F32), jax.ShapeDtypeStruct((1, LANE), F32)],
        scratch_shapes=[pltpu.VMEM((1, LANE), F32)],
        compiler_params=_params(1),
    )(dcum, fl, bf_p)


def _fox_attn_fwd(qn, kn, vb, cum_col, cum_row, hd, t, name):
    s, d = qn.shape
    heads = d // hd
    nq = s // t

    def body(q_ref, k_ref, v_ref, cc_ref, cr_ref, o_ref, lse_ref):
        qi = pl.program_id(1)
        q = q_ref[...]
        cq = cc_ref[...]
        qpos = qi * t + lax.broadcasted_iota(jnp.int32, (t, 1), 0)

        def step(kj, carry, diagonal=False):
            m, l, acc = carry
            off = pl.multiple_of(kj * t, t)
            ks, vs = k_ref[pl.ds(off, t), :], v_ref[pl.ds(off, t), :]
            sc = _dot(q, ks, tb=True) + cq - cr_ref[kj]
            if diagonal:
                kpos = off + lax.broadcasted_iota(jnp.int32, (1, t), 1)
                sc = jnp.where(kpos <= qpos, sc, NEG)
            m_new = jnp.maximum(m, jnp.max(sc, axis=1, keepdims=True))
            alpha = jnp.exp(m - m_new)
            p = jnp.exp(sc - m_new)
            return m_new, alpha * l + jnp.sum(p, axis=1, keepdims=True), alpha * acc + _dot(p, vs)

        init = (jnp.full((t, 1), NEG, F32), jnp.zeros((t, 1), F32), jnp.zeros((t, hd), F32))
        m, l, acc = step(qi, lax.fori_loop(0, qi, step, init), diagonal=True)
        o_ref[...] = acc / l
        lse_ref[...] = m + jnp.log(l)

    return _pcall(
        body, name=name, grid=(heads, nq),
        in_specs=[pl.BlockSpec((t, hd), lambda h, i: (i, h)),
                  pl.BlockSpec((s, hd), lambda h, i: (0, h)),
                  pl.BlockSpec((s, hd), lambda h, i: (0, h)),
                  pl.BlockSpec((None, t, 1), lambda h, i: (h, i, 0)),
                  pl.BlockSpec((None, nq, 1, t), lambda h, i: (h, 0, 0, 0))],
        out_specs=[pl.BlockSpec((t, hd), lambda h, i: (i, h)), pl.BlockSpec((None, t, 1), lambda h, i: (h, i, 0))],
        out_shape=[jax.ShapeDtypeStruct((s, d), F32), jax.ShapeDtypeStruct((heads, s, 1), F32)],
        compiler_params=_params(2),
    )(qn, kn, vb, cum_col, cum_row)


def _fox_attn_bwd(qn, kn, vb, d_o, o, lse, cum_col, cum_row, hd, t, name):
    s, d = qn.shape
    heads = d // hd
    nq = s // t

    def body(q_ref, k_ref, v_ref, do_ref, o_ref, lse_ref, cc_ref, cr_ref,
             dq_ref, dk_ref, dv_ref, dcq_ref, dck_ref, delta):
        kj = pl.program_id(1)

        @pl.when(kj == 0)
        def _():
            dq_ref[...] = jnp.zeros_like(dq_ref)
            dcq_ref[...] = jnp.zeros_like(dcq_ref)
            delta[...] = jnp.sum(do_ref[...] * o_ref[...], axis=1, keepdims=True)

        ks, vs, cr = k_ref[...], v_ref[...], cr_ref[...]
        kpos = kj * t + lax.broadcasted_iota(jnp.int32, (1, t), 1)

        def step(qi, carry, diagonal=False):
            dk, dv, dck = carry
            rows = pl.ds(pl.multiple_of(qi * t, t), t)
            q, d_out = q_ref[rows, :], do_ref[rows, :]
            sc = _dot(q, ks, tb=True) + cc_ref[rows, :] - cr
            p = jnp.exp(sc - lse_ref[rows, :])
            if diagonal:
                qpos = qi * t + lax.broadcasted_iota(jnp.int32, (t, 1), 0)
                p = jnp.where(kpos <= qpos, p, 0.0)
            ds = p * (_dot(d_out, vs, tb=True) - delta[rows, :])
            dq_ref[rows, :] += _dot(ds, ks)
            dcq_ref[rows, :] += jnp.sum(ds, axis=1, keepdims=True)
            return dk + _dot(ds, q, ta=True), dv + _dot(p, d_out, ta=True), dck + _colsum(ds)

        init = (jnp.zeros((t, hd), F32), jnp.zeros((t, hd), F32), jnp.zeros((1, t), F32))
        dk, dv, dck = lax.fori_loop(kj + 1, nq, step, step(kj, init, diagonal=True))
        dk_ref[...] = dk.astype(dk_ref.dtype)
        dv_ref[...] = dv.astype(dv_ref.dtype)
        dck_ref[...] = dck

    head_rows = lambda h, j: (0, h)
    blk = lambda h, j: (j, h)
    return _pcall(
        body, name=name, grid=(heads, nq),
        in_specs=[pl.BlockSpec((s, hd), head_rows), pl.BlockSpec((t, hd), blk), pl.BlockSpec((t, hd), blk),
                  pl.BlockSpec((s, hd), head_rows), pl.BlockSpec((s, hd), head_rows),
                  pl.BlockSpec((None, s, 1), lambda h, j: (h, 0, 0)),
                  pl.BlockSpec((None, s, 1), lambda h, j: (h, 0, 0)),
                  pl.BlockSpec((None, None, 1, t), lambda h, j: (h, j, 0, 0))],
        out_specs=[pl.BlockSpec((s, hd), head_rows), pl.BlockSpec((t, hd), blk), pl.BlockSpec((t, hd), blk),
                   pl.BlockSpec((None, s, 1), lambda h, j: (h, 0, 0)),
                   pl.BlockSpec((None, None, 1, t), lambda h, j: (h, j, 0, 0))],
        out_shape=[jax.ShapeDtypeStruct((s, d), F32), jax.ShapeDtypeStruct((s, d), BF16),
                   jax.ShapeDtypeStruct((s, d), BF16), jax.ShapeDtypeStruct((heads, s, 1), F32),
                   jax.ShapeDtypeStruct((heads, nq, 1, t), F32)],
        scratch_shapes=[pltpu.VMEM((s, 1), F32)],
        compiler_params=_params(2),
    )(qn, kn, vb, d_o, o, lse, cum_col, cum_row)


def _fox_gate_fwd(o, og, name):
    def fn(o, og):
        return (o * _sigmoid(og),)

    return _rowwise(fn, [("row", o), og], [("row", o.shape[1], BF16)], name=name)[0]


def _fox_gate_bwd(o, og, dact, name):
    def fn(o, og, dact):
        sg = _sigmoid(og)
        return dact * sg, dact * o * sg * (1.0 - sg)

    d = o.shape[1]
    return _rowwise(fn, [("row", o), og, ("row", dact)], [("row", d, F32), ("row", d, BF16)], name=name)


def _shift_down(x, n):
    rows = lax.broadcasted_iota(jnp.int32, x.shape, 0)
    return jnp.where(rows >= n, pltpu.roll(x, n, 0), 0.0)


def _shift_up(x, n):
    rows = lax.broadcasted_iota(jnp.int32, x.shape, 0)
    return jnp.where(rows < x.shape[0] - n, pltpu.roll(x, x.shape[0] - n, 0), 0.0)


def _conv(u, w_ref, b):
    return w_ref[0:1, :] * _shift_down(u, 2) + w_ref[1:2, :] * _shift_down(u, 1) + w_ref[2:3, :] * u + b


def _conv_act_fwd(u, cw, cb, name, tc=256):
    s, two_f = u.shape
    dff = two_f // 2
    tc = _tile(dff, tc)
    nb = dff // tc

    def body(ug_ref, uv_ref, wg_ref, wv_ref, bg_ref, bv_ref, a_ref):
        gate = _conv(ug_ref[...], wg_ref, bg_ref[...])
        val = _conv(uv_ref[...], wv_ref, bv_ref[...])
        a_ref[...] = (_silu(gate) * val).astype(a_ref.dtype)

    lo, hi = (lambda j: (0, j)), (lambda j: (0, j + nb))
    return _pcall(
        body, name=name, grid=(nb,),
        in_specs=[pl.BlockSpec((s, tc), lo), pl.BlockSpec((s, tc), hi), pl.BlockSpec((3, tc), lo),
                  pl.BlockSpec((3, tc), hi), pl.BlockSpec((1, tc), lo), pl.BlockSpec((1, tc), hi)],
        out_specs=pl.BlockSpec((s, tc), lo),
        out_shape=jax.ShapeDtypeStruct((s, dff), BF16),
        compiler_params=_params(1),
    )(u, u, cw, cw, cb, cb)


def _conv_act_bwd(u, cw, cb, da, name, tc=128):
    s, two_f = u.shape
    dff = two_f // 2
    tc = _tile(dff, tc)
    nb = dff // tc

    def body(ug_ref, uv_ref, wg_ref, wv_ref, bg_ref, bv_ref, da_ref, du_ref, dw_ref, db_ref):
        ug, uv, da = ug_ref[...], uv_ref[...], da_ref[...]
        gate = _conv(ug, wg_ref, bg_ref[...])
        val = _conv(uv, wv_ref, bv_ref[...])
        sg = _sigmoid(gate)
        d_val = da * (gate * sg)
        d_gate = da * val * (sg * (1.0 + gate * (1.0 - sg)))
        for half, (dc, uu, w_ref) in enumerate(((d_gate, ug, wg_ref), (d_val, uv, wv_ref))):
            du = w_ref[0:1, :] * _shift_up(dc, 2) + w_ref[1:2, :] * _shift_up(dc, 1) + w_ref[2:3, :] * dc
            du_ref[half] = du.astype(du_ref.dtype)
            dw_ref[half, 0:1, :] = _colsum(dc * _shift_down(uu, 2))
            dw_ref[half, 1:2, :] = _colsum(dc * _shift_down(uu, 1))
            dw_ref[half, 2:3, :] = _colsum(dc * uu)
            db_ref[half] = _colsum(dc)

    lo, hi = (lambda j: (0, j)), (lambda j: (0, j + nb))
    both = lambda j: (0, 0, j)
    return _pcall(
        body, name=name, grid=(nb,),
        in_specs=[pl.BlockSpec((s, tc), lo), pl.BlockSpec((s, tc), hi), pl.BlockSpec((3, tc), lo),
                  pl.BlockSpec((3, tc), hi), pl.BlockSpec((1, tc), lo), pl.BlockSpec((1, tc), hi),
                  pl.BlockSpec((s, tc), lo)],
        out_specs=[pl.BlockSpec((2, s, tc), both), pl.BlockSpec((2, 3, tc), both), pl.BlockSpec((2, 1, tc), both)],
        out_shape=[jax.ShapeDtypeStruct((2, s, dff), BF16), jax.ShapeDtypeStruct((2, 3, dff), F32),
                   jax.ShapeDtypeStruct((2, 1, dff), F32)],
        compiler_params=_params(1),
    )(u, u, cw, cw, cb, cb, da)


def _adamw_math(w, g, m, v):
    m = ADAM_B1 * m + (1.0 - ADAM_B1) * g
    v = ADAM_B2 * v + (1.0 - ADAM_B2) * (g * g)
    m_hat = m / (1.0 - ADAM_B1 ** ADAM_STEP)
    v_hat = v / (1.0 - ADAM_B2 ** ADAM_STEP)
    delta = -ADAM_LR * (m_hat / (jnp.sqrt(v_hat) + ADAM_EPS) + ADAM_WD * w)
    return delta, m, v


def _update_tiles(r, c, tr):
    tc = c
    if r % 8:
        tr, tc = r, _tile(c, max(LANE, 512 * 1024 // r // LANE * LANE))
    elif r <= tr:
        tr = r
    while r % tr:
        tr -= 8
    return tr, tc


def _adamw(w, g, m, v, name, tr=128):
    layers, r, c = w.shape
    tr, tc = _update_tiles(r, c, tr)

    def body(w_ref, g_ref, m_ref, v_ref, go_ref, d_ref, mo_ref, vo_ref):
        grad = g_ref[...]
        delta, m_new, v_new = _adamw_math(w_ref[...], grad, m_ref[...], v_ref[...])
        go_ref[...], d_ref[...], mo_ref[...], vo_ref[...] = grad, delta, m_new, v_new

    spec = pl.BlockSpec((None, tr, tc), lambda l, i, j: (l, i, j))
    return _pcall(
        body, name=name, grid=(layers, r // tr, c // tc), in_specs=[spec] * 4, out_specs=[spec] * 4,
        out_shape=[jax.ShapeDtypeStruct((layers, r, c), F32)] * 4, compiler_params=_params(3),
    )(w, g, m, v)


def _adamw_pieces(w, lands, sums, chip, m, v, name, tr=128):
    layers, r, c = w.shape
    tr, tc = _update_tiles(r, c, tr)
    nr, nc = r // tr, c // tc

    def body(chip_ref, w_ref, *rest):
        land_refs, own_refs = rest[:layers], rest[layers:2 * layers]
        m_ref, v_ref, go_ref, d_ref, mo_ref, vo_ref = rest[2 * layers:]
        for layer in range(layers):
            @pl.when(pl.program_id(0) == layer)
            def _(land_ref=land_refs[layer], own_ref=own_refs[layer]):
                grad = jnp.zeros(w_ref.shape, F32)
                for q in range(4):
                    grad = grad + jnp.where(chip_ref[0] == q, own_ref[...], land_ref[q]).astype(F32)
                delta, m_new, v_new = _adamw_math(w_ref[...], grad, m_ref[...], v_ref[...])
                go_ref[...], d_ref[...], mo_ref[...], vo_ref[...] = grad, delta, m_new, v_new

    def walk(k, l, i, j):
        here = l == k
        return jnp.where(here, i, jnp.where(l < k, 0, nr - 1)), jnp.where(here, j, jnp.where(l < k, 0, nc - 1))

    spec = pl.BlockSpec((None, tr, tc), lambda l, i, j, chip_ref: (l, i, j))
    land_specs = [pl.BlockSpec((4, tr, tc), lambda l, i, j, chip_ref, k=k: (0,) + walk(k, l, i, j))
                  for k in range(layers)]
    own_specs = [pl.BlockSpec((None, tr, tc), lambda l, i, j, chip_ref, k=k: (chip_ref[0],) + walk(k, l, i, j))
                 for k in range(layers)]
    return _pcall(
        body, name=name,
        grid_spec=pltpu.PrefetchScalarGridSpec(
            num_scalar_prefetch=1, grid=(layers, nr, nc),
            in_specs=[spec] + land_specs + own_specs + [spec, spec], out_specs=[spec] * 4),
        out_shape=[jax.ShapeDtypeStruct((layers, r, c), F32)] * 4, compiler_params=_params(3),
    )(chip, w, *lands, *sums, m, v)


def _pair_sum(pieces, partner, core, name, tr=512):
    _, r, c = pieces.shape
    tc = c
    if r % 8:
        tr, tc = r, _tile(c, max(LANE, 1024 * 1024 // r // LANE * LANE))
    elif r <= tr:
        tr = r
    while r % tr:
        tr -= 8

    def body(core_ref, mine_ref, partner_ref, out_ref):
        out_ref[...] = (mine_ref[...].astype(F32) + partner_ref[...].astype(F32)).astype(out_ref.dtype)

    return _pcall(
        body, name=name,
        grid_spec=pltpu.PrefetchScalarGridSpec(
            num_scalar_prefetch=1, grid=(4, r // tr, c // tc),
            in_specs=[pl.BlockSpec((None, tr, tc), lambda q, i, j, core_ref: (2 * q + core_ref[0], i, j)),
                      pl.BlockSpec((None, tr, tc), lambda q, i, j, core_ref: (q, i, j))],
            out_specs=pl.BlockSpec((None, tr, tc), lambda q, i, j, core_ref: (q, i, j))),
        out_shape=jax.ShapeDtypeStruct((4, r, c), pieces.dtype), compiler_params=_params(3),
    )(core, pieces, partner)


def _sum8(x, name):
    p = x.shape[2]
    tp = _tile(p, 16 * 1024)

    def body(x_ref, o_ref):
        acc = x_ref[0]
        for i in range(1, N_DEV):
            acc = acc + x_ref[i]
        o_ref[...] = acc

    return _pcall(
        body, name=name, grid=(p // tp,), in_specs=[pl.BlockSpec((N_DEV, 1, tp), lambda i: (0, 0, i))],
        out_specs=pl.BlockSpec((1, tp), lambda i: (0, i)), out_shape=jax.ShapeDtypeStruct((1, p), x.dtype),
        compiler_params=_params(1),
    )(x)


def _exchange(arrays, name, scatter):
    n = len(arrays)
    hbm = pl.BlockSpec(memory_space=pl.ANY)

    def body(*refs):
        ins, outs, token = refs[:n], refs[n:2 * n], refs[2 * n]
        send_sems, recv_sems, local_sems = refs[2 * n + 1:]
        token[...] = jnp.zeros_like(token)
        x, y, c = lax.axis_index("x"), lax.axis_index("y"), lax.axis_index("c")
        me = 4 * x + 2 * y + c
        copies = []
        for a in range(n):
            src_mine = ins[a].at[me] if scatter else ins[a]
            local = pltpu.make_async_copy(src_mine, outs[a].at[me], local_sems.at[a])
            local.start()
            copies.append(local)
            for k in range(1, N_DEV):
                px = 1 - x if k & 4 else x
                py = 1 - y if k & 2 else y
                pc = 1 - c if k & 1 else c
                src = ins[a].at[4 * px + 2 * py + pc] if scatter else ins[a]
                cp = pltpu.make_async_remote_copy(
                    src_ref=src, dst_ref=outs[a].at[me],
                    send_sem=send_sems.at[a * (N_DEV - 1) + k - 1], recv_sem=recv_sems.at[a * (N_DEV - 1) + k - 1],
                    device_id=(px, py, pc), device_id_type=pl.DeviceIdType.MESH)
                cp.start()
                copies.append(cp)
        for cp in copies:
            cp.wait()

    out_shape = [jax.ShapeDtypeStruct(a.shape if scatter else (N_DEV,) + a.shape, a.dtype) for a in arrays]
    res = _pcall(
        body, name=name, in_specs=[hbm] * n, out_specs=[hbm] * n + [pl.BlockSpec(memory_space=pltpu.VMEM)],
        out_shape=out_shape + [jax.ShapeDtypeStruct((8, LANE), F32)],
        scratch_shapes=[pltpu.SemaphoreType.DMA((n * (N_DEV - 1),)), pltpu.SemaphoreType.DMA((n * (N_DEV - 1),)),
                        pltpu.SemaphoreType.DMA((n,))],
        compiler_params=pltpu.CompilerParams(has_side_effects=True),
    )(*arrays)
    return res[:n], res[n][0, 0]


_HBM = pl.BlockSpec(memory_space=pltpu.HBM)
_SEM = pl.BlockSpec(memory_space=pltpu.SEMAPHORE)
_DATAFLOW = pltpu.SideEffectType.DATAFLOW_SIDE_EFFECTING


def _peer(k, x, y, c):
    return (1 - x if k & 4 else x, 1 - y if k & 2 else y, 1 - c if k & 1 else c)


def _pair_plan(x, y, c):
    return [(2 * q + (1 - c), q, (x, y, 1 - c)) for q in range(4)]


def _chip_plan(x, y, c):
    out = []
    for k in _ICI_PEERS:
        px, py, pc = _peer(k, x, y, c)
        out.append((2 * px + py, 2 * x + y, (px, py, pc)))
    return out


def _all_plan(x, y, c):
    return [(0, 4 * x + 2 * y + c, _peer(k, x, y, c)) for k in range(1, N_DEV)]


def _split_start(arrays, plan, name, land_blocks=4):
    n = len(arrays)
    lands = [lax.empty((land_blocks,) + a.shape[1:], a.dtype) for a in arrays]
    n_copies = len(plan(0, 0, 0))

    def body(*refs):
        srcs, dsts = refs[:n], refs[n:2 * n]
        send_sems, recv_sems, token = refs[4 * n:5 * n], refs[5 * n:6 * n], refs[6 * n]
        copies = plan(lax.axis_index("x"), lax.axis_index("y"), lax.axis_index("c"))
        for a in range(n):
            for j, (src_block, dst_block, peer) in enumerate(copies):
                pltpu.make_async_remote_copy(
                    src_ref=srcs[a].at[src_block], dst_ref=dsts[a].at[dst_block],
                    send_sem=send_sems[a].at[j], recv_sem=recv_sems[a].at[j],
                    device_id=peer, device_id_type=pl.DeviceIdType.MESH).start()
        token[...] = jnp.zeros_like(token)

    sems = [pltpu.SemaphoreType.DMA((n_copies,))] * (2 * n)
    res = _pcall(
        body, name=name,
        in_specs=[_HBM] * (2 * n),
        out_specs=[_HBM] * (2 * n) + [_SEM] * (2 * n) + [pl.BlockSpec(memory_space=pltpu.VMEM)],
        out_shape=[pltpu.HBM(a.shape, a.dtype) for a in arrays] + [pltpu.HBM(l.shape, l.dtype) for l in lands]
        + sems + [jax.ShapeDtypeStruct((8, LANE), F32)],
        input_output_aliases={i: i for i in range(2 * n)},
        compiler_params=pltpu.CompilerParams(has_side_effects=_DATAFLOW),
    )(*[pltpu.with_memory_space_constraint(a, pltpu.HBM) for a in arrays],
      *[pltpu.with_memory_space_constraint(l, pltpu.HBM) for l in lands])
    handles = [(res[a], res[n + a], res[2 * n + a], res[3 * n + a]) for a in range(n)]
    return handles, res[4 * n][0, 0]


def _split_wait(handles, plan, after, name):
    n = len(handles)
    after = list(after) if isinstance(after, (list, tuple)) else [after]

    def body(*refs):
        srcs, dsts = refs[:n], refs[n:2 * n]
        send_sems, recv_sems = refs[2 * n:3 * n], refs[3 * n:4 * n]
        copies = plan(lax.axis_index("x"), lax.axis_index("y"), lax.axis_index("c"))
        for a in range(n):
            for j, (src_block, dst_block, peer) in enumerate(copies):
                cp = pltpu.make_async_remote_copy(
                    src_ref=srcs[a].at[src_block], dst_ref=dsts[a].at[dst_block],
                    send_sem=send_sems[a].at[j], recv_sem=recv_sems[a].at[j],
                    device_id=peer, device_id_type=pl.DeviceIdType.MESH)
                cp.wait_send()
                cp.wait_recv()

    srcs, lands = [h[0] for h in handles], [h[1] for h in handles]
    res = _pcall(
        body, name=name,
        in_specs=[_HBM] * (2 * n) + [_SEM] * (2 * n) + [pl.BlockSpec(memory_space=pl.ANY)] * len(after),
        out_specs=[_HBM] * (2 * n),
        out_shape=[pltpu.HBM(t.shape, t.dtype) for t in srcs + lands],
        input_output_aliases={i: i for i in range(2 * n)},
        compiler_params=pltpu.CompilerParams(has_side_effects=_DATAFLOW),
    )(*srcs, *lands, *[h[2] for h in handles], *[h[3] for h in handles], *after)
    return res[:n], res[n:]


_ICI_PEERS = (2, 4, 6)


def _gather2_start(shards, name):
    n = len(shards)
    lands = [lax.empty((N_DEV,) + a.shape, a.dtype) for a in shards]

    def body(*refs):
        srcs, dsts = refs[:n], refs[n:2 * n]
        send_sems, d2d_sems, ici_sems = refs[4 * n:5 * n], refs[5 * n:6 * n], refs[6 * n:7 * n]
        token = refs[7 * n]
        x, y, c = lax.axis_index("x"), lax.axis_index("y"), lax.axis_index("c")
        me = 4 * x + 2 * y + c
        for a in range(n):
            for j, k in enumerate((1,) + _ICI_PEERS):
                recv = d2d_sems[a].at[0] if j == 0 else ici_sems[a].at[j - 1]
                pltpu.make_async_remote_copy(
                    src_ref=srcs[a], dst_ref=dsts[a].at[me], send_sem=send_sems[a].at[j], recv_sem=recv,
                    device_id=_peer(k, x, y, c), device_id_type=pl.DeviceIdType.MESH).start()
        token[...] = jnp.zeros_like(token)

    dma = pltpu.SemaphoreType.DMA
    res = _pcall(
        body, name=name,
        in_specs=[_HBM] * (2 * n),
        out_specs=[_HBM] * (2 * n) + [_SEM] * (3 * n) + [pl.BlockSpec(memory_space=pltpu.VMEM)],
        out_shape=[pltpu.HBM(a.shape, a.dtype) for a in shards] + [pltpu.HBM(l.shape, l.dtype) for l in lands]
        + [dma((4,))] * n + [dma((1,))] * n + [dma((3,))] * n + [jax.ShapeDtypeStruct((8, LANE), F32)],
        input_output_aliases={i: i for i in range(2 * n)},
        compiler_params=pltpu.CompilerParams(has_side_effects=_DATAFLOW),
    )(*[pltpu.with_memory_space_constraint(a, pltpu.HBM) for a in shards],
      *[pltpu.with_memory_space_constraint(l, pltpu.HBM) for l in lands])
    handles = [tuple(res[i * n + a] for i in range(5)) for a in range(n)]
    return handles, res[5 * n][0, 0]


def _gather2_forward(handle, after, name):
    src, land, send_sems, d2d_sem, ici_sems = handle

    def body(land_ref, ici_ref, after_ref, land_out, fwd_send, fwd_recv, token):
        x, y, c = lax.axis_index("x"), lax.axis_index("y"), lax.axis_index("c")
        for j, k in enumerate(_ICI_PEERS):
            px, py, pc = _peer(k, x, y, c)
            block = land_ref.at[4 * px + 2 * py + pc]
            pltpu.make_async_remote_copy(
                src_ref=block, dst_ref=block, send_sem=fwd_send.at[j], recv_sem=ici_ref.at[j],
                device_id=(px, py, pc), device_id_type=pl.DeviceIdType.MESH).wait_recv()
            pltpu.make_async_remote_copy(
                src_ref=block, dst_ref=block, send_sem=fwd_send.at[j], recv_sem=fwd_recv.at[j],
                device_id=(x, y, 1 - c), device_id_type=pl.DeviceIdType.MESH).start()
        token[...] = jnp.zeros_like(token)

    dma = pltpu.SemaphoreType.DMA
    land, fwd_send, fwd_recv, token = _pcall(
        body, name=name,
        in_specs=[_HBM, _SEM, pl.BlockSpec(memory_space=pl.ANY)],
        out_specs=[_HBM, _SEM, _SEM, pl.BlockSpec(memory_space=pltpu.VMEM)],
        out_shape=[pltpu.HBM(land.shape, land.dtype), dma((3,)), dma((3,)), jax.ShapeDtypeStruct((8, LANE), F32)],
        input_output_aliases={0: 0},
        compiler_params=pltpu.CompilerParams(has_side_effects=_DATAFLOW),
    )(land, ici_sems, after)
    return (src, land, send_sems, d2d_sem, fwd_send, fwd_recv), token[0, 0]


def _gather2_wait(handle, after, name):
    src, land, send_sems, d2d_sem, fwd_send, fwd_recv = handle

    def body(src_ref, land_ref, send_ref, d2d_ref, fsend_ref, frecv_ref, after_ref, src_out, land_out):
        x, y, c = lax.axis_index("x"), lax.axis_index("y"), lax.axis_index("c")
        me = 4 * x + 2 * y + c
        sibling = (x, y, 1 - c)
        block = land_ref.at[me]

        def copy(send, recv):
            return pltpu.make_async_remote_copy(src_ref=src_ref, dst_ref=block, send_sem=send, recv_sem=recv,
                                                device_id=sibling, device_id_type=pl.DeviceIdType.MESH)

        for j in range(4):
            copy(send_ref.at[j], d2d_ref.at[0]).wait_send()
        copy(send_ref.at[0], d2d_ref.at[0]).wait_recv()
        for j in range(3):
            copy(fsend_ref.at[j], frecv_ref.at[j]).wait_send()
            copy(fsend_ref.at[j], frecv_ref.at[j]).wait_recv()

    res = _pcall(
        body, name=name,
        in_specs=[_HBM, _HBM, _SEM, _SEM, _SEM, _SEM, pl.BlockSpec(memory_space=pl.ANY)],
        out_specs=[_HBM, _HBM],
        out_shape=[pltpu.HBM(src.shape, src.dtype), pltpu.HBM(land.shape, land.dtype)],
        input_output_aliases={0: 0, 1: 1},
        compiler_params=pltpu.CompilerParams(has_side_effects=_DATAFLOW),
    )(src, land, send_sems, d2d_sem, fwd_send, fwd_recv, after)
    return res[0], res[1]


def _pad_cols(x, width=LANE):
    return jnp.pad(x, ((0, 0), (0, width - x.shape[1])))


def _cols_full(g):
    return jnp.transpose(g, (1, 0, 2)).reshape(g.shape[1], -1)


def _ffn_fwd(x1, p, i, tag):
    h2 = _adaln_fwd(x1, p["norm_ffn"][i], p["sc_f"][i], p["sh_f"][i], f"ffn_norm_{tag}")
    u = _matmul(h2, p["fetch"](f"up{i}", h2), name=f"ffn_up_{tag}", tn=1408, b_shards=True)
    a = _conv_act_fwd(u, p["conv_w"][i], p["conv_b"][i], f"ffn_act_{tag}")
    g_f = p["g_f"][i]
    x2, f = _matmul(a, p["fetch"](f"down{i}", a), name=f"ffn_down_{tag}", tk=1408, out_dtypes=(F32, F32),
                    epilogue=lambda acc, x1, g: (x1 + (1.0 + g) * acc, acc), extras=(("mn", x1), ("n", g_f)))
    return x2, dict(h2=h2, u=u, a=a, f=f)


def _ffn_bwd(dx2, x1, saved, p, i, tag):
    d = x1.shape[1]
    w_up, w_down = p["fetch"](f"up{i}", None), p["fetch"](f"down{i}", None)
    df, dg_f = _residual_bwd(dx2, saved["f"], p["g_f"][i], f"ffn_res_bwd_{tag}")
    da = _matmul(df, w_down, tb=True, name=f"ffn_down_dx_{tag}", tn=1408)
    dw_down = _matmul(saved["a"], df, ta=True, name=f"ffn_down_dw_{tag}", tm=1408, out_dtypes=(BF16,))
    du, dcw, dcb = _conv_act_bwd(saved["u"], p["conv_w"][i], p["conv_b"][i], da, f"ffn_act_bwd_{tag}")
    dcw, dcb = (jnp.concatenate([t[0], t[1]], axis=1) for t in (dcw, dcb))
    tok = p["flush"](du)
    dh2 = _matmul(du, w_up, tb=True, name=f"ffn_up_dx_{tag}", tk=1408, a_halves=True, b_shards=True)
    dw_up = _matmul(saved["h2"], du, ta=True, name=f"ffn_up_dw_{tag}", tn=1408, out_dtypes=(BF16,), b_halves=True,
                    out_shards=True)
    tok = tok + p["send"](f"ffn{i}", [dw_up, dw_down.reshape(N_DEV, -1, d)])
    dx1, dsh, dsc, dgain = _adaln_bwd(x1, dh2, dx2, p["norm_ffn"][i] + tok, p["sc_f"][i], f"ffn_norm_bwd_{tag}")
    grads = dict(conv_w=dcw, conv_b=dcb, norm_ffn=dgain, sh_f=dsh, sc_f=dsc, g_f=dg_f)
    return dx1, grads


def _gla_layer_fwd(x, p, i):
    h1 = _adaln_fwd(x, p["norm_mix"][i], p["sc_m"][i], p["sh_m"][i], "gla_norm")
    w_t, w_tail_t, main = p["fetch"]("gla_in", h1)
    proj = _matmul(h1, w_t, tb=True, b_rows=main, name="gla_in")
    a_tail = _matmul(h1, w_tail_t, tb=True, name="gla_in_tail")
    dk_total = p["gla_wg_p"].shape[1]
    o, states = _gla_fwd(proj, a_tail, p["gla_wg_p"], p["gla_b_gate"], "gla_chunks")
    assert 2 * dk_total == o.shape[1]
    r = ("cols", proj, 2, o.shape[1])
    og = _gla_post_fwd(o, r, p["gla_norm"], "gla_post")
    x1, y = _matmul(og, p["fetch"]("gla_out", og), name="gla_out", out_dtypes=(F32, F32),
                    epilogue=lambda acc, x, g: (x + (1.0 + g) * acc, acc), extras=(("mn", x), ("n", p["g_m"][i])))
    return x1, dict(h1=h1, proj=proj, a_tail=a_tail, o=o, r=r, states=states, og=og, y=y)


def _gla_layer_bwd(dx1, x, sv, p, i):
    d = x.shape[1]
    (w_t, w_tail_t, main), w_out = p["fetch"]("gla_in", None), p["fetch"]("gla_out", None)
    dy, dg_m = _residual_bwd(dx1, sv["y"], p["g_m"][i], "gla_res_bwd")
    dog = _matmul(dy, w_out, tb=True, name="gla_out_dx")
    dw_out = _matmul(sv["og"], dy, ta=True, name="gla_out_dw", out_dtypes=(BF16,))
    tok = p["flush"](dog) + p["send"]("gla_out", [dw_out.reshape(N_DEV, -1, d)])
    d_o, d_r, dgn = _gla_post_bwd(sv["o"], sv["r"], p["gla_norm"] + tok, dog, "gla_post_bwd")
    dq, dk, dv, dga = _gla_bwd(sv["proj"], sv["a_tail"], p["gla_wg_p"], p["gla_b_gate"], sv["states"], d_o,
                               "gla_chunks_bwd")
    tok = p["flush"](dga)
    da_tail = _matmul(dga, p["gla_wg_p"], tb=True, name="gla_gate_dx", out_dtypes=(BF16,))
    dwg = _matmul(sv["a_tail"], dga, ta=True, name="gla_gate_dw")
    dbg = _rowwise(lambda t: (_colsum(t),), [("row", dga)], [("acc", dga.shape[1], F32)], name="gla_gate_db")[0]
    dproj = jnp.concatenate([dq, dk, dv, d_r], axis=1)
    dh_tail = _matmul(da_tail, w_tail_t, name="gla_in_tail_dx")
    dh1 = _matmul(dproj, w_t, b_rows=main, name="gla_in_dx", tk=2048,
                  epilogue=lambda acc, t: (acc + t,), extras=(("mn", dh_tail),))
    rank = p["gla_rank"]
    dw_main = _matmul(dproj, sv["h1"], ta=True, name="gla_in_dw", out_dtypes=(BF16,), out_rows=main + rank)
    dx, dsh, dsc, dgain = _adaln_bwd(x, dh1, dx1, p["norm_mix"][i] + tok, p["sc_m"][i], "gla_norm_bwd")
    grads = dict(gla_w_gate=dwg[:rank], gla_b_gate=dbg, gla_norm=dgn, norm_mix=dgain, sh_m=dsh, sc_m=dsc, g_m=dg_m,
                 gla_w_in_unsent=(dw_main, da_tail, sv["h1"]))
    return dx, grads


def _fox_layer_fwd(x, p, i):
    d = x.shape[1]
    hd = p["fox_q_norm"].shape[1]
    heads = d // hd
    s = x.shape[0]
    t = _tile(s, 512)
    h1 = _adaln_fwd(x, p["norm_mix"][i], p["sc_m"][i], p["sh_m"][i], "fox_norm")
    w_t, w_tail_t, main = p["fetch"]("fox_in", h1)
    proj = _matmul(h1, w_t, tb=True, b_rows=main, name="fox_in")
    fl = _matmul(h1, w_tail_t, tb=True, name="fox_in_tail")
    q, k, v, og = (("cols", proj, j, d) for j in range(4))
    qn, kn, vb = _fox_prep(q, k, v, p["fox_q_norm"], p["fox_k_norm"], d, hd, "fox_prep")
    cum = _fox_cum(fl, p["fox_bf_p"], "fox_cum")
    cum_t = jnp.transpose(cum[:, :heads])
    cum_col, cum_row = cum_t[:, :, None], cum_t.reshape(heads, s // t, 1, t)
    o, lse = _fox_attn_fwd(qn, kn, vb, cum_col, cum_row, hd, t, "fox_attn")
    act = _fox_gate_fwd(o, og, "fox_gate")
    x1, y = _matmul(act, p["fetch"]("fox_out", act), name="fox_out", out_dtypes=(F32, F32),
                    epilogue=lambda acc, x, g: (x + (1.0 + g) * acc, acc), extras=(("mn", x), ("n", p["g_m"][i])))
    return x1, dict(h1=h1, q=q, k=k, og=og, fl=fl, qn=qn, kn=kn, vb=vb, cum_col=cum_col, cum_row=cum_row,
                    o=o, lse=lse, act=act, y=y, t=t, hd=hd)


def _fox_layer_bwd(dx1, x, sv, p, i):
    d = x.shape[1]
    hd, t = sv["hd"], sv["t"]
    heads = d // hd
    s = x.shape[0]
    (w_t, w_tail_t, main), w_out = p["fetch"]("fox_in", None), p["fetch"]("fox_out", None)
    dy, dg_m = _residual_bwd(dx1, sv["y"], p["g_m"][i], "fox_res_bwd")
    dact = _matmul(dy, w_out, tb=True, name="fox_out_dx")
    dw_out = _matmul(sv["act"], dy, ta=True, name="fox_out_dw", out_dtypes=(BF16,))
    d_o, d_og = _fox_gate_bwd(sv["o"], sv["og"], dact, "fox_gate_bwd")
    tok_flush = p["flush"](d_og)
    dqn, dkn, dvb, dcq, dck = _fox_attn_bwd(sv["qn"], sv["kn"], sv["vb"], d_o, sv["o"], sv["lse"], sv["cum_col"],
                                            sv["cum_row"], hd, t, "fox_attn_bwd")
    dq, dk, gq, gk = _fox_prep_bwd(sv["q"], sv["k"], dqn, dkn, p["fox_q_norm"], p["fox_k_norm"], hd, "fox_prep_bwd")
    dcum = _pad_cols(jnp.transpose(dcq[:, :, 0] - dck.reshape(heads, s)))
    dfl, dbf = _fox_cum_bwd(dcum, sv["fl"], p["fox_bf_p"], "fox_cum_bwd")
    dfl_b = dfl.astype(BF16)
    dproj = jnp.concatenate([dq, dk, dvb, d_og], axis=1)
    dh_tail = _matmul(dfl_b, w_tail_t, name="fox_in_tail_dx")
    dh1 = _matmul(dproj, w_t, b_rows=main, name="fox_in_dx", tk=2048,
                  epilogue=lambda acc, tl: (acc + tl,), extras=(("mn", dh_tail),))
    dw_main = _matmul(dproj, sv["h1"], ta=True, name="fox_in_dw", out_dtypes=(BF16,), out_rows=main + heads)
    dw_in = _tail_rows(dfl_b, sv["h1"], dw_main, heads, "fox_in_tail_dw").reshape(N_DEV, -1, d)
    tok = tok_flush + p["send"]("fox", [dw_in, dw_out.reshape(N_DEV, -1, d)])
    dx, dsh, dsc, dgain = _adaln_bwd(x, dh1, dx1, p["norm_mix"][i] + tok, p["sc_m"][i], "fox_norm_bwd")
    grads = dict(fox_b_f=dbf[:, :heads], fox_q_norm=gq.reshape(heads, hd).sum(0, keepdims=True),
                 fox_k_norm=gk.reshape(heads, hd).sum(0, keepdims=True), norm_mix=dgain, sh_m=dsh, sc_m=dsc, g_m=dg_m)
    return dx, grads


SMALL = ("b_mod", "norm_mix", "norm_ffn", "gla_b_gate", "gla_norm", "fox_b_f", "fox_q_norm", "fox_k_norm",
         "ffn_conv_b", "norm_final")
SMALL_SHARDED = ("gla_w_gate", "ffn_conv_w")
BIG = ("gla_w_in", "gla_w_out", "fox_w_in", "fox_w_out", "ffn_w_up", "ffn_w_down")
WEIGHTS = ("w_mod", "b_mod", "norm_mix", "norm_ffn", "gla_w_in", "gla_w_gate", "gla_b_gate", "gla_norm", "gla_w_out",
           "fox_w_in", "fox_b_f", "fox_q_norm", "fox_k_norm", "fox_w_out", "ffn_w_up", "ffn_conv_w", "ffn_conv_b",
           "ffn_w_down", "norm_final")


def _pack(parts):
    flat = jnp.concatenate([p.reshape(-1) for p in parts])
    pad = (-flat.shape[0]) % 1024
    return jnp.pad(flat, (0, pad)).reshape(1, -1)


def _unpack(flat, shapes):
    out, off = [], 0
    for shp in shapes:
        n = 1
        for s in shp:
            n *= s
        out.append(flat[0, off:off + n].reshape(shp))
        off += n
    return out


def kernel(x, c, w_mod, b_mod, norm_mix, norm_ffn, gla_w_in, gla_w_gate, gla_b_gate, gla_norm, gla_w_out, fox_w_in, fox_b_f, fox_q_norm, fox_k_norm, fox_w_out, ffn_w_up, ffn_conv_w, ffn_conv_b, ffn_w_down, norm_final, loss_target, m_w_mod, m_b_mod, m_norm_mix, m_norm_ffn, m_gla_w_in, m_gla_w_gate, m_gla_b_gate, m_gla_norm, m_gla_w_out, m_fox_w_in, m_fox_b_f, m_fox_q_norm, m_fox_k_norm, m_fox_w_out, m_ffn_w_up, m_ffn_conv_w, m_ffn_conv_b, m_ffn_w_down, m_norm_final, v_w_mod, v_b_mod, v_norm_mix, v_norm_ffn, v_gla_w_in, v_gla_w_gate, v_gla_b_gate, v_gla_norm, v_gla_w_out, v_fox_w_in, v_fox_b_f, v_fox_q_norm, v_fox_k_norm, v_fox_w_out, v_ffn_w_up, v_ffn_conv_w, v_ffn_conv_b, v_ffn_w_down, v_norm_final):
    w = dict(w_mod=w_mod, b_mod=b_mod, norm_mix=norm_mix, norm_ffn=norm_ffn, gla_w_in=gla_w_in, gla_w_gate=gla_w_gate,
             gla_b_gate=gla_b_gate, gla_norm=gla_norm, gla_w_out=gla_w_out, fox_w_in=fox_w_in, fox_b_f=fox_b_f,
             fox_q_norm=fox_q_norm, fox_k_norm=fox_k_norm, fox_w_out=fox_w_out, ffn_w_up=ffn_w_up,
             ffn_conv_w=ffn_conv_w, ffn_conv_b=ffn_conv_b, ffn_w_down=ffn_w_down, norm_final=norm_final)
    mom_m = dict(w_mod=m_w_mod, b_mod=m_b_mod, norm_mix=m_norm_mix, norm_ffn=m_norm_ffn, gla_w_in=m_gla_w_in,
                 gla_w_gate=m_gla_w_gate, gla_b_gate=m_gla_b_gate, gla_norm=m_gla_norm, gla_w_out=m_gla_w_out,
                 fox_w_in=m_fox_w_in, fox_b_f=m_fox_b_f, fox_q_norm=m_fox_q_norm, fox_k_norm=m_fox_k_norm,
                 fox_w_out=m_fox_w_out, ffn_w_up=m_ffn_w_up, ffn_conv_w=m_ffn_conv_w, ffn_conv_b=m_ffn_conv_b,
                 ffn_w_down=m_ffn_w_down, norm_final=m_norm_final)
    mom_v = dict(w_mod=v_w_mod, b_mod=v_b_mod, norm_mix=v_norm_mix, norm_ffn=v_norm_ffn, gla_w_in=v_gla_w_in,
                 gla_w_gate=v_gla_w_gate, gla_b_gate=v_gla_b_gate, gla_norm=v_gla_norm, gla_w_out=v_gla_w_out,
                 fox_w_in=v_fox_w_in, fox_b_f=v_fox_b_f, fox_q_norm=v_fox_q_norm, fox_k_norm=v_fox_k_norm,
                 fox_w_out=v_fox_w_out, ffn_w_up=v_ffn_w_up, ffn_conv_w=v_ffn_conv_w, ffn_conv_b=v_ffn_conv_b,
                 ffn_w_down=v_ffn_w_down, norm_final=v_norm_final)

    me = 4 * lax.axis_index("x") + 2 * lax.axis_index("y") + lax.axis_index("c")
    xs, target = x[0], loss_target[0]
    s, d = xs.shape
    depth = w_mod.shape[0]
    mod_cols = w_mod.shape[2]
    rank = gla_w_gate.shape[1]
    hd = fox_q_norm.shape[1]
    fox_heads = d // hd
    dk_total = gla_w_gate.shape[2] * N_DEV

    cond = c * (1.0 / (1.0 + jnp.exp(-c)))
    g, _ = _exchange([gla_w_gate[0], ffn_conv_w, cond], "gather_small", scatter=False)
    cond_all = g[2][:, 0, :]

    cond_pad = jnp.pad(cond_all, ((0, 16 - N_DEV), (0, 0)))
    mod_part = []
    for i in range(depth):
        b_cols = lax.dynamic_slice(b_mod[i:i + 1], (0, me * mod_cols), (1, mod_cols))
        mod_part.append(_matmul(cond_pad, w_mod[i], name=f"mod_{i}", tn=768,
                                epilogue=lambda acc, b: (acc + b,), extras=(("n", b_cols),))[:N_DEV])
    (mod_all,), tok_mod = _exchange([jnp.stack(mod_part)], "gather_mod", scatter=False)
    mod = lax.dynamic_index_in_dim(mod_all, me, axis=2, keepdims=False)
    mod = jnp.transpose(mod, (1, 0, 2)).reshape(depth, 6, 1, d)

    big_names = ["gla_in", "gla_out", "up0", "down0", "fox_in", "fox_out", "up1", "down1"]
    first = [jnp.transpose(gla_w_in[0] + tok_mod).astype(BF16), gla_w_out[0].astype(BF16)]
    handles, tok_first = _gather2_start(first, "gather_weights_start_first")
    rest = [ffn_w_up[0] + tok_first, ffn_w_down[0], jnp.transpose(fox_w_in[0]), fox_w_out[0], ffn_w_up[1],
            ffn_w_down[1]]
    handles_rest, tok0 = _gather2_start([t.astype(BF16) for t in rest], "gather_weights_start_rest")
    handles = handles + handles_rest
    ready, forwarded = {}, {}

    def split_tail(full_t, tail):
        main = full_t.shape[0] - tail
        return full_t, jnp.pad(full_t[main:], ((0, LANE - tail), (0, 0))), main

    def forward(idx, after):
        key = big_names[idx]
        forwarded[key] = _gather2_forward(handles[idx], after, f"gather_{key}_forward")

    def fetch(key, after):
        if key not in ready:
            idx = big_names.index(key)
            if idx == 0:
                forward(0, after)
            handle, _ = forwarded[key]
            mine, land = _gather2_wait(handle, after, f"gather_{key}_wait")
            if idx + 1 < len(big_names):
                forward(idx + 1, land)
                mine = mine + forwarded[big_names[idx + 1]][1].astype(BF16)
            full = lax.dynamic_update_slice(land, mine[None], (me,) + (0,) * mine.ndim)
            if key == "gla_in":
                ready[key] = split_tail(full.reshape(-1, d), rank)
            elif key == "fox_in":
                ready[key] = split_tail(full.reshape(-1, d), fox_heads)
            elif key.startswith("up"):
                ready[key] = full
            else:
                ready[key] = full.reshape(-1, d)
        return ready[key]

    pending, sent = [], {}
    core = lax.axis_index("c").astype(jnp.int32).reshape(1)
    chip = 2 * lax.axis_index("x") + lax.axis_index("y")

    def send(key, pieces):
        hs, tok = _split_start(pieces, _pair_plan, f"scatter_{key}_pair_start")
        pending.append((key, hs))
        return tok

    def flush(after):
        tok = 0.0
        while pending:
            key, hs = pending.pop(0)
            mine, partner = _split_wait(hs, _pair_plan, after, f"scatter_{key}_pair_wait")
            sums = [_pair_sum(pc, pt, core, f"scatter_{key}_pair_sum{a}")
                    for a, (pc, pt) in enumerate(zip(mine, partner))]
            sent[key], t = _split_start(sums, _chip_plan, f"scatter_{key}_chip_start")
            tok = tok + t
        return tok

    p = dict(
        fetch=fetch, send=send, flush=flush,
        gla_wg_p=jnp.pad(_cols_full(g[0]), ((0, LANE - rank), (0, 0))),
        conv_w=[jnp.transpose(g[1][:, i], (1, 0, 2)).reshape(ffn_conv_w.shape[1], -1) for i in range(depth)],
        conv_b=[ffn_conv_b[i:i + 1] for i in range(depth)],
        gla_b_gate=gla_b_gate, gla_norm=gla_norm, fox_q_norm=fox_q_norm, fox_k_norm=fox_k_norm,
        fox_bf_p=_pad_cols(fox_b_f), gla_rank=rank,
        norm_mix=[norm_mix[i:i + 1] + (tok0 if i == 0 else 0.0) for i in range(depth)],
        norm_ffn=[norm_ffn[i:i + 1] for i in range(depth)],
    )

    for j, nm in enumerate(("sh_m", "sc_m", "g_m", "sh_f", "sc_f", "g_f")):
        p[nm] = [mod[i, j] for i in range(depth)]

    acts, saved = [xs], []
    for i in range(depth):
        layer_fwd = _gla_layer_fwd if i % 2 == 0 else _fox_layer_fwd
        x1, sv_mix = layer_fwd(acts[-1], p, i)
        x2, sv_ffn = _ffn_fwd(x1, p, i, str(i))
        saved.append((acts[-1], x1, sv_mix, sv_ffn))
        acts.append(x2)
    dx, d_norm_final, loss_part = _final_loss(acts[-1], target, norm_final.reshape(1, d), "final_loss")

    lg = [None] * depth
    for i in reversed(range(depth)):
        x_in, x1, sv_mix, sv_ffn = saved[i]
        dx, g_ffn = _ffn_bwd(dx, x1, sv_ffn, p, i, str(i))
        layer_bwd = _gla_layer_bwd if i % 2 == 0 else _fox_layer_bwd
        dx, g_mix = layer_bwd(dx, x_in, sv_mix, p, i)
        lg[i] = {**g_ffn, **g_mix}
    grad_x = dx[None]

    gla_l = [i for i in range(depth) if i % 2 == 0]
    fox_l = [i for i in range(depth) if i % 2 == 1]
    small_parts = dict(
        norm_mix=jnp.concatenate([lg[i]["norm_mix"] for i in range(depth)]),
        norm_ffn=jnp.concatenate([lg[i]["norm_ffn"] for i in range(depth)]),
        gla_b_gate=jnp.concatenate([lg[i]["gla_b_gate"] for i in gla_l]),
        gla_norm=jnp.concatenate([lg[i]["gla_norm"] for i in gla_l]),
        fox_b_f=jnp.concatenate([lg[i]["fox_b_f"] for i in fox_l]),
        fox_q_norm=jnp.concatenate([lg[i]["fox_q_norm"] for i in fox_l]),
        fox_k_norm=jnp.concatenate([lg[i]["fox_k_norm"] for i in fox_l]),
        ffn_conv_b=jnp.concatenate([lg[i]["conv_b"] for i in range(depth)]),
        norm_final=d_norm_final,
        gla_w_gate=jnp.stack([lg[i]["gla_w_gate"] for i in gla_l]),
        ffn_conv_w=jnp.stack([lg[i]["conv_w"] for i in range(depth)]),
        loss=loss_part[:, :1],
    )
    order = ("norm_mix", "norm_ffn", "gla_b_gate", "gla_norm", "fox_b_f", "fox_q_norm", "fox_k_norm", "ffn_conv_b",
             "norm_final", "gla_w_gate", "ffn_conv_w", "loss")
    packed = _pack([small_parts[nm] for nm in order])
    dmod = jnp.stack([jnp.concatenate([lg[i][nm] for nm in ("sh_m", "sc_m", "g_m", "sh_f", "sc_f", "g_f")], axis=1)
                      for i in range(depth)])
    hs_small, tok_small = _split_start([packed[None], dmod[None]], _all_plan, "gather_small_grads_start",
                                       land_blocks=N_DEV)
    dw_main, da_tail, h1_gla = lg[0]["gla_w_in_unsent"]
    dw_in_t = _tail_rows(da_tail + tok_small.astype(BF16), h1_gla, dw_main, rank, "gla_in_tail_dw")
    send("gla_in", [dw_in_t.reshape(N_DEV, -1, d)])
    started = pending[-1][1][0][0]

    received = {}

    def arrive(key, after):
        sums, lands = _split_wait(sent[key], _chip_plan, after, f"scatter_{key}_chip_wait")
        received[key] = list(zip(lands, sums))

    for key in ("ffn1", "fox", "ffn0", "gla_out"):
        arrive(key, started)

    out_g, out_d, out_m, out_v = {}, {}, {}, {}

    chip_idx = chip.astype(jnp.int32).reshape(1)

    def update(nm, g_arr, transposed=False):
        swap = (lambda t: jnp.transpose(t, (0, 2, 1))) if transposed else (lambda t: t)
        if isinstance(g_arr, list):
            res = _adamw_pieces(swap(w[nm]), [t[0] for t in g_arr], [t[1] for t in g_arr], chip_idx,
                                swap(mom_m[nm]), swap(mom_v[nm]), f"adamw_{nm}")
        else:
            res = _adamw(w[nm], g_arr, mom_m[nm], mom_v[nm], f"adamw_{nm}")
        out_g[nm], out_d[nm], out_m[nm], out_v[nm] = (swap(t) for t in res)

    update("gla_w_out", [received["gla_out"][0]])
    update("fox_w_out", [received["fox"][1]])
    tok_flush = flush(out_g["fox_w_out"])
    update("ffn_w_up", [received[f"ffn{i}"][0] for i in range(depth)])
    update("fox_w_in", [received["fox"][0]], transposed=True)
    update("ffn_w_down", [received[f"ffn{i}"][1] for i in range(depth)])

    updated = ("gla_w_out", "fox_w_in", "fox_w_out", "ffn_w_up", "ffn_w_down")
    (packed_mine, dmod_mine), (packed_all, dmod_all) = _split_wait(
        hs_small, _all_plan, [out_d[nm] for nm in updated], "gather_small_grads_wait")
    packed_all = lax.dynamic_update_slice(packed_all, packed_mine + tok_flush, (me, 0, 0))
    dmod_all = lax.dynamic_update_slice(dmod_all, dmod_mine, (me, 0, 0, 0))
    summed = _unpack(_sum8(packed_all, "sum_small_grads"), [small_parts[nm].shape for nm in order])
    small_g = dict(zip(order, summed))
    loss = small_g["loss"][0, 0]
    dmod_all = dmod_all[:, :, 0, :]
    grads = {}
    cond_t = _pad_cols(jnp.transpose(cond_all)).astype(BF16)
    dmod_cols = lax.dynamic_slice(dmod_all, (0, 0, me * mod_cols), (N_DEV, depth, mod_cols))
    g_w_mod = []
    for i in range(depth):
        rhs = jnp.pad(dmod_cols[:, i], ((0, LANE - N_DEV), (0, 0)))
        g_w_mod.append(_matmul(cond_t, rhs, name=f"mod_dw_{i}", tn=768))
    grads["w_mod"] = jnp.stack(g_w_mod)
    small_g["b_mod"] = _sum8(dmod_all.reshape(N_DEV, 1, -1), "sum_b_mod").reshape(depth, -1)
    update("w_mod", grads["w_mod"])

    gate_cols = gla_w_gate.shape[2]
    conv_cols = ffn_conv_w.shape[2]
    local_small = dict(small_g)
    local_small["gla_w_gate"] = lax.dynamic_slice_in_dim(small_g["gla_w_gate"], me * gate_cols, gate_cols, axis=2)
    local_small["ffn_conv_w"] = lax.dynamic_slice_in_dim(small_g["ffn_conv_w"], me * conv_cols, conv_cols, axis=2)
    names = SMALL + SMALL_SHARDED
    shapes = [w[nm].shape for nm in names]
    res = _adamw(_pack([w[nm] for nm in names])[None], _pack([local_small[nm] for nm in names])[None],
                 _pack([mom_m[nm] for nm in names])[None], _pack([mom_v[nm] for nm in names])[None], "adamw_small")
    for tgt, flat in zip((out_g, out_d, out_m, out_v), res):
        for nm, arr in zip(names, _unpack(flat[0], shapes)):
            tgt[nm] = arr

    arrive("gla_in", [out_d[nm] for nm in updated + ("w_mod",)])
    update("gla_w_in", [received["gla_in"][0]], transposed=True)

    return (loss, grad_x, *[out_g[n] for n in WEIGHTS], *[out_d[n] for n in WEIGHTS],
            *[out_m[n] for n in WEIGHTS], *[out_v[n] for n in WEIGHTS])
```

```python
import jax
import jax.numpy as jnp
from jax import lax
from jax.experimental import pallas as pl
from jax.experimental.pallas import tpu as pltpu

F32, BF16 = jnp.float32, jnp.bfloat16
N_DEV = 8
GLA_HEADS = 4
GLA_TAU = 16.0
GLA_CHUNK = 64
NORM_EPS = 1e-6
ADAM_LR, ADAM_B1, ADAM_B2, ADAM_EPS, ADAM_WD, ADAM_STEP = 0.001, 0.9, 0.999, 1e-08, 0.01, 10
LANE = 128
VMEM_LIMIT = 56 * 1024 * 1024
NEG = -1e30


def _pcall(body, **kw):
    return pl.pallas_call(body, **kw)


def _params(n_axes):
    return pltpu.CompilerParams(dimension_semantics=("arbitrary",) * n_axes, vmem_limit_bytes=VMEM_LIMIT)


def _tile(dim, pref):
    if dim <= pref:
        return dim
    t = pref
    while dim % t:
        t -= LANE
    assert t > 0, (dim, pref)
    return t


def _dot(a, b, ta=False, tb=False):
    dims = (((0,) if ta else (1,), (1,) if tb else (0,)), ((), ()))
    return lax.dot_general(a.astype(BF16), b.astype(BF16), dims, preferred_element_type=F32)


def _split3(x):
    hi = x.astype(BF16)
    r1 = x - hi.astype(F32)
    mid = r1.astype(BF16)
    lo = (r1 - mid.astype(F32)).astype(BF16)
    return hi, mid, lo


def _tri_matmul(tri, x):
    hi, mid, lo = _split3(x)
    return _dot(tri, hi) + _dot(tri, mid) + _dot(tri, lo)


def _tri(n, upper=False):
    r = lax.broadcasted_iota(jnp.int32, (n, n), 0)
    c = lax.broadcasted_iota(jnp.int32, (n, n), 1)
    return jnp.where((r <= c) if upper else (r >= c), 1.0, 0.0).astype(BF16)


def _log_sigmoid(x):
    return jnp.minimum(x, 0.0) - jnp.log(1.0 + jnp.exp(-jnp.abs(x)))


def _sigmoid(x):
    return 1.0 / (1.0 + jnp.exp(-x))


def _silu(x):
    return x * _sigmoid(x)


def _dsilu(x):
    s = _sigmoid(x)
    return s * (1.0 + x * (1.0 - s))


def _matmul(a, b, *, name, ta=False, tb=False, out_dtypes=(F32,), tm=1024, tn=1024, tk=2048,
            epilogue=None, extras=(), a_halves=False, b_halves=False, b_shards=False, out_shards=False,
            b_rows=None, out_rows=None):
    if a_halves:
        assert not ta
        m, k = a.shape[1], 2 * a.shape[2]
    else:
        m, k = (a.shape[1], a.shape[0]) if ta else a.shape
    if b_halves:
        assert not tb and b.shape[1] == k
        n = 2 * b.shape[2]
    elif b_shards:
        n = b.shape[1] if tb else N_DEV * b.shape[2]
        assert (N_DEV * b.shape[2] if tb else b.shape[1]) == k, (a.shape, b.shape, ta, tb)
    else:
        rows = b.shape[0] if b_rows is None else b_rows
        n = rows if tb else b.shape[1]
        assert (b.shape[1] if tb else rows) == k, (a.shape, b.shape, ta, tb)
    n_unit = n // N_DEV if (out_shards or (b_shards and not tb)) else (n // 2 if b_halves else n)
    k_unit = k // N_DEV if (b_shards and tb) else (k // 2 if a_halves else k)
    tm, tn, tk = _tile(m, tm), _tile(n_unit, tn), _tile(k_unit, tk)
    nk = k // tk
    if a_halves:
        a_spec = pl.BlockSpec((None, tm, tk), lambda i, j, kk: (kk // (nk // 2), i, kk % (nk // 2)))
    elif ta:
        a_spec = pl.BlockSpec((tk, tm), lambda i, j, kk: (kk, i))
    else:
        a_spec = pl.BlockSpec((tm, tk), lambda i, j, kk: (i, kk))
    n_per, k_per = n // tn // N_DEV, nk // N_DEV
    if b_halves:
        b_spec = pl.BlockSpec((None, tk, tn), lambda i, j, kk: (j // (n // tn // 2), kk, j % (n // tn // 2)))
    elif b_shards and tb:
        b_spec = pl.BlockSpec((None, tn, tk), lambda i, j, kk: (kk // k_per, j, kk % k_per))
    elif b_shards:
        b_spec = pl.BlockSpec((None, tk, tn), lambda i, j, kk: (j // n_per, kk, j % n_per))
    elif tb:
        b_spec = pl.BlockSpec((tn, tk), lambda i, j, kk: (j, kk))
    else:
        b_spec = pl.BlockSpec((tk, tn), lambda i, j, kk: (kk, j))
    ex_specs = []
    for kind, arr in extras:
        if kind == "mn":
            assert arr.shape == (m, n), (arr.shape, m, n)
            ex_specs.append(pl.BlockSpec((tm, tn), lambda i, j, kk: (i, j)))
        else:
            assert arr.shape == (1, n), (arr.shape, n)
            ex_specs.append(pl.BlockSpec((1, tn), lambda i, j, kk: (0, j)))
    n_ex, n_out = len(extras), len(out_dtypes)

    def body(a_ref, b_ref, *rest):
        ex, outs, acc = rest[:n_ex], rest[n_ex:n_ex + n_out], rest[-1]
        kk = pl.program_id(2)

        @pl.when(kk == 0)
        def _():
            acc[...] = jnp.zeros_like(acc)

        acc[...] += _dot(a_ref[...], b_ref[...], ta, tb)

        @pl.when(kk == nk - 1)
        def _():
            if epilogue is None:
                vals = (acc[...],)
            else:
                vals = epilogue(acc[...], *[e[...] for e in ex])
            for o, v in zip(outs, vals):
                o[...] = v.astype(o.dtype)

    if out_shards:
        out_spec = pl.BlockSpec((None, tm, tn), lambda i, j, kk: (j // n_per, i, j % n_per))
        out_dims = (N_DEV, m, n // N_DEV)
    else:
        out_spec = pl.BlockSpec((tm, tn), lambda i, j, kk: (i, j))
        out_dims = (m if out_rows is None else out_rows, n)
    res = _pcall(
        body, name=name, grid=(m // tm, n // tn, nk),
        in_specs=[a_spec, b_spec] + ex_specs,
        out_specs=[out_spec] * n_out,
        out_shape=[jax.ShapeDtypeStruct(out_dims, d) for d in out_dtypes],
        scratch_shapes=[pltpu.VMEM((tm, tn), F32)],
        compiler_params=_params(3),
    )(a, b, *[arr for _, arr in extras])
    return res[0] if n_out == 1 else res


def _tail_rows(a, b, into, rows, name, tn=1024):
    k, n = b.shape
    m_total = into.shape[0]
    tn = _tile(n, tn)

    def body(a_ref, b_ref, into_ref, out_ref):
        out_ref[...] = _dot(a_ref[...], b_ref[...], ta=True)[:rows].astype(out_ref.dtype)

    return _pcall(
        body, name=name, grid=(n // tn,),
        in_specs=[pl.BlockSpec((k, a.shape[1]), lambda j: (0, 0)), pl.BlockSpec((k, tn), lambda j: (0, j)),
                  pl.BlockSpec(memory_space=pl.ANY)],
        out_specs=pl.BlockSpec((rows, tn), lambda j: (m_total // rows - 1, j)),
        out_shape=jax.ShapeDtypeStruct(into.shape, into.dtype),
        input_output_aliases={2: 0}, compiler_params=_params(1),
    )(a, b, into)


def _rowwise(fn, ins, outs, *, name, tr=128):
    rows = next(e[1].shape[0] for e in ins if e[0] != "full")
    tr = _tile(rows, tr)
    in_specs = []
    for entry in ins:
        kind, arr = entry[0], entry[1]
        assert kind == "full" or (arr.shape[0] == rows and arr.ndim == 2)
        if kind == "row":
            in_specs.append(pl.BlockSpec((tr, arr.shape[1]), lambda i: (i, 0)))
        elif kind == "cols":
            in_specs.append(pl.BlockSpec((tr, entry[3]), lambda i, cb=entry[2]: (i, cb)))
        else:
            in_specs.append(pl.BlockSpec(arr.shape, lambda i, nd=arr.ndim: (0,) * nd))
    out_specs, out_shape = [], []
    for kind, w, dt in outs:
        if kind == "row":
            out_specs.append(pl.BlockSpec((tr, w), lambda i: (i, 0)))
            out_shape.append(jax.ShapeDtypeStruct((rows, w), dt))
        else:
            out_specs.append(pl.BlockSpec((1, w), lambda i: (0, 0)))
            out_shape.append(jax.ShapeDtypeStruct((1, w), dt))
    n_in = len(ins)

    def body(*refs):
        i = pl.program_id(0)
        vals = fn(*[r[...] for r in refs[:n_in]])
        for (kind, _, _), o, v in zip(outs, refs[n_in:], vals):
            if kind == "row":
                o[...] = v.astype(o.dtype)
            else:
                @pl.when(i == 0)
                def _(o=o):
                    o[...] = jnp.zeros_like(o)

                o[...] += v.astype(o.dtype)

    return _pcall(body, name=name, grid=(rows // tr,), in_specs=in_specs, out_specs=out_specs,
                  out_shape=out_shape, compiler_params=_params(1))(*[e[1] for e in ins])


def _colsum(x):
    return jnp.sum(x, axis=0, keepdims=True)


def _norm_stats(x):
    rstd = lax.rsqrt(jnp.mean(x * x, axis=-1, keepdims=True) + NORM_EPS)
    return x * rstd, rstd


def _norm_bwd(dxhat, xhat, rstd):
    return rstd * (dxhat - xhat * jnp.mean(dxhat * xhat, axis=-1, keepdims=True))


def _adaln_fwd(x, gain, sc, sh, name):
    def fn(x, gain, sc, sh):
        xhat, _ = _norm_stats(x)
        return ((xhat * gain) * (1.0 + sc) + sh,)

    return _rowwise(fn, [("row", x), ("full", gain), ("full", sc), ("full", sh)],
                    [("row", x.shape[1], BF16)], name=name)[0]


def _adaln_bwd(x, dh, dres, gain, sc, name):
    d = x.shape[1]

    def fn(x, dh, dres, gain, sc):
        xhat, rstd = _norm_stats(x)
        dxhat = dh * (gain * (1.0 + sc))
        dx = dres + _norm_bwd(dxhat, xhat, rstd)
        return dx, _colsum(dh), _colsum(dh * (xhat * gain)), _colsum(dh * xhat * (1.0 + sc))

    return _rowwise(fn, [("row", x), ("row", dh), ("row", dres), ("full", gain), ("full", sc)],
                    [("row", d, F32), ("acc", d, F32), ("acc", d, F32), ("acc", d, F32)], name=name)


def _residual_bwd(dx, y, g, name):
    d = dx.shape[1]

    def fn(dx, y, g):
        return dx * (1.0 + g), _colsum(dx * y)

    return _rowwise(fn, [("row", dx), ("row", y), ("full", g)], [("row", d, BF16), ("acc", d, F32)], name=name)


def _final_loss(x, target, gain, name):
    d = x.shape[1]

    def fn(x, t, gain):
        xhat, rstd = _norm_stats(x)
        err = xhat * gain - t
        dy = err * (1.0 / d)
        loss = 0.5 * jnp.sum(jnp.mean(err * err, axis=-1, keepdims=True), axis=0, keepdims=True)
        dx = _norm_bwd(dy * gain, xhat, rstd)
        return dx, _colsum(dy * xhat), jnp.broadcast_to(loss, (1, LANE))

    return _rowwise(fn, [("row", x), ("row", target), ("full", gain)],
                    [("row", d, F32), ("acc", d, F32), ("acc", LANE, F32)], name=name)


def _gla_gates(q, k, a, wg, bg, scale, c):
    ga = _dot(a, wg) + bg
    la = _log_sigmoid(ga) * (1.0 / GLA_TAU)
    b = _tri_matmul(_tri(c), la)
    bl = _colsum(la)
    eb, enb, eend = jnp.exp(b), jnp.exp(-b), jnp.exp(bl - b)
    q = q * scale
    return dict(ga=ga, eb=eb, enb=enb, eend=eend, dec=jnp.exp(bl), q_dec=q * eb, k_inv=k * enb, k_end=k * eend)


def _causal(c):
    return lax.broadcasted_iota(jnp.int32, (c, c), 0) >= lax.broadcasted_iota(jnp.int32, (c, c), 1)


def _gla_specs(heads, c, dk, dv, chunk):
    return [
        pl.BlockSpec((c, heads * dk), lambda n: (chunk(n), 0)),
        pl.BlockSpec((c, heads * dk), lambda n: (chunk(n), 1)),
        pl.BlockSpec((c, heads * dv), lambda n: (chunk(n), 1)),
        pl.BlockSpec((c, LANE), lambda n: (chunk(n), 0)),
        pl.BlockSpec((LANE, heads * dk), lambda n: (0, 0)),
        pl.BlockSpec((1, heads * dk), lambda n: (0, 0)),
    ]


def _gla_fwd(proj, a_tail, wg_p, bg, name):
    s = proj.shape[0]
    heads, c = GLA_HEADS, GLA_CHUNK
    dk = wg_p.shape[1] // heads
    dv = 2 * dk
    n_chunks = s // c
    scale = dk ** -0.5

    def body(q_ref, k_ref, v_ref, a_ref, wg_ref, bg_ref, o_ref, st_ref, state):
        @pl.when(pl.program_id(0) == 0)
        def _():
            state[...] = jnp.zeros_like(state)

        a = a_ref[...]
        for h in range(heads):
            sk, sv = slice(h * dk, (h + 1) * dk), slice(h * dv, (h + 1) * dv)
            g = _gla_gates(q_ref[:, sk], k_ref[:, sk], a, wg_ref[:, sk], bg_ref[:, sk], scale, c)
            v = v_ref[:, sv]
            st = state[h]
            attn = jnp.where(_causal(c), _dot(g["q_dec"], g["k_inv"], tb=True), 0.0)
            o_ref[:, sv] = _dot(attn, v) + _dot(g["q_dec"], st, tb=True)
            st_ref[h] = st.astype(st_ref.dtype)
            state[h] = g["dec"] * st + _dot(v, g["k_end"], ta=True)

    return _pcall(
        body, name=name, grid=(n_chunks,),
        in_specs=_gla_specs(heads, c, dk, dv, lambda n: n),
        out_specs=[pl.BlockSpec((c, heads * dv), lambda n: (n, 0)),
                   pl.BlockSpec((heads, None, dv, dk), lambda n: (0, n, 0, 0))],
        out_shape=[jax.ShapeDtypeStruct((s, heads * dv), F32),
                   jax.ShapeDtypeStruct((heads, n_chunks, dv, dk), BF16)],
        scratch_shapes=[pltpu.VMEM((heads, dv, dk), F32)],
        compiler_params=_params(1),
    )(proj, proj, proj, a_tail, wg_p, bg)


def _gla_bwd(proj, a_tail, wg_p, bg, states, d_o, name):
    s = proj.shape[0]
    heads, c = GLA_HEADS, GLA_CHUNK
    dk = wg_p.shape[1] // heads
    dv = 2 * dk
    n_chunks = s // c
    scale = dk ** -0.5

    def body(q_ref, k_ref, v_ref, a_ref, wg_ref, bg_ref, st_ref, do_ref, dq_ref, dk_ref, dv_ref, dga_ref, dstate):
        @pl.when(pl.program_id(0) == 0)
        def _():
            dstate[...] = jnp.zeros_like(dstate)

        a = a_ref[...]
        mask = _causal(c)
        for h in range(heads):
            sk, sv = slice(h * dk, (h + 1) * dk), slice(h * dv, (h + 1) * dv)
            g = _gla_gates(q_ref[:, sk], k_ref[:, sk], a, wg_ref[:, sk], bg_ref[:, sk], scale, c)
            v, st, dst, d_out = v_ref[:, sv], st_ref[h], dstate[h], do_ref[:, sv]
            q_dec, k_inv, k_end = g["q_dec"], g["k_inv"], g["k_end"]
            attn = jnp.where(mask, _dot(q_dec, k_inv, tb=True), 0.0)
            d_attn = jnp.where(mask, _dot(d_out, v, tb=True), 0.0)
            d_qdec = _dot(d_attn, k_inv) + _dot(d_out, st)
            d_kinv = _dot(d_attn, q_dec, ta=True)
            d_kend = _dot(v, dst)
            dv_ref[:, sv] = (_dot(attn, d_out, ta=True) + _dot(k_end, dst, tb=True)).astype(dv_ref.dtype)
            d_dec = jnp.sum(dst * st.astype(F32), axis=0, keepdims=True)
            dstate[h] = g["dec"] * dst + _dot(d_out, q_dec, ta=True)

            dq_ref[:, sk] = (d_qdec * (scale * g["eb"])).astype(dq_ref.dtype)
            dk_ref[:, sk] = (d_kinv * g["enb"] + d_kend * g["eend"]).astype(dk_ref.dtype)
            kk = d_kend * k_end
            db = d_qdec * q_dec - d_kinv * k_inv - kk
            dbl = jnp.sum(kk, axis=0, keepdims=True) + d_dec * g["dec"]
            last = lax.broadcasted_iota(jnp.int32, db.shape, 0) == c - 1
            db = db + jnp.where(last, dbl, 0.0)
            dla = _tri_matmul(_tri(c, upper=True), db)
            dga_ref[:, sk] = dla * (1.0 / GLA_TAU) * _sigmoid(-g["ga"])

    chunk = lambda n: n_chunks - 1 - n
    rev = lambda n: (chunk(n), 0)
    return _pcall(
        body, name=name, grid=(n_chunks,),
        in_specs=_gla_specs(heads, c, dk, dv, chunk) + [
            pl.BlockSpec((heads, None, dv, dk), lambda n: (0, chunk(n), 0, 0)),
            pl.BlockSpec((c, heads * dv), rev)],
        out_specs=[pl.BlockSpec((c, heads * dk), rev), pl.BlockSpec((c, heads * dk), rev),
                   pl.BlockSpec((c, heads * dv), rev), pl.BlockSpec((c, heads * dk), rev)],
        out_shape=[jax.ShapeDtypeStruct((s, heads * dk), BF16), jax.ShapeDtypeStruct((s, heads * dk), BF16),
                   jax.ShapeDtypeStruct((s, heads * dv), BF16), jax.ShapeDtypeStruct((s, heads * dk), F32)],
        scratch_shapes=[pltpu.VMEM((heads, dv, dk), F32)],
        compiler_params=_params(1),
    )(proj, proj, proj, a_tail, wg_p, bg, states, d_o)


def _gla_post_fwd(o, r, gn, name):
    dvt = o.shape[1]
    dv = dvt // GLA_HEADS

    def fn(o, r, gn):
        outs = []
        for h in range(GLA_HEADS):
            sl = slice(h * dv, (h + 1) * dv)
            ohat, _ = _norm_stats(o[:, sl])
            outs.append((ohat * gn[:, sl]) * _silu(r[:, sl]))
        return (jnp.concatenate(outs, axis=1),)

    return _rowwise(fn, [("row", o), r, ("full", gn)], [("row", dvt, BF16)], name=name)[0]


def _gla_post_bwd(o, r, gn, dog, name):
    dvt = o.shape[1]
    dv = dvt // GLA_HEADS

    def fn(o, r, gn, dog):
        d_o, d_r, d_g = [], [], []
        for h in range(GLA_HEADS):
            sl = slice(h * dv, (h + 1) * dv)
            ohat, rstd = _norm_stats(o[:, sl])
            g, rr, dd = gn[:, sl], r[:, sl], dog[:, sl]
            d_r.append(dd * (ohat * g) * _dsilu(rr))
            don = dd * _silu(rr)
            d_g.append(_colsum(don * ohat))
            d_o.append(_norm_bwd(don * g, ohat, rstd))
        return jnp.concatenate(d_o, axis=1), jnp.concatenate(d_r, axis=1), jnp.concatenate(d_g, axis=1)

    return _rowwise(fn, [("row", o), r, ("full", gn), ("row", dog)],
                    [("row", dvt, F32), ("row", dvt, BF16), ("acc", dvt, F32)], name=name)


def _fox_prep(q, k, v, qg, kg, d, hd, name):
    heads = d // hd
    scale = hd ** -0.5

    def fn(q, k, v, qg, kg):
        qs, ks = [], []
        for h in range(heads):
            sl = slice(h * hd, (h + 1) * hd)
            qs.append(_norm_stats(q[:, sl])[0] * qg * scale)
            ks.append(_norm_stats(k[:, sl])[0] * kg)
        return jnp.concatenate(qs, axis=1), jnp.concatenate(ks, axis=1), v

    return _rowwise(fn, [q, k, v, ("full", qg), ("full", kg)],
                    [("row", d, BF16)] * 3, name=name)


def _fox_prep_bwd(q, k, dqn, dkn, qg, kg, hd, name):
    d = dqn.shape[1]
    heads = d // hd
    scale = hd ** -0.5

    def fn(q, k, dqn, dkn, qg, kg):
        dq, dk, gq, gk = [], [], [], []
        for h in range(heads):
            sl = slice(h * hd, (h + 1) * hd)
            for x, dxn, g, s, dl, gl in ((q, dqn, qg, scale, dq, gq), (k, dkn, kg, 1.0, dk, gk)):
                xhat, rstd = _norm_stats(x[:, sl])
                dn = dxn[:, sl] * s
                gl.append(_colsum(dn * xhat))
                dl.append(_norm_bwd(dn * g, xhat, rstd))
        cat = lambda t: jnp.concatenate(t, axis=1)
        return cat(dq), cat(dk), cat(gq), cat(gk)

    return _rowwise(fn, [q, k, ("row", dqn), ("row", dkn), ("full", qg), ("full", kg)],
                    [("row", d, BF16), ("row", d, BF16), ("acc", d, F32), ("acc", d, F32)], name=name)


def _fox_cum(fl, bf_p, name, tb=256):
    s = fl.shape[0]
    tb = _tile(s, tb)

    def body(fl_ref, bf_ref, cum_ref, carry):
        @pl.when(pl.program_id(0) == 0)
        def _():
            carry[...] = jnp.zeros_like(carry)

        lf = _log_sigmoid(fl_ref[...] + bf_ref[...])
        cum_ref[...] = _tri_matmul(_tri(tb), lf) + carry[...]
        carry[...] += _colsum(lf)

    return _pcall(
        body, name=name, grid=(s // tb,),
        in_specs=[pl.BlockSpec((tb, LANE), lambda i: (i, 0)), pl.BlockSpec((1, LANE), lambda i: (0, 0))],
        out_specs=pl.BlockSpec((tb, LANE), lambda i: (i, 0)),
        out_shape=jax.ShapeDtypeStruct((s, LANE), F32),
        scratch_shapes=[pltpu.VMEM((1, LANE), F32)],
        compiler_params=_params(1),
    )(fl, bf_p)


def _fox_cum_bwd(dcum, fl, bf_p, name, tb=256):
    s = fl.shape[0]
    tb = _tile(s, tb)
    nb = s // tb

    def body(dc_ref, fl_ref, bf_ref, dfl_ref, dbf_ref, carry):
        @pl.when(pl.program_id(0) == 0)
        def _():
            carry[...] = jnp.zeros_like(carry)
            dbf_ref[...] = jnp.zeros_like(dbf_ref)

        dc = dc_ref[...]
        dlf = _tri_matmul(_tri(tb, upper=True), dc) + carry[...]
        carry[...] += _colsum(dc)
        dfl = dlf * _sigmoid(-(fl_ref[...] + bf_ref[...]))
        dfl_ref[...] = dfl
        dbf_ref[...] += _colsum(dfl)

    rev = lambda i: (nb - 1 - i, 0)
    return _pcall(
        body, name=name, grid=(nb,),
        in_specs=[pl.BlockSpec((tb, LANE), rev), pl.BlockSpec((tb, LANE), rev), pl.BlockSpec((1, LANE), lambda i: (0, 0))],
        out_specs=[pl.BlockSpec((tb, LANE), rev), pl.BlockSpec((1, LANE), lambda i: (0, 0))],
        out_shape=[jax.ShapeDtypeStruct((s, LANE), F32), jax.ShapeDtypeStruct((1, LANE), F32)],
        scratch_shapes=[pltpu.VMEM((1, LANE), F32)],
        compiler_params=_params(1),
    )(dcum, fl, bf_p)


def _fox_attn_fwd(qn, kn, vb, cum_col, cum_row, hd, t, name):
    s, d = qn.shape
    heads = d // hd
    nq = s // t

    def body(q_ref, k_ref, v_ref, cc_ref, cr_ref, o_ref, lse_ref):
        qi = pl.program_id(1)
        q = q_ref[...]
        cq = cc_ref[...]
        qpos = qi * t + lax.broadcasted_iota(jnp.int32, (t, 1), 0)

        def step(kj, carry, diagonal=False):
            m, l, acc = carry
            off = pl.multiple_of(kj * t, t)
            ks, vs = k_ref[pl.ds(off, t), :], v_ref[pl.ds(off, t), :]
            sc = _dot(q, ks, tb=True) + cq - cr_ref[kj]
            if diagonal:
                kpos = off + lax.broadcasted_iota(jnp.int32, (1, t), 1)
                sc = jnp.where(kpos <= qpos, sc, NEG)
            m_new = jnp.maximum(m, jnp.max(sc, axis=1, keepdims=True))
            alpha = jnp.exp(m - m_new)
            p = jnp.exp(sc - m_new)
            return m_new, alpha * l + jnp.sum(p, axis=1, keepdims=True), alpha * acc + _dot(p, vs)

        init = (jnp.full((t, 1), NEG, F32), jnp.zeros((t, 1), F32), jnp.zeros((t, hd), F32))
        m, l, acc = step(qi, lax.fori_loop(0, qi, step, init), diagonal=True)
        o_ref[...] = acc / l
        lse_ref[...] = m + jnp.log(l)

    return _pcall(
        body, name=name, grid=(heads, nq),
        in_specs=[pl.BlockSpec((t, hd), lambda h, i: (i, h)),
                  pl.BlockSpec((s, hd), lambda h, i: (0, h)),
                  pl.BlockSpec((s, hd), lambda h, i: (0, h)),
                  pl.BlockSpec((None, t, 1), lambda h, i: (h, i, 0)),
                  pl.BlockSpec((None, nq, 1, t), lambda h, i: (h, 0, 0, 0))],
        out_specs=[pl.BlockSpec((t, hd), lambda h, i: (i, h)), pl.BlockSpec((None, t, 1), lambda h, i: (h, i, 0))],
        out_shape=[jax.ShapeDtypeStruct((s, d), F32), jax.ShapeDtypeStruct((heads, s, 1), F32)],
        compiler_params=_params(2),
    )(qn, kn, vb, cum_col, cum_row)


def _fox_attn_bwd(qn, kn, vb, d_o, o, lse, cum_col, cum_row, hd, t, name):
    s, d = qn.shape
    heads = d // hd
    nq = s // t

    def body(q_ref, k_ref, v_ref, do_ref, o_ref, lse_ref, cc_ref, cr_ref,
             dq_ref, dk_ref, dv_ref, dcq_ref, dck_ref, delta):
        kj = pl.program_id(1)

        @pl.when(kj == 0)
        def _():
            dq_ref[...] = jnp.zeros_like(dq_ref)
            dcq_ref[...] = jnp.zeros_like(dcq_ref)
            delta[...] = jnp.sum(do_ref[...] * o_ref[...], axis=1, keepdims=True)

        ks, vs, cr = k_ref[...], v_ref[...], cr_ref[...]
        kpos = kj * t + lax.broadcasted_iota(jnp.int32, (1, t), 1)

        def step(qi, carry, diagonal=False):
            dk, dv, dck = carry
            rows = pl.ds(pl.multiple_of(qi * t, t), t)
            q, d_out = q_ref[rows, :], do_ref[rows, :]
            sc = _dot(q, ks, tb=True) + cc_ref[rows, :] - cr
            p = jnp.exp(sc - lse_ref[rows, :])
            if diagonal:
                qpos = qi * t + lax.broadcasted_iota(jnp.int32, (t, 1), 0)
                p = jnp.where(kpos <= qpos, p, 0.0)
            ds = p * (_dot(d_out, vs, tb=True) - delta[rows, :])
            dq_ref[rows, :] += _dot(ds, ks)
            dcq_ref[rows, :] += jnp.sum(ds, axis=1, keepdims=True)
            return dk + _dot(ds, q, ta=True), dv + _dot(p, d_out, ta=True), dck + _colsum(ds)

        init = (jnp.zeros((t, hd), F32), jnp.zeros((t, hd), F32), jnp.zeros((1, t), F32))
        dk, dv, dck = lax.fori_loop(kj + 1, nq, step, step(kj, init, diagonal=True))
        dk_ref[...] = dk.astype(dk_ref.dtype)
        dv_ref[...] = dv.astype(dv_ref.dtype)
        dck_ref[...] = dck

    head_rows = lambda h, j: (0, h)
    blk = lambda h, j: (j, h)
    return _pcall(
        body, name=name, grid=(heads, nq),
        in_specs=[pl.BlockSpec((s, hd), head_rows), pl.BlockSpec((t, hd), blk), pl.BlockSpec((t, hd), blk),
                  pl.BlockSpec((s, hd), head_rows), pl.BlockSpec((s, hd), head_rows),
                  pl.BlockSpec((None, s, 1), lambda h, j: (h, 0, 0)),
                  pl.BlockSpec((None, s, 1), lambda h, j: (h, 0, 0)),
                  pl.BlockSpec((None, None, 1, t), lambda h, j: (h, j, 0, 0))],
        out_specs=[pl.BlockSpec((s, hd), head_rows), pl.BlockSpec((t, hd), blk), pl.BlockSpec((t, hd), blk),
                   pl.BlockSpec((None, s, 1), lambda h, j: (h, 0, 0)),
                   pl.BlockSpec((None, None, 1, t), lambda h, j: (h, j, 0, 0))],
        out_shape=[jax.ShapeDtypeStruct((s, d), F32), jax.ShapeDtypeStruct((s, d), BF16),
                   jax.ShapeDtypeStruct((s, d), BF16), jax.ShapeDtypeStruct((heads, s, 1), F32),
                   jax.ShapeDtypeStruct((heads, nq, 1, t), F32)],
        scratch_shapes=[pltpu.VMEM((s, 1), F32)],
        compiler_params=_params(2),
    )(qn, kn, vb, d_o, o, lse, cum_col, cum_row)


def _fox_gate_fwd(o, og, name):
    def fn(o, og):
        return (o * _sigmoid(og),)

    return _rowwise(fn, [("row", o), og], [("row", o.shape[1], BF16)], name=name)[0]


def _fox_gate_bwd(o, og, dact, name):
    def fn(o, og, dact):
        sg = _sigmoid(og)
        return dact * sg, dact * o * sg * (1.0 - sg)

    d = o.shape[1]
    return _rowwise(fn, [("row", o), og, ("row", dact)], [("row", d, F32), ("row", d, BF16)], name=name)


def _shift_down(x, n):
    rows = lax.broadcasted_iota(jnp.int32, x.shape, 0)
    return jnp.where(rows >= n, pltpu.roll(x, n, 0), 0.0)


def _shift_up(x, n):
    rows = lax.broadcasted_iota(jnp.int32, x.shape, 0)
    return jnp.where(rows < x.shape[0] - n, pltpu.roll(x, x.shape[0] - n, 0), 0.0)


def _conv(u, w_ref, b):
    return w_ref[0:1, :] * _shift_down(u, 2) + w_ref[1:2, :] * _shift_down(u, 1) + w_ref[2:3, :] * u + b


def _conv_act_fwd(u, cw, cb, name, tc=256):
    s, two_f = u.shape
    dff = two_f // 2
    tc = _tile(dff, tc)
    nb = dff // tc

    def body(ug_ref, uv_ref, wg_ref, wv_ref, bg_ref, bv_ref, a_ref):
        gate = _conv(ug_ref[...], wg_ref, bg_ref[...])
        val = _conv(uv_ref[...], wv_ref, bv_ref[...])
        a_ref[...] = (_silu(gate) * val).astype(a_ref.dtype)

    lo, hi = (lambda j: (0, j)), (lambda j: (0, j + nb))
    return _pcall(
        body, name=name, grid=(nb,),
        in_specs=[pl.BlockSpec((s, tc), lo), pl.BlockSpec((s, tc), hi), pl.BlockSpec((3, tc), lo),
                  pl.BlockSpec((3, tc), hi), pl.BlockSpec((1, tc), lo), pl.BlockSpec((1, tc), hi)],
        out_specs=pl.BlockSpec((s, tc), lo),
        out_shape=jax.ShapeDtypeStruct((s, dff), BF16),
        compiler_params=_params(1),
    )(u, u, cw, cw, cb, cb)


def _conv_act_bwd(u, cw, cb, da, name, tc=128):
    s, two_f = u.shape
    dff = two_f // 2
    tc = _tile(dff, tc)
    nb = dff // tc

    def body(ug_ref, uv_ref, wg_ref, wv_ref, bg_ref, bv_ref, da_ref, du_ref, dw_ref, db_ref):
        ug, uv, da = ug_ref[...], uv_ref[...], da_ref[...]
        gate = _conv(ug, wg_ref, bg_ref[...])
        val = _conv(uv, wv_ref, bv_ref[...])
        sg = _sigmoid(gate)
        d_val = da * (gate * sg)
        d_gate = da * val * (sg * (1.0 + gate * (1.0 - sg)))
        for half, (dc, uu, w_ref) in enumerate(((d_gate, ug, wg_ref), (d_val, uv, wv_ref))):
            du = w_ref[0:1, :] * _shift_up(dc, 2) + w_ref[1:2, :] * _shift_up(dc, 1) + w_ref[2:3, :] * dc
            du_ref[half] = du.astype(du_ref.dtype)
            dw_ref[half, 0:1, :] = _colsum(dc * _shift_down(uu, 2))
            dw_ref[half, 1:2, :] = _colsum(dc * _shift_down(uu, 1))
            dw_ref[half, 2:3, :] = _colsum(dc * uu)
            db_ref[half] = _colsum(dc)

    lo, hi = (lambda j: (0, j)), (lambda j: (0, j + nb))
    both = lambda j: (0, 0, j)
    return _pcall(
        body, name=name, grid=(nb,),
        in_specs=[pl.BlockSpec((s, tc), lo), pl.BlockSpec((s, tc), hi), pl.BlockSpec((3, tc), lo),
                  pl.BlockSpec((3, tc), hi), pl.BlockSpec((1, tc), lo), pl.BlockSpec((1, tc), hi),
                  pl.BlockSpec((s, tc), lo)],
        out_specs=[pl.BlockSpec((2, s, tc), both), pl.BlockSpec((2, 3, tc), both), pl.BlockSpec((2, 1, tc), both)],
        out_shape=[jax.ShapeDtypeStruct((2, s, dff), BF16), jax.ShapeDtypeStruct((2, 3, dff), F32),
                   jax.ShapeDtypeStruct((2, 1, dff), F32)],
        compiler_params=_params(1),
    )(u, u, cw, cw, cb, cb, da)


def _adamw_math(w, g, m, v):
    m = ADAM_B1 * m + (1.0 - ADAM_B1) * g
    v = ADAM_B2 * v + (1.0 - ADAM_B2) * (g * g)
    m_hat = m / (1.0 - ADAM_B1 ** ADAM_STEP)
    v_hat = v / (1.0 - ADAM_B2 ** ADAM_STEP)
    delta = -ADAM_LR * (m_hat / (jnp.sqrt(v_hat) + ADAM_EPS) + ADAM_WD * w)
    return delta, m, v


def _update_tiles(r, c, tr):
    tc = c
    if r % 8:
        tr, tc = r, _tile(c, max(LANE, 512 * 1024 // r // LANE * LANE))
    elif r <= tr:
        tr = r
    while r % tr:
        tr -= 8
    return tr, tc


def _adamw(w, g, m, v, name, tr=128):
    layers, r, c = w.shape
    tr, tc = _update_tiles(r, c, tr)

    def body(w_ref, g_ref, m_ref, v_ref, go_ref, d_ref, mo_ref, vo_ref):
        grad = g_ref[...]
        delta, m_new, v_new = _adamw_math(w_ref[...], grad, m_ref[...], v_ref[...])
        go_ref[...], d_ref[...], mo_ref[...], vo_ref[...] = grad, delta, m_new, v_new

    spec = pl.BlockSpec((None, tr, tc), lambda l, i, j: (l, i, j))
    return _pcall(
        body, name=name, grid=(layers, r // tr, c // tc), in_specs=[spec] * 4, out_specs=[spec] * 4,
        out_shape=[jax.ShapeDtypeStruct((layers, r, c), F32)] * 4, compiler_params=_params(3),
    )(w, g, m, v)


def _adamw_pieces(w, lands, sums, chip, m, v, name, tr=128):
    layers, r, c = w.shape
    tr, tc = _update_tiles(r, c, tr)
    nr, nc = r // tr, c // tc

    def body(chip_ref, w_ref, *rest):
        land_refs, own_refs = rest[:layers], rest[layers:2 * layers]
        m_ref, v_ref, go_ref, d_ref, mo_ref, vo_ref = rest[2 * layers:]
        for layer in range(layers):
            @pl.when(pl.program_id(0) == layer)
            def _(land_ref=land_refs[layer], own_ref=own_refs[layer]):
                grad = jnp.zeros(w_ref.shape, F32)
                for q in range(4):
                    grad = grad + jnp.where(chip_ref[0] == q, own_ref[...], land_ref[q]).astype(F32)
                delta, m_new, v_new = _adamw_math(w_ref[...], grad, m_ref[...], v_ref[...])
                go_ref[...], d_ref[...], mo_ref[...], vo_ref[...] = grad, delta, m_new, v_new

    def walk(k, l, i, j):
        here = l == k
        return jnp.where(here, i, jnp.where(l < k, 0, nr - 1)), jnp.where(here, j, jnp.where(l < k, 0, nc - 1))

    spec = pl.BlockSpec((None, tr, tc), lambda l, i, j, chip_ref: (l, i, j))
    land_specs = [pl.BlockSpec((4, tr, tc), lambda l, i, j, chip_ref, k=k: (0,) + walk(k, l, i, j))
                  for k in range(layers)]
    own_specs = [pl.BlockSpec((None, tr, tc), lambda l, i, j, chip_ref, k=k: (chip_ref[0],) + walk(k, l, i, j))
                 for k in range(layers)]
    return _pcall(
        body, name=name,
        grid_spec=pltpu.PrefetchScalarGridSpec(
            num_scalar_prefetch=1, grid=(layers, nr, nc),
            in_specs=[spec] + land_specs + own_specs + [spec, spec], out_specs=[spec] * 4),
        out_shape=[jax.ShapeDtypeStruct((layers, r, c), F32)] * 4, compiler_params=_params(3),
    )(chip, w, *lands, *sums, m, v)


def _pair_sum(pieces, partner, core, name, tr=512):
    _, r, c = pieces.shape
    tc = c
    if r % 8:
        tr, tc = r, _tile(c, max(LANE, 1024 * 1024 // r // LANE * LANE))
    elif r <= tr:
        tr = r
    while r % tr:
        tr -= 8

    def body(core_ref, mine_ref, partner_ref, out_ref):
        out_ref[...] = (mine_ref[...].astype(F32) + partner_ref[...].astype(F32)).astype(out_ref.dtype)

    return _pcall(
        body, name=name,
        grid_spec=pltpu.PrefetchScalarGridSpec(
            num_scalar_prefetch=1, grid=(4, r // tr, c // tc),
            in_specs=[pl.BlockSpec((None, tr, tc), lambda q, i, j, core_ref: (2 * q + core_ref[0], i, j)),
                      pl.BlockSpec((None, tr, tc), lambda q, i, j, core_ref: (q, i, j))],
            out_specs=pl.BlockSpec((None, tr, tc), lambda q, i, j, core_ref: (q, i, j))),
        out_shape=jax.ShapeDtypeStruct((4, r, c), pieces.dtype), compiler_params=_params(3),
    )(core, pieces, partner)


def _sum8(x, name):
    p = x.shape[2]
    tp = _tile(p, 16 * 1024)

    def body(x_ref, o_ref):
        acc = x_ref[0]
        for i in range(1, N_DEV):
            acc = acc + x_ref[i]
        o_ref[...] = acc

    return _pcall(
        body, name=name, grid=(p // tp,), in_specs=[pl.BlockSpec((N_DEV, 1, tp), lambda i: (0, 0, i))],
        out_specs=pl.BlockSpec((1, tp), lambda i: (0, i)), out_shape=jax.ShapeDtypeStruct((1, p), x.dtype),
        compiler_params=_params(1),
    )(x)


def _exchange(arrays, name, scatter):
    n = len(arrays)
    hbm = pl.BlockSpec(memory_space=pl.ANY)

    def body(*refs):
        ins, outs, token = refs[:n], refs[n:2 * n], refs[2 * n]
        send_sems, recv_sems, local_sems = refs[2 * n + 1:]
        token[...] = jnp.zeros_like(token)
        x, y, c = lax.axis_index("x"), lax.axis_index("y"), lax.axis_index("c")
        me = 4 * x + 2 * y + c
        copies = []
        for a in range(n):
            src_mine = ins[a].at[me] if scatter else ins[a]
            local = pltpu.make_async_copy(src_mine, outs[a].at[me], local_sems.at[a])
            local.start()
            copies.append(local)
            for k in range(1, N_DEV):
                px = 1 - x if k & 4 else x
                py = 1 - y if k & 2 else y
                pc = 1 - c if k & 1 else c
                src = ins[a].at[4 * px + 2 * py + pc] if scatter else ins[a]
                cp = pltpu.make_async_remote_copy(
                    src_ref=src, dst_ref=outs[a].at[me],
                    send_sem=send_sems.at[a * (N_DEV - 1) + k - 1], recv_sem=recv_sems.at[a * (N_DEV - 1) + k - 1],
                    device_id=(px, py, pc), device_id_type=pl.DeviceIdType.MESH)
                cp.start()
                copies.append(cp)
        for cp in copies:
            cp.wait()

    out_shape = [jax.ShapeDtypeStruct(a.shape if scatter else (N_DEV,) + a.shape, a.dtype) for a in arrays]
    res = _pcall(
        body, name=name, in_specs=[hbm] * n, out_specs=[hbm] * n + [pl.BlockSpec(memory_space=pltpu.VMEM)],
        out_shape=out_shape + [jax.ShapeDtypeStruct((8, LANE), F32)],
        scratch_shapes=[pltpu.SemaphoreType.DMA((n * (N_DEV - 1),)), pltpu.SemaphoreType.DMA((n * (N_DEV - 1),)),
                        pltpu.SemaphoreType.DMA((n,))],
        compiler_params=pltpu.CompilerParams(has_side_effects=True),
    )(*arrays)
    return res[:n], res[n][0, 0]


_HBM = pl.BlockSpec(memory_space=pltpu.HBM)
_SEM = pl.BlockSpec(memory_space=pltpu.SEMAPHORE)
_DATAFLOW = pltpu.SideEffectType.DATAFLOW_SIDE_EFFECTING


def _peer(k, x, y, c):
    return (1 - x if k & 4 else x, 1 - y if k & 2 else y, 1 - c if k & 1 else c)


def _pair_plan(x, y, c):
    return [(2 * q + (1 - c), q, (x, y, 1 - c)) for q in range(4)]


def _chip_plan(x, y, c):
    out = []
    for k in _ICI_PEERS:
        px, py, pc = _peer(k, x, y, c)
        out.append((2 * px + py, 2 * x + y, (px, py, pc)))
    return out


def _all_plan(x, y, c):
    return [(0, 4 * x + 2 * y + c, _peer(k, x, y, c)) for k in range(1, N_DEV)]


def _split_start(arrays, plan, name, land_blocks=4):
    n = len(arrays)
    lands = [lax.empty((land_blocks,) + a.shape[1:], a.dtype) for a in arrays]
    n_copies = len(plan(0, 0, 0))

    def body(*refs):
        srcs, dsts = refs[:n], refs[n:2 * n]
        send_sems, recv_sems, token = refs[4 * n:5 * n], refs[5 * n:6 * n], refs[6 * n]
        copies = plan(lax.axis_index("x"), lax.axis_index("y"), lax.axis_index("c"))
        for a in range(n):
            for j, (src_block, dst_block, peer) in enumerate(copies):
                pltpu.make_async_remote_copy(
                    src_ref=srcs[a].at[src_block], dst_ref=dsts[a].at[dst_block],
                    send_sem=send_sems[a].at[j], recv_sem=recv_sems[a].at[j],
                    device_id=peer, device_id_type=pl.DeviceIdType.MESH).start()
        token[...] = jnp.zeros_like(token)

    sems = [pltpu.SemaphoreType.DMA((n_copies,))] * (2 * n)
    res = _pcall(
        body, name=name,
        in_specs=[_HBM] * (2 * n),
        out_specs=[_HBM] * (2 * n) + [_SEM] * (2 * n) + [pl.BlockSpec(memory_space=pltpu.VMEM)],
        out_shape=[pltpu.HBM(a.shape, a.dtype) for a in arrays] + [pltpu.HBM(l.shape, l.dtype) for l in lands]
        + sems + [jax.ShapeDtypeStruct((8, LANE), F32)],
        input_output_aliases={i: i for i in range(2 * n)},
        compiler_params=pltpu.CompilerParams(has_side_effects=_DATAFLOW),
    )(*[pltpu.with_memory_space_constraint(a, pltpu.HBM) for a in arrays],
      *[pltpu.with_memory_space_constraint(l, pltpu.HBM) for l in lands])
    handles = [(res[a], res[n + a], res[2 * n + a], res[3 * n + a]) for a in range(n)]
    return handles, res[4 * n][0, 0]


def _split_wait(handles, plan, after, name):
    n = len(handles)
    after = list(after) if isinstance(after, (list, tuple)) else [after]

    def body(*refs):
        srcs, dsts = refs[:n], refs[n:2 * n]
        send_sems, recv_sems = refs[2 * n:3 * n], refs[3 * n:4 * n]
        copies = plan(lax.axis_index("x"), lax.axis_index("y"), lax.axis_index("c"))
        for a in range(n):
            for j, (src_block, dst_block, peer) in enumerate(copies):
                cp = pltpu.make_async_remote_copy(
                    src_ref=srcs[a].at[src_block], dst_ref=dsts[a].at[dst_block],
                    send_sem=send_sems[a].at[j], recv_sem=recv_sems[a].at[j],
                    device_id=peer, device_id_type=pl.DeviceIdType.MESH)
                cp.wait_send()
                cp.wait_recv()

    srcs, lands = [h[0] for h in handles], [h[1] for h in handles]
    res = _pcall(
        body, name=name,
        in_specs=[_HBM] * (2 * n) + [_SEM] * (2 * n) + [pl.BlockSpec(memory_space=pl.ANY)] * len(after),
        out_specs=[_HBM] * (2 * n),
        out_shape=[pltpu.HBM(t.shape, t.dtype) for t in srcs + lands],
        input_output_aliases={i: i for i in range(2 * n)},
        compiler_params=pltpu.CompilerParams(has_side_effects=_DATAFLOW),
    )(*srcs, *lands, *[h[2] for h in handles], *[h[3] for h in handles], *after)
    return res[:n], res[n:]


_ICI_PEERS = (2, 4, 6)


def _gather2_start(shards, name):
    n = len(shards)
    lands = [lax.empty((N_DEV,) + a.shape, a.dtype) for a in shards]

    def body(*refs):
        srcs, dsts = refs[:n], refs[n:2 * n]
        send_sems, d2d_sems, ici_sems = refs[4 * n:5 * n], refs[5 * n:6 * n], refs[6 * n:7 * n]
        token = refs[7 * n]
        x, y, c = lax.axis_index("x"), lax.axis_index("y"), lax.axis_index("c")
        me = 4 * x + 2 * y + c
        for a in range(n):
            for j, k in enumerate((1,) + _ICI_PEERS):
                recv = d2d_sems[a].at[0] if j == 0 else ici_sems[a].at[j - 1]
                pltpu.make_async_remote_copy(
                    src_ref=srcs[a], dst_ref=dsts[a].at[me], send_sem=send_sems[a].at[j], recv_sem=recv,
                    device_id=_peer(k, x, y, c), device_id_type=pl.DeviceIdType.MESH).start()
        token[...] = jnp.zeros_like(token)

    dma = pltpu.SemaphoreType.DMA
    res = _pcall(
        body, name=name,
        in_specs=[_HBM] * (2 * n),
        out_specs=[_HBM] * (2 * n) + [_SEM] * (3 * n) + [pl.BlockSpec(memory_space=pltpu.VMEM)],
        out_shape=[pltpu.HBM(a.shape, a.dtype) for a in shards] + [pltpu.HBM(l.shape, l.dtype) for l in lands]
        + [dma((4,))] * n + [dma((1,))] * n + [dma((3,))] * n + [jax.ShapeDtypeStruct((8, LANE), F32)],
        input_output_aliases={i: i for i in range(2 * n)},
        compiler_params=pltpu.CompilerParams(has_side_effects=_DATAFLOW),
    )(*[pltpu.with_memory_space_constraint(a, pltpu.HBM) for a in shards],
      *[pltpu.with_memory_space_constraint(l, pltpu.HBM) for l in lands])
    handles = [tuple(res[i * n + a] for i in range(5)) for a in range(n)]
    return handles, res[5 * n][0, 0]


def _gather2_forward(handle, after, name):
    src, land, send_sems, d2d_sem, ici_sems = handle

    def body(land_ref, ici_ref, d2d_ref, after_ref, land_out, fwd_send, fwd_recv, token):
        x, y, c = lax.axis_index("x"), lax.axis_index("y"), lax.axis_index("c")
        sibling = (x, y, 1 - c)
        arrived = [(_peer(k, x, y, c), ici_ref.at[j]) for j, k in enumerate(_ICI_PEERS)] + [(sibling, d2d_ref.at[0])]
        for j, ((px, py, pc), recv) in enumerate(arrived):
            block = land_ref.at[4 * px + 2 * py + pc]
            pltpu.make_async_remote_copy(
                src_ref=block, dst_ref=block, send_sem=fwd_send.at[j], recv_sem=recv,
                device_id=(px, py, pc), device_id_type=pl.DeviceIdType.MESH).wait_recv()
            pltpu.make_async_remote_copy(
                src_ref=block, dst_ref=block, send_sem=fwd_send.at[j], recv_sem=fwd_recv.at[j],
                device_id=sibling, device_id_type=pl.DeviceIdType.MESH).start()
        token[...] = jnp.zeros_like(token)

    dma = pltpu.SemaphoreType.DMA
    land, fwd_send, fwd_recv, token = _pcall(
        body, name=name,
        in_specs=[_HBM, _SEM, _SEM, pl.BlockSpec(memory_space=pl.ANY)],
        out_specs=[_HBM, _SEM, _SEM, pl.BlockSpec(memory_space=pltpu.VMEM)],
        out_shape=[pltpu.HBM(land.shape, land.dtype), dma((4,)), dma((4,)), jax.ShapeDtypeStruct((8, LANE), F32)],
        input_output_aliases={0: 0},
        compiler_params=pltpu.CompilerParams(has_side_effects=_DATAFLOW),
    )(land, ici_sems, d2d_sem, after)
    return (src, land, send_sems, fwd_send, fwd_recv), token[0, 0]


def _gather2_wait(handle, after, name):
    src, land, send_sems, fwd_send, fwd_recv = handle

    def body(src_ref, land_ref, send_ref, fsend_ref, frecv_ref, after_ref, src_out, land_out):
        x, y, c = lax.axis_index("x"), lax.axis_index("y"), lax.axis_index("c")
        block = land_ref.at[4 * x + 2 * y + c]

        def copy(send, recv):
            return pltpu.make_async_remote_copy(src_ref=src_ref, dst_ref=block, send_sem=send, recv_sem=recv,
                                                device_id=(x, y, 1 - c), device_id_type=pl.DeviceIdType.MESH)

        for j in range(4):
            copy(send_ref.at[j], frecv_ref.at[j]).wait_send()
        for j in range(4):
            copy(fsend_ref.at[j], frecv_ref.at[j]).wait_send()
            copy(fsend_ref.at[j], frecv_ref.at[j]).wait_recv()

    res = _pcall(
        body, name=name,
        in_specs=[_HBM, _HBM, _SEM, _SEM, _SEM, pl.BlockSpec(memory_space=pl.ANY)],
        out_specs=[_HBM, _HBM],
        out_shape=[pltpu.HBM(src.shape, src.dtype), pltpu.HBM(land.shape, land.dtype)],
        input_output_aliases={0: 0, 1: 1},
        compiler_params=pltpu.CompilerParams(has_side_effects=_DATAFLOW),
    )(src, land, send_sems, fwd_send, fwd_recv, after)
    return res[0], res[1]


def _pad_cols(x, width=LANE):
    return jnp.pad(x, ((0, 0), (0, width - x.shape[1])))


def _cols_full(g):
    return jnp.transpose(g, (1, 0, 2)).reshape(g.shape[1], -1)


def _ffn_fwd(x1, p, i, tag):
    h2 = _adaln_fwd(x1, p["norm_ffn"][i], p["sc_f"][i], p["sh_f"][i], f"ffn_norm_{tag}")
    u = _matmul(h2, p["fetch"](f"up{i}", h2), name=f"ffn_up_{tag}", tn=1408, b_shards=True)
    a = _conv_act_fwd(u, p["conv_w"][i], p["conv_b"][i], f"ffn_act_{tag}")
    g_f = p["g_f"][i]
    x2, f = _matmul(a, p["fetch"](f"down{i}", a), name=f"ffn_down_{tag}", tk=1408, out_dtypes=(F32, F32),
                    epilogue=lambda acc, x1, g: (x1 + (1.0 + g) * acc, acc), extras=(("mn", x1), ("n", g_f)))
    return x2, dict(h2=h2, u=u, a=a, f=f)


def _ffn_bwd(dx2, x1, saved, p, i, tag):
    d = x1.shape[1]
    w_up, w_down = p["fetch"](f"up{i}", None), p["fetch"](f"down{i}", None)
    df, dg_f = _residual_bwd(dx2, saved["f"], p["g_f"][i], f"ffn_res_bwd_{tag}")
    da = _matmul(df, w_down, tb=True, name=f"ffn_down_dx_{tag}", tn=1408)
    dw_down = _matmul(saved["a"], df, ta=True, name=f"ffn_down_dw_{tag}", tm=1408, out_dtypes=(BF16,))
    du, dcw, dcb = _conv_act_bwd(saved["u"], p["conv_w"][i], p["conv_b"][i], da, f"ffn_act_bwd_{tag}")
    dcw, dcb = (jnp.concatenate([t[0], t[1]], axis=1) for t in (dcw, dcb))
    tok = p["flush"](du)
    dh2 = _matmul(du, w_up, tb=True, name=f"ffn_up_dx_{tag}", tk=1408, a_halves=True, b_shards=True)
    dw_up = _matmul(saved["h2"], du, ta=True, name=f"ffn_up_dw_{tag}", tn=1408, out_dtypes=(BF16,), b_halves=True,
                    out_shards=True)
    tok = tok + p["send"](f"ffn{i}", [dw_up, dw_down.reshape(N_DEV, -1, d)])
    dx1, dsh, dsc, dgain = _adaln_bwd(x1, dh2, dx2, p["norm_ffn"][i] + tok, p["sc_f"][i], f"ffn_norm_bwd_{tag}")
    grads = dict(conv_w=dcw, conv_b=dcb, norm_ffn=dgain, sh_f=dsh, sc_f=dsc, g_f=dg_f)
    return dx1, grads


def _gla_layer_fwd(x, p, i):
    h1 = _adaln_fwd(x, p["norm_mix"][i], p["sc_m"][i], p["sh_m"][i], "gla_norm")
    w_t, w_tail_t, main = p["fetch"]("gla_in", h1)
    proj = _matmul(h1, w_t, tb=True, b_rows=main, name="gla_in")
    a_tail = _matmul(h1, w_tail_t, tb=True, name="gla_in_tail")
    dk_total = p["gla_wg_p"].shape[1]
    o, states = _gla_fwd(proj, a_tail, p["gla_wg_p"], p["gla_b_gate"], "gla_chunks")
    assert 2 * dk_total == o.shape[1]
    r = ("cols", proj, 2, o.shape[1])
    og = _gla_post_fwd(o, r, p["gla_norm"], "gla_post")
    x1, y = _matmul(og, p["fetch"]("gla_out", og), name="gla_out", out_dtypes=(F32, F32),
                    epilogue=lambda acc, x, g: (x + (1.0 + g) * acc, acc), extras=(("mn", x), ("n", p["g_m"][i])))
    return x1, dict(h1=h1, proj=proj, a_tail=a_tail, o=o, r=r, states=states, og=og, y=y)


def _gla_layer_bwd(dx1, x, sv, p, i):
    d = x.shape[1]
    (w_t, w_tail_t, main), w_out = p["fetch"]("gla_in", None), p["fetch"]("gla_out", None)
    dy, dg_m = _residual_bwd(dx1, sv["y"], p["g_m"][i], "gla_res_bwd")
    dog = _matmul(dy, w_out, tb=True, name="gla_out_dx")
    dw_out = _matmul(sv["og"], dy, ta=True, name="gla_out_dw", out_dtypes=(BF16,))
    tok = p["flush"](dog) + p["send"]("gla_out", [dw_out.reshape(N_DEV, -1, d)])
    d_o, d_r, dgn = _gla_post_bwd(sv["o"], sv["r"], p["gla_norm"] + tok, dog, "gla_post_bwd")
    dq, dk, dv, dga = _gla_bwd(sv["proj"], sv["a_tail"], p["gla_wg_p"], p["gla_b_gate"], sv["states"], d_o,
                               "gla_chunks_bwd")
    tok = p["flush"](dga)
    da_tail = _matmul(dga, p["gla_wg_p"], tb=True, name="gla_gate_dx", out_dtypes=(BF16,))
    dwg = _matmul(sv["a_tail"], dga, ta=True, name="gla_gate_dw")
    dbg = _rowwise(lambda t: (_colsum(t),), [("row", dga)], [("acc", dga.shape[1], F32)], name="gla_gate_db")[0]
    dproj = jnp.concatenate([dq, dk, dv, d_r], axis=1)
    dh_tail = _matmul(da_tail, w_tail_t, name="gla_in_tail_dx")
    dh1 = _matmul(dproj, w_t, b_rows=main, name="gla_in_dx", tk=2048,
                  epilogue=lambda acc, t: (acc + t,), extras=(("mn", dh_tail),))
    rank = p["gla_rank"]
    dw_main = _matmul(dproj, sv["h1"], ta=True, name="gla_in_dw", out_dtypes=(BF16,), out_rows=main + rank)
    dx, dsh, dsc, dgain = _adaln_bwd(x, dh1, dx1, p["norm_mix"][i] + tok, p["sc_m"][i], "gla_norm_bwd")
    grads = dict(gla_w_gate=dwg[:rank], gla_b_gate=dbg, gla_norm=dgn, norm_mix=dgain, sh_m=dsh, sc_m=dsc, g_m=dg_m,
                 gla_w_in_unsent=(dw_main, da_tail, sv["h1"]))
    return dx, grads


def _fox_layer_fwd(x, p, i):
    d = x.shape[1]
    hd = p["fox_q_norm"].shape[1]
    heads = d // hd
    s = x.shape[0]
    t = _tile(s, 512)
    h1 = _adaln_fwd(x, p["norm_mix"][i], p["sc_m"][i], p["sh_m"][i], "fox_norm")
    w_t, w_tail_t, main = p["fetch"]("fox_in", h1)
    proj = _matmul(h1, w_t, tb=True, b_rows=main, name="fox_in")
    fl = _matmul(h1, w_tail_t, tb=True, name="fox_in_tail")
    q, k, v, og = (("cols", proj, j, d) for j in range(4))
    qn, kn, vb = _fox_prep(q, k, v, p["fox_q_norm"], p["fox_k_norm"], d, hd, "fox_prep")
    cum = _fox_cum(fl, p["fox_bf_p"], "fox_cum")
    cum_t = jnp.transpose(cum[:, :heads])
    cum_col, cum_row = cum_t[:, :, None], cum_t.reshape(heads, s // t, 1, t)
    o, lse = _fox_attn_fwd(qn, kn, vb, cum_col, cum_row, hd, t, "fox_attn")
    act = _fox_gate_fwd(o, og, "fox_gate")
    x1, y = _matmul(act, p["fetch"]("fox_out", act), name="fox_out", out_dtypes=(F32, F32),
                    epilogue=lambda acc, x, g: (x + (1.0 + g) * acc, acc), extras=(("mn", x), ("n", p["g_m"][i])))
    return x1, dict(h1=h1, q=q, k=k, og=og, fl=fl, qn=qn, kn=kn, vb=vb, cum_col=cum_col, cum_row=cum_row,
                    o=o, lse=lse, act=act, y=y, t=t, hd=hd)


def _fox_layer_bwd(dx1, x, sv, p, i):
    d = x.shape[1]
    hd, t = sv["hd"], sv["t"]
    heads = d // hd
    s = x.shape[0]
    (w_t, w_tail_t, main), w_out = p["fetch"]("fox_in", None), p["fetch"]("fox_out", None)
    dy, dg_m = _residual_bwd(dx1, sv["y"], p["g_m"][i], "fox_res_bwd")
    dact = _matmul(dy, w_out, tb=True, name="fox_out_dx")
    dw_out = _matmul(sv["act"], dy, ta=True, name="fox_out_dw", out_dtypes=(BF16,))
    d_o, d_og = _fox_gate_bwd(sv["o"], sv["og"], dact, "fox_gate_bwd")
    tok_flush = p["flush"](d_og)
    dqn, dkn, dvb, dcq, dck = _fox_attn_bwd(sv["qn"], sv["kn"], sv["vb"], d_o, sv["o"], sv["lse"], sv["cum_col"],
                                            sv["cum_row"], hd, t, "fox_attn_bwd")
    dq, dk, gq, gk = _fox_prep_bwd(sv["q"], sv["k"], dqn, dkn, p["fox_q_norm"], p["fox_k_norm"], hd, "fox_prep_bwd")
    dcum = _pad_cols(jnp.transpose(dcq[:, :, 0] - dck.reshape(heads, s)))
    dfl, dbf = _fox_cum_bwd(dcum, sv["fl"], p["fox_bf_p"], "fox_cum_bwd")
    dfl_b = dfl.astype(BF16)
    dproj = jnp.concatenate([dq, dk, dvb, d_og], axis=1)
    dh_tail = _matmul(dfl_b, w_tail_t, name="fox_in_tail_dx")
    dh1 = _matmul(dproj, w_t, b_rows=main, name="fox_in_dx", tk=2048,
                  epilogue=lambda acc, tl: (acc + tl,), extras=(("mn", dh_tail),))
    dw_main = _matmul(dproj, sv["h1"], ta=True, name="fox_in_dw", out_dtypes=(BF16,), out_rows=main + heads)
    dw_in = _tail_rows(dfl_b, sv["h1"], dw_main, heads, "fox_in_tail_dw").reshape(N_DEV, -1, d)
    tok = tok_flush + p["send"]("fox", [dw_in, dw_out.reshape(N_DEV, -1, d)])
    dx, dsh, dsc, dgain = _adaln_bwd(x, dh1, dx1, p["norm_mix"][i] + tok, p["sc_m"][i], "fox_norm_bwd")
    grads = dict(fox_b_f=dbf[:, :heads], fox_q_norm=gq.reshape(heads, hd).sum(0, keepdims=True),
                 fox_k_norm=gk.reshape(heads, hd).sum(0, keepdims=True), norm_mix=dgain, sh_m=dsh, sc_m=dsc, g_m=dg_m)
    return dx, grads


SMALL = ("b_mod", "norm_mix", "norm_ffn", "gla_b_gate", "gla_norm", "fox_b_f", "fox_q_norm", "fox_k_norm",
         "ffn_conv_b", "norm_final")
SMALL_SHARDED = ("gla_w_gate", "ffn_conv_w")
BIG = ("gla_w_in", "gla_w_out", "fox_w_in", "fox_w_out", "ffn_w_up", "ffn_w_down")
WEIGHTS = ("w_mod", "b_mod", "norm_mix", "norm_ffn", "gla_w_in", "gla_w_gate", "gla_b_gate", "gla_norm", "gla_w_out",
           "fox_w_in", "fox_b_f", "fox_q_norm", "fox_k_norm", "fox_w_out", "ffn_w_up", "ffn_conv_w", "ffn_conv_b",
           "ffn_w_down", "norm_final")


def _pack(parts):
    flat = jnp.concatenate([p.reshape(-1) for p in parts])
    pad = (-flat.shape[0]) % 1024
    return jnp.pad(flat, (0, pad)).reshape(1, -1)


def _unpack(flat, shapes):
    out, off = [], 0
    for shp in shapes:
        n = 1
        for s in shp:
            n *= s
        out.append(flat[0, off:off + n].reshape(shp))
        off += n
    return out


def kernel(x, c, w_mod, b_mod, norm_mix, norm_ffn, gla_w_in, gla_w_gate, gla_b_gate, gla_norm, gla_w_out, fox_w_in, fox_b_f, fox_q_norm, fox_k_norm, fox_w_out, ffn_w_up, ffn_conv_w, ffn_conv_b, ffn_w_down, norm_final, loss_target, m_w_mod, m_b_mod, m_norm_mix, m_norm_ffn, m_gla_w_in, m_gla_w_gate, m_gla_b_gate, m_gla_norm, m_gla_w_out, m_fox_w_in, m_fox_b_f, m_fox_q_norm, m_fox_k_norm, m_fox_w_out, m_ffn_w_up, m_ffn_conv_w, m_ffn_conv_b, m_ffn_w_down, m_norm_final, v_w_mod, v_b_mod, v_norm_mix, v_norm_ffn, v_gla_w_in, v_gla_w_gate, v_gla_b_gate, v_gla_norm, v_gla_w_out, v_fox_w_in, v_fox_b_f, v_fox_q_norm, v_fox_k_norm, v_fox_w_out, v_ffn_w_up, v_ffn_conv_w, v_ffn_conv_b, v_ffn_w_down, v_norm_final):
    w = dict(w_mod=w_mod, b_mod=b_mod, norm_mix=norm_mix, norm_ffn=norm_ffn, gla_w_in=gla_w_in, gla_w_gate=gla_w_gate,
             gla_b_gate=gla_b_gate, gla_norm=gla_norm, gla_w_out=gla_w_out, fox_w_in=fox_w_in, fox_b_f=fox_b_f,
             fox_q_norm=fox_q_norm, fox_k_norm=fox_k_norm, fox_w_out=fox_w_out, ffn_w_up=ffn_w_up,
             ffn_conv_w=ffn_conv_w, ffn_conv_b=ffn_conv_b, ffn_w_down=ffn_w_down, norm_final=norm_final)
    mom_m = dict(w_mod=m_w_mod, b_mod=m_b_mod, norm_mix=m_norm_mix, norm_ffn=m_norm_ffn, gla_w_in=m_gla_w_in,
                 gla_w_gate=m_gla_w_gate, gla_b_gate=m_gla_b_gate, gla_norm=m_gla_norm, gla_w_out=m_gla_w_out,
                 fox_w_in=m_fox_w_in, fox_b_f=m_fox_b_f, fox_q_norm=m_fox_q_norm, fox_k_norm=m_fox_k_norm,
                 fox_w_out=m_fox_w_out, ffn_w_up=m_ffn_w_up, ffn_conv_w=m_ffn_conv_w, ffn_conv_b=m_ffn_conv_b,
                 ffn_w_down=m_ffn_w_down, norm_final=m_norm_final)
    mom_v = dict(w_mod=v_w_mod, b_mod=v_b_mod, norm_mix=v_norm_mix, norm_ffn=v_norm_ffn, gla_w_in=v_gla_w_in,
                 gla_w_gate=v_gla_w_gate, gla_b_gate=v_gla_b_gate, gla_norm=v_gla_norm, gla_w_out=v_gla_w_out,
                 fox_w_in=v_fox_w_in, fox_b_f=v_fox_b_f, fox_q_norm=v_fox_q_norm, fox_k_norm=v_fox_k_norm,
                 fox_w_out=v_fox_w_out, ffn_w_up=v_ffn_w_up, ffn_conv_w=v_ffn_conv_w, ffn_conv_b=v_ffn_conv_b,
                 ffn_w_down=v_ffn_w_down, norm_final=v_norm_final)

    me = 4 * lax.axis_index("x") + 2 * lax.axis_index("y") + lax.axis_index("c")
    xs, target = x[0], loss_target[0]
    s, d = xs.shape
    depth = w_mod.shape[0]
    mod_cols = w_mod.shape[2]
    rank = gla_w_gate.shape[1]
    hd = fox_q_norm.shape[1]
    fox_heads = d // hd
    dk_total = gla_w_gate.shape[2] * N_DEV

    cond = c * (1.0 / (1.0 + jnp.exp(-c)))
    g, _ = _exchange([gla_w_gate[0], ffn_conv_w, cond], "gather_small", scatter=False)
    cond_all = g[2][:, 0, :]

    cond_pad = jnp.pad(cond_all, ((0, 16 - N_DEV), (0, 0)))
    mod_part = []
    for i in range(depth):
        b_cols = lax.dynamic_slice(b_mod[i:i + 1], (0, me * mod_cols), (1, mod_cols))
        mod_part.append(_matmul(cond_pad, w_mod[i], name=f"mod_{i}", tn=768,
                                epilogue=lambda acc, b: (acc + b,), extras=(("n", b_cols),))[:N_DEV])
    (mod_all,), tok_mod = _exchange([jnp.stack(mod_part)], "gather_mod", scatter=False)
    mod = lax.dynamic_index_in_dim(mod_all, me, axis=2, keepdims=False)
    mod = jnp.transpose(mod, (1, 0, 2)).reshape(depth, 6, 1, d)

    big_names = ["gla_in", "gla_out", "up0", "down0", "fox_in", "fox_out", "up1", "down1"]
    first = [jnp.transpose(gla_w_in[0] + tok_mod).astype(BF16), gla_w_out[0].astype(BF16)]
    handles, tok_first = _gather2_start(first, "gather_weights_start_first")
    rest = [ffn_w_up[0] + tok_first, ffn_w_down[0], jnp.transpose(fox_w_in[0]), fox_w_out[0], ffn_w_up[1],
            ffn_w_down[1]]
    handles_rest, tok0 = _gather2_start([t.astype(BF16) for t in rest], "gather_weights_start_rest")
    handles = handles + handles_rest
    ready, forwarded = {}, {}

    def split_tail(full_t, tail):
        main = full_t.shape[0] - tail
        return full_t, jnp.pad(full_t[main:], ((0, LANE - tail), (0, 0))), main

    def forward(idx, after):
        key = big_names[idx]
        forwarded[key] = _gather2_forward(handles[idx], after, f"gather_{key}_forward")

    def fetch(key, after):
        if key not in ready:
            idx = big_names.index(key)
            if idx == 0:
                forward(0, after)
            handle, _ = forwarded[key]
            _, full = _gather2_wait(handle, after, f"gather_{key}_wait")
            if idx + 1 < len(big_names):
                forward(idx + 1, full)
            if key == "gla_in":
                ready[key] = split_tail(full.reshape(-1, d), rank)
            elif key == "fox_in":
                ready[key] = split_tail(full.reshape(-1, d), fox_heads)
            elif key.startswith("up"):
                ready[key] = full
            else:
                ready[key] = full.reshape(-1, d)
        return ready[key]

    pending, sent = [], {}
    core = lax.axis_index("c").astype(jnp.int32).reshape(1)
    chip = 2 * lax.axis_index("x") + lax.axis_index("y")

    def send(key, pieces):
        hs, tok = _split_start(pieces, _pair_plan, f"scatter_{key}_pair_start")
        pending.append((key, hs))
        return tok

    def flush(after):
        tok = 0.0
        while pending:
            key, hs = pending.pop(0)
            mine, partner = _split_wait(hs, _pair_plan, after, f"scatter_{key}_pair_wait")
            sums = [_pair_sum(pc, pt, core, f"scatter_{key}_pair_sum{a}")
                    for a, (pc, pt) in enumerate(zip(mine, partner))]
            sent[key], t = _split_start(sums, _chip_plan, f"scatter_{key}_chip_start")
            tok = tok + t
        return tok

    p = dict(
        fetch=fetch, send=send, flush=flush,
        gla_wg_p=jnp.pad(_cols_full(g[0]), ((0, LANE - rank), (0, 0))),
        conv_w=[jnp.transpose(g[1][:, i], (1, 0, 2)).reshape(ffn_conv_w.shape[1], -1) for i in range(depth)],
        conv_b=[ffn_conv_b[i:i + 1] for i in range(depth)],
        gla_b_gate=gla_b_gate, gla_norm=gla_norm, fox_q_norm=fox_q_norm, fox_k_norm=fox_k_norm,
        fox_bf_p=_pad_cols(fox_b_f), gla_rank=rank,
        norm_mix=[norm_mix[i:i + 1] + (tok0 if i == 0 else 0.0) for i in range(depth)],
        norm_ffn=[norm_ffn[i:i + 1] for i in range(depth)],
    )

    for j, nm in enumerate(("sh_m", "sc_m", "g_m", "sh_f", "sc_f", "g_f")):
        p[nm] = [mod[i, j] for i in range(depth)]

    acts, saved = [xs], []
    for i in range(depth):
        layer_fwd = _gla_layer_fwd if i % 2 == 0 else _fox_layer_fwd
        x1, sv_mix = layer_fwd(acts[-1], p, i)
        x2, sv_ffn = _ffn_fwd(x1, p, i, str(i))
        saved.append((acts[-1], x1, sv_mix, sv_ffn))
        acts.append(x2)
    dx, d_norm_final, loss_part = _final_loss(acts[-1], target, norm_final.reshape(1, d), "final_loss")

    lg = [None] * depth
    for i in reversed(range(depth)):
        x_in, x1, sv_mix, sv_ffn = saved[i]
        dx, g_ffn = _ffn_bwd(dx, x1, sv_ffn, p, i, str(i))
        layer_bwd = _gla_layer_bwd if i % 2 == 0 else _fox_layer_bwd
        dx, g_mix = layer_bwd(dx, x_in, sv_mix, p, i)
        lg[i] = {**g_ffn, **g_mix}
    grad_x = dx[None]

    gla_l = [i for i in range(depth) if i % 2 == 0]
    fox_l = [i for i in range(depth) if i % 2 == 1]
    small_parts = dict(
        norm_mix=jnp.concatenate([lg[i]["norm_mix"] for i in range(depth)]),
        norm_ffn=jnp.concatenate([lg[i]["norm_ffn"] for i in range(depth)]),
        gla_b_gate=jnp.concatenate([lg[i]["gla_b_gate"] for i in gla_l]),
        gla_norm=jnp.concatenate([lg[i]["gla_norm"] for i in gla_l]),
        fox_b_f=jnp.concatenate([lg[i]["fox_b_f"] for i in fox_l]),
        fox_q_norm=jnp.concatenate([lg[i]["fox_q_norm"] for i in fox_l]),
        fox_k_norm=jnp.concatenate([lg[i]["fox_k_norm"] for i in fox_l]),
        ffn_conv_b=jnp.concatenate([lg[i]["conv_b"] for i in range(depth)]),
        norm_final=d_norm_final,
        gla_w_gate=jnp.stack([lg[i]["gla_w_gate"] for i in gla_l]),
        ffn_conv_w=jnp.stack([lg[i]["conv_w"] for i in range(depth)]),
        loss=loss_part[:, :1],
    )
    order = ("norm_mix", "norm_ffn", "gla_b_gate", "gla_norm", "fox_b_f", "fox_q_norm", "fox_k_norm", "ffn_conv_b",
             "norm_final", "gla_w_gate", "ffn_conv_w", "loss")
    packed = _pack([small_parts[nm] for nm in order])
    dmod = jnp.stack([jnp.concatenate([lg[i][nm] for nm in ("sh_m", "sc_m", "g_m", "sh_f", "sc_f", "g_f")], axis=1)
                      for i in range(depth)])
    hs_small, tok_small = _split_start([packed[None], dmod[None]], _all_plan, "gather_small_grads_start",
                                       land_blocks=N_DEV)
    dw_main, da_tail, h1_gla = lg[0]["gla_w_in_unsent"]
    dw_in_t = _tail_rows(da_tail + tok_small.astype(BF16), h1_gla, dw_main, rank, "gla_in_tail_dw")
    send("gla_in", [dw_in_t.reshape(N_DEV, -1, d)])
    started = pending[-1][1][0][0]

    received = {}

    def arrive(key, after):
        sums, lands = _split_wait(sent[key], _chip_plan, after, f"scatter_{key}_chip_wait")
        received[key] = list(zip(lands, sums))

    for key in ("ffn1", "fox", "ffn0", "gla_out"):
        arrive(key, started)

    out_g, out_d, out_m, out_v = {}, {}, {}, {}

    chip_idx = chip.astype(jnp.int32).reshape(1)

    def update(nm, g_arr, transposed=False):
        swap = (lambda t: jnp.transpose(t, (0, 2, 1))) if transposed else (lambda t: t)
        if isinstance(g_arr, list):
            res = _adamw_pieces(swap(w[nm]), [t[0] for t in g_arr], [t[1] for t in g_arr], chip_idx,
                                swap(mom_m[nm]), swap(mom_v[nm]), f"adamw_{nm}")
        else:
            res = _adamw(w[nm], g_arr, mom_m[nm], mom_v[nm], f"adamw_{nm}")
        out_g[nm], out_d[nm], out_m[nm], out_v[nm] = (swap(t) for t in res)

    update("gla_w_out", [received["gla_out"][0]])
    update("fox_w_out", [received["fox"][1]])
    tok_flush = flush(out_g["fox_w_out"])
    update("ffn_w_up", [received[f"ffn{i}"][0] for i in range(depth)])
    update("fox_w_in", [received["fox"][0]], transposed=True)
    update("ffn_w_down", [received[f"ffn{i}"][1] for i in range(depth)])

    updated = ("gla_w_out", "fox_w_in", "fox_w_out", "ffn_w_up", "ffn_w_down")
    (packed_mine, dmod_mine), (packed_all, dmod_all) = _split_wait(
        hs_small, _all_plan, [out_d[nm] for nm in updated], "gather_small_grads_wait")
    packed_all = lax.dynamic_update_slice(packed_all, packed_mine + tok_flush, (me, 0, 0))
    dmod_all = lax.dynamic_update_slice(dmod_all, dmod_mine, (me, 0, 0, 0))
    summed = _unpack(_sum8(packed_all, "sum_small_grads"), [small_parts[nm].shape for nm in order])
    small_g = dict(zip(order, summed))
    loss = small_g["loss"][0, 0]
    dmod_all = dmod_all[:, :, 0, :]
    grads = {}
    cond_t = _pad_cols(jnp.transpose(cond_all)).astype(BF16)
    dmod_cols = lax.dynamic_slice(dmod_all, (0, 0, me * mod_cols), (N_DEV, depth, mod_cols))
    g_w_mod = []
    for i in range(depth):
        rhs = jnp.pad(dmod_cols[:, i], ((0, LANE - N_DEV), (0, 0)))
        g_w_mod.append(_matmul(cond_t, rhs, name=f"mod_dw_{i}", tn=768))
    grads["w_mod"] = jnp.stack(g_w_mod)
    small_g["b_mod"] = _sum8(dmod_all.reshape(N_DEV, 1, -1), "sum_b_mod").reshape(depth, -1)
    update("w_mod", grads["w_mod"])

    gate_cols = gla_w_gate.shape[2]
    conv_cols = ffn_conv_w.shape[2]
    local_small = dict(small_g)
    local_small["gla_w_gate"] = lax.dynamic_slice_in_dim(small_g["gla_w_gate"], me * gate_cols, gate_cols, axis=2)
    local_small["ffn_conv_w"] = lax.dynamic_slice_in_dim(small_g["ffn_conv_w"], me * conv_cols, conv_cols, axis=2)
    names = SMALL + SMALL_SHARDED
    shapes = [w[nm].shape for nm in names]
    res = _adamw(_pack([w[nm] for nm in names])[None], _pack([local_small[nm] for nm in names])[None],
                 _pack([mom_m[nm] for nm in names])[None], _pack([mom_v[nm] for nm in names])[None], "adamw_small")
    for tgt, flat in zip((out_g, out_d, out_m, out_v), res):
        for nm, arr in zip(names, _unpack(flat[0], shapes)):
            tgt[nm] = arr

    arrive("gla_in", [out_d[nm] for nm in updated + ("w_mod",)])
    update("gla_w_in", [received["gla_in"][0]], transposed=True)

    return (loss, grad_x, *[out_g[n] for n in WEIGHTS], *[out_d[n] for n in WEIGHTS],
            *[out_m[n] for n in WEIGHTS], *[out_v[n] for n in WEIGHTS])
```

```python
import math

import jax
import jax.numpy as jnp
from jax import lax
from jax.experimental import pallas as pl
from jax.experimental.pallas import tpu as pltpu

F32, BF16 = jnp.float32, jnp.bfloat16
N_DEV = 8
GLA_HEADS = 4
GLA_TAU = 16.0
GLA_CHUNK = 64
NORM_EPS = 1e-6
ADAM_LR, ADAM_B1, ADAM_B2, ADAM_EPS, ADAM_WD, ADAM_STEP = 0.001, 0.9, 0.999, 1e-08, 0.01, 10
LANE = 128
VMEM_LIMIT = 56 * 1024 * 1024
NEG = -1e30


def _pcall(body, **kw):
    return pl.pallas_call(body, **kw)


def _params(n_axes):
    return pltpu.CompilerParams(dimension_semantics=("arbitrary",) * n_axes, vmem_limit_bytes=VMEM_LIMIT)


def _tile(dim, pref):
    if dim <= pref:
        return dim
    t = pref
    while dim % t:
        t -= LANE
    assert t > 0, (dim, pref)
    return t


def _dot(a, b, ta=False, tb=False):
    dims = (((0,) if ta else (1,), (1,) if tb else (0,)), ((), ()))
    return lax.dot_general(a.astype(BF16), b.astype(BF16), dims, preferred_element_type=F32)


def _split3(x):
    hi = x.astype(BF16)
    r1 = x - hi.astype(F32)
    mid = r1.astype(BF16)
    lo = (r1 - mid.astype(F32)).astype(BF16)
    return hi, mid, lo


def _tri_matmul(tri, x):
    hi, mid, lo = _split3(x)
    return _dot(tri, hi) + _dot(tri, mid) + _dot(tri, lo)


def _tri(n, upper=False):
    r = lax.broadcasted_iota(jnp.int32, (n, n), 0)
    c = lax.broadcasted_iota(jnp.int32, (n, n), 1)
    return jnp.where((r <= c) if upper else (r >= c), 1.0, 0.0).astype(BF16)


def _log_sigmoid(x):
    return jnp.minimum(x, 0.0) - jnp.log(1.0 + jnp.exp(-jnp.abs(x)))


def _sigmoid(x):
    return 1.0 / (1.0 + jnp.exp(-x))


def _silu(x):
    return x * _sigmoid(x)


def _dsilu(x):
    s = _sigmoid(x)
    return s * (1.0 + x * (1.0 - s))


def _matmul(a, b, *, name, ta=False, tb=False, out_dtypes=(F32,), tm=1024, tn=1024, tk=2048,
            epilogue=None, extras=(), a_halves=False, b_halves=False, b_shards=False, out_shards=False,
            b_rows=None, out_rows=None, b_layer=None, into=None):
    a_parts = list(a) if isinstance(a, (list, tuple)) else [a]
    if a_halves:
        assert not ta
        m, k = a.shape[1], 2 * a.shape[2]
    else:
        cols = sum(p.shape[1] for p in a_parts)
        m, k = (cols, a_parts[0].shape[0]) if ta else (a_parts[0].shape[0], cols)
    if b_halves:
        assert not tb and b.shape[1] == k
        n = 2 * b.shape[2]
    elif b_shards:
        n = b.shape[1] if tb else N_DEV * b.shape[2]
        assert (N_DEV * b.shape[2] if tb else b.shape[1]) == k, (a.shape, b.shape, ta, tb)
    elif b_layer is not None:
        assert not tb and b.shape[1] == k
        n = b.shape[2]
    else:
        rows = b.shape[0] if b_rows is None else b_rows
        n = rows if tb else b.shape[1]
        assert (b.shape[1] if tb else rows) == k, (a.shape, b.shape, ta, tb)
    n_unit = n // N_DEV if (out_shards or (b_shards and not tb)) else (n // 2 if b_halves else n)
    k_unit = k // N_DEV if (b_shards and tb) else (k // 2 if a_halves else k)
    m_unit = m
    if len(a_parts) > 1:
        width = math.gcd(*[p.shape[1] for p in a_parts])
        m_unit, k_unit = (width, k_unit) if ta else (m_unit, width)
    tm, tn, tk = _tile(m_unit, tm), _tile(n_unit, tn), _tile(k_unit, tk)
    nk = k // tk
    col_tile = tm if ta else tk
    starts = [0]
    for part in a_parts:
        assert part.shape[1] % col_tile == 0 or len(a_parts) == 1, (part.shape, col_tile)
        starts.append(starts[-1] + part.shape[1] // col_tile)

    def part_spec(p):
        def pos(t):
            return jnp.clip(t - starts[p], 0, starts[p + 1] - starts[p] - 1)

        if ta:
            return pl.BlockSpec((tk, tm), lambda i, j, kk: (kk, pos(i)))
        return pl.BlockSpec((tm, tk), lambda i, j, kk: (i, pos(kk)))

    if a_halves:
        a_specs = [pl.BlockSpec((None, tm, tk), lambda i, j, kk: (kk // (nk // 2), i, kk % (nk // 2)))]
    else:
        a_specs = [part_spec(p) for p in range(len(a_parts))]
    n_a = len(a_parts)
    n_per, k_per = n // tn // N_DEV, nk // N_DEV
    if b_halves:
        b_spec = pl.BlockSpec((None, tk, tn), lambda i, j, kk: (j // (n // tn // 2), kk, j % (n // tn // 2)))
    elif b_shards and tb:
        b_spec = pl.BlockSpec((None, tn, tk), lambda i, j, kk: (kk // k_per, j, kk % k_per))
    elif b_shards:
        b_spec = pl.BlockSpec((None, tk, tn), lambda i, j, kk: (j // n_per, kk, j % n_per))
    elif b_layer is not None:
        b_spec = pl.BlockSpec((None, tk, tn), lambda i, j, kk: (b_layer, kk, j))
    elif tb:
        b_spec = pl.BlockSpec((tn, tk), lambda i, j, kk: (j, kk))
    else:
        b_spec = pl.BlockSpec((tk, tn), lambda i, j, kk: (kk, j))
    ex_specs = []
    for kind, arr in extras:
        if kind == "mn":
            assert arr.shape == (m, n), (arr.shape, m, n)
            ex_specs.append(pl.BlockSpec((tm, tn), lambda i, j, kk: (i, j)))
        else:
            assert arr.shape == (1, n), (arr.shape, n)
            ex_specs.append(pl.BlockSpec((1, tn), lambda i, j, kk: (0, j)))
    n_ex, n_out = len(extras), len(out_dtypes)

    def body(*refs):
        a_refs, b_ref, rest = refs[:n_a], refs[n_a], refs[n_a + 1:]
        ex, outs, acc = rest[:n_ex], rest[-1 - n_out:-1], rest[-1]
        kk = pl.program_id(2)

        @pl.when(kk == 0)
        def _():
            acc[...] = jnp.zeros_like(acc)

        if n_a == 1:
            acc[...] += _dot(a_refs[0][...], b_ref[...], ta, tb)
        else:
            tile = pl.program_id(0) if ta else kk
            for p, a_ref in enumerate(a_refs):
                @pl.when((tile >= starts[p]) & (tile < starts[p + 1]))
                def _(a_ref=a_ref):
                    acc[...] += _dot(a_ref[...], b_ref[...], ta, tb)

        @pl.when(kk == nk - 1)
        def _():
            if epilogue is None:
                vals = (acc[...],)
            else:
                vals = epilogue(acc[...], *[e[...] for e in ex])
            for o, v in zip(outs, vals):
                o[...] = v.astype(o.dtype)

    if out_shards:
        out_spec = pl.BlockSpec((None, tm, tn), lambda i, j, kk: (j // n_per, i, j % n_per))
        out_dims = (N_DEV, m, n // N_DEV)
    elif into is not None:
        out_spec = pl.BlockSpec((None, tm, tn), lambda i, j, kk: (into[1], i, j))
        out_dims = into[0].shape
    else:
        out_spec = pl.BlockSpec((tm, tn), lambda i, j, kk: (i, j))
        out_dims = (m if out_rows is None else out_rows, n)
    operands = [*a_parts, b, *[arr for _, arr in extras]]
    aliases = {}
    if into is not None:
        assert n_out == 1 and into[0].shape[1:] == (m, n) and into[0].dtype == out_dtypes[0]
        aliases = {len(operands): 0}
        operands.append(into[0])
    res = _pcall(
        body, name=name, grid=(m // tm, n // tn, nk),
        in_specs=a_specs + [b_spec] + ex_specs + [pl.BlockSpec(memory_space=pl.ANY)] * len(aliases),
        out_specs=[out_spec] * n_out,
        out_shape=[jax.ShapeDtypeStruct(out_dims, d) for d in out_dtypes],
        scratch_shapes=[pltpu.VMEM((tm, tn), F32)],
        input_output_aliases=aliases,
        compiler_params=_params(3),
    )(*operands)
    return res[0] if n_out == 1 else res


def _tail_rows(a, b, into, rows, name, tn=1024):
    k, n = b.shape
    m_total = into.shape[0]
    tn = _tile(n, tn)

    def body(a_ref, b_ref, into_ref, out_ref):
        out_ref[...] = _dot(a_ref[...], b_ref[...], ta=True)[:rows].astype(out_ref.dtype)

    return _pcall(
        body, name=name, grid=(n // tn,),
        in_specs=[pl.BlockSpec((k, a.shape[1]), lambda j: (0, 0)), pl.BlockSpec((k, tn), lambda j: (0, j)),
                  pl.BlockSpec(memory_space=pl.ANY)],
        out_specs=pl.BlockSpec((rows, tn), lambda j: (m_total // rows - 1, j)),
        out_shape=jax.ShapeDtypeStruct(into.shape, into.dtype),
        input_output_aliases={2: 0}, compiler_params=_params(1),
    )(a, b, into)


def _rowwise(fn, ins, outs, *, name, tr=128):
    rows = next(e[1].shape[0] for e in ins if e[0] != "full")
    tr = _tile(rows, tr)
    in_specs = []
    for entry in ins:
        kind, arr = entry[0], entry[1]
        assert kind == "full" or (arr.shape[0] == rows and arr.ndim == 2)
        if kind == "row":
            in_specs.append(pl.BlockSpec((tr, arr.shape[1]), lambda i: (i, 0)))
        elif kind == "cols":
            in_specs.append(pl.BlockSpec((tr, entry[3]), lambda i, cb=entry[2]: (i, cb)))
        else:
            in_specs.append(pl.BlockSpec(arr.shape, lambda i, nd=arr.ndim: (0,) * nd))
    out_specs, out_shape = [], []
    for kind, w, dt in outs:
        if kind == "row":
            out_specs.append(pl.BlockSpec((tr, w), lambda i: (i, 0)))
            out_shape.append(jax.ShapeDtypeStruct((rows, w), dt))
        else:
            out_specs.append(pl.BlockSpec((1, w), lambda i: (0, 0)))
            out_shape.append(jax.ShapeDtypeStruct((1, w), dt))
    n_in = len(ins)

    def body(*refs):
        i = pl.program_id(0)
        vals = fn(*[r[...] for r in refs[:n_in]])
        for (kind, _, _), o, v in zip(outs, refs[n_in:], vals):
            if kind == "row":
                o[...] = v.astype(o.dtype)
            else:
                @pl.when(i == 0)
                def _(o=o):
                    o[...] = jnp.zeros_like(o)

                o[...] += v.astype(o.dtype)

    return _pcall(body, name=name, grid=(rows // tr,), in_specs=in_specs, out_specs=out_specs,
                  out_shape=out_shape, compiler_params=_params(1))(*[e[1] for e in ins])


def _colsum(x):
    return jnp.sum(x, axis=0, keepdims=True)


def _norm_stats(x):
    rstd = lax.rsqrt(jnp.mean(x * x, axis=-1, keepdims=True) + NORM_EPS)
    return x * rstd, rstd


def _norm_bwd(dxhat, xhat, rstd):
    return rstd * (dxhat - xhat * jnp.mean(dxhat * xhat, axis=-1, keepdims=True))


def _adaln_fwd(x, gain, sc, sh, name):
    def fn(x, gain, sc, sh):
        xhat, _ = _norm_stats(x)
        return ((xhat * gain) * (1.0 + sc) + sh,)

    return _rowwise(fn, [("row", x), ("full", gain), ("full", sc), ("full", sh)],
                    [("row", x.shape[1], BF16)], name=name)[0]


def _adaln_bwd(x, dh, dres, gain, sc, name):
    d = x.shape[1]

    def fn(x, dh, dres, gain, sc):
        xhat, rstd = _norm_stats(x)
        dxhat = dh * (gain * (1.0 + sc))
        dx = dres + _norm_bwd(dxhat, xhat, rstd)
        return dx, _colsum(dh), _colsum(dh * (xhat * gain)), _colsum(dh * xhat * (1.0 + sc))

    return _rowwise(fn, [("row", x), ("row", dh), ("row", dres), ("full", gain), ("full", sc)],
                    [("row", d, F32), ("acc", d, F32), ("acc", d, F32), ("acc", d, F32)], name=name)


def _residual_bwd(dx, y, g, name):
    d = dx.shape[1]

    def fn(dx, y, g):
        return dx * (1.0 + g), _colsum(dx * y)

    return _rowwise(fn, [("row", dx), ("row", y), ("full", g)], [("row", d, BF16), ("acc", d, F32)], name=name)


def _final_loss(x, target, gain, name):
    d = x.shape[1]

    def fn(x, t, gain):
        xhat, rstd = _norm_stats(x)
        err = xhat * gain - t
        dy = err * (1.0 / d)
        loss = 0.5 * jnp.sum(jnp.mean(err * err, axis=-1, keepdims=True), axis=0, keepdims=True)
        dx = _norm_bwd(dy * gain, xhat, rstd)
        return dx, _colsum(dy * xhat), jnp.broadcast_to(loss, (1, LANE))

    return _rowwise(fn, [("row", x), ("row", target), ("full", gain)],
                    [("row", d, F32), ("acc", d, F32), ("acc", LANE, F32)], name=name)


def _gla_gates(q, k, a, wg, bg, scale, c):
    ga = _dot(a, wg) + bg
    la = _log_sigmoid(ga) * (1.0 / GLA_TAU)
    b = _tri_matmul(_tri(c), la)
    bl = _colsum(la)
    eb, enb, eend = jnp.exp(b), jnp.exp(-b), jnp.exp(bl - b)
    q = q * scale
    return dict(ga=ga, eb=eb, enb=enb, eend=eend, dec=jnp.exp(bl), q_dec=q * eb, k_inv=k * enb, k_end=k * eend)


def _causal(c):
    return lax.broadcasted_iota(jnp.int32, (c, c), 0) >= lax.broadcasted_iota(jnp.int32, (c, c), 1)


def _gla_specs(heads, c, dk, dv, chunk):
    return [
        pl.BlockSpec((c, heads * dk), lambda n: (chunk(n), 0)),
        pl.BlockSpec((c, heads * dk), lambda n: (chunk(n), 1)),
        pl.BlockSpec((c, heads * dv), lambda n: (chunk(n), 1)),
        pl.BlockSpec((c, LANE), lambda n: (chunk(n), 0)),
        pl.BlockSpec((LANE, heads * dk), lambda n: (0, 0)),
        pl.BlockSpec((1, heads * dk), lambda n: (0, 0)),
    ]


def _gla_fwd(proj, a_tail, wg_p, bg, name):
    s = proj.shape[0]
    heads, c = GLA_HEADS, GLA_CHUNK
    dk = wg_p.shape[1] // heads
    dv = 2 * dk
    n_chunks = s // c
    scale = dk ** -0.5

    def body(q_ref, k_ref, v_ref, a_ref, wg_ref, bg_ref, o_ref, st_ref, state):
        @pl.when(pl.program_id(0) == 0)
        def _():
            state[...] = jnp.zeros_like(state)

        a = a_ref[...]
        for h in range(heads):
            sk, sv = slice(h * dk, (h + 1) * dk), slice(h * dv, (h + 1) * dv)
            g = _gla_gates(q_ref[:, sk], k_ref[:, sk], a, wg_ref[:, sk], bg_ref[:, sk], scale, c)
            v = v_ref[:, sv]
            st = state[h]
            attn = jnp.where(_causal(c), _dot(g["q_dec"], g["k_inv"], tb=True), 0.0)
            o_ref[:, sv] = _dot(attn, v) + _dot(g["q_dec"], st, tb=True)
            st_ref[h] = st.astype(st_ref.dtype)
            state[h] = g["dec"] * st + _dot(v, g["k_end"], ta=True)

    return _pcall(
        body, name=name, grid=(n_chunks,),
        in_specs=_gla_specs(heads, c, dk, dv, lambda n: n),
        out_specs=[pl.BlockSpec((c, heads * dv), lambda n: (n, 0)),
                   pl.BlockSpec((heads, None, dv, dk), lambda n: (0, n, 0, 0))],
        out_shape=[jax.ShapeDtypeStruct((s, heads * dv), F32),
                   jax.ShapeDtypeStruct((heads, n_chunks, dv, dk), BF16)],
        scratch_shapes=[pltpu.VMEM((heads, dv, dk), F32)],
        compiler_params=_params(1),
    )(proj, proj, proj, a_tail, wg_p, bg)


def _gla_bwd(proj, a_tail, wg_p, bg, states, d_o, name):
    s = proj.shape[0]
    heads, c = GLA_HEADS, GLA_CHUNK
    dk = wg_p.shape[1] // heads
    dv = 2 * dk
    n_chunks = s // c
    scale = dk ** -0.5

    def body(q_ref, k_ref, v_ref, a_ref, wg_ref, bg_ref, st_ref, do_ref, dq_ref, dk_ref, dv_ref, dga_ref, dstate):
        @pl.when(pl.program_id(0) == 0)
        def _():
            dstate[...] = jnp.zeros_like(dstate)

        a = a_ref[...]
        mask = _causal(c)
        for h in range(heads):
            sk, sv = slice(h * dk, (h + 1) * dk), slice(h * dv, (h + 1) * dv)
            g = _gla_gates(q_ref[:, sk], k_ref[:, sk], a, wg_ref[:, sk], bg_ref[:, sk], scale, c)
            v, st, dst, d_out = v_ref[:, sv], st_ref[h], dstate[h], do_ref[:, sv]
            q_dec, k_inv, k_end = g["q_dec"], g["k_inv"], g["k_end"]
            attn = jnp.where(mask, _dot(q_dec, k_inv, tb=True), 0.0)
            d_attn = jnp.where(mask, _dot(d_out, v, tb=True), 0.0)
            d_qdec = _dot(d_attn, k_inv) + _dot(d_out, st)
            d_kinv = _dot(d_attn, q_dec, ta=True)
            d_kend = _dot(v, dst)
            dv_ref[:, sv] = (_dot(attn, d_out, ta=True) + _dot(k_end, dst, tb=True)).astype(dv_ref.dtype)
            d_dec = jnp.sum(dst * st.astype(F32), axis=0, keepdims=True)
            dstate[h] = g["dec"] * dst + _dot(d_out, q_dec, ta=True)

            dq_ref[:, sk] = (d_qdec * (scale * g["eb"])).astype(dq_ref.dtype)
            dk_ref[:, sk] = (d_kinv * g["enb"] + d_kend * g["eend"]).astype(dk_ref.dtype)
            kk = d_kend * k_end
            db = d_qdec * q_dec - d_kinv * k_inv - kk
            dbl = jnp.sum(kk, axis=0, keepdims=True) + d_dec * g["dec"]
            last = lax.broadcasted_iota(jnp.int32, db.shape, 0) == c - 1
            db = db + jnp.where(last, dbl, 0.0)
            dla = _tri_matmul(_tri(c, upper=True), db)
            dga_ref[:, sk] = dla * (1.0 / GLA_TAU) * _sigmoid(-g["ga"])

    chunk = lambda n: n_chunks - 1 - n
    rev = lambda n: (chunk(n), 0)
    return _pcall(
        body, name=name, grid=(n_chunks,),
        in_specs=_gla_specs(heads, c, dk, dv, chunk) + [
            pl.BlockSpec((heads, None, dv, dk), lambda n: (0, chunk(n), 0, 0)),
            pl.BlockSpec((c, heads * dv), rev)],
        out_specs=[pl.BlockSpec((c, heads * dk), rev), pl.BlockSpec((c, heads * dk), rev),
                   pl.BlockSpec((c, heads * dv), rev), pl.BlockSpec((c, heads * dk), rev)],
        out_shape=[jax.ShapeDtypeStruct((s, heads * dk), BF16), jax.ShapeDtypeStruct((s, heads * dk), BF16),
                   jax.ShapeDtypeStruct((s, heads * dv), BF16), jax.ShapeDtypeStruct((s, heads * dk), F32)],
        scratch_shapes=[pltpu.VMEM((heads, dv, dk), F32)],
        compiler_params=_params(1),
    )(proj, proj, proj, a_tail, wg_p, bg, states, d_o)


def _gla_post_fwd(o, r, gn, name):
    dvt = o.shape[1]
    dv = dvt // GLA_HEADS

    def fn(o, r, gn):
        outs = []
        for h in range(GLA_HEADS):
            sl = slice(h * dv, (h + 1) * dv)
            ohat, _ = _norm_stats(o[:, sl])
            outs.append((ohat * gn[:, sl]) * _silu(r[:, sl]))
        return (jnp.concatenate(outs, axis=1),)

    return _rowwise(fn, [("row", o), r, ("full", gn)], [("row", dvt, BF16)], name=name)[0]


def _gla_post_bwd(o, r, gn, dog, name):
    dvt = o.shape[1]
    dv = dvt // GLA_HEADS

    def fn(o, r, gn, dog):
        d_o, d_r, d_g = [], [], []
        for h in range(GLA_HEADS):
            sl = slice(h * dv, (h + 1) * dv)
            ohat, rstd = _norm_stats(o[:, sl])
            g, rr, dd = gn[:, sl], r[:, sl], dog[:, sl]
            d_r.append(dd * (ohat * g) * _dsilu(rr))
            don = dd * _silu(rr)
            d_g.append(_colsum(don * ohat))
            d_o.append(_norm_bwd(don * g, ohat, rstd))
        return jnp.concatenate(d_o, axis=1), jnp.concatenate(d_r, axis=1), jnp.concatenate(d_g, axis=1)

    return _rowwise(fn, [("row", o), r, ("full", gn), ("row", dog)],
                    [("row", dvt, F32), ("row", dvt, BF16), ("acc", dvt, F32)], name=name)


def _fox_prep(q, k, v, qg, kg, d, hd, name):
    heads = d // hd
    scale = hd ** -0.5

    def fn(q, k, v, qg, kg):
        qs, ks = [], []
        for h in range(heads):
            sl = slice(h * hd, (h + 1) * hd)
            qs.append(_norm_stats(q[:, sl])[0] * qg * scale)
            ks.append(_norm_stats(k[:, sl])[0] * kg)
        return jnp.concatenate(qs, axis=1), jnp.concatenate(ks, axis=1), v

    return _rowwise(fn, [q, k, v, ("full", qg), ("full", kg)],
                    [("row", d, BF16)] * 3, name=name)


def _fox_prep_bwd(q, k, dqn, dkn, qg, kg, hd, name):
    d = dqn.shape[1]
    heads = d // hd
    scale = hd ** -0.5

    def fn(q, k, dqn, dkn, qg, kg):
        dq, dk, gq, gk = [], [], [], []
        for h in range(heads):
            sl = slice(h * hd, (h + 1) * hd)
            for x, dxn, g, s, dl, gl in ((q, dqn, qg, scale, dq, gq), (k, dkn, kg, 1.0, dk, gk)):
                xhat, rstd = _norm_stats(x[:, sl])
                dn = dxn[:, sl] * s
                gl.append(_colsum(dn * xhat))
                dl.append(_norm_bwd(dn * g, xhat, rstd))
        cat = lambda t: jnp.concatenate(t, axis=1)
        return cat(dq), cat(dk), cat(gq), cat(gk)

    return _rowwise(fn, [q, k, ("row", dqn), ("row", dkn), ("full", qg), ("full", kg)],
                    [("row", d, BF16), ("row", d, BF16), ("acc", d, F32), ("acc", d, F32)], name=name)


def _fox_cum(fl, bf_p, name, tb=256):
    s = fl.shape[0]
    tb = _tile(s, tb)

    def body(fl_ref, bf_ref, cum_ref, carry):
        @pl.when(pl.program_id(0) == 0)
        def _():
            carry[...] = jnp.zeros_like(carry)

        lf = _log_sigmoid(fl_ref[...] + bf_ref[...])
        cum_ref[...] = _tri_matmul(_tri(tb), lf) + carry[...]
        carry[...] += _colsum(lf)

    return _pcall(
        body, name=name, grid=(s // tb,),
        in_specs=[pl.BlockSpec((tb, LANE), lambda i: (i, 0)), pl.BlockSpec((1, LANE), lambda i: (0, 0))],
        out_specs=pl.BlockSpec((tb, LANE), lambda i: (i, 0)),
        out_shape=jax.ShapeDtypeStruct((s, LANE), F32),
        scratch_shapes=[pltpu.VMEM((1, LANE), F32)],
        compiler_params=_params(1),
    )(fl, bf_p)


def _fox_cum_bwd(dcum, fl, bf_p, name, tb=256):
    s = fl.shape[0]
    tb = _tile(s, tb)
    nb = s // tb

    def body(dc_ref, fl_ref, bf_ref, dfl_ref, dbf_ref, carry):
        @pl.when(pl.program_id(0) == 0)
        def _():
            carry[...] = jnp.zeros_like(carry)
            dbf_ref[...] = jnp.zeros_like(dbf_ref)

        dc = dc_ref[...]
        dlf = _tri_matmul(_tri(tb, upper=True), dc) + carry[...]
        carry[...] += _colsum(dc)
        dfl = dlf * _sigmoid(-(fl_ref[...] + bf_ref[...]))
        dfl_ref[...] = dfl
        dbf_ref[...] += _colsum(dfl)

    rev = lambda i: (nb - 1 - i, 0)
    return _pcall(
        body, name=name, grid=(nb,),
        in_specs=[pl.BlockSpec((tb, LANE), rev), pl.BlockSpec((tb, LANE), rev), pl.BlockSpec((1, LANE), lambda i: (0, 0))],
        out_specs=[pl.BlockSpec((tb, LANE), rev), pl.BlockSpec((1, LANE), lambda i: (0, 0))],
        out_shape=[jax.ShapeDtypeStruct((s, LANE), F32), jax.ShapeDtypeStruct((1, LANE), F32)],
        scratch_shapes=[pltpu.VMEM((1, LANE), F32)],
        compiler_params=_params(1),
    )(dcum, fl, bf_p)


def _fox_attn_fwd(qn, kn, vb, cum_col, cum_row, hd, t, name):
    s, d = qn.shape
    heads = d // hd
    nq = s // t

    def body(q_ref, k_ref, v_ref, cc_ref, cr_ref, o_ref, lse_ref):
        qi = pl.program_id(1)
        q = q_ref[...]
        cq = cc_ref[...]
        qpos = qi * t + lax.broadcasted_iota(jnp.int32, (t, 1), 0)

        def step(kj, carry, diagonal=False):
            m, l, acc = carry
            off = pl.multiple_of(kj * t, t)
            ks, vs = k_ref[pl.ds(off, t), :], v_ref[pl.ds(off, t), :]
            sc = _dot(q, ks, tb=True) + cq - cr_ref[kj]
            if diagonal:
                kpos = off + lax.broadcasted_iota(jnp.int32, (1, t), 1)
                sc = jnp.where(kpos <= qpos, sc, NEG)
            m_new = jnp.maximum(m, jnp.max(sc, axis=1, keepdims=True))
            alpha = jnp.exp(m - m_new)
            p = jnp.exp(sc - m_new)
            return m_new, alpha * l + jnp.sum(p, axis=1, keepdims=True), alpha * acc + _dot(p, vs)

        init = (jnp.full((t, 1), NEG, F32), jnp.zeros((t, 1), F32), jnp.zeros((t, hd), F32))
        m, l, acc = step(qi, lax.fori_loop(0, qi, step, init), diagonal=True)
        o_ref[...] = acc / l
        lse_ref[...] = m + jnp.log(l)

    return _pcall(
        body, name=name, grid=(heads, nq),
        in_specs=[pl.BlockSpec((t, hd), lambda h, i: (i, h)),
                  pl.BlockSpec((s, hd), lambda h, i: (0, h)),
                  pl.BlockSpec((s, hd), lambda h, i: (0, h)),
                  pl.BlockSpec((None, t, 1), lambda h, i: (h, i, 0)),
                  pl.BlockSpec((None, nq, 1, t), lambda h, i: (h, 0, 0, 0))],
        out_specs=[pl.BlockSpec((t, hd), lambda h, i: (i, h)), pl.BlockSpec((None, t, 1), lambda h, i: (h, i, 0))],
        out_shape=[jax.ShapeDtypeStruct((s, d), F32), jax.ShapeDtypeStruct((heads, s, 1), F32)],
        compiler_params=_params(2),
    )(qn, kn, vb, cum_col, cum_row)


def _fox_attn_bwd(qn, kn, vb, d_o, o, lse, cum_col, cum_row, hd, t, name):
    s, d = qn.shape
    heads = d // hd
    nq = s // t

    def body(q_ref, k_ref, v_ref, do_ref, o_ref, lse_ref, cc_ref, cr_ref,
             dq_ref, dk_ref, dv_ref, dcq_ref, dck_ref, delta):
        kj = pl.program_id(1)

        @pl.when(kj == 0)
        def _():
            dq_ref[...] = jnp.zeros_like(dq_ref)
            dcq_ref[...] = jnp.zeros_like(dcq_ref)
            delta[...] = jnp.sum(do_ref[...] * o_ref[...], axis=1, keepdims=True)

        ks, vs, cr = k_ref[...], v_ref[...], cr_ref[...]
        kpos = kj * t + lax.broadcasted_iota(jnp.int32, (1, t), 1)

        def step(qi, carry, diagonal=False):
            dk, dv, dck = carry
            rows = pl.ds(pl.multiple_of(qi * t, t), t)
            q, d_out = q_ref[rows, :], do_ref[rows, :]
            sc = _dot(q, ks, tb=True) + cc_ref[rows, :] - cr
            p = jnp.exp(sc - lse_ref[rows, :])
            if diagonal:
                qpos = qi * t + lax.broadcasted_iota(jnp.int32, (t, 1), 0)
                p = jnp.where(kpos <= qpos, p, 0.0)
            ds = p * (_dot(d_out, vs, tb=True) - delta[rows, :])
            dq_ref[rows, :] += _dot(ds, ks)
            dcq_ref[rows, :] += jnp.sum(ds, axis=1, keepdims=True)
            return dk + _dot(ds, q, ta=True), dv + _dot(p, d_out, ta=True), dck + _colsum(ds)

        init = (jnp.zeros((t, hd), F32), jnp.zeros((t, hd), F32), jnp.zeros((1, t), F32))
        dk, dv, dck = lax.fori_loop(kj + 1, nq, step, step(kj, init, diagonal=True))
        dk_ref[...] = dk.astype(dk_ref.dtype)
        dv_ref[...] = dv.astype(dv_ref.dtype)
        dck_ref[...] = dck

    head_rows = lambda h, j: (0, h)
    blk = lambda h, j: (j, h)
    return _pcall(
        body, name=name, grid=(heads, nq),
        in_specs=[pl.BlockSpec((s, hd), head_rows), pl.BlockSpec((t, hd), blk), pl.BlockSpec((t, hd), blk),
                  pl.BlockSpec((s, hd), head_rows), pl.BlockSpec((s, hd), head_rows),
                  pl.BlockSpec((None, s, 1), lambda h, j: (h, 0, 0)),
                  pl.BlockSpec((None, s, 1), lambda h, j: (h, 0, 0)),
                  pl.BlockSpec((None, None, 1, t), lambda h, j: (h, j, 0, 0))],
        out_specs=[pl.BlockSpec((s, hd), head_rows), pl.BlockSpec((t, hd), blk), pl.BlockSpec((t, hd), blk),
                   pl.BlockSpec((None, s, 1), lambda h, j: (h, 0, 0)),
                   pl.BlockSpec((None, None, 1, t), lambda h, j: (h, j, 0, 0))],
        out_shape=[jax.ShapeDtypeStruct((s, d), F32), jax.ShapeDtypeStruct((s, d), BF16),
                   jax.ShapeDtypeStruct((s, d), BF16), jax.ShapeDtypeStruct((heads, s, 1), F32),
                   jax.ShapeDtypeStruct((heads, nq, 1, t), F32)],
        scratch_shapes=[pltpu.VMEM((s, 1), F32)],
        compiler_params=_params(2),
    )(qn, kn, vb, d_o, o, lse, cum_col, cum_row)


def _fox_gate_fwd(o, og, name):
    def fn(o, og):
        return (o * _sigmoid(og),)

    return _rowwise(fn, [("row", o), og], [("row", o.shape[1], BF16)], name=name)[0]


def _fox_gate_bwd(o, og, dact, name):
    def fn(o, og, dact):
        sg = _sigmoid(og)
        return dact * sg, dact * o * sg * (1.0 - sg)

    d = o.shape[1]
    return _rowwise(fn, [("row", o), og, ("row", dact)], [("row", d, F32), ("row", d, BF16)], name=name)


def _shift_down(x, n):
    rows = lax.broadcasted_iota(jnp.int32, x.shape, 0)
    return jnp.where(rows >= n, pltpu.roll(x, n, 0), 0.0)


def _shift_up(x, n):
    rows = lax.broadcasted_iota(jnp.int32, x.shape, 0)
    return jnp.where(rows < x.shape[0] - n, pltpu.roll(x, x.shape[0] - n, 0), 0.0)


def _conv(u, w_ref, b):
    return w_ref[0:1, :] * _shift_down(u, 2) + w_ref[1:2, :] * _shift_down(u, 1) + w_ref[2:3, :] * u + b


def _conv_act_fwd(u, cw, cb, name, tc=256):
    s, two_f = u.shape
    dff = two_f // 2
    tc = _tile(dff, tc)
    nb = dff // tc

    def body(ug_ref, uv_ref, wg_ref, wv_ref, bg_ref, bv_ref, a_ref):
        gate = _conv(ug_ref[...], wg_ref, bg_ref[...])
        val = _conv(uv_ref[...], wv_ref, bv_ref[...])
        a_ref[...] = (_silu(gate) * val).astype(a_ref.dtype)

    lo, hi = (lambda j: (0, j)), (lambda j: (0, j + nb))
    return _pcall(
        body, name=name, grid=(nb,),
        in_specs=[pl.BlockSpec((s, tc), lo), pl.BlockSpec((s, tc), hi), pl.BlockSpec((3, tc), lo),
                  pl.BlockSpec((3, tc), hi), pl.BlockSpec((1, tc), lo), pl.BlockSpec((1, tc), hi)],
        out_specs=pl.BlockSpec((s, tc), lo),
        out_shape=jax.ShapeDtypeStruct((s, dff), BF16),
        compiler_params=_params(1),
    )(u, u, cw, cw, cb, cb)


def _conv_act_bwd(u, cw, cb, da, name, tc=128):
    s, two_f = u.shape
    dff = two_f // 2
    tc = _tile(dff, tc)
    nb = dff // tc

    def body(ug_ref, uv_ref, wg_ref, wv_ref, bg_ref, bv_ref, da_ref, du_ref, dw_ref, db_ref):
        ug, uv, da = ug_ref[...], uv_ref[...], da_ref[...]
        gate = _conv(ug, wg_ref, bg_ref[...])
        val = _conv(uv, wv_ref, bv_ref[...])
        sg = _sigmoid(gate)
        d_val = da * (gate * sg)
        d_gate = da * val * (sg * (1.0 + gate * (1.0 - sg)))
        for half, (dc, uu, w_ref) in enumerate(((d_gate, ug, wg_ref), (d_val, uv, wv_ref))):
            du = w_ref[0:1, :] * _shift_up(dc, 2) + w_ref[1:2, :] * _shift_up(dc, 1) + w_ref[2:3, :] * dc
            du_ref[half] = du.astype(du_ref.dtype)
            dw_ref[half, 0:1, :] = _colsum(dc * _shift_down(uu, 2))
            dw_ref[half, 1:2, :] = _colsum(dc * _shift_down(uu, 1))
            dw_ref[half, 2:3, :] = _colsum(dc * uu)
            db_ref[half] = _colsum(dc)

    lo, hi = (lambda j: (0, j)), (lambda j: (0, j + nb))
    both = lambda j: (0, 0, j)
    return _pcall(
        body, name=name, grid=(nb,),
        in_specs=[pl.BlockSpec((s, tc), lo), pl.BlockSpec((s, tc), hi), pl.BlockSpec((3, tc), lo),
                  pl.BlockSpec((3, tc), hi), pl.BlockSpec((1, tc), lo), pl.BlockSpec((1, tc), hi),
                  pl.BlockSpec((s, tc), lo)],
        out_specs=[pl.BlockSpec((2, s, tc), both), pl.BlockSpec((2, 3, tc), both), pl.BlockSpec((2, 1, tc), both)],
        out_shape=[jax.ShapeDtypeStruct((2, s, dff), BF16), jax.ShapeDtypeStruct((2, 3, dff), F32),
                   jax.ShapeDtypeStruct((2, 1, dff), F32)],
        compiler_params=_params(1),
    )(u, u, cw, cw, cb, cb, da)


def _adamw_math(w, g, m, v):
    m = ADAM_B1 * m + (1.0 - ADAM_B1) * g
    v = ADAM_B2 * v + (1.0 - ADAM_B2) * (g * g)
    m_hat = m / (1.0 - ADAM_B1 ** ADAM_STEP)
    v_hat = v / (1.0 - ADAM_B2 ** ADAM_STEP)
    delta = -ADAM_LR * (m_hat / (jnp.sqrt(v_hat) + ADAM_EPS) + ADAM_WD * w)
    return delta, m, v


def _update_tiles(r, c, tr):
    tc = c
    if r % 8:
        tr, tc = r, _tile(c, max(LANE, 512 * 1024 // r // LANE * LANE))
    elif r <= tr:
        tr = r
    while r % tr:
        tr -= 8
    return tr, tc


def _adamw(w, g, m, v, name, tr=128):
    layers, r, c = w.shape
    tr, tc = _update_tiles(r, c, tr)

    def body(w_ref, g_ref, m_ref, v_ref, go_ref, d_ref, mo_ref, vo_ref):
        grad = g_ref[...]
        delta, m_new, v_new = _adamw_math(w_ref[...], grad, m_ref[...], v_ref[...])
        go_ref[...], d_ref[...], mo_ref[...], vo_ref[...] = grad, delta, m_new, v_new

    spec = pl.BlockSpec((None, tr, tc), lambda l, i, j: (l, i, j))
    return _pcall(
        body, name=name, grid=(layers, r // tr, c // tc), in_specs=[spec] * 4, out_specs=[spec] * 4,
        out_shape=[jax.ShapeDtypeStruct((layers, r, c), F32)] * 4, compiler_params=_params(3),
    )(w, g, m, v)


def _adamw_pieces(w, lands, sums, chip, m, v, name, tr=128):
    layers, r, c = w.shape
    tr, tc = _update_tiles(r, c, tr)
    nr, nc = r // tr, c // tc

    def body(chip_ref, w_ref, *rest):
        land_refs, own_refs = rest[:layers], rest[layers:2 * layers]
        m_ref, v_ref, go_ref, d_ref, mo_ref, vo_ref = rest[2 * layers:]
        for layer in range(layers):
            @pl.when(pl.program_id(0) == layer)
            def _(land_ref=land_refs[layer], own_ref=own_refs[layer]):
                grad = jnp.zeros(w_ref.shape, F32)
                for q in range(4):
                    grad = grad + jnp.where(chip_ref[0] == q, own_ref[...], land_ref[q]).astype(F32)
                delta, m_new, v_new = _adamw_math(w_ref[...], grad, m_ref[...], v_ref[...])
                go_ref[...], d_ref[...], mo_ref[...], vo_ref[...] = grad, delta, m_new, v_new

    def walk(k, l, i, j):
        here = l == k
        return jnp.where(here, i, jnp.where(l < k, 0, nr - 1)), jnp.where(here, j, jnp.where(l < k, 0, nc - 1))

    spec = pl.BlockSpec((None, tr, tc), lambda l, i, j, chip_ref: (l, i, j))
    land_specs = [pl.BlockSpec((4, tr, tc), lambda l, i, j, chip_ref, k=k: (0,) + walk(k, l, i, j))
                  for k in range(layers)]
    own_specs = [pl.BlockSpec((None, tr, tc), lambda l, i, j, chip_ref, k=k: (chip_ref[0],) + walk(k, l, i, j))
                 for k in range(layers)]
    return _pcall(
        body, name=name,
        grid_spec=pltpu.PrefetchScalarGridSpec(
            num_scalar_prefetch=1, grid=(layers, nr, nc),
            in_specs=[spec] + land_specs + own_specs + [spec, spec], out_specs=[spec] * 4),
        out_shape=[jax.ShapeDtypeStruct((layers, r, c), F32)] * 4, compiler_params=_params(3),
    )(chip, w, *lands, *sums, m, v)


def _pair_sum(pieces, partner, core, name, tr=512):
    _, r, c = pieces.shape
    tc = c
    if r % 8:
        tr, tc = r, _tile(c, max(LANE, 1024 * 1024 // r // LANE * LANE))
    elif r <= tr:
        tr = r
    while r % tr:
        tr -= 8

    def body(core_ref, mine_ref, partner_ref, out_ref):
        out_ref[...] = (mine_ref[...].astype(F32) + partner_ref[...].astype(F32)).astype(out_ref.dtype)

    return _pcall(
        body, name=name,
        grid_spec=pltpu.PrefetchScalarGridSpec(
            num_scalar_prefetch=1, grid=(4, r // tr, c // tc),
            in_specs=[pl.BlockSpec((None, tr, tc), lambda q, i, j, core_ref: (2 * q + core_ref[0], i, j)),
                      pl.BlockSpec((None, tr, tc), lambda q, i, j, core_ref: (q, i, j))],
            out_specs=pl.BlockSpec((None, tr, tc), lambda q, i, j, core_ref: (q, i, j))),
        out_shape=jax.ShapeDtypeStruct((4, r, c), pieces.dtype), compiler_params=_params(3),
    )(core, pieces, partner)


def _sum8(x, name):
    p = x.shape[2]
    tp = _tile(p, 16 * 1024)

    def body(x_ref, o_ref):
        acc = x_ref[0]
        for i in range(1, N_DEV):
            acc = acc + x_ref[i]
        o_ref[...] = acc

    return _pcall(
        body, name=name, grid=(p // tp,), in_specs=[pl.BlockSpec((N_DEV, 1, tp), lambda i: (0, 0, i))],
        out_specs=pl.BlockSpec((1, tp), lambda i: (0, i)), out_shape=jax.ShapeDtypeStruct((1, p), x.dtype),
        compiler_params=_params(1),
    )(x)


def _exchange(arrays, name, scatter):
    n = len(arrays)
    hbm = pl.BlockSpec(memory_space=pl.ANY)

    def body(*refs):
        ins, outs, token = refs[:n], refs[n:2 * n], refs[2 * n]
        send_sems, recv_sems, local_sems = refs[2 * n + 1:]
        token[...] = jnp.zeros_like(token)
        x, y, c = lax.axis_index("x"), lax.axis_index("y"), lax.axis_index("c")
        me = 4 * x + 2 * y + c
        copies = []
        for a in range(n):
            src_mine = ins[a].at[me] if scatter else ins[a]
            local = pltpu.make_async_copy(src_mine, outs[a].at[me], local_sems.at[a])
            local.start()
            copies.append(local)
            for k in range(1, N_DEV):
                px = 1 - x if k & 4 else x
                py = 1 - y if k & 2 else y
                pc = 1 - c if k & 1 else c
                src = ins[a].at[4 * px + 2 * py + pc] if scatter else ins[a]
                cp = pltpu.make_async_remote_copy(
                    src_ref=src, dst_ref=outs[a].at[me],
                    send_sem=send_sems.at[a * (N_DEV - 1) + k - 1], recv_sem=recv_sems.at[a * (N_DEV - 1) + k - 1],
                    device_id=(px, py, pc), device_id_type=pl.DeviceIdType.MESH)
                cp.start()
                copies.append(cp)
        for cp in copies:
            cp.wait()

    out_shape = [jax.ShapeDtypeStruct(a.shape if scatter else (N_DEV,) + a.shape, a.dtype) for a in arrays]
    res = _pcall(
        body, name=name, in_specs=[hbm] * n, out_specs=[hbm] * n + [pl.BlockSpec(memory_space=pltpu.VMEM)],
        out_shape=out_shape + [jax.ShapeDtypeStruct((8, LANE), F32)],
        scratch_shapes=[pltpu.SemaphoreType.DMA((n * (N_DEV - 1),)), pltpu.SemaphoreType.DMA((n * (N_DEV - 1),)),
                        pltpu.SemaphoreType.DMA((n,))],
        compiler_params=pltpu.CompilerParams(has_side_effects=True),
    )(*arrays)
    return res[:n], res[n][0, 0]


_HBM = pl.BlockSpec(memory_space=pltpu.HBM)
_SEM = pl.BlockSpec(memory_space=pltpu.SEMAPHORE)
_DATAFLOW = pltpu.SideEffectType.DATAFLOW_SIDE_EFFECTING


def _peer(k, x, y, c):
    return (1 - x if k & 4 else x, 1 - y if k & 2 else y, 1 - c if k & 1 else c)


def _pair_plan(x, y, c):
    return [(2 * q + (1 - c), q, (x, y, 1 - c)) for q in range(4)]


def _chip_plan(x, y, c):
    out = []
    for k in _ICI_PEERS:
        px, py, pc = _peer(k, x, y, c)
        out.append((2 * px + py, 2 * x + y, (px, py, pc)))
    return out


def _all_plan(x, y, c):
    return [(0, 4 * x + 2 * y + c, _peer(k, x, y, c)) for k in range(1, N_DEV)]


def _split_start(arrays, plan, name, land_blocks=4):
    n = len(arrays)
    lands = [lax.empty((land_blocks,) + a.shape[1:], a.dtype) for a in arrays]
    n_copies = len(plan(0, 0, 0))

    def body(*refs):
        srcs, dsts = refs[:n], refs[n:2 * n]
        send_sems, recv_sems, token = refs[4 * n:5 * n], refs[5 * n:6 * n], refs[6 * n]
        copies = plan(lax.axis_index("x"), lax.axis_index("y"), lax.axis_index("c"))
        for a in range(n):
            for j, (src_block, dst_block, peer) in enumerate(copies):
                pltpu.make_async_remote_copy(
                    src_ref=srcs[a].at[src_block], dst_ref=dsts[a].at[dst_block],
                    send_sem=send_sems[a].at[j], recv_sem=recv_sems[a].at[j],
                    device_id=peer, device_id_type=pl.DeviceIdType.MESH).start()
        token[...] = jnp.zeros_like(token)

    sems = [pltpu.SemaphoreType.DMA((n_copies,))] * (2 * n)
    res = _pcall(
        body, name=name,
        in_specs=[_HBM] * (2 * n),
        out_specs=[_HBM] * (2 * n) + [_SEM] * (2 * n) + [pl.BlockSpec(memory_space=pltpu.VMEM)],
        out_shape=[pltpu.HBM(a.shape, a.dtype) for a in arrays] + [pltpu.HBM(l.shape, l.dtype) for l in lands]
        + sems + [jax.ShapeDtypeStruct((8, LANE), F32)],
        input_output_aliases={i: i for i in range(2 * n)},
        compiler_params=pltpu.CompilerParams(has_side_effects=_DATAFLOW),
    )(*[pltpu.with_memory_space_constraint(a, pltpu.HBM) for a in arrays],
      *[pltpu.with_memory_space_constraint(l, pltpu.HBM) for l in lands])
    handles = [(res[a], res[n + a], res[2 * n + a], res[3 * n + a]) for a in range(n)]
    return handles, res[4 * n][0, 0]


def _split_wait(handles, plan, after, name):
    n = len(handles)
    after = list(after) if isinstance(after, (list, tuple)) else [after]

    def body(*refs):
        srcs, dsts = refs[:n], refs[n:2 * n]
        send_sems, recv_sems = refs[2 * n:3 * n], refs[3 * n:4 * n]
        copies = plan(lax.axis_index("x"), lax.axis_index("y"), lax.axis_index("c"))
        for a in range(n):
            for j, (src_block, dst_block, peer) in enumerate(copies):
                cp = pltpu.make_async_remote_copy(
                    src_ref=srcs[a].at[src_block], dst_ref=dsts[a].at[dst_block],
                    send_sem=send_sems[a].at[j], recv_sem=recv_sems[a].at[j],
                    device_id=peer, device_id_type=pl.DeviceIdType.MESH)
                cp.wait_send()
                cp.wait_recv()

    srcs, lands = [h[0] for h in handles], [h[1] for h in handles]
    res = _pcall(
        body, name=name,
        in_specs=[_HBM] * (2 * n) + [_SEM] * (2 * n) + [pl.BlockSpec(memory_space=pl.ANY)] * len(after),
        out_specs=[_HBM] * (2 * n),
        out_shape=[pltpu.HBM(t.shape, t.dtype) for t in srcs + lands],
        input_output_aliases={i: i for i in range(2 * n)},
        compiler_params=pltpu.CompilerParams(has_side_effects=_DATAFLOW),
    )(*srcs, *lands, *[h[2] for h in handles], *[h[3] for h in handles], *after)
    return res[:n], res[n:]


_ICI_PEERS = (2, 4, 6)


def _gather2_start(shards, name):
    n = len(shards)
    lands = [lax.empty((N_DEV,) + a.shape, a.dtype) for a in shards]

    def body(*refs):
        srcs, dsts = refs[:n], refs[n:2 * n]
        send_sems, d2d_sems, ici_sems = refs[4 * n:5 * n], refs[5 * n:6 * n], refs[6 * n:7 * n]
        token = refs[7 * n]
        x, y, c = lax.axis_index("x"), lax.axis_index("y"), lax.axis_index("c")
        me = 4 * x + 2 * y + c
        for a in range(n):
            for j, k in enumerate((1,) + _ICI_PEERS):
                recv = d2d_sems[a].at[0] if j == 0 else ici_sems[a].at[j - 1]
                pltpu.make_async_remote_copy(
                    src_ref=srcs[a], dst_ref=dsts[a].at[me], send_sem=send_sems[a].at[j], recv_sem=recv,
                    device_id=_peer(k, x, y, c), device_id_type=pl.DeviceIdType.MESH).start()
        token[...] = jnp.zeros_like(token)

    dma = pltpu.SemaphoreType.DMA
    res = _pcall(
        body, name=name,
        in_specs=[_HBM] * (2 * n),
        out_specs=[_HBM] * (2 * n) + [_SEM] * (3 * n) + [pl.BlockSpec(memory_space=pltpu.VMEM)],
        out_shape=[pltpu.HBM(a.shape, a.dtype) for a in shards] + [pltpu.HBM(l.shape, l.dtype) for l in lands]
        + [dma((4,))] * n + [dma((1,))] * n + [dma((3,))] * n + [jax.ShapeDtypeStruct((8, LANE), F32)],
        input_output_aliases={i: i for i in range(2 * n)},
        compiler_params=pltpu.CompilerParams(has_side_effects=_DATAFLOW),
    )(*[pltpu.with_memory_space_constraint(a, pltpu.HBM) for a in shards],
      *[pltpu.with_memory_space_constraint(l, pltpu.HBM) for l in lands])
    handles = [tuple(res[i * n + a] for i in range(5)) for a in range(n)]
    return handles, res[5 * n][0, 0]


def _gather2_forward(handle, after, name):
    src, land, send_sems, d2d_sem, ici_sems = handle

    def body(land_ref, ici_ref, d2d_ref, after_ref, land_out, fwd_send, fwd_recv, token):
        x, y, c = lax.axis_index("x"), lax.axis_index("y"), lax.axis_index("c")
        sibling = (x, y, 1 - c)
        arrived = [(_peer(k, x, y, c), ici_ref.at[j]) for j, k in enumerate(_ICI_PEERS)] + [(sibling, d2d_ref.at[0])]
        for j, ((px, py, pc), recv) in enumerate(arrived):
            block = land_ref.at[4 * px + 2 * py + pc]
            pltpu.make_async_remote_copy(
                src_ref=block, dst_ref=block, send_sem=fwd_send.at[j], recv_sem=recv,
                device_id=(px, py, pc), device_id_type=pl.DeviceIdType.MESH).wait_recv()
            pltpu.make_async_remote_copy(
                src_ref=block, dst_ref=block, send_sem=fwd_send.at[j], recv_sem=fwd_recv.at[j],
                device_id=sibling, device_id_type=pl.DeviceIdType.MESH).start()
        token[...] = jnp.zeros_like(token)

    dma = pltpu.SemaphoreType.DMA
    land, fwd_send, fwd_recv, token = _pcall(
        body, name=name,
        in_specs=[_HBM, _SEM, _SEM, pl.BlockSpec(memory_space=pl.ANY)],
        out_specs=[_HBM, _SEM, _SEM, pl.BlockSpec(memory_space=pltpu.VMEM)],
        out_shape=[pltpu.HBM(land.shape, land.dtype), dma((4,)), dma((4,)), jax.ShapeDtypeStruct((8, LANE), F32)],
        input_output_aliases={0: 0},
        compiler_params=pltpu.CompilerParams(has_side_effects=_DATAFLOW),
    )(land, ici_sems, d2d_sem, after)
    return (src, land, send_sems, fwd_send, fwd_recv), token[0, 0]


def _gather2_wait(handle, after, name):
    src, land, send_sems, fwd_send, fwd_recv = handle

    def body(src_ref, land_ref, send_ref, fsend_ref, frecv_ref, after_ref, src_out, land_out):
        x, y, c = lax.axis_index("x"), lax.axis_index("y"), lax.axis_index("c")
        block = land_ref.at[4 * x + 2 * y + c]

        def copy(send, recv):
            return pltpu.make_async_remote_copy(src_ref=src_ref, dst_ref=block, send_sem=send, recv_sem=recv,
                                                device_id=(x, y, 1 - c), device_id_type=pl.DeviceIdType.MESH)

        for j in range(4):
            copy(send_ref.at[j], frecv_ref.at[j]).wait_send()
        for j in range(4):
            copy(fsend_ref.at[j], frecv_ref.at[j]).wait_send()
            copy(fsend_ref.at[j], frecv_ref.at[j]).wait_recv()

    res = _pcall(
        body, name=name,
        in_specs=[_HBM, _HBM, _SEM, _SEM, _SEM, pl.BlockSpec(memory_space=pl.ANY)],
        out_specs=[_HBM, _HBM],
        out_shape=[pltpu.HBM(src.shape, src.dtype), pltpu.HBM(land.shape, land.dtype)],
        input_output_aliases={0: 0, 1: 1},
        compiler_params=pltpu.CompilerParams(has_side_effects=_DATAFLOW),
    )(src, land, send_sems, fwd_send, fwd_recv, after)
    return res[0], res[1]


def _pad_cols(x, width=LANE):
    return jnp.pad(x, ((0, 0), (0, width - x.shape[1])))


def _cols_full(g):
    return jnp.transpose(g, (1, 0, 2)).reshape(g.shape[1], -1)


def _ffn_fwd(x1, p, i, tag):
    h2 = _adaln_fwd(x1, p["norm_ffn"][i], p["sc_f"][i], p["sh_f"][i], f"ffn_norm_{tag}")
    u = _matmul(h2, p["fetch"](f"up{i}", h2), name=f"ffn_up_{tag}", tn=1408, b_shards=True)
    a = _conv_act_fwd(u, p["conv_w"][i], p["conv_b"][i], f"ffn_act_{tag}")
    g_f = p["g_f"][i]
    x2, f = _matmul(a, p["fetch"](f"down{i}", a), name=f"ffn_down_{tag}", tk=1408, out_dtypes=(F32, F32),
                    epilogue=lambda acc, x1, g: (x1 + (1.0 + g) * acc, acc), extras=(("mn", x1), ("n", g_f)))
    return x2, dict(h2=h2, u=u, a=a, f=f)


def _ffn_bwd(dx2, x1, saved, p, i, tag):
    d = x1.shape[1]
    w_up, w_down = p["fetch"](f"up{i}", None), p["fetch"](f"down{i}", None)
    df, dg_f = _residual_bwd(dx2, saved["f"], p["g_f"][i], f"ffn_res_bwd_{tag}")
    da = _matmul(df, w_down, tb=True, name=f"ffn_down_dx_{tag}", tn=1408)
    dw_down = _matmul(saved["a"], df, ta=True, name=f"ffn_down_dw_{tag}", tm=1408, out_dtypes=(BF16,))
    du, dcw, dcb = _conv_act_bwd(saved["u"], p["conv_w"][i], p["conv_b"][i], da, f"ffn_act_bwd_{tag}")
    dcw, dcb = (jnp.concatenate([t[0], t[1]], axis=1) for t in (dcw, dcb))
    tok = p["flush"](du)
    dh2 = _matmul(du, w_up, tb=True, name=f"ffn_up_dx_{tag}", tk=1408, a_halves=True, b_shards=True)
    dw_up = _matmul(saved["h2"], du, ta=True, name=f"ffn_up_dw_{tag}", tn=1408, out_dtypes=(BF16,), b_halves=True,
                    out_shards=True)
    tok = tok + p["send"](f"ffn{i}", [dw_up, dw_down.reshape(N_DEV, -1, d)])
    dx1, dsh, dsc, dgain = _adaln_bwd(x1, dh2, dx2, p["norm_ffn"][i] + tok, p["sc_f"][i], f"ffn_norm_bwd_{tag}")
    grads = dict(conv_w=dcw, conv_b=dcb, norm_ffn=dgain, sh_f=dsh, sc_f=dsc, g_f=dg_f)
    return dx1, grads


def _gla_layer_fwd(x, p, i):
    h1 = _adaln_fwd(x, p["norm_mix"][i], p["sc_m"][i], p["sh_m"][i], "gla_norm")
    w_t, w_tail_t, main = p["fetch"]("gla_in", h1)
    proj = _matmul(h1, w_t, tb=True, b_rows=main, name="gla_in")
    a_tail = _matmul(h1, w_tail_t, tb=True, name="gla_in_tail")
    dk_total = p["gla_wg_p"].shape[1]
    o, states = _gla_fwd(proj, a_tail, p["gla_wg_p"], p["gla_b_gate"], "gla_chunks")
    assert 2 * dk_total == o.shape[1]
    r = ("cols", proj, 2, o.shape[1])
    og = _gla_post_fwd(o, r, p["gla_norm"], "gla_post")
    x1, y = _matmul(og, p["fetch"]("gla_out", og), name="gla_out", out_dtypes=(F32, F32),
                    epilogue=lambda acc, x, g: (x + (1.0 + g) * acc, acc), extras=(("mn", x), ("n", p["g_m"][i])))
    return x1, dict(h1=h1, proj=proj, a_tail=a_tail, o=o, r=r, states=states, og=og, y=y)


def _gla_layer_bwd(dx1, x, sv, p, i):
    d = x.shape[1]
    (w_t, w_tail_t, main), w_out = p["fetch"]("gla_in", None), p["fetch"]("gla_out", None)
    dy, dg_m = _residual_bwd(dx1, sv["y"], p["g_m"][i], "gla_res_bwd")
    dog = _matmul(dy, w_out, tb=True, name="gla_out_dx")
    dw_out = _matmul(sv["og"], dy, ta=True, name="gla_out_dw", out_dtypes=(BF16,))
    tok = p["flush"](dog) + p["send"]("gla_out", [dw_out.reshape(N_DEV, -1, d)])
    d_o, d_r, dgn = _gla_post_bwd(sv["o"], sv["r"], p["gla_norm"] + tok, dog, "gla_post_bwd")
    dq, dk, dv, dga = _gla_bwd(sv["proj"], sv["a_tail"], p["gla_wg_p"], p["gla_b_gate"], sv["states"], d_o,
                               "gla_chunks_bwd")
    tok = p["flush"](dga)
    da_tail = _matmul(dga, p["gla_wg_p"], tb=True, name="gla_gate_dx", out_dtypes=(BF16,))
    dwg = _matmul(sv["a_tail"], dga, ta=True, name="gla_gate_dw")
    dbg = _rowwise(lambda t: (_colsum(t),), [("row", dga)], [("acc", dga.shape[1], F32)], name="gla_gate_db")[0]
    dproj = [dq, dk, dv, d_r]
    dh_tail = _matmul(da_tail, w_tail_t, name="gla_in_tail_dx")
    dh1 = _matmul(dproj, w_t, b_rows=main, name="gla_in_dx", tk=1024,
                  epilogue=lambda acc, t: (acc + t,), extras=(("mn", dh_tail),))
    rank = p["gla_rank"]
    dw_main = _matmul(dproj, sv["h1"], ta=True, name="gla_in_dw", tm=512, out_dtypes=(BF16,), out_rows=main + rank)
    dx, dsh, dsc, dgain = _adaln_bwd(x, dh1, dx1, p["norm_mix"][i] + tok, p["sc_m"][i], "gla_norm_bwd")
    grads = dict(gla_w_gate=dwg[:rank], gla_b_gate=dbg, gla_norm=dgn, norm_mix=dgain, sh_m=dsh, sc_m=dsc, g_m=dg_m,
                 gla_w_in_unsent=(dw_main, da_tail, sv["h1"]))
    return dx, grads


def _fox_layer_fwd(x, p, i):
    d = x.shape[1]
    hd = p["fox_q_norm"].shape[1]
    heads = d // hd
    s = x.shape[0]
    t = _tile(s, 512)
    h1 = _adaln_fwd(x, p["norm_mix"][i], p["sc_m"][i], p["sh_m"][i], "fox_norm")
    w_t, w_tail_t, main = p["fetch"]("fox_in", h1)
    proj = _matmul(h1, w_t, tb=True, b_rows=main, name="fox_in")
    fl = _matmul(h1, w_tail_t, tb=True, name="fox_in_tail")
    q, k, v, og = (("cols", proj, j, d) for j in range(4))
    qn, kn, vb = _fox_prep(q, k, v, p["fox_q_norm"], p["fox_k_norm"], d, hd, "fox_prep")
    cum = _fox_cum(fl, p["fox_bf_p"], "fox_cum")
    cum_t = jnp.transpose(cum[:, :heads])
    cum_col, cum_row = cum_t[:, :, None], cum_t.reshape(heads, s // t, 1, t)
    o, lse = _fox_attn_fwd(qn, kn, vb, cum_col, cum_row, hd, t, "fox_attn")
    act = _fox_gate_fwd(o, og, "fox_gate")
    x1, y = _matmul(act, p["fetch"]("fox_out", act), name="fox_out", out_dtypes=(F32, F32),
                    epilogue=lambda acc, x, g: (x + (1.0 + g) * acc, acc), extras=(("mn", x), ("n", p["g_m"][i])))
    return x1, dict(h1=h1, q=q, k=k, og=og, fl=fl, qn=qn, kn=kn, vb=vb, cum_col=cum_col, cum_row=cum_row,
                    o=o, lse=lse, act=act, y=y, t=t, hd=hd)


def _fox_layer_bwd(dx1, x, sv, p, i):
    d = x.shape[1]
    hd, t = sv["hd"], sv["t"]
    heads = d // hd
    s = x.shape[0]
    (w_t, w_tail_t, main), w_out = p["fetch"]("fox_in", None), p["fetch"]("fox_out", None)
    dy, dg_m = _residual_bwd(dx1, sv["y"], p["g_m"][i], "fox_res_bwd")
    dact = _matmul(dy, w_out, tb=True, name="fox_out_dx")
    dw_out = _matmul(sv["act"], dy, ta=True, name="fox_out_dw", out_dtypes=(BF16,))
    d_o, d_og = _fox_gate_bwd(sv["o"], sv["og"], dact, "fox_gate_bwd")
    tok_flush = p["flush"](d_og)
    dqn, dkn, dvb, dcq, dck = _fox_attn_bwd(sv["qn"], sv["kn"], sv["vb"], d_o, sv["o"], sv["lse"], sv["cum_col"],
                                            sv["cum_row"], hd, t, "fox_attn_bwd")
    dq, dk, gq, gk = _fox_prep_bwd(sv["q"], sv["k"], dqn, dkn, p["fox_q_norm"], p["fox_k_norm"], hd, "fox_prep_bwd")
    dcum = _pad_cols(jnp.transpose(dcq[:, :, 0] - dck.reshape(heads, s)))
    dfl, dbf = _fox_cum_bwd(dcum, sv["fl"], p["fox_bf_p"], "fox_cum_bwd")
    dfl_b = dfl.astype(BF16)
    dproj = [dq, dk, dvb, d_og]
    dh_tail = _matmul(dfl_b, w_tail_t, name="fox_in_tail_dx")
    dh1 = _matmul(dproj, w_t, b_rows=main, name="fox_in_dx", tk=1024,
                  epilogue=lambda acc, tl: (acc + tl,), extras=(("mn", dh_tail),))
    dw_main = _matmul(dproj, sv["h1"], ta=True, name="fox_in_dw", tm=512, out_dtypes=(BF16,), out_rows=main + heads)
    dw_in = _tail_rows(dfl_b, sv["h1"], dw_main, heads, "fox_in_tail_dw").reshape(N_DEV, -1, d)
    tok = tok_flush + p["send"]("fox", [dw_in, dw_out.reshape(N_DEV, -1, d)])
    dx, dsh, dsc, dgain = _adaln_bwd(x, dh1, dx1, p["norm_mix"][i] + tok, p["sc_m"][i], "fox_norm_bwd")
    grads = dict(fox_b_f=dbf[:, :heads], fox_q_norm=gq.reshape(heads, hd).sum(0, keepdims=True),
                 fox_k_norm=gk.reshape(heads, hd).sum(0, keepdims=True), norm_mix=dgain, sh_m=dsh, sc_m=dsc, g_m=dg_m)
    return dx, grads


SMALL = ("b_mod", "norm_mix", "norm_ffn", "gla_b_gate", "gla_norm", "fox_b_f", "fox_q_norm", "fox_k_norm",
         "ffn_conv_b", "norm_final")
SMALL_SHARDED = ("gla_w_gate", "ffn_conv_w")
BIG = ("gla_w_in", "gla_w_out", "fox_w_in", "fox_w_out", "ffn_w_up", "ffn_w_down")
WEIGHTS = ("w_mod", "b_mod", "norm_mix", "norm_ffn", "gla_w_in", "gla_w_gate", "gla_b_gate", "gla_norm", "gla_w_out",
           "fox_w_in", "fox_b_f", "fox_q_norm", "fox_k_norm", "fox_w_out", "ffn_w_up", "ffn_conv_w", "ffn_conv_b",
           "ffn_w_down", "norm_final")


def _pack(parts):
    flat = jnp.concatenate([p.reshape(-1) for p in parts])
    pad = (-flat.shape[0]) % 1024
    return jnp.pad(flat, (0, pad)).reshape(1, -1)


def _unpack(flat, shapes):
    out, off = [], 0
    for shp in shapes:
        n = 1
        for s in shp:
            n *= s
        out.append(flat[0, off:off + n].reshape(shp))
        off += n
    return out


def kernel(x, c, w_mod, b_mod, norm_mix, norm_ffn, gla_w_in, gla_w_gate, gla_b_gate, gla_norm, gla_w_out, fox_w_in, fox_b_f, fox_q_norm, fox_k_norm, fox_w_out, ffn_w_up, ffn_conv_w, ffn_conv_b, ffn_w_down, norm_final, loss_target, m_w_mod, m_b_mod, m_norm_mix, m_norm_ffn, m_gla_w_in, m_gla_w_gate, m_gla_b_gate, m_gla_norm, m_gla_w_out, m_fox_w_in, m_fox_b_f, m_fox_q_norm, m_fox_k_norm, m_fox_w_out, m_ffn_w_up, m_ffn_conv_w, m_ffn_conv_b, m_ffn_w_down, m_norm_final, v_w_mod, v_b_mod, v_norm_mix, v_norm_ffn, v_gla_w_in, v_gla_w_gate, v_gla_b_gate, v_gla_norm, v_gla_w_out, v_fox_w_in, v_fox_b_f, v_fox_q_norm, v_fox_k_norm, v_fox_w_out, v_ffn_w_up, v_ffn_conv_w, v_ffn_conv_b, v_ffn_w_down, v_norm_final):
    w = dict(w_mod=w_mod, b_mod=b_mod, norm_mix=norm_mix, norm_ffn=norm_ffn, gla_w_in=gla_w_in, gla_w_gate=gla_w_gate,
             gla_b_gate=gla_b_gate, gla_norm=gla_norm, gla_w_out=gla_w_out, fox_w_in=fox_w_in, fox_b_f=fox_b_f,
             fox_q_norm=fox_q_norm, fox_k_norm=fox_k_norm, fox_w_out=fox_w_out, ffn_w_up=ffn_w_up,
             ffn_conv_w=ffn_conv_w, ffn_conv_b=ffn_conv_b, ffn_w_down=ffn_w_down, norm_final=norm_final)
    mom_m = dict(w_mod=m_w_mod, b_mod=m_b_mod, norm_mix=m_norm_mix, norm_ffn=m_norm_ffn, gla_w_in=m_gla_w_in,
                 gla_w_gate=m_gla_w_gate, gla_b_gate=m_gla_b_gate, gla_norm=m_gla_norm, gla_w_out=m_gla_w_out,
                 fox_w_in=m_fox_w_in, fox_b_f=m_fox_b_f, fox_q_norm=m_fox_q_norm, fox_k_norm=m_fox_k_norm,
                 fox_w_out=m_fox_w_out, ffn_w_up=m_ffn_w_up, ffn_conv_w=m_ffn_conv_w, ffn_conv_b=m_ffn_conv_b,
                 ffn_w_down=m_ffn_w_down, norm_final=m_norm_final)
    mom_v = dict(w_mod=v_w_mod, b_mod=v_b_mod, norm_mix=v_norm_mix, norm_ffn=v_norm_ffn, gla_w_in=v_gla_w_in,
                 gla_w_gate=v_gla_w_gate, gla_b_gate=v_gla_b_gate, gla_norm=v_gla_norm, gla_w_out=v_gla_w_out,
                 fox_w_in=v_fox_w_in, fox_b_f=v_fox_b_f, fox_q_norm=v_fox_q_norm, fox_k_norm=v_fox_k_norm,
                 fox_w_out=v_fox_w_out, ffn_w_up=v_ffn_w_up, ffn_conv_w=v_ffn_conv_w, ffn_conv_b=v_ffn_conv_b,
                 ffn_w_down=v_ffn_w_down, norm_final=v_norm_final)

    me = 4 * lax.axis_index("x") + 2 * lax.axis_index("y") + lax.axis_index("c")
    xs, target = x[0], loss_target[0]
    s, d = xs.shape
    depth = w_mod.shape[0]
    mod_cols = w_mod.shape[2]
    rank = gla_w_gate.shape[1]
    hd = fox_q_norm.shape[1]
    fox_heads = d // hd
    dk_total = gla_w_gate.shape[2] * N_DEV

    cond = c * (1.0 / (1.0 + jnp.exp(-c)))
    g, _ = _exchange([gla_w_gate[0], ffn_conv_w, cond], "gather_small", scatter=False)
    cond_all = g[2][:, 0, :]

    cond_pad = jnp.pad(cond_all, ((0, 16 - N_DEV), (0, 0)))
    mod_part = []
    for i in range(depth):
        b_cols = lax.dynamic_slice(b_mod[i:i + 1], (0, me * mod_cols), (1, mod_cols))
        mod_part.append(_matmul(cond_pad, w_mod, b_layer=i, name=f"mod_{i}", tn=768,
                                epilogue=lambda acc, b: (acc + b,), extras=(("n", b_cols),))[:N_DEV])
    (mod_all,), tok_mod = _exchange([jnp.stack(mod_part)], "gather_mod", scatter=False)
    mod = lax.dynamic_index_in_dim(mod_all, me, axis=2, keepdims=False)
    mod = jnp.transpose(mod, (1, 0, 2)).reshape(depth, 6, 1, d)

    big_names = ["gla_in", "gla_out", "up0", "down0", "fox_in", "fox_out", "up1", "down1"]
    first = [jnp.transpose(gla_w_in[0] + tok_mod).astype(BF16), gla_w_out[0].astype(BF16)]
    handles, tok_first = _gather2_start(first, "gather_weights_start_first")
    rest = [ffn_w_up[0] + tok_first, ffn_w_down[0], jnp.transpose(fox_w_in[0]), fox_w_out[0], ffn_w_up[1],
            ffn_w_down[1]]
    handles_rest, tok0 = _gather2_start([t.astype(BF16) for t in rest], "gather_weights_start_rest")
    handles = handles + handles_rest
    ready, forwarded = {}, {}

    def split_tail(full_t, tail):
        main = full_t.shape[0] - tail
        return full_t, jnp.pad(full_t[main:], ((0, LANE - tail), (0, 0))), main

    def forward(idx, after):
        key = big_names[idx]
        forwarded[key] = _gather2_forward(handles[idx], after, f"gather_{key}_forward")

    def fetch(key, after):
        if key not in ready:
            idx = big_names.index(key)
            if idx == 0:
                forward(0, after)
            handle, _ = forwarded[key]
            _, full = _gather2_wait(handle, after, f"gather_{key}_wait")
            if idx + 1 < len(big_names):
                forward(idx + 1, full)
            if key == "gla_in":
                ready[key] = split_tail(full.reshape(-1, d), rank)
            elif key == "fox_in":
                ready[key] = split_tail(full.reshape(-1, d), fox_heads)
            elif key.startswith("up"):
                ready[key] = full
            else:
                ready[key] = full.reshape(-1, d)
        return ready[key]

    pending, sent = [], {}
    core = lax.axis_index("c").astype(jnp.int32).reshape(1)
    chip = 2 * lax.axis_index("x") + lax.axis_index("y")

    def send(key, pieces):
        hs, tok = _split_start(pieces, _pair_plan, f"scatter_{key}_pair_start")
        pending.append((key, hs))
        return tok

    def flush(after):
        tok = 0.0
        while pending:
            key, hs = pending.pop(0)
            mine, partner = _split_wait(hs, _pair_plan, after, f"scatter_{key}_pair_wait")
            sums = [_pair_sum(pc, pt, core, f"scatter_{key}_pair_sum{a}")
                    for a, (pc, pt) in enumerate(zip(mine, partner))]
            sent[key], t = _split_start(sums, _chip_plan, f"scatter_{key}_chip_start")
            tok = tok + t
        return tok

    p = dict(
        fetch=fetch, send=send, flush=flush,
        gla_wg_p=jnp.pad(_cols_full(g[0]), ((0, LANE - rank), (0, 0))),
        conv_w=[jnp.transpose(g[1][:, i], (1, 0, 2)).reshape(ffn_conv_w.shape[1], -1) for i in range(depth)],
        conv_b=[ffn_conv_b[i:i + 1] for i in range(depth)],
        gla_b_gate=gla_b_gate, gla_norm=gla_norm, fox_q_norm=fox_q_norm, fox_k_norm=fox_k_norm,
        fox_bf_p=_pad_cols(fox_b_f), gla_rank=rank,
        norm_mix=[norm_mix[i:i + 1] + (tok0 if i == 0 else 0.0) for i in range(depth)],
        norm_ffn=[norm_ffn[i:i + 1] for i in range(depth)],
    )

    for j, nm in enumerate(("sh_m", "sc_m", "g_m", "sh_f", "sc_f", "g_f")):
        p[nm] = [mod[i, j] for i in range(depth)]

    acts, saved = [xs], []
    for i in range(depth):
        layer_fwd = _gla_layer_fwd if i % 2 == 0 else _fox_layer_fwd
        x1, sv_mix = layer_fwd(acts[-1], p, i)
        x2, sv_ffn = _ffn_fwd(x1, p, i, str(i))
        saved.append((acts[-1], x1, sv_mix, sv_ffn))
        acts.append(x2)
    dx, d_norm_final, loss_part = _final_loss(acts[-1], target, norm_final.reshape(1, d), "final_loss")

    lg = [None] * depth
    for i in reversed(range(depth)):
        x_in, x1, sv_mix, sv_ffn = saved[i]
        dx, g_ffn = _ffn_bwd(dx, x1, sv_ffn, p, i, str(i))
        layer_bwd = _gla_layer_bwd if i % 2 == 0 else _fox_layer_bwd
        dx, g_mix = layer_bwd(dx, x_in, sv_mix, p, i)
        lg[i] = {**g_ffn, **g_mix}
    grad_x = dx[None]

    gla_l = [i for i in range(depth) if i % 2 == 0]
    fox_l = [i for i in range(depth) if i % 2 == 1]
    small_parts = dict(
        norm_mix=jnp.concatenate([lg[i]["norm_mix"] for i in range(depth)]),
        norm_ffn=jnp.concatenate([lg[i]["norm_ffn"] for i in range(depth)]),
        gla_b_gate=jnp.concatenate([lg[i]["gla_b_gate"] for i in gla_l]),
        gla_norm=jnp.concatenate([lg[i]["gla_norm"] for i in gla_l]),
        fox_b_f=jnp.concatenate([lg[i]["fox_b_f"] for i in fox_l]),
        fox_q_norm=jnp.concatenate([lg[i]["fox_q_norm"] for i in fox_l]),
        fox_k_norm=jnp.concatenate([lg[i]["fox_k_norm"] for i in fox_l]),
        ffn_conv_b=jnp.concatenate([lg[i]["conv_b"] for i in range(depth)]),
        norm_final=d_norm_final,
        gla_w_gate=jnp.stack([lg[i]["gla_w_gate"] for i in gla_l]),
        ffn_conv_w=jnp.stack([lg[i]["conv_w"] for i in range(depth)]),
        loss=loss_part[:, :1],
    )
    order = ("norm_mix", "norm_ffn", "gla_b_gate", "gla_norm", "fox_b_f", "fox_q_norm", "fox_k_norm", "ffn_conv_b",
             "norm_final", "gla_w_gate", "ffn_conv_w", "loss")
    packed = _pack([small_parts[nm] for nm in order])
    dmod = jnp.stack([jnp.concatenate([lg[i][nm] for nm in ("sh_m", "sc_m", "g_m", "sh_f", "sc_f", "g_f")], axis=1)
                      for i in range(depth)])
    hs_small, tok_small = _split_start([packed[None], dmod[None]], _all_plan, "gather_small_grads_start",
                                       land_blocks=N_DEV)
    dw_main, da_tail, h1_gla = lg[0]["gla_w_in_unsent"]
    dw_in_t = _tail_rows(da_tail + tok_small.astype(BF16), h1_gla, dw_main, rank, "gla_in_tail_dw")
    send("gla_in", [dw_in_t.reshape(N_DEV, -1, d)])
    started = pending[-1][1][0][0]

    received = {}

    def arrive(key, after):
        sums, lands = _split_wait(sent[key], _chip_plan, after, f"scatter_{key}_chip_wait")
        received[key] = list(zip(lands, sums))

    for key in ("ffn1", "fox", "ffn0", "gla_out"):
        arrive(key, started)

    out_g, out_d, out_m, out_v = {}, {}, {}, {}

    chip_idx = chip.astype(jnp.int32).reshape(1)

    def update(nm, g_arr, transposed=False):
        swap = (lambda t: jnp.transpose(t, (0, 2, 1))) if transposed else (lambda t: t)
        if isinstance(g_arr, list):
            res = _adamw_pieces(swap(w[nm]), [t[0] for t in g_arr], [t[1] for t in g_arr], chip_idx,
                                swap(mom_m[nm]), swap(mom_v[nm]), f"adamw_{nm}")
        else:
            res = _adamw(w[nm], g_arr, mom_m[nm], mom_v[nm], f"adamw_{nm}")
        out_g[nm], out_d[nm], out_m[nm], out_v[nm] = (swap(t) for t in res)

    update("gla_w_out", [received["gla_out"][0]])
    update("fox_w_out", [received["fox"][1]])
    tok_flush = flush(out_g["fox_w_out"])
    update("ffn_w_up", [received[f"ffn{i}"][0] for i in range(depth)])
    update("fox_w_in", [received["fox"][0]], transposed=True)
    update("ffn_w_down", [received[f"ffn{i}"][1] for i in range(depth)])

    updated = ("gla_w_out", "fox_w_in", "fox_w_out", "ffn_w_up", "ffn_w_down")
    (packed_mine, dmod_mine), (packed_all, dmod_all) = _split_wait(
        hs_small, _all_plan, [out_d[nm] for nm in updated], "gather_small_grads_wait")
    packed_all = lax.dynamic_update_slice(packed_all, packed_mine + tok_flush, (me, 0, 0))
    dmod_all = lax.dynamic_update_slice(dmod_all, dmod_mine, (me, 0, 0, 0))
    summed = _unpack(_sum8(packed_all, "sum_small_grads"), [small_parts[nm].shape for nm in order])
    small_g = dict(zip(order, summed))
    loss = small_g["loss"][0, 0]
    dmod_all = dmod_all[:, :, 0, :]
    grads = {}
    cond_t = _pad_cols(jnp.transpose(cond_all)).astype(BF16)
    dmod_cols = lax.dynamic_slice(dmod_all, (0, 0, me * mod_cols), (N_DEV, depth, mod_cols))
    g_w_mod = lax.empty(w_mod.shape, F32)
    for i in range(depth):
        rhs = jnp.pad(dmod_cols[:, i], ((0, LANE - N_DEV), (0, 0)))
        g_w_mod = _matmul(cond_t, rhs, name=f"mod_dw_{i}", tn=768, into=(g_w_mod, i))
    grads["w_mod"] = g_w_mod
    small_g["b_mod"] = _sum8(dmod_all.reshape(N_DEV, 1, -1), "sum_b_mod").reshape(depth, -1)
    update("w_mod", grads["w_mod"])

    gate_cols = gla_w_gate.shape[2]
    conv_cols = ffn_conv_w.shape[2]
    local_small = dict(small_g)
    local_small["gla_w_gate"] = lax.dynamic_slice_in_dim(small_g["gla_w_gate"], me * gate_cols, gate_cols, axis=2)
    local_small["ffn_conv_w"] = lax.dynamic_slice_in_dim(small_g["ffn_conv_w"], me * conv_cols, conv_cols, axis=2)
    names = SMALL + SMALL_SHARDED
    shapes = [w[nm].shape for nm in names]
    res = _adamw(_pack([w[nm] for nm in names])[None], _pack([local_small[nm] for nm in names])[None],
                 _pack([mom_m[nm] for nm in names])[None], _pack([mom_v[nm] for nm in names])[None], "adamw_small")
    for tgt, flat in zip((out_g, out_d, out_m, out_v), res):
        for nm, arr in zip(names, _unpack(flat[0], shapes)):
            tgt[nm] = arr

    arrive("gla_in", [out_d[nm] for nm in updated + ("w_mod",)])
    update("gla_w_in", [received["gla_in"][0]], transposed=True)

    return (loss, grad_x, *[out_g[n] for n in WEIGHTS], *[out_d[n] for n in WEIGHTS],
            *[out_m[n] for n in WEIGHTS], *[out_v[n] for n in WEIGHTS])
```

```python
import jax
import jax.numpy as jnp
from jax import lax
from jax.experimental import pallas as pl
from jax.experimental.pallas import tpu as pltpu

F32, BF16 = jnp.float32, jnp.bfloat16
N_DEV = 8
GLA_HEADS = 4
GLA_TAU = 16.0
GLA_CHUNK = 64
NORM_EPS = 1e-6
ADAM_LR, ADAM_B1, ADAM_B2, ADAM_EPS, ADAM_WD, ADAM_STEP = 0.001, 0.9, 0.999, 1e-08, 0.01, 10
LANE = 128
VMEM_LIMIT = 56 * 1024 * 1024
NEG = -1e30


def _pcall(body, **kw):
    return pl.pallas_call(body, **kw)


def _params(n_axes):
    return pltpu.CompilerParams(dimension_semantics=("arbitrary",) * n_axes, vmem_limit_bytes=VMEM_LIMIT)


def _tile(dim, pref):
    if dim <= pref:
        return dim
    t = pref
    while dim % t:
        t -= LANE
    assert t > 0, (dim, pref)
    return t


def _dot(a, b, ta=False, tb=False):
    dims = (((0,) if ta else (1,), (1,) if tb else (0,)), ((), ()))
    return lax.dot_general(a.astype(BF16), b.astype(BF16), dims, preferred_element_type=F32)


def _split3(x):
    hi = x.astype(BF16)
    r1 = x - hi.astype(F32)
    mid = r1.astype(BF16)
    lo = (r1 - mid.astype(F32)).astype(BF16)
    return hi, mid, lo


def _tri_matmul(tri, x):
    hi, mid, lo = _split3(x)
    return _dot(tri, hi) + _dot(tri, mid) + _dot(tri, lo)


def _tri(n, upper=False):
    r = lax.broadcasted_iota(jnp.int32, (n, n), 0)
    c = lax.broadcasted_iota(jnp.int32, (n, n), 1)
    return jnp.where((r <= c) if upper else (r >= c), 1.0, 0.0).astype(BF16)


def _log_sigmoid(x):
    return jnp.minimum(x, 0.0) - jnp.log(1.0 + jnp.exp(-jnp.abs(x)))


def _sigmoid(x):
    return 1.0 / (1.0 + jnp.exp(-x))


def _silu(x):
    return x * _sigmoid(x)


def _dsilu(x):
    s = _sigmoid(x)
    return s * (1.0 + x * (1.0 - s))


def _matmul(a, b, *, name, ta=False, tb=False, out_dtypes=(F32,), tm=1024, tn=1024, tk=2048,
            epilogue=None, extras=(), a_halves=False, b_halves=False, b_shards=False, out_shards=False,
            b_rows=None, out_rows=None, b_layer=None, into=None):
    if a_halves:
        assert not ta
        m, k = a.shape[1], 2 * a.shape[2]
    else:
        m, k = (a.shape[1], a.shape[0]) if ta else a.shape
    if b_halves:
        assert not tb and b.shape[1] == k
        n = 2 * b.shape[2]
    elif b_shards:
        n = b.shape[1] if tb else N_DEV * b.shape[2]
        assert (N_DEV * b.shape[2] if tb else b.shape[1]) == k, (a.shape, b.shape, ta, tb)
    elif b_layer is not None:
        assert not tb and b.shape[1] == k
        n = b.shape[2]
    else:
        rows = b.shape[0] if b_rows is None else b_rows
        n = rows if tb else b.shape[1]
        assert (b.shape[1] if tb else rows) == k, (a.shape, b.shape, ta, tb)
    n_unit = n // N_DEV if (out_shards or (b_shards and not tb)) else (n // 2 if b_halves else n)
    k_unit = k // N_DEV if (b_shards and tb) else (k // 2 if a_halves else k)
    tm, tn, tk = _tile(m, tm), _tile(n_unit, tn), _tile(k_unit, tk)
    nk = k // tk
    if a_halves:
        a_spec = pl.BlockSpec((None, tm, tk), lambda i, j, kk: (kk // (nk // 2), i, kk % (nk // 2)))
    elif ta:
        a_spec = pl.BlockSpec((tk, tm), lambda i, j, kk: (kk, i))
    else:
        a_spec = pl.BlockSpec((tm, tk), lambda i, j, kk: (i, kk))
    n_per, k_per = n // tn // N_DEV, nk // N_DEV
    if b_halves:
        b_spec = pl.BlockSpec((None, tk, tn), lambda i, j, kk: (j // (n // tn // 2), kk, j % (n // tn // 2)))
    elif b_shards and tb:
        b_spec = pl.BlockSpec((None, tn, tk), lambda i, j, kk: (kk // k_per, j, kk % k_per))
    elif b_shards:
        b_spec = pl.BlockSpec((None, tk, tn), lambda i, j, kk: (j // n_per, kk, j % n_per))
    elif b_layer is not None:
        b_spec = pl.BlockSpec((None, tk, tn), lambda i, j, kk: (b_layer, kk, j))
    elif tb:
        b_spec = pl.BlockSpec((tn, tk), lambda i, j, kk: (j, kk))
    else:
        b_spec = pl.BlockSpec((tk, tn), lambda i, j, kk: (kk, j))
    ex_specs = []
    for kind, arr in extras:
        if kind == "mn":
            assert arr.shape == (m, n), (arr.shape, m, n)
            ex_specs.append(pl.BlockSpec((tm, tn), lambda i, j, kk: (i, j)))
        else:
            assert arr.shape == (1, n), (arr.shape, n)
            ex_specs.append(pl.BlockSpec((1, tn), lambda i, j, kk: (0, j)))
    n_ex, n_out = len(extras), len(out_dtypes)

    def body(a_ref, b_ref, *rest):
        ex, outs, acc = rest[:n_ex], rest[-1 - n_out:-1], rest[-1]
        kk = pl.program_id(2)

        @pl.when(kk == 0)
        def _():
            acc[...] = jnp.zeros_like(acc)

        acc[...] += _dot(a_ref[...], b_ref[...], ta, tb)

        @pl.when(kk == nk - 1)
        def _():
            if epilogue is None:
                vals = (acc[...],)
            else:
                vals = epilogue(acc[...], *[e[...] for e in ex])
            for o, v in zip(outs, vals):
                o[...] = v.astype(o.dtype)

    if out_shards:
        out_spec = pl.BlockSpec((None, tm, tn), lambda i, j, kk: (j // n_per, i, j % n_per))
        out_dims = (N_DEV, m, n // N_DEV)
    elif into is not None:
        out_spec = pl.BlockSpec((None, tm, tn), lambda i, j, kk: (into[1], i, j))
        out_dims = into[0].shape
    else:
        out_spec = pl.BlockSpec((tm, tn), lambda i, j, kk: (i, j))
        out_dims = (m if out_rows is None else out_rows, n)
    operands = [a, b, *[arr for _, arr in extras]]
    aliases = {}
    if into is not None:
        assert n_out == 1 and into[0].shape[1:] == (m, n) and into[0].dtype == out_dtypes[0]
        aliases = {len(operands): 0}
        operands.append(into[0])
    res = _pcall(
        body, name=name, grid=(m // tm, n // tn, nk),
        in_specs=[a_spec, b_spec] + ex_specs + [pl.BlockSpec(memory_space=pl.ANY)] * len(aliases),
        out_specs=[out_spec] * n_out,
        out_shape=[jax.ShapeDtypeStruct(out_dims, d) for d in out_dtypes],
        scratch_shapes=[pltpu.VMEM((tm, tn), F32)],
        input_output_aliases=aliases,
        compiler_params=_params(3),
    )(*operands)
    return res[0] if n_out == 1 else res


def _tail_rows(a, b, into, rows, name, tn=1024):
    k, n = b.shape
    m_total = into.shape[0]
    tn = _tile(n, tn)

    def body(a_ref, b_ref, into_ref, out_ref):
        out_ref[...] = _dot(a_ref[...], b_ref[...], ta=True)[:rows].astype(out_ref.dtype)

    return _pcall(
        body, name=name, grid=(n // tn,),
        in_specs=[pl.BlockSpec((k, a.shape[1]), lambda j: (0, 0)), pl.BlockSpec((k, tn), lambda j: (0, j)),
                  pl.BlockSpec(memory_space=pl.ANY)],
        out_specs=pl.BlockSpec((rows, tn), lambda j: (m_total // rows - 1, j)),
        out_shape=jax.ShapeDtypeStruct(into.shape, into.dtype),
        input_output_aliases={2: 0}, compiler_params=_params(1),
    )(a, b, into)


def _rowwise(fn, ins, outs, *, name, tr=128, into=None):
    rows = next(e[1].shape[0] for e in ins if e[0] != "full")
    tr = _tile(rows, tr)
    in_specs = []
    for entry in ins:
        kind, arr = entry[0], entry[1]
        assert kind == "full" or (arr.shape[0] == rows and arr.ndim == 2)
        if kind == "row":
            in_specs.append(pl.BlockSpec((tr, arr.shape[1]), lambda i: (i, 0)))
        elif kind == "cols":
            in_specs.append(pl.BlockSpec((tr, entry[3]), lambda i, cb=entry[2]: (i, cb)))
        else:
            in_specs.append(pl.BlockSpec(arr.shape, lambda i, nd=arr.ndim: (0,) * nd))
    out_specs, out_shape = [], []
    for entry in outs:
        kind, w, dt = entry[:3]
        if kind == "row":
            out_specs.append(pl.BlockSpec((tr, w), lambda i: (i, 0)))
            out_shape.append(jax.ShapeDtypeStruct((rows, w), dt))
        elif kind == "band":
            out_specs.append(pl.BlockSpec((tr, w), lambda i, cb=entry[3]: (i, cb)))
            out_shape.append(jax.ShapeDtypeStruct((rows, entry[4]), dt))
        else:
            out_specs.append(pl.BlockSpec((1, w), lambda i: (0, 0)))
            out_shape.append(jax.ShapeDtypeStruct((1, w), dt))
    n_in = len(ins)
    operands = [e[1] for e in ins]
    aliases = {}
    if into is not None:
        aliases = {len(operands): into[1]}
        in_specs.append(pl.BlockSpec(memory_space=pl.ANY))
        operands.append(into[0])

    def body(*refs):
        i = pl.program_id(0)
        vals = fn(*[r[...] for r in refs[:n_in]])
        for entry, o, v in zip(outs, refs[len(operands):], vals):
            if entry[0] == "acc":
                @pl.when(i == 0)
                def _(o=o):
                    o[...] = jnp.zeros_like(o)

                o[...] += v.astype(o.dtype)
            else:
                o[...] = v.astype(o.dtype)

    return _pcall(body, name=name, grid=(rows // tr,), in_specs=in_specs, out_specs=out_specs,
                  out_shape=out_shape, input_output_aliases=aliases, compiler_params=_params(1))(*operands)


def _colsum(x):
    return jnp.sum(x, axis=0, keepdims=True)


def _norm_stats(x):
    rstd = lax.rsqrt(jnp.mean(x * x, axis=-1, keepdims=True) + NORM_EPS)
    return x * rstd, rstd


def _norm_bwd(dxhat, xhat, rstd):
    return rstd * (dxhat - xhat * jnp.mean(dxhat * xhat, axis=-1, keepdims=True))


def _adaln_fwd(x, gain, sc, sh, name):
    def fn(x, gain, sc, sh):
        xhat, _ = _norm_stats(x)
        return ((xhat * gain) * (1.0 + sc) + sh,)

    return _rowwise(fn, [("row", x), ("full", gain), ("full", sc), ("full", sh)],
                    [("row", x.shape[1], BF16)], name=name)[0]


def _adaln_bwd(x, dh, dres, gain, sc, name):
    d = x.shape[1]

    def fn(x, dh, dres, gain, sc):
        xhat, rstd = _norm_stats(x)
        dxhat = dh * (gain * (1.0 + sc))
        dx = dres + _norm_bwd(dxhat, xhat, rstd)
        return dx, _colsum(dh), _colsum(dh * (xhat * gain)), _colsum(dh * xhat * (1.0 + sc))

    return _rowwise(fn, [("row", x), ("row", dh), ("row", dres), ("full", gain), ("full", sc)],
                    [("row", d, F32), ("acc", d, F32), ("acc", d, F32), ("acc", d, F32)], name=name)


def _residual_bwd(dx, y, g, name):
    d = dx.shape[1]

    def fn(dx, y, g):
        return dx * (1.0 + g), _colsum(dx * y)

    return _rowwise(fn, [("row", dx), ("row", y), ("full", g)], [("row", d, BF16), ("acc", d, F32)], name=name)


def _final_loss(x, target, gain, name):
    d = x.shape[1]

    def fn(x, t, gain):
        xhat, rstd = _norm_stats(x)
        err = xhat * gain - t
        dy = err * (1.0 / d)
        loss = 0.5 * jnp.sum(jnp.mean(err * err, axis=-1, keepdims=True), axis=0, keepdims=True)
        dx = _norm_bwd(dy * gain, xhat, rstd)
        return dx, _colsum(dy * xhat), jnp.broadcast_to(loss, (1, LANE))

    return _rowwise(fn, [("row", x), ("row", target), ("full", gain)],
                    [("row", d, F32), ("acc", d, F32), ("acc", LANE, F32)], name=name)


def _gla_gates(q, k, a, wg, bg, scale, c):
    ga = _dot(a, wg) + bg
    la = _log_sigmoid(ga) * (1.0 / GLA_TAU)
    b = _tri_matmul(_tri(c), la)
    bl = _colsum(la)
    eb, enb, eend = jnp.exp(b), jnp.exp(-b), jnp.exp(bl - b)
    q = q * scale
    return dict(ga=ga, eb=eb, enb=enb, eend=eend, dec=jnp.exp(bl), q_dec=q * eb, k_inv=k * enb, k_end=k * eend)


def _causal(c):
    return lax.broadcasted_iota(jnp.int32, (c, c), 0) >= lax.broadcasted_iota(jnp.int32, (c, c), 1)


def _gla_specs(heads, c, dk, dv, chunk):
    return [
        pl.BlockSpec((c, heads * dk), lambda n: (chunk(n), 0)),
        pl.BlockSpec((c, heads * dk), lambda n: (chunk(n), 1)),
        pl.BlockSpec((c, heads * dv), lambda n: (chunk(n), 1)),
        pl.BlockSpec((c, LANE), lambda n: (chunk(n), 0)),
        pl.BlockSpec((LANE, heads * dk), lambda n: (0, 0)),
        pl.BlockSpec((1, heads * dk), lambda n: (0, 0)),
    ]


def _gla_fwd(proj, a_tail, wg_p, bg, name):
    s = proj.shape[0]
    heads, c = GLA_HEADS, GLA_CHUNK
    dk = wg_p.shape[1] // heads
    dv = 2 * dk
    n_chunks = s // c
    scale = dk ** -0.5

    def body(q_ref, k_ref, v_ref, a_ref, wg_ref, bg_ref, o_ref, st_ref, state):
        @pl.when(pl.program_id(0) == 0)
        def _():
            state[...] = jnp.zeros_like(state)

        a = a_ref[...]
        for h in range(heads):
            sk, sv = slice(h * dk, (h + 1) * dk), slice(h * dv, (h + 1) * dv)
            g = _gla_gates(q_ref[:, sk], k_ref[:, sk], a, wg_ref[:, sk], bg_ref[:, sk], scale, c)
            v = v_ref[:, sv]
            st = state[h]
            attn = jnp.where(_causal(c), _dot(g["q_dec"], g["k_inv"], tb=True), 0.0)
            o_ref[:, sv] = _dot(attn, v) + _dot(g["q_dec"], st, tb=True)
            st_ref[h] = st.astype(st_ref.dtype)
            state[h] = g["dec"] * st + _dot(v, g["k_end"], ta=True)

    return _pcall(
        body, name=name, grid=(n_chunks,),
        in_specs=_gla_specs(heads, c, dk, dv, lambda n: n),
        out_specs=[pl.BlockSpec((c, heads * dv), lambda n: (n, 0)),
                   pl.BlockSpec((heads, None, dv, dk), lambda n: (0, n, 0, 0))],
        out_shape=[jax.ShapeDtypeStruct((s, heads * dv), F32),
                   jax.ShapeDtypeStruct((heads, n_chunks, dv, dk), BF16)],
        scratch_shapes=[pltpu.VMEM((heads, dv, dk), F32)],
        compiler_params=_params(1),
    )(proj, proj, proj, a_tail, wg_p, bg)


def _gla_bwd(proj, a_tail, wg_p, bg, states, d_o, dproj, name):
    s = proj.shape[0]
    heads, c = GLA_HEADS, GLA_CHUNK
    dk = wg_p.shape[1] // heads
    dv = 2 * dk
    n_chunks = s // c
    scale = dk ** -0.5
    k0, v0 = heads * dk, 2 * heads * dk

    def body(q_ref, k_ref, v_ref, a_ref, wg_ref, bg_ref, st_ref, do_ref, dproj_in, dqkv_ref, dga_ref, dstate):
        @pl.when(pl.program_id(0) == 0)
        def _():
            dstate[...] = jnp.zeros_like(dstate)

        a = a_ref[...]
        mask = _causal(c)
        for h in range(heads):
            sk, sv = slice(h * dk, (h + 1) * dk), slice(h * dv, (h + 1) * dv)
            out_k, out_v = slice(k0 + h * dk, k0 + (h + 1) * dk), slice(v0 + h * dv, v0 + (h + 1) * dv)
            g = _gla_gates(q_ref[:, sk], k_ref[:, sk], a, wg_ref[:, sk], bg_ref[:, sk], scale, c)
            v, st, dst, d_out = v_ref[:, sv], st_ref[h], dstate[h], do_ref[:, sv]
            q_dec, k_inv, k_end = g["q_dec"], g["k_inv"], g["k_end"]
            attn = jnp.where(mask, _dot(q_dec, k_inv, tb=True), 0.0)
            d_attn = jnp.where(mask, _dot(d_out, v, tb=True), 0.0)
            d_qdec = _dot(d_attn, k_inv) + _dot(d_out, st)
            d_kinv = _dot(d_attn, q_dec, ta=True)
            d_kend = _dot(v, dst)
            dqkv_ref[:, out_v] = (_dot(attn, d_out, ta=True) + _dot(k_end, dst, tb=True)).astype(dqkv_ref.dtype)
            d_dec = jnp.sum(dst * st.astype(F32), axis=0, keepdims=True)
            dstate[h] = g["dec"] * dst + _dot(d_out, q_dec, ta=True)

            dqkv_ref[:, sk] = (d_qdec * (scale * g["eb"])).astype(dqkv_ref.dtype)
            dqkv_ref[:, out_k] = (d_kinv * g["enb"] + d_kend * g["eend"]).astype(dqkv_ref.dtype)
            kk = d_kend * k_end
            db = d_qdec * q_dec - d_kinv * k_inv - kk
            dbl = jnp.sum(kk, axis=0, keepdims=True) + d_dec * g["dec"]
            last = lax.broadcasted_iota(jnp.int32, db.shape, 0) == c - 1
            db = db + jnp.where(last, dbl, 0.0)
            dla = _tri_matmul(_tri(c, upper=True), db)
            dga_ref[:, sk] = dla * (1.0 / GLA_TAU) * _sigmoid(-g["ga"])

    chunk = lambda n: n_chunks - 1 - n
    rev = lambda n: (chunk(n), 0)
    return _pcall(
        body, name=name, grid=(n_chunks,),
        in_specs=_gla_specs(heads, c, dk, dv, chunk) + [
            pl.BlockSpec((heads, None, dv, dk), lambda n: (0, chunk(n), 0, 0)),
            pl.BlockSpec((c, heads * dv), rev), pl.BlockSpec(memory_space=pl.ANY)],
        out_specs=[pl.BlockSpec((c, v0 + heads * dv), rev), pl.BlockSpec((c, heads * dk), rev)],
        out_shape=[jax.ShapeDtypeStruct(dproj.shape, dproj.dtype), jax.ShapeDtypeStruct((s, heads * dk), F32)],
        scratch_shapes=[pltpu.VMEM((heads, dv, dk), F32)],
        input_output_aliases={8: 0},
        compiler_params=_params(1),
    )(proj, proj, proj, a_tail, wg_p, bg, states, d_o, dproj)


def _gla_post_fwd(o, r, gn, name):
    dvt = o.shape[1]
    dv = dvt // GLA_HEADS

    def fn(o, r, gn):
        outs = []
        for h in range(GLA_HEADS):
            sl = slice(h * dv, (h + 1) * dv)
            ohat, _ = _norm_stats(o[:, sl])
            outs.append((ohat * gn[:, sl]) * _silu(r[:, sl]))
        return (jnp.concatenate(outs, axis=1),)

    return _rowwise(fn, [("row", o), r, ("full", gn)], [("row", dvt, BF16)], name=name)[0]


def _gla_post_bwd(o, r, gn, dog, name):
    dvt = o.shape[1]
    dv = dvt // GLA_HEADS

    def fn(o, r, gn, dog):
        d_o, d_r, d_g = [], [], []
        for h in range(GLA_HEADS):
            sl = slice(h * dv, (h + 1) * dv)
            ohat, rstd = _norm_stats(o[:, sl])
            g, rr, dd = gn[:, sl], r[:, sl], dog[:, sl]
            d_r.append(dd * (ohat * g) * _dsilu(rr))
            don = dd * _silu(rr)
            d_g.append(_colsum(don * ohat))
            d_o.append(_norm_bwd(don * g, ohat, rstd))
        return jnp.concatenate(d_o, axis=1), jnp.concatenate(d_r, axis=1), jnp.concatenate(d_g, axis=1)

    return _rowwise(fn, [("row", o), r, ("full", gn), ("row", dog)],
                    [("row", dvt, F32), ("band", dvt, BF16, 2, 3 * dvt), ("acc", dvt, F32)], name=name)


def _fox_prep(q, k, v, qg, kg, d, hd, name):
    heads = d // hd
    scale = hd ** -0.5

    def fn(q, k, v, qg, kg):
        qs, ks = [], []
        for h in range(heads):
            sl = slice(h * hd, (h + 1) * hd)
            qs.append(_norm_stats(q[:, sl])[0] * qg * scale)
            ks.append(_norm_stats(k[:, sl])[0] * kg)
        return jnp.concatenate(qs, axis=1), jnp.concatenate(ks, axis=1), v

    return _rowwise(fn, [q, k, v, ("full", qg), ("full", kg)],
                    [("row", d, BF16)] * 3, name=name)


def _fox_prep_bwd(q, k, dqn, dkn, qg, kg, hd, dproj, name):
    d = dqn.shape[1]
    heads = d // hd
    scale = hd ** -0.5

    def fn(q, k, dqn, dkn, qg, kg):
        dq, dk, gq, gk = [], [], [], []
        for h in range(heads):
            sl = slice(h * hd, (h + 1) * hd)
            for x, dxn, g, s, dl, gl in ((q, dqn, qg, scale, dq, gq), (k, dkn, kg, 1.0, dk, gk)):
                xhat, rstd = _norm_stats(x[:, sl])
                dn = dxn[:, sl] * s
                gl.append(_colsum(dn * xhat))
                dl.append(_norm_bwd(dn * g, xhat, rstd))
        cat = lambda t: jnp.concatenate(t, axis=1)
        return cat(dq + dk), cat(gq), cat(gk)

    return _rowwise(fn, [q, k, ("row", dqn), ("row", dkn), ("full", qg), ("full", kg)],
                    [("band", 2 * d, BF16, 0, 4 * d), ("acc", d, F32), ("acc", d, F32)], name=name, into=(dproj, 0))


def _fox_cum(fl, bf_p, name, tb=256):
    s = fl.shape[0]
    tb = _tile(s, tb)

    def body(fl_ref, bf_ref, cum_ref, carry):
        @pl.when(pl.program_id(0) == 0)
        def _():
            carry[...] = jnp.zeros_like(carry)

        lf = _log_sigmoid(fl_ref[...] + bf_ref[...])
        cum_ref[...] = _tri_matmul(_tri(tb), lf) + carry[...]
        carry[...] += _colsum(lf)

    return _pcall(
        body, name=name, grid=(s // tb,),
        in_specs=[pl.BlockSpec((tb, LANE), lambda i: (i, 0)), pl.BlockSpec((1, LANE), lambda i: (0, 0))],
        out_specs=pl.BlockSpec((tb, LANE), lambda i: (i, 0)),
        out_shape=jax.ShapeDtypeStruct((s, LANE), F32),
        scratch_shapes=[pltpu.VMEM((1, LANE), F32)],
        compiler_params=_params(1),
    )(fl, bf_p)


def _fox_cum_bwd(dcum, fl, bf_p, name, tb=256):
    s = fl.shape[0]
    tb = _tile(s, tb)
    nb = s // tb

    def body(dc_ref, fl_ref, bf_ref, dfl_ref, dbf_ref, carry):
        @pl.when(pl.program_id(0) == 0)
        def _():
            carry[...] = jnp.zeros_like(carry)
            dbf_ref[...] = jnp.zeros_like(dbf_ref)

        dc = dc_ref[...]
        dlf = _tri_matmul(_tri(tb, upper=True), dc) + carry[...]
        carry[...] += _colsum(dc)
        dfl = dlf * _sigmoid(-(fl_ref[...] + bf_ref[...]))
        dfl_ref[...] = dfl
        dbf_ref[...] += _colsum(dfl)

    rev = lambda i: (nb - 1 - i, 0)
    return _pcall(
        body, name=name, grid=(nb,),
        in_specs=[pl.BlockSpec((tb, LANE), rev), pl.BlockSpec((tb, LANE), rev), pl.BlockSpec((1, LANE), lambda i: (0, 0))],
        out_specs=[pl.BlockSpec((tb, LANE), rev), pl.BlockSpec((1, LANE), lambda i: (0, 0))],
        out_shape=[jax.ShapeDtypeStruct((s, LANE), F32), jax.ShapeDtypeStruct((1, LANE), F32)],
        scratch_shapes=[pltpu.VMEM((1, LANE), F32)],
        compiler_params=_params(1),
    )(dcum, fl, bf_p)


def _fox_attn_fwd(qn, kn, vb, cum_col, cum_row, hd, t, name):
    s, d = qn.shape
    heads = d // hd
    nq = s // t

    def body(q_ref, k_ref, v_ref, cc_ref, cr_ref, o_ref, lse_ref):
        qi = pl.program_id(1)
        q = q_ref[...]
        cq = cc_ref[...]
        qpos = qi * t + lax.broadcasted_iota(jnp.int32, (t, 1), 0)

        def step(kj, carry, diagonal=False):
            m, l, acc = carry
            off = pl.multiple_of(kj * t, t)
            ks, vs = k_ref[pl.ds(off, t), :], v_ref[pl.ds(off, t), :]
            sc = _dot(q, ks, tb=True) + cq - cr_ref[kj]
            if diagonal:
                kpos = off + lax.broadcasted_iota(jnp.int32, (1, t), 1)
                sc = jnp.where(kpos <= qpos, sc, NEG)
            m_new = jnp.maximum(m, jnp.max(sc, axis=1, keepdims=True))
            alpha = jnp.exp(m - m_new)
            p = jnp.exp(sc - m_new)
            return m_new, alpha * l + jnp.sum(p, axis=1, keepdims=True), alpha * acc + _dot(p, vs)

        init = (jnp.full((t, 1), NEG, F32), jnp.zeros((t, 1), F32), jnp.zeros((t, hd), F32))
        m, l, acc = step(qi, lax.fori_loop(0, qi, step, init), diagonal=True)
        o_ref[...] = acc / l
        lse_ref[...] = m + jnp.log(l)

    return _pcall(
        body, name=name, grid=(heads, nq),
        in_specs=[pl.BlockSpec((t, hd), lambda h, i: (i, h)),
                  pl.BlockSpec((s, hd), lambda h, i: (0, h)),
                  pl.BlockSpec((s, hd), lambda h, i: (0, h)),
                  pl.BlockSpec((None, t, 1), lambda h, i: (h, i, 0)),
                  pl.BlockSpec((None, nq, 1, t), lambda h, i: (h, 0, 0, 0))],
        out_specs=[pl.BlockSpec((t, hd), lambda h, i: (i, h)), pl.BlockSpec((None, t, 1), lambda h, i: (h, i, 0))],
        out_shape=[jax.ShapeDtypeStruct((s, d), F32), jax.ShapeDtypeStruct((heads, s, 1), F32)],
        compiler_params=_params(2),
    )(qn, kn, vb, cum_col, cum_row)


def _fox_attn_bwd(qn, kn, vb, d_o, o, lse, cum_col, cum_row, hd, t, dproj, name):
    s, d = qn.shape
    heads = d // hd
    nq = s // t

    def body(q_ref, k_ref, v_ref, do_ref, o_ref, lse_ref, cc_ref, cr_ref, dproj_in,
             dq_ref, dk_ref, dv_ref, dcq_ref, dck_ref, delta):
        kj = pl.program_id(1)

        @pl.when(kj == 0)
        def _():
            dq_ref[...] = jnp.zeros_like(dq_ref)
            dcq_ref[...] = jnp.zeros_like(dcq_ref)
            delta[...] = jnp.sum(do_ref[...] * o_ref[...], axis=1, keepdims=True)

        ks, vs, cr = k_ref[...], v_ref[...], cr_ref[...]
        kpos = kj * t + lax.broadcasted_iota(jnp.int32, (1, t), 1)

        def step(qi, carry, diagonal=False):
            dk, dv, dck = carry
            rows = pl.ds(pl.multiple_of(qi * t, t), t)
            q, d_out = q_ref[rows, :], do_ref[rows, :]
            sc = _dot(q, ks, tb=True) + cc_ref[rows, :] - cr
            p = jnp.exp(sc - lse_ref[rows, :])
            if diagonal:
                qpos = qi * t + lax.broadcasted_iota(jnp.int32, (t, 1), 0)
                p = jnp.where(kpos <= qpos, p, 0.0)
            ds = p * (_dot(d_out, vs, tb=True) - delta[rows, :])
            dq_ref[rows, :] += _dot(ds, ks)
            dcq_ref[rows, :] += jnp.sum(ds, axis=1, keepdims=True)
            return dk + _dot(ds, q, ta=True), dv + _dot(p, d_out, ta=True), dck + _colsum(ds)

        init = (jnp.zeros((t, hd), F32), jnp.zeros((t, hd), F32), jnp.zeros((1, t), F32))
        dk, dv, dck = lax.fori_loop(kj + 1, nq, step, step(kj, init, diagonal=True))
        dk_ref[...] = dk.astype(dk_ref.dtype)
        dv_ref[...] = dv.astype(dv_ref.dtype)
        dck_ref[...] = dck

    head_rows = lambda h, j: (0, h)
    blk = lambda h, j: (j, h)
    return _pcall(
        body, name=name, grid=(heads, nq),
        in_specs=[pl.BlockSpec((s, hd), head_rows), pl.BlockSpec((t, hd), blk), pl.BlockSpec((t, hd), blk),
                  pl.BlockSpec((s, hd), head_rows), pl.BlockSpec((s, hd), head_rows),
                  pl.BlockSpec((None, s, 1), lambda h, j: (h, 0, 0)),
                  pl.BlockSpec((None, s, 1), lambda h, j: (h, 0, 0)),
                  pl.BlockSpec((None, None, 1, t), lambda h, j: (h, j, 0, 0)),
                  pl.BlockSpec(memory_space=pl.ANY)],
        out_specs=[pl.BlockSpec((s, hd), head_rows), pl.BlockSpec((t, hd), blk),
                   pl.BlockSpec((t, hd), lambda h, j: (j, 2 * heads + h)),
                   pl.BlockSpec((None, s, 1), lambda h, j: (h, 0, 0)),
                   pl.BlockSpec((None, None, 1, t), lambda h, j: (h, j, 0, 0))],
        out_shape=[jax.ShapeDtypeStruct((s, d), F32), jax.ShapeDtypeStruct((s, d), BF16),
                   jax.ShapeDtypeStruct(dproj.shape, dproj.dtype), jax.ShapeDtypeStruct((heads, s, 1), F32),
                   jax.ShapeDtypeStruct((heads, nq, 1, t), F32)],
        scratch_shapes=[pltpu.VMEM((s, 1), F32)],
        input_output_aliases={8: 2},
        compiler_params=_params(2),
    )(qn, kn, vb, d_o, o, lse, cum_col, cum_row, dproj)


def _fox_gate_fwd(o, og, name):
    def fn(o, og):
        return (o * _sigmoid(og),)

    return _rowwise(fn, [("row", o), og], [("row", o.shape[1], BF16)], name=name)[0]


def _fox_gate_bwd(o, og, dact, name):
    def fn(o, og, dact):
        sg = _sigmoid(og)
        return dact * sg, dact * o * sg * (1.0 - sg)

    d = o.shape[1]
    return _rowwise(fn, [("row", o), og, ("row", dact)], [("row", d, F32), ("band", d, BF16, 3, 4 * d)], name=name)


def _shift_down(x, n):
    rows = lax.broadcasted_iota(jnp.int32, x.shape, 0)
    return jnp.where(rows >= n, pltpu.roll(x, n, 0), 0.0)


def _shift_up(x, n):
    rows = lax.broadcasted_iota(jnp.int32, x.shape, 0)
    return jnp.where(rows < x.shape[0] - n, pltpu.roll(x, x.shape[0] - n, 0), 0.0)


def _conv(u, w_ref, b):
    return w_ref[0:1, :] * _shift_down(u, 2) + w_ref[1:2, :] * _shift_down(u, 1) + w_ref[2:3, :] * u + b


def _conv_act_fwd(u, cw, cb, name, tc=256):
    s, two_f = u.shape
    dff = two_f // 2
    tc = _tile(dff, tc)
    nb = dff // tc

    def body(ug_ref, uv_ref, wg_ref, wv_ref, bg_ref, bv_ref, a_ref):
        gate = _conv(ug_ref[...], wg_ref, bg_ref[...])
        val = _conv(uv_ref[...], wv_ref, bv_ref[...])
        a_ref[...] = (_silu(gate) * val).astype(a_ref.dtype)

    lo, hi = (lambda j: (0, j)), (lambda j: (0, j + nb))
    return _pcall(
        body, name=name, grid=(nb,),
        in_specs=[pl.BlockSpec((s, tc), lo), pl.BlockSpec((s, tc), hi), pl.BlockSpec((3, tc), lo),
                  pl.BlockSpec((3, tc), hi), pl.BlockSpec((1, tc), lo), pl.BlockSpec((1, tc), hi)],
        out_specs=pl.BlockSpec((s, tc), lo),
        out_shape=jax.ShapeDtypeStruct((s, dff), BF16),
        compiler_params=_params(1),
    )(u, u, cw, cw, cb, cb)


def _conv_act_bwd(u, cw, cb, da, name, tc=128):
    s, two_f = u.shape
    dff = two_f // 2
    tc = _tile(dff, tc)
    nb = dff // tc

    def body(ug_ref, uv_ref, wg_ref, wv_ref, bg_ref, bv_ref, da_ref, du_ref, dw_ref, db_ref):
        ug, uv, da = ug_ref[...], uv_ref[...], da_ref[...]
        gate = _conv(ug, wg_ref, bg_ref[...])
        val = _conv(uv, wv_ref, bv_ref[...])
        sg = _sigmoid(gate)
        d_val = da * (gate * sg)
        d_gate = da * val * (sg * (1.0 + gate * (1.0 - sg)))
        for half, (dc, uu, w_ref) in enumerate(((d_gate, ug, wg_ref), (d_val, uv, wv_ref))):
            du = w_ref[0:1, :] * _shift_up(dc, 2) + w_ref[1:2, :] * _shift_up(dc, 1) + w_ref[2:3, :] * dc
            du_ref[half] = du.astype(du_ref.dtype)
            dw_ref[half, 0:1, :] = _colsum(dc * _shift_down(uu, 2))
            dw_ref[half, 1:2, :] = _colsum(dc * _shift_down(uu, 1))
            dw_ref[half, 2:3, :] = _colsum(dc * uu)
            db_ref[half] = _colsum(dc)

    lo, hi = (lambda j: (0, j)), (lambda j: (0, j + nb))
    both = lambda j: (0, 0, j)
    return _pcall(
        body, name=name, grid=(nb,),
        in_specs=[pl.BlockSpec((s, tc), lo), pl.BlockSpec((s, tc), hi), pl.BlockSpec((3, tc), lo),
                  pl.BlockSpec((3, tc), hi), pl.BlockSpec((1, tc), lo), pl.BlockSpec((1, tc), hi),
                  pl.BlockSpec((s, tc), lo)],
        out_specs=[pl.BlockSpec((2, s, tc), both), pl.BlockSpec((2, 3, tc), both), pl.BlockSpec((2, 1, tc), both)],
        out_shape=[jax.ShapeDtypeStruct((2, s, dff), BF16), jax.ShapeDtypeStruct((2, 3, dff), F32),
                   jax.ShapeDtypeStruct((2, 1, dff), F32)],
        compiler_params=_params(1),
    )(u, u, cw, cw, cb, cb, da)


def _adamw_math(w, g, m, v):
    m = ADAM_B1 * m + (1.0 - ADAM_B1) * g
    v = ADAM_B2 * v + (1.0 - ADAM_B2) * (g * g)
    m_hat = m / (1.0 - ADAM_B1 ** ADAM_STEP)
    v_hat = v / (1.0 - ADAM_B2 ** ADAM_STEP)
    delta = -ADAM_LR * (m_hat / (jnp.sqrt(v_hat) + ADAM_EPS) + ADAM_WD * w)
    return delta, m, v


def _update_tiles(r, c, tr):
    tc = c
    if r % 8:
        tr, tc = r, _tile(c, max(LANE, 512 * 1024 // r // LANE * LANE))
    elif r <= tr:
        tr = r
    while r % tr:
        tr -= 8
    return tr, tc


def _adamw(w, g, m, v, name, tr=128):
    layers, r, c = w.shape
    tr, tc = _update_tiles(r, c, tr)

    def body(w_ref, g_ref, m_ref, v_ref, go_ref, d_ref, mo_ref, vo_ref):
        grad = g_ref[...]
        delta, m_new, v_new = _adamw_math(w_ref[...], grad, m_ref[...], v_ref[...])
        go_ref[...], d_ref[...], mo_ref[...], vo_ref[...] = grad, delta, m_new, v_new

    spec = pl.BlockSpec((None, tr, tc), lambda l, i, j: (l, i, j))
    return _pcall(
        body, name=name, grid=(layers, r // tr, c // tc), in_specs=[spec] * 4, out_specs=[spec] * 4,
        out_shape=[jax.ShapeDtypeStruct((layers, r, c), F32)] * 4, compiler_params=_params(3),
    )(w, g, m, v)


def _adamw_pieces(w, lands, sums, chip, m, v, name, tr=128):
    layers, r, c = w.shape
    tr, tc = _update_tiles(r, c, tr)
    nr, nc = r // tr, c // tc

    def body(chip_ref, w_ref, *rest):
        land_refs, own_refs = rest[:layers], rest[layers:2 * layers]
        m_ref, v_ref, go_ref, d_ref, mo_ref, vo_ref = rest[2 * layers:]
        for layer in range(layers):
            @pl.when(pl.program_id(0) == layer)
            def _(land_ref=land_refs[layer], own_ref=own_refs[layer]):
                grad = jnp.zeros(w_ref.shape, F32)
                for q in range(4):
                    grad = grad + jnp.where(chip_ref[0] == q, own_ref[...], land_ref[q]).astype(F32)
                delta, m_new, v_new = _adamw_math(w_ref[...], grad, m_ref[...], v_ref[...])
                go_ref[...], d_ref[...], mo_ref[...], vo_ref[...] = grad, delta, m_new, v_new

    def walk(k, l, i, j):
        here = l == k
        return jnp.where(here, i, jnp.where(l < k, 0, nr - 1)), jnp.where(here, j, jnp.where(l < k, 0, nc - 1))

    spec = pl.BlockSpec((None, tr, tc), lambda l, i, j, chip_ref: (l, i, j))
    land_specs = [pl.BlockSpec((4, tr, tc), lambda l, i, j, chip_ref, k=k: (0,) + walk(k, l, i, j))
                  for k in range(layers)]
    own_specs = [pl.BlockSpec((None, tr, tc), lambda l, i, j, chip_ref, k=k: (chip_ref[0],) + walk(k, l, i, j))
                 for k in range(layers)]
    return _pcall(
        body, name=name,
        grid_spec=pltpu.PrefetchScalarGridSpec(
            num_scalar_prefetch=1, grid=(layers, nr, nc),
            in_specs=[spec] + land_specs + own_specs + [spec, spec], out_specs=[spec] * 4),
        out_shape=[jax.ShapeDtypeStruct((layers, r, c), F32)] * 4, compiler_params=_params(3),
    )(chip, w, *lands, *sums, m, v)


def _pair_sum(pieces, partner, core, name, tr=512):
    _, r, c = pieces.shape
    tc = c
    if r % 8:
        tr, tc = r, _tile(c, max(LANE, 1024 * 1024 // r // LANE * LANE))
    elif r <= tr:
        tr = r
    while r % tr:
        tr -= 8

    def body(core_ref, mine_ref, partner_ref, out_ref):
        out_ref[...] = (mine_ref[...].astype(F32) + partner_ref[...].astype(F32)).astype(out_ref.dtype)

    return _pcall(
        body, name=name,
        grid_spec=pltpu.PrefetchScalarGridSpec(
            num_scalar_prefetch=1, grid=(4, r // tr, c // tc),
            in_specs=[pl.BlockSpec((None, tr, tc), lambda q, i, j, core_ref: (2 * q + core_ref[0], i, j)),
                      pl.BlockSpec((None, tr, tc), lambda q, i, j, core_ref: (q, i, j))],
            out_specs=pl.BlockSpec((None, tr, tc), lambda q, i, j, core_ref: (q, i, j))),
        out_shape=jax.ShapeDtypeStruct((4, r, c), pieces.dtype), compiler_params=_params(3),
    )(core, pieces, partner)


def _sum8(x, name):
    p = x.shape[2]
    tp = _tile(p, 16 * 1024)

    def body(x_ref, o_ref):
        acc = x_ref[0]
        for i in range(1, N_DEV):
            acc = acc + x_ref[i]
        o_ref[...] = acc

    return _pcall(
        body, name=name, grid=(p // tp,), in_specs=[pl.BlockSpec((N_DEV, 1, tp), lambda i: (0, 0, i))],
        out_specs=pl.BlockSpec((1, tp), lambda i: (0, i)), out_shape=jax.ShapeDtypeStruct((1, p), x.dtype),
        compiler_params=_params(1),
    )(x)


def _exchange(arrays, name, scatter):
    n = len(arrays)
    hbm = pl.BlockSpec(memory_space=pl.ANY)

    def body(*refs):
        ins, outs, token = refs[:n], refs[n:2 * n], refs[2 * n]
        send_sems, recv_sems, local_sems = refs[2 * n + 1:]
        token[...] = jnp.zeros_like(token)
        x, y, c = lax.axis_index("x"), lax.axis_index("y"), lax.axis_index("c")
        me = 4 * x + 2 * y + c
        copies = []
        for a in range(n):
            src_mine = ins[a].at[me] if scatter else ins[a]
            local = pltpu.make_async_copy(src_mine, outs[a].at[me], local_sems.at[a])
            local.start()
            copies.append(local)
            for k in range(1, N_DEV):
                px = 1 - x if k & 4 else x
                py = 1 - y if k & 2 else y
                pc = 1 - c if k & 1 else c
                src = ins[a].at[4 * px + 2 * py + pc] if scatter else ins[a]
                cp = pltpu.make_async_remote_copy(
                    src_ref=src, dst_ref=outs[a].at[me],
                    send_sem=send_sems.at[a * (N_DEV - 1) + k - 1], recv_sem=recv_sems.at[a * (N_DEV - 1) + k - 1],
                    device_id=(px, py, pc), device_id_type=pl.DeviceIdType.MESH)
                cp.start()
                copies.append(cp)
        for cp in copies:
            cp.wait()

    out_shape = [jax.ShapeDtypeStruct(a.shape if scatter else (N_DEV,) + a.shape, a.dtype) for a in arrays]
    res = _pcall(
        body, name=name, in_specs=[hbm] * n, out_specs=[hbm] * n + [pl.BlockSpec(memory_space=pltpu.VMEM)],
        out_shape=out_shape + [jax.ShapeDtypeStruct((8, LANE), F32)],
        scratch_shapes=[pltpu.SemaphoreType.DMA((n * (N_DEV - 1),)), pltpu.SemaphoreType.DMA((n * (N_DEV - 1),)),
                        pltpu.SemaphoreType.DMA((n,))],
        compiler_params=pltpu.CompilerParams(has_side_effects=True),
    )(*arrays)
    return res[:n], res[n][0, 0]


_HBM = pl.BlockSpec(memory_space=pltpu.HBM)
_SEM = pl.BlockSpec(memory_space=pltpu.SEMAPHORE)
_DATAFLOW = pltpu.SideEffectType.DATAFLOW_SIDE_EFFECTING


def _peer(k, x, y, c):
    return (1 - x if k & 4 else x, 1 - y if k & 2 else y, 1 - c if k & 1 else c)


def _pair_plan(x, y, c):
    return [(2 * q + (1 - c), q, (x, y, 1 - c)) for q in range(4)]


def _chip_plan(x, y, c):
    out = []
    for k in _ICI_PEERS:
        px, py, pc = _peer(k, x, y, c)
        out.append((2 * px + py, 2 * x + y, (px, py, pc)))
    return out


def _all_plan(x, y, c):
    return [(0, 4 * x + 2 * y + c, _peer(k, x, y, c)) for k in range(1, N_DEV)]


def _split_start(arrays, plan, name, land_blocks=4):
    n = len(arrays)
    lands = [lax.empty((land_blocks,) + a.shape[1:], a.dtype) for a in arrays]
    n_copies = len(plan(0, 0, 0))

    def body(*refs):
        srcs, dsts = refs[:n], refs[n:2 * n]
        send_sems, recv_sems, token = refs[4 * n:5 * n], refs[5 * n:6 * n], refs[6 * n]
        copies = plan(lax.axis_index("x"), lax.axis_index("y"), lax.axis_index("c"))
        for a in range(n):
            for j, (src_block, dst_block, peer) in enumerate(copies):
                pltpu.make_async_remote_copy(
                    src_ref=srcs[a].at[src_block], dst_ref=dsts[a].at[dst_block],
                    send_sem=send_sems[a].at[j], recv_sem=recv_sems[a].at[j],
                    device_id=peer, device_id_type=pl.DeviceIdType.MESH).start()
        token[...] = jnp.zeros_like(token)

    sems = [pltpu.SemaphoreType.DMA((n_copies,))] * (2 * n)
    res = _pcall(
        body, name=name,
        in_specs=[_HBM] * (2 * n),
        out_specs=[_HBM] * (2 * n) + [_SEM] * (2 * n) + [pl.BlockSpec(memory_space=pltpu.VMEM)],
        out_shape=[pltpu.HBM(a.shape, a.dtype) for a in arrays] + [pltpu.HBM(l.shape, l.dtype) for l in lands]
        + sems + [jax.ShapeDtypeStruct((8, LANE), F32)],
        input_output_aliases={i: i for i in range(2 * n)},
        compiler_params=pltpu.CompilerParams(has_side_effects=_DATAFLOW),
    )(*[pltpu.with_memory_space_constraint(a, pltpu.HBM) for a in arrays],
      *[pltpu.with_memory_space_constraint(l, pltpu.HBM) for l in lands])
    handles = [(res[a], res[n + a], res[2 * n + a], res[3 * n + a]) for a in range(n)]
    return handles, res[4 * n][0, 0]


def _split_wait(handles, plan, after, name):
    n = len(handles)
    after = list(after) if isinstance(after, (list, tuple)) else [after]

    def body(*refs):
        srcs, dsts = refs[:n], refs[n:2 * n]
        send_sems, recv_sems = refs[2 * n:3 * n], refs[3 * n:4 * n]
        copies = plan(lax.axis_index("x"), lax.axis_index("y"), lax.axis_index("c"))
        for a in range(n):
            for j, (src_block, dst_block, peer) in enumerate(copies):
                cp = pltpu.make_async_remote_copy(
                    src_ref=srcs[a].at[src_block], dst_ref=dsts[a].at[dst_block],
                    send_sem=send_sems[a].at[j], recv_sem=recv_sems[a].at[j],
                    device_id=peer, device_id_type=pl.DeviceIdType.MESH)
                cp.wait_send()
                cp.wait_recv()

    srcs, lands = [h[0] for h in handles], [h[1] for h in handles]
    res = _pcall(
        body, name=name,
        in_specs=[_HBM] * (2 * n) + [_SEM] * (2 * n) + [pl.BlockSpec(memory_space=pl.ANY)] * len(after),
        out_specs=[_HBM] * (2 * n),
        out_shape=[pltpu.HBM(t.shape, t.dtype) for t in srcs + lands],
        input_output_aliases={i: i for i in range(2 * n)},
        compiler_params=pltpu.CompilerParams(has_side_effects=_DATAFLOW),
    )(*srcs, *lands, *[h[2] for h in handles], *[h[3] for h in handles], *after)
    return res[:n], res[n:]


_ICI_PEERS = (2, 4, 6)


def _gather2_start(shards, name):
    n = len(shards)
    lands = [lax.empty((N_DEV,) + a.shape, a.dtype) for a in shards]

    def body(*refs):
        srcs, dsts = refs[:n], refs[n:2 * n]
        send_sems, d2d_sems, ici_sems = refs[4 * n:5 * n], refs[5 * n:6 * n], refs[6 * n:7 * n]
        token = refs[7 * n]
        x, y, c = lax.axis_index("x"), lax.axis_index("y"), lax.axis_index("c")
        me = 4 * x + 2 * y + c
        for a in range(n):
            for j, k in enumerate((1,) + _ICI_PEERS):
                recv = d2d_sems[a].at[0] if j == 0 else ici_sems[a].at[j - 1]
                pltpu.make_async_remote_copy(
                    src_ref=srcs[a], dst_ref=dsts[a].at[me], send_sem=send_sems[a].at[j], recv_sem=recv,
                    device_id=_peer(k, x, y, c), device_id_type=pl.DeviceIdType.MESH).start()
        token[...] = jnp.zeros_like(token)

    dma = pltpu.SemaphoreType.DMA
    res = _pcall(
        body, name=name,
        in_specs=[_HBM] * (2 * n),
        out_specs=[_HBM] * (2 * n) + [_SEM] * (3 * n) + [pl.BlockSpec(memory_space=pltpu.VMEM)],
        out_shape=[pltpu.HBM(a.shape, a.dtype) for a in shards] + [pltpu.HBM(l.shape, l.dtype) for l in lands]
        + [dma((4,))] * n + [dma((1,))] * n + [dma((3,))] * n + [jax.ShapeDtypeStruct((8, LANE), F32)],
        input_output_aliases={i: i for i in range(2 * n)},
        compiler_params=pltpu.CompilerParams(has_side_effects=_DATAFLOW),
    )(*[pltpu.with_memory_space_constraint(a, pltpu.HBM) for a in shards],
      *[pltpu.with_memory_space_constraint(l, pltpu.HBM) for l in lands])
    handles = [tuple(res[i * n + a] for i in range(5)) for a in range(n)]
    return handles, res[5 * n][0, 0]


def _gather2_forward(handle, after, name):
    src, land, send_sems, d2d_sem, ici_sems = handle

    def body(land_ref, ici_ref, d2d_ref, after_ref, land_out, fwd_send, fwd_recv, token):
        x, y, c = lax.axis_index("x"), lax.axis_index("y"), lax.axis_index("c")
        sibling = (x, y, 1 - c)
        arrived = [(_peer(k, x, y, c), ici_ref.at[j]) for j, k in enumerate(_ICI_PEERS)] + [(sibling, d2d_ref.at[0])]
        for j, ((px, py, pc), recv) in enumerate(arrived):
            block = land_ref.at[4 * px + 2 * py + pc]
            pltpu.make_async_remote_copy(
                src_ref=block, dst_ref=block, send_sem=fwd_send.at[j], recv_sem=recv,
                device_id=(px, py, pc), device_id_type=pl.DeviceIdType.MESH).wait_recv()
            pltpu.make_async_remote_copy(
                src_ref=block, dst_ref=block, send_sem=fwd_send.at[j], recv_sem=fwd_recv.at[j],
                device_id=sibling, device_id_type=pl.DeviceIdType.MESH).start()
        token[...] = jnp.zeros_like(token)

    dma = pltpu.SemaphoreType.DMA
    land, fwd_send, fwd_recv, token = _pcall(
        body, name=name,
        in_specs=[_HBM, _SEM, _SEM, pl.BlockSpec(memory_space=pl.ANY)],
        out_specs=[_HBM, _SEM, _SEM, pl.BlockSpec(memory_space=pltpu.VMEM)],
        out_shape=[pltpu.HBM(land.shape, land.dtype), dma((4,)), dma((4,)), jax.ShapeDtypeStruct((8, LANE), F32)],
        input_output_aliases={0: 0},
        compiler_params=pltpu.CompilerParams(has_side_effects=_DATAFLOW),
    )(land, ici_sems, d2d_sem, after)
    return (src, land, send_sems, fwd_send, fwd_recv), token[0, 0]


def _gather2_wait(handle, after, name):
    src, land, send_sems, fwd_send, fwd_recv = handle

    def body(src_ref, land_ref, send_ref, fsend_ref, frecv_ref, after_ref, src_out, land_out):
        x, y, c = lax.axis_index("x"), lax.axis_index("y"), lax.axis_index("c")
        block = land_ref.at[4 * x + 2 * y + c]

        def copy(send, recv):
            return pltpu.make_async_remote_copy(src_ref=src_ref, dst_ref=block, send_sem=send, recv_sem=recv,
                                                device_id=(x, y, 1 - c), device_id_type=pl.DeviceIdType.MESH)

        for j in range(4):
            copy(send_ref.at[j], frecv_ref.at[j]).wait_send()
        for j in range(4):
            copy(fsend_ref.at[j], frecv_ref.at[j]).wait_send()
            copy(fsend_ref.at[j], frecv_ref.at[j]).wait_recv()

    res = _pcall(
        body, name=name,
        in_specs=[_HBM, _HBM, _SEM, _SEM, _SEM, pl.BlockSpec(memory_space=pl.ANY)],
        out_specs=[_HBM, _HBM],
        out_shape=[pltpu.HBM(src.shape, src.dtype), pltpu.HBM(land.shape, land.dtype)],
        input_output_aliases={0: 0, 1: 1},
        compiler_params=pltpu.CompilerParams(has_side_effects=_DATAFLOW),
    )(src, land, send_sems, fwd_send, fwd_recv, after)
    return res[0], res[1]


def _pad_cols(x, width=LANE):
    return jnp.pad(x, ((0, 0), (0, width - x.shape[1])))


def _cols_full(g):
    return jnp.transpose(g, (1, 0, 2)).reshape(g.shape[1], -1)


def _ffn_fwd(x1, p, i, tag):
    h2 = _adaln_fwd(x1, p["norm_ffn"][i], p["sc_f"][i], p["sh_f"][i], f"ffn_norm_{tag}")
    u = _matmul(h2, p["fetch"](f"up{i}", h2), name=f"ffn_up_{tag}", tn=1408, b_shards=True)
    a = _conv_act_fwd(u, p["conv_w"][i], p["conv_b"][i], f"ffn_act_{tag}")
    g_f = p["g_f"][i]
    x2, f = _matmul(a, p["fetch"](f"down{i}", a), name=f"ffn_down_{tag}", tk=1408, out_dtypes=(F32, F32),
                    epilogue=lambda acc, x1, g: (x1 + (1.0 + g) * acc, acc), extras=(("mn", x1), ("n", g_f)))
    return x2, dict(h2=h2, u=u, a=a, f=f)


def _ffn_bwd(dx2, x1, saved, p, i, tag):
    d = x1.shape[1]
    w_up, w_down = p["fetch"](f"up{i}", None), p["fetch"](f"down{i}", None)
    df, dg_f = _residual_bwd(dx2, saved["f"], p["g_f"][i], f"ffn_res_bwd_{tag}")
    da = _matmul(df, w_down, tb=True, name=f"ffn_down_dx_{tag}", tn=1408)
    dw_down = _matmul(saved["a"], df, ta=True, name=f"ffn_down_dw_{tag}", tm=1408, out_dtypes=(BF16,))
    du, dcw, dcb = _conv_act_bwd(saved["u"], p["conv_w"][i], p["conv_b"][i], da, f"ffn_act_bwd_{tag}")
    dcw, dcb = (jnp.concatenate([t[0], t[1]], axis=1) for t in (dcw, dcb))
    tok = p["flush"](du)
    dh2 = _matmul(du, w_up, tb=True, name=f"ffn_up_dx_{tag}", tk=1408, a_halves=True, b_shards=True)
    dw_up = _matmul(saved["h2"], du, ta=True, name=f"ffn_up_dw_{tag}", tn=1408, out_dtypes=(BF16,), b_halves=True,
                    out_shards=True)
    tok = tok + p["send"](f"ffn{i}", [dw_up, dw_down.reshape(N_DEV, -1, d)])
    dx1, dsh, dsc, dgain = _adaln_bwd(x1, dh2, dx2, p["norm_ffn"][i] + tok, p["sc_f"][i], f"ffn_norm_bwd_{tag}")
    grads = dict(conv_w=dcw, conv_b=dcb, norm_ffn=dgain, sh_f=dsh, sc_f=dsc, g_f=dg_f)
    return dx1, grads


def _gla_layer_fwd(x, p, i):
    h1 = _adaln_fwd(x, p["norm_mix"][i], p["sc_m"][i], p["sh_m"][i], "gla_norm")
    w_t, w_tail_t, main = p["fetch"]("gla_in", h1)
    proj = _matmul(h1, w_t, tb=True, b_rows=main, name="gla_in")
    a_tail = _matmul(h1, w_tail_t, tb=True, name="gla_in_tail")
    dk_total = p["gla_wg_p"].shape[1]
    o, states = _gla_fwd(proj, a_tail, p["gla_wg_p"], p["gla_b_gate"], "gla_chunks")
    assert 2 * dk_total == o.shape[1]
    r = ("cols", proj, 2, o.shape[1])
    og = _gla_post_fwd(o, r, p["gla_norm"], "gla_post")
    x1, y = _matmul(og, p["fetch"]("gla_out", og), name="gla_out", out_dtypes=(F32, F32),
                    epilogue=lambda acc, x, g: (x + (1.0 + g) * acc, acc), extras=(("mn", x), ("n", p["g_m"][i])))
    return x1, dict(h1=h1, proj=proj, a_tail=a_tail, o=o, r=r, states=states, og=og, y=y)


def _gla_layer_bwd(dx1, x, sv, p, i):
    d = x.shape[1]
    (w_t, w_tail_t, main), w_out = p["fetch"]("gla_in", None), p["fetch"]("gla_out", None)
    dy, dg_m = _residual_bwd(dx1, sv["y"], p["g_m"][i], "gla_res_bwd")
    dog = _matmul(dy, w_out, tb=True, name="gla_out_dx")
    dw_out = _matmul(sv["og"], dy, ta=True, name="gla_out_dw", out_dtypes=(BF16,))
    tok = p["flush"](dog) + p["send"]("gla_out", [dw_out.reshape(N_DEV, -1, d)])
    d_o, dproj, dgn = _gla_post_bwd(sv["o"], sv["r"], p["gla_norm"] + tok, dog, "gla_post_bwd")
    dproj, dga = _gla_bwd(sv["proj"], sv["a_tail"], p["gla_wg_p"], p["gla_b_gate"], sv["states"], d_o, dproj,
                          "gla_chunks_bwd")
    tok = p["flush"](dga)
    da_tail = _matmul(dga, p["gla_wg_p"], tb=True, name="gla_gate_dx", out_dtypes=(BF16,))
    dwg = _matmul(sv["a_tail"], dga, ta=True, name="gla_gate_dw")
    dbg = _rowwise(lambda t: (_colsum(t),), [("row", dga)], [("acc", dga.shape[1], F32)], name="gla_gate_db")[0]
    dh_tail = _matmul(da_tail, w_tail_t, name="gla_in_tail_dx")
    dh1 = _matmul(dproj, w_t, b_rows=main, name="gla_in_dx", tk=2048,
                  epilogue=lambda acc, t: (acc + t,), extras=(("mn", dh_tail),))
    rank = p["gla_rank"]
    dw_main = _matmul(dproj, sv["h1"], ta=True, name="gla_in_dw", out_dtypes=(BF16,), out_rows=main + rank)
    dx, dsh, dsc, dgain = _adaln_bwd(x, dh1, dx1, p["norm_mix"][i] + tok, p["sc_m"][i], "gla_norm_bwd")
    grads = dict(gla_w_gate=dwg[:rank], gla_b_gate=dbg, gla_norm=dgn, norm_mix=dgain, sh_m=dsh, sc_m=dsc, g_m=dg_m,
                 gla_w_in_unsent=(dw_main, da_tail, sv["h1"]))
    return dx, grads


def _fox_layer_fwd(x, p, i):
    d = x.shape[1]
    hd = p["fox_q_norm"].shape[1]
    heads = d // hd
    s = x.shape[0]
    t = _tile(s, 512)
    h1 = _adaln_fwd(x, p["norm_mix"][i], p["sc_m"][i], p["sh_m"][i], "fox_norm")
    w_t, w_tail_t, main = p["fetch"]("fox_in", h1)
    proj = _matmul(h1, w_t, tb=True, b_rows=main, name="fox_in")
    fl = _matmul(h1, w_tail_t, tb=True, name="fox_in_tail")
    q, k, v, og = (("cols", proj, j, d) for j in range(4))
    qn, kn, vb = _fox_prep(q, k, v, p["fox_q_norm"], p["fox_k_norm"], d, hd, "fox_prep")
    cum = _fox_cum(fl, p["fox_bf_p"], "fox_cum")
    cum_t = jnp.transpose(cum[:, :heads])
    cum_col, cum_row = cum_t[:, :, None], cum_t.reshape(heads, s // t, 1, t)
    o, lse = _fox_attn_fwd(qn, kn, vb, cum_col, cum_row, hd, t, "fox_attn")
    act = _fox_gate_fwd(o, og, "fox_gate")
    x1, y = _matmul(act, p["fetch"]("fox_out", act), name="fox_out", out_dtypes=(F32, F32),
                    epilogue=lambda acc, x, g: (x + (1.0 + g) * acc, acc), extras=(("mn", x), ("n", p["g_m"][i])))
    return x1, dict(h1=h1, q=q, k=k, og=og, fl=fl, qn=qn, kn=kn, vb=vb, cum_col=cum_col, cum_row=cum_row,
                    o=o, lse=lse, act=act, y=y, t=t, hd=hd)


def _fox_layer_bwd(dx1, x, sv, p, i):
    d = x.shape[1]
    hd, t = sv["hd"], sv["t"]
    heads = d // hd
    s = x.shape[0]
    (w_t, w_tail_t, main), w_out = p["fetch"]("fox_in", None), p["fetch"]("fox_out", None)
    dy, dg_m = _residual_bwd(dx1, sv["y"], p["g_m"][i], "fox_res_bwd")
    dact = _matmul(dy, w_out, tb=True, name="fox_out_dx")
    dw_out = _matmul(sv["act"], dy, ta=True, name="fox_out_dw", out_dtypes=(BF16,))
    d_o, dproj = _fox_gate_bwd(sv["o"], sv["og"], dact, "fox_gate_bwd")
    tok_flush = p["flush"](d_o)
    dqn, dkn, dproj, dcq, dck = _fox_attn_bwd(sv["qn"], sv["kn"], sv["vb"], d_o, sv["o"], sv["lse"], sv["cum_col"],
                                              sv["cum_row"], hd, t, dproj, "fox_attn_bwd")
    dproj, gq, gk = _fox_prep_bwd(sv["q"], sv["k"], dqn, dkn, p["fox_q_norm"], p["fox_k_norm"], hd, dproj,
                                  "fox_prep_bwd")
    dcum = _pad_cols(jnp.transpose(dcq[:, :, 0] - dck.reshape(heads, s)))
    dfl, dbf = _fox_cum_bwd(dcum, sv["fl"], p["fox_bf_p"], "fox_cum_bwd")
    dfl_b = dfl.astype(BF16)
    dh_tail = _matmul(dfl_b, w_tail_t, name="fox_in_tail_dx")
    dh1 = _matmul(dproj, w_t, b_rows=main, name="fox_in_dx", tk=2048,
                  epilogue=lambda acc, tl: (acc + tl,), extras=(("mn", dh_tail),))
    dw_main = _matmul(dproj, sv["h1"], ta=True, name="fox_in_dw", out_dtypes=(BF16,), out_rows=main + heads)
    dw_in = _tail_rows(dfl_b, sv["h1"], dw_main, heads, "fox_in_tail_dw").reshape(N_DEV, -1, d)
    tok = tok_flush + p["send"]("fox", [dw_in, dw_out.reshape(N_DEV, -1, d)])
    dx, dsh, dsc, dgain = _adaln_bwd(x, dh1, dx1, p["norm_mix"][i] + tok, p["sc_m"][i], "fox_norm_bwd")
    grads = dict(fox_b_f=dbf[:, :heads], fox_q_norm=gq.reshape(heads, hd).sum(0, keepdims=True),
                 fox_k_norm=gk.reshape(heads, hd).sum(0, keepdims=True), norm_mix=dgain, sh_m=dsh, sc_m=dsc, g_m=dg_m)
    return dx, grads


SMALL = ("b_mod", "norm_mix", "norm_ffn", "gla_b_gate", "gla_norm", "fox_b_f", "fox_q_norm", "fox_k_norm",
         "ffn_conv_b", "norm_final")
SMALL_SHARDED = ("gla_w_gate", "ffn_conv_w")
BIG = ("gla_w_in", "gla_w_out", "fox_w_in", "fox_w_out", "ffn_w_up", "ffn_w_down")
WEIGHTS = ("w_mod", "b_mod", "norm_mix", "norm_ffn", "gla_w_in", "gla_w_gate", "gla_b_gate", "gla_norm", "gla_w_out",
           "fox_w_in", "fox_b_f", "fox_q_norm", "fox_k_norm", "fox_w_out", "ffn_w_up", "ffn_conv_w", "ffn_conv_b",
           "ffn_w_down", "norm_final")


def _pack(parts):
    flat = jnp.concatenate([p.reshape(-1) for p in parts])
    pad = (-flat.shape[0]) % 1024
    return jnp.pad(flat, (0, pad)).reshape(1, -1)


def _unpack(flat, shapes):
    out, off = [], 0
    for shp in shapes:
        n = 1
        for s in shp:
            n *= s
        out.append(flat[0, off:off + n].reshape(shp))
        off += n
    return out


def kernel(x, c, w_mod, b_mod, norm_mix, norm_ffn, gla_w_in, gla_w_gate, gla_b_gate, gla_norm, gla_w_out, fox_w_in, fox_b_f, fox_q_norm, fox_k_norm, fox_w_out, ffn_w_up, ffn_conv_w, ffn_conv_b, ffn_w_down, norm_final, loss_target, m_w_mod, m_b_mod, m_norm_mix, m_norm_ffn, m_gla_w_in, m_gla_w_gate, m_gla_b_gate, m_gla_norm, m_gla_w_out, m_fox_w_in, m_fox_b_f, m_fox_q_norm, m_fox_k_norm, m_fox_w_out, m_ffn_w_up, m_ffn_conv_w, m_ffn_conv_b, m_ffn_w_down, m_norm_final, v_w_mod, v_b_mod, v_norm_mix, v_norm_ffn, v_gla_w_in, v_gla_w_gate, v_gla_b_gate, v_gla_norm, v_gla_w_out, v_fox_w_in, v_fox_b_f, v_fox_q_norm, v_fox_k_norm, v_fox_w_out, v_ffn_w_up, v_ffn_conv_w, v_ffn_conv_b, v_ffn_w_down, v_norm_final):
    w = dict(w_mod=w_mod, b_mod=b_mod, norm_mix=norm_mix, norm_ffn=norm_ffn, gla_w_in=gla_w_in, gla_w_gate=gla_w_gate,
             gla_b_gate=gla_b_gate, gla_norm=gla_norm, gla_w_out=gla_w_out, fox_w_in=fox_w_in, fox_b_f=fox_b_f,
             fox_q_norm=fox_q_norm, fox_k_norm=fox_k_norm, fox_w_out=fox_w_out, ffn_w_up=ffn_w_up,
             ffn_conv_w=ffn_conv_w, ffn_conv_b=ffn_conv_b, ffn_w_down=ffn_w_down, norm_final=norm_final)
    mom_m = dict(w_mod=m_w_mod, b_mod=m_b_mod, norm_mix=m_norm_mix, norm_ffn=m_norm_ffn, gla_w_in=m_gla_w_in,
                 gla_w_gate=m_gla_w_gate, gla_b_gate=m_gla_b_gate, gla_norm=m_gla_norm, gla_w_out=m_gla_w_out,
                 fox_w_in=m_fox_w_in, fox_b_f=m_fox_b_f, fox_q_norm=m_fox_q_norm, fox_k_norm=m_fox_k_norm,
                 fox_w_out=m_fox_w_out, ffn_w_up=m_ffn_w_up, ffn_conv_w=m_ffn_conv_w, ffn_conv_b=m_ffn_conv_b,
                 ffn_w_down=m_ffn_w_down, norm_final=m_norm_final)
    mom_v = dict(w_mod=v_w_mod, b_mod=v_b_mod, norm_mix=v_norm_mix, norm_ffn=v_norm_ffn, gla_w_in=v_gla_w_in,
                 gla_w_gate=v_gla_w_gate, gla_b_gate=v_gla_b_gate, gla_norm=v_gla_norm, gla_w_out=v_gla_w_out,
                 fox_w_in=v_fox_w_in, fox_b_f=v_fox_b_f, fox_q_norm=v_fox_q_norm, fox_k_norm=v_fox_k_norm,
                 fox_w_out=v_fox_w_out, ffn_w_up=v_ffn_w_up, ffn_conv_w=v_ffn_conv_w, ffn_conv_b=v_ffn_conv_b,
                 ffn_w_down=v_ffn_w_down, norm_final=v_norm_final)

    me = 4 * lax.axis_index("x") + 2 * lax.axis_index("y") + lax.axis_index("c")
    xs, target = x[0], loss_target[0]
    s, d = xs.shape
    depth = w_mod.shape[0]
    mod_cols = w_mod.shape[2]
    rank = gla_w_gate.shape[1]
    hd = fox_q_norm.shape[1]
    fox_heads = d // hd
    dk_total = gla_w_gate.shape[2] * N_DEV

    cond = c * (1.0 / (1.0 + jnp.exp(-c)))
    g, _ = _exchange([gla_w_gate[0], ffn_conv_w, cond], "gather_small", scatter=False)
    cond_all = g[2][:, 0, :]

    cond_pad = jnp.pad(cond_all, ((0, 16 - N_DEV), (0, 0)))
    mod_part = []
    for i in range(depth):
        b_cols = lax.dynamic_slice(b_mod[i:i + 1], (0, me * mod_cols), (1, mod_cols))
        mod_part.append(_matmul(cond_pad, w_mod, b_layer=i, name=f"mod_{i}", tn=768,
                                epilogue=lambda acc, b: (acc + b,), extras=(("n", b_cols),))[:N_DEV])
    (mod_all,), tok_mod = _exchange([jnp.stack(mod_part)], "gather_mod", scatter=False)
    mod = lax.dynamic_index_in_dim(mod_all, me, axis=2, keepdims=False)
    mod = jnp.transpose(mod, (1, 0, 2)).reshape(depth, 6, 1, d)

    big_names = ["gla_in", "gla_out", "up0", "down0", "fox_in", "fox_out", "up1", "down1"]
    first = [jnp.transpose(gla_w_in[0] + tok_mod).astype(BF16), gla_w_out[0].astype(BF16)]
    handles, tok_first = _gather2_start(first, "gather_weights_start_first")
    rest = [ffn_w_up[0] + tok_first, ffn_w_down[0], jnp.transpose(fox_w_in[0]), fox_w_out[0], ffn_w_up[1],
            ffn_w_down[1]]
    handles_rest, tok0 = _gather2_start([t.astype(BF16) for t in rest], "gather_weights_start_rest")
    handles = handles + handles_rest
    ready, forwarded = {}, {}

    def split_tail(full_t, tail):
        main = full_t.shape[0] - tail
        return full_t, jnp.pad(full_t[main:], ((0, LANE - tail), (0, 0))), main

    def forward(idx, after):
        key = big_names[idx]
        forwarded[key] = _gather2_forward(handles[idx], after, f"gather_{key}_forward")

    def fetch(key, after):
        if key not in ready:
            idx = big_names.index(key)
            if idx == 0:
                forward(0, after)
            handle, _ = forwarded[key]
            _, full = _gather2_wait(handle, after, f"gather_{key}_wait")
            if idx + 1 < len(big_names):
                forward(idx + 1, full)
            if key == "gla_in":
                ready[key] = split_tail(full.reshape(-1, d), rank)
            elif key == "fox_in":
                ready[key] = split_tail(full.reshape(-1, d), fox_heads)
            elif key.startswith("up"):
                ready[key] = full
            else:
                ready[key] = full.reshape(-1, d)
        return ready[key]

    pending, sent = [], {}
    core = lax.axis_index("c").astype(jnp.int32).reshape(1)
    chip = 2 * lax.axis_index("x") + lax.axis_index("y")

    def send(key, pieces):
        hs, tok = _split_start(pieces, _pair_plan, f"scatter_{key}_pair_start")
        pending.append((key, hs))
        return tok

    def flush(after):
        tok = 0.0
        while pending:
            key, hs = pending.pop(0)
            mine, partner = _split_wait(hs, _pair_plan, after, f"scatter_{key}_pair_wait")
            sums = [_pair_sum(pc, pt, core, f"scatter_{key}_pair_sum{a}")
                    for a, (pc, pt) in enumerate(zip(mine, partner))]
            sent[key], t = _split_start(sums, _chip_plan, f"scatter_{key}_chip_start")
            tok = tok + t
        return tok

    p = dict(
        fetch=fetch, send=send, flush=flush,
        gla_wg_p=jnp.pad(_cols_full(g[0]), ((0, LANE - rank), (0, 0))),
        conv_w=[jnp.transpose(g[1][:, i], (1, 0, 2)).reshape(ffn_conv_w.shape[1], -1) for i in range(depth)],
        conv_b=[ffn_conv_b[i:i + 1] for i in range(depth)],
        gla_b_gate=gla_b_gate, gla_norm=gla_norm, fox_q_norm=fox_q_norm, fox_k_norm=fox_k_norm,
        fox_bf_p=_pad_cols(fox_b_f), gla_rank=rank,
        norm_mix=[norm_mix[i:i + 1] + (tok0 if i == 0 else 0.0) for i in range(depth)],
        norm_ffn=[norm_ffn[i:i + 1] for i in range(depth)],
    )

    for j, nm in enumerate(("sh_m", "sc_m", "g_m", "sh_f", "sc_f", "g_f")):
        p[nm] = [mod[i, j] for i in range(depth)]

    acts, saved = [xs], []
    for i in range(depth):
        layer_fwd = _gla_layer_fwd if i % 2 == 0 else _fox_layer_fwd
        x1, sv_mix = layer_fwd(acts[-1], p, i)
        x2, sv_ffn = _ffn_fwd(x1, p, i, str(i))
        saved.append((acts[-1], x1, sv_mix, sv_ffn))
        acts.append(x2)
    dx, d_norm_final, loss_part = _final_loss(acts[-1], target, norm_final.reshape(1, d), "final_loss")

    lg = [None] * depth
    for i in reversed(range(depth)):
        x_in, x1, sv_mix, sv_ffn = saved[i]
        dx, g_ffn = _ffn_bwd(dx, x1, sv_ffn, p, i, str(i))
        layer_bwd = _gla_layer_bwd if i % 2 == 0 else _fox_layer_bwd
        dx, g_mix = layer_bwd(dx, x_in, sv_mix, p, i)
        lg[i] = {**g_ffn, **g_mix}
    grad_x = dx[None]

    gla_l = [i for i in range(depth) if i % 2 == 0]
    fox_l = [i for i in range(depth) if i % 2 == 1]
    small_parts = dict(
        norm_mix=jnp.concatenate([lg[i]["norm_mix"] for i in range(depth)]),
        norm_ffn=jnp.concatenate([lg[i]["norm_ffn"] for i in range(depth)]),
        gla_b_gate=jnp.concatenate([lg[i]["gla_b_gate"] for i in gla_l]),
        gla_norm=jnp.concatenate([lg[i]["gla_norm"] for i in gla_l]),
        fox_b_f=jnp.concatenate([lg[i]["fox_b_f"] for i in fox_l]),
        fox_q_norm=jnp.concatenate([lg[i]["fox_q_norm"] for i in fox_l]),
        fox_k_norm=jnp.concatenate([lg[i]["fox_k_norm"] for i in fox_l]),
        ffn_conv_b=jnp.concatenate([lg[i]["conv_b"] for i in range(depth)]),
        norm_final=d_norm_final,
        gla_w_gate=jnp.stack([lg[i]["gla_w_gate"] for i in gla_l]),
        ffn_conv_w=jnp.stack([lg[i]["conv_w"] for i in range(depth)]),
        loss=loss_part[:, :1],
    )
    order = ("norm_mix", "norm_ffn", "gla_b_gate", "gla_norm", "fox_b_f", "fox_q_norm", "fox_k_norm", "ffn_conv_b",
             "norm_final", "gla_w_gate", "ffn_conv_w", "loss")
    packed = _pack([small_parts[nm] for nm in order])
    dmod = jnp.stack([jnp.concatenate([lg[i][nm] for nm in ("sh_m", "sc_m", "g_m", "sh_f", "sc_f", "g_f")], axis=1)
                      for i in range(depth)])
    hs_small, tok_small = _split_start([packed[None], dmod[None]], _all_plan, "gather_small_grads_start",
                                       land_blocks=N_DEV)
    dw_main, da_tail, h1_gla = lg[0]["gla_w_in_unsent"]
    dw_in_t = _tail_rows(da_tail + tok_small.astype(BF16), h1_gla, dw_main, rank, "gla_in_tail_dw")
    send("gla_in", [dw_in_t.reshape(N_DEV, -1, d)])
    started = pending[-1][1][0][0]

    received = {}

    def arrive(key, after):
        sums, lands = _split_wait(sent[key], _chip_plan, after, f"scatter_{key}_chip_wait")
        received[key] = list(zip(lands, sums))

    for key in ("ffn1", "fox", "ffn0", "gla_out"):
        arrive(key, started)

    out_g, out_d, out_m, out_v = {}, {}, {}, {}

    chip_idx = chip.astype(jnp.int32).reshape(1)

    def update(nm, g_arr, transposed=False):
        swap = (lambda t: jnp.transpose(t, (0, 2, 1))) if transposed else (lambda t: t)
        if isinstance(g_arr, list):
            res = _adamw_pieces(swap(w[nm]), [t[0] for t in g_arr], [t[1] for t in g_arr], chip_idx,
                                swap(mom_m[nm]), swap(mom_v[nm]), f"adamw_{nm}")
        else:
            res = _adamw(w[nm], g_arr, mom_m[nm], mom_v[nm], f"adamw_{nm}")
        out_g[nm], out_d[nm], out_m[nm], out_v[nm] = (swap(t) for t in res)

    update("gla_w_out", [received["gla_out"][0]])
    update("fox_w_out", [received["fox"][1]])
    tok_flush = flush(out_g["fox_w_out"])
    update("ffn_w_up", [received[f"ffn{i}"][0] for i in range(depth)])
    update("fox_w_in", [received["fox"][0]], transposed=True)
    update("ffn_w_down", [received[f"ffn{i}"][1] for i in range(depth)])

    updated = ("gla_w_out", "fox_w_in", "fox_w_out", "ffn_w_up", "ffn_w_down")
    (packed_mine, dmod_mine), (packed_all, dmod_all) = _split_wait(
        hs_small, _all_plan, [out_d[nm] for nm in updated], "gather_small_grads_wait")
    packed_all = lax.dynamic_update_slice(packed_all, packed_mine + tok_flush, (me, 0, 0))
    dmod_all = lax.dynamic_update_slice(dmod_all, dmod_mine, (me, 0, 0, 0))
    summed = _unpack(_sum8(packed_all, "sum_small_grads"), [small_parts[nm].shape for nm in order])
    small_g = dict(zip(order, summed))
    loss = small_g["loss"][0, 0]
    dmod_all = dmod_all[:, :, 0, :]
    grads = {}
    cond_t = _pad_cols(jnp.transpose(cond_all)).astype(BF16)
    dmod_cols = lax.dynamic_slice(dmod_all, (0, 0, me * mod_cols), (N_DEV, depth, mod_cols))
    g_w_mod = lax.empty(w_mod.shape, F32)
    for i in range(depth):
        rhs = jnp.pad(dmod_cols[:, i], ((0, LANE - N_DEV), (0, 0)))
        g_w_mod = _matmul(cond_t, rhs, name=f"mod_dw_{i}", tn=768, into=(g_w_mod, i))
    grads["w_mod"] = g_w_mod
    small_g["b_mod"] = _sum8(dmod_all.reshape(N_DEV, 1, -1), "sum_b_mod").reshape(depth, -1)
    update("w_mod", grads["w_mod"])

    gate_cols = gla_w_gate.shape[2]
    conv_cols = ffn_conv_w.shape[2]
    local_small = dict(small_g)
    local_small["gla_w_gate"] = lax.dynamic_slice_in_dim(small_g["gla_w_gate"], me * gate_cols, gate_cols, axis=2)
    local_small["ffn_conv_w"] = lax.dynamic_slice_in_dim(small_g["ffn_conv_w"], me * conv_cols, conv_cols, axis=2)
    names = SMALL + SMALL_SHARDED
    shapes = [w[nm].shape for nm in names]
    res = _adamw(_pack([w[nm] for nm in names])[None], _pack([local_small[nm] for nm in names])[None],
                 _pack([mom_m[nm] for nm in names])[None], _pack([mom_v[nm] for nm in names])[None], "adamw_small")
    for tgt, flat in zip((out_g, out_d, out_m, out_v), res):
        for nm, arr in zip(names, _unpack(flat[0], shapes)):
            tgt[nm] = arr

    arrive("gla_in", [out_d[nm] for nm in updated + ("w_mod",)])
    update("gla_w_in", [received["gla_in"][0]], transposed=True)

    return (loss, grad_x, *[out_g[n] for n in WEIGHTS], *[out_d[n] for n in WEIGHTS],
            *[out_m[n] for n in WEIGHTS], *[out_v[n] for n in WEIGHTS])
```

```python
import jax
import jax.numpy as jnp
from jax import lax
from jax.experimental import pallas as pl
from jax.experimental.pallas import tpu as pltpu

F32, BF16 = jnp.float32, jnp.bfloat16
N_DEV = 8
GLA_HEADS = 4
GLA_TAU = 16.0
GLA_CHUNK = 64
NORM_EPS = 1e-6
ADAM_LR, ADAM_B1, ADAM_B2, ADAM_EPS, ADAM_WD, ADAM_STEP = 0.001, 0.9, 0.999, 1e-08, 0.01, 10
LANE = 128
VMEM_LIMIT = 56 * 1024 * 1024
NEG = -1e30


def _pcall(body, **kw):
    return pl.pallas_call(body, **kw)


def _params(n_axes):
    return pltpu.CompilerParams(dimension_semantics=("arbitrary",) * n_axes, vmem_limit_bytes=VMEM_LIMIT)


def _tile(dim, pref):
    if dim <= pref:
        return dim
    t = pref
    while dim % t:
        t -= LANE
    assert t > 0, (dim, pref)
    return t


def _dot(a, b, ta=False, tb=False):
    dims = (((0,) if ta else (1,), (1,) if tb else (0,)), ((), ()))
    return lax.dot_general(a.astype(BF16), b.astype(BF16), dims, preferred_element_type=F32)


def _split3(x):
    hi = x.astype(BF16)
    r1 = x - hi.astype(F32)
    mid = r1.astype(BF16)
    lo = (r1 - mid.astype(F32)).astype(BF16)
    return hi, mid, lo


def _tri_matmul(tri, x):
    hi, mid, lo = _split3(x)
    return _dot(tri, hi) + _dot(tri, mid) + _dot(tri, lo)


def _tri(n, upper=False):
    r = lax.broadcasted_iota(jnp.int32, (n, n), 0)
    c = lax.broadcasted_iota(jnp.int32, (n, n), 1)
    return jnp.where((r <= c) if upper else (r >= c), 1.0, 0.0).astype(BF16)


def _log_sigmoid(x):
    return jnp.minimum(x, 0.0) - jnp.log(1.0 + jnp.exp(-jnp.abs(x)))


def _sigmoid(x):
    return 1.0 / (1.0 + jnp.exp(-x))


def _silu(x):
    return x * _sigmoid(x)


def _dsilu(x):
    s = _sigmoid(x)
    return s * (1.0 + x * (1.0 - s))


def _matmul(a, b, *, name, ta=False, tb=False, out_dtypes=(F32,), tm=1024, tn=1024, tk=2048,
            epilogue=None, extras=(), a_halves=False, b_halves=False, b_shards=False, out_shards=False,
            b_rows=None, out_rows=None, b_layer=None, into=None):
    if a_halves:
        assert not ta
        m, k = a.shape[1], 2 * a.shape[2]
    else:
        m, k = (a.shape[1], a.shape[0]) if ta else a.shape
    if b_halves:
        assert not tb and b.shape[1] == k
        n = 2 * b.shape[2]
    elif b_shards:
        n = b.shape[1] if tb else N_DEV * b.shape[2]
        assert (N_DEV * b.shape[2] if tb else b.shape[1]) == k, (a.shape, b.shape, ta, tb)
    elif b_layer is not None:
        assert not tb and b.shape[1] == k
        n = b.shape[2]
    else:
        rows = b.shape[0] if b_rows is None else b_rows
        n = rows if tb else b.shape[1]
        assert (b.shape[1] if tb else rows) == k, (a.shape, b.shape, ta, tb)
    n_unit = n // N_DEV if (out_shards or (b_shards and not tb)) else (n // 2 if b_halves else n)
    k_unit = k // N_DEV if (b_shards and tb) else (k // 2 if a_halves else k)
    tm, tn, tk = _tile(m, tm), _tile(n_unit, tn), _tile(k_unit, tk)
    nk = k // tk
    if a_halves:
        a_spec = pl.BlockSpec((None, tm, tk), lambda i, j, kk: (kk // (nk // 2), i, kk % (nk // 2)))
    elif ta:
        a_spec = pl.BlockSpec((tk, tm), lambda i, j, kk: (kk, i))
    else:
        a_spec = pl.BlockSpec((tm, tk), lambda i, j, kk: (i, kk))
    n_per, k_per = n // tn // N_DEV, nk // N_DEV
    if b_halves:
        b_spec = pl.BlockSpec((None, tk, tn), lambda i, j, kk: (j // (n // tn // 2), kk, j % (n // tn // 2)))
    elif b_shards and tb:
        b_spec = pl.BlockSpec((None, tn, tk), lambda i, j, kk: (kk // k_per, j, kk % k_per))
    elif b_shards:
        b_spec = pl.BlockSpec((None, tk, tn), lambda i, j, kk: (j // n_per, kk, j % n_per))
    elif b_layer is not None:
        b_spec = pl.BlockSpec((None, tk, tn), lambda i, j, kk: (b_layer, kk, j))
    elif tb:
        b_spec = pl.BlockSpec((tn, tk), lambda i, j, kk: (j, kk))
    else:
        b_spec = pl.BlockSpec((tk, tn), lambda i, j, kk: (kk, j))
    ex_specs = []
    for kind, arr in extras:
        if kind == "mn":
            assert arr.shape == (m, n), (arr.shape, m, n)
            ex_specs.append(pl.BlockSpec((tm, tn), lambda i, j, kk: (i, j)))
        else:
            assert arr.shape == (1, n), (arr.shape, n)
            ex_specs.append(pl.BlockSpec((1, tn), lambda i, j, kk: (0, j)))
    n_ex, n_out = len(extras), len(out_dtypes)

    def body(a_ref, b_ref, *rest):
        ex, outs, acc = rest[:n_ex], rest[-1 - n_out:-1], rest[-1]
        kk = pl.program_id(2)

        @pl.when(kk == 0)
        def _():
            acc[...] = jnp.zeros_like(acc)

        acc[...] += _dot(a_ref[...], b_ref[...], ta, tb)

        @pl.when(kk == nk - 1)
        def _():
            if epilogue is None:
                vals = (acc[...],)
            else:
                vals = epilogue(acc[...], *[e[...] for e in ex])
            for o, v in zip(outs, vals):
                o[...] = v.astype(o.dtype)

    if out_shards:
        out_spec = pl.BlockSpec((None, tm, tn), lambda i, j, kk: (j // n_per, i, j % n_per))
        out_dims = (N_DEV, m, n // N_DEV)
    elif into is not None:
        out_spec = pl.BlockSpec((None, tm, tn), lambda i, j, kk: (into[1], i, j))
        out_dims = into[0].shape
    else:
        out_spec = pl.BlockSpec((tm, tn), lambda i, j, kk: (i, j))
        out_dims = (m if out_rows is None else out_rows, n)
    operands = [a, b, *[arr for _, arr in extras]]
    aliases = {}
    if into is not None:
        assert n_out == 1 and into[0].shape[1:] == (m, n) and into[0].dtype == out_dtypes[0]
        aliases = {len(operands): 0}
        operands.append(into[0])
    res = _pcall(
        body, name=name, grid=(m // tm, n // tn, nk),
        in_specs=[a_spec, b_spec] + ex_specs + [pl.BlockSpec(memory_space=pl.ANY)] * len(aliases),
        out_specs=[out_spec] * n_out,
        out_shape=[jax.ShapeDtypeStruct(out_dims, d) for d in out_dtypes],
        scratch_shapes=[pltpu.VMEM((tm, tn), F32)],
        input_output_aliases=aliases,
        compiler_params=_params(3),
    )(*operands)
    return res[0] if n_out == 1 else res


def _tail_rows(a, b, into, rows, name, tn=1024):
    k, n = b.shape
    m_total = into.shape[0]
    tn = _tile(n, tn)

    def body(a_ref, b_ref, into_ref, out_ref):
        out_ref[...] = _dot(a_ref[...], b_ref[...], ta=True)[:rows].astype(out_ref.dtype)

    return _pcall(
        body, name=name, grid=(n // tn,),
        in_specs=[pl.BlockSpec((k, a.shape[1]), lambda j: (0, 0)), pl.BlockSpec((k, tn), lambda j: (0, j)),
                  pl.BlockSpec(memory_space=pl.ANY)],
        out_specs=pl.BlockSpec((rows, tn), lambda j: (m_total // rows - 1, j)),
        out_shape=jax.ShapeDtypeStruct(into.shape, into.dtype),
        input_output_aliases={2: 0}, compiler_params=_params(1),
    )(a, b, into)


def _rowwise(fn, ins, outs, *, name, tr=128, into=None):
    rows = next(e[1].shape[0] for e in ins if e[0] != "full")
    tr = _tile(rows, tr)
    in_specs = []
    for entry in ins:
        kind, arr = entry[0], entry[1]
        assert kind == "full" or (arr.shape[0] == rows and arr.ndim == 2)
        if kind == "row":
            in_specs.append(pl.BlockSpec((tr, arr.shape[1]), lambda i: (i, 0)))
        elif kind == "cols":
            in_specs.append(pl.BlockSpec((tr, entry[3]), lambda i, cb=entry[2]: (i, cb)))
        else:
            in_specs.append(pl.BlockSpec(arr.shape, lambda i, nd=arr.ndim: (0,) * nd))
    out_specs, out_shape = [], []
    for entry in outs:
        kind, w, dt = entry[:3]
        if kind == "row":
            out_specs.append(pl.BlockSpec((tr, w), lambda i: (i, 0)))
            out_shape.append(jax.ShapeDtypeStruct((rows, w), dt))
        elif kind == "band":
            out_specs.append(pl.BlockSpec((tr, w), lambda i, cb=entry[3]: (i, cb)))
            out_shape.append(jax.ShapeDtypeStruct((rows, entry[4]), dt))
        else:
            out_specs.append(pl.BlockSpec((1, w), lambda i: (0, 0)))
            out_shape.append(jax.ShapeDtypeStruct((1, w), dt))
    n_in = len(ins)
    operands = [e[1] for e in ins]
    aliases = {}
    if into is not None:
        aliases = {len(operands): into[1]}
        in_specs.append(pl.BlockSpec(memory_space=pl.ANY))
        operands.append(into[0])

    def body(*refs):
        i = pl.program_id(0)
        vals = fn(*[r[...] for r in refs[:n_in]])
        for entry, o, v in zip(outs, refs[len(operands):], vals):
            if entry[0] == "acc":
                @pl.when(i == 0)
                def _(o=o):
                    o[...] = jnp.zeros_like(o)

                o[...] += v.astype(o.dtype)
            else:
                o[...] = v.astype(o.dtype)

    return _pcall(body, name=name, grid=(rows // tr,), in_specs=in_specs, out_specs=out_specs,
                  out_shape=out_shape, input_output_aliases=aliases, compiler_params=_params(1))(*operands)


def _colsum(x):
    return jnp.sum(x, axis=0, keepdims=True)


def _norm_stats(x):
    rstd = lax.rsqrt(jnp.mean(x * x, axis=-1, keepdims=True) + NORM_EPS)
    return x * rstd, rstd


def _norm_bwd(dxhat, xhat, rstd):
    return rstd * (dxhat - xhat * jnp.mean(dxhat * xhat, axis=-1, keepdims=True))


def _adaln_fwd(x, gain, sc, sh, name):
    def fn(x, gain, sc, sh):
        xhat, _ = _norm_stats(x)
        return ((xhat * gain) * (1.0 + sc) + sh,)

    return _rowwise(fn, [("row", x), ("full", gain), ("full", sc), ("full", sh)],
                    [("row", x.shape[1], BF16)], name=name)[0]


def _adaln_bwd(x, dh, dres, gain, sc, name, branch=None):
    d = x.shape[1]

    def fn(x, dh, dres, gain, sc, *br):
        xhat, rstd = _norm_stats(x)
        dxhat = dh * (gain * (1.0 + sc))
        dx = dres + _norm_bwd(dxhat, xhat, rstd)
        return (dx, _colsum(dh), _colsum(dh * (xhat * gain)), _colsum(dh * xhat * (1.0 + sc))) + _branch_bwd(dx, *br)

    return _rowwise(fn, [("row", x), ("row", dh), ("row", dres), ("full", gain), ("full", sc)] + _branch_ins(branch),
                    [("row", d, F32), ("acc", d, F32), ("acc", d, F32), ("acc", d, F32)] + _branch_outs(branch, d),
                    name=name)


def _branch_ins(branch):
    return [] if branch is None else [("row", branch[0]), ("full", branch[1])]


def _branch_outs(branch, d):
    return [] if branch is None else [("row", d, BF16), ("acc", d, F32)]


def _branch_bwd(dx, *branch):
    if not branch:
        return ()
    y, g = branch
    return dx * (1.0 + g), _colsum(dx * y)


def _final_loss(x, target, gain, name, branch):
    d = x.shape[1]

    def fn(x, t, gain, *br):
        xhat, rstd = _norm_stats(x)
        err = xhat * gain - t
        dy = err * (1.0 / d)
        loss = 0.5 * jnp.sum(jnp.mean(err * err, axis=-1, keepdims=True), axis=0, keepdims=True)
        dx = _norm_bwd(dy * gain, xhat, rstd)
        return (dx, _colsum(dy * xhat), jnp.broadcast_to(loss, (1, LANE))) + _branch_bwd(dx, *br)

    return _rowwise(fn, [("row", x), ("row", target), ("full", gain)] + _branch_ins(branch),
                    [("row", d, F32), ("acc", d, F32), ("acc", LANE, F32)] + _branch_outs(branch, d), name=name)


def _gla_gates(q, k, a, wg, bg, scale, c):
    ga = _dot(a, wg) + bg
    la = _log_sigmoid(ga) * (1.0 / GLA_TAU)
    b = _tri_matmul(_tri(c), la)
    bl = _colsum(la)
    eb, enb, eend = jnp.exp(b), jnp.exp(-b), jnp.exp(bl - b)
    q = q * scale
    return dict(ga=ga, eb=eb, enb=enb, eend=eend, dec=jnp.exp(bl), q_dec=q * eb, k_inv=k * enb, k_end=k * eend)


def _causal(c):
    return lax.broadcasted_iota(jnp.int32, (c, c), 0) >= lax.broadcasted_iota(jnp.int32, (c, c), 1)


def _gla_specs(heads, c, dk, dv, chunk):
    return [
        pl.BlockSpec((c, heads * dk), lambda n: (chunk(n), 0)),
        pl.BlockSpec((c, heads * dk), lambda n: (chunk(n), 1)),
        pl.BlockSpec((c, heads * dv), lambda n: (chunk(n), 1)),
        pl.BlockSpec((c, LANE), lambda n: (chunk(n), 0)),
        pl.BlockSpec((LANE, heads * dk), lambda n: (0, 0)),
        pl.BlockSpec((1, heads * dk), lambda n: (0, 0)),
    ]


def _gla_fwd(proj, a_tail, wg_p, bg, name):
    s = proj.shape[0]
    heads, c = GLA_HEADS, GLA_CHUNK
    dk = wg_p.shape[1] // heads
    dv = 2 * dk
    n_chunks = s // c
    scale = dk ** -0.5

    def body(q_ref, k_ref, v_ref, a_ref, wg_ref, bg_ref, o_ref, st_ref, state):
        @pl.when(pl.program_id(0) == 0)
        def _():
            state[...] = jnp.zeros_like(state)

        a = a_ref[...]
        for h in range(heads):
            sk, sv = slice(h * dk, (h + 1) * dk), slice(h * dv, (h + 1) * dv)
            g = _gla_gates(q_ref[:, sk], k_ref[:, sk], a, wg_ref[:, sk], bg_ref[:, sk], scale, c)
            v = v_ref[:, sv]
            st = state[h]
            attn = jnp.where(_causal(c), _dot(g["q_dec"], g["k_inv"], tb=True), 0.0)
            o_ref[:, sv] = _dot(attn, v) + _dot(g["q_dec"], st, tb=True)
            st_ref[h] = st.astype(st_ref.dtype)
            state[h] = g["dec"] * st + _dot(v, g["k_end"], ta=True)

    return _pcall(
        body, name=name, grid=(n_chunks,),
        in_specs=_gla_specs(heads, c, dk, dv, lambda n: n),
        out_specs=[pl.BlockSpec((c, heads * dv), lambda n: (n, 0)),
                   pl.BlockSpec((heads, None, dv, dk), lambda n: (0, n, 0, 0))],
        out_shape=[jax.ShapeDtypeStruct((s, heads * dv), F32),
                   jax.ShapeDtypeStruct((heads, n_chunks, dv, dk), BF16)],
        scratch_shapes=[pltpu.VMEM((heads, dv, dk), F32)],
        compiler_params=_params(1),
    )(proj, proj, proj, a_tail, wg_p, bg)


def _gla_bwd(proj, a_tail, wg_p, bg, states, d_o, dproj, name):
    s = proj.shape[0]
    heads, c = GLA_HEADS, GLA_CHUNK
    dk = wg_p.shape[1] // heads
    dv = 2 * dk
    n_chunks = s // c
    scale = dk ** -0.5
    k0, v0 = heads * dk, 2 * heads * dk

    def body(q_ref, k_ref, v_ref, a_ref, wg_ref, bg_ref, st_ref, do_ref, dproj_in, dqkv_ref, dga_ref, dstate):
        @pl.when(pl.program_id(0) == 0)
        def _():
            dstate[...] = jnp.zeros_like(dstate)

        a = a_ref[...]
        mask = _causal(c)
        for h in range(heads):
            sk, sv = slice(h * dk, (h + 1) * dk), slice(h * dv, (h + 1) * dv)
            out_k, out_v = slice(k0 + h * dk, k0 + (h + 1) * dk), slice(v0 + h * dv, v0 + (h + 1) * dv)
            g = _gla_gates(q_ref[:, sk], k_ref[:, sk], a, wg_ref[:, sk], bg_ref[:, sk], scale, c)
            v, st, dst, d_out = v_ref[:, sv], st_ref[h], dstate[h], do_ref[:, sv]
            q_dec, k_inv, k_end = g["q_dec"], g["k_inv"], g["k_end"]
            attn = jnp.where(mask, _dot(q_dec, k_inv, tb=True), 0.0)
            d_attn = jnp.where(mask, _dot(d_out, v, tb=True), 0.0)
            d_qdec = _dot(d_attn, k_inv) + _dot(d_out, st)
            d_kinv = _dot(d_attn, q_dec, ta=True)
            d_kend = _dot(v, dst)
            dqkv_ref[:, out_v] = (_dot(attn, d_out, ta=True) + _dot(k_end, dst, tb=True)).astype(dqkv_ref.dtype)
            d_dec = jnp.sum(dst * st.astype(F32), axis=0, keepdims=True)
            dstate[h] = g["dec"] * dst + _dot(d_out, q_dec, ta=True)

            dqkv_ref[:, sk] = (d_qdec * (scale * g["eb"])).astype(dqkv_ref.dtype)
            dqkv_ref[:, out_k] = (d_kinv * g["enb"] + d_kend * g["eend"]).astype(dqkv_ref.dtype)
            kk = d_kend * k_end
            db = d_qdec * q_dec - d_kinv * k_inv - kk
            dbl = jnp.sum(kk, axis=0, keepdims=True) + d_dec * g["dec"]
            last = lax.broadcasted_iota(jnp.int32, db.shape, 0) == c - 1
            db = db + jnp.where(last, dbl, 0.0)
            dla = _tri_matmul(_tri(c, upper=True), db)
            dga_ref[:, sk] = dla * (1.0 / GLA_TAU) * _sigmoid(-g["ga"])

    chunk = lambda n: n_chunks - 1 - n
    rev = lambda n: (chunk(n), 0)
    return _pcall(
        body, name=name, grid=(n_chunks,),
        in_specs=_gla_specs(heads, c, dk, dv, chunk) + [
            pl.BlockSpec((heads, None, dv, dk), lambda n: (0, chunk(n), 0, 0)),
            pl.BlockSpec((c, heads * dv), rev), pl.BlockSpec(memory_space=pl.ANY)],
        out_specs=[pl.BlockSpec((c, v0 + heads * dv), rev), pl.BlockSpec((c, heads * dk), rev)],
        out_shape=[jax.ShapeDtypeStruct(dproj.shape, dproj.dtype), jax.ShapeDtypeStruct((s, heads * dk), F32)],
        scratch_shapes=[pltpu.VMEM((heads, dv, dk), F32)],
        input_output_aliases={8: 0},
        compiler_params=_params(1),
    )(proj, proj, proj, a_tail, wg_p, bg, states, d_o, dproj)


def _gla_post_fwd(o, r, gn, name):
    dvt = o.shape[1]
    dv = dvt // GLA_HEADS

    def fn(o, r, gn):
        outs = []
        for h in range(GLA_HEADS):
            sl = slice(h * dv, (h + 1) * dv)
            ohat, _ = _norm_stats(o[:, sl])
            outs.append((ohat * gn[:, sl]) * _silu(r[:, sl]))
        return (jnp.concatenate(outs, axis=1),)

    return _rowwise(fn, [("row", o), r, ("full", gn)], [("row", dvt, BF16)], name=name)[0]


def _gla_post_bwd(o, r, gn, dog, name):
    dvt = o.shape[1]
    dv = dvt // GLA_HEADS

    def fn(o, r, gn, dog):
        d_o, d_r, d_g = [], [], []
        for h in range(GLA_HEADS):
            sl = slice(h * dv, (h + 1) * dv)
            ohat, rstd = _norm_stats(o[:, sl])
            g, rr, dd = gn[:, sl], r[:, sl], dog[:, sl]
            d_r.append(dd * (ohat * g) * _dsilu(rr))
            don = dd * _silu(rr)
            d_g.append(_colsum(don * ohat))
            d_o.append(_norm_bwd(don * g, ohat, rstd))
        return jnp.concatenate(d_o, axis=1), jnp.concatenate(d_r, axis=1), jnp.concatenate(d_g, axis=1)

    return _rowwise(fn, [("row", o), r, ("full", gn), ("row", dog)],
                    [("row", dvt, F32), ("band", dvt, BF16, 2, 3 * dvt), ("acc", dvt, F32)], name=name)


def _fox_prep(q, k, v, qg, kg, d, hd, name):
    heads = d // hd
    scale = hd ** -0.5

    def fn(q, k, v, qg, kg):
        qs, ks = [], []
        for h in range(heads):
            sl = slice(h * hd, (h + 1) * hd)
            qs.append(_norm_stats(q[:, sl])[0] * qg * scale)
            ks.append(_norm_stats(k[:, sl])[0] * kg)
        return jnp.concatenate(qs, axis=1), jnp.concatenate(ks, axis=1), v

    return _rowwise(fn, [q, k, v, ("full", qg), ("full", kg)],
                    [("row", d, BF16)] * 3, name=name)


def _fox_prep_bwd(q, k, dqn, dkn, qg, kg, hd, dproj, name):
    d = dqn.shape[1]
    heads = d // hd
    scale = hd ** -0.5

    def fn(q, k, dqn, dkn, qg, kg):
        dq, dk, gq, gk = [], [], [], []
        for h in range(heads):
            sl = slice(h * hd, (h + 1) * hd)
            for x, dxn, g, s, dl, gl in ((q, dqn, qg, scale, dq, gq), (k, dkn, kg, 1.0, dk, gk)):
                xhat, rstd = _norm_stats(x[:, sl])
                dn = dxn[:, sl] * s
                gl.append(_colsum(dn * xhat))
                dl.append(_norm_bwd(dn * g, xhat, rstd))
        cat = lambda t: jnp.concatenate(t, axis=1)
        return cat(dq + dk), cat(gq), cat(gk)

    return _rowwise(fn, [q, k, ("row", dqn), ("row", dkn), ("full", qg), ("full", kg)],
                    [("band", 2 * d, BF16, 0, 4 * d), ("acc", d, F32), ("acc", d, F32)], name=name, into=(dproj, 0))


def _fox_cum(fl, bf_p, name, tb=256):
    s = fl.shape[0]
    tb = _tile(s, tb)

    def body(fl_ref, bf_ref, cum_ref, carry):
        @pl.when(pl.program_id(0) == 0)
        def _():
            carry[...] = jnp.zeros_like(carry)

        lf = _log_sigmoid(fl_ref[...] + bf_ref[...])
        cum_ref[...] = _tri_matmul(_tri(tb), lf) + carry[...]
        carry[...] += _colsum(lf)

    return _pcall(
        body, name=name, grid=(s // tb,),
        in_specs=[pl.BlockSpec((tb, LANE), lambda i: (i, 0)), pl.BlockSpec((1, LANE), lambda i: (0, 0))],
        out_specs=pl.BlockSpec((tb, LANE), lambda i: (i, 0)),
        out_shape=jax.ShapeDtypeStruct((s, LANE), F32),
        scratch_shapes=[pltpu.VMEM((1, LANE), F32)],
        compiler_params=_params(1),
    )(fl, bf_p)


def _fox_cum_bwd(dcum, fl, bf_p, name, tb=256):
    s = fl.shape[0]
    tb = _tile(s, tb)
    nb = s // tb

    def body(dc_ref, fl_ref, bf_ref, dfl_ref, dbf_ref, carry):
        @pl.when(pl.program_id(0) == 0)
        def _():
            carry[...] = jnp.zeros_like(carry)
            dbf_ref[...] = jnp.zeros_like(dbf_ref)

        dc = dc_ref[...]
        dlf = _tri_matmul(_tri(tb, upper=True), dc) + carry[...]
        carry[...] += _colsum(dc)
        dfl = dlf * _sigmoid(-(fl_ref[...] + bf_ref[...]))
        dfl_ref[...] = dfl
        dbf_ref[...] += _colsum(dfl)

    rev = lambda i: (nb - 1 - i, 0)
    return _pcall(
        body, name=name, grid=(nb,),
        in_specs=[pl.BlockSpec((tb, LANE), rev), pl.BlockSpec((tb, LANE), rev), pl.BlockSpec((1, LANE), lambda i: (0, 0))],
        out_specs=[pl.BlockSpec((tb, LANE), rev), pl.BlockSpec((1, LANE), lambda i: (0, 0))],
        out_shape=[jax.ShapeDtypeStruct((s, LANE), F32), jax.ShapeDtypeStruct((1, LANE), F32)],
        scratch_shapes=[pltpu.VMEM((1, LANE), F32)],
        compiler_params=_params(1),
    )(dcum, fl, bf_p)


def _fox_attn_fwd(qn, kn, vb, cum_col, cum_row, hd, t, name):
    s, d = qn.shape
    heads = d // hd
    nq = s // t

    def body(q_ref, k_ref, v_ref, cc_ref, cr_ref, o_ref, lse_ref):
        qi = pl.program_id(1)
        q = q_ref[...]
        cq = cc_ref[...]
        qpos = qi * t + lax.broadcasted_iota(jnp.int32, (t, 1), 0)

        def step(kj, carry, diagonal=False):
            m, l, acc = carry
            off = pl.multiple_of(kj * t, t)
            ks, vs = k_ref[pl.ds(off, t), :], v_ref[pl.ds(off, t), :]
            sc = _dot(q, ks, tb=True) + cq - cr_ref[kj]
            if diagonal:
                kpos = off + lax.broadcasted_iota(jnp.int32, (1, t), 1)
                sc = jnp.where(kpos <= qpos, sc, NEG)
            m_new = jnp.maximum(m, jnp.max(sc, axis=1, keepdims=True))
            alpha = jnp.exp(m - m_new)
            p = jnp.exp(sc - m_new)
            return m_new, alpha * l + jnp.sum(p, axis=1, keepdims=True), alpha * acc + _dot(p, vs)

        init = (jnp.full((t, 1), NEG, F32), jnp.zeros((t, 1), F32), jnp.zeros((t, hd), F32))
        m, l, acc = step(qi, lax.fori_loop(0, qi, step, init), diagonal=True)
        o_ref[...] = acc / l
        lse_ref[...] = m + jnp.log(l)

    return _pcall(
        body, name=name, grid=(heads, nq),
        in_specs=[pl.BlockSpec((t, hd), lambda h, i: (i, h)),
                  pl.BlockSpec((s, hd), lambda h, i: (0, h)),
                  pl.BlockSpec((s, hd), lambda h, i: (0, h)),
                  pl.BlockSpec((None, t, 1), lambda h, i: (h, i, 0)),
                  pl.BlockSpec((None, nq, 1, t), lambda h, i: (h, 0, 0, 0))],
        out_specs=[pl.BlockSpec((t, hd), lambda h, i: (i, h)), pl.BlockSpec((None, t, 1), lambda h, i: (h, i, 0))],
        out_shape=[jax.ShapeDtypeStruct((s, d), F32), jax.ShapeDtypeStruct((heads, s, 1), F32)],
        compiler_params=_params(2),
    )(qn, kn, vb, cum_col, cum_row)


def _fox_attn_bwd(qn, kn, vb, d_o, o, lse, cum_col, cum_row, hd, t, dproj, name):
    s, d = qn.shape
    heads = d // hd
    nq = s // t

    def body(q_ref, k_ref, v_ref, do_ref, o_ref, lse_ref, cc_ref, cr_ref, dproj_in,
             dq_ref, dk_ref, dv_ref, dcq_ref, dck_ref, delta):
        kj = pl.program_id(1)

        @pl.when(kj == 0)
        def _():
            dq_ref[...] = jnp.zeros_like(dq_ref)
            dcq_ref[...] = jnp.zeros_like(dcq_ref)
            delta[...] = jnp.sum(do_ref[...] * o_ref[...], axis=1, keepdims=True)

        ks, vs, cr = k_ref[...], v_ref[...], cr_ref[...]
        kpos = kj * t + lax.broadcasted_iota(jnp.int32, (1, t), 1)

        def step(qi, carry, diagonal=False):
            dk, dv, dck = carry
            rows = pl.ds(pl.multiple_of(qi * t, t), t)
            q, d_out = q_ref[rows, :], do_ref[rows, :]
            sc = _dot(q, ks, tb=True) + cc_ref[rows, :] - cr
            p = jnp.exp(sc - lse_ref[rows, :])
            if diagonal:
                qpos = qi * t + lax.broadcasted_iota(jnp.int32, (t, 1), 0)
                p = jnp.where(kpos <= qpos, p, 0.0)
            ds = p * (_dot(d_out, vs, tb=True) - delta[rows, :])
            dq_ref[rows, :] += _dot(ds, ks)
            dcq_ref[rows, :] += jnp.sum(ds, axis=1, keepdims=True)
            return dk + _dot(ds, q, ta=True), dv + _dot(p, d_out, ta=True), dck + _colsum(ds)

        init = (jnp.zeros((t, hd), F32), jnp.zeros((t, hd), F32), jnp.zeros((1, t), F32))
        dk, dv, dck = lax.fori_loop(kj + 1, nq, step, step(kj, init, diagonal=True))
        dk_ref[...] = dk.astype(dk_ref.dtype)
        dv_ref[...] = dv.astype(dv_ref.dtype)
        dck_ref[...] = dck

    head_rows = lambda h, j: (0, h)
    blk = lambda h, j: (j, h)
    return _pcall(
        body, name=name, grid=(heads, nq),
        in_specs=[pl.BlockSpec((s, hd), head_rows), pl.BlockSpec((t, hd), blk), pl.BlockSpec((t, hd), blk),
                  pl.BlockSpec((s, hd), head_rows), pl.BlockSpec((s, hd), head_rows),
                  pl.BlockSpec((None, s, 1), lambda h, j: (h, 0, 0)),
                  pl.BlockSpec((None, s, 1), lambda h, j: (h, 0, 0)),
                  pl.BlockSpec((None, None, 1, t), lambda h, j: (h, j, 0, 0)),
                  pl.BlockSpec(memory_space=pl.ANY)],
        out_specs=[pl.BlockSpec((s, hd), head_rows), pl.BlockSpec((t, hd), blk),
                   pl.BlockSpec((t, hd), lambda h, j: (j, 2 * heads + h)),
                   pl.BlockSpec((None, s, 1), lambda h, j: (h, 0, 0)),
                   pl.BlockSpec((None, None, 1, t), lambda h, j: (h, j, 0, 0))],
        out_shape=[jax.ShapeDtypeStruct((s, d), F32), jax.ShapeDtypeStruct((s, d), BF16),
                   jax.ShapeDtypeStruct(dproj.shape, dproj.dtype), jax.ShapeDtypeStruct((heads, s, 1), F32),
                   jax.ShapeDtypeStruct((heads, nq, 1, t), F32)],
        scratch_shapes=[pltpu.VMEM((s, 1), F32)],
        input_output_aliases={8: 2},
        compiler_params=_params(2),
    )(qn, kn, vb, d_o, o, lse, cum_col, cum_row, dproj)


def _fox_gate_fwd(o, og, name):
    def fn(o, og):
        return (o * _sigmoid(og),)

    return _rowwise(fn, [("row", o), og], [("row", o.shape[1], BF16)], name=name)[0]


def _fox_gate_bwd(o, og, dact, name):
    def fn(o, og, dact):
        sg = _sigmoid(og)
        return dact * sg, dact * o * sg * (1.0 - sg)

    d = o.shape[1]
    return _rowwise(fn, [("row", o), og, ("row", dact)], [("row", d, F32), ("band", d, BF16, 3, 4 * d)], name=name)


def _shift_down(x, n):
    rows = lax.broadcasted_iota(jnp.int32, x.shape, 0)
    return jnp.where(rows >= n, pltpu.roll(x, n, 0), 0.0)


def _shift_up(x, n):
    rows = lax.broadcasted_iota(jnp.int32, x.shape, 0)
    return jnp.where(rows < x.shape[0] - n, pltpu.roll(x, x.shape[0] - n, 0), 0.0)


def _conv(u, w_ref, b):
    return w_ref[0:1, :] * _shift_down(u, 2) + w_ref[1:2, :] * _shift_down(u, 1) + w_ref[2:3, :] * u + b


def _conv_act_fwd(u, cw, cb, name, tc=256):
    s, two_f = u.shape
    dff = two_f // 2
    tc = _tile(dff, tc)
    nb = dff // tc

    def body(ug_ref, uv_ref, wg_ref, wv_ref, bg_ref, bv_ref, a_ref):
        gate = _conv(ug_ref[...], wg_ref, bg_ref[...])
        val = _conv(uv_ref[...], wv_ref, bv_ref[...])
        a_ref[...] = (_silu(gate) * val).astype(a_ref.dtype)

    lo, hi = (lambda j: (0, j)), (lambda j: (0, j + nb))
    return _pcall(
        body, name=name, grid=(nb,),
        in_specs=[pl.BlockSpec((s, tc), lo), pl.BlockSpec((s, tc), hi), pl.BlockSpec((3, tc), lo),
                  pl.BlockSpec((3, tc), hi), pl.BlockSpec((1, tc), lo), pl.BlockSpec((1, tc), hi)],
        out_specs=pl.BlockSpec((s, tc), lo),
        out_shape=jax.ShapeDtypeStruct((s, dff), BF16),
        compiler_params=_params(1),
    )(u, u, cw, cw, cb, cb)


def _conv_act_bwd(u, cw, cb, da, name, tc=128):
    s, two_f = u.shape
    dff = two_f // 2
    tc = _tile(dff, tc)
    nb = dff // tc

    def body(ug_ref, uv_ref, wg_ref, wv_ref, bg_ref, bv_ref, da_ref, du_ref, dw_ref, db_ref):
        ug, uv, da = ug_ref[...], uv_ref[...], da_ref[...]
        gate = _conv(ug, wg_ref, bg_ref[...])
        val = _conv(uv, wv_ref, bv_ref[...])
        sg = _sigmoid(gate)
        d_val = da * (gate * sg)
        d_gate = da * val * (sg * (1.0 + gate * (1.0 - sg)))
        for half, (dc, uu, w_ref) in enumerate(((d_gate, ug, wg_ref), (d_val, uv, wv_ref))):
            du = w_ref[0:1, :] * _shift_up(dc, 2) + w_ref[1:2, :] * _shift_up(dc, 1) + w_ref[2:3, :] * dc
            du_ref[half] = du.astype(du_ref.dtype)
            dw_ref[half, 0:1, :] = _colsum(dc * _shift_down(uu, 2))
            dw_ref[half, 1:2, :] = _colsum(dc * _shift_down(uu, 1))
            dw_ref[half, 2:3, :] = _colsum(dc * uu)
            db_ref[half] = _colsum(dc)

    lo, hi = (lambda j: (0, j)), (lambda j: (0, j + nb))
    both = lambda j: (0, 0, j)
    return _pcall(
        body, name=name, grid=(nb,),
        in_specs=[pl.BlockSpec((s, tc), lo), pl.BlockSpec((s, tc), hi), pl.BlockSpec((3, tc), lo),
                  pl.BlockSpec((3, tc), hi), pl.BlockSpec((1, tc), lo), pl.BlockSpec((1, tc), hi),
                  pl.BlockSpec((s, tc), lo)],
        out_specs=[pl.BlockSpec((2, s, tc), both), pl.BlockSpec((2, 3, tc), both), pl.BlockSpec((2, 1, tc), both)],
        out_shape=[jax.ShapeDtypeStruct((2, s, dff), BF16), jax.ShapeDtypeStruct((2, 3, dff), F32),
                   jax.ShapeDtypeStruct((2, 1, dff), F32)],
        compiler_params=_params(1),
    )(u, u, cw, cw, cb, cb, da)


def _adamw_math(w, g, m, v):
    m = ADAM_B1 * m + (1.0 - ADAM_B1) * g
    v = ADAM_B2 * v + (1.0 - ADAM_B2) * (g * g)
    m_hat = m / (1.0 - ADAM_B1 ** ADAM_STEP)
    v_hat = v / (1.0 - ADAM_B2 ** ADAM_STEP)
    delta = -ADAM_LR * (m_hat / (jnp.sqrt(v_hat) + ADAM_EPS) + ADAM_WD * w)
    return delta, m, v


def _update_tiles(r, c, tr):
    tc = c
    if r % 8:
        tr, tc = r, _tile(c, max(LANE, 512 * 1024 // r // LANE * LANE))
    elif r <= tr:
        tr = r
    while r % tr:
        tr -= 8
    return tr, tc


def _adamw(w, g, m, v, name, tr=128):
    layers, r, c = w.shape
    tr, tc = _update_tiles(r, c, tr)

    def body(w_ref, g_ref, m_ref, v_ref, go_ref, d_ref, mo_ref, vo_ref):
        grad = g_ref[...]
        delta, m_new, v_new = _adamw_math(w_ref[...], grad, m_ref[...], v_ref[...])
        go_ref[...], d_ref[...], mo_ref[...], vo_ref[...] = grad, delta, m_new, v_new

    spec = pl.BlockSpec((None, tr, tc), lambda l, i, j: (l, i, j))
    return _pcall(
        body, name=name, grid=(layers, r // tr, c // tc), in_specs=[spec] * 4, out_specs=[spec] * 4,
        out_shape=[jax.ShapeDtypeStruct((layers, r, c), F32)] * 4, compiler_params=_params(3),
    )(w, g, m, v)


def _adamw_pieces(w, lands, sums, chip, m, v, name, tr=128):
    layers, r, c = w.shape
    tr, tc = _update_tiles(r, c, tr)
    nr, nc = r // tr, c // tc

    def body(chip_ref, w_ref, *rest):
        land_refs, own_refs = rest[:layers], rest[layers:2 * layers]
        m_ref, v_ref, go_ref, d_ref, mo_ref, vo_ref = rest[2 * layers:]
        for layer in range(layers):
            @pl.when(pl.program_id(0) == layer)
            def _(land_ref=land_refs[layer], own_ref=own_refs[layer]):
                grad = jnp.zeros(w_ref.shape, F32)
                for q in range(4):
                    grad = grad + jnp.where(chip_ref[0] == q, own_ref[...], land_ref[q]).astype(F32)
                delta, m_new, v_new = _adamw_math(w_ref[...], grad, m_ref[...], v_ref[...])
                go_ref[...], d_ref[...], mo_ref[...], vo_ref[...] = grad, delta, m_new, v_new

    def walk(k, l, i, j):
        here = l == k
        return jnp.where(here, i, jnp.where(l < k, 0, nr - 1)), jnp.where(here, j, jnp.where(l < k, 0, nc - 1))

    spec = pl.BlockSpec((None, tr, tc), lambda l, i, j, chip_ref: (l, i, j))
    land_specs = [pl.BlockSpec((4, tr, tc), lambda l, i, j, chip_ref, k=k: (0,) + walk(k, l, i, j))
                  for k in range(layers)]
    own_specs = [pl.BlockSpec((None, tr, tc), lambda l, i, j, chip_ref, k=k: (chip_ref[0],) + walk(k, l, i, j))
                 for k in range(layers)]
    return _pcall(
        body, name=name,
        grid_spec=pltpu.PrefetchScalarGridSpec(
            num_scalar_prefetch=1, grid=(layers, nr, nc),
            in_specs=[spec] + land_specs + own_specs + [spec, spec], out_specs=[spec] * 4),
        out_shape=[jax.ShapeDtypeStruct((layers, r, c), F32)] * 4, compiler_params=_params(3),
    )(chip, w, *lands, *sums, m, v)


def _pair_sum(pieces, partner, core, name, tr=512):
    _, r, c = pieces.shape
    tc = c
    if r % 8:
        tr, tc = r, _tile(c, max(LANE, 1024 * 1024 // r // LANE * LANE))
    elif r <= tr:
        tr = r
    while r % tr:
        tr -= 8

    def body(core_ref, mine_ref, partner_ref, out_ref):
        out_ref[...] = (mine_ref[...].astype(F32) + partner_ref[...].astype(F32)).astype(out_ref.dtype)

    return _pcall(
        body, name=name,
        grid_spec=pltpu.PrefetchScalarGridSpec(
            num_scalar_prefetch=1, grid=(4, r // tr, c // tc),
            in_specs=[pl.BlockSpec((None, tr, tc), lambda q, i, j, core_ref: (2 * q + core_ref[0], i, j)),
                      pl.BlockSpec((None, tr, tc), lambda q, i, j, core_ref: (q, i, j))],
            out_specs=pl.BlockSpec((None, tr, tc), lambda q, i, j, core_ref: (q, i, j))),
        out_shape=jax.ShapeDtypeStruct((4, r, c), pieces.dtype), compiler_params=_params(3),
    )(core, pieces, partner)


def _sum8(x, name):
    p = x.shape[2]
    tp = _tile(p, 16 * 1024)

    def body(x_ref, o_ref):
        acc = x_ref[0]
        for i in range(1, N_DEV):
            acc = acc + x_ref[i]
        o_ref[...] = acc

    return _pcall(
        body, name=name, grid=(p // tp,), in_specs=[pl.BlockSpec((N_DEV, 1, tp), lambda i: (0, 0, i))],
        out_specs=pl.BlockSpec((1, tp), lambda i: (0, i)), out_shape=jax.ShapeDtypeStruct((1, p), x.dtype),
        compiler_params=_params(1),
    )(x)


def _exchange(arrays, name, scatter):
    n = len(arrays)
    hbm = pl.BlockSpec(memory_space=pl.ANY)

    def body(*refs):
        ins, outs, token = refs[:n], refs[n:2 * n], refs[2 * n]
        send_sems, recv_sems, local_sems = refs[2 * n + 1:]
        token[...] = jnp.zeros_like(token)
        x, y, c = lax.axis_index("x"), lax.axis_index("y"), lax.axis_index("c")
        me = 4 * x + 2 * y + c
        copies = []
        for a in range(n):
            src_mine = ins[a].at[me] if scatter else ins[a]
            local = pltpu.make_async_copy(src_mine, outs[a].at[me], local_sems.at[a])
            local.start()
            copies.append(local)
            for k in range(1, N_DEV):
                px = 1 - x if k & 4 else x
                py = 1 - y if k & 2 else y
                pc = 1 - c if k & 1 else c
                src = ins[a].at[4 * px + 2 * py + pc] if scatter else ins[a]
                cp = pltpu.make_async_remote_copy(
                    src_ref=src, dst_ref=outs[a].at[me],
                    send_sem=send_sems.at[a * (N_DEV - 1) + k - 1], recv_sem=recv_sems.at[a * (N_DEV - 1) + k - 1],
                    device_id=(px, py, pc), device_id_type=pl.DeviceIdType.MESH)
                cp.start()
                copies.append(cp)
        for cp in copies:
            cp.wait()

    out_shape = [jax.ShapeDtypeStruct(a.shape if scatter else (N_DEV,) + a.shape, a.dtype) for a in arrays]
    res = _pcall(
        body, name=name, in_specs=[hbm] * n, out_specs=[hbm] * n + [pl.BlockSpec(memory_space=pltpu.VMEM)],
        out_shape=out_shape + [jax.ShapeDtypeStruct((8, LANE), F32)],
        scratch_shapes=[pltpu.SemaphoreType.DMA((n * (N_DEV - 1),)), pltpu.SemaphoreType.DMA((n * (N_DEV - 1),)),
                        pltpu.SemaphoreType.DMA((n,))],
        compiler_params=pltpu.CompilerParams(has_side_effects=True),
    )(*arrays)
    return res[:n], res[n][0, 0]


_HBM = pl.BlockSpec(memory_space=pltpu.HBM)
_SEM = pl.BlockSpec(memory_space=pltpu.SEMAPHORE)
_DATAFLOW = pltpu.SideEffectType.DATAFLOW_SIDE_EFFECTING


def _peer(k, x, y, c):
    return (1 - x if k & 4 else x, 1 - y if k & 2 else y, 1 - c if k & 1 else c)


def _pair_plan(x, y, c):
    return [(2 * q + (1 - c), q, (x, y, 1 - c)) for q in range(4)]


def _chip_plan(x, y, c):
    out = []
    for k in _ICI_PEERS:
        px, py, pc = _peer(k, x, y, c)
        out.append((2 * px + py, 2 * x + y, (px, py, pc)))
    return out


def _all_plan(x, y, c):
    return [(0, 4 * x + 2 * y + c, _peer(k, x, y, c)) for k in range(1, N_DEV)]


def _split_start(arrays, plan, name, land_blocks=4):
    n = len(arrays)
    lands = [lax.empty((land_blocks,) + a.shape[1:], a.dtype) for a in arrays]
    n_copies = len(plan(0, 0, 0))

    def body(*refs):
        srcs, dsts = refs[:n], refs[n:2 * n]
        send_sems, recv_sems, token = refs[4 * n:5 * n], refs[5 * n:6 * n], refs[6 * n]
        copies = plan(lax.axis_index("x"), lax.axis_index("y"), lax.axis_index("c"))
        for a in range(n):
            for j, (src_block, dst_block, peer) in enumerate(copies):
                pltpu.make_async_remote_copy(
                    src_ref=srcs[a].at[src_block], dst_ref=dsts[a].at[dst_block],
                    send_sem=send_sems[a].at[j], recv_sem=recv_sems[a].at[j],
                    device_id=peer, device_id_type=pl.DeviceIdType.MESH).start()
        token[...] = jnp.zeros_like(token)

    sems = [pltpu.SemaphoreType.DMA((n_copies,))] * (2 * n)
    res = _pcall(
        body, name=name,
        in_specs=[_HBM] * (2 * n),
        out_specs=[_HBM] * (2 * n) + [_SEM] * (2 * n) + [pl.BlockSpec(memory_space=pltpu.VMEM)],
        out_shape=[pltpu.HBM(a.shape, a.dtype) for a in arrays] + [pltpu.HBM(l.shape, l.dtype) for l in lands]
        + sems + [jax.ShapeDtypeStruct((8, LANE), F32)],
        input_output_aliases={i: i for i in range(2 * n)},
        compiler_params=pltpu.CompilerParams(has_side_effects=_DATAFLOW),
    )(*[pltpu.with_memory_space_constraint(a, pltpu.HBM) for a in arrays],
      *[pltpu.with_memory_space_constraint(l, pltpu.HBM) for l in lands])
    handles = [(res[a], res[n + a], res[2 * n + a], res[3 * n + a]) for a in range(n)]
    return handles, res[4 * n][0, 0]


def _split_wait(handles, plan, after, name):
    n = len(handles)
    after = list(after) if isinstance(after, (list, tuple)) else [after]

    def body(*refs):
        srcs, dsts = refs[:n], refs[n:2 * n]
        send_sems, recv_sems = refs[2 * n:3 * n], refs[3 * n:4 * n]
        copies = plan(lax.axis_index("x"), lax.axis_index("y"), lax.axis_index("c"))
        for a in range(n):
            for j, (src_block, dst_block, peer) in enumerate(copies):
                cp = pltpu.make_async_remote_copy(
                    src_ref=srcs[a].at[src_block], dst_ref=dsts[a].at[dst_block],
                    send_sem=send_sems[a].at[j], recv_sem=recv_sems[a].at[j],
                    device_id=peer, device_id_type=pl.DeviceIdType.MESH)
                cp.wait_send()
                cp.wait_recv()

    srcs, lands = [h[0] for h in handles], [h[1] for h in handles]
    res = _pcall(
        body, name=name,
        in_specs=[_HBM] * (2 * n) + [_SEM] * (2 * n) + [pl.BlockSpec(memory_space=pl.ANY)] * len(after),
        out_specs=[_HBM] * (2 * n),
        out_shape=[pltpu.HBM(t.shape, t.dtype) for t in srcs + lands],
        input_output_aliases={i: i for i in range(2 * n)},
        compiler_params=pltpu.CompilerParams(has_side_effects=_DATAFLOW),
    )(*srcs, *lands, *[h[2] for h in handles], *[h[3] for h in handles], *after)
    return res[:n], res[n:]


_ICI_PEERS = (2, 4, 6)


def _gather2_start(shards, name):
    n = len(shards)
    lands = [lax.empty((N_DEV,) + a.shape, a.dtype) for a in shards]

    def body(*refs):
        srcs, dsts = refs[:n], refs[n:2 * n]
        send_sems, d2d_sems, ici_sems = refs[4 * n:5 * n], refs[5 * n:6 * n], refs[6 * n:7 * n]
        token = refs[7 * n]
        x, y, c = lax.axis_index("x"), lax.axis_index("y"), lax.axis_index("c")
        me = 4 * x + 2 * y + c
        for a in range(n):
            for j, k in enumerate((1,) + _ICI_PEERS):
                recv = d2d_sems[a].at[0] if j == 0 else ici_sems[a].at[j - 1]
                pltpu.make_async_remote_copy(
                    src_ref=srcs[a], dst_ref=dsts[a].at[me], send_sem=send_sems[a].at[j], recv_sem=recv,
                    device_id=_peer(k, x, y, c), device_id_type=pl.DeviceIdType.MESH).start()
        token[...] = jnp.zeros_like(token)

    dma = pltpu.SemaphoreType.DMA
    res = _pcall(
        body, name=name,
        in_specs=[_HBM] * (2 * n),
        out_specs=[_HBM] * (2 * n) + [_SEM] * (3 * n) + [pl.BlockSpec(memory_space=pltpu.VMEM)],
        out_shape=[pltpu.HBM(a.shape, a.dtype) for a in shards] + [pltpu.HBM(l.shape, l.dtype) for l in lands]
        + [dma((4,))] * n + [dma((1,))] * n + [dma((3,))] * n + [jax.ShapeDtypeStruct((8, LANE), F32)],
        input_output_aliases={i: i for i in range(2 * n)},
        compiler_params=pltpu.CompilerParams(has_side_effects=_DATAFLOW),
    )(*[pltpu.with_memory_space_constraint(a, pltpu.HBM) for a in shards],
      *[pltpu.with_memory_space_constraint(l, pltpu.HBM) for l in lands])
    handles = [tuple(res[i * n + a] for i in range(5)) for a in range(n)]
    return handles, res[5 * n][0, 0]


def _gather2_forward(handle, after, name):
    src, land, send_sems, d2d_sem, ici_sems = handle

    def body(land_ref, ici_ref, d2d_ref, after_ref, land_out, fwd_send, fwd_recv, token):
        x, y, c = lax.axis_index("x"), lax.axis_index("y"), lax.axis_index("c")
        sibling = (x, y, 1 - c)
        arrived = [(_peer(k, x, y, c), ici_ref.at[j]) for j, k in enumerate(_ICI_PEERS)] + [(sibling, d2d_ref.at[0])]
        for j, ((px, py, pc), recv) in enumerate(arrived):
            block = land_ref.at[4 * px + 2 * py + pc]
            pltpu.make_async_remote_copy(
                src_ref=block, dst_ref=block, send_sem=fwd_send.at[j], recv_sem=recv,
                device_id=(px, py, pc), device_id_type=pl.DeviceIdType.MESH).wait_recv()
            pltpu.make_async_remote_copy(
                src_ref=block, dst_ref=block, send_sem=fwd_send.at[j], recv_sem=fwd_recv.at[j],
                device_id=sibling, device_id_type=pl.DeviceIdType.MESH).start()
        token[...] = jnp.zeros_like(token)

    dma = pltpu.SemaphoreType.DMA
    land, fwd_send, fwd_recv, token = _pcall(
        body, name=name,
        in_specs=[_HBM, _SEM, _SEM, pl.BlockSpec(memory_space=pl.ANY)],
        out_specs=[_HBM, _SEM, _SEM, pl.BlockSpec(memory_space=pltpu.VMEM)],
        out_shape=[pltpu.HBM(land.shape, land.dtype), dma((4,)), dma((4,)), jax.ShapeDtypeStruct((8, LANE), F32)],
        input_output_aliases={0: 0},
        compiler_params=pltpu.CompilerParams(has_side_effects=_DATAFLOW),
    )(land, ici_sems, d2d_sem, after)
    return (src, land, send_sems, fwd_send, fwd_recv), token[0, 0]


def _gather2_wait(handle, after, name):
    src, land, send_sems, fwd_send, fwd_recv = handle

    def body(src_ref, land_ref, send_ref, fsend_ref, frecv_ref, after_ref, src_out, land_out):
        x, y, c = lax.axis_index("x"), lax.axis_index("y"), lax.axis_index("c")
        block = land_ref.at[4 * x + 2 * y + c]

        def copy(send, recv):
            return pltpu.make_async_remote_copy(src_ref=src_ref, dst_ref=block, send_sem=send, recv_sem=recv,
                                                device_id=(x, y, 1 - c), device_id_type=pl.DeviceIdType.MESH)

        for j in range(4):
            copy(send_ref.at[j], frecv_ref.at[j]).wait_send()
        for j in range(4):
            copy(fsend_ref.at[j], frecv_ref.at[j]).wait_send()
            copy(fsend_ref.at[j], frecv_ref.at[j]).wait_recv()

    res = _pcall(
        body, name=name,
        in_specs=[_HBM, _HBM, _SEM, _SEM, _SEM, pl.BlockSpec(memory_space=pl.ANY)],
        out_specs=[_HBM, _HBM],
        out_shape=[pltpu.HBM(src.shape, src.dtype), pltpu.HBM(land.shape, land.dtype)],
        input_output_aliases={0: 0, 1: 1},
        compiler_params=pltpu.CompilerParams(has_side_effects=_DATAFLOW),
    )(src, land, send_sems, fwd_send, fwd_recv, after)
    return res[0], res[1]


def _pad_cols(x, width=LANE):
    return jnp.pad(x, ((0, 0), (0, width - x.shape[1])))


def _cols_full(g):
    return jnp.transpose(g, (1, 0, 2)).reshape(g.shape[1], -1)


def _ffn_fwd(x1, p, i, tag):
    h2 = _adaln_fwd(x1, p["norm_ffn"][i], p["sc_f"][i], p["sh_f"][i], f"ffn_norm_{tag}")
    u = _matmul(h2, p["fetch"](f"up{i}", h2), name=f"ffn_up_{tag}", tn=1408, b_shards=True)
    a = _conv_act_fwd(u, p["conv_w"][i], p["conv_b"][i], f"ffn_act_{tag}")
    g_f = p["g_f"][i]
    x2, f = _matmul(a, p["fetch"](f"down{i}", a), name=f"ffn_down_{tag}", tk=1408, out_dtypes=(F32, F32),
                    epilogue=lambda acc, x1, g: (x1 + (1.0 + g) * acc, acc), extras=(("mn", x1), ("n", g_f)))
    return x2, dict(h2=h2, u=u, a=a, f=f)


def _ffn_bwd(incoming, x1, saved, p, i, tag, branch):
    d = x1.shape[1]
    dx2, df, dg_f = incoming
    w_up, w_down = p["fetch"](f"up{i}", None), p["fetch"](f"down{i}", None)
    da = _matmul(df, w_down, tb=True, name=f"ffn_down_dx_{tag}", tn=1408)
    dw_down = _matmul(saved["a"], df, ta=True, name=f"ffn_down_dw_{tag}", tm=1408, out_dtypes=(BF16,))
    du, dcw, dcb = _conv_act_bwd(saved["u"], p["conv_w"][i], p["conv_b"][i], da, f"ffn_act_bwd_{tag}")
    dcw, dcb = (jnp.concatenate([t[0], t[1]], axis=1) for t in (dcw, dcb))
    tok = p["flush"](du)
    dh2 = _matmul(du, w_up, tb=True, name=f"ffn_up_dx_{tag}", tk=1408, a_halves=True, b_shards=True)
    dw_up = _matmul(saved["h2"], du, ta=True, name=f"ffn_up_dw_{tag}", tn=1408, out_dtypes=(BF16,), b_halves=True,
                    out_shards=True)
    tok = tok + p["send"](f"ffn{i}", [dw_up, dw_down.reshape(N_DEV, -1, d)])
    dx1, dsh, dsc, dgain, dy, dg_m = _adaln_bwd(x1, dh2, dx2, p["norm_ffn"][i] + tok, p["sc_f"][i],
                                                f"ffn_norm_bwd_{tag}", branch)
    grads = dict(conv_w=dcw, conv_b=dcb, norm_ffn=dgain, sh_f=dsh, sc_f=dsc, g_f=dg_f)
    return (dx1, dy, dg_m), grads


def _gla_layer_fwd(x, p, i):
    h1 = _adaln_fwd(x, p["norm_mix"][i], p["sc_m"][i], p["sh_m"][i], "gla_norm")
    w_t, w_tail_t, main = p["fetch"]("gla_in", h1)
    proj = _matmul(h1, w_t, tb=True, b_rows=main, name="gla_in")
    a_tail = _matmul(h1, w_tail_t, tb=True, name="gla_in_tail")
    dk_total = p["gla_wg_p"].shape[1]
    o, states = _gla_fwd(proj, a_tail, p["gla_wg_p"], p["gla_b_gate"], "gla_chunks")
    assert 2 * dk_total == o.shape[1]
    r = ("cols", proj, 2, o.shape[1])
    og = _gla_post_fwd(o, r, p["gla_norm"], "gla_post")
    x1, y = _matmul(og, p["fetch"]("gla_out", og), name="gla_out", out_dtypes=(F32, F32),
                    epilogue=lambda acc, x, g: (x + (1.0 + g) * acc, acc), extras=(("mn", x), ("n", p["g_m"][i])))
    return x1, dict(h1=h1, proj=proj, a_tail=a_tail, o=o, r=r, states=states, og=og, y=y)


def _gla_layer_bwd(incoming, x, sv, p, i, branch):
    d = x.shape[1]
    dx1, dy, dg_m = incoming
    (w_t, w_tail_t, main), w_out = p["fetch"]("gla_in", None), p["fetch"]("gla_out", None)
    dog = _matmul(dy, w_out, tb=True, name="gla_out_dx")
    dw_out = _matmul(sv["og"], dy, ta=True, name="gla_out_dw", out_dtypes=(BF16,))
    tok = p["flush"](dog) + p["send"]("gla_out", [dw_out.reshape(N_DEV, -1, d)])
    d_o, dproj, dgn = _gla_post_bwd(sv["o"], sv["r"], p["gla_norm"] + tok, dog, "gla_post_bwd")
    dproj, dga = _gla_bwd(sv["proj"], sv["a_tail"], p["gla_wg_p"], p["gla_b_gate"], sv["states"], d_o, dproj,
                          "gla_chunks_bwd")
    tok = p["flush"](dga)
    da_tail = _matmul(dga, p["gla_wg_p"], tb=True, name="gla_gate_dx", out_dtypes=(BF16,))
    dwg = _matmul(sv["a_tail"], dga, ta=True, name="gla_gate_dw")
    dbg = _rowwise(lambda t: (_colsum(t),), [("row", dga)], [("acc", dga.shape[1], F32)], name="gla_gate_db")[0]
    dh_tail = _matmul(da_tail, w_tail_t, name="gla_in_tail_dx")
    dh1 = _matmul(dproj, w_t, b_rows=main, name="gla_in_dx", tk=2048,
                  epilogue=lambda acc, t: (acc + t,), extras=(("mn", dh_tail),))
    rank = p["gla_rank"]
    dw_main = _matmul(dproj, sv["h1"], ta=True, name="gla_in_dw", out_dtypes=(BF16,), out_rows=main + rank)
    dx, dsh, dsc, dgain, *into_branch = _adaln_bwd(x, dh1, dx1, p["norm_mix"][i] + tok, p["sc_m"][i], "gla_norm_bwd",
                                                   branch)
    grads = dict(gla_w_gate=dwg[:rank], gla_b_gate=dbg, gla_norm=dgn, norm_mix=dgain, sh_m=dsh, sc_m=dsc, g_m=dg_m,
                 gla_w_in_unsent=(dw_main, da_tail, sv["h1"]))
    return (dx, *into_branch), grads


def _fox_layer_fwd(x, p, i):
    d = x.shape[1]
    hd = p["fox_q_norm"].shape[1]
    heads = d // hd
    s = x.shape[0]
    t = _tile(s, 512)
    h1 = _adaln_fwd(x, p["norm_mix"][i], p["sc_m"][i], p["sh_m"][i], "fox_norm")
    w_t, w_tail_t, main = p["fetch"]("fox_in", h1)
    proj = _matmul(h1, w_t, tb=True, b_rows=main, name="fox_in")
    fl = _matmul(h1, w_tail_t, tb=True, name="fox_in_tail")
    q, k, v, og = (("cols", proj, j, d) for j in range(4))
    qn, kn, vb = _fox_prep(q, k, v, p["fox_q_norm"], p["fox_k_norm"], d, hd, "fox_prep")
    cum = _fox_cum(fl, p["fox_bf_p"], "fox_cum")
    cum_t = jnp.transpose(cum[:, :heads])
    cum_col, cum_row = cum_t[:, :, None], cum_t.reshape(heads, s // t, 1, t)
    o, lse = _fox_attn_fwd(qn, kn, vb, cum_col, cum_row, hd, t, "fox_attn")
    act = _fox_gate_fwd(o, og, "fox_gate")
    x1, y = _matmul(act, p["fetch"]("fox_out", act), name="fox_out", out_dtypes=(F32, F32),
                    epilogue=lambda acc, x, g: (x + (1.0 + g) * acc, acc), extras=(("mn", x), ("n", p["g_m"][i])))
    return x1, dict(h1=h1, q=q, k=k, og=og, fl=fl, qn=qn, kn=kn, vb=vb, cum_col=cum_col, cum_row=cum_row,
                    o=o, lse=lse, act=act, y=y, t=t, hd=hd)


def _fox_layer_bwd(incoming, x, sv, p, i, branch):
    d = x.shape[1]
    hd, t = sv["hd"], sv["t"]
    heads = d // hd
    s = x.shape[0]
    dx1, dy, dg_m = incoming
    (w_t, w_tail_t, main), w_out = p["fetch"]("fox_in", None), p["fetch"]("fox_out", None)
    dact = _matmul(dy, w_out, tb=True, name="fox_out_dx")
    dw_out = _matmul(sv["act"], dy, ta=True, name="fox_out_dw", out_dtypes=(BF16,))
    d_o, dproj = _fox_gate_bwd(sv["o"], sv["og"], dact, "fox_gate_bwd")
    tok_flush = p["flush"](d_o)
    dqn, dkn, dproj, dcq, dck = _fox_attn_bwd(sv["qn"], sv["kn"], sv["vb"], d_o, sv["o"], sv["lse"], sv["cum_col"],
                                              sv["cum_row"], hd, t, dproj, "fox_attn_bwd")
    dproj, gq, gk = _fox_prep_bwd(sv["q"], sv["k"], dqn, dkn, p["fox_q_norm"], p["fox_k_norm"], hd, dproj,
                                  "fox_prep_bwd")
    dcum = _pad_cols(jnp.transpose(dcq[:, :, 0] - dck.reshape(heads, s)))
    dfl, dbf = _fox_cum_bwd(dcum, sv["fl"], p["fox_bf_p"], "fox_cum_bwd")
    dfl_b = dfl.astype(BF16)
    dh_tail = _matmul(dfl_b, w_tail_t, name="fox_in_tail_dx")
    dh1 = _matmul(dproj, w_t, b_rows=main, name="fox_in_dx", tk=2048,
                  epilogue=lambda acc, tl: (acc + tl,), extras=(("mn", dh_tail),))
    dw_main = _matmul(dproj, sv["h1"], ta=True, name="fox_in_dw", out_dtypes=(BF16,), out_rows=main + heads)
    dw_in = _tail_rows(dfl_b, sv["h1"], dw_main, heads, "fox_in_tail_dw").reshape(N_DEV, -1, d)
    tok = tok_flush + p["send"]("fox", [dw_in, dw_out.reshape(N_DEV, -1, d)])
    dx, dsh, dsc, dgain, *into_branch = _adaln_bwd(x, dh1, dx1, p["norm_mix"][i] + tok, p["sc_m"][i], "fox_norm_bwd",
                                                   branch)
    grads = dict(fox_b_f=dbf[:, :heads], fox_q_norm=gq.reshape(heads, hd).sum(0, keepdims=True),
                 fox_k_norm=gk.reshape(heads, hd).sum(0, keepdims=True), norm_mix=dgain, sh_m=dsh, sc_m=dsc, g_m=dg_m)
    return (dx, *into_branch), grads


SMALL = ("b_mod", "norm_mix", "norm_ffn", "gla_b_gate", "gla_norm", "fox_b_f", "fox_q_norm", "fox_k_norm",
         "ffn_conv_b", "norm_final")
SMALL_SHARDED = ("gla_w_gate", "ffn_conv_w")
BIG = ("gla_w_in", "gla_w_out", "fox_w_in", "fox_w_out", "ffn_w_up", "ffn_w_down")
WEIGHTS = ("w_mod", "b_mod", "norm_mix", "norm_ffn", "gla_w_in", "gla_w_gate", "gla_b_gate", "gla_norm", "gla_w_out",
           "fox_w_in", "fox_b_f", "fox_q_norm", "fox_k_norm", "fox_w_out", "ffn_w_up", "ffn_conv_w", "ffn_conv_b",
           "ffn_w_down", "norm_final")


def _pack(parts):
    flat = jnp.concatenate([p.reshape(-1) for p in parts])
    pad = (-flat.shape[0]) % 1024
    return jnp.pad(flat, (0, pad)).reshape(1, -1)


def _unpack(flat, shapes):
    out, off = [], 0
    for shp in shapes:
        n = 1
        for s in shp:
            n *= s
        out.append(flat[0, off:off + n].reshape(shp))
        off += n
    return out


def kernel(x, c, w_mod, b_mod, norm_mix, norm_ffn, gla_w_in, gla_w_gate, gla_b_gate, gla_norm, gla_w_out, fox_w_in, fox_b_f, fox_q_norm, fox_k_norm, fox_w_out, ffn_w_up, ffn_conv_w, ffn_conv_b, ffn_w_down, norm_final, loss_target, m_w_mod, m_b_mod, m_norm_mix, m_norm_ffn, m_gla_w_in, m_gla_w_gate, m_gla_b_gate, m_gla_norm, m_gla_w_out, m_fox_w_in, m_fox_b_f, m_fox_q_norm, m_fox_k_norm, m_fox_w_out, m_ffn_w_up, m_ffn_conv_w, m_ffn_conv_b, m_ffn_w_down, m_norm_final, v_w_mod, v_b_mod, v_norm_mix, v_norm_ffn, v_gla_w_in, v_gla_w_gate, v_gla_b_gate, v_gla_norm, v_gla_w_out, v_fox_w_in, v_fox_b_f, v_fox_q_norm, v_fox_k_norm, v_fox_w_out, v_ffn_w_up, v_ffn_conv_w, v_ffn_conv_b, v_ffn_w_down, v_norm_final):
    w = dict(w_mod=w_mod, b_mod=b_mod, norm_mix=norm_mix, norm_ffn=norm_ffn, gla_w_in=gla_w_in, gla_w_gate=gla_w_gate,
             gla_b_gate=gla_b_gate, gla_norm=gla_norm, gla_w_out=gla_w_out, fox_w_in=fox_w_in, fox_b_f=fox_b_f,
             fox_q_norm=fox_q_norm, fox_k_norm=fox_k_norm, fox_w_out=fox_w_out, ffn_w_up=ffn_w_up,
             ffn_conv_w=ffn_conv_w, ffn_conv_b=ffn_conv_b, ffn_w_down=ffn_w_down, norm_final=norm_final)
    mom_m = dict(w_mod=m_w_mod, b_mod=m_b_mod, norm_mix=m_norm_mix, norm_ffn=m_norm_ffn, gla_w_in=m_gla_w_in,
                 gla_w_gate=m_gla_w_gate, gla_b_gate=m_gla_b_gate, gla_norm=m_gla_norm, gla_w_out=m_gla_w_out,
                 fox_w_in=m_fox_w_in, fox_b_f=m_fox_b_f, fox_q_norm=m_fox_q_norm, fox_k_norm=m_fox_k_norm,
                 fox_w_out=m_fox_w_out, ffn_w_up=m_ffn_w_up, ffn_conv_w=m_ffn_conv_w, ffn_conv_b=m_ffn_conv_b,
                 ffn_w_down=m_ffn_w_down, norm_final=m_norm_final)
    mom_v = dict(w_mod=v_w_mod, b_mod=v_b_mod, norm_mix=v_norm_mix, norm_ffn=v_norm_ffn, gla_w_in=v_gla_w_in,
                 gla_w_gate=v_gla_w_gate, gla_b_gate=v_gla_b_gate, gla_norm=v_gla_norm, gla_w_out=v_gla_w_out,
                 fox_w_in=v_fox_w_in, fox_b_f=v_fox_b_f, fox_q_norm=v_fox_q_norm, fox_k_norm=v_fox_k_norm,
                 fox_w_out=v_fox_w_out, ffn_w_up=v_ffn_w_up, ffn_conv_w=v_ffn_conv_w, ffn_conv_b=v_ffn_conv_b,
                 ffn_w_down=v_ffn_w_down, norm_final=v_norm_final)

    me = 4 * lax.axis_index("x") + 2 * lax.axis_index("y") + lax.axis_index("c")
    xs, target = x[0], loss_target[0]
    s, d = xs.shape
    depth = w_mod.shape[0]
    mod_cols = w_mod.shape[2]
    rank = gla_w_gate.shape[1]
    hd = fox_q_norm.shape[1]
    fox_heads = d // hd
    dk_total = gla_w_gate.shape[2] * N_DEV

    cond = c * (1.0 / (1.0 + jnp.exp(-c)))
    g, _ = _exchange([gla_w_gate[0], ffn_conv_w, cond], "gather_small", scatter=False)
    cond_all = g[2][:, 0, :]

    cond_pad = jnp.pad(cond_all, ((0, 16 - N_DEV), (0, 0)))
    mod_part = []
    for i in range(depth):
        b_cols = lax.dynamic_slice(b_mod[i:i + 1], (0, me * mod_cols), (1, mod_cols))
        mod_part.append(_matmul(cond_pad, w_mod, b_layer=i, name=f"mod_{i}", tn=768,
                                epilogue=lambda acc, b: (acc + b,), extras=(("n", b_cols),))[:N_DEV])
    (mod_all,), tok_mod = _exchange([jnp.stack(mod_part)], "gather_mod", scatter=False)
    mod = lax.dynamic_index_in_dim(mod_all, me, axis=2, keepdims=False)
    mod = jnp.transpose(mod, (1, 0, 2)).reshape(depth, 6, 1, d)

    big_names = ["gla_in", "gla_out", "up0", "down0", "fox_in", "fox_out", "up1", "down1"]
    first = [jnp.transpose(gla_w_in[0] + tok_mod).astype(BF16), gla_w_out[0].astype(BF16)]
    handles, tok_first = _gather2_start(first, "gather_weights_start_first")
    rest = [ffn_w_up[0] + tok_first, ffn_w_down[0], jnp.transpose(fox_w_in[0]), fox_w_out[0], ffn_w_up[1],
            ffn_w_down[1]]
    handles_rest, tok0 = _gather2_start([t.astype(BF16) for t in rest], "gather_weights_start_rest")
    handles = handles + handles_rest
    ready, forwarded = {}, {}

    def split_tail(full_t, tail):
        main = full_t.shape[0] - tail
        return full_t, jnp.pad(full_t[main:], ((0, LANE - tail), (0, 0))), main

    def forward(idx, after):
        key = big_names[idx]
        forwarded[key] = _gather2_forward(handles[idx], after, f"gather_{key}_forward")

    def fetch(key, after):
        if key not in ready:
            idx = big_names.index(key)
            if idx == 0:
                forward(0, after)
            handle, _ = forwarded[key]
            _, full = _gather2_wait(handle, after, f"gather_{key}_wait")
            if idx + 1 < len(big_names):
                forward(idx + 1, full)
            if key == "gla_in":
                ready[key] = split_tail(full.reshape(-1, d), rank)
            elif key == "fox_in":
                ready[key] = split_tail(full.reshape(-1, d), fox_heads)
            elif key.startswith("up"):
                ready[key] = full
            else:
                ready[key] = full.reshape(-1, d)
        return ready[key]

    pending, sent = [], {}
    core = lax.axis_index("c").astype(jnp.int32).reshape(1)
    chip = 2 * lax.axis_index("x") + lax.axis_index("y")

    def send(key, pieces):
        hs, tok = _split_start(pieces, _pair_plan, f"scatter_{key}_pair_start")
        pending.append((key, hs))
        return tok

    def flush(after):
        tok = 0.0
        while pending:
            key, hs = pending.pop(0)
            mine, partner = _split_wait(hs, _pair_plan, after, f"scatter_{key}_pair_wait")
            sums = [_pair_sum(pc, pt, core, f"scatter_{key}_pair_sum{a}")
                    for a, (pc, pt) in enumerate(zip(mine, partner))]
            sent[key], t = _split_start(sums, _chip_plan, f"scatter_{key}_chip_start")
            tok = tok + t
        return tok

    p = dict(
        fetch=fetch, send=send, flush=flush,
        gla_wg_p=jnp.pad(_cols_full(g[0]), ((0, LANE - rank), (0, 0))),
        conv_w=[jnp.transpose(g[1][:, i], (1, 0, 2)).reshape(ffn_conv_w.shape[1], -1) for i in range(depth)],
        conv_b=[ffn_conv_b[i:i + 1] for i in range(depth)],
        gla_b_gate=gla_b_gate, gla_norm=gla_norm, fox_q_norm=fox_q_norm, fox_k_norm=fox_k_norm,
        fox_bf_p=_pad_cols(fox_b_f), gla_rank=rank,
        norm_mix=[norm_mix[i:i + 1] + (tok0 if i == 0 else 0.0) for i in range(depth)],
        norm_ffn=[norm_ffn[i:i + 1] for i in range(depth)],
    )

    for j, nm in enumerate(("sh_m", "sc_m", "g_m", "sh_f", "sc_f", "g_f")):
        p[nm] = [mod[i, j] for i in range(depth)]

    acts, saved = [xs], []
    for i in range(depth):
        layer_fwd = _gla_layer_fwd if i % 2 == 0 else _fox_layer_fwd
        x1, sv_mix = layer_fwd(acts[-1], p, i)
        x2, sv_ffn = _ffn_fwd(x1, p, i, str(i))
        saved.append((acts[-1], x1, sv_mix, sv_ffn))
        acts.append(x2)
    last_ffn = (saved[-1][3]["f"], p["g_f"][depth - 1])
    dx, d_norm_final, loss_part, *into_branch = _final_loss(acts[-1], target, norm_final.reshape(1, d), "final_loss",
                                                            last_ffn)
    incoming = (dx, *into_branch)

    lg = [None] * depth
    for i in reversed(range(depth)):
        x_in, x1, sv_mix, sv_ffn = saved[i]
        incoming, g_ffn = _ffn_bwd(incoming, x1, sv_ffn, p, i, str(i), (sv_mix["y"], p["g_m"][i]))
        layer_bwd = _gla_layer_bwd if i % 2 == 0 else _fox_layer_bwd
        before = (saved[i - 1][3]["f"], p["g_f"][i - 1]) if i else None
        incoming, g_mix = layer_bwd(incoming, x_in, sv_mix, p, i, before)
        lg[i] = {**g_ffn, **g_mix}
    grad_x = incoming[0][None]

    gla_l = [i for i in range(depth) if i % 2 == 0]
    fox_l = [i for i in range(depth) if i % 2 == 1]
    small_parts = dict(
        norm_mix=jnp.concatenate([lg[i]["norm_mix"] for i in range(depth)]),
        norm_ffn=jnp.concatenate([lg[i]["norm_ffn"] for i in range(depth)]),
        gla_b_gate=jnp.concatenate([lg[i]["gla_b_gate"] for i in gla_l]),
        gla_norm=jnp.concatenate([lg[i]["gla_norm"] for i in gla_l]),
        fox_b_f=jnp.concatenate([lg[i]["fox_b_f"] for i in fox_l]),
        fox_q_norm=jnp.concatenate([lg[i]["fox_q_norm"] for i in fox_l]),
        fox_k_norm=jnp.concatenate([lg[i]["fox_k_norm"] for i in fox_l]),
        ffn_conv_b=jnp.concatenate([lg[i]["conv_b"] for i in range(depth)]),
        norm_final=d_norm_final,
        gla_w_gate=jnp.stack([lg[i]["gla_w_gate"] for i in gla_l]),
        ffn_conv_w=jnp.stack([lg[i]["conv_w"] for i in range(depth)]),
        loss=loss_part[:, :1],
    )
    order = ("norm_mix", "norm_ffn", "gla_b_gate", "gla_norm", "fox_b_f", "fox_q_norm", "fox_k_norm", "ffn_conv_b",
             "norm_final", "gla_w_gate", "ffn_conv_w", "loss")
    packed = _pack([small_parts[nm] for nm in order])
    dmod = jnp.stack([jnp.concatenate([lg[i][nm] for nm in ("sh_m", "sc_m", "g_m", "sh_f", "sc_f", "g_f")], axis=1)
                      for i in range(depth)])
    hs_small, tok_small = _split_start([packed[None], dmod[None]], _all_plan, "gather_small_grads_start",
                                       land_blocks=N_DEV)
    dw_main, da_tail, h1_gla = lg[0]["gla_w_in_unsent"]
    dw_in_t = _tail_rows(da_tail + tok_small.astype(BF16), h1_gla, dw_main, rank, "gla_in_tail_dw")
    send("gla_in", [dw_in_t.reshape(N_DEV, -1, d)])
    started = pending[-1][1][0][0]

    received = {}

    def arrive(key, after):
        sums, lands = _split_wait(sent[key], _chip_plan, after, f"scatter_{key}_chip_wait")
        received[key] = list(zip(lands, sums))

    for key in ("ffn1", "fox", "ffn0", "gla_out"):
        arrive(key, started)

    out_g, out_d, out_m, out_v = {}, {}, {}, {}

    chip_idx = chip.astype(jnp.int32).reshape(1)

    def update(nm, g_arr, transposed=False):
        swap = (lambda t: jnp.transpose(t, (0, 2, 1))) if transposed else (lambda t: t)
        if isinstance(g_arr, list):
            res = _adamw_pieces(swap(w[nm]), [t[0] for t in g_arr], [t[1] for t in g_arr], chip_idx,
                                swap(mom_m[nm]), swap(mom_v[nm]), f"adamw_{nm}")
        else:
            res = _adamw(w[nm], g_arr, mom_m[nm], mom_v[nm], f"adamw_{nm}")
        out_g[nm], out_d[nm], out_m[nm], out_v[nm] = (swap(t) for t in res)

    update("gla_w_out", [received["gla_out"][0]])
    update("fox_w_out", [received["fox"][1]])
    tok_flush = flush(out_g["fox_w_out"])
    update("ffn_w_up", [received[f"ffn{i}"][0] for i in range(depth)])
    update("fox_w_in", [received["fox"][0]], transposed=True)
    update("ffn_w_down", [received[f"ffn{i}"][1] for i in range(depth)])

    updated = ("gla_w_out", "fox_w_in", "fox_w_out", "ffn_w_up", "ffn_w_down")
    (packed_mine, dmod_mine), (packed_all, dmod_all) = _split_wait(
        hs_small, _all_plan, [out_d[nm] for nm in updated], "gather_small_grads_wait")
    packed_all = lax.dynamic_update_slice(packed_all, packed_mine + tok_flush, (me, 0, 0))
    dmod_all = lax.dynamic_update_slice(dmod_all, dmod_mine, (me, 0, 0, 0))
    summed = _unpack(_sum8(packed_all, "sum_small_grads"), [small_parts[nm].shape for nm in order])
    small_g = dict(zip(order, summed))
    loss = small_g["loss"][0, 0]
    dmod_all = dmod_all[:, :, 0, :]
    grads = {}
    cond_t = _pad_cols(jnp.transpose(cond_all)).astype(BF16)
    dmod_cols = lax.dynamic_slice(dmod_all, (0, 0, me * mod_cols), (N_DEV, depth, mod_cols))
    g_w_mod = lax.empty(w_mod.shape, F32)
    for i in range(depth):
        rhs = jnp.pad(dmod_cols[:, i], ((0, LANE - N_DEV), (0, 0)))
        g_w_mod = _matmul(cond_t, rhs, name=f"mod_dw_{i}", tn=768, into=(g_w_mod, i))
    grads["w_mod"] = g_w_mod
    small_g["b_mod"] = _sum8(dmod_all.reshape(N_DEV, 1, -1), "sum_b_mod").reshape(depth, -1)
    update("w_mod", grads["w_mod"])

    gate_cols = gla_w_gate.shape[2]
    conv_cols = ffn_conv_w.shape[2]
    local_small = dict(small_g)
    local_small["gla_w_gate"] = lax.dynamic_slice_in_dim(small_g["gla_w_gate"], me * gate_cols, gate_cols, axis=2)
    local_small["ffn_conv_w"] = lax.dynamic_slice_in_dim(small_g["ffn_conv_w"], me * conv_cols, conv_cols, axis=2)
    names = SMALL + SMALL_SHARDED
    shapes = [w[nm].shape for nm in names]
    res = _adamw(_pack([w[nm] for nm in names])[None], _pack([local_small[nm] for nm in names])[None],
                 _pack([mom_m[nm] for nm in names])[None], _pack([mom_v[nm] for nm in names])[None], "adamw_small")
    for tgt, flat in zip((out_g, out_d, out_m, out_v), res):
        for nm, arr in zip(names, _unpack(flat[0], shapes)):
            tgt[nm] = arr

    arrive("gla_in", [out_d[nm] for nm in updated + ("w_mod",)])
    update("gla_w_in", [received["gla_in"][0]], transposed=True)

    return (loss, grad_x, *[out_g[n] for n in WEIGHTS], *[out_d[n] for n in WEIGHTS],
            *[out_m[n] for n in WEIGHTS], *[out_v[n] for n in WEIGHTS])
```

```python
import jax
import jax.numpy as jnp
from jax import lax
from jax.experimental import pallas as pl
from jax.experimental.pallas import tpu as pltpu

F32, BF16 = jnp.float32, jnp.bfloat16
N_DEV = 8
GLA_HEADS = 4
GLA_TAU = 16.0
GLA_CHUNK = 64
NORM_EPS = 1e-6
ADAM_LR, ADAM_B1, ADAM_B2, ADAM_EPS, ADAM_WD, ADAM_STEP = 0.001, 0.9, 0.999, 1e-08, 0.01, 10
LANE = 128
VMEM_LIMIT = 56 * 1024 * 1024
NEG = -1e30


def _pcall(body, **kw):
    return pl.pallas_call(body, **kw)


def _params(n_axes):
    return pltpu.CompilerParams(dimension_semantics=("arbitrary",) * n_axes, vmem_limit_bytes=VMEM_LIMIT)


def _tile(dim, pref):
    if dim <= pref:
        return dim
    t = pref
    while dim % t:
        t -= LANE
    assert t > 0, (dim, pref)
    return t


def _dot(a, b, ta=False, tb=False):
    dims = (((0,) if ta else (1,), (1,) if tb else (0,)), ((), ()))
    return lax.dot_general(a.astype(BF16), b.astype(BF16), dims, preferred_element_type=F32)


def _split3(x):
    hi = x.astype(BF16)
    r1 = x - hi.astype(F32)
    mid = r1.astype(BF16)
    lo = (r1 - mid.astype(F32)).astype(BF16)
    return hi, mid, lo


def _tri_matmul(tri, x):
    hi, mid, lo = _split3(x)
    return _dot(tri, hi) + _dot(tri, mid) + _dot(tri, lo)


def _tri(n, upper=False):
    r = lax.broadcasted_iota(jnp.int32, (n, n), 0)
    c = lax.broadcasted_iota(jnp.int32, (n, n), 1)
    return jnp.where((r <= c) if upper else (r >= c), 1.0, 0.0).astype(BF16)


def _log_sigmoid(x):
    return jnp.minimum(x, 0.0) - jnp.log(1.0 + jnp.exp(-jnp.abs(x)))


def _sigmoid(x):
    return 1.0 / (1.0 + jnp.exp(-x))


def _silu(x):
    return x * _sigmoid(x)


def _dsilu(x):
    s = _sigmoid(x)
    return s * (1.0 + x * (1.0 - s))


def _matmul(a, b, *, name, ta=False, tb=False, out_dtypes=(F32,), tm=1024, tn=1024, tk=2048,
            epilogue=None, extras=(), a_halves=False, b_halves=False, b_shards=False, out_shards=False,
            b_rows=None, out_rows=None, b_layer=None, into=None):
    if a_halves:
        assert not ta
        m, k = a.shape[1], 2 * a.shape[2]
    else:
        m, k = (a.shape[1], a.shape[0]) if ta else a.shape
    if b_halves:
        assert not tb and b.shape[1] == k
        n = 2 * b.shape[2]
    elif b_shards:
        n = b.shape[1] if tb else N_DEV * b.shape[2]
        assert (N_DEV * b.shape[2] if tb else b.shape[1]) == k, (a.shape, b.shape, ta, tb)
    elif b_layer is not None:
        assert not tb and b.shape[1] == k
        n = b.shape[2]
    else:
        rows = b.shape[0] if b_rows is None else b_rows
        n = rows if tb else b.shape[1]
        assert (b.shape[1] if tb else rows) == k, (a.shape, b.shape, ta, tb)
    n_unit = n // N_DEV if (out_shards or (b_shards and not tb)) else (n // 2 if b_halves else n)
    k_unit = k // N_DEV if (b_shards and tb) else (k // 2 if a_halves else k)
    tm, tn, tk = _tile(m, tm), _tile(n_unit, tn), _tile(k_unit, tk)
    nk = k // tk
    if a_halves:
        a_spec = pl.BlockSpec((None, tm, tk), lambda i, j, kk: (kk // (nk // 2), i, kk % (nk // 2)))
    elif ta:
        a_spec = pl.BlockSpec((tk, tm), lambda i, j, kk: (kk, i))
    else:
        a_spec = pl.BlockSpec((tm, tk), lambda i, j, kk: (i, kk))
    n_per, k_per = n // tn // N_DEV, nk // N_DEV
    if b_halves:
        b_spec = pl.BlockSpec((None, tk, tn), lambda i, j, kk: (j // (n // tn // 2), kk, j % (n // tn // 2)))
    elif b_shards and tb:
        b_spec = pl.BlockSpec((None, tn, tk), lambda i, j, kk: (kk // k_per, j, kk % k_per))
    elif b_shards:
        b_spec = pl.BlockSpec((None, tk, tn), lambda i, j, kk: (j // n_per, kk, j % n_per))
    elif b_layer is not None:
        b_spec = pl.BlockSpec((None, tk, tn), lambda i, j, kk: (b_layer, kk, j))
    elif tb:
        b_spec = pl.BlockSpec((tn, tk), lambda i, j, kk: (j, kk))
    else:
        b_spec = pl.BlockSpec((tk, tn), lambda i, j, kk: (kk, j))
    ex_specs = []
    for kind, arr in extras:
        if kind == "mn":
            assert arr.shape == (m, n), (arr.shape, m, n)
            ex_specs.append(pl.BlockSpec((tm, tn), lambda i, j, kk: (i, j)))
        else:
            assert arr.shape == (1, n), (arr.shape, n)
            ex_specs.append(pl.BlockSpec((1, tn), lambda i, j, kk: (0, j)))
    n_ex, n_out = len(extras), len(out_dtypes)

    def body(a_ref, b_ref, *rest):
        ex, outs, acc = rest[:n_ex], rest[-1 - n_out:-1], rest[-1]
        kk = pl.program_id(2)

        @pl.when(kk == 0)
        def _():
            acc[...] = jnp.zeros_like(acc)

        acc[...] += _dot(a_ref[...], b_ref[...], ta, tb)

        @pl.when(kk == nk - 1)
        def _():
            if epilogue is None:
                vals = (acc[...],)
            else:
                vals = epilogue(acc[...], *[e[...] for e in ex])
            for o, v in zip(outs, vals):
                o[...] = v.astype(o.dtype)

    if out_shards:
        out_spec = pl.BlockSpec((None, tm, tn), lambda i, j, kk: (j // n_per, i, j % n_per))
        out_dims = (N_DEV, m, n // N_DEV)
    elif into is not None:
        out_spec = pl.BlockSpec((None, tm, tn), lambda i, j, kk: (into[1], i, j))
        out_dims = into[0].shape
    else:
        out_spec = pl.BlockSpec((tm, tn), lambda i, j, kk: (i, j))
        out_dims = (m if out_rows is None else out_rows, n)
    operands = [a, b, *[arr for _, arr in extras]]
    aliases = {}
    if into is not None:
        assert n_out == 1 and into[0].shape[1:] == (m, n) and into[0].dtype == out_dtypes[0]
        aliases = {len(operands): 0}
        operands.append(into[0])
    res = _pcall(
        body, name=name, grid=(m // tm, n // tn, nk),
        in_specs=[a_spec, b_spec] + ex_specs + [pl.BlockSpec(memory_space=pl.ANY)] * len(aliases),
        out_specs=[out_spec] * n_out,
        out_shape=[jax.ShapeDtypeStruct(out_dims, d) for d in out_dtypes],
        scratch_shapes=[pltpu.VMEM((tm, tn), F32)],
        input_output_aliases=aliases,
        compiler_params=_params(3),
    )(*operands)
    return res[0] if n_out == 1 else res


def _tail_rows(a, b, into, rows, name, tn=1024):
    k, n = b.shape
    m_total = into.shape[0]
    tn = _tile(n, tn)

    def body(a_ref, b_ref, into_ref, out_ref):
        out_ref[...] = _dot(a_ref[...], b_ref[...], ta=True)[:rows].astype(out_ref.dtype)

    return _pcall(
        body, name=name, grid=(n // tn,),
        in_specs=[pl.BlockSpec((k, a.shape[1]), lambda j: (0, 0)), pl.BlockSpec((k, tn), lambda j: (0, j)),
                  pl.BlockSpec(memory_space=pl.ANY)],
        out_specs=pl.BlockSpec((rows, tn), lambda j: (m_total // rows - 1, j)),
        out_shape=jax.ShapeDtypeStruct(into.shape, into.dtype),
        input_output_aliases={2: 0}, compiler_params=_params(1),
    )(a, b, into)


def _rowwise(fn, ins, outs, *, name, tr=128, into=None):
    rows = next(e[1].shape[0] for e in ins if e[0] != "full")
    tr = _tile(rows, tr)
    in_specs = []
    for entry in ins:
        kind, arr = entry[0], entry[1]
        assert kind == "full" or (arr.shape[0] == rows and arr.ndim == 2)
        if kind == "row":
            in_specs.append(pl.BlockSpec((tr, arr.shape[1]), lambda i: (i, 0)))
        elif kind == "cols":
            in_specs.append(pl.BlockSpec((tr, entry[3]), lambda i, cb=entry[2]: (i, cb)))
        else:
            in_specs.append(pl.BlockSpec(arr.shape, lambda i, nd=arr.ndim: (0,) * nd))
    out_specs, out_shape = [], []
    for entry in outs:
        kind, w, dt = entry[:3]
        if kind == "row":
            out_specs.append(pl.BlockSpec((tr, w), lambda i: (i, 0)))
            out_shape.append(jax.ShapeDtypeStruct((rows, w), dt))
        elif kind == "band":
            out_specs.append(pl.BlockSpec((tr, w), lambda i, cb=entry[3]: (i, cb)))
            out_shape.append(jax.ShapeDtypeStruct((rows, entry[4]), dt))
        else:
            out_specs.append(pl.BlockSpec((1, w), lambda i: (0, 0)))
            out_shape.append(jax.ShapeDtypeStruct((1, w), dt))
    n_in = len(ins)
    operands = [e[1] for e in ins]
    aliases = {}
    if into is not None:
        aliases = {len(operands): into[1]}
        in_specs.append(pl.BlockSpec(memory_space=pl.ANY))
        operands.append(into[0])

    def body(*refs):
        i = pl.program_id(0)
        vals = fn(*[r[...] for r in refs[:n_in]])
        for entry, o, v in zip(outs, refs[len(operands):], vals):
            if entry[0] == "acc":
                @pl.when(i == 0)
                def _(o=o):
                    o[...] = jnp.zeros_like(o)

                o[...] += v.astype(o.dtype)
            else:
                o[...] = v.astype(o.dtype)

    return _pcall(body, name=name, grid=(rows // tr,), in_specs=in_specs, out_specs=out_specs,
                  out_shape=out_shape, input_output_aliases=aliases, compiler_params=_params(1))(*operands)


def _colsum(x):
    return jnp.sum(x, axis=0, keepdims=True)


def _norm_stats(x):
    rstd = lax.rsqrt(jnp.mean(x * x, axis=-1, keepdims=True) + NORM_EPS)
    return x * rstd, rstd


def _norm_bwd(dxhat, xhat, rstd):
    return rstd * (dxhat - xhat * jnp.mean(dxhat * xhat, axis=-1, keepdims=True))


def _adaln_fwd(x, gain, sc, sh, name):
    def fn(x, gain, sc, sh):
        xhat, _ = _norm_stats(x)
        return ((xhat * gain) * (1.0 + sc) + sh,)

    return _rowwise(fn, [("row", x), ("full", gain), ("full", sc), ("full", sh)],
                    [("row", x.shape[1], BF16)], name=name)[0]


def _adaln_bwd(x, dh, dres, gain, sc, name, branch=None):
    d = x.shape[1]

    def fn(x, dh, dres, gain, sc, *br):
        xhat, rstd = _norm_stats(x)
        dxhat = dh * (gain * (1.0 + sc))
        dx = dres + _norm_bwd(dxhat, xhat, rstd)
        return (dx, _colsum(dh), _colsum(dh * (xhat * gain)), _colsum(dh * xhat * (1.0 + sc))) + _branch_bwd(dx, *br)

    return _rowwise(fn, [("row", x), ("row", dh), ("row", dres), ("full", gain), ("full", sc)] + _branch_ins(branch),
                    [("row", d, F32), ("acc", d, F32), ("acc", d, F32), ("acc", d, F32)] + _branch_outs(branch, d),
                    name=name)


def _branch_ins(branch):
    return [] if branch is None else [("row", branch[0]), ("full", branch[1])]


def _branch_outs(branch, d):
    return [] if branch is None else [("row", d, BF16), ("acc", d, F32)]


def _branch_bwd(dx, *branch):
    if not branch:
        return ()
    y, g = branch
    return dx * (1.0 + g), _colsum(dx * y)


def _final_loss(x, target, gain, name, branch):
    d = x.shape[1]

    def fn(x, t, gain, *br):
        xhat, rstd = _norm_stats(x)
        err = xhat * gain - t
        dy = err * (1.0 / d)
        loss = 0.5 * jnp.sum(jnp.mean(err * err, axis=-1, keepdims=True), axis=0, keepdims=True)
        dx = _norm_bwd(dy * gain, xhat, rstd)
        return (dx, _colsum(dy * xhat), jnp.broadcast_to(loss, (1, LANE))) + _branch_bwd(dx, *br)

    return _rowwise(fn, [("row", x), ("row", target), ("full", gain)] + _branch_ins(branch),
                    [("row", d, F32), ("acc", d, F32), ("acc", LANE, F32)] + _branch_outs(branch, d), name=name)


def _gla_gates(q, k, a, wg, bg, scale, c):
    ga = _dot(a, wg) + bg
    la = _log_sigmoid(ga) * (1.0 / GLA_TAU)
    b = _tri_matmul(_tri(c), la)
    bl = _colsum(la)
    eb, enb, eend = jnp.exp(b), jnp.exp(-b), jnp.exp(bl - b)
    q = q * scale
    return dict(ga=ga, eb=eb, enb=enb, eend=eend, dec=jnp.exp(bl), q_dec=q * eb, k_inv=k * enb, k_end=k * eend)


def _causal(c):
    return lax.broadcasted_iota(jnp.int32, (c, c), 0) >= lax.broadcasted_iota(jnp.int32, (c, c), 1)


def _gla_specs(heads, c, dk, dv, chunk):
    return [
        pl.BlockSpec((c, heads * dk), lambda n: (chunk(n), 0)),
        pl.BlockSpec((c, heads * dk), lambda n: (chunk(n), 1)),
        pl.BlockSpec((c, heads * dv), lambda n: (chunk(n), 1)),
        pl.BlockSpec((c, LANE), lambda n: (chunk(n), 0)),
        pl.BlockSpec((LANE, heads * dk), lambda n: (0, 0)),
        pl.BlockSpec((1, heads * dk), lambda n: (0, 0)),
    ]


def _gla_fwd(proj, a_tail, wg_p, bg, name):
    s = proj.shape[0]
    heads, c = GLA_HEADS, GLA_CHUNK
    dk = wg_p.shape[1] // heads
    dv = 2 * dk
    n_chunks = s // c
    scale = dk ** -0.5

    def body(q_ref, k_ref, v_ref, a_ref, wg_ref, bg_ref, o_ref, st_ref, state):
        @pl.when(pl.program_id(0) == 0)
        def _():
            state[...] = jnp.zeros_like(state)

        a = a_ref[...]
        for h in range(heads):
            sk, sv = slice(h * dk, (h + 1) * dk), slice(h * dv, (h + 1) * dv)
            g = _gla_gates(q_ref[:, sk], k_ref[:, sk], a, wg_ref[:, sk], bg_ref[:, sk], scale, c)
            v = v_ref[:, sv]
            st = state[h]
            attn = jnp.where(_causal(c), _dot(g["q_dec"], g["k_inv"], tb=True), 0.0)
            o_ref[:, sv] = _dot(attn, v) + _dot(g["q_dec"], st, tb=True)
            st_ref[h] = st.astype(st_ref.dtype)
            state[h] = g["dec"] * st + _dot(v, g["k_end"], ta=True)

    return _pcall(
        body, name=name, grid=(n_chunks,),
        in_specs=_gla_specs(heads, c, dk, dv, lambda n: n),
        out_specs=[pl.BlockSpec((c, heads * dv), lambda n: (n, 0)),
                   pl.BlockSpec((heads, None, dv, dk), lambda n: (0, n, 0, 0))],
        out_shape=[jax.ShapeDtypeStruct((s, heads * dv), F32),
                   jax.ShapeDtypeStruct((heads, n_chunks, dv, dk), BF16)],
        scratch_shapes=[pltpu.VMEM((heads, dv, dk), F32)],
        compiler_params=_params(1),
    )(proj, proj, proj, a_tail, wg_p, bg)


def _gla_bwd(proj, a_tail, wg_p, bg, states, d_o, dproj, name):
    s = proj.shape[0]
    heads, c = GLA_HEADS, GLA_CHUNK
    dk = wg_p.shape[1] // heads
    dv = 2 * dk
    n_chunks = s // c
    scale = dk ** -0.5
    k0, v0 = heads * dk, 2 * heads * dk

    def body(q_ref, k_ref, v_ref, a_ref, wg_ref, bg_ref, st_ref, do_ref, dproj_in, dqkv_ref, dga_ref, dstate):
        @pl.when(pl.program_id(0) == 0)
        def _():
            dstate[...] = jnp.zeros_like(dstate)

        a = a_ref[...]
        mask = _causal(c)
        for h in range(heads):
            sk, sv = slice(h * dk, (h + 1) * dk), slice(h * dv, (h + 1) * dv)
            out_k, out_v = slice(k0 + h * dk, k0 + (h + 1) * dk), slice(v0 + h * dv, v0 + (h + 1) * dv)
            g = _gla_gates(q_ref[:, sk], k_ref[:, sk], a, wg_ref[:, sk], bg_ref[:, sk], scale, c)
            v, st, dst, d_out = v_ref[:, sv], st_ref[h], dstate[h], do_ref[:, sv]
            q_dec, k_inv, k_end = g["q_dec"], g["k_inv"], g["k_end"]
            attn = jnp.where(mask, _dot(q_dec, k_inv, tb=True), 0.0)
            d_attn = jnp.where(mask, _dot(d_out, v, tb=True), 0.0)
            d_qdec = _dot(d_attn, k_inv) + _dot(d_out, st)
            d_kinv = _dot(d_attn, q_dec, ta=True)
            d_kend = _dot(v, dst)
            dqkv_ref[:, out_v] = (_dot(attn, d_out, ta=True) + _dot(k_end, dst, tb=True)).astype(dqkv_ref.dtype)
            d_dec = jnp.sum(dst * st.astype(F32), axis=0, keepdims=True)
            dstate[h] = g["dec"] * dst + _dot(d_out, q_dec, ta=True)

            dqkv_ref[:, sk] = (d_qdec * (scale * g["eb"])).astype(dqkv_ref.dtype)
            dqkv_ref[:, out_k] = (d_kinv * g["enb"] + d_kend * g["eend"]).astype(dqkv_ref.dtype)
            kk = d_kend * k_end
            db = d_qdec * q_dec - d_kinv * k_inv - kk
            dbl = jnp.sum(kk, axis=0, keepdims=True) + d_dec * g["dec"]
            last = lax.broadcasted_iota(jnp.int32, db.shape, 0) == c - 1
            db = db + jnp.where(last, dbl, 0.0)
            dla = _tri_matmul(_tri(c, upper=True), db)
            dga_ref[:, sk] = dla * (1.0 / GLA_TAU) * _sigmoid(-g["ga"])

    chunk = lambda n: n_chunks - 1 - n
    rev = lambda n: (chunk(n), 0)
    return _pcall(
        body, name=name, grid=(n_chunks,),
        in_specs=_gla_specs(heads, c, dk, dv, chunk) + [
            pl.BlockSpec((heads, None, dv, dk), lambda n: (0, chunk(n), 0, 0)),
            pl.BlockSpec((c, heads * dv), rev), pl.BlockSpec(memory_space=pl.ANY)],
        out_specs=[pl.BlockSpec((c, v0 + heads * dv), rev), pl.BlockSpec((c, heads * dk), rev)],
        out_shape=[jax.ShapeDtypeStruct(dproj.shape, dproj.dtype), jax.ShapeDtypeStruct((s, heads * dk), F32)],
        scratch_shapes=[pltpu.VMEM((heads, dv, dk), F32)],
        input_output_aliases={8: 0},
        compiler_params=_params(1),
    )(proj, proj, proj, a_tail, wg_p, bg, states, d_o, dproj)


def _gla_post_fwd(o, r, gn, name):
    dvt = o.shape[1]
    dv = dvt // GLA_HEADS

    def fn(o, r, gn):
        outs = []
        for h in range(GLA_HEADS):
            sl = slice(h * dv, (h + 1) * dv)
            ohat, _ = _norm_stats(o[:, sl])
            outs.append((ohat * gn[:, sl]) * _silu(r[:, sl]))
        return (jnp.concatenate(outs, axis=1),)

    return _rowwise(fn, [("row", o), r, ("full", gn)], [("row", dvt, BF16)], name=name)[0]


def _gla_post_bwd(o, r, gn, dog, name):
    dvt = o.shape[1]
    dv = dvt // GLA_HEADS

    def fn(o, r, gn, dog):
        d_o, d_r, d_g = [], [], []
        for h in range(GLA_HEADS):
            sl = slice(h * dv, (h + 1) * dv)
            ohat, rstd = _norm_stats(o[:, sl])
            g, rr, dd = gn[:, sl], r[:, sl], dog[:, sl]
            d_r.append(dd * (ohat * g) * _dsilu(rr))
            don = dd * _silu(rr)
            d_g.append(_colsum(don * ohat))
            d_o.append(_norm_bwd(don * g, ohat, rstd))
        return jnp.concatenate(d_o, axis=1), jnp.concatenate(d_r, axis=1), jnp.concatenate(d_g, axis=1)

    return _rowwise(fn, [("row", o), r, ("full", gn), ("row", dog)],
                    [("row", dvt, F32), ("band", dvt, BF16, 2, 3 * dvt), ("acc", dvt, F32)], name=name)


def _fox_prep(q, k, v, qg, kg, d, hd, name):
    heads = d // hd
    scale = hd ** -0.5

    def fn(q, k, v, qg, kg):
        qs, ks = [], []
        for h in range(heads):
            sl = slice(h * hd, (h + 1) * hd)
            qs.append(_norm_stats(q[:, sl])[0] * qg * scale)
            ks.append(_norm_stats(k[:, sl])[0] * kg)
        return jnp.concatenate(qs, axis=1), jnp.concatenate(ks, axis=1), v

    return _rowwise(fn, [q, k, v, ("full", qg), ("full", kg)],
                    [("row", d, BF16)] * 3, name=name)


def _fox_prep_bwd(q, k, dqn, dkn, qg, kg, hd, dproj, name):
    d = dqn.shape[1]
    heads = d // hd
    scale = hd ** -0.5

    def fn(q, k, dqn, dkn, qg, kg):
        dq, dk, gq, gk = [], [], [], []
        for h in range(heads):
            sl = slice(h * hd, (h + 1) * hd)
            for x, dxn, g, s, dl, gl in ((q, dqn, qg, scale, dq, gq), (k, dkn, kg, 1.0, dk, gk)):
                xhat, rstd = _norm_stats(x[:, sl])
                dn = dxn[:, sl] * s
                gl.append(_colsum(dn * xhat))
                dl.append(_norm_bwd(dn * g, xhat, rstd))
        cat = lambda t: jnp.concatenate(t, axis=1)
        return cat(dq + dk), cat(gq), cat(gk)

    return _rowwise(fn, [q, k, ("row", dqn), ("row", dkn), ("full", qg), ("full", kg)],
                    [("band", 2 * d, BF16, 0, 4 * d), ("acc", d, F32), ("acc", d, F32)], name=name, into=(dproj, 0))


def _fox_cum(fl, bf_p, name, tb=256):
    s = fl.shape[0]
    tb = _tile(s, tb)

    def body(fl_ref, bf_ref, cum_ref, carry):
        @pl.when(pl.program_id(0) == 0)
        def _():
            carry[...] = jnp.zeros_like(carry)

        lf = _log_sigmoid(fl_ref[...] + bf_ref[...])
        cum_ref[...] = _tri_matmul(_tri(tb), lf) + carry[...]
        carry[...] += _colsum(lf)

    return _pcall(
        body, name=name, grid=(s // tb,),
        in_specs=[pl.BlockSpec((tb, LANE), lambda i: (i, 0)), pl.BlockSpec((1, LANE), lambda i: (0, 0))],
        out_specs=pl.BlockSpec((tb, LANE), lambda i: (i, 0)),
        out_shape=jax.ShapeDtypeStruct((s, LANE), F32),
        scratch_shapes=[pltpu.VMEM((1, LANE), F32)],
        compiler_params=_params(1),
    )(fl, bf_p)


def _fox_cum_bwd(dcum, fl, bf_p, name, tb=256):
    s = fl.shape[0]
    tb = _tile(s, tb)
    nb = s // tb

    def body(dc_ref, fl_ref, bf_ref, dfl_ref, dbf_ref, carry):
        @pl.when(pl.program_id(0) == 0)
        def _():
            carry[...] = jnp.zeros_like(carry)
            dbf_ref[...] = jnp.zeros_like(dbf_ref)

        dc = dc_ref[...]
        dlf = _tri_matmul(_tri(tb, upper=True), dc) + carry[...]
        carry[...] += _colsum(dc)
        dfl = dlf * _sigmoid(-(fl_ref[...] + bf_ref[...]))
        dfl_ref[...] = dfl
        dbf_ref[...] += _colsum(dfl)

    rev = lambda i: (nb - 1 - i, 0)
    return _pcall(
        body, name=name, grid=(nb,),
        in_specs=[pl.BlockSpec((tb, LANE), rev), pl.BlockSpec((tb, LANE), rev), pl.BlockSpec((1, LANE), lambda i: (0, 0))],
        out_specs=[pl.BlockSpec((tb, LANE), rev), pl.BlockSpec((1, LANE), lambda i: (0, 0))],
        out_shape=[jax.ShapeDtypeStruct((s, LANE), F32), jax.ShapeDtypeStruct((1, LANE), F32)],
        scratch_shapes=[pltpu.VMEM((1, LANE), F32)],
        compiler_params=_params(1),
    )(dcum, fl, bf_p)


def _fox_attn_fwd(qn, kn, vb, cum_col, cum_row, hd, t, name):
    s, d = qn.shape
    heads = d // hd
    nq = s // t

    def body(q_ref, k_ref, v_ref, cc_ref, cr_ref, o_ref, lse_ref):
        qi = pl.program_id(1)
        q = q_ref[...]
        cq = cc_ref[...]
        qpos = qi * t + lax.broadcasted_iota(jnp.int32, (t, 1), 0)

        def step(kj, carry, diagonal=False):
            m, l, acc = carry
            off = pl.multiple_of(kj * t, t)
            ks, vs = k_ref[pl.ds(off, t), :], v_ref[pl.ds(off, t), :]
            sc = _dot(q, ks, tb=True) + cq - cr_ref[kj]
            if diagonal:
                kpos = off + lax.broadcasted_iota(jnp.int32, (1, t), 1)
                sc = jnp.where(kpos <= qpos, sc, NEG)
            m_new = jnp.maximum(m, jnp.max(sc, axis=1, keepdims=True))
            alpha = jnp.exp(m - m_new)
            p = jnp.exp(sc - m_new)
            return m_new, alpha * l + jnp.sum(p, axis=1, keepdims=True), alpha * acc + _dot(p, vs)

        init = (jnp.full((t, 1), NEG, F32), jnp.zeros((t, 1), F32), jnp.zeros((t, hd), F32))
        m, l, acc = step(qi, lax.fori_loop(0, qi, step, init), diagonal=True)
        o_ref[...] = acc / l
        lse_ref[...] = m + jnp.log(l)

    return _pcall(
        body, name=name, grid=(heads, nq),
        in_specs=[pl.BlockSpec((t, hd), lambda h, i: (i, h)),
                  pl.BlockSpec((s, hd), lambda h, i: (0, h)),
                  pl.BlockSpec((s, hd), lambda h, i: (0, h)),
                  pl.BlockSpec((None, t, 1), lambda h, i: (h, i, 0)),
                  pl.BlockSpec((None, nq, 1, t), lambda h, i: (h, 0, 0, 0))],
        out_specs=[pl.BlockSpec((t, hd), lambda h, i: (i, h)), pl.BlockSpec((None, t, 1), lambda h, i: (h, i, 0))],
        out_shape=[jax.ShapeDtypeStruct((s, d), F32), jax.ShapeDtypeStruct((heads, s, 1), F32)],
        compiler_params=_params(2),
    )(qn, kn, vb, cum_col, cum_row)


def _fox_attn_bwd(qn, kn, vb, d_o, o, lse, cum_col, cum_row, hd, t, dproj, name):
    s, d = qn.shape
    heads = d // hd
    nq = s // t

    def body(q_ref, k_ref, v_ref, do_ref, o_ref, lse_ref, cc_ref, cr_ref, dproj_in,
             dq_ref, dk_ref, dv_ref, dcq_ref, dck_ref, delta):
        kj = pl.program_id(1)

        @pl.when(kj == 0)
        def _():
            dq_ref[...] = jnp.zeros_like(dq_ref)
            dcq_ref[...] = jnp.zeros_like(dcq_ref)
            delta[...] = jnp.sum(do_ref[...] * o_ref[...], axis=1, keepdims=True)

        ks, vs, cr = k_ref[...], v_ref[...], cr_ref[...]
        kpos = kj * t + lax.broadcasted_iota(jnp.int32, (1, t), 1)

        def step(qi, carry, diagonal=False):
            dk, dv, dck = carry
            rows = pl.ds(pl.multiple_of(qi * t, t), t)
            q, d_out = q_ref[rows, :], do_ref[rows, :]
            sc = _dot(q, ks, tb=True) + cc_ref[rows, :] - cr
            p = jnp.exp(sc - lse_ref[rows, :])
            if diagonal:
                qpos = qi * t + lax.broadcasted_iota(jnp.int32, (t, 1), 0)
                p = jnp.where(kpos <= qpos, p, 0.0)
            ds = p * (_dot(d_out, vs, tb=True) - delta[rows, :])
            dq_ref[rows, :] += _dot(ds, ks)
            dcq_ref[rows, :] += jnp.sum(ds, axis=1, keepdims=True)
            return dk + _dot(ds, q, ta=True), dv + _dot(p, d_out, ta=True), dck + _colsum(ds)

        init = (jnp.zeros((t, hd), F32), jnp.zeros((t, hd), F32), jnp.zeros((1, t), F32))
        dk, dv, dck = lax.fori_loop(kj + 1, nq, step, step(kj, init, diagonal=True))
        dk_ref[...] = dk.astype(dk_ref.dtype)
        dv_ref[...] = dv.astype(dv_ref.dtype)
        dck_ref[...] = dck

    head_rows = lambda h, j: (0, h)
    blk = lambda h, j: (j, h)
    return _pcall(
        body, name=name, grid=(heads, nq),
        in_specs=[pl.BlockSpec((s, hd), head_rows), pl.BlockSpec((t, hd), blk), pl.BlockSpec((t, hd), blk),
                  pl.BlockSpec((s, hd), head_rows), pl.BlockSpec((s, hd), head_rows),
                  pl.BlockSpec((None, s, 1), lambda h, j: (h, 0, 0)),
                  pl.BlockSpec((None, s, 1), lambda h, j: (h, 0, 0)),
                  pl.BlockSpec((None, None, 1, t), lambda h, j: (h, j, 0, 0)),
                  pl.BlockSpec(memory_space=pl.ANY)],
        out_specs=[pl.BlockSpec((s, hd), head_rows), pl.BlockSpec((t, hd), blk),
                   pl.BlockSpec((t, hd), lambda h, j: (j, 2 * heads + h)),
                   pl.BlockSpec((None, s, 1), lambda h, j: (h, 0, 0)),
                   pl.BlockSpec((None, None, 1, t), lambda h, j: (h, j, 0, 0))],
        out_shape=[jax.ShapeDtypeStruct((s, d), F32), jax.ShapeDtypeStruct((s, d), BF16),
                   jax.ShapeDtypeStruct(dproj.shape, dproj.dtype), jax.ShapeDtypeStruct((heads, s, 1), F32),
                   jax.ShapeDtypeStruct((heads, nq, 1, t), F32)],
        scratch_shapes=[pltpu.VMEM((s, 1), F32)],
        input_output_aliases={8: 2},
        compiler_params=_params(2),
    )(qn, kn, vb, d_o, o, lse, cum_col, cum_row, dproj)


def _fox_gate_fwd(o, og, name):
    def fn(o, og):
        return (o * _sigmoid(og),)

    return _rowwise(fn, [("row", o), og], [("row", o.shape[1], BF16)], name=name)[0]


def _fox_gate_bwd(o, og, dact, name):
    def fn(o, og, dact):
        sg = _sigmoid(og)
        return dact * sg, dact * o * sg * (1.0 - sg)

    d = o.shape[1]
    return _rowwise(fn, [("row", o), og, ("row", dact)], [("row", d, F32), ("band", d, BF16, 3, 4 * d)], name=name)


def _shift_down(x, n):
    rows = lax.broadcasted_iota(jnp.int32, x.shape, 0)
    return jnp.where(rows >= n, pltpu.roll(x, n, 0), 0.0)


def _shift_up(x, n):
    rows = lax.broadcasted_iota(jnp.int32, x.shape, 0)
    return jnp.where(rows < x.shape[0] - n, pltpu.roll(x, x.shape[0] - n, 0), 0.0)


def _conv(u, w_ref, b):
    return w_ref[0:1, :] * _shift_down(u, 2) + w_ref[1:2, :] * _shift_down(u, 1) + w_ref[2:3, :] * u + b


def _conv_act_fwd(u, cw, cb, name, tc=256):
    s, two_f = u.shape
    dff = two_f // 2
    tc = _tile(dff, tc)
    nb = dff // tc

    def body(ug_ref, uv_ref, wg_ref, wv_ref, bg_ref, bv_ref, a_ref):
        gate = _conv(ug_ref[...], wg_ref, bg_ref[...])
        val = _conv(uv_ref[...], wv_ref, bv_ref[...])
        a_ref[...] = (_silu(gate) * val).astype(a_ref.dtype)

    lo, hi = (lambda j: (0, j)), (lambda j: (0, j + nb))
    return _pcall(
        body, name=name, grid=(nb,),
        in_specs=[pl.BlockSpec((s, tc), lo), pl.BlockSpec((s, tc), hi), pl.BlockSpec((3, tc), lo),
                  pl.BlockSpec((3, tc), hi), pl.BlockSpec((1, tc), lo), pl.BlockSpec((1, tc), hi)],
        out_specs=pl.BlockSpec((s, tc), lo),
        out_shape=jax.ShapeDtypeStruct((s, dff), BF16),
        compiler_params=_params(1),
    )(u, u, cw, cw, cb, cb)


def _conv_act_bwd(u, cw, cb, da, name, tc=128):
    s, two_f = u.shape
    dff = two_f // 2
    tc = _tile(dff, tc)
    nb = dff // tc

    def body(ug_ref, uv_ref, wg_ref, wv_ref, bg_ref, bv_ref, da_ref, du_ref, dw_ref, db_ref):
        ug, uv, da = ug_ref[...], uv_ref[...], da_ref[...]
        gate = _conv(ug, wg_ref, bg_ref[...])
        val = _conv(uv, wv_ref, bv_ref[...])
        sg = _sigmoid(gate)
        d_val = da * (gate * sg)
        d_gate = da * val * (sg * (1.0 + gate * (1.0 - sg)))
        for half, (dc, uu, w_ref) in enumerate(((d_gate, ug, wg_ref), (d_val, uv, wv_ref))):
            du = w_ref[0:1, :] * _shift_up(dc, 2) + w_ref[1:2, :] * _shift_up(dc, 1) + w_ref[2:3, :] * dc
            du_ref[half] = du.astype(du_ref.dtype)
            dw_ref[half, 0:1, :] = _colsum(dc * _shift_down(uu, 2))
            dw_ref[half, 1:2, :] = _colsum(dc * _shift_down(uu, 1))
            dw_ref[half, 2:3, :] = _colsum(dc * uu)
            db_ref[half] = _colsum(dc)

    lo, hi = (lambda j: (0, j)), (lambda j: (0, j + nb))
    both = lambda j: (0, 0, j)
    return _pcall(
        body, name=name, grid=(nb,),
        in_specs=[pl.BlockSpec((s, tc), lo), pl.BlockSpec((s, tc), hi), pl.BlockSpec((3, tc), lo),
                  pl.BlockSpec((3, tc), hi), pl.BlockSpec((1, tc), lo), pl.BlockSpec((1, tc), hi),
                  pl.BlockSpec((s, tc), lo)],
        out_specs=[pl.BlockSpec((2, s, tc), both), pl.BlockSpec((2, 3, tc), both), pl.BlockSpec((2, 1, tc), both)],
        out_shape=[jax.ShapeDtypeStruct((2, s, dff), BF16), jax.ShapeDtypeStruct((2, 3, dff), F32),
                   jax.ShapeDtypeStruct((2, 1, dff), F32)],
        compiler_params=_params(1),
    )(u, u, cw, cw, cb, cb, da)


def _adamw_math(w, g, m, v):
    m = ADAM_B1 * m + (1.0 - ADAM_B1) * g
    v = ADAM_B2 * v + (1.0 - ADAM_B2) * (g * g)
    m_hat = m / (1.0 - ADAM_B1 ** ADAM_STEP)
    v_hat = v / (1.0 - ADAM_B2 ** ADAM_STEP)
    delta = -ADAM_LR * (m_hat / (jnp.sqrt(v_hat) + ADAM_EPS) + ADAM_WD * w)
    return delta, m, v


def _update_tiles(r, c, tr):
    tc = c
    if r % 8:
        tr, tc = r, _tile(c, max(LANE, 512 * 1024 // r // LANE * LANE))
    elif r <= tr:
        tr = r
    while r % tr:
        tr -= 8
    return tr, tc


def _adamw(w, g, m, v, name, tr=128):
    layers, r, c = w.shape
    tr, tc = _update_tiles(r, c, tr)

    def body(w_ref, g_ref, m_ref, v_ref, go_ref, d_ref, mo_ref, vo_ref):
        grad = g_ref[...]
        delta, m_new, v_new = _adamw_math(w_ref[...], grad, m_ref[...], v_ref[...])
        go_ref[...], d_ref[...], mo_ref[...], vo_ref[...] = grad, delta, m_new, v_new

    spec = pl.BlockSpec((None, tr, tc), lambda l, i, j: (l, i, j))
    return _pcall(
        body, name=name, grid=(layers, r // tr, c // tc), in_specs=[spec] * 4, out_specs=[spec] * 4,
        out_shape=[jax.ShapeDtypeStruct((layers, r, c), F32)] * 4, compiler_params=_params(3),
    )(w, g, m, v)


def _adamw_pieces(w, lands, sums, chip, m, v, name, tr=128):
    layers, r, c = w.shape
    tr, tc = _update_tiles(r, c, tr)
    nr, nc = r // tr, c // tc

    def body(chip_ref, w_ref, *rest):
        land_refs, own_refs = rest[:layers], rest[layers:2 * layers]
        m_ref, v_ref, go_ref, d_ref, mo_ref, vo_ref = rest[2 * layers:]
        for layer in range(layers):
            @pl.when(pl.program_id(0) == layer)
            def _(land_ref=land_refs[layer], own_ref=own_refs[layer]):
                grad = jnp.zeros(w_ref.shape, F32)
                for q in range(4):
                    grad = grad + jnp.where(chip_ref[0] == q, own_ref[...], land_ref[q]).astype(F32)
                delta, m_new, v_new = _adamw_math(w_ref[...], grad, m_ref[...], v_ref[...])
                go_ref[...], d_ref[...], mo_ref[...], vo_ref[...] = grad, delta, m_new, v_new

    def walk(k, l, i, j):
        here = l == k
        return jnp.where(here, i, jnp.where(l < k, 0, nr - 1)), jnp.where(here, j, jnp.where(l < k, 0, nc - 1))

    spec = pl.BlockSpec((None, tr, tc), lambda l, i, j, chip_ref: (l, i, j))
    land_specs = [pl.BlockSpec((4, tr, tc), lambda l, i, j, chip_ref, k=k: (0,) + walk(k, l, i, j))
                  for k in range(layers)]
    own_specs = [pl.BlockSpec((None, tr, tc), lambda l, i, j, chip_ref, k=k: (chip_ref[0],) + walk(k, l, i, j))
                 for k in range(layers)]
    return _pcall(
        body, name=name,
        grid_spec=pltpu.PrefetchScalarGridSpec(
            num_scalar_prefetch=1, grid=(layers, nr, nc),
            in_specs=[spec] + land_specs + own_specs + [spec, spec], out_specs=[spec] * 4),
        out_shape=[jax.ShapeDtypeStruct((layers, r, c), F32)] * 4, compiler_params=_params(3),
    )(chip, w, *lands, *sums, m, v)


def _pair_sum(pieces, partner, core, name, tr=512):
    _, r, c = pieces.shape
    tc = c
    if r % 8:
        tr, tc = r, _tile(c, max(LANE, 1024 * 1024 // r // LANE * LANE))
    elif r <= tr:
        tr = r
    while r % tr:
        tr -= 8

    def body(core_ref, mine_ref, partner_ref, out_ref):
        out_ref[...] = (mine_ref[...].astype(F32) + partner_ref[...].astype(F32)).astype(out_ref.dtype)

    return _pcall(
        body, name=name,
        grid_spec=pltpu.PrefetchScalarGridSpec(
            num_scalar_prefetch=1, grid=(4, r // tr, c // tc),
            in_specs=[pl.BlockSpec((None, tr, tc), lambda q, i, j, core_ref: (2 * q + core_ref[0], i, j)),
                      pl.BlockSpec((None, tr, tc), lambda q, i, j, core_ref: (q, i, j))],
            out_specs=pl.BlockSpec((None, tr, tc), lambda q, i, j, core_ref: (q, i, j))),
        out_shape=jax.ShapeDtypeStruct((4, r, c), pieces.dtype), compiler_params=_params(3),
    )(core, pieces, partner)


def _sum8(x, name):
    p = x.shape[2]
    tp = _tile(p, 16 * 1024)

    def body(x_ref, o_ref):
        acc = x_ref[0]
        for i in range(1, N_DEV):
            acc = acc + x_ref[i]
        o_ref[...] = acc

    return _pcall(
        body, name=name, grid=(p // tp,), in_specs=[pl.BlockSpec((N_DEV, 1, tp), lambda i: (0, 0, i))],
        out_specs=pl.BlockSpec((1, tp), lambda i: (0, i)), out_shape=jax.ShapeDtypeStruct((1, p), x.dtype),
        compiler_params=_params(1),
    )(x)


def _exchange(arrays, name, scatter):
    n = len(arrays)
    hbm = pl.BlockSpec(memory_space=pl.ANY)

    def body(*refs):
        ins, outs, token = refs[:n], refs[n:2 * n], refs[2 * n]
        send_sems, recv_sems, local_sems = refs[2 * n + 1:]
        token[...] = jnp.zeros_like(token)
        x, y, c = lax.axis_index("x"), lax.axis_index("y"), lax.axis_index("c")
        me = 4 * x + 2 * y + c
        copies = []
        for a in range(n):
            src_mine = ins[a].at[me] if scatter else ins[a]
            local = pltpu.make_async_copy(src_mine, outs[a].at[me], local_sems.at[a])
            local.start()
            copies.append(local)
            for k in range(1, N_DEV):
                px = 1 - x if k & 4 else x
                py = 1 - y if k & 2 else y
                pc = 1 - c if k & 1 else c
                src = ins[a].at[4 * px + 2 * py + pc] if scatter else ins[a]
                cp = pltpu.make_async_remote_copy(
                    src_ref=src, dst_ref=outs[a].at[me],
                    send_sem=send_sems.at[a * (N_DEV - 1) + k - 1], recv_sem=recv_sems.at[a * (N_DEV - 1) + k - 1],
                    device_id=(px, py, pc), device_id_type=pl.DeviceIdType.MESH)
                cp.start()
                copies.append(cp)
        for cp in copies:
            cp.wait()

    out_shape = [jax.ShapeDtypeStruct(a.shape if scatter else (N_DEV,) + a.shape, a.dtype) for a in arrays]
    res = _pcall(
        body, name=name, in_specs=[hbm] * n, out_specs=[hbm] * n + [pl.BlockSpec(memory_space=pltpu.VMEM)],
        out_shape=out_shape + [jax.ShapeDtypeStruct((8, LANE), F32)],
        scratch_shapes=[pltpu.SemaphoreType.DMA((n * (N_DEV - 1),)), pltpu.SemaphoreType.DMA((n * (N_DEV - 1),)),
                        pltpu.SemaphoreType.DMA((n,))],
        compiler_params=pltpu.CompilerParams(has_side_effects=True),
    )(*arrays)
    return res[:n], res[n][0, 0]


_HBM = pl.BlockSpec(memory_space=pltpu.HBM)
_SEM = pl.BlockSpec(memory_space=pltpu.SEMAPHORE)
_DATAFLOW = pltpu.SideEffectType.DATAFLOW_SIDE_EFFECTING


def _peer(k, x, y, c):
    return (1 - x if k & 4 else x, 1 - y if k & 2 else y, 1 - c if k & 1 else c)


def _pair_plan(x, y, c):
    return [(2 * q + (1 - c), q, (x, y, 1 - c)) for q in range(4)]


def _chip_plan(x, y, c):
    out = []
    for k in _ICI_PEERS:
        px, py, pc = _peer(k, x, y, c)
        out.append((2 * px + py, 2 * x + y, (px, py, pc)))
    return out


def _all_plan(x, y, c):
    return [(0, 4 * x + 2 * y + c, _peer(k, x, y, c)) for k in range(1, N_DEV)]


def _split_start(arrays, plan, name, land_blocks=4):
    n = len(arrays)
    lands = [lax.empty((land_blocks,) + a.shape[1:], a.dtype) for a in arrays]
    n_copies = len(plan(0, 0, 0))

    def body(*refs):
        srcs, dsts = refs[:n], refs[n:2 * n]
        send_sems, recv_sems, token = refs[4 * n:5 * n], refs[5 * n:6 * n], refs[6 * n]
        copies = plan(lax.axis_index("x"), lax.axis_index("y"), lax.axis_index("c"))
        for a in range(n):
            for j, (src_block, dst_block, peer) in enumerate(copies):
                pltpu.make_async_remote_copy(
                    src_ref=srcs[a].at[src_block], dst_ref=dsts[a].at[dst_block],
                    send_sem=send_sems[a].at[j], recv_sem=recv_sems[a].at[j],
                    device_id=peer, device_id_type=pl.DeviceIdType.MESH).start()
        token[...] = jnp.zeros_like(token)

    sems = [pltpu.SemaphoreType.DMA((n_copies,))] * (2 * n)
    res = _pcall(
        body, name=name,
        in_specs=[_HBM] * (2 * n),
        out_specs=[_HBM] * (2 * n) + [_SEM] * (2 * n) + [pl.BlockSpec(memory_space=pltpu.VMEM)],
        out_shape=[pltpu.HBM(a.shape, a.dtype) for a in arrays] + [pltpu.HBM(l.shape, l.dtype) for l in lands]
        + sems + [jax.ShapeDtypeStruct((8, LANE), F32)],
        input_output_aliases={i: i for i in range(2 * n)},
        compiler_params=pltpu.CompilerParams(has_side_effects=_DATAFLOW),
    )(*[pltpu.with_memory_space_constraint(a, pltpu.HBM) for a in arrays],
      *[pltpu.with_memory_space_constraint(l, pltpu.HBM) for l in lands])
    handles = [(res[a], res[n + a], res[2 * n + a], res[3 * n + a]) for a in range(n)]
    return handles, res[4 * n][0, 0]


def _split_wait(handles, plan, after, name):
    n = len(handles)
    after = list(after) if isinstance(after, (list, tuple)) else [after]

    def body(*refs):
        srcs, dsts = refs[:n], refs[n:2 * n]
        send_sems, recv_sems = refs[2 * n:3 * n], refs[3 * n:4 * n]
        copies = plan(lax.axis_index("x"), lax.axis_index("y"), lax.axis_index("c"))
        for a in range(n):
            for j, (src_block, dst_block, peer) in enumerate(copies):
                cp = pltpu.make_async_remote_copy(
                    src_ref=srcs[a].at[src_block], dst_ref=dsts[a].at[dst_block],
                    send_sem=send_sems[a].at[j], recv_sem=recv_sems[a].at[j],
                    device_id=peer, device_id_type=pl.DeviceIdType.MESH)
                cp.wait_send()
                cp.wait_recv()

    srcs, lands = [h[0] for h in handles], [h[1] for h in handles]
    res = _pcall(
        body, name=name,
        in_specs=[_HBM] * (2 * n) + [_SEM] * (2 * n) + [pl.BlockSpec(memory_space=pl.ANY)] * len(after),
        out_specs=[_HBM] * (2 * n),
        out_shape=[pltpu.HBM(t.shape, t.dtype) for t in srcs + lands],
        input_output_aliases={i: i for i in range(2 * n)},
        compiler_params=pltpu.CompilerParams(has_side_effects=_DATAFLOW),
    )(*srcs, *lands, *[h[2] for h in handles], *[h[3] for h in handles], *after)
    return res[:n], res[n:]


_ICI_PEERS = (2, 4, 6)


def _gather2_start(shards, name):
    n = len(shards)
    lands = [lax.empty((N_DEV,) + a.shape, a.dtype) for a in shards]

    def body(*refs):
        srcs, dsts = refs[:n], refs[n:2 * n]
        send_sems, d2d_sems, ici_sems = refs[4 * n:5 * n], refs[5 * n:6 * n], refs[6 * n:7 * n]
        token = refs[7 * n]
        x, y, c = lax.axis_index("x"), lax.axis_index("y"), lax.axis_index("c")
        me = 4 * x + 2 * y + c
        for a in range(n):
            for j, k in enumerate((1,) + _ICI_PEERS):
                recv = d2d_sems[a].at[0] if j == 0 else ici_sems[a].at[j - 1]
                pltpu.make_async_remote_copy(
                    src_ref=srcs[a], dst_ref=dsts[a].at[me], send_sem=send_sems[a].at[j], recv_sem=recv,
                    device_id=_peer(k, x, y, c), device_id_type=pl.DeviceIdType.MESH).start()
        token[...] = jnp.zeros_like(token)

    dma = pltpu.SemaphoreType.DMA
    res = _pcall(
        body, name=name,
        in_specs=[_HBM] * (2 * n),
        out_specs=[_HBM] * (2 * n) + [_SEM] * (3 * n) + [pl.BlockSpec(memory_space=pltpu.VMEM)],
        out_shape=[pltpu.HBM(a.shape, a.dtype) for a in shards] + [pltpu.HBM(l.shape, l.dtype) for l in lands]
        + [dma((4,))] * n + [dma((1,))] * n + [dma((3,))] * n + [jax.ShapeDtypeStruct((8, LANE), F32)],
        input_output_aliases={i: i for i in range(2 * n)},
        compiler_params=pltpu.CompilerParams(has_side_effects=_DATAFLOW),
    )(*[pltpu.with_memory_space_constraint(a, pltpu.HBM) for a in shards],
      *[pltpu.with_memory_space_constraint(l, pltpu.HBM) for l in lands])
    handles = [tuple(res[i * n + a] for i in range(5)) for a in range(n)]
    return handles, res[5 * n][0, 0]


def _gather2_forward(handle, after, name):
    src, land, send_sems, d2d_sem, ici_sems = handle

    def body(land_ref, ici_ref, d2d_ref, after_ref, land_out, fwd_send, fwd_recv, token):
        x, y, c = lax.axis_index("x"), lax.axis_index("y"), lax.axis_index("c")
        sibling = (x, y, 1 - c)
        arrived = [(_peer(k, x, y, c), ici_ref.at[j]) for j, k in enumerate(_ICI_PEERS)] + [(sibling, d2d_ref.at[0])]
        for j, ((px, py, pc), recv) in enumerate(arrived):
            block = land_ref.at[4 * px + 2 * py + pc]
            pltpu.make_async_remote_copy(
                src_ref=block, dst_ref=block, send_sem=fwd_send.at[j], recv_sem=recv,
                device_id=(px, py, pc), device_id_type=pl.DeviceIdType.MESH).wait_recv()
            pltpu.make_async_remote_copy(
                src_ref=block, dst_ref=block, send_sem=fwd_send.at[j], recv_sem=fwd_recv.at[j],
                device_id=sibling, device_id_type=pl.DeviceIdType.MESH).start()
        token[...] = jnp.zeros_like(token)

    dma = pltpu.SemaphoreType.DMA
    land, fwd_send, fwd_recv, token = _pcall(
        body, name=name,
        in_specs=[_HBM, _SEM, _SEM, pl.BlockSpec(memory_space=pl.ANY)],
        out_specs=[_HBM, _SEM, _SEM, pl.BlockSpec(memory_space=pltpu.VMEM)],
        out_shape=[pltpu.HBM(land.shape, land.dtype), dma((4,)), dma((4,)), jax.ShapeDtypeStruct((8, LANE), F32)],
        input_output_aliases={0: 0},
        compiler_params=pltpu.CompilerParams(has_side_effects=_DATAFLOW),
    )(land, ici_sems, d2d_sem, after)
    return (src, land, send_sems, fwd_send, fwd_recv), token[0, 0]


def _gather2_wait(handle, after, name):
    src, land, send_sems, fwd_send, fwd_recv = handle

    def body(src_ref, land_ref, send_ref, fsend_ref, frecv_ref, after_ref, src_out, land_out):
        x, y, c = lax.axis_index("x"), lax.axis_index("y"), lax.axis_index("c")
        block = land_ref.at[4 * x + 2 * y + c]

        def copy(send, recv):
            return pltpu.make_async_remote_copy(src_ref=src_ref, dst_ref=block, send_sem=send, recv_sem=recv,
                                                device_id=(x, y, 1 - c), device_id_type=pl.DeviceIdType.MESH)

        for j in range(4):
            copy(send_ref.at[j], frecv_ref.at[j]).wait_send()
        for j in range(4):
            copy(fsend_ref.at[j], frecv_ref.at[j]).wait_send()
            copy(fsend_ref.at[j], frecv_ref.at[j]).wait_recv()

    res = _pcall(
        body, name=name,
        in_specs=[_HBM, _HBM, _SEM, _SEM, _SEM, pl.BlockSpec(memory_space=pl.ANY)],
        out_specs=[_HBM, _HBM],
        out_shape=[pltpu.HBM(src.shape, src.dtype), pltpu.HBM(land.shape, land.dtype)],
        input_output_aliases={0: 0, 1: 1},
        compiler_params=pltpu.CompilerParams(has_side_effects=_DATAFLOW),
    )(src, land, send_sems, fwd_send, fwd_recv, after)
    return res[0], res[1]


def _pad_cols(x, width=LANE):
    return jnp.pad(x, ((0, 0), (0, width - x.shape[1])))


def _cols_full(g):
    return jnp.transpose(g, (1, 0, 2)).reshape(g.shape[1], -1)


def _ffn_fwd(x1, p, i, tag):
    h2 = _adaln_fwd(x1, p["norm_ffn"][i], p["sc_f"][i], p["sh_f"][i], f"ffn_norm_{tag}")
    u = _matmul(h2, p["fetch"](f"up{i}", h2), name=f"ffn_up_{tag}", tn=1408, b_shards=True)
    a = _conv_act_fwd(u, p["conv_w"][i], p["conv_b"][i], f"ffn_act_{tag}")
    g_f = p["g_f"][i]
    x2, f = _matmul(a, p["fetch"](f"down{i}", a), name=f"ffn_down_{tag}", tk=1408, out_dtypes=(F32, F32),
                    epilogue=lambda acc, x1, g: (x1 + (1.0 + g) * acc, acc), extras=(("mn", x1), ("n", g_f)))
    return x2, dict(h2=h2, u=u, a=a, f=f)


def _ffn_bwd(incoming, x1, saved, p, i, tag, branch):
    d = x1.shape[1]
    dx2, df, dg_f = incoming
    w_up, w_down = p["fetch"](f"up{i}", None), p["fetch"](f"down{i}", None)
    da = _matmul(df, w_down, tb=True, name=f"ffn_down_dx_{tag}", tn=1408)
    dw_down = _matmul(saved["a"], df, ta=True, name=f"ffn_down_dw_{tag}", tm=1408, out_dtypes=(BF16,))
    du, dcw, dcb = _conv_act_bwd(saved["u"], p["conv_w"][i], p["conv_b"][i], da, f"ffn_act_bwd_{tag}")
    dcw, dcb = (jnp.concatenate([t[0], t[1]], axis=1) for t in (dcw, dcb))
    tok = p["flush"](du)
    dh2 = _matmul(du, w_up, tb=True, name=f"ffn_up_dx_{tag}", tn=2048, tk=1408, a_halves=True, b_shards=True)
    dw_up = _matmul(saved["h2"], du, ta=True, name=f"ffn_up_dw_{tag}", tn=1408, out_dtypes=(BF16,), b_halves=True,
                    out_shards=True)
    tok = tok + p["send"](f"ffn{i}", [dw_up, dw_down.reshape(N_DEV, -1, d)])
    dx1, dsh, dsc, dgain, dy, dg_m = _adaln_bwd(x1, dh2, dx2, p["norm_ffn"][i] + tok, p["sc_f"][i],
                                                f"ffn_norm_bwd_{tag}", branch)
    grads = dict(conv_w=dcw, conv_b=dcb, norm_ffn=dgain, sh_f=dsh, sc_f=dsc, g_f=dg_f)
    return (dx1, dy, dg_m), grads


def _gla_layer_fwd(x, p, i):
    h1 = _adaln_fwd(x, p["norm_mix"][i], p["sc_m"][i], p["sh_m"][i], "gla_norm")
    w_t, w_tail_t, main = p["fetch"]("gla_in", h1)
    proj = _matmul(h1, w_t, tb=True, b_rows=main, name="gla_in")
    a_tail = _matmul(h1, w_tail_t, tb=True, name="gla_in_tail")
    dk_total = p["gla_wg_p"].shape[1]
    o, states = _gla_fwd(proj, a_tail, p["gla_wg_p"], p["gla_b_gate"], "gla_chunks")
    assert 2 * dk_total == o.shape[1]
    r = ("cols", proj, 2, o.shape[1])
    og = _gla_post_fwd(o, r, p["gla_norm"], "gla_post")
    x1, y = _matmul(og, p["fetch"]("gla_out", og), name="gla_out", out_dtypes=(F32, F32),
                    epilogue=lambda acc, x, g: (x + (1.0 + g) * acc, acc), extras=(("mn", x), ("n", p["g_m"][i])))
    return x1, dict(h1=h1, proj=proj, a_tail=a_tail, o=o, r=r, states=states, og=og, y=y)


def _gla_layer_bwd(incoming, x, sv, p, i, branch):
    d = x.shape[1]
    dx1, dy, dg_m = incoming
    (w_t, w_tail_t, main), w_out = p["fetch"]("gla_in", None), p["fetch"]("gla_out", None)
    dog = _matmul(dy, w_out, tb=True, name="gla_out_dx")
    dw_out = _matmul(sv["og"], dy, ta=True, name="gla_out_dw", out_dtypes=(BF16,))
    tok = p["flush"](dog) + p["send"]("gla_out", [dw_out.reshape(N_DEV, -1, d)])
    d_o, dproj, dgn = _gla_post_bwd(sv["o"], sv["r"], p["gla_norm"] + tok, dog, "gla_post_bwd")
    dproj, dga = _gla_bwd(sv["proj"], sv["a_tail"], p["gla_wg_p"], p["gla_b_gate"], sv["states"], d_o, dproj,
                          "gla_chunks_bwd")
    tok = p["flush"](dga)
    da_tail = _matmul(dga, p["gla_wg_p"], tb=True, name="gla_gate_dx", out_dtypes=(BF16,))
    dwg = _matmul(sv["a_tail"], dga, ta=True, name="gla_gate_dw")
    dbg = _rowwise(lambda t: (_colsum(t),), [("row", dga)], [("acc", dga.shape[1], F32)], name="gla_gate_db")[0]
    dh_tail = _matmul(da_tail, w_tail_t, name="gla_in_tail_dx")
    dh1 = _matmul(dproj, w_t, b_rows=main, name="gla_in_dx", tk=2048,
                  epilogue=lambda acc, t: (acc + t,), extras=(("mn", dh_tail),))
    rank = p["gla_rank"]
    dw_main = _matmul(dproj, sv["h1"], ta=True, name="gla_in_dw", out_dtypes=(BF16,), out_rows=main + rank)
    dx, dsh, dsc, dgain, *into_branch = _adaln_bwd(x, dh1, dx1, p["norm_mix"][i] + tok, p["sc_m"][i], "gla_norm_bwd",
                                                   branch)
    grads = dict(gla_w_gate=dwg[:rank], gla_b_gate=dbg, gla_norm=dgn, norm_mix=dgain, sh_m=dsh, sc_m=dsc, g_m=dg_m,
                 gla_w_in_unsent=(dw_main, da_tail, sv["h1"]))
    return (dx, *into_branch), grads


def _fox_layer_fwd(x, p, i):
    d = x.shape[1]
    hd = p["fox_q_norm"].shape[1]
    heads = d // hd
    s = x.shape[0]
    t = _tile(s, 512)
    h1 = _adaln_fwd(x, p["norm_mix"][i], p["sc_m"][i], p["sh_m"][i], "fox_norm")
    w_t, w_tail_t, main = p["fetch"]("fox_in", h1)
    proj = _matmul(h1, w_t, tb=True, b_rows=main, name="fox_in")
    fl = _matmul(h1, w_tail_t, tb=True, name="fox_in_tail")
    q, k, v, og = (("cols", proj, j, d) for j in range(4))
    qn, kn, vb = _fox_prep(q, k, v, p["fox_q_norm"], p["fox_k_norm"], d, hd, "fox_prep")
    cum = _fox_cum(fl, p["fox_bf_p"], "fox_cum")
    cum_t = jnp.transpose(cum[:, :heads])
    cum_col, cum_row = cum_t[:, :, None], cum_t.reshape(heads, s // t, 1, t)
    o, lse = _fox_attn_fwd(qn, kn, vb, cum_col, cum_row, hd, t, "fox_attn")
    act = _fox_gate_fwd(o, og, "fox_gate")
    x1, y = _matmul(act, p["fetch"]("fox_out", act), name="fox_out", out_dtypes=(F32, F32),
                    epilogue=lambda acc, x, g: (x + (1.0 + g) * acc, acc), extras=(("mn", x), ("n", p["g_m"][i])))
    return x1, dict(h1=h1, q=q, k=k, og=og, fl=fl, qn=qn, kn=kn, vb=vb, cum_col=cum_col, cum_row=cum_row,
                    o=o, lse=lse, act=act, y=y, t=t, hd=hd)


def _fox_layer_bwd(incoming, x, sv, p, i, branch):
    d = x.shape[1]
    hd, t = sv["hd"], sv["t"]
    heads = d // hd
    s = x.shape[0]
    dx1, dy, dg_m = incoming
    (w_t, w_tail_t, main), w_out = p["fetch"]("fox_in", None), p["fetch"]("fox_out", None)
    dact = _matmul(dy, w_out, tb=True, name="fox_out_dx")
    dw_out = _matmul(sv["act"], dy, ta=True, name="fox_out_dw", out_dtypes=(BF16,))
    d_o, dproj = _fox_gate_bwd(sv["o"], sv["og"], dact, "fox_gate_bwd")
    tok_flush = p["flush"](d_o)
    dqn, dkn, dproj, dcq, dck = _fox_attn_bwd(sv["qn"], sv["kn"], sv["vb"], d_o, sv["o"], sv["lse"], sv["cum_col"],
                                              sv["cum_row"], hd, t, dproj, "fox_attn_bwd")
    dproj, gq, gk = _fox_prep_bwd(sv["q"], sv["k"], dqn, dkn, p["fox_q_norm"], p["fox_k_norm"], hd, dproj,
                                  "fox_prep_bwd")
    dcum = _pad_cols(jnp.transpose(dcq[:, :, 0] - dck.reshape(heads, s)))
    dfl, dbf = _fox_cum_bwd(dcum, sv["fl"], p["fox_bf_p"], "fox_cum_bwd")
    dfl_b = dfl.astype(BF16)
    dh_tail = _matmul(dfl_b, w_tail_t, name="fox_in_tail_dx")
    dh1 = _matmul(dproj, w_t, b_rows=main, name="fox_in_dx", tk=2048,
                  epilogue=lambda acc, tl: (acc + tl,), extras=(("mn", dh_tail),))
    dw_main = _matmul(dproj, sv["h1"], ta=True, name="fox_in_dw", out_dtypes=(BF16,), out_rows=main + heads)
    dw_in = _tail_rows(dfl_b, sv["h1"], dw_main, heads, "fox_in_tail_dw").reshape(N_DEV, -1, d)
    tok = tok_flush + p["send"]("fox", [dw_in, dw_out.reshape(N_DEV, -1, d)])
    dx, dsh, dsc, dgain, *into_branch = _adaln_bwd(x, dh1, dx1, p["norm_mix"][i] + tok, p["sc_m"][i], "fox_norm_bwd",
                                                   branch)
    grads = dict(fox_b_f=dbf[:, :heads], fox_q_norm=gq.reshape(heads, hd).sum(0, keepdims=True),
                 fox_k_norm=gk.reshape(heads, hd).sum(0, keepdims=True), norm_mix=dgain, sh_m=dsh, sc_m=dsc, g_m=dg_m)
    return (dx, *into_branch), grads


SMALL = ("b_mod", "norm_mix", "norm_ffn", "gla_b_gate", "gla_norm", "fox_b_f", "fox_q_norm", "fox_k_norm",
         "ffn_conv_b", "norm_final")
SMALL_SHARDED = ("gla_w_gate", "ffn_conv_w")
BIG = ("gla_w_in", "gla_w_out", "fox_w_in", "fox_w_out", "ffn_w_up", "ffn_w_down")
WEIGHTS = ("w_mod", "b_mod", "norm_mix", "norm_ffn", "gla_w_in", "gla_w_gate", "gla_b_gate", "gla_norm", "gla_w_out",
           "fox_w_in", "fox_b_f", "fox_q_norm", "fox_k_norm", "fox_w_out", "ffn_w_up", "ffn_conv_w", "ffn_conv_b",
           "ffn_w_down", "norm_final")


def _pack(parts):
    flat = jnp.concatenate([p.reshape(-1) for p in parts])
    pad = (-flat.shape[0]) % 1024
    return jnp.pad(flat, (0, pad)).reshape(1, -1)


def _unpack(flat, shapes):
    out, off = [], 0
    for shp in shapes:
        n = 1
        for s in shp:
            n *= s
        out.append(flat[0, off:off + n].reshape(shp))
        off += n
    return out


def kernel(x, c, w_mod, b_mod, norm_mix, norm_ffn, gla_w_in, gla_w_gate, gla_b_gate, gla_norm, gla_w_out, fox_w_in, fox_b_f, fox_q_norm, fox_k_norm, fox_w_out, ffn_w_up, ffn_conv_w, ffn_conv_b, ffn_w_down, norm_final, loss_target, m_w_mod, m_b_mod, m_norm_mix, m_norm_ffn, m_gla_w_in, m_gla_w_gate, m_gla_b_gate, m_gla_norm, m_gla_w_out, m_fox_w_in, m_fox_b_f, m_fox_q_norm, m_fox_k_norm, m_fox_w_out, m_ffn_w_up, m_ffn_conv_w, m_ffn_conv_b, m_ffn_w_down, m_norm_final, v_w_mod, v_b_mod, v_norm_mix, v_norm_ffn, v_gla_w_in, v_gla_w_gate, v_gla_b_gate, v_gla_norm, v_gla_w_out, v_fox_w_in, v_fox_b_f, v_fox_q_norm, v_fox_k_norm, v_fox_w_out, v_ffn_w_up, v_ffn_conv_w, v_ffn_conv_b, v_ffn_w_down, v_norm_final):
    w = dict(w_mod=w_mod, b_mod=b_mod, norm_mix=norm_mix, norm_ffn=norm_ffn, gla_w_in=gla_w_in, gla_w_gate=gla_w_gate,
             gla_b_gate=gla_b_gate, gla_norm=gla_norm, gla_w_out=gla_w_out, fox_w_in=fox_w_in, fox_b_f=fox_b_f,
             fox_q_norm=fox_q_norm, fox_k_norm=fox_k_norm, fox_w_out=fox_w_out, ffn_w_up=ffn_w_up,
             ffn_conv_w=ffn_conv_w, ffn_conv_b=ffn_conv_b, ffn_w_down=ffn_w_down, norm_final=norm_final)
    mom_m = dict(w_mod=m_w_mod, b_mod=m_b_mod, norm_mix=m_norm_mix, norm_ffn=m_norm_ffn, gla_w_in=m_gla_w_in,
                 gla_w_gate=m_gla_w_gate, gla_b_gate=m_gla_b_gate, gla_norm=m_gla_norm, gla_w_out=m_gla_w_out,
                 fox_w_in=m_fox_w_in, fox_b_f=m_fox_b_f, fox_q_norm=m_fox_q_norm, fox_k_norm=m_fox_k_norm,
                 fox_w_out=m_fox_w_out, ffn_w_up=m_ffn_w_up, ffn_conv_w=m_ffn_conv_w, ffn_conv_b=m_ffn_conv_b,
                 ffn_w_down=m_ffn_w_down, norm_final=m_norm_final)
    mom_v = dict(w_mod=v_w_mod, b_mod=v_b_mod, norm_mix=v_norm_mix, norm_ffn=v_norm_ffn, gla_w_in=v_gla_w_in,
                 gla_w_gate=v_gla_w_gate, gla_b_gate=v_gla_b_gate, gla_norm=v_gla_norm, gla_w_out=v_gla_w_out,
                 fox_w_in=v_fox_w_in, fox_b_f=v_fox_b_f, fox_q_norm=v_fox_q_norm, fox_k_norm=v_fox_k_norm,
                 fox_w_out=v_fox_w_out, ffn_w_up=v_ffn_w_up, ffn_conv_w=v_ffn_conv_w, ffn_conv_b=v_ffn_conv_b,
                 ffn_w_down=v_ffn_w_down, norm_final=v_norm_final)

    me = 4 * lax.axis_index("x") + 2 * lax.axis_index("y") + lax.axis_index("c")
    xs, target = x[0], loss_target[0]
    s, d = xs.shape
    depth = w_mod.shape[0]
    mod_cols = w_mod.shape[2]
    rank = gla_w_gate.shape[1]
    hd = fox_q_norm.shape[1]
    fox_heads = d // hd
    dk_total = gla_w_gate.shape[2] * N_DEV

    cond = c * (1.0 / (1.0 + jnp.exp(-c)))
    g, _ = _exchange([gla_w_gate[0], ffn_conv_w, cond], "gather_small", scatter=False)
    cond_all = g[2][:, 0, :]

    cond_pad = jnp.pad(cond_all, ((0, 16 - N_DEV), (0, 0)))
    mod_part = []
    for i in range(depth):
        b_cols = lax.dynamic_slice(b_mod[i:i + 1], (0, me * mod_cols), (1, mod_cols))
        mod_part.append(_matmul(cond_pad, w_mod, b_layer=i, name=f"mod_{i}", tn=768,
                                epilogue=lambda acc, b: (acc + b,), extras=(("n", b_cols),))[:N_DEV])
    (mod_all,), tok_mod = _exchange([jnp.stack(mod_part)], "gather_mod", scatter=False)
    mod = lax.dynamic_index_in_dim(mod_all, me, axis=2, keepdims=False)
    mod = jnp.transpose(mod, (1, 0, 2)).reshape(depth, 6, 1, d)

    big_names = ["gla_in", "gla_out", "up0", "down0", "fox_in", "fox_out", "up1", "down1"]
    first = [jnp.transpose(gla_w_in[0] + tok_mod).astype(BF16), gla_w_out[0].astype(BF16)]
    handles, tok_first = _gather2_start(first, "gather_weights_start_first")
    rest = [ffn_w_up[0] + tok_first, ffn_w_down[0], jnp.transpose(fox_w_in[0]), fox_w_out[0], ffn_w_up[1],
            ffn_w_down[1]]
    handles_rest, tok0 = _gather2_start([t.astype(BF16) for t in rest], "gather_weights_start_rest")
    handles = handles + handles_rest
    ready, forwarded = {}, {}

    def split_tail(full_t, tail):
        main = full_t.shape[0] - tail
        return full_t, jnp.pad(full_t[main:], ((0, LANE - tail), (0, 0))), main

    def forward(idx, after):
        key = big_names[idx]
        forwarded[key] = _gather2_forward(handles[idx], after, f"gather_{key}_forward")

    def fetch(key, after):
        if key not in ready:
            idx = big_names.index(key)
            if idx == 0:
                forward(0, after)
            handle, _ = forwarded[key]
            _, full = _gather2_wait(handle, after, f"gather_{key}_wait")
            if idx + 1 < len(big_names):
                forward(idx + 1, full)
            if key == "gla_in":
                ready[key] = split_tail(full.reshape(-1, d), rank)
            elif key == "fox_in":
                ready[key] = split_tail(full.reshape(-1, d), fox_heads)
            elif key.startswith("up"):
                ready[key] = full
            else:
                ready[key] = full.reshape(-1, d)
        return ready[key]

    pending, sent = [], {}
    core = lax.axis_index("c").astype(jnp.int32).reshape(1)
    chip = 2 * lax.axis_index("x") + lax.axis_index("y")

    def send(key, pieces):
        hs, tok = _split_start(pieces, _pair_plan, f"scatter_{key}_pair_start")
        pending.append((key, hs))
        return tok

    def flush(after):
        tok = 0.0
        while pending:
            key, hs = pending.pop(0)
            mine, partner = _split_wait(hs, _pair_plan, after, f"scatter_{key}_pair_wait")
            sums = [_pair_sum(pc, pt, core, f"scatter_{key}_pair_sum{a}")
                    for a, (pc, pt) in enumerate(zip(mine, partner))]
            sent[key], t = _split_start(sums, _chip_plan, f"scatter_{key}_chip_start")
            tok = tok + t
        return tok

    p = dict(
        fetch=fetch, send=send, flush=flush,
        gla_wg_p=jnp.pad(_cols_full(g[0]), ((0, LANE - rank), (0, 0))),
        conv_w=[jnp.transpose(g[1][:, i], (1, 0, 2)).reshape(ffn_conv_w.shape[1], -1) for i in range(depth)],
        conv_b=[ffn_conv_b[i:i + 1] for i in range(depth)],
        gla_b_gate=gla_b_gate, gla_norm=gla_norm, fox_q_norm=fox_q_norm, fox_k_norm=fox_k_norm,
        fox_bf_p=_pad_cols(fox_b_f), gla_rank=rank,
        norm_mix=[norm_mix[i:i + 1] + (tok0 if i == 0 else 0.0) for i in range(depth)],
        norm_ffn=[norm_ffn[i:i + 1] for i in range(depth)],
    )

    for j, nm in enumerate(("sh_m", "sc_m", "g_m", "sh_f", "sc_f", "g_f")):
        p[nm] = [mod[i, j] for i in range(depth)]

    acts, saved = [xs], []
    for i in range(depth):
        layer_fwd = _gla_layer_fwd if i % 2 == 0 else _fox_layer_fwd
        x1, sv_mix = layer_fwd(acts[-1], p, i)
        x2, sv_ffn = _ffn_fwd(x1, p, i, str(i))
        saved.append((acts[-1], x1, sv_mix, sv_ffn))
        acts.append(x2)
    last_ffn = (saved[-1][3]["f"], p["g_f"][depth - 1])
    dx, d_norm_final, loss_part, *into_branch = _final_loss(acts[-1], target, norm_final.reshape(1, d), "final_loss",
                                                            last_ffn)
    incoming = (dx, *into_branch)

    lg = [None] * depth
    for i in reversed(range(depth)):
        x_in, x1, sv_mix, sv_ffn = saved[i]
        incoming, g_ffn = _ffn_bwd(incoming, x1, sv_ffn, p, i, str(i), (sv_mix["y"], p["g_m"][i]))
        layer_bwd = _gla_layer_bwd if i % 2 == 0 else _fox_layer_bwd
        before = (saved[i - 1][3]["f"], p["g_f"][i - 1]) if i else None
        incoming, g_mix = layer_bwd(incoming, x_in, sv_mix, p, i, before)
        lg[i] = {**g_ffn, **g_mix}
    grad_x = incoming[0][None]

    gla_l = [i for i in range(depth) if i % 2 == 0]
    fox_l = [i for i in range(depth) if i % 2 == 1]
    small_parts = dict(
        norm_mix=jnp.concatenate([lg[i]["norm_mix"] for i in range(depth)]),
        norm_ffn=jnp.concatenate([lg[i]["norm_ffn"] for i in range(depth)]),
        gla_b_gate=jnp.concatenate([lg[i]["gla_b_gate"] for i in gla_l]),
        gla_norm=jnp.concatenate([lg[i]["gla_norm"] for i in gla_l]),
        fox_b_f=jnp.concatenate([lg[i]["fox_b_f"] for i in fox_l]),
        fox_q_norm=jnp.concatenate([lg[i]["fox_q_norm"] for i in fox_l]),
        fox_k_norm=jnp.concatenate([lg[i]["fox_k_norm"] for i in fox_l]),
        ffn_conv_b=jnp.concatenate([lg[i]["conv_b"] for i in range(depth)]),
        norm_final=d_norm_final,
        gla_w_gate=jnp.stack([lg[i]["gla_w_gate"] for i in gla_l]),
        ffn_conv_w=jnp.stack([lg[i]["conv_w"] for i in range(depth)]),
        loss=loss_part[:, :1],
    )
    order = ("norm_mix", "norm_ffn", "gla_b_gate", "gla_norm", "fox_b_f", "fox_q_norm", "fox_k_norm", "ffn_conv_b",
             "norm_final", "gla_w_gate", "ffn_conv_w", "loss")
    packed = _pack([small_parts[nm] for nm in order])
    dmod = jnp.stack([jnp.concatenate([lg[i][nm] for nm in ("sh_m", "sc_m", "g_m", "sh_f", "sc_f", "g_f")], axis=1)
                      for i in range(depth)])
    hs_small, tok_small = _split_start([packed[None], dmod[None]], _all_plan, "gather_small_grads_start",
                                       land_blocks=N_DEV)
    dw_main, da_tail, h1_gla = lg[0]["gla_w_in_unsent"]
    dw_in_t = _tail_rows(da_tail + tok_small.astype(BF16), h1_gla, dw_main, rank, "gla_in_tail_dw")
    send("gla_in", [dw_in_t.reshape(N_DEV, -1, d)])
    started = pending[-1][1][0][0]

    received = {}

    def arrive(key, after):
        sums, lands = _split_wait(sent[key], _chip_plan, after, f"scatter_{key}_chip_wait")
        received[key] = list(zip(lands, sums))

    for key in ("ffn1", "fox", "ffn0", "gla_out"):
        arrive(key, started)

    out_g, out_d, out_m, out_v = {}, {}, {}, {}

    chip_idx = chip.astype(jnp.int32).reshape(1)

    def update(nm, g_arr, transposed=False):
        swap = (lambda t: jnp.transpose(t, (0, 2, 1))) if transposed else (lambda t: t)
        if isinstance(g_arr, list):
            res = _adamw_pieces(swap(w[nm]), [t[0] for t in g_arr], [t[1] for t in g_arr], chip_idx,
                                swap(mom_m[nm]), swap(mom_v[nm]), f"adamw_{nm}")
        else:
            res = _adamw(w[nm], g_arr, mom_m[nm], mom_v[nm], f"adamw_{nm}")
        out_g[nm], out_d[nm], out_m[nm], out_v[nm] = (swap(t) for t in res)

    update("gla_w_out", [received["gla_out"][0]])
    update("fox_w_out", [received["fox"][1]])
    tok_flush = flush(out_g["fox_w_out"])
    update("ffn_w_up", [received[f"ffn{i}"][0] for i in range(depth)])
    update("fox_w_in", [received["fox"][0]], transposed=True)
    update("ffn_w_down", [received[f"ffn{i}"][1] for i in range(depth)])

    updated = ("gla_w_out", "fox_w_in", "fox_w_out", "ffn_w_up", "ffn_w_down")
    (packed_mine, dmod_mine), (packed_all, dmod_all) = _split_wait(
        hs_small, _all_plan, [out_d[nm] for nm in updated], "gather_small_grads_wait")
    packed_all = lax.dynamic_update_slice(packed_all, packed_mine + tok_flush, (me, 0, 0))
    dmod_all = lax.dynamic_update_slice(dmod_all, dmod_mine, (me, 0, 0, 0))
    summed = _unpack(_sum8(packed_all, "sum_small_grads"), [small_parts[nm].shape for nm in order])
    small_g = dict(zip(order, summed))
    loss = small_g["loss"][0, 0]
    dmod_all = dmod_all[:, :, 0, :]
    grads = {}
    cond_t = _pad_cols(jnp.transpose(cond_all)).astype(BF16)
    dmod_cols = lax.dynamic_slice(dmod_all, (0, 0, me * mod_cols), (N_DEV, depth, mod_cols))
    g_w_mod = lax.empty(w_mod.shape, F32)
    for i in range(depth):
        rhs = jnp.pad(dmod_cols[:, i], ((0, LANE - N_DEV), (0, 0)))
        g_w_mod = _matmul(cond_t, rhs, name=f"mod_dw_{i}", tn=768, into=(g_w_mod, i))
    grads["w_mod"] = g_w_mod
    small_g["b_mod"] = _sum8(dmod_all.reshape(N_DEV, 1, -1), "sum_b_mod").reshape(depth, -1)
    update("w_mod", grads["w_mod"])

    gate_cols = gla_w_gate.shape[2]
    conv_cols = ffn_conv_w.shape[2]
    local_small = dict(small_g)
    local_small["gla_w_gate"] = lax.dynamic_slice_in_dim(small_g["gla_w_gate"], me * gate_cols, gate_cols, axis=2)
    local_small["ffn_conv_w"] = lax.dynamic_slice_in_dim(small_g["ffn_conv_w"], me * conv_cols, conv_cols, axis=2)
    names = SMALL + SMALL_SHARDED
    shapes = [w[nm].shape for nm in names]
    res = _adamw(_pack([w[nm] for nm in names])[None], _pack([local_small[nm] for nm in names])[None],
                 _pack([mom_m[nm] for nm in names])[None], _pack([mom_v[nm] for nm in names])[None], "adamw_small")
    for tgt, flat in zip((out_g, out_d, out_m, out_v), res):
        for nm, arr in zip(names, _unpack(flat[0], shapes)):
            tgt[nm] = arr

    arrive("gla_in", [out_d[nm] for nm in updated + ("w_mod",)])
    update("gla_w_in", [received["gla_in"][0]], transposed=True)

    return (loss, grad_x, *[out_g[n] for n in WEIGHTS], *[out_d[n] for n in WEIGHTS],
            *[out_m[n] for n in WEIGHTS], *[out_v[n] for n in WEIGHTS])
```

```python
import jax
import jax.numpy as jnp
from jax import lax
from jax.experimental import pallas as pl
from jax.experimental.pallas import tpu as pltpu

F32, BF16 = jnp.float32, jnp.bfloat16
N_DEV = 8
GLA_HEADS = 4
GLA_TAU = 16.0
GLA_CHUNK = 64
NORM_EPS = 1e-6
ADAM_LR, ADAM_B1, ADAM_B2, ADAM_EPS, ADAM_WD, ADAM_STEP = 0.001, 0.9, 0.999, 1e-08, 0.01, 10
LANE = 128
VMEM_LIMIT = 56 * 1024 * 1024
NEG = -1e30


def _pcall(body, **kw):
    return pl.pallas_call(body, **kw)


def _params(n_axes):
    return pltpu.CompilerParams(dimension_semantics=("arbitrary",) * n_axes, vmem_limit_bytes=VMEM_LIMIT)


def _tile(dim, pref):
    if dim <= pref:
        return dim
    t = pref
    while dim % t:
        t -= LANE
    assert t > 0, (dim, pref)
    return t


def _dot(a, b, ta=False, tb=False):
    dims = (((0,) if ta else (1,), (1,) if tb else (0,)), ((), ()))
    return lax.dot_general(a.astype(BF16), b.astype(BF16), dims, preferred_element_type=F32)


def _split3(x):
    hi = x.astype(BF16)
    r1 = x - hi.astype(F32)
    mid = r1.astype(BF16)
    lo = (r1 - mid.astype(F32)).astype(BF16)
    return hi, mid, lo


def _tri_matmul(tri, x):
    hi, mid, lo = _split3(x)
    return _dot(tri, hi) + _dot(tri, mid) + _dot(tri, lo)


def _tri(n, upper=False):
    r = lax.broadcasted_iota(jnp.int32, (n, n), 0)
    c = lax.broadcasted_iota(jnp.int32, (n, n), 1)
    return jnp.where((r <= c) if upper else (r >= c), 1.0, 0.0).astype(BF16)


def _log_sigmoid(x):
    return jnp.minimum(x, 0.0) - jnp.log(1.0 + jnp.exp(-jnp.abs(x)))


def _sigmoid(x):
    return 1.0 / (1.0 + jnp.exp(-x))


def _silu(x):
    return x * _sigmoid(x)


def _dsilu(x):
    s = _sigmoid(x)
    return s * (1.0 + x * (1.0 - s))


def _matmul(a, b, *, name, ta=False, tb=False, out_dtypes=(F32,), tm=1024, tn=1024, tk=2048,
            epilogue=None, extras=(), a_halves=False, b_halves=False, b_shards=False, out_shards=False,
            b_rows=None, out_rows=None, b_layer=None, into=None):
    if a_halves:
        assert not ta
        m, k = a.shape[1], 2 * a.shape[2]
    else:
        m, k = (a.shape[1], a.shape[0]) if ta else a.shape
    if b_halves:
        assert not tb and b.shape[1] == k
        n = 2 * b.shape[2]
    elif b_shards:
        n = b.shape[1] if tb else N_DEV * b.shape[2]
        assert (N_DEV * b.shape[2] if tb else b.shape[1]) == k, (a.shape, b.shape, ta, tb)
    elif b_layer is not None:
        assert not tb and b.shape[1] == k
        n = b.shape[2]
    else:
        rows = b.shape[0] if b_rows is None else b_rows
        n = rows if tb else b.shape[1]
        assert (b.shape[1] if tb else rows) == k, (a.shape, b.shape, ta, tb)
    n_unit = n // N_DEV if (out_shards or (b_shards and not tb)) else (n // 2 if b_halves else n)
    k_unit = k // N_DEV if (b_shards and tb) else (k // 2 if a_halves else k)
    tm, tn, tk = _tile(m, tm), _tile(n_unit, tn), _tile(k_unit, tk)
    nk = k // tk
    if a_halves:
        a_spec = pl.BlockSpec((None, tm, tk), lambda i, j, kk: (kk // (nk // 2), i, kk % (nk // 2)))
    elif ta:
        a_spec = pl.BlockSpec((tk, tm), lambda i, j, kk: (kk, i))
    else:
        a_spec = pl.BlockSpec((tm, tk), lambda i, j, kk: (i, kk))
    n_per, k_per = n // tn // N_DEV, nk // N_DEV
    if b_halves:
        b_spec = pl.BlockSpec((None, tk, tn), lambda i, j, kk: (j // (n // tn // 2), kk, j % (n // tn // 2)))
    elif b_shards and tb:
        b_spec = pl.BlockSpec((None, tn, tk), lambda i, j, kk: (kk // k_per, j, kk % k_per))
    elif b_shards:
        b_spec = pl.BlockSpec((None, tk, tn), lambda i, j, kk: (j // n_per, kk, j % n_per))
    elif b_layer is not None:
        b_spec = pl.BlockSpec((None, tk, tn), lambda i, j, kk: (b_layer, kk, j))
    elif tb:
        b_spec = pl.BlockSpec((tn, tk), lambda i, j, kk: (j, kk))
    else:
        b_spec = pl.BlockSpec((tk, tn), lambda i, j, kk: (kk, j))
    ex_specs = []
    for kind, arr in extras:
        if kind == "mn":
            assert arr.shape == (m, n), (arr.shape, m, n)
            ex_specs.append(pl.BlockSpec((tm, tn), lambda i, j, kk: (i, j)))
        else:
            assert arr.shape == (1, n), (arr.shape, n)
            ex_specs.append(pl.BlockSpec((1, tn), lambda i, j, kk: (0, j)))
    n_ex, n_out = len(extras), len(out_dtypes)

    def body(a_ref, b_ref, *rest):
        ex, outs, acc = rest[:n_ex], rest[-1 - n_out:-1], rest[-1]
        kk = pl.program_id(2)

        @pl.when(kk == 0)
        def _():
            acc[...] = jnp.zeros_like(acc)

        acc[...] += _dot(a_ref[...], b_ref[...], ta, tb)

        @pl.when(kk == nk - 1)
        def _():
            if epilogue is None:
                vals = (acc[...],)
            else:
                vals = epilogue(acc[...], *[e[...] for e in ex])
            for o, v in zip(outs, vals):
                o[...] = v.astype(o.dtype)

    if out_shards:
        out_spec = pl.BlockSpec((None, tm, tn), lambda i, j, kk: (j // n_per, i, j % n_per))
        out_dims = (N_DEV, m, n // N_DEV)
    elif into is not None:
        out_spec = pl.BlockSpec((None, tm, tn), lambda i, j, kk: (into[1], i, j))
        out_dims = into[0].shape
    else:
        out_spec = pl.BlockSpec((tm, tn), lambda i, j, kk: (i, j))
        out_dims = (m if out_rows is None else out_rows, n)
    operands = [a, b, *[arr for _, arr in extras]]
    aliases = {}
    if into is not None:
        assert n_out == 1 and into[0].shape[1:] == (m, n) and into[0].dtype == out_dtypes[0]
        aliases = {len(operands): 0}
        operands.append(into[0])
    res = _pcall(
        body, name=name, grid=(m // tm, n // tn, nk),
        in_specs=[a_spec, b_spec] + ex_specs + [pl.BlockSpec(memory_space=pl.ANY)] * len(aliases),
        out_specs=[out_spec] * n_out,
        out_shape=[jax.ShapeDtypeStruct(out_dims, d) for d in out_dtypes],
        scratch_shapes=[pltpu.VMEM((tm, tn), F32)],
        input_output_aliases=aliases,
        compiler_params=_params(3),
    )(*operands)
    return res[0] if n_out == 1 else res


def _tail_rows(a, b, into, rows, name, tn=1024):
    k, n = b.shape
    m_total = into.shape[0]
    tn = _tile(n, tn)

    def body(a_ref, b_ref, into_ref, out_ref):
        out_ref[...] = _dot(a_ref[...], b_ref[...], ta=True)[:rows].astype(out_ref.dtype)

    return _pcall(
        body, name=name, grid=(n // tn,),
        in_specs=[pl.BlockSpec((k, a.shape[1]), lambda j: (0, 0)), pl.BlockSpec((k, tn), lambda j: (0, j)),
                  pl.BlockSpec(memory_space=pl.ANY)],
        out_specs=pl.BlockSpec((rows, tn), lambda j: (m_total // rows - 1, j)),
        out_shape=jax.ShapeDtypeStruct(into.shape, into.dtype),
        input_output_aliases={2: 0}, compiler_params=_params(1),
    )(a, b, into)


def _rowwise(fn, ins, outs, *, name, tr=128, into=None):
    rows = next(e[1].shape[0] for e in ins if e[0] != "full")
    tr = _tile(rows, tr)
    in_specs = []
    for entry in ins:
        kind, arr = entry[0], entry[1]
        assert kind == "full" or (arr.shape[0] == rows and arr.ndim == 2)
        if kind == "row":
            in_specs.append(pl.BlockSpec((tr, arr.shape[1]), lambda i: (i, 0)))
        elif kind == "cols":
            in_specs.append(pl.BlockSpec((tr, entry[3]), lambda i, cb=entry[2]: (i, cb)))
        else:
            in_specs.append(pl.BlockSpec(arr.shape, lambda i, nd=arr.ndim: (0,) * nd))
    out_specs, out_shape = [], []
    for entry in outs:
        kind, w, dt = entry[:3]
        if kind == "row":
            out_specs.append(pl.BlockSpec((tr, w), lambda i: (i, 0)))
            out_shape.append(jax.ShapeDtypeStruct((rows, w), dt))
        elif kind == "band":
            out_specs.append(pl.BlockSpec((tr, w), lambda i, cb=entry[3]: (i, cb)))
            out_shape.append(jax.ShapeDtypeStruct((rows, entry[4]), dt))
        else:
            out_specs.append(pl.BlockSpec((1, w), lambda i: (0, 0)))
            out_shape.append(jax.ShapeDtypeStruct((1, w), dt))
    n_in = len(ins)
    operands = [e[1] for e in ins]
    aliases = {}
    if into is not None:
        aliases = {len(operands): into[1]}
        in_specs.append(pl.BlockSpec(memory_space=pl.ANY))
        operands.append(into[0])

    def body(*refs):
        i = pl.program_id(0)
        vals = fn(*[r[...] for r in refs[:n_in]])
        for entry, o, v in zip(outs, refs[len(operands):], vals):
            if entry[0] == "acc":
                @pl.when(i == 0)
                def _(o=o):
                    o[...] = jnp.zeros_like(o)

                o[...] += v.astype(o.dtype)
            else:
                o[...] = v.astype(o.dtype)

    return _pcall(body, name=name, grid=(rows // tr,), in_specs=in_specs, out_specs=out_specs,
                  out_shape=out_shape, input_output_aliases=aliases, compiler_params=_params(1))(*operands)


def _colsum(x):
    return jnp.sum(x, axis=0, keepdims=True)


def _norm_stats(x):
    rstd = lax.rsqrt(jnp.mean(x * x, axis=-1, keepdims=True) + NORM_EPS)
    return x * rstd, rstd


def _norm_bwd(dxhat, xhat, rstd):
    return rstd * (dxhat - xhat * jnp.mean(dxhat * xhat, axis=-1, keepdims=True))


def _adaln_fwd(x, gain, sc, sh, name):
    def fn(x, gain, sc, sh):
        xhat, _ = _norm_stats(x)
        return ((xhat * gain) * (1.0 + sc) + sh,)

    return _rowwise(fn, [("row", x), ("full", gain), ("full", sc), ("full", sh)],
                    [("row", x.shape[1], BF16)], name=name)[0]


def _adaln_bwd(x, dh, dres, gain, sc, name, branch=None):
    d = x.shape[1]

    def fn(x, dh, dres, gain, sc, *br):
        xhat, rstd = _norm_stats(x)
        dxhat = dh * (gain * (1.0 + sc))
        dx = dres + _norm_bwd(dxhat, xhat, rstd)
        return (dx, _colsum(dh), _colsum(dh * (xhat * gain)), _colsum(dh * xhat * (1.0 + sc))) + _branch_bwd(dx, *br)

    return _rowwise(fn, [("row", x), ("row", dh), ("row", dres), ("full", gain), ("full", sc)] + _branch_ins(branch),
                    [("row", d, F32), ("acc", d, F32), ("acc", d, F32), ("acc", d, F32)] + _branch_outs(branch, d),
                    name=name)


def _branch_ins(branch):
    return [] if branch is None else [("row", branch[0]), ("full", branch[1])]


def _branch_outs(branch, d):
    return [] if branch is None else [("row", d, BF16), ("acc", d, F32)]


def _branch_bwd(dx, *branch):
    if not branch:
        return ()
    y, g = branch
    return dx * (1.0 + g), _colsum(dx * y)


def _final_loss(x, target, gain, name, branch):
    d = x.shape[1]

    def fn(x, t, gain, *br):
        xhat, rstd = _norm_stats(x)
        err = xhat * gain - t
        dy = err * (1.0 / d)
        loss = 0.5 * jnp.sum(jnp.mean(err * err, axis=-1, keepdims=True), axis=0, keepdims=True)
        dx = _norm_bwd(dy * gain, xhat, rstd)
        return (dx, _colsum(dy * xhat), jnp.broadcast_to(loss, (1, LANE))) + _branch_bwd(dx, *br)

    return _rowwise(fn, [("row", x), ("row", target), ("full", gain)] + _branch_ins(branch),
                    [("row", d, F32), ("acc", d, F32), ("acc", LANE, F32)] + _branch_outs(branch, d), name=name)


def _gla_gates(q, k, a, wg, bg, scale, c):
    ga = _dot(a, wg) + bg
    la = _log_sigmoid(ga) * (1.0 / GLA_TAU)
    b = _tri_matmul(_tri(c), la)
    bl = _colsum(la)
    eb, enb, eend = jnp.exp(b), jnp.exp(-b), jnp.exp(bl - b)
    q = q * scale
    return dict(ga=ga, eb=eb, enb=enb, eend=eend, dec=jnp.exp(bl), q_dec=q * eb, k_inv=k * enb, k_end=k * eend)


def _causal(c):
    return lax.broadcasted_iota(jnp.int32, (c, c), 0) >= lax.broadcasted_iota(jnp.int32, (c, c), 1)


def _gla_specs(heads, c, dk, dv, chunk):
    return [
        pl.BlockSpec((c, heads * dk), lambda n: (chunk(n), 0)),
        pl.BlockSpec((c, heads * dk), lambda n: (chunk(n), 1)),
        pl.BlockSpec((c, heads * dv), lambda n: (chunk(n), 1)),
        pl.BlockSpec((c, LANE), lambda n: (chunk(n), 0)),
        pl.BlockSpec((LANE, heads * dk), lambda n: (0, 0)),
        pl.BlockSpec((1, heads * dk), lambda n: (0, 0)),
    ]


def _gla_fwd(proj, a_tail, wg_p, bg, name):
    s = proj.shape[0]
    heads, c = GLA_HEADS, GLA_CHUNK
    dk = wg_p.shape[1] // heads
    dv = 2 * dk
    n_chunks = s // c
    scale = dk ** -0.5

    def body(q_ref, k_ref, v_ref, a_ref, wg_ref, bg_ref, o_ref, st_ref, state):
        @pl.when(pl.program_id(0) == 0)
        def _():
            state[...] = jnp.zeros_like(state)

        a = a_ref[...]
        for h in range(heads):
            sk, sv = slice(h * dk, (h + 1) * dk), slice(h * dv, (h + 1) * dv)
            g = _gla_gates(q_ref[:, sk], k_ref[:, sk], a, wg_ref[:, sk], bg_ref[:, sk], scale, c)
            v = v_ref[:, sv]
            st = state[h]
            attn = jnp.where(_causal(c), _dot(g["q_dec"], g["k_inv"], tb=True), 0.0)
            o_ref[:, sv] = _dot(attn, v) + _dot(g["q_dec"], st, tb=True)
            st_ref[h] = st.astype(st_ref.dtype)
            state[h] = g["dec"] * st + _dot(v, g["k_end"], ta=True)

    return _pcall(
        body, name=name, grid=(n_chunks,),
        in_specs=_gla_specs(heads, c, dk, dv, lambda n: n),
        out_specs=[pl.BlockSpec((c, heads * dv), lambda n: (n, 0)),
                   pl.BlockSpec((heads, None, dv, dk), lambda n: (0, n, 0, 0))],
        out_shape=[jax.ShapeDtypeStruct((s, heads * dv), F32),
                   jax.ShapeDtypeStruct((heads, n_chunks, dv, dk), BF16)],
        scratch_shapes=[pltpu.VMEM((heads, dv, dk), F32)],
        compiler_params=_params(1),
    )(proj, proj, proj, a_tail, wg_p, bg)


def _gla_bwd(proj, a_tail, wg_p, bg, states, d_o, dproj, name):
    s = proj.shape[0]
    heads, c = GLA_HEADS, GLA_CHUNK
    dk = wg_p.shape[1] // heads
    dv = 2 * dk
    n_chunks = s // c
    scale = dk ** -0.5
    k0, v0 = heads * dk, 2 * heads * dk

    def body(q_ref, k_ref, v_ref, a_ref, wg_ref, bg_ref, st_ref, do_ref, dproj_in, dqkv_ref, dga_ref, dstate):
        @pl.when(pl.program_id(0) == 0)
        def _():
            dstate[...] = jnp.zeros_like(dstate)

        a = a_ref[...]
        mask = _causal(c)
        for h in range(heads):
            sk, sv = slice(h * dk, (h + 1) * dk), slice(h * dv, (h + 1) * dv)
            out_k, out_v = slice(k0 + h * dk, k0 + (h + 1) * dk), slice(v0 + h * dv, v0 + (h + 1) * dv)
            g = _gla_gates(q_ref[:, sk], k_ref[:, sk], a, wg_ref[:, sk], bg_ref[:, sk], scale, c)
            v, st, dst, d_out = v_ref[:, sv], st_ref[h], dstate[h], do_ref[:, sv]
            q_dec, k_inv, k_end = g["q_dec"], g["k_inv"], g["k_end"]
            attn = jnp.where(mask, _dot(q_dec, k_inv, tb=True), 0.0)
            d_attn = jnp.where(mask, _dot(d_out, v, tb=True), 0.0)
            d_qdec = _dot(d_attn, k_inv) + _dot(d_out, st)
            d_kinv = _dot(d_attn, q_dec, ta=True)
            d_kend = _dot(v, dst)
            dqkv_ref[:, out_v] = (_dot(attn, d_out, ta=True) + _dot(k_end, dst, tb=True)).astype(dqkv_ref.dtype)
            d_dec = jnp.sum(dst * st.astype(F32), axis=0, keepdims=True)
            dstate[h] = g["dec"] * dst + _dot(d_out, q_dec, ta=True)

            dqkv_ref[:, sk] = (d_qdec * (scale * g["eb"])).astype(dqkv_ref.dtype)
            dqkv_ref[:, out_k] = (d_kinv * g["enb"] + d_kend * g["eend"]).astype(dqkv_ref.dtype)
            kk = d_kend * k_end
            db = d_qdec * q_dec - d_kinv * k_inv - kk
            dbl = jnp.sum(kk, axis=0, keepdims=True) + d_dec * g["dec"]
            last = lax.broadcasted_iota(jnp.int32, db.shape, 0) == c - 1
            db = db + jnp.where(last, dbl, 0.0)
            dla = _tri_matmul(_tri(c, upper=True), db)
            dga_ref[:, sk] = dla * (1.0 / GLA_TAU) * _sigmoid(-g["ga"])

    chunk = lambda n: n_chunks - 1 - n
    rev = lambda n: (chunk(n), 0)
    return _pcall(
        body, name=name, grid=(n_chunks,),
        in_specs=_gla_specs(heads, c, dk, dv, chunk) + [
            pl.BlockSpec((heads, None, dv, dk), lambda n: (0, chunk(n), 0, 0)),
            pl.BlockSpec((c, heads * dv), rev), pl.BlockSpec(memory_space=pl.ANY)],
        out_specs=[pl.BlockSpec((c, v0 + heads * dv), rev), pl.BlockSpec((c, heads * dk), rev)],
        out_shape=[jax.ShapeDtypeStruct(dproj.shape, dproj.dtype), jax.ShapeDtypeStruct((s, heads * dk), F32)],
        scratch_shapes=[pltpu.VMEM((heads, dv, dk), F32)],
        input_output_aliases={8: 0},
        compiler_params=_params(1),
    )(proj, proj, proj, a_tail, wg_p, bg, states, d_o, dproj)


def _gla_post_fwd(o, r, gn, name):
    dvt = o.shape[1]
    dv = dvt // GLA_HEADS

    def fn(o, r, gn):
        outs = []
        for h in range(GLA_HEADS):
            sl = slice(h * dv, (h + 1) * dv)
            ohat, _ = _norm_stats(o[:, sl])
            outs.append((ohat * gn[:, sl]) * _silu(r[:, sl]))
        return (jnp.concatenate(outs, axis=1),)

    return _rowwise(fn, [("row", o), r, ("full", gn)], [("row", dvt, BF16)], name=name)[0]


def _gla_post_bwd(o, r, gn, dog, name):
    dvt = o.shape[1]
    dv = dvt // GLA_HEADS

    def fn(o, r, gn, dog):
        d_o, d_r, d_g = [], [], []
        for h in range(GLA_HEADS):
            sl = slice(h * dv, (h + 1) * dv)
            ohat, rstd = _norm_stats(o[:, sl])
            g, rr, dd = gn[:, sl], r[:, sl], dog[:, sl]
            d_r.append(dd * (ohat * g) * _dsilu(rr))
            don = dd * _silu(rr)
            d_g.append(_colsum(don * ohat))
            d_o.append(_norm_bwd(don * g, ohat, rstd))
        return jnp.concatenate(d_o, axis=1), jnp.concatenate(d_r, axis=1), jnp.concatenate(d_g, axis=1)

    return _rowwise(fn, [("row", o), r, ("full", gn), ("row", dog)],
                    [("row", dvt, F32), ("band", dvt, BF16, 2, 3 * dvt), ("acc", dvt, F32)], name=name)


def _fox_prep(q, k, v, qg, kg, d, hd, name):
    heads = d // hd
    scale = hd ** -0.5

    def fn(q, k, v, qg, kg):
        qs, ks = [], []
        for h in range(heads):
            sl = slice(h * hd, (h + 1) * hd)
            qs.append(_norm_stats(q[:, sl])[0] * qg * scale)
            ks.append(_norm_stats(k[:, sl])[0] * kg)
        return jnp.concatenate(qs, axis=1), jnp.concatenate(ks, axis=1), v

    return _rowwise(fn, [q, k, v, ("full", qg), ("full", kg)],
                    [("row", d, BF16)] * 3, name=name)


def _fox_prep_bwd(q, k, dqn, dkn, qg, kg, hd, dproj, name):
    d = dqn.shape[1]
    heads = d // hd
    scale = hd ** -0.5

    def fn(q, k, dqn, dkn, qg, kg):
        dq, dk, gq, gk = [], [], [], []
        for h in range(heads):
            sl = slice(h * hd, (h + 1) * hd)
            for x, dxn, g, s, dl, gl in ((q, dqn, qg, scale, dq, gq), (k, dkn, kg, 1.0, dk, gk)):
                xhat, rstd = _norm_stats(x[:, sl])
                dn = dxn[:, sl] * s
                gl.append(_colsum(dn * xhat))
                dl.append(_norm_bwd(dn * g, xhat, rstd))
        cat = lambda t: jnp.concatenate(t, axis=1)
        return cat(dq + dk), cat(gq), cat(gk)

    return _rowwise(fn, [q, k, ("row", dqn), ("row", dkn), ("full", qg), ("full", kg)],
                    [("band", 2 * d, BF16, 0, 4 * d), ("acc", d, F32), ("acc", d, F32)], name=name, into=(dproj, 0))


def _fox_cum(fl, bf_p, name, tb=256):
    s = fl.shape[0]
    tb = _tile(s, tb)

    def body(fl_ref, bf_ref, cum_ref, carry):
        @pl.when(pl.program_id(0) == 0)
        def _():
            carry[...] = jnp.zeros_like(carry)

        lf = _log_sigmoid(fl_ref[...] + bf_ref[...])
        cum_ref[...] = _tri_matmul(_tri(tb), lf) + carry[...]
        carry[...] += _colsum(lf)

    return _pcall(
        body, name=name, grid=(s // tb,),
        in_specs=[pl.BlockSpec((tb, LANE), lambda i: (i, 0)), pl.BlockSpec((1, LANE), lambda i: (0, 0))],
        out_specs=pl.BlockSpec((tb, LANE), lambda i: (i, 0)),
        out_shape=jax.ShapeDtypeStruct((s, LANE), F32),
        scratch_shapes=[pltpu.VMEM((1, LANE), F32)],
        compiler_params=_params(1),
    )(fl, bf_p)


def _fox_cum_bwd(dcum, fl, bf_p, name, tb=256):
    s = fl.shape[0]
    tb = _tile(s, tb)
    nb = s // tb

    def body(dc_ref, fl_ref, bf_ref, dfl_ref, dbf_ref, carry):
        @pl.when(pl.program_id(0) == 0)
        def _():
            carry[...] = jnp.zeros_like(carry)
            dbf_ref[...] = jnp.zeros_like(dbf_ref)

        dc = dc_ref[...]
        dlf = _tri_matmul(_tri(tb, upper=True), dc) + carry[...]
        carry[...] += _colsum(dc)
        dfl = dlf * _sigmoid(-(fl_ref[...] + bf_ref[...]))
        dfl_ref[...] = dfl
        dbf_ref[...] += _colsum(dfl)

    rev = lambda i: (nb - 1 - i, 0)
    return _pcall(
        body, name=name, grid=(nb,),
        in_specs=[pl.BlockSpec((tb, LANE), rev), pl.BlockSpec((tb, LANE), rev), pl.BlockSpec((1, LANE), lambda i: (0, 0))],
        out_specs=[pl.BlockSpec((tb, LANE), rev), pl.BlockSpec((1, LANE), lambda i: (0, 0))],
        out_shape=[jax.ShapeDtypeStruct((s, LANE), F32), jax.ShapeDtypeStruct((1, LANE), F32)],
        scratch_shapes=[pltpu.VMEM((1, LANE), F32)],
        compiler_params=_params(1),
    )(dcum, fl, bf_p)


def _fox_attn_fwd(qn, kn, vb, cum_col, cum_row, hd, t, name):
    s, d = qn.shape
    heads = d // hd
    nq = s // t

    def body(q_ref, k_ref, v_ref, cc_ref, cr_ref, o_ref, lse_ref):
        qi = pl.program_id(1)
        q = q_ref[...]
        cq = cc_ref[...]
        qpos = qi * t + lax.broadcasted_iota(jnp.int32, (t, 1), 0)

        def step(kj, carry, diagonal=False):
            m, l, acc = carry
            off = pl.multiple_of(kj * t, t)
            ks, vs = k_ref[pl.ds(off, t), :], v_ref[pl.ds(off, t), :]
            sc = _dot(q, ks, tb=True) + cq - cr_ref[kj]
            if diagonal:
                kpos = off + lax.broadcasted_iota(jnp.int32, (1, t), 1)
                sc = jnp.where(kpos <= qpos, sc, NEG)
            m_new = jnp.maximum(m, jnp.max(sc, axis=1, keepdims=True))
            alpha = jnp.exp(m - m_new)
            p = jnp.exp(sc - m_new)
            return m_new, alpha * l + jnp.sum(p, axis=1, keepdims=True), alpha * acc + _dot(p, vs)

        init = (jnp.full((t, 1), NEG, F32), jnp.zeros((t, 1), F32), jnp.zeros((t, hd), F32))
        m, l, acc = step(qi, lax.fori_loop(0, qi, step, init), diagonal=True)
        o_ref[...] = acc / l
        lse_ref[...] = m + jnp.log(l)

    return _pcall(
        body, name=name, grid=(heads, nq),
        in_specs=[pl.BlockSpec((t, hd), lambda h, i: (i, h)),
                  pl.BlockSpec((s, hd), lambda h, i: (0, h)),
                  pl.BlockSpec((s, hd), lambda h, i: (0, h)),
                  pl.BlockSpec((None, t, 1), lambda h, i: (h, i, 0)),
                  pl.BlockSpec((None, nq, 1, t), lambda h, i: (h, 0, 0, 0))],
        out_specs=[pl.BlockSpec((t, hd), lambda h, i: (i, h)), pl.BlockSpec((None, t, 1), lambda h, i: (h, i, 0))],
        out_shape=[jax.ShapeDtypeStruct((s, d), F32), jax.ShapeDtypeStruct((heads, s, 1), F32)],
        compiler_params=_params(2),
    )(qn, kn, vb, cum_col, cum_row)


def _fox_attn_bwd(qn, kn, vb, d_o, o, lse, cum_col, cum_row, hd, t, dproj, name):
    s, d = qn.shape
    heads = d // hd
    nq = s // t

    def body(q_ref, k_ref, v_ref, do_ref, o_ref, lse_ref, cc_ref, cr_ref, dproj_in,
             dq_ref, dk_ref, dv_ref, dcq_ref, dck_ref, delta):
        kj = pl.program_id(1)

        @pl.when(kj == 0)
        def _():
            dq_ref[...] = jnp.zeros_like(dq_ref)
            dcq_ref[...] = jnp.zeros_like(dcq_ref)
            delta[...] = jnp.sum(do_ref[...] * o_ref[...], axis=1, keepdims=True)

        ks, vs, cr = k_ref[...], v_ref[...], cr_ref[...]
        kpos = kj * t + lax.broadcasted_iota(jnp.int32, (1, t), 1)

        def step(qi, carry, diagonal=False):
            dk, dv, dck = carry
            rows = pl.ds(pl.multiple_of(qi * t, t), t)
            q, d_out = q_ref[rows, :], do_ref[rows, :]
            sc = _dot(q, ks, tb=True) + cc_ref[rows, :] - cr
            p = jnp.exp(sc - lse_ref[rows, :])
            if diagonal:
                qpos = qi * t + lax.broadcasted_iota(jnp.int32, (t, 1), 0)
                p = jnp.where(kpos <= qpos, p, 0.0)
            ds = p * (_dot(d_out, vs, tb=True) - delta[rows, :])
            dq_ref[rows, :] += _dot(ds, ks)
            dcq_ref[rows, :] += jnp.sum(ds, axis=1, keepdims=True)
            return dk + _dot(ds, q, ta=True), dv + _dot(p, d_out, ta=True), dck + _colsum(ds)

        init = (jnp.zeros((t, hd), F32), jnp.zeros((t, hd), F32), jnp.zeros((1, t), F32))
        dk, dv, dck = lax.fori_loop(kj + 1, nq, step, step(kj, init, diagonal=True))
        dk_ref[...] = dk.astype(dk_ref.dtype)
        dv_ref[...] = dv.astype(dv_ref.dtype)
        dck_ref[...] = dck

    head_rows = lambda h, j: (0, h)
    blk = lambda h, j: (j, h)
    return _pcall(
        body, name=name, grid=(heads, nq),
        in_specs=[pl.BlockSpec((s, hd), head_rows), pl.BlockSpec((t, hd), blk), pl.BlockSpec((t, hd), blk),
                  pl.BlockSpec((s, hd), head_rows), pl.BlockSpec((s, hd), head_rows),
                  pl.BlockSpec((None, s, 1), lambda h, j: (h, 0, 0)),
                  pl.BlockSpec((None, s, 1), lambda h, j: (h, 0, 0)),
                  pl.BlockSpec((None, None, 1, t), lambda h, j: (h, j, 0, 0)),
                  pl.BlockSpec(memory_space=pl.ANY)],
        out_specs=[pl.BlockSpec((s, hd), head_rows), pl.BlockSpec((t, hd), blk),
                   pl.BlockSpec((t, hd), lambda h, j: (j, 2 * heads + h)),
                   pl.BlockSpec((None, s, 1), lambda h, j: (h, 0, 0)),
                   pl.BlockSpec((None, None, 1, t), lambda h, j: (h, j, 0, 0))],
        out_shape=[jax.ShapeDtypeStruct((s, d), F32), jax.ShapeDtypeStruct((s, d), BF16),
                   jax.ShapeDtypeStruct(dproj.shape, dproj.dtype), jax.ShapeDtypeStruct((heads, s, 1), F32),
                   jax.ShapeDtypeStruct((heads, nq, 1, t), F32)],
        scratch_shapes=[pltpu.VMEM((s, 1), F32)],
        input_output_aliases={8: 2},
        compiler_params=_params(2),
    )(qn, kn, vb, d_o, o, lse, cum_col, cum_row, dproj)


def _fox_gate_fwd(o, og, name):
    def fn(o, og):
        return (o * _sigmoid(og),)

    return _rowwise(fn, [("row", o), og], [("row", o.shape[1], BF16)], name=name)[0]


def _fox_gate_bwd(o, og, dact, name):
    def fn(o, og, dact):
        sg = _sigmoid(og)
        return dact * sg, dact * o * sg * (1.0 - sg)

    d = o.shape[1]
    return _rowwise(fn, [("row", o), og, ("row", dact)], [("row", d, F32), ("band", d, BF16, 3, 4 * d)], name=name)


def _shift_down(x, n):
    rows = lax.broadcasted_iota(jnp.int32, x.shape, 0)
    return jnp.where(rows >= n, pltpu.roll(x, n, 0), 0.0)


def _shift_up(x, n):
    rows = lax.broadcasted_iota(jnp.int32, x.shape, 0)
    return jnp.where(rows < x.shape[0] - n, pltpu.roll(x, x.shape[0] - n, 0), 0.0)


def _conv(u, w_ref, b):
    return w_ref[0:1, :] * _shift_down(u, 2) + w_ref[1:2, :] * _shift_down(u, 1) + w_ref[2:3, :] * u + b


def _conv_act_fwd(u, cw, cb, name, tc=256):
    s, two_f = u.shape
    dff = two_f // 2
    tc = _tile(dff, tc)
    nb = dff // tc

    def body(ug_ref, uv_ref, wg_ref, wv_ref, bg_ref, bv_ref, a_ref):
        gate = _conv(ug_ref[...], wg_ref, bg_ref[...])
        val = _conv(uv_ref[...], wv_ref, bv_ref[...])
        a_ref[...] = (_silu(gate) * val).astype(a_ref.dtype)

    lo, hi = (lambda j: (0, j)), (lambda j: (0, j + nb))
    return _pcall(
        body, name=name, grid=(nb,),
        in_specs=[pl.BlockSpec((s, tc), lo), pl.BlockSpec((s, tc), hi), pl.BlockSpec((3, tc), lo),
                  pl.BlockSpec((3, tc), hi), pl.BlockSpec((1, tc), lo), pl.BlockSpec((1, tc), hi)],
        out_specs=pl.BlockSpec((s, tc), lo),
        out_shape=jax.ShapeDtypeStruct((s, dff), BF16),
        compiler_params=_params(1),
    )(u, u, cw, cw, cb, cb)


def _conv_act_bwd(u, cw, cb, da, name, tc=128):
    s, two_f = u.shape
    dff = two_f // 2
    tc = _tile(dff, tc)
    nb = dff // tc

    def body(ug_ref, uv_ref, wg_ref, wv_ref, bg_ref, bv_ref, da_ref, du_ref, dw_ref, db_ref):
        ug, uv, da = ug_ref[...], uv_ref[...], da_ref[...]
        gate = _conv(ug, wg_ref, bg_ref[...])
        val = _conv(uv, wv_ref, bv_ref[...])
        sg = _sigmoid(gate)
        d_val = da * (gate * sg)
        d_gate = da * val * (sg * (1.0 + gate * (1.0 - sg)))
        for half, (dc, uu, w_ref) in enumerate(((d_gate, ug, wg_ref), (d_val, uv, wv_ref))):
            du = w_ref[0:1, :] * _shift_up(dc, 2) + w_ref[1:2, :] * _shift_up(dc, 1) + w_ref[2:3, :] * dc
            du_ref[half] = du.astype(du_ref.dtype)
            dw_ref[half, 0:1, :] = _colsum(dc * _shift_down(uu, 2))
            dw_ref[half, 1:2, :] = _colsum(dc * _shift_down(uu, 1))
            dw_ref[half, 2:3, :] = _colsum(dc * uu)
            db_ref[half] = _colsum(dc)

    lo, hi = (lambda j: (0, j)), (lambda j: (0, j + nb))
    both = lambda j: (0, 0, j)
    return _pcall(
        body, name=name, grid=(nb,),
        in_specs=[pl.BlockSpec((s, tc), lo), pl.BlockSpec((s, tc), hi), pl.BlockSpec((3, tc), lo),
                  pl.BlockSpec((3, tc), hi), pl.BlockSpec((1, tc), lo), pl.BlockSpec((1, tc), hi),
                  pl.BlockSpec((s, tc), lo)],
        out_specs=[pl.BlockSpec((2, s, tc), both), pl.BlockSpec((2, 3, tc), both), pl.BlockSpec((2, 1, tc), both)],
        out_shape=[jax.ShapeDtypeStruct((2, s, dff), BF16), jax.ShapeDtypeStruct((2, 3, dff), F32),
                   jax.ShapeDtypeStruct((2, 1, dff), F32)],
        compiler_params=_params(1),
    )(u, u, cw, cw, cb, cb, da)


def _adamw_math(w, g, m, v):
    m = ADAM_B1 * m + (1.0 - ADAM_B1) * g
    v = ADAM_B2 * v + (1.0 - ADAM_B2) * (g * g)
    m_hat = m / (1.0 - ADAM_B1 ** ADAM_STEP)
    v_hat = v / (1.0 - ADAM_B2 ** ADAM_STEP)
    delta = -ADAM_LR * (m_hat / (jnp.sqrt(v_hat) + ADAM_EPS) + ADAM_WD * w)
    return delta, m, v


def _update_tiles(r, c, tr):
    tc = c
    if r % 8:
        tr, tc = r, _tile(c, max(LANE, 512 * 1024 // r // LANE * LANE))
    elif r <= tr:
        tr = r
    while r % tr:
        tr -= 8
    return tr, tc


def _adamw(w, g, m, v, name, tr=128):
    layers, r, c = w.shape
    tr, tc = _update_tiles(r, c, tr)

    def body(w_ref, g_ref, m_ref, v_ref, go_ref, d_ref, mo_ref, vo_ref):
        grad = g_ref[...]
        delta, m_new, v_new = _adamw_math(w_ref[...], grad, m_ref[...], v_ref[...])
        go_ref[...], d_ref[...], mo_ref[...], vo_ref[...] = grad, delta, m_new, v_new

    spec = pl.BlockSpec((None, tr, tc), lambda l, i, j: (l, i, j))
    return _pcall(
        body, name=name, grid=(layers, r // tr, c // tc), in_specs=[spec] * 4, out_specs=[spec] * 4,
        out_shape=[jax.ShapeDtypeStruct((layers, r, c), F32)] * 4, compiler_params=_params(3),
    )(w, g, m, v)


def _adamw_pieces(w, lands, sums, chip, m, v, name, tr=128):
    layers, r, c = w.shape
    tr, tc = _update_tiles(r, c, tr)
    nr, nc = r // tr, c // tc

    def body(chip_ref, w_ref, *rest):
        land_refs, own_refs = rest[:layers], rest[layers:2 * layers]
        m_ref, v_ref, go_ref, d_ref, mo_ref, vo_ref = rest[2 * layers:]
        for layer in range(layers):
            @pl.when(pl.program_id(0) == layer)
            def _(land_ref=land_refs[layer], own_ref=own_refs[layer]):
                grad = jnp.zeros(w_ref.shape, F32)
                for q in range(4):
                    grad = grad + jnp.where(chip_ref[0] == q, own_ref[...], land_ref[q]).astype(F32)
                delta, m_new, v_new = _adamw_math(w_ref[...], grad, m_ref[...], v_ref[...])
                go_ref[...], d_ref[...], mo_ref[...], vo_ref[...] = grad, delta, m_new, v_new

    def walk(k, l, i, j):
        here = l == k
        return jnp.where(here, i, jnp.where(l < k, 0, nr - 1)), jnp.where(here, j, jnp.where(l < k, 0, nc - 1))

    spec = pl.BlockSpec((None, tr, tc), lambda l, i, j, chip_ref: (l, i, j))
    land_specs = [pl.BlockSpec((4, tr, tc), lambda l, i, j, chip_ref, k=k: (0,) + walk(k, l, i, j))
                  for k in range(layers)]
    own_specs = [pl.BlockSpec((None, tr, tc), lambda l, i, j, chip_ref, k=k: (chip_ref[0],) + walk(k, l, i, j))
                 for k in range(layers)]
    return _pcall(
        body, name=name,
        grid_spec=pltpu.PrefetchScalarGridSpec(
            num_scalar_prefetch=1, grid=(layers, nr, nc),
            in_specs=[spec] + land_specs + own_specs + [spec, spec], out_specs=[spec] * 4),
        out_shape=[jax.ShapeDtypeStruct((layers, r, c), F32)] * 4, compiler_params=_params(3),
    )(chip, w, *lands, *sums, m, v)


def _pair_sum(pieces, partner, core, name, tr=512):
    _, r, c = pieces.shape
    tc = c
    if r % 8:
        tr, tc = r, _tile(c, max(LANE, 1024 * 1024 // r // LANE * LANE))
    elif r <= tr:
        tr = r
    while r % tr:
        tr -= 8

    def body(core_ref, mine_ref, partner_ref, out_ref):
        out_ref[...] = (mine_ref[...].astype(F32) + partner_ref[...].astype(F32)).astype(out_ref.dtype)

    return _pcall(
        body, name=name,
        grid_spec=pltpu.PrefetchScalarGridSpec(
            num_scalar_prefetch=1, grid=(4, r // tr, c // tc),
            in_specs=[pl.BlockSpec((None, tr, tc), lambda q, i, j, core_ref: (2 * q + core_ref[0], i, j)),
                      pl.BlockSpec((None, tr, tc), lambda q, i, j, core_ref: (q, i, j))],
            out_specs=pl.BlockSpec((None, tr, tc), lambda q, i, j, core_ref: (q, i, j))),
        out_shape=jax.ShapeDtypeStruct((4, r, c), pieces.dtype), compiler_params=_params(3),
    )(core, pieces, partner)


def _sum8(x, name):
    p = x.shape[2]
    tp = _tile(p, 16 * 1024)

    def body(x_ref, o_ref):
        acc = x_ref[0]
        for i in range(1, N_DEV):
            acc = acc + x_ref[i]
        o_ref[...] = acc

    return _pcall(
        body, name=name, grid=(p // tp,), in_specs=[pl.BlockSpec((N_DEV, 1, tp), lambda i: (0, 0, i))],
        out_specs=pl.BlockSpec((1, tp), lambda i: (0, i)), out_shape=jax.ShapeDtypeStruct((1, p), x.dtype),
        compiler_params=_params(1),
    )(x)


def _exchange(arrays, name, scatter):
    n = len(arrays)
    hbm = pl.BlockSpec(memory_space=pl.ANY)

    def body(*refs):
        ins, outs, token = refs[:n], refs[n:2 * n], refs[2 * n]
        send_sems, recv_sems, local_sems = refs[2 * n + 1:]
        token[...] = jnp.zeros_like(token)
        x, y, c = lax.axis_index("x"), lax.axis_index("y"), lax.axis_index("c")
        me = 4 * x + 2 * y + c
        copies = []
        for a in range(n):
            src_mine = ins[a].at[me] if scatter else ins[a]
            local = pltpu.make_async_copy(src_mine, outs[a].at[me], local_sems.at[a])
            local.start()
            copies.append(local)
            for k in range(1, N_DEV):
                px = 1 - x if k & 4 else x
                py = 1 - y if k & 2 else y
                pc = 1 - c if k & 1 else c
                src = ins[a].at[4 * px + 2 * py + pc] if scatter else ins[a]
                cp = pltpu.make_async_remote_copy(
                    src_ref=src, dst_ref=outs[a].at[me],
                    send_sem=send_sems.at[a * (N_DEV - 1) + k - 1], recv_sem=recv_sems.at[a * (N_DEV - 1) + k - 1],
                    device_id=(px, py, pc), device_id_type=pl.DeviceIdType.MESH)
                cp.start()
                copies.append(cp)
        for cp in copies:
            cp.wait()

    out_shape = [jax.ShapeDtypeStruct(a.shape if scatter else (N_DEV,) + a.shape, a.dtype) for a in arrays]
    res = _pcall(
        body, name=name, in_specs=[hbm] * n, out_specs=[hbm] * n + [pl.BlockSpec(memory_space=pltpu.VMEM)],
        out_shape=out_shape + [jax.ShapeDtypeStruct((8, LANE), F32)],
        scratch_shapes=[pltpu.SemaphoreType.DMA((n * (N_DEV - 1),)), pltpu.SemaphoreType.DMA((n * (N_DEV - 1),)),
                        pltpu.SemaphoreType.DMA((n,))],
        compiler_params=pltpu.CompilerParams(has_side_effects=True),
    )(*arrays)
    return res[:n], res[n][0, 0]


_HBM = pl.BlockSpec(memory_space=pltpu.HBM)
_SEM = pl.BlockSpec(memory_space=pltpu.SEMAPHORE)
_DATAFLOW = pltpu.SideEffectType.DATAFLOW_SIDE_EFFECTING


def _peer(k, x, y, c):
    return (1 - x if k & 4 else x, 1 - y if k & 2 else y, 1 - c if k & 1 else c)


def _pair_plan(x, y, c):
    return [(2 * q + (1 - c), q, (x, y, 1 - c)) for q in range(4)]


def _chip_plan(x, y, c):
    out = []
    for k in _ICI_PEERS:
        px, py, pc = _peer(k, x, y, c)
        out.append((2 * px + py, 2 * x + y, (px, py, pc)))
    return out


def _all_plan(x, y, c):
    return [(0, 4 * x + 2 * y + c, _peer(k, x, y, c)) for k in range(1, N_DEV)]


def _split_start(arrays, plan, name, land_blocks=4):
    n = len(arrays)
    lands = [lax.empty((land_blocks,) + a.shape[1:], a.dtype) for a in arrays]
    n_copies = len(plan(0, 0, 0))

    def body(*refs):
        srcs, dsts = refs[:n], refs[n:2 * n]
        send_sems, recv_sems, token = refs[4 * n:5 * n], refs[5 * n:6 * n], refs[6 * n]
        copies = plan(lax.axis_index("x"), lax.axis_index("y"), lax.axis_index("c"))
        for a in range(n):
            for j, (src_block, dst_block, peer) in enumerate(copies):
                pltpu.make_async_remote_copy(
                    src_ref=srcs[a].at[src_block], dst_ref=dsts[a].at[dst_block],
                    send_sem=send_sems[a].at[j], recv_sem=recv_sems[a].at[j],
                    device_id=peer, device_id_type=pl.DeviceIdType.MESH).start()
        token[...] = jnp.zeros_like(token)

    sems = [pltpu.SemaphoreType.DMA((n_copies,))] * (2 * n)
    res = _pcall(
        body, name=name,
        in_specs=[_HBM] * (2 * n),
        out_specs=[_HBM] * (2 * n) + [_SEM] * (2 * n) + [pl.BlockSpec(memory_space=pltpu.VMEM)],
        out_shape=[pltpu.HBM(a.shape, a.dtype) for a in arrays] + [pltpu.HBM(l.shape, l.dtype) for l in lands]
        + sems + [jax.ShapeDtypeStruct((8, LANE), F32)],
        input_output_aliases={i: i for i in range(2 * n)},
        compiler_params=pltpu.CompilerParams(has_side_effects=_DATAFLOW),
    )(*[pltpu.with_memory_space_constraint(a, pltpu.HBM) for a in arrays],
      *[pltpu.with_memory_space_constraint(l, pltpu.HBM) for l in lands])
    handles = [(res[a], res[n + a], res[2 * n + a], res[3 * n + a]) for a in range(n)]
    return handles, res[4 * n][0, 0]


def _split_wait(handles, plan, after, name):
    n = len(handles)
    after = list(after) if isinstance(after, (list, tuple)) else [after]

    def body(*refs):
        srcs, dsts = refs[:n], refs[n:2 * n]
        send_sems, recv_sems = refs[2 * n:3 * n], refs[3 * n:4 * n]
        copies = plan(lax.axis_index("x"), lax.axis_index("y"), lax.axis_index("c"))
        for a in range(n):
            for j, (src_block, dst_block, peer) in enumerate(copies):
                cp = pltpu.make_async_remote_copy(
                    src_ref=srcs[a].at[src_block], dst_ref=dsts[a].at[dst_block],
                    send_sem=send_sems[a].at[j], recv_sem=recv_sems[a].at[j],
                    device_id=peer, device_id_type=pl.DeviceIdType.MESH)
                cp.wait_send()
                cp.wait_recv()

    srcs, lands = [h[0] for h in handles], [h[1] for h in handles]
    res = _pcall(
        body, name=name,
        in_specs=[_HBM] * (2 * n) + [_SEM] * (2 * n) + [pl.BlockSpec(memory_space=pl.ANY)] * len(after),
        out_specs=[_HBM] * (2 * n),
        out_shape=[pltpu.HBM(t.shape, t.dtype) for t in srcs + lands],
        input_output_aliases={i: i for i in range(2 * n)},
        compiler_params=pltpu.CompilerParams(has_side_effects=_DATAFLOW),
    )(*srcs, *lands, *[h[2] for h in handles], *[h[3] for h in handles], *after)
    return res[:n], res[n:]


_ICI_PEERS = (2, 4, 6)


def _gather2_start(shards, name):
    n = len(shards)
    lands = [lax.empty((N_DEV,) + a.shape, a.dtype) for a in shards]

    def body(*refs):
        srcs, dsts = refs[:n], refs[n:2 * n]
        send_sems, d2d_sems, ici_sems = refs[4 * n:5 * n], refs[5 * n:6 * n], refs[6 * n:7 * n]
        token = refs[7 * n]
        x, y, c = lax.axis_index("x"), lax.axis_index("y"), lax.axis_index("c")
        me = 4 * x + 2 * y + c
        for a in range(n):
            for j, k in enumerate((1,) + _ICI_PEERS):
                recv = d2d_sems[a].at[0] if j == 0 else ici_sems[a].at[j - 1]
                pltpu.make_async_remote_copy(
                    src_ref=srcs[a], dst_ref=dsts[a].at[me], send_sem=send_sems[a].at[j], recv_sem=recv,
                    device_id=_peer(k, x, y, c), device_id_type=pl.DeviceIdType.MESH).start()
        token[...] = jnp.zeros_like(token)

    dma = pltpu.SemaphoreType.DMA
    res = _pcall(
        body, name=name,
        in_specs=[_HBM] * (2 * n),
        out_specs=[_HBM] * (2 * n) + [_SEM] * (3 * n) + [pl.BlockSpec(memory_space=pltpu.VMEM)],
        out_shape=[pltpu.HBM(a.shape, a.dtype) for a in shards] + [pltpu.HBM(l.shape, l.dtype) for l in lands]
        + [dma((4,))] * n + [dma((1,))] * n + [dma((3,))] * n + [jax.ShapeDtypeStruct((8, LANE), F32)],
        input_output_aliases={i: i for i in range(2 * n)},
        compiler_params=pltpu.CompilerParams(has_side_effects=_DATAFLOW),
    )(*[pltpu.with_memory_space_constraint(a, pltpu.HBM) for a in shards],
      *[pltpu.with_memory_space_constraint(l, pltpu.HBM) for l in lands])
    handles = [tuple(res[i * n + a] for i in range(5)) for a in range(n)]
    return handles, res[5 * n][0, 0]


def _gather2_forward(handle, after, name):
    src, land, send_sems, d2d_sem, ici_sems = handle

    def body(land_ref, ici_ref, d2d_ref, after_ref, land_out, fwd_send, fwd_recv, token):
        x, y, c = lax.axis_index("x"), lax.axis_index("y"), lax.axis_index("c")
        sibling = (x, y, 1 - c)
        arrived = [(_peer(k, x, y, c), ici_ref.at[j]) for j, k in enumerate(_ICI_PEERS)] + [(sibling, d2d_ref.at[0])]
        for j, ((px, py, pc), recv) in enumerate(arrived):
            block = land_ref.at[4 * px + 2 * py + pc]
            pltpu.make_async_remote_copy(
                src_ref=block, dst_ref=block, send_sem=fwd_send.at[j], recv_sem=recv,
                device_id=(px, py, pc), device_id_type=pl.DeviceIdType.MESH).wait_recv()
            pltpu.make_async_remote_copy(
                src_ref=block, dst_ref=block, send_sem=fwd_send.at[j], recv_sem=fwd_recv.at[j],
                device_id=sibling, device_id_type=pl.DeviceIdType.MESH).start()
        token[...] = jnp.zeros_like(token)

    dma = pltpu.SemaphoreType.DMA
    land, fwd_send, fwd_recv, token = _pcall(
        body, name=name,
        in_specs=[_HBM, _SEM, _SEM, pl.BlockSpec(memory_space=pl.ANY)],
        out_specs=[_HBM, _SEM, _SEM, pl.BlockSpec(memory_space=pltpu.VMEM)],
        out_shape=[pltpu.HBM(land.shape, land.dtype), dma((4,)), dma((4,)), jax.ShapeDtypeStruct((8, LANE), F32)],
        input_output_aliases={0: 0},
        compiler_params=pltpu.CompilerParams(has_side_effects=_DATAFLOW),
    )(land, ici_sems, d2d_sem, after)
    return (src, land, send_sems, fwd_send, fwd_recv), token[0, 0]


def _gather2_wait(handle, after, name):
    src, land, send_sems, fwd_send, fwd_recv = handle

    def body(src_ref, land_ref, send_ref, fsend_ref, frecv_ref, after_ref, src_out, land_out):
        x, y, c = lax.axis_index("x"), lax.axis_index("y"), lax.axis_index("c")
        block = land_ref.at[4 * x + 2 * y + c]

        def copy(send, recv):
            return pltpu.make_async_remote_copy(src_ref=src_ref, dst_ref=block, send_sem=send, recv_sem=recv,
                                                device_id=(x, y, 1 - c), device_id_type=pl.DeviceIdType.MESH)

        for j in range(4):
            copy(send_ref.at[j], frecv_ref.at[j]).wait_send()
        for j in range(4):
            copy(fsend_ref.at[j], frecv_ref.at[j]).wait_send()
            copy(fsend_ref.at[j], frecv_ref.at[j]).wait_recv()

    res = _pcall(
        body, name=name,
        in_specs=[_HBM, _HBM, _SEM, _SEM, _SEM, pl.BlockSpec(memory_space=pl.ANY)],
        out_specs=[_HBM, _HBM],
        out_shape=[pltpu.HBM(src.shape, src.dtype), pltpu.HBM(land.shape, land.dtype)],
        input_output_aliases={0: 0, 1: 1},
        compiler_params=pltpu.CompilerParams(has_side_effects=_DATAFLOW),
    )(src, land, send_sems, fwd_send, fwd_recv, after)
    return res[0], res[1]


def _pad_cols(x, width=LANE):
    return jnp.pad(x, ((0, 0), (0, width - x.shape[1])))


def _cols_full(g):
    return jnp.transpose(g, (1, 0, 2)).reshape(g.shape[1], -1)


def _ffn_fwd(x1, p, i, tag):
    h2 = _adaln_fwd(x1, p["norm_ffn"][i], p["sc_f"][i], p["sh_f"][i], f"ffn_norm_{tag}")
    u = _matmul(h2, p["fetch"](f"up{i}", h2), name=f"ffn_up_{tag}", tn=1408, b_shards=True)
    a = _conv_act_fwd(u, p["conv_w"][i], p["conv_b"][i], f"ffn_act_{tag}")
    g_f = p["g_f"][i]
    x2, f = _matmul(a, p["fetch"](f"down{i}", a), name=f"ffn_down_{tag}", tk=1408, out_dtypes=(F32, F32),
                    epilogue=lambda acc, x1, g: (x1 + (1.0 + g) * acc, acc), extras=(("mn", x1), ("n", g_f)))
    return x2, dict(h2=h2, u=u, a=a, f=f)


def _ffn_bwd(incoming, x1, saved, p, i, tag, branch):
    d = x1.shape[1]
    dx2, df, dg_f = incoming
    w_up, w_down = p["fetch"](f"up{i}", None), p["fetch"](f"down{i}", None)
    da = _matmul(df, w_down, tb=True, name=f"ffn_down_dx_{tag}", tn=1408)
    dw_down = _matmul(saved["a"], df, ta=True, name=f"ffn_down_dw_{tag}", tm=1408, out_dtypes=(BF16,))
    du, dcw, dcb = _conv_act_bwd(saved["u"], p["conv_w"][i], p["conv_b"][i], da, f"ffn_act_bwd_{tag}")
    dcw, dcb = (jnp.concatenate([t[0], t[1]], axis=1) for t in (dcw, dcb))
    tok = p["flush"](du)
    dh2 = _matmul(du, w_up, tb=True, name=f"ffn_up_dx_{tag}", tn=2048, tk=1408, a_halves=True, b_shards=True)
    dw_up = _matmul(saved["h2"], du, ta=True, name=f"ffn_up_dw_{tag}", tn=1408, out_dtypes=(BF16,), b_halves=True,
                    out_shards=True)
    tok = tok + p["send"](f"ffn{i}", [dw_up, dw_down.reshape(N_DEV, -1, d)])
    dx1, dsh, dsc, dgain, dy, dg_m = _adaln_bwd(x1, dh2, dx2, p["norm_ffn"][i] + tok, p["sc_f"][i],
                                                f"ffn_norm_bwd_{tag}", branch)
    grads = dict(conv_w=dcw, conv_b=dcb, norm_ffn=dgain, sh_f=dsh, sc_f=dsc, g_f=dg_f)
    return (dx1, dy, dg_m), grads


def _gla_layer_fwd(x, p, i):
    h1 = _adaln_fwd(x, p["norm_mix"][i], p["sc_m"][i], p["sh_m"][i], "gla_norm")
    w_t, w_tail_t, main = p["fetch"]("gla_in", h1)
    proj = _matmul(h1, w_t, tb=True, b_rows=main, tm=2048, name="gla_in")
    a_tail = _matmul(h1, w_tail_t, tb=True, name="gla_in_tail")
    dk_total = p["gla_wg_p"].shape[1]
    o, states = _gla_fwd(proj, a_tail, p["gla_wg_p"], p["gla_b_gate"], "gla_chunks")
    assert 2 * dk_total == o.shape[1]
    r = ("cols", proj, 2, o.shape[1])
    og = _gla_post_fwd(o, r, p["gla_norm"], "gla_post")
    x1, y = _matmul(og, p["fetch"]("gla_out", og), name="gla_out", out_dtypes=(F32, F32),
                    epilogue=lambda acc, x, g: (x + (1.0 + g) * acc, acc), extras=(("mn", x), ("n", p["g_m"][i])))
    return x1, dict(h1=h1, proj=proj, a_tail=a_tail, o=o, r=r, states=states, og=og, y=y)


def _gla_layer_bwd(incoming, x, sv, p, i, branch):
    d = x.shape[1]
    dx1, dy, dg_m = incoming
    (w_t, w_tail_t, main), w_out = p["fetch"]("gla_in", None), p["fetch"]("gla_out", None)
    dog = _matmul(dy, w_out, tb=True, name="gla_out_dx")
    dw_out = _matmul(sv["og"], dy, ta=True, name="gla_out_dw", out_dtypes=(BF16,))
    tok = p["flush"](dog) + p["send"]("gla_out", [dw_out.reshape(N_DEV, -1, d)])
    d_o, dproj, dgn = _gla_post_bwd(sv["o"], sv["r"], p["gla_norm"] + tok, dog, "gla_post_bwd")
    dproj, dga = _gla_bwd(sv["proj"], sv["a_tail"], p["gla_wg_p"], p["gla_b_gate"], sv["states"], d_o, dproj,
                          "gla_chunks_bwd")
    tok = p["flush"](dga)
    da_tail = _matmul(dga, p["gla_wg_p"], tb=True, name="gla_gate_dx", out_dtypes=(BF16,))
    dwg = _matmul(sv["a_tail"], dga, ta=True, name="gla_gate_dw")
    dbg = _rowwise(lambda t: (_colsum(t),), [("row", dga)], [("acc", dga.shape[1], F32)], name="gla_gate_db")[0]
    dh_tail = _matmul(da_tail, w_tail_t, name="gla_in_tail_dx")
    dh1 = _matmul(dproj, w_t, b_rows=main, name="gla_in_dx", tk=2048,
                  epilogue=lambda acc, t: (acc + t,), extras=(("mn", dh_tail),))
    rank = p["gla_rank"]
    dw_main = _matmul(dproj, sv["h1"], ta=True, name="gla_in_dw", out_dtypes=(BF16,), out_rows=main + rank)
    dx, dsh, dsc, dgain, *into_branch = _adaln_bwd(x, dh1, dx1, p["norm_mix"][i] + tok, p["sc_m"][i], "gla_norm_bwd",
                                                   branch)
    grads = dict(gla_w_gate=dwg[:rank], gla_b_gate=dbg, gla_norm=dgn, norm_mix=dgain, sh_m=dsh, sc_m=dsc, g_m=dg_m,
                 gla_w_in_unsent=(dw_main, da_tail, sv["h1"]))
    return (dx, *into_branch), grads


def _fox_layer_fwd(x, p, i):
    d = x.shape[1]
    hd = p["fox_q_norm"].shape[1]
    heads = d // hd
    s = x.shape[0]
    t = _tile(s, 512)
    h1 = _adaln_fwd(x, p["norm_mix"][i], p["sc_m"][i], p["sh_m"][i], "fox_norm")
    w_t, w_tail_t, main = p["fetch"]("fox_in", h1)
    proj = _matmul(h1, w_t, tb=True, b_rows=main, tm=2048, name="fox_in")
    fl = _matmul(h1, w_tail_t, tb=True, name="fox_in_tail")
    q, k, v, og = (("cols", proj, j, d) for j in range(4))
    qn, kn, vb = _fox_prep(q, k, v, p["fox_q_norm"], p["fox_k_norm"], d, hd, "fox_prep")
    cum = _fox_cum(fl, p["fox_bf_p"], "fox_cum")
    cum_t = jnp.transpose(cum[:, :heads])
    cum_col, cum_row = cum_t[:, :, None], cum_t.reshape(heads, s // t, 1, t)
    o, lse = _fox_attn_fwd(qn, kn, vb, cum_col, cum_row, hd, t, "fox_attn")
    act = _fox_gate_fwd(o, og, "fox_gate")
    x1, y = _matmul(act, p["fetch"]("fox_out", act), name="fox_out", out_dtypes=(F32, F32),
                    epilogue=lambda acc, x, g: (x + (1.0 + g) * acc, acc), extras=(("mn", x), ("n", p["g_m"][i])))
    return x1, dict(h1=h1, q=q, k=k, og=og, fl=fl, qn=qn, kn=kn, vb=vb, cum_col=cum_col, cum_row=cum_row,
                    o=o, lse=lse, act=act, y=y, t=t, hd=hd)


def _fox_layer_bwd(incoming, x, sv, p, i, branch):
    d = x.shape[1]
    hd, t = sv["hd"], sv["t"]
    heads = d // hd
    s = x.shape[0]
    dx1, dy, dg_m = incoming
    (w_t, w_tail_t, main), w_out = p["fetch"]("fox_in", None), p["fetch"]("fox_out", None)
    dact = _matmul(dy, w_out, tb=True, name="fox_out_dx")
    dw_out = _matmul(sv["act"], dy, ta=True, name="fox_out_dw", out_dtypes=(BF16,))
    d_o, dproj = _fox_gate_bwd(sv["o"], sv["og"], dact, "fox_gate_bwd")
    tok_flush = p["flush"](d_o)
    dqn, dkn, dproj, dcq, dck = _fox_attn_bwd(sv["qn"], sv["kn"], sv["vb"], d_o, sv["o"], sv["lse"], sv["cum_col"],
                                              sv["cum_row"], hd, t, dproj, "fox_attn_bwd")
    dproj, gq, gk = _fox_prep_bwd(sv["q"], sv["k"], dqn, dkn, p["fox_q_norm"], p["fox_k_norm"], hd, dproj,
                                  "fox_prep_bwd")
    dcum = _pad_cols(jnp.transpose(dcq[:, :, 0] - dck.reshape(heads, s)))
    dfl, dbf = _fox_cum_bwd(dcum, sv["fl"], p["fox_bf_p"], "fox_cum_bwd")
    dfl_b = dfl.astype(BF16)
    dh_tail = _matmul(dfl_b, w_tail_t, name="fox_in_tail_dx")
    dh1 = _matmul(dproj, w_t, b_rows=main, name="fox_in_dx", tk=2048,
                  epilogue=lambda acc, tl: (acc + tl,), extras=(("mn", dh_tail),))
    dw_main = _matmul(dproj, sv["h1"], ta=True, name="fox_in_dw", out_dtypes=(BF16,), out_rows=main + heads)
    dw_in = _tail_rows(dfl_b, sv["h1"], dw_main, heads, "fox_in_tail_dw").reshape(N_DEV, -1, d)
    tok = tok_flush + p["send"]("fox", [dw_in, dw_out.reshape(N_DEV, -1, d)])
    dx, dsh, dsc, dgain, *into_branch = _adaln_bwd(x, dh1, dx1, p["norm_mix"][i] + tok, p["sc_m"][i], "fox_norm_bwd",
                                                   branch)
    grads = dict(fox_b_f=dbf[:, :heads], fox_q_norm=gq.reshape(heads, hd).sum(0, keepdims=True),
                 fox_k_norm=gk.reshape(heads, hd).sum(0, keepdims=True), norm_mix=dgain, sh_m=dsh, sc_m=dsc, g_m=dg_m)
    return (dx, *into_branch), grads


SMALL = ("b_mod", "norm_mix", "norm_ffn", "gla_b_gate", "gla_norm", "fox_b_f", "fox_q_norm", "fox_k_norm",
         "ffn_conv_b", "norm_final")
SMALL_SHARDED = ("gla_w_gate", "ffn_conv_w")
BIG = ("gla_w_in", "gla_w_out", "fox_w_in", "fox_w_out", "ffn_w_up", "ffn_w_down")
WEIGHTS = ("w_mod", "b_mod", "norm_mix", "norm_ffn", "gla_w_in", "gla_w_gate", "gla_b_gate", "gla_norm", "gla_w_out",
           "fox_w_in", "fox_b_f", "fox_q_norm", "fox_k_norm", "fox_w_out", "ffn_w_up", "ffn_conv_w", "ffn_conv_b",
           "ffn_w_down", "norm_final")


def _pack(parts):
    flat = jnp.concatenate([p.reshape(-1) for p in parts])
    pad = (-flat.shape[0]) % 1024
    return jnp.pad(flat, (0, pad)).reshape(1, -1)


def _unpack(flat, shapes):
    out, off = [], 0
    for shp in shapes:
        n = 1
        for s in shp:
            n *= s
        out.append(flat[0, off:off + n].reshape(shp))
        off += n
    return out


def kernel(x, c, w_mod, b_mod, norm_mix, norm_ffn, gla_w_in, gla_w_gate, gla_b_gate, gla_norm, gla_w_out, fox_w_in, fox_b_f, fox_q_norm, fox_k_norm, fox_w_out, ffn_w_up, ffn_conv_w, ffn_conv_b, ffn_w_down, norm_final, loss_target, m_w_mod, m_b_mod, m_norm_mix, m_norm_ffn, m_gla_w_in, m_gla_w_gate, m_gla_b_gate, m_gla_norm, m_gla_w_out, m_fox_w_in, m_fox_b_f, m_fox_q_norm, m_fox_k_norm, m_fox_w_out, m_ffn_w_up, m_ffn_conv_w, m_ffn_conv_b, m_ffn_w_down, m_norm_final, v_w_mod, v_b_mod, v_norm_mix, v_norm_ffn, v_gla_w_in, v_gla_w_gate, v_gla_b_gate, v_gla_norm, v_gla_w_out, v_fox_w_in, v_fox_b_f, v_fox_q_norm, v_fox_k_norm, v_fox_w_out, v_ffn_w_up, v_ffn_conv_w, v_ffn_conv_b, v_ffn_w_down, v_norm_final):
    w = dict(w_mod=w_mod, b_mod=b_mod, norm_mix=norm_mix, norm_ffn=norm_ffn, gla_w_in=gla_w_in, gla_w_gate=gla_w_gate,
             gla_b_gate=gla_b_gate, gla_norm=gla_norm, gla_w_out=gla_w_out, fox_w_in=fox_w_in, fox_b_f=fox_b_f,
             fox_q_norm=fox_q_norm, fox_k_norm=fox_k_norm, fox_w_out=fox_w_out, ffn_w_up=ffn_w_up,
             ffn_conv_w=ffn_conv_w, ffn_conv_b=ffn_conv_b, ffn_w_down=ffn_w_down, norm_final=norm_final)
    mom_m = dict(w_mod=m_w_mod, b_mod=m_b_mod, norm_mix=m_norm_mix, norm_ffn=m_norm_ffn, gla_w_in=m_gla_w_in,
                 gla_w_gate=m_gla_w_gate, gla_b_gate=m_gla_b_gate, gla_norm=m_gla_norm, gla_w_out=m_gla_w_out,
                 fox_w_in=m_fox_w_in, fox_b_f=m_fox_b_f, fox_q_norm=m_fox_q_norm, fox_k_norm=m_fox_k_norm,
                 fox_w_out=m_fox_w_out, ffn_w_up=m_ffn_w_up, ffn_conv_w=m_ffn_conv_w, ffn_conv_b=m_ffn_conv_b,
                 ffn_w_down=m_ffn_w_down, norm_final=m_norm_final)
    mom_v = dict(w_mod=v_w_mod, b_mod=v_b_mod, norm_mix=v_norm_mix, norm_ffn=v_norm_ffn, gla_w_in=v_gla_w_in,
                 gla_w_gate=v_gla_w_gate, gla_b_gate=v_gla_b_gate, gla_norm=v_gla_norm, gla_w_out=v_gla_w_out,
                 fox_w_in=v_fox_w_in, fox_b_f=v_fox_b_f, fox_q_norm=v_fox_q_norm, fox_k_norm=v_fox_k_norm,
                 fox_w_out=v_fox_w_out, ffn_w_up=v_ffn_w_up, ffn_conv_w=v_ffn_conv_w, ffn_conv_b=v_ffn_conv_b,
                 ffn_w_down=v_ffn_w_down, norm_final=v_norm_final)

    me = 4 * lax.axis_index("x") + 2 * lax.axis_index("y") + lax.axis_index("c")
    xs, target = x[0], loss_target[0]
    s, d = xs.shape
    depth = w_mod.shape[0]
    mod_cols = w_mod.shape[2]
    rank = gla_w_gate.shape[1]
    hd = fox_q_norm.shape[1]
    fox_heads = d // hd
    dk_total = gla_w_gate.shape[2] * N_DEV

    cond = c * (1.0 / (1.0 + jnp.exp(-c)))
    g, _ = _exchange([gla_w_gate[0], ffn_conv_w, cond], "gather_small", scatter=False)
    cond_all = g[2][:, 0, :]

    cond_pad = jnp.pad(cond_all, ((0, 16 - N_DEV), (0, 0)))
    mod_part = []
    for i in range(depth):
        b_cols = lax.dynamic_slice(b_mod[i:i + 1], (0, me * mod_cols), (1, mod_cols))
        mod_part.append(_matmul(cond_pad, w_mod, b_layer=i, name=f"mod_{i}", tn=768,
                                epilogue=lambda acc, b: (acc + b,), extras=(("n", b_cols),))[:N_DEV])
    (mod_all,), tok_mod = _exchange([jnp.stack(mod_part)], "gather_mod", scatter=False)
    mod = lax.dynamic_index_in_dim(mod_all, me, axis=2, keepdims=False)
    mod = jnp.transpose(mod, (1, 0, 2)).reshape(depth, 6, 1, d)

    big_names = ["gla_in", "gla_out", "up0", "down0", "fox_in", "fox_out", "up1", "down1"]
    first = [jnp.transpose(gla_w_in[0] + tok_mod).astype(BF16), gla_w_out[0].astype(BF16)]
    handles, tok_first = _gather2_start(first, "gather_weights_start_first")
    rest = [ffn_w_up[0] + tok_first, ffn_w_down[0], jnp.transpose(fox_w_in[0]), fox_w_out[0], ffn_w_up[1],
            ffn_w_down[1]]
    handles_rest, tok0 = _gather2_start([t.astype(BF16) for t in rest], "gather_weights_start_rest")
    handles = handles + handles_rest
    ready, forwarded = {}, {}

    def split_tail(full_t, tail):
        main = full_t.shape[0] - tail
        return full_t, jnp.pad(full_t[main:], ((0, LANE - tail), (0, 0))), main

    def forward(idx, after):
        key = big_names[idx]
        forwarded[key] = _gather2_forward(handles[idx], after, f"gather_{key}_forward")

    def fetch(key, after):
        if key not in ready:
            idx = big_names.index(key)
            if idx == 0:
                forward(0, after)
            handle, _ = forwarded[key]
            _, full = _gather2_wait(handle, after, f"gather_{key}_wait")
            if idx + 1 < len(big_names):
                forward(idx + 1, full)
            if key == "gla_in":
                ready[key] = split_tail(full.reshape(-1, d), rank)
            elif key == "fox_in":
                ready[key] = split_tail(full.reshape(-1, d), fox_heads)
            elif key.startswith("up"):
                ready[key] = full
            else:
                ready[key] = full.reshape(-1, d)
        return ready[key]

    pending, sent = [], {}
    core = lax.axis_index("c").astype(jnp.int32).reshape(1)
    chip = 2 * lax.axis_index("x") + lax.axis_index("y")

    def send(key, pieces):
        hs, tok = _split_start(pieces, _pair_plan, f"scatter_{key}_pair_start")
        pending.append((key, hs))
        return tok

    def flush(after):
        tok = 0.0
        while pending:
            key, hs = pending.pop(0)
            mine, partner = _split_wait(hs, _pair_plan, after, f"scatter_{key}_pair_wait")
            sums = [_pair_sum(pc, pt, core, f"scatter_{key}_pair_sum{a}")
                    for a, (pc, pt) in enumerate(zip(mine, partner))]
            sent[key], t = _split_start(sums, _chip_plan, f"scatter_{key}_chip_start")
            tok = tok + t
        return tok

    p = dict(
        fetch=fetch, send=send, flush=flush,
        gla_wg_p=jnp.pad(_cols_full(g[0]), ((0, LANE - rank), (0, 0))),
        conv_w=[jnp.transpose(g[1][:, i], (1, 0, 2)).reshape(ffn_conv_w.shape[1], -1) for i in range(depth)],
        conv_b=[ffn_conv_b[i:i + 1] for i in range(depth)],
        gla_b_gate=gla_b_gate, gla_norm=gla_norm, fox_q_norm=fox_q_norm, fox_k_norm=fox_k_norm,
        fox_bf_p=_pad_cols(fox_b_f), gla_rank=rank,
        norm_mix=[norm_mix[i:i + 1] + (tok0 if i == 0 else 0.0) for i in range(depth)],
        norm_ffn=[norm_ffn[i:i + 1] for i in range(depth)],
    )

    for j, nm in enumerate(("sh_m", "sc_m", "g_m", "sh_f", "sc_f", "g_f")):
        p[nm] = [mod[i, j] for i in range(depth)]

    acts, saved = [xs], []
    for i in range(depth):
        layer_fwd = _gla_layer_fwd if i % 2 == 0 else _fox_layer_fwd
        x1, sv_mix = layer_fwd(acts[-1], p, i)
        x2, sv_ffn = _ffn_fwd(x1, p, i, str(i))
        saved.append((acts[-1], x1, sv_mix, sv_ffn))
        acts.append(x2)
    last_ffn = (saved[-1][3]["f"], p["g_f"][depth - 1])
    dx, d_norm_final, loss_part, *into_branch = _final_loss(acts[-1], target, norm_final.reshape(1, d), "final_loss",
                                                            last_ffn)
    incoming = (dx, *into_branch)

    lg = [None] * depth
    for i in reversed(range(depth)):
        x_in, x1, sv_mix, sv_ffn = saved[i]
        incoming, g_ffn = _ffn_bwd(incoming, x1, sv_ffn, p, i, str(i), (sv_mix["y"], p["g_m"][i]))
        layer_bwd = _gla_layer_bwd if i % 2 == 0 else _fox_layer_bwd
        before = (saved[i - 1][3]["f"], p["g_f"][i - 1]) if i else None
        incoming, g_mix = layer_bwd(incoming, x_in, sv_mix, p, i, before)
        lg[i] = {**g_ffn, **g_mix}
    grad_x = incoming[0][None]

    gla_l = [i for i in range(depth) if i % 2 == 0]
    fox_l = [i for i in range(depth) if i % 2 == 1]
    small_parts = dict(
        norm_mix=jnp.concatenate([lg[i]["norm_mix"] for i in range(depth)]),
        norm_ffn=jnp.concatenate([lg[i]["norm_ffn"] for i in range(depth)]),
        gla_b_gate=jnp.concatenate([lg[i]["gla_b_gate"] for i in gla_l]),
        gla_norm=jnp.concatenate([lg[i]["gla_norm"] for i in gla_l]),
        fox_b_f=jnp.concatenate([lg[i]["fox_b_f"] for i in fox_l]),
        fox_q_norm=jnp.concatenate([lg[i]["fox_q_norm"] for i in fox_l]),
        fox_k_norm=jnp.concatenate([lg[i]["fox_k_norm"] for i in fox_l]),
        ffn_conv_b=jnp.concatenate([lg[i]["conv_b"] for i in range(depth)]),
        norm_final=d_norm_final,
        gla_w_gate=jnp.stack([lg[i]["gla_w_gate"] for i in gla_l]),
        ffn_conv_w=jnp.stack([lg[i]["conv_w"] for i in range(depth)]),
        loss=loss_part[:, :1],
    )
    order = ("norm_mix", "norm_ffn", "gla_b_gate", "gla_norm", "fox_b_f", "fox_q_norm", "fox_k_norm", "ffn_conv_b",
             "norm_final", "gla_w_gate", "ffn_conv_w", "loss")
    packed = _pack([small_parts[nm] for nm in order])
    dmod = jnp.stack([jnp.concatenate([lg[i][nm] for nm in ("sh_m", "sc_m", "g_m", "sh_f", "sc_f", "g_f")], axis=1)
                      for i in range(depth)])
    hs_small, tok_small = _split_start([packed[None], dmod[None]], _all_plan, "gather_small_grads_start",
                                       land_blocks=N_DEV)
    dw_main, da_tail, h1_gla = lg[0]["gla_w_in_unsent"]
    dw_in_t = _tail_rows(da_tail + tok_small.astype(BF16), h1_gla, dw_main, rank, "gla_in_tail_dw")
    send("gla_in", [dw_in_t.reshape(N_DEV, -1, d)])
    started = pending[-1][1][0][0]

    received = {}

    def arrive(key, after):
        sums, lands = _split_wait(sent[key], _chip_plan, after, f"scatter_{key}_chip_wait")
        received[key] = list(zip(lands, sums))

    for key in ("ffn1", "fox", "ffn0", "gla_out"):
        arrive(key, started)

    out_g, out_d, out_m, out_v = {}, {}, {}, {}

    chip_idx = chip.astype(jnp.int32).reshape(1)

    def update(nm, g_arr, transposed=False):
        swap = (lambda t: jnp.transpose(t, (0, 2, 1))) if transposed else (lambda t: t)
        if isinstance(g_arr, list):
            res = _adamw_pieces(swap(w[nm]), [t[0] for t in g_arr], [t[1] for t in g_arr], chip_idx,
                                swap(mom_m[nm]), swap(mom_v[nm]), f"adamw_{nm}")
        else:
            res = _adamw(w[nm], g_arr, mom_m[nm], mom_v[nm], f"adamw_{nm}")
        out_g[nm], out_d[nm], out_m[nm], out_v[nm] = (swap(t) for t in res)

    update("gla_w_out", [received["gla_out"][0]])
    update("fox_w_out", [received["fox"][1]])
    tok_flush = flush(out_g["fox_w_out"])
    update("ffn_w_up", [received[f"ffn{i}"][0] for i in range(depth)])
    update("fox_w_in", [received["fox"][0]], transposed=True)
    update("ffn_w_down", [received[f"ffn{i}"][1] for i in range(depth)])

    updated = ("gla_w_out", "fox_w_in", "fox_w_out", "ffn_w_up", "ffn_w_down")
    (packed_mine, dmod_mine), (packed_all, dmod_all) = _split_wait(
        hs_small, _all_plan, [out_d[nm] for nm in updated], "gather_small_grads_wait")
    packed_all = lax.dynamic_update_slice(packed_all, packed_mine + tok_flush, (me, 0, 0))
    dmod_all = lax.dynamic_update_slice(dmod_all, dmod_mine, (me, 0, 0, 0))
    summed = _unpack(_sum8(packed_all, "sum_small_grads"), [small_parts[nm].shape for nm in order])
    small_g = dict(zip(order, summed))
    loss = small_g["loss"][0, 0]
    dmod_all = dmod_all[:, :, 0, :]
    grads = {}
    cond_t = _pad_cols(jnp.transpose(cond_all)).astype(BF16)
    dmod_cols = lax.dynamic_slice(dmod_all, (0, 0, me * mod_cols), (N_DEV, depth, mod_cols))
    g_w_mod = lax.empty(w_mod.shape, F32)
    for i in range(depth):
        rhs = jnp.pad(dmod_cols[:, i], ((0, LANE - N_DEV), (0, 0)))
        g_w_mod = _matmul(cond_t, rhs, name=f"mod_dw_{i}", tn=768, into=(g_w_mod, i))
    grads["w_mod"] = g_w_mod
    small_g["b_mod"] = _sum8(dmod_all.reshape(N_DEV, 1, -1), "sum_b_mod").reshape(depth, -1)
    update("w_mod", grads["w_mod"])

    gate_cols = gla_w_gate.shape[2]
    conv_cols = ffn_conv_w.shape[2]
    local_small = dict(small_g)
    local_small["gla_w_gate"] = lax.dynamic_slice_in_dim(small_g["gla_w_gate"], me * gate_cols, gate_cols, axis=2)
    local_small["ffn_conv_w"] = lax.dynamic_slice_in_dim(small_g["ffn_conv_w"], me * conv_cols, conv_cols, axis=2)
    names = SMALL + SMALL_SHARDED
    shapes = [w[nm].shape for nm in names]
    res = _adamw(_pack([w[nm] for nm in names])[None], _pack([local_small[nm] for nm in names])[None],
                 _pack([mom_m[nm] for nm in names])[None], _pack([mom_v[nm] for nm in names])[None], "adamw_small")
    for tgt, flat in zip((out_g, out_d, out_m, out_v), res):
        for nm, arr in zip(names, _unpack(flat[0], shapes)):
            tgt[nm] = arr

    arrive("gla_in", [out_d[nm] for nm in updated + ("w_mod",)])
    update("gla_w_in", [received["gla_in"][0]], transposed=True)

    return (loss, grad_x, *[out_g[n] for n in WEIGHTS], *[out_d[n] for n in WEIGHTS],
            *[out_m[n] for n in WEIGHTS], *[out_v[n] for n in WEIGHTS])
```

```python
import jax
import jax.numpy as jnp
from jax import lax
from jax.experimental import pallas as pl
from jax.experimental.pallas import tpu as pltpu

F32, BF16 = jnp.float32, jnp.bfloat16
N_DEV = 8
GLA_HEADS = 4
GLA_TAU = 16.0
GLA_CHUNK = 64
NORM_EPS = 1e-6
ADAM_LR, ADAM_B1, ADAM_B2, ADAM_EPS, ADAM_WD, ADAM_STEP = 0.001, 0.9, 0.999, 1e-08, 0.01, 10
LANE = 128
VMEM_LIMIT = 56 * 1024 * 1024
NEG = -1e30


def _pcall(body, **kw):
    return pl.pallas_call(body, **kw)


def _params(n_axes):
    return pltpu.CompilerParams(dimension_semantics=("arbitrary",) * n_axes, vmem_limit_bytes=VMEM_LIMIT)


def _tile(dim, pref):
    if dim <= pref:
        return dim
    t = pref
    while dim % t:
        t -= LANE
    assert t > 0, (dim, pref)
    return t


def _dot(a, b, ta=False, tb=False):
    dims = (((0,) if ta else (1,), (1,) if tb else (0,)), ((), ()))
    return lax.dot_general(a.astype(BF16), b.astype(BF16), dims, preferred_element_type=F32)


def _split3(x):
    hi = x.astype(BF16)
    r1 = x - hi.astype(F32)
    mid = r1.astype(BF16)
    lo = (r1 - mid.astype(F32)).astype(BF16)
    return hi, mid, lo


def _tri_matmul(tri, x):
    hi, mid, lo = _split3(x)
    return _dot(tri, hi) + _dot(tri, mid) + _dot(tri, lo)


def _tri(n, upper=False):
    r = lax.broadcasted_iota(jnp.int32, (n, n), 0)
    c = lax.broadcasted_iota(jnp.int32, (n, n), 1)
    return jnp.where((r <= c) if upper else (r >= c), 1.0, 0.0).astype(BF16)


def _log_sigmoid(x):
    return jnp.minimum(x, 0.0) - jnp.log(1.0 + jnp.exp(-jnp.abs(x)))


def _sigmoid(x):
    return 1.0 / (1.0 + jnp.exp(-x))


def _silu(x):
    return x * _sigmoid(x)


def _dsilu(x):
    s = _sigmoid(x)
    return s * (1.0 + x * (1.0 - s))


def _matmul(a, b, *, name, ta=False, tb=False, out_dtypes=(F32,), tm=1024, tn=1024, tk=2048,
            epilogue=None, extras=(), a_halves=False, b_halves=False, b_shards=False, out_shards=False,
            b_rows=None, out_rows=None, b_layer=None, into=None):
    if a_halves:
        assert not ta
        m, k = a.shape[1], 2 * a.shape[2]
    else:
        m, k = (a.shape[1], a.shape[0]) if ta else a.shape
    if b_halves:
        assert not tb and b.shape[1] == k
        n = 2 * b.shape[2]
    elif b_shards:
        n = b.shape[1] if tb else N_DEV * b.shape[2]
        assert (N_DEV * b.shape[2] if tb else b.shape[1]) == k, (a.shape, b.shape, ta, tb)
    elif b_layer is not None:
        assert not tb and b.shape[1] == k
        n = b.shape[2]
    else:
        rows = b.shape[0] if b_rows is None else b_rows
        n = rows if tb else b.shape[1]
        assert (b.shape[1] if tb else rows) == k, (a.shape, b.shape, ta, tb)
    n_unit = n // N_DEV if (out_shards or (b_shards and not tb)) else (n // 2 if b_halves else n)
    k_unit = k // N_DEV if (b_shards and tb) else (k // 2 if a_halves else k)
    tm, tn, tk = _tile(m, tm), _tile(n_unit, tn), _tile(k_unit, tk)
    nk = k // tk
    if a_halves:
        a_spec = pl.BlockSpec((None, tm, tk), lambda i, j, kk: (kk // (nk // 2), i, kk % (nk // 2)))
    elif ta:
        a_spec = pl.BlockSpec((tk, tm), lambda i, j, kk: (kk, i))
    else:
        a_spec = pl.BlockSpec((tm, tk), lambda i, j, kk: (i, kk))
    n_per, k_per = n // tn // N_DEV, nk // N_DEV
    if b_halves:
        b_spec = pl.BlockSpec((None, tk, tn), lambda i, j, kk: (j // (n // tn // 2), kk, j % (n // tn // 2)))
    elif b_shards and tb:
        b_spec = pl.BlockSpec((None, tn, tk), lambda i, j, kk: (kk // k_per, j, kk % k_per))
    elif b_shards:
        b_spec = pl.BlockSpec((None, tk, tn), lambda i, j, kk: (j // n_per, kk, j % n_per))
    elif b_layer is not None:
        b_spec = pl.BlockSpec((None, tk, tn), lambda i, j, kk: (b_layer, kk, j))
    elif tb:
        b_spec = pl.BlockSpec((tn, tk), lambda i, j, kk: (j, kk))
    else:
        b_spec = pl.BlockSpec((tk, tn), lambda i, j, kk: (kk, j))
    ex_specs = []
    for kind, arr in extras:
        if kind == "mn":
            assert arr.shape == (m, n), (arr.shape, m, n)
            ex_specs.append(pl.BlockSpec((tm, tn), lambda i, j, kk: (i, j)))
        else:
            assert arr.shape == (1, n), (arr.shape, n)
            ex_specs.append(pl.BlockSpec((1, tn), lambda i, j, kk: (0, j)))
    n_ex, n_out = len(extras), len(out_dtypes)

    def body(a_ref, b_ref, *rest):
        ex, outs, acc = rest[:n_ex], rest[-1 - n_out:-1], rest[-1]
        kk = pl.program_id(2)

        @pl.when(kk == 0)
        def _():
            acc[...] = jnp.zeros_like(acc)

        acc[...] += _dot(a_ref[...], b_ref[...], ta, tb)

        @pl.when(kk == nk - 1)
        def _():
            if epilogue is None:
                vals = (acc[...],)
            else:
                vals = epilogue(acc[...], *[e[...] for e in ex])
            for o, v in zip(outs, vals):
                o[...] = v.astype(o.dtype)

    if out_shards:
        out_spec = pl.BlockSpec((None, tm, tn), lambda i, j, kk: (j // n_per, i, j % n_per))
        out_dims = (N_DEV, m, n // N_DEV)
    elif into is not None:
        out_spec = pl.BlockSpec((None, tm, tn), lambda i, j, kk: (into[1], i, j))
        out_dims = into[0].shape
    else:
        out_spec = pl.BlockSpec((tm, tn), lambda i, j, kk: (i, j))
        out_dims = (m if out_rows is None else out_rows, n)
    operands = [a, b, *[arr for _, arr in extras]]
    aliases = {}
    if into is not None:
        assert n_out == 1 and into[0].shape[1:] == (m, n) and into[0].dtype == out_dtypes[0]
        aliases = {len(operands): 0}
        operands.append(into[0])
    res = _pcall(
        body, name=name, grid=(m // tm, n // tn, nk),
        in_specs=[a_spec, b_spec] + ex_specs + [pl.BlockSpec(memory_space=pl.ANY)] * len(aliases),
        out_specs=[out_spec] * n_out,
        out_shape=[jax.ShapeDtypeStruct(out_dims, d) for d in out_dtypes],
        scratch_shapes=[pltpu.VMEM((tm, tn), F32)],
        input_output_aliases=aliases,
        compiler_params=_params(3),
    )(*operands)
    return res[0] if n_out == 1 else res


def _tail_rows(a, b, into, rows, name, tn=1024):
    k, n = b.shape
    m_total = into.shape[0]
    tn = _tile(n, tn)

    def body(a_ref, b_ref, into_ref, out_ref):
        out_ref[...] = _dot(a_ref[...], b_ref[...], ta=True)[:rows].astype(out_ref.dtype)

    return _pcall(
        body, name=name, grid=(n // tn,),
        in_specs=[pl.BlockSpec((k, a.shape[1]), lambda j: (0, 0)), pl.BlockSpec((k, tn), lambda j: (0, j)),
                  pl.BlockSpec(memory_space=pl.ANY)],
        out_specs=pl.BlockSpec((rows, tn), lambda j: (m_total // rows - 1, j)),
        out_shape=jax.ShapeDtypeStruct(into.shape, into.dtype),
        input_output_aliases={2: 0}, compiler_params=_params(1),
    )(a, b, into)


def _rowwise(fn, ins, outs, *, name, tr=128, into=None):
    rows = next(e[1].shape[0] for e in ins if e[0] != "full")
    tr = _tile(rows, tr)
    in_specs = []
    for entry in ins:
        kind, arr = entry[0], entry[1]
        assert kind == "full" or (arr.shape[0] == rows and arr.ndim == 2)
        if kind == "row":
            in_specs.append(pl.BlockSpec((tr, arr.shape[1]), lambda i: (i, 0)))
        elif kind == "cols":
            in_specs.append(pl.BlockSpec((tr, entry[3]), lambda i, cb=entry[2]: (i, cb)))
        else:
            in_specs.append(pl.BlockSpec(arr.shape, lambda i, nd=arr.ndim: (0,) * nd))
    out_specs, out_shape = [], []
    for entry in outs:
        kind, w, dt = entry[:3]
        if kind == "row":
            out_specs.append(pl.BlockSpec((tr, w), lambda i: (i, 0)))
            out_shape.append(jax.ShapeDtypeStruct((rows, w), dt))
        elif kind == "band":
            out_specs.append(pl.BlockSpec((tr, w), lambda i, cb=entry[3]: (i, cb)))
            out_shape.append(jax.ShapeDtypeStruct((rows, entry[4]), dt))
        else:
            out_specs.append(pl.BlockSpec((1, w), lambda i: (0, 0)))
            out_shape.append(jax.ShapeDtypeStruct((1, w), dt))
    n_in = len(ins)
    operands = [e[1] for e in ins]
    aliases = {}
    if into is not None:
        aliases = {len(operands): into[1]}
        in_specs.append(pl.BlockSpec(memory_space=pl.ANY))
        operands.append(into[0])

    def body(*refs):
        i = pl.program_id(0)
        vals = fn(*[r[...] for r in refs[:n_in]])
        for entry, o, v in zip(outs, refs[len(operands):], vals):
            if entry[0] == "acc":
                @pl.when(i == 0)
                def _(o=o):
                    o[...] = jnp.zeros_like(o)

                o[...] += v.astype(o.dtype)
            else:
                o[...] = v.astype(o.dtype)

    return _pcall(body, name=name, grid=(rows // tr,), in_specs=in_specs, out_specs=out_specs,
                  out_shape=out_shape, input_output_aliases=aliases, compiler_params=_params(1))(*operands)


def _colsum(x):
    return jnp.sum(x, axis=0, keepdims=True)


def _norm_stats(x):
    rstd = lax.rsqrt(jnp.mean(x * x, axis=-1, keepdims=True) + NORM_EPS)
    return x * rstd, rstd


def _norm_bwd(dxhat, xhat, rstd):
    return rstd * (dxhat - xhat * jnp.mean(dxhat * xhat, axis=-1, keepdims=True))


def _adaln_fwd(x, gain, sc, sh, name):
    def fn(x, gain, sc, sh):
        xhat, _ = _norm_stats(x)
        return ((xhat * gain) * (1.0 + sc) + sh,)

    return _rowwise(fn, [("row", x), ("full", gain), ("full", sc), ("full", sh)],
                    [("row", x.shape[1], BF16)], name=name)[0]


def _adaln_bwd(x, dh, dres, gain, sc, name, branch=None):
    d = x.shape[1]

    def fn(x, dh, dres, gain, sc, *br):
        xhat, rstd = _norm_stats(x)
        dxhat = dh * (gain * (1.0 + sc))
        dx = dres + _norm_bwd(dxhat, xhat, rstd)
        return (dx, _colsum(dh), _colsum(dh * (xhat * gain)), _colsum(dh * xhat * (1.0 + sc))) + _branch_bwd(dx, *br)

    return _rowwise(fn, [("row", x), ("row", dh), ("row", dres), ("full", gain), ("full", sc)] + _branch_ins(branch),
                    [("row", d, F32), ("acc", d, F32), ("acc", d, F32), ("acc", d, F32)] + _branch_outs(branch, d),
                    name=name)


def _branch_ins(branch):
    return [] if branch is None else [("row", branch[0]), ("full", branch[1])]


def _branch_outs(branch, d):
    return [] if branch is None else [("row", d, BF16), ("acc", d, F32)]


def _branch_bwd(dx, *branch):
    if not branch:
        return ()
    y, g = branch
    return dx * (1.0 + g), _colsum(dx * y)


def _final_loss(x, target, gain, name, branch):
    d = x.shape[1]

    def fn(x, t, gain, *br):
        xhat, rstd = _norm_stats(x)
        err = xhat * gain - t
        dy = err * (1.0 / d)
        loss = 0.5 * jnp.sum(jnp.mean(err * err, axis=-1, keepdims=True), axis=0, keepdims=True)
        dx = _norm_bwd(dy * gain, xhat, rstd)
        return (dx, _colsum(dy * xhat), jnp.broadcast_to(loss, (1, LANE))) + _branch_bwd(dx, *br)

    return _rowwise(fn, [("row", x), ("row", target), ("full", gain)] + _branch_ins(branch),
                    [("row", d, F32), ("acc", d, F32), ("acc", LANE, F32)] + _branch_outs(branch, d), name=name)


def _gla_gates(q, k, a, wg, bg, scale, c):
    ga = _dot(a, wg) + bg
    la = _log_sigmoid(ga) * (1.0 / GLA_TAU)
    b = _tri_matmul(_tri(c), la)
    bl = _colsum(la)
    eb, enb, eend = jnp.exp(b), jnp.exp(-b), jnp.exp(bl - b)
    q = q * scale
    return dict(ga=ga, eb=eb, enb=enb, eend=eend, dec=jnp.exp(bl), q_dec=q * eb, k_inv=k * enb, k_end=k * eend)


def _causal(c):
    return lax.broadcasted_iota(jnp.int32, (c, c), 0) >= lax.broadcasted_iota(jnp.int32, (c, c), 1)


def _gla_specs(heads, c, dk, dv, chunk):
    return [
        pl.BlockSpec((c, heads * dk), lambda n: (chunk(n), 0)),
        pl.BlockSpec((c, heads * dk), lambda n: (chunk(n), 1)),
        pl.BlockSpec((c, heads * dv), lambda n: (chunk(n), 1)),
        pl.BlockSpec((c, LANE), lambda n: (chunk(n), 0)),
        pl.BlockSpec((LANE, heads * dk), lambda n: (0, 0)),
        pl.BlockSpec((1, heads * dk), lambda n: (0, 0)),
    ]


def _gla_fwd(proj, a_tail, wg_p, bg, name):
    s = proj.shape[0]
    heads, c = GLA_HEADS, GLA_CHUNK
    dk = wg_p.shape[1] // heads
    dv = 2 * dk
    n_chunks = s // c
    scale = dk ** -0.5

    def body(q_ref, k_ref, v_ref, a_ref, wg_ref, bg_ref, o_ref, st_ref, state):
        @pl.when(pl.program_id(0) == 0)
        def _():
            state[...] = jnp.zeros_like(state)

        a = a_ref[...]
        for h in range(heads):
            sk, sv = slice(h * dk, (h + 1) * dk), slice(h * dv, (h + 1) * dv)
            g = _gla_gates(q_ref[:, sk], k_ref[:, sk], a, wg_ref[:, sk], bg_ref[:, sk], scale, c)
            v = v_ref[:, sv]
            st = state[h]
            attn = jnp.where(_causal(c), _dot(g["q_dec"], g["k_inv"], tb=True), 0.0)
            o_ref[:, sv] = _dot(attn, v) + _dot(g["q_dec"], st, tb=True)
            st_ref[h] = st.astype(st_ref.dtype)
            state[h] = g["dec"] * st + _dot(v, g["k_end"], ta=True)

    return _pcall(
        body, name=name, grid=(n_chunks,),
        in_specs=_gla_specs(heads, c, dk, dv, lambda n: n),
        out_specs=[pl.BlockSpec((c, heads * dv), lambda n: (n, 0)),
                   pl.BlockSpec((heads, None, dv, dk), lambda n: (0, n, 0, 0))],
        out_shape=[jax.ShapeDtypeStruct((s, heads * dv), F32),
                   jax.ShapeDtypeStruct((heads, n_chunks, dv, dk), BF16)],
        scratch_shapes=[pltpu.VMEM((heads, dv, dk), F32)],
        compiler_params=_params(1),
    )(proj, proj, proj, a_tail, wg_p, bg)


def _gla_bwd(proj, a_tail, wg_p, bg, states, d_o, dproj, name):
    s = proj.shape[0]
    heads, c = GLA_HEADS, GLA_CHUNK
    dk = wg_p.shape[1] // heads
    dv = 2 * dk
    n_chunks = s // c
    scale = dk ** -0.5
    k0, v0 = heads * dk, 2 * heads * dk

    def body(q_ref, k_ref, v_ref, a_ref, wg_ref, bg_ref, st_ref, do_ref, dproj_in, dqkv_ref, dga_ref, dstate):
        @pl.when(pl.program_id(0) == 0)
        def _():
            dstate[...] = jnp.zeros_like(dstate)

        a = a_ref[...]
        mask = _causal(c)
        for h in range(heads):
            sk, sv = slice(h * dk, (h + 1) * dk), slice(h * dv, (h + 1) * dv)
            out_k, out_v = slice(k0 + h * dk, k0 + (h + 1) * dk), slice(v0 + h * dv, v0 + (h + 1) * dv)
            g = _gla_gates(q_ref[:, sk], k_ref[:, sk], a, wg_ref[:, sk], bg_ref[:, sk], scale, c)
            v, st, dst, d_out = v_ref[:, sv], st_ref[h], dstate[h], do_ref[:, sv]
            q_dec, k_inv, k_end = g["q_dec"], g["k_inv"], g["k_end"]
            attn = jnp.where(mask, _dot(q_dec, k_inv, tb=True), 0.0)
            d_attn = jnp.where(mask, _dot(d_out, v, tb=True), 0.0)
            d_qdec = _dot(d_attn, k_inv) + _dot(d_out, st)
            d_kinv = _dot(d_attn, q_dec, ta=True)
            d_kend = _dot(v, dst)
            dqkv_ref[:, out_v] = (_dot(attn, d_out, ta=True) + _dot(k_end, dst, tb=True)).astype(dqkv_ref.dtype)
            d_dec = jnp.sum(dst * st.astype(F32), axis=0, keepdims=True)
            dstate[h] = g["dec"] * dst + _dot(d_out, q_dec, ta=True)

            dqkv_ref[:, sk] = (d_qdec * (scale * g["eb"])).astype(dqkv_ref.dtype)
            dqkv_ref[:, out_k] = (d_kinv * g["enb"] + d_kend * g["eend"]).astype(dqkv_ref.dtype)
            kk = d_kend * k_end
            db = d_qdec * q_dec - d_kinv * k_inv - kk
            dbl = jnp.sum(kk, axis=0, keepdims=True) + d_dec * g["dec"]
            last = lax.broadcasted_iota(jnp.int32, db.shape, 0) == c - 1
            db = db + jnp.where(last, dbl, 0.0)
            dla = _tri_matmul(_tri(c, upper=True), db)
            dga_ref[:, sk] = dla * (1.0 / GLA_TAU) * _sigmoid(-g["ga"])

    chunk = lambda n: n_chunks - 1 - n
    rev = lambda n: (chunk(n), 0)
    return _pcall(
        body, name=name, grid=(n_chunks,),
        in_specs=_gla_specs(heads, c, dk, dv, chunk) + [
            pl.BlockSpec((heads, None, dv, dk), lambda n: (0, chunk(n), 0, 0)),
            pl.BlockSpec((c, heads * dv), rev), pl.BlockSpec(memory_space=pl.ANY)],
        out_specs=[pl.BlockSpec((c, v0 + heads * dv), rev), pl.BlockSpec((c, heads * dk), rev)],
        out_shape=[jax.ShapeDtypeStruct(dproj.shape, dproj.dtype), jax.ShapeDtypeStruct((s, heads * dk), F32)],
        scratch_shapes=[pltpu.VMEM((heads, dv, dk), F32)],
        input_output_aliases={8: 0},
        compiler_params=_params(1),
    )(proj, proj, proj, a_tail, wg_p, bg, states, d_o, dproj)


def _gla_post_fwd(o, r, gn, name):
    dvt = o.shape[1]
    dv = dvt // GLA_HEADS

    def fn(o, r, gn):
        outs = []
        for h in range(GLA_HEADS):
            sl = slice(h * dv, (h + 1) * dv)
            ohat, _ = _norm_stats(o[:, sl])
            outs.append((ohat * gn[:, sl]) * _silu(r[:, sl]))
        return (jnp.concatenate(outs, axis=1),)

    return _rowwise(fn, [("row", o), r, ("full", gn)], [("row", dvt, BF16)], name=name)[0]


def _gla_post_bwd(o, r, gn, dog, name):
    dvt = o.shape[1]
    dv = dvt // GLA_HEADS

    def fn(o, r, gn, dog):
        d_o, d_r, d_g = [], [], []
        for h in range(GLA_HEADS):
            sl = slice(h * dv, (h + 1) * dv)
            ohat, rstd = _norm_stats(o[:, sl])
            g, rr, dd = gn[:, sl], r[:, sl], dog[:, sl]
            d_r.append(dd * (ohat * g) * _dsilu(rr))
            don = dd * _silu(rr)
            d_g.append(_colsum(don * ohat))
            d_o.append(_norm_bwd(don * g, ohat, rstd))
        return jnp.concatenate(d_o, axis=1), jnp.concatenate(d_r, axis=1), jnp.concatenate(d_g, axis=1)

    return _rowwise(fn, [("row", o), r, ("full", gn), ("row", dog)],
                    [("row", dvt, F32), ("band", dvt, BF16, 2, 3 * dvt), ("acc", dvt, F32)], name=name)


def _fox_prep(q, k, v, qg, kg, d, hd, name):
    heads = d // hd
    scale = hd ** -0.5

    def fn(q, k, v, qg, kg):
        qs, ks = [], []
        for h in range(heads):
            sl = slice(h * hd, (h + 1) * hd)
            qs.append(_norm_stats(q[:, sl])[0] * qg * scale)
            ks.append(_norm_stats(k[:, sl])[0] * kg)
        return jnp.concatenate(qs, axis=1), jnp.concatenate(ks, axis=1), v

    return _rowwise(fn, [q, k, v, ("full", qg), ("full", kg)],
                    [("row", d, BF16)] * 3, name=name)


def _fox_prep_bwd(q, k, dqn, dkn, qg, kg, hd, dproj, name):
    d = dqn.shape[1]
    heads = d // hd
    scale = hd ** -0.5

    def fn(q, k, dqn, dkn, qg, kg):
        dq, dk, gq, gk = [], [], [], []
        for h in range(heads):
            sl = slice(h * hd, (h + 1) * hd)
            for x, dxn, g, s, dl, gl in ((q, dqn, qg, scale, dq, gq), (k, dkn, kg, 1.0, dk, gk)):
                xhat, rstd = _norm_stats(x[:, sl])
                dn = dxn[:, sl] * s
                gl.append(_colsum(dn * xhat))
                dl.append(_norm_bwd(dn * g, xhat, rstd))
        cat = lambda t: jnp.concatenate(t, axis=1)
        return cat(dq + dk), cat(gq), cat(gk)

    return _rowwise(fn, [q, k, ("row", dqn), ("row", dkn), ("full", qg), ("full", kg)],
                    [("band", 2 * d, BF16, 0, 4 * d), ("acc", d, F32), ("acc", d, F32)], name=name, into=(dproj, 0))


def _fox_cum(fl, bf_p, name, tb=256):
    s = fl.shape[0]
    tb = _tile(s, tb)

    def body(fl_ref, bf_ref, cum_ref, carry):
        @pl.when(pl.program_id(0) == 0)
        def _():
            carry[...] = jnp.zeros_like(carry)

        lf = _log_sigmoid(fl_ref[...] + bf_ref[...])
        cum_ref[...] = _tri_matmul(_tri(tb), lf) + carry[...]
        carry[...] += _colsum(lf)

    return _pcall(
        body, name=name, grid=(s // tb,),
        in_specs=[pl.BlockSpec((tb, LANE), lambda i: (i, 0)), pl.BlockSpec((1, LANE), lambda i: (0, 0))],
        out_specs=pl.BlockSpec((tb, LANE), lambda i: (i, 0)),
        out_shape=jax.ShapeDtypeStruct((s, LANE), F32),
        scratch_shapes=[pltpu.VMEM((1, LANE), F32)],
        compiler_params=_params(1),
    )(fl, bf_p)


def _fox_cum_bwd(dcum, fl, bf_p, name, tb=256):
    s = fl.shape[0]
    tb = _tile(s, tb)
    nb = s // tb

    def body(dc_ref, fl_ref, bf_ref, dfl_ref, dbf_ref, carry):
        @pl.when(pl.program_id(0) == 0)
        def _():
            carry[...] = jnp.zeros_like(carry)
            dbf_ref[...] = jnp.zeros_like(dbf_ref)

        dc = dc_ref[...]
        dlf = _tri_matmul(_tri(tb, upper=True), dc) + carry[...]
        carry[...] += _colsum(dc)
        dfl = dlf * _sigmoid(-(fl_ref[...] + bf_ref[...]))
        dfl_ref[...] = dfl
        dbf_ref[...] += _colsum(dfl)

    rev = lambda i: (nb - 1 - i, 0)
    return _pcall(
        body, name=name, grid=(nb,),
        in_specs=[pl.BlockSpec((tb, LANE), rev), pl.BlockSpec((tb, LANE), rev), pl.BlockSpec((1, LANE), lambda i: (0, 0))],
        out_specs=[pl.BlockSpec((tb, LANE), rev), pl.BlockSpec((1, LANE), lambda i: (0, 0))],
        out_shape=[jax.ShapeDtypeStruct((s, LANE), F32), jax.ShapeDtypeStruct((1, LANE), F32)],
        scratch_shapes=[pltpu.VMEM((1, LANE), F32)],
        compiler_params=_params(1),
    )(dcum, fl, bf_p)


def _fox_attn_fwd(qn, kn, vb, cum_col, cum_row, hd, t, name):
    s, d = qn.shape
    heads = d // hd
    nq = s // t

    def body(q_ref, k_ref, v_ref, cc_ref, cr_ref, o_ref, lse_ref):
        qi = pl.program_id(1)
        q = q_ref[...]
        cq = cc_ref[...]
        qpos = qi * t + lax.broadcasted_iota(jnp.int32, (t, 1), 0)

        def step(kj, carry, diagonal=False):
            m, l, acc = carry
            off = pl.multiple_of(kj * t, t)
            ks, vs = k_ref[pl.ds(off, t), :], v_ref[pl.ds(off, t), :]
            sc = _dot(q, ks, tb=True) + cq - cr_ref[kj]
            if diagonal:
                kpos = off + lax.broadcasted_iota(jnp.int32, (1, t), 1)
                sc = jnp.where(kpos <= qpos, sc, NEG)
            m_new = jnp.maximum(m, jnp.max(sc, axis=1, keepdims=True))
            alpha = jnp.exp(m - m_new)
            p = jnp.exp(sc - m_new)
            return m_new, alpha * l + jnp.sum(p, axis=1, keepdims=True), alpha * acc + _dot(p, vs)

        init = (jnp.full((t, 1), NEG, F32), jnp.zeros((t, 1), F32), jnp.zeros((t, hd), F32))
        m, l, acc = step(qi, lax.fori_loop(0, qi, step, init), diagonal=True)
        o_ref[...] = acc / l
        lse_ref[...] = m + jnp.log(l)

    return _pcall(
        body, name=name, grid=(heads, nq),
        in_specs=[pl.BlockSpec((t, hd), lambda h, i: (i, h)),
                  pl.BlockSpec((s, hd), lambda h, i: (0, h)),
                  pl.BlockSpec((s, hd), lambda h, i: (0, h)),
                  pl.BlockSpec((None, t, 1), lambda h, i: (h, i, 0)),
                  pl.BlockSpec((None, nq, 1, t), lambda h, i: (h, 0, 0, 0))],
        out_specs=[pl.BlockSpec((t, hd), lambda h, i: (i, h)), pl.BlockSpec((None, t, 1), lambda h, i: (h, i, 0))],
        out_shape=[jax.ShapeDtypeStruct((s, d), F32), jax.ShapeDtypeStruct((heads, s, 1), F32)],
        compiler_params=_params(2),
    )(qn, kn, vb, cum_col, cum_row)


def _fox_attn_bwd(qn, kn, vb, d_o, o, lse, cum_col, cum_row, hd, t, dproj, name):
    s, d = qn.shape
    heads = d // hd
    nq = s // t

    def body(q_ref, k_ref, v_ref, do_ref, o_ref, lse_ref, cc_ref, cr_ref, dproj_in,
             dq_ref, dk_ref, dv_ref, dcq_ref, dck_ref, delta):
        kj = pl.program_id(1)

        @pl.when(kj == 0)
        def _():
            dq_ref[...] = jnp.zeros_like(dq_ref)
            dcq_ref[...] = jnp.zeros_like(dcq_ref)
            delta[...] = jnp.sum(do_ref[...] * o_ref[...], axis=1, keepdims=True)

        ks, vs, cr = k_ref[...], v_ref[...], cr_ref[...]
        kpos = kj * t + lax.broadcasted_iota(jnp.int32, (1, t), 1)

        def step(qi, carry, diagonal=False):
            dk, dv, dck = carry
            rows = pl.ds(pl.multiple_of(qi * t, t), t)
            q, d_out = q_ref[rows, :], do_ref[rows, :]
            sc = _dot(q, ks, tb=True) + cc_ref[rows, :] - cr
            p = jnp.exp(sc - lse_ref[rows, :])
            if diagonal:
                qpos = qi * t + lax.broadcasted_iota(jnp.int32, (t, 1), 0)
                p = jnp.where(kpos <= qpos, p, 0.0)
            ds = p * (_dot(d_out, vs, tb=True) - delta[rows, :])
            dq_ref[rows, :] += _dot(ds, ks)
            dcq_ref[rows, :] += jnp.sum(ds, axis=1, keepdims=True)
            return dk + _dot(ds, q, ta=True), dv + _dot(p, d_out, ta=True), dck + _colsum(ds)

        init = (jnp.zeros((t, hd), F32), jnp.zeros((t, hd), F32), jnp.zeros((1, t), F32))
        dk, dv, dck = lax.fori_loop(kj + 1, nq, step, step(kj, init, diagonal=True))
        dk_ref[...] = dk.astype(dk_ref.dtype)
        dv_ref[...] = dv.astype(dv_ref.dtype)
        dck_ref[...] = dck

    head_rows = lambda h, j: (0, h)
    blk = lambda h, j: (j, h)
    return _pcall(
        body, name=name, grid=(heads, nq),
        in_specs=[pl.BlockSpec((s, hd), head_rows), pl.BlockSpec((t, hd), blk), pl.BlockSpec((t, hd), blk),
                  pl.BlockSpec((s, hd), head_rows), pl.BlockSpec((s, hd), head_rows),
                  pl.BlockSpec((None, s, 1), lambda h, j: (h, 0, 0)),
                  pl.BlockSpec((None, s, 1), lambda h, j: (h, 0, 0)),
                  pl.BlockSpec((None, None, 1, t), lambda h, j: (h, j, 0, 0)),
                  pl.BlockSpec(memory_space=pl.ANY)],
        out_specs=[pl.BlockSpec((s, hd), head_rows), pl.BlockSpec((t, hd), blk),
                   pl.BlockSpec((t, hd), lambda h, j: (j, 2 * heads + h)),
                   pl.BlockSpec((None, s, 1), lambda h, j: (h, 0, 0)),
                   pl.BlockSpec((None, None, 1, t), lambda h, j: (h, j, 0, 0))],
        out_shape=[jax.ShapeDtypeStruct((s, d), F32), jax.ShapeDtypeStruct((s, d), BF16),
                   jax.ShapeDtypeStruct(dproj.shape, dproj.dtype), jax.ShapeDtypeStruct((heads, s, 1), F32),
                   jax.ShapeDtypeStruct((heads, nq, 1, t), F32)],
        scratch_shapes=[pltpu.VMEM((s, 1), F32)],
        input_output_aliases={8: 2},
        compiler_params=_params(2),
    )(qn, kn, vb, d_o, o, lse, cum_col, cum_row, dproj)


def _fox_gate_fwd(o, og, name):
    def fn(o, og):
        return (o * _sigmoid(og),)

    return _rowwise(fn, [("row", o), og], [("row", o.shape[1], BF16)], name=name)[0]


def _fox_gate_bwd(o, og, dact, name):
    def fn(o, og, dact):
        sg = _sigmoid(og)
        return dact * sg, dact * o * sg * (1.0 - sg)

    d = o.shape[1]
    return _rowwise(fn, [("row", o), og, ("row", dact)], [("row", d, F32), ("band", d, BF16, 3, 4 * d)], name=name)


def _shift_down(x, n):
    rows = lax.broadcasted_iota(jnp.int32, x.shape, 0)
    return jnp.where(rows >= n, pltpu.roll(x, n, 0), 0.0)


def _shift_up(x, n):
    rows = lax.broadcasted_iota(jnp.int32, x.shape, 0)
    return jnp.where(rows < x.shape[0] - n, pltpu.roll(x, x.shape[0] - n, 0), 0.0)


def _conv(u, w_ref, b):
    return w_ref[0:1, :] * _shift_down(u, 2) + w_ref[1:2, :] * _shift_down(u, 1) + w_ref[2:3, :] * u + b


def _conv_act_fwd(u, cw, cb, name, tc=256):
    s, two_f = u.shape
    dff = two_f // 2
    tc = _tile(dff, tc)
    nb = dff // tc

    def body(ug_ref, uv_ref, wg_ref, wv_ref, bg_ref, bv_ref, a_ref):
        gate = _conv(ug_ref[...], wg_ref, bg_ref[...])
        val = _conv(uv_ref[...], wv_ref, bv_ref[...])
        a_ref[...] = (_silu(gate) * val).astype(a_ref.dtype)

    lo, hi = (lambda j: (0, j)), (lambda j: (0, j + nb))
    return _pcall(
        body, name=name, grid=(nb,),
        in_specs=[pl.BlockSpec((s, tc), lo), pl.BlockSpec((s, tc), hi), pl.BlockSpec((3, tc), lo),
                  pl.BlockSpec((3, tc), hi), pl.BlockSpec((1, tc), lo), pl.BlockSpec((1, tc), hi)],
        out_specs=pl.BlockSpec((s, tc), lo),
        out_shape=jax.ShapeDtypeStruct((s, dff), BF16),
        compiler_params=_params(1),
    )(u, u, cw, cw, cb, cb)


def _conv_act_bwd(u, cw, cb, da, name, tc=128):
    s, two_f = u.shape
    dff = two_f // 2
    tc = _tile(dff, tc)
    nb = dff // tc

    def body(ug_ref, uv_ref, wg_ref, wv_ref, bg_ref, bv_ref, da_ref, du_ref, dw_ref, db_ref):
        ug, uv, da = ug_ref[...], uv_ref[...], da_ref[...]
        gate = _conv(ug, wg_ref, bg_ref[...])
        val = _conv(uv, wv_ref, bv_ref[...])
        sg = _sigmoid(gate)
        d_val = da * (gate * sg)
        d_gate = da * val * (sg * (1.0 + gate * (1.0 - sg)))
        for half, (dc, uu, w_ref) in enumerate(((d_gate, ug, wg_ref), (d_val, uv, wv_ref))):
            du = w_ref[0:1, :] * _shift_up(dc, 2) + w_ref[1:2, :] * _shift_up(dc, 1) + w_ref[2:3, :] * dc
            du_ref[half] = du.astype(du_ref.dtype)
            dw_ref[half, 0:1, :] = _colsum(dc * _shift_down(uu, 2))
            dw_ref[half, 1:2, :] = _colsum(dc * _shift_down(uu, 1))
            dw_ref[half, 2:3, :] = _colsum(dc * uu)
            db_ref[half] = _colsum(dc)

    lo, hi = (lambda j: (0, j)), (lambda j: (0, j + nb))
    both = lambda j: (0, 0, j)
    return _pcall(
        body, name=name, grid=(nb,),
        in_specs=[pl.BlockSpec((s, tc), lo), pl.BlockSpec((s, tc), hi), pl.BlockSpec((3, tc), lo),
                  pl.BlockSpec((3, tc), hi), pl.BlockSpec((1, tc), lo), pl.BlockSpec((1, tc), hi),
                  pl.BlockSpec((s, tc), lo)],
        out_specs=[pl.BlockSpec((2, s, tc), both), pl.BlockSpec((2, 3, tc), both), pl.BlockSpec((2, 1, tc), both)],
        out_shape=[jax.ShapeDtypeStruct((2, s, dff), BF16), jax.ShapeDtypeStruct((2, 3, dff), F32),
                   jax.ShapeDtypeStruct((2, 1, dff), F32)],
        compiler_params=_params(1),
    )(u, u, cw, cw, cb, cb, da)


def _adamw_math(w, g, m, v):
    m = ADAM_B1 * m + (1.0 - ADAM_B1) * g
    v = ADAM_B2 * v + (1.0 - ADAM_B2) * (g * g)
    m_hat = m / (1.0 - ADAM_B1 ** ADAM_STEP)
    v_hat = v / (1.0 - ADAM_B2 ** ADAM_STEP)
    delta = -ADAM_LR * (m_hat / (jnp.sqrt(v_hat) + ADAM_EPS) + ADAM_WD * w)
    return delta, m, v


def _update_tiles(r, c, tr):
    tc = c
    if r % 8:
        tr, tc = r, _tile(c, max(LANE, 512 * 1024 // r // LANE * LANE))
    elif r <= tr:
        tr = r
    while r % tr:
        tr -= 8
    return tr, tc


def _adamw(w, g, m, v, name, tr=128):
    layers, r, c = w.shape
    tr, tc = _update_tiles(r, c, tr)

    def body(w_ref, g_ref, m_ref, v_ref, go_ref, d_ref, mo_ref, vo_ref):
        grad = g_ref[...]
        delta, m_new, v_new = _adamw_math(w_ref[...], grad, m_ref[...], v_ref[...])
        go_ref[...], d_ref[...], mo_ref[...], vo_ref[...] = grad, delta, m_new, v_new

    spec = pl.BlockSpec((None, tr, tc), lambda l, i, j: (l, i, j))
    return _pcall(
        body, name=name, grid=(layers, r // tr, c // tc), in_specs=[spec] * 4, out_specs=[spec] * 4,
        out_shape=[jax.ShapeDtypeStruct((layers, r, c), F32)] * 4, compiler_params=_params(3),
    )(w, g, m, v)


def _adamw_pieces(w, lands, sums, chip, m, v, name, tr=128):
    layers, r, c = w.shape
    tr, tc = _update_tiles(r, c, tr)
    nr, nc = r // tr, c // tc

    def body(chip_ref, w_ref, *rest):
        land_refs, own_refs = rest[:layers], rest[layers:2 * layers]
        m_ref, v_ref, go_ref, d_ref, mo_ref, vo_ref = rest[2 * layers:]
        for layer in range(layers):
            @pl.when(pl.program_id(0) == layer)
            def _(land_ref=land_refs[layer], own_ref=own_refs[layer]):
                grad = jnp.zeros(w_ref.shape, F32)
                for q in range(4):
                    grad = grad + jnp.where(chip_ref[0] == q, own_ref[...], land_ref[q]).astype(F32)
                delta, m_new, v_new = _adamw_math(w_ref[...], grad, m_ref[...], v_ref[...])
                go_ref[...], d_ref[...], mo_ref[...], vo_ref[...] = grad, delta, m_new, v_new

    def walk(k, l, i, j):
        here = l == k
        return jnp.where(here, i, jnp.where(l < k, 0, nr - 1)), jnp.where(here, j, jnp.where(l < k, 0, nc - 1))

    spec = pl.BlockSpec((None, tr, tc), lambda l, i, j, chip_ref: (l, i, j))
    land_specs = [pl.BlockSpec((4, tr, tc), lambda l, i, j, chip_ref, k=k: (0,) + walk(k, l, i, j))
                  for k in range(layers)]
    own_specs = [pl.BlockSpec((None, tr, tc), lambda l, i, j, chip_ref, k=k: (chip_ref[0],) + walk(k, l, i, j))
                 for k in range(layers)]
    return _pcall(
        body, name=name,
        grid_spec=pltpu.PrefetchScalarGridSpec(
            num_scalar_prefetch=1, grid=(layers, nr, nc),
            in_specs=[spec] + land_specs + own_specs + [spec, spec], out_specs=[spec] * 4),
        out_shape=[jax.ShapeDtypeStruct((layers, r, c), F32)] * 4, compiler_params=_params(3),
    )(chip, w, *lands, *sums, m, v)


def _pair_sum(pieces, partner, core, name, tr=512):
    _, r, c = pieces.shape
    tc = c
    if r % 8:
        tr, tc = r, _tile(c, max(LANE, 1024 * 1024 // r // LANE * LANE))
    elif r <= tr:
        tr = r
    while r % tr:
        tr -= 8

    def body(core_ref, mine_ref, partner_ref, out_ref):
        out_ref[...] = (mine_ref[...].astype(F32) + partner_ref[...].astype(F32)).astype(out_ref.dtype)

    return _pcall(
        body, name=name,
        grid_spec=pltpu.PrefetchScalarGridSpec(
            num_scalar_prefetch=1, grid=(4, r // tr, c // tc),
            in_specs=[pl.BlockSpec((None, tr, tc), lambda q, i, j, core_ref: (2 * q + core_ref[0], i, j)),
                      pl.BlockSpec((None, tr, tc), lambda q, i, j, core_ref: (q, i, j))],
            out_specs=pl.BlockSpec((None, tr, tc), lambda q, i, j, core_ref: (q, i, j))),
        out_shape=jax.ShapeDtypeStruct((4, r, c), pieces.dtype), compiler_params=_params(3),
    )(core, pieces, partner)


def _sum8(x, name):
    p = x.shape[2]
    tp = _tile(p, 16 * 1024)

    def body(x_ref, o_ref):
        acc = x_ref[0]
        for i in range(1, N_DEV):
            acc = acc + x_ref[i]
        o_ref[...] = acc

    return _pcall(
        body, name=name, grid=(p // tp,), in_specs=[pl.BlockSpec((N_DEV, 1, tp), lambda i: (0, 0, i))],
        out_specs=pl.BlockSpec((1, tp), lambda i: (0, i)), out_shape=jax.ShapeDtypeStruct((1, p), x.dtype),
        compiler_params=_params(1),
    )(x)


def _exchange(arrays, name, scatter):
    n = len(arrays)
    hbm = pl.BlockSpec(memory_space=pl.ANY)

    def body(*refs):
        ins, outs, token = refs[:n], refs[n:2 * n], refs[2 * n]
        send_sems, recv_sems, local_sems = refs[2 * n + 1:]
        token[...] = jnp.zeros_like(token)
        x, y, c = lax.axis_index("x"), lax.axis_index("y"), lax.axis_index("c")
        me = 4 * x + 2 * y + c
        copies = []
        for a in range(n):
            src_mine = ins[a].at[me] if scatter else ins[a]
            local = pltpu.make_async_copy(src_mine, outs[a].at[me], local_sems.at[a])
            local.start()
            copies.append(local)
            for k in range(1, N_DEV):
                px = 1 - x if k & 4 else x
                py = 1 - y if k & 2 else y
                pc = 1 - c if k & 1 else c
                src = ins[a].at[4 * px + 2 * py + pc] if scatter else ins[a]
                cp = pltpu.make_async_remote_copy(
                    src_ref=src, dst_ref=outs[a].at[me],
                    send_sem=send_sems.at[a * (N_DEV - 1) + k - 1], recv_sem=recv_sems.at[a * (N_DEV - 1) + k - 1],
                    device_id=(px, py, pc), device_id_type=pl.DeviceIdType.MESH)
                cp.start()
                copies.append(cp)
        for cp in copies:
            cp.wait()

    out_shape = [jax.ShapeDtypeStruct(a.shape if scatter else (N_DEV,) + a.shape, a.dtype) for a in arrays]
    res = _pcall(
        body, name=name, in_specs=[hbm] * n, out_specs=[hbm] * n + [pl.BlockSpec(memory_space=pltpu.VMEM)],
        out_shape=out_shape + [jax.ShapeDtypeStruct((8, LANE), F32)],
        scratch_shapes=[pltpu.SemaphoreType.DMA((n * (N_DEV - 1),)), pltpu.SemaphoreType.DMA((n * (N_DEV - 1),)),
                        pltpu.SemaphoreType.DMA((n,))],
        compiler_params=pltpu.CompilerParams(has_side_effects=True),
    )(*arrays)
    return res[:n], res[n][0, 0]


_HBM = pl.BlockSpec(memory_space=pltpu.HBM)
_SEM = pl.BlockSpec(memory_space=pltpu.SEMAPHORE)
_DATAFLOW = pltpu.SideEffectType.DATAFLOW_SIDE_EFFECTING


def _peer(k, x, y, c):
    return (1 - x if k & 4 else x, 1 - y if k & 2 else y, 1 - c if k & 1 else c)


def _pair_plan(x, y, c):
    return [(2 * q + (1 - c), q, (x, y, 1 - c)) for q in range(4)]


def _chip_plan(x, y, c):
    out = []
    for k in _ICI_PEERS:
        px, py, pc = _peer(k, x, y, c)
        out.append((2 * px + py, 2 * x + y, (px, py, pc)))
    return out


def _all_plan(x, y, c):
    return [(0, 4 * x + 2 * y + c, _peer(k, x, y, c)) for k in range(1, N_DEV)]


def _split_start(arrays, plan, name, land_blocks=4):
    n = len(arrays)
    lands = [lax.empty((land_blocks,) + a.shape[1:], a.dtype) for a in arrays]
    n_copies = len(plan(0, 0, 0))

    def body(*refs):
        srcs, dsts = refs[:n], refs[n:2 * n]
        send_sems, recv_sems, token = refs[4 * n:5 * n], refs[5 * n:6 * n], refs[6 * n]
        copies = plan(lax.axis_index("x"), lax.axis_index("y"), lax.axis_index("c"))
        for a in range(n):
            for j, (src_block, dst_block, peer) in enumerate(copies):
                pltpu.make_async_remote_copy(
                    src_ref=srcs[a].at[src_block], dst_ref=dsts[a].at[dst_block],
                    send_sem=send_sems[a].at[j], recv_sem=recv_sems[a].at[j],
                    device_id=peer, device_id_type=pl.DeviceIdType.MESH).start()
        token[...] = jnp.zeros_like(token)

    sems = [pltpu.SemaphoreType.DMA((n_copies,))] * (2 * n)
    res = _pcall(
        body, name=name,
        in_specs=[_HBM] * (2 * n),
        out_specs=[_HBM] * (2 * n) + [_SEM] * (2 * n) + [pl.BlockSpec(memory_space=pltpu.VMEM)],
        out_shape=[pltpu.HBM(a.shape, a.dtype) for a in arrays] + [pltpu.HBM(l.shape, l.dtype) for l in lands]
        + sems + [jax.ShapeDtypeStruct((8, LANE), F32)],
        input_output_aliases={i: i for i in range(2 * n)},
        compiler_params=pltpu.CompilerParams(has_side_effects=_DATAFLOW),
    )(*[pltpu.with_memory_space_constraint(a, pltpu.HBM) for a in arrays],
      *[pltpu.with_memory_space_constraint(l, pltpu.HBM) for l in lands])
    handles = [(res[a], res[n + a], res[2 * n + a], res[3 * n + a]) for a in range(n)]
    return handles, res[4 * n][0, 0]


def _split_wait(handles, plan, after, name):
    n = len(handles)
    after = list(after) if isinstance(after, (list, tuple)) else [after]

    def body(*refs):
        srcs, dsts = refs[:n], refs[n:2 * n]
        send_sems, recv_sems = refs[2 * n:3 * n], refs[3 * n:4 * n]
        copies = plan(lax.axis_index("x"), lax.axis_index("y"), lax.axis_index("c"))
        for a in range(n):
            for j, (src_block, dst_block, peer) in enumerate(copies):
                cp = pltpu.make_async_remote_copy(
                    src_ref=srcs[a].at[src_block], dst_ref=dsts[a].at[dst_block],
                    send_sem=send_sems[a].at[j], recv_sem=recv_sems[a].at[j],
                    device_id=peer, device_id_type=pl.DeviceIdType.MESH)
                cp.wait_send()
                cp.wait_recv()

    srcs, lands = [h[0] for h in handles], [h[1] for h in handles]
    res = _pcall(
        body, name=name,
        in_specs=[_HBM] * (2 * n) + [_SEM] * (2 * n) + [pl.BlockSpec(memory_space=pl.ANY)] * len(after),
        out_specs=[_HBM] * (2 * n),
        out_shape=[pltpu.HBM(t.shape, t.dtype) for t in srcs + lands],
        input_output_aliases={i: i for i in range(2 * n)},
        compiler_params=pltpu.CompilerParams(has_side_effects=_DATAFLOW),
    )(*srcs, *lands, *[h[2] for h in handles], *[h[3] for h in handles], *after)
    return res[:n], res[n:]


_ICI_PEERS = (2, 4, 6)


def _gather2_start(shards, name):
    n = len(shards)
    lands = [lax.empty((N_DEV,) + a.shape, a.dtype) for a in shards]

    def body(*refs):
        srcs, dsts = refs[:n], refs[n:2 * n]
        send_sems, d2d_sems, ici_sems = refs[4 * n:5 * n], refs[5 * n:6 * n], refs[6 * n:7 * n]
        token = refs[7 * n]
        x, y, c = lax.axis_index("x"), lax.axis_index("y"), lax.axis_index("c")
        me = 4 * x + 2 * y + c
        for a in range(n):
            for j, k in enumerate((1,) + _ICI_PEERS):
                recv = d2d_sems[a].at[0] if j == 0 else ici_sems[a].at[j - 1]
                pltpu.make_async_remote_copy(
                    src_ref=srcs[a], dst_ref=dsts[a].at[me], send_sem=send_sems[a].at[j], recv_sem=recv,
                    device_id=_peer(k, x, y, c), device_id_type=pl.DeviceIdType.MESH).start()
        token[...] = jnp.zeros_like(token)

    dma = pltpu.SemaphoreType.DMA
    res = _pcall(
        body, name=name,
        in_specs=[_HBM] * (2 * n),
        out_specs=[_HBM] * (2 * n) + [_SEM] * (3 * n) + [pl.BlockSpec(memory_space=pltpu.VMEM)],
        out_shape=[pltpu.HBM(a.shape, a.dtype) for a in shards] + [pltpu.HBM(l.shape, l.dtype) for l in lands]
        + [dma((4,))] * n + [dma((1,))] * n + [dma((3,))] * n + [jax.ShapeDtypeStruct((8, LANE), F32)],
        input_output_aliases={i: i for i in range(2 * n)},
        compiler_params=pltpu.CompilerParams(has_side_effects=_DATAFLOW),
    )(*[pltpu.with_memory_space_constraint(a, pltpu.HBM) for a in shards],
      *[pltpu.with_memory_space_constraint(l, pltpu.HBM) for l in lands])
    handles = [tuple(res[i * n + a] for i in range(5)) for a in range(n)]
    return handles, res[5 * n][0, 0]


def _gather2_forward(handle, after, name):
    src, land, send_sems, d2d_sem, ici_sems = handle

    def body(land_ref, ici_ref, d2d_ref, after_ref, land_out, fwd_send, fwd_recv, token):
        x, y, c = lax.axis_index("x"), lax.axis_index("y"), lax.axis_index("c")
        sibling = (x, y, 1 - c)
        arrived = [(_peer(k, x, y, c), ici_ref.at[j]) for j, k in enumerate(_ICI_PEERS)] + [(sibling, d2d_ref.at[0])]
        for j, ((px, py, pc), recv) in enumerate(arrived):
            block = land_ref.at[4 * px + 2 * py + pc]
            pltpu.make_async_remote_copy(
                src_ref=block, dst_ref=block, send_sem=fwd_send.at[j], recv_sem=recv,
                device_id=(px, py, pc), device_id_type=pl.DeviceIdType.MESH).wait_recv()
            pltpu.make_async_remote_copy(
                src_ref=block, dst_ref=block, send_sem=fwd_send.at[j], recv_sem=fwd_recv.at[j],
                device_id=sibling, device_id_type=pl.DeviceIdType.MESH).start()
        token[...] = jnp.zeros_like(token)

    dma = pltpu.SemaphoreType.DMA
    land, fwd_send, fwd_recv, token = _pcall(
        body, name=name,
        in_specs=[_HBM, _SEM, _SEM, pl.BlockSpec(memory_space=pl.ANY)],
        out_specs=[_HBM, _SEM, _SEM, pl.BlockSpec(memory_space=pltpu.VMEM)],
        out_shape=[pltpu.HBM(land.shape, land.dtype), dma((4,)), dma((4,)), jax.ShapeDtypeStruct((8, LANE), F32)],
        input_output_aliases={0: 0},
        compiler_params=pltpu.CompilerParams(has_side_effects=_DATAFLOW),
    )(land, ici_sems, d2d_sem, after)
    return (src, land, send_sems, fwd_send, fwd_recv), token[0, 0]


def _gather2_wait(handle, after, name):
    src, land, send_sems, fwd_send, fwd_recv = handle

    def body(src_ref, land_ref, send_ref, fsend_ref, frecv_ref, after_ref, src_out, land_out):
        x, y, c = lax.axis_index("x"), lax.axis_index("y"), lax.axis_index("c")
        block = land_ref.at[4 * x + 2 * y + c]

        def copy(send, recv):
            return pltpu.make_async_remote_copy(src_ref=src_ref, dst_ref=block, send_sem=send, recv_sem=recv,
                                                device_id=(x, y, 1 - c), device_id_type=pl.DeviceIdType.MESH)

        for j in range(4):
            copy(send_ref.at[j], frecv_ref.at[j]).wait_send()
        for j in range(4):
            copy(fsend_ref.at[j], frecv_ref.at[j]).wait_send()
            copy(fsend_ref.at[j], frecv_ref.at[j]).wait_recv()

    res = _pcall(
        body, name=name,
        in_specs=[_HBM, _HBM, _SEM, _SEM, _SEM, pl.BlockSpec(memory_space=pl.ANY)],
        out_specs=[_HBM, _HBM],
        out_shape=[pltpu.HBM(src.shape, src.dtype), pltpu.HBM(land.shape, land.dtype)],
        input_output_aliases={0: 0, 1: 1},
        compiler_params=pltpu.CompilerParams(has_side_effects=_DATAFLOW),
    )(src, land, send_sems, fwd_send, fwd_recv, after)
    return res[0], res[1]


def _pad_cols(x, width=LANE):
    return jnp.pad(x, ((0, 0), (0, width - x.shape[1])))


def _cols_full(g):
    return jnp.transpose(g, (1, 0, 2)).reshape(g.shape[1], -1)


def _ffn_fwd(x1, p, i, tag):
    h2 = _adaln_fwd(x1, p["norm_ffn"][i], p["sc_f"][i], p["sh_f"][i], f"ffn_norm_{tag}")
    u = _matmul(h2, p["fetch"](f"up{i}", h2), name=f"ffn_up_{tag}", tn=1408, b_shards=True)
    a = _conv_act_fwd(u, p["conv_w"][i], p["conv_b"][i], f"ffn_act_{tag}")
    g_f = p["g_f"][i]
    x2, f = _matmul(a, p["fetch"](f"down{i}", a), name=f"ffn_down_{tag}", tk=1408, out_dtypes=(F32, F32),
                    epilogue=lambda acc, x1, g: (x1 + (1.0 + g) * acc, acc), extras=(("mn", x1), ("n", g_f)))
    return x2, dict(h2=h2, u=u, a=a, f=f)


def _ffn_bwd(incoming, x1, saved, p, i, tag, branch):
    d = x1.shape[1]
    dx2, df, dg_f = incoming
    w_up, w_down = p["fetch"](f"up{i}", None), p["fetch"](f"down{i}", None)
    da = _matmul(df, w_down, tb=True, name=f"ffn_down_dx_{tag}", tn=1408)
    dw_down = _matmul(saved["a"], df, ta=True, name=f"ffn_down_dw_{tag}", tm=1408, out_dtypes=(BF16,))
    du, dcw, dcb = _conv_act_bwd(saved["u"], p["conv_w"][i], p["conv_b"][i], da, f"ffn_act_bwd_{tag}")
    dcw, dcb = (jnp.concatenate([t[0], t[1]], axis=1) for t in (dcw, dcb))
    tok = p["flush"](du)
    dh2 = _matmul(du, w_up, tb=True, name=f"ffn_up_dx_{tag}", tn=2048, tk=1408, a_halves=True, b_shards=True)
    dw_up = _matmul(saved["h2"], du, ta=True, name=f"ffn_up_dw_{tag}", tn=1408, out_dtypes=(BF16,), b_halves=True,
                    out_shards=True)
    tok = tok + p["send"](f"ffn{i}", [dw_up, dw_down.reshape(N_DEV, -1, d)])
    dx1, dsh, dsc, dgain, dy, dg_m = _adaln_bwd(x1, dh2, dx2, p["norm_ffn"][i] + tok, p["sc_f"][i],
                                                f"ffn_norm_bwd_{tag}", branch)
    grads = dict(conv_w=dcw, conv_b=dcb, norm_ffn=dgain, sh_f=dsh, sc_f=dsc, g_f=dg_f)
    return (dx1, dy, dg_m), grads


def _gla_layer_fwd(x, p, i):
    h1 = _adaln_fwd(x, p["norm_mix"][i], p["sc_m"][i], p["sh_m"][i], "gla_norm")
    w_t, w_tail_t, main = p["fetch"]("gla_in", h1)
    proj = _matmul(h1, w_t, tb=True, b_rows=main, name="gla_in")
    a_tail = _matmul(h1, w_tail_t, tb=True, name="gla_in_tail")
    dk_total = p["gla_wg_p"].shape[1]
    o, states = _gla_fwd(proj, a_tail, p["gla_wg_p"], p["gla_b_gate"], "gla_chunks")
    assert 2 * dk_total == o.shape[1]
    r = ("cols", proj, 2, o.shape[1])
    og = _gla_post_fwd(o, r, p["gla_norm"], "gla_post")
    x1, y = _matmul(og, p["fetch"]("gla_out", og), name="gla_out", out_dtypes=(F32, F32),
                    epilogue=lambda acc, x, g: (x + (1.0 + g) * acc, acc), extras=(("mn", x), ("n", p["g_m"][i])))
    return x1, dict(h1=h1, proj=proj, a_tail=a_tail, o=o, r=r, states=states, og=og, y=y)


def _gla_layer_bwd(incoming, x, sv, p, i, branch):
    d = x.shape[1]
    dx1, dy, dg_m = incoming
    (w_t, w_tail_t, main), w_out = p["fetch"]("gla_in", None), p["fetch"]("gla_out", None)
    dog = _matmul(dy, w_out, tb=True, name="gla_out_dx")
    dw_out = _matmul(sv["og"], dy, ta=True, name="gla_out_dw", out_dtypes=(BF16,))
    tok = p["flush"](dog) + p["send"]("gla_out", [dw_out.reshape(N_DEV, -1, d)])
    d_o, dproj, dgn = _gla_post_bwd(sv["o"], sv["r"], p["gla_norm"] + tok, dog, "gla_post_bwd")
    dproj, dga = _gla_bwd(sv["proj"], sv["a_tail"], p["gla_wg_p"], p["gla_b_gate"], sv["states"], d_o, dproj,
                          "gla_chunks_bwd")
    tok = p["flush"](dga)
    da_tail = _matmul(dga, p["gla_wg_p"], tb=True, name="gla_gate_dx", out_dtypes=(BF16,))
    dwg = _matmul(sv["a_tail"], dga, ta=True, name="gla_gate_dw")
    dbg = _rowwise(lambda t: (_colsum(t),), [("row", dga)], [("acc", dga.shape[1], F32)], name="gla_gate_db")[0]
    dh_tail = _matmul(da_tail, w_tail_t, name="gla_in_tail_dx")
    dh1 = _matmul(dproj, w_t, b_rows=main, name="gla_in_dx", tk=2048,
                  epilogue=lambda acc, t: (acc + t,), extras=(("mn", dh_tail),))
    rank = p["gla_rank"]
    dw_main = _matmul(dproj, sv["h1"], ta=True, name="gla_in_dw", out_dtypes=(BF16,), out_rows=main + rank)
    dx, dsh, dsc, dgain, *into_branch = _adaln_bwd(x, dh1, dx1, p["norm_mix"][i] + tok, p["sc_m"][i], "gla_norm_bwd",
                                                   branch)
    grads = dict(gla_w_gate=dwg[:rank], gla_b_gate=dbg, gla_norm=dgn, norm_mix=dgain, sh_m=dsh, sc_m=dsc, g_m=dg_m,
                 gla_w_in_unsent=(dw_main, da_tail, sv["h1"]))
    return (dx, *into_branch), grads


def _fox_layer_fwd(x, p, i):
    d = x.shape[1]
    hd = p["fox_q_norm"].shape[1]
    heads = d // hd
    s = x.shape[0]
    t = _tile(s, 1024)
    h1 = _adaln_fwd(x, p["norm_mix"][i], p["sc_m"][i], p["sh_m"][i], "fox_norm")
    w_t, w_tail_t, main = p["fetch"]("fox_in", h1)
    proj = _matmul(h1, w_t, tb=True, b_rows=main, name="fox_in")
    fl = _matmul(h1, w_tail_t, tb=True, name="fox_in_tail")
    q, k, v, og = (("cols", proj, j, d) for j in range(4))
    qn, kn, vb = _fox_prep(q, k, v, p["fox_q_norm"], p["fox_k_norm"], d, hd, "fox_prep")
    cum = _fox_cum(fl, p["fox_bf_p"], "fox_cum")
    cum_t = jnp.transpose(cum[:, :heads])
    cum_col, cum_row = cum_t[:, :, None], cum_t.reshape(heads, s // t, 1, t)
    o, lse = _fox_attn_fwd(qn, kn, vb, cum_col, cum_row, hd, t, "fox_attn")
    act = _fox_gate_fwd(o, og, "fox_gate")
    x1, y = _matmul(act, p["fetch"]("fox_out", act), name="fox_out", out_dtypes=(F32, F32),
                    epilogue=lambda acc, x, g: (x + (1.0 + g) * acc, acc), extras=(("mn", x), ("n", p["g_m"][i])))
    return x1, dict(h1=h1, q=q, k=k, og=og, fl=fl, qn=qn, kn=kn, vb=vb, cum_col=cum_col, cum_row=cum_row,
                    o=o, lse=lse, act=act, y=y, t=t, hd=hd)


def _fox_layer_bwd(incoming, x, sv, p, i, branch):
    d = x.shape[1]
    hd, t = sv["hd"], sv["t"]
    heads = d // hd
    s = x.shape[0]
    dx1, dy, dg_m = incoming
    (w_t, w_tail_t, main), w_out = p["fetch"]("fox_in", None), p["fetch"]("fox_out", None)
    dact = _matmul(dy, w_out, tb=True, name="fox_out_dx")
    dw_out = _matmul(sv["act"], dy, ta=True, name="fox_out_dw", out_dtypes=(BF16,))
    d_o, dproj = _fox_gate_bwd(sv["o"], sv["og"], dact, "fox_gate_bwd")
    tok_flush = p["flush"](d_o)
    dqn, dkn, dproj, dcq, dck = _fox_attn_bwd(sv["qn"], sv["kn"], sv["vb"], d_o, sv["o"], sv["lse"], sv["cum_col"],
                                              sv["cum_row"], hd, t, dproj, "fox_attn_bwd")
    dproj, gq, gk = _fox_prep_bwd(sv["q"], sv["k"], dqn, dkn, p["fox_q_norm"], p["fox_k_norm"], hd, dproj,
                                  "fox_prep_bwd")
    dcum = _pad_cols(jnp.transpose(dcq[:, :, 0] - dck.reshape(heads, s)))
    dfl, dbf = _fox_cum_bwd(dcum, sv["fl"], p["fox_bf_p"], "fox_cum_bwd")
    dfl_b = dfl.astype(BF16)
    dh_tail = _matmul(dfl_b, w_tail_t, name="fox_in_tail_dx")
    dh1 = _matmul(dproj, w_t, b_rows=main, name="fox_in_dx", tk=2048,
                  epilogue=lambda acc, tl: (acc + tl,), extras=(("mn", dh_tail),))
    dw_main = _matmul(dproj, sv["h1"], ta=True, name="fox_in_dw", out_dtypes=(BF16,), out_rows=main + heads)
    dw_in = _tail_rows(dfl_b, sv["h1"], dw_main, heads, "fox_in_tail_dw").reshape(N_DEV, -1, d)
    tok = tok_flush + p["send"]("fox", [dw_in, dw_out.reshape(N_DEV, -1, d)])
    dx, dsh, dsc, dgain, *into_branch = _adaln_bwd(x, dh1, dx1, p["norm_mix"][i] + tok, p["sc_m"][i], "fox_norm_bwd",
                                                   branch)
    grads = dict(fox_b_f=dbf[:, :heads], fox_q_norm=gq.reshape(heads, hd).sum(0, keepdims=True),
                 fox_k_norm=gk.reshape(heads, hd).sum(0, keepdims=True), norm_mix=dgain, sh_m=dsh, sc_m=dsc, g_m=dg_m)
    return (dx, *into_branch), grads


SMALL = ("b_mod", "norm_mix", "norm_ffn", "gla_b_gate", "gla_norm", "fox_b_f", "fox_q_norm", "fox_k_norm",
         "ffn_conv_b", "norm_final")
SMALL_SHARDED = ("gla_w_gate", "ffn_conv_w")
BIG = ("gla_w_in", "gla_w_out", "fox_w_in", "fox_w_out", "ffn_w_up", "ffn_w_down")
WEIGHTS = ("w_mod", "b_mod", "norm_mix", "norm_ffn", "gla_w_in", "gla_w_gate", "gla_b_gate", "gla_norm", "gla_w_out",
           "fox_w_in", "fox_b_f", "fox_q_norm", "fox_k_norm", "fox_w_out", "ffn_w_up", "ffn_conv_w", "ffn_conv_b",
           "ffn_w_down", "norm_final")


def _pack(parts):
    flat = jnp.concatenate([p.reshape(-1) for p in parts])
    pad = (-flat.shape[0]) % 1024
    return jnp.pad(flat, (0, pad)).reshape(1, -1)


def _unpack(flat, shapes):
    out, off = [], 0
    for shp in shapes:
        n = 1
        for s in shp:
            n *= s
        out.append(flat[0, off:off + n].reshape(shp))
        off += n
    return out


def kernel(x, c, w_mod, b_mod, norm_mix, norm_ffn, gla_w_in, gla_w_gate, gla_b_gate, gla_norm, gla_w_out, fox_w_in, fox_b_f, fox_q_norm, fox_k_norm, fox_w_out, ffn_w_up, ffn_conv_w, ffn_conv_b, ffn_w_down, norm_final, loss_target, m_w_mod, m_b_mod, m_norm_mix, m_norm_ffn, m_gla_w_in, m_gla_w_gate, m_gla_b_gate, m_gla_norm, m_gla_w_out, m_fox_w_in, m_fox_b_f, m_fox_q_norm, m_fox_k_norm, m_fox_w_out, m_ffn_w_up, m_ffn_conv_w, m_ffn_conv_b, m_ffn_w_down, m_norm_final, v_w_mod, v_b_mod, v_norm_mix, v_norm_ffn, v_gla_w_in, v_gla_w_gate, v_gla_b_gate, v_gla_norm, v_gla_w_out, v_fox_w_in, v_fox_b_f, v_fox_q_norm, v_fox_k_norm, v_fox_w_out, v_ffn_w_up, v_ffn_conv_w, v_ffn_conv_b, v_ffn_w_down, v_norm_final):
    w = dict(w_mod=w_mod, b_mod=b_mod, norm_mix=norm_mix, norm_ffn=norm_ffn, gla_w_in=gla_w_in, gla_w_gate=gla_w_gate,
             gla_b_gate=gla_b_gate, gla_norm=gla_norm, gla_w_out=gla_w_out, fox_w_in=fox_w_in, fox_b_f=fox_b_f,
             fox_q_norm=fox_q_norm, fox_k_norm=fox_k_norm, fox_w_out=fox_w_out, ffn_w_up=ffn_w_up,
             ffn_conv_w=ffn_conv_w, ffn_conv_b=ffn_conv_b, ffn_w_down=ffn_w_down, norm_final=norm_final)
    mom_m = dict(w_mod=m_w_mod, b_mod=m_b_mod, norm_mix=m_norm_mix, norm_ffn=m_norm_ffn, gla_w_in=m_gla_w_in,
                 gla_w_gate=m_gla_w_gate, gla_b_gate=m_gla_b_gate, gla_norm=m_gla_norm, gla_w_out=m_gla_w_out,
                 fox_w_in=m_fox_w_in, fox_b_f=m_fox_b_f, fox_q_norm=m_fox_q_norm, fox_k_norm=m_fox_k_norm,
                 fox_w_out=m_fox_w_out, ffn_w_up=m_ffn_w_up, ffn_conv_w=m_ffn_conv_w, ffn_conv_b=m_ffn_conv_b,
                 ffn_w_down=m_ffn_w_down, norm_final=m_norm_final)
    mom_v = dict(w_mod=v_w_mod, b_mod=v_b_mod, norm_mix=v_norm_mix, norm_ffn=v_norm_ffn, gla_w_in=v_gla_w_in,
                 gla_w_gate=v_gla_w_gate, gla_b_gate=v_gla_b_gate, gla_norm=v_gla_norm, gla_w_out=v_gla_w_out,
                 fox_w_in=v_fox_w_in, fox_b_f=v_fox_b_f, fox_q_norm=v_fox_q_norm, fox_k_norm=v_fox_k_norm,
                 fox_w_out=v_fox_w_out, ffn_w_up=v_ffn_w_up, ffn_conv_w=v_ffn_conv_w, ffn_conv_b=v_ffn_conv_b,
                 ffn_w_down=v_ffn_w_down, norm_final=v_norm_final)

    me = 4 * lax.axis_index("x") + 2 * lax.axis_index("y") + lax.axis_index("c")
    xs, target = x[0], loss_target[0]
    s, d = xs.shape
    depth = w_mod.shape[0]
    mod_cols = w_mod.shape[2]
    rank = gla_w_gate.shape[1]
    hd = fox_q_norm.shape[1]
    fox_heads = d // hd
    dk_total = gla_w_gate.shape[2] * N_DEV

    cond = c * (1.0 / (1.0 + jnp.exp(-c)))
    g, _ = _exchange([gla_w_gate[0], ffn_conv_w, cond], "gather_small", scatter=False)
    cond_all = g[2][:, 0, :]

    cond_pad = jnp.pad(cond_all, ((0, 16 - N_DEV), (0, 0)))
    mod_part = []
    for i in range(depth):
        b_cols = lax.dynamic_slice(b_mod[i:i + 1], (0, me * mod_cols), (1, mod_cols))
        mod_part.append(_matmul(cond_pad, w_mod, b_layer=i, name=f"mod_{i}", tn=768,
                                epilogue=lambda acc, b: (acc + b,), extras=(("n", b_cols),))[:N_DEV])
    (mod_all,), tok_mod = _exchange([jnp.stack(mod_part)], "gather_mod", scatter=False)
    mod = lax.dynamic_index_in_dim(mod_all, me, axis=2, keepdims=False)
    mod = jnp.transpose(mod, (1, 0, 2)).reshape(depth, 6, 1, d)

    big_names = ["gla_in", "gla_out", "up0", "down0", "fox_in", "fox_out", "up1", "down1"]
    first = [jnp.transpose(gla_w_in[0] + tok_mod).astype(BF16), gla_w_out[0].astype(BF16)]
    handles, tok_first = _gather2_start(first, "gather_weights_start_first")
    rest = [ffn_w_up[0] + tok_first, ffn_w_down[0], jnp.transpose(fox_w_in[0]), fox_w_out[0], ffn_w_up[1],
            ffn_w_down[1]]
    handles_rest, tok0 = _gather2_start([t.astype(BF16) for t in rest], "gather_weights_start_rest")
    handles = handles + handles_rest
    ready, forwarded = {}, {}

    def split_tail(full_t, tail):
        main = full_t.shape[0] - tail
        return full_t, jnp.pad(full_t[main:], ((0, LANE - tail), (0, 0))), main

    def forward(idx, after):
        key = big_names[idx]
        forwarded[key] = _gather2_forward(handles[idx], after, f"gather_{key}_forward")

    def fetch(key, after):
        if key not in ready:
            idx = big_names.index(key)
            if idx == 0:
                forward(0, after)
            handle, _ = forwarded[key]
            _, full = _gather2_wait(handle, after, f"gather_{key}_wait")
            if idx + 1 < len(big_names):
                forward(idx + 1, full)
            if key == "gla_in":
                ready[key] = split_tail(full.reshape(-1, d), rank)
            elif key == "fox_in":
                ready[key] = split_tail(full.reshape(-1, d), fox_heads)
            elif key.startswith("up"):
                ready[key] = full
            else:
                ready[key] = full.reshape(-1, d)
        return ready[key]

    pending, sent = [], {}
    core = lax.axis_index("c").astype(jnp.int32).reshape(1)
    chip = 2 * lax.axis_index("x") + lax.axis_index("y")

    def send(key, pieces):
        hs, tok = _split_start(pieces, _pair_plan, f"scatter_{key}_pair_start")
        pending.append((key, hs))
        return tok

    def flush(after):
        tok = 0.0
        while pending:
            key, hs = pending.pop(0)
            mine, partner = _split_wait(hs, _pair_plan, after, f"scatter_{key}_pair_wait")
            sums = [_pair_sum(pc, pt, core, f"scatter_{key}_pair_sum{a}")
                    for a, (pc, pt) in enumerate(zip(mine, partner))]
            sent[key], t = _split_start(sums, _chip_plan, f"scatter_{key}_chip_start")
            tok = tok + t
        return tok

    p = dict(
        fetch=fetch, send=send, flush=flush,
        gla_wg_p=jnp.pad(_cols_full(g[0]), ((0, LANE - rank), (0, 0))),
        conv_w=[jnp.transpose(g[1][:, i], (1, 0, 2)).reshape(ffn_conv_w.shape[1], -1) for i in range(depth)],
        conv_b=[ffn_conv_b[i:i + 1] for i in range(depth)],
        gla_b_gate=gla_b_gate, gla_norm=gla_norm, fox_q_norm=fox_q_norm, fox_k_norm=fox_k_norm,
        fox_bf_p=_pad_cols(fox_b_f), gla_rank=rank,
        norm_mix=[norm_mix[i:i + 1] + (tok0 if i == 0 else 0.0) for i in range(depth)],
        norm_ffn=[norm_ffn[i:i + 1] for i in range(depth)],
    )

    for j, nm in enumerate(("sh_m", "sc_m", "g_m", "sh_f", "sc_f", "g_f")):
        p[nm] = [mod[i, j] for i in range(depth)]

    acts, saved = [xs], []
    for i in range(depth):
        layer_fwd = _gla_layer_fwd if i % 2 == 0 else _fox_layer_fwd
        x1, sv_mix = layer_fwd(acts[-1], p, i)
        x2, sv_ffn = _ffn_fwd(x1, p, i, str(i))
        saved.append((acts[-1], x1, sv_mix, sv_ffn))
        acts.append(x2)
    last_ffn = (saved[-1][3]["f"], p["g_f"][depth - 1])
    dx, d_norm_final, loss_part, *into_branch = _final_loss(acts[-1], target, norm_final.reshape(1, d), "final_loss",
                                                            last_ffn)
    incoming = (dx, *into_branch)

    lg = [None] * depth
    for i in reversed(range(depth)):
        x_in, x1, sv_mix, sv_ffn = saved[i]
        incoming, g_ffn = _ffn_bwd(incoming, x1, sv_ffn, p, i, str(i), (sv_mix["y"], p["g_m"][i]))
        layer_bwd = _gla_layer_bwd if i % 2 == 0 else _fox_layer_bwd
        before = (saved[i - 1][3]["f"], p["g_f"][i - 1]) if i else None
        incoming, g_mix = layer_bwd(incoming, x_in, sv_mix, p, i, before)
        lg[i] = {**g_ffn, **g_mix}
    grad_x = incoming[0][None]

    gla_l = [i for i in range(depth) if i % 2 == 0]
    fox_l = [i for i in range(depth) if i % 2 == 1]
    small_parts = dict(
        norm_mix=jnp.concatenate([lg[i]["norm_mix"] for i in range(depth)]),
        norm_ffn=jnp.concatenate([lg[i]["norm_ffn"] for i in range(depth)]),
        gla_b_gate=jnp.concatenate([lg[i]["gla_b_gate"] for i in gla_l]),
        gla_norm=jnp.concatenate([lg[i]["gla_norm"] for i in gla_l]),
        fox_b_f=jnp.concatenate([lg[i]["fox_b_f"] for i in fox_l]),
        fox_q_norm=jnp.concatenate([lg[i]["fox_q_norm"] for i in fox_l]),
        fox_k_norm=jnp.concatenate([lg[i]["fox_k_norm"] for i in fox_l]),
        ffn_conv_b=jnp.concatenate([lg[i]["conv_b"] for i in range(depth)]),
        norm_final=d_norm_final,
        gla_w_gate=jnp.stack([lg[i]["gla_w_gate"] for i in gla_l]),
        ffn_conv_w=jnp.stack([lg[i]["conv_w"] for i in range(depth)]),
        loss=loss_part[:, :1],
    )
    order = ("norm_mix", "norm_ffn", "gla_b_gate", "gla_norm", "fox_b_f", "fox_q_norm", "fox_k_norm", "ffn_conv_b",
             "norm_final", "gla_w_gate", "ffn_conv_w", "loss")
    packed = _pack([small_parts[nm] for nm in order])
    dmod = jnp.stack([jnp.concatenate([lg[i][nm] for nm in ("sh_m", "sc_m", "g_m", "sh_f", "sc_f", "g_f")], axis=1)
                      for i in range(depth)])
    hs_small, tok_small = _split_start([packed[None], dmod[None]], _all_plan, "gather_small_grads_start",
                                       land_blocks=N_DEV)
    dw_main, da_tail, h1_gla = lg[0]["gla_w_in_unsent"]
    dw_in_t = _tail_rows(da_tail + tok_small.astype(BF16), h1_gla, dw_main, rank, "gla_in_tail_dw")
    send("gla_in", [dw_in_t.reshape(N_DEV, -1, d)])
    started = pending[-1][1][0][0]

    received = {}

    def arrive(key, after):
        sums, lands = _split_wait(sent[key], _chip_plan, after, f"scatter_{key}_chip_wait")
        received[key] = list(zip(lands, sums))

    for key in ("ffn1", "fox", "ffn0", "gla_out"):
        arrive(key, started)

    out_g, out_d, out_m, out_v = {}, {}, {}, {}

    chip_idx = chip.astype(jnp.int32).reshape(1)

    def update(nm, g_arr, transposed=False):
        swap = (lambda t: jnp.transpose(t, (0, 2, 1))) if transposed else (lambda t: t)
        if isinstance(g_arr, list):
            res = _adamw_pieces(swap(w[nm]), [t[0] for t in g_arr], [t[1] for t in g_arr], chip_idx,
                                swap(mom_m[nm]), swap(mom_v[nm]), f"adamw_{nm}")
        else:
            res = _adamw(w[nm], g_arr, mom_m[nm], mom_v[nm], f"adamw_{nm}")
        out_g[nm], out_d[nm], out_m[nm], out_v[nm] = (swap(t) for t in res)

    update("gla_w_out", [received["gla_out"][0]])
    update("fox_w_out", [received["fox"][1]])
    tok_flush = flush(out_g["fox_w_out"])
    update("ffn_w_up", [received[f"ffn{i}"][0] for i in range(depth)])
    update("fox_w_in", [received["fox"][0]], transposed=True)
    update("ffn_w_down", [received[f"ffn{i}"][1] for i in range(depth)])

    updated = ("gla_w_out", "fox_w_in", "fox_w_out", "ffn_w_up", "ffn_w_down")
    (packed_mine, dmod_mine), (packed_all, dmod_all) = _split_wait(
        hs_small, _all_plan, [out_d[nm] for nm in updated], "gather_small_grads_wait")
    packed_all = lax.dynamic_update_slice(packed_all, packed_mine + tok_flush, (me, 0, 0))
    dmod_all = lax.dynamic_update_slice(dmod_all, dmod_mine, (me, 0, 0, 0))
    summed = _unpack(_sum8(packed_all, "sum_small_grads"), [small_parts[nm].shape for nm in order])
    small_g = dict(zip(order, summed))
    loss = small_g["loss"][0, 0]
    dmod_all = dmod_all[:, :, 0, :]
    grads = {}
    cond_t = _pad_cols(jnp.transpose(cond_all)).astype(BF16)
    dmod_cols = lax.dynamic_slice(dmod_all, (0, 0, me * mod_cols), (N_DEV, depth, mod_cols))
    g_w_mod = lax.empty(w_mod.shape, F32)
    for i in range(depth):
        rhs = jnp.pad(dmod_cols[:, i], ((0, LANE - N_DEV), (0, 0)))
        g_w_mod = _matmul(cond_t, rhs, name=f"mod_dw_{i}", tn=768, into=(g_w_mod, i))
    grads["w_mod"] = g_w_mod
    small_g["b_mod"] = _sum8(dmod_all.reshape(N_DEV, 1, -1), "sum_b_mod").reshape(depth, -1)
    update("w_mod", grads["w_mod"])

    gate_cols = gla_w_gate.shape[2]
    conv_cols = ffn_conv_w.shape[2]
    local_small = dict(small_g)
    local_small["gla_w_gate"] = lax.dynamic_slice_in_dim(small_g["gla_w_gate"], me * gate_cols, gate_cols, axis=2)
    local_small["ffn_conv_w"] = lax.dynamic_slice_in_dim(small_g["ffn_conv_w"], me * conv_cols, conv_cols, axis=2)
    names = SMALL + SMALL_SHARDED
    shapes = [w[nm].shape for nm in names]
    res = _adamw(_pack([w[nm] for nm in names])[None], _pack([local_small[nm] for nm in names])[None],
                 _pack([mom_m[nm] for nm in names])[None], _pack([mom_v[nm] for nm in names])[None], "adamw_small")
    for tgt, flat in zip((out_g, out_d, out_m, out_v), res):
        for nm, arr in zip(names, _unpack(flat[0], shapes)):
            tgt[nm] = arr

    arrive("gla_in", [out_d[nm] for nm in updated + ("w_mod",)])
    update("gla_w_in", [received["gla_in"][0]], transposed=True)

    return (loss, grad_x, *[out_g[n] for n in WEIGHTS], *[out_d[n] for n in WEIGHTS],
            *[out_m[n] for n in WEIGHTS], *[out_v[n] for n in WEIGHTS])
```

```python
import jax
import jax.numpy as jnp
from jax import lax
from jax.experimental import pallas as pl
from jax.experimental.pallas import tpu as pltpu

F32, BF16 = jnp.float32, jnp.bfloat16
N_DEV = 8
GLA_HEADS = 4
GLA_TAU = 16.0
GLA_CHUNK = 64
NORM_EPS = 1e-6
ADAM_LR, ADAM_B1, ADAM_B2, ADAM_EPS, ADAM_WD, ADAM_STEP = 0.001, 0.9, 0.999, 1e-08, 0.01, 10
LANE = 128
VMEM_LIMIT = 56 * 1024 * 1024
NEG = -1e30


def _pcall(body, **kw):
    return pl.pallas_call(body, **kw)


def _params(n_axes):
    return pltpu.CompilerParams(dimension_semantics=("arbitrary",) * n_axes, vmem_limit_bytes=VMEM_LIMIT)


def _tile(dim, pref):
    if dim <= pref:
        return dim
    t = pref
    while dim % t:
        t -= LANE
    assert t > 0, (dim, pref)
    return t


def _dot(a, b, ta=False, tb=False):
    dims = (((0,) if ta else (1,), (1,) if tb else (0,)), ((), ()))
    return lax.dot_general(a.astype(BF16), b.astype(BF16), dims, preferred_element_type=F32)


def _split3(x):
    hi = x.astype(BF16)
    r1 = x - hi.astype(F32)
    mid = r1.astype(BF16)
    lo = (r1 - mid.astype(F32)).astype(BF16)
    return hi, mid, lo


def _tri_matmul(tri, x):
    hi, mid, lo = _split3(x)
    return _dot(tri, hi) + _dot(tri, mid) + _dot(tri, lo)


def _tri(n, upper=False):
    r = lax.broadcasted_iota(jnp.int32, (n, n), 0)
    c = lax.broadcasted_iota(jnp.int32, (n, n), 1)
    return jnp.where((r <= c) if upper else (r >= c), 1.0, 0.0).astype(BF16)


def _log_sigmoid(x):
    return jnp.minimum(x, 0.0) - jnp.log(1.0 + jnp.exp(-jnp.abs(x)))


def _sigmoid(x):
    return 1.0 / (1.0 + jnp.exp(-x))


def _silu(x):
    return x * _sigmoid(x)


def _dsilu(x):
    s = _sigmoid(x)
    return s * (1.0 + x * (1.0 - s))


def _matmul(a, b, *, name, ta=False, tb=False, out_dtypes=(F32,), tm=1024, tn=1024, tk=2048,
            epilogue=None, extras=(), a_halves=False, b_halves=False, b_shards=False, out_shards=False,
            b_rows=None, out_rows=None, b_layer=None, into=None):
    if a_halves:
        assert not ta
        m, k = a.shape[1], 2 * a.shape[2]
    else:
        m, k = (a.shape[1], a.shape[0]) if ta else a.shape
    if b_halves:
        assert not tb and b.shape[1] == k
        n = 2 * b.shape[2]
    elif b_shards:
        n = b.shape[1] if tb else N_DEV * b.shape[2]
        assert (N_DEV * b.shape[2] if tb else b.shape[1]) == k, (a.shape, b.shape, ta, tb)
    elif b_layer is not None:
        assert not tb and b.shape[1] == k
        n = b.shape[2]
    else:
        rows = b.shape[0] if b_rows is None else b_rows
        n = rows if tb else b.shape[1]
        assert (b.shape[1] if tb else rows) == k, (a.shape, b.shape, ta, tb)
    n_unit = n // N_DEV if (out_shards or (b_shards and not tb)) else (n // 2 if b_halves else n)
    k_unit = k // N_DEV if (b_shards and tb) else (k // 2 if a_halves else k)
    tm, tn, tk = _tile(m, tm), _tile(n_unit, tn), _tile(k_unit, tk)
    nk = k // tk
    if a_halves:
        a_spec = pl.BlockSpec((None, tm, tk), lambda i, j, kk: (kk // (nk // 2), i, kk % (nk // 2)))
    elif ta:
        a_spec = pl.BlockSpec((tk, tm), lambda i, j, kk: (kk, i))
    else:
        a_spec = pl.BlockSpec((tm, tk), lambda i, j, kk: (i, kk))
    n_per, k_per = n // tn // N_DEV, nk // N_DEV
    if b_halves:
        b_spec = pl.BlockSpec((None, tk, tn), lambda i, j, kk: (j // (n // tn // 2), kk, j % (n // tn // 2)))
    elif b_shards and tb:
        b_spec = pl.BlockSpec((None, tn, tk), lambda i, j, kk: (kk // k_per, j, kk % k_per))
    elif b_shards:
        b_spec = pl.BlockSpec((None, tk, tn), lambda i, j, kk: (j // n_per, kk, j % n_per))
    elif b_layer is not None:
        b_spec = pl.BlockSpec((None, tk, tn), lambda i, j, kk: (b_layer, kk, j))
    elif tb:
        b_spec = pl.BlockSpec((tn, tk), lambda i, j, kk: (j, kk))
    else:
        b_spec = pl.BlockSpec((tk, tn), lambda i, j, kk: (kk, j))
    ex_specs = []
    for kind, arr in extras:
        if kind == "mn":
            assert arr.shape == (m, n), (arr.shape, m, n)
            ex_specs.append(pl.BlockSpec((tm, tn), lambda i, j, kk: (i, j)))
        else:
            assert arr.shape == (1, n), (arr.shape, n)
            ex_specs.append(pl.BlockSpec((1, tn), lambda i, j, kk: (0, j)))
    n_ex, n_out = len(extras), len(out_dtypes)

    def body(a_ref, b_ref, *rest):
        ex, outs, acc = rest[:n_ex], rest[-1 - n_out:-1], rest[-1]
        kk = pl.program_id(2)

        @pl.when(kk == 0)
        def _():
            acc[...] = jnp.zeros_like(acc)

        acc[...] += _dot(a_ref[...], b_ref[...], ta, tb)

        @pl.when(kk == nk - 1)
        def _():
            if epilogue is None:
                vals = (acc[...],)
            else:
                vals = epilogue(acc[...], *[e[...] for e in ex])
            for o, v in zip(outs, vals):
                o[...] = v.astype(o.dtype)

    if out_shards:
        out_spec = pl.BlockSpec((None, tm, tn), lambda i, j, kk: (j // n_per, i, j % n_per))
        out_dims = (N_DEV, m, n // N_DEV)
    elif into is not None:
        out_spec = pl.BlockSpec((None, tm, tn), lambda i, j, kk: (into[1], i, j))
        out_dims = into[0].shape
    else:
        out_spec = pl.BlockSpec((tm, tn), lambda i, j, kk: (i, j))
        out_dims = (m if out_rows is None else out_rows, n)
    operands = [a, b, *[arr for _, arr in extras]]
    aliases = {}
    if into is not None:
        assert n_out == 1 and into[0].shape[1:] == (m, n) and into[0].dtype == out_dtypes[0]
        aliases = {len(operands): 0}
        operands.append(into[0])
    res = _pcall(
        body, name=name, grid=(m // tm, n // tn, nk),
        in_specs=[a_spec, b_spec] + ex_specs + [pl.BlockSpec(memory_space=pl.ANY)] * len(aliases),
        out_specs=[out_spec] * n_out,
        out_shape=[jax.ShapeDtypeStruct(out_dims, d) for d in out_dtypes],
        scratch_shapes=[pltpu.VMEM((tm, tn), F32)],
        input_output_aliases=aliases,
        compiler_params=_params(3),
    )(*operands)
    return res[0] if n_out == 1 else res


def _tail_rows(a, b, into, rows, name, tn=1024):
    k, n = b.shape
    m_total = into.shape[0]
    tn = _tile(n, tn)

    def body(a_ref, b_ref, into_ref, out_ref):
        out_ref[...] = _dot(a_ref[...], b_ref[...], ta=True)[:rows].astype(out_ref.dtype)

    return _pcall(
        body, name=name, grid=(n // tn,),
        in_specs=[pl.BlockSpec((k, a.shape[1]), lambda j: (0, 0)), pl.BlockSpec((k, tn), lambda j: (0, j)),
                  pl.BlockSpec(memory_space=pl.ANY)],
        out_specs=pl.BlockSpec((rows, tn), lambda j: (m_total // rows - 1, j)),
        out_shape=jax.ShapeDtypeStruct(into.shape, into.dtype),
        input_output_aliases={2: 0}, compiler_params=_params(1),
    )(a, b, into)


def _rowwise(fn, ins, outs, *, name, tr=256, into=None):
    rows = next(e[1].shape[0] for e in ins if e[0] != "full")
    tr = _tile(rows, tr)
    in_specs = []
    for entry in ins:
        kind, arr = entry[0], entry[1]
        assert kind == "full" or (arr.shape[0] == rows and arr.ndim == 2)
        if kind == "row":
            in_specs.append(pl.BlockSpec((tr, arr.shape[1]), lambda i: (i, 0)))
        elif kind == "cols":
            in_specs.append(pl.BlockSpec((tr, entry[3]), lambda i, cb=entry[2]: (i, cb)))
        else:
            in_specs.append(pl.BlockSpec(arr.shape, lambda i, nd=arr.ndim: (0,) * nd))
    out_specs, out_shape = [], []
    for entry in outs:
        kind, w, dt = entry[:3]
        if kind == "row":
            out_specs.append(pl.BlockSpec((tr, w), lambda i: (i, 0)))
            out_shape.append(jax.ShapeDtypeStruct((rows, w), dt))
        elif kind == "band":
            out_specs.append(pl.BlockSpec((tr, w), lambda i, cb=entry[3]: (i, cb)))
            out_shape.append(jax.ShapeDtypeStruct((rows, entry[4]), dt))
        else:
            out_specs.append(pl.BlockSpec((1, w), lambda i: (0, 0)))
            out_shape.append(jax.ShapeDtypeStruct((1, w), dt))
    n_in = len(ins)
    operands = [e[1] for e in ins]
    aliases = {}
    if into is not None:
        aliases = {len(operands): into[1]}
        in_specs.append(pl.BlockSpec(memory_space=pl.ANY))
        operands.append(into[0])

    def body(*refs):
        i = pl.program_id(0)
        vals = fn(*[r[...] for r in refs[:n_in]])
        for entry, o, v in zip(outs, refs[len(operands):], vals):
            if entry[0] == "acc":
                @pl.when(i == 0)
                def _(o=o):
                    o[...] = jnp.zeros_like(o)

                o[...] += v.astype(o.dtype)
            else:
                o[...] = v.astype(o.dtype)

    return _pcall(body, name=name, grid=(rows // tr,), in_specs=in_specs, out_specs=out_specs,
                  out_shape=out_shape, input_output_aliases=aliases, compiler_params=_params(1))(*operands)


def _colsum(x):
    return jnp.sum(x, axis=0, keepdims=True)


def _norm_stats(x):
    rstd = lax.rsqrt(jnp.mean(x * x, axis=-1, keepdims=True) + NORM_EPS)
    return x * rstd, rstd


def _norm_bwd(dxhat, xhat, rstd):
    return rstd * (dxhat - xhat * jnp.mean(dxhat * xhat, axis=-1, keepdims=True))


def _adaln_fwd(x, gain, sc, sh, name):
    def fn(x, gain, sc, sh):
        xhat, _ = _norm_stats(x)
        return ((xhat * gain) * (1.0 + sc) + sh,)

    return _rowwise(fn, [("row", x), ("full", gain), ("full", sc), ("full", sh)],
                    [("row", x.shape[1], BF16)], name=name)[0]


def _adaln_bwd(x, dh, dres, gain, sc, name, branch=None):
    d = x.shape[1]

    def fn(x, dh, dres, gain, sc, *br):
        xhat, rstd = _norm_stats(x)
        dxhat = dh * (gain * (1.0 + sc))
        dx = dres + _norm_bwd(dxhat, xhat, rstd)
        return (dx, _colsum(dh), _colsum(dh * (xhat * gain)), _colsum(dh * xhat * (1.0 + sc))) + _branch_bwd(dx, *br)

    return _rowwise(fn, [("row", x), ("row", dh), ("row", dres), ("full", gain), ("full", sc)] + _branch_ins(branch),
                    [("row", d, F32), ("acc", d, F32), ("acc", d, F32), ("acc", d, F32)] + _branch_outs(branch, d),
                    name=name)


def _branch_ins(branch):
    return [] if branch is None else [("row", branch[0]), ("full", branch[1])]


def _branch_outs(branch, d):
    return [] if branch is None else [("row", d, BF16), ("acc", d, F32)]


def _branch_bwd(dx, *branch):
    if not branch:
        return ()
    y, g = branch
    return dx * (1.0 + g), _colsum(dx * y)


def _final_loss(x, target, gain, name, branch):
    d = x.shape[1]

    def fn(x, t, gain, *br):
        xhat, rstd = _norm_stats(x)
        err = xhat * gain - t
        dy = err * (1.0 / d)
        loss = 0.5 * jnp.sum(jnp.mean(err * err, axis=-1, keepdims=True), axis=0, keepdims=True)
        dx = _norm_bwd(dy * gain, xhat, rstd)
        return (dx, _colsum(dy * xhat), jnp.broadcast_to(loss, (1, LANE))) + _branch_bwd(dx, *br)

    return _rowwise(fn, [("row", x), ("row", target), ("full", gain)] + _branch_ins(branch),
                    [("row", d, F32), ("acc", d, F32), ("acc", LANE, F32)] + _branch_outs(branch, d), name=name)


def _gla_gates(q, k, a, wg, bg, scale, c):
    ga = _dot(a, wg) + bg
    la = _log_sigmoid(ga) * (1.0 / GLA_TAU)
    b = _tri_matmul(_tri(c), la)
    bl = _colsum(la)
    eb, enb, eend = jnp.exp(b), jnp.exp(-b), jnp.exp(bl - b)
    q = q * scale
    return dict(ga=ga, eb=eb, enb=enb, eend=eend, dec=jnp.exp(bl), q_dec=q * eb, k_inv=k * enb, k_end=k * eend)


def _causal(c):
    return lax.broadcasted_iota(jnp.int32, (c, c), 0) >= lax.broadcasted_iota(jnp.int32, (c, c), 1)


def _gla_specs(heads, c, dk, dv, chunk):
    return [
        pl.BlockSpec((c, heads * dk), lambda n: (chunk(n), 0)),
        pl.BlockSpec((c, heads * dk), lambda n: (chunk(n), 1)),
        pl.BlockSpec((c, heads * dv), lambda n: (chunk(n), 1)),
        pl.BlockSpec((c, LANE), lambda n: (chunk(n), 0)),
        pl.BlockSpec((LANE, heads * dk), lambda n: (0, 0)),
        pl.BlockSpec((1, heads * dk), lambda n: (0, 0)),
    ]


def _gla_fwd(proj, a_tail, wg_p, bg, name):
    s = proj.shape[0]
    heads, c = GLA_HEADS, GLA_CHUNK
    dk = wg_p.shape[1] // heads
    dv = 2 * dk
    n_chunks = s // c
    scale = dk ** -0.5

    def body(q_ref, k_ref, v_ref, a_ref, wg_ref, bg_ref, o_ref, st_ref, state):
        @pl.when(pl.program_id(0) == 0)
        def _():
            state[...] = jnp.zeros_like(state)

        a = a_ref[...]
        for h in range(heads):
            sk, sv = slice(h * dk, (h + 1) * dk), slice(h * dv, (h + 1) * dv)
            g = _gla_gates(q_ref[:, sk], k_ref[:, sk], a, wg_ref[:, sk], bg_ref[:, sk], scale, c)
            v = v_ref[:, sv]
            st = state[h]
            attn = jnp.where(_causal(c), _dot(g["q_dec"], g["k_inv"], tb=True), 0.0)
            o_ref[:, sv] = _dot(attn, v) + _dot(g["q_dec"], st, tb=True)
            st_ref[h] = st.astype(st_ref.dtype)
            state[h] = g["dec"] * st + _dot(v, g["k_end"], ta=True)

    return _pcall(
        body, name=name, grid=(n_chunks,),
        in_specs=_gla_specs(heads, c, dk, dv, lambda n: n),
        out_specs=[pl.BlockSpec((c, heads * dv), lambda n: (n, 0)),
                   pl.BlockSpec((heads, None, dv, dk), lambda n: (0, n, 0, 0))],
        out_shape=[jax.ShapeDtypeStruct((s, heads * dv), F32),
                   jax.ShapeDtypeStruct((heads, n_chunks, dv, dk), BF16)],
        scratch_shapes=[pltpu.VMEM((heads, dv, dk), F32)],
        compiler_params=_params(1),
    )(proj, proj, proj, a_tail, wg_p, bg)


def _gla_bwd(proj, a_tail, wg_p, bg, states, d_o, dproj, name):
    s = proj.shape[0]
    heads, c = GLA_HEADS, GLA_CHUNK
    dk = wg_p.shape[1] // heads
    dv = 2 * dk
    n_chunks = s // c
    scale = dk ** -0.5
    k0, v0 = heads * dk, 2 * heads * dk

    def body(q_ref, k_ref, v_ref, a_ref, wg_ref, bg_ref, st_ref, do_ref, dproj_in, dqkv_ref, dga_ref, dstate):
        @pl.when(pl.program_id(0) == 0)
        def _():
            dstate[...] = jnp.zeros_like(dstate)

        a = a_ref[...]
        mask = _causal(c)
        for h in range(heads):
            sk, sv = slice(h * dk, (h + 1) * dk), slice(h * dv, (h + 1) * dv)
            out_k, out_v = slice(k0 + h * dk, k0 + (h + 1) * dk), slice(v0 + h * dv, v0 + (h + 1) * dv)
            g = _gla_gates(q_ref[:, sk], k_ref[:, sk], a, wg_ref[:, sk], bg_ref[:, sk], scale, c)
            v, st, dst, d_out = v_ref[:, sv], st_ref[h], dstate[h], do_ref[:, sv]
            q_dec, k_inv, k_end = g["q_dec"], g["k_inv"], g["k_end"]
            attn = jnp.where(mask, _dot(q_dec, k_inv, tb=True), 0.0)
            d_attn = jnp.where(mask, _dot(d_out, v, tb=True), 0.0)
            d_qdec = _dot(d_attn, k_inv) + _dot(d_out, st)
            d_kinv = _dot(d_attn, q_dec, ta=True)
            d_kend = _dot(v, dst)
            dqkv_ref[:, out_v] = (_dot(attn, d_out, ta=True) + _dot(k_end, dst, tb=True)).astype(dqkv_ref.dtype)
            d_dec = jnp.sum(dst * st.astype(F32), axis=0, keepdims=True)
            dstate[h] = g["dec"] * dst + _dot(d_out, q_dec, ta=True)

            dqkv_ref[:, sk] = (d_qdec * (scale * g["eb"])).astype(dqkv_ref.dtype)
            dqkv_ref[:, out_k] = (d_kinv * g["enb"] + d_kend * g["eend"]).astype(dqkv_ref.dtype)
            kk = d_kend * k_end
            db = d_qdec * q_dec - d_kinv * k_inv - kk
            dbl = jnp.sum(kk, axis=0, keepdims=True) + d_dec * g["dec"]
            last = lax.broadcasted_iota(jnp.int32, db.shape, 0) == c - 1
            db = db + jnp.where(last, dbl, 0.0)
            dla = _tri_matmul(_tri(c, upper=True), db)
            dga_ref[:, sk] = dla * (1.0 / GLA_TAU) * _sigmoid(-g["ga"])

    chunk = lambda n: n_chunks - 1 - n
    rev = lambda n: (chunk(n), 0)
    return _pcall(
        body, name=name, grid=(n_chunks,),
        in_specs=_gla_specs(heads, c, dk, dv, chunk) + [
            pl.BlockSpec((heads, None, dv, dk), lambda n: (0, chunk(n), 0, 0)),
            pl.BlockSpec((c, heads * dv), rev), pl.BlockSpec(memory_space=pl.ANY)],
        out_specs=[pl.BlockSpec((c, v0 + heads * dv), rev), pl.BlockSpec((c, heads * dk), rev)],
        out_shape=[jax.ShapeDtypeStruct(dproj.shape, dproj.dtype), jax.ShapeDtypeStruct((s, heads * dk), F32)],
        scratch_shapes=[pltpu.VMEM((heads, dv, dk), F32)],
        input_output_aliases={8: 0},
        compiler_params=_params(1),
    )(proj, proj, proj, a_tail, wg_p, bg, states, d_o, dproj)


def _gla_post_fwd(o, r, gn, name):
    dvt = o.shape[1]
    dv = dvt // GLA_HEADS

    def fn(o, r, gn):
        outs = []
        for h in range(GLA_HEADS):
            sl = slice(h * dv, (h + 1) * dv)
            ohat, _ = _norm_stats(o[:, sl])
            outs.append((ohat * gn[:, sl]) * _silu(r[:, sl]))
        return (jnp.concatenate(outs, axis=1),)

    return _rowwise(fn, [("row", o), r, ("full", gn)], [("row", dvt, BF16)], name=name)[0]


def _gla_post_bwd(o, r, gn, dog, name):
    dvt = o.shape[1]
    dv = dvt // GLA_HEADS

    def fn(o, r, gn, dog):
        d_o, d_r, d_g = [], [], []
        for h in range(GLA_HEADS):
            sl = slice(h * dv, (h + 1) * dv)
            ohat, rstd = _norm_stats(o[:, sl])
            g, rr, dd = gn[:, sl], r[:, sl], dog[:, sl]
            d_r.append(dd * (ohat * g) * _dsilu(rr))
            don = dd * _silu(rr)
            d_g.append(_colsum(don * ohat))
            d_o.append(_norm_bwd(don * g, ohat, rstd))
        return jnp.concatenate(d_o, axis=1), jnp.concatenate(d_r, axis=1), jnp.concatenate(d_g, axis=1)

    return _rowwise(fn, [("row", o), r, ("full", gn), ("row", dog)],
                    [("row", dvt, F32), ("band", dvt, BF16, 2, 3 * dvt), ("acc", dvt, F32)], name=name)


def _fox_prep(q, k, v, qg, kg, d, hd, name):
    heads = d // hd
    scale = hd ** -0.5

    def fn(q, k, v, qg, kg):
        qs, ks = [], []
        for h in range(heads):
            sl = slice(h * hd, (h + 1) * hd)
            qs.append(_norm_stats(q[:, sl])[0] * qg * scale)
            ks.append(_norm_stats(k[:, sl])[0] * kg)
        return jnp.concatenate(qs, axis=1), jnp.concatenate(ks, axis=1), v

    return _rowwise(fn, [q, k, v, ("full", qg), ("full", kg)],
                    [("row", d, BF16)] * 3, name=name)


def _fox_prep_bwd(q, k, dqn, dkn, qg, kg, hd, dproj, name):
    d = dqn.shape[1]
    heads = d // hd
    scale = hd ** -0.5

    def fn(q, k, dqn, dkn, qg, kg):
        dq, dk, gq, gk = [], [], [], []
        for h in range(heads):
            sl = slice(h * hd, (h + 1) * hd)
            for x, dxn, g, s, dl, gl in ((q, dqn, qg, scale, dq, gq), (k, dkn, kg, 1.0, dk, gk)):
                xhat, rstd = _norm_stats(x[:, sl])
                dn = dxn[:, sl] * s
                gl.append(_colsum(dn * xhat))
                dl.append(_norm_bwd(dn * g, xhat, rstd))
        cat = lambda t: jnp.concatenate(t, axis=1)
        return cat(dq + dk), cat(gq), cat(gk)

    return _rowwise(fn, [q, k, ("row", dqn), ("row", dkn), ("full", qg), ("full", kg)],
                    [("band", 2 * d, BF16, 0, 4 * d), ("acc", d, F32), ("acc", d, F32)], name=name, into=(dproj, 0))


def _fox_cum(fl, bf_p, name, tb=256):
    s = fl.shape[0]
    tb = _tile(s, tb)

    def body(fl_ref, bf_ref, cum_ref, carry):
        @pl.when(pl.program_id(0) == 0)
        def _():
            carry[...] = jnp.zeros_like(carry)

        lf = _log_sigmoid(fl_ref[...] + bf_ref[...])
        cum_ref[...] = _tri_matmul(_tri(tb), lf) + carry[...]
        carry[...] += _colsum(lf)

    return _pcall(
        body, name=name, grid=(s // tb,),
        in_specs=[pl.BlockSpec((tb, LANE), lambda i: (i, 0)), pl.BlockSpec((1, LANE), lambda i: (0, 0))],
        out_specs=pl.BlockSpec((tb, LANE), lambda i: (i, 0)),
        out_shape=jax.ShapeDtypeStruct((s, LANE), F32),
        scratch_shapes=[pltpu.VMEM((1, LANE), F32)],
        compiler_params=_params(1),
    )(fl, bf_p)


def _fox_cum_bwd(dcum, fl, bf_p, name, tb=256):
    s = fl.shape[0]
    tb = _tile(s, tb)
    nb = s // tb

    def body(dc_ref, fl_ref, bf_ref, dfl_ref, dbf_ref, carry):
        @pl.when(pl.program_id(0) == 0)
        def _():
            carry[...] = jnp.zeros_like(carry)
            dbf_ref[...] = jnp.zeros_like(dbf_ref)

        dc = dc_ref[...]
        dlf = _tri_matmul(_tri(tb, upper=True), dc) + carry[...]
        carry[...] += _colsum(dc)
        dfl = dlf * _sigmoid(-(fl_ref[...] + bf_ref[...]))
        dfl_ref[...] = dfl
        dbf_ref[...] += _colsum(dfl)

    rev = lambda i: (nb - 1 - i, 0)
    return _pcall(
        body, name=name, grid=(nb,),
        in_specs=[pl.BlockSpec((tb, LANE), rev), pl.BlockSpec((tb, LANE), rev), pl.BlockSpec((1, LANE), lambda i: (0, 0))],
        out_specs=[pl.BlockSpec((tb, LANE), rev), pl.BlockSpec((1, LANE), lambda i: (0, 0))],
        out_shape=[jax.ShapeDtypeStruct((s, LANE), F32), jax.ShapeDtypeStruct((1, LANE), F32)],
        scratch_shapes=[pltpu.VMEM((1, LANE), F32)],
        compiler_params=_params(1),
    )(dcum, fl, bf_p)


def _fox_attn_fwd(qn, kn, vb, cum_col, cum_row, hd, t, name):
    s, d = qn.shape
    heads = d // hd
    nq = s // t

    def body(q_ref, k_ref, v_ref, cc_ref, cr_ref, o_ref, lse_ref):
        qi = pl.program_id(1)
        q = q_ref[...]
        cq = cc_ref[...]
        qpos = qi * t + lax.broadcasted_iota(jnp.int32, (t, 1), 0)

        def step(kj, carry, diagonal=False):
            m, l, acc = carry
            off = pl.multiple_of(kj * t, t)
            ks, vs = k_ref[pl.ds(off, t), :], v_ref[pl.ds(off, t), :]
            sc = _dot(q, ks, tb=True) + cq - cr_ref[kj]
            if diagonal:
                kpos = off + lax.broadcasted_iota(jnp.int32, (1, t), 1)
                sc = jnp.where(kpos <= qpos, sc, NEG)
            m_new = jnp.maximum(m, jnp.max(sc, axis=1, keepdims=True))
            alpha = jnp.exp(m - m_new)
            p = jnp.exp(sc - m_new)
            return m_new, alpha * l + jnp.sum(p, axis=1, keepdims=True), alpha * acc + _dot(p, vs)

        init = (jnp.full((t, 1), NEG, F32), jnp.zeros((t, 1), F32), jnp.zeros((t, hd), F32))
        m, l, acc = step(qi, lax.fori_loop(0, qi, step, init), diagonal=True)
        o_ref[...] = acc / l
        lse_ref[...] = m + jnp.log(l)

    return _pcall(
        body, name=name, grid=(heads, nq),
        in_specs=[pl.BlockSpec((t, hd), lambda h, i: (i, h)),
                  pl.BlockSpec((s, hd), lambda h, i: (0, h)),
                  pl.BlockSpec((s, hd), lambda h, i: (0, h)),
                  pl.BlockSpec((None, t, 1), lambda h, i: (h, i, 0)),
                  pl.BlockSpec((None, nq, 1, t), lambda h, i: (h, 0, 0, 0))],
        out_specs=[pl.BlockSpec((t, hd), lambda h, i: (i, h)), pl.BlockSpec((None, t, 1), lambda h, i: (h, i, 0))],
        out_shape=[jax.ShapeDtypeStruct((s, d), F32), jax.ShapeDtypeStruct((heads, s, 1), F32)],
        compiler_params=_params(2),
    )(qn, kn, vb, cum_col, cum_row)


def _fox_attn_bwd(qn, kn, vb, d_o, o, lse, cum_col, cum_row, hd, t, dproj, name):
    s, d = qn.shape
    heads = d // hd
    nq = s // t

    def body(q_ref, k_ref, v_ref, do_ref, o_ref, lse_ref, cc_ref, cr_ref, dproj_in,
             dq_ref, dk_ref, dv_ref, dcq_ref, dck_ref, delta):
        kj = pl.program_id(1)

        @pl.when(kj == 0)
        def _():
            dq_ref[...] = jnp.zeros_like(dq_ref)
            dcq_ref[...] = jnp.zeros_like(dcq_ref)
            delta[...] = jnp.sum(do_ref[...] * o_ref[...], axis=1, keepdims=True)

        ks, vs, cr = k_ref[...], v_ref[...], cr_ref[...]
        kpos = kj * t + lax.broadcasted_iota(jnp.int32, (1, t), 1)

        def step(qi, carry, diagonal=False):
            dk, dv, dck = carry
            rows = pl.ds(pl.multiple_of(qi * t, t), t)
            q, d_out = q_ref[rows, :], do_ref[rows, :]
            sc = _dot(q, ks, tb=True) + cc_ref[rows, :] - cr
            p = jnp.exp(sc - lse_ref[rows, :])
            if diagonal:
                qpos = qi * t + lax.broadcasted_iota(jnp.int32, (t, 1), 0)
                p = jnp.where(kpos <= qpos, p, 0.0)
            ds = p * (_dot(d_out, vs, tb=True) - delta[rows, :])
            dq_ref[rows, :] += _dot(ds, ks)
            dcq_ref[rows, :] += jnp.sum(ds, axis=1, keepdims=True)
            return dk + _dot(ds, q, ta=True), dv + _dot(p, d_out, ta=True), dck + _colsum(ds)

        init = (jnp.zeros((t, hd), F32), jnp.zeros((t, hd), F32), jnp.zeros((1, t), F32))
        dk, dv, dck = lax.fori_loop(kj + 1, nq, step, step(kj, init, diagonal=True))
        dk_ref[...] = dk.astype(dk_ref.dtype)
        dv_ref[...] = dv.astype(dv_ref.dtype)
        dck_ref[...] = dck

    head_rows = lambda h, j: (0, h)
    blk = lambda h, j: (j, h)
    return _pcall(
        body, name=name, grid=(heads, nq),
        in_specs=[pl.BlockSpec((s, hd), head_rows), pl.BlockSpec((t, hd), blk), pl.BlockSpec((t, hd), blk),
                  pl.BlockSpec((s, hd), head_rows), pl.BlockSpec((s, hd), head_rows),
                  pl.BlockSpec((None, s, 1), lambda h, j: (h, 0, 0)),
                  pl.BlockSpec((None, s, 1), lambda h, j: (h, 0, 0)),
                  pl.BlockSpec((None, None, 1, t), lambda h, j: (h, j, 0, 0)),
                  pl.BlockSpec(memory_space=pl.ANY)],
        out_specs=[pl.BlockSpec((s, hd), head_rows), pl.BlockSpec((t, hd), blk),
                   pl.BlockSpec((t, hd), lambda h, j: (j, 2 * heads + h)),
                   pl.BlockSpec((None, s, 1), lambda h, j: (h, 0, 0)),
                   pl.BlockSpec((None, None, 1, t), lambda h, j: (h, j, 0, 0))],
        out_shape=[jax.ShapeDtypeStruct((s, d), F32), jax.ShapeDtypeStruct((s, d), BF16),
                   jax.ShapeDtypeStruct(dproj.shape, dproj.dtype), jax.ShapeDtypeStruct((heads, s, 1), F32),
                   jax.ShapeDtypeStruct((heads, nq, 1, t), F32)],
        scratch_shapes=[pltpu.VMEM((s, 1), F32)],
        input_output_aliases={8: 2},
        compiler_params=_params(2),
    )(qn, kn, vb, d_o, o, lse, cum_col, cum_row, dproj)


def _fox_gate_fwd(o, og, name):
    def fn(o, og):
        return (o * _sigmoid(og),)

    return _rowwise(fn, [("row", o), og], [("row", o.shape[1], BF16)], name=name)[0]


def _fox_gate_bwd(o, og, dact, name):
    def fn(o, og, dact):
        sg = _sigmoid(og)
        return dact * sg, dact * o * sg * (1.0 - sg)

    d = o.shape[1]
    return _rowwise(fn, [("row", o), og, ("row", dact)], [("row", d, F32), ("band", d, BF16, 3, 4 * d)], name=name)


def _shift_down(x, n):
    rows = lax.broadcasted_iota(jnp.int32, x.shape, 0)
    return jnp.where(rows >= n, pltpu.roll(x, n, 0), 0.0)


def _shift_up(x, n):
    rows = lax.broadcasted_iota(jnp.int32, x.shape, 0)
    return jnp.where(rows < x.shape[0] - n, pltpu.roll(x, x.shape[0] - n, 0), 0.0)


def _conv(u, w_ref, b):
    return w_ref[0:1, :] * _shift_down(u, 2) + w_ref[1:2, :] * _shift_down(u, 1) + w_ref[2:3, :] * u + b


def _conv_act_fwd(u, cw, cb, name, tc=256):
    s, two_f = u.shape
    dff = two_f // 2
    tc = _tile(dff, tc)
    nb = dff // tc

    def body(ug_ref, uv_ref, wg_ref, wv_ref, bg_ref, bv_ref, a_ref):
        gate = _conv(ug_ref[...], wg_ref, bg_ref[...])
        val = _conv(uv_ref[...], wv_ref, bv_ref[...])
        a_ref[...] = (_silu(gate) * val).astype(a_ref.dtype)

    lo, hi = (lambda j: (0, j)), (lambda j: (0, j + nb))
    return _pcall(
        body, name=name, grid=(nb,),
        in_specs=[pl.BlockSpec((s, tc), lo), pl.BlockSpec((s, tc), hi), pl.BlockSpec((3, tc), lo),
                  pl.BlockSpec((3, tc), hi), pl.BlockSpec((1, tc), lo), pl.BlockSpec((1, tc), hi)],
        out_specs=pl.BlockSpec((s, tc), lo),
        out_shape=jax.ShapeDtypeStruct((s, dff), BF16),
        compiler_params=_params(1),
    )(u, u, cw, cw, cb, cb)


def _conv_act_bwd(u, cw, cb, da, name, tc=128):
    s, two_f = u.shape
    dff = two_f // 2
    tc = _tile(dff, tc)
    nb = dff // tc

    def body(ug_ref, uv_ref, wg_ref, wv_ref, bg_ref, bv_ref, da_ref, du_ref, dw_ref, db_ref):
        ug, uv, da = ug_ref[...], uv_ref[...], da_ref[...]
        gate = _conv(ug, wg_ref, bg_ref[...])
        val = _conv(uv, wv_ref, bv_ref[...])
        sg = _sigmoid(gate)
        d_val = da * (gate * sg)
        d_gate = da * val * (sg * (1.0 + gate * (1.0 - sg)))
        for half, (dc, uu, w_ref) in enumerate(((d_gate, ug, wg_ref), (d_val, uv, wv_ref))):
            du = w_ref[0:1, :] * _shift_up(dc, 2) + w_ref[1:2, :] * _shift_up(dc, 1) + w_ref[2:3, :] * dc
            du_ref[half] = du.astype(du_ref.dtype)
            dw_ref[half, 0:1, :] = _colsum(dc * _shift_down(uu, 2))
            dw_ref[half, 1:2, :] = _colsum(dc * _shift_down(uu, 1))
            dw_ref[half, 2:3, :] = _colsum(dc * uu)
            db_ref[half] = _colsum(dc)

    lo, hi = (lambda j: (0, j)), (lambda j: (0, j + nb))
    both = lambda j: (0, 0, j)
    return _pcall(
        body, name=name, grid=(nb,),
        in_specs=[pl.BlockSpec((s, tc), lo), pl.BlockSpec((s, tc), hi), pl.BlockSpec((3, tc), lo),
                  pl.BlockSpec((3, tc), hi), pl.BlockSpec((1, tc), lo), pl.BlockSpec((1, tc), hi),
                  pl.BlockSpec((s, tc), lo)],
        out_specs=[pl.BlockSpec((2, s, tc), both), pl.BlockSpec((2, 3, tc), both), pl.BlockSpec((2, 1, tc), both)],
        out_shape=[jax.ShapeDtypeStruct((2, s, dff), BF16), jax.ShapeDtypeStruct((2, 3, dff), F32),
                   jax.ShapeDtypeStruct((2, 1, dff), F32)],
        compiler_params=_params(1),
    )(u, u, cw, cw, cb, cb, da)


def _adamw_math(w, g, m, v):
    m = ADAM_B1 * m + (1.0 - ADAM_B1) * g
    v = ADAM_B2 * v + (1.0 - ADAM_B2) * (g * g)
    m_hat = m / (1.0 - ADAM_B1 ** ADAM_STEP)
    v_hat = v / (1.0 - ADAM_B2 ** ADAM_STEP)
    delta = -ADAM_LR * (m_hat / (jnp.sqrt(v_hat) + ADAM_EPS) + ADAM_WD * w)
    return delta, m, v


def _update_tiles(r, c, tr):
    tc = c
    if r % 8:
        tr, tc = r, _tile(c, max(LANE, 512 * 1024 // r // LANE * LANE))
    elif r <= tr:
        tr = r
    while r % tr:
        tr -= 8
    return tr, tc


def _adamw(w, g, m, v, name, tr=128):
    layers, r, c = w.shape
    tr, tc = _update_tiles(r, c, tr)

    def body(w_ref, g_ref, m_ref, v_ref, go_ref, d_ref, mo_ref, vo_ref):
        grad = g_ref[...]
        delta, m_new, v_new = _adamw_math(w_ref[...], grad, m_ref[...], v_ref[...])
        go_ref[...], d_ref[...], mo_ref[...], vo_ref[...] = grad, delta, m_new, v_new

    spec = pl.BlockSpec((None, tr, tc), lambda l, i, j: (l, i, j))
    return _pcall(
        body, name=name, grid=(layers, r // tr, c // tc), in_specs=[spec] * 4, out_specs=[spec] * 4,
        out_shape=[jax.ShapeDtypeStruct((layers, r, c), F32)] * 4, compiler_params=_params(3),
    )(w, g, m, v)


def _adamw_pieces(w, lands, sums, chip, m, v, name, tr=128):
    layers, r, c = w.shape
    tr, tc = _update_tiles(r, c, tr)
    nr, nc = r // tr, c // tc

    def body(chip_ref, w_ref, *rest):
        land_refs, own_refs = rest[:layers], rest[layers:2 * layers]
        m_ref, v_ref, go_ref, d_ref, mo_ref, vo_ref = rest[2 * layers:]
        for layer in range(layers):
            @pl.when(pl.program_id(0) == layer)
            def _(land_ref=land_refs[layer], own_ref=own_refs[layer]):
                grad = jnp.zeros(w_ref.shape, F32)
                for q in range(4):
                    grad = grad + jnp.where(chip_ref[0] == q, own_ref[...], land_ref[q]).astype(F32)
                delta, m_new, v_new = _adamw_math(w_ref[...], grad, m_ref[...], v_ref[...])
                go_ref[...], d_ref[...], mo_ref[...], vo_ref[...] = grad, delta, m_new, v_new

    def walk(k, l, i, j):
        here = l == k
        return jnp.where(here, i, jnp.where(l < k, 0, nr - 1)), jnp.where(here, j, jnp.where(l < k, 0, nc - 1))

    spec = pl.BlockSpec((None, tr, tc), lambda l, i, j, chip_ref: (l, i, j))
    land_specs = [pl.BlockSpec((4, tr, tc), lambda l, i, j, chip_ref, k=k: (0,) + walk(k, l, i, j))
                  for k in range(layers)]
    own_specs = [pl.BlockSpec((None, tr, tc), lambda l, i, j, chip_ref, k=k: (chip_ref[0],) + walk(k, l, i, j))
                 for k in range(layers)]
    return _pcall(
        body, name=name,
        grid_spec=pltpu.PrefetchScalarGridSpec(
            num_scalar_prefetch=1, grid=(layers, nr, nc),
            in_specs=[spec] + land_specs + own_specs + [spec, spec], out_specs=[spec] * 4),
        out_shape=[jax.ShapeDtypeStruct((layers, r, c), F32)] * 4, compiler_params=_params(3),
    )(chip, w, *lands, *sums, m, v)


def _pair_sum(pieces, partner, core, name, tr=512):
    _, r, c = pieces.shape
    tc = c
    if r % 8:
        tr, tc = r, _tile(c, max(LANE, 1024 * 1024 // r // LANE * LANE))
    elif r <= tr:
        tr = r
    while r % tr:
        tr -= 8

    def body(core_ref, mine_ref, partner_ref, out_ref):
        out_ref[...] = (mine_ref[...].astype(F32) + partner_ref[...].astype(F32)).astype(out_ref.dtype)

    return _pcall(
        body, name=name,
        grid_spec=pltpu.PrefetchScalarGridSpec(
            num_scalar_prefetch=1, grid=(4, r // tr, c // tc),
            in_specs=[pl.BlockSpec((None, tr, tc), lambda q, i, j, core_ref: (2 * q + core_ref[0], i, j)),
                      pl.BlockSpec((None, tr, tc), lambda q, i, j, core_ref: (q, i, j))],
            out_specs=pl.BlockSpec((None, tr, tc), lambda q, i, j, core_ref: (q, i, j))),
        out_shape=jax.ShapeDtypeStruct((4, r, c), pieces.dtype), compiler_params=_params(3),
    )(core, pieces, partner)


def _sum8(x, name):
    p = x.shape[2]
    tp = _tile(p, 16 * 1024)

    def body(x_ref, o_ref):
        acc = x_ref[0]
        for i in range(1, N_DEV):
            acc = acc + x_ref[i]
        o_ref[...] = acc

    return _pcall(
        body, name=name, grid=(p // tp,), in_specs=[pl.BlockSpec((N_DEV, 1, tp), lambda i: (0, 0, i))],
        out_specs=pl.BlockSpec((1, tp), lambda i: (0, i)), out_shape=jax.ShapeDtypeStruct((1, p), x.dtype),
        compiler_params=_params(1),
    )(x)


def _exchange(arrays, name, scatter):
    n = len(arrays)
    hbm = pl.BlockSpec(memory_space=pl.ANY)

    def body(*refs):
        ins, outs, token = refs[:n], refs[n:2 * n], refs[2 * n]
        send_sems, recv_sems, local_sems = refs[2 * n + 1:]
        token[...] = jnp.zeros_like(token)
        x, y, c = lax.axis_index("x"), lax.axis_index("y"), lax.axis_index("c")
        me = 4 * x + 2 * y + c
        copies = []
        for a in range(n):
            src_mine = ins[a].at[me] if scatter else ins[a]
            local = pltpu.make_async_copy(src_mine, outs[a].at[me], local_sems.at[a])
            local.start()
            copies.append(local)
            for k in range(1, N_DEV):
                px = 1 - x if k & 4 else x
                py = 1 - y if k & 2 else y
                pc = 1 - c if k & 1 else c
                src = ins[a].at[4 * px + 2 * py + pc] if scatter else ins[a]
                cp = pltpu.make_async_remote_copy(
                    src_ref=src, dst_ref=outs[a].at[me],
                    send_sem=send_sems.at[a * (N_DEV - 1) + k - 1], recv_sem=recv_sems.at[a * (N_DEV - 1) + k - 1],
                    device_id=(px, py, pc), device_id_type=pl.DeviceIdType.MESH)
                cp.start()
                copies.append(cp)
        for cp in copies:
            cp.wait()

    out_shape = [jax.ShapeDtypeStruct(a.shape if scatter else (N_DEV,) + a.shape, a.dtype) for a in arrays]
    res = _pcall(
        body, name=name, in_specs=[hbm] * n, out_specs=[hbm] * n + [pl.BlockSpec(memory_space=pltpu.VMEM)],
        out_shape=out_shape + [jax.ShapeDtypeStruct((8, LANE), F32)],
        scratch_shapes=[pltpu.SemaphoreType.DMA((n * (N_DEV - 1),)), pltpu.SemaphoreType.DMA((n * (N_DEV - 1),)),
                        pltpu.SemaphoreType.DMA((n,))],
        compiler_params=pltpu.CompilerParams(has_side_effects=True),
    )(*arrays)
    return res[:n], res[n][0, 0]


_HBM = pl.BlockSpec(memory_space=pltpu.HBM)
_SEM = pl.BlockSpec(memory_space=pltpu.SEMAPHORE)
_DATAFLOW = pltpu.SideEffectType.DATAFLOW_SIDE_EFFECTING


def _peer(k, x, y, c):
    return (1 - x if k & 4 else x, 1 - y if k & 2 else y, 1 - c if k & 1 else c)


def _pair_plan(x, y, c):
    return [(2 * q + (1 - c), q, (x, y, 1 - c)) for q in range(4)]


def _chip_plan(x, y, c):
    out = []
    for k in _ICI_PEERS:
        px, py, pc = _peer(k, x, y, c)
        out.append((2 * px + py, 2 * x + y, (px, py, pc)))
    return out


def _all_plan(x, y, c):
    return [(0, 4 * x + 2 * y + c, _peer(k, x, y, c)) for k in range(1, N_DEV)]


def _split_start(arrays, plan, name, land_blocks=4):
    n = len(arrays)
    lands = [lax.empty((land_blocks,) + a.shape[1:], a.dtype) for a in arrays]
    n_copies = len(plan(0, 0, 0))

    def body(*refs):
        srcs, dsts = refs[:n], refs[n:2 * n]
        send_sems, recv_sems, token = refs[4 * n:5 * n], refs[5 * n:6 * n], refs[6 * n]
        copies = plan(lax.axis_index("x"), lax.axis_index("y"), lax.axis_index("c"))
        for a in range(n):
            for j, (src_block, dst_block, peer) in enumerate(copies):
                pltpu.make_async_remote_copy(
                    src_ref=srcs[a].at[src_block], dst_ref=dsts[a].at[dst_block],
                    send_sem=send_sems[a].at[j], recv_sem=recv_sems[a].at[j],
                    device_id=peer, device_id_type=pl.DeviceIdType.MESH).start()
        token[...] = jnp.zeros_like(token)

    sems = [pltpu.SemaphoreType.DMA((n_copies,))] * (2 * n)
    res = _pcall(
        body, name=name,
        in_specs=[_HBM] * (2 * n),
        out_specs=[_HBM] * (2 * n) + [_SEM] * (2 * n) + [pl.BlockSpec(memory_space=pltpu.VMEM)],
        out_shape=[pltpu.HBM(a.shape, a.dtype) for a in arrays] + [pltpu.HBM(l.shape, l.dtype) for l in lands]
        + sems + [jax.ShapeDtypeStruct((8, LANE), F32)],
        input_output_aliases={i: i for i in range(2 * n)},
        compiler_params=pltpu.CompilerParams(has_side_effects=_DATAFLOW),
    )(*[pltpu.with_memory_space_constraint(a, pltpu.HBM) for a in arrays],
      *[pltpu.with_memory_space_constraint(l, pltpu.HBM) for l in lands])
    handles = [(res[a], res[n + a], res[2 * n + a], res[3 * n + a]) for a in range(n)]
    return handles, res[4 * n][0, 0]


def _split_wait(handles, plan, after, name):
    n = len(handles)
    after = list(after) if isinstance(after, (list, tuple)) else [after]

    def body(*refs):
        srcs, dsts = refs[:n], refs[n:2 * n]
        send_sems, recv_sems = refs[2 * n:3 * n], refs[3 * n:4 * n]
        copies = plan(lax.axis_index("x"), lax.axis_index("y"), lax.axis_index("c"))
        for a in range(n):
            for j, (src_block, dst_block, peer) in enumerate(copies):
                cp = pltpu.make_async_remote_copy(
                    src_ref=srcs[a].at[src_block], dst_ref=dsts[a].at[dst_block],
                    send_sem=send_sems[a].at[j], recv_sem=recv_sems[a].at[j],
                    device_id=peer, device_id_type=pl.DeviceIdType.MESH)
                cp.wait_send()
                cp.wait_recv()

    srcs, lands = [h[0] for h in handles], [h[1] for h in handles]
    res = _pcall(
        body, name=name,
        in_specs=[_HBM] * (2 * n) + [_SEM] * (2 * n) + [pl.BlockSpec(memory_space=pl.ANY)] * len(after),
        out_specs=[_HBM] * (2 * n),
        out_shape=[pltpu.HBM(t.shape, t.dtype) for t in srcs + lands],
        input_output_aliases={i: i for i in range(2 * n)},
        compiler_params=pltpu.CompilerParams(has_side_effects=_DATAFLOW),
    )(*srcs, *lands, *[h[2] for h in handles], *[h[3] for h in handles], *after)
    return res[:n], res[n:]


_ICI_PEERS = (2, 4, 6)


def _gather2_start(shards, name):
    n = len(shards)
    lands = [lax.empty((N_DEV,) + a.shape, a.dtype) for a in shards]

    def body(*refs):
        srcs, dsts = refs[:n], refs[n:2 * n]
        send_sems, d2d_sems, ici_sems = refs[4 * n:5 * n], refs[5 * n:6 * n], refs[6 * n:7 * n]
        token = refs[7 * n]
        x, y, c = lax.axis_index("x"), lax.axis_index("y"), lax.axis_index("c")
        me = 4 * x + 2 * y + c
        for a in range(n):
            for j, k in enumerate((1,) + _ICI_PEERS):
                recv = d2d_sems[a].at[0] if j == 0 else ici_sems[a].at[j - 1]
                pltpu.make_async_remote_copy(
                    src_ref=srcs[a], dst_ref=dsts[a].at[me], send_sem=send_sems[a].at[j], recv_sem=recv,
                    device_id=_peer(k, x, y, c), device_id_type=pl.DeviceIdType.MESH).start()
        token[...] = jnp.zeros_like(token)

    dma = pltpu.SemaphoreType.DMA
    res = _pcall(
        body, name=name,
        in_specs=[_HBM] * (2 * n),
        out_specs=[_HBM] * (2 * n) + [_SEM] * (3 * n) + [pl.BlockSpec(memory_space=pltpu.VMEM)],
        out_shape=[pltpu.HBM(a.shape, a.dtype) for a in shards] + [pltpu.HBM(l.shape, l.dtype) for l in lands]
        + [dma((4,))] * n + [dma((1,))] * n + [dma((3,))] * n + [jax.ShapeDtypeStruct((8, LANE), F32)],
        input_output_aliases={i: i for i in range(2 * n)},
        compiler_params=pltpu.CompilerParams(has_side_effects=_DATAFLOW),
    )(*[pltpu.with_memory_space_constraint(a, pltpu.HBM) for a in shards],
      *[pltpu.with_memory_space_constraint(l, pltpu.HBM) for l in lands])
    handles = [tuple(res[i * n + a] for i in range(5)) for a in range(n)]
    return handles, res[5 * n][0, 0]


def _gather2_forward(handle, after, name):
    src, land, send_sems, d2d_sem, ici_sems = handle

    def body(land_ref, ici_ref, d2d_ref, after_ref, land_out, fwd_send, fwd_recv, token):
        x, y, c = lax.axis_index("x"), lax.axis_index("y"), lax.axis_index("c")
        sibling = (x, y, 1 - c)
        arrived = [(_peer(k, x, y, c), ici_ref.at[j]) for j, k in enumerate(_ICI_PEERS)] + [(sibling, d2d_ref.at[0])]
        for j, ((px, py, pc), recv) in enumerate(arrived):
            block = land_ref.at[4 * px + 2 * py + pc]
            pltpu.make_async_remote_copy(
                src_ref=block, dst_ref=block, send_sem=fwd_send.at[j], recv_sem=recv,
                device_id=(px, py, pc), device_id_type=pl.DeviceIdType.MESH).wait_recv()
            pltpu.make_async_remote_copy(
                src_ref=block, dst_ref=block, send_sem=fwd_send.at[j], recv_sem=fwd_recv.at[j],
                device_id=sibling, device_id_type=pl.DeviceIdType.MESH).start()
        token[...] = jnp.zeros_like(token)

    dma = pltpu.SemaphoreType.DMA
    land, fwd_send, fwd_recv, token = _pcall(
        body, name=name,
        in_specs=[_HBM, _SEM, _SEM, pl.BlockSpec(memory_space=pl.ANY)],
        out_specs=[_HBM, _SEM, _SEM, pl.BlockSpec(memory_space=pltpu.VMEM)],
        out_shape=[pltpu.HBM(land.shape, land.dtype), dma((4,)), dma((4,)), jax.ShapeDtypeStruct((8, LANE), F32)],
        input_output_aliases={0: 0},
        compiler_params=pltpu.CompilerParams(has_side_effects=_DATAFLOW),
    )(land, ici_sems, d2d_sem, after)
    return (src, land, send_sems, fwd_send, fwd_recv), token[0, 0]


def _gather2_wait(handle, after, name):
    src, land, send_sems, fwd_send, fwd_recv = handle

    def body(src_ref, land_ref, send_ref, fsend_ref, frecv_ref, after_ref, src_out, land_out):
        x, y, c = lax.axis_index("x"), lax.axis_index("y"), lax.axis_index("c")
        block = land_ref.at[4 * x + 2 * y + c]

        def copy(send, recv):
            return pltpu.make_async_remote_copy(src_ref=src_ref, dst_ref=block, send_sem=send, recv_sem=recv,
                                                device_id=(x, y, 1 - c), device_id_type=pl.DeviceIdType.MESH)

        for j in range(4):
            copy(send_ref.at[j], frecv_ref.at[j]).wait_send()
        for j in range(4):
            copy(fsend_ref.at[j], frecv_ref.at[j]).wait_send()
            copy(fsend_ref.at[j], frecv_ref.at[j]).wait_recv()

    res = _pcall(
        body, name=name,
        in_specs=[_HBM, _HBM, _SEM, _SEM, _SEM, pl.BlockSpec(memory_space=pl.ANY)],
        out_specs=[_HBM, _HBM],
        out_shape=[pltpu.HBM(src.shape, src.dtype), pltpu.HBM(land.shape, land.dtype)],
        input_output_aliases={0: 0, 1: 1},
        compiler_params=pltpu.CompilerParams(has_side_effects=_DATAFLOW),
    )(src, land, send_sems, fwd_send, fwd_recv, after)
    return res[0], res[1]


def _pad_cols(x, width=LANE):
    return jnp.pad(x, ((0, 0), (0, width - x.shape[1])))


def _cols_full(g):
    return jnp.transpose(g, (1, 0, 2)).reshape(g.shape[1], -1)


def _ffn_fwd(x1, p, i, tag):
    h2 = _adaln_fwd(x1, p["norm_ffn"][i], p["sc_f"][i], p["sh_f"][i], f"ffn_norm_{tag}")
    u = _matmul(h2, p["fetch"](f"up{i}", h2), name=f"ffn_up_{tag}", tn=1408, b_shards=True)
    a = _conv_act_fwd(u, p["conv_w"][i], p["conv_b"][i], f"ffn_act_{tag}")
    g_f = p["g_f"][i]
    x2, f = _matmul(a, p["fetch"](f"down{i}", a), name=f"ffn_down_{tag}", tk=1408, out_dtypes=(F32, F32),
                    epilogue=lambda acc, x1, g: (x1 + (1.0 + g) * acc, acc), extras=(("mn", x1), ("n", g_f)))
    return x2, dict(h2=h2, u=u, a=a, f=f)


def _ffn_bwd(incoming, x1, saved, p, i, tag, branch):
    d = x1.shape[1]
    dx2, df, dg_f = incoming
    w_up, w_down = p["fetch"](f"up{i}", None), p["fetch"](f"down{i}", None)
    da = _matmul(df, w_down, tb=True, name=f"ffn_down_dx_{tag}", tn=1408)
    dw_down = _matmul(saved["a"], df, ta=True, name=f"ffn_down_dw_{tag}", tm=1408, out_dtypes=(BF16,))
    du, dcw, dcb = _conv_act_bwd(saved["u"], p["conv_w"][i], p["conv_b"][i], da, f"ffn_act_bwd_{tag}")
    dcw, dcb = (jnp.concatenate([t[0], t[1]], axis=1) for t in (dcw, dcb))
    tok = p["flush"](du)
    dh2 = _matmul(du, w_up, tb=True, name=f"ffn_up_dx_{tag}", tn=2048, tk=1408, a_halves=True, b_shards=True)
    dw_up = _matmul(saved["h2"], du, ta=True, name=f"ffn_up_dw_{tag}", tn=1408, out_dtypes=(BF16,), b_halves=True,
                    out_shards=True)
    tok = tok + p["send"](f"ffn{i}", [dw_up, dw_down.reshape(N_DEV, -1, d)])
    dx1, dsh, dsc, dgain, dy, dg_m = _adaln_bwd(x1, dh2, dx2, p["norm_ffn"][i] + tok, p["sc_f"][i],
                                                f"ffn_norm_bwd_{tag}", branch)
    grads = dict(conv_w=dcw, conv_b=dcb, norm_ffn=dgain, sh_f=dsh, sc_f=dsc, g_f=dg_f)
    return (dx1, dy, dg_m), grads


def _gla_layer_fwd(x, p, i):
    h1 = _adaln_fwd(x, p["norm_mix"][i], p["sc_m"][i], p["sh_m"][i], "gla_norm")
    w_t, w_tail_t, main = p["fetch"]("gla_in", h1)
    proj = _matmul(h1, w_t, tb=True, b_rows=main, name="gla_in")
    a_tail = _matmul(h1, w_tail_t, tb=True, name="gla_in_tail")
    dk_total = p["gla_wg_p"].shape[1]
    o, states = _gla_fwd(proj, a_tail, p["gla_wg_p"], p["gla_b_gate"], "gla_chunks")
    assert 2 * dk_total == o.shape[1]
    r = ("cols", proj, 2, o.shape[1])
    og = _gla_post_fwd(o, r, p["gla_norm"], "gla_post")
    x1, y = _matmul(og, p["fetch"]("gla_out", og), name="gla_out", out_dtypes=(F32, F32),
                    epilogue=lambda acc, x, g: (x + (1.0 + g) * acc, acc), extras=(("mn", x), ("n", p["g_m"][i])))
    return x1, dict(h1=h1, proj=proj, a_tail=a_tail, o=o, r=r, states=states, og=og, y=y)


def _gla_layer_bwd(incoming, x, sv, p, i, branch):
    d = x.shape[1]
    dx1, dy, dg_m = incoming
    (w_t, w_tail_t, main), w_out = p["fetch"]("gla_in", None), p["fetch"]("gla_out", None)
    dog = _matmul(dy, w_out, tb=True, name="gla_out_dx")
    dw_out = _matmul(sv["og"], dy, ta=True, name="gla_out_dw", out_dtypes=(BF16,))
    tok = p["flush"](dog) + p["send"]("gla_out", [dw_out.reshape(N_DEV, -1, d)])
    d_o, dproj, dgn = _gla_post_bwd(sv["o"], sv["r"], p["gla_norm"] + tok, dog, "gla_post_bwd")
    dproj, dga = _gla_bwd(sv["proj"], sv["a_tail"], p["gla_wg_p"], p["gla_b_gate"], sv["states"], d_o, dproj,
                          "gla_chunks_bwd")
    tok = p["flush"](dga)
    da_tail = _matmul(dga, p["gla_wg_p"], tb=True, name="gla_gate_dx", out_dtypes=(BF16,))
    dwg = _matmul(sv["a_tail"], dga, ta=True, name="gla_gate_dw")
    dbg = _rowwise(lambda t: (_colsum(t),), [("row", dga)], [("acc", dga.shape[1], F32)], name="gla_gate_db")[0]
    dh_tail = _matmul(da_tail, w_tail_t, name="gla_in_tail_dx")
    dh1 = _matmul(dproj, w_t, b_rows=main, name="gla_in_dx", tk=2048,
                  epilogue=lambda acc, t: (acc + t,), extras=(("mn", dh_tail),))
    rank = p["gla_rank"]
    dw_main = _matmul(dproj, sv["h1"], ta=True, name="gla_in_dw", out_dtypes=(BF16,), out_rows=main + rank)
    dx, dsh, dsc, dgain, *into_branch = _adaln_bwd(x, dh1, dx1, p["norm_mix"][i] + tok, p["sc_m"][i], "gla_norm_bwd",
                                                   branch)
    grads = dict(gla_w_gate=dwg[:rank], gla_b_gate=dbg, gla_norm=dgn, norm_mix=dgain, sh_m=dsh, sc_m=dsc, g_m=dg_m,
                 gla_w_in_unsent=(dw_main, da_tail, sv["h1"]))
    return (dx, *into_branch), grads


def _fox_layer_fwd(x, p, i):
    d = x.shape[1]
    hd = p["fox_q_norm"].shape[1]
    heads = d // hd
    s = x.shape[0]
    t = _tile(s, 1024)
    h1 = _adaln_fwd(x, p["norm_mix"][i], p["sc_m"][i], p["sh_m"][i], "fox_norm")
    w_t, w_tail_t, main = p["fetch"]("fox_in", h1)
    proj = _matmul(h1, w_t, tb=True, b_rows=main, name="fox_in")
    fl = _matmul(h1, w_tail_t, tb=True, name="fox_in_tail")
    q, k, v, og = (("cols", proj, j, d) for j in range(4))
    qn, kn, vb = _fox_prep(q, k, v, p["fox_q_norm"], p["fox_k_norm"], d, hd, "fox_prep")
    cum = _fox_cum(fl, p["fox_bf_p"], "fox_cum")
    cum_t = jnp.transpose(cum[:, :heads])
    cum_col, cum_row = cum_t[:, :, None], cum_t.reshape(heads, s // t, 1, t)
    o, lse = _fox_attn_fwd(qn, kn, vb, cum_col, cum_row, hd, t, "fox_attn")
    act = _fox_gate_fwd(o, og, "fox_gate")
    x1, y = _matmul(act, p["fetch"]("fox_out", act), name="fox_out", out_dtypes=(F32, F32),
                    epilogue=lambda acc, x, g: (x + (1.0 + g) * acc, acc), extras=(("mn", x), ("n", p["g_m"][i])))
    return x1, dict(h1=h1, q=q, k=k, og=og, fl=fl, qn=qn, kn=kn, vb=vb, cum_col=cum_col, cum_row=cum_row,
                    o=o, lse=lse, act=act, y=y, t=t, hd=hd)


def _fox_layer_bwd(incoming, x, sv, p, i, branch):
    d = x.shape[1]
    hd, t = sv["hd"], sv["t"]
    heads = d // hd
    s = x.shape[0]
    dx1, dy, dg_m = incoming
    (w_t, w_tail_t, main), w_out = p["fetch"]("fox_in", None), p["fetch"]("fox_out", None)
    dact = _matmul(dy, w_out, tb=True, name="fox_out_dx")
    dw_out = _matmul(sv["act"], dy, ta=True, name="fox_out_dw", out_dtypes=(BF16,))
    d_o, dproj = _fox_gate_bwd(sv["o"], sv["og"], dact, "fox_gate_bwd")
    tok_flush = p["flush"](d_o)
    dqn, dkn, dproj, dcq, dck = _fox_attn_bwd(sv["qn"], sv["kn"], sv["vb"], d_o, sv["o"], sv["lse"], sv["cum_col"],
                                              sv["cum_row"], hd, t, dproj, "fox_attn_bwd")
    dproj, gq, gk = _fox_prep_bwd(sv["q"], sv["k"], dqn, dkn, p["fox_q_norm"], p["fox_k_norm"], hd, dproj,
                                  "fox_prep_bwd")
    dcum = _pad_cols(jnp.transpose(dcq[:, :, 0] - dck.reshape(heads, s)))
    dfl, dbf = _fox_cum_bwd(dcum, sv["fl"], p["fox_bf_p"], "fox_cum_bwd")
    dfl_b = dfl.astype(BF16)
    dh_tail = _matmul(dfl_b, w_tail_t, name="fox_in_tail_dx")
    dh1 = _matmul(dproj, w_t, b_rows=main, name="fox_in_dx", tk=2048,
                  epilogue=lambda acc, tl: (acc + tl,), extras=(("mn", dh_tail),))
    dw_main = _matmul(dproj, sv["h1"], ta=True, name="fox_in_dw", out_dtypes=(BF16,), out_rows=main + heads)
    dw_in = _tail_rows(dfl_b, sv["h1"], dw_main, heads, "fox_in_tail_dw").reshape(N_DEV, -1, d)
    tok = tok_flush + p["send"]("fox", [dw_in, dw_out.reshape(N_DEV, -1, d)])
    dx, dsh, dsc, dgain, *into_branch = _adaln_bwd(x, dh1, dx1, p["norm_mix"][i] + tok, p["sc_m"][i], "fox_norm_bwd",
                                                   branch)
    grads = dict(fox_b_f=dbf[:, :heads], fox_q_norm=gq.reshape(heads, hd).sum(0, keepdims=True),
                 fox_k_norm=gk.reshape(heads, hd).sum(0, keepdims=True), norm_mix=dgain, sh_m=dsh, sc_m=dsc, g_m=dg_m)
    return (dx, *into_branch), grads


SMALL = ("b_mod", "norm_mix", "norm_ffn", "gla_b_gate", "gla_norm", "fox_b_f", "fox_q_norm", "fox_k_norm",
         "ffn_conv_b", "norm_final")
SMALL_SHARDED = ("gla_w_gate", "ffn_conv_w")
BIG = ("gla_w_in", "gla_w_out", "fox_w_in", "fox_w_out", "ffn_w_up", "ffn_w_down")
WEIGHTS = ("w_mod", "b_mod", "norm_mix", "norm_ffn", "gla_w_in", "gla_w_gate", "gla_b_gate", "gla_norm", "gla_w_out",
           "fox_w_in", "fox_b_f", "fox_q_norm", "fox_k_norm", "fox_w_out", "ffn_w_up", "ffn_conv_w", "ffn_conv_b",
           "ffn_w_down", "norm_final")


def _pack(parts):
    flat = jnp.concatenate([p.reshape(-1) for p in parts])
    pad = (-flat.shape[0]) % 1024
    return jnp.pad(flat, (0, pad)).reshape(1, -1)


def _unpack(flat, shapes):
    out, off = [], 0
    for shp in shapes:
        n = 1
        for s in shp:
            n *= s
        out.append(flat[0, off:off + n].reshape(shp))
        off += n
    return out


def kernel(x, c, w_mod, b_mod, norm_mix, norm_ffn, gla_w_in, gla_w_gate, gla_b_gate, gla_norm, gla_w_out, fox_w_in, fox_b_f, fox_q_norm, fox_k_norm, fox_w_out, ffn_w_up, ffn_conv_w, ffn_conv_b, ffn_w_down, norm_final, loss_target, m_w_mod, m_b_mod, m_norm_mix, m_norm_ffn, m_gla_w_in, m_gla_w_gate, m_gla_b_gate, m_gla_norm, m_gla_w_out, m_fox_w_in, m_fox_b_f, m_fox_q_norm, m_fox_k_norm, m_fox_w_out, m_ffn_w_up, m_ffn_conv_w, m_ffn_conv_b, m_ffn_w_down, m_norm_final, v_w_mod, v_b_mod, v_norm_mix, v_norm_ffn, v_gla_w_in, v_gla_w_gate, v_gla_b_gate, v_gla_norm, v_gla_w_out, v_fox_w_in, v_fox_b_f, v_fox_q_norm, v_fox_k_norm, v_fox_w_out, v_ffn_w_up, v_ffn_conv_w, v_ffn_conv_b, v_ffn_w_down, v_norm_final):
    w = dict(w_mod=w_mod, b_mod=b_mod, norm_mix=norm_mix, norm_ffn=norm_ffn, gla_w_in=gla_w_in, gla_w_gate=gla_w_gate,
             gla_b_gate=gla_b_gate, gla_norm=gla_norm, gla_w_out=gla_w_out, fox_w_in=fox_w_in, fox_b_f=fox_b_f,
             fox_q_norm=fox_q_norm, fox_k_norm=fox_k_norm, fox_w_out=fox_w_out, ffn_w_up=ffn_w_up,
             ffn_conv_w=ffn_conv_w, ffn_conv_b=ffn_conv_b, ffn_w_down=ffn_w_down, norm_final=norm_final)
    mom_m = dict(w_mod=m_w_mod, b_mod=m_b_mod, norm_mix=m_norm_mix, norm_ffn=m_norm_ffn, gla_w_in=m_gla_w_in,
                 gla_w_gate=m_gla_w_gate, gla_b_gate=m_gla_b_gate, gla_norm=m_gla_norm, gla_w_out=m_gla_w_out,
                 fox_w_in=m_fox_w_in, fox_b_f=m_fox_b_f, fox_q_norm=m_fox_q_norm, fox_k_norm=m_fox_k_norm,
                 fox_w_out=m_fox_w_out, ffn_w_up=m_ffn_w_up, ffn_conv_w=m_ffn_conv_w, ffn_conv_b=m_ffn_conv_b,
                 ffn_w_down=m_ffn_w_down, norm_final=m_norm_final)
    mom_v = dict(w_mod=v_w_mod, b_mod=v_b_mod, norm_mix=v_norm_mix, norm_ffn=v_norm_ffn, gla_w_in=v_gla_w_in,
                 gla_w_gate=v_gla_w_gate, gla_b_gate=v_gla_b_gate, gla_norm=v_gla_norm, gla_w_out=v_gla_w_out,
                 fox_w_in=v_fox_w_in, fox_b_f=v_fox_b_f, fox_q_norm=v_fox_q_norm, fox_k_norm=v_fox_k_norm,
                 fox_w_out=v_fox_w_out, ffn_w_up=v_ffn_w_up, ffn_conv_w=v_ffn_conv_w, ffn_conv_b=v_ffn_conv_b,
                 ffn_w_down=v_ffn_w_down, norm_final=v_norm_final)

    me = 4 * lax.axis_index("x") + 2 * lax.axis_index("y") + lax.axis_index("c")
    xs, target = x[0], loss_target[0]
    s, d = xs.shape
    depth = w_mod.shape[0]
    mod_cols = w_mod.shape[2]
    rank = gla_w_gate.shape[1]
    hd = fox_q_norm.shape[1]
    fox_heads = d // hd
    dk_total = gla_w_gate.shape[2] * N_DEV

    cond = c * (1.0 / (1.0 + jnp.exp(-c)))
    g, _ = _exchange([gla_w_gate[0], ffn_conv_w, cond], "gather_small", scatter=False)
    cond_all = g[2][:, 0, :]

    cond_pad = jnp.pad(cond_all, ((0, 16 - N_DEV), (0, 0)))
    mod_part = []
    for i in range(depth):
        b_cols = lax.dynamic_slice(b_mod[i:i + 1], (0, me * mod_cols), (1, mod_cols))
        mod_part.append(_matmul(cond_pad, w_mod, b_layer=i, name=f"mod_{i}", tn=768,
                                epilogue=lambda acc, b: (acc + b,), extras=(("n", b_cols),))[:N_DEV])
    (mod_all,), tok_mod = _exchange([jnp.stack(mod_part)], "gather_mod", scatter=False)
    mod = lax.dynamic_index_in_dim(mod_all, me, axis=2, keepdims=False)
    mod = jnp.transpose(mod, (1, 0, 2)).reshape(depth, 6, 1, d)

    big_names = ["gla_in", "gla_out", "up0", "down0", "fox_in", "fox_out", "up1", "down1"]
    first = [jnp.transpose(gla_w_in[0] + tok_mod).astype(BF16), gla_w_out[0].astype(BF16)]
    handles, tok_first = _gather2_start(first, "gather_weights_start_first")
    rest = [ffn_w_up[0] + tok_first, ffn_w_down[0], jnp.transpose(fox_w_in[0]), fox_w_out[0], ffn_w_up[1],
            ffn_w_down[1]]
    handles_rest, tok0 = _gather2_start([t.astype(BF16) for t in rest], "gather_weights_start_rest")
    handles = handles + handles_rest
    ready, forwarded = {}, {}

    def split_tail(full_t, tail):
        main = full_t.shape[0] - tail
        return full_t, jnp.pad(full_t[main:], ((0, LANE - tail), (0, 0))), main

    def forward(idx, after):
        key = big_names[idx]
        forwarded[key] = _gather2_forward(handles[idx], after, f"gather_{key}_forward")

    def fetch(key, after):
        if key not in ready:
            idx = big_names.index(key)
            if idx == 0:
                forward(0, after)
            handle, _ = forwarded[key]
            _, full = _gather2_wait(handle, after, f"gather_{key}_wait")
            if idx + 1 < len(big_names):
                forward(idx + 1, full)
            if key == "gla_in":
                ready[key] = split_tail(full.reshape(-1, d), rank)
            elif key == "fox_in":
                ready[key] = split_tail(full.reshape(-1, d), fox_heads)
            elif key.startswith("up"):
                ready[key] = full
            else:
                ready[key] = full.reshape(-1, d)
        return ready[key]

    pending, sent = [], {}
    core = lax.axis_index("c").astype(jnp.int32).reshape(1)
    chip = 2 * lax.axis_index("x") + lax.axis_index("y")

    def send(key, pieces):
        hs, tok = _split_start(pieces, _pair_plan, f"scatter_{key}_pair_start")
        pending.append((key, hs))
        return tok

    def flush(after):
        tok = 0.0
        while pending:
            key, hs = pending.pop(0)
            mine, partner = _split_wait(hs, _pair_plan, after, f"scatter_{key}_pair_wait")
            sums = [_pair_sum(pc, pt, core, f"scatter_{key}_pair_sum{a}")
                    for a, (pc, pt) in enumerate(zip(mine, partner))]
            sent[key], t = _split_start(sums, _chip_plan, f"scatter_{key}_chip_start")
            tok = tok + t
        return tok

    p = dict(
        fetch=fetch, send=send, flush=flush,
        gla_wg_p=jnp.pad(_cols_full(g[0]), ((0, LANE - rank), (0, 0))),
        conv_w=[jnp.transpose(g[1][:, i], (1, 0, 2)).reshape(ffn_conv_w.shape[1], -1) for i in range(depth)],
        conv_b=[ffn_conv_b[i:i + 1] for i in range(depth)],
        gla_b_gate=gla_b_gate, gla_norm=gla_norm, fox_q_norm=fox_q_norm, fox_k_norm=fox_k_norm,
        fox_bf_p=_pad_cols(fox_b_f), gla_rank=rank,
        norm_mix=[norm_mix[i:i + 1] + (tok0 if i == 0 else 0.0) for i in range(depth)],
        norm_ffn=[norm_ffn[i:i + 1] for i in range(depth)],
    )

    for j, nm in enumerate(("sh_m", "sc_m", "g_m", "sh_f", "sc_f", "g_f")):
        p[nm] = [mod[i, j] for i in range(depth)]

    acts, saved = [xs], []
    for i in range(depth):
        layer_fwd = _gla_layer_fwd if i % 2 == 0 else _fox_layer_fwd
        x1, sv_mix = layer_fwd(acts[-1], p, i)
        x2, sv_ffn = _ffn_fwd(x1, p, i, str(i))
        saved.append((acts[-1], x1, sv_mix, sv_ffn))
        acts.append(x2)
    last_ffn = (saved[-1][3]["f"], p["g_f"][depth - 1])
    dx, d_norm_final, loss_part, *into_branch = _final_loss(acts[-1], target, norm_final.reshape(1, d), "final_loss",
                                                            last_ffn)
    incoming = (dx, *into_branch)

    lg = [None] * depth
    for i in reversed(range(depth)):
        x_in, x1, sv_mix, sv_ffn = saved[i]
        incoming, g_ffn = _ffn_bwd(incoming, x1, sv_ffn, p, i, str(i), (sv_mix["y"], p["g_m"][i]))
        layer_bwd = _gla_layer_bwd if i % 2 == 0 else _fox_layer_bwd
        before = (saved[i - 1][3]["f"], p["g_f"][i - 1]) if i else None
        incoming, g_mix = layer_bwd(incoming, x_in, sv_mix, p, i, before)
        lg[i] = {**g_ffn, **g_mix}
    grad_x = incoming[0][None]

    gla_l = [i for i in range(depth) if i % 2 == 0]
    fox_l = [i for i in range(depth) if i % 2 == 1]
    small_parts = dict(
        norm_mix=jnp.concatenate([lg[i]["norm_mix"] for i in range(depth)]),
        norm_ffn=jnp.concatenate([lg[i]["norm_ffn"] for i in range(depth)]),
        gla_b_gate=jnp.concatenate([lg[i]["gla_b_gate"] for i in gla_l]),
        gla_norm=jnp.concatenate([lg[i]["gla_norm"] for i in gla_l]),
        fox_b_f=jnp.concatenate([lg[i]["fox_b_f"] for i in fox_l]),
        fox_q_norm=jnp.concatenate([lg[i]["fox_q_norm"] for i in fox_l]),
        fox_k_norm=jnp.concatenate([lg[i]["fox_k_norm"] for i in fox_l]),
        ffn_conv_b=jnp.concatenate([lg[i]["conv_b"] for i in range(depth)]),
        norm_final=d_norm_final,
        gla_w_gate=jnp.stack([lg[i]["gla_w_gate"] for i in gla_l]),
        ffn_conv_w=jnp.stack([lg[i]["conv_w"] for i in range(depth)]),
        loss=loss_part[:, :1],
    )
    order = ("norm_mix", "norm_ffn", "gla_b_gate", "gla_norm", "fox_b_f", "fox_q_norm", "fox_k_norm", "ffn_conv_b",
             "norm_final", "gla_w_gate", "ffn_conv_w", "loss")
    packed = _pack([small_parts[nm] for nm in order])
    dmod = jnp.stack([jnp.concatenate([lg[i][nm] for nm in ("sh_m", "sc_m", "g_m", "sh_f", "sc_f", "g_f")], axis=1)
                      for i in range(depth)])
    hs_small, tok_small = _split_start([packed[None], dmod[None]], _all_plan, "gather_small_grads_start",
                                       land_blocks=N_DEV)
    dw_main, da_tail, h1_gla = lg[0]["gla_w_in_unsent"]
    dw_in_t = _tail_rows(da_tail + tok_small.astype(BF16), h1_gla, dw_main, rank, "gla_in_tail_dw")
    send("gla_in", [dw_in_t.reshape(N_DEV, -1, d)])
    started = pending[-1][1][0][0]

    received = {}

    def arrive(key, after):
        sums, lands = _split_wait(sent[key], _chip_plan, after, f"scatter_{key}_chip_wait")
        received[key] = list(zip(lands, sums))

    for key in ("ffn1", "fox", "ffn0", "gla_out"):
        arrive(key, started)

    out_g, out_d, out_m, out_v = {}, {}, {}, {}

    chip_idx = chip.astype(jnp.int32).reshape(1)

    def update(nm, g_arr, transposed=False):
        swap = (lambda t: jnp.transpose(t, (0, 2, 1))) if transposed else (lambda t: t)
        if isinstance(g_arr, list):
            res = _adamw_pieces(swap(w[nm]), [t[0] for t in g_arr], [t[1] for t in g_arr], chip_idx,
                                swap(mom_m[nm]), swap(mom_v[nm]), f"adamw_{nm}")
        else:
            res = _adamw(w[nm], g_arr, mom_m[nm], mom_v[nm], f"adamw_{nm}")
        out_g[nm], out_d[nm], out_m[nm], out_v[nm] = (swap(t) for t in res)

    update("gla_w_out", [received["gla_out"][0]])
    update("fox_w_out", [received["fox"][1]])
    tok_flush = flush(out_g["fox_w_out"])
    update("ffn_w_up", [received[f"ffn{i}"][0] for i in range(depth)])
    update("fox_w_in", [received["fox"][0]], transposed=True)
    update("ffn_w_down", [received[f"ffn{i}"][1] for i in range(depth)])

    updated = ("gla_w_out", "fox_w_in", "fox_w_out", "ffn_w_up", "ffn_w_down")
    (packed_mine, dmod_mine), (packed_all, dmod_all) = _split_wait(
        hs_small, _all_plan, [out_d[nm] for nm in updated], "gather_small_grads_wait")
    packed_all = lax.dynamic_update_slice(packed_all, packed_mine + tok_flush, (me, 0, 0))
    dmod_all = lax.dynamic_update_slice(dmod_all, dmod_mine, (me, 0, 0, 0))
    summed = _unpack(_sum8(packed_all, "sum_small_grads"), [small_parts[nm].shape for nm in order])
    small_g = dict(zip(order, summed))
    loss = small_g["loss"][0, 0]
    dmod_all = dmod_all[:, :, 0, :]
    grads = {}
    cond_t = _pad_cols(jnp.transpose(cond_all)).astype(BF16)
    dmod_cols = lax.dynamic_slice(dmod_all, (0, 0, me * mod_cols), (N_DEV, depth, mod_cols))
    g_w_mod = lax.empty(w_mod.shape, F32)
    for i in range(depth):
        rhs = jnp.pad(dmod_cols[:, i], ((0, LANE - N_DEV), (0, 0)))
        g_w_mod = _matmul(cond_t, rhs, name=f"mod_dw_{i}", tn=768, into=(g_w_mod, i))
    grads["w_mod"] = g_w_mod
    small_g["b_mod"] = _sum8(dmod_all.reshape(N_DEV, 1, -1), "sum_b_mod").reshape(depth, -1)
    update("w_mod", grads["w_mod"])

    gate_cols = gla_w_gate.shape[2]
    conv_cols = ffn_conv_w.shape[2]
    local_small = dict(small_g)
    local_small["gla_w_gate"] = lax.dynamic_slice_in_dim(small_g["gla_w_gate"], me * gate_cols, gate_cols, axis=2)
    local_small["ffn_conv_w"] = lax.dynamic_slice_in_dim(small_g["ffn_conv_w"], me * conv_cols, conv_cols, axis=2)
    names = SMALL + SMALL_SHARDED
    shapes = [w[nm].shape for nm in names]
    res = _adamw(_pack([w[nm] for nm in names])[None], _pack([local_small[nm] for nm in names])[None],
                 _pack([mom_m[nm] for nm in names])[None], _pack([mom_v[nm] for nm in names])[None], "adamw_small")
    for tgt, flat in zip((out_g, out_d, out_m, out_v), res):
        for nm, arr in zip(names, _unpack(flat[0], shapes)):
            tgt[nm] = arr

    arrive("gla_in", [out_d[nm] for nm in updated + ("w_mod",)])
    update("gla_w_in", [received["gla_in"][0]], transposed=True)

    return (loss, grad_x, *[out_g[n] for n in WEIGHTS], *[out_d[n] for n in WEIGHTS],
            *[out_m[n] for n in WEIGHTS], *[out_v[n] for n in WEIGHTS])
```

```python
import jax
import jax.numpy as jnp
from jax import lax
from jax.experimental import pallas as pl
from jax.experimental.pallas import tpu as pltpu

F32, BF16 = jnp.float32, jnp.bfloat16
N_DEV = 8
GLA_HEADS = 4
GLA_TAU = 16.0
GLA_CHUNK = 64
NORM_EPS = 1e-6
ADAM_LR, ADAM_B1, ADAM_B2, ADAM_EPS, ADAM_WD, ADAM_STEP = 0.001, 0.9, 0.999, 1e-08, 0.01, 10
LANE = 128
VMEM_LIMIT = 56 * 1024 * 1024
NEG = -1e30


def _pcall(body, **kw):
    return pl.pallas_call(body, **kw)


def _params(n_axes):
    return pltpu.CompilerParams(dimension_semantics=("arbitrary",) * n_axes, vmem_limit_bytes=VMEM_LIMIT)


def _tile(dim, pref):
    if dim <= pref:
        return dim
    t = pref
    while dim % t:
        t -= LANE
    assert t > 0, (dim, pref)
    return t


def _dot(a, b, ta=False, tb=False):
    dims = (((0,) if ta else (1,), (1,) if tb else (0,)), ((), ()))
    return lax.dot_general(a.astype(BF16), b.astype(BF16), dims, preferred_element_type=F32)


def _split3(x):
    hi = x.astype(BF16)
    r1 = x - hi.astype(F32)
    mid = r1.astype(BF16)
    lo = (r1 - mid.astype(F32)).astype(BF16)
    return hi, mid, lo


def _tri_matmul(tri, x):
    hi, mid, lo = _split3(x)
    return _dot(tri, hi) + _dot(tri, mid) + _dot(tri, lo)


def _tri(n, upper=False):
    r = lax.broadcasted_iota(jnp.int32, (n, n), 0)
    c = lax.broadcasted_iota(jnp.int32, (n, n), 1)
    return jnp.where((r <= c) if upper else (r >= c), 1.0, 0.0).astype(BF16)


def _log_sigmoid(x):
    return jnp.minimum(x, 0.0) - jnp.log(1.0 + jnp.exp(-jnp.abs(x)))


def _sigmoid(x):
    return 1.0 / (1.0 + jnp.exp(-x))


def _silu(x):
    return x * _sigmoid(x)


def _dsilu(x):
    s = _sigmoid(x)
    return s * (1.0 + x * (1.0 - s))


def _matmul(a, b, *, name, ta=False, tb=False, out_dtypes=(F32,), tm=1024, tn=1024, tk=2048,
            epilogue=None, extras=(), a_halves=False, b_halves=False, b_shards=False, out_shards=False,
            b_rows=None, out_rows=None, b_layer=None, into=None):
    if a_halves:
        assert not ta
        m, k = a.shape[1], 2 * a.shape[2]
    else:
        m, k = (a.shape[1], a.shape[0]) if ta else a.shape
    if b_halves:
        assert not tb and b.shape[1] == k
        n = 2 * b.shape[2]
    elif b_shards:
        n = b.shape[1] if tb else N_DEV * b.shape[2]
        assert (N_DEV * b.shape[2] if tb else b.shape[1]) == k, (a.shape, b.shape, ta, tb)
    elif b_layer is not None:
        assert not tb and b.shape[1] == k
        n = b.shape[2]
    else:
        rows = b.shape[0] if b_rows is None else b_rows
        n = rows if tb else b.shape[1]
        assert (b.shape[1] if tb else rows) == k, (a.shape, b.shape, ta, tb)
    n_unit = n // N_DEV if (out_shards or (b_shards and not tb)) else (n // 2 if b_halves else n)
    k_unit = k // N_DEV if (b_shards and tb) else (k // 2 if a_halves else k)
    tm, tn, tk = _tile(m, tm), _tile(n_unit, tn), _tile(k_unit, tk)
    nk = k // tk
    if a_halves:
        a_spec = pl.BlockSpec((None, tm, tk), lambda i, j, kk: (kk // (nk // 2), i, kk % (nk // 2)))
    elif ta:
        a_spec = pl.BlockSpec((tk, tm), lambda i, j, kk: (kk, i))
    else:
        a_spec = pl.BlockSpec((tm, tk), lambda i, j, kk: (i, kk))
    n_per, k_per = n // tn // N_DEV, nk // N_DEV
    if b_halves:
        b_spec = pl.BlockSpec((None, tk, tn), lambda i, j, kk: (j // (n // tn // 2), kk, j % (n // tn // 2)))
    elif b_shards and tb:
        b_spec = pl.BlockSpec((None, tn, tk), lambda i, j, kk: (kk // k_per, j, kk % k_per))
    elif b_shards:
        b_spec = pl.BlockSpec((None, tk, tn), lambda i, j, kk: (j // n_per, kk, j % n_per))
    elif b_layer is not None:
        b_spec = pl.BlockSpec((None, tk, tn), lambda i, j, kk: (b_layer, kk, j))
    elif tb:
        b_spec = pl.BlockSpec((tn, tk), lambda i, j, kk: (j, kk))
    else:
        b_spec = pl.BlockSpec((tk, tn), lambda i, j, kk: (kk, j))
    ex_specs = []
    for kind, arr in extras:
        if kind == "mn":
            assert arr.shape == (m, n), (arr.shape, m, n)
            ex_specs.append(pl.BlockSpec((tm, tn), lambda i, j, kk: (i, j)))
        else:
            assert arr.shape == (1, n), (arr.shape, n)
            ex_specs.append(pl.BlockSpec((1, tn), lambda i, j, kk: (0, j)))
    n_ex, n_out = len(extras), len(out_dtypes)

    def body(a_ref, b_ref, *rest):
        ex, outs, acc = rest[:n_ex], rest[-1 - n_out:-1], rest[-1]
        kk = pl.program_id(2)

        @pl.when(kk == 0)
        def _():
            acc[...] = jnp.zeros_like(acc)

        acc[...] += _dot(a_ref[...], b_ref[...], ta, tb)

        @pl.when(kk == nk - 1)
        def _():
            if epilogue is None:
                vals = (acc[...],)
            else:
                vals = epilogue(acc[...], *[e[...] for e in ex])
            for o, v in zip(outs, vals):
                o[...] = v.astype(o.dtype)

    if out_shards:
        out_spec = pl.BlockSpec((None, tm, tn), lambda i, j, kk: (j // n_per, i, j % n_per))
        out_dims = (N_DEV, m, n // N_DEV)
    elif into is not None:
        out_spec = pl.BlockSpec((None, tm, tn), lambda i, j, kk: (into[1], i, j))
        out_dims = into[0].shape
    else:
        out_spec = pl.BlockSpec((tm, tn), lambda i, j, kk: (i, j))
        out_dims = (m if out_rows is None else out_rows, n)
    operands = [a, b, *[arr for _, arr in extras]]
    aliases = {}
    if into is not None:
        assert n_out == 1 and into[0].shape[1:] == (m, n) and into[0].dtype == out_dtypes[0]
        aliases = {len(operands): 0}
        operands.append(into[0])
    res = _pcall(
        body, name=name, grid=(m // tm, n // tn, nk),
        in_specs=[a_spec, b_spec] + ex_specs + [pl.BlockSpec(memory_space=pl.ANY)] * len(aliases),
        out_specs=[out_spec] * n_out,
        out_shape=[jax.ShapeDtypeStruct(out_dims, d) for d in out_dtypes],
        scratch_shapes=[pltpu.VMEM((tm, tn), F32)],
        input_output_aliases=aliases,
        compiler_params=_params(3),
    )(*operands)
    return res[0] if n_out == 1 else res


def _tail_rows(a, b, into, rows, name, tn=1024):
    k, n = b.shape
    m_total = into.shape[0]
    tn = _tile(n, tn)

    def body(a_ref, b_ref, into_ref, out_ref):
        out_ref[...] = _dot(a_ref[...], b_ref[...], ta=True)[:rows].astype(out_ref.dtype)

    return _pcall(
        body, name=name, grid=(n // tn,),
        in_specs=[pl.BlockSpec((k, a.shape[1]), lambda j: (0, 0)), pl.BlockSpec((k, tn), lambda j: (0, j)),
                  pl.BlockSpec(memory_space=pl.ANY)],
        out_specs=pl.BlockSpec((rows, tn), lambda j: (m_total // rows - 1, j)),
        out_shape=jax.ShapeDtypeStruct(into.shape, into.dtype),
        input_output_aliases={2: 0}, compiler_params=_params(1),
    )(a, b, into)


def _rowwise(fn, ins, outs, *, name, tr=256, into=None):
    rows = next(e[1].shape[0] for e in ins if e[0] != "full")
    tr = _tile(rows, tr)
    in_specs = []
    for entry in ins:
        kind, arr = entry[0], entry[1]
        assert kind == "full" or (arr.shape[0] == rows and arr.ndim == 2)
        if kind == "row":
            in_specs.append(pl.BlockSpec((tr, arr.shape[1]), lambda i: (i, 0)))
        elif kind == "cols":
            in_specs.append(pl.BlockSpec((tr, entry[3]), lambda i, cb=entry[2]: (i, cb)))
        else:
            in_specs.append(pl.BlockSpec(arr.shape, lambda i, nd=arr.ndim: (0,) * nd))
    out_specs, out_shape = [], []
    for entry in outs:
        kind, w, dt = entry[:3]
        if kind == "row":
            out_specs.append(pl.BlockSpec((tr, w), lambda i: (i, 0)))
            out_shape.append(jax.ShapeDtypeStruct((rows, w), dt))
        elif kind == "band":
            out_specs.append(pl.BlockSpec((tr, w), lambda i, cb=entry[3]: (i, cb)))
            out_shape.append(jax.ShapeDtypeStruct((rows, entry[4]), dt))
        else:
            out_specs.append(pl.BlockSpec((1, w), lambda i: (0, 0)))
            out_shape.append(jax.ShapeDtypeStruct((1, w), dt))
    n_in = len(ins)
    operands = [e[1] for e in ins]
    aliases = {}
    if into is not None:
        aliases = {len(operands): into[1]}
        in_specs.append(pl.BlockSpec(memory_space=pl.ANY))
        operands.append(into[0])

    def body(*refs):
        i = pl.program_id(0)
        vals = fn(*[r[...] for r in refs[:n_in]])
        for entry, o, v in zip(outs, refs[len(operands):], vals):
            if entry[0] == "acc":
                @pl.when(i == 0)
                def _(o=o):
                    o[...] = jnp.zeros_like(o)

                o[...] += v.astype(o.dtype)
            else:
                o[...] = v.astype(o.dtype)

    return _pcall(body, name=name, grid=(rows // tr,), in_specs=in_specs, out_specs=out_specs,
                  out_shape=out_shape, input_output_aliases=aliases, compiler_params=_params(1))(*operands)


def _colsum(x):
    return jnp.sum(x, axis=0, keepdims=True)


def _norm_stats(x):
    rstd = lax.rsqrt(jnp.mean(x * x, axis=-1, keepdims=True) + NORM_EPS)
    return x * rstd, rstd


def _norm_bwd(dxhat, xhat, rstd):
    return rstd * (dxhat - xhat * jnp.mean(dxhat * xhat, axis=-1, keepdims=True))


def _adaln_fwd(x, gain, sc, sh, name):
    def fn(x, gain, sc, sh):
        xhat, _ = _norm_stats(x)
        return ((xhat * gain) * (1.0 + sc) + sh,)

    return _rowwise(fn, [("row", x), ("full", gain), ("full", sc), ("full", sh)],
                    [("row", x.shape[1], BF16)], name=name)[0]


def _adaln_bwd(x, dh, dres, gain, sc, name, branch=None):
    d = x.shape[1]

    def fn(x, dh, dres, gain, sc, *br):
        xhat, rstd = _norm_stats(x)
        dxhat = dh * (gain * (1.0 + sc))
        dx = dres + _norm_bwd(dxhat, xhat, rstd)
        return (dx, _colsum(dh), _colsum(dh * (xhat * gain)), _colsum(dh * xhat * (1.0 + sc))) + _branch_bwd(dx, *br)

    return _rowwise(fn, [("row", x), ("row", dh), ("row", dres), ("full", gain), ("full", sc)] + _branch_ins(branch),
                    [("row", d, F32), ("acc", d, F32), ("acc", d, F32), ("acc", d, F32)] + _branch_outs(branch, d),
                    name=name)


def _branch_ins(branch):
    return [] if branch is None else [("row", branch[0]), ("full", branch[1])]


def _branch_outs(branch, d):
    return [] if branch is None else [("row", d, BF16), ("acc", d, F32)]


def _branch_bwd(dx, *branch):
    if not branch:
        return ()
    y, g = branch
    return dx * (1.0 + g), _colsum(dx * y)


def _final_loss(x, target, gain, name, branch):
    d = x.shape[1]

    def fn(x, t, gain, *br):
        xhat, rstd = _norm_stats(x)
        err = xhat * gain - t
        dy = err * (1.0 / d)
        loss = 0.5 * jnp.sum(jnp.mean(err * err, axis=-1, keepdims=True), axis=0, keepdims=True)
        dx = _norm_bwd(dy * gain, xhat, rstd)
        return (dx, _colsum(dy * xhat), jnp.broadcast_to(loss, (1, LANE))) + _branch_bwd(dx, *br)

    return _rowwise(fn, [("row", x), ("row", target), ("full", gain)] + _branch_ins(branch),
                    [("row", d, F32), ("acc", d, F32), ("acc", LANE, F32)] + _branch_outs(branch, d), name=name)


def _gla_gates(q, k, a, wg, bg, scale, c):
    ga = _dot(a, wg) + bg
    la = _log_sigmoid(ga) * (1.0 / GLA_TAU)
    b = _tri_matmul(_tri(c), la)
    bl = _colsum(la)
    eb, enb, eend = jnp.exp(b), jnp.exp(-b), jnp.exp(bl - b)
    q = q * scale
    return dict(ga=ga, eb=eb, enb=enb, eend=eend, dec=jnp.exp(bl), q_dec=q * eb, k_inv=k * enb, k_end=k * eend)


def _causal(c):
    return lax.broadcasted_iota(jnp.int32, (c, c), 0) >= lax.broadcasted_iota(jnp.int32, (c, c), 1)


def _gla_specs(heads, c, dk, dv, chunk):
    return [
        pl.BlockSpec((c, heads * dk), lambda n: (chunk(n), 0)),
        pl.BlockSpec((c, heads * dk), lambda n: (chunk(n), 1)),
        pl.BlockSpec((c, heads * dv), lambda n: (chunk(n), 1)),
        pl.BlockSpec((c, LANE), lambda n: (chunk(n), 0)),
        pl.BlockSpec((LANE, heads * dk), lambda n: (0, 0)),
        pl.BlockSpec((1, heads * dk), lambda n: (0, 0)),
    ]


def _gla_fwd(proj, a_tail, wg_p, bg, name):
    s = proj.shape[0]
    heads, c = GLA_HEADS, GLA_CHUNK
    dk = wg_p.shape[1] // heads
    dv = 2 * dk
    n_chunks = s // c
    scale = dk ** -0.5

    def body(q_ref, k_ref, v_ref, a_ref, wg_ref, bg_ref, o_ref, st_ref, state):
        @pl.when(pl.program_id(0) == 0)
        def _():
            state[...] = jnp.zeros_like(state)

        a = a_ref[...]
        for h in range(heads):
            sk, sv = slice(h * dk, (h + 1) * dk), slice(h * dv, (h + 1) * dv)
            g = _gla_gates(q_ref[:, sk], k_ref[:, sk], a, wg_ref[:, sk], bg_ref[:, sk], scale, c)
            v = v_ref[:, sv]
            st = state[h]
            attn = jnp.where(_causal(c), _dot(g["q_dec"], g["k_inv"], tb=True), 0.0)
            o_ref[:, sv] = _dot(attn, v) + _dot(g["q_dec"], st, tb=True)
            st_ref[h] = st.astype(st_ref.dtype)
            state[h] = g["dec"] * st + _dot(v, g["k_end"], ta=True)

    return _pcall(
        body, name=name, grid=(n_chunks,),
        in_specs=_gla_specs(heads, c, dk, dv, lambda n: n),
        out_specs=[pl.BlockSpec((c, heads * dv), lambda n: (n, 0)),
                   pl.BlockSpec((heads, None, dv, dk), lambda n: (0, n, 0, 0))],
        out_shape=[jax.ShapeDtypeStruct((s, heads * dv), F32),
                   jax.ShapeDtypeStruct((heads, n_chunks, dv, dk), BF16)],
        scratch_shapes=[pltpu.VMEM((heads, dv, dk), F32)],
        compiler_params=_params(1),
    )(proj, proj, proj, a_tail, wg_p, bg)


def _gla_bwd(proj, a_tail, wg_p, bg, states, d_o, dproj, name):
    s = proj.shape[0]
    heads, c = GLA_HEADS, GLA_CHUNK
    dk = wg_p.shape[1] // heads
    dv = 2 * dk
    n_chunks = s // c
    scale = dk ** -0.5
    k0, v0 = heads * dk, 2 * heads * dk

    def body(q_ref, k_ref, v_ref, a_ref, wg_ref, bg_ref, st_ref, do_ref, dproj_in, dqkv_ref, dga_ref, dstate):
        @pl.when(pl.program_id(0) == 0)
        def _():
            dstate[...] = jnp.zeros_like(dstate)

        a = a_ref[...]
        mask = _causal(c)
        for h in range(heads):
            sk, sv = slice(h * dk, (h + 1) * dk), slice(h * dv, (h + 1) * dv)
            out_k, out_v = slice(k0 + h * dk, k0 + (h + 1) * dk), slice(v0 + h * dv, v0 + (h + 1) * dv)
            g = _gla_gates(q_ref[:, sk], k_ref[:, sk], a, wg_ref[:, sk], bg_ref[:, sk], scale, c)
            v, st, dst, d_out = v_ref[:, sv], st_ref[h], dstate[h], do_ref[:, sv]
            q_dec, k_inv, k_end = g["q_dec"], g["k_inv"], g["k_end"]
            attn = jnp.where(mask, _dot(q_dec, k_inv, tb=True), 0.0)
            d_attn = jnp.where(mask, _dot(d_out, v, tb=True), 0.0)
            d_qdec = _dot(d_attn, k_inv) + _dot(d_out, st)
            d_kinv = _dot(d_attn, q_dec, ta=True)
            d_kend = _dot(v, dst)
            dqkv_ref[:, out_v] = (_dot(attn, d_out, ta=True) + _dot(k_end, dst, tb=True)).astype(dqkv_ref.dtype)
            d_dec = jnp.sum(dst * st.astype(F32), axis=0, keepdims=True)
            dstate[h] = g["dec"] * dst + _dot(d_out, q_dec, ta=True)

            dqkv_ref[:, sk] = (d_qdec * (scale * g["eb"])).astype(dqkv_ref.dtype)
            dqkv_ref[:, out_k] = (d_kinv * g["enb"] + d_kend * g["eend"]).astype(dqkv_ref.dtype)
            kk = d_kend * k_end
            db = d_qdec * q_dec - d_kinv * k_inv - kk
            dbl = jnp.sum(kk, axis=0, keepdims=True) + d_dec * g["dec"]
            last = lax.broadcasted_iota(jnp.int32, db.shape, 0) == c - 1
            db = db + jnp.where(last, dbl, 0.0)
            dla = _tri_matmul(_tri(c, upper=True), db)
            dga_ref[:, sk] = dla * (1.0 / GLA_TAU) * _sigmoid(-g["ga"])

    chunk = lambda n: n_chunks - 1 - n
    rev = lambda n: (chunk(n), 0)
    return _pcall(
        body, name=name, grid=(n_chunks,),
        in_specs=_gla_specs(heads, c, dk, dv, chunk) + [
            pl.BlockSpec((heads, None, dv, dk), lambda n: (0, chunk(n), 0, 0)),
            pl.BlockSpec((c, heads * dv), rev), pl.BlockSpec(memory_space=pl.ANY)],
        out_specs=[pl.BlockSpec((c, v0 + heads * dv), rev), pl.BlockSpec((c, heads * dk), rev)],
        out_shape=[jax.ShapeDtypeStruct(dproj.shape, dproj.dtype), jax.ShapeDtypeStruct((s, heads * dk), F32)],
        scratch_shapes=[pltpu.VMEM((heads, dv, dk), F32)],
        input_output_aliases={8: 0},
        compiler_params=_params(1),
    )(proj, proj, proj, a_tail, wg_p, bg, states, d_o, dproj)


def _gla_post_fwd(o, r, gn, name):
    dvt = o.shape[1]
    dv = dvt // GLA_HEADS

    def fn(o, r, gn):
        outs = []
        for h in range(GLA_HEADS):
            sl = slice(h * dv, (h + 1) * dv)
            ohat, _ = _norm_stats(o[:, sl])
            outs.append((ohat * gn[:, sl]) * _silu(r[:, sl]))
        return (jnp.concatenate(outs, axis=1),)

    return _rowwise(fn, [("row", o), r, ("full", gn)], [("row", dvt, BF16)], name=name)[0]


def _gla_post_bwd(o, r, gn, dog, name):
    dvt = o.shape[1]
    dv = dvt // GLA_HEADS

    def fn(o, r, gn, dog):
        d_o, d_r, d_g = [], [], []
        for h in range(GLA_HEADS):
            sl = slice(h * dv, (h + 1) * dv)
            ohat, rstd = _norm_stats(o[:, sl])
            g, rr, dd = gn[:, sl], r[:, sl], dog[:, sl]
            d_r.append(dd * (ohat * g) * _dsilu(rr))
            don = dd * _silu(rr)
            d_g.append(_colsum(don * ohat))
            d_o.append(_norm_bwd(don * g, ohat, rstd))
        return jnp.concatenate(d_o, axis=1), jnp.concatenate(d_r, axis=1), jnp.concatenate(d_g, axis=1)

    return _rowwise(fn, [("row", o), r, ("full", gn), ("row", dog)],
                    [("row", dvt, F32), ("band", dvt, BF16, 2, 3 * dvt), ("acc", dvt, F32)], name=name)


def _fox_prep(q, k, v, qg, kg, d, hd, name):
    heads = d // hd
    scale = hd ** -0.5

    def fn(q, k, v, qg, kg):
        qs, ks = [], []
        for h in range(heads):
            sl = slice(h * hd, (h + 1) * hd)
            qs.append(_norm_stats(q[:, sl])[0] * qg * scale)
            ks.append(_norm_stats(k[:, sl])[0] * kg)
        return jnp.concatenate(qs, axis=1), jnp.concatenate(ks, axis=1), v

    return _rowwise(fn, [q, k, v, ("full", qg), ("full", kg)],
                    [("row", d, BF16)] * 3, name=name)


def _fox_prep_bwd(q, k, dqn, dkn, qg, kg, hd, dproj, name):
    d = dqn.shape[1]
    heads = d // hd
    scale = hd ** -0.5

    def fn(q, k, dqn, dkn, qg, kg):
        dq, dk, gq, gk = [], [], [], []
        for h in range(heads):
            sl = slice(h * hd, (h + 1) * hd)
            for x, dxn, g, s, dl, gl in ((q, dqn, qg, scale, dq, gq), (k, dkn, kg, 1.0, dk, gk)):
                xhat, rstd = _norm_stats(x[:, sl])
                dn = dxn[:, sl] * s
                gl.append(_colsum(dn * xhat))
                dl.append(_norm_bwd(dn * g, xhat, rstd))
        cat = lambda t: jnp.concatenate(t, axis=1)
        return cat(dq + dk), cat(gq), cat(gk)

    return _rowwise(fn, [q, k, ("row", dqn), ("row", dkn), ("full", qg), ("full", kg)],
                    [("band", 2 * d, BF16, 0, 4 * d), ("acc", d, F32), ("acc", d, F32)], name=name, into=(dproj, 0))


def _fox_cum(fl, bf_p, name, tb=256):
    s = fl.shape[0]
    tb = _tile(s, tb)

    def body(fl_ref, bf_ref, cum_ref, carry):
        @pl.when(pl.program_id(0) == 0)
        def _():
            carry[...] = jnp.zeros_like(carry)

        lf = _log_sigmoid(fl_ref[...] + bf_ref[...])
        cum_ref[...] = _tri_matmul(_tri(tb), lf) + carry[...]
        carry[...] += _colsum(lf)

    return _pcall(
        body, name=name, grid=(s // tb,),
        in_specs=[pl.BlockSpec((tb, LANE), lambda i: (i, 0)), pl.BlockSpec((1, LANE), lambda i: (0, 0))],
        out_specs=pl.BlockSpec((tb, LANE), lambda i: (i, 0)),
        out_shape=jax.ShapeDtypeStruct((s, LANE), F32),
        scratch_shapes=[pltpu.VMEM((1, LANE), F32)],
        compiler_params=_params(1),
    )(fl, bf_p)


def _fox_cum_bwd(dcum, fl, bf_p, name, tb=256):
    s = fl.shape[0]
    tb = _tile(s, tb)
    nb = s // tb

    def body(dc_ref, fl_ref, bf_ref, dfl_ref, dbf_ref, carry):
        @pl.when(pl.program_id(0) == 0)
        def _():
            carry[...] = jnp.zeros_like(carry)
            dbf_ref[...] = jnp.zeros_like(dbf_ref)

        dc = dc_ref[...]
        dlf = _tri_matmul(_tri(tb, upper=True), dc) + carry[...]
        carry[...] += _colsum(dc)
        dfl = dlf * _sigmoid(-(fl_ref[...] + bf_ref[...]))
        dfl_ref[...] = dfl
        dbf_ref[...] += _colsum(dfl)

    rev = lambda i: (nb - 1 - i, 0)
    return _pcall(
        body, name=name, grid=(nb,),
        in_specs=[pl.BlockSpec((tb, LANE), rev), pl.BlockSpec((tb, LANE), rev), pl.BlockSpec((1, LANE), lambda i: (0, 0))],
        out_specs=[pl.BlockSpec((tb, LANE), rev), pl.BlockSpec((1, LANE), lambda i: (0, 0))],
        out_shape=[jax.ShapeDtypeStruct((s, LANE), F32), jax.ShapeDtypeStruct((1, LANE), F32)],
        scratch_shapes=[pltpu.VMEM((1, LANE), F32)],
        compiler_params=_params(1),
    )(dcum, fl, bf_p)


def _fox_attn_fwd(qn, kn, vb, cum_col, cum_row, hd, t, name):
    s, d = qn.shape
    heads = d // hd
    nq = s // t

    def body(q_ref, k_ref, v_ref, cc_ref, cr_ref, o_ref, lse_ref):
        qi = pl.program_id(1)
        q = q_ref[...]
        cq = cc_ref[...]
        qpos = qi * t + lax.broadcasted_iota(jnp.int32, (t, 1), 0)

        def step(kj, carry, diagonal=False):
            m, l, acc = carry
            off = pl.multiple_of(kj * t, t)
            ks, vs = k_ref[pl.ds(off, t), :], v_ref[pl.ds(off, t), :]
            sc = _dot(q, ks, tb=True) + cq - cr_ref[kj]
            if diagonal:
                kpos = off + lax.broadcasted_iota(jnp.int32, (1, t), 1)
                sc = jnp.where(kpos <= qpos, sc, NEG)
            m_new = jnp.maximum(m, jnp.max(sc, axis=1, keepdims=True))
            alpha = jnp.exp(m - m_new)
            p = jnp.exp(sc - m_new)
            return m_new, alpha * l + jnp.sum(p, axis=1, keepdims=True), alpha * acc + _dot(p, vs)

        init = (jnp.full((t, 1), NEG, F32), jnp.zeros((t, 1), F32), jnp.zeros((t, hd), F32))
        m, l, acc = step(qi, lax.fori_loop(0, qi, step, init), diagonal=True)
        o_ref[...] = acc / l
        lse_ref[...] = m + jnp.log(l)

    return _pcall(
        body, name=name, grid=(heads, nq),
        in_specs=[pl.BlockSpec((t, hd), lambda h, i: (i, h)),
                  pl.BlockSpec((s, hd), lambda h, i: (0, h)),
                  pl.BlockSpec((s, hd), lambda h, i: (0, h)),
                  pl.BlockSpec((None, t, 1), lambda h, i: (h, i, 0)),
                  pl.BlockSpec((None, nq, 1, t), lambda h, i: (h, 0, 0, 0))],
        out_specs=[pl.BlockSpec((t, hd), lambda h, i: (i, h)), pl.BlockSpec((None, t, 1), lambda h, i: (h, i, 0))],
        out_shape=[jax.ShapeDtypeStruct((s, d), F32), jax.ShapeDtypeStruct((heads, s, 1), F32)],
        compiler_params=_params(2),
    )(qn, kn, vb, cum_col, cum_row)


def _fox_attn_bwd(qn, kn, vb, d_o, o, lse, cum_col, cum_row, hd, t, dproj, name):
    s, d = qn.shape
    heads = d // hd
    nq = s // t

    def body(q_ref, k_ref, v_ref, do_ref, o_ref, lse_ref, cc_ref, cr_ref, dproj_in,
             dq_ref, dk_ref, dv_ref, dcq_ref, dck_ref, delta, do_b):
        kj = pl.program_id(1)

        @pl.when(kj == 0)
        def _():
            dq_ref[...] = jnp.zeros_like(dq_ref)
            dcq_ref[...] = jnp.zeros_like(dcq_ref)
            delta[...] = jnp.sum(do_ref[...] * o_ref[...], axis=1, keepdims=True)
            do_b[...] = do_ref[...].astype(do_b.dtype)

        ks, vs, cr = k_ref[...], v_ref[...], cr_ref[...]
        kpos = kj * t + lax.broadcasted_iota(jnp.int32, (1, t), 1)

        def step(qi, carry, diagonal=False):
            dk, dv, dck = carry
            rows = pl.ds(pl.multiple_of(qi * t, t), t)
            q, d_out = q_ref[rows, :], do_b[rows, :]
            sc = _dot(q, ks, tb=True) + cc_ref[rows, :] - cr
            p = jnp.exp(sc - lse_ref[rows, :])
            if diagonal:
                qpos = qi * t + lax.broadcasted_iota(jnp.int32, (t, 1), 0)
                p = jnp.where(kpos <= qpos, p, 0.0)
            ds = p * (_dot(d_out, vs, tb=True) - delta[rows, :])
            dq_ref[rows, :] += _dot(ds, ks)
            dcq_ref[rows, :] += jnp.sum(ds, axis=1, keepdims=True)
            return dk + _dot(ds, q, ta=True), dv + _dot(p, d_out, ta=True), dck + _colsum(ds)

        init = (jnp.zeros((t, hd), F32), jnp.zeros((t, hd), F32), jnp.zeros((1, t), F32))
        dk, dv, dck = lax.fori_loop(kj + 1, nq, step, step(kj, init, diagonal=True))
        dk_ref[...] = dk.astype(dk_ref.dtype)
        dv_ref[...] = dv.astype(dv_ref.dtype)
        dck_ref[...] = dck

    head_rows = lambda h, j: (0, h)
    blk = lambda h, j: (j, h)
    return _pcall(
        body, name=name, grid=(heads, nq),
        in_specs=[pl.BlockSpec((s, hd), head_rows), pl.BlockSpec((t, hd), blk), pl.BlockSpec((t, hd), blk),
                  pl.BlockSpec((s, hd), head_rows), pl.BlockSpec((s, hd), head_rows),
                  pl.BlockSpec((None, s, 1), lambda h, j: (h, 0, 0)),
                  pl.BlockSpec((None, s, 1), lambda h, j: (h, 0, 0)),
                  pl.BlockSpec((None, None, 1, t), lambda h, j: (h, j, 0, 0)),
                  pl.BlockSpec(memory_space=pl.ANY)],
        out_specs=[pl.BlockSpec((s, hd), head_rows), pl.BlockSpec((t, hd), blk),
                   pl.BlockSpec((t, hd), lambda h, j: (j, 2 * heads + h)),
                   pl.BlockSpec((None, s, 1), lambda h, j: (h, 0, 0)),
                   pl.BlockSpec((None, None, 1, t), lambda h, j: (h, j, 0, 0))],
        out_shape=[jax.ShapeDtypeStruct((s, d), F32), jax.ShapeDtypeStruct((s, d), BF16),
                   jax.ShapeDtypeStruct(dproj.shape, dproj.dtype), jax.ShapeDtypeStruct((heads, s, 1), F32),
                   jax.ShapeDtypeStruct((heads, nq, 1, t), F32)],
        scratch_shapes=[pltpu.VMEM((s, 1), F32), pltpu.VMEM((s, hd), BF16)],
        input_output_aliases={8: 2},
        compiler_params=_params(2),
    )(qn, kn, vb, d_o, o, lse, cum_col, cum_row, dproj)


def _fox_gate_fwd(o, og, name):
    def fn(o, og):
        return (o * _sigmoid(og),)

    return _rowwise(fn, [("row", o), og], [("row", o.shape[1], BF16)], name=name)[0]


def _fox_gate_bwd(o, og, dact, name):
    def fn(o, og, dact):
        sg = _sigmoid(og)
        return dact * sg, dact * o * sg * (1.0 - sg)

    d = o.shape[1]
    return _rowwise(fn, [("row", o), og, ("row", dact)], [("row", d, F32), ("band", d, BF16, 3, 4 * d)], name=name)


def _shift_down(x, n):
    rows = lax.broadcasted_iota(jnp.int32, x.shape, 0)
    return jnp.where(rows >= n, pltpu.roll(x, n, 0), 0.0)


def _shift_up(x, n):
    rows = lax.broadcasted_iota(jnp.int32, x.shape, 0)
    return jnp.where(rows < x.shape[0] - n, pltpu.roll(x, x.shape[0] - n, 0), 0.0)


def _conv(u, w_ref, b):
    return w_ref[0:1, :] * _shift_down(u, 2) + w_ref[1:2, :] * _shift_down(u, 1) + w_ref[2:3, :] * u + b


def _conv_act_fwd(u, cw, cb, name, tc=256):
    s, two_f = u.shape
    dff = two_f // 2
    tc = _tile(dff, tc)
    nb = dff // tc

    def body(ug_ref, uv_ref, wg_ref, wv_ref, bg_ref, bv_ref, a_ref):
        gate = _conv(ug_ref[...], wg_ref, bg_ref[...])
        val = _conv(uv_ref[...], wv_ref, bv_ref[...])
        a_ref[...] = (_silu(gate) * val).astype(a_ref.dtype)

    lo, hi = (lambda j: (0, j)), (lambda j: (0, j + nb))
    return _pcall(
        body, name=name, grid=(nb,),
        in_specs=[pl.BlockSpec((s, tc), lo), pl.BlockSpec((s, tc), hi), pl.BlockSpec((3, tc), lo),
                  pl.BlockSpec((3, tc), hi), pl.BlockSpec((1, tc), lo), pl.BlockSpec((1, tc), hi)],
        out_specs=pl.BlockSpec((s, tc), lo),
        out_shape=jax.ShapeDtypeStruct((s, dff), BF16),
        compiler_params=_params(1),
    )(u, u, cw, cw, cb, cb)


def _conv_act_bwd(u, cw, cb, da, name, tc=128):
    s, two_f = u.shape
    dff = two_f // 2
    tc = _tile(dff, tc)
    nb = dff // tc

    def body(ug_ref, uv_ref, wg_ref, wv_ref, bg_ref, bv_ref, da_ref, du_ref, dw_ref, db_ref):
        ug, uv, da = ug_ref[...], uv_ref[...], da_ref[...]
        gate = _conv(ug, wg_ref, bg_ref[...])
        val = _conv(uv, wv_ref, bv_ref[...])
        sg = _sigmoid(gate)
        d_val = da * (gate * sg)
        d_gate = da * val * (sg * (1.0 + gate * (1.0 - sg)))
        for half, (dc, uu, w_ref) in enumerate(((d_gate, ug, wg_ref), (d_val, uv, wv_ref))):
            du = w_ref[0:1, :] * _shift_up(dc, 2) + w_ref[1:2, :] * _shift_up(dc, 1) + w_ref[2:3, :] * dc
            du_ref[half] = du.astype(du_ref.dtype)
            dw_ref[half, 0:1, :] = _colsum(dc * _shift_down(uu, 2))
            dw_ref[half, 1:2, :] = _colsum(dc * _shift_down(uu, 1))
            dw_ref[half, 2:3, :] = _colsum(dc * uu)
            db_ref[half] = _colsum(dc)

    lo, hi = (lambda j: (0, j)), (lambda j: (0, j + nb))
    both = lambda j: (0, 0, j)
    return _pcall(
        body, name=name, grid=(nb,),
        in_specs=[pl.BlockSpec((s, tc), lo), pl.BlockSpec((s, tc), hi), pl.BlockSpec((3, tc), lo),
                  pl.BlockSpec((3, tc), hi), pl.BlockSpec((1, tc), lo), pl.BlockSpec((1, tc), hi),
                  pl.BlockSpec((s, tc), lo)],
        out_specs=[pl.BlockSpec((2, s, tc), both), pl.BlockSpec((2, 3, tc), both), pl.BlockSpec((2, 1, tc), both)],
        out_shape=[jax.ShapeDtypeStruct((2, s, dff), BF16), jax.ShapeDtypeStruct((2, 3, dff), F32),
                   jax.ShapeDtypeStruct((2, 1, dff), F32)],
        compiler_params=_params(1),
    )(u, u, cw, cw, cb, cb, da)


def _adamw_math(w, g, m, v):
    m = ADAM_B1 * m + (1.0 - ADAM_B1) * g
    v = ADAM_B2 * v + (1.0 - ADAM_B2) * (g * g)
    m_hat = m / (1.0 - ADAM_B1 ** ADAM_STEP)
    v_hat = v / (1.0 - ADAM_B2 ** ADAM_STEP)
    delta = -ADAM_LR * (m_hat / (jnp.sqrt(v_hat) + ADAM_EPS) + ADAM_WD * w)
    return delta, m, v


def _update_tiles(r, c, tr):
    tc = c
    if r % 8:
        tr, tc = r, _tile(c, max(LANE, 512 * 1024 // r // LANE * LANE))
    elif r <= tr:
        tr = r
    while r % tr:
        tr -= 8
    return tr, tc


def _adamw(w, g, m, v, name, tr=128):
    layers, r, c = w.shape
    tr, tc = _update_tiles(r, c, tr)

    def body(w_ref, g_ref, m_ref, v_ref, go_ref, d_ref, mo_ref, vo_ref):
        grad = g_ref[...]
        delta, m_new, v_new = _adamw_math(w_ref[...], grad, m_ref[...], v_ref[...])
        go_ref[...], d_ref[...], mo_ref[...], vo_ref[...] = grad, delta, m_new, v_new

    spec = pl.BlockSpec((None, tr, tc), lambda l, i, j: (l, i, j))
    return _pcall(
        body, name=name, grid=(layers, r // tr, c // tc), in_specs=[spec] * 4, out_specs=[spec] * 4,
        out_shape=[jax.ShapeDtypeStruct((layers, r, c), F32)] * 4, compiler_params=_params(3),
    )(w, g, m, v)


def _adamw_pieces(w, lands, sums, chip, m, v, name, tr=128):
    layers, r, c = w.shape
    tr, tc = _update_tiles(r, c, tr)
    nr, nc = r // tr, c // tc

    def body(chip_ref, w_ref, *rest):
        land_refs, own_refs = rest[:layers], rest[layers:2 * layers]
        m_ref, v_ref, go_ref, d_ref, mo_ref, vo_ref = rest[2 * layers:]
        for layer in range(layers):
            @pl.when(pl.program_id(0) == layer)
            def _(land_ref=land_refs[layer], own_ref=own_refs[layer]):
                grad = jnp.zeros(w_ref.shape, F32)
                for q in range(4):
                    grad = grad + jnp.where(chip_ref[0] == q, own_ref[...], land_ref[q]).astype(F32)
                delta, m_new, v_new = _adamw_math(w_ref[...], grad, m_ref[...], v_ref[...])
                go_ref[...], d_ref[...], mo_ref[...], vo_ref[...] = grad, delta, m_new, v_new

    def walk(k, l, i, j):
        here = l == k
        return jnp.where(here, i, jnp.where(l < k, 0, nr - 1)), jnp.where(here, j, jnp.where(l < k, 0, nc - 1))

    spec = pl.BlockSpec((None, tr, tc), lambda l, i, j, chip_ref: (l, i, j))
    land_specs = [pl.BlockSpec((4, tr, tc), lambda l, i, j, chip_ref, k=k: (0,) + walk(k, l, i, j))
                  for k in range(layers)]
    own_specs = [pl.BlockSpec((None, tr, tc), lambda l, i, j, chip_ref, k=k: (chip_ref[0],) + walk(k, l, i, j))
                 for k in range(layers)]
    return _pcall(
        body, name=name,
        grid_spec=pltpu.PrefetchScalarGridSpec(
            num_scalar_prefetch=1, grid=(layers, nr, nc),
            in_specs=[spec] + land_specs + own_specs + [spec, spec], out_specs=[spec] * 4),
        out_shape=[jax.ShapeDtypeStruct((layers, r, c), F32)] * 4, compiler_params=_params(3),
    )(chip, w, *lands, *sums, m, v)


def _pair_sum(pieces, partner, core, name, tr=512):
    _, r, c = pieces.shape
    tc = c
    if r % 8:
        tr, tc = r, _tile(c, max(LANE, 1024 * 1024 // r // LANE * LANE))
    elif r <= tr:
        tr = r
    while r % tr:
        tr -= 8

    def body(core_ref, mine_ref, partner_ref, out_ref):
        out_ref[...] = (mine_ref[...].astype(F32) + partner_ref[...].astype(F32)).astype(out_ref.dtype)

    return _pcall(
        body, name=name,
        grid_spec=pltpu.PrefetchScalarGridSpec(
            num_scalar_prefetch=1, grid=(4, r // tr, c // tc),
            in_specs=[pl.BlockSpec((None, tr, tc), lambda q, i, j, core_ref: (2 * q + core_ref[0], i, j)),
                      pl.BlockSpec((None, tr, tc), lambda q, i, j, core_ref: (q, i, j))],
            out_specs=pl.BlockSpec((None, tr, tc), lambda q, i, j, core_ref: (q, i, j))),
        out_shape=jax.ShapeDtypeStruct((4, r, c), pieces.dtype), compiler_params=_params(3),
    )(core, pieces, partner)


def _sum8(x, name):
    p = x.shape[2]
    tp = _tile(p, 16 * 1024)

    def body(x_ref, o_ref):
        acc = x_ref[0]
        for i in range(1, N_DEV):
            acc = acc + x_ref[i]
        o_ref[...] = acc

    return _pcall(
        body, name=name, grid=(p // tp,), in_specs=[pl.BlockSpec((N_DEV, 1, tp), lambda i: (0, 0, i))],
        out_specs=pl.BlockSpec((1, tp), lambda i: (0, i)), out_shape=jax.ShapeDtypeStruct((1, p), x.dtype),
        compiler_params=_params(1),
    )(x)


def _exchange(arrays, name, scatter):
    n = len(arrays)
    hbm = pl.BlockSpec(memory_space=pl.ANY)

    def body(*refs):
        ins, outs, token = refs[:n], refs[n:2 * n], refs[2 * n]
        send_sems, recv_sems, local_sems = refs[2 * n + 1:]
        token[...] = jnp.zeros_like(token)
        x, y, c = lax.axis_index("x"), lax.axis_index("y"), lax.axis_index("c")
        me = 4 * x + 2 * y + c
        copies = []
        for a in range(n):
            src_mine = ins[a].at[me] if scatter else ins[a]
            local = pltpu.make_async_copy(src_mine, outs[a].at[me], local_sems.at[a])
            local.start()
            copies.append(local)
            for k in range(1, N_DEV):
                px = 1 - x if k & 4 else x
                py = 1 - y if k & 2 else y
                pc = 1 - c if k & 1 else c
                src = ins[a].at[4 * px + 2 * py + pc] if scatter else ins[a]
                cp = pltpu.make_async_remote_copy(
                    src_ref=src, dst_ref=outs[a].at[me],
                    send_sem=send_sems.at[a * (N_DEV - 1) + k - 1], recv_sem=recv_sems.at[a * (N_DEV - 1) + k - 1],
                    device_id=(px, py, pc), device_id_type=pl.DeviceIdType.MESH)
                cp.start()
                copies.append(cp)
        for cp in copies:
            cp.wait()

    out_shape = [jax.ShapeDtypeStruct(a.shape if scatter else (N_DEV,) + a.shape, a.dtype) for a in arrays]
    res = _pcall(
        body, name=name, in_specs=[hbm] * n, out_specs=[hbm] * n + [pl.BlockSpec(memory_space=pltpu.VMEM)],
        out_shape=out_shape + [jax.ShapeDtypeStruct((8, LANE), F32)],
        scratch_shapes=[pltpu.SemaphoreType.DMA((n * (N_DEV - 1),)), pltpu.SemaphoreType.DMA((n * (N_DEV - 1),)),
                        pltpu.SemaphoreType.DMA((n,))],
        compiler_params=pltpu.CompilerParams(has_side_effects=True),
    )(*arrays)
    return res[:n], res[n][0, 0]


_HBM = pl.BlockSpec(memory_space=pltpu.HBM)
_SEM = pl.BlockSpec(memory_space=pltpu.SEMAPHORE)
_DATAFLOW = pltpu.SideEffectType.DATAFLOW_SIDE_EFFECTING


def _peer(k, x, y, c):
    return (1 - x if k & 4 else x, 1 - y if k & 2 else y, 1 - c if k & 1 else c)


def _pair_plan(x, y, c):
    return [(2 * q + (1 - c), q, (x, y, 1 - c)) for q in range(4)]


def _chip_plan(x, y, c):
    out = []
    for k in _ICI_PEERS:
        px, py, pc = _peer(k, x, y, c)
        out.append((2 * px + py, 2 * x + y, (px, py, pc)))
    return out


def _all_plan(x, y, c):
    return [(0, 4 * x + 2 * y + c, _peer(k, x, y, c)) for k in range(1, N_DEV)]


def _split_start(arrays, plan, name, land_blocks=4):
    n = len(arrays)
    lands = [lax.empty((land_blocks,) + a.shape[1:], a.dtype) for a in arrays]
    n_copies = len(plan(0, 0, 0))

    def body(*refs):
        srcs, dsts = refs[:n], refs[n:2 * n]
        send_sems, recv_sems, token = refs[4 * n:5 * n], refs[5 * n:6 * n], refs[6 * n]
        copies = plan(lax.axis_index("x"), lax.axis_index("y"), lax.axis_index("c"))
        for a in range(n):
            for j, (src_block, dst_block, peer) in enumerate(copies):
                pltpu.make_async_remote_copy(
                    src_ref=srcs[a].at[src_block], dst_ref=dsts[a].at[dst_block],
                    send_sem=send_sems[a].at[j], recv_sem=recv_sems[a].at[j],
                    device_id=peer, device_id_type=pl.DeviceIdType.MESH).start()
        token[...] = jnp.zeros_like(token)

    sems = [pltpu.SemaphoreType.DMA((n_copies,))] * (2 * n)
    res = _pcall(
        body, name=name,
        in_specs=[_HBM] * (2 * n),
        out_specs=[_HBM] * (2 * n) + [_SEM] * (2 * n) + [pl.BlockSpec(memory_space=pltpu.VMEM)],
        out_shape=[pltpu.HBM(a.shape, a.dtype) for a in arrays] + [pltpu.HBM(l.shape, l.dtype) for l in lands]
        + sems + [jax.ShapeDtypeStruct((8, LANE), F32)],
        input_output_aliases={i: i for i in range(2 * n)},
        compiler_params=pltpu.CompilerParams(has_side_effects=_DATAFLOW),
    )(*[pltpu.with_memory_space_constraint(a, pltpu.HBM) for a in arrays],
      *[pltpu.with_memory_space_constraint(l, pltpu.HBM) for l in lands])
    handles = [(res[a], res[n + a], res[2 * n + a], res[3 * n + a]) for a in range(n)]
    return handles, res[4 * n][0, 0]


def _split_wait(handles, plan, after, name):
    n = len(handles)
    after = list(after) if isinstance(after, (list, tuple)) else [after]

    def body(*refs):
        srcs, dsts = refs[:n], refs[n:2 * n]
        send_sems, recv_sems = refs[2 * n:3 * n], refs[3 * n:4 * n]
        copies = plan(lax.axis_index("x"), lax.axis_index("y"), lax.axis_index("c"))
        for a in range(n):
            for j, (src_block, dst_block, peer) in enumerate(copies):
                cp = pltpu.make_async_remote_copy(
                    src_ref=srcs[a].at[src_block], dst_ref=dsts[a].at[dst_block],
                    send_sem=send_sems[a].at[j], recv_sem=recv_sems[a].at[j],
                    device_id=peer, device_id_type=pl.DeviceIdType.MESH)
                cp.wait_send()
                cp.wait_recv()

    srcs, lands = [h[0] for h in handles], [h[1] for h in handles]
    res = _pcall(
        body, name=name,
        in_specs=[_HBM] * (2 * n) + [_SEM] * (2 * n) + [pl.BlockSpec(memory_space=pl.ANY)] * len(after),
        out_specs=[_HBM] * (2 * n),
        out_shape=[pltpu.HBM(t.shape, t.dtype) for t in srcs + lands],
        input_output_aliases={i: i for i in range(2 * n)},
        compiler_params=pltpu.CompilerParams(has_side_effects=_DATAFLOW),
    )(*srcs, *lands, *[h[2] for h in handles], *[h[3] for h in handles], *after)
    return res[:n], res[n:]


_ICI_PEERS = (2, 4, 6)


def _gather2_start(shards, name):
    n = len(shards)
    lands = [lax.empty((N_DEV,) + a.shape, a.dtype) for a in shards]

    def body(*refs):
        srcs, dsts = refs[:n], refs[n:2 * n]
        send_sems, d2d_sems, ici_sems = refs[4 * n:5 * n], refs[5 * n:6 * n], refs[6 * n:7 * n]
        token = refs[7 * n]
        x, y, c = lax.axis_index("x"), lax.axis_index("y"), lax.axis_index("c")
        me = 4 * x + 2 * y + c
        for a in range(n):
            for j, k in enumerate((1,) + _ICI_PEERS):
                recv = d2d_sems[a].at[0] if j == 0 else ici_sems[a].at[j - 1]
                pltpu.make_async_remote_copy(
                    src_ref=srcs[a], dst_ref=dsts[a].at[me], send_sem=send_sems[a].at[j], recv_sem=recv,
                    device_id=_peer(k, x, y, c), device_id_type=pl.DeviceIdType.MESH).start()
        token[...] = jnp.zeros_like(token)

    dma = pltpu.SemaphoreType.DMA
    res = _pcall(
        body, name=name,
        in_specs=[_HBM] * (2 * n),
        out_specs=[_HBM] * (2 * n) + [_SEM] * (3 * n) + [pl.BlockSpec(memory_space=pltpu.VMEM)],
        out_shape=[pltpu.HBM(a.shape, a.dtype) for a in shards] + [pltpu.HBM(l.shape, l.dtype) for l in lands]
        + [dma((4,))] * n + [dma((1,))] * n + [dma((3,))] * n + [jax.ShapeDtypeStruct((8, LANE), F32)],
        input_output_aliases={i: i for i in range(2 * n)},
        compiler_params=pltpu.CompilerParams(has_side_effects=_DATAFLOW),
    )(*[pltpu.with_memory_space_constraint(a, pltpu.HBM) for a in shards],
      *[pltpu.with_memory_space_constraint(l, pltpu.HBM) for l in lands])
    handles = [tuple(res[i * n + a] for i in range(5)) for a in range(n)]
    return handles, res[5 * n][0, 0]


def _gather2_forward(handle, after, name):
    src, land, send_sems, d2d_sem, ici_sems = handle

    def body(land_ref, ici_ref, d2d_ref, after_ref, land_out, fwd_send, fwd_recv, token):
        x, y, c = lax.axis_index("x"), lax.axis_index("y"), lax.axis_index("c")
        sibling = (x, y, 1 - c)
        arrived = [(_peer(k, x, y, c), ici_ref.at[j]) for j, k in enumerate(_ICI_PEERS)] + [(sibling, d2d_ref.at[0])]
        for j, ((px, py, pc), recv) in enumerate(arrived):
            block = land_ref.at[4 * px + 2 * py + pc]
            pltpu.make_async_remote_copy(
                src_ref=block, dst_ref=block, send_sem=fwd_send.at[j], recv_sem=recv,
                device_id=(px, py, pc), device_id_type=pl.DeviceIdType.MESH).wait_recv()
            pltpu.make_async_remote_copy(
                src_ref=block, dst_ref=block, send_sem=fwd_send.at[j], recv_sem=fwd_recv.at[j],
                device_id=sibling, device_id_type=pl.DeviceIdType.MESH).start()
        token[...] = jnp.zeros_like(token)

    dma = pltpu.SemaphoreType.DMA
    land, fwd_send, fwd_recv, token = _pcall(
        body, name=name,
        in_specs=[_HBM, _SEM, _SEM, pl.BlockSpec(memory_space=pl.ANY)],
        out_specs=[_HBM, _SEM, _SEM, pl.BlockSpec(memory_space=pltpu.VMEM)],
        out_shape=[pltpu.HBM(land.shape, land.dtype), dma((4,)), dma((4,)), jax.ShapeDtypeStruct((8, LANE), F32)],
        input_output_aliases={0: 0},
        compiler_params=pltpu.CompilerParams(has_side_effects=_DATAFLOW),
    )(land, ici_sems, d2d_sem, after)
    return (src, land, send_sems, fwd_send, fwd_recv), token[0, 0]


def _gather2_wait(handle, after, name):
    src, land, send_sems, fwd_send, fwd_recv = handle

    def body(src_ref, land_ref, send_ref, fsend_ref, frecv_ref, after_ref, src_out, land_out):
        x, y, c = lax.axis_index("x"), lax.axis_index("y"), lax.axis_index("c")
        block = land_ref.at[4 * x + 2 * y + c]

        def copy(send, recv):
            return pltpu.make_async_remote_copy(src_ref=src_ref, dst_ref=block, send_sem=send, recv_sem=recv,
                                                device_id=(x, y, 1 - c), device_id_type=pl.DeviceIdType.MESH)

        for j in range(4):
            copy(send_ref.at[j], frecv_ref.at[j]).wait_send()
        for j in range(4):
            copy(fsend_ref.at[j], frecv_ref.at[j]).wait_send()
            copy(fsend_ref.at[j], frecv_ref.at[j]).wait_recv()

    res = _pcall(
        body, name=name,
        in_specs=[_HBM, _HBM, _SEM, _SEM, _SEM, pl.BlockSpec(memory_space=pl.ANY)],
        out_specs=[_HBM, _HBM],
        out_shape=[pltpu.HBM(src.shape, src.dtype), pltpu.HBM(land.shape, land.dtype)],
        input_output_aliases={0: 0, 1: 1},
        compiler_params=pltpu.CompilerParams(has_side_effects=_DATAFLOW),
    )(src, land, send_sems, fwd_send, fwd_recv, after)
    return res[0], res[1]


def _pad_cols(x, width=LANE):
    return jnp.pad(x, ((0, 0), (0, width - x.shape[1])))


def _cols_full(g):
    return jnp.transpose(g, (1, 0, 2)).reshape(g.shape[1], -1)


def _ffn_fwd(x1, p, i, tag):
    h2 = _adaln_fwd(x1, p["norm_ffn"][i], p["sc_f"][i], p["sh_f"][i], f"ffn_norm_{tag}")
    u = _matmul(h2, p["fetch"](f"up{i}", h2), name=f"ffn_up_{tag}", tn=1408, b_shards=True)
    a = _conv_act_fwd(u, p["conv_w"][i], p["conv_b"][i], f"ffn_act_{tag}")
    g_f = p["g_f"][i]
    x2, f = _matmul(a, p["fetch"](f"down{i}", a), name=f"ffn_down_{tag}", tk=1408, out_dtypes=(F32, F32),
                    epilogue=lambda acc, x1, g: (x1 + (1.0 + g) * acc, acc), extras=(("mn", x1), ("n", g_f)))
    return x2, dict(h2=h2, u=u, a=a, f=f)


def _ffn_bwd(incoming, x1, saved, p, i, tag, branch):
    d = x1.shape[1]
    dx2, df, dg_f = incoming
    w_up, w_down = p["fetch"](f"up{i}", None), p["fetch"](f"down{i}", None)
    da = _matmul(df, w_down, tb=True, name=f"ffn_down_dx_{tag}", tn=1408)
    dw_down = _matmul(saved["a"], df, ta=True, name=f"ffn_down_dw_{tag}", tm=1408, out_dtypes=(BF16,))
    du, dcw, dcb = _conv_act_bwd(saved["u"], p["conv_w"][i], p["conv_b"][i], da, f"ffn_act_bwd_{tag}")
    dcw, dcb = (jnp.concatenate([t[0], t[1]], axis=1) for t in (dcw, dcb))
    tok = p["flush"](du)
    dh2 = _matmul(du, w_up, tb=True, name=f"ffn_up_dx_{tag}", tn=2048, tk=1408, a_halves=True, b_shards=True)
    dw_up = _matmul(saved["h2"], du, ta=True, name=f"ffn_up_dw_{tag}", tn=1408, out_dtypes=(BF16,), b_halves=True,
                    out_shards=True)
    tok = tok + p["send"](f"ffn{i}", [dw_up, dw_down.reshape(N_DEV, -1, d)])
    dx1, dsh, dsc, dgain, dy, dg_m = _adaln_bwd(x1, dh2, dx2, p["norm_ffn"][i] + tok, p["sc_f"][i],
                                                f"ffn_norm_bwd_{tag}", branch)
    grads = dict(conv_w=dcw, conv_b=dcb, norm_ffn=dgain, sh_f=dsh, sc_f=dsc, g_f=dg_f)
    return (dx1, dy, dg_m), grads


def _gla_layer_fwd(x, p, i):
    h1 = _adaln_fwd(x, p["norm_mix"][i], p["sc_m"][i], p["sh_m"][i], "gla_norm")
    w_t, w_tail_t, main = p["fetch"]("gla_in", h1)
    proj = _matmul(h1, w_t, tb=True, b_rows=main, name="gla_in")
    a_tail = _matmul(h1, w_tail_t, tb=True, name="gla_in_tail")
    dk_total = p["gla_wg_p"].shape[1]
    o, states = _gla_fwd(proj, a_tail, p["gla_wg_p"], p["gla_b_gate"], "gla_chunks")
    assert 2 * dk_total == o.shape[1]
    r = ("cols", proj, 2, o.shape[1])
    og = _gla_post_fwd(o, r, p["gla_norm"], "gla_post")
    x1, y = _matmul(og, p["fetch"]("gla_out", og), name="gla_out", out_dtypes=(F32, F32),
                    epilogue=lambda acc, x, g: (x + (1.0 + g) * acc, acc), extras=(("mn", x), ("n", p["g_m"][i])))
    return x1, dict(h1=h1, proj=proj, a_tail=a_tail, o=o, r=r, states=states, og=og, y=y)


def _gla_layer_bwd(incoming, x, sv, p, i, branch):
    d = x.shape[1]
    dx1, dy, dg_m = incoming
    (w_t, w_tail_t, main), w_out = p["fetch"]("gla_in", None), p["fetch"]("gla_out", None)
    dog = _matmul(dy, w_out, tb=True, name="gla_out_dx")
    dw_out = _matmul(sv["og"], dy, ta=True, name="gla_out_dw", out_dtypes=(BF16,))
    tok = p["flush"](dog) + p["send"]("gla_out", [dw_out.reshape(N_DEV, -1, d)])
    d_o, dproj, dgn = _gla_post_bwd(sv["o"], sv["r"], p["gla_norm"] + tok, dog, "gla_post_bwd")
    dproj, dga = _gla_bwd(sv["proj"], sv["a_tail"], p["gla_wg_p"], p["gla_b_gate"], sv["states"], d_o, dproj,
                          "gla_chunks_bwd")
    tok = p["flush"](dga)
    da_tail = _matmul(dga, p["gla_wg_p"], tb=True, name="gla_gate_dx", out_dtypes=(BF16,))
    dwg = _matmul(sv["a_tail"], dga, ta=True, name="gla_gate_dw")
    dbg = _rowwise(lambda t: (_colsum(t),), [("row", dga)], [("acc", dga.shape[1], F32)], name="gla_gate_db")[0]
    dh_tail = _matmul(da_tail, w_tail_t, name="gla_in_tail_dx")
    dh1 = _matmul(dproj, w_t, b_rows=main, name="gla_in_dx", tk=2048,
                  epilogue=lambda acc, t: (acc + t,), extras=(("mn", dh_tail),))
    rank = p["gla_rank"]
    dw_main = _matmul(dproj, sv["h1"], ta=True, name="gla_in_dw", out_dtypes=(BF16,), out_rows=main + rank)
    dx, dsh, dsc, dgain, *into_branch = _adaln_bwd(x, dh1, dx1, p["norm_mix"][i] + tok, p["sc_m"][i], "gla_norm_bwd",
                                                   branch)
    grads = dict(gla_w_gate=dwg[:rank], gla_b_gate=dbg, gla_norm=dgn, norm_mix=dgain, sh_m=dsh, sc_m=dsc, g_m=dg_m,
                 gla_w_in_unsent=(dw_main, da_tail, sv["h1"]))
    return (dx, *into_branch), grads


def _fox_layer_fwd(x, p, i):
    d = x.shape[1]
    hd = p["fox_q_norm"].shape[1]
    heads = d // hd
    s = x.shape[0]
    t = _tile(s, 1024)
    h1 = _adaln_fwd(x, p["norm_mix"][i], p["sc_m"][i], p["sh_m"][i], "fox_norm")
    w_t, w_tail_t, main = p["fetch"]("fox_in", h1)
    proj = _matmul(h1, w_t, tb=True, b_rows=main, name="fox_in")
    fl = _matmul(h1, w_tail_t, tb=True, name="fox_in_tail")
    q, k, v, og = (("cols", proj, j, d) for j in range(4))
    qn, kn, vb = _fox_prep(q, k, v, p["fox_q_norm"], p["fox_k_norm"], d, hd, "fox_prep")
    cum = _fox_cum(fl, p["fox_bf_p"], "fox_cum")
    cum_t = jnp.transpose(cum[:, :heads])
    cum_col, cum_row = cum_t[:, :, None], cum_t.reshape(heads, s // t, 1, t)
    o, lse = _fox_attn_fwd(qn, kn, vb, cum_col, cum_row, hd, t, "fox_attn")
    act = _fox_gate_fwd(o, og, "fox_gate")
    x1, y = _matmul(act, p["fetch"]("fox_out", act), name="fox_out", out_dtypes=(F32, F32),
                    epilogue=lambda acc, x, g: (x + (1.0 + g) * acc, acc), extras=(("mn", x), ("n", p["g_m"][i])))
    return x1, dict(h1=h1, q=q, k=k, og=og, fl=fl, qn=qn, kn=kn, vb=vb, cum_col=cum_col, cum_row=cum_row,
                    o=o, lse=lse, act=act, y=y, t=t, hd=hd)


def _fox_layer_bwd(incoming, x, sv, p, i, branch):
    d = x.shape[1]
    hd, t = sv["hd"], sv["t"]
    heads = d // hd
    s = x.shape[0]
    dx1, dy, dg_m = incoming
    (w_t, w_tail_t, main), w_out = p["fetch"]("fox_in", None), p["fetch"]("fox_out", None)
    dact = _matmul(dy, w_out, tb=True, name="fox_out_dx")
    dw_out = _matmul(sv["act"], dy, ta=True, name="fox_out_dw", out_dtypes=(BF16,))
    d_o, dproj = _fox_gate_bwd(sv["o"], sv["og"], dact, "fox_gate_bwd")
    tok_flush = p["flush"](d_o)
    dqn, dkn, dproj, dcq, dck = _fox_attn_bwd(sv["qn"], sv["kn"], sv["vb"], d_o, sv["o"], sv["lse"], sv["cum_col"],
                                              sv["cum_row"], hd, t, dproj, "fox_attn_bwd")
    dproj, gq, gk = _fox_prep_bwd(sv["q"], sv["k"], dqn, dkn, p["fox_q_norm"], p["fox_k_norm"], hd, dproj,
                                  "fox_prep_bwd")
    dcum = _pad_cols(jnp.transpose(dcq[:, :, 0] - dck.reshape(heads, s)))
    dfl, dbf = _fox_cum_bwd(dcum, sv["fl"], p["fox_bf_p"], "fox_cum_bwd")
    dfl_b = dfl.astype(BF16)
    dh_tail = _matmul(dfl_b, w_tail_t, name="fox_in_tail_dx")
    dh1 = _matmul(dproj, w_t, b_rows=main, name="fox_in_dx", tk=2048,
                  epilogue=lambda acc, tl: (acc + tl,), extras=(("mn", dh_tail),))
    dw_main = _matmul(dproj, sv["h1"], ta=True, name="fox_in_dw", out_dtypes=(BF16,), out_rows=main + heads)
    dw_in = _tail_rows(dfl_b, sv["h1"], dw_main, heads, "fox_in_tail_dw").reshape(N_DEV, -1, d)
    tok = tok_flush + p["send"]("fox", [dw_in, dw_out.reshape(N_DEV, -1, d)])
    dx, dsh, dsc, dgain, *into_branch = _adaln_bwd(x, dh1, dx1, p["norm_mix"][i] + tok, p["sc_m"][i], "fox_norm_bwd",
                                                   branch)
    grads = dict(fox_b_f=dbf[:, :heads], fox_q_norm=gq.reshape(heads, hd).sum(0, keepdims=True),
                 fox_k_norm=gk.reshape(heads, hd).sum(0, keepdims=True), norm_mix=dgain, sh_m=dsh, sc_m=dsc, g_m=dg_m)
    return (dx, *into_branch), grads


SMALL = ("b_mod", "norm_mix", "norm_ffn", "gla_b_gate", "gla_norm", "fox_b_f", "fox_q_norm", "fox_k_norm",
         "ffn_conv_b", "norm_final")
SMALL_SHARDED = ("gla_w_gate", "ffn_conv_w")
BIG = ("gla_w_in", "gla_w_out", "fox_w_in", "fox_w_out", "ffn_w_up", "ffn_w_down")
WEIGHTS = ("w_mod", "b_mod", "norm_mix", "norm_ffn", "gla_w_in", "gla_w_gate", "gla_b_gate", "gla_norm", "gla_w_out",
           "fox_w_in", "fox_b_f", "fox_q_norm", "fox_k_norm", "fox_w_out", "ffn_w_up", "ffn_conv_w", "ffn_conv_b",
           "ffn_w_down", "norm_final")


def _pack(parts):
    flat = jnp.concatenate([p.reshape(-1) for p in parts])
    pad = (-flat.shape[0]) % 1024
    return jnp.pad(flat, (0, pad)).reshape(1, -1)


def _unpack(flat, shapes):
    out, off = [], 0
    for shp in shapes:
        n = 1
        for s in shp:
            n *= s
        out.append(flat[0, off:off + n].reshape(shp))
        off += n
    return out


def kernel(x, c, w_mod, b_mod, norm_mix, norm_ffn, gla_w_in, gla_w_gate, gla_b_gate, gla_norm, gla_w_out, fox_w_in, fox_b_f, fox_q_norm, fox_k_norm, fox_w_out, ffn_w_up, ffn_conv_w, ffn_conv_b, ffn_w_down, norm_final, loss_target, m_w_mod, m_b_mod, m_norm_mix, m_norm_ffn, m_gla_w_in, m_gla_w_gate, m_gla_b_gate, m_gla_norm, m_gla_w_out, m_fox_w_in, m_fox_b_f, m_fox_q_norm, m_fox_k_norm, m_fox_w_out, m_ffn_w_up, m_ffn_conv_w, m_ffn_conv_b, m_ffn_w_down, m_norm_final, v_w_mod, v_b_mod, v_norm_mix, v_norm_ffn, v_gla_w_in, v_gla_w_gate, v_gla_b_gate, v_gla_norm, v_gla_w_out, v_fox_w_in, v_fox_b_f, v_fox_q_norm, v_fox_k_norm, v_fox_w_out, v_ffn_w_up, v_ffn_conv_w, v_ffn_conv_b, v_ffn_w_down, v_norm_final):
    w = dict(w_mod=w_mod, b_mod=b_mod, norm_mix=norm_mix, norm_ffn=norm_ffn, gla_w_in=gla_w_in, gla_w_gate=gla_w_gate,
             gla_b_gate=gla_b_gate, gla_norm=gla_norm, gla_w_out=gla_w_out, fox_w_in=fox_w_in, fox_b_f=fox_b_f,
             fox_q_norm=fox_q_norm, fox_k_norm=fox_k_norm, fox_w_out=fox_w_out, ffn_w_up=ffn_w_up,
             ffn_conv_w=ffn_conv_w, ffn_conv_b=ffn_conv_b, ffn_w_down=ffn_w_down, norm_final=norm_final)
    mom_m = dict(w_mod=m_w_mod, b_mod=m_b_mod, norm_mix=m_norm_mix, norm_ffn=m_norm_ffn, gla_w_in=m_gla_w_in,
                 gla_w_gate=m_gla_w_gate, gla_b_gate=m_gla_b_gate, gla_norm=m_gla_norm, gla_w_out=m_gla_w_out,
                 fox_w_in=m_fox_w_in, fox_b_f=m_fox_b_f, fox_q_norm=m_fox_q_norm, fox_k_norm=m_fox_k_norm,
                 fox_w_out=m_fox_w_out, ffn_w_up=m_ffn_w_up, ffn_conv_w=m_ffn_conv_w, ffn_conv_b=m_ffn_conv_b,
                 ffn_w_down=m_ffn_w_down, norm_final=m_norm_final)
    mom_v = dict(w_mod=v_w_mod, b_mod=v_b_mod, norm_mix=v_norm_mix, norm_ffn=v_norm_ffn, gla_w_in=v_gla_w_in,
                 gla_w_gate=v_gla_w_gate, gla_b_gate=v_gla_b_gate, gla_norm=v_gla_norm, gla_w_out=v_gla_w_out,
                 fox_w_in=v_fox_w_in, fox_b_f=v_fox_b_f, fox_q_norm=v_fox_q_norm, fox_k_norm=v_fox_k_norm,
                 fox_w_out=v_fox_w_out, ffn_w_up=v_ffn_w_up, ffn_conv_w=v_ffn_conv_w, ffn_conv_b=v_ffn_conv_b,
                 ffn_w_down=v_ffn_w_down, norm_final=v_norm_final)

    me = 4 * lax.axis_index("x") + 2 * lax.axis_index("y") + lax.axis_index("c")
    xs, target = x[0], loss_target[0]
    s, d = xs.shape
    depth = w_mod.shape[0]
    mod_cols = w_mod.shape[2]
    rank = gla_w_gate.shape[1]
    hd = fox_q_norm.shape[1]
    fox_heads = d // hd
    dk_total = gla_w_gate.shape[2] * N_DEV

    cond = c * (1.0 / (1.0 + jnp.exp(-c)))
    g, _ = _exchange([gla_w_gate[0], ffn_conv_w, cond], "gather_small", scatter=False)
    cond_all = g[2][:, 0, :]

    cond_pad = jnp.pad(cond_all, ((0, 16 - N_DEV), (0, 0)))
    mod_part = []
    for i in range(depth):
        b_cols = lax.dynamic_slice(b_mod[i:i + 1], (0, me * mod_cols), (1, mod_cols))
        mod_part.append(_matmul(cond_pad, w_mod, b_layer=i, name=f"mod_{i}", tn=768,
                                epilogue=lambda acc, b: (acc + b,), extras=(("n", b_cols),))[:N_DEV])
    (mod_all,), tok_mod = _exchange([jnp.stack(mod_part)], "gather_mod", scatter=False)
    mod = lax.dynamic_index_in_dim(mod_all, me, axis=2, keepdims=False)
    mod = jnp.transpose(mod, (1, 0, 2)).reshape(depth, 6, 1, d)

    big_names = ["gla_in", "gla_out", "up0", "down0", "fox_in", "fox_out", "up1", "down1"]
    first = [jnp.transpose(gla_w_in[0] + tok_mod).astype(BF16), gla_w_out[0].astype(BF16)]
    handles, tok_first = _gather2_start(first, "gather_weights_start_first")
    rest = [ffn_w_up[0] + tok_first, ffn_w_down[0], jnp.transpose(fox_w_in[0]), fox_w_out[0], ffn_w_up[1],
            ffn_w_down[1]]
    handles_rest, tok0 = _gather2_start([t.astype(BF16) for t in rest], "gather_weights_start_rest")
    handles = handles + handles_rest
    ready, forwarded = {}, {}

    def split_tail(full_t, tail):
        main = full_t.shape[0] - tail
        return full_t, jnp.pad(full_t[main:], ((0, LANE - tail), (0, 0))), main

    def forward(idx, after):
        key = big_names[idx]
        forwarded[key] = _gather2_forward(handles[idx], after, f"gather_{key}_forward")

    def fetch(key, after):
        if key not in ready:
            idx = big_names.index(key)
            if idx == 0:
                forward(0, after)
            handle, _ = forwarded[key]
            _, full = _gather2_wait(handle, after, f"gather_{key}_wait")
            if idx + 1 < len(big_names):
                forward(idx + 1, full)
            if key == "gla_in":
                ready[key] = split_tail(full.reshape(-1, d), rank)
            elif key == "fox_in":
                ready[key] = split_tail(full.reshape(-1, d), fox_heads)
            elif key.startswith("up"):
                ready[key] = full
            else:
                ready[key] = full.reshape(-1, d)
        return ready[key]

    pending, sent = [], {}
    core = lax.axis_index("c").astype(jnp.int32).reshape(1)
    chip = 2 * lax.axis_index("x") + lax.axis_index("y")

    def send(key, pieces):
        hs, tok = _split_start(pieces, _pair_plan, f"scatter_{key}_pair_start")
        pending.append((key, hs))
        return tok

    def flush(after):
        tok = 0.0
        while pending:
            key, hs = pending.pop(0)
            mine, partner = _split_wait(hs, _pair_plan, after, f"scatter_{key}_pair_wait")
            sums = [_pair_sum(pc, pt, core, f"scatter_{key}_pair_sum{a}")
                    for a, (pc, pt) in enumerate(zip(mine, partner))]
            sent[key], t = _split_start(sums, _chip_plan, f"scatter_{key}_chip_start")
            tok = tok + t
        return tok

    p = dict(
        fetch=fetch, send=send, flush=flush,
        gla_wg_p=jnp.pad(_cols_full(g[0]), ((0, LANE - rank), (0, 0))),
        conv_w=[jnp.transpose(g[1][:, i], (1, 0, 2)).reshape(ffn_conv_w.shape[1], -1) for i in range(depth)],
        conv_b=[ffn_conv_b[i:i + 1] for i in range(depth)],
        gla_b_gate=gla_b_gate, gla_norm=gla_norm, fox_q_norm=fox_q_norm, fox_k_norm=fox_k_norm,
        fox_bf_p=_pad_cols(fox_b_f), gla_rank=rank,
        norm_mix=[norm_mix[i:i + 1] + (tok0 if i == 0 else 0.0) for i in range(depth)],
        norm_ffn=[norm_ffn[i:i + 1] for i in range(depth)],
    )

    for j, nm in enumerate(("sh_m", "sc_m", "g_m", "sh_f", "sc_f", "g_f")):
        p[nm] = [mod[i, j] for i in range(depth)]

    acts, saved = [xs], []
    for i in range(depth):
        layer_fwd = _gla_layer_fwd if i % 2 == 0 else _fox_layer_fwd
        x1, sv_mix = layer_fwd(acts[-1], p, i)
        x2, sv_ffn = _ffn_fwd(x1, p, i, str(i))
        saved.append((acts[-1], x1, sv_mix, sv_ffn))
        acts.append(x2)
    last_ffn = (saved[-1][3]["f"], p["g_f"][depth - 1])
    dx, d_norm_final, loss_part, *into_branch = _final_loss(acts[-1], target, norm_final.reshape(1, d), "final_loss",
                                                            last_ffn)
    incoming = (dx, *into_branch)

    lg = [None] * depth
    for i in reversed(range(depth)):
        x_in, x1, sv_mix, sv_ffn = saved[i]
        incoming, g_ffn = _ffn_bwd(incoming, x1, sv_ffn, p, i, str(i), (sv_mix["y"], p["g_m"][i]))
        layer_bwd = _gla_layer_bwd if i % 2 == 0 else _fox_layer_bwd
        before = (saved[i - 1][3]["f"], p["g_f"][i - 1]) if i else None
        incoming, g_mix = layer_bwd(incoming, x_in, sv_mix, p, i, before)
        lg[i] = {**g_ffn, **g_mix}
    grad_x = incoming[0][None]

    gla_l = [i for i in range(depth) if i % 2 == 0]
    fox_l = [i for i in range(depth) if i % 2 == 1]
    small_parts = dict(
        norm_mix=jnp.concatenate([lg[i]["norm_mix"] for i in range(depth)]),
        norm_ffn=jnp.concatenate([lg[i]["norm_ffn"] for i in range(depth)]),
        gla_b_gate=jnp.concatenate([lg[i]["gla_b_gate"] for i in gla_l]),
        gla_norm=jnp.concatenate([lg[i]["gla_norm"] for i in gla_l]),
        fox_b_f=jnp.concatenate([lg[i]["fox_b_f"] for i in fox_l]),
        fox_q_norm=jnp.concatenate([lg[i]["fox_q_norm"] for i in fox_l]),
        fox_k_norm=jnp.concatenate([lg[i]["fox_k_norm"] for i in fox_l]),
        ffn_conv_b=jnp.concatenate([lg[i]["conv_b"] for i in range(depth)]),
        norm_final=d_norm_final,
        gla_w_gate=jnp.stack([lg[i]["gla_w_gate"] for i in gla_l]),
        ffn_conv_w=jnp.stack([lg[i]["conv_w"] for i in range(depth)]),
        loss=loss_part[:, :1],
    )
    order = ("norm_mix", "norm_ffn", "gla_b_gate", "gla_norm", "fox_b_f", "fox_q_norm", "fox_k_norm", "ffn_conv_b",
             "norm_final", "gla_w_gate", "ffn_conv_w", "loss")
    packed = _pack([small_parts[nm] for nm in order])
    dmod = jnp.stack([jnp.concatenate([lg[i][nm] for nm in ("sh_m", "sc_m", "g_m", "sh_f", "sc_f", "g_f")], axis=1)
                      for i in range(depth)])
    hs_small, tok_small = _split_start([packed[None], dmod[None]], _all_plan, "gather_small_grads_start",
                                       land_blocks=N_DEV)
    dw_main, da_tail, h1_gla = lg[0]["gla_w_in_unsent"]
    dw_in_t = _tail_rows(da_tail + tok_small.astype(BF16), h1_gla, dw_main, rank, "gla_in_tail_dw")
    send("gla_in", [dw_in_t.reshape(N_DEV, -1, d)])
    started = pending[-1][1][0][0]

    received = {}

    def arrive(key, after):
        sums, lands = _split_wait(sent[key], _chip_plan, after, f"scatter_{key}_chip_wait")
        received[key] = list(zip(lands, sums))

    for key in ("ffn1", "fox", "ffn0", "gla_out"):
        arrive(key, started)

    out_g, out_d, out_m, out_v = {}, {}, {}, {}

    chip_idx = chip.astype(jnp.int32).reshape(1)

    def update(nm, g_arr, transposed=False):
        swap = (lambda t: jnp.transpose(t, (0, 2, 1))) if transposed else (lambda t: t)
        if isinstance(g_arr, list):
            res = _adamw_pieces(swap(w[nm]), [t[0] for t in g_arr], [t[1] for t in g_arr], chip_idx,
                                swap(mom_m[nm]), swap(mom_v[nm]), f"adamw_{nm}")
        else:
            res = _adamw(w[nm], g_arr, mom_m[nm], mom_v[nm], f"adamw_{nm}")
        out_g[nm], out_d[nm], out_m[nm], out_v[nm] = (swap(t) for t in res)

    update("gla_w_out", [received["gla_out"][0]])
    update("fox_w_out", [received["fox"][1]])
    tok_flush = flush(out_g["fox_w_out"])
    update("ffn_w_up", [received[f"ffn{i}"][0] for i in range(depth)])
    update("fox_w_in", [received["fox"][0]], transposed=True)
    update("ffn_w_down", [received[f"ffn{i}"][1] for i in range(depth)])

    updated = ("gla_w_out", "fox_w_in", "fox_w_out", "ffn_w_up", "ffn_w_down")
    (packed_mine, dmod_mine), (packed_all, dmod_all) = _split_wait(
        hs_small, _all_plan, [out_d[nm] for nm in updated], "gather_small_grads_wait")
    packed_all = lax.dynamic_update_slice(packed_all, packed_mine + tok_flush, (me, 0, 0))
    dmod_all = lax.dynamic_update_slice(dmod_all, dmod_mine, (me, 0, 0, 0))
    summed = _unpack(_sum8(packed_all, "sum_small_grads"), [small_parts[nm].shape for nm in order])
    small_g = dict(zip(order, summed))
    loss = small_g["loss"][0, 0]
    dmod_all = dmod_all[:, :, 0, :]
    grads = {}
    cond_t = _pad_cols(jnp.transpose(cond_all)).astype(BF16)
    dmod_cols = lax.dynamic_slice(dmod_all, (0, 0, me * mod_cols), (N_DEV, depth, mod_cols))
    g_w_mod = lax.empty(w_mod.shape, F32)
    for i in range(depth):
        rhs = jnp.pad(dmod_cols[:, i], ((0, LANE - N_DEV), (0, 0)))
        g_w_mod = _matmul(cond_t, rhs, name=f"mod_dw_{i}", tn=768, into=(g_w_mod, i))
    grads["w_mod"] = g_w_mod
    small_g["b_mod"] = _sum8(dmod_all.reshape(N_DEV, 1, -1), "sum_b_mod").reshape(depth, -1)
    update("w_mod", grads["w_mod"])

    gate_cols = gla_w_gate.shape[2]
    conv_cols = ffn_conv_w.shape[2]
    local_small = dict(small_g)
    local_small["gla_w_gate"] = lax.dynamic_slice_in_dim(small_g["gla_w_gate"], me * gate_cols, gate_cols, axis=2)
    local_small["ffn_conv_w"] = lax.dynamic_slice_in_dim(small_g["ffn_conv_w"], me * conv_cols, conv_cols, axis=2)
    names = SMALL + SMALL_SHARDED
    shapes = [w[nm].shape for nm in names]
    res = _adamw(_pack([w[nm] for nm in names])[None], _pack([local_small[nm] for nm in names])[None],
                 _pack([mom_m[nm] for nm in names])[None], _pack([mom_v[nm] for nm in names])[None], "adamw_small")
    for tgt, flat in zip((out_g, out_d, out_m, out_v), res):
        for nm, arr in zip(names, _unpack(flat[0], shapes)):
            tgt[nm] = arr

    arrive("gla_in", [out_d[nm] for nm in updated + ("w_mod",)])
    update("gla_w_in", [received["gla_in"][0]], transposed=True)

    return (loss, grad_x, *[out_g[n] for n in WEIGHTS], *[out_d[n] for n in WEIGHTS],
            *[out_m[n] for n in WEIGHTS], *[out_v[n] for n in WEIGHTS])
```
